```python
import jax, jax.numpy as jnp
from jax import lax
import numpy as np

D_MODEL = 1024
BATCH = 8
SEQ = 2048
DEPTH = 2
DEC_BATCH = 32
DEC_SEQ = 16
PAST_LEN = 4096

CHUNK = 64
N_A_LAYERS = DEPTH // 2
N_B_LAYERS = DEPTH - N_A_LAYERS
HG_EXPAND = 128
HG_HEADS = D_MODEL // HG_EXPAND
HG_DK = HG_EXPAND
HG_DV = D_MODEL // HG_HEADS
MLA_HEADS = 16
MLA_NOPE = 64
MLA_ROPE = 32
MLA_V = 64
Q_LORA = 384
KV_LORA = 256
ROPE_THETA = 10000.0
Q_BLOCK = 128
N_EXPERTS = 64
TOP_K = 8
N_GROUPS = 8
TOPK_GROUPS = 4
EXPERT_FF = 256
SHARED_FF = 256
ROUTED_SCALE = 2.5
MOE_BLOCK = 512
EPS = 1e-6

kernel_name = "yoco_hgrn2_mla_moe_stream_step"

F32 = jnp.float32


def rmsnorm(x, g):
    xf = x.astype(F32)
    y = xf * lax.rsqrt(jnp.mean(xf * xf, axis=-1, keepdims=True) + EPS)
    return (y * g.astype(F32)).astype(x.dtype)


def rope(x, pos):
    half = MLA_ROPE // 2
    inv = 1.0 / (ROPE_THETA ** (jnp.arange(half, dtype=F32) * 2.0 / MLA_ROPE))
    ang = pos.astype(F32)[:, None] * inv[None, :]
    shp = (pos.shape[0],) + (1,) * (x.ndim - 3) + (half,)
    cos, sin = jnp.cos(ang).reshape(shp), jnp.sin(ang).reshape(shp)
    xf = x.astype(F32)
    x1, x2 = xf[..., :half], xf[..., half:]
    return jnp.concatenate([x1 * cos - x2 * sin, x2 * cos + x1 * sin], axis=-1).astype(x.dtype)


def gla_chunk_step(S, inp):
    q, k, v, g = inp
    L = q.shape[2]
    b = jnp.cumsum(g, axis=2)
    o_inter = jnp.einsum('bhtk,bhkv->bhtv', q * jnp.exp(b), S)
    causal = jnp.tril(jnp.ones((L, L), dtype=bool))
    diff = b[:, :, :, None, :] - b[:, :, None, :, :]
    decay = jnp.exp(jnp.where(causal[:, :, None], diff, -jnp.inf))
    scores = jnp.einsum('bhtsk,bhsk->bhts', q[:, :, :, None, :] * decay, k)
    o_intra = jnp.einsum('bhts,bhsv->bhtv', scores, v)
    b_last = b[:, :, -1, :]
    S_new = jnp.exp(b_last)[..., None] * S + jnp.einsum(
        'bhsk,bhsv->bhkv', k * jnp.exp(b_last[:, :, None, :] - b), v)
    return S_new, o_inter + o_intra


def hgrn2_mixer(h, S0, w_in, lb, onorm_g, w_out):
    Bsz, T, D = h.shape
    q, f, i, g = jnp.split(h @ w_in, 4, axis=-1)
    q = jax.nn.silu(q.astype(F32))
    lbf = lb.astype(F32)
    fg = lbf + (1.0 - lbf) * jax.nn.sigmoid(f.astype(F32))
    k = 1.0 - fg
    logf = jnp.log(fg)
    L = CHUNK if T % CHUNK == 0 else T
    n = T // L

    def to_blocks(a, dh):
        return a.reshape(Bsz, n, L, HG_HEADS, dh).transpose(1, 0, 3, 2, 4)

    S_fin, o = lax.scan(gla_chunk_step, S0.astype(F32),
                        (to_blocks(q, HG_DK), to_blocks(k, HG_DK),
                         to_blocks(i.astype(F32), HG_DV), to_blocks(logf, HG_DK)))
    o = o.transpose(1, 0, 3, 2, 4).reshape(Bsz, T, HG_HEADS, HG_DV)
    o = rmsnorm(o, onorm_g.reshape(HG_HEADS, HG_DV)).reshape(Bsz, T, D)
    o = o * jax.nn.silu(g.astype(F32))
    return o.astype(h.dtype) @ w_out, S_fin.astype(S0.dtype)


def shared_kv(x, pos, kv_in_g, w_dkv, kv_lat_g):
    z = rmsnorm(x, kv_in_g) @ w_dkv
    lat = rmsnorm(z[..., :KV_LORA], kv_lat_g)
    krope = rope(z[..., KV_LORA:], pos)
    return lat, krope


def mla_queries(h, pos, w_dq, q_norm_g, w_uq):
    cq = rmsnorm(h @ w_dq, q_norm_g)
    q = jnp.einsum('bsr,rhe->bshe', cq, w_uq)
    return q[..., :MLA_NOPE], rope(q[..., MLA_NOPE:], pos)


def mla_attend_prompt(q_nope, q_rope, k_nope, v, k_rope, pos):
    Bsz, S = q_nope.shape[:2]
    nb = S // Q_BLOCK
    scale = (MLA_NOPE + MLA_ROPE) ** -0.5
    kchunk = pos // CHUNK

    def blk(a):
        return jnp.moveaxis(a.reshape((Bsz, nb, Q_BLOCK) + a.shape[2:]), 1, 0)

    def one_block(args):
        qn, qr, qp = args
        s = (jnp.einsum('bqhe,bkhe->bhqk', qn, k_nope).astype(F32)
             + jnp.einsum('bqhr,bkr->bhqk', qr, k_rope).astype(F32)) * scale
        mask = kchunk[None, :] <= (qp // CHUNK)[:, None]
        p = jax.nn.softmax(jnp.where(mask, s, -jnp.inf), axis=-1)
        return jnp.einsum('bhqk,bkhe->bqhe', p.astype(v.dtype), v)

    o = lax.map(one_block, (blk(q_nope), blk(q_rope), pos.reshape(nb, Q_BLOCK)))
    return jnp.moveaxis(o, 0, 1).reshape(Bsz, S, MLA_HEADS, MLA_V)


def mla_attend_sample(q_nope, q_rope, lat_all, krope_all, qpos, kpos, w_uk, w_uv):
    scale = (MLA_NOPE + MLA_ROPE) ** -0.5
    q_lat = jnp.einsum('bqhe,rhe->bqhr', q_nope, w_uk)
    s = (jnp.einsum('bqhr,bkr->bhqk', q_lat, lat_all).astype(F32)
         + jnp.einsum('bqhe,bke->bhqk', q_rope, krope_all).astype(F32)) * scale
    mask = (kpos // CHUNK)[None, :] <= (qpos // CHUNK)[:, None]
    p = jax.nn.softmax(jnp.where(mask, s, -jnp.inf), axis=-1)
    o_lat = jnp.einsum('bhqk,bkr->bqhr', p.astype(lat_all.dtype), lat_all)
    return jnp.einsum('bqhr,rhe->bqhe', o_lat, w_uv)


def route(t, router_w, router_bias):
    s = jax.nn.sigmoid((t @ router_w).astype(F32))
    sb = s + router_bias.astype(F32)
    grp = sb.reshape(-1, N_GROUPS, N_EXPERTS // N_GROUPS)
    gscore = jnp.sum(lax.top_k(grp, 2)[0], axis=-1)
    _, gidx = lax.top_k(gscore, TOPK_GROUPS)
    gmask = jnp.sum(jax.nn.one_hot(gidx, N_GROUPS, dtype=F32), axis=-2)
    emask = jnp.repeat(gmask, N_EXPERTS // N_GROUPS, axis=-1) > 0
    _, eidx = lax.top_k(jnp.where(emask, sb, -jnp.inf), TOP_K)
    w = jnp.take_along_axis(s, eidx, axis=-1)
    w = w / jnp.sum(w, axis=-1, keepdims=True) * ROUTED_SCALE
    return jnp.sum(jax.nn.one_hot(eidx, N_EXPERTS, dtype=F32) * w[..., None], axis=-2)


def routed_experts(t, gates, w_in, w_out):
    N = t.shape[0]
    blk = min(MOE_BLOCK, N)
    pad = (-N) % blk
    tp = jnp.pad(t, ((0, pad), (0, 0)))
    gp = jnp.pad(gates, ((0, pad), (0, 0)))
    nb = (N + pad) // blk

    def one(args):
        tb, gb = args
        hu = jnp.einsum('nd,edf->nef', tb, w_in)
        a, u = jnp.split(hu, 2, axis=-1)
        act = jax.nn.silu(a) * u * gb[..., None].astype(hu.dtype)
        return jnp.einsum('nef,efd->nd', act, w_out)

    out = lax.map(one, (tp.reshape(nb, blk, -1), gp.reshape(nb, blk, -1)))
    return out.reshape(nb * blk, -1)[:N]


def moe(h, router_w, router_bias, w_in, w_out, sh_in, sh_out):
    Bsz, T, D = h.shape
    t = h.reshape(Bsz * T, D)
    gates = route(t, router_w, router_bias)
    routed = routed_experts(t, gates, w_in, w_out)
    su, sv = jnp.split(t @ sh_in, 2, axis=-1)
    shared = (jax.nn.silu(su) * sv) @ sh_out
    return (routed.astype(h.dtype) + shared).reshape(Bsz, T, D)


def trunk(x, c, pos, hg_state, past_lat, past_krope,
          ada_w, ada_b, norm1_g, norm2_g,
          hg_w_in, hg_lb_logits, hg_onorm_g, hg_w_out,
          kv_in_g, w_dkv, kv_lat_g, w_uk, w_uv,
          w_dq, q_norm_g, w_uq, w_o,
          router_w, router_bias, exp_w_in, exp_w_out, sh_w_in, sh_w_out, final_g):
    Bsz, T, D = x.shape
    c_act = jax.nn.silu(c)
    lbs = jnp.cumsum(jax.nn.softmax(hg_lb_logits.astype(F32), axis=0), axis=0)
    hg_new = []
    lat = krope = k_nope = v_full = lat_all = krope_all = kpos = None
    for l in range(DEPTH):
        mod = c_act @ ada_w[l] + ada_b[l]
        sh1, sc1, g1, sh2, sc2, g2 = [m[:, None, :] for m in jnp.split(mod, 6, axis=-1)]
        h = rmsnorm(x, norm1_g[l]) * (1 + sc1) + sh1
        if l < N_A_LAYERS:
            y, s_new = hgrn2_mixer(h, hg_state[l], hg_w_in[l], lbs[l], hg_onorm_g[l], hg_w_out[l])
            hg_new.append(s_new)
        else:
            bi = l - N_A_LAYERS
            q_nope, q_rope = mla_queries(h, pos, w_dq[bi], q_norm_g[bi], w_uq[bi])
            if past_lat is None:
                o = mla_attend_prompt(q_nope, q_rope, k_nope, v_full, krope, pos)
            else:
                o = mla_attend_sample(q_nope, q_rope, lat_all, krope_all, pos, kpos, w_uk, w_uv)
            y = o.reshape(Bsz, T, MLA_HEADS * MLA_V) @ w_o[bi]
        x = x + g1 * y
        h = rmsnorm(x, norm2_g[l]) * (1 + sc2) + sh2
        x = x + g2 * moe(h, router_w[l], router_bias[l], exp_w_in[l], exp_w_out[l], sh_w_in[l], sh_w_out[l])
        if l == N_A_LAYERS - 1:
            lat, krope = shared_kv(x, pos, kv_in_g, w_dkv, kv_lat_g)
            if past_lat is None:
                k_nope = jnp.einsum('bsr,rhe->bshe', lat, w_uk)
                v_full = jnp.einsum('bsr,rhe->bshe', lat, w_uv)
            else:
                lat_all = jnp.concatenate([past_lat.astype(lat.dtype), lat], axis=1)
                krope_all = jnp.concatenate([past_krope.astype(krope.dtype), krope], axis=1)
                kpos = jnp.arange(lat_all.shape[1], dtype=jnp.int32)
    return rmsnorm(x, final_g), jnp.stack(hg_new, axis=0), lat, krope


def setup_inputs(seed: int = 0) -> dict:
    key = jax.random.key(seed)
    ks = jax.random.split(key, 32)
    D = D_MODEL

    def nrm(k, shape, scale):
        return jax.random.normal(k, shape, F32) * scale

    def gain(k, shape):
        return 1.0 + 0.05 * jax.random.normal(k, shape, F32)

    return {
        "x_prompt": nrm(ks[0], (BATCH, SEQ, D), 1.0),
        "x_sample": nrm(ks[1], (DEC_BATCH, DEC_SEQ, D), 1.0),
        "state_hgrn": nrm(ks[2], (N_A_LAYERS, DEC_BATCH, HG_HEADS, HG_DK, HG_DV), 0.5),
        "cache_mla_latent": nrm(ks[3], (DEC_BATCH, PAST_LEN, KV_LORA), 1.0),
        "cache_mla_krope": nrm(ks[4], (DEC_BATCH, PAST_LEN, MLA_ROPE), 1.0),
        "c_prompt": nrm(ks[5], (BATCH, D), 1.0),
        "c_sample": nrm(ks[6], (DEC_BATCH, D), 1.0),
        "ada_w": nrm(ks[7], (DEPTH, D, 6 * D), 0.5 * D ** -0.5),
        "ada_b": nrm(ks[8], (DEPTH, 6 * D), 0.01),
        "norm1_g": gain(ks[9], (DEPTH, D)),
        "norm2_g": gain(ks[10], (DEPTH, D)),
        "hg_w_in": nrm(ks[11], (N_A_LAYERS, D, 4 * D), D ** -0.5),
        "hg_lb_logits": nrm(ks[12], (N_A_LAYERS + 1, D), 0.1),
        "hg_onorm_g": gain(ks[13], (N_A_LAYERS, D)),
        "hg_w_out": nrm(ks[14], (N_A_LAYERS, D, D), D ** -0.5),
        "kv_in_g": gain(ks[15], (D,)),
        "w_dkv": nrm(ks[16], (D, KV_LORA + MLA_ROPE), D ** -0.5),
        "kv_lat_g": gain(ks[17], (KV_LORA,)),
        "w_uk": nrm(ks[18], (KV_LORA, MLA_HEADS, MLA_NOPE), KV_LORA ** -0.5),
        "w_uv": nrm(ks[19], (KV_LORA, MLA_HEADS, MLA_V), KV_LORA ** -0.5),
        "w_dq": nrm(ks[20], (N_B_LAYERS, D, Q_LORA), D ** -0.5),
        "q_norm_g": gain(ks[21], (N_B_LAYERS, Q_LORA)),
        "w_uq": nrm(ks[22], (N_B_LAYERS, Q_LORA, MLA_HEADS, MLA_NOPE + MLA_ROPE), Q_LORA ** -0.5),
        "w_o": nrm(ks[23], (N_B_LAYERS, MLA_HEADS * MLA_V, D), (MLA_HEADS * MLA_V) ** -0.5),
        "router_w": nrm(ks[24], (DEPTH, D, N_EXPERTS), D ** -0.5),
        "router_bias": nrm(ks[25], (DEPTH, N_EXPERTS), 0.01),
        "exp_w_in": nrm(ks[26], (DEPTH, N_EXPERTS, D, 2 * EXPERT_FF), D ** -0.5),
        "exp_w_out": nrm(ks[27], (DEPTH, N_EXPERTS, EXPERT_FF, D), EXPERT_FF ** -0.5),
        "sh_w_in": nrm(ks[28], (DEPTH, D, 2 * SHARED_FF), D ** -0.5),
        "sh_w_out": nrm(ks[29], (DEPTH, SHARED_FF, D), SHARED_FF ** -0.5),
        "final_g": gain(ks[30], (D,)),
    }


def reference(x_prompt, x_sample, state_hgrn, cache_mla_latent, cache_mla_krope, c_prompt, c_sample,
              ada_w, ada_b, norm1_g, norm2_g,
              hg_w_in, hg_lb_logits, hg_onorm_g, hg_w_out,
              kv_in_g, w_dkv, kv_lat_g, w_uk, w_uv,
              w_dq, q_norm_g, w_uq, w_o,
              router_w, router_bias, exp_w_in, exp_w_out, sh_w_in, sh_w_out, final_g):
    Bp, Sp, _ = x_prompt.shape
    Bs, Ss, _ = x_sample.shape
    past = cache_mla_latent.shape[1]
    pos_p = jnp.arange(Sp, dtype=jnp.int32)
    pos_s = past + jnp.arange(Ss, dtype=jnp.int32)
    zero_state = jnp.zeros((N_A_LAYERS, Bp, HG_HEADS, HG_DK, HG_DV), x_prompt.dtype)
    y_prompt, st_prompt, lat_prompt, kr_prompt = trunk(
        x_prompt, c_prompt, pos_p, zero_state, None, None,
        ada_w, ada_b, norm1_g, norm2_g,
        hg_w_in, hg_lb_logits, hg_onorm_g, hg_w_out,
        kv_in_g, w_dkv, kv_lat_g, w_uk, w_uv,
        w_dq, q_norm_g, w_uq, w_o,
        router_w, router_bias, exp_w_in, exp_w_out, sh_w_in, sh_w_out, final_g)
    y_sample, st_sample, lat_sample, kr_sample = trunk(
        x_sample, c_sample, pos_s, state_hgrn, cache_mla_latent, cache_mla_krope,
        ada_w, ada_b, norm1_g, norm2_g,
        hg_w_in, hg_lb_logits, hg_onorm_g, hg_w_out,
        kv_in_g, w_dkv, kv_lat_g, w_uk, w_uv,
        w_dq, q_norm_g, w_uq, w_o,
        router_w, router_bias, exp_w_in, exp_w_out, sh_w_in, sh_w_out, final_g)
    return (y_prompt, y_sample, st_prompt, st_sample, lat_prompt, kr_prompt, lat_sample, kr_sample)
```

```python
import functools

import numpy as np
import jax
import jax.numpy as jnp
from jax import lax
from jax.experimental import pallas as pl
from jax.experimental.pallas import tpu as pltpu

F32 = jnp.float32
BF16 = jnp.bfloat16

D_MODEL = 1024
CHUNK = 64
HG_HEADS = 8
HG_DK = 128
HG_DV = 128
MLA_HEADS = 16
MLA_NOPE = 64
MLA_ROPE = 32
MLA_V = 64
Q_LORA = 384
KV_LORA = 256
ROPE_THETA = 10000.0
N_EXPERTS = 64
TOP_K = 8
N_GROUPS = 8
TOPK_GROUPS = 4
EXPERT_FF = 256
SHARED_FF = 256
ROUTED_SCALE = 2.5
EPS = 1e-6

HEAD_PAD = 128
QK_SCALE = (MLA_NOPE + MLA_ROPE) ** -0.5
VMEM_LIMIT = 56 * 1024 * 1024
NEG_INF = float("-inf")


def _cparams(n_axes):
    return pltpu.CompilerParams(dimension_semantics=("arbitrary",) * n_axes,
                                vmem_limit_bytes=VMEM_LIMIT)


def _silu(x):
    return x * jax.nn.sigmoid(x)


def _rms(x, g):
    ms = jnp.mean(x * x, axis=-1, keepdims=True)
    return x * lax.rsqrt(ms + EPS) * g


def _dot(a, b):
    return jnp.dot(a, b, preferred_element_type=F32)


def _dot_nt(a, b):
    return lax.dot_general(a, b, (((1,), (1,)), ((), ())), preferred_element_type=F32)


def _dot_tn(a, b):
    return lax.dot_general(a, b, (((0,), (0,)), ((), ())), preferred_element_type=F32)


def _row_blocks(B, T, rows):
    if T >= rows:
        assert T % rows == 0
        bb, tt = 1, rows
    else:
        assert rows % T == 0 and B % (rows // T) == 0
        bb, tt = rows // T, T
    nt = T // tt
    return bb, tt, (B // bb) * nt, (lambda i: (i // nt, i % nt))


def _ada_kernel(c_ref, w_ref, b_ref, o_ref):
    a = _silu(c_ref[...]).astype(BF16)
    o_ref[...] = _dot(a, w_ref[...].astype(BF16)) + b_ref[...]


def ada_mod(c, ada_w, ada_b):
    R, D = c.shape
    L, _, N = ada_w.shape
    tn = 1536
    return pl.pallas_call(
        _ada_kernel,
        grid=(L, N // tn),
        in_specs=[pl.BlockSpec((R, D), lambda l, j: (0, 0)),
                  pl.BlockSpec((None, D, tn), lambda l, j: (l, 0, j)),
                  pl.BlockSpec((None, 1, tn), lambda l, j: (l, 0, j))],
        out_specs=pl.BlockSpec((None, R, tn), lambda l, j: (l, 0, j)),
        out_shape=jax.ShapeDtypeStruct((L, R, N), F32),
        compiler_params=_cparams(2),
        name="ada_mod",
    )(c, ada_w, ada_b.reshape(L, 1, N))


def _norm_kernel(*refs, modulated):
    if modulated:
        x_ref, g_ref, sc_ref, sh_ref, o_ref = refs
    else:
        x_ref, g_ref, o_ref = refs
    y = _rms(x_ref[...], g_ref[...])
    if modulated:
        y = y * (1.0 + sc_ref[...]) + sh_ref[...]
    o_ref[...] = y.astype(o_ref.dtype)


def norm_mod(x, g, mod=None, sc_idx=0, sh_idx=0, out_dtype=BF16, rows=512):
    B, T, D = x.shape
    bb, tt, nblk, ij = _row_blocks(B, T, rows)
    xspec = pl.BlockSpec((bb, tt, D), lambda i: ij(i) + (0,))
    in_specs = [xspec, pl.BlockSpec((1, D), lambda i: (0, 0))]
    args = [x, g.reshape(1, D)]
    if mod is not None:
        in_specs += [pl.BlockSpec((bb, 1, D), lambda i: (ij(i)[0], 0, sc_idx)),
                     pl.BlockSpec((bb, 1, D), lambda i: (ij(i)[0], 0, sh_idx))]
        args += [mod, mod]
    return pl.pallas_call(
        functools.partial(_norm_kernel, modulated=mod is not None),
        grid=(nblk,),
        in_specs=in_specs,
        out_specs=xspec,
        out_shape=jax.ShapeDtypeStruct((B, T, D), out_dtype),
        compiler_params=_cparams(1),
        name="norm_mod",
    )(*args)


def _linear_kernel(*refs, residual):
    if residual:
        a_ref, w_ref, x_ref, gate_ref, o_ref, wb_ref = refs
    else:
        a_ref, w_ref, o_ref, wb_ref = refs

    @pl.when(pl.program_id(1) == 0)
    def _():
        wb_ref[...] = w_ref[...].astype(BF16)

    bb, tt, K = a_ref.shape
    y = _dot(a_ref[...].reshape(bb * tt, K).astype(BF16), wb_ref[...])
    y = y.reshape(bb, tt, y.shape[-1])
    if residual:
        y = x_ref[...] + gate_ref[...] * y
    o_ref[...] = y.astype(o_ref.dtype)


def linear(a, w, l, out_dtype, x=None, mod=None, gate_idx=0, rows=512, tn=1024):
    B, T, K = a.shape
    _, _, N = w.shape
    tn = min(tn, N)
    bb, tt, nblk, ij = _row_blocks(B, T, rows)
    in_specs = [pl.BlockSpec((bb, tt, K), lambda j, i: ij(i) + (0,)),
                pl.BlockSpec((None, K, tn), lambda j, i: (l, 0, j))]
    args = [a, w]
    ospec = pl.BlockSpec((bb, tt, tn), lambda j, i: ij(i) + (j,))
    if x is not None:
        gsteps = D_MODEL // tn
        in_specs += [ospec, pl.BlockSpec((bb, 1, tn), lambda j, i: (ij(i)[0], 0, gate_idx * gsteps + j))]
        args += [x, mod]
    return pl.pallas_call(
        functools.partial(_linear_kernel, residual=x is not None),
        grid=(N // tn, nblk),
        in_specs=in_specs,
        out_specs=ospec,
        out_shape=jax.ShapeDtypeStruct((B, T, N), out_dtype),
        scratch_shapes=[pltpu.VMEM((K, tn), BF16)],
        compiler_params=_cparams(2),
        name="linear",
    )(*args)


def _gla_kernel(*refs, L, n_chunks, has_init):
    if has_init:
        q_ref, f_ref, i_ref, g_ref, lb_ref, on_ref, s0_ref, o_ref, so_ref, st_ref = refs
    else:
        q_ref, f_ref, i_ref, g_ref, lb_ref, on_ref, o_ref, so_ref, st_ref = refs
    t = pl.program_id(2)

    @pl.when(t == 0)
    def _():
        if has_init:
            st_ref[...] = s0_ref[0, 0].T
        else:
            st_ref[...] = jnp.zeros_like(st_ref)

    lb = lb_ref[...]
    row = lax.broadcasted_iota(jnp.int32, (L, L), 0)
    col = lax.broadcasted_iota(jnp.int32, (L, L), 1)
    causal = col <= row
    tri = causal.astype(F32)
    mid = L // 2 - 1

    def chunk(c, carry):
        r0 = pl.multiple_of(c * L, L)
        q = _silu(q_ref[0, pl.ds(r0, L), :])
        fg = lb + (1.0 - lb) * jax.nn.sigmoid(f_ref[0, pl.ds(r0, L), :])
        k = 1.0 - fg
        v = i_ref[0, pl.ds(r0, L), :].astype(BF16)
        b = jnp.dot(tri, jnp.log(fg), preferred_element_type=F32, precision=lax.Precision.HIGHEST)
        b_mid = b[mid:mid + 1, :]
        b_last = b[L - 1:L, :]
        st = st_ref[...]
        o = _dot_nt((q * jnp.exp(b)).astype(BF16), st.astype(BF16))
        qa = (q * jnp.exp(b - b_mid)).astype(BF16)
        kb = (k * jnp.exp(b_mid - b)).astype(BF16)
        scores = jnp.where(causal, _dot_nt(qa, kb), 0.0)
        o = o + _dot(scores.astype(BF16), v)
        kd = (k * jnp.exp(b_last - b)).astype(BF16)
        st_ref[...] = st * jnp.exp(b_last) + _dot_tn(v, kd)
        o = _rms(o, on_ref[...]) * _silu(g_ref[0, pl.ds(r0, L), :])
        o_ref[0, pl.ds(r0, L), :] = o.astype(o_ref.dtype)
        return carry

    lax.fori_loop(0, n_chunks, chunk, 0)

    @pl.when(t == pl.num_programs(2) - 1)
    def _():
        so_ref[0, 0] = st_ref[...].T


def gla(z, lb, onorm_g, s0):
    B, T, _ = z.shape
    L = CHUNK if T % CHUNK == 0 else T
    tt = min(T, 512)
    n_chunks = tt // L
    H = HG_HEADS

    def zspec(part):
        return pl.BlockSpec((1, tt, HG_DK), lambda b, h, t: (b, t, part * H + h))

    hspec = pl.BlockSpec((1, HG_DK), lambda b, h, t: (0, h))
    sspec = pl.BlockSpec((1, 1, HG_DK, HG_DV), lambda b, h, t: (b, h, 0, 0))
    in_specs = [zspec(0), zspec(1), zspec(2), zspec(3), hspec, hspec]
    args = [z, z, z, z, lb.reshape(1, D_MODEL), onorm_g.reshape(1, D_MODEL)]
    if s0 is not None:
        in_specs.append(sspec)
        args.append(s0)
    return pl.pallas_call(
        functools.partial(_gla_kernel, L=L, n_chunks=n_chunks, has_init=s0 is not None),
        grid=(B, H, T // tt),
        in_specs=in_specs,
        out_specs=[pl.BlockSpec((1, tt, HG_DV), lambda b, h, t: (b, t, h)), sspec],
        out_shape=[jax.ShapeDtypeStruct((B, T, D_MODEL), BF16),
                   jax.ShapeDtypeStruct((B, H, HG_DK, HG_DV), F32)],
        scratch_shapes=[pltpu.VMEM((HG_DV, HG_DK), F32)],
        compiler_params=_cparams(3),
        name="gla",
    )(*args)


def _route_kernel(h_ref, rw_ref, bias_ref, o_ref):
    bb, tt, D = h_ref.shape
    M = bb * tt
    G, E = N_GROUPS, N_EXPERTS // N_GROUPS
    h = h_ref[...].reshape(M, D)
    logits = _dot_nt(rw_ref[...].astype(BF16), h)
    s = jax.nn.sigmoid(logits)
    sb = (s + bias_ref[...]).reshape(G, E, M)
    s = s.reshape(G, E, M)
    e_in = lax.broadcasted_iota(jnp.int32, (G, E, M), 1)
    g_id = lax.broadcasted_iota(jnp.int32, (G, 1, M), 0)
    e_id = lax.broadcasted_iota(jnp.int32, (G, E, M), 0) * E + e_in

    m1 = jnp.max(sb, axis=1, keepdims=True)
    first = jnp.min(jnp.where(sb == m1, e_in, E), axis=1, keepdims=True)
    m2 = jnp.max(jnp.where(e_in == first, NEG_INF, sb), axis=1, keepdims=True)
    gs = m1 + m2

    rank = jnp.zeros((G, 1, M), jnp.int32)
    for j in range(G):
        gj = gs[j:j + 1]
        beats = (gj > gs) | ((gj == gs) & (j < g_id))
        rank = rank + beats.astype(jnp.int32)
    gsel = rank < TOPK_GROUPS

    vals = jnp.where(gsel, sb, NEG_INF)
    w = jnp.zeros((G, E, M), F32)
    for _ in range(TOP_K):
        m = jnp.max(jnp.max(vals, axis=1, keepdims=True), axis=0, keepdims=True)
        cand = jnp.where(vals == m, e_id, N_EXPERTS)
        first = jnp.min(jnp.min(cand, axis=1, keepdims=True), axis=0, keepdims=True)
        hit = e_id == first
        w = jnp.where(hit, s, w)
        vals = jnp.where(hit, NEG_INF, vals)

    tot = jnp.sum(jnp.sum(w, axis=1, keepdims=True), axis=0, keepdims=True)
    o_ref[...] = (w / tot * ROUTED_SCALE).reshape(N_EXPERTS, M)


def route(h, router_w_t, router_bias, l, rows=512):
    B, T, D = h.shape
    bb, tt, nblk, ij = _row_blocks(B, T, rows)
    return pl.pallas_call(
        _route_kernel,
        grid=(nblk,),
        in_specs=[pl.BlockSpec((bb, tt, D), lambda i: ij(i) + (0,)),
                  pl.BlockSpec((None, N_EXPERTS, D), lambda i: (l, 0, 0)),
                  pl.BlockSpec((None, N_EXPERTS, 1), lambda i: (l, 0, 0))],
        out_specs=pl.BlockSpec((N_EXPERTS, bb * tt), lambda i: (0, i)),
        out_shape=jax.ShapeDtypeStruct((N_EXPERTS, B * T), F32),
        compiler_params=_cparams(1),
        name="route",
    )(h, router_w_t, router_bias.reshape(-1, N_EXPERTS, 1))


def _moe_dense_kernel(h_ref, gate_ref, wi_ref, wo_ref, si_ref, so_ref, x_ref, g2_ref, o_ref, acc_ref):
    e = pl.program_id(1)
    bb, tt, D = h_ref.shape
    h = h_ref[...].reshape(bb * tt, D)

    @pl.when(e == 0)
    def _():
        hu = _dot(h, si_ref[...].astype(BF16))
        act = _silu(hu[:, :SHARED_FF]) * hu[:, SHARED_FF:]
        acc_ref[...] = _dot(act.astype(BF16), so_ref[...].astype(BF16))

    hu = _dot(h, wi_ref[...].astype(BF16))
    act = _silu(hu[:, :EXPERT_FF]) * hu[:, EXPERT_FF:] * gate_ref[...]
    acc_ref[...] += _dot(act.astype(BF16), wo_ref[...].astype(BF16))

    @pl.when(e == pl.num_programs(1) - 1)
    def _():
        o_ref[...] = x_ref[...] + g2_ref[...] * acc_ref[...].reshape(bb, tt, D)


def moe_dense(h, gates3, exp_w_in, exp_w_out, sh_w_in, sh_w_out, x, mod, gate_idx, l, rows=1024):
    B, T, D = h.shape
    bb, tt, nblk, ij = _row_blocks(B, T, min(rows, B * T))
    M = bb * tt
    xspec = pl.BlockSpec((bb, tt, D), lambda i, e: ij(i) + (0,))
    return pl.pallas_call(
        _moe_dense_kernel,
        grid=(nblk, N_EXPERTS),
        in_specs=[xspec,
                  pl.BlockSpec((None, M, 1), lambda i, e: (e, i, 0)),
                  pl.BlockSpec((None, None, D, 2 * EXPERT_FF), lambda i, e: (l, e, 0, 0)),
                  pl.BlockSpec((None, None, EXPERT_FF, D), lambda i, e: (l, e, 0, 0)),
                  pl.BlockSpec((None, D, 2 * SHARED_FF), lambda i, e: (l, 0, 0)),
                  pl.BlockSpec((None, SHARED_FF, D), lambda i, e: (l, 0, 0)),
                  xspec,
                  pl.BlockSpec((bb, 1, D), lambda i, e: (ij(i)[0], 0, gate_idx))],
        out_specs=xspec,
        out_shape=jax.ShapeDtypeStruct((B, T, D), F32),
        scratch_shapes=[pltpu.VMEM((M, D), F32)],
        compiler_params=_cparams(2),
        name="moe_dense",
    )(h, gates3, exp_w_in, exp_w_out, sh_w_in, sh_w_out, x, mod)


def _shared_kv_kernel(x_ref, g_ref, w_ref, lg_ref, cos_ref, sin_ref, lat_ref, kr_ref):
    bb, tt, D = x_ref.shape
    xn = _rms(x_ref[...], g_ref[...]).reshape(bb * tt, D).astype(BF16)
    z = _dot(xn, w_ref[...].astype(BF16))
    lat = _rms(z[:, :KV_LORA], lg_ref[...])
    lat_ref[...] = lat.reshape(bb, tt, KV_LORA)
    zr = z[:, KV_LORA:KV_LORA + MLA_ROPE].reshape(bb, tt, MLA_ROPE)
    zq = z[:, KV_LORA + 128:KV_LORA + 128 + MLA_ROPE].reshape(bb, tt, MLA_ROPE)
    kr_ref[...] = zr * cos_ref[...] + zq * sin_ref[...]


def shared_kv(x, kv_in_g, w_kv, kv_lat_g, cos32, sin32, rows=512):
    B, T, D = x.shape
    bb, tt, nblk, ij = _row_blocks(B, T, rows)
    tspec = pl.BlockSpec((tt, MLA_ROPE), lambda i: (ij(i)[1], 0))
    return pl.pallas_call(
        _shared_kv_kernel,
        grid=(nblk,),
        in_specs=[pl.BlockSpec((bb, tt, D), lambda i: ij(i) + (0,)),
                  pl.BlockSpec((1, D), lambda i: (0, 0)),
                  pl.BlockSpec(w_kv.shape, lambda i: (0, 0)),
                  pl.BlockSpec((1, KV_LORA), lambda i: (0, 0)),
                  tspec, tspec],
        out_specs=[pl.BlockSpec((bb, tt, KV_LORA), lambda i: ij(i) + (0,)),
                   pl.BlockSpec((bb, tt, MLA_ROPE), lambda i: ij(i) + (0,))],
        out_shape=[jax.ShapeDtypeStruct((B, T, KV_LORA), F32),
                   jax.ShapeDtypeStruct((B, T, MLA_ROPE), F32)],
        compiler_params=_cparams(1),
        name="shared_kv",
    )(x, kv_in_g.reshape(1, D), w_kv, kv_lat_g.reshape(1, KV_LORA), cos32, sin32)


def _kv_expand_kernel(lat_ref, kr_ref, wk_ref, ek_ref, wv_ref, k_ref, v_ref):
    bb, tt, _ = lat_ref.shape
    lat = lat_ref[...].reshape(bb * tt, KV_LORA).astype(BF16)
    kr = kr_ref[...].reshape(bb * tt, MLA_ROPE).astype(BF16)
    k = _dot(lat, wk_ref[...].astype(BF16)) + _dot(kr, ek_ref[...].astype(BF16))
    k_ref[...] = k.reshape(bb, tt, k.shape[-1]).astype(k_ref.dtype)
    v = _dot(lat, wv_ref[...].astype(BF16))
    v_ref[...] = v.reshape(bb, tt, v.shape[-1]).astype(v_ref.dtype)


def kv_expand(lat, kr, wk_pad, ek, wv, rows=512):
    B, T, _ = lat.shape
    bb, tt, nblk, ij = _row_blocks(B, T, rows)
    NK, NV = wk_pad.shape[1], wv.shape[1]

    def full(a):
        return pl.BlockSpec(a.shape, lambda i: (0, 0))

    def rowspec(n):
        return pl.BlockSpec((bb, tt, n), lambda i: ij(i) + (0,))

    return pl.pallas_call(
        _kv_expand_kernel,
        grid=(nblk,),
        in_specs=[rowspec(KV_LORA), rowspec(MLA_ROPE), full(wk_pad), full(ek), full(wv)],
        out_specs=[rowspec(NK), rowspec(NV)],
        out_shape=[jax.ShapeDtypeStruct((B, T, NK), BF16), jax.ShapeDtypeStruct((B, T, NV), BF16)],
        compiler_params=_cparams(1),
        name="kv_expand",
    )(lat, kr, wk_pad, ek, wv)


def _query_kernel(h_ref, wdq_ref, qg_ref, wq_ref, wqr_ref, c_ref, s_ref, q_ref, wdq_b, wq_b, wqr_b):
    @pl.when(pl.program_id(0) == 0)
    def _():
        wdq_b[...] = wdq_ref[...].astype(BF16)
        wq_b[...] = wq_ref[...].astype(BF16)
        wqr_b[...] = wqr_ref[...].astype(BF16)

    bb, tt, D = h_ref.shape
    h = h_ref[...].reshape(bb * tt, D)
    cq = _rms(_dot(h, wdq_b[...]), qg_ref[...]).astype(BF16)
    q1 = _dot(cq, wq_b[...]).reshape(bb, tt, -1)
    q2 = _dot(cq, wqr_b[...]).reshape(bb, tt, -1)
    c = c_ref[...]
    s = s_ref[...]
    for hd in range(MLA_HEADS):
        sl = slice(hd * HEAD_PAD, (hd + 1) * HEAD_PAD)
        q_ref[:, :, sl] = (q1[:, :, sl] * c + q2[:, :, sl] * s).astype(q_ref.dtype)


def mla_queries(h, w_dq, q_norm_g, wq_pad, wq_rot, l, c128, s128, rows=512):
    B, T, D = h.shape
    bb, tt, nblk, ij = _row_blocks(B, T, rows)
    NQ = wq_pad.shape[-1]
    tspec = pl.BlockSpec((tt, HEAD_PAD), lambda i: (ij(i)[1], 0))
    return pl.pallas_call(
        _query_kernel,
        grid=(nblk,),
        in_specs=[pl.BlockSpec((bb, tt, D), lambda i: ij(i) + (0,)),
                  pl.BlockSpec((None, D, Q_LORA), lambda i: (l, 0, 0)),
                  pl.BlockSpec((None, 1, Q_LORA), lambda i: (l, 0, 0)),
                  pl.BlockSpec((None, Q_LORA, NQ), lambda i: (l, 0, 0)),
                  pl.BlockSpec((None, Q_LORA, NQ), lambda i: (l, 0, 0)),
                  tspec, tspec],
        out_specs=pl.BlockSpec((bb, tt, NQ), lambda i: ij(i) + (0,)),
        out_shape=jax.ShapeDtypeStruct((B, T, NQ), BF16),
        scratch_shapes=[pltpu.VMEM((D, Q_LORA), BF16), pltpu.VMEM((Q_LORA, NQ), BF16),
                        pltpu.VMEM((Q_LORA, NQ), BF16)],
        compiler_params=_cparams(1),
        name="mla_queries",
    )(h, w_dq, q_norm_g.reshape(-1, 1, Q_LORA), wq_pad, wq_rot, c128, s128)


def _attn_prompt_kernel(qi_tab, ki_tab, q_ref, k_ref, v_ref, o_ref, m_ref, l_ref, acc_ref, *, tq, tk):
    p_id = pl.program_id(1)
    qi = qi_tab[p_id]
    ki = ki_tab[p_id]

    @pl.when(ki == 0)
    def _():
        m_ref[...] = jnp.full_like(m_ref, NEG_INF)
        l_ref[...] = jnp.zeros_like(l_ref)
        acc_ref[...] = jnp.zeros_like(acc_ref)

    qchunk = (qi * tq + lax.broadcasted_iota(jnp.int32, (tq, tk), 0)) // CHUNK
    kchunk = (ki * tk + lax.broadcasted_iota(jnp.int32, (tq, tk), 1)) // CHUNK
    mask = kchunk <= qchunk
    left = lax.broadcasted_iota(jnp.int32, (tq, HEAD_PAD), 1) < MLA_V

    for pair in range(MLA_HEADS // 2):
        vp = v_ref[0, :, pair * 128:(pair + 1) * 128]
        pv, alpha = [], []
        for hd in (2 * pair, 2 * pair + 1):
            sl = slice(hd * HEAD_PAD, (hd + 1) * HEAD_PAD)
            s = jnp.where(mask, _dot_nt(q_ref[0, :, sl], k_ref[0, :, sl]), NEG_INF)
            m_prev = m_ref[hd]
            m_new = jnp.maximum(m_prev, jnp.max(s, axis=-1, keepdims=True))
            a = jnp.exp(m_prev - m_new)
            p = jnp.exp(s - m_new[:, :1])
            l_ref[hd] = a * l_ref[hd] + jnp.sum(p, axis=-1, keepdims=True)
            m_ref[hd] = m_new
            pv.append(_dot(p.astype(BF16), vp))
            alpha.append(a)
        psl = slice(pair * 128, (pair + 1) * 128)
        acc_ref[:, psl] = jnp.where(left, alpha[0], alpha[1]) * acc_ref[:, psl] + jnp.where(left, pv[0], pv[1])

    @pl.when(ki == qi)
    def _():
        for pair in range(MLA_HEADS // 2):
            psl = slice(pair * 128, (pair + 1) * 128)
            lsum = jnp.where(left, l_ref[2 * pair], l_ref[2 * pair + 1])
            o_ref[0, :, psl] = (acc_ref[:, psl] / lsum).astype(o_ref.dtype)


def attn_prompt(q, k, v, tq=256):
    B, T, NQ = q.shape
    NV = v.shape[-1]
    tk = tq
    assert tq % CHUNK == 0
    nq = T // tq
    pairs = [(a, b) for a in range(nq) for b in range(a + 1)]
    qi_tab = jnp.asarray([a for a, _ in pairs], jnp.int32)
    ki_tab = jnp.asarray([b for _, b in pairs], jnp.int32)
    grid_spec = pltpu.PrefetchScalarGridSpec(
        num_scalar_prefetch=2,
        grid=(B, len(pairs)),
        in_specs=[pl.BlockSpec((1, tq, NQ), lambda b, p, qt, kt: (b, qt[p], 0)),
                  pl.BlockSpec((1, tk, NQ), lambda b, p, qt, kt: (b, kt[p], 0)),
                  pl.BlockSpec((1, tk, NV), lambda b, p, qt, kt: (b, kt[p], 0))],
        out_specs=pl.BlockSpec((1, tq, NV), lambda b, p, qt, kt: (b, qt[p], 0)),
        scratch_shapes=[pltpu.VMEM((MLA_HEADS, tq, HEAD_PAD), F32),
                        pltpu.VMEM((MLA_HEADS, tq, HEAD_PAD), F32),
                        pltpu.VMEM((tq, NV), F32)],
    )
    return pl.pallas_call(
        functools.partial(_attn_prompt_kernel, tq=tq, tk=tk),
        grid_spec=grid_spec,
        out_shape=jax.ShapeDtypeStruct((B, T, NV), BF16),
        compiler_params=_cparams(2),
        name="attn_prompt",
    )(qi_tab, ki_tab, q, k, v)


def _absorb_kernel(q_ref, m_ref, o_ref):
    o_ref[...] = _dot(q_ref[...], m_ref[...].astype(BF16)).astype(o_ref.dtype)


def absorb_queries(q2d, m_abs):
    N = q2d.shape[0]
    H, _, W = m_abs.shape
    return pl.pallas_call(
        _absorb_kernel,
        grid=(H,),
        in_specs=[pl.BlockSpec((N, HEAD_PAD), lambda h: (0, h)),
                  pl.BlockSpec((None, HEAD_PAD, W), lambda h: (h, 0, 0))],
        out_specs=pl.BlockSpec((None, N, W), lambda h: (h, 0, 0)),
        out_shape=jax.ShapeDtypeStruct((H, N, W), BF16),
        compiler_params=_cparams(1),
        name="absorb_queries",
    )(q2d, m_abs)


def _attn_sample_kernel(q_ref, lat_ref, kr_ref, nlat_ref, nkr_ref, o_ref, m_ref, l_ref, acc_ref):
    kb = pl.program_id(1)
    H, Q, W = q_ref.shape
    q = q_ref[...].reshape(H * Q, W)
    q_lat = q[:, :KV_LORA]
    q_rope = q[:, KV_LORA:KV_LORA + MLA_ROPE]

    def update(lat, kr):
        lat = lat.astype(BF16)
        s = _dot_nt(q_lat, lat) + _dot_nt(q_rope, kr.astype(BF16))
        m_prev = m_ref[...]
        m_new = jnp.maximum(m_prev, jnp.max(s, axis=-1, keepdims=True))
        a = jnp.exp(m_prev - m_new)
        p = jnp.exp(s - m_new[:, :1])
        l_ref[...] = a * l_ref[...] + jnp.sum(p, axis=-1, keepdims=True)
        m_ref[...] = m_new
        acc_ref[...] = jnp.concatenate([a, a], axis=-1) * acc_ref[...] + _dot(p.astype(BF16), lat)

    @pl.when(kb == 0)
    def _():
        m_ref[...] = jnp.full_like(m_ref, NEG_INF)
        l_ref[...] = jnp.zeros_like(l_ref)
        acc_ref[...] = jnp.zeros_like(acc_ref)
        update(nlat_ref[0], nkr_ref[0])

    update(lat_ref[0], kr_ref[0])

    @pl.when(kb == pl.num_programs(1) - 1)
    def _():
        lsum = l_ref[...]
        o = acc_ref[...] / jnp.concatenate([lsum, lsum], axis=-1)
        o_ref[...] = o.reshape(H, Q, KV_LORA).astype(o_ref.dtype)


def attn_sample(q_abs, cache_lat, cache_kr, new_lat, new_kr, tk=1024):
    H, N, W = q_abs.shape
    B, P, _ = cache_lat.shape
    Q = new_lat.shape[1]
    qpos = P + np.arange(Q)
    kpos = np.arange(P + Q)
    assert bool(np.all((kpos // CHUNK)[None, :] <= (qpos // CHUNK)[:, None]))
    return pl.pallas_call(
        _attn_sample_kernel,
        grid=(B, P // tk),
        in_specs=[pl.BlockSpec((H, Q, W), lambda b, kb: (0, b, 0)),
                  pl.BlockSpec((1, tk, KV_LORA), lambda b, kb: (b, kb, 0)),
                  pl.BlockSpec((1, tk, MLA_ROPE), lambda b, kb: (b, kb, 0)),
                  pl.BlockSpec((1, Q, KV_LORA), lambda b, kb: (b, 0, 0)),
                  pl.BlockSpec((1, Q, MLA_ROPE), lambda b, kb: (b, 0, 0))],
        out_specs=pl.BlockSpec((H, Q, KV_LORA), lambda b, kb: (0, b, 0)),
        out_shape=jax.ShapeDtypeStruct((H, N, KV_LORA), BF16),
        scratch_shapes=[pltpu.VMEM((H * Q, 128), F32), pltpu.VMEM((H * Q, 128), F32),
                        pltpu.VMEM((H * Q, KV_LORA), F32)],
        compiler_params=_cparams(2),
        name="attn_sample",
    )(q_abs, cache_lat, cache_kr, new_lat, new_kr)


def _unabsorb_kernel(o_ref, w_ref, out_ref):
    out_ref[...] = (_dot(o_ref[0], w_ref[0].astype(BF16))
                    + _dot(o_ref[1], w_ref[1].astype(BF16))).astype(out_ref.dtype)


def unabsorb(o_lat, wuv_pad):
    H, N, R = o_lat.shape
    return pl.pallas_call(
        _unabsorb_kernel,
        grid=(H // 2,),
        in_specs=[pl.BlockSpec((2, N, R), lambda p: (p, 0, 0)),
                  pl.BlockSpec((2, R, 128), lambda p: (p, 0, 0))],
        out_specs=pl.BlockSpec((N, 128), lambda p: (0, p)),
        out_shape=jax.ShapeDtypeStruct((N, (H // 2) * 128), BF16),
        compiler_params=_cparams(1),
        name="unabsorb",
    )(o_lat, wuv_pad)


def _rope_tables(pos):
    half = MLA_ROPE // 2
    inv = 1.0 / (ROPE_THETA ** (np.arange(half, dtype=np.float64) * 2.0 / MLA_ROPE))
    ang = np.asarray(pos, np.float64)[:, None] * inv[None, :]
    cos = np.concatenate([np.cos(ang), np.cos(ang)], axis=-1)
    sin = np.concatenate([np.sin(ang), np.sin(ang)], axis=-1)
    T = cos.shape[0]
    c128 = np.zeros((T, HEAD_PAD)); s128 = np.zeros((T, HEAD_PAD))
    c128[:, :MLA_NOPE] = 1.0
    c128[:, MLA_NOPE:MLA_NOPE + MLA_ROPE] = cos
    s128[:, MLA_NOPE:MLA_NOPE + MLA_ROPE] = sin
    return (jnp.asarray(cos, F32), jnp.asarray(sin, F32),
            jnp.asarray(c128 * QK_SCALE, F32), jnp.asarray(s128 * QK_SCALE, F32))


def _rot_half_cols(w):
    half = w.shape[-1] // 2
    return jnp.concatenate([-w[..., half:], w[..., :half]], axis=-1)


def _prep_weights(w_dkv, w_uk, w_uv, w_uq, router_w):
    D = D_MODEL
    w_lat, w_rope = w_dkv[:, :KV_LORA], w_dkv[:, KV_LORA:]
    pad96 = jnp.zeros((D, 128 - MLA_ROPE), F32)
    w_kv = jnp.concatenate([w_lat, w_rope, pad96, _rot_half_cols(w_rope), pad96], axis=-1)

    zpad = HEAD_PAD - MLA_NOPE
    wk_pad = jnp.pad(w_uk, ((0, 0), (0, 0), (0, zpad))).reshape(KV_LORA, MLA_HEADS * HEAD_PAD)
    ek = jnp.zeros((MLA_ROPE, MLA_HEADS, HEAD_PAD), F32)
    ek = ek.at[:, :, MLA_NOPE:MLA_NOPE + MLA_ROPE].set(
        jnp.broadcast_to(jnp.eye(MLA_ROPE, dtype=F32)[:, None, :], (MLA_ROPE, MLA_HEADS, MLA_ROPE)))
    ek = ek.reshape(MLA_ROPE, MLA_HEADS * HEAD_PAD)
    wv = w_uv.reshape(KV_LORA, MLA_HEADS * MLA_V)

    nb = w_uq.shape[0]
    qn, qr = w_uq[..., :MLA_NOPE], w_uq[..., MLA_NOPE:]
    z32 = jnp.zeros(qr.shape[:-1] + (HEAD_PAD - MLA_NOPE - MLA_ROPE,), F32)
    wq_pad = jnp.concatenate([qn, qr, z32], axis=-1).reshape(nb, Q_LORA, MLA_HEADS * HEAD_PAD)
    wq_rot = jnp.concatenate([jnp.zeros_like(qn), _rot_half_cols(qr), z32], axis=-1)
    wq_rot = wq_rot.reshape(nb, Q_LORA, MLA_HEADS * HEAD_PAD)

    m_abs = jnp.zeros((MLA_HEADS, HEAD_PAD, KV_LORA + 128), F32)
    m_abs = m_abs.at[:, :MLA_NOPE, :KV_LORA].set(jnp.transpose(w_uk, (1, 2, 0)))
    m_abs = m_abs.at[:, MLA_NOPE:MLA_NOPE + MLA_ROPE, KV_LORA:KV_LORA + MLA_ROPE].set(
        jnp.broadcast_to(jnp.eye(MLA_ROPE, dtype=F32), (MLA_HEADS, MLA_ROPE, MLA_ROPE)))

    wuv_h = jnp.transpose(w_uv, (1, 0, 2))
    even = jnp.pad(wuv_h, ((0, 0), (0, 0), (0, 64)))
    odd = jnp.pad(wuv_h, ((0, 0), (0, 0), (64, 0)))
    wuv_pad = jnp.where((jnp.arange(MLA_HEADS) % 2 == 0)[:, None, None], even, odd)

    rw_t = jnp.transpose(router_w, (0, 2, 1))
    return dict(w_kv=w_kv, wk_pad=wk_pad, ek=ek, wv=wv, wq_pad=wq_pad, wq_rot=wq_rot,
                m_abs=m_abs, wuv_pad=wuv_pad, rw_t=rw_t)


def _trunk(x, mod, pos, hg_state, past_lat, past_kr, P, W):
    B, T, D = x.shape
    n_a = P["hg_w_in"].shape[0]
    depth = P["norm1_g"].shape[0]
    lbs = jnp.cumsum(jax.nn.softmax(P["hg_lb_logits"].astype(F32), axis=0), axis=0)
    cos32, sin32, c128, s128 = _rope_tables(pos)
    hg_new = []
    lat = kr = k_all = v_all = None
    for l in range(depth):
        m = mod[l]
        h = norm_mod(x, P["norm1_g"][l], m, sc_idx=1, sh_idx=0)
        if l < n_a:
            z = linear(h, P["hg_w_in"], l, F32)
            o, s_new = gla(z, lbs[l], P["hg_onorm_g"][l], None if hg_state is None else hg_state[l])
            hg_new.append(s_new)
            x = linear(o, P["hg_w_out"], l, F32, x=x, mod=m, gate_idx=2)
        else:
            bi = l - n_a
            q = mla_queries(h, P["w_dq"], P["q_norm_g"], W["wq_pad"], W["wq_rot"], bi, c128, s128)
            if past_lat is None:
                o = attn_prompt(q, k_all, v_all)
            else:
                q_abs = absorb_queries(q.reshape(B * T, -1), W["m_abs"])
                o_lat = attn_sample(q_abs, past_lat, past_kr, lat, kr)
                o = unabsorb(o_lat, W["wuv_pad"]).reshape(B, T, -1)
            x = linear(o, P["w_o"], bi, F32, x=x, mod=m, gate_idx=2)
        h = norm_mod(x, P["norm2_g"][l], m, sc_idx=4, sh_idx=3)
        gates_t = route(h, W["rw_t"], P["router_bias"], l)
        x = moe_dense(h, gates_t.reshape(N_EXPERTS, B * T, 1), P["exp_w_in"], P["exp_w_out"],
                      P["sh_w_in"], P["sh_w_out"], x, m, 5, l)
        if l == n_a - 1:
            lat, kr = shared_kv(x, P["kv_in_g"], W["w_kv"], P["kv_lat_g"], cos32, sin32)
            if past_lat is None:
                k_all, v_all = kv_expand(lat, kr, W["wk_pad"], W["ek"], W["wv"])
    y = norm_mod(x, P["final_g"], out_dtype=F32)
    return y, jnp.stack(hg_new, axis=0), lat, kr


def kernel(x_prompt, x_sample, state_hgrn, cache_mla_latent, cache_mla_krope, c_prompt, c_sample, ada_w, ada_b, norm1_g, norm2_g, hg_w_in, hg_lb_logits, hg_onorm_g, hg_w_out, kv_in_g, w_dkv, kv_lat_g, w_uk, w_uv, w_dq, q_norm_g, w_uq, w_o, router_w, router_bias, exp_w_in, exp_w_out, sh_w_in, sh_w_out, final_g):
    Bp, Sp, _ = x_prompt.shape
    Bs, Ss, _ = x_sample.shape
    past = cache_mla_latent.shape[1]
    P = dict(norm1_g=norm1_g, norm2_g=norm2_g, hg_w_in=hg_w_in, hg_lb_logits=hg_lb_logits,
             hg_onorm_g=hg_onorm_g, hg_w_out=hg_w_out, kv_in_g=kv_in_g, kv_lat_g=kv_lat_g,
             w_dq=w_dq, q_norm_g=q_norm_g, w_o=w_o, router_bias=router_bias,
             exp_w_in=exp_w_in, exp_w_out=exp_w_out, sh_w_in=sh_w_in, sh_w_out=sh_w_out, final_g=final_g)
    W = _prep_weights(w_dkv, w_uk, w_uv, w_uq, router_w)
    mod = ada_mod(jnp.concatenate([c_prompt, c_sample], axis=0), ada_w, ada_b)
    mod_p = mod[:, :Bp, None, :]
    mod_s = mod[:, Bp:, None, :]
    y_p, st_p, lat_p, kr_p = _trunk(x_prompt, mod_p, np.arange(Sp), None, None, None, P, W)
    y_s, st_s, lat_s, kr_s = _trunk(x_sample, mod_s, past + np.arange(Ss), state_hgrn,
                                    cache_mla_latent, cache_mla_krope, P, W)
    return (y_p, y_s, st_p, st_s, lat_p, kr_p, lat_s, kr_s)
```

```python
import dataclasses
import functools

import numpy as np
import jax
import jax.numpy as jnp
from jax import lax
from jax.experimental import pallas as pl
from jax.experimental.pallas import tpu as pltpu
from jax.experimental.pallas import tpu_sc as plsc

F32 = jnp.float32
BF16 = jnp.bfloat16

D_MODEL = 1024
CHUNK = 64
HG_HEADS = 8
HG_DK = 128
HG_DV = 128
MLA_HEADS = 16
MLA_NOPE = 64
MLA_ROPE = 32
MLA_V = 64
Q_LORA = 384
KV_LORA = 256
ROPE_THETA = 10000.0
N_EXPERTS = 64
TOP_K = 8
N_GROUPS = 8
TOPK_GROUPS = 4
EXPERT_FF = 256
SHARED_FF = 256
ROUTED_SCALE = 2.5
EPS = 1e-6

HEAD_PAD = 128
QK_SCALE = (MLA_NOPE + MLA_ROPE) ** -0.5
VMEM_LIMIT = 56 * 1024 * 1024
NEG_INF = float("-inf")
SC_CORES = 2
SC_SUBCORES = 16
SC_WORKERS = SC_CORES * SC_SUBCORES
SC_LANES = 16
SC_WINDOW = 64
MOE_TILE = 512


def _cparams(n_axes):
    return pltpu.CompilerParams(dimension_semantics=("arbitrary",) * n_axes,
                                vmem_limit_bytes=VMEM_LIMIT)


def _silu(x):
    return x * jax.nn.sigmoid(x)


def _rms(x, g):
    ms = jnp.mean(x * x, axis=-1, keepdims=True)
    return x * lax.rsqrt(ms + EPS) * g


def _dot(a, b):
    return jnp.dot(a, b, preferred_element_type=F32)


def _dot_nt(a, b):
    return lax.dot_general(a, b, (((1,), (1,)), ((), ())), preferred_element_type=F32)


def _dot_tn(a, b):
    return lax.dot_general(a, b, (((0,), (0,)), ((), ())), preferred_element_type=F32)


def _row_blocks(B, T, rows):
    if T >= rows:
        assert T % rows == 0
        bb, tt = 1, rows
    else:
        assert rows % T == 0 and B % (rows // T) == 0
        bb, tt = rows // T, T
    nt = T // tt
    return bb, tt, (B // bb) * nt, (lambda i: (i // nt, i % nt))


def _ada_kernel(c_ref, w_ref, b_ref, o_ref):
    a = _silu(c_ref[...]).astype(BF16)
    o_ref[...] = _dot(a, w_ref[...].astype(BF16)) + b_ref[...]


def ada_mod(c, ada_w, ada_b):
    R, D = c.shape
    L, _, N = ada_w.shape
    tn = 1536
    return pl.pallas_call(
        _ada_kernel,
        grid=(L, N // tn),
        in_specs=[pl.BlockSpec((R, D), lambda l, j: (0, 0)),
                  pl.BlockSpec((None, D, tn), lambda l, j: (l, 0, j)),
                  pl.BlockSpec((None, 1, tn), lambda l, j: (l, 0, j))],
        out_specs=pl.BlockSpec((None, R, tn), lambda l, j: (l, 0, j)),
        out_shape=jax.ShapeDtypeStruct((L, R, N), F32),
        compiler_params=_cparams(2),
        name="ada_mod",
    )(c, ada_w, ada_b.reshape(L, 1, N))


def _pack_pairs(y):
    half = y.shape[-1] // 2
    bits = lax.bitcast_convert_type(y.astype(BF16).astype(F32), jnp.uint32)
    word = lax.shift_right_logical(bits[:, :half], jnp.uint32(16)) | bits[:, half:]
    return lax.bitcast_convert_type(word, jnp.int32)


def _unpack_pairs(word, dtype=BF16):
    u = lax.bitcast_convert_type(word, jnp.uint32)
    lo = lax.bitcast_convert_type(lax.shift_left(u, jnp.uint32(16)), F32)
    hi = lax.bitcast_convert_type(u & jnp.uint32(0xFFFF0000), F32)
    return lo.astype(dtype), hi.astype(dtype)


def _norm_kernel(*refs, modulated, packed):
    if modulated:
        x_ref, g_ref, sc_ref, sh_ref, o_ref = refs
    else:
        x_ref, g_ref, o_ref = refs
    y = _rms(x_ref[...], g_ref[...])
    if modulated:
        y = y * (1.0 + sc_ref[...]) + sh_ref[...]
    if packed:
        bb, tt, D = y.shape
        o_ref[...] = _pack_pairs(y.reshape(bb * tt, D))
    else:
        o_ref[...] = y.astype(o_ref.dtype)


def norm_mod(x, g, mod=None, sc_idx=0, sh_idx=0, out_dtype=BF16, rows=512, packed=False):
    B, T, D = x.shape
    bb, tt, nblk, ij = _row_blocks(B, T, rows)
    xspec = pl.BlockSpec((bb, tt, D), lambda i: ij(i) + (0,))
    in_specs = [xspec, pl.BlockSpec((1, D), lambda i: (0, 0))]
    args = [x, g.reshape(1, D)]
    if mod is not None:
        in_specs += [pl.BlockSpec((bb, 1, D), lambda i: (ij(i)[0], 0, sc_idx)),
                     pl.BlockSpec((bb, 1, D), lambda i: (ij(i)[0], 0, sh_idx))]
        args += [mod, mod]
    if packed:
        out_specs = pl.BlockSpec((bb * tt, D // 2), lambda i: (i, 0))
        out_shape = jax.ShapeDtypeStruct((B * T, D // 2), jnp.int32)
    else:
        out_specs = xspec
        out_shape = jax.ShapeDtypeStruct((B, T, D), out_dtype)
    return pl.pallas_call(
        functools.partial(_norm_kernel, modulated=mod is not None, packed=packed),
        grid=(nblk,),
        in_specs=in_specs,
        out_specs=out_specs,
        out_shape=out_shape,
        compiler_params=_cparams(1),
        name="norm_mod",
    )(*args)


def _linear_kernel(*refs, residual):
    if residual:
        a_ref, w_ref, x_ref, gate_ref, o_ref, wb_ref = refs
    else:
        a_ref, w_ref, o_ref, wb_ref = refs

    @pl.when(pl.program_id(1) == 0)
    def _():
        wb_ref[...] = w_ref[...].astype(BF16)

    bb, tt, K = a_ref.shape
    y = _dot(a_ref[...].reshape(bb * tt, K).astype(BF16), wb_ref[...])
    y = y.reshape(bb, tt, y.shape[-1])
    if residual:
        y = x_ref[...] + gate_ref[...] * y
    o_ref[...] = y.astype(o_ref.dtype)


def linear(a, w, l, out_dtype, x=None, mod=None, gate_idx=0, rows=512, tn=1024):
    B, T, K = a.shape
    _, _, N = w.shape
    tn = min(tn, N)
    bb, tt, nblk, ij = _row_blocks(B, T, rows)
    in_specs = [pl.BlockSpec((bb, tt, K), lambda j, i: ij(i) + (0,)),
                pl.BlockSpec((None, K, tn), lambda j, i: (l, 0, j))]
    args = [a, w]
    ospec = pl.BlockSpec((bb, tt, tn), lambda j, i: ij(i) + (j,))
    if x is not None:
        gsteps = D_MODEL // tn
        in_specs += [ospec, pl.BlockSpec((bb, 1, tn), lambda j, i: (ij(i)[0], 0, gate_idx * gsteps + j))]
        args += [x, mod]
    return pl.pallas_call(
        functools.partial(_linear_kernel, residual=x is not None),
        grid=(N // tn, nblk),
        in_specs=in_specs,
        out_specs=ospec,
        out_shape=jax.ShapeDtypeStruct((B, T, N), out_dtype),
        scratch_shapes=[pltpu.VMEM((K, tn), BF16)],
        compiler_params=_cparams(2),
        name="linear",
    )(*args)


def _gla_kernel(*refs, L, n_chunks, has_init):
    if has_init:
        q_ref, f_ref, i_ref, g_ref, lb_ref, on_ref, s0_ref, o_ref, so_ref, st_ref = refs
    else:
        q_ref, f_ref, i_ref, g_ref, lb_ref, on_ref, o_ref, so_ref, st_ref = refs
    t = pl.program_id(2)

    @pl.when(t == 0)
    def _():
        if has_init:
            st_ref[...] = s0_ref[0, 0].T
        else:
            st_ref[...] = jnp.zeros_like(st_ref)

    lb = lb_ref[...]
    row = lax.broadcasted_iota(jnp.int32, (L, L), 0)
    col = lax.broadcasted_iota(jnp.int32, (L, L), 1)
    causal = col <= row
    tri = causal.astype(F32)
    mid = L // 2 - 1

    def chunk(c, carry):
        r0 = pl.multiple_of(c * L, L)
        q = _silu(q_ref[0, pl.ds(r0, L), :])
        fg = lb + (1.0 - lb) * jax.nn.sigmoid(f_ref[0, pl.ds(r0, L), :])
        k = 1.0 - fg
        v = i_ref[0, pl.ds(r0, L), :].astype(BF16)
        b = jnp.dot(tri, jnp.log(fg), preferred_element_type=F32, precision=lax.Precision.HIGHEST)
        b_mid = b[mid:mid + 1, :]
        b_last = b[L - 1:L, :]
        st = st_ref[...]
        o = _dot_nt((q * jnp.exp(b)).astype(BF16), st.astype(BF16))
        qa = (q * jnp.exp(b - b_mid)).astype(BF16)
        kb = (k * jnp.exp(b_mid - b)).astype(BF16)
        scores = jnp.where(causal, _dot_nt(qa, kb), 0.0)
        o = o + _dot(scores.astype(BF16), v)
        kd = (k * jnp.exp(b_last - b)).astype(BF16)
        st_ref[...] = st * jnp.exp(b_last) + _dot_tn(v, kd)
        o = _rms(o, on_ref[...]) * _silu(g_ref[0, pl.ds(r0, L), :])
        o_ref[0, pl.ds(r0, L), :] = o.astype(o_ref.dtype)
        return carry

    lax.fori_loop(0, n_chunks, chunk, 0)

    @pl.when(t == pl.num_programs(2) - 1)
    def _():
        so_ref[0, 0] = st_ref[...].T


def gla(z, lb, onorm_g, s0):
    B, T, _ = z.shape
    L = CHUNK if T % CHUNK == 0 else T
    tt = min(T, 512)
    n_chunks = tt // L
    H = HG_HEADS

    def zspec(part):
        return pl.BlockSpec((1, tt, HG_DK), lambda b, h, t: (b, t, part * H + h))

    hspec = pl.BlockSpec((1, HG_DK), lambda b, h, t: (0, h))
    sspec = pl.BlockSpec((1, 1, HG_DK, HG_DV), lambda b, h, t: (b, h, 0, 0))
    in_specs = [zspec(0), zspec(1), zspec(2), zspec(3), hspec, hspec]
    args = [z, z, z, z, lb.reshape(1, D_MODEL), onorm_g.reshape(1, D_MODEL)]
    if s0 is not None:
        in_specs.append(sspec)
        args.append(s0)
    return pl.pallas_call(
        functools.partial(_gla_kernel, L=L, n_chunks=n_chunks, has_init=s0 is not None),
        grid=(B, H, T // tt),
        in_specs=in_specs,
        out_specs=[pl.BlockSpec((1, tt, HG_DV), lambda b, h, t: (b, t, h)), sspec],
        out_shape=[jax.ShapeDtypeStruct((B, T, D_MODEL), BF16),
                   jax.ShapeDtypeStruct((B, H, HG_DK, HG_DV), F32)],
        scratch_shapes=[pltpu.VMEM((HG_DV, HG_DK), F32)],
        compiler_params=_cparams(3),
        name="gla",
    )(*args)


def _route_kernel(h_ref, rw_ref, bias_ref, pos_ref, w_ref, te_ref, nu_ref,
                  e_s, r_s, base_s, start_s, *, tile_rows):
    ph = pl.program_id(0)
    i = pl.program_id(1)
    M = h_ref.shape[0]
    half = h_ref.shape[1]
    G, E = N_GROUPS, N_EXPERTS // N_GROUPS
    e_flat = lax.broadcasted_iota(jnp.int32, (N_EXPERTS, M), 0)

    @pl.when(ph == 1)
    def _():
        @pl.when(i == 0)
        def _():
            cnt = base_s[...]
            padded = jnp.floor((cnt + (tile_rows - 1)) * (1.0 / tile_rows)) * tile_rows
            r = lax.broadcasted_iota(jnp.int32, (N_EXPERTS, N_EXPERTS), 0)
            c = lax.broadcasted_iota(jnp.int32, (N_EXPERTS, N_EXPERTS), 1)
            start = jnp.dot((c < r).astype(F32), padded, preferred_element_type=F32,
                            precision=lax.Precision.HIGHEST)
            start_s[...] = start
            seg_end = (start + padded)[:, :1]
            nt = te_ref.shape[1]
            tile_lo = (lax.broadcasted_iota(jnp.int32, (N_EXPERTS, nt), 1) * tile_rows).astype(F32)
            owner = jnp.sum((seg_end <= tile_lo).astype(F32), axis=0, keepdims=True)
            te_ref[...] = jnp.minimum(owner, N_EXPERTS - 1.0).astype(jnp.int32)
            total = jnp.max(start + padded, axis=0, keepdims=True)
            nu_ref[...] = (total * (1.0 / tile_rows)).astype(jnp.int32)

        start_col = start_s[:, :1]
        for k in range(TOP_K):
            hit = e_flat == e_s[i, k:k + 1, :]
            seg = jnp.sum(jnp.where(hit, start_col, 0.0), axis=0, keepdims=True)
            pos_ref[k:k + 1, :] = (seg + r_s[i, k:k + 1, :]).astype(jnp.int32)

    @pl.when(ph == 0)
    def _():
        _route_pass0(h_ref, rw_ref, bias_ref, w_ref, e_s, r_s, base_s, i, M, half, G, E)


def _route_pass0(h_ref, rw_ref, bias_ref, w_ref, e_s, r_s, base_s, i, M, half, G, E):
    @pl.when(i == 0)
    def _():
        base_s[...] = jnp.zeros_like(base_s)

    lo, hi = _unpack_pairs(h_ref[...])
    rw = rw_ref[...].astype(BF16)
    logits = _dot_nt(rw[:, :half], lo) + _dot_nt(rw[:, half:], hi)
    s = jax.nn.sigmoid(logits)
    sb = (s + bias_ref[...]).reshape(G, E, M)
    s = s.reshape(G, E, M)
    e_in = lax.broadcasted_iota(jnp.int32, (G, E, M), 1)
    g_id = lax.broadcasted_iota(jnp.int32, (G, 1, M), 0)
    e_id = lax.broadcasted_iota(jnp.int32, (G, E, M), 0) * E + e_in

    m1 = jnp.max(sb, axis=1, keepdims=True)
    first = jnp.min(jnp.where(sb == m1, e_in, E), axis=1, keepdims=True)
    m2 = jnp.max(jnp.where(e_in == first, NEG_INF, sb), axis=1, keepdims=True)
    gs = m1 + m2

    rank = jnp.zeros((G, 1, M), jnp.int32)
    for j in range(G):
        gj = gs[j:j + 1]
        beats = (gj > gs) | ((gj == gs) & (j < g_id))
        rank = rank + beats.astype(jnp.int32)
    gsel = rank < TOPK_GROUPS

    vals = jnp.where(gsel, sb, NEG_INF)
    w = jnp.zeros((G, E, M), F32)
    selm = jnp.zeros((G, E, M), F32)
    chosen = []
    for _ in range(TOP_K):
        m = jnp.max(jnp.max(vals, axis=1, keepdims=True), axis=0, keepdims=True)
        cand = jnp.where(vals == m, e_id, N_EXPERTS)
        first = jnp.min(jnp.min(cand, axis=1, keepdims=True), axis=0, keepdims=True)
        hit = e_id == first
        w = jnp.where(hit, s, w)
        selm = jnp.where(hit, 1.0, selm)
        vals = jnp.where(hit, NEG_INF, vals)
        chosen.append(first.reshape(1, M))

    tot = jnp.sum(jnp.sum(w, axis=1, keepdims=True), axis=0, keepdims=True)
    gates = (w / tot * ROUTED_SCALE).reshape(N_EXPERTS, M)
    selm = selm.reshape(N_EXPERTS, M)

    earlier = (lax.broadcasted_iota(jnp.int32, (M, M), 0)
               < lax.broadcasted_iota(jnp.int32, (M, M), 1)).astype(BF16)
    rank = base_s[:, :1] + _dot(selm.astype(BF16), earlier)
    base_s[...] = base_s[...] + jnp.sum(selm, axis=1, keepdims=True)
    e_flat = lax.broadcasted_iota(jnp.int32, (N_EXPERTS, M), 0)
    for k in range(TOP_K):
        hit = e_flat == chosen[k]
        e_s[i, k:k + 1, :] = chosen[k]
        r_s[i, k:k + 1, :] = jnp.sum(jnp.where(hit, rank, 0.0), axis=0, keepdims=True)
        w_ref[k:k + 1, :] = jnp.sum(jnp.where(hit, gates, 0.0), axis=0, keepdims=True)


def route(hp, router_w_t, router_bias, l, tile_rows, n_tiles, rows=512):
    N, half = hp.shape
    M = rows
    nT = N // M
    assert N % M == 0
    nt_pad = -(-n_tiles // 128) * 128

    def p0(ph, i):
        return i * (1 - ph) + (nT - 1) * ph

    return pl.pallas_call(
        functools.partial(_route_kernel, tile_rows=tile_rows),
        grid=(2, nT),
        in_specs=[pl.BlockSpec((M, half), lambda ph, i: (p0(ph, i), 0)),
                  pl.BlockSpec((None, N_EXPERTS, 2 * half), lambda ph, i: (l, 0, 0)),
                  pl.BlockSpec((None, N_EXPERTS, 1), lambda ph, i: (l, 0, 0))],
        out_specs=[pl.BlockSpec((TOP_K, M), lambda ph, i: (0, i * ph)),
                   pl.BlockSpec((TOP_K, M), lambda ph, i: (0, p0(ph, i))),
                   pl.BlockSpec((1, nt_pad), lambda ph, i: (0, 0)),
                   pl.BlockSpec((1, 128), lambda ph, i: (0, 0))],
        out_shape=[jax.ShapeDtypeStruct((TOP_K, N), jnp.int32),
                   jax.ShapeDtypeStruct((TOP_K, N), F32),
                   jax.ShapeDtypeStruct((1, nt_pad), jnp.int32),
                   jax.ShapeDtypeStruct((1, 128), jnp.int32)],
        scratch_shapes=[pltpu.VMEM((nT, TOP_K, M), jnp.int32), pltpu.VMEM((nT, TOP_K, M), F32),
                        pltpu.VMEM((N_EXPERTS, 128), F32), pltpu.VMEM((N_EXPERTS, 128), F32)],
        compiler_params=_cparams(2),
        name="route",
    )(hp, router_w_t, router_bias.reshape(-1, N_EXPERTS, 1))


def _sc_mesh():
    return plsc.VectorSubcoreMesh(core_axis_name="core", subcore_axis_name="subcore")


def sc_invert(pos_flat, n_tok, n_out):
    n = pos_flat.shape[0]
    per = n_out // SC_WORKERS
    chunk = n_tok // 16
    assert n_out % SC_WORKERS == 0 and per % SC_LANES == 0
    assert n_tok % chunk == 0 and n % chunk == 0 and chunk % SC_LANES == 0
    cp = pltpu.CompilerParams()
    if "needs_layout_passes" in pltpu.CompilerParams.__dataclass_fields__:
        cp = dataclasses.replace(cp, needs_layout_passes=False)

    @functools.partial(
        pl.kernel, out_type=jax.ShapeDtypeStruct((n_out,), jnp.int32), mesh=_sc_mesh(),
        scratch_types=[pltpu.VMEM((chunk,), jnp.int32), pltpu.VMEM((per,), jnp.int32)],
        compiler_params=cp, name="sc_invert")
    def k(pos_hbm, src_hbm, pos_v, src_v):
        wid = lax.axis_index("subcore") * SC_CORES + lax.axis_index("core")
        lo = wid * per
        lane = lax.iota(jnp.int32, SC_LANES)

        @pl.loop(0, per, step=SC_LANES)
        def _(r):
            src_v[pl.ds(r, SC_LANES)] = jnp.zeros((SC_LANES,), jnp.int32)

        @pl.loop(0, n // chunk)
        def _(c):
            base = c * chunk
            pltpu.sync_copy(pos_hbm.at[pl.ds(base, chunk)], pos_v)
            tok0 = lax.rem(base, n_tok)

            @pl.loop(0, chunk, step=SC_LANES)
            def _(r):
                p = pos_v[pl.ds(r, SC_LANES)] - lo
                mine = (p >= 0) & (p < per)
                plsc.store_scatter(src_v, [jnp.where(mine, p, 0)], tok0 + r + lane, mask=mine)

        pltpu.sync_copy(src_v, src_hbm.at[pl.ds(lo, per)])

    return k(pos_flat)


def sc_gather(x, idx):
    n = idx.shape[0]
    dim = x.shape[1]
    assert n % (SC_WINDOW * SC_WORKERS) == 0

    @functools.partial(
        pl.kernel, out_type=jax.ShapeDtypeStruct((n, dim), x.dtype), mesh=_sc_mesh(),
        scratch_types=[], name="sc_gather")
    def k(x_hbm, i_hbm, o_hbm):
        def body(i_vmem, o_vmem):
            pltpu.sync_copy(x_hbm.at[i_vmem.at[0]], o_vmem)

        pltpu.emit_pipeline(
            body, grid=(n // SC_WINDOW,),
            in_specs=[pl.BlockSpec((1, SC_WINDOW), index_map=lambda i: (i, 0))],
            out_specs=[pl.BlockSpec((SC_WINDOW, dim), index_map=lambda i: (i, 0))],
            core_axis_name=("core", "subcore"),
            dimension_semantics=(pltpu.PARALLEL,),
        )(i_hbm, o_hbm)

    return k(x, idx.reshape(n // SC_WINDOW, SC_WINDOW))


def _moe_gemm_kernel(te_ref, nu_ref, x_ref, wi_ref, wo_ref, o_ref, wi_b, wo_b):
    j = pl.program_id(0)
    e_now = te_ref[j]
    e_prev = te_ref[jnp.maximum(j - 1, 0)]

    @pl.when((j == 0) | (e_now != e_prev))
    def _():
        wi_b[...] = wi_ref[...].astype(BF16)
        wo_b[...] = wo_ref[...].astype(BF16)

    @pl.when(j < nu_ref[0])
    def _():
        lo, hi = _unpack_pairs(x_ref[...])
        half = lo.shape[1]
        hu = _dot(lo, wi_b[:half, :]) + _dot(hi, wi_b[half:, :])
        act = (_silu(hu[:, :EXPERT_FF]) * hu[:, EXPERT_FF:]).astype(BF16)
        o_ref[...] = _pack_pairs(_dot(act, wo_b[...]))

    @pl.when(j >= nu_ref[0])
    def _():
        o_ref[...] = jnp.zeros_like(o_ref)


def moe_gemm(xs, tile_expert, n_used, exp_w_in, exp_w_out, l, tile_rows, n_tiles):
    P, half = xs.shape
    D = 2 * half
    assert P == n_tiles * tile_rows
    grid_spec = pltpu.PrefetchScalarGridSpec(
        num_scalar_prefetch=2,
        grid=(n_tiles,),
        in_specs=[pl.BlockSpec((tile_rows, half), lambda j, te, nu: (j, 0)),
                  pl.BlockSpec((None, None, D, 2 * EXPERT_FF), lambda j, te, nu: (l, te[j], 0, 0)),
                  pl.BlockSpec((None, None, EXPERT_FF, D), lambda j, te, nu: (l, te[j], 0, 0))],
        out_specs=pl.BlockSpec((tile_rows, half), lambda j, te, nu: (j, 0)),
        scratch_shapes=[pltpu.VMEM((D, 2 * EXPERT_FF), BF16), pltpu.VMEM((EXPERT_FF, D), BF16)],
    )
    return pl.pallas_call(
        _moe_gemm_kernel,
        grid_spec=grid_spec,
        out_shape=jax.ShapeDtypeStruct((P, half), jnp.int32),
        compiler_params=_cparams(1),
        name="moe_gemm",
    )(tile_expert, n_used, xs, exp_w_in, exp_w_out)


def _moe_combine_kernel(y_ref, w_ref, h_ref, si_ref, so_ref, x_ref, g2_ref, o_ref, si_b, so_b):
    @pl.when(pl.program_id(0) == 0)
    def _():
        si_b[...] = si_ref[...].astype(BF16)
        so_b[...] = so_ref[...].astype(BF16)

    bb, tt, D = x_ref.shape
    half = D // 2
    w = w_ref[...]
    acc_lo = jnp.zeros((bb * tt, half), F32)
    acc_hi = jnp.zeros((bb * tt, half), F32)
    for k in range(TOP_K):
        lo, hi = _unpack_pairs(y_ref[k], F32)
        acc_lo = acc_lo + w[:, k:k + 1] * lo
        acc_hi = acc_hi + w[:, k:k + 1] * hi
    hlo, hhi = _unpack_pairs(h_ref[...])
    hu = _dot(hlo, si_b[:half, :]) + _dot(hhi, si_b[half:, :])
    act = (_silu(hu[:, :SHARED_FF]) * hu[:, SHARED_FF:]).astype(BF16)
    y = jnp.concatenate([acc_lo, acc_hi], axis=-1) + _dot(act, so_b[...])
    o_ref[...] = x_ref[...] + g2_ref[...] * y.reshape(bb, tt, D)


def moe_combine(y8, w_t, hp, sh_w_in, sh_w_out, x, mod, gate_idx, l, row0, rows=256):
    B, T, D = x.shape
    half = D // 2
    bb, tt, nblk, ij = _row_blocks(B, T, rows)
    M = bb * tt
    assert row0 % M == 0
    off = row0 // M
    xspec = pl.BlockSpec((bb, tt, D), lambda i: ij(i) + (0,))
    return pl.pallas_call(
        _moe_combine_kernel,
        grid=(nblk,),
        in_specs=[pl.BlockSpec((TOP_K, M, half), lambda i: (0, off + i, 0)),
                  pl.BlockSpec((M, TOP_K), lambda i: (off + i, 0)),
                  pl.BlockSpec((M, half), lambda i: (off + i, 0)),
                  pl.BlockSpec((None, D, 2 * SHARED_FF), lambda i: (l, 0, 0)),
                  pl.BlockSpec((None, SHARED_FF, D), lambda i: (l, 0, 0)),
                  xspec,
                  pl.BlockSpec((bb, 1, D), lambda i: (ij(i)[0], 0, gate_idx))],
        out_specs=xspec,
        out_shape=jax.ShapeDtypeStruct((B, T, D), F32),
        scratch_shapes=[pltpu.VMEM((D, 2 * SHARED_FF), BF16), pltpu.VMEM((SHARED_FF, D), BF16)],
        compiler_params=_cparams(1),
        name="moe_combine",
    )(y8, w_t, hp, sh_w_in, sh_w_out, x, mod)


def _shared_kv_kernel(x_ref, g_ref, w_ref, lg_ref, cos_ref, sin_ref, lat_ref, kr_ref):
    bb, tt, D = x_ref.shape
    xn = _rms(x_ref[...], g_ref[...]).reshape(bb * tt, D).astype(BF16)
    z = _dot(xn, w_ref[...].astype(BF16))
    lat = _rms(z[:, :KV_LORA], lg_ref[...])
    lat_ref[...] = lat.reshape(bb, tt, KV_LORA)
    zr = z[:, KV_LORA:KV_LORA + MLA_ROPE].reshape(bb, tt, MLA_ROPE)
    zq = z[:, KV_LORA + 128:KV_LORA + 128 + MLA_ROPE].reshape(bb, tt, MLA_ROPE)
    kr_ref[...] = zr * cos_ref[...] + zq * sin_ref[...]


def shared_kv(x, kv_in_g, w_kv, kv_lat_g, cos32, sin32, rows=512):
    B, T, D = x.shape
    bb, tt, nblk, ij = _row_blocks(B, T, rows)
    tspec = pl.BlockSpec((tt, MLA_ROPE), lambda i: (ij(i)[1], 0))
    return pl.pallas_call(
        _shared_kv_kernel,
        grid=(nblk,),
        in_specs=[pl.BlockSpec((bb, tt, D), lambda i: ij(i) + (0,)),
                  pl.BlockSpec((1, D), lambda i: (0, 0)),
                  pl.BlockSpec(w_kv.shape, lambda i: (0, 0)),
                  pl.BlockSpec((1, KV_LORA), lambda i: (0, 0)),
                  tspec, tspec],
        out_specs=[pl.BlockSpec((bb, tt, KV_LORA), lambda i: ij(i) + (0,)),
                   pl.BlockSpec((bb, tt, MLA_ROPE), lambda i: ij(i) + (0,))],
        out_shape=[jax.ShapeDtypeStruct((B, T, KV_LORA), F32),
                   jax.ShapeDtypeStruct((B, T, MLA_ROPE), F32)],
        compiler_params=_cparams(1),
        name="shared_kv",
    )(x, kv_in_g.reshape(1, D), w_kv, kv_lat_g.reshape(1, KV_LORA), cos32, sin32)


def _kv_expand_kernel(lat_ref, kr_ref, wk_ref, ek_ref, wv_ref, k_ref, v_ref):
    bb, tt, _ = lat_ref.shape
    lat = lat_ref[...].reshape(bb * tt, KV_LORA).astype(BF16)
    kr = kr_ref[...].reshape(bb * tt, MLA_ROPE).astype(BF16)
    k = _dot(lat, wk_ref[...].astype(BF16)) + _dot(kr, ek_ref[...].astype(BF16))
    k_ref[...] = k.reshape(bb, tt, k.shape[-1]).astype(k_ref.dtype)
    v = _dot(lat, wv_ref[...].astype(BF16))
    v_ref[...] = v.reshape(bb, tt, v.shape[-1]).astype(v_ref.dtype)


def kv_expand(lat, kr, wk_pad, ek, wv, rows=512):
    B, T, _ = lat.shape
    bb, tt, nblk, ij = _row_blocks(B, T, rows)
    NK, NV = wk_pad.shape[1], wv.shape[1]

    def full(a):
        return pl.BlockSpec(a.shape, lambda i: (0, 0))

    def rowspec(n):
        return pl.BlockSpec((bb, tt, n), lambda i: ij(i) + (0,))

    return pl.pallas_call(
        _kv_expand_kernel,
        grid=(nblk,),
        in_specs=[rowspec(KV_LORA), rowspec(MLA_ROPE), full(wk_pad), full(ek), full(wv)],
        out_specs=[rowspec(NK), rowspec(NV)],
        out_shape=[jax.ShapeDtypeStruct((B, T, NK), BF16), jax.ShapeDtypeStruct((B, T, NV), BF16)],
        compiler_params=_cparams(1),
        name="kv_expand",
    )(lat, kr, wk_pad, ek, wv)


def _query_kernel(h_ref, wdq_ref, qg_ref, wq_ref, wqr_ref, c_ref, s_ref, q_ref, wdq_b, wq_b, wqr_b):
    @pl.when(pl.program_id(0) == 0)
    def _():
        wdq_b[...] = wdq_ref[...].astype(BF16)
        wq_b[...] = wq_ref[...].astype(BF16)
        wqr_b[...] = wqr_ref[...].astype(BF16)

    bb, tt, D = h_ref.shape
    h = h_ref[...].reshape(bb * tt, D)
    cq = _rms(_dot(h, wdq_b[...]), qg_ref[...]).astype(BF16)
    q1 = _dot(cq, wq_b[...]).reshape(bb, tt, -1)
    q2 = _dot(cq, wqr_b[...]).reshape(bb, tt, -1)
    c = c_ref[...]
    s = s_ref[...]
    for hd in range(MLA_HEADS):
        sl = slice(hd * HEAD_PAD, (hd + 1) * HEAD_PAD)
        q_ref[:, :, sl] = (q1[:, :, sl] * c + q2[:, :, sl] * s).astype(q_ref.dtype)


def mla_queries(h, w_dq, q_norm_g, wq_pad, wq_rot, l, c128, s128, rows=512):
    B, T, D = h.shape
    bb, tt, nblk, ij = _row_blocks(B, T, rows)
    NQ = wq_pad.shape[-1]
    tspec = pl.BlockSpec((tt, HEAD_PAD), lambda i: (ij(i)[1], 0))
    return pl.pallas_call(
        _query_kernel,
        grid=(nblk,),
        in_specs=[pl.BlockSpec((bb, tt, D), lambda i: ij(i) + (0,)),
                  pl.BlockSpec((None, D, Q_LORA), lambda i: (l, 0, 0)),
                  pl.BlockSpec((None, 1, Q_LORA), lambda i: (l, 0, 0)),
                  pl.BlockSpec((None, Q_LORA, NQ), lambda i: (l, 0, 0)),
                  pl.BlockSpec((None, Q_LORA, NQ), lambda i: (l, 0, 0)),
                  tspec, tspec],
        out_specs=pl.BlockSpec((bb, tt, NQ), lambda i: ij(i) + (0,)),
        out_shape=jax.ShapeDtypeStruct((B, T, NQ), BF16),
        scratch_shapes=[pltpu.VMEM((D, Q_LORA), BF16), pltpu.VMEM((Q_LORA, NQ), BF16),
                        pltpu.VMEM((Q_LORA, NQ), BF16)],
        compiler_params=_cparams(1),
        name="mla_queries",
    )(h, w_dq, q_norm_g.reshape(-1, 1, Q_LORA), wq_pad, wq_rot, c128, s128)


def _attn_prompt_kernel(qi_tab, ki_tab, q_ref, k_ref, v_ref, o_ref, m_ref, l_ref, acc_ref, *, tq, tk):
    p_id = pl.program_id(1)
    qi = qi_tab[p_id]
    ki = ki_tab[p_id]

    @pl.when(ki == 0)
    def _():
        m_ref[...] = jnp.full_like(m_ref, NEG_INF)
        l_ref[...] = jnp.zeros_like(l_ref)
        acc_ref[...] = jnp.zeros_like(acc_ref)

    qchunk = (qi * tq + lax.broadcasted_iota(jnp.int32, (tq, tk), 0)) // CHUNK
    kchunk = (ki * tk + lax.broadcasted_iota(jnp.int32, (tq, tk), 1)) // CHUNK
    mask = kchunk <= qchunk
    left = lax.broadcasted_iota(jnp.int32, (tq, HEAD_PAD), 1) < MLA_V

    for pair in range(MLA_HEADS // 2):
        vp = v_ref[0, :, pair * 128:(pair + 1) * 128]
        pv, alpha = [], []
        for hd in (2 * pair, 2 * pair + 1):
            sl = slice(hd * HEAD_PAD, (hd + 1) * HEAD_PAD)
            s = jnp.where(mask, _dot_nt(q_ref[0, :, sl], k_ref[0, :, sl]), NEG_INF)
            m_prev = m_ref[hd]
            m_new = jnp.maximum(m_prev, jnp.max(s, axis=-1, keepdims=True))
            a = jnp.exp(m_prev - m_new)
            p = jnp.exp(s - m_new[:, :1])
            l_ref[hd] = a * l_ref[hd] + jnp.sum(p, axis=-1, keepdims=True)
            m_ref[hd] = m_new
            pv.append(_dot(p.astype(BF16), vp))
            alpha.append(a)
        psl = slice(pair * 128, (pair + 1) * 128)
        acc_ref[:, psl] = jnp.where(left, alpha[0], alpha[1]) * acc_ref[:, psl] + jnp.where(left, pv[0], pv[1])

    @pl.when(ki == qi)
    def _():
        for pair in range(MLA_HEADS // 2):
            psl = slice(pair * 128, (pair + 1) * 128)
            lsum = jnp.where(left, l_ref[2 * pair], l_ref[2 * pair + 1])
            o_ref[0, :, psl] = (acc_ref[:, psl] / lsum).astype(o_ref.dtype)


def attn_prompt(q, k, v, tq=256):
    B, T, NQ = q.shape
    NV = v.shape[-1]
    tk = tq
    assert tq % CHUNK == 0
    nq = T // tq
    pairs = [(a, b) for a in range(nq) for b in range(a + 1)]
    qi_tab = jnp.asarray([a for a, _ in pairs], jnp.int32)
    ki_tab = jnp.asarray([b for _, b in pairs], jnp.int32)
    grid_spec = pltpu.PrefetchScalarGridSpec(
        num_scalar_prefetch=2,
        grid=(B, len(pairs)),
        in_specs=[pl.BlockSpec((1, tq, NQ), lambda b, p, qt, kt: (b, qt[p], 0)),
                  pl.BlockSpec((1, tk, NQ), lambda b, p, qt, kt: (b, kt[p], 0)),
                  pl.BlockSpec((1, tk, NV), lambda b, p, qt, kt: (b, kt[p], 0))],
        out_specs=pl.BlockSpec((1, tq, NV), lambda b, p, qt, kt: (b, qt[p], 0)),
        scratch_shapes=[pltpu.VMEM((MLA_HEADS, tq, HEAD_PAD), F32),
                        pltpu.VMEM((MLA_HEADS, tq, HEAD_PAD), F32),
                        pltpu.VMEM((tq, NV), F32)],
    )
    return pl.pallas_call(
        functools.partial(_attn_prompt_kernel, tq=tq, tk=tk),
        grid_spec=grid_spec,
        out_shape=jax.ShapeDtypeStruct((B, T, NV), BF16),
        compiler_params=_cparams(2),
        name="attn_prompt",
    )(qi_tab, ki_tab, q, k, v)


def _absorb_kernel(q_ref, m_ref, o_ref):
    o_ref[...] = _dot(q_ref[...], m_ref[...].astype(BF16)).astype(o_ref.dtype)


def absorb_queries(q2d, m_abs):
    N = q2d.shape[0]
    H, _, W = m_abs.shape
    return pl.pallas_call(
        _absorb_kernel,
        grid=(H,),
        in_specs=[pl.BlockSpec((N, HEAD_PAD), lambda h: (0, h)),
                  pl.BlockSpec((None, HEAD_PAD, W), lambda h: (h, 0, 0))],
        out_specs=pl.BlockSpec((None, N, W), lambda h: (h, 0, 0)),
        out_shape=jax.ShapeDtypeStruct((H, N, W), BF16),
        compiler_params=_cparams(1),
        name="absorb_queries",
    )(q2d, m_abs)


def _attn_sample_kernel(q_ref, lat_ref, kr_ref, nlat_ref, nkr_ref, o_ref, m_ref, l_ref, acc_ref):
    kb = pl.program_id(1)
    H, Q, W = q_ref.shape
    q = q_ref[...].reshape(H * Q, W)
    q_lat = q[:, :KV_LORA]
    q_rope = q[:, KV_LORA:KV_LORA + MLA_ROPE]

    def update(lat, kr):
        lat = lat.astype(BF16)
        s = _dot_nt(q_lat, lat) + _dot_nt(q_rope, kr.astype(BF16))
        m_prev = m_ref[...]
        m_new = jnp.maximum(m_prev, jnp.max(s, axis=-1, keepdims=True))
        a = jnp.exp(m_prev - m_new)
        p = jnp.exp(s - m_new[:, :1])
        l_ref[...] = a * l_ref[...] + jnp.sum(p, axis=-1, keepdims=True)
        m_ref[...] = m_new
        acc_ref[...] = jnp.concatenate([a, a], axis=-1) * acc_ref[...] + _dot(p.astype(BF16), lat)

    @pl.when(kb == 0)
    def _():
        m_ref[...] = jnp.full_like(m_ref, NEG_INF)
        l_ref[...] = jnp.zeros_like(l_ref)
        acc_ref[...] = jnp.zeros_like(acc_ref)
        update(nlat_ref[0], nkr_ref[0])

    update(lat_ref[0], kr_ref[0])

    @pl.when(kb == pl.num_programs(1) - 1)
    def _():
        lsum = l_ref[...]
        o = acc_ref[...] / jnp.concatenate([lsum, lsum], axis=-1)
        o_ref[...] = o.reshape(H, Q, KV_LORA).astype(o_ref.dtype)


def attn_sample(q_abs, cache_lat, cache_kr, new_lat, new_kr, tk=1024):
    H, N, W = q_abs.shape
    B, P, _ = cache_lat.shape
    Q = new_lat.shape[1]
    qpos = P + np.arange(Q)
    kpos = np.arange(P + Q)
    assert bool(np.all((kpos // CHUNK)[None, :] <= (qpos // CHUNK)[:, None]))
    return pl.pallas_call(
        _attn_sample_kernel,
        grid=(B, P // tk),
        in_specs=[pl.BlockSpec((H, Q, W), lambda b, kb: (0, b, 0)),
                  pl.BlockSpec((1, tk, KV_LORA), lambda b, kb: (b, kb, 0)),
                  pl.BlockSpec((1, tk, MLA_ROPE), lambda b, kb: (b, kb, 0)),
                  pl.BlockSpec((1, Q, KV_LORA), lambda b, kb: (b, 0, 0)),
                  pl.BlockSpec((1, Q, MLA_ROPE), lambda b, kb: (b, 0, 0))],
        out_specs=pl.BlockSpec((H, Q, KV_LORA), lambda b, kb: (0, b, 0)),
        out_shape=jax.ShapeDtypeStruct((H, N, KV_LORA), BF16),
        scratch_shapes=[pltpu.VMEM((H * Q, 128), F32), pltpu.VMEM((H * Q, 128), F32),
                        pltpu.VMEM((H * Q, KV_LORA), F32)],
        compiler_params=_cparams(2),
        name="attn_sample",
    )(q_abs, cache_lat, cache_kr, new_lat, new_kr)


def _unabsorb_kernel(o_ref, w_ref, out_ref):
    out_ref[...] = (_dot(o_ref[0], w_ref[0].astype(BF16))
                    + _dot(o_ref[1], w_ref[1].astype(BF16))).astype(out_ref.dtype)


def unabsorb(o_lat, wuv_pad):
    H, N, R = o_lat.shape
    return pl.pallas_call(
        _unabsorb_kernel,
        grid=(H // 2,),
        in_specs=[pl.BlockSpec((2, N, R), lambda p: (p, 0, 0)),
                  pl.BlockSpec((2, R, 128), lambda p: (p, 0, 0))],
        out_specs=pl.BlockSpec((N, 128), lambda p: (0, p)),
        out_shape=jax.ShapeDtypeStruct((N, (H // 2) * 128), BF16),
        compiler_params=_cparams(1),
        name="unabsorb",
    )(o_lat, wuv_pad)


def _rope_tables(pos):
    half = MLA_ROPE // 2
    inv = 1.0 / (ROPE_THETA ** (np.arange(half, dtype=np.float64) * 2.0 / MLA_ROPE))
    ang = np.asarray(pos, np.float64)[:, None] * inv[None, :]
    cos = np.concatenate([np.cos(ang), np.cos(ang)], axis=-1)
    sin = np.concatenate([np.sin(ang), np.sin(ang)], axis=-1)
    T = cos.shape[0]
    c128 = np.zeros((T, HEAD_PAD)); s128 = np.zeros((T, HEAD_PAD))
    c128[:, :MLA_NOPE] = 1.0
    c128[:, MLA_NOPE:MLA_NOPE + MLA_ROPE] = cos
    s128[:, MLA_NOPE:MLA_NOPE + MLA_ROPE] = sin
    return (jnp.asarray(cos, F32), jnp.asarray(sin, F32),
            jnp.asarray(c128 * QK_SCALE, F32), jnp.asarray(s128 * QK_SCALE, F32))


def _rot_half_cols(w):
    half = w.shape[-1] // 2
    return jnp.concatenate([-w[..., half:], w[..., :half]], axis=-1)


def _prep_weights(w_dkv, w_uk, w_uv, w_uq, router_w):
    D = D_MODEL
    w_lat, w_rope = w_dkv[:, :KV_LORA], w_dkv[:, KV_LORA:]
    pad96 = jnp.zeros((D, 128 - MLA_ROPE), F32)
    w_kv = jnp.concatenate([w_lat, w_rope, pad96, _rot_half_cols(w_rope), pad96], axis=-1)

    zpad = HEAD_PAD - MLA_NOPE
    wk_pad = jnp.pad(w_uk, ((0, 0), (0, 0), (0, zpad))).reshape(KV_LORA, MLA_HEADS * HEAD_PAD)
    ek = jnp.zeros((MLA_ROPE, MLA_HEADS, HEAD_PAD), F32)
    ek = ek.at[:, :, MLA_NOPE:MLA_NOPE + MLA_ROPE].set(
        jnp.broadcast_to(jnp.eye(MLA_ROPE, dtype=F32)[:, None, :], (MLA_ROPE, MLA_HEADS, MLA_ROPE)))
    ek = ek.reshape(MLA_ROPE, MLA_HEADS * HEAD_PAD)
    wv = w_uv.reshape(KV_LORA, MLA_HEADS * MLA_V)

    nb = w_uq.shape[0]
    qn, qr = w_uq[..., :MLA_NOPE], w_uq[..., MLA_NOPE:]
    z32 = jnp.zeros(qr.shape[:-1] + (HEAD_PAD - MLA_NOPE - MLA_ROPE,), F32)
    wq_pad = jnp.concatenate([qn, qr, z32], axis=-1).reshape(nb, Q_LORA, MLA_HEADS * HEAD_PAD)
    wq_rot = jnp.concatenate([jnp.zeros_like(qn), _rot_half_cols(qr), z32], axis=-1)
    wq_rot = wq_rot.reshape(nb, Q_LORA, MLA_HEADS * HEAD_PAD)

    m_abs = jnp.zeros((MLA_HEADS, HEAD_PAD, KV_LORA + 128), F32)
    m_abs = m_abs.at[:, :MLA_NOPE, :KV_LORA].set(jnp.transpose(w_uk, (1, 2, 0)))
    m_abs = m_abs.at[:, MLA_NOPE:MLA_NOPE + MLA_ROPE, KV_LORA:KV_LORA + MLA_ROPE].set(
        jnp.broadcast_to(jnp.eye(MLA_ROPE, dtype=F32), (MLA_HEADS, MLA_ROPE, MLA_ROPE)))

    wuv_h = jnp.transpose(w_uv, (1, 0, 2))
    even = jnp.pad(wuv_h, ((0, 0), (0, 0), (0, 64)))
    odd = jnp.pad(wuv_h, ((0, 0), (0, 0), (64, 0)))
    wuv_pad = jnp.where((jnp.arange(MLA_HEADS) % 2 == 0)[:, None, None], even, odd)

    rw_t = jnp.transpose(router_w, (0, 2, 1))
    return dict(w_kv=w_kv, wk_pad=wk_pad, ek=ek, wv=wv, wq_pad=wq_pad, wq_rot=wq_rot,
                m_abs=m_abs, wuv_pad=wuv_pad, rw_t=rw_t)


def _mixer(st, l, P, W):
    x, m = st["x"], st["mod"][l]
    B, T, _ = x.shape
    n_a = P["hg_w_in"].shape[0]
    h = norm_mod(x, P["norm1_g"][l], m, sc_idx=1, sh_idx=0)
    if l < n_a:
        z = linear(h, P["hg_w_in"], l, F32)
        s0 = None if st["hg_state"] is None else st["hg_state"][l]
        o, s_new = gla(z, st["lbs"][l], P["hg_onorm_g"][l], s0)
        st["hg_new"].append(s_new)
        st["x"] = linear(o, P["hg_w_out"], l, F32, x=x, mod=m, gate_idx=2)
    else:
        bi = l - n_a
        q = mla_queries(h, P["w_dq"], P["q_norm_g"], W["wq_pad"], W["wq_rot"], bi, st["c128"], st["s128"])
        if st["past_lat"] is None:
            o = attn_prompt(q, st["k_all"], st["v_all"])
        else:
            q_abs = absorb_queries(q.reshape(B * T, -1), W["m_abs"])
            o_lat = attn_sample(q_abs, st["past_lat"], st["past_kr"], st["lat"], st["kr"])
            o = unabsorb(o_lat, W["wuv_pad"]).reshape(B, T, -1)
        st["x"] = linear(o, P["w_o"], bi, F32, x=x, mod=m, gate_idx=2)


def _moe(groups, l, P, W):
    hp = jnp.concatenate([norm_mod(st["x"], P["norm2_g"][l], st["mod"][l], sc_idx=4, sh_idx=3, packed=True)
                          for st in groups], axis=0)
    n_tok = hp.shape[0]
    n_tiles = (TOP_K * n_tok) // MOE_TILE + N_EXPERTS
    pos, w8, tile_expert, n_used = route(hp, W["rw_t"], P["router_bias"], l, MOE_TILE, n_tiles)
    pos_flat = pos.reshape(-1)
    src = sc_invert(pos_flat, n_tok, n_tiles * MOE_TILE)
    xs = sc_gather(hp, src)
    out = moe_gemm(xs, tile_expert.reshape(-1), n_used[0, :1], P["exp_w_in"], P["exp_w_out"], l,
                   MOE_TILE, n_tiles)
    y8 = sc_gather(out, pos_flat).reshape(TOP_K, n_tok, -1)
    w_t = w8.T
    row0 = 0
    for st in groups:
        B, T, _ = st["x"].shape
        st["x"] = moe_combine(y8, w_t, hp, P["sh_w_in"], P["sh_w_out"], st["x"], st["mod"][l], 5, l, row0)
        row0 += B * T


def _group_state(x, mod, pos, hg_state, past_lat, past_kr, lbs):
    cos32, sin32, c128, s128 = _rope_tables(pos)
    return dict(x=x, mod=mod, hg_state=hg_state, past_lat=past_lat, past_kr=past_kr, lbs=lbs,
                cos32=cos32, sin32=sin32, c128=c128, s128=s128, hg_new=[],
                lat=None, kr=None, k_all=None, v_all=None)


def kernel(x_prompt, x_sample, state_hgrn, cache_mla_latent, cache_mla_krope, c_prompt, c_sample, ada_w, ada_b, norm1_g, norm2_g, hg_w_in, hg_lb_logits, hg_onorm_g, hg_w_out, kv_in_g, w_dkv, kv_lat_g, w_uk, w_uv, w_dq, q_norm_g, w_uq, w_o, router_w, router_bias, exp_w_in, exp_w_out, sh_w_in, sh_w_out, final_g):
    Bp, Sp, _ = x_prompt.shape
    Bs, Ss, _ = x_sample.shape
    past = cache_mla_latent.shape[1]
    P = dict(norm1_g=norm1_g, norm2_g=norm2_g, hg_w_in=hg_w_in, hg_lb_logits=hg_lb_logits,
             hg_onorm_g=hg_onorm_g, hg_w_out=hg_w_out, kv_in_g=kv_in_g, kv_lat_g=kv_lat_g,
             w_dq=w_dq, q_norm_g=q_norm_g, w_o=w_o, router_bias=router_bias,
             exp_w_in=exp_w_in, exp_w_out=exp_w_out, sh_w_in=sh_w_in, sh_w_out=sh_w_out, final_g=final_g)
    W = _prep_weights(w_dkv, w_uk, w_uv, w_uq, router_w)
    mod = ada_mod(jnp.concatenate([c_prompt, c_sample], axis=0), ada_w, ada_b)
    lbs = jnp.cumsum(jax.nn.softmax(hg_lb_logits.astype(F32), axis=0), axis=0)
    gp = _group_state(x_prompt, mod[:, :Bp, None, :], np.arange(Sp), None, None, None, lbs)
    gs = _group_state(x_sample, mod[:, Bp:, None, :], past + np.arange(Ss), state_hgrn,
                      cache_mla_latent, cache_mla_krope, lbs)
    groups = [gp, gs]
    n_a = hg_w_in.shape[0]
    for l in range(norm1_g.shape[0]):
        for st in groups:
            _mixer(st, l, P, W)
        _moe(groups, l, P, W)
        if l == n_a - 1:
            for st in groups:
                st["lat"], st["kr"] = shared_kv(st["x"], kv_in_g, W["w_kv"], kv_lat_g, st["cos32"], st["sin32"])
            gp["k_all"], gp["v_all"] = kv_expand(gp["lat"], gp["kr"], W["wk_pad"], W["ek"], W["wv"])
    outs = []
    for st in groups:
        outs.append((norm_mod(st["x"], final_g, out_dtype=F32), jnp.stack(st["hg_new"], axis=0)))
    (y_p, st_p), (y_s, st_s) = outs
    return (y_p, y_s, st_p, st_s, gp["lat"], gp["kr"], gs["lat"], gs["kr"])
```

```python
import dataclasses
import functools

import numpy as np
import jax
import jax.numpy as jnp
from jax import lax
from jax.experimental import pallas as pl
from jax.experimental.pallas import tpu as pltpu
from jax.experimental.pallas import tpu_sc as plsc

F32 = jnp.float32
BF16 = jnp.bfloat16

D_MODEL = 1024
CHUNK = 64
HG_HEADS = 8
HG_DK = 128
HG_DV = 128
MLA_HEADS = 16
MLA_NOPE = 64
MLA_ROPE = 32
MLA_V = 64
Q_LORA = 384
KV_LORA = 256
ROPE_THETA = 10000.0
N_EXPERTS = 64
TOP_K = 8
N_GROUPS = 8
TOPK_GROUPS = 4
EXPERT_FF = 256
SHARED_FF = 256
ROUTED_SCALE = 2.5
EPS = 1e-6

HEAD_PAD = 128
QK_SCALE = (MLA_NOPE + MLA_ROPE) ** -0.5
VMEM_LIMIT = 56 * 1024 * 1024
NEG_INF = float("-inf")
SC_CORES = 2
SC_SUBCORES = 16
SC_WORKERS = SC_CORES * SC_SUBCORES
SC_LANES = 16
SC_WINDOW = 64
MOE_TILE = 512


def _cparams(n_axes):
    return pltpu.CompilerParams(dimension_semantics=("arbitrary",) * n_axes,
                                vmem_limit_bytes=VMEM_LIMIT)


def _silu(x):
    return x * jax.nn.sigmoid(x)


def _rms(x, g):
    ms = jnp.mean(x * x, axis=-1, keepdims=True)
    return x * lax.rsqrt(ms + EPS) * g


def _dot(a, b):
    return jnp.dot(a, b, preferred_element_type=F32)


def _dot_nt(a, b):
    return lax.dot_general(a, b, (((1,), (1,)), ((), ())), preferred_element_type=F32)


def _dot_tn(a, b):
    return lax.dot_general(a, b, (((0,), (0,)), ((), ())), preferred_element_type=F32)


def _row_blocks(B, T, rows):
    if T >= rows:
        assert T % rows == 0
        bb, tt = 1, rows
    else:
        assert rows % T == 0 and B % (rows // T) == 0
        bb, tt = rows // T, T
    nt = T // tt
    return bb, tt, (B // bb) * nt, (lambda i: (i // nt, i % nt))


def _ada_kernel(c_ref, w_ref, b_ref, o_ref):
    a = _silu(c_ref[...]).astype(BF16)
    o_ref[...] = _dot(a, w_ref[...].astype(BF16)) + b_ref[...]


def ada_mod(c, ada_w, ada_b):
    R, D = c.shape
    L, _, N = ada_w.shape
    tn = 1536
    return pl.pallas_call(
        _ada_kernel,
        grid=(L, N // tn),
        in_specs=[pl.BlockSpec((R, D), lambda l, j: (0, 0)),
                  pl.BlockSpec((None, D, tn), lambda l, j: (l, 0, j)),
                  pl.BlockSpec((None, 1, tn), lambda l, j: (l, 0, j))],
        out_specs=pl.BlockSpec((None, R, tn), lambda l, j: (l, 0, j)),
        out_shape=jax.ShapeDtypeStruct((L, R, N), F32),
        compiler_params=_cparams(2),
        name="ada_mod",
    )(c, ada_w, ada_b.reshape(L, 1, N))


def _pack_pairs(y):
    half = y.shape[-1] // 2
    bits = lax.bitcast_convert_type(y.astype(BF16).astype(F32), jnp.uint32)
    word = lax.shift_right_logical(bits[:, :half], jnp.uint32(16)) | bits[:, half:]
    return lax.bitcast_convert_type(word, jnp.int32)


def _unpack_pairs(word, dtype=BF16):
    u = lax.bitcast_convert_type(word, jnp.uint32)
    lo = lax.bitcast_convert_type(lax.shift_left(u, jnp.uint32(16)), F32)
    hi = lax.bitcast_convert_type(u & jnp.uint32(0xFFFF0000), F32)
    return lo.astype(dtype), hi.astype(dtype)


def _norm_kernel(*refs, modulated, packed):
    if modulated:
        x_ref, g_ref, sc_ref, sh_ref, o_ref = refs
    else:
        x_ref, g_ref, o_ref = refs
    y = _rms(x_ref[...], g_ref[...])
    if modulated:
        y = y * (1.0 + sc_ref[...]) + sh_ref[...]
    if packed:
        bb, tt, D = y.shape
        o_ref[...] = _pack_pairs(y.reshape(bb * tt, D))
    else:
        o_ref[...] = y.astype(o_ref.dtype)


def norm_mod(x, g, mod=None, sc_idx=0, sh_idx=0, out_dtype=BF16, rows=512, packed=False):
    B, T, D = x.shape
    bb, tt, nblk, ij = _row_blocks(B, T, rows)
    xspec = pl.BlockSpec((bb, tt, D), lambda i: ij(i) + (0,))
    in_specs = [xspec, pl.BlockSpec((1, D), lambda i: (0, 0))]
    args = [x, g.reshape(1, D)]
    if mod is not None:
        in_specs += [pl.BlockSpec((bb, 1, D), lambda i: (ij(i)[0], 0, sc_idx)),
                     pl.BlockSpec((bb, 1, D), lambda i: (ij(i)[0], 0, sh_idx))]
        args += [mod, mod]
    if packed:
        out_specs = pl.BlockSpec((bb * tt, D // 2), lambda i: (i, 0))
        out_shape = jax.ShapeDtypeStruct((B * T, D // 2), jnp.int32)
    else:
        out_specs = xspec
        out_shape = jax.ShapeDtypeStruct((B, T, D), out_dtype)
    return pl.pallas_call(
        functools.partial(_norm_kernel, modulated=mod is not None, packed=packed),
        grid=(nblk,),
        in_specs=in_specs,
        out_specs=out_specs,
        out_shape=out_shape,
        compiler_params=_cparams(1),
        name="norm_mod",
    )(*args)


def _linear_kernel(*refs, residual):
    if residual:
        a_ref, w_ref, x_ref, gate_ref, o_ref, wb_ref = refs
    else:
        a_ref, w_ref, o_ref, wb_ref = refs

    @pl.when(pl.program_id(1) == 0)
    def _():
        wb_ref[...] = w_ref[...].astype(BF16)

    bb, tt, K = a_ref.shape
    y = _dot(a_ref[...].reshape(bb * tt, K).astype(BF16), wb_ref[...])
    y = y.reshape(bb, tt, y.shape[-1])
    if residual:
        y = x_ref[...] + gate_ref[...] * y
    o_ref[...] = y.astype(o_ref.dtype)


def linear(a, w, l, out_dtype, x=None, mod=None, gate_idx=0, rows=512, tn=1024):
    B, T, K = a.shape
    _, _, N = w.shape
    tn = min(tn, N)
    bb, tt, nblk, ij = _row_blocks(B, T, rows)
    in_specs = [pl.BlockSpec((bb, tt, K), lambda j, i: ij(i) + (0,)),
                pl.BlockSpec((None, K, tn), lambda j, i: (l, 0, j))]
    args = [a, w]
    ospec = pl.BlockSpec((bb, tt, tn), lambda j, i: ij(i) + (j,))
    if x is not None:
        gsteps = D_MODEL // tn
        in_specs += [ospec, pl.BlockSpec((bb, 1, tn), lambda j, i: (ij(i)[0], 0, gate_idx * gsteps + j))]
        args += [x, mod]
    return pl.pallas_call(
        functools.partial(_linear_kernel, residual=x is not None),
        grid=(N // tn, nblk),
        in_specs=in_specs,
        out_specs=ospec,
        out_shape=jax.ShapeDtypeStruct((B, T, N), out_dtype),
        scratch_shapes=[pltpu.VMEM((K, tn), BF16)],
        compiler_params=_cparams(2),
        name="linear",
    )(*args)


def _gla_kernel(*refs, L, n_chunks, has_init):
    if has_init:
        q_ref, f_ref, i_ref, g_ref, lb_ref, on_ref, s0_ref, o_ref, so_ref, st_ref = refs
    else:
        q_ref, f_ref, i_ref, g_ref, lb_ref, on_ref, o_ref, so_ref, st_ref = refs
    t = pl.program_id(2)

    @pl.when(t == 0)
    def _():
        if has_init:
            st_ref[...] = s0_ref[0, 0].T
        else:
            st_ref[...] = jnp.zeros_like(st_ref)

    lb = lb_ref[...]
    row = lax.broadcasted_iota(jnp.int32, (L, L), 0)
    col = lax.broadcasted_iota(jnp.int32, (L, L), 1)
    causal = col <= row
    tri = causal.astype(F32)
    mid = L // 2 - 1

    def chunk(c, carry):
        r0 = pl.multiple_of(c * L, L)
        q = _silu(q_ref[0, pl.ds(r0, L), :])
        fg = lb + (1.0 - lb) * jax.nn.sigmoid(f_ref[0, pl.ds(r0, L), :])
        k = 1.0 - fg
        v = i_ref[0, pl.ds(r0, L), :].astype(BF16)
        b = jnp.dot(tri, jnp.log(fg), preferred_element_type=F32, precision=lax.Precision.HIGHEST)
        b_mid = b[mid:mid + 1, :]
        b_last = b[L - 1:L, :]
        st = st_ref[...]
        o = _dot_nt((q * jnp.exp(b)).astype(BF16), st.astype(BF16))
        qa = (q * jnp.exp(b - b_mid)).astype(BF16)
        kb = (k * jnp.exp(b_mid - b)).astype(BF16)
        scores = jnp.where(causal, _dot_nt(qa, kb), 0.0)
        o = o + _dot(scores.astype(BF16), v)
        kd = (k * jnp.exp(b_last - b)).astype(BF16)
        st_ref[...] = st * jnp.exp(b_last) + _dot_tn(v, kd)
        o = _rms(o, on_ref[...]) * _silu(g_ref[0, pl.ds(r0, L), :])
        o_ref[0, pl.ds(r0, L), :] = o.astype(o_ref.dtype)
        return carry

    lax.fori_loop(0, n_chunks, chunk, 0)

    @pl.when(t == pl.num_programs(2) - 1)
    def _():
        so_ref[0, 0] = st_ref[...].T


def gla(z, lb, onorm_g, s0):
    B, T, _ = z.shape
    L = CHUNK if T % CHUNK == 0 else T
    tt = min(T, 512)
    n_chunks = tt // L
    H = HG_HEADS

    def zspec(part):
        return pl.BlockSpec((1, tt, HG_DK), lambda b, h, t: (b, t, part * H + h))

    hspec = pl.BlockSpec((1, HG_DK), lambda b, h, t: (0, h))
    sspec = pl.BlockSpec((1, 1, HG_DK, HG_DV), lambda b, h, t: (b, h, 0, 0))
    in_specs = [zspec(0), zspec(1), zspec(2), zspec(3), hspec, hspec]
    args = [z, z, z, z, lb.reshape(1, D_MODEL), onorm_g.reshape(1, D_MODEL)]
    if s0 is not None:
        in_specs.append(sspec)
        args.append(s0)
    return pl.pallas_call(
        functools.partial(_gla_kernel, L=L, n_chunks=n_chunks, has_init=s0 is not None),
        grid=(B, H, T // tt),
        in_specs=in_specs,
        out_specs=[pl.BlockSpec((1, tt, HG_DV), lambda b, h, t: (b, t, h)), sspec],
        out_shape=[jax.ShapeDtypeStruct((B, T, D_MODEL), BF16),
                   jax.ShapeDtypeStruct((B, H, HG_DK, HG_DV), F32)],
        scratch_shapes=[pltpu.VMEM((HG_DV, HG_DK), F32)],
        compiler_params=_cparams(3),
        name="gla",
    )(*args)


def _route_kernel(h_ref, rw_ref, bias_ref, pos_ref, w_ref, te_ref, nu_ref,
                  e_s, r_s, base_s, start_s, *, tile_rows):
    ph = pl.program_id(0)
    i = pl.program_id(1)
    M = h_ref.shape[0]
    half = h_ref.shape[1]
    G, E = N_GROUPS, N_EXPERTS // N_GROUPS
    e_flat = lax.broadcasted_iota(jnp.int32, (N_EXPERTS, M), 0)

    @pl.when(ph == 1)
    def _():
        @pl.when(i == 0)
        def _():
            cnt = base_s[...]
            padded = jnp.floor((cnt + (tile_rows - 1)) * (1.0 / tile_rows)) * tile_rows
            r = lax.broadcasted_iota(jnp.int32, (N_EXPERTS, N_EXPERTS), 0)
            c = lax.broadcasted_iota(jnp.int32, (N_EXPERTS, N_EXPERTS), 1)
            start = jnp.dot((c < r).astype(F32), padded, preferred_element_type=F32,
                            precision=lax.Precision.HIGHEST)
            start_s[...] = start
            seg_end = (start + padded)[:, :1]
            nt = te_ref.shape[1]
            tile_lo = (lax.broadcasted_iota(jnp.int32, (N_EXPERTS, nt), 1) * tile_rows).astype(F32)
            owner = jnp.sum((seg_end <= tile_lo).astype(F32), axis=0, keepdims=True)
            te_ref[...] = jnp.minimum(owner, N_EXPERTS - 1.0).astype(jnp.int32)
            total = jnp.max(start + padded, axis=0, keepdims=True)
            nu_ref[...] = (total * (1.0 / tile_rows)).astype(jnp.int32)

        start_col = start_s[:, :1]
        for k in range(TOP_K):
            hit = e_flat == e_s[i, k:k + 1, :]
            seg = jnp.sum(jnp.where(hit, start_col, 0.0), axis=0, keepdims=True)
            pos_ref[k:k + 1, :] = (seg + r_s[i, k:k + 1, :]).astype(jnp.int32)

    @pl.when(ph == 0)
    def _():
        _route_pass0(h_ref, rw_ref, bias_ref, w_ref, e_s, r_s, base_s, i, M, half, G, E)


def _route_pass0(h_ref, rw_ref, bias_ref, w_ref, e_s, r_s, base_s, i, M, half, G, E):
    @pl.when(i == 0)
    def _():
        base_s[...] = jnp.zeros_like(base_s)

    lo, hi = _unpack_pairs(h_ref[...])
    rw = rw_ref[...].astype(BF16)
    logits = _dot_nt(rw[:, :half], lo) + _dot_nt(rw[:, half:], hi)
    s = jax.nn.sigmoid(logits)
    sb = (s + bias_ref[...]).reshape(G, E, M)
    s = s.reshape(G, E, M)
    e_in = lax.broadcasted_iota(jnp.int32, (G, E, M), 1)
    g_id = lax.broadcasted_iota(jnp.int32, (G, 1, M), 0)
    e_id = lax.broadcasted_iota(jnp.int32, (G, E, M), 0) * E + e_in

    m1 = jnp.max(sb, axis=1, keepdims=True)
    first = jnp.min(jnp.where(sb == m1, e_in, E), axis=1, keepdims=True)
    m2 = jnp.max(jnp.where(e_in == first, NEG_INF, sb), axis=1, keepdims=True)
    gs = m1 + m2

    rank = jnp.zeros((G, 1, M), jnp.int32)
    for j in range(G):
        gj = gs[j:j + 1]
        beats = (gj > gs) | ((gj == gs) & (j < g_id))
        rank = rank + beats.astype(jnp.int32)
    gsel = rank < TOPK_GROUPS

    vals = jnp.where(gsel, sb, NEG_INF)
    w = jnp.zeros((G, E, M), F32)
    selm = jnp.zeros((G, E, M), F32)
    chosen = []
    for _ in range(TOP_K):
        m = jnp.max(jnp.max(vals, axis=1, keepdims=True), axis=0, keepdims=True)
        cand = jnp.where(vals == m, e_id, N_EXPERTS)
        first = jnp.min(jnp.min(cand, axis=1, keepdims=True), axis=0, keepdims=True)
        hit = e_id == first
        w = jnp.where(hit, s, w)
        selm = jnp.where(hit, 1.0, selm)
        vals = jnp.where(hit, NEG_INF, vals)
        chosen.append(first.reshape(1, M))

    tot = jnp.sum(jnp.sum(w, axis=1, keepdims=True), axis=0, keepdims=True)
    gates = (w / tot * ROUTED_SCALE).reshape(N_EXPERTS, M)
    selm = selm.reshape(N_EXPERTS, M)

    earlier = (lax.broadcasted_iota(jnp.int32, (M, M), 0)
               < lax.broadcasted_iota(jnp.int32, (M, M), 1)).astype(BF16)
    rank = base_s[:, :1] + _dot(selm.astype(BF16), earlier)
    base_s[...] = base_s[...] + jnp.sum(selm, axis=1, keepdims=True)
    e_flat = lax.broadcasted_iota(jnp.int32, (N_EXPERTS, M), 0)
    for k in range(TOP_K):
        hit = e_flat == chosen[k]
        e_s[i, k:k + 1, :] = chosen[k]
        r_s[i, k:k + 1, :] = jnp.sum(jnp.where(hit, rank, 0.0), axis=0, keepdims=True)
        w_ref[k:k + 1, :] = jnp.sum(jnp.where(hit, gates, 0.0), axis=0, keepdims=True)


def route(hp, router_w_t, router_bias, l, tile_rows, n_tiles, rows=512):
    N, half = hp.shape
    M = rows
    nT = N // M
    assert N % M == 0
    nt_pad = -(-n_tiles // 128) * 128

    def p0(ph, i):
        return i * (1 - ph) + (nT - 1) * ph

    return pl.pallas_call(
        functools.partial(_route_kernel, tile_rows=tile_rows),
        grid=(2, nT),
        in_specs=[pl.BlockSpec((M, half), lambda ph, i: (p0(ph, i), 0)),
                  pl.BlockSpec((None, N_EXPERTS, 2 * half), lambda ph, i: (l, 0, 0)),
                  pl.BlockSpec((None, N_EXPERTS, 1), lambda ph, i: (l, 0, 0))],
        out_specs=[pl.BlockSpec((TOP_K, M), lambda ph, i: (0, i * ph)),
                   pl.BlockSpec((TOP_K, M), lambda ph, i: (0, p0(ph, i))),
                   pl.BlockSpec((1, nt_pad), lambda ph, i: (0, 0)),
                   pl.BlockSpec((1, 128), lambda ph, i: (0, 0))],
        out_shape=[jax.ShapeDtypeStruct((TOP_K, N), jnp.int32),
                   jax.ShapeDtypeStruct((TOP_K, N), F32),
                   jax.ShapeDtypeStruct((1, nt_pad), jnp.int32),
                   jax.ShapeDtypeStruct((1, 128), jnp.int32)],
        scratch_shapes=[pltpu.VMEM((nT, TOP_K, M), jnp.int32), pltpu.VMEM((nT, TOP_K, M), F32),
                        pltpu.VMEM((N_EXPERTS, 128), F32), pltpu.VMEM((N_EXPERTS, 128), F32)],
        compiler_params=_cparams(2),
        name="route",
    )(hp, router_w_t, router_bias.reshape(-1, N_EXPERTS, 1))


def _sc_mesh():
    return plsc.VectorSubcoreMesh(core_axis_name="core", subcore_axis_name="subcore")


def sc_invert(pos_flat, n_tok, n_out):
    n = pos_flat.shape[0]
    per = n_out // SC_WORKERS
    chunk = n_tok
    assert n_out % SC_WORKERS == 0 and per % SC_LANES == 0
    assert n_tok % chunk == 0 and n % chunk == 0 and chunk % SC_LANES == 0
    cp = pltpu.CompilerParams()
    if "needs_layout_passes" in pltpu.CompilerParams.__dataclass_fields__:
        cp = dataclasses.replace(cp, needs_layout_passes=False)

    @functools.partial(
        pl.kernel, out_type=jax.ShapeDtypeStruct((n_out,), jnp.int32), mesh=_sc_mesh(),
        scratch_types=[pltpu.VMEM((chunk,), jnp.int32), pltpu.VMEM((per,), jnp.int32)],
        compiler_params=cp, name="sc_invert")
    def k(pos_hbm, src_hbm, pos_v, src_v):
        wid = lax.axis_index("subcore") * SC_CORES + lax.axis_index("core")
        lo = wid * per
        lane = lax.iota(jnp.int32, SC_LANES)

        @pl.loop(0, per, step=SC_LANES)
        def _(r):
            src_v[pl.ds(r, SC_LANES)] = lax.rem(lo + r + lane, n_tok)

        @pl.loop(0, n // chunk)
        def _(c):
            base = c * chunk
            pltpu.sync_copy(pos_hbm.at[pl.ds(base, chunk)], pos_v)
            tok0 = lax.rem(base, n_tok)

            @pl.loop(0, chunk, step=SC_LANES)
            def _(r):
                p = pos_v[pl.ds(r, SC_LANES)] - lo
                mine = (p >= 0) & (p < per)
                plsc.store_scatter(src_v, [jnp.where(mine, p, 0)], tok0 + r + lane, mask=mine)

        pltpu.sync_copy(src_v, src_hbm.at[pl.ds(lo, per)])

    return k(pos_flat)


def sc_gather(x, idx):
    n = idx.shape[0]
    dim = x.shape[1]
    assert n % (SC_WINDOW * SC_WORKERS) == 0

    @functools.partial(
        pl.kernel, out_type=jax.ShapeDtypeStruct((n, dim), x.dtype), mesh=_sc_mesh(),
        scratch_types=[], name="sc_gather")
    def k(x_hbm, i_hbm, o_hbm):
        def body(i_vmem, o_vmem):
            pltpu.sync_copy(x_hbm.at[i_vmem.at[0]], o_vmem)

        pltpu.emit_pipeline(
            body, grid=(n // SC_WINDOW,),
            in_specs=[pl.BlockSpec((1, SC_WINDOW), index_map=lambda i: (i, 0))],
            out_specs=[pl.BlockSpec((SC_WINDOW, dim), index_map=lambda i: (i, 0))],
            core_axis_name=("core", "subcore"),
            dimension_semantics=(pltpu.PARALLEL,),
        )(i_hbm, o_hbm)

    return k(x, idx.reshape(n // SC_WINDOW, SC_WINDOW))


def _moe_gemm_kernel(te_ref, nu_ref, x_ref, wi_ref, wo_ref, o_ref, wi_b, wo_b):
    j = pl.program_id(0)
    e_now = te_ref[j]
    e_prev = te_ref[jnp.maximum(j - 1, 0)]

    @pl.when((j == 0) | (e_now != e_prev))
    def _():
        wi_b[...] = wi_ref[...].astype(BF16)
        wo_b[...] = wo_ref[...].astype(BF16)

    @pl.when(j < nu_ref[0])
    def _():
        lo, hi = _unpack_pairs(x_ref[...])
        half = lo.shape[1]
        hu = _dot(lo, wi_b[:half, :]) + _dot(hi, wi_b[half:, :])
        act = (_silu(hu[:, :EXPERT_FF]) * hu[:, EXPERT_FF:]).astype(BF16)
        o_ref[...] = _pack_pairs(_dot(act, wo_b[...]))

    @pl.when(j >= nu_ref[0])
    def _():
        o_ref[...] = jnp.zeros_like(o_ref)


def moe_gemm(xs, tile_expert, n_used, exp_w_in, exp_w_out, l, tile_rows, n_tiles):
    P, half = xs.shape
    D = 2 * half
    assert P == n_tiles * tile_rows
    grid_spec = pltpu.PrefetchScalarGridSpec(
        num_scalar_prefetch=2,
        grid=(n_tiles,),
        in_specs=[pl.BlockSpec((tile_rows, half), lambda j, te, nu: (j, 0)),
                  pl.BlockSpec((None, None, D, 2 * EXPERT_FF), lambda j, te, nu: (l, te[j], 0, 0)),
                  pl.BlockSpec((None, None, EXPERT_FF, D), lambda j, te, nu: (l, te[j], 0, 0))],
        out_specs=pl.BlockSpec((tile_rows, half), lambda j, te, nu: (j, 0)),
        scratch_shapes=[pltpu.VMEM((D, 2 * EXPERT_FF), BF16), pltpu.VMEM((EXPERT_FF, D), BF16)],
    )
    return pl.pallas_call(
        _moe_gemm_kernel,
        grid_spec=grid_spec,
        out_shape=jax.ShapeDtypeStruct((P, half), jnp.int32),
        compiler_params=_cparams(1),
        name="moe_gemm",
    )(tile_expert, n_used, xs, exp_w_in, exp_w_out)


def _moe_combine_kernel(y_ref, w_ref, h_ref, si_ref, so_ref, x_ref, g2_ref, o_ref, si_b, so_b):
    @pl.when(pl.program_id(0) == 0)
    def _():
        si_b[...] = si_ref[...].astype(BF16)
        so_b[...] = so_ref[...].astype(BF16)

    bb, tt, D = x_ref.shape
    half = D // 2
    w = w_ref[...]
    acc_lo = jnp.zeros((bb * tt, half), F32)
    acc_hi = jnp.zeros((bb * tt, half), F32)
    for k in range(TOP_K):
        lo, hi = _unpack_pairs(y_ref[k], F32)
        acc_lo = acc_lo + w[:, k:k + 1] * lo
        acc_hi = acc_hi + w[:, k:k + 1] * hi
    hlo, hhi = _unpack_pairs(h_ref[...])
    hu = _dot(hlo, si_b[:half, :]) + _dot(hhi, si_b[half:, :])
    act = (_silu(hu[:, :SHARED_FF]) * hu[:, SHARED_FF:]).astype(BF16)
    y = jnp.concatenate([acc_lo, acc_hi], axis=-1) + _dot(act, so_b[...])
    o_ref[...] = x_ref[...] + g2_ref[...] * y.reshape(bb, tt, D)


def moe_combine(y8, w_t, hp, sh_w_in, sh_w_out, x, mod, gate_idx, l, row0, rows=256):
    B, T, D = x.shape
    half = D // 2
    bb, tt, nblk, ij = _row_blocks(B, T, rows)
    M = bb * tt
    assert row0 % M == 0
    off = row0 // M
    xspec = pl.BlockSpec((bb, tt, D), lambda i: ij(i) + (0,))
    return pl.pallas_call(
        _moe_combine_kernel,
        grid=(nblk,),
        in_specs=[pl.BlockSpec((TOP_K, M, half), lambda i: (0, off + i, 0)),
                  pl.BlockSpec((M, TOP_K), lambda i: (off + i, 0)),
                  pl.BlockSpec((M, half), lambda i: (off + i, 0)),
                  pl.BlockSpec((None, D, 2 * SHARED_FF), lambda i: (l, 0, 0)),
                  pl.BlockSpec((None, SHARED_FF, D), lambda i: (l, 0, 0)),
                  xspec,
                  pl.BlockSpec((bb, 1, D), lambda i: (ij(i)[0], 0, gate_idx))],
        out_specs=xspec,
        out_shape=jax.ShapeDtypeStruct((B, T, D), F32),
        scratch_shapes=[pltpu.VMEM((D, 2 * SHARED_FF), BF16), pltpu.VMEM((SHARED_FF, D), BF16)],
        compiler_params=_cparams(1),
        name="moe_combine",
    )(y8, w_t, hp, sh_w_in, sh_w_out, x, mod)


def _shared_kv_kernel(x_ref, g_ref, w_ref, lg_ref, cos_ref, sin_ref, lat_ref, kr_ref):
    bb, tt, D = x_ref.shape
    xn = _rms(x_ref[...], g_ref[...]).reshape(bb * tt, D).astype(BF16)
    z = _dot(xn, w_ref[...].astype(BF16))
    lat = _rms(z[:, :KV_LORA], lg_ref[...])
    lat_ref[...] = lat.reshape(bb, tt, KV_LORA)
    zr = z[:, KV_LORA:KV_LORA + MLA_ROPE].reshape(bb, tt, MLA_ROPE)
    zq = z[:, KV_LORA + 128:KV_LORA + 128 + MLA_ROPE].reshape(bb, tt, MLA_ROPE)
    kr_ref[...] = zr * cos_ref[...] + zq * sin_ref[...]


def shared_kv(x, kv_in_g, w_kv, kv_lat_g, cos32, sin32, rows=512):
    B, T, D = x.shape
    bb, tt, nblk, ij = _row_blocks(B, T, rows)
    tspec = pl.BlockSpec((tt, MLA_ROPE), lambda i: (ij(i)[1], 0))
    return pl.pallas_call(
        _shared_kv_kernel,
        grid=(nblk,),
        in_specs=[pl.BlockSpec((bb, tt, D), lambda i: ij(i) + (0,)),
                  pl.BlockSpec((1, D), lambda i: (0, 0)),
                  pl.BlockSpec(w_kv.shape, lambda i: (0, 0)),
                  pl.BlockSpec((1, KV_LORA), lambda i: (0, 0)),
                  tspec, tspec],
        out_specs=[pl.BlockSpec((bb, tt, KV_LORA), lambda i: ij(i) + (0,)),
                   pl.BlockSpec((bb, tt, MLA_ROPE), lambda i: ij(i) + (0,))],
        out_shape=[jax.ShapeDtypeStruct((B, T, KV_LORA), F32),
                   jax.ShapeDtypeStruct((B, T, MLA_ROPE), F32)],
        compiler_params=_cparams(1),
        name="shared_kv",
    )(x, kv_in_g.reshape(1, D), w_kv, kv_lat_g.reshape(1, KV_LORA), cos32, sin32)


def _kv_expand_kernel(lat_ref, kr_ref, wk_ref, ek_ref, wv_ref, k_ref, v_ref):
    bb, tt, _ = lat_ref.shape
    lat = lat_ref[...].reshape(bb * tt, KV_LORA).astype(BF16)
    kr = kr_ref[...].reshape(bb * tt, MLA_ROPE).astype(BF16)
    k = _dot(lat, wk_ref[...].astype(BF16)) + _dot(kr, ek_ref[...].astype(BF16))
    k_ref[...] = k.reshape(bb, tt, k.shape[-1]).astype(k_ref.dtype)
    v = _dot(lat, wv_ref[...].astype(BF16))
    v_ref[...] = v.reshape(bb, tt, v.shape[-1]).astype(v_ref.dtype)


def kv_expand(lat, kr, wk_pad, ek, wv, rows=512):
    B, T, _ = lat.shape
    bb, tt, nblk, ij = _row_blocks(B, T, rows)
    NK, NV = wk_pad.shape[1], wv.shape[1]

    def full(a):
        return pl.BlockSpec(a.shape, lambda i: (0, 0))

    def rowspec(n):
        return pl.BlockSpec((bb, tt, n), lambda i: ij(i) + (0,))

    return pl.pallas_call(
        _kv_expand_kernel,
        grid=(nblk,),
        in_specs=[rowspec(KV_LORA), rowspec(MLA_ROPE), full(wk_pad), full(ek), full(wv)],
        out_specs=[rowspec(NK), rowspec(NV)],
        out_shape=[jax.ShapeDtypeStruct((B, T, NK), BF16), jax.ShapeDtypeStruct((B, T, NV), BF16)],
        compiler_params=_cparams(1),
        name="kv_expand",
    )(lat, kr, wk_pad, ek, wv)


def _query_kernel(h_ref, wdq_ref, qg_ref, wq_ref, wqr_ref, c_ref, s_ref, q_ref, wdq_b, wq_b, wqr_b):
    @pl.when(pl.program_id(0) == 0)
    def _():
        wdq_b[...] = wdq_ref[...].astype(BF16)
        wq_b[...] = wq_ref[...].astype(BF16)
        wqr_b[...] = wqr_ref[...].astype(BF16)

    bb, tt, D = h_ref.shape
    h = h_ref[...].reshape(bb * tt, D)
    cq = _rms(_dot(h, wdq_b[...]), qg_ref[...]).astype(BF16)
    q1 = _dot(cq, wq_b[...]).reshape(bb, tt, -1)
    q2 = _dot(cq, wqr_b[...]).reshape(bb, tt, -1)
    c = c_ref[...]
    s = s_ref[...]
    for hd in range(MLA_HEADS):
        sl = slice(hd * HEAD_PAD, (hd + 1) * HEAD_PAD)
        q_ref[:, :, sl] = (q1[:, :, sl] * c + q2[:, :, sl] * s).astype(q_ref.dtype)


def mla_queries(h, w_dq, q_norm_g, wq_pad, wq_rot, l, c128, s128, rows=512):
    B, T, D = h.shape
    bb, tt, nblk, ij = _row_blocks(B, T, rows)
    NQ = wq_pad.shape[-1]
    tspec = pl.BlockSpec((tt, HEAD_PAD), lambda i: (ij(i)[1], 0))
    return pl.pallas_call(
        _query_kernel,
        grid=(nblk,),
        in_specs=[pl.BlockSpec((bb, tt, D), lambda i: ij(i) + (0,)),
                  pl.BlockSpec((None, D, Q_LORA), lambda i: (l, 0, 0)),
                  pl.BlockSpec((None, 1, Q_LORA), lambda i: (l, 0, 0)),
                  pl.BlockSpec((None, Q_LORA, NQ), lambda i: (l, 0, 0)),
                  pl.BlockSpec((None, Q_LORA, NQ), lambda i: (l, 0, 0)),
                  tspec, tspec],
        out_specs=pl.BlockSpec((bb, tt, NQ), lambda i: ij(i) + (0,)),
        out_shape=jax.ShapeDtypeStruct((B, T, NQ), BF16),
        scratch_shapes=[pltpu.VMEM((D, Q_LORA), BF16), pltpu.VMEM((Q_LORA, NQ), BF16),
                        pltpu.VMEM((Q_LORA, NQ), BF16)],
        compiler_params=_cparams(1),
        name="mla_queries",
    )(h, w_dq, q_norm_g.reshape(-1, 1, Q_LORA), wq_pad, wq_rot, c128, s128)


def _attn_prompt_kernel(qi_tab, ki_tab, q_ref, k_ref, v_ref, o_ref, m_ref, l_ref, acc_ref, *, tq, tk):
    p_id = pl.program_id(1)
    qi = qi_tab[p_id]
    ki = ki_tab[p_id]

    @pl.when(ki == 0)
    def _():
        m_ref[...] = jnp.full_like(m_ref, NEG_INF)
        l_ref[...] = jnp.zeros_like(l_ref)
        acc_ref[...] = jnp.zeros_like(acc_ref)

    qchunk = (qi * tq + lax.broadcasted_iota(jnp.int32, (tq, tk), 0)) // CHUNK
    kchunk = (ki * tk + lax.broadcasted_iota(jnp.int32, (tq, tk), 1)) // CHUNK
    mask = kchunk <= qchunk
    left = lax.broadcasted_iota(jnp.int32, (tq, HEAD_PAD), 1) < MLA_V

    for pair in range(MLA_HEADS // 2):
        vp = v_ref[0, :, pair * 128:(pair + 1) * 128]
        pv, alpha = [], []
        for hd in (2 * pair, 2 * pair + 1):
            sl = slice(hd * HEAD_PAD, (hd + 1) * HEAD_PAD)
            s = jnp.where(mask, _dot_nt(q_ref[0, :, sl], k_ref[0, :, sl]), NEG_INF)
            m_prev = m_ref[hd]
            m_new = jnp.maximum(m_prev, jnp.max(s, axis=-1, keepdims=True))
            a = jnp.exp(m_prev - m_new)
            p = jnp.exp(s - m_new[:, :1])
            l_ref[hd] = a * l_ref[hd] + jnp.sum(p, axis=-1, keepdims=True)
            m_ref[hd] = m_new
            pv.append(_dot(p.astype(BF16), vp))
            alpha.append(a)
        psl = slice(pair * 128, (pair + 1) * 128)
        acc_ref[:, psl] = jnp.where(left, alpha[0], alpha[1]) * acc_ref[:, psl] + jnp.where(left, pv[0], pv[1])

    @pl.when(ki == qi)
    def _():
        for pair in range(MLA_HEADS // 2):
            psl = slice(pair * 128, (pair + 1) * 128)
            lsum = jnp.where(left, l_ref[2 * pair], l_ref[2 * pair + 1])
            o_ref[0, :, psl] = (acc_ref[:, psl] / lsum).astype(o_ref.dtype)


def attn_prompt(q, k, v, tq=256):
    B, T, NQ = q.shape
    NV = v.shape[-1]
    tk = tq
    assert tq % CHUNK == 0
    nq = T // tq
    pairs = [(a, b) for a in range(nq) for b in range(a + 1)]
    qi_tab = jnp.asarray([a for a, _ in pairs], jnp.int32)
    ki_tab = jnp.asarray([b for _, b in pairs], jnp.int32)
    grid_spec = pltpu.PrefetchScalarGridSpec(
        num_scalar_prefetch=2,
        grid=(B, len(pairs)),
        in_specs=[pl.BlockSpec((1, tq, NQ), lambda b, p, qt, kt: (b, qt[p], 0)),
                  pl.BlockSpec((1, tk, NQ), lambda b, p, qt, kt: (b, kt[p], 0)),
                  pl.BlockSpec((1, tk, NV), lambda b, p, qt, kt: (b, kt[p], 0))],
        out_specs=pl.BlockSpec((1, tq, NV), lambda b, p, qt, kt: (b, qt[p], 0)),
        scratch_shapes=[pltpu.VMEM((MLA_HEADS, tq, HEAD_PAD), F32),
                        pltpu.VMEM((MLA_HEADS, tq, HEAD_PAD), F32),
                        pltpu.VMEM((tq, NV), F32)],
    )
    return pl.pallas_call(
        functools.partial(_attn_prompt_kernel, tq=tq, tk=tk),
        grid_spec=grid_spec,
        out_shape=jax.ShapeDtypeStruct((B, T, NV), BF16),
        compiler_params=_cparams(2),
        name="attn_prompt",
    )(qi_tab, ki_tab, q, k, v)


def _absorb_kernel(q_ref, m_ref, o_ref):
    o_ref[...] = _dot(q_ref[...], m_ref[...].astype(BF16)).astype(o_ref.dtype)


def absorb_queries(q2d, m_abs):
    N = q2d.shape[0]
    H, _, W = m_abs.shape
    return pl.pallas_call(
        _absorb_kernel,
        grid=(H,),
        in_specs=[pl.BlockSpec((N, HEAD_PAD), lambda h: (0, h)),
                  pl.BlockSpec((None, HEAD_PAD, W), lambda h: (h, 0, 0))],
        out_specs=pl.BlockSpec((None, N, W), lambda h: (h, 0, 0)),
        out_shape=jax.ShapeDtypeStruct((H, N, W), BF16),
        compiler_params=_cparams(1),
        name="absorb_queries",
    )(q2d, m_abs)


def _attn_sample_kernel(q_ref, lat_ref, kr_ref, nlat_ref, nkr_ref, o_ref, m_ref, l_ref, acc_ref):
    kb = pl.program_id(1)
    H, Q, W = q_ref.shape
    q = q_ref[...].reshape(H * Q, W)
    q_lat = q[:, :KV_LORA]
    q_rope = q[:, KV_LORA:KV_LORA + MLA_ROPE]

    def update(lat, kr):
        lat = lat.astype(BF16)
        s = _dot_nt(q_lat, lat) + _dot_nt(q_rope, kr.astype(BF16))
        m_prev = m_ref[...]
        m_new = jnp.maximum(m_prev, jnp.max(s, axis=-1, keepdims=True))
        a = jnp.exp(m_prev - m_new)
        p = jnp.exp(s - m_new[:, :1])
        l_ref[...] = a * l_ref[...] + jnp.sum(p, axis=-1, keepdims=True)
        m_ref[...] = m_new
        acc_ref[...] = jnp.concatenate([a, a], axis=-1) * acc_ref[...] + _dot(p.astype(BF16), lat)

    @pl.when(kb == 0)
    def _():
        m_ref[...] = jnp.full_like(m_ref, NEG_INF)
        l_ref[...] = jnp.zeros_like(l_ref)
        acc_ref[...] = jnp.zeros_like(acc_ref)
        update(nlat_ref[0], nkr_ref[0])

    update(lat_ref[0], kr_ref[0])

    @pl.when(kb == pl.num_programs(1) - 1)
    def _():
        lsum = l_ref[...]
        o = acc_ref[...] / jnp.concatenate([lsum, lsum], axis=-1)
        o_ref[...] = o.reshape(H, Q, KV_LORA).astype(o_ref.dtype)


def attn_sample(q_abs, cache_lat, cache_kr, new_lat, new_kr, tk=1024):
    H, N, W = q_abs.shape
    B, P, _ = cache_lat.shape
    Q = new_lat.shape[1]
    qpos = P + np.arange(Q)
    kpos = np.arange(P + Q)
    assert bool(np.all((kpos // CHUNK)[None, :] <= (qpos // CHUNK)[:, None]))
    return pl.pallas_call(
        _attn_sample_kernel,
        grid=(B, P // tk),
        in_specs=[pl.BlockSpec((H, Q, W), lambda b, kb: (0, b, 0)),
                  pl.BlockSpec((1, tk, KV_LORA), lambda b, kb: (b, kb, 0)),
                  pl.BlockSpec((1, tk, MLA_ROPE), lambda b, kb: (b, kb, 0)),
                  pl.BlockSpec((1, Q, KV_LORA), lambda b, kb: (b, 0, 0)),
                  pl.BlockSpec((1, Q, MLA_ROPE), lambda b, kb: (b, 0, 0))],
        out_specs=pl.BlockSpec((H, Q, KV_LORA), lambda b, kb: (0, b, 0)),
        out_shape=jax.ShapeDtypeStruct((H, N, KV_LORA), BF16),
        scratch_shapes=[pltpu.VMEM((H * Q, 128), F32), pltpu.VMEM((H * Q, 128), F32),
                        pltpu.VMEM((H * Q, KV_LORA), F32)],
        compiler_params=_cparams(2),
        name="attn_sample",
    )(q_abs, cache_lat, cache_kr, new_lat, new_kr)


def _unabsorb_kernel(o_ref, w_ref, out_ref):
    out_ref[...] = (_dot(o_ref[0], w_ref[0].astype(BF16))
                    + _dot(o_ref[1], w_ref[1].astype(BF16))).astype(out_ref.dtype)


def unabsorb(o_lat, wuv_pad):
    H, N, R = o_lat.shape
    return pl.pallas_call(
        _unabsorb_kernel,
        grid=(H // 2,),
        in_specs=[pl.BlockSpec((2, N, R), lambda p: (p, 0, 0)),
                  pl.BlockSpec((2, R, 128), lambda p: (p, 0, 0))],
        out_specs=pl.BlockSpec((N, 128), lambda p: (0, p)),
        out_shape=jax.ShapeDtypeStruct((N, (H // 2) * 128), BF16),
        compiler_params=_cparams(1),
        name="unabsorb",
    )(o_lat, wuv_pad)


def _rope_tables(pos):
    half = MLA_ROPE // 2
    inv = 1.0 / (ROPE_THETA ** (np.arange(half, dtype=np.float64) * 2.0 / MLA_ROPE))
    ang = np.asarray(pos, np.float64)[:, None] * inv[None, :]
    cos = np.concatenate([np.cos(ang), np.cos(ang)], axis=-1)
    sin = np.concatenate([np.sin(ang), np.sin(ang)], axis=-1)
    T = cos.shape[0]
    c128 = np.zeros((T, HEAD_PAD)); s128 = np.zeros((T, HEAD_PAD))
    c128[:, :MLA_NOPE] = 1.0
    c128[:, MLA_NOPE:MLA_NOPE + MLA_ROPE] = cos
    s128[:, MLA_NOPE:MLA_NOPE + MLA_ROPE] = sin
    return (jnp.asarray(cos, F32), jnp.asarray(sin, F32),
            jnp.asarray(c128 * QK_SCALE, F32), jnp.asarray(s128 * QK_SCALE, F32))


def _rot_half_cols(w):
    half = w.shape[-1] // 2
    return jnp.concatenate([-w[..., half:], w[..., :half]], axis=-1)


def _prep_weights(w_dkv, w_uk, w_uv, w_uq, router_w):
    D = D_MODEL
    w_lat, w_rope = w_dkv[:, :KV_LORA], w_dkv[:, KV_LORA:]
    pad96 = jnp.zeros((D, 128 - MLA_ROPE), F32)
    w_kv = jnp.concatenate([w_lat, w_rope, pad96, _rot_half_cols(w_rope), pad96], axis=-1)

    zpad = HEAD_PAD - MLA_NOPE
    wk_pad = jnp.pad(w_uk, ((0, 0), (0, 0), (0, zpad))).reshape(KV_LORA, MLA_HEADS * HEAD_PAD)
    ek = jnp.zeros((MLA_ROPE, MLA_HEADS, HEAD_PAD), F32)
    ek = ek.at[:, :, MLA_NOPE:MLA_NOPE + MLA_ROPE].set(
        jnp.broadcast_to(jnp.eye(MLA_ROPE, dtype=F32)[:, None, :], (MLA_ROPE, MLA_HEADS, MLA_ROPE)))
    ek = ek.reshape(MLA_ROPE, MLA_HEADS * HEAD_PAD)
    wv = w_uv.reshape(KV_LORA, MLA_HEADS * MLA_V)

    nb = w_uq.shape[0]
    qn, qr = w_uq[..., :MLA_NOPE], w_uq[..., MLA_NOPE:]
    z32 = jnp.zeros(qr.shape[:-1] + (HEAD_PAD - MLA_NOPE - MLA_ROPE,), F32)
    wq_pad = jnp.concatenate([qn, qr, z32], axis=-1).reshape(nb, Q_LORA, MLA_HEADS * HEAD_PAD)
    wq_rot = jnp.concatenate([jnp.zeros_like(qn), _rot_half_cols(qr), z32], axis=-1)
    wq_rot = wq_rot.reshape(nb, Q_LORA, MLA_HEADS * HEAD_PAD)

    m_abs = jnp.zeros((MLA_HEADS, HEAD_PAD, KV_LORA + 128), F32)
    m_abs = m_abs.at[:, :MLA_NOPE, :KV_LORA].set(jnp.transpose(w_uk, (1, 2, 0)))
    m_abs = m_abs.at[:, MLA_NOPE:MLA_NOPE + MLA_ROPE, KV_LORA:KV_LORA + MLA_ROPE].set(
        jnp.broadcast_to(jnp.eye(MLA_ROPE, dtype=F32), (MLA_HEADS, MLA_ROPE, MLA_ROPE)))

    wuv_h = jnp.transpose(w_uv, (1, 0, 2))
    even = jnp.pad(wuv_h, ((0, 0), (0, 0), (0, 64)))
    odd = jnp.pad(wuv_h, ((0, 0), (0, 0), (64, 0)))
    wuv_pad = jnp.where((jnp.arange(MLA_HEADS) % 2 == 0)[:, None, None], even, odd)

    rw_t = jnp.transpose(router_w, (0, 2, 1))
    return dict(w_kv=w_kv, wk_pad=wk_pad, ek=ek, wv=wv, wq_pad=wq_pad, wq_rot=wq_rot,
                m_abs=m_abs, wuv_pad=wuv_pad, rw_t=rw_t)


def _mixer(st, l, P, W):
    x, m = st["x"], st["mod"][l]
    B, T, _ = x.shape
    n_a = P["hg_w_in"].shape[0]
    h = norm_mod(x, P["norm1_g"][l], m, sc_idx=1, sh_idx=0)
    if l < n_a:
        z = linear(h, P["hg_w_in"], l, F32)
        s0 = None if st["hg_state"] is None else st["hg_state"][l]
        o, s_new = gla(z, st["lbs"][l], P["hg_onorm_g"][l], s0)
        st["hg_new"].append(s_new)
        st["x"] = linear(o, P["hg_w_out"], l, F32, x=x, mod=m, gate_idx=2)
    else:
        bi = l - n_a
        q = mla_queries(h, P["w_dq"], P["q_norm_g"], W["wq_pad"], W["wq_rot"], bi, st["c128"], st["s128"])
        if st["past_lat"] is None:
            o = attn_prompt(q, st["k_all"], st["v_all"])
        else:
            q_abs = absorb_queries(q.reshape(B * T, -1), W["m_abs"])
            o_lat = attn_sample(q_abs, st["past_lat"], st["past_kr"], st["lat"], st["kr"])
            o = unabsorb(o_lat, W["wuv_pad"]).reshape(B, T, -1)
        st["x"] = linear(o, P["w_o"], bi, F32, x=x, mod=m, gate_idx=2)


def _moe(groups, l, P, W):
    hp = jnp.concatenate([norm_mod(st["x"], P["norm2_g"][l], st["mod"][l], sc_idx=4, sh_idx=3, packed=True)
                          for st in groups], axis=0)
    n_tok = hp.shape[0]
    n_tiles = (TOP_K * n_tok) // MOE_TILE + N_EXPERTS
    pos, w8, tile_expert, n_used = route(hp, W["rw_t"], P["router_bias"], l, MOE_TILE, n_tiles)
    pos_flat = pos.reshape(-1)
    src = sc_invert(pos_flat, n_tok, n_tiles * MOE_TILE)
    xs = sc_gather(hp, src)
    out = moe_gemm(xs, tile_expert.reshape(-1), n_used[0, :1], P["exp_w_in"], P["exp_w_out"], l,
                   MOE_TILE, n_tiles)
    y8 = sc_gather(out, pos_flat).reshape(TOP_K, n_tok, -1)
    w_t = w8.T
    row0 = 0
    for st in groups:
        B, T, _ = st["x"].shape
        st["x"] = moe_combine(y8, w_t, hp, P["sh_w_in"], P["sh_w_out"], st["x"], st["mod"][l], 5, l, row0)
        row0 += B * T


def _group_state(x, mod, pos, hg_state, past_lat, past_kr, lbs):
    cos32, sin32, c128, s128 = _rope_tables(pos)
    return dict(x=x, mod=mod, hg_state=hg_state, past_lat=past_lat, past_kr=past_kr, lbs=lbs,
                cos32=cos32, sin32=sin32, c128=c128, s128=s128, hg_new=[],
                lat=None, kr=None, k_all=None, v_all=None)


def kernel(x_prompt, x_sample, state_hgrn, cache_mla_latent, cache_mla_krope, c_prompt, c_sample, ada_w, ada_b, norm1_g, norm2_g, hg_w_in, hg_lb_logits, hg_onorm_g, hg_w_out, kv_in_g, w_dkv, kv_lat_g, w_uk, w_uv, w_dq, q_norm_g, w_uq, w_o, router_w, router_bias, exp_w_in, exp_w_out, sh_w_in, sh_w_out, final_g):
    Bp, Sp, _ = x_prompt.shape
    Bs, Ss, _ = x_sample.shape
    past = cache_mla_latent.shape[1]
    P = dict(norm1_g=norm1_g, norm2_g=norm2_g, hg_w_in=hg_w_in, hg_lb_logits=hg_lb_logits,
             hg_onorm_g=hg_onorm_g, hg_w_out=hg_w_out, kv_in_g=kv_in_g, kv_lat_g=kv_lat_g,
             w_dq=w_dq, q_norm_g=q_norm_g, w_o=w_o, router_bias=router_bias,
             exp_w_in=exp_w_in, exp_w_out=exp_w_out, sh_w_in=sh_w_in, sh_w_out=sh_w_out, final_g=final_g)
    W = _prep_weights(w_dkv, w_uk, w_uv, w_uq, router_w)
    mod = ada_mod(jnp.concatenate([c_prompt, c_sample], axis=0), ada_w, ada_b)
    lbs = jnp.cumsum(jax.nn.softmax(hg_lb_logits.astype(F32), axis=0), axis=0)
    gp = _group_state(x_prompt, mod[:, :Bp, None, :], np.arange(Sp), None, None, None, lbs)
    gs = _group_state(x_sample, mod[:, Bp:, None, :], past + np.arange(Ss), state_hgrn,
                      cache_mla_latent, cache_mla_krope, lbs)
    groups = [gp, gs]
    n_a = hg_w_in.shape[0]
    for l in range(norm1_g.shape[0]):
        for st in groups:
            _mixer(st, l, P, W)
        _moe(groups, l, P, W)
        if l == n_a - 1:
            for st in groups:
                st["lat"], st["kr"] = shared_kv(st["x"], kv_in_g, W["w_kv"], kv_lat_g, st["cos32"], st["sin32"])
            gp["k_all"], gp["v_all"] = kv_expand(gp["lat"], gp["kr"], W["wk_pad"], W["ek"], W["wv"])
    outs = []
    for st in groups:
        outs.append((norm_mod(st["x"], final_g, out_dtype=F32), jnp.stack(st["hg_new"], axis=0)))
    (y_p, st_p), (y_s, st_s) = outs
    return (y_p, y_s, st_p, st_s, gp["lat"], gp["kr"], gs["lat"], gs["kr"])
```

```python
import dataclasses
import functools

import numpy as np
import jax
import jax.numpy as jnp
from jax import lax
from jax.experimental import pallas as pl
from jax.experimental.pallas import tpu as pltpu
from jax.experimental.pallas import tpu_sc as plsc

F32 = jnp.float32
BF16 = jnp.bfloat16

D_MODEL = 1024
CHUNK = 64
HG_HEADS = 8
HG_DK = 128
HG_DV = 128
MLA_HEADS = 16
MLA_NOPE = 64
MLA_ROPE = 32
MLA_V = 64
Q_LORA = 384
KV_LORA = 256
ROPE_THETA = 10000.0
N_EXPERTS = 64
TOP_K = 8
N_GROUPS = 8
TOPK_GROUPS = 4
EXPERT_FF = 256
SHARED_FF = 256
ROUTED_SCALE = 2.5
EPS = 1e-6

HEAD_PAD = 128
ATTN_LOOKAHEAD = 4
V_ROWS = MLA_V + 16
QK_SCALE = (MLA_NOPE + MLA_ROPE) ** -0.5
Q_PRESCALE = QK_SCALE * float(np.log2(np.e))
VMEM_LIMIT = 56 * 1024 * 1024
NEG_INF = float("-inf")
SC_CORES = 2
SC_SUBCORES = 16
SC_WORKERS = SC_CORES * SC_SUBCORES
SC_LANES = 16
SC_WINDOW = 64
MOE_TILE = 512


def _cparams(n_axes):
    return pltpu.CompilerParams(dimension_semantics=("arbitrary",) * n_axes,
                                vmem_limit_bytes=VMEM_LIMIT)


def _silu(x):
    return x * jax.nn.sigmoid(x)


def _rms(x, g):
    ms = jnp.mean(x * x, axis=-1, keepdims=True)
    return x * lax.rsqrt(ms + EPS) * g


def _dot(a, b):
    return jnp.dot(a, b, preferred_element_type=F32)


def _dot_nt(a, b):
    return lax.dot_general(a, b, (((1,), (1,)), ((), ())), preferred_element_type=F32)


def _dot_tn(a, b):
    return lax.dot_general(a, b, (((0,), (0,)), ((), ())), preferred_element_type=F32)


def _row_blocks(B, T, rows):
    if T >= rows:
        assert T % rows == 0
        bb, tt = 1, rows
    else:
        assert rows % T == 0 and B % (rows // T) == 0
        bb, tt = rows // T, T
    nt = T // tt
    return bb, tt, (B // bb) * nt, (lambda i: (i // nt, i % nt))


def _ada_kernel(c_ref, w_ref, b_ref, o_ref):
    a = _silu(c_ref[...]).astype(BF16)
    o_ref[...] = _dot(a, w_ref[...].astype(BF16)) + b_ref[...]


def ada_mod(c, ada_w, ada_b):
    R, D = c.shape
    L, _, N = ada_w.shape
    tn = 1536
    return pl.pallas_call(
        _ada_kernel,
        grid=(L, N // tn),
        in_specs=[pl.BlockSpec((R, D), lambda l, j: (0, 0)),
                  pl.BlockSpec((None, D, tn), lambda l, j: (l, 0, j)),
                  pl.BlockSpec((None, 1, tn), lambda l, j: (l, 0, j))],
        out_specs=pl.BlockSpec((None, R, tn), lambda l, j: (l, 0, j)),
        out_shape=jax.ShapeDtypeStruct((L, R, N), F32),
        compiler_params=_cparams(2),
        name="ada_mod",
    )(c, ada_w, ada_b.reshape(L, 1, N))


def _pack_pairs(y):
    half = y.shape[-1] // 2
    bits = lax.bitcast_convert_type(y.astype(BF16).astype(F32), jnp.uint32)
    word = lax.shift_right_logical(bits[:, :half], jnp.uint32(16)) | bits[:, half:]
    return lax.bitcast_convert_type(word, jnp.int32)


def _unpack_pairs(word, dtype=BF16):
    u = lax.bitcast_convert_type(word, jnp.uint32)
    lo = lax.bitcast_convert_type(lax.shift_left(u, jnp.uint32(16)), F32)
    hi = lax.bitcast_convert_type(u & jnp.uint32(0xFFFF0000), F32)
    return lo.astype(dtype), hi.astype(dtype)


def _norm_kernel(*refs, modulated, packed):
    if modulated:
        x_ref, g_ref, sc_ref, sh_ref, o_ref = refs
    else:
        x_ref, g_ref, o_ref = refs
    y = _rms(x_ref[...], g_ref[...])
    if modulated:
        y = y * (1.0 + sc_ref[...]) + sh_ref[...]
    if packed:
        bb, tt, D = y.shape
        o_ref[...] = _pack_pairs(y.reshape(bb * tt, D))
    else:
        o_ref[...] = y.astype(o_ref.dtype)


def norm_mod(x, g, mod=None, sc_idx=0, sh_idx=0, out_dtype=BF16, rows=512, packed=False):
    B, T, D = x.shape
    bb, tt, nblk, ij = _row_blocks(B, T, rows)
    xspec = pl.BlockSpec((bb, tt, D), lambda i: ij(i) + (0,))
    in_specs = [xspec, pl.BlockSpec((1, D), lambda i: (0, 0))]
    args = [x, g.reshape(1, D)]
    if mod is not None:
        in_specs += [pl.BlockSpec((bb, 1, D), lambda i: (ij(i)[0], 0, sc_idx)),
                     pl.BlockSpec((bb, 1, D), lambda i: (ij(i)[0], 0, sh_idx))]
        args += [mod, mod]
    if packed:
        out_specs = pl.BlockSpec((bb * tt, D // 2), lambda i: (i, 0))
        out_shape = jax.ShapeDtypeStruct((B * T, D // 2), jnp.int32)
    else:
        out_specs = xspec
        out_shape = jax.ShapeDtypeStruct((B, T, D), out_dtype)
    return pl.pallas_call(
        functools.partial(_norm_kernel, modulated=mod is not None, packed=packed),
        grid=(nblk,),
        in_specs=in_specs,
        out_specs=out_specs,
        out_shape=out_shape,
        compiler_params=_cparams(1),
        name="norm_mod",
    )(*args)


def _linear_kernel(*refs, residual):
    if residual:
        a_ref, w_ref, x_ref, gate_ref, o_ref, wb_ref = refs
    else:
        a_ref, w_ref, o_ref, wb_ref = refs

    @pl.when(pl.program_id(1) == 0)
    def _():
        wb_ref[...] = w_ref[...].astype(BF16)

    bb, tt, K = a_ref.shape
    y = _dot(a_ref[...].reshape(bb * tt, K).astype(BF16), wb_ref[...])
    y = y.reshape(bb, tt, y.shape[-1])
    if residual:
        y = x_ref[...] + gate_ref[...] * y
    o_ref[...] = y.astype(o_ref.dtype)


def linear(a, w, l, out_dtype, x=None, mod=None, gate_idx=0, rows=512, tn=1024):
    B, T, K = a.shape
    _, _, N = w.shape
    tn = min(tn, N)
    bb, tt, nblk, ij = _row_blocks(B, T, rows)
    in_specs = [pl.BlockSpec((bb, tt, K), lambda j, i: ij(i) + (0,)),
                pl.BlockSpec((None, K, tn), lambda j, i: (l, 0, j))]
    args = [a, w]
    ospec = pl.BlockSpec((bb, tt, tn), lambda j, i: ij(i) + (j,))
    if x is not None:
        gsteps = D_MODEL // tn
        in_specs += [ospec, pl.BlockSpec((bb, 1, tn), lambda j, i: (ij(i)[0], 0, gate_idx * gsteps + j))]
        args += [x, mod]
    return pl.pallas_call(
        functools.partial(_linear_kernel, residual=x is not None),
        grid=(N // tn, nblk),
        in_specs=in_specs,
        out_specs=ospec,
        out_shape=jax.ShapeDtypeStruct((B, T, N), out_dtype),
        scratch_shapes=[pltpu.VMEM((K, tn), BF16)],
        compiler_params=_cparams(2),
        name="linear",
    )(*args)


def _gla_kernel(*refs, L, n_chunks, has_init):
    if has_init:
        q_ref, f_ref, i_ref, g_ref, lb_ref, on_ref, s0_ref, o_ref, so_ref, st_ref = refs
    else:
        q_ref, f_ref, i_ref, g_ref, lb_ref, on_ref, o_ref, so_ref, st_ref = refs
    t = pl.program_id(1)
    H = st_ref.shape[0]

    @pl.when(t == 0)
    def _():
        for h in range(H):
            if has_init:
                st_ref[h] = s0_ref[0, h].T
            else:
                st_ref[h] = jnp.zeros(st_ref.shape[1:], F32)

    lb = lb_ref[...]
    onorm = on_ref[...]
    row = lax.broadcasted_iota(jnp.int32, (L, L), 0)
    col = lax.broadcasted_iota(jnp.int32, (L, L), 1)
    causal = col <= row
    tri = causal.astype(BF16)
    mid = L // 2 - 1

    def chunk(c, carry):
        r0 = pl.multiple_of(c * L, L)
        q = _silu(q_ref[0, pl.ds(r0, L), :])
        fg = lb + (1.0 - lb) * jax.nn.sigmoid(f_ref[0, pl.ds(r0, L), :])
        k = 1.0 - fg
        v = i_ref[0, pl.ds(r0, L), :].astype(BF16)
        gate = _silu(g_ref[0, pl.ds(r0, L), :])
        logf = jnp.log(fg)
        hi = logf.astype(BF16)
        lo = (logf - hi.astype(F32)).astype(BF16)
        b = _dot(tri, hi) + _dot(tri, lo)
        b_mid = b[mid:mid + 1, :]
        b_last = b[L - 1:L, :]
        up = jnp.exp(b - b_mid)
        down = jnp.exp(b_mid - b)
        qa = q * up
        kb = k * down
        qe = (qa * jnp.exp(b_mid)).astype(BF16)
        kd = (kb * jnp.exp(b_last - b_mid)).astype(BF16)
        qa = qa.astype(BF16)
        kb = kb.astype(BF16)
        decay = jnp.exp(b_last)
        sls = [slice(h * HG_DK, (h + 1) * HG_DK) for h in range(H)]
        sts = [st_ref[h] for h in range(H)]
        scores = [_dot_nt(qa[:, sl], kb[:, sl]) for sl in sls]
        inter = [_dot_nt(qe[:, sl], st.astype(BF16)) for sl, st in zip(sls, sts)]
        outer = [_dot_tn(v[:, sl], kd[:, sl]) for sl in sls]
        intra = [_dot(jnp.where(causal, sc, 0.0).astype(BF16), v[:, sl]) for sc, sl in zip(scores, sls)]
        for h, sl in enumerate(sls):
            st_ref[h] = sts[h] * decay[:, sl] + outer[h]
            o = _rms(inter[h] + intra[h], onorm[:, sl]) * gate[:, sl]
            o_ref[0, pl.ds(r0, L), sl] = o.astype(o_ref.dtype)
        return carry

    lax.fori_loop(0, n_chunks, chunk, 0)

    @pl.when(t == pl.num_programs(1) - 1)
    def _():
        for h in range(H):
            so_ref[0, h] = st_ref[h].T


def gla(z, lb, onorm_g, s0):
    B, T, _ = z.shape
    L = CHUNK if T % CHUNK == 0 else T
    tt = min(T, 512)
    n_chunks = tt // L
    H, D = HG_HEADS, D_MODEL

    def zspec(part):
        return pl.BlockSpec((1, tt, D), lambda b, t: (b, t, part))

    hspec = pl.BlockSpec((1, D), lambda b, t: (0, 0))
    sspec = pl.BlockSpec((1, H, HG_DK, HG_DV), lambda b, t: (b, 0, 0, 0))
    in_specs = [zspec(0), zspec(1), zspec(2), zspec(3), hspec, hspec]
    args = [z, z, z, z, lb.reshape(1, D), onorm_g.reshape(1, D)]
    if s0 is not None:
        in_specs.append(sspec)
        args.append(s0)
    return pl.pallas_call(
        functools.partial(_gla_kernel, L=L, n_chunks=n_chunks, has_init=s0 is not None),
        grid=(B, T // tt),
        in_specs=in_specs,
        out_specs=[pl.BlockSpec((1, tt, D), lambda b, t: (b, t, 0)), sspec],
        out_shape=[jax.ShapeDtypeStruct((B, T, D), BF16),
                   jax.ShapeDtypeStruct((B, H, HG_DK, HG_DV), F32)],
        scratch_shapes=[pltpu.VMEM((H, HG_DV, HG_DK), F32)],
        compiler_params=_cparams(2),
        name="gla",
    )(*args)


def _route_kernel(h_ref, rw_ref, bias_ref, pos_ref, w_ref, te_ref, nu_ref,
                  e_s, r_s, base_s, start_s, *, tile_rows):
    ph = pl.program_id(0)
    i = pl.program_id(1)
    M = h_ref.shape[0]
    half = h_ref.shape[1]
    G, E = N_GROUPS, N_EXPERTS // N_GROUPS
    e_flat = lax.broadcasted_iota(jnp.int32, (N_EXPERTS, M), 0)

    @pl.when(ph == 1)
    def _():
        @pl.when(i == 0)
        def _():
            cnt = base_s[...]
            padded = jnp.floor((cnt + (tile_rows - 1)) * (1.0 / tile_rows)) * tile_rows
            r = lax.broadcasted_iota(jnp.int32, (N_EXPERTS, N_EXPERTS), 0)
            c = lax.broadcasted_iota(jnp.int32, (N_EXPERTS, N_EXPERTS), 1)
            start = jnp.dot((c < r).astype(F32), padded, preferred_element_type=F32,
                            precision=lax.Precision.HIGHEST)
            start_s[...] = start
            seg_end = (start + padded)[:, :1]
            nt = te_ref.shape[1]
            tile_lo = (lax.broadcasted_iota(jnp.int32, (N_EXPERTS, nt), 1) * tile_rows).astype(F32)
            owner = jnp.sum((seg_end <= tile_lo).astype(F32), axis=0, keepdims=True)
            te_ref[...] = jnp.minimum(owner, N_EXPERTS - 1.0).astype(jnp.int32)
            total = jnp.max(start + padded, axis=0, keepdims=True)
            nu_ref[...] = (total * (1.0 / tile_rows)).astype(jnp.int32)

        start_col = start_s[:, :1]
        for k in range(TOP_K):
            hit = e_flat == e_s[i, k:k + 1, :]
            seg = jnp.sum(jnp.where(hit, start_col, 0.0), axis=0, keepdims=True)
            pos_ref[k:k + 1, :] = (seg + r_s[i, k:k + 1, :]).astype(jnp.int32)

    @pl.when(ph == 0)
    def _():
        _route_pass0(h_ref, rw_ref, bias_ref, w_ref, e_s, r_s, base_s, i, M, half, G, E)


def _route_pass0(h_ref, rw_ref, bias_ref, w_ref, e_s, r_s, base_s, i, M, half, G, E):
    @pl.when(i == 0)
    def _():
        base_s[...] = jnp.zeros_like(base_s)

    lo, hi = _unpack_pairs(h_ref[...])
    rw = rw_ref[...].astype(BF16)
    logits = _dot_nt(rw[:, :half], lo) + _dot_nt(rw[:, half:], hi)
    s = jax.nn.sigmoid(logits)
    sb = (s + bias_ref[...]).reshape(G, E, M)
    s = s.reshape(G, E, M)
    e_in = lax.broadcasted_iota(jnp.int32, (G, E, M), 1)
    g_id = lax.broadcasted_iota(jnp.int32, (G, 1, M), 0)
    e_id = lax.broadcasted_iota(jnp.int32, (G, E, M), 0) * E + e_in

    m1 = jnp.max(sb, axis=1, keepdims=True)
    first = jnp.min(jnp.where(sb == m1, e_in, E), axis=1, keepdims=True)
    m2 = jnp.max(jnp.where(e_in == first, NEG_INF, sb), axis=1, keepdims=True)
    gs = m1 + m2

    rank = jnp.zeros((G, 1, M), jnp.int32)
    for j in range(G):
        gj = gs[j:j + 1]
        beats = (gj > gs) | ((gj == gs) & (j < g_id))
        rank = rank + beats.astype(jnp.int32)
    gsel = rank < TOPK_GROUPS

    vals = jnp.where(gsel, sb, NEG_INF)
    w = jnp.zeros((G, E, M), F32)
    selm = jnp.zeros((G, E, M), F32)
    chosen = []
    for _ in range(TOP_K):
        m = jnp.max(jnp.max(vals, axis=1, keepdims=True), axis=0, keepdims=True)
        cand = jnp.where(vals == m, e_id, N_EXPERTS)
        first = jnp.min(jnp.min(cand, axis=1, keepdims=True), axis=0, keepdims=True)
        hit = e_id == first
        w = jnp.where(hit, s, w)
        selm = jnp.where(hit, 1.0, selm)
        vals = jnp.where(hit, NEG_INF, vals)
        chosen.append(first.reshape(1, M))

    tot = jnp.sum(jnp.sum(w, axis=1, keepdims=True), axis=0, keepdims=True)
    gates = (w / tot * ROUTED_SCALE).reshape(N_EXPERTS, M)
    selm = selm.reshape(N_EXPERTS, M)

    earlier = (lax.broadcasted_iota(jnp.int32, (M, M), 0)
               < lax.broadcasted_iota(jnp.int32, (M, M), 1)).astype(BF16)
    rank = base_s[:, :1] + _dot(selm.astype(BF16), earlier)
    base_s[...] = base_s[...] + jnp.sum(selm, axis=1, keepdims=True)
    e_flat = lax.broadcasted_iota(jnp.int32, (N_EXPERTS, M), 0)
    for k in range(TOP_K):
        hit = e_flat == chosen[k]
        e_s[i, k:k + 1, :] = chosen[k]
        r_s[i, k:k + 1, :] = jnp.sum(jnp.where(hit, rank, 0.0), axis=0, keepdims=True)
        w_ref[k:k + 1, :] = jnp.sum(jnp.where(hit, gates, 0.0), axis=0, keepdims=True)


def route(hp, router_w_t, router_bias, l, tile_rows, n_tiles, rows=512):
    N, half = hp.shape
    M = rows
    nT = N // M
    assert N % M == 0
    nt_pad = -(-n_tiles // 128) * 128

    def p0(ph, i):
        return i * (1 - ph) + (nT - 1) * ph

    return pl.pallas_call(
        functools.partial(_route_kernel, tile_rows=tile_rows),
        grid=(2, nT),
        in_specs=[pl.BlockSpec((M, half), lambda ph, i: (p0(ph, i), 0)),
                  pl.BlockSpec((None, N_EXPERTS, 2 * half), lambda ph, i: (l, 0, 0)),
                  pl.BlockSpec((None, N_EXPERTS, 1), lambda ph, i: (l, 0, 0))],
        out_specs=[pl.BlockSpec((TOP_K, M), lambda ph, i: (0, i * ph)),
                   pl.BlockSpec((TOP_K, M), lambda ph, i: (0, p0(ph, i))),
                   pl.BlockSpec((1, nt_pad), lambda ph, i: (0, 0)),
                   pl.BlockSpec((1, 128), lambda ph, i: (0, 0))],
        out_shape=[jax.ShapeDtypeStruct((TOP_K, N), jnp.int32),
                   jax.ShapeDtypeStruct((TOP_K, N), F32),
                   jax.ShapeDtypeStruct((1, nt_pad), jnp.int32),
                   jax.ShapeDtypeStruct((1, 128), jnp.int32)],
        scratch_shapes=[pltpu.VMEM((nT, TOP_K, M), jnp.int32), pltpu.VMEM((nT, TOP_K, M), F32),
                        pltpu.VMEM((N_EXPERTS, 128), F32), pltpu.VMEM((N_EXPERTS, 128), F32)],
        compiler_params=_cparams(2),
        name="route",
    )(hp, router_w_t, router_bias.reshape(-1, N_EXPERTS, 1))


def _sc_mesh():
    return plsc.VectorSubcoreMesh(core_axis_name="core", subcore_axis_name="subcore")


def sc_invert(pos_flat, n_tok, n_out):
    n = pos_flat.shape[0]
    per = n_out // SC_WORKERS
    chunk = n_tok
    assert n_out % SC_WORKERS == 0 and per % SC_LANES == 0
    assert n_tok % chunk == 0 and n % chunk == 0 and chunk % SC_LANES == 0
    cp = pltpu.CompilerParams()
    if "needs_layout_passes" in pltpu.CompilerParams.__dataclass_fields__:
        cp = dataclasses.replace(cp, needs_layout_passes=False)

    @functools.partial(
        pl.kernel, out_type=jax.ShapeDtypeStruct((n_out,), jnp.int32), mesh=_sc_mesh(),
        scratch_types=[pltpu.VMEM((chunk,), jnp.int32), pltpu.VMEM((per,), jnp.int32)],
        compiler_params=cp, name="sc_invert")
    def k(pos_hbm, src_hbm, pos_v, src_v):
        wid = lax.axis_index("subcore") * SC_CORES + lax.axis_index("core")
        lo = wid * per
        lane = lax.iota(jnp.int32, SC_LANES)

        @pl.loop(0, per, step=SC_LANES)
        def _(r):
            src_v[pl.ds(r, SC_LANES)] = lax.rem(lo + r + lane, n_tok)

        @pl.loop(0, n // chunk)
        def _(c):
            base = c * chunk
            pltpu.sync_copy(pos_hbm.at[pl.ds(base, chunk)], pos_v)
            tok0 = lax.rem(base, n_tok)

            @pl.loop(0, chunk, step=SC_LANES)
            def _(r):
                p = pos_v[pl.ds(r, SC_LANES)] - lo
                mine = (p >= 0) & (p < per)
                plsc.store_scatter(src_v, [jnp.where(mine, p, 0)], tok0 + r + lane, mask=mine)

        pltpu.sync_copy(src_v, src_hbm.at[pl.ds(lo, per)])

    return k(pos_flat)


def sc_gather(x, idx):
    n = idx.shape[0]
    dim = x.shape[1]
    assert n % (SC_WINDOW * SC_WORKERS) == 0

    @functools.partial(
        pl.kernel, out_type=jax.ShapeDtypeStruct((n, dim), x.dtype), mesh=_sc_mesh(),
        scratch_types=[], name="sc_gather")
    def k(x_hbm, i_hbm, o_hbm):
        def body(i_vmem, o_vmem):
            pltpu.sync_copy(x_hbm.at[i_vmem.at[0]], o_vmem)

        pltpu.emit_pipeline(
            body, grid=(n // SC_WINDOW,),
            in_specs=[pl.BlockSpec((1, SC_WINDOW), index_map=lambda i: (i, 0))],
            out_specs=[pl.BlockSpec((SC_WINDOW, dim), index_map=lambda i: (i, 0))],
            core_axis_name=("core", "subcore"),
            dimension_semantics=(pltpu.PARALLEL,),
        )(i_hbm, o_hbm)

    return k(x, idx.reshape(n // SC_WINDOW, SC_WINDOW))


def _moe_gemm_kernel(te_ref, nu_ref, x_ref, wi_ref, wo_ref, o_ref, wi_b, wo_b):
    j = pl.program_id(0)
    e_now = te_ref[j]
    e_prev = te_ref[jnp.maximum(j - 1, 0)]

    @pl.when((j == 0) | (e_now != e_prev))
    def _():
        wi_b[...] = wi_ref[...].astype(BF16)
        wo_b[...] = wo_ref[...].astype(BF16)

    @pl.when(j < nu_ref[0])
    def _():
        lo, hi = _unpack_pairs(x_ref[...])
        half = lo.shape[1]
        hu = _dot(lo, wi_b[:half, :]) + _dot(hi, wi_b[half:, :])
        act = (_silu(hu[:, :EXPERT_FF]) * hu[:, EXPERT_FF:]).astype(BF16)
        o_ref[...] = _pack_pairs(_dot(act, wo_b[...]))

    @pl.when(j >= nu_ref[0])
    def _():
        o_ref[...] = jnp.zeros_like(o_ref)


def moe_gemm(xs, tile_expert, n_used, exp_w_in, exp_w_out, l, tile_rows, n_tiles):
    P, half = xs.shape
    D = 2 * half
    assert P == n_tiles * tile_rows
    grid_spec = pltpu.PrefetchScalarGridSpec(
        num_scalar_prefetch=2,
        grid=(n_tiles,),
        in_specs=[pl.BlockSpec((tile_rows, half), lambda j, te, nu: (j, 0)),
                  pl.BlockSpec((None, None, D, 2 * EXPERT_FF), lambda j, te, nu: (l, te[j], 0, 0)),
                  pl.BlockSpec((None, None, EXPERT_FF, D), lambda j, te, nu: (l, te[j], 0, 0))],
        out_specs=pl.BlockSpec((tile_rows, half), lambda j, te, nu: (j, 0)),
        scratch_shapes=[pltpu.VMEM((D, 2 * EXPERT_FF), BF16), pltpu.VMEM((EXPERT_FF, D), BF16)],
    )
    return pl.pallas_call(
        _moe_gemm_kernel,
        grid_spec=grid_spec,
        out_shape=jax.ShapeDtypeStruct((P, half), jnp.int32),
        compiler_params=_cparams(1),
        name="moe_gemm",
    )(tile_expert, n_used, xs, exp_w_in, exp_w_out)


def _moe_combine_kernel(y_ref, w_ref, h_ref, si_ref, so_ref, x_ref, g2_ref, o_ref, si_b, so_b):
    @pl.when(pl.program_id(0) == 0)
    def _():
        si_b[...] = si_ref[...].astype(BF16)
        so_b[...] = so_ref[...].astype(BF16)

    bb, tt, D = x_ref.shape
    half = D // 2
    w = w_ref[...]
    acc_lo = jnp.zeros((bb * tt, half), F32)
    acc_hi = jnp.zeros((bb * tt, half), F32)
    for k in range(TOP_K):
        lo, hi = _unpack_pairs(y_ref[k], F32)
        acc_lo = acc_lo + w[:, k:k + 1] * lo
        acc_hi = acc_hi + w[:, k:k + 1] * hi
    hlo, hhi = _unpack_pairs(h_ref[...])
    hu = _dot(hlo, si_b[:half, :]) + _dot(hhi, si_b[half:, :])
    act = (_silu(hu[:, :SHARED_FF]) * hu[:, SHARED_FF:]).astype(BF16)
    y = jnp.concatenate([acc_lo, acc_hi], axis=-1) + _dot(act, so_b[...])
    o_ref[...] = x_ref[...] + g2_ref[...] * y.reshape(bb, tt, D)


def moe_combine(y8, w_t, hp, sh_w_in, sh_w_out, x, mod, gate_idx, l, row0, rows=256):
    B, T, D = x.shape
    half = D // 2
    bb, tt, nblk, ij = _row_blocks(B, T, rows)
    M = bb * tt
    assert row0 % M == 0
    off = row0 // M
    xspec = pl.BlockSpec((bb, tt, D), lambda i: ij(i) + (0,))
    return pl.pallas_call(
        _moe_combine_kernel,
        grid=(nblk,),
        in_specs=[pl.BlockSpec((TOP_K, M, half), lambda i: (0, off + i, 0)),
                  pl.BlockSpec((M, TOP_K), lambda i: (off + i, 0)),
                  pl.BlockSpec((M, half), lambda i: (off + i, 0)),
                  pl.BlockSpec((None, D, 2 * SHARED_FF), lambda i: (l, 0, 0)),
                  pl.BlockSpec((None, SHARED_FF, D), lambda i: (l, 0, 0)),
                  xspec,
                  pl.BlockSpec((bb, 1, D), lambda i: (ij(i)[0], 0, gate_idx))],
        out_specs=xspec,
        out_shape=jax.ShapeDtypeStruct((B, T, D), F32),
        scratch_shapes=[pltpu.VMEM((D, 2 * SHARED_FF), BF16), pltpu.VMEM((SHARED_FF, D), BF16)],
        compiler_params=_cparams(1),
        name="moe_combine",
    )(y8, w_t, hp, sh_w_in, sh_w_out, x, mod)


def _shared_kv_kernel(x_ref, g_ref, w_ref, lg_ref, cos_ref, sin_ref, lat_ref, kr_ref):
    bb, tt, D = x_ref.shape
    xn = _rms(x_ref[...], g_ref[...]).reshape(bb * tt, D).astype(BF16)
    z = _dot(xn, w_ref[...].astype(BF16))
    lat = _rms(z[:, :KV_LORA], lg_ref[...])
    lat_ref[...] = lat.reshape(bb, tt, KV_LORA)
    zr = z[:, KV_LORA:KV_LORA + MLA_ROPE].reshape(bb, tt, MLA_ROPE)
    zq = z[:, KV_LORA + 128:KV_LORA + 128 + MLA_ROPE].reshape(bb, tt, MLA_ROPE)
    kr_ref[...] = zr * cos_ref[...] + zq * sin_ref[...]


def shared_kv(x, kv_in_g, w_kv, kv_lat_g, cos32, sin32, rows=512):
    B, T, D = x.shape
    bb, tt, nblk, ij = _row_blocks(B, T, rows)
    tspec = pl.BlockSpec((tt, MLA_ROPE), lambda i: (ij(i)[1], 0))
    return pl.pallas_call(
        _shared_kv_kernel,
        grid=(nblk,),
        in_specs=[pl.BlockSpec((bb, tt, D), lambda i: ij(i) + (0,)),
                  pl.BlockSpec((1, D), lambda i: (0, 0)),
                  pl.BlockSpec(w_kv.shape, lambda i: (0, 0)),
                  pl.BlockSpec((1, KV_LORA), lambda i: (0, 0)),
                  tspec, tspec],
        out_specs=[pl.BlockSpec((bb, tt, KV_LORA), lambda i: ij(i) + (0,)),
                   pl.BlockSpec((bb, tt, MLA_ROPE), lambda i: ij(i) + (0,))],
        out_shape=[jax.ShapeDtypeStruct((B, T, KV_LORA), F32),
                   jax.ShapeDtypeStruct((B, T, MLA_ROPE), F32)],
        compiler_params=_cparams(1),
        name="shared_kv",
    )(x, kv_in_g.reshape(1, D), w_kv, kv_lat_g.reshape(1, KV_LORA), cos32, sin32)


def _kv_expand_kernel(lat_ref, kr_ref, wk_ref, ek_ref, wvt_ref, ones_ref, k_ref, vt_ref):
    lat = lat_ref[0].astype(BF16)
    kr = kr_ref[0].astype(BF16)
    k = _dot(lat, wk_ref[...].astype(BF16)) + _dot(kr, ek_ref[...].astype(BF16))
    k_ref[0] = k.astype(k_ref.dtype)
    vt = _dot_nt(wvt_ref[...].astype(BF16), lat) + ones_ref[...]
    vt_ref[0] = vt.astype(vt_ref.dtype)


def kv_expand(lat, kr, wk_pad, ek, wvt_ext, ones_col, rows=512):
    B, T, _ = lat.shape
    tt = rows
    NK, NVT = wk_pad.shape[1], wvt_ext.shape[0]

    def full(a):
        return pl.BlockSpec(a.shape, lambda b, t: (0, 0))

    def rowspec(n):
        return pl.BlockSpec((1, tt, n), lambda b, t: (b, t, 0))

    return pl.pallas_call(
        _kv_expand_kernel,
        grid=(B, T // tt),
        in_specs=[rowspec(KV_LORA), rowspec(MLA_ROPE), full(wk_pad), full(ek), full(wvt_ext), full(ones_col)],
        out_specs=[rowspec(NK), pl.BlockSpec((1, NVT, tt), lambda b, t: (b, 0, t))],
        out_shape=[jax.ShapeDtypeStruct((B, T, NK), BF16), jax.ShapeDtypeStruct((B, NVT, T), BF16)],
        compiler_params=_cparams(2),
        name="kv_expand",
    )(lat, kr, wk_pad, ek, wvt_ext, ones_col)


def _query_kernel(h_ref, wdq_ref, qg_ref, wq_ref, wqr_ref, c_ref, s_ref, q_ref, wdq_b, wq_b, wqr_b):
    @pl.when(pl.program_id(0) == 0)
    def _():
        wdq_b[...] = wdq_ref[...].astype(BF16)
        wq_b[...] = wq_ref[...].astype(BF16)
        wqr_b[...] = wqr_ref[...].astype(BF16)

    bb, tt, D = h_ref.shape
    h = h_ref[...].reshape(bb * tt, D)
    cq = _rms(_dot(h, wdq_b[...]), qg_ref[...]).astype(BF16)
    q1 = _dot(cq, wq_b[...]).reshape(bb, tt, -1)
    q2 = _dot(cq, wqr_b[...]).reshape(bb, tt, -1)
    c = c_ref[...]
    s = s_ref[...]
    for hd in range(MLA_HEADS):
        sl = slice(hd * HEAD_PAD, (hd + 1) * HEAD_PAD)
        q_ref[:, :, sl] = (q1[:, :, sl] * c + q2[:, :, sl] * s).astype(q_ref.dtype)


def mla_queries(h, w_dq, q_norm_g, wq_pad, wq_rot, l, c128, s128, rows=512):
    B, T, D = h.shape
    bb, tt, nblk, ij = _row_blocks(B, T, rows)
    NQ = wq_pad.shape[-1]
    tspec = pl.BlockSpec((tt, HEAD_PAD), lambda i: (ij(i)[1], 0))
    return pl.pallas_call(
        _query_kernel,
        grid=(nblk,),
        in_specs=[pl.BlockSpec((bb, tt, D), lambda i: ij(i) + (0,)),
                  pl.BlockSpec((None, D, Q_LORA), lambda i: (l, 0, 0)),
                  pl.BlockSpec((None, 1, Q_LORA), lambda i: (l, 0, 0)),
                  pl.BlockSpec((None, Q_LORA, NQ), lambda i: (l, 0, 0)),
                  pl.BlockSpec((None, Q_LORA, NQ), lambda i: (l, 0, 0)),
                  tspec, tspec],
        out_specs=pl.BlockSpec((bb, tt, NQ), lambda i: ij(i) + (0,)),
        out_shape=jax.ShapeDtypeStruct((B, T, NQ), BF16),
        scratch_shapes=[pltpu.VMEM((D, Q_LORA), BF16), pltpu.VMEM((Q_LORA, NQ), BF16),
                        pltpu.VMEM((Q_LORA, NQ), BF16)],
        compiler_params=_cparams(1),
        name="mla_queries",
    )(h, w_dq, q_norm_g.reshape(-1, 1, Q_LORA), wq_pad, wq_rot, c128, s128)


def _attn_prompt_kernel(qi_tab, ki_tab, q_ref, k_ref, vt_ref, o_ref, *scratch, tq, tk):
    H = MLA_HEADS
    m_refs, l_refs, acc_refs = scratch[:H], scratch[H:2 * H], scratch[2 * H:]
    p_id = pl.program_id(1)
    qi = qi_tab[p_id]
    ki = ki_tab[p_id]

    @pl.when(ki == 0)
    def _():
        for hd in range(H):
            m_refs[hd][...] = jnp.full(m_refs[hd].shape, NEG_INF, F32)
            l_refs[hd][...] = jnp.zeros(l_refs[hd].shape, F32)
            acc_refs[hd][...] = jnp.zeros(acc_refs[hd].shape, F32)

    def block(masked):
        if masked:
            kchunk = (ki * tk + lax.broadcasted_iota(jnp.int32, (tk, tq), 0)) // CHUNK
            qchunk = (qi * tq + lax.broadcasted_iota(jnp.int32, (tk, tq), 1)) // CHUNK
            mask = kchunk <= qchunk
        def scores(hd):
            sl = slice(hd * HEAD_PAD, (hd + 1) * HEAD_PAD)
            return _dot_nt(k_ref[0, :, sl], q_ref[0, :, sl])

        pending = [scores(hd) for hd in range(ATTN_LOOKAHEAD)]
        for hd in range(H):
            if hd + ATTN_LOOKAHEAD < H:
                pending.append(scores(hd + ATTN_LOOKAHEAD))
            s = pending.pop(0)
            if masked:
                s = jnp.where(mask, s, NEG_INF)
            m_prev = m_refs[hd][...]
            m_new = jnp.maximum(m_prev, jnp.max(s, axis=0, keepdims=True))
            a = jnp.exp2(m_prev - m_new)
            p = jnp.exp2(s - m_new).astype(BF16)
            pv = _dot(vt_ref[0, hd * V_ROWS:(hd + 1) * V_ROWS, :], p)
            acc_refs[hd][...] = a * acc_refs[hd][...] + pv[:MLA_V]
            l_refs[hd][...] = a * l_refs[hd][...] + pv[MLA_V:MLA_V + 1]
            m_refs[hd][...] = m_new

    @pl.when(ki < qi)
    def _():
        block(False)

    @pl.when(ki == qi)
    def _():
        block(True)
        o_t = jnp.concatenate([acc_refs[hd][...] / l_refs[hd][...] for hd in range(H)], axis=0)
        o_ref[0] = o_t.T.astype(o_ref.dtype)


def attn_prompt(q, k, vt, tq=256):
    B, T, NQ = q.shape
    NVT = vt.shape[1]
    NV = MLA_HEADS * MLA_V
    tk = tq
    assert tq % CHUNK == 0
    nq = T // tq
    pairs = [(a, b) for a in range(nq) for b in range(a + 1)]
    qi_tab = jnp.asarray([a for a, _ in pairs], jnp.int32)
    ki_tab = jnp.asarray([b for _, b in pairs], jnp.int32)
    grid_spec = pltpu.PrefetchScalarGridSpec(
        num_scalar_prefetch=2,
        grid=(B, len(pairs)),
        in_specs=[pl.BlockSpec((1, tq, NQ), lambda b, p, qt, kt: (b, qt[p], 0)),
                  pl.BlockSpec((1, tk, NQ), lambda b, p, qt, kt: (b, kt[p], 0)),
                  pl.BlockSpec((1, NVT, tk), lambda b, p, qt, kt: (b, 0, kt[p]))],
        out_specs=pl.BlockSpec((1, tq, NV), lambda b, p, qt, kt: (b, qt[p], 0)),
        scratch_shapes=([pltpu.VMEM((1, tq), F32)] * (2 * MLA_HEADS)
                        + [pltpu.VMEM((MLA_V, tq), F32)] * MLA_HEADS),
    )
    return pl.pallas_call(
        functools.partial(_attn_prompt_kernel, tq=tq, tk=tk),
        grid_spec=grid_spec,
        out_shape=jax.ShapeDtypeStruct((B, T, NV), BF16),
        compiler_params=_cparams(2),
        name="attn_prompt",
    )(qi_tab, ki_tab, q, k, vt)


def _absorb_kernel(q_ref, m_ref, o_ref):
    o_ref[...] = _dot(q_ref[...], m_ref[...].astype(BF16)).astype(o_ref.dtype)


def absorb_queries(q2d, m_abs):
    N = q2d.shape[0]
    H, _, W = m_abs.shape
    return pl.pallas_call(
        _absorb_kernel,
        grid=(H,),
        in_specs=[pl.BlockSpec((N, HEAD_PAD), lambda h: (0, h)),
                  pl.BlockSpec((None, HEAD_PAD, W), lambda h: (h, 0, 0))],
        out_specs=pl.BlockSpec((None, N, W), lambda h: (h, 0, 0)),
        out_shape=jax.ShapeDtypeStruct((H, N, W), BF16),
        compiler_params=_cparams(1),
        name="absorb_queries",
    )(q2d, m_abs)


def _attn_sample_kernel(q_ref, lat_ref, kr_ref, nlat_ref, nkr_ref, o_ref, m_ref, l_ref, acc_ref):
    kb = pl.program_id(1)
    H, Q, W = q_ref.shape
    q = q_ref[...].reshape(H * Q, W)
    q_lat = q[:, :KV_LORA]
    q_rope = q[:, KV_LORA:KV_LORA + MLA_ROPE]

    def update(lat, kr):
        lat = lat.astype(BF16)
        s = _dot_nt(q_lat, lat) + _dot_nt(q_rope, kr.astype(BF16))
        m_prev = m_ref[...]
        m_new = jnp.maximum(m_prev, jnp.max(s, axis=-1, keepdims=True))
        a = jnp.exp2(m_prev - m_new)
        p = jnp.exp2(s - m_new[:, :1])
        l_ref[...] = a * l_ref[...] + jnp.sum(p, axis=-1, keepdims=True)
        m_ref[...] = m_new
        acc_ref[...] = jnp.concatenate([a, a], axis=-1) * acc_ref[...] + _dot(p.astype(BF16), lat)

    @pl.when(kb == 0)
    def _():
        m_ref[...] = jnp.full_like(m_ref, NEG_INF)
        l_ref[...] = jnp.zeros_like(l_ref)
        acc_ref[...] = jnp.zeros_like(acc_ref)
        update(nlat_ref[0], nkr_ref[0])

    update(lat_ref[0], kr_ref[0])

    @pl.when(kb == pl.num_programs(1) - 1)
    def _():
        lsum = l_ref[...]
        o = acc_ref[...] / jnp.concatenate([lsum, lsum], axis=-1)
        o_ref[...] = o.reshape(H, Q, KV_LORA).astype(o_ref.dtype)


def attn_sample(q_abs, cache_lat, cache_kr, new_lat, new_kr, tk=1024):
    H, N, W = q_abs.shape
    B, P, _ = cache_lat.shape
    Q = new_lat.shape[1]
    qpos = P + np.arange(Q)
    kpos = np.arange(P + Q)
    assert bool(np.all((kpos // CHUNK)[None, :] <= (qpos // CHUNK)[:, None]))
    return pl.pallas_call(
        _attn_sample_kernel,
        grid=(B, P // tk),
        in_specs=[pl.BlockSpec((H, Q, W), lambda b, kb: (0, b, 0)),
                  pl.BlockSpec((1, tk, KV_LORA), lambda b, kb: (b, kb, 0)),
                  pl.BlockSpec((1, tk, MLA_ROPE), lambda b, kb: (b, kb, 0)),
                  pl.BlockSpec((1, Q, KV_LORA), lambda b, kb: (b, 0, 0)),
                  pl.BlockSpec((1, Q, MLA_ROPE), lambda b, kb: (b, 0, 0))],
        out_specs=pl.BlockSpec((H, Q, KV_LORA), lambda b, kb: (0, b, 0)),
        out_shape=jax.ShapeDtypeStruct((H, N, KV_LORA), BF16),
        scratch_shapes=[pltpu.VMEM((H * Q, 128), F32), pltpu.VMEM((H * Q, 128), F32),
                        pltpu.VMEM((H * Q, KV_LORA), F32)],
        compiler_params=_cparams(2),
        name="attn_sample",
    )(q_abs, cache_lat, cache_kr, new_lat, new_kr)


def _unabsorb_kernel(o_ref, w_ref, out_ref):
    out_ref[...] = (_dot(o_ref[0], w_ref[0].astype(BF16))
                    + _dot(o_ref[1], w_ref[1].astype(BF16))).astype(out_ref.dtype)


def unabsorb(o_lat, wuv_pad):
    H, N, R = o_lat.shape
    return pl.pallas_call(
        _unabsorb_kernel,
        grid=(H // 2,),
        in_specs=[pl.BlockSpec((2, N, R), lambda p: (p, 0, 0)),
                  pl.BlockSpec((2, R, 128), lambda p: (p, 0, 0))],
        out_specs=pl.BlockSpec((N, 128), lambda p: (0, p)),
        out_shape=jax.ShapeDtypeStruct((N, (H // 2) * 128), BF16),
        compiler_params=_cparams(1),
        name="unabsorb",
    )(o_lat, wuv_pad)


def _rope_tables(pos):
    half = MLA_ROPE // 2
    inv = 1.0 / (ROPE_THETA ** (np.arange(half, dtype=np.float64) * 2.0 / MLA_ROPE))
    ang = np.asarray(pos, np.float64)[:, None] * inv[None, :]
    cos = np.concatenate([np.cos(ang), np.cos(ang)], axis=-1)
    sin = np.concatenate([np.sin(ang), np.sin(ang)], axis=-1)
    T = cos.shape[0]
    c128 = np.zeros((T, HEAD_PAD)); s128 = np.zeros((T, HEAD_PAD))
    c128[:, :MLA_NOPE] = 1.0
    c128[:, MLA_NOPE:MLA_NOPE + MLA_ROPE] = cos
    s128[:, MLA_NOPE:MLA_NOPE + MLA_ROPE] = sin
    return (jnp.asarray(cos, F32), jnp.asarray(sin, F32),
            jnp.asarray(c128 * Q_PRESCALE, F32), jnp.asarray(s128 * Q_PRESCALE, F32))


def _rot_half_cols(w):
    half = w.shape[-1] // 2
    return jnp.concatenate([-w[..., half:], w[..., :half]], axis=-1)


def _prep_weights(w_dkv, w_uk, w_uv, w_uq, router_w):
    D = D_MODEL
    w_lat, w_rope = w_dkv[:, :KV_LORA], w_dkv[:, KV_LORA:]
    pad96 = jnp.zeros((D, 128 - MLA_ROPE), F32)
    w_kv = jnp.concatenate([w_lat, w_rope, pad96, _rot_half_cols(w_rope), pad96], axis=-1)

    zpad = HEAD_PAD - MLA_NOPE
    wk_pad = jnp.pad(w_uk, ((0, 0), (0, 0), (0, zpad))).reshape(KV_LORA, MLA_HEADS * HEAD_PAD)
    ek = jnp.zeros((MLA_ROPE, MLA_HEADS, HEAD_PAD), F32)
    ek = ek.at[:, :, MLA_NOPE:MLA_NOPE + MLA_ROPE].set(
        jnp.broadcast_to(jnp.eye(MLA_ROPE, dtype=F32)[:, None, :], (MLA_ROPE, MLA_HEADS, MLA_ROPE)))
    ek = ek.reshape(MLA_ROPE, MLA_HEADS * HEAD_PAD)
    wvt = jnp.transpose(w_uv, (1, 2, 0))
    wvt_ext = jnp.pad(wvt, ((0, 0), (0, V_ROWS - MLA_V), (0, 0))).reshape(MLA_HEADS * V_ROWS, KV_LORA)
    ones_col = jnp.tile((jnp.arange(V_ROWS) >= MLA_V).astype(F32), MLA_HEADS).reshape(-1, 1)

    nb = w_uq.shape[0]
    qn, qr = w_uq[..., :MLA_NOPE], w_uq[..., MLA_NOPE:]
    z32 = jnp.zeros(qr.shape[:-1] + (HEAD_PAD - MLA_NOPE - MLA_ROPE,), F32)
    wq_pad = jnp.concatenate([qn, qr, z32], axis=-1).reshape(nb, Q_LORA, MLA_HEADS * HEAD_PAD)
    wq_rot = jnp.concatenate([jnp.zeros_like(qn), _rot_half_cols(qr), z32], axis=-1)
    wq_rot = wq_rot.reshape(nb, Q_LORA, MLA_HEADS * HEAD_PAD)

    m_abs = jnp.zeros((MLA_HEADS, HEAD_PAD, KV_LORA + 128), F32)
    m_abs = m_abs.at[:, :MLA_NOPE, :KV_LORA].set(jnp.transpose(w_uk, (1, 2, 0)))
    m_abs = m_abs.at[:, MLA_NOPE:MLA_NOPE + MLA_ROPE, KV_LORA:KV_LORA + MLA_ROPE].set(
        jnp.broadcast_to(jnp.eye(MLA_ROPE, dtype=F32), (MLA_HEADS, MLA_ROPE, MLA_ROPE)))

    wuv_h = jnp.transpose(w_uv, (1, 0, 2))
    even = jnp.pad(wuv_h, ((0, 0), (0, 0), (0, 64)))
    odd = jnp.pad(wuv_h, ((0, 0), (0, 0), (64, 0)))
    wuv_pad = jnp.where((jnp.arange(MLA_HEADS) % 2 == 0)[:, None, None], even, odd)

    rw_t = jnp.transpose(router_w, (0, 2, 1))
    return dict(w_kv=w_kv, wk_pad=wk_pad, ek=ek, wvt_ext=wvt_ext, ones_col=ones_col, wq_pad=wq_pad, wq_rot=wq_rot,
                m_abs=m_abs, wuv_pad=wuv_pad, rw_t=rw_t)


def _mixer(st, l, P, W):
    x, m = st["x"], st["mod"][l]
    B, T, _ = x.shape
    n_a = P["hg_w_in"].shape[0]
    h = norm_mod(x, P["norm1_g"][l], m, sc_idx=1, sh_idx=0)
    if l < n_a:
        z = linear(h, P["hg_w_in"], l, F32)
        s0 = None if st["hg_state"] is None else st["hg_state"][l]
        o, s_new = gla(z, st["lbs"][l], P["hg_onorm_g"][l], s0)
        st["hg_new"].append(s_new)
        st["x"] = linear(o, P["hg_w_out"], l, F32, x=x, mod=m, gate_idx=2)
    else:
        bi = l - n_a
        q = mla_queries(h, P["w_dq"], P["q_norm_g"], W["wq_pad"], W["wq_rot"], bi, st["c128"], st["s128"])
        if st["past_lat"] is None:
            o = attn_prompt(q, st["k_all"], st["v_all"])
        else:
            q_abs = absorb_queries(q.reshape(B * T, -1), W["m_abs"])
            o_lat = attn_sample(q_abs, st["past_lat"], st["past_kr"], st["lat"], st["kr"])
            o = unabsorb(o_lat, W["wuv_pad"]).reshape(B, T, -1)
        st["x"] = linear(o, P["w_o"], bi, F32, x=x, mod=m, gate_idx=2)


def _moe(groups, l, P, W):
    hp = jnp.concatenate([norm_mod(st["x"], P["norm2_g"][l], st["mod"][l], sc_idx=4, sh_idx=3, packed=True)
                          for st in groups], axis=0)
    n_tok = hp.shape[0]
    n_tiles = (TOP_K * n_tok) // MOE_TILE + N_EXPERTS
    pos, w8, tile_expert, n_used = route(hp, W["rw_t"], P["router_bias"], l, MOE_TILE, n_tiles)
    pos_flat = pos.reshape(-1)
    src = sc_invert(pos_flat, n_tok, n_tiles * MOE_TILE)
    xs = sc_gather(hp, src)
    out = moe_gemm(xs, tile_expert.reshape(-1), n_used[0, :1], P["exp_w_in"], P["exp_w_out"], l,
                   MOE_TILE, n_tiles)
    y8 = sc_gather(out, pos_flat).reshape(TOP_K, n_tok, -1)
    w_t = w8.T
    row0 = 0
    for st in groups:
        B, T, _ = st["x"].shape
        st["x"] = moe_combine(y8, w_t, hp, P["sh_w_in"], P["sh_w_out"], st["x"], st["mod"][l], 5, l, row0)
        row0 += B * T


def _group_state(x, mod, pos, hg_state, past_lat, past_kr, lbs):
    cos32, sin32, c128, s128 = _rope_tables(pos)
    return dict(x=x, mod=mod, hg_state=hg_state, past_lat=past_lat, past_kr=past_kr, lbs=lbs,
                cos32=cos32, sin32=sin32, c128=c128, s128=s128, hg_new=[],
                lat=None, kr=None, k_all=None, v_all=None)


def kernel(x_prompt, x_sample, state_hgrn, cache_mla_latent, cache_mla_krope, c_prompt, c_sample, ada_w, ada_b, norm1_g, norm2_g, hg_w_in, hg_lb_logits, hg_onorm_g, hg_w_out, kv_in_g, w_dkv, kv_lat_g, w_uk, w_uv, w_dq, q_norm_g, w_uq, w_o, router_w, router_bias, exp_w_in, exp_w_out, sh_w_in, sh_w_out, final_g):
    Bp, Sp, _ = x_prompt.shape
    Bs, Ss, _ = x_sample.shape
    past = cache_mla_latent.shape[1]
    P = dict(norm1_g=norm1_g, norm2_g=norm2_g, hg_w_in=hg_w_in, hg_lb_logits=hg_lb_logits,
             hg_onorm_g=hg_onorm_g, hg_w_out=hg_w_out, kv_in_g=kv_in_g, kv_lat_g=kv_lat_g,
             w_dq=w_dq, q_norm_g=q_norm_g, w_o=w_o, router_bias=router_bias,
             exp_w_in=exp_w_in, exp_w_out=exp_w_out, sh_w_in=sh_w_in, sh_w_out=sh_w_out, final_g=final_g)
    W = _prep_weights(w_dkv, w_uk, w_uv, w_uq, router_w)
    mod = ada_mod(jnp.concatenate([c_prompt, c_sample], axis=0), ada_w, ada_b)
    lbs = jnp.cumsum(jax.nn.softmax(hg_lb_logits.astype(F32), axis=0), axis=0)
    gp = _group_state(x_prompt, mod[:, :Bp, None, :], np.arange(Sp), None, None, None, lbs)
    gs = _group_state(x_sample, mod[:, Bp:, None, :], past + np.arange(Ss), state_hgrn,
                      cache_mla_latent, cache_mla_krope, lbs)
    groups = [gp, gs]
    n_a = hg_w_in.shape[0]
    for l in range(norm1_g.shape[0]):
        for st in groups:
            _mixer(st, l, P, W)
        _moe(groups, l, P, W)
        if l == n_a - 1:
            for st in groups:
                st["lat"], st["kr"] = shared_kv(st["x"], kv_in_g, W["w_kv"], kv_lat_g, st["cos32"], st["sin32"])
            gp["k_all"], gp["v_all"] = kv_expand(gp["lat"], gp["kr"], W["wk_pad"], W["ek"], W["wvt_ext"],
                                                 W["ones_col"])
    outs = []
    for st in groups:
        outs.append((norm_mod(st["x"], final_g, out_dtype=F32), jnp.stack(st["hg_new"], axis=0)))
    (y_p, st_p), (y_s, st_s) = outs
    return (y_p, y_s, st_p, st_s, gp["lat"], gp["kr"], gs["lat"], gs["kr"])
```

```python
import dataclasses
import functools

import numpy as np
import jax
import jax.numpy as jnp
from jax import lax
from jax.experimental import pallas as pl
from jax.experimental.pallas import tpu as pltpu
from jax.experimental.pallas import tpu_sc as plsc

F32 = jnp.float32
BF16 = jnp.bfloat16

D_MODEL = 1024
CHUNK = 64
HG_HEADS = 8
HG_DK = 128
HG_DV = 128
MLA_HEADS = 16
MLA_NOPE = 64
MLA_ROPE = 32
MLA_V = 64
Q_LORA = 384
KV_LORA = 256
ROPE_THETA = 10000.0
N_EXPERTS = 64
TOP_K = 8
N_GROUPS = 8
TOPK_GROUPS = 4
EXPERT_FF = 256
SHARED_FF = 256
ROUTED_SCALE = 2.5
EPS = 1e-6

HEAD_PAD = 128
ATTN_LOOKAHEAD = 4
V_ROWS = MLA_V + 16
QK_SCALE = (MLA_NOPE + MLA_ROPE) ** -0.5
Q_PRESCALE = QK_SCALE * float(np.log2(np.e))
VMEM_LIMIT = 56 * 1024 * 1024
NEG_INF = float("-inf")
SC_CORES = 2
SC_SUBCORES = 16
SC_WORKERS = SC_CORES * SC_SUBCORES
SC_LANES = 16
SC_WINDOW = 64
MOE_TILE = 512
PROMPT_STREAMS = 2
MOE_SUB = 2


def _cparams(n_axes):
    return pltpu.CompilerParams(dimension_semantics=("arbitrary",) * n_axes,
                                vmem_limit_bytes=VMEM_LIMIT)


def _silu(x):
    return x * jax.nn.sigmoid(x)


def _rms(x, g):
    ms = jnp.mean(x * x, axis=-1, keepdims=True)
    return x * lax.rsqrt(ms + EPS) * g


def _dot(a, b):
    return jnp.dot(a, b, preferred_element_type=F32)


def _dot_nt(a, b):
    return lax.dot_general(a, b, (((1,), (1,)), ((), ())), preferred_element_type=F32)


def _dot_tn(a, b):
    return lax.dot_general(a, b, (((0,), (0,)), ((), ())), preferred_element_type=F32)


def _row_blocks(B, T, rows):
    if T >= rows:
        assert T % rows == 0
        bb, tt = 1, rows
    else:
        assert rows % T == 0 and B % (rows // T) == 0
        bb, tt = rows // T, T
    nt = T // tt
    return bb, tt, (B // bb) * nt, (lambda i: (i // nt, i % nt))


def _ada_kernel(c_ref, w_ref, b_ref, o_ref):
    a = _silu(c_ref[...]).astype(BF16)
    o_ref[...] = _dot(a, w_ref[...].astype(BF16)) + b_ref[...]


def ada_mod(c, ada_w, ada_b):
    R, D = c.shape
    L, _, N = ada_w.shape
    tn = 1536
    return pl.pallas_call(
        _ada_kernel,
        grid=(L, N // tn),
        in_specs=[pl.BlockSpec((R, D), lambda l, j: (0, 0)),
                  pl.BlockSpec((None, D, tn), lambda l, j: (l, 0, j)),
                  pl.BlockSpec((None, 1, tn), lambda l, j: (l, 0, j))],
        out_specs=pl.BlockSpec((None, R, tn), lambda l, j: (l, 0, j)),
        out_shape=jax.ShapeDtypeStruct((L, R, N), F32),
        compiler_params=_cparams(2),
        name="ada_mod",
    )(c, ada_w, ada_b.reshape(L, 1, N))


def _pack_pairs(y):
    half = y.shape[-1] // 2
    bits = lax.bitcast_convert_type(y.astype(BF16).astype(F32), jnp.uint32)
    word = lax.shift_right_logical(bits[:, :half], jnp.uint32(16)) | bits[:, half:]
    return lax.bitcast_convert_type(word, jnp.int32)


def _unpack_pairs(word, dtype=BF16):
    u = lax.bitcast_convert_type(word, jnp.uint32)
    lo = lax.bitcast_convert_type(lax.shift_left(u, jnp.uint32(16)), F32)
    hi = lax.bitcast_convert_type(u & jnp.uint32(0xFFFF0000), F32)
    return lo.astype(dtype), hi.astype(dtype)


def _norm_kernel(*refs, modulated, packed):
    if modulated:
        x_ref, g_ref, sc_ref, sh_ref, o_ref = refs
    else:
        x_ref, g_ref, o_ref = refs
    y = _rms(x_ref[...], g_ref[...])
    if modulated:
        y = y * (1.0 + sc_ref[...]) + sh_ref[...]
    if packed:
        bb, tt, D = y.shape
        o_ref[...] = _pack_pairs(y.reshape(bb * tt, D))
    else:
        o_ref[...] = y.astype(o_ref.dtype)


def norm_mod(x, g, mod=None, sc_idx=0, sh_idx=0, out_dtype=BF16, rows=512, packed=False):
    B, T, D = x.shape
    bb, tt, nblk, ij = _row_blocks(B, T, rows)
    xspec = pl.BlockSpec((bb, tt, D), lambda i: ij(i) + (0,))
    in_specs = [xspec, pl.BlockSpec((1, D), lambda i: (0, 0))]
    args = [x, g.reshape(1, D)]
    if mod is not None:
        in_specs += [pl.BlockSpec((bb, 1, D), lambda i: (ij(i)[0], 0, sc_idx)),
                     pl.BlockSpec((bb, 1, D), lambda i: (ij(i)[0], 0, sh_idx))]
        args += [mod, mod]
    if packed:
        out_specs = pl.BlockSpec((bb * tt, D // 2), lambda i: (i, 0))
        out_shape = jax.ShapeDtypeStruct((B * T, D // 2), jnp.int32)
    else:
        out_specs = xspec
        out_shape = jax.ShapeDtypeStruct((B, T, D), out_dtype)
    return pl.pallas_call(
        functools.partial(_norm_kernel, modulated=mod is not None, packed=packed),
        grid=(nblk,),
        in_specs=in_specs,
        out_specs=out_specs,
        out_shape=out_shape,
        compiler_params=_cparams(1),
        name="norm_mod",
    )(*args)


def _linear_kernel(*refs, residual):
    if residual:
        a_ref, w_ref, x_ref, gate_ref, o_ref, wb_ref = refs
    else:
        a_ref, w_ref, o_ref, wb_ref = refs

    @pl.when(pl.program_id(1) == 0)
    def _():
        wb_ref[...] = w_ref[...].astype(BF16)

    bb, tt, K = a_ref.shape
    y = _dot(a_ref[...].reshape(bb * tt, K).astype(BF16), wb_ref[...])
    y = y.reshape(bb, tt, y.shape[-1])
    if residual:
        y = x_ref[...] + gate_ref[...] * y
    o_ref[...] = y.astype(o_ref.dtype)


def linear(a, w, l, out_dtype, x=None, mod=None, gate_idx=0, rows=512, tn=1024):
    B, T, K = a.shape
    _, _, N = w.shape
    tn = min(tn, N)
    bb, tt, nblk, ij = _row_blocks(B, T, rows)
    in_specs = [pl.BlockSpec((bb, tt, K), lambda j, i: ij(i) + (0,)),
                pl.BlockSpec((None, K, tn), lambda j, i: (l, 0, j))]
    args = [a, w]
    ospec = pl.BlockSpec((bb, tt, tn), lambda j, i: ij(i) + (j,))
    if x is not None:
        gsteps = D_MODEL // tn
        in_specs += [ospec, pl.BlockSpec((bb, 1, tn), lambda j, i: (ij(i)[0], 0, gate_idx * gsteps + j))]
        args += [x, mod]
    return pl.pallas_call(
        functools.partial(_linear_kernel, residual=x is not None),
        grid=(N // tn, nblk),
        in_specs=in_specs,
        out_specs=ospec,
        out_shape=jax.ShapeDtypeStruct((B, T, N), out_dtype),
        scratch_shapes=[pltpu.VMEM((K, tn), BF16)],
        compiler_params=_cparams(2),
        name="linear",
    )(*args)


def _gla_kernel(*refs, L, n_chunks, has_init):
    if has_init:
        q_ref, f_ref, i_ref, g_ref, lb_ref, on_ref, s0_ref, o_ref, so_ref, st_ref = refs
    else:
        q_ref, f_ref, i_ref, g_ref, lb_ref, on_ref, o_ref, so_ref, st_ref = refs
    t = pl.program_id(1)
    H = st_ref.shape[0]

    @pl.when(t == 0)
    def _():
        for h in range(H):
            if has_init:
                st_ref[h] = s0_ref[0, h].T
            else:
                st_ref[h] = jnp.zeros(st_ref.shape[1:], F32)

    lb = lb_ref[...]
    onorm = on_ref[...]
    row = lax.broadcasted_iota(jnp.int32, (L, L), 0)
    col = lax.broadcasted_iota(jnp.int32, (L, L), 1)
    causal = col <= row
    tri = causal.astype(BF16)
    mid = L // 2 - 1

    def chunk(c, carry):
        r0 = pl.multiple_of(c * L, L)
        q = _silu(q_ref[0, pl.ds(r0, L), :])
        fg = lb + (1.0 - lb) * jax.nn.sigmoid(f_ref[0, pl.ds(r0, L), :])
        k = 1.0 - fg
        v = i_ref[0, pl.ds(r0, L), :].astype(BF16)
        gate = _silu(g_ref[0, pl.ds(r0, L), :])
        logf = jnp.log(fg)
        hi = logf.astype(BF16)
        lo = (logf - hi.astype(F32)).astype(BF16)
        b = _dot(tri, hi) + _dot(tri, lo)
        b_mid = b[mid:mid + 1, :]
        b_last = b[L - 1:L, :]
        up = jnp.exp(b - b_mid)
        down = jnp.exp(b_mid - b)
        qa = q * up
        kb = k * down
        qe = (qa * jnp.exp(b_mid)).astype(BF16)
        kd = (kb * jnp.exp(b_last - b_mid)).astype(BF16)
        qa = qa.astype(BF16)
        kb = kb.astype(BF16)
        decay = jnp.exp(b_last)
        sls = [slice(h * HG_DK, (h + 1) * HG_DK) for h in range(H)]
        sts = [st_ref[h] for h in range(H)]
        scores = [_dot_nt(qa[:, sl], kb[:, sl]) for sl in sls]
        inter = [_dot_nt(qe[:, sl], st.astype(BF16)) for sl, st in zip(sls, sts)]
        outer = [_dot_tn(v[:, sl], kd[:, sl]) for sl in sls]
        intra = [_dot(jnp.where(causal, sc, 0.0).astype(BF16), v[:, sl]) for sc, sl in zip(scores, sls)]
        for h, sl in enumerate(sls):
            st_ref[h] = sts[h] * decay[:, sl] + outer[h]
            o = _rms(inter[h] + intra[h], onorm[:, sl]) * gate[:, sl]
            o_ref[0, pl.ds(r0, L), sl] = o.astype(o_ref.dtype)
        return carry

    lax.fori_loop(0, n_chunks, chunk, 0)

    @pl.when(t == pl.num_programs(1) - 1)
    def _():
        for h in range(H):
            so_ref[0, h] = st_ref[h].T


def gla(z, lb, onorm_g, s0):
    B, T, _ = z.shape
    L = CHUNK if T % CHUNK == 0 else T
    tt = min(T, 512)
    n_chunks = tt // L
    H, D = HG_HEADS, D_MODEL

    def zspec(part):
        return pl.BlockSpec((1, tt, D), lambda b, t: (b, t, part))

    hspec = pl.BlockSpec((1, D), lambda b, t: (0, 0))
    sspec = pl.BlockSpec((1, H, HG_DK, HG_DV), lambda b, t: (b, 0, 0, 0))
    in_specs = [zspec(0), zspec(1), zspec(2), zspec(3), hspec, hspec]
    args = [z, z, z, z, lb.reshape(1, D), onorm_g.reshape(1, D)]
    if s0 is not None:
        in_specs.append(sspec)
        args.append(s0)
    return pl.pallas_call(
        functools.partial(_gla_kernel, L=L, n_chunks=n_chunks, has_init=s0 is not None),
        grid=(B, T // tt),
        in_specs=in_specs,
        out_specs=[pl.BlockSpec((1, tt, D), lambda b, t: (b, t, 0)), sspec],
        out_shape=[jax.ShapeDtypeStruct((B, T, D), BF16),
                   jax.ShapeDtypeStruct((B, H, HG_DK, HG_DV), F32)],
        scratch_shapes=[pltpu.VMEM((H, HG_DV, HG_DK), F32)],
        compiler_params=_cparams(2),
        name="gla",
    )(*args)


def _route_kernel(h_ref, rw_ref, bias_ref, pos_ref, w_ref, te_ref, nu_ref,
                  e_s, r_s, base_s, start_s, *, tile_rows):
    ph = pl.program_id(0)
    i = pl.program_id(1)
    M = h_ref.shape[0]
    half = h_ref.shape[1]
    G, E = N_GROUPS, N_EXPERTS // N_GROUPS
    e_flat = lax.broadcasted_iota(jnp.int32, (N_EXPERTS, M), 0)

    @pl.when(ph == 1)
    def _():
        @pl.when(i == 0)
        def _():
            cnt = base_s[...]
            padded = jnp.floor((cnt + (tile_rows - 1)) * (1.0 / tile_rows)) * tile_rows
            r = lax.broadcasted_iota(jnp.int32, (N_EXPERTS, N_EXPERTS), 0)
            c = lax.broadcasted_iota(jnp.int32, (N_EXPERTS, N_EXPERTS), 1)
            start = jnp.dot((c < r).astype(F32), padded, preferred_element_type=F32,
                            precision=lax.Precision.HIGHEST)
            start_s[...] = start
            seg_end = (start + padded)[:, :1]
            nt = te_ref.shape[1]
            tile_lo = (lax.broadcasted_iota(jnp.int32, (N_EXPERTS, nt), 1) * tile_rows).astype(F32)
            owner = jnp.sum((seg_end <= tile_lo).astype(F32), axis=0, keepdims=True)
            te_ref[...] = jnp.minimum(owner, N_EXPERTS - 1.0).astype(jnp.int32)
            total = jnp.max(start + padded, axis=0, keepdims=True)
            nu_ref[...] = (total * (1.0 / tile_rows)).astype(jnp.int32)

        start_col = start_s[:, :1]
        for k in range(TOP_K):
            hit = e_flat == e_s[i, k:k + 1, :]
            seg = jnp.sum(jnp.where(hit, start_col, 0.0), axis=0, keepdims=True)
            pos_ref[k:k + 1, :] = (seg + r_s[i, k:k + 1, :]).astype(jnp.int32)

    @pl.when(ph == 0)
    def _():
        _route_pass0(h_ref, rw_ref, bias_ref, w_ref, e_s, r_s, base_s, i, M, half, G, E)


def _route_pass0(h_ref, rw_ref, bias_ref, w_ref, e_s, r_s, base_s, i, M, half, G, E):
    @pl.when(i == 0)
    def _():
        base_s[...] = jnp.zeros_like(base_s)

    lo, hi = _unpack_pairs(h_ref[...])
    rw = rw_ref[...].astype(BF16)
    logits = _dot_nt(rw[:, :half], lo) + _dot_nt(rw[:, half:], hi)
    s = jax.nn.sigmoid(logits)
    sb = (s + bias_ref[...]).reshape(G, E, M)
    s = s.reshape(G, E, M)
    e_in = lax.broadcasted_iota(jnp.int32, (G, E, M), 1)
    g_id = lax.broadcasted_iota(jnp.int32, (G, 1, M), 0)
    e_id = lax.broadcasted_iota(jnp.int32, (G, E, M), 0) * E + e_in

    m1 = jnp.max(sb, axis=1, keepdims=True)
    first = jnp.min(jnp.where(sb == m1, e_in, E), axis=1, keepdims=True)
    m2 = jnp.max(jnp.where(e_in == first, NEG_INF, sb), axis=1, keepdims=True)
    gs = m1 + m2

    rank = jnp.zeros((G, 1, M), jnp.int32)
    for j in range(G):
        gj = gs[j:j + 1]
        beats = (gj > gs) | ((gj == gs) & (j < g_id))
        rank = rank + beats.astype(jnp.int32)
    gsel = rank < TOPK_GROUPS

    vals = jnp.where(gsel, sb, NEG_INF)
    w = jnp.zeros((G, E, M), F32)
    selm = jnp.zeros((G, E, M), F32)
    chosen = []
    for _ in range(TOP_K):
        m = jnp.max(jnp.max(vals, axis=1, keepdims=True), axis=0, keepdims=True)
        cand = jnp.where(vals == m, e_id, N_EXPERTS)
        first = jnp.min(jnp.min(cand, axis=1, keepdims=True), axis=0, keepdims=True)
        hit = e_id == first
        w = jnp.where(hit, s, w)
        selm = jnp.where(hit, 1.0, selm)
        vals = jnp.where(hit, NEG_INF, vals)
        chosen.append(first.reshape(1, M))

    tot = jnp.sum(jnp.sum(w, axis=1, keepdims=True), axis=0, keepdims=True)
    gates = (w / tot * ROUTED_SCALE).reshape(N_EXPERTS, M)
    selm = selm.reshape(N_EXPERTS, M)

    earlier = (lax.broadcasted_iota(jnp.int32, (M, M), 0)
               < lax.broadcasted_iota(jnp.int32, (M, M), 1)).astype(BF16)
    rank = base_s[:, :1] + _dot(selm.astype(BF16), earlier)
    base_s[...] = base_s[...] + jnp.sum(selm, axis=1, keepdims=True)
    e_flat = lax.broadcasted_iota(jnp.int32, (N_EXPERTS, M), 0)
    for k in range(TOP_K):
        hit = e_flat == chosen[k]
        e_s[i, k:k + 1, :] = chosen[k]
        r_s[i, k:k + 1, :] = jnp.sum(jnp.where(hit, rank, 0.0), axis=0, keepdims=True)
        w_ref[k:k + 1, :] = jnp.sum(jnp.where(hit, gates, 0.0), axis=0, keepdims=True)


def route(hp, router_w_t, router_bias, l, tile_rows, n_tiles, rows=512):
    N, half = hp.shape
    M = rows
    nT = N // M
    assert N % M == 0
    nt_pad = -(-n_tiles // 128) * 128

    def p0(ph, i):
        return i * (1 - ph) + (nT - 1) * ph

    return pl.pallas_call(
        functools.partial(_route_kernel, tile_rows=tile_rows),
        grid=(2, nT),
        in_specs=[pl.BlockSpec((M, half), lambda ph, i: (p0(ph, i), 0)),
                  pl.BlockSpec((None, N_EXPERTS, 2 * half), lambda ph, i: (l, 0, 0)),
                  pl.BlockSpec((None, N_EXPERTS, 1), lambda ph, i: (l, 0, 0))],
        out_specs=[pl.BlockSpec((TOP_K, M), lambda ph, i: (0, i * ph)),
                   pl.BlockSpec((TOP_K, M), lambda ph, i: (0, p0(ph, i))),
                   pl.BlockSpec((1, nt_pad), lambda ph, i: (0, 0)),
                   pl.BlockSpec((1, 128), lambda ph, i: (0, 0))],
        out_shape=[jax.ShapeDtypeStruct((TOP_K, N), jnp.int32),
                   jax.ShapeDtypeStruct((TOP_K, N), F32),
                   jax.ShapeDtypeStruct((1, nt_pad), jnp.int32),
                   jax.ShapeDtypeStruct((1, 128), jnp.int32)],
        scratch_shapes=[pltpu.VMEM((nT, TOP_K, M), jnp.int32), pltpu.VMEM((nT, TOP_K, M), F32),
                        pltpu.VMEM((N_EXPERTS, 128), F32), pltpu.VMEM((N_EXPERTS, 128), F32)],
        compiler_params=_cparams(2),
        name="route",
    )(hp, router_w_t, router_bias.reshape(-1, N_EXPERTS, 1))


def _sc_mesh():
    return plsc.VectorSubcoreMesh(core_axis_name="core", subcore_axis_name="subcore")


def sc_invert(pos_flat, n_tok, n_out):
    n = pos_flat.shape[0]
    per = n_out // SC_WORKERS
    chunk = n_tok
    assert n_out % SC_WORKERS == 0 and per % SC_LANES == 0
    assert n_tok % chunk == 0 and n % chunk == 0 and chunk % SC_LANES == 0
    cp = pltpu.CompilerParams()
    if "needs_layout_passes" in pltpu.CompilerParams.__dataclass_fields__:
        cp = dataclasses.replace(cp, needs_layout_passes=False)

    @functools.partial(
        pl.kernel, out_type=jax.ShapeDtypeStruct((n_out,), jnp.int32), mesh=_sc_mesh(),
        scratch_types=[pltpu.VMEM((chunk,), jnp.int32), pltpu.VMEM((per,), jnp.int32)],
        compiler_params=cp, name="sc_invert")
    def k(pos_hbm, src_hbm, pos_v, src_v):
        wid = lax.axis_index("subcore") * SC_CORES + lax.axis_index("core")
        lo = wid * per
        lane = lax.iota(jnp.int32, SC_LANES)

        @pl.loop(0, per, step=SC_LANES)
        def _(r):
            src_v[pl.ds(r, SC_LANES)] = lax.rem(lo + r + lane, n_tok)

        @pl.loop(0, n // chunk)
        def _(c):
            base = c * chunk
            pltpu.sync_copy(pos_hbm.at[pl.ds(base, chunk)], pos_v)
            tok0 = lax.rem(base, n_tok)

            @pl.loop(0, chunk, step=SC_LANES)
            def _(r):
                p = pos_v[pl.ds(r, SC_LANES)] - lo
                mine = (p >= 0) & (p < per)
                plsc.store_scatter(src_v, [jnp.where(mine, p, 0)], tok0 + r + lane, mask=mine)

        pltpu.sync_copy(src_v, src_hbm.at[pl.ds(lo, per)])

    return k(pos_flat)


def sc_gather(x, idx):
    n = idx.shape[0]
    dim = x.shape[1]
    assert n % (SC_WINDOW * SC_WORKERS) == 0

    @functools.partial(
        pl.kernel, out_type=jax.ShapeDtypeStruct((n, dim), x.dtype), mesh=_sc_mesh(),
        scratch_types=[], name="sc_gather")
    def k(x_hbm, i_hbm, o_hbm):
        def body(i_vmem, o_vmem):
            pltpu.sync_copy(x_hbm.at[i_vmem.at[0]], o_vmem)

        pltpu.emit_pipeline(
            body, grid=(n // SC_WINDOW,),
            in_specs=[pl.BlockSpec((1, SC_WINDOW), index_map=lambda i: (i, 0))],
            out_specs=[pl.BlockSpec((SC_WINDOW, dim), index_map=lambda i: (i, 0))],
            core_axis_name=("core", "subcore"),
            dimension_semantics=(pltpu.PARALLEL,),
        )(i_hbm, o_hbm)

    return k(x, idx.reshape(n // SC_WINDOW, SC_WINDOW))


def _moe_gemm_kernel(te_ref, nu_ref, x_ref, wi_ref, wo_ref, o_ref, wi_b, wo_b):
    j = pl.program_id(0)
    e_now = te_ref[j]
    e_prev = te_ref[jnp.maximum(j - 1, 0)]

    @pl.when((j == 0) | (e_now != e_prev))
    def _():
        wi_b[...] = wi_ref[...].astype(BF16)
        wo_b[...] = wo_ref[...].astype(BF16)

    @pl.when(j < nu_ref[0])
    def _():
        rows = x_ref.shape[0] // MOE_SUB
        half = x_ref.shape[1]
        xs = [_unpack_pairs(x_ref[r * rows:(r + 1) * rows, :]) for r in range(MOE_SUB)]
        hus = [_dot(lo, wi_b[:half, :]) + _dot(hi, wi_b[half:, :]) for lo, hi in xs]
        acts = [(_silu(hu[:, :EXPERT_FF]) * hu[:, EXPERT_FF:]).astype(BF16) for hu in hus]
        outs = [_dot(act, wo_b[...]) for act in acts]
        for r, out in enumerate(outs):
            o_ref[r * rows:(r + 1) * rows, :] = _pack_pairs(out)

    @pl.when(j >= nu_ref[0])
    def _():
        o_ref[...] = jnp.zeros_like(o_ref)


def moe_gemm(xs, tile_expert, n_used, exp_w_in, exp_w_out, l, tile_rows, n_tiles):
    P, half = xs.shape
    D = 2 * half
    assert P == n_tiles * tile_rows
    grid_spec = pltpu.PrefetchScalarGridSpec(
        num_scalar_prefetch=2,
        grid=(n_tiles,),
        in_specs=[pl.BlockSpec((tile_rows, half), lambda j, te, nu: (j, 0)),
                  pl.BlockSpec((None, None, D, 2 * EXPERT_FF), lambda j, te, nu: (l, te[j], 0, 0)),
                  pl.BlockSpec((None, None, EXPERT_FF, D), lambda j, te, nu: (l, te[j], 0, 0))],
        out_specs=pl.BlockSpec((tile_rows, half), lambda j, te, nu: (j, 0)),
        scratch_shapes=[pltpu.VMEM((D, 2 * EXPERT_FF), BF16), pltpu.VMEM((EXPERT_FF, D), BF16)],
    )
    return pl.pallas_call(
        _moe_gemm_kernel,
        grid_spec=grid_spec,
        out_shape=jax.ShapeDtypeStruct((P, half), jnp.int32),
        compiler_params=_cparams(1),
        name="moe_gemm",
    )(tile_expert, n_used, xs, exp_w_in, exp_w_out)


def _moe_combine_kernel(y_ref, w_ref, h_ref, si_ref, so_ref, x_ref, g2_ref, o_ref, si_b, so_b):
    @pl.when(pl.program_id(0) == 0)
    def _():
        si_b[...] = si_ref[...].astype(BF16)
        so_b[...] = so_ref[...].astype(BF16)

    bb, tt, D = x_ref.shape
    half = D // 2
    w = w_ref[...]
    acc_lo = jnp.zeros((bb * tt, half), F32)
    acc_hi = jnp.zeros((bb * tt, half), F32)
    for k in range(TOP_K):
        lo, hi = _unpack_pairs(y_ref[k], F32)
        acc_lo = acc_lo + w[:, k:k + 1] * lo
        acc_hi = acc_hi + w[:, k:k + 1] * hi
    hlo, hhi = _unpack_pairs(h_ref[...])
    hu = _dot(hlo, si_b[:half, :]) + _dot(hhi, si_b[half:, :])
    act = (_silu(hu[:, :SHARED_FF]) * hu[:, SHARED_FF:]).astype(BF16)
    y = jnp.concatenate([acc_lo, acc_hi], axis=-1) + _dot(act, so_b[...])
    o_ref[...] = x_ref[...] + g2_ref[...] * y.reshape(bb, tt, D)


def moe_combine(y8, w_t, hp, sh_w_in, sh_w_out, x, mod, gate_idx, l, row0, rows=256):
    B, T, D = x.shape
    half = D // 2
    bb, tt, nblk, ij = _row_blocks(B, T, rows)
    M = bb * tt
    assert row0 % M == 0
    off = row0 // M
    xspec = pl.BlockSpec((bb, tt, D), lambda i: ij(i) + (0,))
    return pl.pallas_call(
        _moe_combine_kernel,
        grid=(nblk,),
        in_specs=[pl.BlockSpec((TOP_K, M, half), lambda i: (0, off + i, 0)),
                  pl.BlockSpec((M, TOP_K), lambda i: (off + i, 0)),
                  pl.BlockSpec((M, half), lambda i: (off + i, 0)),
                  pl.BlockSpec((None, D, 2 * SHARED_FF), lambda i: (l, 0, 0)),
                  pl.BlockSpec((None, SHARED_FF, D), lambda i: (l, 0, 0)),
                  xspec,
                  pl.BlockSpec((bb, 1, D), lambda i: (ij(i)[0], 0, gate_idx))],
        out_specs=xspec,
        out_shape=jax.ShapeDtypeStruct((B, T, D), F32),
        scratch_shapes=[pltpu.VMEM((D, 2 * SHARED_FF), BF16), pltpu.VMEM((SHARED_FF, D), BF16)],
        compiler_params=_cparams(1),
        name="moe_combine",
    )(y8, w_t, hp, sh_w_in, sh_w_out, x, mod)


def _shared_kv_kernel(x_ref, g_ref, w_ref, lg_ref, cos_ref, sin_ref, lat_ref, kr_ref):
    bb, tt, D = x_ref.shape
    xn = _rms(x_ref[...], g_ref[...]).reshape(bb * tt, D).astype(BF16)
    z = _dot(xn, w_ref[...].astype(BF16))
    lat = _rms(z[:, :KV_LORA], lg_ref[...])
    lat_ref[...] = lat.reshape(bb, tt, KV_LORA)
    zr = z[:, KV_LORA:KV_LORA + MLA_ROPE].reshape(bb, tt, MLA_ROPE)
    zq = z[:, KV_LORA + 128:KV_LORA + 128 + MLA_ROPE].reshape(bb, tt, MLA_ROPE)
    kr_ref[...] = zr * cos_ref[...] + zq * sin_ref[...]


def shared_kv(x, kv_in_g, w_kv, kv_lat_g, cos32, sin32, rows=512):
    B, T, D = x.shape
    bb, tt, nblk, ij = _row_blocks(B, T, rows)
    tspec = pl.BlockSpec((tt, MLA_ROPE), lambda i: (ij(i)[1], 0))
    return pl.pallas_call(
        _shared_kv_kernel,
        grid=(nblk,),
        in_specs=[pl.BlockSpec((bb, tt, D), lambda i: ij(i) + (0,)),
                  pl.BlockSpec((1, D), lambda i: (0, 0)),
                  pl.BlockSpec(w_kv.shape, lambda i: (0, 0)),
                  pl.BlockSpec((1, KV_LORA), lambda i: (0, 0)),
                  tspec, tspec],
        out_specs=[pl.BlockSpec((bb, tt, KV_LORA), lambda i: ij(i) + (0,)),
                   pl.BlockSpec((bb, tt, MLA_ROPE), lambda i: ij(i) + (0,))],
        out_shape=[jax.ShapeDtypeStruct((B, T, KV_LORA), F32),
                   jax.ShapeDtypeStruct((B, T, MLA_ROPE), F32)],
        compiler_params=_cparams(1),
        name="shared_kv",
    )(x, kv_in_g.reshape(1, D), w_kv, kv_lat_g.reshape(1, KV_LORA), cos32, sin32)


def _kv_expand_kernel(lat_ref, kr_ref, wk_ref, ek_ref, wvt_ref, ones_ref, k_ref, vt_ref):
    lat = lat_ref[0].astype(BF16)
    kr = kr_ref[0].astype(BF16)
    k = _dot(lat, wk_ref[...].astype(BF16)) + _dot(kr, ek_ref[...].astype(BF16))
    k_ref[0] = k.astype(k_ref.dtype)
    vt = _dot_nt(wvt_ref[...].astype(BF16), lat) + ones_ref[...]
    vt_ref[0] = vt.astype(vt_ref.dtype)


def kv_expand(lat, kr, wk_pad, ek, wvt_ext, ones_col, rows=512):
    B, T, _ = lat.shape
    tt = rows
    NK, NVT = wk_pad.shape[1], wvt_ext.shape[0]

    def full(a):
        return pl.BlockSpec(a.shape, lambda b, t: (0, 0))

    def rowspec(n):
        return pl.BlockSpec((1, tt, n), lambda b, t: (b, t, 0))

    return pl.pallas_call(
        _kv_expand_kernel,
        grid=(B, T // tt),
        in_specs=[rowspec(KV_LORA), rowspec(MLA_ROPE), full(wk_pad), full(ek), full(wvt_ext), full(ones_col)],
        out_specs=[rowspec(NK), pl.BlockSpec((1, NVT, tt), lambda b, t: (b, 0, t))],
        out_shape=[jax.ShapeDtypeStruct((B, T, NK), BF16), jax.ShapeDtypeStruct((B, NVT, T), BF16)],
        compiler_params=_cparams(2),
        name="kv_expand",
    )(lat, kr, wk_pad, ek, wvt_ext, ones_col)


def _query_kernel(h_ref, wdq_ref, qg_ref, wq_ref, wqr_ref, c_ref, s_ref, q_ref, wdq_b, wq_b, wqr_b):
    @pl.when(pl.program_id(0) == 0)
    def _():
        wdq_b[...] = wdq_ref[...].astype(BF16)
        wq_b[...] = wq_ref[...].astype(BF16)
        wqr_b[...] = wqr_ref[...].astype(BF16)

    bb, tt, D = h_ref.shape
    h = h_ref[...].reshape(bb * tt, D)
    cq = _rms(_dot(h, wdq_b[...]), qg_ref[...]).astype(BF16)
    q1 = _dot(cq, wq_b[...]).reshape(bb, tt, -1)
    q2 = _dot(cq, wqr_b[...]).reshape(bb, tt, -1)
    c = c_ref[...]
    s = s_ref[...]
    for hd in range(MLA_HEADS):
        sl = slice(hd * HEAD_PAD, (hd + 1) * HEAD_PAD)
        q_ref[:, :, sl] = (q1[:, :, sl] * c + q2[:, :, sl] * s).astype(q_ref.dtype)


def mla_queries(h, w_dq, q_norm_g, wq_pad, wq_rot, l, c128, s128, rows=512):
    B, T, D = h.shape
    bb, tt, nblk, ij = _row_blocks(B, T, rows)
    NQ = wq_pad.shape[-1]
    tspec = pl.BlockSpec((tt, HEAD_PAD), lambda i: (ij(i)[1], 0))
    return pl.pallas_call(
        _query_kernel,
        grid=(nblk,),
        in_specs=[pl.BlockSpec((bb, tt, D), lambda i: ij(i) + (0,)),
                  pl.BlockSpec((None, D, Q_LORA), lambda i: (l, 0, 0)),
                  pl.BlockSpec((None, 1, Q_LORA), lambda i: (l, 0, 0)),
                  pl.BlockSpec((None, Q_LORA, NQ), lambda i: (l, 0, 0)),
                  pl.BlockSpec((None, Q_LORA, NQ), lambda i: (l, 0, 0)),
                  tspec, tspec],
        out_specs=pl.BlockSpec((bb, tt, NQ), lambda i: ij(i) + (0,)),
        out_shape=jax.ShapeDtypeStruct((B, T, NQ), BF16),
        scratch_shapes=[pltpu.VMEM((D, Q_LORA), BF16), pltpu.VMEM((Q_LORA, NQ), BF16),
                        pltpu.VMEM((Q_LORA, NQ), BF16)],
        compiler_params=_cparams(1),
        name="mla_queries",
    )(h, w_dq, q_norm_g.reshape(-1, 1, Q_LORA), wq_pad, wq_rot, c128, s128)


def _attn_prompt_kernel(qi_tab, ki_tab, q_ref, k_ref, vt_ref, o_ref, *scratch, tq, tk):
    H = MLA_HEADS
    m_refs, l_refs, acc_refs = scratch[:H], scratch[H:2 * H], scratch[2 * H:]
    p_id = pl.program_id(1)
    qi = qi_tab[p_id]
    ki = ki_tab[p_id]

    @pl.when(ki == 0)
    def _():
        for hd in range(H):
            m_refs[hd][...] = jnp.full(m_refs[hd].shape, NEG_INF, F32)
            l_refs[hd][...] = jnp.zeros(l_refs[hd].shape, F32)
            acc_refs[hd][...] = jnp.zeros(acc_refs[hd].shape, F32)

    def block(masked):
        if masked:
            kchunk = (ki * tk + lax.broadcasted_iota(jnp.int32, (tk, tq), 0)) // CHUNK
            qchunk = (qi * tq + lax.broadcasted_iota(jnp.int32, (tk, tq), 1)) // CHUNK
            mask = kchunk <= qchunk
        def scores(hd):
            sl = slice(hd * HEAD_PAD, (hd + 1) * HEAD_PAD)
            return _dot_nt(k_ref[0, :, sl], q_ref[0, :, sl])

        pending = [scores(hd) for hd in range(ATTN_LOOKAHEAD)]
        for hd in range(H):
            if hd + ATTN_LOOKAHEAD < H:
                pending.append(scores(hd + ATTN_LOOKAHEAD))
            s = pending.pop(0)
            if masked:
                s = jnp.where(mask, s, NEG_INF)
            m_prev = m_refs[hd][...]
            m_new = jnp.maximum(m_prev, jnp.max(s, axis=0, keepdims=True))
            a = jnp.exp2(m_prev - m_new)
            p = jnp.exp2(s - m_new).astype(BF16)
            pv = _dot(vt_ref[0, hd * V_ROWS:(hd + 1) * V_ROWS, :], p)
            acc_refs[hd][...] = a * acc_refs[hd][...] + pv[:MLA_V]
            l_refs[hd][...] = a * l_refs[hd][...] + pv[MLA_V:MLA_V + 1]
            m_refs[hd][...] = m_new

    @pl.when(ki < qi)
    def _():
        block(False)

    @pl.when(ki == qi)
    def _():
        block(True)
        o_t = jnp.concatenate([acc_refs[hd][...] / l_refs[hd][...] for hd in range(H)], axis=0)
        o_ref[0] = o_t.T.astype(o_ref.dtype)


def attn_prompt(q, k, vt, tq=256):
    B, T, NQ = q.shape
    NVT = vt.shape[1]
    NV = MLA_HEADS * MLA_V
    tk = tq
    assert tq % CHUNK == 0
    nq = T // tq
    pairs = [(a, b) for a in range(nq) for b in range(a + 1)]
    qi_tab = jnp.asarray([a for a, _ in pairs], jnp.int32)
    ki_tab = jnp.asarray([b for _, b in pairs], jnp.int32)
    grid_spec = pltpu.PrefetchScalarGridSpec(
        num_scalar_prefetch=2,
        grid=(B, len(pairs)),
        in_specs=[pl.BlockSpec((1, tq, NQ), lambda b, p, qt, kt: (b, qt[p], 0)),
                  pl.BlockSpec((1, tk, NQ), lambda b, p, qt, kt: (b, kt[p], 0)),
                  pl.BlockSpec((1, NVT, tk), lambda b, p, qt, kt: (b, 0, kt[p]))],
        out_specs=pl.BlockSpec((1, tq, NV), lambda b, p, qt, kt: (b, qt[p], 0)),
        scratch_shapes=([pltpu.VMEM((1, tq), F32)] * (2 * MLA_HEADS)
                        + [pltpu.VMEM((MLA_V, tq), F32)] * MLA_HEADS),
    )
    return pl.pallas_call(
        functools.partial(_attn_prompt_kernel, tq=tq, tk=tk),
        grid_spec=grid_spec,
        out_shape=jax.ShapeDtypeStruct((B, T, NV), BF16),
        compiler_params=_cparams(2),
        name="attn_prompt",
    )(qi_tab, ki_tab, q, k, vt)


def _absorb_kernel(q_ref, m_ref, o_ref):
    o_ref[...] = _dot(q_ref[...], m_ref[...].astype(BF16)).astype(o_ref.dtype)


def absorb_queries(q2d, m_abs):
    N = q2d.shape[0]
    H, _, W = m_abs.shape
    return pl.pallas_call(
        _absorb_kernel,
        grid=(H,),
        in_specs=[pl.BlockSpec((N, HEAD_PAD), lambda h: (0, h)),
                  pl.BlockSpec((None, HEAD_PAD, W), lambda h: (h, 0, 0))],
        out_specs=pl.BlockSpec((None, N, W), lambda h: (h, 0, 0)),
        out_shape=jax.ShapeDtypeStruct((H, N, W), BF16),
        compiler_params=_cparams(1),
        name="absorb_queries",
    )(q2d, m_abs)


def _attn_sample_kernel(q_ref, lat_ref, kr_ref, nlat_ref, nkr_ref, o_ref, m_ref, l_ref, acc_ref):
    kb = pl.program_id(1)
    H, Q, W = q_ref.shape
    q = q_ref[...].reshape(H * Q, W)
    q_lat = q[:, :KV_LORA]
    q_rope = q[:, KV_LORA:KV_LORA + MLA_ROPE]

    def update(lat, kr):
        lat = lat.astype(BF16)
        s = _dot_nt(q_lat, lat) + _dot_nt(q_rope, kr.astype(BF16))
        m_prev = m_ref[...]
        m_new = jnp.maximum(m_prev, jnp.max(s, axis=-1, keepdims=True))
        a = jnp.exp2(m_prev - m_new)
        p = jnp.exp2(s - m_new[:, :1])
        l_ref[...] = a * l_ref[...] + jnp.sum(p, axis=-1, keepdims=True)
        m_ref[...] = m_new
        acc_ref[...] = jnp.concatenate([a, a], axis=-1) * acc_ref[...] + _dot(p.astype(BF16), lat)

    @pl.when(kb == 0)
    def _():
        m_ref[...] = jnp.full_like(m_ref, NEG_INF)
        l_ref[...] = jnp.zeros_like(l_ref)
        acc_ref[...] = jnp.zeros_like(acc_ref)
        update(nlat_ref[0], nkr_ref[0])

    update(lat_ref[0], kr_ref[0])

    @pl.when(kb == pl.num_programs(1) - 1)
    def _():
        lsum = l_ref[...]
        o = acc_ref[...] / jnp.concatenate([lsum, lsum], axis=-1)
        o_ref[...] = o.reshape(H, Q, KV_LORA).astype(o_ref.dtype)


def attn_sample(q_abs, cache_lat, cache_kr, new_lat, new_kr, tk=1024):
    H, N, W = q_abs.shape
    B, P, _ = cache_lat.shape
    Q = new_lat.shape[1]
    qpos = P + np.arange(Q)
    kpos = np.arange(P + Q)
    assert bool(np.all((kpos // CHUNK)[None, :] <= (qpos // CHUNK)[:, None]))
    return pl.pallas_call(
        _attn_sample_kernel,
        grid=(B, P // tk),
        in_specs=[pl.BlockSpec((H, Q, W), lambda b, kb: (0, b, 0)),
                  pl.BlockSpec((1, tk, KV_LORA), lambda b, kb: (b, kb, 0)),
                  pl.BlockSpec((1, tk, MLA_ROPE), lambda b, kb: (b, kb, 0)),
                  pl.BlockSpec((1, Q, KV_LORA), lambda b, kb: (b, 0, 0)),
                  pl.BlockSpec((1, Q, MLA_ROPE), lambda b, kb: (b, 0, 0))],
        out_specs=pl.BlockSpec((H, Q, KV_LORA), lambda b, kb: (0, b, 0)),
        out_shape=jax.ShapeDtypeStruct((H, N, KV_LORA), BF16),
        scratch_shapes=[pltpu.VMEM((H * Q, 128), F32), pltpu.VMEM((H * Q, 128), F32),
                        pltpu.VMEM((H * Q, KV_LORA), F32)],
        compiler_params=_cparams(2),
        name="attn_sample",
    )(q_abs, cache_lat, cache_kr, new_lat, new_kr)


def _unabsorb_kernel(o_ref, w_ref, out_ref):
    out_ref[...] = (_dot(o_ref[0], w_ref[0].astype(BF16))
                    + _dot(o_ref[1], w_ref[1].astype(BF16))).astype(out_ref.dtype)


def unabsorb(o_lat, wuv_pad):
    H, N, R = o_lat.shape
    return pl.pallas_call(
        _unabsorb_kernel,
        grid=(H // 2,),
        in_specs=[pl.BlockSpec((2, N, R), lambda p: (p, 0, 0)),
                  pl.BlockSpec((2, R, 128), lambda p: (p, 0, 0))],
        out_specs=pl.BlockSpec((N, 128), lambda p: (0, p)),
        out_shape=jax.ShapeDtypeStruct((N, (H // 2) * 128), BF16),
        compiler_params=_cparams(1),
        name="unabsorb",
    )(o_lat, wuv_pad)


def _rope_tables(pos):
    half = MLA_ROPE // 2
    inv = 1.0 / (ROPE_THETA ** (np.arange(half, dtype=np.float64) * 2.0 / MLA_ROPE))
    ang = np.asarray(pos, np.float64)[:, None] * inv[None, :]
    cos = np.concatenate([np.cos(ang), np.cos(ang)], axis=-1)
    sin = np.concatenate([np.sin(ang), np.sin(ang)], axis=-1)
    T = cos.shape[0]
    c128 = np.zeros((T, HEAD_PAD)); s128 = np.zeros((T, HEAD_PAD))
    c128[:, :MLA_NOPE] = 1.0
    c128[:, MLA_NOPE:MLA_NOPE + MLA_ROPE] = cos
    s128[:, MLA_NOPE:MLA_NOPE + MLA_ROPE] = sin
    return (jnp.asarray(cos, F32), jnp.asarray(sin, F32),
            jnp.asarray(c128 * Q_PRESCALE, F32), jnp.asarray(s128 * Q_PRESCALE, F32))


def _rot_half_cols(w):
    half = w.shape[-1] // 2
    return jnp.concatenate([-w[..., half:], w[..., :half]], axis=-1)


def _prep_weights(w_dkv, w_uk, w_uv, w_uq, router_w):
    D = D_MODEL
    w_lat, w_rope = w_dkv[:, :KV_LORA], w_dkv[:, KV_LORA:]
    pad96 = jnp.zeros((D, 128 - MLA_ROPE), F32)
    w_kv = jnp.concatenate([w_lat, w_rope, pad96, _rot_half_cols(w_rope), pad96], axis=-1)

    zpad = HEAD_PAD - MLA_NOPE
    wk_pad = jnp.pad(w_uk, ((0, 0), (0, 0), (0, zpad))).reshape(KV_LORA, MLA_HEADS * HEAD_PAD)
    ek = jnp.zeros((MLA_ROPE, MLA_HEADS, HEAD_PAD), F32)
    ek = ek.at[:, :, MLA_NOPE:MLA_NOPE + MLA_ROPE].set(
        jnp.broadcast_to(jnp.eye(MLA_ROPE, dtype=F32)[:, None, :], (MLA_ROPE, MLA_HEADS, MLA_ROPE)))
    ek = ek.reshape(MLA_ROPE, MLA_HEADS * HEAD_PAD)
    wvt = jnp.transpose(w_uv, (1, 2, 0))
    wvt_ext = jnp.pad(wvt, ((0, 0), (0, V_ROWS - MLA_V), (0, 0))).reshape(MLA_HEADS * V_ROWS, KV_LORA)
    ones_col = jnp.tile((jnp.arange(V_ROWS) >= MLA_V).astype(F32), MLA_HEADS).reshape(-1, 1)

    nb = w_uq.shape[0]
    qn, qr = w_uq[..., :MLA_NOPE], w_uq[..., MLA_NOPE:]
    z32 = jnp.zeros(qr.shape[:-1] + (HEAD_PAD - MLA_NOPE - MLA_ROPE,), F32)
    wq_pad = jnp.concatenate([qn, qr, z32], axis=-1).reshape(nb, Q_LORA, MLA_HEADS * HEAD_PAD)
    wq_rot = jnp.concatenate([jnp.zeros_like(qn), _rot_half_cols(qr), z32], axis=-1)
    wq_rot = wq_rot.reshape(nb, Q_LORA, MLA_HEADS * HEAD_PAD)

    m_abs = jnp.zeros((MLA_HEADS, HEAD_PAD, KV_LORA + 128), F32)
    m_abs = m_abs.at[:, :MLA_NOPE, :KV_LORA].set(jnp.transpose(w_uk, (1, 2, 0)))
    m_abs = m_abs.at[:, MLA_NOPE:MLA_NOPE + MLA_ROPE, KV_LORA:KV_LORA + MLA_ROPE].set(
        jnp.broadcast_to(jnp.eye(MLA_ROPE, dtype=F32), (MLA_HEADS, MLA_ROPE, MLA_ROPE)))

    wuv_h = jnp.transpose(w_uv, (1, 0, 2))
    even = jnp.pad(wuv_h, ((0, 0), (0, 0), (0, 64)))
    odd = jnp.pad(wuv_h, ((0, 0), (0, 0), (64, 0)))
    wuv_pad = jnp.where((jnp.arange(MLA_HEADS) % 2 == 0)[:, None, None], even, odd)

    rw_t = jnp.transpose(router_w, (0, 2, 1))
    return dict(w_kv=w_kv, wk_pad=wk_pad, ek=ek, wvt_ext=wvt_ext, ones_col=ones_col, wq_pad=wq_pad, wq_rot=wq_rot,
                m_abs=m_abs, wuv_pad=wuv_pad, rw_t=rw_t)


def _mixer(st, l, P, W):
    x, m = st["x"], st["mod"][l]
    B, T, _ = x.shape
    n_a = P["hg_w_in"].shape[0]
    h = norm_mod(x, P["norm1_g"][l], m, sc_idx=1, sh_idx=0)
    if l < n_a:
        z = linear(h, P["hg_w_in"], l, F32)
        s0 = None if st["hg_state"] is None else st["hg_state"][l]
        o, s_new = gla(z, st["lbs"][l], P["hg_onorm_g"][l], s0)
        st["hg_new"].append(s_new)
        st["x"] = linear(o, P["hg_w_out"], l, F32, x=x, mod=m, gate_idx=2)
    else:
        bi = l - n_a
        q = mla_queries(h, P["w_dq"], P["q_norm_g"], W["wq_pad"], W["wq_rot"], bi, st["c128"], st["s128"])
        if st["past_lat"] is None:
            o = attn_prompt(q, st["k_all"], st["v_all"])
        else:
            q_abs = absorb_queries(q.reshape(B * T, -1), W["m_abs"])
            o_lat = attn_sample(q_abs, st["past_lat"], st["past_kr"], st["lat"], st["kr"])
            o = unabsorb(o_lat, W["wuv_pad"]).reshape(B, T, -1)
        st["x"] = linear(o, P["w_o"], bi, F32, x=x, mod=m, gate_idx=2)


def _moe(groups, l, P, W):
    hp = jnp.concatenate([norm_mod(st["x"], P["norm2_g"][l], st["mod"][l], sc_idx=4, sh_idx=3, packed=True)
                          for st in groups], axis=0)
    n_tok = hp.shape[0]
    n_tiles = (TOP_K * n_tok) // MOE_TILE + N_EXPERTS
    pos, w8, tile_expert, n_used = route(hp, W["rw_t"], P["router_bias"], l, MOE_TILE, n_tiles)
    pos_flat = pos.reshape(-1)
    src = sc_invert(pos_flat, n_tok, n_tiles * MOE_TILE)
    xs = sc_gather(hp, src)
    out = moe_gemm(xs, tile_expert.reshape(-1), n_used[0, :1], P["exp_w_in"], P["exp_w_out"], l,
                   MOE_TILE, n_tiles)
    y8 = sc_gather(out, pos_flat).reshape(TOP_K, n_tok, -1)
    w_t = w8.T
    row0 = 0
    for st in groups:
        B, T, _ = st["x"].shape
        st["x"] = moe_combine(y8, w_t, hp, P["sh_w_in"], P["sh_w_out"], st["x"], st["mod"][l], 5, l, row0)
        row0 += B * T


def _group_state(x, mod, pos, hg_state, past_lat, past_kr, lbs):
    cos32, sin32, c128, s128 = _rope_tables(pos)
    return dict(x=x, mod=mod, hg_state=hg_state, past_lat=past_lat, past_kr=past_kr, lbs=lbs,
                cos32=cos32, sin32=sin32, c128=c128, s128=s128, hg_new=[],
                lat=None, kr=None, k_all=None, v_all=None)


def kernel(x_prompt, x_sample, state_hgrn, cache_mla_latent, cache_mla_krope, c_prompt, c_sample, ada_w, ada_b, norm1_g, norm2_g, hg_w_in, hg_lb_logits, hg_onorm_g, hg_w_out, kv_in_g, w_dkv, kv_lat_g, w_uk, w_uv, w_dq, q_norm_g, w_uq, w_o, router_w, router_bias, exp_w_in, exp_w_out, sh_w_in, sh_w_out, final_g):
    Bp, Sp, _ = x_prompt.shape
    Bs, Ss, _ = x_sample.shape
    past = cache_mla_latent.shape[1]
    P = dict(norm1_g=norm1_g, norm2_g=norm2_g, hg_w_in=hg_w_in, hg_lb_logits=hg_lb_logits,
             hg_onorm_g=hg_onorm_g, hg_w_out=hg_w_out, kv_in_g=kv_in_g, kv_lat_g=kv_lat_g,
             w_dq=w_dq, q_norm_g=q_norm_g, w_o=w_o, router_bias=router_bias,
             exp_w_in=exp_w_in, exp_w_out=exp_w_out, sh_w_in=sh_w_in, sh_w_out=sh_w_out, final_g=final_g)
    W = _prep_weights(w_dkv, w_uk, w_uv, w_uq, router_w)
    mod = ada_mod(jnp.concatenate([c_prompt, c_sample], axis=0), ada_w, ada_b)
    lbs = jnp.cumsum(jax.nn.softmax(hg_lb_logits.astype(F32), axis=0), axis=0)
    bsz = Bp // PROMPT_STREAMS
    prompts = [_group_state(x_prompt[i * bsz:(i + 1) * bsz], mod[:, i * bsz:(i + 1) * bsz, None, :],
                            np.arange(Sp), None, None, None, lbs) for i in range(PROMPT_STREAMS)]
    gs = _group_state(x_sample, mod[:, Bp:, None, :], past + np.arange(Ss), state_hgrn,
                      cache_mla_latent, cache_mla_krope, lbs)
    streams = [[prompts[0], gs]] + [[g] for g in prompts[1:]]
    n_a = hg_w_in.shape[0]
    for l in range(norm1_g.shape[0]):
        for groups in streams:
            for st in groups:
                _mixer(st, l, P, W)
            _moe(groups, l, P, W)
            if l == n_a - 1:
                for st in groups:
                    st["lat"], st["kr"] = shared_kv(st["x"], kv_in_g, W["w_kv"], kv_lat_g, st["cos32"], st["sin32"])
                    if st["past_lat"] is None:
                        st["k_all"], st["v_all"] = kv_expand(st["lat"], st["kr"], W["wk_pad"], W["ek"],
                                                             W["wvt_ext"], W["ones_col"])
    for st in prompts + [gs]:
        st["y"] = norm_mod(st["x"], final_g, out_dtype=F32)
        st["hg_out"] = jnp.stack(st["hg_new"], axis=0)

    def cat(key, axis=0):
        return jnp.concatenate([g[key] for g in prompts], axis=axis)

    return (cat("y"), gs["y"], cat("hg_out", 1), gs["hg_out"], cat("lat"), cat("kr"), gs["lat"], gs["kr"])
```

```python
import dataclasses
import functools

import numpy as np
import jax
import jax.numpy as jnp
from jax import lax
from jax.experimental import pallas as pl
from jax.experimental.pallas import tpu as pltpu
from jax.experimental.pallas import tpu_sc as plsc

F32 = jnp.float32
BF16 = jnp.bfloat16

D_MODEL = 1024
CHUNK = 64
HG_HEADS = 8
HG_DK = 128
HG_DV = 128
MLA_HEADS = 16
MLA_NOPE = 64
MLA_ROPE = 32
MLA_V = 64
Q_LORA = 384
KV_LORA = 256
ROPE_THETA = 10000.0
N_EXPERTS = 64
TOP_K = 8
N_GROUPS = 8
TOPK_GROUPS = 4
EXPERT_FF = 256
SHARED_FF = 256
ROUTED_SCALE = 2.5
EPS = 1e-6

HEAD_PAD = 128
ATTN_LOOKAHEAD = 4
V_ROWS = MLA_V + 16
QK_SCALE = (MLA_NOPE + MLA_ROPE) ** -0.5
Q_PRESCALE = QK_SCALE * float(np.log2(np.e))
VMEM_LIMIT = 56 * 1024 * 1024
NEG_INF = float("-inf")
SC_CORES = 2
SC_SUBCORES = 16
SC_WORKERS = SC_CORES * SC_SUBCORES
SC_LANES = 16
SC_WINDOW = 64
MOE_TILE = 512
PROMPT_STREAMS = 1
MOE_SUB = 2


def _cparams(n_axes):
    return pltpu.CompilerParams(dimension_semantics=("arbitrary",) * n_axes,
                                vmem_limit_bytes=VMEM_LIMIT)


def _silu(x):
    return x * jax.nn.sigmoid(x)


def _rms(x, g):
    ms = jnp.mean(x * x, axis=-1, keepdims=True)
    return x * lax.rsqrt(ms + EPS) * g


def _dot(a, b):
    return jnp.dot(a, b, preferred_element_type=F32)


def _dot_nt(a, b):
    return lax.dot_general(a, b, (((1,), (1,)), ((), ())), preferred_element_type=F32)


def _dot_tn(a, b):
    return lax.dot_general(a, b, (((0,), (0,)), ((), ())), preferred_element_type=F32)


def _row_blocks(B, T, rows):
    if T >= rows:
        assert T % rows == 0
        bb, tt = 1, rows
    else:
        assert rows % T == 0 and B % (rows // T) == 0
        bb, tt = rows // T, T
    nt = T // tt
    return bb, tt, (B // bb) * nt, (lambda i: (i // nt, i % nt))


def _ada_kernel(c_ref, w_ref, b_ref, o_ref):
    a = _silu(c_ref[...]).astype(BF16)
    o_ref[...] = _dot(a, w_ref[...].astype(BF16)) + b_ref[...]


def ada_mod(c, ada_w, ada_b):
    R, D = c.shape
    L, _, N = ada_w.shape
    tn = 1536
    return pl.pallas_call(
        _ada_kernel,
        grid=(L, N // tn),
        in_specs=[pl.BlockSpec((R, D), lambda l, j: (0, 0)),
                  pl.BlockSpec((None, D, tn), lambda l, j: (l, 0, j)),
                  pl.BlockSpec((None, 1, tn), lambda l, j: (l, 0, j))],
        out_specs=pl.BlockSpec((None, R, tn), lambda l, j: (l, 0, j)),
        out_shape=jax.ShapeDtypeStruct((L, R, N), F32),
        compiler_params=_cparams(2),
        name="ada_mod",
    )(c, ada_w, ada_b.reshape(L, 1, N))


def _pack_pairs(y):
    half = y.shape[-1] // 2
    bits = lax.bitcast_convert_type(y.astype(BF16).astype(F32), jnp.uint32)
    word = lax.shift_right_logical(bits[:, :half], jnp.uint32(16)) | bits[:, half:]
    return lax.bitcast_convert_type(word, jnp.int32)


def _unpack_pairs(word, dtype=BF16):
    u = lax.bitcast_convert_type(word, jnp.uint32)
    lo = lax.bitcast_convert_type(lax.shift_left(u, jnp.uint32(16)), F32)
    hi = lax.bitcast_convert_type(u & jnp.uint32(0xFFFF0000), F32)
    return lo.astype(dtype), hi.astype(dtype)


def _norm_kernel(*refs, modulated, packed):
    if modulated:
        x_ref, g_ref, sc_ref, sh_ref, o_ref = refs
    else:
        x_ref, g_ref, o_ref = refs
    y = _rms(x_ref[...], g_ref[...])
    if modulated:
        y = y * (1.0 + sc_ref[...]) + sh_ref[...]
    if packed:
        bb, tt, D = y.shape
        o_ref[...] = _pack_pairs(y.reshape(bb * tt, D))
    else:
        o_ref[...] = y.astype(o_ref.dtype)


def norm_mod(x, g, mod=None, sc_idx=0, sh_idx=0, out_dtype=BF16, rows=512, packed=False):
    B, T, D = x.shape
    bb, tt, nblk, ij = _row_blocks(B, T, rows)
    xspec = pl.BlockSpec((bb, tt, D), lambda i: ij(i) + (0,))
    in_specs = [xspec, pl.BlockSpec((1, D), lambda i: (0, 0))]
    args = [x, g.reshape(1, D)]
    if mod is not None:
        in_specs += [pl.BlockSpec((bb, 1, D), lambda i: (ij(i)[0], 0, sc_idx)),
                     pl.BlockSpec((bb, 1, D), lambda i: (ij(i)[0], 0, sh_idx))]
        args += [mod, mod]
    if packed:
        out_specs = pl.BlockSpec((bb * tt, D // 2), lambda i: (i, 0))
        out_shape = jax.ShapeDtypeStruct((B * T, D // 2), jnp.int32)
    else:
        out_specs = xspec
        out_shape = jax.ShapeDtypeStruct((B, T, D), out_dtype)
    return pl.pallas_call(
        functools.partial(_norm_kernel, modulated=mod is not None, packed=packed),
        grid=(nblk,),
        in_specs=in_specs,
        out_specs=out_specs,
        out_shape=out_shape,
        compiler_params=_cparams(1),
        name="norm_mod",
    )(*args)


def _linear_kernel(*refs, residual):
    if residual:
        a_ref, w_ref, x_ref, gate_ref, o_ref, wb_ref = refs
    else:
        a_ref, w_ref, o_ref, wb_ref = refs

    @pl.when(pl.program_id(1) == 0)
    def _():
        wb_ref[...] = w_ref[...].astype(BF16)

    bb, tt, K = a_ref.shape
    y = _dot(a_ref[...].reshape(bb * tt, K).astype(BF16), wb_ref[...])
    y = y.reshape(bb, tt, y.shape[-1])
    if residual:
        y = x_ref[...] + gate_ref[...] * y
    o_ref[...] = y.astype(o_ref.dtype)


def linear(a, w, l, out_dtype, x=None, mod=None, gate_idx=0, rows=512, tn=1024):
    B, T, K = a.shape
    _, _, N = w.shape
    tn = min(tn, N)
    bb, tt, nblk, ij = _row_blocks(B, T, rows)
    in_specs = [pl.BlockSpec((bb, tt, K), lambda j, i: ij(i) + (0,)),
                pl.BlockSpec((None, K, tn), lambda j, i: (l, 0, j))]
    args = [a, w]
    ospec = pl.BlockSpec((bb, tt, tn), lambda j, i: ij(i) + (j,))
    if x is not None:
        gsteps = D_MODEL // tn
        in_specs += [ospec, pl.BlockSpec((bb, 1, tn), lambda j, i: (ij(i)[0], 0, gate_idx * gsteps + j))]
        args += [x, mod]
    return pl.pallas_call(
        functools.partial(_linear_kernel, residual=x is not None),
        grid=(N // tn, nblk),
        in_specs=in_specs,
        out_specs=ospec,
        out_shape=jax.ShapeDtypeStruct((B, T, N), out_dtype),
        scratch_shapes=[pltpu.VMEM((K, tn), BF16)],
        compiler_params=_cparams(2),
        name="linear",
    )(*args)


def _gla_kernel(*refs, L, n_chunks, has_init):
    if has_init:
        q_ref, f_ref, i_ref, g_ref, lb_ref, on_ref, s0_ref, o_ref, so_ref, st_ref = refs
    else:
        q_ref, f_ref, i_ref, g_ref, lb_ref, on_ref, o_ref, so_ref, st_ref = refs
    t = pl.program_id(1)
    H = st_ref.shape[0]

    @pl.when(t == 0)
    def _():
        for h in range(H):
            if has_init:
                st_ref[h] = s0_ref[0, h].T
            else:
                st_ref[h] = jnp.zeros(st_ref.shape[1:], F32)

    lb = lb_ref[...]
    onorm = on_ref[...]
    row = lax.broadcasted_iota(jnp.int32, (L, L), 0)
    col = lax.broadcasted_iota(jnp.int32, (L, L), 1)
    causal = col <= row
    tri = causal.astype(BF16)
    mid = L // 2 - 1

    def chunk(c, carry):
        r0 = pl.multiple_of(c * L, L)
        q = _silu(q_ref[0, pl.ds(r0, L), :])
        fg = lb + (1.0 - lb) * jax.nn.sigmoid(f_ref[0, pl.ds(r0, L), :])
        k = 1.0 - fg
        v = i_ref[0, pl.ds(r0, L), :].astype(BF16)
        gate = _silu(g_ref[0, pl.ds(r0, L), :])
        logf = jnp.log(fg)
        hi = logf.astype(BF16)
        lo = (logf - hi.astype(F32)).astype(BF16)
        b = _dot(tri, hi) + _dot(tri, lo)
        b_mid = b[mid:mid + 1, :]
        b_last = b[L - 1:L, :]
        up = jnp.exp(b - b_mid)
        down = jnp.exp(b_mid - b)
        qa = q * up
        kb = k * down
        qe = (qa * jnp.exp(b_mid)).astype(BF16)
        kd = (kb * jnp.exp(b_last - b_mid)).astype(BF16)
        qa = qa.astype(BF16)
        kb = kb.astype(BF16)
        decay = jnp.exp(b_last)
        sls = [slice(h * HG_DK, (h + 1) * HG_DK) for h in range(H)]
        sts = [st_ref[h] for h in range(H)]
        scores = [_dot_nt(qa[:, sl], kb[:, sl]) for sl in sls]
        inter = [_dot_nt(qe[:, sl], st.astype(BF16)) for sl, st in zip(sls, sts)]
        outer = [_dot_tn(v[:, sl], kd[:, sl]) for sl in sls]
        intra = [_dot(jnp.where(causal, sc, 0.0).astype(BF16), v[:, sl]) for sc, sl in zip(scores, sls)]
        for h, sl in enumerate(sls):
            st_ref[h] = sts[h] * decay[:, sl] + outer[h]
            o = _rms(inter[h] + intra[h], onorm[:, sl]) * gate[:, sl]
            o_ref[0, pl.ds(r0, L), sl] = o.astype(o_ref.dtype)
        return carry

    lax.fori_loop(0, n_chunks, chunk, 0)

    @pl.when(t == pl.num_programs(1) - 1)
    def _():
        for h in range(H):
            so_ref[0, h] = st_ref[h].T


def gla(z, lb, onorm_g, s0):
    B, T, _ = z.shape
    L = CHUNK if T % CHUNK == 0 else T
    tt = min(T, 512)
    n_chunks = tt // L
    H, D = HG_HEADS, D_MODEL

    def zspec(part):
        return pl.BlockSpec((1, tt, D), lambda b, t: (b, t, part))

    hspec = pl.BlockSpec((1, D), lambda b, t: (0, 0))
    sspec = pl.BlockSpec((1, H, HG_DK, HG_DV), lambda b, t: (b, 0, 0, 0))
    in_specs = [zspec(0), zspec(1), zspec(2), zspec(3), hspec, hspec]
    args = [z, z, z, z, lb.reshape(1, D), onorm_g.reshape(1, D)]
    if s0 is not None:
        in_specs.append(sspec)
        args.append(s0)
    return pl.pallas_call(
        functools.partial(_gla_kernel, L=L, n_chunks=n_chunks, has_init=s0 is not None),
        grid=(B, T // tt),
        in_specs=in_specs,
        out_specs=[pl.BlockSpec((1, tt, D), lambda b, t: (b, t, 0)), sspec],
        out_shape=[jax.ShapeDtypeStruct((B, T, D), BF16),
                   jax.ShapeDtypeStruct((B, H, HG_DK, HG_DV), F32)],
        scratch_shapes=[pltpu.VMEM((H, HG_DV, HG_DK), F32)],
        compiler_params=_cparams(2),
        name="gla",
    )(*args)


def _route_kernel(h_ref, rw_ref, bias_ref, pos_ref, w_ref, te_ref, nu_ref,
                  e_s, r_s, base_s, start_s, *, tile_rows):
    ph = pl.program_id(0)
    i = pl.program_id(1)
    M = h_ref.shape[0]
    half = h_ref.shape[1]
    G, E = N_GROUPS, N_EXPERTS // N_GROUPS
    e_flat = lax.broadcasted_iota(jnp.int32, (N_EXPERTS, M), 0)

    @pl.when(ph == 1)
    def _():
        @pl.when(i == 0)
        def _():
            cnt = base_s[...]
            padded = jnp.floor((cnt + (tile_rows - 1)) * (1.0 / tile_rows)) * tile_rows
            r = lax.broadcasted_iota(jnp.int32, (N_EXPERTS, N_EXPERTS), 0)
            c = lax.broadcasted_iota(jnp.int32, (N_EXPERTS, N_EXPERTS), 1)
            start = jnp.dot((c < r).astype(F32), padded, preferred_element_type=F32,
                            precision=lax.Precision.HIGHEST)
            start_s[...] = start
            te_ref[...] = (start * (1.0 / tile_rows)).astype(jnp.int32)
            nu_ref[...] = (padded * (1.0 / tile_rows)).astype(jnp.int32)

        start_col = start_s[:, :1]
        for k in range(TOP_K):
            hit = e_flat == e_s[i, k:k + 1, :]
            seg = jnp.sum(jnp.where(hit, start_col, 0.0), axis=0, keepdims=True)
            pos_ref[k:k + 1, :] = (seg + r_s[i, k:k + 1, :]).astype(jnp.int32)

    @pl.when(ph == 0)
    def _():
        _route_pass0(h_ref, rw_ref, bias_ref, w_ref, e_s, r_s, base_s, i, M, half, G, E)


def _route_pass0(h_ref, rw_ref, bias_ref, w_ref, e_s, r_s, base_s, i, M, half, G, E):
    @pl.when(i == 0)
    def _():
        base_s[...] = jnp.zeros_like(base_s)

    lo, hi = _unpack_pairs(h_ref[...])
    rw = rw_ref[...].astype(BF16)
    logits = _dot_nt(rw[:, :half], lo) + _dot_nt(rw[:, half:], hi)
    s = jax.nn.sigmoid(logits)
    sb = (s + bias_ref[...]).reshape(G, E, M)
    s = s.reshape(G, E, M)
    e_in = lax.broadcasted_iota(jnp.int32, (G, E, M), 1)
    g_id = lax.broadcasted_iota(jnp.int32, (G, 1, M), 0)
    e_id = lax.broadcasted_iota(jnp.int32, (G, E, M), 0) * E + e_in

    m1 = jnp.max(sb, axis=1, keepdims=True)
    first = jnp.min(jnp.where(sb == m1, e_in, E), axis=1, keepdims=True)
    m2 = jnp.max(jnp.where(e_in == first, NEG_INF, sb), axis=1, keepdims=True)
    gs = m1 + m2

    rank = jnp.zeros((G, 1, M), jnp.int32)
    for j in range(G):
        gj = gs[j:j + 1]
        beats = (gj > gs) | ((gj == gs) & (j < g_id))
        rank = rank + beats.astype(jnp.int32)
    gsel = rank < TOPK_GROUPS

    vals = jnp.where(gsel, sb, NEG_INF)
    w = jnp.zeros((G, E, M), F32)
    selm = jnp.zeros((G, E, M), F32)
    chosen = []
    for _ in range(TOP_K):
        m = jnp.max(jnp.max(vals, axis=1, keepdims=True), axis=0, keepdims=True)
        cand = jnp.where(vals == m, e_id, N_EXPERTS)
        first = jnp.min(jnp.min(cand, axis=1, keepdims=True), axis=0, keepdims=True)
        hit = e_id == first
        w = jnp.where(hit, s, w)
        selm = jnp.where(hit, 1.0, selm)
        vals = jnp.where(hit, NEG_INF, vals)
        chosen.append(first.reshape(1, M))

    tot = jnp.sum(jnp.sum(w, axis=1, keepdims=True), axis=0, keepdims=True)
    gates = (w / tot * ROUTED_SCALE).reshape(N_EXPERTS, M)
    selm = selm.reshape(N_EXPERTS, M)

    earlier = (lax.broadcasted_iota(jnp.int32, (M, M), 0)
               < lax.broadcasted_iota(jnp.int32, (M, M), 1)).astype(BF16)
    rank = base_s[:, :1] + _dot(selm.astype(BF16), earlier)
    base_s[...] = base_s[...] + jnp.sum(selm, axis=1, keepdims=True)
    e_flat = lax.broadcasted_iota(jnp.int32, (N_EXPERTS, M), 0)
    for k in range(TOP_K):
        hit = e_flat == chosen[k]
        e_s[i, k:k + 1, :] = chosen[k]
        r_s[i, k:k + 1, :] = jnp.sum(jnp.where(hit, rank, 0.0), axis=0, keepdims=True)
        w_ref[k:k + 1, :] = jnp.sum(jnp.where(hit, gates, 0.0), axis=0, keepdims=True)


def route(hp, router_w_t, router_bias, l, tile_rows, rows=512):
    N, half = hp.shape
    M = rows
    nT = N // M
    assert N % M == 0

    def p0(ph, i):
        return i * (1 - ph) + (nT - 1) * ph

    return pl.pallas_call(
        functools.partial(_route_kernel, tile_rows=tile_rows),
        grid=(2, nT),
        in_specs=[pl.BlockSpec((M, half), lambda ph, i: (p0(ph, i), 0)),
                  pl.BlockSpec((None, N_EXPERTS, 2 * half), lambda ph, i: (l, 0, 0)),
                  pl.BlockSpec((None, N_EXPERTS, 1), lambda ph, i: (l, 0, 0))],
        out_specs=[pl.BlockSpec((TOP_K, M), lambda ph, i: (0, i * ph)),
                   pl.BlockSpec((TOP_K, M), lambda ph, i: (0, p0(ph, i))),
                   pl.BlockSpec((N_EXPERTS, 128), lambda ph, i: (0, 0)),
                   pl.BlockSpec((N_EXPERTS, 128), lambda ph, i: (0, 0))],
        out_shape=[jax.ShapeDtypeStruct((TOP_K, N), jnp.int32),
                   jax.ShapeDtypeStruct((TOP_K, N), F32),
                   jax.ShapeDtypeStruct((N_EXPERTS, 128), jnp.int32),
                   jax.ShapeDtypeStruct((N_EXPERTS, 128), jnp.int32)],
        scratch_shapes=[pltpu.VMEM((nT, TOP_K, M), jnp.int32), pltpu.VMEM((nT, TOP_K, M), F32),
                        pltpu.VMEM((N_EXPERTS, 128), F32), pltpu.VMEM((N_EXPERTS, 128), F32)],
        compiler_params=_cparams(2),
        name="route",
    )(hp, router_w_t, router_bias.reshape(-1, N_EXPERTS, 1))


def _sc_mesh():
    return plsc.VectorSubcoreMesh(core_axis_name="core", subcore_axis_name="subcore")


def sc_invert(pos_flat, n_tok, n_out):
    n = pos_flat.shape[0]
    per = n_out // SC_WORKERS
    chunk = n_tok
    assert n_out % SC_WORKERS == 0 and per % SC_LANES == 0
    assert n_tok % chunk == 0 and n % chunk == 0 and chunk % SC_LANES == 0
    cp = pltpu.CompilerParams()
    if "needs_layout_passes" in pltpu.CompilerParams.__dataclass_fields__:
        cp = dataclasses.replace(cp, needs_layout_passes=False)

    @functools.partial(
        pl.kernel, out_type=jax.ShapeDtypeStruct((n_out,), jnp.int32), mesh=_sc_mesh(),
        scratch_types=[pltpu.VMEM((chunk,), jnp.int32), pltpu.VMEM((per,), jnp.int32)],
        compiler_params=cp, name="sc_invert")
    def k(pos_hbm, src_hbm, pos_v, src_v):
        wid = lax.axis_index("subcore") * SC_CORES + lax.axis_index("core")
        lo = wid * per
        lane = lax.iota(jnp.int32, SC_LANES)

        @pl.loop(0, per, step=SC_LANES)
        def _(r):
            src_v[pl.ds(r, SC_LANES)] = lax.rem(lo + r + lane, n_tok)

        @pl.loop(0, n // chunk)
        def _(c):
            base = c * chunk
            pltpu.sync_copy(pos_hbm.at[pl.ds(base, chunk)], pos_v)
            tok0 = lax.rem(base, n_tok)

            @pl.loop(0, chunk, step=SC_LANES)
            def _(r):
                p = pos_v[pl.ds(r, SC_LANES)] - lo
                mine = (p >= 0) & (p < per)
                plsc.store_scatter(src_v, [jnp.where(mine, p, 0)], tok0 + r + lane, mask=mine)

        pltpu.sync_copy(src_v, src_hbm.at[pl.ds(lo, per)])

    return k(pos_flat)


def sc_gather(x, idx):
    n = idx.shape[0]
    dim = x.shape[1]
    assert n % (SC_WINDOW * SC_WORKERS) == 0

    @functools.partial(
        pl.kernel, out_type=jax.ShapeDtypeStruct((n, dim), x.dtype), mesh=_sc_mesh(),
        scratch_types=[], name="sc_gather")
    def k(x_hbm, i_hbm, o_hbm):
        def body(i_vmem, o_vmem):
            pltpu.sync_copy(x_hbm.at[i_vmem.at[0]], o_vmem)

        pltpu.emit_pipeline(
            body, grid=(n // SC_WINDOW,),
            in_specs=[pl.BlockSpec((1, SC_WINDOW), index_map=lambda i: (i, 0))],
            out_specs=[pl.BlockSpec((SC_WINDOW, dim), index_map=lambda i: (i, 0))],
            core_axis_name=("core", "subcore"),
            dimension_semantics=(pltpu.PARALLEL,),
        )(i_hbm, o_hbm)

    return k(x, idx.reshape(n // SC_WINDOW, SC_WINDOW))


def _moe_gemm_kernel(ts_ref, tn_ref, x_hbm, wi_ref, wo_ref, o_hbm, wi_b, wo_b, xbuf, obuf, in_sem, out_sem,
                     *, tile_rows, n_tiles):
    e = pl.program_id(0)
    last = pl.num_programs(0) - 1
    t0 = ts_ref[e]
    n = tn_ref[e]
    n_used = ts_ref[last] + tn_ref[last]

    def x_copy(g, slot):
        rows = pl.ds(pl.multiple_of(g * tile_rows, tile_rows), tile_rows)
        return pltpu.make_async_copy(x_hbm.at[rows], xbuf.at[slot], in_sem.at[slot])

    def o_copy(g, slot):
        rows = pl.ds(pl.multiple_of(g * tile_rows, tile_rows), tile_rows)
        return pltpu.make_async_copy(obuf.at[slot], o_hbm.at[rows], out_sem.at[slot])

    @pl.when((e == 0) & (n_used > 0))
    def _():
        x_copy(0, 0).start()

    @pl.when(n > 0)
    def _():
        wi_b[...] = wi_ref[...].astype(BF16)
        wo_b[...] = wo_ref[...].astype(BF16)

    def tile(i, carry):
        g = t0 + i
        slot = lax.rem(g, 2)
        x_copy(g, slot).wait()

        @pl.when(g + 1 < n_used)
        def _():
            x_copy(g + 1, 1 - slot).start()

        @pl.when(g >= 2)
        def _():
            o_copy(g - 2, slot).wait()

        rows = tile_rows // MOE_SUB
        half = xbuf.shape[2]
        xs = [_unpack_pairs(xbuf[slot, r * rows:(r + 1) * rows, :]) for r in range(MOE_SUB)]
        hus = [_dot(lo, wi_b[:half, :]) + _dot(hi, wi_b[half:, :]) for lo, hi in xs]
        acts = [(_silu(hu[:, :EXPERT_FF]) * hu[:, EXPERT_FF:]).astype(BF16) for hu in hus]
        outs = [_dot(act, wo_b[...]) for act in acts]
        for r, out in enumerate(outs):
            obuf[slot, r * rows:(r + 1) * rows, :] = _pack_pairs(out)
        o_copy(g, slot).start()
        return carry

    lax.fori_loop(0, n, tile, 0)

    @pl.when(e == last)
    def _():
        @pl.when(n_used >= 2)
        def _():
            o_copy(n_used - 2, lax.rem(n_used, 2)).wait()

        @pl.when(n_used >= 1)
        def _():
            o_copy(n_used - 1, 1 - lax.rem(n_used, 2)).wait()

        obuf[0] = jnp.zeros(obuf.shape[1:], obuf.dtype)

        def clear(g, carry):
            cp = o_copy(g, 0)
            cp.start()
            cp.wait()
            return carry

        lax.fori_loop(n_used, n_tiles, clear, 0)


def moe_gemm(xs, tile_start, tile_count, exp_w_in, exp_w_out, l, tile_rows, n_tiles):
    P, half = xs.shape
    D = 2 * half
    assert P == n_tiles * tile_rows
    hbm = pl.BlockSpec(memory_space=pl.ANY)
    grid_spec = pltpu.PrefetchScalarGridSpec(
        num_scalar_prefetch=2,
        grid=(N_EXPERTS,),
        in_specs=[hbm,
                  pl.BlockSpec((None, None, D, 2 * EXPERT_FF), lambda e, ts, tn: (l, e, 0, 0)),
                  pl.BlockSpec((None, None, EXPERT_FF, D), lambda e, ts, tn: (l, e, 0, 0))],
        out_specs=hbm,
        scratch_shapes=[pltpu.VMEM((D, 2 * EXPERT_FF), BF16), pltpu.VMEM((EXPERT_FF, D), BF16),
                        pltpu.VMEM((2, tile_rows, half), jnp.int32), pltpu.VMEM((2, tile_rows, half), jnp.int32),
                        pltpu.SemaphoreType.DMA((2,)), pltpu.SemaphoreType.DMA((2,))],
    )
    return pl.pallas_call(
        functools.partial(_moe_gemm_kernel, tile_rows=tile_rows, n_tiles=n_tiles),
        grid_spec=grid_spec,
        out_shape=jax.ShapeDtypeStruct((P, half), jnp.int32),
        compiler_params=_cparams(1),
        name="moe_gemm",
    )(tile_start, tile_count, xs, exp_w_in, exp_w_out)


def _moe_combine_kernel(y_ref, w_ref, h_ref, si_ref, so_ref, x_ref, g2_ref, o_ref, si_b, so_b):
    @pl.when(pl.program_id(0) == 0)
    def _():
        si_b[...] = si_ref[...].astype(BF16)
        so_b[...] = so_ref[...].astype(BF16)

    bb, tt, D = x_ref.shape
    half = D // 2
    w = w_ref[...]
    acc_lo = jnp.zeros((bb * tt, half), F32)
    acc_hi = jnp.zeros((bb * tt, half), F32)
    for k in range(TOP_K):
        lo, hi = _unpack_pairs(y_ref[k], F32)
        acc_lo = acc_lo + w[:, k:k + 1] * lo
        acc_hi = acc_hi + w[:, k:k + 1] * hi
    hlo, hhi = _unpack_pairs(h_ref[...])
    hu = _dot(hlo, si_b[:half, :]) + _dot(hhi, si_b[half:, :])
    act = (_silu(hu[:, :SHARED_FF]) * hu[:, SHARED_FF:]).astype(BF16)
    y = jnp.concatenate([acc_lo, acc_hi], axis=-1) + _dot(act, so_b[...])
    o_ref[...] = x_ref[...] + g2_ref[...] * y.reshape(bb, tt, D)


def moe_combine(y8, w_t, hp, sh_w_in, sh_w_out, x, mod, gate_idx, l, row0, rows=256):
    B, T, D = x.shape
    half = D // 2
    bb, tt, nblk, ij = _row_blocks(B, T, rows)
    M = bb * tt
    assert row0 % M == 0
    off = row0 // M
    xspec = pl.BlockSpec((bb, tt, D), lambda i: ij(i) + (0,))
    return pl.pallas_call(
        _moe_combine_kernel,
        grid=(nblk,),
        in_specs=[pl.BlockSpec((TOP_K, M, half), lambda i: (0, off + i, 0)),
                  pl.BlockSpec((M, TOP_K), lambda i: (off + i, 0)),
                  pl.BlockSpec((M, half), lambda i: (off + i, 0)),
                  pl.BlockSpec((None, D, 2 * SHARED_FF), lambda i: (l, 0, 0)),
                  pl.BlockSpec((None, SHARED_FF, D), lambda i: (l, 0, 0)),
                  xspec,
                  pl.BlockSpec((bb, 1, D), lambda i: (ij(i)[0], 0, gate_idx))],
        out_specs=xspec,
        out_shape=jax.ShapeDtypeStruct((B, T, D), F32),
        scratch_shapes=[pltpu.VMEM((D, 2 * SHARED_FF), BF16), pltpu.VMEM((SHARED_FF, D), BF16)],
        compiler_params=_cparams(1),
        name="moe_combine",
    )(y8, w_t, hp, sh_w_in, sh_w_out, x, mod)


def _shared_kv_kernel(x_ref, g_ref, w_ref, lg_ref, cos_ref, sin_ref, lat_ref, kr_ref):
    bb, tt, D = x_ref.shape
    xn = _rms(x_ref[...], g_ref[...]).reshape(bb * tt, D).astype(BF16)
    z = _dot(xn, w_ref[...].astype(BF16))
    lat = _rms(z[:, :KV_LORA], lg_ref[...])
    lat_ref[...] = lat.reshape(bb, tt, KV_LORA)
    zr = z[:, KV_LORA:KV_LORA + MLA_ROPE].reshape(bb, tt, MLA_ROPE)
    zq = z[:, KV_LORA + 128:KV_LORA + 128 + MLA_ROPE].reshape(bb, tt, MLA_ROPE)
    kr_ref[...] = zr * cos_ref[...] + zq * sin_ref[...]


def shared_kv(x, kv_in_g, w_kv, kv_lat_g, cos32, sin32, rows=512):
    B, T, D = x.shape
    bb, tt, nblk, ij = _row_blocks(B, T, rows)
    tspec = pl.BlockSpec((tt, MLA_ROPE), lambda i: (ij(i)[1], 0))
    return pl.pallas_call(
        _shared_kv_kernel,
        grid=(nblk,),
        in_specs=[pl.BlockSpec((bb, tt, D), lambda i: ij(i) + (0,)),
                  pl.BlockSpec((1, D), lambda i: (0, 0)),
                  pl.BlockSpec(w_kv.shape, lambda i: (0, 0)),
                  pl.BlockSpec((1, KV_LORA), lambda i: (0, 0)),
                  tspec, tspec],
        out_specs=[pl.BlockSpec((bb, tt, KV_LORA), lambda i: ij(i) + (0,)),
                   pl.BlockSpec((bb, tt, MLA_ROPE), lambda i: ij(i) + (0,))],
        out_shape=[jax.ShapeDtypeStruct((B, T, KV_LORA), F32),
                   jax.ShapeDtypeStruct((B, T, MLA_ROPE), F32)],
        compiler_params=_cparams(1),
        name="shared_kv",
    )(x, kv_in_g.reshape(1, D), w_kv, kv_lat_g.reshape(1, KV_LORA), cos32, sin32)


def _kv_expand_kernel(lat_ref, kr_ref, wk_ref, ek_ref, wvt_ref, ones_ref, k_ref, vt_ref):
    lat = lat_ref[0].astype(BF16)
    kr = kr_ref[0].astype(BF16)
    k = _dot(lat, wk_ref[...].astype(BF16)) + _dot(kr, ek_ref[...].astype(BF16))
    k_ref[0] = k.astype(k_ref.dtype)
    vt = _dot_nt(wvt_ref[...].astype(BF16), lat) + ones_ref[...]
    vt_ref[0] = vt.astype(vt_ref.dtype)


def kv_expand(lat, kr, wk_pad, ek, wvt_ext, ones_col, rows=512):
    B, T, _ = lat.shape
    tt = rows
    NK, NVT = wk_pad.shape[1], wvt_ext.shape[0]

    def full(a):
        return pl.BlockSpec(a.shape, lambda b, t: (0, 0))

    def rowspec(n):
        return pl.BlockSpec((1, tt, n), lambda b, t: (b, t, 0))

    return pl.pallas_call(
        _kv_expand_kernel,
        grid=(B, T // tt),
        in_specs=[rowspec(KV_LORA), rowspec(MLA_ROPE), full(wk_pad), full(ek), full(wvt_ext), full(ones_col)],
        out_specs=[rowspec(NK), pl.BlockSpec((1, NVT, tt), lambda b, t: (b, 0, t))],
        out_shape=[jax.ShapeDtypeStruct((B, T, NK), BF16), jax.ShapeDtypeStruct((B, NVT, T), BF16)],
        compiler_params=_cparams(2),
        name="kv_expand",
    )(lat, kr, wk_pad, ek, wvt_ext, ones_col)


def _query_kernel(h_ref, wdq_ref, qg_ref, wq_ref, wqr_ref, c_ref, s_ref, q_ref, wdq_b, wq_b, wqr_b):
    @pl.when(pl.program_id(0) == 0)
    def _():
        wdq_b[...] = wdq_ref[...].astype(BF16)
        wq_b[...] = wq_ref[...].astype(BF16)
        wqr_b[...] = wqr_ref[...].astype(BF16)

    bb, tt, D = h_ref.shape
    h = h_ref[...].reshape(bb * tt, D)
    cq = _rms(_dot(h, wdq_b[...]), qg_ref[...]).astype(BF16)
    q1 = _dot(cq, wq_b[...]).reshape(bb, tt, -1)
    q2 = _dot(cq, wqr_b[...]).reshape(bb, tt, -1)
    c = c_ref[...]
    s = s_ref[...]
    for hd in range(MLA_HEADS):
        sl = slice(hd * HEAD_PAD, (hd + 1) * HEAD_PAD)
        q_ref[:, :, sl] = (q1[:, :, sl] * c + q2[:, :, sl] * s).astype(q_ref.dtype)


def mla_queries(h, w_dq, q_norm_g, wq_pad, wq_rot, l, c128, s128, rows=512):
    B, T, D = h.shape
    bb, tt, nblk, ij = _row_blocks(B, T, rows)
    NQ = wq_pad.shape[-1]
    tspec = pl.BlockSpec((tt, HEAD_PAD), lambda i: (ij(i)[1], 0))
    return pl.pallas_call(
        _query_kernel,
        grid=(nblk,),
        in_specs=[pl.BlockSpec((bb, tt, D), lambda i: ij(i) + (0,)),
                  pl.BlockSpec((None, D, Q_LORA), lambda i: (l, 0, 0)),
                  pl.BlockSpec((None, 1, Q_LORA), lambda i: (l, 0, 0)),
                  pl.BlockSpec((None, Q_LORA, NQ), lambda i: (l, 0, 0)),
                  pl.BlockSpec((None, Q_LORA, NQ), lambda i: (l, 0, 0)),
                  tspec, tspec],
        out_specs=pl.BlockSpec((bb, tt, NQ), lambda i: ij(i) + (0,)),
        out_shape=jax.ShapeDtypeStruct((B, T, NQ), BF16),
        scratch_shapes=[pltpu.VMEM((D, Q_LORA), BF16), pltpu.VMEM((Q_LORA, NQ), BF16),
                        pltpu.VMEM((Q_LORA, NQ), BF16)],
        compiler_params=_cparams(1),
        name="mla_queries",
    )(h, w_dq, q_norm_g.reshape(-1, 1, Q_LORA), wq_pad, wq_rot, c128, s128)


def _attn_prompt_kernel(qi_tab, ki_tab, q_ref, k_ref, vt_ref, o_ref, *scratch, tq, tk):
    H = MLA_HEADS
    m_refs, l_refs, acc_refs = scratch[:H], scratch[H:2 * H], scratch[2 * H:]
    p_id = pl.program_id(1)
    qi = qi_tab[p_id]
    ki = ki_tab[p_id]

    @pl.when(ki == 0)
    def _():
        for hd in range(H):
            m_refs[hd][...] = jnp.full(m_refs[hd].shape, NEG_INF, F32)
            l_refs[hd][...] = jnp.zeros(l_refs[hd].shape, F32)
            acc_refs[hd][...] = jnp.zeros(acc_refs[hd].shape, F32)

    def block(masked):
        if masked:
            kchunk = (ki * tk + lax.broadcasted_iota(jnp.int32, (tk, tq), 0)) // CHUNK
            qchunk = (qi * tq + lax.broadcasted_iota(jnp.int32, (tk, tq), 1)) // CHUNK
            mask = kchunk <= qchunk
        def scores(hd):
            sl = slice(hd * HEAD_PAD, (hd + 1) * HEAD_PAD)
            return _dot_nt(k_ref[0, :, sl], q_ref[0, :, sl])

        pending = [scores(hd) for hd in range(ATTN_LOOKAHEAD)]
        for hd in range(H):
            if hd + ATTN_LOOKAHEAD < H:
                pending.append(scores(hd + ATTN_LOOKAHEAD))
            s = pending.pop(0)
            if masked:
                s = jnp.where(mask, s, NEG_INF)
            m_prev = m_refs[hd][...]
            m_new = jnp.maximum(m_prev, jnp.max(s, axis=0, keepdims=True))
            a = jnp.exp2(m_prev - m_new)
            p = jnp.exp2(s - m_new).astype(BF16)
            pv = _dot(vt_ref[0, hd * V_ROWS:(hd + 1) * V_ROWS, :], p)
            acc_refs[hd][...] = a * acc_refs[hd][...] + pv[:MLA_V]
            l_refs[hd][...] = a * l_refs[hd][...] + pv[MLA_V:MLA_V + 1]
            m_refs[hd][...] = m_new

    @pl.when(ki < qi)
    def _():
        block(False)

    @pl.when(ki == qi)
    def _():
        block(True)
        o_t = jnp.concatenate([acc_refs[hd][...] / l_refs[hd][...] for hd in range(H)], axis=0)
        o_ref[0] = o_t.T.astype(o_ref.dtype)


def attn_prompt(q, k, vt, tq=256):
    B, T, NQ = q.shape
    NVT = vt.shape[1]
    NV = MLA_HEADS * MLA_V
    tk = tq
    assert tq % CHUNK == 0
    nq = T // tq
    pairs = [(a, b) for a in range(nq) for b in range(a + 1)]
    qi_tab = jnp.asarray([a for a, _ in pairs], jnp.int32)
    ki_tab = jnp.asarray([b for _, b in pairs], jnp.int32)
    grid_spec = pltpu.PrefetchScalarGridSpec(
        num_scalar_prefetch=2,
        grid=(B, len(pairs)),
        in_specs=[pl.BlockSpec((1, tq, NQ), lambda b, p, qt, kt: (b, qt[p], 0)),
                  pl.BlockSpec((1, tk, NQ), lambda b, p, qt, kt: (b, kt[p], 0)),
                  pl.BlockSpec((1, NVT, tk), lambda b, p, qt, kt: (b, 0, kt[p]))],
        out_specs=pl.BlockSpec((1, tq, NV), lambda b, p, qt, kt: (b, qt[p], 0)),
        scratch_shapes=([pltpu.VMEM((1, tq), F32)] * (2 * MLA_HEADS)
                        + [pltpu.VMEM((MLA_V, tq), F32)] * MLA_HEADS),
    )
    return pl.pallas_call(
        functools.partial(_attn_prompt_kernel, tq=tq, tk=tk),
        grid_spec=grid_spec,
        out_shape=jax.ShapeDtypeStruct((B, T, NV), BF16),
        compiler_params=_cparams(2),
        name="attn_prompt",
    )(qi_tab, ki_tab, q, k, vt)


def _absorb_kernel(q_ref, m_ref, o_ref):
    o_ref[...] = _dot(q_ref[...], m_ref[...].astype(BF16)).astype(o_ref.dtype)


def absorb_queries(q2d, m_abs):
    N = q2d.shape[0]
    H, _, W = m_abs.shape
    return pl.pallas_call(
        _absorb_kernel,
        grid=(H,),
        in_specs=[pl.BlockSpec((N, HEAD_PAD), lambda h: (0, h)),
                  pl.BlockSpec((None, HEAD_PAD, W), lambda h: (h, 0, 0))],
        out_specs=pl.BlockSpec((None, N, W), lambda h: (h, 0, 0)),
        out_shape=jax.ShapeDtypeStruct((H, N, W), BF16),
        compiler_params=_cparams(1),
        name="absorb_queries",
    )(q2d, m_abs)


def _attn_sample_kernel(q_ref, lat_ref, kr_ref, nlat_ref, nkr_ref, o_ref, m_ref, l_ref, acc_ref):
    kb = pl.program_id(1)
    H, Q, W = q_ref.shape
    q = q_ref[...].reshape(H * Q, W)
    q_lat = q[:, :KV_LORA]
    q_rope = q[:, KV_LORA:KV_LORA + MLA_ROPE]

    def update(lat, kr):
        lat = lat.astype(BF16)
        s = _dot_nt(q_lat, lat) + _dot_nt(q_rope, kr.astype(BF16))
        m_prev = m_ref[...]
        m_new = jnp.maximum(m_prev, jnp.max(s, axis=-1, keepdims=True))
        a = jnp.exp2(m_prev - m_new)
        p = jnp.exp2(s - m_new[:, :1])
        l_ref[...] = a * l_ref[...] + jnp.sum(p, axis=-1, keepdims=True)
        m_ref[...] = m_new
        acc_ref[...] = jnp.concatenate([a, a], axis=-1) * acc_ref[...] + _dot(p.astype(BF16), lat)

    @pl.when(kb == 0)
    def _():
        m_ref[...] = jnp.full_like(m_ref, NEG_INF)
        l_ref[...] = jnp.zeros_like(l_ref)
        acc_ref[...] = jnp.zeros_like(acc_ref)
        update(nlat_ref[0], nkr_ref[0])

    update(lat_ref[0], kr_ref[0])

    @pl.when(kb == pl.num_programs(1) - 1)
    def _():
        lsum = l_ref[...]
        o = acc_ref[...] / jnp.concatenate([lsum, lsum], axis=-1)
        o_ref[...] = o.reshape(H, Q, KV_LORA).astype(o_ref.dtype)


def attn_sample(q_abs, cache_lat, cache_kr, new_lat, new_kr, tk=1024):
    H, N, W = q_abs.shape
    B, P, _ = cache_lat.shape
    Q = new_lat.shape[1]
    qpos = P + np.arange(Q)
    kpos = np.arange(P + Q)
    assert bool(np.all((kpos // CHUNK)[None, :] <= (qpos // CHUNK)[:, None]))
    return pl.pallas_call(
        _attn_sample_kernel,
        grid=(B, P // tk),
        in_specs=[pl.BlockSpec((H, Q, W), lambda b, kb: (0, b, 0)),
                  pl.BlockSpec((1, tk, KV_LORA), lambda b, kb: (b, kb, 0)),
                  pl.BlockSpec((1, tk, MLA_ROPE), lambda b, kb: (b, kb, 0)),
                  pl.BlockSpec((1, Q, KV_LORA), lambda b, kb: (b, 0, 0)),
                  pl.BlockSpec((1, Q, MLA_ROPE), lambda b, kb: (b, 0, 0))],
        out_specs=pl.BlockSpec((H, Q, KV_LORA), lambda b, kb: (0, b, 0)),
        out_shape=jax.ShapeDtypeStruct((H, N, KV_LORA), BF16),
        scratch_shapes=[pltpu.VMEM((H * Q, 128), F32), pltpu.VMEM((H * Q, 128), F32),
                        pltpu.VMEM((H * Q, KV_LORA), F32)],
        compiler_params=_cparams(2),
        name="attn_sample",
    )(q_abs, cache_lat, cache_kr, new_lat, new_kr)


def _unabsorb_kernel(o_ref, w_ref, out_ref):
    out_ref[...] = (_dot(o_ref[0], w_ref[0].astype(BF16))
                    + _dot(o_ref[1], w_ref[1].astype(BF16))).astype(out_ref.dtype)


def unabsorb(o_lat, wuv_pad):
    H, N, R = o_lat.shape
    return pl.pallas_call(
        _unabsorb_kernel,
        grid=(H // 2,),
        in_specs=[pl.BlockSpec((2, N, R), lambda p: (p, 0, 0)),
                  pl.BlockSpec((2, R, 128), lambda p: (p, 0, 0))],
        out_specs=pl.BlockSpec((N, 128), lambda p: (0, p)),
        out_shape=jax.ShapeDtypeStruct((N, (H // 2) * 128), BF16),
        compiler_params=_cparams(1),
        name="unabsorb",
    )(o_lat, wuv_pad)


def _rope_tables(pos):
    half = MLA_ROPE // 2
    inv = 1.0 / (ROPE_THETA ** (np.arange(half, dtype=np.float64) * 2.0 / MLA_ROPE))
    ang = np.asarray(pos, np.float64)[:, None] * inv[None, :]
    cos = np.concatenate([np.cos(ang), np.cos(ang)], axis=-1)
    sin = np.concatenate([np.sin(ang), np.sin(ang)], axis=-1)
    T = cos.shape[0]
    c128 = np.zeros((T, HEAD_PAD)); s128 = np.zeros((T, HEAD_PAD))
    c128[:, :MLA_NOPE] = 1.0
    c128[:, MLA_NOPE:MLA_NOPE + MLA_ROPE] = cos
    s128[:, MLA_NOPE:MLA_NOPE + MLA_ROPE] = sin
    return (jnp.asarray(cos, F32), jnp.asarray(sin, F32),
            jnp.asarray(c128 * Q_PRESCALE, F32), jnp.asarray(s128 * Q_PRESCALE, F32))


def _rot_half_cols(w):
    half = w.shape[-1] // 2
    return jnp.concatenate([-w[..., half:], w[..., :half]], axis=-1)


def _prep_weights(w_dkv, w_uk, w_uv, w_uq, router_w):
    D = D_MODEL
    w_lat, w_rope = w_dkv[:, :KV_LORA], w_dkv[:, KV_LORA:]
    pad96 = jnp.zeros((D, 128 - MLA_ROPE), F32)
    w_kv = jnp.concatenate([w_lat, w_rope, pad96, _rot_half_cols(w_rope), pad96], axis=-1)

    zpad = HEAD_PAD - MLA_NOPE
    wk_pad = jnp.pad(w_uk, ((0, 0), (0, 0), (0, zpad))).reshape(KV_LORA, MLA_HEADS * HEAD_PAD)
    ek = jnp.zeros((MLA_ROPE, MLA_HEADS, HEAD_PAD), F32)
    ek = ek.at[:, :, MLA_NOPE:MLA_NOPE + MLA_ROPE].set(
        jnp.broadcast_to(jnp.eye(MLA_ROPE, dtype=F32)[:, None, :], (MLA_ROPE, MLA_HEADS, MLA_ROPE)))
    ek = ek.reshape(MLA_ROPE, MLA_HEADS * HEAD_PAD)
    wvt = jnp.transpose(w_uv, (1, 2, 0))
    wvt_ext = jnp.pad(wvt, ((0, 0), (0, V_ROWS - MLA_V), (0, 0))).reshape(MLA_HEADS * V_ROWS, KV_LORA)
    ones_col = jnp.tile((jnp.arange(V_ROWS) >= MLA_V).astype(F32), MLA_HEADS).reshape(-1, 1)

    nb = w_uq.shape[0]
    qn, qr = w_uq[..., :MLA_NOPE], w_uq[..., MLA_NOPE:]
    z32 = jnp.zeros(qr.shape[:-1] + (HEAD_PAD - MLA_NOPE - MLA_ROPE,), F32)
    wq_pad = jnp.concatenate([qn, qr, z32], axis=-1).reshape(nb, Q_LORA, MLA_HEADS * HEAD_PAD)
    wq_rot = jnp.concatenate([jnp.zeros_like(qn), _rot_half_cols(qr), z32], axis=-1)
    wq_rot = wq_rot.reshape(nb, Q_LORA, MLA_HEADS * HEAD_PAD)

    m_abs = jnp.zeros((MLA_HEADS, HEAD_PAD, KV_LORA + 128), F32)
    m_abs = m_abs.at[:, :MLA_NOPE, :KV_LORA].set(jnp.transpose(w_uk, (1, 2, 0)))
    m_abs = m_abs.at[:, MLA_NOPE:MLA_NOPE + MLA_ROPE, KV_LORA:KV_LORA + MLA_ROPE].set(
        jnp.broadcast_to(jnp.eye(MLA_ROPE, dtype=F32), (MLA_HEADS, MLA_ROPE, MLA_ROPE)))

    wuv_h = jnp.transpose(w_uv, (1, 0, 2))
    even = jnp.pad(wuv_h, ((0, 0), (0, 0), (0, 64)))
    odd = jnp.pad(wuv_h, ((0, 0), (0, 0), (64, 0)))
    wuv_pad = jnp.where((jnp.arange(MLA_HEADS) % 2 == 0)[:, None, None], even, odd)

    rw_t = jnp.transpose(router_w, (0, 2, 1))
    return dict(w_kv=w_kv, wk_pad=wk_pad, ek=ek, wvt_ext=wvt_ext, ones_col=ones_col, wq_pad=wq_pad, wq_rot=wq_rot,
                m_abs=m_abs, wuv_pad=wuv_pad, rw_t=rw_t)


def _mixer(st, l, P, W):
    x, m = st["x"], st["mod"][l]
    B, T, _ = x.shape
    n_a = P["hg_w_in"].shape[0]
    h = norm_mod(x, P["norm1_g"][l], m, sc_idx=1, sh_idx=0)
    if l < n_a:
        z = linear(h, P["hg_w_in"], l, F32)
        s0 = None if st["hg_state"] is None else st["hg_state"][l]
        o, s_new = gla(z, st["lbs"][l], P["hg_onorm_g"][l], s0)
        st["hg_new"].append(s_new)
        st["x"] = linear(o, P["hg_w_out"], l, F32, x=x, mod=m, gate_idx=2)
    else:
        bi = l - n_a
        q = mla_queries(h, P["w_dq"], P["q_norm_g"], W["wq_pad"], W["wq_rot"], bi, st["c128"], st["s128"])
        if st["past_lat"] is None:
            o = attn_prompt(q, st["k_all"], st["v_all"])
        else:
            q_abs = absorb_queries(q.reshape(B * T, -1), W["m_abs"])
            o_lat = attn_sample(q_abs, st["past_lat"], st["past_kr"], st["lat"], st["kr"])
            o = unabsorb(o_lat, W["wuv_pad"]).reshape(B, T, -1)
        st["x"] = linear(o, P["w_o"], bi, F32, x=x, mod=m, gate_idx=2)


def _moe(groups, l, P, W):
    hp = jnp.concatenate([norm_mod(st["x"], P["norm2_g"][l], st["mod"][l], sc_idx=4, sh_idx=3, packed=True)
                          for st in groups], axis=0)
    n_tok = hp.shape[0]
    n_tiles = (TOP_K * n_tok) // MOE_TILE + N_EXPERTS
    pos, w8, tile_start, tile_count = route(hp, W["rw_t"], P["router_bias"], l, MOE_TILE)
    pos_flat = pos.reshape(-1)
    src = sc_invert(pos_flat, n_tok, n_tiles * MOE_TILE)
    xs = sc_gather(hp, src)
    out = moe_gemm(xs, tile_start[:, 0], tile_count[:, 0], P["exp_w_in"], P["exp_w_out"], l,
                   MOE_TILE, n_tiles)
    y8 = sc_gather(out, pos_flat).reshape(TOP_K, n_tok, -1)
    w_t = w8.T
    row0 = 0
    for st in groups:
        B, T, _ = st["x"].shape
        st["x"] = moe_combine(y8, w_t, hp, P["sh_w_in"], P["sh_w_out"], st["x"], st["mod"][l], 5, l, row0)
        row0 += B * T


def _group_state(x, mod, pos, hg_state, past_lat, past_kr, lbs):
    cos32, sin32, c128, s128 = _rope_tables(pos)
    return dict(x=x, mod=mod, hg_state=hg_state, past_lat=past_lat, past_kr=past_kr, lbs=lbs,
                cos32=cos32, sin32=sin32, c128=c128, s128=s128, hg_new=[],
                lat=None, kr=None, k_all=None, v_all=None)


def kernel(x_prompt, x_sample, state_hgrn, cache_mla_latent, cache_mla_krope, c_prompt, c_sample, ada_w, ada_b, norm1_g, norm2_g, hg_w_in, hg_lb_logits, hg_onorm_g, hg_w_out, kv_in_g, w_dkv, kv_lat_g, w_uk, w_uv, w_dq, q_norm_g, w_uq, w_o, router_w, router_bias, exp_w_in, exp_w_out, sh_w_in, sh_w_out, final_g):
    Bp, Sp, _ = x_prompt.shape
    Bs, Ss, _ = x_sample.shape
    past = cache_mla_latent.shape[1]
    P = dict(norm1_g=norm1_g, norm2_g=norm2_g, hg_w_in=hg_w_in, hg_lb_logits=hg_lb_logits,
             hg_onorm_g=hg_onorm_g, hg_w_out=hg_w_out, kv_in_g=kv_in_g, kv_lat_g=kv_lat_g,
             w_dq=w_dq, q_norm_g=q_norm_g, w_o=w_o, router_bias=router_bias,
             exp_w_in=exp_w_in, exp_w_out=exp_w_out, sh_w_in=sh_w_in, sh_w_out=sh_w_out, final_g=final_g)
    W = _prep_weights(w_dkv, w_uk, w_uv, w_uq, router_w)
    mod = ada_mod(jnp.concatenate([c_prompt, c_sample], axis=0), ada_w, ada_b)
    lbs = jnp.cumsum(jax.nn.softmax(hg_lb_logits.astype(F32), axis=0), axis=0)
    bsz = Bp // PROMPT_STREAMS
    prompts = [_group_state(x_prompt[i * bsz:(i + 1) * bsz], mod[:, i * bsz:(i + 1) * bsz, None, :],
                            np.arange(Sp), None, None, None, lbs) for i in range(PROMPT_STREAMS)]
    gs = _group_state(x_sample, mod[:, Bp:, None, :], past + np.arange(Ss), state_hgrn,
                      cache_mla_latent, cache_mla_krope, lbs)
    streams = [[prompts[0], gs]] + [[g] for g in prompts[1:]]
    n_a = hg_w_in.shape[0]
    for l in range(norm1_g.shape[0]):
        for groups in streams:
            for st in groups:
                _mixer(st, l, P, W)
            _moe(groups, l, P, W)
            if l == n_a - 1:
                for st in groups:
                    st["lat"], st["kr"] = shared_kv(st["x"], kv_in_g, W["w_kv"], kv_lat_g, st["cos32"], st["sin32"])
                    if st["past_lat"] is None:
                        st["k_all"], st["v_all"] = kv_expand(st["lat"], st["kr"], W["wk_pad"], W["ek"],
                                                             W["wvt_ext"], W["ones_col"])
    for st in prompts + [gs]:
        st["y"] = norm_mod(st["x"], final_g, out_dtype=F32)
        st["hg_out"] = jnp.stack(st["hg_new"], axis=0)

    def cat(key, axis=0):
        return jnp.concatenate([g[key] for g in prompts], axis=axis)

    return (cat("y"), gs["y"], cat("hg_out", 1), gs["hg_out"], cat("lat"), cat("kr"), gs["lat"], gs["kr"])
```

```python
import dataclasses
import functools

import numpy as np
import jax
import jax.numpy as jnp
from jax import lax
from jax.experimental import pallas as pl
from jax.experimental.pallas import tpu as pltpu
from jax.experimental.pallas import tpu_sc as plsc

F32 = jnp.float32
BF16 = jnp.bfloat16

D_MODEL = 1024
CHUNK = 64
HG_HEADS = 8
HG_DK = 128
HG_DV = 128
MLA_HEADS = 16
MLA_NOPE = 64
MLA_ROPE = 32
MLA_V = 64
Q_LORA = 384
KV_LORA = 256
ROPE_THETA = 10000.0
N_EXPERTS = 64
TOP_K = 8
N_GROUPS = 8
TOPK_GROUPS = 4
EXPERT_FF = 256
SHARED_FF = 256
ROUTED_SCALE = 2.5
EPS = 1e-6

HEAD_PAD = 128
SAMPLE_KEY_SUB = 4
ATTN_LOOKAHEAD = 4
V_ROWS = MLA_V + 16
QK_SCALE = (MLA_NOPE + MLA_ROPE) ** -0.5
Q_PRESCALE = QK_SCALE * float(np.log2(np.e))
VMEM_LIMIT = 56 * 1024 * 1024
NEG_INF = float("-inf")
SC_CORES = 2
SC_SUBCORES = 16
SC_WORKERS = SC_CORES * SC_SUBCORES
SC_LANES = 16
SC_WINDOW = 64
MOE_TILE = 512
PROMPT_STREAMS = 1
MOE_NBUF = 4
MOE_SUB = 2


def _cparams(n_axes):
    return pltpu.CompilerParams(dimension_semantics=("arbitrary",) * n_axes,
                                vmem_limit_bytes=VMEM_LIMIT)


def _silu(x):
    return x * jax.nn.sigmoid(x)


def _rms(x, g):
    ms = jnp.mean(x * x, axis=-1, keepdims=True)
    return x * lax.rsqrt(ms + EPS) * g


def _dot(a, b):
    return jnp.dot(a, b, preferred_element_type=F32)


def _dot_nt(a, b):
    return lax.dot_general(a, b, (((1,), (1,)), ((), ())), preferred_element_type=F32)


def _dot_tn(a, b):
    return lax.dot_general(a, b, (((0,), (0,)), ((), ())), preferred_element_type=F32)


def _row_blocks(B, T, rows):
    if T >= rows:
        assert T % rows == 0
        bb, tt = 1, rows
    else:
        assert rows % T == 0 and B % (rows // T) == 0
        bb, tt = rows // T, T
    nt = T // tt
    return bb, tt, (B // bb) * nt, (lambda i: (i // nt, i % nt))


def _ada_kernel(c_ref, w_ref, b_ref, o_ref):
    a = _silu(c_ref[...]).astype(BF16)
    o_ref[...] = _dot(a, w_ref[...].astype(BF16)) + b_ref[...]


def ada_mod(c, ada_w, ada_b):
    R, D = c.shape
    L, _, N = ada_w.shape
    tn = 1536
    return pl.pallas_call(
        _ada_kernel,
        grid=(L, N // tn),
        in_specs=[pl.BlockSpec((R, D), lambda l, j: (0, 0)),
                  pl.BlockSpec((None, D, tn), lambda l, j: (l, 0, j)),
                  pl.BlockSpec((None, 1, tn), lambda l, j: (l, 0, j))],
        out_specs=pl.BlockSpec((None, R, tn), lambda l, j: (l, 0, j)),
        out_shape=jax.ShapeDtypeStruct((L, R, N), F32),
        compiler_params=_cparams(2),
        name="ada_mod",
    )(c, ada_w, ada_b.reshape(L, 1, N))


def _pack_pairs(y):
    half = y.shape[-1] // 2
    bits = lax.bitcast_convert_type(y.astype(BF16).astype(F32), jnp.uint32)
    word = lax.shift_right_logical(bits[:, :half], jnp.uint32(16)) | bits[:, half:]
    return lax.bitcast_convert_type(word, jnp.int32)


def _unpack_pairs(word, dtype=BF16):
    u = lax.bitcast_convert_type(word, jnp.uint32)
    lo = lax.bitcast_convert_type(lax.shift_left(u, jnp.uint32(16)), F32)
    hi = lax.bitcast_convert_type(u & jnp.uint32(0xFFFF0000), F32)
    return lo.astype(dtype), hi.astype(dtype)


def _norm_kernel(*refs, modulated, packed):
    if modulated:
        x_ref, g_ref, sc_ref, sh_ref, o_ref = refs
    else:
        x_ref, g_ref, o_ref = refs
    y = _rms(x_ref[...], g_ref[...])
    if modulated:
        y = y * (1.0 + sc_ref[...]) + sh_ref[...]
    if packed:
        bb, tt, D = y.shape
        o_ref[...] = _pack_pairs(y.reshape(bb * tt, D))
    else:
        o_ref[...] = y.astype(o_ref.dtype)


def norm_mod(x, g, mod=None, sc_idx=0, sh_idx=0, out_dtype=BF16, rows=512, packed=False):
    B, T, D = x.shape
    bb, tt, nblk, ij = _row_blocks(B, T, rows)
    xspec = pl.BlockSpec((bb, tt, D), lambda i: ij(i) + (0,))
    in_specs = [xspec, pl.BlockSpec((1, D), lambda i: (0, 0))]
    args = [x, g.reshape(1, D)]
    if mod is not None:
        in_specs += [pl.BlockSpec((bb, 1, D), lambda i: (ij(i)[0], 0, sc_idx)),
                     pl.BlockSpec((bb, 1, D), lambda i: (ij(i)[0], 0, sh_idx))]
        args += [mod, mod]
    if packed:
        out_specs = pl.BlockSpec((bb * tt, D // 2), lambda i: (i, 0))
        out_shape = jax.ShapeDtypeStruct((B * T, D // 2), jnp.int32)
    else:
        out_specs = xspec
        out_shape = jax.ShapeDtypeStruct((B, T, D), out_dtype)
    return pl.pallas_call(
        functools.partial(_norm_kernel, modulated=mod is not None, packed=packed),
        grid=(nblk,),
        in_specs=in_specs,
        out_specs=out_specs,
        out_shape=out_shape,
        compiler_params=_cparams(1),
        name="norm_mod",
    )(*args)


def _linear_kernel(*refs, residual):
    if residual:
        a_ref, w_ref, x_ref, gate_ref, o_ref, wb_ref = refs
    else:
        a_ref, w_ref, o_ref, wb_ref = refs

    @pl.when(pl.program_id(1) == 0)
    def _():
        wb_ref[...] = w_ref[...].astype(BF16)

    bb, tt, K = a_ref.shape
    y = _dot(a_ref[...].reshape(bb * tt, K).astype(BF16), wb_ref[...])
    y = y.reshape(bb, tt, y.shape[-1])
    if residual:
        y = x_ref[...] + gate_ref[...] * y
    o_ref[...] = y.astype(o_ref.dtype)


def linear(a, w, l, out_dtype, x=None, mod=None, gate_idx=0, rows=512, tn=1024):
    B, T, K = a.shape
    _, _, N = w.shape
    tn = min(tn, N)
    bb, tt, nblk, ij = _row_blocks(B, T, rows)
    in_specs = [pl.BlockSpec((bb, tt, K), lambda j, i: ij(i) + (0,)),
                pl.BlockSpec((None, K, tn), lambda j, i: (l, 0, j))]
    args = [a, w]
    ospec = pl.BlockSpec((bb, tt, tn), lambda j, i: ij(i) + (j,))
    if x is not None:
        gsteps = D_MODEL // tn
        in_specs += [ospec, pl.BlockSpec((bb, 1, tn), lambda j, i: (ij(i)[0], 0, gate_idx * gsteps + j))]
        args += [x, mod]
    return pl.pallas_call(
        functools.partial(_linear_kernel, residual=x is not None),
        grid=(N // tn, nblk),
        in_specs=in_specs,
        out_specs=ospec,
        out_shape=jax.ShapeDtypeStruct((B, T, N), out_dtype),
        scratch_shapes=[pltpu.VMEM((K, tn), BF16)],
        compiler_params=_cparams(2),
        name="linear",
    )(*args)


def _gla_kernel(*refs, L, n_chunks, has_init):
    if has_init:
        q_ref, f_ref, i_ref, g_ref, lb_ref, on_ref, s0_ref, o_ref, so_ref, st_ref = refs
    else:
        q_ref, f_ref, i_ref, g_ref, lb_ref, on_ref, o_ref, so_ref, st_ref = refs
    t = pl.program_id(1)
    H = st_ref.shape[0]

    @pl.when(t == 0)
    def _():
        for h in range(H):
            if has_init:
                st_ref[h] = s0_ref[0, h].T
            else:
                st_ref[h] = jnp.zeros(st_ref.shape[1:], F32)

    lb = lb_ref[...]
    onorm = on_ref[...]
    row = lax.broadcasted_iota(jnp.int32, (L, L), 0)
    col = lax.broadcasted_iota(jnp.int32, (L, L), 1)
    causal = col <= row
    tri = causal.astype(BF16)
    mid = L // 2 - 1

    def chunk(c, carry):
        r0 = pl.multiple_of(c * L, L)
        q = _silu(q_ref[0, pl.ds(r0, L), :])
        fg = lb + (1.0 - lb) * jax.nn.sigmoid(f_ref[0, pl.ds(r0, L), :])
        k = 1.0 - fg
        v = i_ref[0, pl.ds(r0, L), :].astype(BF16)
        gate = _silu(g_ref[0, pl.ds(r0, L), :])
        logf = jnp.log(fg)
        hi = logf.astype(BF16)
        lo = (logf - hi.astype(F32)).astype(BF16)
        b = _dot(tri, hi) + _dot(tri, lo)
        b_mid = b[mid:mid + 1, :]
        b_last = b[L - 1:L, :]
        up = jnp.exp(b - b_mid)
        down = jnp.exp(b_mid - b)
        qa = q * up
        kb = k * down
        qe = (qa * jnp.exp(b_mid)).astype(BF16)
        kd = (kb * jnp.exp(b_last - b_mid)).astype(BF16)
        qa = qa.astype(BF16)
        kb = kb.astype(BF16)
        decay = jnp.exp(b_last)
        sls = [slice(h * HG_DK, (h + 1) * HG_DK) for h in range(H)]
        sts = [st_ref[h] for h in range(H)]
        scores = [_dot_nt(qa[:, sl], kb[:, sl]) for sl in sls]
        inter = [_dot_nt(qe[:, sl], st.astype(BF16)) for sl, st in zip(sls, sts)]
        outer = [_dot_tn(v[:, sl], kd[:, sl]) for sl in sls]
        intra = [_dot(jnp.where(causal, sc, 0.0).astype(BF16), v[:, sl]) for sc, sl in zip(scores, sls)]
        for h, sl in enumerate(sls):
            st_ref[h] = sts[h] * decay[:, sl] + outer[h]
            o = _rms(inter[h] + intra[h], onorm[:, sl]) * gate[:, sl]
            o_ref[0, pl.ds(r0, L), sl] = o.astype(o_ref.dtype)
        return carry

    lax.fori_loop(0, n_chunks, chunk, 0)

    @pl.when(t == pl.num_programs(1) - 1)
    def _():
        for h in range(H):
            so_ref[0, h] = st_ref[h].T


def gla(z, lb, onorm_g, s0):
    B, T, _ = z.shape
    L = CHUNK if T % CHUNK == 0 else T
    tt = min(T, 512)
    n_chunks = tt // L
    H, D = HG_HEADS, D_MODEL

    def zspec(part):
        return pl.BlockSpec((1, tt, D), lambda b, t: (b, t, part))

    hspec = pl.BlockSpec((1, D), lambda b, t: (0, 0))
    sspec = pl.BlockSpec((1, H, HG_DK, HG_DV), lambda b, t: (b, 0, 0, 0))
    in_specs = [zspec(0), zspec(1), zspec(2), zspec(3), hspec, hspec]
    args = [z, z, z, z, lb.reshape(1, D), onorm_g.reshape(1, D)]
    if s0 is not None:
        in_specs.append(sspec)
        args.append(s0)
    return pl.pallas_call(
        functools.partial(_gla_kernel, L=L, n_chunks=n_chunks, has_init=s0 is not None),
        grid=(B, T // tt),
        in_specs=in_specs,
        out_specs=[pl.BlockSpec((1, tt, D), lambda b, t: (b, t, 0)), sspec],
        out_shape=[jax.ShapeDtypeStruct((B, T, D), BF16),
                   jax.ShapeDtypeStruct((B, H, HG_DK, HG_DV), F32)],
        scratch_shapes=[pltpu.VMEM((H, HG_DV, HG_DK), F32)],
        compiler_params=_cparams(2),
        name="gla",
    )(*args)


def _route_kernel(h_ref, rw_ref, bias_ref, pos_ref, w_ref, te_ref, nu_ref,
                  e_s, r_s, base_s, start_s, *, tile_rows):
    ph = pl.program_id(0)
    i = pl.program_id(1)
    M = h_ref.shape[0]
    half = h_ref.shape[1]
    G, E = N_GROUPS, N_EXPERTS // N_GROUPS
    e_flat = lax.broadcasted_iota(jnp.int32, (N_EXPERTS, M), 0)

    @pl.when(ph == 1)
    def _():
        @pl.when(i == 0)
        def _():
            cnt = base_s[...]
            padded = jnp.floor((cnt + (tile_rows - 1)) * (1.0 / tile_rows)) * tile_rows
            r = lax.broadcasted_iota(jnp.int32, (N_EXPERTS, N_EXPERTS), 0)
            c = lax.broadcasted_iota(jnp.int32, (N_EXPERTS, N_EXPERTS), 1)
            start = jnp.dot((c < r).astype(F32), padded, preferred_element_type=F32,
                            precision=lax.Precision.HIGHEST)
            start_s[...] = start
            te_ref[...] = (start * (1.0 / tile_rows)).astype(jnp.int32)
            nu_ref[...] = (padded * (1.0 / tile_rows)).astype(jnp.int32)

        start_col = start_s[:, :1]
        for k in range(TOP_K):
            hit = e_flat == e_s[i, k:k + 1, :]
            seg = jnp.sum(jnp.where(hit, start_col, 0.0), axis=0, keepdims=True)
            pos_ref[k:k + 1, :] = (seg + r_s[i, k:k + 1, :]).astype(jnp.int32)

    @pl.when(ph == 0)
    def _():
        _route_pass0(h_ref, rw_ref, bias_ref, w_ref, e_s, r_s, base_s, i, M, half, G, E)


def _route_pass0(h_ref, rw_ref, bias_ref, w_ref, e_s, r_s, base_s, i, M, half, G, E):
    @pl.when(i == 0)
    def _():
        base_s[...] = jnp.zeros_like(base_s)

    lo, hi = _unpack_pairs(h_ref[...])
    rw = rw_ref[...].astype(BF16)
    logits = _dot_nt(rw[:, :half], lo) + _dot_nt(rw[:, half:], hi)
    s = jax.nn.sigmoid(logits)
    sb = (s + bias_ref[...]).reshape(G, E, M)
    s = s.reshape(G, E, M)
    e_in = lax.broadcasted_iota(jnp.int32, (G, E, M), 1)
    g_id = lax.broadcasted_iota(jnp.int32, (G, 1, M), 0)
    e_id = lax.broadcasted_iota(jnp.int32, (G, E, M), 0) * E + e_in

    m1 = jnp.max(sb, axis=1, keepdims=True)
    first = jnp.min(jnp.where(sb == m1, e_in, E), axis=1, keepdims=True)
    m2 = jnp.max(jnp.where(e_in == first, NEG_INF, sb), axis=1, keepdims=True)
    gs = m1 + m2

    rank = jnp.zeros((G, 1, M), jnp.int32)
    for j in range(G):
        gj = gs[j:j + 1]
        beats = (gj > gs) | ((gj == gs) & (j < g_id))
        rank = rank + beats.astype(jnp.int32)
    gsel = rank < TOPK_GROUPS

    vals = jnp.where(gsel, sb, NEG_INF)
    w = jnp.zeros((G, E, M), F32)
    selm = jnp.zeros((G, E, M), F32)
    chosen = []
    for _ in range(TOP_K):
        m = jnp.max(jnp.max(vals, axis=1, keepdims=True), axis=0, keepdims=True)
        cand = jnp.where(vals == m, e_id, N_EXPERTS)
        first = jnp.min(jnp.min(cand, axis=1, keepdims=True), axis=0, keepdims=True)
        hit = e_id == first
        w = jnp.where(hit, s, w)
        selm = jnp.where(hit, 1.0, selm)
        vals = jnp.where(hit, NEG_INF, vals)
        chosen.append(first.reshape(1, M))

    tot = jnp.sum(jnp.sum(w, axis=1, keepdims=True), axis=0, keepdims=True)
    gates = (w / tot * ROUTED_SCALE).reshape(N_EXPERTS, M)
    selm = selm.reshape(N_EXPERTS, M)

    earlier = (lax.broadcasted_iota(jnp.int32, (M, M), 0)
               < lax.broadcasted_iota(jnp.int32, (M, M), 1)).astype(BF16)
    rank = base_s[:, :1] + _dot(selm.astype(BF16), earlier)
    base_s[...] = base_s[...] + jnp.sum(selm, axis=1, keepdims=True)
    e_flat = lax.broadcasted_iota(jnp.int32, (N_EXPERTS, M), 0)
    for k in range(TOP_K):
        hit = e_flat == chosen[k]
        e_s[i, k:k + 1, :] = chosen[k]
        r_s[i, k:k + 1, :] = jnp.sum(jnp.where(hit, rank, 0.0), axis=0, keepdims=True)
        w_ref[k:k + 1, :] = jnp.sum(jnp.where(hit, gates, 0.0), axis=0, keepdims=True)


def route(hp, router_w_t, router_bias, l, tile_rows, rows=512):
    N, half = hp.shape
    M = rows
    nT = N // M
    assert N % M == 0

    def p0(ph, i):
        return i * (1 - ph) + (nT - 1) * ph

    return pl.pallas_call(
        functools.partial(_route_kernel, tile_rows=tile_rows),
        grid=(2, nT),
        in_specs=[pl.BlockSpec((M, half), lambda ph, i: (p0(ph, i), 0)),
                  pl.BlockSpec((None, N_EXPERTS, 2 * half), lambda ph, i: (l, 0, 0)),
                  pl.BlockSpec((None, N_EXPERTS, 1), lambda ph, i: (l, 0, 0))],
        out_specs=[pl.BlockSpec((TOP_K, M), lambda ph, i: (0, i * ph)),
                   pl.BlockSpec((TOP_K, M), lambda ph, i: (0, p0(ph, i))),
                   pl.BlockSpec((N_EXPERTS, 128), lambda ph, i: (0, 0)),
                   pl.BlockSpec((N_EXPERTS, 128), lambda ph, i: (0, 0))],
        out_shape=[jax.ShapeDtypeStruct((TOP_K, N), jnp.int32),
                   jax.ShapeDtypeStruct((TOP_K, N), F32),
                   jax.ShapeDtypeStruct((N_EXPERTS, 128), jnp.int32),
                   jax.ShapeDtypeStruct((N_EXPERTS, 128), jnp.int32)],
        scratch_shapes=[pltpu.VMEM((nT, TOP_K, M), jnp.int32), pltpu.VMEM((nT, TOP_K, M), F32),
                        pltpu.VMEM((N_EXPERTS, 128), F32), pltpu.VMEM((N_EXPERTS, 128), F32)],
        compiler_params=_cparams(2),
        name="route",
    )(hp, router_w_t, router_bias.reshape(-1, N_EXPERTS, 1))


def _sc_mesh():
    return plsc.VectorSubcoreMesh(core_axis_name="core", subcore_axis_name="subcore")


def sc_invert(pos_flat, n_tok, n_out):
    n = pos_flat.shape[0]
    per = n_out // SC_WORKERS
    chunk = n_tok
    assert n_out % SC_WORKERS == 0 and per % SC_LANES == 0
    assert n_tok % chunk == 0 and n % chunk == 0 and chunk % SC_LANES == 0
    cp = pltpu.CompilerParams()
    if "needs_layout_passes" in pltpu.CompilerParams.__dataclass_fields__:
        cp = dataclasses.replace(cp, needs_layout_passes=False)

    @functools.partial(
        pl.kernel, out_type=jax.ShapeDtypeStruct((n_out,), jnp.int32), mesh=_sc_mesh(),
        scratch_types=[pltpu.VMEM((chunk,), jnp.int32), pltpu.VMEM((per,), jnp.int32)],
        compiler_params=cp, name="sc_invert")
    def k(pos_hbm, src_hbm, pos_v, src_v):
        wid = lax.axis_index("subcore") * SC_CORES + lax.axis_index("core")
        lo = wid * per
        lane = lax.iota(jnp.int32, SC_LANES)

        @pl.loop(0, per, step=SC_LANES)
        def _(r):
            src_v[pl.ds(r, SC_LANES)] = lax.rem(lo + r + lane, n_tok)

        @pl.loop(0, n // chunk)
        def _(c):
            base = c * chunk
            pltpu.sync_copy(pos_hbm.at[pl.ds(base, chunk)], pos_v)
            tok0 = lax.rem(base, n_tok)

            @pl.loop(0, chunk, step=SC_LANES)
            def _(r):
                p = pos_v[pl.ds(r, SC_LANES)] - lo
                mine = (p >= 0) & (p < per)
                plsc.store_scatter(src_v, [jnp.where(mine, p, 0)], tok0 + r + lane, mask=mine)

        pltpu.sync_copy(src_v, src_hbm.at[pl.ds(lo, per)])

    return k(pos_flat)


def sc_gather(x, idx):
    n = idx.shape[0]
    dim = x.shape[1]
    assert n % (SC_WINDOW * SC_WORKERS) == 0

    @functools.partial(
        pl.kernel, out_type=jax.ShapeDtypeStruct((n, dim), x.dtype), mesh=_sc_mesh(),
        scratch_types=[], name="sc_gather")
    def k(x_hbm, i_hbm, o_hbm):
        def body(i_vmem, o_vmem):
            pltpu.sync_copy(x_hbm.at[i_vmem.at[0]], o_vmem)

        pltpu.emit_pipeline(
            body, grid=(n // SC_WINDOW,),
            in_specs=[pl.BlockSpec((1, SC_WINDOW), index_map=lambda i: (i, 0))],
            out_specs=[pl.BlockSpec((SC_WINDOW, dim), index_map=lambda i: (i, 0))],
            core_axis_name=("core", "subcore"),
            dimension_semantics=(pltpu.PARALLEL,),
        )(i_hbm, o_hbm)

    return k(x, idx.reshape(n // SC_WINDOW, SC_WINDOW))


def _moe_gemm_kernel(ts_ref, tn_ref, x_hbm, wi_ref, wo_ref, o_hbm, wi_b, wo_b, xbuf, obuf, in_sem, out_sem,
                     *, tile_rows, n_tiles):
    e = pl.program_id(0)
    last = pl.num_programs(0) - 1
    t0 = ts_ref[e]
    n = tn_ref[e]
    n_used = ts_ref[last] + tn_ref[last]

    def x_copy(g, slot):
        rows = pl.ds(pl.multiple_of(g * tile_rows, tile_rows), tile_rows)
        return pltpu.make_async_copy(x_hbm.at[rows], xbuf.at[slot], in_sem.at[slot])

    def o_copy(g, slot):
        rows = pl.ds(pl.multiple_of(g * tile_rows, tile_rows), tile_rows)
        return pltpu.make_async_copy(obuf.at[slot], o_hbm.at[rows], out_sem.at[slot])

    @pl.when(e == 0)
    def _():
        for g0 in range(MOE_NBUF - 1):
            @pl.when(g0 < n_used)
            def _():
                x_copy(g0, g0).start()

    @pl.when(n > 0)
    def _():
        wi_b[...] = wi_ref[...].astype(BF16)
        wo_b[...] = wo_ref[...].astype(BF16)

    def tile(i, carry):
        g = t0 + i
        slot = lax.rem(g, MOE_NBUF)
        x_copy(g, slot).wait()
        ahead = g + (MOE_NBUF - 1)

        @pl.when(ahead < n_used)
        def _():
            x_copy(ahead, lax.rem(ahead, MOE_NBUF)).start()

        @pl.when(g >= MOE_NBUF)
        def _():
            o_copy(g - MOE_NBUF, slot).wait()

        rows = tile_rows // MOE_SUB
        half = xbuf.shape[2]
        xs = [_unpack_pairs(xbuf[slot, r * rows:(r + 1) * rows, :]) for r in range(MOE_SUB)]
        hus = [_dot(lo, wi_b[:half, :]) + _dot(hi, wi_b[half:, :]) for lo, hi in xs]
        acts = [(_silu(hu[:, :EXPERT_FF]) * hu[:, EXPERT_FF:]).astype(BF16) for hu in hus]
        outs = [_dot(act, wo_b[...]) for act in acts]
        for r, out in enumerate(outs):
            obuf[slot, r * rows:(r + 1) * rows, :] = _pack_pairs(out)
        o_copy(g, slot).start()
        return carry

    lax.fori_loop(0, n, tile, 0)

    @pl.when(e == last)
    def _():
        for back in range(MOE_NBUF, 0, -1):
            @pl.when(n_used >= back)
            def _():
                o_copy(n_used - back, lax.rem(n_used - back, MOE_NBUF)).wait()

        obuf[0] = jnp.zeros(obuf.shape[1:], obuf.dtype)

        def clear(g, carry):
            cp = o_copy(g, 0)
            cp.start()
            cp.wait()
            return carry

        lax.fori_loop(n_used, n_tiles, clear, 0)


def moe_gemm(xs, tile_start, tile_count, exp_w_in, exp_w_out, l, tile_rows, n_tiles):
    P, half = xs.shape
    D = 2 * half
    assert P == n_tiles * tile_rows
    hbm = pl.BlockSpec(memory_space=pl.ANY)
    grid_spec = pltpu.PrefetchScalarGridSpec(
        num_scalar_prefetch=2,
        grid=(N_EXPERTS,),
        in_specs=[hbm,
                  pl.BlockSpec((None, None, D, 2 * EXPERT_FF), lambda e, ts, tn: (l, e, 0, 0)),
                  pl.BlockSpec((None, None, EXPERT_FF, D), lambda e, ts, tn: (l, e, 0, 0))],
        out_specs=hbm,
        scratch_shapes=[pltpu.VMEM((D, 2 * EXPERT_FF), BF16), pltpu.VMEM((EXPERT_FF, D), BF16),
                        pltpu.VMEM((MOE_NBUF, tile_rows, half), jnp.int32),
                        pltpu.VMEM((MOE_NBUF, tile_rows, half), jnp.int32),
                        pltpu.SemaphoreType.DMA((MOE_NBUF,)), pltpu.SemaphoreType.DMA((MOE_NBUF,))],
    )
    return pl.pallas_call(
        functools.partial(_moe_gemm_kernel, tile_rows=tile_rows, n_tiles=n_tiles),
        grid_spec=grid_spec,
        out_shape=jax.ShapeDtypeStruct((P, half), jnp.int32),
        compiler_params=_cparams(1),
        name="moe_gemm",
    )(tile_start, tile_count, xs, exp_w_in, exp_w_out)


def _moe_combine_kernel(*refs, final):
    if final:
        y_ref, w_ref, h_ref, si_ref, so_ref, x_ref, g2_ref, fg_ref, o_ref, si_b, so_b = refs
    else:
        y_ref, w_ref, h_ref, si_ref, so_ref, x_ref, g2_ref, o_ref, si_b, so_b = refs

    @pl.when(pl.program_id(0) == 0)
    def _():
        si_b[...] = si_ref[...].astype(BF16)
        so_b[...] = so_ref[...].astype(BF16)

    bb, tt, D = x_ref.shape
    half = D // 2
    w = w_ref[...]
    acc_lo = jnp.zeros((bb * tt, half), F32)
    acc_hi = jnp.zeros((bb * tt, half), F32)
    for k in range(TOP_K):
        lo, hi = _unpack_pairs(y_ref[k], F32)
        acc_lo = acc_lo + w[:, k:k + 1] * lo
        acc_hi = acc_hi + w[:, k:k + 1] * hi
    hlo, hhi = _unpack_pairs(h_ref[...])
    hu = _dot(hlo, si_b[:half, :]) + _dot(hhi, si_b[half:, :])
    act = (_silu(hu[:, :SHARED_FF]) * hu[:, SHARED_FF:]).astype(BF16)
    y = jnp.concatenate([acc_lo, acc_hi], axis=-1) + _dot(act, so_b[...])
    x_new = x_ref[...] + g2_ref[...] * y.reshape(bb, tt, D)
    o_ref[...] = _rms(x_new, fg_ref[...]) if final else x_new


def moe_combine(y8, w_t, hp, sh_w_in, sh_w_out, x, mod, gate_idx, l, row0, final_g=None, rows=256):
    B, T, D = x.shape
    half = D // 2
    bb, tt, nblk, ij = _row_blocks(B, T, rows)
    M = bb * tt
    assert row0 % M == 0
    off = row0 // M
    xspec = pl.BlockSpec((bb, tt, D), lambda i: ij(i) + (0,))
    in_specs = [pl.BlockSpec((TOP_K, M, half), lambda i: (0, off + i, 0)),
                pl.BlockSpec((M, TOP_K), lambda i: (off + i, 0)),
                pl.BlockSpec((M, half), lambda i: (off + i, 0)),
                pl.BlockSpec((None, D, 2 * SHARED_FF), lambda i: (l, 0, 0)),
                pl.BlockSpec((None, SHARED_FF, D), lambda i: (l, 0, 0)),
                xspec,
                pl.BlockSpec((bb, 1, D), lambda i: (ij(i)[0], 0, gate_idx))]
    args = [y8, w_t, hp, sh_w_in, sh_w_out, x, mod]
    if final_g is not None:
        in_specs.append(pl.BlockSpec((1, D), lambda i: (0, 0)))
        args.append(final_g.reshape(1, D))
    return pl.pallas_call(
        functools.partial(_moe_combine_kernel, final=final_g is not None),
        grid=(nblk,),
        in_specs=in_specs,
        out_specs=xspec,
        out_shape=jax.ShapeDtypeStruct((B, T, D), F32),
        scratch_shapes=[pltpu.VMEM((D, 2 * SHARED_FF), BF16), pltpu.VMEM((SHARED_FF, D), BF16)],
        compiler_params=_cparams(1),
        name="moe_combine",
    )(*args)


def _shared_kv_kernel(x_ref, g_ref, w_ref, lg_ref, cos_ref, sin_ref, lat_ref, kr_ref):
    bb, tt, D = x_ref.shape
    xn = _rms(x_ref[...], g_ref[...]).reshape(bb * tt, D).astype(BF16)
    z = _dot(xn, w_ref[...].astype(BF16))
    lat = _rms(z[:, :KV_LORA], lg_ref[...])
    lat_ref[...] = lat.reshape(bb, tt, KV_LORA)
    zr = z[:, KV_LORA:KV_LORA + MLA_ROPE].reshape(bb, tt, MLA_ROPE)
    zq = z[:, KV_LORA + 128:KV_LORA + 128 + MLA_ROPE].reshape(bb, tt, MLA_ROPE)
    kr_ref[...] = zr * cos_ref[...] + zq * sin_ref[...]


def shared_kv(x, kv_in_g, w_kv, kv_lat_g, cos32, sin32, rows=512):
    B, T, D = x.shape
    bb, tt, nblk, ij = _row_blocks(B, T, rows)
    tspec = pl.BlockSpec((tt, MLA_ROPE), lambda i: (ij(i)[1], 0))
    return pl.pallas_call(
        _shared_kv_kernel,
        grid=(nblk,),
        in_specs=[pl.BlockSpec((bb, tt, D), lambda i: ij(i) + (0,)),
                  pl.BlockSpec((1, D), lambda i: (0, 0)),
                  pl.BlockSpec(w_kv.shape, lambda i: (0, 0)),
                  pl.BlockSpec((1, KV_LORA), lambda i: (0, 0)),
                  tspec, tspec],
        out_specs=[pl.BlockSpec((bb, tt, KV_LORA), lambda i: ij(i) + (0,)),
                   pl.BlockSpec((bb, tt, MLA_ROPE), lambda i: ij(i) + (0,))],
        out_shape=[jax.ShapeDtypeStruct((B, T, KV_LORA), F32),
                   jax.ShapeDtypeStruct((B, T, MLA_ROPE), F32)],
        compiler_params=_cparams(1),
        name="shared_kv",
    )(x, kv_in_g.reshape(1, D), w_kv, kv_lat_g.reshape(1, KV_LORA), cos32, sin32)


def _kv_expand_kernel(lat_ref, kr_ref, wk_ref, ek_ref, wvt_ref, ones_ref, k_ref, vt_ref):
    lat = lat_ref[0].astype(BF16)
    kr = kr_ref[0].astype(BF16)
    k = _dot(lat, wk_ref[...].astype(BF16)) + _dot(kr, ek_ref[...].astype(BF16))
    k_ref[0] = k.astype(k_ref.dtype)
    vt = _dot_nt(wvt_ref[...].astype(BF16), lat) + ones_ref[...]
    vt_ref[0] = vt.astype(vt_ref.dtype)


def kv_expand(lat, kr, wk_pad, ek, wvt_ext, ones_col, rows=512):
    B, T, _ = lat.shape
    tt = rows
    NK, NVT = wk_pad.shape[1], wvt_ext.shape[0]

    def full(a):
        return pl.BlockSpec(a.shape, lambda b, t: (0, 0))

    def rowspec(n):
        return pl.BlockSpec((1, tt, n), lambda b, t: (b, t, 0))

    return pl.pallas_call(
        _kv_expand_kernel,
        grid=(B, T // tt),
        in_specs=[rowspec(KV_LORA), rowspec(MLA_ROPE), full(wk_pad), full(ek), full(wvt_ext), full(ones_col)],
        out_specs=[rowspec(NK), pl.BlockSpec((1, NVT, tt), lambda b, t: (b, 0, t))],
        out_shape=[jax.ShapeDtypeStruct((B, T, NK), BF16), jax.ShapeDtypeStruct((B, NVT, T), BF16)],
        compiler_params=_cparams(2),
        name="kv_expand",
    )(lat, kr, wk_pad, ek, wvt_ext, ones_col)


def _query_kernel(h_ref, wdq_ref, qg_ref, wq_ref, wqr_ref, c_ref, s_ref, q_ref, wdq_b, wq_b, wqr_b):
    @pl.when(pl.program_id(0) == 0)
    def _():
        wdq_b[...] = wdq_ref[...].astype(BF16)
        wq_b[...] = wq_ref[...].astype(BF16)
        wqr_b[...] = wqr_ref[...].astype(BF16)

    bb, tt, D = h_ref.shape
    h = h_ref[...].reshape(bb * tt, D)
    cq = _rms(_dot(h, wdq_b[...]), qg_ref[...]).astype(BF16)
    q1 = _dot(cq, wq_b[...]).reshape(bb, tt, -1)
    q2 = _dot(cq, wqr_b[...]).reshape(bb, tt, -1)
    c = c_ref[...]
    s = s_ref[...]
    for hd in range(MLA_HEADS):
        sl = slice(hd * HEAD_PAD, (hd + 1) * HEAD_PAD)
        q_ref[:, :, sl] = (q1[:, :, sl] * c + q2[:, :, sl] * s).astype(q_ref.dtype)


def mla_queries(h, w_dq, q_norm_g, wq_pad, wq_rot, l, c128, s128, rows=512):
    B, T, D = h.shape
    bb, tt, nblk, ij = _row_blocks(B, T, rows)
    NQ = wq_pad.shape[-1]
    tspec = pl.BlockSpec((tt, HEAD_PAD), lambda i: (ij(i)[1], 0))
    return pl.pallas_call(
        _query_kernel,
        grid=(nblk,),
        in_specs=[pl.BlockSpec((bb, tt, D), lambda i: ij(i) + (0,)),
                  pl.BlockSpec((None, D, Q_LORA), lambda i: (l, 0, 0)),
                  pl.BlockSpec((None, 1, Q_LORA), lambda i: (l, 0, 0)),
                  pl.BlockSpec((None, Q_LORA, NQ), lambda i: (l, 0, 0)),
                  pl.BlockSpec((None, Q_LORA, NQ), lambda i: (l, 0, 0)),
                  tspec, tspec],
        out_specs=pl.BlockSpec((bb, tt, NQ), lambda i: ij(i) + (0,)),
        out_shape=jax.ShapeDtypeStruct((B, T, NQ), BF16),
        scratch_shapes=[pltpu.VMEM((D, Q_LORA), BF16), pltpu.VMEM((Q_LORA, NQ), BF16),
                        pltpu.VMEM((Q_LORA, NQ), BF16)],
        compiler_params=_cparams(1),
        name="mla_queries",
    )(h, w_dq, q_norm_g.reshape(-1, 1, Q_LORA), wq_pad, wq_rot, c128, s128)


def _attn_prompt_kernel(qi_tab, ki_tab, q_ref, k_ref, vt_ref, o_ref, *scratch, tq, tk):
    H = MLA_HEADS
    m_refs, l_refs, acc_refs = scratch[:H], scratch[H:2 * H], scratch[2 * H:]
    p_id = pl.program_id(1)
    qi = qi_tab[p_id]
    ki = ki_tab[p_id]

    @pl.when(ki == 0)
    def _():
        for hd in range(H):
            m_refs[hd][...] = jnp.full(m_refs[hd].shape, NEG_INF, F32)
            l_refs[hd][...] = jnp.zeros(l_refs[hd].shape, F32)
            acc_refs[hd][...] = jnp.zeros(acc_refs[hd].shape, F32)

    def block(masked):
        if masked:
            kchunk = (ki * tk + lax.broadcasted_iota(jnp.int32, (tk, tq), 0)) // CHUNK
            qchunk = (qi * tq + lax.broadcasted_iota(jnp.int32, (tk, tq), 1)) // CHUNK
            mask = kchunk <= qchunk
        def scores(hd):
            sl = slice(hd * HEAD_PAD, (hd + 1) * HEAD_PAD)
            return _dot_nt(k_ref[0, :, sl], q_ref[0, :, sl])

        pending = [scores(hd) for hd in range(ATTN_LOOKAHEAD)]
        for hd in range(H):
            if hd + ATTN_LOOKAHEAD < H:
                pending.append(scores(hd + ATTN_LOOKAHEAD))
            s = pending.pop(0)
            if masked:
                s = jnp.where(mask, s, NEG_INF)
            m_prev = m_refs[hd][...]
            m_new = jnp.maximum(m_prev, jnp.max(s, axis=0, keepdims=True))
            a = jnp.exp2(m_prev - m_new)
            p = jnp.exp2(s - m_new).astype(BF16)
            pv = _dot(vt_ref[0, hd * V_ROWS:(hd + 1) * V_ROWS, :], p)
            acc_refs[hd][...] = a * acc_refs[hd][...] + pv[:MLA_V]
            l_refs[hd][...] = a * l_refs[hd][...] + pv[MLA_V:MLA_V + 1]
            m_refs[hd][...] = m_new

    @pl.when(ki < qi)
    def _():
        block(False)

    @pl.when(ki == qi)
    def _():
        block(True)
        o_t = jnp.concatenate([acc_refs[hd][...] / l_refs[hd][...] for hd in range(H)], axis=0)
        o_ref[0] = o_t.T.astype(o_ref.dtype)


def attn_prompt(q, k, vt, tq=256):
    B, T, NQ = q.shape
    NVT = vt.shape[1]
    NV = MLA_HEADS * MLA_V
    tk = tq
    assert tq % CHUNK == 0
    nq = T // tq
    pairs = [(a, b) for a in range(nq) for b in range(a + 1)]
    qi_tab = jnp.asarray([a for a, _ in pairs], jnp.int32)
    ki_tab = jnp.asarray([b for _, b in pairs], jnp.int32)
    grid_spec = pltpu.PrefetchScalarGridSpec(
        num_scalar_prefetch=2,
        grid=(B, len(pairs)),
        in_specs=[pl.BlockSpec((1, tq, NQ), lambda b, p, qt, kt: (b, qt[p], 0)),
                  pl.BlockSpec((1, tk, NQ), lambda b, p, qt, kt: (b, kt[p], 0)),
                  pl.BlockSpec((1, NVT, tk), lambda b, p, qt, kt: (b, 0, kt[p]))],
        out_specs=pl.BlockSpec((1, tq, NV), lambda b, p, qt, kt: (b, qt[p], 0)),
        scratch_shapes=([pltpu.VMEM((1, tq), F32)] * (2 * MLA_HEADS)
                        + [pltpu.VMEM((MLA_V, tq), F32)] * MLA_HEADS),
    )
    return pl.pallas_call(
        functools.partial(_attn_prompt_kernel, tq=tq, tk=tk),
        grid_spec=grid_spec,
        out_shape=jax.ShapeDtypeStruct((B, T, NV), BF16),
        compiler_params=_cparams(2),
        name="attn_prompt",
    )(qi_tab, ki_tab, q, k, vt)


def _absorb_kernel(q_ref, m_ref, o_ref):
    o_ref[...] = _dot(q_ref[...], m_ref[...].astype(BF16)).astype(o_ref.dtype)


def absorb_queries(q2d, m_abs):
    N = q2d.shape[0]
    H, _, W = m_abs.shape
    return pl.pallas_call(
        _absorb_kernel,
        grid=(H,),
        in_specs=[pl.BlockSpec((N, HEAD_PAD), lambda h: (0, h)),
                  pl.BlockSpec((None, HEAD_PAD, W), lambda h: (h, 0, 0))],
        out_specs=pl.BlockSpec((None, N, W), lambda h: (h, 0, 0)),
        out_shape=jax.ShapeDtypeStruct((H, N, W), BF16),
        compiler_params=_cparams(1),
        name="absorb_queries",
    )(q2d, m_abs)


def _attn_sample_kernel(q_ref, lat_ref, kr_ref, nlat_ref, nkr_ref, o_ref, m_ref, l_ref, acc_ref):
    kb = pl.program_id(1)
    H, Q, W = q_ref.shape
    q = q_ref[...].reshape(H * Q, W)
    q_lat = q[:, :KV_LORA]
    q_rope = q[:, KV_LORA:KV_LORA + MLA_ROPE]

    def update(lat_tile, kr_tile, n_sub):
        sub = lat_tile.shape[0] // n_sub
        lats = [lat_tile[j * sub:(j + 1) * sub, :].astype(BF16) for j in range(n_sub)]
        krs = [kr_tile[j * sub:(j + 1) * sub, :].astype(BF16) for j in range(n_sub)]
        ss = [_dot_nt(q_lat, lat) + _dot_nt(q_rope, kr) for lat, kr in zip(lats, krs)]
        m_prev = m_ref[...]
        m_new = m_prev
        for s in ss:
            m_new = jnp.maximum(m_new, jnp.max(s, axis=-1, keepdims=True))
        a = jnp.exp2(m_prev - m_new)
        ps = [jnp.exp2(s - m_new[:, :1]) for s in ss]
        pv = _dot(ps[0].astype(BF16), lats[0])
        psum = jnp.sum(ps[0], axis=-1, keepdims=True)
        for p, lat in zip(ps[1:], lats[1:]):
            pv = pv + _dot(p.astype(BF16), lat)
            psum = psum + jnp.sum(p, axis=-1, keepdims=True)
        l_ref[...] = a * l_ref[...] + psum
        m_ref[...] = m_new
        acc_ref[...] = jnp.concatenate([a, a], axis=-1) * acc_ref[...] + pv

    @pl.when(kb == 0)
    def _():
        m_ref[...] = jnp.full_like(m_ref, NEG_INF)
        l_ref[...] = jnp.zeros_like(l_ref)
        acc_ref[...] = jnp.zeros_like(acc_ref)
        update(nlat_ref[0], nkr_ref[0], 1)

    update(lat_ref[0], kr_ref[0], SAMPLE_KEY_SUB)

    @pl.when(kb == pl.num_programs(1) - 1)
    def _():
        lsum = l_ref[...]
        o = acc_ref[...] / jnp.concatenate([lsum, lsum], axis=-1)
        o_ref[...] = o.reshape(H, Q, KV_LORA).astype(o_ref.dtype)


def attn_sample(q_abs, cache_lat, cache_kr, new_lat, new_kr, tk=2048):
    H, N, W = q_abs.shape
    B, P, _ = cache_lat.shape
    Q = new_lat.shape[1]
    qpos = P + np.arange(Q)
    kpos = np.arange(P + Q)
    assert bool(np.all((kpos // CHUNK)[None, :] <= (qpos // CHUNK)[:, None]))
    return pl.pallas_call(
        _attn_sample_kernel,
        grid=(B, P // tk),
        in_specs=[pl.BlockSpec((H, Q, W), lambda b, kb: (0, b, 0)),
                  pl.BlockSpec((1, tk, KV_LORA), lambda b, kb: (b, kb, 0)),
                  pl.BlockSpec((1, tk, MLA_ROPE), lambda b, kb: (b, kb, 0)),
                  pl.BlockSpec((1, Q, KV_LORA), lambda b, kb: (b, 0, 0)),
                  pl.BlockSpec((1, Q, MLA_ROPE), lambda b, kb: (b, 0, 0))],
        out_specs=pl.BlockSpec((H, Q, KV_LORA), lambda b, kb: (0, b, 0)),
        out_shape=jax.ShapeDtypeStruct((H, N, KV_LORA), BF16),
        scratch_shapes=[pltpu.VMEM((H * Q, 128), F32), pltpu.VMEM((H * Q, 128), F32),
                        pltpu.VMEM((H * Q, KV_LORA), F32)],
        compiler_params=_cparams(2),
        name="attn_sample",
    )(q_abs, cache_lat, cache_kr, new_lat, new_kr)


def _unabsorb_kernel(o_ref, w_ref, out_ref):
    out_ref[...] = (_dot(o_ref[0], w_ref[0].astype(BF16))
                    + _dot(o_ref[1], w_ref[1].astype(BF16))).astype(out_ref.dtype)


def unabsorb(o_lat, wuv_pad):
    H, N, R = o_lat.shape
    return pl.pallas_call(
        _unabsorb_kernel,
        grid=(H // 2,),
        in_specs=[pl.BlockSpec((2, N, R), lambda p: (p, 0, 0)),
                  pl.BlockSpec((2, R, 128), lambda p: (p, 0, 0))],
        out_specs=pl.BlockSpec((N, 128), lambda p: (0, p)),
        out_shape=jax.ShapeDtypeStruct((N, (H // 2) * 128), BF16),
        compiler_params=_cparams(1),
        name="unabsorb",
    )(o_lat, wuv_pad)


def _rope_tables(pos):
    half = MLA_ROPE // 2
    inv = 1.0 / (ROPE_THETA ** (np.arange(half, dtype=np.float64) * 2.0 / MLA_ROPE))
    ang = np.asarray(pos, np.float64)[:, None] * inv[None, :]
    cos = np.concatenate([np.cos(ang), np.cos(ang)], axis=-1)
    sin = np.concatenate([np.sin(ang), np.sin(ang)], axis=-1)
    T = cos.shape[0]
    c128 = np.zeros((T, HEAD_PAD)); s128 = np.zeros((T, HEAD_PAD))
    c128[:, :MLA_NOPE] = 1.0
    c128[:, MLA_NOPE:MLA_NOPE + MLA_ROPE] = cos
    s128[:, MLA_NOPE:MLA_NOPE + MLA_ROPE] = sin
    return (jnp.asarray(cos, F32), jnp.asarray(sin, F32),
            jnp.asarray(c128 * Q_PRESCALE, F32), jnp.asarray(s128 * Q_PRESCALE, F32))


def _rot_half_cols(w):
    half = w.shape[-1] // 2
    return jnp.concatenate([-w[..., half:], w[..., :half]], axis=-1)


def _prep_weights(w_dkv, w_uk, w_uv, w_uq, router_w):
    D = D_MODEL
    w_lat, w_rope = w_dkv[:, :KV_LORA], w_dkv[:, KV_LORA:]
    pad96 = jnp.zeros((D, 128 - MLA_ROPE), F32)
    w_kv = jnp.concatenate([w_lat, w_rope, pad96, _rot_half_cols(w_rope), pad96], axis=-1)

    zpad = HEAD_PAD - MLA_NOPE
    wk_pad = jnp.pad(w_uk, ((0, 0), (0, 0), (0, zpad))).reshape(KV_LORA, MLA_HEADS * HEAD_PAD)
    ek = jnp.zeros((MLA_ROPE, MLA_HEADS, HEAD_PAD), F32)
    ek = ek.at[:, :, MLA_NOPE:MLA_NOPE + MLA_ROPE].set(
        jnp.broadcast_to(jnp.eye(MLA_ROPE, dtype=F32)[:, None, :], (MLA_ROPE, MLA_HEADS, MLA_ROPE)))
    ek = ek.reshape(MLA_ROPE, MLA_HEADS * HEAD_PAD)
    wvt = jnp.transpose(w_uv, (1, 2, 0))
    wvt_ext = jnp.pad(wvt, ((0, 0), (0, V_ROWS - MLA_V), (0, 0))).reshape(MLA_HEADS * V_ROWS, KV_LORA)
    ones_col = jnp.tile((jnp.arange(V_ROWS) >= MLA_V).astype(F32), MLA_HEADS).reshape(-1, 1)

    nb = w_uq.shape[0]
    qn, qr = w_uq[..., :MLA_NOPE], w_uq[..., MLA_NOPE:]
    z32 = jnp.zeros(qr.shape[:-1] + (HEAD_PAD - MLA_NOPE - MLA_ROPE,), F32)
    wq_pad = jnp.concatenate([qn, qr, z32], axis=-1).reshape(nb, Q_LORA, MLA_HEADS * HEAD_PAD)
    wq_rot = jnp.concatenate([jnp.zeros_like(qn), _rot_half_cols(qr), z32], axis=-1)
    wq_rot = wq_rot.reshape(nb, Q_LORA, MLA_HEADS * HEAD_PAD)

    m_abs = jnp.zeros((MLA_HEADS, HEAD_PAD, KV_LORA + 128), F32)
    m_abs = m_abs.at[:, :MLA_NOPE, :KV_LORA].set(jnp.transpose(w_uk, (1, 2, 0)))
    m_abs = m_abs.at[:, MLA_NOPE:MLA_NOPE + MLA_ROPE, KV_LORA:KV_LORA + MLA_ROPE].set(
        jnp.broadcast_to(jnp.eye(MLA_ROPE, dtype=F32), (MLA_HEADS, MLA_ROPE, MLA_ROPE)))

    wuv_h = jnp.transpose(w_uv, (1, 0, 2))
    even = jnp.pad(wuv_h, ((0, 0), (0, 0), (0, 64)))
    odd = jnp.pad(wuv_h, ((0, 0), (0, 0), (64, 0)))
    wuv_pad = jnp.where((jnp.arange(MLA_HEADS) % 2 == 0)[:, None, None], even, odd)

    rw_t = jnp.transpose(router_w, (0, 2, 1))
    return dict(w_kv=w_kv, wk_pad=wk_pad, ek=ek, wvt_ext=wvt_ext, ones_col=ones_col, wq_pad=wq_pad, wq_rot=wq_rot,
                m_abs=m_abs, wuv_pad=wuv_pad, rw_t=rw_t)


def _mixer(st, l, P, W):
    x, m = st["x"], st["mod"][l]
    B, T, _ = x.shape
    n_a = P["hg_w_in"].shape[0]
    h = norm_mod(x, P["norm1_g"][l], m, sc_idx=1, sh_idx=0)
    if l < n_a:
        z = linear(h, P["hg_w_in"], l, F32)
        s0 = None if st["hg_state"] is None else st["hg_state"][l]
        o, s_new = gla(z, st["lbs"][l], P["hg_onorm_g"][l], s0)
        st["hg_new"].append(s_new)
        st["x"] = linear(o, P["hg_w_out"], l, F32, x=x, mod=m, gate_idx=2)
    else:
        bi = l - n_a
        q = mla_queries(h, P["w_dq"], P["q_norm_g"], W["wq_pad"], W["wq_rot"], bi, st["c128"], st["s128"])
        if st["past_lat"] is None:
            o = attn_prompt(q, st["k_all"], st["v_all"])
        else:
            q_abs = absorb_queries(q.reshape(B * T, -1), W["m_abs"])
            o_lat = attn_sample(q_abs, st["past_lat"], st["past_kr"], st["lat"], st["kr"])
            o = unabsorb(o_lat, W["wuv_pad"]).reshape(B, T, -1)
        st["x"] = linear(o, P["w_o"], bi, F32, x=x, mod=m, gate_idx=2)


def _moe(groups, l, P, W):
    hp = jnp.concatenate([norm_mod(st["x"], P["norm2_g"][l], st["mod"][l], sc_idx=4, sh_idx=3, packed=True)
                          for st in groups], axis=0)
    n_tok = hp.shape[0]
    n_tiles = (TOP_K * n_tok) // MOE_TILE + N_EXPERTS
    pos, w8, tile_start, tile_count = route(hp, W["rw_t"], P["router_bias"], l, MOE_TILE)
    pos_flat = pos.reshape(-1)
    src = sc_invert(pos_flat, n_tok, n_tiles * MOE_TILE)
    xs = sc_gather(hp, src)
    out = moe_gemm(xs, tile_start[:, 0], tile_count[:, 0], P["exp_w_in"], P["exp_w_out"], l,
                   MOE_TILE, n_tiles)
    y8 = sc_gather(out, pos_flat).reshape(TOP_K, n_tok, -1)
    w_t = w8.T
    row0 = 0
    for st in groups:
        B, T, _ = st["x"].shape
        final_g = P["final_g"] if l == P["norm1_g"].shape[0] - 1 else None
        st["x"] = moe_combine(y8, w_t, hp, P["sh_w_in"], P["sh_w_out"], st["x"], st["mod"][l], 5, l, row0,
                              final_g=final_g)
        row0 += B * T


def _group_state(x, mod, pos, hg_state, past_lat, past_kr, lbs):
    cos32, sin32, c128, s128 = _rope_tables(pos)
    return dict(x=x, mod=mod, hg_state=hg_state, past_lat=past_lat, past_kr=past_kr, lbs=lbs,
                cos32=cos32, sin32=sin32, c128=c128, s128=s128, hg_new=[],
                lat=None, kr=None, k_all=None, v_all=None)


def kernel(x_prompt, x_sample, state_hgrn, cache_mla_latent, cache_mla_krope, c_prompt, c_sample, ada_w, ada_b, norm1_g, norm2_g, hg_w_in, hg_lb_logits, hg_onorm_g, hg_w_out, kv_in_g, w_dkv, kv_lat_g, w_uk, w_uv, w_dq, q_norm_g, w_uq, w_o, router_w, router_bias, exp_w_in, exp_w_out, sh_w_in, sh_w_out, final_g):
    Bp, Sp, _ = x_prompt.shape
    Bs, Ss, _ = x_sample.shape
    past = cache_mla_latent.shape[1]
    P = dict(norm1_g=norm1_g, norm2_g=norm2_g, hg_w_in=hg_w_in, hg_lb_logits=hg_lb_logits,
             hg_onorm_g=hg_onorm_g, hg_w_out=hg_w_out, kv_in_g=kv_in_g, kv_lat_g=kv_lat_g,
             w_dq=w_dq, q_norm_g=q_norm_g, w_o=w_o, router_bias=router_bias,
             exp_w_in=exp_w_in, exp_w_out=exp_w_out, sh_w_in=sh_w_in, sh_w_out=sh_w_out, final_g=final_g)
    W = _prep_weights(w_dkv, w_uk, w_uv, w_uq, router_w)
    mod = ada_mod(jnp.concatenate([c_prompt, c_sample], axis=0), ada_w, ada_b)
    lbs = jnp.cumsum(jax.nn.softmax(hg_lb_logits.astype(F32), axis=0), axis=0)
    bsz = Bp // PROMPT_STREAMS
    prompts = [_group_state(x_prompt[i * bsz:(i + 1) * bsz], mod[:, i * bsz:(i + 1) * bsz, None, :],
                            np.arange(Sp), None, None, None, lbs) for i in range(PROMPT_STREAMS)]
    gs = _group_state(x_sample, mod[:, Bp:, None, :], past + np.arange(Ss), state_hgrn,
                      cache_mla_latent, cache_mla_krope, lbs)
    streams = [[prompts[0], gs]] + [[g] for g in prompts[1:]]
    n_a = hg_w_in.shape[0]
    for l in range(norm1_g.shape[0]):
        for groups in streams:
            for st in groups:
                _mixer(st, l, P, W)
            _moe(groups, l, P, W)
            if l == n_a - 1:
                for st in groups:
                    st["lat"], st["kr"] = shared_kv(st["x"], kv_in_g, W["w_kv"], kv_lat_g, st["cos32"], st["sin32"])
                    if st["past_lat"] is None:
                        st["k_all"], st["v_all"] = kv_expand(st["lat"], st["kr"], W["wk_pad"], W["ek"],
                                                             W["wvt_ext"], W["ones_col"])
    for st in prompts + [gs]:
        st["y"] = st["x"]
        st["hg_out"] = jnp.stack(st["hg_new"], axis=0)

    def cat(key, axis=0):
        return jnp.concatenate([g[key] for g in prompts], axis=axis)

    return (cat("y"), gs["y"], cat("hg_out", 1), gs["hg_out"], cat("lat"), cat("kr"), gs["lat"], gs["kr"])
```

```python
import dataclasses
import functools

import numpy as np
import jax
import jax.numpy as jnp
from jax import lax
from jax.experimental import pallas as pl
from jax.experimental.pallas import tpu as pltpu
from jax.experimental.pallas import tpu_sc as plsc

F32 = jnp.float32
BF16 = jnp.bfloat16

D_MODEL = 1024
CHUNK = 64
HG_HEADS = 8
HG_DK = 128
HG_DV = 128
MLA_HEADS = 16
MLA_NOPE = 64
MLA_ROPE = 32
MLA_V = 64
Q_LORA = 384
KV_LORA = 256
ROPE_THETA = 10000.0
N_EXPERTS = 64
TOP_K = 8
N_GROUPS = 8
TOPK_GROUPS = 4
EXPERT_FF = 256
SHARED_FF = 256
ROUTED_SCALE = 2.5
EPS = 1e-6

HEAD_PAD = 128
SAMPLE_KEY_SUB = 4
ATTN_LOOKAHEAD = 4
V_ROWS = MLA_V + 16
QK_SCALE = (MLA_NOPE + MLA_ROPE) ** -0.5
Q_PRESCALE = QK_SCALE * float(np.log2(np.e))
VMEM_LIMIT = 56 * 1024 * 1024
NEG_INF = float("-inf")
SC_CORES = 2
SC_SUBCORES = 16
SC_WORKERS = SC_CORES * SC_SUBCORES
SC_LANES = 16
SC_WINDOW = 64
MOE_TILE = 512
PROMPT_STREAMS = 1
MOE_NBUF = 4
MOE_SUB = 2


def _cparams(n_axes):
    return pltpu.CompilerParams(dimension_semantics=("arbitrary",) * n_axes,
                                vmem_limit_bytes=VMEM_LIMIT)


def _silu(x):
    return x * jax.nn.sigmoid(x)


def _rms(x, g):
    ms = jnp.mean(x * x, axis=-1, keepdims=True)
    return x * lax.rsqrt(ms + EPS) * g


def _dot(a, b):
    return jnp.dot(a, b, preferred_element_type=F32)


def _dot_nt(a, b):
    return lax.dot_general(a, b, (((1,), (1,)), ((), ())), preferred_element_type=F32)


def _dot_tn(a, b):
    return lax.dot_general(a, b, (((0,), (0,)), ((), ())), preferred_element_type=F32)


def _row_blocks(B, T, rows):
    if T >= rows:
        assert T % rows == 0
        bb, tt = 1, rows
    else:
        assert rows % T == 0 and B % (rows // T) == 0
        bb, tt = rows // T, T
    nt = T // tt
    return bb, tt, (B // bb) * nt, (lambda i: (i // nt, i % nt))


def _ada_kernel(c_ref, w_ref, b_ref, o_ref):
    a = _silu(c_ref[...]).astype(BF16)
    o_ref[...] = _dot(a, w_ref[...].astype(BF16)) + b_ref[...]


def ada_mod(c, ada_w, ada_b):
    R, D = c.shape
    L, _, N = ada_w.shape
    tn = 1536
    return pl.pallas_call(
        _ada_kernel,
        grid=(L, N // tn),
        in_specs=[pl.BlockSpec((R, D), lambda l, j: (0, 0)),
                  pl.BlockSpec((None, D, tn), lambda l, j: (l, 0, j)),
                  pl.BlockSpec((None, 1, tn), lambda l, j: (l, 0, j))],
        out_specs=pl.BlockSpec((None, R, tn), lambda l, j: (l, 0, j)),
        out_shape=jax.ShapeDtypeStruct((L, R, N), F32),
        compiler_params=_cparams(2),
        name="ada_mod",
    )(c, ada_w, ada_b.reshape(L, 1, N))


def _pack_pairs(y):
    half = y.shape[-1] // 2
    bits = lax.bitcast_convert_type(y.astype(BF16).astype(F32), jnp.uint32)
    word = lax.shift_right_logical(bits[:, :half], jnp.uint32(16)) | bits[:, half:]
    return lax.bitcast_convert_type(word, jnp.int32)


def _unpack_pairs(word, dtype=BF16):
    u = lax.bitcast_convert_type(word, jnp.uint32)
    lo = lax.bitcast_convert_type(lax.shift_left(u, jnp.uint32(16)), F32)
    hi = lax.bitcast_convert_type(u & jnp.uint32(0xFFFF0000), F32)
    return lo.astype(dtype), hi.astype(dtype)


def _norm_kernel(*refs, modulated, packed):
    if modulated:
        x_ref, g_ref, sc_ref, sh_ref, o_ref = refs
    else:
        x_ref, g_ref, o_ref = refs
    y = _rms(x_ref[...], g_ref[...])
    if modulated:
        y = y * (1.0 + sc_ref[...]) + sh_ref[...]
    if packed:
        bb, tt, D = y.shape
        o_ref[...] = _pack_pairs(y.reshape(bb * tt, D))
    else:
        o_ref[...] = y.astype(o_ref.dtype)


def norm_mod(x, g, mod=None, sc_idx=0, sh_idx=0, out_dtype=BF16, rows=512, packed=False):
    B, T, D = x.shape
    bb, tt, nblk, ij = _row_blocks(B, T, rows)
    xspec = pl.BlockSpec((bb, tt, D), lambda i: ij(i) + (0,))
    in_specs = [xspec, pl.BlockSpec((1, D), lambda i: (0, 0))]
    args = [x, g.reshape(1, D)]
    if mod is not None:
        in_specs += [pl.BlockSpec((bb, 1, D), lambda i: (ij(i)[0], 0, sc_idx)),
                     pl.BlockSpec((bb, 1, D), lambda i: (ij(i)[0], 0, sh_idx))]
        args += [mod, mod]
    if packed:
        out_specs = pl.BlockSpec((bb * tt, D // 2), lambda i: (i, 0))
        out_shape = jax.ShapeDtypeStruct((B * T, D // 2), jnp.int32)
    else:
        out_specs = xspec
        out_shape = jax.ShapeDtypeStruct((B, T, D), out_dtype)
    return pl.pallas_call(
        functools.partial(_norm_kernel, modulated=mod is not None, packed=packed),
        grid=(nblk,),
        in_specs=in_specs,
        out_specs=out_specs,
        out_shape=out_shape,
        compiler_params=_cparams(1),
        name="norm_mod",
    )(*args)


def _linear_kernel(*refs, residual, norm_next):
    if norm_next:
        a_ref, w_ref, x_ref, gate_ref, ng_ref, nsc_ref, nsh_ref, o_ref, hp_ref, wb_ref = refs
    elif residual:
        a_ref, w_ref, x_ref, gate_ref, o_ref, wb_ref = refs
    else:
        a_ref, w_ref, o_ref, wb_ref = refs

    @pl.when(pl.program_id(1) == 0)
    def _():
        wb_ref[...] = w_ref[...].astype(BF16)

    bb, tt, K = a_ref.shape
    y = _dot(a_ref[...].reshape(bb * tt, K).astype(BF16), wb_ref[...])
    y = y.reshape(bb, tt, y.shape[-1])
    if residual:
        y = x_ref[...] + gate_ref[...] * y
    o_ref[...] = y.astype(o_ref.dtype)
    if norm_next:
        h = _rms(y, ng_ref[...]) * (1.0 + nsc_ref[...]) + nsh_ref[...]
        hp_ref[...] = _pack_pairs(h.reshape(bb * tt, h.shape[-1]))


def linear(a, w, l, out_dtype, x=None, mod=None, gate_idx=0, rows=512, tn=1024, next_norm=None):
    B, T, K = a.shape
    _, _, N = w.shape
    tn = min(tn, N)
    bb, tt, nblk, ij = _row_blocks(B, T, rows)
    in_specs = [pl.BlockSpec((bb, tt, K), lambda j, i: ij(i) + (0,)),
                pl.BlockSpec((None, K, tn), lambda j, i: (l, 0, j))]
    args = [a, w]
    ospec = pl.BlockSpec((bb, tt, tn), lambda j, i: ij(i) + (j,))
    out_specs = ospec
    out_shape = jax.ShapeDtypeStruct((B, T, N), out_dtype)
    if x is not None:
        gsteps = D_MODEL // tn
        in_specs += [ospec, pl.BlockSpec((bb, 1, tn), lambda j, i: (ij(i)[0], 0, gate_idx * gsteps + j))]
        args += [x, mod]
    if next_norm is not None:
        assert x is not None and tn == N
        gain, sc_idx, sh_idx = next_norm
        in_specs += [pl.BlockSpec((1, N), lambda j, i: (0, 0)),
                     pl.BlockSpec((bb, 1, N), lambda j, i: (ij(i)[0], 0, sc_idx)),
                     pl.BlockSpec((bb, 1, N), lambda j, i: (ij(i)[0], 0, sh_idx))]
        args += [gain.reshape(1, N), mod, mod]
        out_specs = [ospec, pl.BlockSpec((bb * tt, N // 2), lambda j, i: (i, 0))]
        out_shape = [out_shape, jax.ShapeDtypeStruct((B * T, N // 2), jnp.int32)]
    return pl.pallas_call(
        functools.partial(_linear_kernel, residual=x is not None, norm_next=next_norm is not None),
        grid=(N // tn, nblk),
        in_specs=in_specs,
        out_specs=out_specs,
        out_shape=out_shape,
        scratch_shapes=[pltpu.VMEM((K, tn), BF16)],
        compiler_params=_cparams(2),
        name="linear",
    )(*args)


def _gla_kernel(*refs, L, n_chunks, has_init):
    if has_init:
        q_ref, f_ref, i_ref, g_ref, lb_ref, on_ref, s0_ref, o_ref, so_ref, st_ref = refs
    else:
        q_ref, f_ref, i_ref, g_ref, lb_ref, on_ref, o_ref, so_ref, st_ref = refs
    t = pl.program_id(1)
    H = st_ref.shape[0]

    @pl.when(t == 0)
    def _():
        for h in range(H):
            if has_init:
                st_ref[h] = s0_ref[0, h].T
            else:
                st_ref[h] = jnp.zeros(st_ref.shape[1:], F32)

    lb = lb_ref[...]
    onorm = on_ref[...]
    row = lax.broadcasted_iota(jnp.int32, (L, L), 0)
    col = lax.broadcasted_iota(jnp.int32, (L, L), 1)
    causal = col <= row
    tri = causal.astype(BF16)
    mid = L // 2 - 1

    def chunk(c, carry):
        r0 = pl.multiple_of(c * L, L)
        q = _silu(q_ref[0, pl.ds(r0, L), :])
        fg = lb + (1.0 - lb) * jax.nn.sigmoid(f_ref[0, pl.ds(r0, L), :])
        k = 1.0 - fg
        v = i_ref[0, pl.ds(r0, L), :].astype(BF16)
        gate = _silu(g_ref[0, pl.ds(r0, L), :])
        logf = jnp.log(fg)
        hi = logf.astype(BF16)
        lo = (logf - hi.astype(F32)).astype(BF16)
        b = _dot(tri, hi) + _dot(tri, lo)
        b_mid = b[mid:mid + 1, :]
        b_last = b[L - 1:L, :]
        up = jnp.exp(b - b_mid)
        down = jnp.exp(b_mid - b)
        qa = q * up
        kb = k * down
        qe = (qa * jnp.exp(b_mid)).astype(BF16)
        kd = (kb * jnp.exp(b_last - b_mid)).astype(BF16)
        qa = qa.astype(BF16)
        kb = kb.astype(BF16)
        decay = jnp.exp(b_last)
        sls = [slice(h * HG_DK, (h + 1) * HG_DK) for h in range(H)]
        sts = [st_ref[h] for h in range(H)]
        scores = [_dot_nt(qa[:, sl], kb[:, sl]) for sl in sls]
        inter = [_dot_nt(qe[:, sl], st.astype(BF16)) for sl, st in zip(sls, sts)]
        outer = [_dot_tn(v[:, sl], kd[:, sl]) for sl in sls]
        intra = [_dot(jnp.where(causal, sc, 0.0).astype(BF16), v[:, sl]) for sc, sl in zip(scores, sls)]
        for h, sl in enumerate(sls):
            st_ref[h] = sts[h] * decay[:, sl] + outer[h]
            o = _rms(inter[h] + intra[h], onorm[:, sl]) * gate[:, sl]
            o_ref[0, pl.ds(r0, L), sl] = o.astype(o_ref.dtype)
        return carry

    lax.fori_loop(0, n_chunks, chunk, 0)

    @pl.when(t == pl.num_programs(1) - 1)
    def _():
        for h in range(H):
            so_ref[0, h] = st_ref[h].T


def gla(z, lb, onorm_g, s0):
    B, T, _ = z.shape
    L = CHUNK if T % CHUNK == 0 else T
    tt = min(T, 512)
    n_chunks = tt // L
    H, D = HG_HEADS, D_MODEL

    def zspec(part):
        return pl.BlockSpec((1, tt, D), lambda b, t: (b, t, part))

    hspec = pl.BlockSpec((1, D), lambda b, t: (0, 0))
    sspec = pl.BlockSpec((1, H, HG_DK, HG_DV), lambda b, t: (b, 0, 0, 0))
    in_specs = [zspec(0), zspec(1), zspec(2), zspec(3), hspec, hspec]
    args = [z, z, z, z, lb.reshape(1, D), onorm_g.reshape(1, D)]
    if s0 is not None:
        in_specs.append(sspec)
        args.append(s0)
    return pl.pallas_call(
        functools.partial(_gla_kernel, L=L, n_chunks=n_chunks, has_init=s0 is not None),
        grid=(B, T // tt),
        in_specs=in_specs,
        out_specs=[pl.BlockSpec((1, tt, D), lambda b, t: (b, t, 0)), sspec],
        out_shape=[jax.ShapeDtypeStruct((B, T, D), BF16),
                   jax.ShapeDtypeStruct((B, H, HG_DK, HG_DV), F32)],
        scratch_shapes=[pltpu.VMEM((H, HG_DV, HG_DK), F32)],
        compiler_params=_cparams(2),
        name="gla",
    )(*args)


def _route_kernel(h_ref, rw_ref, bias_ref, pos_ref, w_ref, te_ref, nu_ref,
                  e_s, r_s, base_s, start_s, *, tile_rows):
    ph = pl.program_id(0)
    i = pl.program_id(1)
    M = h_ref.shape[0]
    half = h_ref.shape[1]
    G, E = N_GROUPS, N_EXPERTS // N_GROUPS
    e_flat = lax.broadcasted_iota(jnp.int32, (N_EXPERTS, M), 0)

    @pl.when(ph == 1)
    def _():
        @pl.when(i == 0)
        def _():
            cnt = base_s[...]
            padded = jnp.floor((cnt + (tile_rows - 1)) * (1.0 / tile_rows)) * tile_rows
            r = lax.broadcasted_iota(jnp.int32, (N_EXPERTS, N_EXPERTS), 0)
            c = lax.broadcasted_iota(jnp.int32, (N_EXPERTS, N_EXPERTS), 1)
            start = jnp.dot((c < r).astype(F32), padded, preferred_element_type=F32,
                            precision=lax.Precision.HIGHEST)
            start_s[...] = start
            te_ref[...] = (start * (1.0 / tile_rows)).astype(jnp.int32)
            nu_ref[...] = (padded * (1.0 / tile_rows)).astype(jnp.int32)

        start_col = start_s[:, :1]
        for k in range(TOP_K):
            hit = e_flat == e_s[i, k:k + 1, :]
            seg = jnp.sum(jnp.where(hit, start_col, 0.0), axis=0, keepdims=True)
            pos_ref[k:k + 1, :] = (seg + r_s[i, k:k + 1, :]).astype(jnp.int32)

    @pl.when(ph == 0)
    def _():
        _route_pass0(h_ref, rw_ref, bias_ref, w_ref, e_s, r_s, base_s, i, M, half, G, E)


def _route_pass0(h_ref, rw_ref, bias_ref, w_ref, e_s, r_s, base_s, i, M, half, G, E):
    @pl.when(i == 0)
    def _():
        base_s[...] = jnp.zeros_like(base_s)

    lo, hi = _unpack_pairs(h_ref[...])
    rw = rw_ref[...].astype(BF16)
    logits = _dot_nt(rw[:, :half], lo) + _dot_nt(rw[:, half:], hi)
    s = jax.nn.sigmoid(logits)
    sb = (s + bias_ref[...]).reshape(G, E, M)
    s = s.reshape(G, E, M)
    e_in = lax.broadcasted_iota(jnp.int32, (G, E, M), 1)
    g_id = lax.broadcasted_iota(jnp.int32, (G, 1, M), 0)
    e_id = lax.broadcasted_iota(jnp.int32, (G, E, M), 0) * E + e_in

    m1 = jnp.max(sb, axis=1, keepdims=True)
    first = jnp.min(jnp.where(sb == m1, e_in, E), axis=1, keepdims=True)
    m2 = jnp.max(jnp.where(e_in == first, NEG_INF, sb), axis=1, keepdims=True)
    gs = m1 + m2

    rank = jnp.zeros((G, 1, M), jnp.int32)
    for j in range(G):
        gj = gs[j:j + 1]
        beats = (gj > gs) | ((gj == gs) & (j < g_id))
        rank = rank + beats.astype(jnp.int32)
    gsel = rank < TOPK_GROUPS

    vals = jnp.where(gsel, sb, NEG_INF)
    w = jnp.zeros((G, E, M), F32)
    selm = jnp.zeros((G, E, M), F32)
    chosen = []
    for _ in range(TOP_K):
        m = jnp.max(jnp.max(vals, axis=1, keepdims=True), axis=0, keepdims=True)
        cand = jnp.where(vals == m, e_id, N_EXPERTS)
        first = jnp.min(jnp.min(cand, axis=1, keepdims=True), axis=0, keepdims=True)
        hit = e_id == first
        w = jnp.where(hit, s, w)
        selm = jnp.where(hit, 1.0, selm)
        vals = jnp.where(hit, NEG_INF, vals)
        chosen.append(first.reshape(1, M))

    tot = jnp.sum(jnp.sum(w, axis=1, keepdims=True), axis=0, keepdims=True)
    gates = (w / tot * ROUTED_SCALE).reshape(N_EXPERTS, M)
    selm = selm.reshape(N_EXPERTS, M)

    earlier = (lax.broadcasted_iota(jnp.int32, (M, M), 0)
               < lax.broadcasted_iota(jnp.int32, (M, M), 1)).astype(BF16)
    rank = base_s[:, :1] + _dot(selm.astype(BF16), earlier)
    base_s[...] = base_s[...] + jnp.sum(selm, axis=1, keepdims=True)
    e_flat = lax.broadcasted_iota(jnp.int32, (N_EXPERTS, M), 0)
    for k in range(TOP_K):
        hit = e_flat == chosen[k]
        e_s[i, k:k + 1, :] = chosen[k]
        r_s[i, k:k + 1, :] = jnp.sum(jnp.where(hit, rank, 0.0), axis=0, keepdims=True)
        w_ref[k:k + 1, :] = jnp.sum(jnp.where(hit, gates, 0.0), axis=0, keepdims=True)


def route(hp, router_w_t, router_bias, l, tile_rows, rows=512):
    N, half = hp.shape
    M = rows
    nT = N // M
    assert N % M == 0

    def p0(ph, i):
        return i * (1 - ph) + (nT - 1) * ph

    return pl.pallas_call(
        functools.partial(_route_kernel, tile_rows=tile_rows),
        grid=(2, nT),
        in_specs=[pl.BlockSpec((M, half), lambda ph, i: (p0(ph, i), 0)),
                  pl.BlockSpec((None, N_EXPERTS, 2 * half), lambda ph, i: (l, 0, 0)),
                  pl.BlockSpec((None, N_EXPERTS, 1), lambda ph, i: (l, 0, 0))],
        out_specs=[pl.BlockSpec((TOP_K, M), lambda ph, i: (0, i * ph)),
                   pl.BlockSpec((TOP_K, M), lambda ph, i: (0, p0(ph, i))),
                   pl.BlockSpec((N_EXPERTS, 128), lambda ph, i: (0, 0)),
                   pl.BlockSpec((N_EXPERTS, 128), lambda ph, i: (0, 0))],
        out_shape=[jax.ShapeDtypeStruct((TOP_K, N), jnp.int32),
                   jax.ShapeDtypeStruct((TOP_K, N), F32),
                   jax.ShapeDtypeStruct((N_EXPERTS, 128), jnp.int32),
                   jax.ShapeDtypeStruct((N_EXPERTS, 128), jnp.int32)],
        scratch_shapes=[pltpu.VMEM((nT, TOP_K, M), jnp.int32), pltpu.VMEM((nT, TOP_K, M), F32),
                        pltpu.VMEM((N_EXPERTS, 128), F32), pltpu.VMEM((N_EXPERTS, 128), F32)],
        compiler_params=_cparams(2),
        name="route",
    )(hp, router_w_t, router_bias.reshape(-1, N_EXPERTS, 1))


def _sc_mesh():
    return plsc.VectorSubcoreMesh(core_axis_name="core", subcore_axis_name="subcore")


def sc_invert(pos_flat, n_tok, n_out):
    n = pos_flat.shape[0]
    per = n_out // SC_WORKERS
    chunk = n_tok
    assert n_out % SC_WORKERS == 0 and per % SC_LANES == 0
    assert n_tok % chunk == 0 and n % chunk == 0 and chunk % SC_LANES == 0
    cp = pltpu.CompilerParams()
    if "needs_layout_passes" in pltpu.CompilerParams.__dataclass_fields__:
        cp = dataclasses.replace(cp, needs_layout_passes=False)

    @functools.partial(
        pl.kernel, out_type=jax.ShapeDtypeStruct((n_out,), jnp.int32), mesh=_sc_mesh(),
        scratch_types=[pltpu.VMEM((chunk,), jnp.int32), pltpu.VMEM((per,), jnp.int32)],
        compiler_params=cp, name="sc_invert")
    def k(pos_hbm, src_hbm, pos_v, src_v):
        wid = lax.axis_index("subcore") * SC_CORES + lax.axis_index("core")
        lo = wid * per
        lane = lax.iota(jnp.int32, SC_LANES)

        @pl.loop(0, per, step=SC_LANES)
        def _(r):
            src_v[pl.ds(r, SC_LANES)] = lax.rem(lo + r + lane, n_tok)

        @pl.loop(0, n // chunk)
        def _(c):
            base = c * chunk
            pltpu.sync_copy(pos_hbm.at[pl.ds(base, chunk)], pos_v)
            tok0 = lax.rem(base, n_tok)

            @plsc.parallel_loop(0, chunk, step=SC_LANES, unroll=8)
            def _(r):
                p = pos_v[pl.ds(r, SC_LANES)] - lo
                mine = (p >= 0) & (p < per)
                plsc.store_scatter(src_v, [jnp.where(mine, p, 0)], tok0 + r + lane, mask=mine)

        pltpu.sync_copy(src_v, src_hbm.at[pl.ds(lo, per)])

    return k(pos_flat)


def sc_gather(x, idx):
    n = idx.shape[0]
    dim = x.shape[1]
    assert n % (SC_WINDOW * SC_WORKERS) == 0

    @functools.partial(
        pl.kernel, out_type=jax.ShapeDtypeStruct((n, dim), x.dtype), mesh=_sc_mesh(),
        scratch_types=[], name="sc_gather")
    def k(x_hbm, i_hbm, o_hbm):
        def body(i_vmem, o_vmem):
            pltpu.sync_copy(x_hbm.at[i_vmem.at[0]], o_vmem)

        pltpu.emit_pipeline(
            body, grid=(n // SC_WINDOW,),
            in_specs=[pl.BlockSpec((1, SC_WINDOW), index_map=lambda i: (i, 0))],
            out_specs=[pl.BlockSpec((SC_WINDOW, dim), index_map=lambda i: (i, 0))],
            core_axis_name=("core", "subcore"),
            dimension_semantics=(pltpu.PARALLEL,),
        )(i_hbm, o_hbm)

    return k(x, idx.reshape(n // SC_WINDOW, SC_WINDOW))


def _moe_gemm_kernel(ts_ref, tn_ref, x_hbm, wi_ref, wo_ref, o_hbm, wi_b, wo_b, xbuf, obuf, in_sem, out_sem,
                     *, tile_rows, n_tiles):
    e = pl.program_id(0)
    last = pl.num_programs(0) - 1
    t0 = ts_ref[e]
    n = tn_ref[e]
    n_used = ts_ref[last] + tn_ref[last]

    def x_copy(g, slot):
        rows = pl.ds(pl.multiple_of(g * tile_rows, tile_rows), tile_rows)
        return pltpu.make_async_copy(x_hbm.at[rows], xbuf.at[slot], in_sem.at[slot])

    def o_copy(g, slot):
        rows = pl.ds(pl.multiple_of(g * tile_rows, tile_rows), tile_rows)
        return pltpu.make_async_copy(obuf.at[slot], o_hbm.at[rows], out_sem.at[slot])

    @pl.when(e == 0)
    def _():
        for g0 in range(MOE_NBUF - 1):
            @pl.when(g0 < n_used)
            def _():
                x_copy(g0, g0).start()

    @pl.when(n > 0)
    def _():
        wi_b[...] = wi_ref[...].astype(BF16)
        wo_b[...] = wo_ref[...].astype(BF16)

    def tile(i, carry):
        g = t0 + i
        slot = lax.rem(g, MOE_NBUF)
        x_copy(g, slot).wait()
        ahead = g + (MOE_NBUF - 1)

        @pl.when(ahead < n_used)
        def _():
            x_copy(ahead, lax.rem(ahead, MOE_NBUF)).start()

        @pl.when(g >= MOE_NBUF)
        def _():
            o_copy(g - MOE_NBUF, slot).wait()

        rows = tile_rows // MOE_SUB
        half = xbuf.shape[2]
        xs = [_unpack_pairs(xbuf[slot, r * rows:(r + 1) * rows, :]) for r in range(MOE_SUB)]
        hus = [_dot(lo, wi_b[:half, :]) + _dot(hi, wi_b[half:, :]) for lo, hi in xs]
        acts = [(_silu(hu[:, :EXPERT_FF]) * hu[:, EXPERT_FF:]).astype(BF16) for hu in hus]
        outs = [_dot(act, wo_b[...]) for act in acts]
        for r, out in enumerate(outs):
            obuf[slot, r * rows:(r + 1) * rows, :] = _pack_pairs(out)
        o_copy(g, slot).start()
        return carry

    lax.fori_loop(0, n, tile, 0)

    @pl.when(e == last)
    def _():
        for back in range(MOE_NBUF, 0, -1):
            @pl.when(n_used >= back)
            def _():
                o_copy(n_used - back, lax.rem(n_used - back, MOE_NBUF)).wait()

        obuf[0] = jnp.zeros(obuf.shape[1:], obuf.dtype)

        def clear(g, carry):
            cp = o_copy(g, 0)
            cp.start()
            cp.wait()
            return carry

        lax.fori_loop(n_used, n_tiles, clear, 0)


def moe_gemm(xs, tile_start, tile_count, exp_w_in, exp_w_out, l, tile_rows, n_tiles):
    P, half = xs.shape
    D = 2 * half
    assert P == n_tiles * tile_rows
    hbm = pl.BlockSpec(memory_space=pl.ANY)
    grid_spec = pltpu.PrefetchScalarGridSpec(
        num_scalar_prefetch=2,
        grid=(N_EXPERTS,),
        in_specs=[hbm,
                  pl.BlockSpec((None, None, D, 2 * EXPERT_FF), lambda e, ts, tn: (l, e, 0, 0)),
                  pl.BlockSpec((None, None, EXPERT_FF, D), lambda e, ts, tn: (l, e, 0, 0))],
        out_specs=hbm,
        scratch_shapes=[pltpu.VMEM((D, 2 * EXPERT_FF), BF16), pltpu.VMEM((EXPERT_FF, D), BF16),
                        pltpu.VMEM((MOE_NBUF, tile_rows, half), jnp.int32),
                        pltpu.VMEM((MOE_NBUF, tile_rows, half), jnp.int32),
                        pltpu.SemaphoreType.DMA((MOE_NBUF,)), pltpu.SemaphoreType.DMA((MOE_NBUF,))],
    )
    return pl.pallas_call(
        functools.partial(_moe_gemm_kernel, tile_rows=tile_rows, n_tiles=n_tiles),
        grid_spec=grid_spec,
        out_shape=jax.ShapeDtypeStruct((P, half), jnp.int32),
        compiler_params=_cparams(1),
        name="moe_gemm",
    )(tile_start, tile_count, xs, exp_w_in, exp_w_out)


def _moe_combine_kernel(*refs, final, norm_next):
    if final:
        y_ref, w_ref, h_ref, si_ref, so_ref, x_ref, g2_ref, fg_ref, o_ref, si_b, so_b = refs
    elif norm_next:
        (y_ref, w_ref, h_ref, si_ref, so_ref, x_ref, g2_ref, ng_ref, nsc_ref, nsh_ref,
         o_ref, hn_ref, si_b, so_b) = refs
    else:
        y_ref, w_ref, h_ref, si_ref, so_ref, x_ref, g2_ref, o_ref, si_b, so_b = refs

    @pl.when(pl.program_id(0) == 0)
    def _():
        si_b[...] = si_ref[...].astype(BF16)
        so_b[...] = so_ref[...].astype(BF16)

    bb, tt, D = x_ref.shape
    half = D // 2
    w = w_ref[...]
    acc_lo = jnp.zeros((bb * tt, half), F32)
    acc_hi = jnp.zeros((bb * tt, half), F32)
    for k in range(TOP_K):
        lo, hi = _unpack_pairs(y_ref[k], F32)
        acc_lo = acc_lo + w[:, k:k + 1] * lo
        acc_hi = acc_hi + w[:, k:k + 1] * hi
    hlo, hhi = _unpack_pairs(h_ref[...])
    hu = _dot(hlo, si_b[:half, :]) + _dot(hhi, si_b[half:, :])
    act = (_silu(hu[:, :SHARED_FF]) * hu[:, SHARED_FF:]).astype(BF16)
    y = jnp.concatenate([acc_lo, acc_hi], axis=-1) + _dot(act, so_b[...])
    x_new = x_ref[...] + g2_ref[...] * y.reshape(bb, tt, D)
    o_ref[...] = _rms(x_new, fg_ref[...]) if final else x_new
    if norm_next:
        hn = _rms(x_new, ng_ref[...]) * (1.0 + nsc_ref[...]) + nsh_ref[...]
        hn_ref[...] = hn.astype(hn_ref.dtype)


def moe_combine(y8, w_t, hp, sh_w_in, sh_w_out, x, mod, gate_idx, l, row0, final_g=None, next_norm=None,
                rows=256):
    B, T, D = x.shape
    half = D // 2
    bb, tt, nblk, ij = _row_blocks(B, T, rows)
    M = bb * tt
    assert row0 % M == 0
    off = row0 // M
    xspec = pl.BlockSpec((bb, tt, D), lambda i: ij(i) + (0,))
    in_specs = [pl.BlockSpec((TOP_K, M, half), lambda i: (0, off + i, 0)),
                pl.BlockSpec((M, TOP_K), lambda i: (off + i, 0)),
                pl.BlockSpec((M, half), lambda i: (off + i, 0)),
                pl.BlockSpec((None, D, 2 * SHARED_FF), lambda i: (l, 0, 0)),
                pl.BlockSpec((None, SHARED_FF, D), lambda i: (l, 0, 0)),
                xspec,
                pl.BlockSpec((bb, 1, D), lambda i: (ij(i)[0], 0, gate_idx))]
    args = [y8, w_t, hp, sh_w_in, sh_w_out, x, mod]
    out_specs = xspec
    out_shape = jax.ShapeDtypeStruct((B, T, D), F32)
    if final_g is not None:
        assert next_norm is None
        in_specs.append(pl.BlockSpec((1, D), lambda i: (0, 0)))
        args.append(final_g.reshape(1, D))
    if next_norm is not None:
        gain, mod_next, sc_idx, sh_idx = next_norm
        in_specs += [pl.BlockSpec((1, D), lambda i: (0, 0)),
                     pl.BlockSpec((bb, 1, D), lambda i: (ij(i)[0], 0, sc_idx)),
                     pl.BlockSpec((bb, 1, D), lambda i: (ij(i)[0], 0, sh_idx))]
        args += [gain.reshape(1, D), mod_next, mod_next]
        out_specs = [xspec, xspec]
        out_shape = [out_shape, jax.ShapeDtypeStruct((B, T, D), BF16)]
    return pl.pallas_call(
        functools.partial(_moe_combine_kernel, final=final_g is not None, norm_next=next_norm is not None),
        grid=(nblk,),
        in_specs=in_specs,
        out_specs=out_specs,
        out_shape=out_shape,
        scratch_shapes=[pltpu.VMEM((D, 2 * SHARED_FF), BF16), pltpu.VMEM((SHARED_FF, D), BF16)],
        compiler_params=_cparams(1),
        name="moe_combine",
    )(*args)


def _shared_kv_kernel(x_ref, g_ref, w_ref, lg_ref, cos_ref, sin_ref, lat_ref, kr_ref):
    bb, tt, D = x_ref.shape
    xn = _rms(x_ref[...], g_ref[...]).reshape(bb * tt, D).astype(BF16)
    z = _dot(xn, w_ref[...].astype(BF16))
    lat = _rms(z[:, :KV_LORA], lg_ref[...])
    lat_ref[...] = lat.reshape(bb, tt, KV_LORA)
    zr = z[:, KV_LORA:KV_LORA + MLA_ROPE].reshape(bb, tt, MLA_ROPE)
    zq = z[:, KV_LORA + 128:KV_LORA + 128 + MLA_ROPE].reshape(bb, tt, MLA_ROPE)
    kr_ref[...] = zr * cos_ref[...] + zq * sin_ref[...]


def shared_kv(x, kv_in_g, w_kv, kv_lat_g, cos32, sin32, rows=512):
    B, T, D = x.shape
    bb, tt, nblk, ij = _row_blocks(B, T, rows)
    tspec = pl.BlockSpec((tt, MLA_ROPE), lambda i: (ij(i)[1], 0))
    return pl.pallas_call(
        _shared_kv_kernel,
        grid=(nblk,),
        in_specs=[pl.BlockSpec((bb, tt, D), lambda i: ij(i) + (0,)),
                  pl.BlockSpec((1, D), lambda i: (0, 0)),
                  pl.BlockSpec(w_kv.shape, lambda i: (0, 0)),
                  pl.BlockSpec((1, KV_LORA), lambda i: (0, 0)),
                  tspec, tspec],
        out_specs=[pl.BlockSpec((bb, tt, KV_LORA), lambda i: ij(i) + (0,)),
                   pl.BlockSpec((bb, tt, MLA_ROPE), lambda i: ij(i) + (0,))],
        out_shape=[jax.ShapeDtypeStruct((B, T, KV_LORA), F32),
                   jax.ShapeDtypeStruct((B, T, MLA_ROPE), F32)],
        compiler_params=_cparams(1),
        name="shared_kv",
    )(x, kv_in_g.reshape(1, D), w_kv, kv_lat_g.reshape(1, KV_LORA), cos32, sin32)


def _kv_expand_kernel(lat_ref, kr_ref, wk_ref, ek_ref, wvt_ref, ones_ref, k_ref, vt_ref):
    lat = lat_ref[0].astype(BF16)
    kr = kr_ref[0].astype(BF16)
    k = _dot(lat, wk_ref[...].astype(BF16)) + _dot(kr, ek_ref[...].astype(BF16))
    k_ref[0] = k.astype(k_ref.dtype)
    vt = _dot_nt(wvt_ref[...].astype(BF16), lat) + ones_ref[...]
    vt_ref[0] = vt.astype(vt_ref.dtype)


def kv_expand(lat, kr, wk_pad, ek, wvt_ext, ones_col, rows=512):
    B, T, _ = lat.shape
    tt = rows
    NK, NVT = wk_pad.shape[1], wvt_ext.shape[0]

    def full(a):
        return pl.BlockSpec(a.shape, lambda b, t: (0, 0))

    def rowspec(n):
        return pl.BlockSpec((1, tt, n), lambda b, t: (b, t, 0))

    return pl.pallas_call(
        _kv_expand_kernel,
        grid=(B, T // tt),
        in_specs=[rowspec(KV_LORA), rowspec(MLA_ROPE), full(wk_pad), full(ek), full(wvt_ext), full(ones_col)],
        out_specs=[rowspec(NK), pl.BlockSpec((1, NVT, tt), lambda b, t: (b, 0, t))],
        out_shape=[jax.ShapeDtypeStruct((B, T, NK), BF16), jax.ShapeDtypeStruct((B, NVT, T), BF16)],
        compiler_params=_cparams(2),
        name="kv_expand",
    )(lat, kr, wk_pad, ek, wvt_ext, ones_col)


def _query_kernel(h_ref, wdq_ref, qg_ref, wq_ref, wqr_ref, c_ref, s_ref, q_ref, wdq_b, wq_b, wqr_b):
    @pl.when(pl.program_id(0) == 0)
    def _():
        wdq_b[...] = wdq_ref[...].astype(BF16)
        wq_b[...] = wq_ref[...].astype(BF16)
        wqr_b[...] = wqr_ref[...].astype(BF16)

    bb, tt, D = h_ref.shape
    h = h_ref[...].reshape(bb * tt, D)
    cq = _rms(_dot(h, wdq_b[...]), qg_ref[...]).astype(BF16)
    q1 = _dot(cq, wq_b[...]).reshape(bb, tt, -1)
    q2 = _dot(cq, wqr_b[...]).reshape(bb, tt, -1)
    c = c_ref[...]
    s = s_ref[...]
    for hd in range(MLA_HEADS):
        sl = slice(hd * HEAD_PAD, (hd + 1) * HEAD_PAD)
        q_ref[:, :, sl] = (q1[:, :, sl] * c + q2[:, :, sl] * s).astype(q_ref.dtype)


def mla_queries(h, w_dq, q_norm_g, wq_pad, wq_rot, l, c128, s128, rows=512):
    B, T, D = h.shape
    bb, tt, nblk, ij = _row_blocks(B, T, rows)
    NQ = wq_pad.shape[-1]
    tspec = pl.BlockSpec((tt, HEAD_PAD), lambda i: (ij(i)[1], 0))
    return pl.pallas_call(
        _query_kernel,
        grid=(nblk,),
        in_specs=[pl.BlockSpec((bb, tt, D), lambda i: ij(i) + (0,)),
                  pl.BlockSpec((None, D, Q_LORA), lambda i: (l, 0, 0)),
                  pl.BlockSpec((None, 1, Q_LORA), lambda i: (l, 0, 0)),
                  pl.BlockSpec((None, Q_LORA, NQ), lambda i: (l, 0, 0)),
                  pl.BlockSpec((None, Q_LORA, NQ), lambda i: (l, 0, 0)),
                  tspec, tspec],
        out_specs=pl.BlockSpec((bb, tt, NQ), lambda i: ij(i) + (0,)),
        out_shape=jax.ShapeDtypeStruct((B, T, NQ), BF16),
        scratch_shapes=[pltpu.VMEM((D, Q_LORA), BF16), pltpu.VMEM((Q_LORA, NQ), BF16),
                        pltpu.VMEM((Q_LORA, NQ), BF16)],
        compiler_params=_cparams(1),
        name="mla_queries",
    )(h, w_dq, q_norm_g.reshape(-1, 1, Q_LORA), wq_pad, wq_rot, c128, s128)


def _attn_prompt_kernel(qi_tab, ki_tab, q_ref, k_ref, vt_ref, o_ref, *scratch, tq, tk):
    H = MLA_HEADS
    m_refs, l_refs, acc_refs = scratch[:H], scratch[H:2 * H], scratch[2 * H:]
    p_id = pl.program_id(1)
    qi = qi_tab[p_id]
    ki = ki_tab[p_id]

    @pl.when(ki == 0)
    def _():
        for hd in range(H):
            m_refs[hd][...] = jnp.full(m_refs[hd].shape, NEG_INF, F32)
            l_refs[hd][...] = jnp.zeros(l_refs[hd].shape, F32)
            acc_refs[hd][...] = jnp.zeros(acc_refs[hd].shape, F32)

    def block(masked):
        if masked:
            kchunk = (ki * tk + lax.broadcasted_iota(jnp.int32, (tk, tq), 0)) // CHUNK
            qchunk = (qi * tq + lax.broadcasted_iota(jnp.int32, (tk, tq), 1)) // CHUNK
            mask = kchunk <= qchunk
        def scores(hd):
            sl = slice(hd * HEAD_PAD, (hd + 1) * HEAD_PAD)
            return _dot_nt(k_ref[0, :, sl], q_ref[0, :, sl])

        pending = [scores(hd) for hd in range(ATTN_LOOKAHEAD)]
        for hd in range(H):
            if hd + ATTN_LOOKAHEAD < H:
                pending.append(scores(hd + ATTN_LOOKAHEAD))
            s = pending.pop(0)
            if masked:
                s = jnp.where(mask, s, NEG_INF)
            m_prev = m_refs[hd][...]
            m_new = jnp.maximum(m_prev, jnp.max(s, axis=0, keepdims=True))
            a = jnp.exp2(m_prev - m_new)
            p = jnp.exp2(s - m_new).astype(BF16)
            pv = _dot(vt_ref[0, hd * V_ROWS:(hd + 1) * V_ROWS, :], p)
            acc_refs[hd][...] = a * acc_refs[hd][...] + pv[:MLA_V]
            l_refs[hd][...] = a * l_refs[hd][...] + pv[MLA_V:MLA_V + 1]
            m_refs[hd][...] = m_new

    @pl.when(ki < qi)
    def _():
        block(False)

    @pl.when(ki == qi)
    def _():
        block(True)
        o_t = jnp.concatenate([acc_refs[hd][...] / l_refs[hd][...] for hd in range(H)], axis=0)
        o_ref[0] = o_t.T.astype(o_ref.dtype)


def attn_prompt(q, k, vt, tq=256):
    B, T, NQ = q.shape
    NVT = vt.shape[1]
    NV = MLA_HEADS * MLA_V
    tk = tq
    assert tq % CHUNK == 0
    nq = T // tq
    pairs = [(a, b) for a in range(nq) for b in range(a + 1)]
    qi_tab = jnp.asarray([a for a, _ in pairs], jnp.int32)
    ki_tab = jnp.asarray([b for _, b in pairs], jnp.int32)
    grid_spec = pltpu.PrefetchScalarGridSpec(
        num_scalar_prefetch=2,
        grid=(B, len(pairs)),
        in_specs=[pl.BlockSpec((1, tq, NQ), lambda b, p, qt, kt: (b, qt[p], 0)),
                  pl.BlockSpec((1, tk, NQ), lambda b, p, qt, kt: (b, kt[p], 0)),
                  pl.BlockSpec((1, NVT, tk), lambda b, p, qt, kt: (b, 0, kt[p]))],
        out_specs=pl.BlockSpec((1, tq, NV), lambda b, p, qt, kt: (b, qt[p], 0)),
        scratch_shapes=([pltpu.VMEM((1, tq), F32)] * (2 * MLA_HEADS)
                        + [pltpu.VMEM((MLA_V, tq), F32)] * MLA_HEADS),
    )
    return pl.pallas_call(
        functools.partial(_attn_prompt_kernel, tq=tq, tk=tk),
        grid_spec=grid_spec,
        out_shape=jax.ShapeDtypeStruct((B, T, NV), BF16),
        compiler_params=_cparams(2),
        name="attn_prompt",
    )(qi_tab, ki_tab, q, k, vt)


def _absorb_kernel(q_ref, m_ref, o_ref):
    o_ref[...] = _dot(q_ref[...], m_ref[...].astype(BF16)).astype(o_ref.dtype)


def absorb_queries(q2d, m_abs):
    N = q2d.shape[0]
    H, _, W = m_abs.shape
    return pl.pallas_call(
        _absorb_kernel,
        grid=(H,),
        in_specs=[pl.BlockSpec((N, HEAD_PAD), lambda h: (0, h)),
                  pl.BlockSpec((None, HEAD_PAD, W), lambda h: (h, 0, 0))],
        out_specs=pl.BlockSpec((None, N, W), lambda h: (h, 0, 0)),
        out_shape=jax.ShapeDtypeStruct((H, N, W), BF16),
        compiler_params=_cparams(1),
        name="absorb_queries",
    )(q2d, m_abs)


def _attn_sample_kernel(q_ref, lat_ref, kr_ref, nlat_ref, nkr_ref, o_ref, m_ref, l_ref, acc_ref):
    kb = pl.program_id(1)
    H, Q, W = q_ref.shape
    q = q_ref[...].reshape(H * Q, W)
    q_lat = q[:, :KV_LORA]
    q_rope = q[:, KV_LORA:KV_LORA + MLA_ROPE]

    def update(lat_tile, kr_tile, n_sub, kr_transposed):
        sub = lat_tile.shape[0] // n_sub
        lats = [lat_tile[j * sub:(j + 1) * sub, :].astype(BF16) for j in range(n_sub)]
        if kr_transposed:
            krs = [kr_tile[:, j * sub:(j + 1) * sub].astype(BF16) for j in range(n_sub)]
            ss = [_dot_nt(q_lat, lat) + _dot(q_rope, kr) for lat, kr in zip(lats, krs)]
        else:
            krs = [kr_tile[j * sub:(j + 1) * sub, :].astype(BF16) for j in range(n_sub)]
            ss = [_dot_nt(q_lat, lat) + _dot_nt(q_rope, kr) for lat, kr in zip(lats, krs)]
        m_prev = m_ref[...]
        m_new = m_prev
        for s in ss:
            m_new = jnp.maximum(m_new, jnp.max(s, axis=-1, keepdims=True))
        a = jnp.exp2(m_prev - m_new)
        ps = [jnp.exp2(s - m_new[:, :1]) for s in ss]
        pv = _dot(ps[0].astype(BF16), lats[0])
        psum = jnp.sum(ps[0], axis=-1, keepdims=True)
        for p, lat in zip(ps[1:], lats[1:]):
            pv = pv + _dot(p.astype(BF16), lat)
            psum = psum + jnp.sum(p, axis=-1, keepdims=True)
        l_ref[...] = a * l_ref[...] + psum
        m_ref[...] = m_new
        acc_ref[...] = jnp.concatenate([a, a], axis=-1) * acc_ref[...] + pv

    @pl.when(kb == 0)
    def _():
        m_ref[...] = jnp.full_like(m_ref, NEG_INF)
        l_ref[...] = jnp.zeros_like(l_ref)
        acc_ref[...] = jnp.zeros_like(acc_ref)
        update(nlat_ref[0], nkr_ref[0], 1, False)

    update(lat_ref[0], kr_ref[0], SAMPLE_KEY_SUB, True)

    @pl.when(kb == pl.num_programs(1) - 1)
    def _():
        lsum = l_ref[...]
        o = acc_ref[...] / jnp.concatenate([lsum, lsum], axis=-1)
        o_ref[...] = o.reshape(H, Q, KV_LORA).astype(o_ref.dtype)


def attn_sample(q_abs, cache_lat, cache_kr_t, new_lat, new_kr, tk=2048):
    H, N, W = q_abs.shape
    B, P, _ = cache_lat.shape
    Q = new_lat.shape[1]
    qpos = P + np.arange(Q)
    kpos = np.arange(P + Q)
    assert bool(np.all((kpos // CHUNK)[None, :] <= (qpos // CHUNK)[:, None]))
    return pl.pallas_call(
        _attn_sample_kernel,
        grid=(B, P // tk),
        in_specs=[pl.BlockSpec((H, Q, W), lambda b, kb: (0, b, 0)),
                  pl.BlockSpec((1, tk, KV_LORA), lambda b, kb: (b, kb, 0)),
                  pl.BlockSpec((1, MLA_ROPE, tk), lambda b, kb: (b, 0, kb)),
                  pl.BlockSpec((1, Q, KV_LORA), lambda b, kb: (b, 0, 0)),
                  pl.BlockSpec((1, Q, MLA_ROPE), lambda b, kb: (b, 0, 0))],
        out_specs=pl.BlockSpec((H, Q, KV_LORA), lambda b, kb: (0, b, 0)),
        out_shape=jax.ShapeDtypeStruct((H, N, KV_LORA), BF16),
        scratch_shapes=[pltpu.VMEM((H * Q, 128), F32), pltpu.VMEM((H * Q, 128), F32),
                        pltpu.VMEM((H * Q, KV_LORA), F32)],
        compiler_params=_cparams(2),
        name="attn_sample",
    )(q_abs, cache_lat, cache_kr_t, new_lat, new_kr)


def _unabsorb_kernel(o_ref, w_ref, out_ref):
    out_ref[...] = (_dot(o_ref[0], w_ref[0].astype(BF16))
                    + _dot(o_ref[1], w_ref[1].astype(BF16))).astype(out_ref.dtype)


def unabsorb(o_lat, wuv_pad):
    H, N, R = o_lat.shape
    return pl.pallas_call(
        _unabsorb_kernel,
        grid=(H // 2,),
        in_specs=[pl.BlockSpec((2, N, R), lambda p: (p, 0, 0)),
                  pl.BlockSpec((2, R, 128), lambda p: (p, 0, 0))],
        out_specs=pl.BlockSpec((N, 128), lambda p: (0, p)),
        out_shape=jax.ShapeDtypeStruct((N, (H // 2) * 128), BF16),
        compiler_params=_cparams(1),
        name="unabsorb",
    )(o_lat, wuv_pad)


def _rope_tables(pos):
    half = MLA_ROPE // 2
    inv = 1.0 / (ROPE_THETA ** (np.arange(half, dtype=np.float64) * 2.0 / MLA_ROPE))
    ang = np.asarray(pos, np.float64)[:, None] * inv[None, :]
    cos = np.concatenate([np.cos(ang), np.cos(ang)], axis=-1)
    sin = np.concatenate([np.sin(ang), np.sin(ang)], axis=-1)
    T = cos.shape[0]
    c128 = np.zeros((T, HEAD_PAD)); s128 = np.zeros((T, HEAD_PAD))
    c128[:, :MLA_NOPE] = 1.0
    c128[:, MLA_NOPE:MLA_NOPE + MLA_ROPE] = cos
    s128[:, MLA_NOPE:MLA_NOPE + MLA_ROPE] = sin
    return (jnp.asarray(cos, F32), jnp.asarray(sin, F32),
            jnp.asarray(c128 * Q_PRESCALE, F32), jnp.asarray(s128 * Q_PRESCALE, F32))


def _rot_half_cols(w):
    half = w.shape[-1] // 2
    return jnp.concatenate([-w[..., half:], w[..., :half]], axis=-1)


def _prep_weights(w_dkv, w_uk, w_uv, w_uq, router_w):
    D = D_MODEL
    w_lat, w_rope = w_dkv[:, :KV_LORA], w_dkv[:, KV_LORA:]
    pad96 = jnp.zeros((D, 128 - MLA_ROPE), F32)
    w_kv = jnp.concatenate([w_lat, w_rope, pad96, _rot_half_cols(w_rope), pad96], axis=-1)

    zpad = HEAD_PAD - MLA_NOPE
    wk_pad = jnp.pad(w_uk, ((0, 0), (0, 0), (0, zpad))).reshape(KV_LORA, MLA_HEADS * HEAD_PAD)
    ek = jnp.zeros((MLA_ROPE, MLA_HEADS, HEAD_PAD), F32)
    ek = ek.at[:, :, MLA_NOPE:MLA_NOPE + MLA_ROPE].set(
        jnp.broadcast_to(jnp.eye(MLA_ROPE, dtype=F32)[:, None, :], (MLA_ROPE, MLA_HEADS, MLA_ROPE)))
    ek = ek.reshape(MLA_ROPE, MLA_HEADS * HEAD_PAD)
    wvt = jnp.transpose(w_uv, (1, 2, 0))
    wvt_ext = jnp.pad(wvt, ((0, 0), (0, V_ROWS - MLA_V), (0, 0))).reshape(MLA_HEADS * V_ROWS, KV_LORA)
    ones_col = jnp.tile((jnp.arange(V_ROWS) >= MLA_V).astype(F32), MLA_HEADS).reshape(-1, 1)

    nb = w_uq.shape[0]
    qn, qr = w_uq[..., :MLA_NOPE], w_uq[..., MLA_NOPE:]
    z32 = jnp.zeros(qr.shape[:-1] + (HEAD_PAD - MLA_NOPE - MLA_ROPE,), F32)
    wq_pad = jnp.concatenate([qn, qr, z32], axis=-1).reshape(nb, Q_LORA, MLA_HEADS * HEAD_PAD)
    wq_rot = jnp.concatenate([jnp.zeros_like(qn), _rot_half_cols(qr), z32], axis=-1)
    wq_rot = wq_rot.reshape(nb, Q_LORA, MLA_HEADS * HEAD_PAD)

    m_abs = jnp.zeros((MLA_HEADS, HEAD_PAD, KV_LORA + 128), F32)
    m_abs = m_abs.at[:, :MLA_NOPE, :KV_LORA].set(jnp.transpose(w_uk, (1, 2, 0)))
    m_abs = m_abs.at[:, MLA_NOPE:MLA_NOPE + MLA_ROPE, KV_LORA:KV_LORA + MLA_ROPE].set(
        jnp.broadcast_to(jnp.eye(MLA_ROPE, dtype=F32), (MLA_HEADS, MLA_ROPE, MLA_ROPE)))

    wuv_h = jnp.transpose(w_uv, (1, 0, 2))
    even = jnp.pad(wuv_h, ((0, 0), (0, 0), (0, 64)))
    odd = jnp.pad(wuv_h, ((0, 0), (0, 0), (64, 0)))
    wuv_pad = jnp.where((jnp.arange(MLA_HEADS) % 2 == 0)[:, None, None], even, odd)

    rw_t = jnp.transpose(router_w, (0, 2, 1))
    return dict(w_kv=w_kv, wk_pad=wk_pad, ek=ek, wvt_ext=wvt_ext, ones_col=ones_col, wq_pad=wq_pad, wq_rot=wq_rot,
                m_abs=m_abs, wuv_pad=wuv_pad, rw_t=rw_t)


def _mixer(st, l, P, W):
    x, m = st["x"], st["mod"][l]
    B, T, _ = x.shape
    n_a = P["hg_w_in"].shape[0]
    h = st.pop("h_next", None)
    if h is None:
        h = norm_mod(x, P["norm1_g"][l], m, sc_idx=1, sh_idx=0)
    norm2 = (P["norm2_g"][l], 4, 3)
    if l < n_a:
        z = linear(h, P["hg_w_in"], l, F32)
        s0 = None if st["hg_state"] is None else st["hg_state"][l]
        o, s_new = gla(z, st["lbs"][l], P["hg_onorm_g"][l], s0)
        st["hg_new"].append(s_new)
        st["x"], st["hp"] = linear(o, P["hg_w_out"], l, F32, x=x, mod=m, gate_idx=2, next_norm=norm2)
    else:
        bi = l - n_a
        q = mla_queries(h, P["w_dq"], P["q_norm_g"], W["wq_pad"], W["wq_rot"], bi, st["c128"], st["s128"])
        if st["past_lat"] is None:
            o = attn_prompt(q, st["k_all"], st["v_all"])
        else:
            q_abs = absorb_queries(q.reshape(B * T, -1), W["m_abs"])
            o_lat = attn_sample(q_abs, st["past_lat"], st["past_kr"], st["lat"], st["kr"])
            o = unabsorb(o_lat, W["wuv_pad"]).reshape(B, T, -1)
        st["x"], st["hp"] = linear(o, P["w_o"], bi, F32, x=x, mod=m, gate_idx=2, next_norm=norm2)


def _moe(groups, l, P, W):
    hp = jnp.concatenate([st["hp"] for st in groups], axis=0)
    n_tok = hp.shape[0]
    n_tiles = (TOP_K * n_tok) // MOE_TILE + N_EXPERTS
    pos, w8, tile_start, tile_count = route(hp, W["rw_t"], P["router_bias"], l, MOE_TILE)
    pos_flat = pos.reshape(-1)
    src = sc_invert(pos_flat, n_tok, n_tiles * MOE_TILE)
    xs = sc_gather(hp, src)
    out = moe_gemm(xs, tile_start[:, 0], tile_count[:, 0], P["exp_w_in"], P["exp_w_out"], l,
                   MOE_TILE, n_tiles)
    y8 = sc_gather(out, pos_flat).reshape(TOP_K, n_tok, -1)
    w_t = w8.T
    row0 = 0
    for st in groups:
        B, T, _ = st["x"].shape
        if l == P["norm1_g"].shape[0] - 1:
            st["x"] = moe_combine(y8, w_t, hp, P["sh_w_in"], P["sh_w_out"], st["x"], st["mod"][l], 5, l, row0,
                                  final_g=P["final_g"])
        else:
            nxt = (P["norm1_g"][l + 1], st["mod"][l + 1], 1, 0)
            st["x"], st["h_next"] = moe_combine(y8, w_t, hp, P["sh_w_in"], P["sh_w_out"], st["x"],
                                                st["mod"][l], 5, l, row0, next_norm=nxt)
        row0 += B * T


def _group_state(x, mod, pos, hg_state, past_lat, past_kr, lbs):
    cos32, sin32, c128, s128 = _rope_tables(pos)
    return dict(x=x, mod=mod, hg_state=hg_state, past_lat=past_lat, past_kr=past_kr, lbs=lbs,
                cos32=cos32, sin32=sin32, c128=c128, s128=s128, hg_new=[],
                lat=None, kr=None, k_all=None, v_all=None)


def kernel(x_prompt, x_sample, state_hgrn, cache_mla_latent, cache_mla_krope, c_prompt, c_sample, ada_w, ada_b, norm1_g, norm2_g, hg_w_in, hg_lb_logits, hg_onorm_g, hg_w_out, kv_in_g, w_dkv, kv_lat_g, w_uk, w_uv, w_dq, q_norm_g, w_uq, w_o, router_w, router_bias, exp_w_in, exp_w_out, sh_w_in, sh_w_out, final_g):
    Bp, Sp, _ = x_prompt.shape
    Bs, Ss, _ = x_sample.shape
    past = cache_mla_latent.shape[1]
    P = dict(norm1_g=norm1_g, norm2_g=norm2_g, hg_w_in=hg_w_in, hg_lb_logits=hg_lb_logits,
             hg_onorm_g=hg_onorm_g, hg_w_out=hg_w_out, kv_in_g=kv_in_g, kv_lat_g=kv_lat_g,
             w_dq=w_dq, q_norm_g=q_norm_g, w_o=w_o, router_bias=router_bias,
             exp_w_in=exp_w_in, exp_w_out=exp_w_out, sh_w_in=sh_w_in, sh_w_out=sh_w_out, final_g=final_g)
    W = _prep_weights(w_dkv, w_uk, w_uv, w_uq, router_w)
    mod = ada_mod(jnp.concatenate([c_prompt, c_sample], axis=0), ada_w, ada_b)
    lbs = jnp.cumsum(jax.nn.softmax(hg_lb_logits.astype(F32), axis=0), axis=0)
    bsz = Bp // PROMPT_STREAMS
    prompts = [_group_state(x_prompt[i * bsz:(i + 1) * bsz], mod[:, i * bsz:(i + 1) * bsz, None, :],
                            np.arange(Sp), None, None, None, lbs) for i in range(PROMPT_STREAMS)]
    gs = _group_state(x_sample, mod[:, Bp:, None, :], past + np.arange(Ss), state_hgrn,
                      cache_mla_latent, jnp.transpose(cache_mla_krope, (0, 2, 1)), lbs)
    streams = [[prompts[0], gs]] + [[g] for g in prompts[1:]]
    n_a = hg_w_in.shape[0]
    for l in range(norm1_g.shape[0]):
        for groups in streams:
            for st in groups:
                _mixer(st, l, P, W)
            _moe(groups, l, P, W)
            if l == n_a - 1:
                for st in groups:
                    st["lat"], st["kr"] = shared_kv(st["x"], kv_in_g, W["w_kv"], kv_lat_g, st["cos32"], st["sin32"])
                    if st["past_lat"] is None:
                        st["k_all"], st["v_all"] = kv_expand(st["lat"], st["kr"], W["wk_pad"], W["ek"],
                                                             W["wvt_ext"], W["ones_col"])
    for st in prompts + [gs]:
        st["y"] = st["x"]
        st["hg_out"] = jnp.stack(st["hg_new"], axis=0)

    def cat(key, axis=0):
        return jnp.concatenate([g[key] for g in prompts], axis=axis)

    return (cat("y"), gs["y"], cat("hg_out", 1), gs["hg_out"], cat("lat"), cat("kr"), gs["lat"], gs["kr"])
```

```python
import dataclasses
import functools

import numpy as np
import jax
import jax.numpy as jnp
from jax import lax
from jax.experimental import pallas as pl
from jax.experimental.pallas import tpu as pltpu
from jax.experimental.pallas import tpu_sc as plsc

F32 = jnp.float32
BF16 = jnp.bfloat16

D_MODEL = 1024
CHUNK = 64
HG_HEADS = 8
HG_DK = 128
HG_DV = 128
MLA_HEADS = 16
MLA_NOPE = 64
MLA_ROPE = 32
MLA_V = 64
Q_LORA = 384
KV_LORA = 256
ROPE_THETA = 10000.0
N_EXPERTS = 64
TOP_K = 8
N_GROUPS = 8
TOPK_GROUPS = 4
EXPERT_FF = 256
SHARED_FF = 256
ROUTED_SCALE = 2.5
EPS = 1e-6

HEAD_PAD = 128
SAMPLE_KEY_SUB = 4
ATTN_LOOKAHEAD = 4
V_ROWS = MLA_V + 16
QK_SCALE = (MLA_NOPE + MLA_ROPE) ** -0.5
Q_PRESCALE = QK_SCALE * float(np.log2(np.e))
VMEM_LIMIT = 56 * 1024 * 1024
NEG_INF = float("-inf")
SC_CORES = 2
SC_SUBCORES = 16
SC_WORKERS = SC_CORES * SC_SUBCORES
SC_LANES = 16
SC_WINDOW = 64
MOE_TILE = 512
PROMPT_STREAMS = 2
MOE_NBUF = 4
MOE_SUB = 2


def _cparams(n_axes):
    return pltpu.CompilerParams(dimension_semantics=("arbitrary",) * n_axes,
                                vmem_limit_bytes=VMEM_LIMIT)


def _silu(x):
    return x * jax.nn.sigmoid(x)


def _rms(x, g):
    ms = jnp.mean(x * x, axis=-1, keepdims=True)
    return x * lax.rsqrt(ms + EPS) * g


def _dot(a, b):
    return jnp.dot(a, b, preferred_element_type=F32)


def _dot_nt(a, b):
    return lax.dot_general(a, b, (((1,), (1,)), ((), ())), preferred_element_type=F32)


def _dot_tn(a, b):
    return lax.dot_general(a, b, (((0,), (0,)), ((), ())), preferred_element_type=F32)


def _row_blocks(B, T, rows):
    if T >= rows:
        assert T % rows == 0
        bb, tt = 1, rows
    else:
        assert rows % T == 0 and B % (rows // T) == 0
        bb, tt = rows // T, T
    nt = T // tt
    return bb, tt, (B // bb) * nt, (lambda i: (i // nt, i % nt))


def _ada_kernel(c_ref, w_ref, b_ref, o_ref):
    a = _silu(c_ref[...]).astype(BF16)
    o_ref[...] = _dot(a, w_ref[...].astype(BF16)) + b_ref[...]


def ada_mod(c, ada_w, ada_b):
    R, D = c.shape
    L, _, N = ada_w.shape
    tn = 1536
    return pl.pallas_call(
        _ada_kernel,
        grid=(L, N // tn),
        in_specs=[pl.BlockSpec((R, D), lambda l, j: (0, 0)),
                  pl.BlockSpec((None, D, tn), lambda l, j: (l, 0, j)),
                  pl.BlockSpec((None, 1, tn), lambda l, j: (l, 0, j))],
        out_specs=pl.BlockSpec((None, R, tn), lambda l, j: (l, 0, j)),
        out_shape=jax.ShapeDtypeStruct((L, R, N), F32),
        compiler_params=_cparams(2),
        name="ada_mod",
    )(c, ada_w, ada_b.reshape(L, 1, N))


def _pack_pairs(y):
    half = y.shape[-1] // 2
    bits = lax.bitcast_convert_type(y.astype(BF16).astype(F32), jnp.uint32)
    word = lax.shift_right_logical(bits[:, :half], jnp.uint32(16)) | bits[:, half:]
    return lax.bitcast_convert_type(word, jnp.int32)


def _unpack_pairs(word, dtype=BF16):
    u = lax.bitcast_convert_type(word, jnp.uint32)
    lo = lax.bitcast_convert_type(lax.shift_left(u, jnp.uint32(16)), F32)
    hi = lax.bitcast_convert_type(u & jnp.uint32(0xFFFF0000), F32)
    return lo.astype(dtype), hi.astype(dtype)


def _norm_kernel(*refs, modulated, packed):
    if modulated:
        x_ref, g_ref, sc_ref, sh_ref, o_ref = refs
    else:
        x_ref, g_ref, o_ref = refs
    y = _rms(x_ref[...], g_ref[...])
    if modulated:
        y = y * (1.0 + sc_ref[...]) + sh_ref[...]
    if packed:
        bb, tt, D = y.shape
        o_ref[...] = _pack_pairs(y.reshape(bb * tt, D))
    else:
        o_ref[...] = y.astype(o_ref.dtype)


def norm_mod(x, g, mod=None, sc_idx=0, sh_idx=0, out_dtype=BF16, rows=512, packed=False):
    B, T, D = x.shape
    bb, tt, nblk, ij = _row_blocks(B, T, rows)
    xspec = pl.BlockSpec((bb, tt, D), lambda i: ij(i) + (0,))
    in_specs = [xspec, pl.BlockSpec((1, D), lambda i: (0, 0))]
    args = [x, g.reshape(1, D)]
    if mod is not None:
        in_specs += [pl.BlockSpec((bb, 1, D), lambda i: (ij(i)[0], 0, sc_idx)),
                     pl.BlockSpec((bb, 1, D), lambda i: (ij(i)[0], 0, sh_idx))]
        args += [mod, mod]
    if packed:
        out_specs = pl.BlockSpec((bb * tt, D // 2), lambda i: (i, 0))
        out_shape = jax.ShapeDtypeStruct((B * T, D // 2), jnp.int32)
    else:
        out_specs = xspec
        out_shape = jax.ShapeDtypeStruct((B, T, D), out_dtype)
    return pl.pallas_call(
        functools.partial(_norm_kernel, modulated=mod is not None, packed=packed),
        grid=(nblk,),
        in_specs=in_specs,
        out_specs=out_specs,
        out_shape=out_shape,
        compiler_params=_cparams(1),
        name="norm_mod",
    )(*args)


def _linear_kernel(*refs, residual, norm_next):
    if norm_next:
        a_ref, w_ref, x_ref, gate_ref, ng_ref, nsc_ref, nsh_ref, o_ref, hp_ref, wb_ref = refs
    elif residual:
        a_ref, w_ref, x_ref, gate_ref, o_ref, wb_ref = refs
    else:
        a_ref, w_ref, o_ref, wb_ref = refs

    @pl.when(pl.program_id(1) == 0)
    def _():
        wb_ref[...] = w_ref[...].astype(BF16)

    bb, tt, K = a_ref.shape
    y = _dot(a_ref[...].reshape(bb * tt, K).astype(BF16), wb_ref[...])
    y = y.reshape(bb, tt, y.shape[-1])
    if residual:
        y = x_ref[...] + gate_ref[...] * y
    o_ref[...] = y.astype(o_ref.dtype)
    if norm_next:
        h = _rms(y, ng_ref[...]) * (1.0 + nsc_ref[...]) + nsh_ref[...]
        hp_ref[...] = _pack_pairs(h.reshape(bb * tt, h.shape[-1]))


def linear(a, w, l, out_dtype, x=None, mod=None, gate_idx=0, rows=512, tn=1024, next_norm=None):
    B, T, K = a.shape
    _, _, N = w.shape
    tn = min(tn, N)
    bb, tt, nblk, ij = _row_blocks(B, T, rows)
    in_specs = [pl.BlockSpec((bb, tt, K), lambda j, i: ij(i) + (0,)),
                pl.BlockSpec((None, K, tn), lambda j, i: (l, 0, j))]
    args = [a, w]
    ospec = pl.BlockSpec((bb, tt, tn), lambda j, i: ij(i) + (j,))
    out_specs = ospec
    out_shape = jax.ShapeDtypeStruct((B, T, N), out_dtype)
    if x is not None:
        gsteps = D_MODEL // tn
        in_specs += [ospec, pl.BlockSpec((bb, 1, tn), lambda j, i: (ij(i)[0], 0, gate_idx * gsteps + j))]
        args += [x, mod]
    if next_norm is not None:
        assert x is not None and tn == N
        gain, sc_idx, sh_idx = next_norm
        in_specs += [pl.BlockSpec((1, N), lambda j, i: (0, 0)),
                     pl.BlockSpec((bb, 1, N), lambda j, i: (ij(i)[0], 0, sc_idx)),
                     pl.BlockSpec((bb, 1, N), lambda j, i: (ij(i)[0], 0, sh_idx))]
        args += [gain.reshape(1, N), mod, mod]
        out_specs = [ospec, pl.BlockSpec((bb * tt, N // 2), lambda j, i: (i, 0))]
        out_shape = [out_shape, jax.ShapeDtypeStruct((B * T, N // 2), jnp.int32)]
    return pl.pallas_call(
        functools.partial(_linear_kernel, residual=x is not None, norm_next=next_norm is not None),
        grid=(N // tn, nblk),
        in_specs=in_specs,
        out_specs=out_specs,
        out_shape=out_shape,
        scratch_shapes=[pltpu.VMEM((K, tn), BF16)],
        compiler_params=_cparams(2),
        name="linear",
    )(*args)


def _gla_kernel(*refs, L, n_chunks, has_init):
    if has_init:
        q_ref, f_ref, i_ref, g_ref, lb_ref, on_ref, s0_ref, o_ref, so_ref, st_ref = refs
    else:
        q_ref, f_ref, i_ref, g_ref, lb_ref, on_ref, o_ref, so_ref, st_ref = refs
    t = pl.program_id(1)
    H = st_ref.shape[0]

    @pl.when(t == 0)
    def _():
        for h in range(H):
            if has_init:
                st_ref[h] = s0_ref[0, h].T
            else:
                st_ref[h] = jnp.zeros(st_ref.shape[1:], F32)

    lb = lb_ref[...]
    onorm = on_ref[...]
    row = lax.broadcasted_iota(jnp.int32, (L, L), 0)
    col = lax.broadcasted_iota(jnp.int32, (L, L), 1)
    causal = col <= row
    tri = causal.astype(BF16)
    mid = L // 2 - 1

    def chunk(c, carry):
        r0 = pl.multiple_of(c * L, L)
        q = _silu(q_ref[0, pl.ds(r0, L), :])
        fg = lb + (1.0 - lb) * jax.nn.sigmoid(f_ref[0, pl.ds(r0, L), :])
        k = 1.0 - fg
        v = i_ref[0, pl.ds(r0, L), :].astype(BF16)
        gate = _silu(g_ref[0, pl.ds(r0, L), :])
        logf = jnp.log(fg)
        hi = logf.astype(BF16)
        lo = (logf - hi.astype(F32)).astype(BF16)
        b = _dot(tri, hi) + _dot(tri, lo)
        b_mid = b[mid:mid + 1, :]
        b_last = b[L - 1:L, :]
        up = jnp.exp(b - b_mid)
        down = jnp.exp(b_mid - b)
        qa = q * up
        kb = k * down
        qe = (qa * jnp.exp(b_mid)).astype(BF16)
        kd = (kb * jnp.exp(b_last - b_mid)).astype(BF16)
        qa = qa.astype(BF16)
        kb = kb.astype(BF16)
        decay = jnp.exp(b_last)
        sls = [slice(h * HG_DK, (h + 1) * HG_DK) for h in range(H)]
        sts = [st_ref[h] for h in range(H)]
        scores = [_dot_nt(qa[:, sl], kb[:, sl]) for sl in sls]
        inter = [_dot_nt(qe[:, sl], st.astype(BF16)) for sl, st in zip(sls, sts)]
        outer = [_dot_tn(v[:, sl], kd[:, sl]) for sl in sls]
        intra = [_dot(jnp.where(causal, sc, 0.0).astype(BF16), v[:, sl]) for sc, sl in zip(scores, sls)]
        for h, sl in enumerate(sls):
            st_ref[h] = sts[h] * decay[:, sl] + outer[h]
            o = _rms(inter[h] + intra[h], onorm[:, sl]) * gate[:, sl]
            o_ref[0, pl.ds(r0, L), sl] = o.astype(o_ref.dtype)
        return carry

    lax.fori_loop(0, n_chunks, chunk, 0)

    @pl.when(t == pl.num_programs(1) - 1)
    def _():
        for h in range(H):
            so_ref[0, h] = st_ref[h].T


def gla(z, lb, onorm_g, s0):
    B, T, _ = z.shape
    L = CHUNK if T % CHUNK == 0 else T
    tt = min(T, 512)
    n_chunks = tt // L
    H, D = HG_HEADS, D_MODEL

    def zspec(part):
        return pl.BlockSpec((1, tt, D), lambda b, t: (b, t, part))

    hspec = pl.BlockSpec((1, D), lambda b, t: (0, 0))
    sspec = pl.BlockSpec((1, H, HG_DK, HG_DV), lambda b, t: (b, 0, 0, 0))
    in_specs = [zspec(0), zspec(1), zspec(2), zspec(3), hspec, hspec]
    args = [z, z, z, z, lb.reshape(1, D), onorm_g.reshape(1, D)]
    if s0 is not None:
        in_specs.append(sspec)
        args.append(s0)
    return pl.pallas_call(
        functools.partial(_gla_kernel, L=L, n_chunks=n_chunks, has_init=s0 is not None),
        grid=(B, T // tt),
        in_specs=in_specs,
        out_specs=[pl.BlockSpec((1, tt, D), lambda b, t: (b, t, 0)), sspec],
        out_shape=[jax.ShapeDtypeStruct((B, T, D), BF16),
                   jax.ShapeDtypeStruct((B, H, HG_DK, HG_DV), F32)],
        scratch_shapes=[pltpu.VMEM((H, HG_DV, HG_DK), F32)],
        compiler_params=_cparams(2),
        name="gla",
    )(*args)


def _route_kernel(h_ref, rw_ref, bias_ref, pos_ref, w_ref, te_ref, nu_ref,
                  e_s, r_s, base_s, start_s, *, tile_rows):
    ph = pl.program_id(0)
    i = pl.program_id(1)
    M = h_ref.shape[0]
    half = h_ref.shape[1]
    G, E = N_GROUPS, N_EXPERTS // N_GROUPS
    e_flat = lax.broadcasted_iota(jnp.int32, (N_EXPERTS, M), 0)

    @pl.when(ph == 1)
    def _():
        @pl.when(i == 0)
        def _():
            cnt = base_s[...]
            padded = jnp.floor((cnt + (tile_rows - 1)) * (1.0 / tile_rows)) * tile_rows
            r = lax.broadcasted_iota(jnp.int32, (N_EXPERTS, N_EXPERTS), 0)
            c = lax.broadcasted_iota(jnp.int32, (N_EXPERTS, N_EXPERTS), 1)
            start = jnp.dot((c < r).astype(F32), padded, preferred_element_type=F32,
                            precision=lax.Precision.HIGHEST)
            start_s[...] = start
            te_ref[...] = (start * (1.0 / tile_rows)).astype(jnp.int32)
            nu_ref[...] = (padded * (1.0 / tile_rows)).astype(jnp.int32)

        start_col = start_s[:, :1]
        for k in range(TOP_K):
            hit = e_flat == e_s[i, k:k + 1, :]
            seg = jnp.sum(jnp.where(hit, start_col, 0.0), axis=0, keepdims=True)
            pos_ref[k:k + 1, :] = (seg + r_s[i, k:k + 1, :]).astype(jnp.int32)

    @pl.when(ph == 0)
    def _():
        _route_pass0(h_ref, rw_ref, bias_ref, w_ref, e_s, r_s, base_s, i, M, half, G, E)


def _route_pass0(h_ref, rw_ref, bias_ref, w_ref, e_s, r_s, base_s, i, M, half, G, E):
    @pl.when(i == 0)
    def _():
        base_s[...] = jnp.zeros_like(base_s)

    lo, hi = _unpack_pairs(h_ref[...])
    rw = rw_ref[...].astype(BF16)
    logits = _dot_nt(rw[:, :half], lo) + _dot_nt(rw[:, half:], hi)
    s = jax.nn.sigmoid(logits)
    sb = (s + bias_ref[...]).reshape(G, E, M)
    s = s.reshape(G, E, M)
    e_in = lax.broadcasted_iota(jnp.int32, (G, E, M), 1)
    g_id = lax.broadcasted_iota(jnp.int32, (G, 1, M), 0)
    e_id = lax.broadcasted_iota(jnp.int32, (G, E, M), 0) * E + e_in

    m1 = jnp.max(sb, axis=1, keepdims=True)
    first = jnp.min(jnp.where(sb == m1, e_in, E), axis=1, keepdims=True)
    m2 = jnp.max(jnp.where(e_in == first, NEG_INF, sb), axis=1, keepdims=True)
    gs = m1 + m2

    rank = jnp.zeros((G, 1, M), jnp.int32)
    for j in range(G):
        gj = gs[j:j + 1]
        beats = (gj > gs) | ((gj == gs) & (j < g_id))
        rank = rank + beats.astype(jnp.int32)
    gsel = rank < TOPK_GROUPS

    vals = jnp.where(gsel, sb, NEG_INF)
    w = jnp.zeros((G, E, M), F32)
    selm = jnp.zeros((G, E, M), F32)
    chosen = []
    for _ in range(TOP_K):
        m = jnp.max(jnp.max(vals, axis=1, keepdims=True), axis=0, keepdims=True)
        cand = jnp.where(vals == m, e_id, N_EXPERTS)
        first = jnp.min(jnp.min(cand, axis=1, keepdims=True), axis=0, keepdims=True)
        hit = e_id == first
        w = jnp.where(hit, s, w)
        selm = jnp.where(hit, 1.0, selm)
        vals = jnp.where(hit, NEG_INF, vals)
        chosen.append(first.reshape(1, M))

    tot = jnp.sum(jnp.sum(w, axis=1, keepdims=True), axis=0, keepdims=True)
    gates = (w / tot * ROUTED_SCALE).reshape(N_EXPERTS, M)
    selm = selm.reshape(N_EXPERTS, M)

    earlier = (lax.broadcasted_iota(jnp.int32, (M, M), 0)
               < lax.broadcasted_iota(jnp.int32, (M, M), 1)).astype(BF16)
    rank = base_s[:, :1] + _dot(selm.astype(BF16), earlier)
    base_s[...] = base_s[...] + jnp.sum(selm, axis=1, keepdims=True)
    e_flat = lax.broadcasted_iota(jnp.int32, (N_EXPERTS, M), 0)
    for k in range(TOP_K):
        hit = e_flat == chosen[k]
        e_s[i, k:k + 1, :] = chosen[k]
        r_s[i, k:k + 1, :] = jnp.sum(jnp.where(hit, rank, 0.0), axis=0, keepdims=True)
        w_ref[k:k + 1, :] = jnp.sum(jnp.where(hit, gates, 0.0), axis=0, keepdims=True)


def route(hp, router_w_t, router_bias, l, tile_rows, rows=512):
    N, half = hp.shape
    M = rows
    nT = N // M
    assert N % M == 0

    def p0(ph, i):
        return i * (1 - ph) + (nT - 1) * ph

    return pl.pallas_call(
        functools.partial(_route_kernel, tile_rows=tile_rows),
        grid=(2, nT),
        in_specs=[pl.BlockSpec((M, half), lambda ph, i: (p0(ph, i), 0)),
                  pl.BlockSpec((None, N_EXPERTS, 2 * half), lambda ph, i: (l, 0, 0)),
                  pl.BlockSpec((None, N_EXPERTS, 1), lambda ph, i: (l, 0, 0))],
        out_specs=[pl.BlockSpec((TOP_K, M), lambda ph, i: (0, i * ph)),
                   pl.BlockSpec((TOP_K, M), lambda ph, i: (0, p0(ph, i))),
                   pl.BlockSpec((N_EXPERTS, 128), lambda ph, i: (0, 0)),
                   pl.BlockSpec((N_EXPERTS, 128), lambda ph, i: (0, 0))],
        out_shape=[jax.ShapeDtypeStruct((TOP_K, N), jnp.int32),
                   jax.ShapeDtypeStruct((TOP_K, N), F32),
                   jax.ShapeDtypeStruct((N_EXPERTS, 128), jnp.int32),
                   jax.ShapeDtypeStruct((N_EXPERTS, 128), jnp.int32)],
        scratch_shapes=[pltpu.VMEM((nT, TOP_K, M), jnp.int32), pltpu.VMEM((nT, TOP_K, M), F32),
                        pltpu.VMEM((N_EXPERTS, 128), F32), pltpu.VMEM((N_EXPERTS, 128), F32)],
        compiler_params=_cparams(2),
        name="route",
    )(hp, router_w_t, router_bias.reshape(-1, N_EXPERTS, 1))


def _sc_mesh():
    return plsc.VectorSubcoreMesh(core_axis_name="core", subcore_axis_name="subcore")


def sc_invert(pos_flat, n_tok, n_out):
    n = pos_flat.shape[0]
    per = n_out // SC_WORKERS
    chunk = n_tok
    assert n_out % SC_WORKERS == 0 and per % SC_LANES == 0
    assert n_tok % chunk == 0 and n % chunk == 0 and chunk % SC_LANES == 0
    cp = pltpu.CompilerParams()
    if "needs_layout_passes" in pltpu.CompilerParams.__dataclass_fields__:
        cp = dataclasses.replace(cp, needs_layout_passes=False)

    @functools.partial(
        pl.kernel, out_type=jax.ShapeDtypeStruct((n_out,), jnp.int32), mesh=_sc_mesh(),
        scratch_types=[pltpu.VMEM((chunk,), jnp.int32), pltpu.VMEM((per,), jnp.int32)],
        compiler_params=cp, name="sc_invert")
    def k(pos_hbm, src_hbm, pos_v, src_v):
        wid = lax.axis_index("subcore") * SC_CORES + lax.axis_index("core")
        lo = wid * per
        lane = lax.iota(jnp.int32, SC_LANES)

        @pl.loop(0, per, step=SC_LANES)
        def _(r):
            src_v[pl.ds(r, SC_LANES)] = lax.rem(lo + r + lane, n_tok)

        @pl.loop(0, n // chunk)
        def _(c):
            base = c * chunk
            pltpu.sync_copy(pos_hbm.at[pl.ds(base, chunk)], pos_v)
            tok0 = lax.rem(base, n_tok)

            @plsc.parallel_loop(0, chunk, step=SC_LANES, unroll=8)
            def _(r):
                p = pos_v[pl.ds(r, SC_LANES)] - lo
                mine = (p >= 0) & (p < per)
                plsc.store_scatter(src_v, [jnp.where(mine, p, 0)], tok0 + r + lane, mask=mine)

        pltpu.sync_copy(src_v, src_hbm.at[pl.ds(lo, per)])

    return k(pos_flat)


def sc_gather(x, idx):
    n = idx.shape[0]
    dim = x.shape[1]
    assert n % (SC_WINDOW * SC_WORKERS) == 0

    @functools.partial(
        pl.kernel, out_type=jax.ShapeDtypeStruct((n, dim), x.dtype), mesh=_sc_mesh(),
        scratch_types=[], name="sc_gather")
    def k(x_hbm, i_hbm, o_hbm):
        def body(i_vmem, o_vmem):
            pltpu.sync_copy(x_hbm.at[i_vmem.at[0]], o_vmem)

        pltpu.emit_pipeline(
            body, grid=(n // SC_WINDOW,),
            in_specs=[pl.BlockSpec((1, SC_WINDOW), index_map=lambda i: (i, 0))],
            out_specs=[pl.BlockSpec((SC_WINDOW, dim), index_map=lambda i: (i, 0))],
            core_axis_name=("core", "subcore"),
            dimension_semantics=(pltpu.PARALLEL,),
        )(i_hbm, o_hbm)

    return k(x, idx.reshape(n // SC_WINDOW, SC_WINDOW))


def _moe_gemm_kernel(ts_ref, tn_ref, x_hbm, wi_ref, wo_ref, o_hbm, wi_b, wo_b, xbuf, obuf, in_sem, out_sem,
                     *, tile_rows, n_tiles):
    e = pl.program_id(0)
    last = pl.num_programs(0) - 1
    t0 = ts_ref[e]
    n = tn_ref[e]
    n_used = ts_ref[last] + tn_ref[last]

    def x_copy(g, slot):
        rows = pl.ds(pl.multiple_of(g * tile_rows, tile_rows), tile_rows)
        return pltpu.make_async_copy(x_hbm.at[rows], xbuf.at[slot], in_sem.at[slot])

    def o_copy(g, slot):
        rows = pl.ds(pl.multiple_of(g * tile_rows, tile_rows), tile_rows)
        return pltpu.make_async_copy(obuf.at[slot], o_hbm.at[rows], out_sem.at[slot])

    @pl.when(e == 0)
    def _():
        for g0 in range(MOE_NBUF - 1):
            @pl.when(g0 < n_used)
            def _():
                x_copy(g0, g0).start()

    @pl.when(n > 0)
    def _():
        wi_b[...] = wi_ref[...].astype(BF16)
        wo_b[...] = wo_ref[...].astype(BF16)

    def tile(i, carry):
        g = t0 + i
        slot = lax.rem(g, MOE_NBUF)
        x_copy(g, slot).wait()
        ahead = g + (MOE_NBUF - 1)

        @pl.when(ahead < n_used)
        def _():
            x_copy(ahead, lax.rem(ahead, MOE_NBUF)).start()

        @pl.when(g >= MOE_NBUF)
        def _():
            o_copy(g - MOE_NBUF, slot).wait()

        rows = tile_rows // MOE_SUB
        half = xbuf.shape[2]
        xs = [_unpack_pairs(xbuf[slot, r * rows:(r + 1) * rows, :]) for r in range(MOE_SUB)]
        hus = [_dot(lo, wi_b[:half, :]) + _dot(hi, wi_b[half:, :]) for lo, hi in xs]
        acts = [(_silu(hu[:, :EXPERT_FF]) * hu[:, EXPERT_FF:]).astype(BF16) for hu in hus]
        outs = [_dot(act, wo_b[...]) for act in acts]
        for r, out in enumerate(outs):
            obuf[slot, r * rows:(r + 1) * rows, :] = _pack_pairs(out)
        o_copy(g, slot).start()
        return carry

    lax.fori_loop(0, n, tile, 0)

    @pl.when(e == last)
    def _():
        for back in range(MOE_NBUF, 0, -1):
            @pl.when(n_used >= back)
            def _():
                o_copy(n_used - back, lax.rem(n_used - back, MOE_NBUF)).wait()

        obuf[0] = jnp.zeros(obuf.shape[1:], obuf.dtype)

        def clear(g, carry):
            cp = o_copy(g, 0)
            cp.start()
            cp.wait()
            return carry

        lax.fori_loop(n_used, n_tiles, clear, 0)


def moe_gemm(xs, tile_start, tile_count, exp_w_in, exp_w_out, l, tile_rows, n_tiles):
    P, half = xs.shape
    D = 2 * half
    assert P == n_tiles * tile_rows
    hbm = pl.BlockSpec(memory_space=pl.ANY)
    grid_spec = pltpu.PrefetchScalarGridSpec(
        num_scalar_prefetch=2,
        grid=(N_EXPERTS,),
        in_specs=[hbm,
                  pl.BlockSpec((None, None, D, 2 * EXPERT_FF), lambda e, ts, tn: (l, e, 0, 0)),
                  pl.BlockSpec((None, None, EXPERT_FF, D), lambda e, ts, tn: (l, e, 0, 0))],
        out_specs=hbm,
        scratch_shapes=[pltpu.VMEM((D, 2 * EXPERT_FF), BF16), pltpu.VMEM((EXPERT_FF, D), BF16),
                        pltpu.VMEM((MOE_NBUF, tile_rows, half), jnp.int32),
                        pltpu.VMEM((MOE_NBUF, tile_rows, half), jnp.int32),
                        pltpu.SemaphoreType.DMA((MOE_NBUF,)), pltpu.SemaphoreType.DMA((MOE_NBUF,))],
    )
    return pl.pallas_call(
        functools.partial(_moe_gemm_kernel, tile_rows=tile_rows, n_tiles=n_tiles),
        grid_spec=grid_spec,
        out_shape=jax.ShapeDtypeStruct((P, half), jnp.int32),
        compiler_params=_cparams(1),
        name="moe_gemm",
    )(tile_start, tile_count, xs, exp_w_in, exp_w_out)


def _moe_combine_kernel(*refs, final, norm_next):
    if final:
        y_ref, w_ref, h_ref, si_ref, so_ref, x_ref, g2_ref, fg_ref, o_ref, si_b, so_b = refs
    elif norm_next:
        (y_ref, w_ref, h_ref, si_ref, so_ref, x_ref, g2_ref, ng_ref, nsc_ref, nsh_ref,
         o_ref, hn_ref, si_b, so_b) = refs
    else:
        y_ref, w_ref, h_ref, si_ref, so_ref, x_ref, g2_ref, o_ref, si_b, so_b = refs

    @pl.when(pl.program_id(0) == 0)
    def _():
        si_b[...] = si_ref[...].astype(BF16)
        so_b[...] = so_ref[...].astype(BF16)

    bb, tt, D = x_ref.shape
    half = D // 2
    w = w_ref[...]
    acc_lo = jnp.zeros((bb * tt, half), F32)
    acc_hi = jnp.zeros((bb * tt, half), F32)
    for k in range(TOP_K):
        lo, hi = _unpack_pairs(y_ref[k], F32)
        acc_lo = acc_lo + w[:, k:k + 1] * lo
        acc_hi = acc_hi + w[:, k:k + 1] * hi
    hlo, hhi = _unpack_pairs(h_ref[...])
    hu = _dot(hlo, si_b[:half, :]) + _dot(hhi, si_b[half:, :])
    act = (_silu(hu[:, :SHARED_FF]) * hu[:, SHARED_FF:]).astype(BF16)
    y = jnp.concatenate([acc_lo, acc_hi], axis=-1) + _dot(act, so_b[...])
    x_new = x_ref[...] + g2_ref[...] * y.reshape(bb, tt, D)
    o_ref[...] = _rms(x_new, fg_ref[...]) if final else x_new
    if norm_next:
        hn = _rms(x_new, ng_ref[...]) * (1.0 + nsc_ref[...]) + nsh_ref[...]
        hn_ref[...] = hn.astype(hn_ref.dtype)


def moe_combine(y8, w_t, hp, sh_w_in, sh_w_out, x, mod, gate_idx, l, row0, final_g=None, next_norm=None,
                rows=256):
    B, T, D = x.shape
    half = D // 2
    bb, tt, nblk, ij = _row_blocks(B, T, rows)
    M = bb * tt
    assert row0 % M == 0
    off = row0 // M
    xspec = pl.BlockSpec((bb, tt, D), lambda i: ij(i) + (0,))
    in_specs = [pl.BlockSpec((TOP_K, M, half), lambda i: (0, off + i, 0)),
                pl.BlockSpec((M, TOP_K), lambda i: (off + i, 0)),
                pl.BlockSpec((M, half), lambda i: (off + i, 0)),
                pl.BlockSpec((None, D, 2 * SHARED_FF), lambda i: (l, 0, 0)),
                pl.BlockSpec((None, SHARED_FF, D), lambda i: (l, 0, 0)),
                xspec,
                pl.BlockSpec((bb, 1, D), lambda i: (ij(i)[0], 0, gate_idx))]
    args = [y8, w_t, hp, sh_w_in, sh_w_out, x, mod]
    out_specs = xspec
    out_shape = jax.ShapeDtypeStruct((B, T, D), F32)
    if final_g is not None:
        assert next_norm is None
        in_specs.append(pl.BlockSpec((1, D), lambda i: (0, 0)))
        args.append(final_g.reshape(1, D))
    if next_norm is not None:
        gain, mod_next, sc_idx, sh_idx = next_norm
        in_specs += [pl.BlockSpec((1, D), lambda i: (0, 0)),
                     pl.BlockSpec((bb, 1, D), lambda i: (ij(i)[0], 0, sc_idx)),
                     pl.BlockSpec((bb, 1, D), lambda i: (ij(i)[0], 0, sh_idx))]
        args += [gain.reshape(1, D), mod_next, mod_next]
        out_specs = [xspec, xspec]
        out_shape = [out_shape, jax.ShapeDtypeStruct((B, T, D), BF16)]
    return pl.pallas_call(
        functools.partial(_moe_combine_kernel, final=final_g is not None, norm_next=next_norm is not None),
        grid=(nblk,),
        in_specs=in_specs,
        out_specs=out_specs,
        out_shape=out_shape,
        scratch_shapes=[pltpu.VMEM((D, 2 * SHARED_FF), BF16), pltpu.VMEM((SHARED_FF, D), BF16)],
        compiler_params=_cparams(1),
        name="moe_combine",
    )(*args)


def _shared_kv_kernel(x_ref, g_ref, w_ref, lg_ref, cos_ref, sin_ref, lat_ref, kr_ref):
    bb, tt, D = x_ref.shape
    xn = _rms(x_ref[...], g_ref[...]).reshape(bb * tt, D).astype(BF16)
    z = _dot(xn, w_ref[...].astype(BF16))
    lat = _rms(z[:, :KV_LORA], lg_ref[...])
    lat_ref[...] = lat.reshape(bb, tt, KV_LORA)
    zr = z[:, KV_LORA:KV_LORA + MLA_ROPE].reshape(bb, tt, MLA_ROPE)
    zq = z[:, KV_LORA + 128:KV_LORA + 128 + MLA_ROPE].reshape(bb, tt, MLA_ROPE)
    kr_ref[...] = zr * cos_ref[...] + zq * sin_ref[...]


def shared_kv(x, kv_in_g, w_kv, kv_lat_g, cos32, sin32, rows=512):
    B, T, D = x.shape
    bb, tt, nblk, ij = _row_blocks(B, T, rows)
    tspec = pl.BlockSpec((tt, MLA_ROPE), lambda i: (ij(i)[1], 0))
    return pl.pallas_call(
        _shared_kv_kernel,
        grid=(nblk,),
        in_specs=[pl.BlockSpec((bb, tt, D), lambda i: ij(i) + (0,)),
                  pl.BlockSpec((1, D), lambda i: (0, 0)),
                  pl.BlockSpec(w_kv.shape, lambda i: (0, 0)),
                  pl.BlockSpec((1, KV_LORA), lambda i: (0, 0)),
                  tspec, tspec],
        out_specs=[pl.BlockSpec((bb, tt, KV_LORA), lambda i: ij(i) + (0,)),
                   pl.BlockSpec((bb, tt, MLA_ROPE), lambda i: ij(i) + (0,))],
        out_shape=[jax.ShapeDtypeStruct((B, T, KV_LORA), F32),
                   jax.ShapeDtypeStruct((B, T, MLA_ROPE), F32)],
        compiler_params=_cparams(1),
        name="shared_kv",
    )(x, kv_in_g.reshape(1, D), w_kv, kv_lat_g.reshape(1, KV_LORA), cos32, sin32)


def _kv_expand_kernel(lat_ref, kr_ref, wk_ref, ek_ref, wvt_ref, ones_ref, k_ref, vt_ref):
    lat = lat_ref[0].astype(BF16)
    kr = kr_ref[0].astype(BF16)
    k = _dot(lat, wk_ref[...].astype(BF16)) + _dot(kr, ek_ref[...].astype(BF16))
    k_ref[0] = k.astype(k_ref.dtype)
    vt = _dot_nt(wvt_ref[...].astype(BF16), lat) + ones_ref[...]
    vt_ref[0] = vt.astype(vt_ref.dtype)


def kv_expand(lat, kr, wk_pad, ek, wvt_ext, ones_col, rows=512):
    B, T, _ = lat.shape
    tt = rows
    NK, NVT = wk_pad.shape[1], wvt_ext.shape[0]

    def full(a):
        return pl.BlockSpec(a.shape, lambda b, t: (0, 0))

    def rowspec(n):
        return pl.BlockSpec((1, tt, n), lambda b, t: (b, t, 0))

    return pl.pallas_call(
        _kv_expand_kernel,
        grid=(B, T // tt),
        in_specs=[rowspec(KV_LORA), rowspec(MLA_ROPE), full(wk_pad), full(ek), full(wvt_ext), full(ones_col)],
        out_specs=[rowspec(NK), pl.BlockSpec((1, NVT, tt), lambda b, t: (b, 0, t))],
        out_shape=[jax.ShapeDtypeStruct((B, T, NK), BF16), jax.ShapeDtypeStruct((B, NVT, T), BF16)],
        compiler_params=_cparams(2),
        name="kv_expand",
    )(lat, kr, wk_pad, ek, wvt_ext, ones_col)


def _query_kernel(h_ref, wdq_ref, qg_ref, wq_ref, wqr_ref, c_ref, s_ref, q_ref, wdq_b, wq_b, wqr_b):
    @pl.when(pl.program_id(0) == 0)
    def _():
        wdq_b[...] = wdq_ref[...].astype(BF16)
        wq_b[...] = wq_ref[...].astype(BF16)
        wqr_b[...] = wqr_ref[...].astype(BF16)

    bb, tt, D = h_ref.shape
    h = h_ref[...].reshape(bb * tt, D)
    cq = _rms(_dot(h, wdq_b[...]), qg_ref[...]).astype(BF16)
    q1 = _dot(cq, wq_b[...]).reshape(bb, tt, -1)
    q2 = _dot(cq, wqr_b[...]).reshape(bb, tt, -1)
    c = c_ref[...]
    s = s_ref[...]
    for hd in range(MLA_HEADS):
        sl = slice(hd * HEAD_PAD, (hd + 1) * HEAD_PAD)
        q_ref[:, :, sl] = (q1[:, :, sl] * c + q2[:, :, sl] * s).astype(q_ref.dtype)


def mla_queries(h, w_dq, q_norm_g, wq_pad, wq_rot, l, c128, s128, rows=512):
    B, T, D = h.shape
    bb, tt, nblk, ij = _row_blocks(B, T, rows)
    NQ = wq_pad.shape[-1]
    tspec = pl.BlockSpec((tt, HEAD_PAD), lambda i: (ij(i)[1], 0))
    return pl.pallas_call(
        _query_kernel,
        grid=(nblk,),
        in_specs=[pl.BlockSpec((bb, tt, D), lambda i: ij(i) + (0,)),
                  pl.BlockSpec((None, D, Q_LORA), lambda i: (l, 0, 0)),
                  pl.BlockSpec((None, 1, Q_LORA), lambda i: (l, 0, 0)),
                  pl.BlockSpec((None, Q_LORA, NQ), lambda i: (l, 0, 0)),
                  pl.BlockSpec((None, Q_LORA, NQ), lambda i: (l, 0, 0)),
                  tspec, tspec],
        out_specs=pl.BlockSpec((bb, tt, NQ), lambda i: ij(i) + (0,)),
        out_shape=jax.ShapeDtypeStruct((B, T, NQ), BF16),
        scratch_shapes=[pltpu.VMEM((D, Q_LORA), BF16), pltpu.VMEM((Q_LORA, NQ), BF16),
                        pltpu.VMEM((Q_LORA, NQ), BF16)],
        compiler_params=_cparams(1),
        name="mla_queries",
    )(h, w_dq, q_norm_g.reshape(-1, 1, Q_LORA), wq_pad, wq_rot, c128, s128)


def _attn_prompt_kernel(qi_tab, ki_tab, q_ref, k_ref, vt_ref, o_ref, *scratch, tq, tk):
    H = MLA_HEADS
    m_refs, l_refs, acc_refs = scratch[:H], scratch[H:2 * H], scratch[2 * H:]
    p_id = pl.program_id(1)
    qi = qi_tab[p_id]
    ki = ki_tab[p_id]

    @pl.when(ki == 0)
    def _():
        for hd in range(H):
            m_refs[hd][...] = jnp.full(m_refs[hd].shape, NEG_INF, F32)
            l_refs[hd][...] = jnp.zeros(l_refs[hd].shape, F32)
            acc_refs[hd][...] = jnp.zeros(acc_refs[hd].shape, F32)

    def block(masked):
        if masked:
            kchunk = (ki * tk + lax.broadcasted_iota(jnp.int32, (tk, tq), 0)) // CHUNK
            qchunk = (qi * tq + lax.broadcasted_iota(jnp.int32, (tk, tq), 1)) // CHUNK
            mask = kchunk <= qchunk
        def scores(hd):
            sl = slice(hd * HEAD_PAD, (hd + 1) * HEAD_PAD)
            return _dot_nt(k_ref[0, :, sl], q_ref[0, :, sl])

        pending = [scores(hd) for hd in range(ATTN_LOOKAHEAD)]
        for hd in range(H):
            if hd + ATTN_LOOKAHEAD < H:
                pending.append(scores(hd + ATTN_LOOKAHEAD))
            s = pending.pop(0)
            if masked:
                s = jnp.where(mask, s, NEG_INF)
            m_prev = m_refs[hd][...]
            m_new = jnp.maximum(m_prev, jnp.max(s, axis=0, keepdims=True))
            a = jnp.exp2(m_prev - m_new)
            p = jnp.exp2(s - m_new).astype(BF16)
            pv = _dot(vt_ref[0, hd * V_ROWS:(hd + 1) * V_ROWS, :], p)
            acc_refs[hd][...] = a * acc_refs[hd][...] + pv[:MLA_V]
            l_refs[hd][...] = a * l_refs[hd][...] + pv[MLA_V:MLA_V + 1]
            m_refs[hd][...] = m_new

    @pl.when(ki < qi)
    def _():
        block(False)

    @pl.when(ki == qi)
    def _():
        block(True)
        o_t = jnp.concatenate([acc_refs[hd][...] / l_refs[hd][...] for hd in range(H)], axis=0)
        o_ref[0] = o_t.T.astype(o_ref.dtype)


def attn_prompt(q, k, vt, tq=256):
    B, T, NQ = q.shape
    NVT = vt.shape[1]
    NV = MLA_HEADS * MLA_V
    tk = tq
    assert tq % CHUNK == 0
    nq = T // tq
    pairs = [(a, b) for a in range(nq) for b in range(a + 1)]
    qi_tab = jnp.asarray([a for a, _ in pairs], jnp.int32)
    ki_tab = jnp.asarray([b for _, b in pairs], jnp.int32)
    grid_spec = pltpu.PrefetchScalarGridSpec(
        num_scalar_prefetch=2,
        grid=(B, len(pairs)),
        in_specs=[pl.BlockSpec((1, tq, NQ), lambda b, p, qt, kt: (b, qt[p], 0)),
                  pl.BlockSpec((1, tk, NQ), lambda b, p, qt, kt: (b, kt[p], 0)),
                  pl.BlockSpec((1, NVT, tk), lambda b, p, qt, kt: (b, 0, kt[p]))],
        out_specs=pl.BlockSpec((1, tq, NV), lambda b, p, qt, kt: (b, qt[p], 0)),
        scratch_shapes=([pltpu.VMEM((1, tq), F32)] * (2 * MLA_HEADS)
                        + [pltpu.VMEM((MLA_V, tq), F32)] * MLA_HEADS),
    )
    return pl.pallas_call(
        functools.partial(_attn_prompt_kernel, tq=tq, tk=tk),
        grid_spec=grid_spec,
        out_shape=jax.ShapeDtypeStruct((B, T, NV), BF16),
        compiler_params=_cparams(2),
        name="attn_prompt",
    )(qi_tab, ki_tab, q, k, vt)


def _absorb_kernel(q_ref, m_ref, o_ref):
    o_ref[...] = _dot(q_ref[...], m_ref[...].astype(BF16)).astype(o_ref.dtype)


def absorb_queries(q2d, m_abs):
    N = q2d.shape[0]
    H, _, W = m_abs.shape
    return pl.pallas_call(
        _absorb_kernel,
        grid=(H,),
        in_specs=[pl.BlockSpec((N, HEAD_PAD), lambda h: (0, h)),
                  pl.BlockSpec((None, HEAD_PAD, W), lambda h: (h, 0, 0))],
        out_specs=pl.BlockSpec((None, N, W), lambda h: (h, 0, 0)),
        out_shape=jax.ShapeDtypeStruct((H, N, W), BF16),
        compiler_params=_cparams(1),
        name="absorb_queries",
    )(q2d, m_abs)


def _attn_sample_kernel(q_ref, lat_ref, kr_ref, nlat_ref, nkr_ref, o_ref, m_ref, l_ref, acc_ref):
    kb = pl.program_id(1)
    H, Q, W = q_ref.shape
    q = q_ref[...].reshape(H * Q, W)
    q_lat = q[:, :KV_LORA]
    q_rope = q[:, KV_LORA:KV_LORA + MLA_ROPE]

    def update(lat_tile, kr_tile, n_sub, kr_transposed):
        sub = lat_tile.shape[0] // n_sub
        lats = [lat_tile[j * sub:(j + 1) * sub, :].astype(BF16) for j in range(n_sub)]
        if kr_transposed:
            krs = [kr_tile[:, j * sub:(j + 1) * sub].astype(BF16) for j in range(n_sub)]
            ss = [_dot_nt(q_lat, lat) + _dot(q_rope, kr) for lat, kr in zip(lats, krs)]
        else:
            krs = [kr_tile[j * sub:(j + 1) * sub, :].astype(BF16) for j in range(n_sub)]
            ss = [_dot_nt(q_lat, lat) + _dot_nt(q_rope, kr) for lat, kr in zip(lats, krs)]
        m_prev = m_ref[...]
        m_new = m_prev
        for s in ss:
            m_new = jnp.maximum(m_new, jnp.max(s, axis=-1, keepdims=True))
        a = jnp.exp2(m_prev - m_new)
        ps = [jnp.exp2(s - m_new[:, :1]) for s in ss]
        pv = _dot(ps[0].astype(BF16), lats[0])
        psum = jnp.sum(ps[0], axis=-1, keepdims=True)
        for p, lat in zip(ps[1:], lats[1:]):
            pv = pv + _dot(p.astype(BF16), lat)
            psum = psum + jnp.sum(p, axis=-1, keepdims=True)
        l_ref[...] = a * l_ref[...] + psum
        m_ref[...] = m_new
        acc_ref[...] = jnp.concatenate([a, a], axis=-1) * acc_ref[...] + pv

    @pl.when(kb == 0)
    def _():
        m_ref[...] = jnp.full_like(m_ref, NEG_INF)
        l_ref[...] = jnp.zeros_like(l_ref)
        acc_ref[...] = jnp.zeros_like(acc_ref)
        update(nlat_ref[0], nkr_ref[0], 1, False)

    update(lat_ref[0], kr_ref[0], SAMPLE_KEY_SUB, True)

    @pl.when(kb == pl.num_programs(1) - 1)
    def _():
        lsum = l_ref[...]
        o = acc_ref[...] / jnp.concatenate([lsum, lsum], axis=-1)
        o_ref[...] = o.reshape(H, Q, KV_LORA).astype(o_ref.dtype)


def attn_sample(q_abs, cache_lat, cache_kr_t, new_lat, new_kr, tk=2048):
    H, N, W = q_abs.shape
    B, P, _ = cache_lat.shape
    Q = new_lat.shape[1]
    qpos = P + np.arange(Q)
    kpos = np.arange(P + Q)
    assert bool(np.all((kpos // CHUNK)[None, :] <= (qpos // CHUNK)[:, None]))
    return pl.pallas_call(
        _attn_sample_kernel,
        grid=(B, P // tk),
        in_specs=[pl.BlockSpec((H, Q, W), lambda b, kb: (0, b, 0)),
                  pl.BlockSpec((1, tk, KV_LORA), lambda b, kb: (b, kb, 0)),
                  pl.BlockSpec((1, MLA_ROPE, tk), lambda b, kb: (b, 0, kb)),
                  pl.BlockSpec((1, Q, KV_LORA), lambda b, kb: (b, 0, 0)),
                  pl.BlockSpec((1, Q, MLA_ROPE), lambda b, kb: (b, 0, 0))],
        out_specs=pl.BlockSpec((H, Q, KV_LORA), lambda b, kb: (0, b, 0)),
        out_shape=jax.ShapeDtypeStruct((H, N, KV_LORA), BF16),
        scratch_shapes=[pltpu.VMEM((H * Q, 128), F32), pltpu.VMEM((H * Q, 128), F32),
                        pltpu.VMEM((H * Q, KV_LORA), F32)],
        compiler_params=_cparams(2),
        name="attn_sample",
    )(q_abs, cache_lat, cache_kr_t, new_lat, new_kr)


def _unabsorb_kernel(o_ref, w_ref, out_ref):
    out_ref[...] = (_dot(o_ref[0], w_ref[0].astype(BF16))
                    + _dot(o_ref[1], w_ref[1].astype(BF16))).astype(out_ref.dtype)


def unabsorb(o_lat, wuv_pad):
    H, N, R = o_lat.shape
    return pl.pallas_call(
        _unabsorb_kernel,
        grid=(H // 2,),
        in_specs=[pl.BlockSpec((2, N, R), lambda p: (p, 0, 0)),
                  pl.BlockSpec((2, R, 128), lambda p: (p, 0, 0))],
        out_specs=pl.BlockSpec((N, 128), lambda p: (0, p)),
        out_shape=jax.ShapeDtypeStruct((N, (H // 2) * 128), BF16),
        compiler_params=_cparams(1),
        name="unabsorb",
    )(o_lat, wuv_pad)


def _rope_tables(pos):
    half = MLA_ROPE // 2
    inv = 1.0 / (ROPE_THETA ** (np.arange(half, dtype=np.float64) * 2.0 / MLA_ROPE))
    ang = np.asarray(pos, np.float64)[:, None] * inv[None, :]
    cos = np.concatenate([np.cos(ang), np.cos(ang)], axis=-1)
    sin = np.concatenate([np.sin(ang), np.sin(ang)], axis=-1)
    T = cos.shape[0]
    c128 = np.zeros((T, HEAD_PAD)); s128 = np.zeros((T, HEAD_PAD))
    c128[:, :MLA_NOPE] = 1.0
    c128[:, MLA_NOPE:MLA_NOPE + MLA_ROPE] = cos
    s128[:, MLA_NOPE:MLA_NOPE + MLA_ROPE] = sin
    return (jnp.asarray(cos, F32), jnp.asarray(sin, F32),
            jnp.asarray(c128 * Q_PRESCALE, F32), jnp.asarray(s128 * Q_PRESCALE, F32))


def _rot_half_cols(w):
    half = w.shape[-1] // 2
    return jnp.concatenate([-w[..., half:], w[..., :half]], axis=-1)


def _prep_weights(w_dkv, w_uk, w_uv, w_uq, router_w):
    D = D_MODEL
    w_lat, w_rope = w_dkv[:, :KV_LORA], w_dkv[:, KV_LORA:]
    pad96 = jnp.zeros((D, 128 - MLA_ROPE), F32)
    w_kv = jnp.concatenate([w_lat, w_rope, pad96, _rot_half_cols(w_rope), pad96], axis=-1)

    zpad = HEAD_PAD - MLA_NOPE
    wk_pad = jnp.pad(w_uk, ((0, 0), (0, 0), (0, zpad))).reshape(KV_LORA, MLA_HEADS * HEAD_PAD)
    ek = jnp.zeros((MLA_ROPE, MLA_HEADS, HEAD_PAD), F32)
    ek = ek.at[:, :, MLA_NOPE:MLA_NOPE + MLA_ROPE].set(
        jnp.broadcast_to(jnp.eye(MLA_ROPE, dtype=F32)[:, None, :], (MLA_ROPE, MLA_HEADS, MLA_ROPE)))
    ek = ek.reshape(MLA_ROPE, MLA_HEADS * HEAD_PAD)
    wvt = jnp.transpose(w_uv, (1, 2, 0))
    wvt_ext = jnp.pad(wvt, ((0, 0), (0, V_ROWS - MLA_V), (0, 0))).reshape(MLA_HEADS * V_ROWS, KV_LORA)
    ones_col = jnp.tile((jnp.arange(V_ROWS) >= MLA_V).astype(F32), MLA_HEADS).reshape(-1, 1)

    nb = w_uq.shape[0]
    qn, qr = w_uq[..., :MLA_NOPE], w_uq[..., MLA_NOPE:]
    z32 = jnp.zeros(qr.shape[:-1] + (HEAD_PAD - MLA_NOPE - MLA_ROPE,), F32)
    wq_pad = jnp.concatenate([qn, qr, z32], axis=-1).reshape(nb, Q_LORA, MLA_HEADS * HEAD_PAD)
    wq_rot = jnp.concatenate([jnp.zeros_like(qn), _rot_half_cols(qr), z32], axis=-1)
    wq_rot = wq_rot.reshape(nb, Q_LORA, MLA_HEADS * HEAD_PAD)

    m_abs = jnp.zeros((MLA_HEADS, HEAD_PAD, KV_LORA + 128), F32)
    m_abs = m_abs.at[:, :MLA_NOPE, :KV_LORA].set(jnp.transpose(w_uk, (1, 2, 0)))
    m_abs = m_abs.at[:, MLA_NOPE:MLA_NOPE + MLA_ROPE, KV_LORA:KV_LORA + MLA_ROPE].set(
        jnp.broadcast_to(jnp.eye(MLA_ROPE, dtype=F32), (MLA_HEADS, MLA_ROPE, MLA_ROPE)))

    wuv_h = jnp.transpose(w_uv, (1, 0, 2))
    even = jnp.pad(wuv_h, ((0, 0), (0, 0), (0, 64)))
    odd = jnp.pad(wuv_h, ((0, 0), (0, 0), (64, 0)))
    wuv_pad = jnp.where((jnp.arange(MLA_HEADS) % 2 == 0)[:, None, None], even, odd)

    rw_t = jnp.transpose(router_w, (0, 2, 1))
    return dict(w_kv=w_kv, wk_pad=wk_pad, ek=ek, wvt_ext=wvt_ext, ones_col=ones_col, wq_pad=wq_pad, wq_rot=wq_rot,
                m_abs=m_abs, wuv_pad=wuv_pad, rw_t=rw_t)


def _mixer(st, l, P, W):
    x, m = st["x"], st["mod"][l]
    B, T, _ = x.shape
    n_a = P["hg_w_in"].shape[0]
    h = st.pop("h_next", None)
    if h is None:
        h = norm_mod(x, P["norm1_g"][l], m, sc_idx=1, sh_idx=0)
    norm2 = (P["norm2_g"][l], 4, 3)
    if l < n_a:
        z = linear(h, P["hg_w_in"], l, F32)
        s0 = None if st["hg_state"] is None else st["hg_state"][l]
        o, s_new = gla(z, st["lbs"][l], P["hg_onorm_g"][l], s0)
        st["hg_new"].append(s_new)
        st["x"], st["hp"] = linear(o, P["hg_w_out"], l, F32, x=x, mod=m, gate_idx=2, next_norm=norm2)
    else:
        bi = l - n_a
        q = mla_queries(h, P["w_dq"], P["q_norm_g"], W["wq_pad"], W["wq_rot"], bi, st["c128"], st["s128"])
        if st["past_lat"] is None:
            o = attn_prompt(q, st["k_all"], st["v_all"])
        else:
            q_abs = absorb_queries(q.reshape(B * T, -1), W["m_abs"])
            o_lat = attn_sample(q_abs, st["past_lat"], st["past_kr"], st["lat"], st["kr"])
            o = unabsorb(o_lat, W["wuv_pad"]).reshape(B, T, -1)
        st["x"], st["hp"] = linear(o, P["w_o"], bi, F32, x=x, mod=m, gate_idx=2, next_norm=norm2)


def _moe(groups, l, P, W):
    hp = jnp.concatenate([st["hp"] for st in groups], axis=0)
    n_tok = hp.shape[0]
    n_tiles = (TOP_K * n_tok) // MOE_TILE + N_EXPERTS
    pos, w8, tile_start, tile_count = route(hp, W["rw_t"], P["router_bias"], l, MOE_TILE)
    pos_flat = pos.reshape(-1)
    src = sc_invert(pos_flat, n_tok, n_tiles * MOE_TILE)
    xs = sc_gather(hp, src)
    out = moe_gemm(xs, tile_start[:, 0], tile_count[:, 0], P["exp_w_in"], P["exp_w_out"], l,
                   MOE_TILE, n_tiles)
    y8 = sc_gather(out, pos_flat).reshape(TOP_K, n_tok, -1)
    w_t = w8.T
    row0 = 0
    for st in groups:
        B, T, _ = st["x"].shape
        if l == P["norm1_g"].shape[0] - 1:
            st["x"] = moe_combine(y8, w_t, hp, P["sh_w_in"], P["sh_w_out"], st["x"], st["mod"][l], 5, l, row0,
                                  final_g=P["final_g"])
        else:
            nxt = (P["norm1_g"][l + 1], st["mod"][l + 1], 1, 0)
            st["x"], st["h_next"] = moe_combine(y8, w_t, hp, P["sh_w_in"], P["sh_w_out"], st["x"],
                                                st["mod"][l], 5, l, row0, next_norm=nxt)
        row0 += B * T


def _group_state(x, mod, pos, hg_state, past_lat, past_kr, lbs):
    cos32, sin32, c128, s128 = _rope_tables(pos)
    return dict(x=x, mod=mod, hg_state=hg_state, past_lat=past_lat, past_kr=past_kr, lbs=lbs,
                cos32=cos32, sin32=sin32, c128=c128, s128=s128, hg_new=[],
                lat=None, kr=None, k_all=None, v_all=None)


def kernel(x_prompt, x_sample, state_hgrn, cache_mla_latent, cache_mla_krope, c_prompt, c_sample, ada_w, ada_b, norm1_g, norm2_g, hg_w_in, hg_lb_logits, hg_onorm_g, hg_w_out, kv_in_g, w_dkv, kv_lat_g, w_uk, w_uv, w_dq, q_norm_g, w_uq, w_o, router_w, router_bias, exp_w_in, exp_w_out, sh_w_in, sh_w_out, final_g):
    Bp, Sp, _ = x_prompt.shape
    Bs, Ss, _ = x_sample.shape
    past = cache_mla_latent.shape[1]
    P = dict(norm1_g=norm1_g, norm2_g=norm2_g, hg_w_in=hg_w_in, hg_lb_logits=hg_lb_logits,
             hg_onorm_g=hg_onorm_g, hg_w_out=hg_w_out, kv_in_g=kv_in_g, kv_lat_g=kv_lat_g,
             w_dq=w_dq, q_norm_g=q_norm_g, w_o=w_o, router_bias=router_bias,
             exp_w_in=exp_w_in, exp_w_out=exp_w_out, sh_w_in=sh_w_in, sh_w_out=sh_w_out, final_g=final_g)
    W = _prep_weights(w_dkv, w_uk, w_uv, w_uq, router_w)
    mod = ada_mod(jnp.concatenate([c_prompt, c_sample], axis=0), ada_w, ada_b)
    lbs = jnp.cumsum(jax.nn.softmax(hg_lb_logits.astype(F32), axis=0), axis=0)
    bsz = Bp // PROMPT_STREAMS
    prompts = [_group_state(x_prompt[i * bsz:(i + 1) * bsz], mod[:, i * bsz:(i + 1) * bsz, None, :],
                            np.arange(Sp), None, None, None, lbs) for i in range(PROMPT_STREAMS)]
    gs = _group_state(x_sample, mod[:, Bp:, None, :], past + np.arange(Ss), state_hgrn,
                      cache_mla_latent, jnp.transpose(cache_mla_krope, (0, 2, 1)), lbs)
    streams = [[prompts[0], gs]] + [[g] for g in prompts[1:]]
    n_a = hg_w_in.shape[0]
    for l in range(norm1_g.shape[0]):
        for groups in streams:
            for st in groups:
                _mixer(st, l, P, W)
            _moe(groups, l, P, W)
            if l == n_a - 1:
                for st in groups:
                    st["lat"], st["kr"] = shared_kv(st["x"], kv_in_g, W["w_kv"], kv_lat_g, st["cos32"], st["sin32"])
                    if st["past_lat"] is None:
                        st["k_all"], st["v_all"] = kv_expand(st["lat"], st["kr"], W["wk_pad"], W["ek"],
                                                             W["wvt_ext"], W["ones_col"])
    for st in prompts + [gs]:
        st["y"] = st["x"]
        st["hg_out"] = jnp.stack(st["hg_new"], axis=0)

    def cat(key, axis=0):
        return jnp.concatenate([g[key] for g in prompts], axis=axis)

    return (cat("y"), gs["y"], cat("hg_out", 1), gs["hg_out"], cat("lat"), cat("kr"), gs["lat"], gs["kr"])
```

```python
import dataclasses
import functools

import numpy as np
import jax
import jax.numpy as jnp
from jax import lax
from jax.experimental import pallas as pl
from jax.experimental.pallas import tpu as pltpu
from jax.experimental.pallas import tpu_sc as plsc

F32 = jnp.float32
BF16 = jnp.bfloat16

D_MODEL = 1024
CHUNK = 64
HG_HEADS = 8
HG_DK = 128
HG_DV = 128
MLA_HEADS = 16
MLA_NOPE = 64
MLA_ROPE = 32
MLA_V = 64
Q_LORA = 384
KV_LORA = 256
ROPE_THETA = 10000.0
N_EXPERTS = 64
TOP_K = 8
N_GROUPS = 8
TOPK_GROUPS = 4
EXPERT_FF = 256
SHARED_FF = 256
ROUTED_SCALE = 2.5
EPS = 1e-6

HEAD_PAD = 128
SAMPLE_KEY_SUB = 4
ATTN_LOOKAHEAD = 4
V_ROWS = MLA_V + 16
QK_SCALE = (MLA_NOPE + MLA_ROPE) ** -0.5
Q_PRESCALE = QK_SCALE * float(np.log2(np.e))
VMEM_LIMIT = 56 * 1024 * 1024
NEG_INF = float("-inf")
SC_CORES = 2
SC_SUBCORES = 16
SC_WORKERS = SC_CORES * SC_SUBCORES
SC_LANES = 16
SC_WINDOW = 64
MOE_TILE = 512
PROMPT_STREAMS = 1
MOE_NBUF = 4
MOE_SUB = 2


def _cparams(n_axes):
    return pltpu.CompilerParams(dimension_semantics=("arbitrary",) * n_axes,
                                vmem_limit_bytes=VMEM_LIMIT)


def _silu(x):
    return x * jax.nn.sigmoid(x)


def _rms(x, g):
    ms = jnp.mean(x * x, axis=-1, keepdims=True)
    return x * lax.rsqrt(ms + EPS) * g


def _dot(a, b):
    return jnp.dot(a, b, preferred_element_type=F32)


def _dot_nt(a, b):
    return lax.dot_general(a, b, (((1,), (1,)), ((), ())), preferred_element_type=F32)


def _dot_tn(a, b):
    return lax.dot_general(a, b, (((0,), (0,)), ((), ())), preferred_element_type=F32)


def _row_blocks(B, T, rows):
    if T >= rows:
        assert T % rows == 0
        bb, tt = 1, rows
    else:
        assert rows % T == 0 and B % (rows // T) == 0
        bb, tt = rows // T, T
    nt = T // tt
    return bb, tt, (B // bb) * nt, (lambda i: (i // nt, i % nt))


def _ada_kernel(c_ref, w_ref, b_ref, o_ref):
    a = _silu(c_ref[...]).astype(BF16)
    o_ref[...] = _dot(a, w_ref[...].astype(BF16)) + b_ref[...]


def ada_mod(c, ada_w, ada_b):
    R, D = c.shape
    L, _, N = ada_w.shape
    tn = 1536
    return pl.pallas_call(
        _ada_kernel,
        grid=(L, N // tn),
        in_specs=[pl.BlockSpec((R, D), lambda l, j: (0, 0)),
                  pl.BlockSpec((None, D, tn), lambda l, j: (l, 0, j)),
                  pl.BlockSpec((None, 1, tn), lambda l, j: (l, 0, j))],
        out_specs=pl.BlockSpec((None, R, tn), lambda l, j: (l, 0, j)),
        out_shape=jax.ShapeDtypeStruct((L, R, N), F32),
        compiler_params=_cparams(2),
        name="ada_mod",
    )(c, ada_w, ada_b.reshape(L, 1, N))


def _pack_pairs(y):
    half = y.shape[-1] // 2
    bits = lax.bitcast_convert_type(y.astype(BF16).astype(F32), jnp.uint32)
    word = lax.shift_right_logical(bits[:, :half], jnp.uint32(16)) | bits[:, half:]
    return lax.bitcast_convert_type(word, jnp.int32)


def _unpack_pairs(word, dtype=BF16):
    u = lax.bitcast_convert_type(word, jnp.uint32)
    lo = lax.bitcast_convert_type(lax.shift_left(u, jnp.uint32(16)), F32)
    hi = lax.bitcast_convert_type(u & jnp.uint32(0xFFFF0000), F32)
    return lo.astype(dtype), hi.astype(dtype)


def _norm_kernel(*refs, modulated, packed):
    if modulated:
        x_ref, g_ref, sc_ref, sh_ref, o_ref = refs
    else:
        x_ref, g_ref, o_ref = refs
    y = _rms(x_ref[...], g_ref[...])
    if modulated:
        y = y * (1.0 + sc_ref[...]) + sh_ref[...]
    if packed:
        bb, tt, D = y.shape
        o_ref[...] = _pack_pairs(y.reshape(bb * tt, D))
    else:
        o_ref[...] = y.astype(o_ref.dtype)


def norm_mod(x, g, mod=None, sc_idx=0, sh_idx=0, out_dtype=BF16, rows=512, packed=False):
    B, T, D = x.shape
    bb, tt, nblk, ij = _row_blocks(B, T, rows)
    xspec = pl.BlockSpec((bb, tt, D), lambda i: ij(i) + (0,))
    in_specs = [xspec, pl.BlockSpec((1, D), lambda i: (0, 0))]
    args = [x, g.reshape(1, D)]
    if mod is not None:
        in_specs += [pl.BlockSpec((bb, 1, D), lambda i: (ij(i)[0], 0, sc_idx)),
                     pl.BlockSpec((bb, 1, D), lambda i: (ij(i)[0], 0, sh_idx))]
        args += [mod, mod]
    if packed:
        out_specs = pl.BlockSpec((bb * tt, D // 2), lambda i: (i, 0))
        out_shape = jax.ShapeDtypeStruct((B * T, D // 2), jnp.int32)
    else:
        out_specs = xspec
        out_shape = jax.ShapeDtypeStruct((B, T, D), out_dtype)
    return pl.pallas_call(
        functools.partial(_norm_kernel, modulated=mod is not None, packed=packed),
        grid=(nblk,),
        in_specs=in_specs,
        out_specs=out_specs,
        out_shape=out_shape,
        compiler_params=_cparams(1),
        name="norm_mod",
    )(*args)


def _linear_kernel(*refs, residual, norm_next):
    if norm_next:
        a_ref, w_ref, x_ref, gate_ref, ng_ref, nsc_ref, nsh_ref, o_ref, hp_ref, wb_ref = refs
    elif residual:
        a_ref, w_ref, x_ref, gate_ref, o_ref, wb_ref = refs
    else:
        a_ref, w_ref, o_ref, wb_ref = refs

    @pl.when(pl.program_id(1) == 0)
    def _():
        wb_ref[...] = w_ref[...].astype(BF16)

    bb, tt, K = a_ref.shape
    y = _dot(a_ref[...].reshape(bb * tt, K).astype(BF16), wb_ref[...])
    y = y.reshape(bb, tt, y.shape[-1])
    if residual:
        y = x_ref[...] + gate_ref[...] * y
    o_ref[...] = y.astype(o_ref.dtype)
    if norm_next:
        h = _rms(y, ng_ref[...]) * (1.0 + nsc_ref[...]) + nsh_ref[...]
        hp_ref[...] = _pack_pairs(h.reshape(bb * tt, h.shape[-1]))


def linear(a, w, l, out_dtype, x=None, mod=None, gate_idx=0, rows=512, tn=1024, next_norm=None):
    B, T, K = a.shape
    _, _, N = w.shape
    tn = min(tn, N)
    bb, tt, nblk, ij = _row_blocks(B, T, rows)
    in_specs = [pl.BlockSpec((bb, tt, K), lambda j, i: ij(i) + (0,)),
                pl.BlockSpec((None, K, tn), lambda j, i: (l, 0, j))]
    args = [a, w]
    ospec = pl.BlockSpec((bb, tt, tn), lambda j, i: ij(i) + (j,))
    out_specs = ospec
    out_shape = jax.ShapeDtypeStruct((B, T, N), out_dtype)
    if x is not None:
        gsteps = D_MODEL // tn
        in_specs += [ospec, pl.BlockSpec((bb, 1, tn), lambda j, i: (ij(i)[0], 0, gate_idx * gsteps + j))]
        args += [x, mod]
    if next_norm is not None:
        assert x is not None and tn == N
        gain, sc_idx, sh_idx = next_norm
        in_specs += [pl.BlockSpec((1, N), lambda j, i: (0, 0)),
                     pl.BlockSpec((bb, 1, N), lambda j, i: (ij(i)[0], 0, sc_idx)),
                     pl.BlockSpec((bb, 1, N), lambda j, i: (ij(i)[0], 0, sh_idx))]
        args += [gain.reshape(1, N), mod, mod]
        out_specs = [ospec, pl.BlockSpec((bb * tt, N // 2), lambda j, i: (i, 0))]
        out_shape = [out_shape, jax.ShapeDtypeStruct((B * T, N // 2), jnp.int32)]
    return pl.pallas_call(
        functools.partial(_linear_kernel, residual=x is not None, norm_next=next_norm is not None),
        grid=(N // tn, nblk),
        in_specs=in_specs,
        out_specs=out_specs,
        out_shape=out_shape,
        scratch_shapes=[pltpu.VMEM((K, tn), BF16)],
        compiler_params=_cparams(2),
        name="linear",
    )(*args)


def _gla_kernel(*refs, L, n_chunks, has_init):
    if has_init:
        q_ref, f_ref, i_ref, g_ref, lb_ref, on_ref, s0_ref, o_ref, so_ref, st_ref = refs
    else:
        q_ref, f_ref, i_ref, g_ref, lb_ref, on_ref, o_ref, so_ref, st_ref = refs
    t = pl.program_id(1)
    H = st_ref.shape[0]

    @pl.when(t == 0)
    def _():
        for h in range(H):
            if has_init:
                st_ref[h] = s0_ref[0, h].T
            else:
                st_ref[h] = jnp.zeros(st_ref.shape[1:], F32)

    lb = lb_ref[...]
    onorm = on_ref[...]
    row = lax.broadcasted_iota(jnp.int32, (L, L), 0)
    col = lax.broadcasted_iota(jnp.int32, (L, L), 1)
    causal = col <= row
    tri = causal.astype(BF16)
    mid = L // 2 - 1

    def chunk(c, carry):
        r0 = pl.multiple_of(c * L, L)
        q = _silu(q_ref[0, pl.ds(r0, L), :])
        fg = lb + (1.0 - lb) * jax.nn.sigmoid(f_ref[0, pl.ds(r0, L), :])
        k = 1.0 - fg
        v = i_ref[0, pl.ds(r0, L), :].astype(BF16)
        gate = _silu(g_ref[0, pl.ds(r0, L), :])
        logf = jnp.log(fg)
        hi = logf.astype(BF16)
        lo = (logf - hi.astype(F32)).astype(BF16)
        b = _dot(tri, hi) + _dot(tri, lo)
        b_mid = b[mid:mid + 1, :]
        b_last = b[L - 1:L, :]
        up = jnp.exp(b - b_mid)
        down = jnp.exp(b_mid - b)
        qa = q * up
        kb = k * down
        qe = (qa * jnp.exp(b_mid)).astype(BF16)
        kd = (kb * jnp.exp(b_last - b_mid)).astype(BF16)
        qa = qa.astype(BF16)
        kb = kb.astype(BF16)
        decay = jnp.exp(b_last)
        sls = [slice(h * HG_DK, (h + 1) * HG_DK) for h in range(H)]
        sts = [st_ref[h] for h in range(H)]
        scores = [_dot_nt(qa[:, sl], kb[:, sl]) for sl in sls]
        inter = [_dot_nt(qe[:, sl], st.astype(BF16)) for sl, st in zip(sls, sts)]
        outer = [_dot_tn(v[:, sl], kd[:, sl]) for sl in sls]
        intra = [_dot(jnp.where(causal, sc, 0.0).astype(BF16), v[:, sl]) for sc, sl in zip(scores, sls)]
        for h, sl in enumerate(sls):
            st_ref[h] = sts[h] * decay[:, sl] + outer[h]
            o = _rms(inter[h] + intra[h], onorm[:, sl]) * gate[:, sl]
            o_ref[0, pl.ds(r0, L), sl] = o.astype(o_ref.dtype)
        return carry

    lax.fori_loop(0, n_chunks, chunk, 0, unroll=4 if n_chunks % 4 == 0 else 1)

    @pl.when(t == pl.num_programs(1) - 1)
    def _():
        for h in range(H):
            so_ref[0, h] = st_ref[h].T


def gla(z, lb, onorm_g, s0):
    B, T, _ = z.shape
    L = CHUNK if T % CHUNK == 0 else T
    tt = min(T, 512)
    n_chunks = tt // L
    H, D = HG_HEADS, D_MODEL

    def zspec(part):
        return pl.BlockSpec((1, tt, D), lambda b, t: (b, t, part))

    hspec = pl.BlockSpec((1, D), lambda b, t: (0, 0))
    sspec = pl.BlockSpec((1, H, HG_DK, HG_DV), lambda b, t: (b, 0, 0, 0))
    in_specs = [zspec(0), zspec(1), zspec(2), zspec(3), hspec, hspec]
    args = [z, z, z, z, lb.reshape(1, D), onorm_g.reshape(1, D)]
    if s0 is not None:
        in_specs.append(sspec)
        args.append(s0)
    return pl.pallas_call(
        functools.partial(_gla_kernel, L=L, n_chunks=n_chunks, has_init=s0 is not None),
        grid=(B, T // tt),
        in_specs=in_specs,
        out_specs=[pl.BlockSpec((1, tt, D), lambda b, t: (b, t, 0)), sspec],
        out_shape=[jax.ShapeDtypeStruct((B, T, D), BF16),
                   jax.ShapeDtypeStruct((B, H, HG_DK, HG_DV), F32)],
        scratch_shapes=[pltpu.VMEM((H, HG_DV, HG_DK), F32)],
        compiler_params=_cparams(2),
        name="gla",
    )(*args)


def _route_kernel(h_ref, rw_ref, bias_ref, pos_ref, w_ref, te_ref, nu_ref,
                  e_s, r_s, base_s, start_s, *, tile_rows):
    ph = pl.program_id(0)
    i = pl.program_id(1)
    M = h_ref.shape[0]
    half = h_ref.shape[1]
    G, E = N_GROUPS, N_EXPERTS // N_GROUPS
    e_flat = lax.broadcasted_iota(jnp.int32, (N_EXPERTS, M), 0)

    @pl.when(ph == 1)
    def _():
        @pl.when(i == 0)
        def _():
            cnt = base_s[...]
            padded = jnp.floor((cnt + (tile_rows - 1)) * (1.0 / tile_rows)) * tile_rows
            r = lax.broadcasted_iota(jnp.int32, (N_EXPERTS, N_EXPERTS), 0)
            c = lax.broadcasted_iota(jnp.int32, (N_EXPERTS, N_EXPERTS), 1)
            start = jnp.dot((c < r).astype(F32), padded, preferred_element_type=F32,
                            precision=lax.Precision.HIGHEST)
            start_s[...] = start
            te_ref[...] = (start * (1.0 / tile_rows)).astype(jnp.int32)
            nu_ref[...] = (padded * (1.0 / tile_rows)).astype(jnp.int32)

        start_col = start_s[:, :1]
        for k in range(TOP_K):
            hit = e_flat == e_s[i, k:k + 1, :]
            seg = jnp.sum(jnp.where(hit, start_col, 0.0), axis=0, keepdims=True)
            pos_ref[k:k + 1, :] = (seg + r_s[i, k:k + 1, :]).astype(jnp.int32)

    @pl.when(ph == 0)
    def _():
        _route_pass0(h_ref, rw_ref, bias_ref, w_ref, e_s, r_s, base_s, i, M, half, G, E)


def _route_pass0(h_ref, rw_ref, bias_ref, w_ref, e_s, r_s, base_s, i, M, half, G, E):
    @pl.when(i == 0)
    def _():
        base_s[...] = jnp.zeros_like(base_s)

    lo, hi = _unpack_pairs(h_ref[...])
    rw = rw_ref[...].astype(BF16)
    logits = _dot_nt(rw[:, :half], lo) + _dot_nt(rw[:, half:], hi)
    s = jax.nn.sigmoid(logits)
    sb = (s + bias_ref[...]).reshape(G, E, M)
    s = s.reshape(G, E, M)
    e_in = lax.broadcasted_iota(jnp.int32, (G, E, M), 1).astype(F32)
    g_id = lax.broadcasted_iota(jnp.int32, (G, 1, M), 0)
    e_id = lax.broadcasted_iota(jnp.int32, (G, E, M), 0).astype(F32) * E + e_in

    def all_max(a):
        return jnp.max(jnp.max(a, axis=0, keepdims=True), axis=1, keepdims=True)

    def all_min(a):
        return jnp.min(jnp.min(a, axis=0, keepdims=True), axis=1, keepdims=True)

    def all_sum(a):
        return jnp.sum(jnp.sum(a, axis=0, keepdims=True), axis=1, keepdims=True)

    m1 = jnp.max(sb, axis=1, keepdims=True)
    first = jnp.min(jnp.where(sb == m1, e_in, float(E)), axis=1, keepdims=True)
    m2 = jnp.max(jnp.where(e_in == first, NEG_INF, sb), axis=1, keepdims=True)
    gs = m1 + m2

    rank = jnp.zeros((G, 1, M), jnp.int32)
    for j in range(G):
        gj = gs[j:j + 1]
        beats = (gj > gs) | ((gj == gs) & (j < g_id))
        rank = rank + beats.astype(jnp.int32)
    gsel = rank < TOPK_GROUPS

    vals = jnp.where(gsel, sb, NEG_INF)
    selm = jnp.zeros((G, E, M), F32)
    chosen, score = [], []
    for _ in range(TOP_K):
        m = all_max(vals)
        first = all_min(jnp.where(vals == m, e_id, float(N_EXPERTS)))
        hit = e_id == first
        score.append(all_sum(jnp.where(hit, s, 0.0)))
        selm = jnp.where(hit, 1.0, selm)
        vals = jnp.where(hit, NEG_INF, vals)
        chosen.append(first)

    tot = score[0]
    for sc in score[1:]:
        tot = tot + sc
    norm = ROUTED_SCALE / tot

    selm = selm.reshape(N_EXPERTS, M)
    earlier = (lax.broadcasted_iota(jnp.int32, (M, M), 0)
               < lax.broadcasted_iota(jnp.int32, (M, M), 1)).astype(BF16)
    rank = (base_s[:, :1] + _dot(selm.astype(BF16), earlier)).reshape(G, E, M)
    base_s[...] = base_s[...] + jnp.sum(selm, axis=1, keepdims=True)
    for k in range(TOP_K):
        hit = e_id == chosen[k]
        e_s[i, k:k + 1, :] = chosen[k].reshape(1, M).astype(jnp.int32)
        r_s[i, k:k + 1, :] = all_sum(jnp.where(hit, rank, 0.0)).reshape(1, M)
        w_ref[k:k + 1, :] = (score[k] * norm).reshape(1, M)


def route(hp, router_w_t, router_bias, l, tile_rows, rows=512):
    N, half = hp.shape
    M = rows
    nT = N // M
    assert N % M == 0

    def p0(ph, i):
        return i * (1 - ph) + (nT - 1) * ph

    return pl.pallas_call(
        functools.partial(_route_kernel, tile_rows=tile_rows),
        grid=(2, nT),
        in_specs=[pl.BlockSpec((M, half), lambda ph, i: (p0(ph, i), 0)),
                  pl.BlockSpec((None, N_EXPERTS, 2 * half), lambda ph, i: (l, 0, 0)),
                  pl.BlockSpec((None, N_EXPERTS, 1), lambda ph, i: (l, 0, 0))],
        out_specs=[pl.BlockSpec((TOP_K, M), lambda ph, i: (0, i * ph)),
                   pl.BlockSpec((TOP_K, M), lambda ph, i: (0, p0(ph, i))),
                   pl.BlockSpec((N_EXPERTS, 128), lambda ph, i: (0, 0)),
                   pl.BlockSpec((N_EXPERTS, 128), lambda ph, i: (0, 0))],
        out_shape=[jax.ShapeDtypeStruct((TOP_K, N), jnp.int32),
                   jax.ShapeDtypeStruct((TOP_K, N), F32),
                   jax.ShapeDtypeStruct((N_EXPERTS, 128), jnp.int32),
                   jax.ShapeDtypeStruct((N_EXPERTS, 128), jnp.int32)],
        scratch_shapes=[pltpu.VMEM((nT, TOP_K, M), jnp.int32), pltpu.VMEM((nT, TOP_K, M), F32),
                        pltpu.VMEM((N_EXPERTS, 128), F32), pltpu.VMEM((N_EXPERTS, 128), F32)],
        compiler_params=_cparams(2),
        name="route",
    )(hp, router_w_t, router_bias.reshape(-1, N_EXPERTS, 1))


def _sc_mesh():
    return plsc.VectorSubcoreMesh(core_axis_name="core", subcore_axis_name="subcore")


def sc_invert(pos_flat, n_tok, n_out):
    n = pos_flat.shape[0]
    per = n_out // SC_WORKERS
    chunk = n_tok
    assert n_out % SC_WORKERS == 0 and per % SC_LANES == 0
    assert n_tok % chunk == 0 and n % chunk == 0 and chunk % SC_LANES == 0
    cp = pltpu.CompilerParams()
    if "needs_layout_passes" in pltpu.CompilerParams.__dataclass_fields__:
        cp = dataclasses.replace(cp, needs_layout_passes=False)

    @functools.partial(
        pl.kernel, out_type=jax.ShapeDtypeStruct((n_out,), jnp.int32), mesh=_sc_mesh(),
        scratch_types=[pltpu.VMEM((chunk,), jnp.int32), pltpu.VMEM((per,), jnp.int32)],
        compiler_params=cp, name="sc_invert")
    def k(pos_hbm, src_hbm, pos_v, src_v):
        wid = lax.axis_index("subcore") * SC_CORES + lax.axis_index("core")
        lo = wid * per
        lane = lax.iota(jnp.int32, SC_LANES)

        @pl.loop(0, per, step=SC_LANES)
        def _(r):
            src_v[pl.ds(r, SC_LANES)] = lax.rem(lo + r + lane, n_tok)

        @pl.loop(0, n // chunk)
        def _(c):
            base = c * chunk
            pltpu.sync_copy(pos_hbm.at[pl.ds(base, chunk)], pos_v)
            tok0 = lax.rem(base, n_tok)

            @plsc.parallel_loop(0, chunk, step=SC_LANES, unroll=8)
            def _(r):
                p = pos_v[pl.ds(r, SC_LANES)] - lo
                mine = (p >= 0) & (p < per)
                plsc.store_scatter(src_v, [jnp.where(mine, p, 0)], tok0 + r + lane, mask=mine)

        pltpu.sync_copy(src_v, src_hbm.at[pl.ds(lo, per)])

    return k(pos_flat)


def sc_gather(x, idx):
    n = idx.shape[0]
    dim = x.shape[1]
    assert n % (SC_WINDOW * SC_WORKERS) == 0

    @functools.partial(
        pl.kernel, out_type=jax.ShapeDtypeStruct((n, dim), x.dtype), mesh=_sc_mesh(),
        scratch_types=[], name="sc_gather")
    def k(x_hbm, i_hbm, o_hbm):
        def body(i_vmem, o_vmem):
            pltpu.sync_copy(x_hbm.at[i_vmem.at[0]], o_vmem)

        pltpu.emit_pipeline(
            body, grid=(n // SC_WINDOW,),
            in_specs=[pl.BlockSpec((1, SC_WINDOW), index_map=lambda i: (i, 0))],
            out_specs=[pl.BlockSpec((SC_WINDOW, dim), index_map=lambda i: (i, 0))],
            core_axis_name=("core", "subcore"),
            dimension_semantics=(pltpu.PARALLEL,),
        )(i_hbm, o_hbm)

    return k(x, idx.reshape(n // SC_WINDOW, SC_WINDOW))


def _moe_gemm_kernel(ts_ref, tn_ref, x_hbm, wi_ref, wo_ref, o_hbm, wi_b, wo_b, xbuf, obuf, in_sem, out_sem,
                     *, tile_rows, n_tiles):
    e = pl.program_id(0)
    last = pl.num_programs(0) - 1
    t0 = ts_ref[e]
    n = tn_ref[e]
    n_used = ts_ref[last] + tn_ref[last]

    def x_copy(g, slot):
        rows = pl.ds(pl.multiple_of(g * tile_rows, tile_rows), tile_rows)
        return pltpu.make_async_copy(x_hbm.at[rows], xbuf.at[slot], in_sem.at[slot])

    def o_copy(g, slot):
        rows = pl.ds(pl.multiple_of(g * tile_rows, tile_rows), tile_rows)
        return pltpu.make_async_copy(obuf.at[slot], o_hbm.at[rows], out_sem.at[slot])

    @pl.when(e == 0)
    def _():
        for g0 in range(MOE_NBUF - 1):
            @pl.when(g0 < n_used)
            def _():
                x_copy(g0, g0).start()

    @pl.when(n > 0)
    def _():
        wi_b[...] = wi_ref[...].astype(BF16)
        wo_b[...] = wo_ref[...].astype(BF16)

    def tile(i, carry):
        g = t0 + i
        slot = lax.rem(g, MOE_NBUF)
        x_copy(g, slot).wait()
        ahead = g + (MOE_NBUF - 1)

        @pl.when(ahead < n_used)
        def _():
            x_copy(ahead, lax.rem(ahead, MOE_NBUF)).start()

        @pl.when(g >= MOE_NBUF)
        def _():
            o_copy(g - MOE_NBUF, slot).wait()

        rows = tile_rows // MOE_SUB
        half = xbuf.shape[2]
        xs = [_unpack_pairs(xbuf[slot, r * rows:(r + 1) * rows, :]) for r in range(MOE_SUB)]
        hus = [_dot(lo, wi_b[:half, :]) + _dot(hi, wi_b[half:, :]) for lo, hi in xs]
        acts = [(_silu(hu[:, :EXPERT_FF]) * hu[:, EXPERT_FF:]).astype(BF16) for hu in hus]
        outs = [_dot(act, wo_b[...]) for act in acts]
        for r, out in enumerate(outs):
            obuf[slot, r * rows:(r + 1) * rows, :] = _pack_pairs(out)
        o_copy(g, slot).start()
        return carry

    lax.fori_loop(0, n, tile, 0)

    @pl.when(e == last)
    def _():
        for back in range(MOE_NBUF, 0, -1):
            @pl.when(n_used >= back)
            def _():
                o_copy(n_used - back, lax.rem(n_used - back, MOE_NBUF)).wait()

        obuf[0] = jnp.zeros(obuf.shape[1:], obuf.dtype)

        def clear(g, carry):
            cp = o_copy(g, 0)
            cp.start()
            cp.wait()
            return carry

        lax.fori_loop(n_used, n_tiles, clear, 0)


def moe_gemm(xs, tile_start, tile_count, exp_w_in, exp_w_out, l, tile_rows, n_tiles):
    P, half = xs.shape
    D = 2 * half
    assert P == n_tiles * tile_rows
    hbm = pl.BlockSpec(memory_space=pl.ANY)
    grid_spec = pltpu.PrefetchScalarGridSpec(
        num_scalar_prefetch=2,
        grid=(N_EXPERTS,),
        in_specs=[hbm,
                  pl.BlockSpec((None, None, D, 2 * EXPERT_FF), lambda e, ts, tn: (l, e, 0, 0)),
                  pl.BlockSpec((None, None, EXPERT_FF, D), lambda e, ts, tn: (l, e, 0, 0))],
        out_specs=hbm,
        scratch_shapes=[pltpu.VMEM((D, 2 * EXPERT_FF), BF16), pltpu.VMEM((EXPERT_FF, D), BF16),
                        pltpu.VMEM((MOE_NBUF, tile_rows, half), jnp.int32),
                        pltpu.VMEM((MOE_NBUF, tile_rows, half), jnp.int32),
                        pltpu.SemaphoreType.DMA((MOE_NBUF,)), pltpu.SemaphoreType.DMA((MOE_NBUF,))],
    )
    return pl.pallas_call(
        functools.partial(_moe_gemm_kernel, tile_rows=tile_rows, n_tiles=n_tiles),
        grid_spec=grid_spec,
        out_shape=jax.ShapeDtypeStruct((P, half), jnp.int32),
        compiler_params=_cparams(1),
        name="moe_gemm",
    )(tile_start, tile_count, xs, exp_w_in, exp_w_out)


def _moe_combine_kernel(*refs, final, norm_next):
    if final:
        y_ref, w_ref, h_ref, si_ref, so_ref, x_ref, g2_ref, fg_ref, o_ref, si_b, so_b = refs
    elif norm_next:
        (y_ref, w_ref, h_ref, si_ref, so_ref, x_ref, g2_ref, ng_ref, nsc_ref, nsh_ref,
         o_ref, hn_ref, si_b, so_b) = refs
    else:
        y_ref, w_ref, h_ref, si_ref, so_ref, x_ref, g2_ref, o_ref, si_b, so_b = refs

    @pl.when(pl.program_id(0) == 0)
    def _():
        si_b[...] = si_ref[...].astype(BF16)
        so_b[...] = so_ref[...].astype(BF16)

    bb, tt, D = x_ref.shape
    half = D // 2
    w = w_ref[...]
    acc_lo = jnp.zeros((bb * tt, half), F32)
    acc_hi = jnp.zeros((bb * tt, half), F32)
    for k in range(TOP_K):
        lo, hi = _unpack_pairs(y_ref[k], F32)
        acc_lo = acc_lo + w[:, k:k + 1] * lo
        acc_hi = acc_hi + w[:, k:k + 1] * hi
    hlo, hhi = _unpack_pairs(h_ref[...])
    hu = _dot(hlo, si_b[:half, :]) + _dot(hhi, si_b[half:, :])
    act = (_silu(hu[:, :SHARED_FF]) * hu[:, SHARED_FF:]).astype(BF16)
    y = jnp.concatenate([acc_lo, acc_hi], axis=-1) + _dot(act, so_b[...])
    x_new = x_ref[...] + g2_ref[...] * y.reshape(bb, tt, D)
    o_ref[...] = _rms(x_new, fg_ref[...]) if final else x_new
    if norm_next:
        hn = _rms(x_new, ng_ref[...]) * (1.0 + nsc_ref[...]) + nsh_ref[...]
        hn_ref[...] = hn.astype(hn_ref.dtype)


def moe_combine(y8, w_t, hp, sh_w_in, sh_w_out, x, mod, gate_idx, l, row0, final_g=None, next_norm=None,
                rows=256):
    B, T, D = x.shape
    half = D // 2
    bb, tt, nblk, ij = _row_blocks(B, T, rows)
    M = bb * tt
    assert row0 % M == 0
    off = row0 // M
    xspec = pl.BlockSpec((bb, tt, D), lambda i: ij(i) + (0,))
    in_specs = [pl.BlockSpec((TOP_K, M, half), lambda i: (0, off + i, 0)),
                pl.BlockSpec((M, TOP_K), lambda i: (off + i, 0)),
                pl.BlockSpec((M, half), lambda i: (off + i, 0)),
                pl.BlockSpec((None, D, 2 * SHARED_FF), lambda i: (l, 0, 0)),
                pl.BlockSpec((None, SHARED_FF, D), lambda i: (l, 0, 0)),
                xspec,
                pl.BlockSpec((bb, 1, D), lambda i: (ij(i)[0], 0, gate_idx))]
    args = [y8, w_t, hp, sh_w_in, sh_w_out, x, mod]
    out_specs = xspec
    out_shape = jax.ShapeDtypeStruct((B, T, D), F32)
    if final_g is not None:
        assert next_norm is None
        in_specs.append(pl.BlockSpec((1, D), lambda i: (0, 0)))
        args.append(final_g.reshape(1, D))
    if next_norm is not None:
        gain, mod_next, sc_idx, sh_idx = next_norm
        in_specs += [pl.BlockSpec((1, D), lambda i: (0, 0)),
                     pl.BlockSpec((bb, 1, D), lambda i: (ij(i)[0], 0, sc_idx)),
                     pl.BlockSpec((bb, 1, D), lambda i: (ij(i)[0], 0, sh_idx))]
        args += [gain.reshape(1, D), mod_next, mod_next]
        out_specs = [xspec, xspec]
        out_shape = [out_shape, jax.ShapeDtypeStruct((B, T, D), BF16)]
    return pl.pallas_call(
        functools.partial(_moe_combine_kernel, final=final_g is not None, norm_next=next_norm is not None),
        grid=(nblk,),
        in_specs=in_specs,
        out_specs=out_specs,
        out_shape=out_shape,
        scratch_shapes=[pltpu.VMEM((D, 2 * SHARED_FF), BF16), pltpu.VMEM((SHARED_FF, D), BF16)],
        compiler_params=_cparams(1),
        name="moe_combine",
    )(*args)


def _shared_kv_kernel(x_ref, g_ref, w_ref, lg_ref, cos_ref, sin_ref, lat_ref, kr_ref):
    bb, tt, D = x_ref.shape
    xn = _rms(x_ref[...], g_ref[...]).reshape(bb * tt, D).astype(BF16)
    z = _dot(xn, w_ref[...].astype(BF16))
    lat = _rms(z[:, :KV_LORA], lg_ref[...])
    lat_ref[...] = lat.reshape(bb, tt, KV_LORA)
    zr = z[:, KV_LORA:KV_LORA + MLA_ROPE].reshape(bb, tt, MLA_ROPE)
    zq = z[:, KV_LORA + 128:KV_LORA + 128 + MLA_ROPE].reshape(bb, tt, MLA_ROPE)
    kr_ref[...] = zr * cos_ref[...] + zq * sin_ref[...]


def shared_kv(x, kv_in_g, w_kv, kv_lat_g, cos32, sin32, rows=512):
    B, T, D = x.shape
    bb, tt, nblk, ij = _row_blocks(B, T, rows)
    tspec = pl.BlockSpec((tt, MLA_ROPE), lambda i: (ij(i)[1], 0))
    return pl.pallas_call(
        _shared_kv_kernel,
        grid=(nblk,),
        in_specs=[pl.BlockSpec((bb, tt, D), lambda i: ij(i) + (0,)),
                  pl.BlockSpec((1, D), lambda i: (0, 0)),
                  pl.BlockSpec(w_kv.shape, lambda i: (0, 0)),
                  pl.BlockSpec((1, KV_LORA), lambda i: (0, 0)),
                  tspec, tspec],
        out_specs=[pl.BlockSpec((bb, tt, KV_LORA), lambda i: ij(i) + (0,)),
                   pl.BlockSpec((bb, tt, MLA_ROPE), lambda i: ij(i) + (0,))],
        out_shape=[jax.ShapeDtypeStruct((B, T, KV_LORA), F32),
                   jax.ShapeDtypeStruct((B, T, MLA_ROPE), F32)],
        compiler_params=_cparams(1),
        name="shared_kv",
    )(x, kv_in_g.reshape(1, D), w_kv, kv_lat_g.reshape(1, KV_LORA), cos32, sin32)


def _kv_expand_kernel(lat_ref, kr_ref, wk_ref, ek_ref, wvt_ref, ones_ref, k_ref, vt_ref):
    lat = lat_ref[0].astype(BF16)
    kr = kr_ref[0].astype(BF16)
    k = _dot(lat, wk_ref[...].astype(BF16)) + _dot(kr, ek_ref[...].astype(BF16))
    k_ref[0] = k.astype(k_ref.dtype)
    vt = _dot_nt(wvt_ref[...].astype(BF16), lat) + ones_ref[...]
    vt_ref[0] = vt.astype(vt_ref.dtype)


def kv_expand(lat, kr, wk_pad, ek, wvt_ext, ones_col, rows=512):
    B, T, _ = lat.shape
    tt = rows
    NK, NVT = wk_pad.shape[1], wvt_ext.shape[0]

    def full(a):
        return pl.BlockSpec(a.shape, lambda b, t: (0, 0))

    def rowspec(n):
        return pl.BlockSpec((1, tt, n), lambda b, t: (b, t, 0))

    return pl.pallas_call(
        _kv_expand_kernel,
        grid=(B, T // tt),
        in_specs=[rowspec(KV_LORA), rowspec(MLA_ROPE), full(wk_pad), full(ek), full(wvt_ext), full(ones_col)],
        out_specs=[rowspec(NK), pl.BlockSpec((1, NVT, tt), lambda b, t: (b, 0, t))],
        out_shape=[jax.ShapeDtypeStruct((B, T, NK), BF16), jax.ShapeDtypeStruct((B, NVT, T), BF16)],
        compiler_params=_cparams(2),
        name="kv_expand",
    )(lat, kr, wk_pad, ek, wvt_ext, ones_col)


def _query_kernel(h_ref, wdq_ref, qg_ref, wq_ref, wqr_ref, c_ref, s_ref, q_ref, wdq_b, wq_b, wqr_b):
    @pl.when(pl.program_id(0) == 0)
    def _():
        wdq_b[...] = wdq_ref[...].astype(BF16)
        wq_b[...] = wq_ref[...].astype(BF16)
        wqr_b[...] = wqr_ref[...].astype(BF16)

    bb, tt, D = h_ref.shape
    h = h_ref[...].reshape(bb * tt, D)
    cq = _rms(_dot(h, wdq_b[...]), qg_ref[...]).astype(BF16)
    q1 = _dot(cq, wq_b[...]).reshape(bb, tt, -1)
    q2 = _dot(cq, wqr_b[...]).reshape(bb, tt, -1)
    c = c_ref[...]
    s = s_ref[...]
    for hd in range(MLA_HEADS):
        sl = slice(hd * HEAD_PAD, (hd + 1) * HEAD_PAD)
        q_ref[:, :, sl] = (q1[:, :, sl] * c + q2[:, :, sl] * s).astype(q_ref.dtype)


def mla_queries(h, w_dq, q_norm_g, wq_pad, wq_rot, l, c128, s128, rows=512):
    B, T, D = h.shape
    bb, tt, nblk, ij = _row_blocks(B, T, rows)
    NQ = wq_pad.shape[-1]
    tspec = pl.BlockSpec((tt, HEAD_PAD), lambda i: (ij(i)[1], 0))
    return pl.pallas_call(
        _query_kernel,
        grid=(nblk,),
        in_specs=[pl.BlockSpec((bb, tt, D), lambda i: ij(i) + (0,)),
                  pl.BlockSpec((None, D, Q_LORA), lambda i: (l, 0, 0)),
                  pl.BlockSpec((None, 1, Q_LORA), lambda i: (l, 0, 0)),
                  pl.BlockSpec((None, Q_LORA, NQ), lambda i: (l, 0, 0)),
                  pl.BlockSpec((None, Q_LORA, NQ), lambda i: (l, 0, 0)),
                  tspec, tspec],
        out_specs=pl.BlockSpec((bb, tt, NQ), lambda i: ij(i) + (0,)),
        out_shape=jax.ShapeDtypeStruct((B, T, NQ), BF16),
        scratch_shapes=[pltpu.VMEM((D, Q_LORA), BF16), pltpu.VMEM((Q_LORA, NQ), BF16),
                        pltpu.VMEM((Q_LORA, NQ), BF16)],
        compiler_params=_cparams(1),
        name="mla_queries",
    )(h, w_dq, q_norm_g.reshape(-1, 1, Q_LORA), wq_pad, wq_rot, c128, s128)


def _query_t_kernel(h_ref, wdq_ref, qg_ref, wqt_ref, wqrt_ref, cos_ref, sin_ref, qt_ref, wdq_b, wqt_b, wqrt_b):
    @pl.when((pl.program_id(0) == 0) & (pl.program_id(1) == 0))
    def _():
        wdq_b[...] = wdq_ref[...].astype(BF16)
        wqt_b[...] = wqt_ref[...].astype(BF16)
        wqrt_b[...] = wqrt_ref[...].astype(BF16)

    cq = _rms(_dot(h_ref[0], wdq_b[...]), qg_ref[...]).astype(BF16)
    q1 = _dot_nt(wqt_b[...], cq)
    q2 = _dot_nt(wqrt_b[...], cq)
    cos = cos_ref[...]
    sin = sin_ref[...]
    pad = jnp.zeros((HEAD_PAD - MLA_NOPE - MLA_ROPE, q1.shape[1]), qt_ref.dtype)
    for hd in range(MLA_HEADS):
        r0 = hd * HEAD_PAD
        rope = (q1[r0 + MLA_NOPE:r0 + MLA_NOPE + MLA_ROPE] * cos
                + q2[hd * MLA_ROPE:(hd + 1) * MLA_ROPE] * sin)
        qt_ref[0, r0:r0 + MLA_NOPE, :] = (q1[r0:r0 + MLA_NOPE] * Q_PRESCALE).astype(qt_ref.dtype)
        qt_ref[0, r0 + MLA_NOPE:r0 + MLA_NOPE + MLA_ROPE, :] = rope.astype(qt_ref.dtype)
        qt_ref[0, r0 + MLA_NOPE + MLA_ROPE:r0 + HEAD_PAD, :] = pad


def mla_queries_t(h, w_dq, q_norm_g, wq_t, wqr_t, l, cos_t, sin_t, rows=512):
    B, T, D = h.shape
    tt = rows
    NQ = wq_t.shape[1]
    NR = wqr_t.shape[1]
    tspec = pl.BlockSpec((MLA_ROPE, tt), lambda b, t: (0, t))
    return pl.pallas_call(
        _query_t_kernel,
        grid=(B, T // tt),
        in_specs=[pl.BlockSpec((1, tt, D), lambda b, t: (b, t, 0)),
                  pl.BlockSpec((None, D, Q_LORA), lambda b, t: (l, 0, 0)),
                  pl.BlockSpec((None, 1, Q_LORA), lambda b, t: (l, 0, 0)),
                  pl.BlockSpec((None, NQ, Q_LORA), lambda b, t: (l, 0, 0)),
                  pl.BlockSpec((None, NR, Q_LORA), lambda b, t: (l, 0, 0)),
                  tspec, tspec],
        out_specs=pl.BlockSpec((1, NQ, tt), lambda b, t: (b, 0, t)),
        out_shape=jax.ShapeDtypeStruct((B, NQ, T), BF16),
        scratch_shapes=[pltpu.VMEM((D, Q_LORA), BF16), pltpu.VMEM((NQ, Q_LORA), BF16),
                        pltpu.VMEM((NR, Q_LORA), BF16)],
        compiler_params=_cparams(2),
        name="mla_queries_t",
    )(h, w_dq, q_norm_g.reshape(-1, 1, Q_LORA), wq_t, wqr_t, cos_t, sin_t)


def _attn_prompt_kernel(qi_tab, ki_tab, qt_ref, k_ref, vt_ref, o_ref, *scratch, tq, tk):
    H = MLA_HEADS
    m_refs, l_refs, acc_refs = scratch[:H], scratch[H:2 * H], scratch[2 * H:]
    p_id = pl.program_id(1)
    qi = qi_tab[p_id]
    ki = ki_tab[p_id]

    @pl.when(ki == 0)
    def _():
        for hd in range(H):
            m_refs[hd][...] = jnp.full(m_refs[hd].shape, NEG_INF, F32)
            l_refs[hd][...] = jnp.zeros(l_refs[hd].shape, F32)
            acc_refs[hd][...] = jnp.zeros(acc_refs[hd].shape, F32)

    def block(masked):
        if masked:
            kchunk = (ki * tk + lax.broadcasted_iota(jnp.int32, (tk, tq), 0)) // CHUNK
            qchunk = (qi * tq + lax.broadcasted_iota(jnp.int32, (tk, tq), 1)) // CHUNK
            mask = kchunk <= qchunk
        def scores(hd):
            sl = slice(hd * HEAD_PAD, (hd + 1) * HEAD_PAD)
            return _dot(k_ref[0, :, sl], qt_ref[0, sl, :])

        pending = [scores(hd) for hd in range(ATTN_LOOKAHEAD)]
        for hd in range(H):
            if hd + ATTN_LOOKAHEAD < H:
                pending.append(scores(hd + ATTN_LOOKAHEAD))
            s = pending.pop(0)
            if masked:
                s = jnp.where(mask, s, NEG_INF)
            m_prev = m_refs[hd][...]
            m_new = jnp.maximum(m_prev, jnp.max(s, axis=0, keepdims=True))
            a = jnp.exp2(m_prev - m_new)
            p = jnp.exp2(s - m_new).astype(BF16)
            pv = _dot(vt_ref[0, hd * V_ROWS:(hd + 1) * V_ROWS, :], p)
            acc_refs[hd][...] = a * acc_refs[hd][...] + pv[:MLA_V]
            l_refs[hd][...] = a * l_refs[hd][...] + pv[MLA_V:MLA_V + 1]
            m_refs[hd][...] = m_new

    @pl.when(ki < qi)
    def _():
        block(False)

    @pl.when(ki == qi)
    def _():
        block(True)
        o_t = jnp.concatenate([acc_refs[hd][...] / l_refs[hd][...] for hd in range(H)], axis=0)
        o_ref[0] = o_t.T.astype(o_ref.dtype)


def attn_prompt(qt, k, vt, tq=256):
    B, NQ, T = qt.shape
    NVT = vt.shape[1]
    NV = MLA_HEADS * MLA_V
    tk = tq
    assert tq % CHUNK == 0
    nq = T // tq
    pairs = [(a, b) for a in range(nq) for b in range(a + 1)]
    qi_tab = jnp.asarray([a for a, _ in pairs], jnp.int32)
    ki_tab = jnp.asarray([b for _, b in pairs], jnp.int32)
    grid_spec = pltpu.PrefetchScalarGridSpec(
        num_scalar_prefetch=2,
        grid=(B, len(pairs)),
        in_specs=[pl.BlockSpec((1, NQ, tq), lambda b, p, qt, kt: (b, 0, qt[p])),
                  pl.BlockSpec((1, tk, NQ), lambda b, p, qt, kt: (b, kt[p], 0)),
                  pl.BlockSpec((1, NVT, tk), lambda b, p, qt, kt: (b, 0, kt[p]))],
        out_specs=pl.BlockSpec((1, tq, NV), lambda b, p, qt, kt: (b, qt[p], 0)),
        scratch_shapes=([pltpu.VMEM((1, tq), F32)] * (2 * MLA_HEADS)
                        + [pltpu.VMEM((MLA_V, tq), F32)] * MLA_HEADS),
    )
    return pl.pallas_call(
        functools.partial(_attn_prompt_kernel, tq=tq, tk=tk),
        grid_spec=grid_spec,
        out_shape=jax.ShapeDtypeStruct((B, T, NV), BF16),
        compiler_params=_cparams(2),
        name="attn_prompt",
    )(qi_tab, ki_tab, qt, k, vt)


def _absorb_kernel(q_ref, m_ref, o_ref):
    o_ref[...] = _dot(q_ref[...], m_ref[...].astype(BF16)).astype(o_ref.dtype)


def absorb_queries(q2d, m_abs):
    N = q2d.shape[0]
    H, _, W = m_abs.shape
    return pl.pallas_call(
        _absorb_kernel,
        grid=(H,),
        in_specs=[pl.BlockSpec((N, HEAD_PAD), lambda h: (0, h)),
                  pl.BlockSpec((None, HEAD_PAD, W), lambda h: (h, 0, 0))],
        out_specs=pl.BlockSpec((None, N, W), lambda h: (h, 0, 0)),
        out_shape=jax.ShapeDtypeStruct((H, N, W), BF16),
        compiler_params=_cparams(1),
        name="absorb_queries",
    )(q2d, m_abs)


def _attn_sample_kernel(q_ref, lat_ref, kr_ref, nlat_ref, nkr_ref, o_ref, m_ref, l_ref, acc_ref):
    kb = pl.program_id(1)
    H, Q, W = q_ref.shape
    q = q_ref[...].reshape(H * Q, W)
    q_lat = q[:, :KV_LORA]
    q_rope = q[:, KV_LORA:KV_LORA + MLA_ROPE]

    def update(lat_tile, kr_tile, n_sub, kr_transposed):
        sub = lat_tile.shape[0] // n_sub
        lats = [lat_tile[j * sub:(j + 1) * sub, :].astype(BF16) for j in range(n_sub)]
        if kr_transposed:
            krs = [kr_tile[:, j * sub:(j + 1) * sub].astype(BF16) for j in range(n_sub)]
            ss = [_dot_nt(q_lat, lat) + _dot(q_rope, kr) for lat, kr in zip(lats, krs)]
        else:
            krs = [kr_tile[j * sub:(j + 1) * sub, :].astype(BF16) for j in range(n_sub)]
            ss = [_dot_nt(q_lat, lat) + _dot_nt(q_rope, kr) for lat, kr in zip(lats, krs)]
        m_prev = m_ref[...]
        m_new = m_prev
        for s in ss:
            m_new = jnp.maximum(m_new, jnp.max(s, axis=-1, keepdims=True))
        a = jnp.exp2(m_prev - m_new)
        ps = [jnp.exp2(s - m_new[:, :1]) for s in ss]
        pv = _dot(ps[0].astype(BF16), lats[0])
        psum = jnp.sum(ps[0], axis=-1, keepdims=True)
        for p, lat in zip(ps[1:], lats[1:]):
            pv = pv + _dot(p.astype(BF16), lat)
            psum = psum + jnp.sum(p, axis=-1, keepdims=True)
        l_ref[...] = a * l_ref[...] + psum
        m_ref[...] = m_new
        acc_ref[...] = jnp.concatenate([a, a], axis=-1) * acc_ref[...] + pv

    @pl.when(kb == 0)
    def _():
        m_ref[...] = jnp.full_like(m_ref, NEG_INF)
        l_ref[...] = jnp.zeros_like(l_ref)
        acc_ref[...] = jnp.zeros_like(acc_ref)
        update(nlat_ref[0], nkr_ref[0], 1, False)

    update(lat_ref[0], kr_ref[0], SAMPLE_KEY_SUB, True)

    @pl.when(kb == pl.num_programs(1) - 1)
    def _():
        lsum = l_ref[...]
        o = acc_ref[...] / jnp.concatenate([lsum, lsum], axis=-1)
        o_ref[...] = o.reshape(H, Q, KV_LORA).astype(o_ref.dtype)


def attn_sample(q_abs, cache_lat, cache_kr_t, new_lat, new_kr, tk=2048):
    H, N, W = q_abs.shape
    B, P, _ = cache_lat.shape
    Q = new_lat.shape[1]
    qpos = P + np.arange(Q)
    kpos = np.arange(P + Q)
    assert bool(np.all((kpos // CHUNK)[None, :] <= (qpos // CHUNK)[:, None]))
    return pl.pallas_call(
        _attn_sample_kernel,
        grid=(B, P // tk),
        in_specs=[pl.BlockSpec((H, Q, W), lambda b, kb: (0, b, 0)),
                  pl.BlockSpec((1, tk, KV_LORA), lambda b, kb: (b, kb, 0)),
                  pl.BlockSpec((1, MLA_ROPE, tk), lambda b, kb: (b, 0, kb)),
                  pl.BlockSpec((1, Q, KV_LORA), lambda b, kb: (b, 0, 0)),
                  pl.BlockSpec((1, Q, MLA_ROPE), lambda b, kb: (b, 0, 0))],
        out_specs=pl.BlockSpec((H, Q, KV_LORA), lambda b, kb: (0, b, 0)),
        out_shape=jax.ShapeDtypeStruct((H, N, KV_LORA), BF16),
        scratch_shapes=[pltpu.VMEM((H * Q, 128), F32), pltpu.VMEM((H * Q, 128), F32),
                        pltpu.VMEM((H * Q, KV_LORA), F32)],
        compiler_params=_cparams(2),
        name="attn_sample",
    )(q_abs, cache_lat, cache_kr_t, new_lat, new_kr)


def _unabsorb_kernel(o_ref, w_ref, out_ref):
    out_ref[...] = (_dot(o_ref[0], w_ref[0].astype(BF16))
                    + _dot(o_ref[1], w_ref[1].astype(BF16))).astype(out_ref.dtype)


def unabsorb(o_lat, wuv_pad):
    H, N, R = o_lat.shape
    return pl.pallas_call(
        _unabsorb_kernel,
        grid=(H // 2,),
        in_specs=[pl.BlockSpec((2, N, R), lambda p: (p, 0, 0)),
                  pl.BlockSpec((2, R, 128), lambda p: (p, 0, 0))],
        out_specs=pl.BlockSpec((N, 128), lambda p: (0, p)),
        out_shape=jax.ShapeDtypeStruct((N, (H // 2) * 128), BF16),
        compiler_params=_cparams(1),
        name="unabsorb",
    )(o_lat, wuv_pad)


def _rope_tables(pos):
    half = MLA_ROPE // 2
    inv = 1.0 / (ROPE_THETA ** (np.arange(half, dtype=np.float64) * 2.0 / MLA_ROPE))
    ang = np.asarray(pos, np.float64)[:, None] * inv[None, :]
    cos = np.concatenate([np.cos(ang), np.cos(ang)], axis=-1)
    sin = np.concatenate([np.sin(ang), np.sin(ang)], axis=-1)
    T = cos.shape[0]
    c128 = np.zeros((T, HEAD_PAD)); s128 = np.zeros((T, HEAD_PAD))
    c128[:, :MLA_NOPE] = 1.0
    c128[:, MLA_NOPE:MLA_NOPE + MLA_ROPE] = cos
    s128[:, MLA_NOPE:MLA_NOPE + MLA_ROPE] = sin
    return dict(cos32=jnp.asarray(cos, F32), sin32=jnp.asarray(sin, F32),
                c128=jnp.asarray(c128 * Q_PRESCALE, F32), s128=jnp.asarray(s128 * Q_PRESCALE, F32),
                cos_t=jnp.asarray(cos.T * Q_PRESCALE, F32), sin_t=jnp.asarray(sin.T * Q_PRESCALE, F32))


def _rot_half_cols(w):
    half = w.shape[-1] // 2
    return jnp.concatenate([-w[..., half:], w[..., :half]], axis=-1)


def _prep_weights(w_dkv, w_uk, w_uv, w_uq, router_w):
    D = D_MODEL
    w_lat, w_rope = w_dkv[:, :KV_LORA], w_dkv[:, KV_LORA:]
    pad96 = jnp.zeros((D, 128 - MLA_ROPE), F32)
    w_kv = jnp.concatenate([w_lat, w_rope, pad96, _rot_half_cols(w_rope), pad96], axis=-1)

    zpad = HEAD_PAD - MLA_NOPE
    wk_pad = jnp.pad(w_uk, ((0, 0), (0, 0), (0, zpad))).reshape(KV_LORA, MLA_HEADS * HEAD_PAD)
    ek = jnp.zeros((MLA_ROPE, MLA_HEADS, HEAD_PAD), F32)
    ek = ek.at[:, :, MLA_NOPE:MLA_NOPE + MLA_ROPE].set(
        jnp.broadcast_to(jnp.eye(MLA_ROPE, dtype=F32)[:, None, :], (MLA_ROPE, MLA_HEADS, MLA_ROPE)))
    ek = ek.reshape(MLA_ROPE, MLA_HEADS * HEAD_PAD)
    wvt = jnp.transpose(w_uv, (1, 2, 0))
    wvt_ext = jnp.pad(wvt, ((0, 0), (0, V_ROWS - MLA_V), (0, 0))).reshape(MLA_HEADS * V_ROWS, KV_LORA)
    ones_col = jnp.tile((jnp.arange(V_ROWS) >= MLA_V).astype(F32), MLA_HEADS).reshape(-1, 1)

    nb = w_uq.shape[0]
    qn, qr = w_uq[..., :MLA_NOPE], w_uq[..., MLA_NOPE:]
    z32 = jnp.zeros(qr.shape[:-1] + (HEAD_PAD - MLA_NOPE - MLA_ROPE,), F32)
    wq_pad = jnp.concatenate([qn, qr, z32], axis=-1).reshape(nb, Q_LORA, MLA_HEADS * HEAD_PAD)
    wq_rot = jnp.concatenate([jnp.zeros_like(qn), _rot_half_cols(qr), z32], axis=-1)
    wq_rot = wq_rot.reshape(nb, Q_LORA, MLA_HEADS * HEAD_PAD)
    wq_t = jnp.transpose(wq_pad, (0, 2, 1))
    wqr_t = jnp.transpose(_rot_half_cols(qr).reshape(nb, Q_LORA, MLA_HEADS * MLA_ROPE), (0, 2, 1))

    m_abs = jnp.zeros((MLA_HEADS, HEAD_PAD, KV_LORA + 128), F32)
    m_abs = m_abs.at[:, :MLA_NOPE, :KV_LORA].set(jnp.transpose(w_uk, (1, 2, 0)))
    m_abs = m_abs.at[:, MLA_NOPE:MLA_NOPE + MLA_ROPE, KV_LORA:KV_LORA + MLA_ROPE].set(
        jnp.broadcast_to(jnp.eye(MLA_ROPE, dtype=F32), (MLA_HEADS, MLA_ROPE, MLA_ROPE)))

    wuv_h = jnp.transpose(w_uv, (1, 0, 2))
    even = jnp.pad(wuv_h, ((0, 0), (0, 0), (0, 64)))
    odd = jnp.pad(wuv_h, ((0, 0), (0, 0), (64, 0)))
    wuv_pad = jnp.where((jnp.arange(MLA_HEADS) % 2 == 0)[:, None, None], even, odd)

    rw_t = jnp.transpose(router_w, (0, 2, 1))
    return dict(w_kv=w_kv, wk_pad=wk_pad, ek=ek, wvt_ext=wvt_ext, ones_col=ones_col, wq_pad=wq_pad, wq_rot=wq_rot, wq_t=wq_t, wqr_t=wqr_t,
                m_abs=m_abs, wuv_pad=wuv_pad, rw_t=rw_t)


def _mixer(st, l, P, W):
    x, m = st["x"], st["mod"][l]
    B, T, _ = x.shape
    n_a = P["hg_w_in"].shape[0]
    h = st.pop("h_next", None)
    if h is None:
        h = norm_mod(x, P["norm1_g"][l], m, sc_idx=1, sh_idx=0)
    norm2 = (P["norm2_g"][l], 4, 3)
    if l < n_a:
        z = linear(h, P["hg_w_in"], l, F32)
        s0 = None if st["hg_state"] is None else st["hg_state"][l]
        o, s_new = gla(z, st["lbs"][l], P["hg_onorm_g"][l], s0)
        st["hg_new"].append(s_new)
        st["x"], st["hp"] = linear(o, P["hg_w_out"], l, F32, x=x, mod=m, gate_idx=2, next_norm=norm2)
    else:
        bi = l - n_a
        if st["past_lat"] is None:
            qt = mla_queries_t(h, P["w_dq"], P["q_norm_g"], W["wq_t"], W["wqr_t"], bi, st["cos_t"], st["sin_t"])
            o = attn_prompt(qt, st["k_all"], st["v_all"])
        else:
            q = mla_queries(h, P["w_dq"], P["q_norm_g"], W["wq_pad"], W["wq_rot"], bi, st["c128"], st["s128"])
            q_abs = absorb_queries(q.reshape(B * T, -1), W["m_abs"])
            o_lat = attn_sample(q_abs, st["past_lat"], st["past_kr"], st["lat"], st["kr"])
            o = unabsorb(o_lat, W["wuv_pad"]).reshape(B, T, -1)
        st["x"], st["hp"] = linear(o, P["w_o"], bi, F32, x=x, mod=m, gate_idx=2, next_norm=norm2)


def _moe(groups, l, P, W):
    hp = jnp.concatenate([st["hp"] for st in groups], axis=0)
    n_tok = hp.shape[0]
    n_tiles = (TOP_K * n_tok) // MOE_TILE + N_EXPERTS
    pos, w8, tile_start, tile_count = route(hp, W["rw_t"], P["router_bias"], l, MOE_TILE)
    pos_flat = pos.reshape(-1)
    src = sc_invert(pos_flat, n_tok, n_tiles * MOE_TILE)
    xs = sc_gather(hp, src)
    out = moe_gemm(xs, tile_start[:, 0], tile_count[:, 0], P["exp_w_in"], P["exp_w_out"], l,
                   MOE_TILE, n_tiles)
    y8 = sc_gather(out, pos_flat).reshape(TOP_K, n_tok, -1)
    w_t = w8.T
    row0 = 0
    for st in groups:
        B, T, _ = st["x"].shape
        if l == P["norm1_g"].shape[0] - 1:
            st["x"] = moe_combine(y8, w_t, hp, P["sh_w_in"], P["sh_w_out"], st["x"], st["mod"][l], 5, l, row0,
                                  final_g=P["final_g"])
        else:
            nxt = (P["norm1_g"][l + 1], st["mod"][l + 1], 1, 0)
            st["x"], st["h_next"] = moe_combine(y8, w_t, hp, P["sh_w_in"], P["sh_w_out"], st["x"],
                                                st["mod"][l], 5, l, row0, next_norm=nxt)
        row0 += B * T


def _group_state(x, mod, pos, hg_state, past_lat, past_kr, lbs):
    return dict(x=x, mod=mod, hg_state=hg_state, past_lat=past_lat, past_kr=past_kr, lbs=lbs,
                **_rope_tables(pos), hg_new=[],
                lat=None, kr=None, k_all=None, v_all=None)


def kernel(x_prompt, x_sample, state_hgrn, cache_mla_latent, cache_mla_krope, c_prompt, c_sample, ada_w, ada_b, norm1_g, norm2_g, hg_w_in, hg_lb_logits, hg_onorm_g, hg_w_out, kv_in_g, w_dkv, kv_lat_g, w_uk, w_uv, w_dq, q_norm_g, w_uq, w_o, router_w, router_bias, exp_w_in, exp_w_out, sh_w_in, sh_w_out, final_g):
    Bp, Sp, _ = x_prompt.shape
    Bs, Ss, _ = x_sample.shape
    past = cache_mla_latent.shape[1]
    P = dict(norm1_g=norm1_g, norm2_g=norm2_g, hg_w_in=hg_w_in, hg_lb_logits=hg_lb_logits,
             hg_onorm_g=hg_onorm_g, hg_w_out=hg_w_out, kv_in_g=kv_in_g, kv_lat_g=kv_lat_g,
             w_dq=w_dq, q_norm_g=q_norm_g, w_o=w_o, router_bias=router_bias,
             exp_w_in=exp_w_in, exp_w_out=exp_w_out, sh_w_in=sh_w_in, sh_w_out=sh_w_out, final_g=final_g)
    W = _prep_weights(w_dkv, w_uk, w_uv, w_uq, router_w)
    mod = ada_mod(jnp.concatenate([c_prompt, c_sample], axis=0), ada_w, ada_b)
    lbs = jnp.cumsum(jax.nn.softmax(hg_lb_logits.astype(F32), axis=0), axis=0)
    bsz = Bp // PROMPT_STREAMS
    prompts = [_group_state(x_prompt[i * bsz:(i + 1) * bsz], mod[:, i * bsz:(i + 1) * bsz, None, :],
                            np.arange(Sp), None, None, None, lbs) for i in range(PROMPT_STREAMS)]
    gs = _group_state(x_sample, mod[:, Bp:, None, :], past + np.arange(Ss), state_hgrn,
                      cache_mla_latent, jnp.transpose(cache_mla_krope, (0, 2, 1)), lbs)
    streams = [[prompts[0], gs]] + [[g] for g in prompts[1:]]
    n_a = hg_w_in.shape[0]
    for l in range(norm1_g.shape[0]):
        for groups in streams:
            for st in groups:
                _mixer(st, l, P, W)
            _moe(groups, l, P, W)
            if l == n_a - 1:
                for st in groups:
                    st["lat"], st["kr"] = shared_kv(st["x"], kv_in_g, W["w_kv"], kv_lat_g, st["cos32"], st["sin32"])
                    if st["past_lat"] is None:
                        st["k_all"], st["v_all"] = kv_expand(st["lat"], st["kr"], W["wk_pad"], W["ek"],
                                                             W["wvt_ext"], W["ones_col"])
    for st in prompts + [gs]:
        st["y"] = st["x"]
        st["hg_out"] = jnp.stack(st["hg_new"], axis=0)

    def cat(key, axis=0):
        return jnp.concatenate([g[key] for g in prompts], axis=axis)

    return (cat("y"), gs["y"], cat("hg_out", 1), gs["hg_out"], cat("lat"), cat("kr"), gs["lat"], gs["kr"])
```

```python
import dataclasses
import functools

import numpy as np
import jax
import jax.numpy as jnp
from jax import lax
from jax.experimental import pallas as pl
from jax.experimental.pallas import tpu as pltpu
from jax.experimental.pallas import tpu_sc as plsc

F32 = jnp.float32
BF16 = jnp.bfloat16

D_MODEL = 1024
CHUNK = 64
HG_HEADS = 8
HG_DK = 128
HG_DV = 128
MLA_HEADS = 16
MLA_NOPE = 64
MLA_ROPE = 32
MLA_V = 64
Q_LORA = 384
KV_LORA = 256
ROPE_THETA = 10000.0
N_EXPERTS = 64
TOP_K = 8
N_GROUPS = 8
TOPK_GROUPS = 4
EXPERT_FF = 256
SHARED_FF = 256
ROUTED_SCALE = 2.5
EPS = 1e-6

HEAD_PAD = 128
SAMPLE_KEY_SUB = 4
ATTN_LOOKAHEAD = 6
V_ROWS = MLA_V + 16
QK_SCALE = (MLA_NOPE + MLA_ROPE) ** -0.5
Q_PRESCALE = QK_SCALE * float(np.log2(np.e))
VMEM_LIMIT = 56 * 1024 * 1024
NEG_INF = float("-inf")
SC_CORES = 2
SC_SUBCORES = 16
SC_WORKERS = SC_CORES * SC_SUBCORES
SC_LANES = 16
SC_WINDOW = 64
MOE_TILE = 512
PROMPT_STREAMS = 1
MOE_NBUF = 4
MOE_SUB = 2


def _cparams(n_axes):
    return pltpu.CompilerParams(dimension_semantics=("arbitrary",) * n_axes,
                                vmem_limit_bytes=VMEM_LIMIT)


def _silu(x):
    return x * jax.nn.sigmoid(x)


def _rms(x, g):
    ms = jnp.mean(x * x, axis=-1, keepdims=True)
    return x * lax.rsqrt(ms + EPS) * g


def _dot(a, b):
    return jnp.dot(a, b, preferred_element_type=F32)


def _dot_nt(a, b):
    return lax.dot_general(a, b, (((1,), (1,)), ((), ())), preferred_element_type=F32)


def _dot_tn(a, b):
    return lax.dot_general(a, b, (((0,), (0,)), ((), ())), preferred_element_type=F32)


def _row_blocks(B, T, rows):
    if T >= rows:
        assert T % rows == 0
        bb, tt = 1, rows
    else:
        assert rows % T == 0 and B % (rows // T) == 0
        bb, tt = rows // T, T
    nt = T // tt
    return bb, tt, (B // bb) * nt, (lambda i: (i // nt, i % nt))


def _ada_kernel(c_ref, w_ref, b_ref, o_ref):
    a = _silu(c_ref[...]).astype(BF16)
    o_ref[...] = _dot(a, w_ref[...].astype(BF16)) + b_ref[...]


def ada_mod(c, ada_w, ada_b):
    R, D = c.shape
    L, _, N = ada_w.shape
    tn = 1536
    return pl.pallas_call(
        _ada_kernel,
        grid=(L, N // tn),
        in_specs=[pl.BlockSpec((R, D), lambda l, j: (0, 0)),
                  pl.BlockSpec((None, D, tn), lambda l, j: (l, 0, j)),
                  pl.BlockSpec((None, 1, tn), lambda l, j: (l, 0, j))],
        out_specs=pl.BlockSpec((None, R, tn), lambda l, j: (l, 0, j)),
        out_shape=jax.ShapeDtypeStruct((L, R, N), F32),
        compiler_params=_cparams(2),
        name="ada_mod",
    )(c, ada_w, ada_b.reshape(L, 1, N))


def _pack_pairs(y):
    half = y.shape[-1] // 2
    bits = lax.bitcast_convert_type(y.astype(BF16).astype(F32), jnp.uint32)
    word = lax.shift_right_logical(bits[:, :half], jnp.uint32(16)) | bits[:, half:]
    return lax.bitcast_convert_type(word, jnp.int32)


def _unpack_pairs(word, dtype=BF16):
    u = lax.bitcast_convert_type(word, jnp.uint32)
    lo = lax.bitcast_convert_type(lax.shift_left(u, jnp.uint32(16)), F32)
    hi = lax.bitcast_convert_type(u & jnp.uint32(0xFFFF0000), F32)
    return lo.astype(dtype), hi.astype(dtype)


def _norm_kernel(*refs, modulated, packed):
    if modulated:
        x_ref, g_ref, sc_ref, sh_ref, o_ref = refs
    else:
        x_ref, g_ref, o_ref = refs
    y = _rms(x_ref[...], g_ref[...])
    if modulated:
        y = y * (1.0 + sc_ref[...]) + sh_ref[...]
    if packed:
        bb, tt, D = y.shape
        o_ref[...] = _pack_pairs(y.reshape(bb * tt, D))
    else:
        o_ref[...] = y.astype(o_ref.dtype)


def norm_mod(x, g, mod=None, sc_idx=0, sh_idx=0, out_dtype=BF16, rows=512, packed=False):
    B, T, D = x.shape
    bb, tt, nblk, ij = _row_blocks(B, T, rows)
    xspec = pl.BlockSpec((bb, tt, D), lambda i: ij(i) + (0,))
    in_specs = [xspec, pl.BlockSpec((1, D), lambda i: (0, 0))]
    args = [x, g.reshape(1, D)]
    if mod is not None:
        in_specs += [pl.BlockSpec((bb, 1, D), lambda i: (ij(i)[0], 0, sc_idx)),
                     pl.BlockSpec((bb, 1, D), lambda i: (ij(i)[0], 0, sh_idx))]
        args += [mod, mod]
    if packed:
        out_specs = pl.BlockSpec((bb * tt, D // 2), lambda i: (i, 0))
        out_shape = jax.ShapeDtypeStruct((B * T, D // 2), jnp.int32)
    else:
        out_specs = xspec
        out_shape = jax.ShapeDtypeStruct((B, T, D), out_dtype)
    return pl.pallas_call(
        functools.partial(_norm_kernel, modulated=mod is not None, packed=packed),
        grid=(nblk,),
        in_specs=in_specs,
        out_specs=out_specs,
        out_shape=out_shape,
        compiler_params=_cparams(1),
        name="norm_mod",
    )(*args)


def _linear_kernel(*refs, residual, norm_next, shared_rows, n_main):
    if norm_next and shared_rows:
        a_ref, w_ref, x_ref, gate_ref, ng_ref, nsc_ref, nsh_ref, _, o_ref, hp_ref, wb_ref = refs
    elif norm_next:
        a_ref, w_ref, x_ref, gate_ref, ng_ref, nsc_ref, nsh_ref, o_ref, hp_ref, wb_ref = refs
    elif residual:
        a_ref, w_ref, x_ref, gate_ref, o_ref, wb_ref = refs
    else:
        a_ref, w_ref, o_ref, wb_ref = refs

    @pl.when(pl.program_id(1) == 0)
    def _():
        wb_ref[...] = w_ref[...].astype(BF16)

    def main():
        bb, tt, K = a_ref.shape
        y = _dot(a_ref[...].reshape(bb * tt, K).astype(BF16), wb_ref[...])
        y = y.reshape(bb, tt, y.shape[-1])
        if residual:
            y = x_ref[...] + gate_ref[...] * y
        o_ref[...] = y.astype(o_ref.dtype)
        if norm_next:
            h = _rms(y, ng_ref[...]) * (1.0 + nsc_ref[...]) + nsh_ref[...]
            hp_ref[...] = _pack_pairs(h.reshape(bb * tt, h.shape[-1]))

    if n_main is None:
        main()
    else:
        pl.when(pl.program_id(1) < n_main)(main)

        @pl.when(pl.program_id(1) >= n_main)
        def _():
            hp_ref[...] = jnp.zeros(hp_ref.shape, hp_ref.dtype)


def linear(a, w, l, out_dtype, x=None, mod=None, gate_idx=0, rows=512, tn=1024, next_norm=None,
           rows_total=None, row0=0, rows_buf=None):
    B, T, K = a.shape
    _, _, N = w.shape
    tn = min(tn, N)
    bb, tt, nblk, ij0 = _row_blocks(B, T, rows)
    n_extra = 0
    if next_norm is not None and rows_buf is None and rows_total is not None:
        assert row0 == 0 and (rows_total - B * T) % (bb * tt) == 0
        n_extra = (rows_total - B * T) // (bb * tt)

    def ij(i):
        return ij0(jnp.minimum(i, nblk - 1)) if n_extra else ij0(i)

    in_specs = [pl.BlockSpec((bb, tt, K), lambda j, i: ij(i) + (0,)),
                pl.BlockSpec((None, K, tn), lambda j, i: (l, 0, j))]
    args = [a, w]
    ospec = pl.BlockSpec((bb, tt, tn), lambda j, i: ij(i) + (j,))
    out_specs = ospec
    out_shape = jax.ShapeDtypeStruct((B, T, N), out_dtype)
    aliases = {}
    if x is not None:
        gsteps = D_MODEL // tn
        in_specs += [ospec, pl.BlockSpec((bb, 1, tn), lambda j, i: (ij(i)[0], 0, gate_idx * gsteps + j))]
        args += [x, mod]
    if next_norm is not None:
        assert x is not None and tn == N
        gain, sc_idx, sh_idx = next_norm
        in_specs += [pl.BlockSpec((1, N), lambda j, i: (0, 0)),
                     pl.BlockSpec((bb, 1, N), lambda j, i: (ij(i)[0], 0, sc_idx)),
                     pl.BlockSpec((bb, 1, N), lambda j, i: (ij(i)[0], 0, sh_idx))]
        args += [gain.reshape(1, N), mod, mod]
        assert row0 % (bb * tt) == 0
        off = row0 // (bb * tt)
        out_specs = [ospec, pl.BlockSpec((bb * tt, N // 2), lambda j, i: (off + i, 0))]
        out_shape = [out_shape, jax.ShapeDtypeStruct((rows_total or B * T, N // 2), jnp.int32)]
        if rows_buf is not None:
            in_specs.append(pl.BlockSpec(memory_space=pl.ANY))
            args.append(rows_buf)
            aliases = {len(args) - 1: 1}
    return pl.pallas_call(
        functools.partial(_linear_kernel, residual=x is not None, norm_next=next_norm is not None,
                          shared_rows=rows_buf is not None, n_main=nblk if n_extra else None),
        grid=(N // tn, nblk + n_extra),
        in_specs=in_specs,
        out_specs=out_specs,
        out_shape=out_shape,
        scratch_shapes=[pltpu.VMEM((K, tn), BF16)],
        input_output_aliases=aliases,
        compiler_params=_cparams(2),
        name="linear",
    )(*args)


def _gla_kernel(*refs, L, n_chunks, has_init):
    if has_init:
        q_ref, f_ref, i_ref, g_ref, lb_ref, on_ref, s0_ref, o_ref, so_ref, st_ref = refs
    else:
        q_ref, f_ref, i_ref, g_ref, lb_ref, on_ref, o_ref, so_ref, st_ref = refs
    t = pl.program_id(1)
    H = st_ref.shape[0]

    @pl.when(t == 0)
    def _():
        for h in range(H):
            if has_init:
                st_ref[h] = s0_ref[0, h].T
            else:
                st_ref[h] = jnp.zeros(st_ref.shape[1:], F32)

    lb = lb_ref[...]
    onorm = on_ref[...]
    row = lax.broadcasted_iota(jnp.int32, (L, L), 0)
    col = lax.broadcasted_iota(jnp.int32, (L, L), 1)
    causal = col <= row
    tri = causal.astype(BF16)
    mid = L // 2 - 1

    def chunk(c, carry):
        rows = pl.ds(pl.multiple_of(c * L, L), L)

        def write_o(sl, o):
            o_ref[0, rows, sl] = o.astype(o_ref.dtype)

        _gla_chunk(q_ref[0, rows, :], f_ref[0, rows, :], i_ref[0, rows, :], g_ref[0, rows, :],
                   lb, onorm, tri, causal, st_ref, write_o)
        return carry

    lax.fori_loop(0, n_chunks, chunk, 0, unroll=4 if n_chunks % 4 == 0 else 1)

    @pl.when(t == pl.num_programs(1) - 1)
    def _():
        for h in range(H):
            so_ref[0, h] = st_ref[h].T


def _gla_chunk(q, f, v, g, lb, onorm, tri, causal, st_ref, write_o):
    L = q.shape[0]
    H = st_ref.shape[0]
    mid = L // 2 - 1
    q = _silu(q)
    fg = lb + (1.0 - lb) * jax.nn.sigmoid(f)
    k = 1.0 - fg
    v = v.astype(BF16)
    gate = _silu(g)
    logf = jnp.log(fg)
    hi = logf.astype(BF16)
    lo = (logf - hi.astype(F32)).astype(BF16)
    b = _dot(tri, hi) + _dot(tri, lo)
    b_mid = b[mid:mid + 1, :]
    b_last = b[L - 1:L, :]
    qa = q * jnp.exp(b - b_mid)
    kb = k * jnp.exp(b_mid - b)
    qe = (qa * jnp.exp(b_mid)).astype(BF16)
    kd = (kb * jnp.exp(b_last - b_mid)).astype(BF16)
    qa = qa.astype(BF16)
    kb = kb.astype(BF16)
    decay = jnp.exp(b_last)
    sls = [slice(h * HG_DK, (h + 1) * HG_DK) for h in range(H)]
    sts = [st_ref[h] for h in range(H)]
    scores = [_dot_nt(qa[:, sl], kb[:, sl]) for sl in sls]
    inter = [_dot_nt(qe[:, sl], st.astype(BF16)) for sl, st in zip(sls, sts)]
    outer = [_dot_tn(v[:, sl], kd[:, sl]) for sl in sls]
    intra = [_dot(jnp.where(causal, sc, 0.0).astype(BF16), v[:, sl]) for sc, sl in zip(scores, sls)]
    for h, sl in enumerate(sls):
        st_ref[h] = sts[h] * decay[:, sl] + outer[h]
        write_o(sl, _rms(inter[h] + intra[h], onorm[:, sl]) * gate[:, sl])


def _hgrn_fused_kernel(h_ref, w_hbm, lb_ref, on_ref, o_ref, so_ref, w_b, stage, zbuf, st_ref, sem,
                       *, layer, n_seq_blocks, L):
    s = pl.program_id(0)
    H = st_ref.shape[0]
    n_slices, tt, ncol = zbuf.shape[1], zbuf.shape[2], zbuf.shape[3]
    n_chunks = tt // L
    assert n_chunks == n_slices
    slot_in = lax.rem(s, 2)
    slot_out = 1 - slot_in
    blk_out = jnp.maximum(s - 1, 0)

    @pl.when(s == 0)
    def _():
        for j in range(n_slices):
            cp = pltpu.make_async_copy(w_hbm.at[layer, :, pl.ds(j * ncol, ncol)], stage, sem)
            cp.start()
            cp.wait()
            w_b[j] = stage[...].astype(BF16)
        zbuf[1] = jnp.zeros(zbuf.shape[1:], F32)

    @pl.when(lax.rem(blk_out, n_seq_blocks) == 0)
    def _():
        for h in range(H):
            st_ref[h] = jnp.zeros(st_ref.shape[1:], F32)

    lb = lb_ref[...]
    onorm = on_ref[...]
    causal = (lax.broadcasted_iota(jnp.int32, (L, L), 1) <= lax.broadcasted_iota(jnp.int32, (L, L), 0))
    tri = causal.astype(BF16)
    parts = D_MODEL // ncol

    def step(c, carry):
        r0 = pl.multiple_of(c * L, L)

        def part(p):
            return jnp.concatenate([zbuf[slot_out, p * parts + j, pl.ds(r0, L), :] for j in range(parts)], axis=-1)

        def write_o(sl, o):
            o_ref[0, pl.ds(r0, L), sl] = o.astype(o_ref.dtype)

        _gla_chunk(part(0), part(1), part(2), part(3), lb, onorm, tri, causal, st_ref, write_o)
        zbuf[slot_in, c] = _dot(h_ref[0], w_b[c])
        return carry

    lax.fori_loop(0, n_chunks, step, 0, unroll=2)

    @pl.when((s >= 1) & (lax.rem(blk_out, n_seq_blocks) == n_seq_blocks - 1))
    def _():
        for h in range(H):
            so_ref[0, h] = st_ref[h].T


def hgrn_fused(h, w_in, l, lb, onorm_g, tt=512):
    B, T, D = h.shape
    L = CHUNK
    assert T % tt == 0 and tt // L == (4 * D) // 512
    nT = T // tt
    nblk = B * nT
    H = HG_HEADS

    def blk_in(s):
        i = jnp.minimum(s, nblk - 1)
        return (i // nT, i % nT, 0)

    def blk_out(s):
        i = jnp.maximum(s - 1, 0)
        return (i // nT, i % nT, 0)

    return pl.pallas_call(
        functools.partial(_hgrn_fused_kernel, layer=l, n_seq_blocks=nT, L=L),
        grid=(nblk + 1,),
        in_specs=[pl.BlockSpec((1, tt, D), blk_in),
                  pl.BlockSpec(memory_space=pl.ANY),
                  pl.BlockSpec((1, D), lambda s: (0, 0)),
                  pl.BlockSpec((1, D), lambda s: (0, 0))],
        out_specs=[pl.BlockSpec((1, tt, D), blk_out),
                   pl.BlockSpec((1, H, HG_DK, HG_DV), lambda s: (jnp.maximum(s - 1, 0) // nT, 0, 0, 0))],
        out_shape=[jax.ShapeDtypeStruct((B, T, D), BF16),
                   jax.ShapeDtypeStruct((B, H, HG_DK, HG_DV), F32)],
        scratch_shapes=[pltpu.VMEM((4 * D // 512, D, 512), BF16), pltpu.VMEM((D, 512), F32),
                        pltpu.VMEM((2, 4 * D // 512, tt, 512), F32), pltpu.VMEM((H, HG_DV, HG_DK), F32),
                        pltpu.SemaphoreType.DMA],
        compiler_params=_cparams(1),
        name="hgrn_fused",
    )(h, w_in, lb.reshape(1, D), onorm_g.reshape(1, D))


def gla(z, lb, onorm_g, s0):
    B, T, _ = z.shape
    L = CHUNK if T % CHUNK == 0 else T
    tt = min(T, 512)
    n_chunks = tt // L
    H, D = HG_HEADS, D_MODEL

    def zspec(part):
        return pl.BlockSpec((1, tt, D), lambda b, t: (b, t, part))

    hspec = pl.BlockSpec((1, D), lambda b, t: (0, 0))
    sspec = pl.BlockSpec((1, H, HG_DK, HG_DV), lambda b, t: (b, 0, 0, 0))
    in_specs = [zspec(0), zspec(1), zspec(2), zspec(3), hspec, hspec]
    args = [z, z, z, z, lb.reshape(1, D), onorm_g.reshape(1, D)]
    if s0 is not None:
        in_specs.append(sspec)
        args.append(s0)
    return pl.pallas_call(
        functools.partial(_gla_kernel, L=L, n_chunks=n_chunks, has_init=s0 is not None),
        grid=(B, T // tt),
        in_specs=in_specs,
        out_specs=[pl.BlockSpec((1, tt, D), lambda b, t: (b, t, 0)), sspec],
        out_shape=[jax.ShapeDtypeStruct((B, T, D), BF16),
                   jax.ShapeDtypeStruct((B, H, HG_DK, HG_DV), F32)],
        scratch_shapes=[pltpu.VMEM((H, HG_DV, HG_DK), F32)],
        compiler_params=_cparams(2),
        name="gla",
    )(*args)


def _route_kernel(h_ref, rw_ref, bias_ref, pos_ref, w_ref, te_ref, nu_ref,
                  e_s, r_s, base_s, start_s, *, tile_rows):
    ph = pl.program_id(0)
    i = pl.program_id(1)
    M = h_ref.shape[0]
    half = h_ref.shape[1]
    G, E = N_GROUPS, N_EXPERTS // N_GROUPS
    e_flat = lax.broadcasted_iota(jnp.int32, (N_EXPERTS, M), 0)

    @pl.when(ph == 1)
    def _():
        @pl.when(i == 0)
        def _():
            cnt = base_s[...]
            padded = jnp.floor((cnt + (tile_rows - 1)) * (1.0 / tile_rows)) * tile_rows
            r = lax.broadcasted_iota(jnp.int32, (N_EXPERTS, N_EXPERTS), 0)
            c = lax.broadcasted_iota(jnp.int32, (N_EXPERTS, N_EXPERTS), 1)
            start = jnp.dot((c < r).astype(F32), padded, preferred_element_type=F32,
                            precision=lax.Precision.HIGHEST)
            start_s[...] = start
            te_ref[...] = (start * (1.0 / tile_rows)).astype(jnp.int32)
            nu_ref[...] = (padded * (1.0 / tile_rows)).astype(jnp.int32)

        start_col = start_s[:, :1]
        for k in range(TOP_K):
            hit = e_flat == e_s[i, k:k + 1, :]
            seg = jnp.sum(jnp.where(hit, start_col, 0.0), axis=0, keepdims=True)
            pos_ref[k:k + 1, :] = (seg + r_s[i, k:k + 1, :]).astype(jnp.int32)

    @pl.when(ph == 0)
    def _():
        _route_pass0(h_ref, rw_ref, bias_ref, w_ref, e_s, r_s, base_s, i, M, half, G, E)


def _route_pass0(h_ref, rw_ref, bias_ref, w_ref, e_s, r_s, base_s, i, M, half, G, E):
    @pl.when(i == 0)
    def _():
        base_s[...] = jnp.zeros_like(base_s)

    lo, hi = _unpack_pairs(h_ref[...])
    rw = rw_ref[...].astype(BF16)
    logits = _dot_nt(rw[:, :half], lo) + _dot_nt(rw[:, half:], hi)
    s = jax.nn.sigmoid(logits)
    sb = (s + bias_ref[...]).reshape(G, E, M)
    s = s.reshape(G, E, M)
    e_in = lax.broadcasted_iota(jnp.int32, (G, E, M), 1).astype(F32)
    g_id = lax.broadcasted_iota(jnp.int32, (G, 1, M), 0)
    e_id = lax.broadcasted_iota(jnp.int32, (G, E, M), 0).astype(F32) * E + e_in

    def all_max(a):
        return jnp.max(jnp.max(a, axis=0, keepdims=True), axis=1, keepdims=True)

    def all_min(a):
        return jnp.min(jnp.min(a, axis=0, keepdims=True), axis=1, keepdims=True)

    def all_sum(a):
        return jnp.sum(jnp.sum(a, axis=0, keepdims=True), axis=1, keepdims=True)

    m1 = jnp.max(sb, axis=1, keepdims=True)
    first = jnp.min(jnp.where(sb == m1, e_in, float(E)), axis=1, keepdims=True)
    m2 = jnp.max(jnp.where(e_in == first, NEG_INF, sb), axis=1, keepdims=True)
    gs = m1 + m2

    rank = jnp.zeros((G, 1, M), jnp.int32)
    for j in range(G):
        gj = gs[j:j + 1]
        beats = (gj > gs) | ((gj == gs) & (j < g_id))
        rank = rank + beats.astype(jnp.int32)
    gsel = rank < TOPK_GROUPS

    vals = jnp.where(gsel, sb, NEG_INF)
    selm = jnp.zeros((G, E, M), F32)
    chosen, score = [], []
    for _ in range(TOP_K):
        m = all_max(vals)
        first = all_min(jnp.where(vals == m, e_id, float(N_EXPERTS)))
        hit = e_id == first
        score.append(all_sum(jnp.where(hit, s, 0.0)))
        selm = jnp.where(hit, 1.0, selm)
        vals = jnp.where(hit, NEG_INF, vals)
        chosen.append(first)

    tot = score[0]
    for sc in score[1:]:
        tot = tot + sc
    norm = ROUTED_SCALE / tot

    selm = selm.reshape(N_EXPERTS, M)
    earlier = (lax.broadcasted_iota(jnp.int32, (M, M), 0)
               < lax.broadcasted_iota(jnp.int32, (M, M), 1)).astype(BF16)
    rank = (base_s[:, :1] + _dot(selm.astype(BF16), earlier)).reshape(G, E, M)
    base_s[...] = base_s[...] + jnp.sum(selm, axis=1, keepdims=True)
    for k in range(TOP_K):
        hit = e_id == chosen[k]
        e_s[i, k:k + 1, :] = chosen[k].reshape(1, M).astype(jnp.int32)
        r_s[i, k:k + 1, :] = all_sum(jnp.where(hit, rank, 0.0)).reshape(1, M)
        w_ref[k:k + 1, :] = (score[k] * norm).reshape(1, M)


def route(hp, router_w_t, router_bias, l, tile_rows, rows=512):
    N, half = hp.shape
    M = rows
    nT = N // M
    assert N % M == 0

    def p0(ph, i):
        return i * (1 - ph) + (nT - 1) * ph

    return pl.pallas_call(
        functools.partial(_route_kernel, tile_rows=tile_rows),
        grid=(2, nT),
        in_specs=[pl.BlockSpec((M, half), lambda ph, i: (p0(ph, i), 0)),
                  pl.BlockSpec((None, N_EXPERTS, 2 * half), lambda ph, i: (l, 0, 0)),
                  pl.BlockSpec((None, N_EXPERTS, 1), lambda ph, i: (l, 0, 0))],
        out_specs=[pl.BlockSpec((TOP_K, M), lambda ph, i: (0, i * ph)),
                   pl.BlockSpec((TOP_K, M), lambda ph, i: (0, p0(ph, i))),
                   pl.BlockSpec((N_EXPERTS, 128), lambda ph, i: (0, 0)),
                   pl.BlockSpec((N_EXPERTS, 128), lambda ph, i: (0, 0))],
        out_shape=[jax.ShapeDtypeStruct((TOP_K, N), jnp.int32),
                   jax.ShapeDtypeStruct((TOP_K, N), F32),
                   jax.ShapeDtypeStruct((N_EXPERTS, 128), jnp.int32),
                   jax.ShapeDtypeStruct((N_EXPERTS, 128), jnp.int32)],
        scratch_shapes=[pltpu.VMEM((nT, TOP_K, M), jnp.int32), pltpu.VMEM((nT, TOP_K, M), F32),
                        pltpu.VMEM((N_EXPERTS, 128), F32), pltpu.VMEM((N_EXPERTS, 128), F32)],
        compiler_params=_cparams(2),
        name="route",
    )(hp, router_w_t, router_bias.reshape(-1, N_EXPERTS, 1))


def _sc_mesh():
    return plsc.VectorSubcoreMesh(core_axis_name="core", subcore_axis_name="subcore")


def sc_invert(pos_flat, n_tok, n_out):
    n = pos_flat.shape[0]
    per = n_out // SC_WORKERS
    chunk = n_tok
    assert n_out % SC_WORKERS == 0 and per % SC_LANES == 0
    assert n_tok % chunk == 0 and n % chunk == 0 and chunk % SC_LANES == 0
    cp = pltpu.CompilerParams()
    if "needs_layout_passes" in pltpu.CompilerParams.__dataclass_fields__:
        cp = dataclasses.replace(cp, needs_layout_passes=False)

    @functools.partial(
        pl.kernel, out_type=jax.ShapeDtypeStruct((n_out,), jnp.int32), mesh=_sc_mesh(),
        scratch_types=[pltpu.VMEM((chunk,), jnp.int32), pltpu.VMEM((per,), jnp.int32)],
        compiler_params=cp, name="sc_invert")
    def k(pos_hbm, src_hbm, pos_v, src_v):
        wid = lax.axis_index("subcore") * SC_CORES + lax.axis_index("core")
        lo = wid * per
        lane = lax.iota(jnp.int32, SC_LANES)

        @pl.loop(0, per, step=SC_LANES)
        def _(r):
            src_v[pl.ds(r, SC_LANES)] = lax.rem(lo + r + lane, n_tok)

        @pl.loop(0, n // chunk)
        def _(c):
            base = c * chunk
            pltpu.sync_copy(pos_hbm.at[pl.ds(base, chunk)], pos_v)
            tok0 = lax.rem(base, n_tok)

            @plsc.parallel_loop(0, chunk, step=SC_LANES, unroll=8)
            def _(r):
                p = pos_v[pl.ds(r, SC_LANES)] - lo
                mine = (p >= 0) & (p < per)
                plsc.store_scatter(src_v, [jnp.where(mine, p, 0)], tok0 + r + lane, mask=mine)

        pltpu.sync_copy(src_v, src_hbm.at[pl.ds(lo, per)])

    return k(pos_flat)


def sc_gather(x, idx):
    n = idx.shape[0]
    dim = x.shape[1]
    assert n % (SC_WINDOW * SC_WORKERS) == 0

    @functools.partial(
        pl.kernel, out_type=jax.ShapeDtypeStruct((n, dim), x.dtype), mesh=_sc_mesh(),
        scratch_types=[], name="sc_gather")
    def k(x_hbm, i_hbm, o_hbm):
        def body(i_vmem, o_vmem):
            pltpu.sync_copy(x_hbm.at[i_vmem.at[0]], o_vmem)

        pltpu.emit_pipeline(
            body, grid=(n // SC_WINDOW,),
            in_specs=[pl.BlockSpec((1, SC_WINDOW), index_map=lambda i: (i, 0))],
            out_specs=[pl.BlockSpec((SC_WINDOW, dim), index_map=lambda i: (i, 0))],
            core_axis_name=("core", "subcore"),
            dimension_semantics=(pltpu.PARALLEL,),
        )(i_hbm, o_hbm)

    return k(x, idx.reshape(n // SC_WINDOW, SC_WINDOW))


def _moe_gemm_kernel(ts_ref, tn_ref, x_hbm, wi_ref, wo_ref, o_hbm, wi_b, wo_b, xbuf, obuf, in_sem, out_sem,
                     *, tile_rows, n_tiles):
    e = pl.program_id(0)
    last = pl.num_programs(0) - 1
    t0 = ts_ref[e]
    n = tn_ref[e]
    n_used = ts_ref[last] + tn_ref[last]

    def x_copy(g, slot):
        rows = pl.ds(pl.multiple_of(g * tile_rows, tile_rows), tile_rows)
        return pltpu.make_async_copy(x_hbm.at[rows], xbuf.at[slot], in_sem.at[slot])

    def o_copy(g, slot):
        rows = pl.ds(pl.multiple_of(g * tile_rows, tile_rows), tile_rows)
        return pltpu.make_async_copy(obuf.at[slot], o_hbm.at[rows], out_sem.at[slot])

    @pl.when(e == 0)
    def _():
        for g0 in range(MOE_NBUF - 1):
            @pl.when(g0 < n_used)
            def _():
                x_copy(g0, g0).start()

    @pl.when(n > 0)
    def _():
        wi_b[...] = wi_ref[...].astype(BF16)
        wo_b[...] = wo_ref[...].astype(BF16)

    def tile(i, carry):
        g = t0 + i
        slot = lax.rem(g, MOE_NBUF)
        x_copy(g, slot).wait()
        ahead = g + (MOE_NBUF - 1)

        @pl.when(ahead < n_used)
        def _():
            x_copy(ahead, lax.rem(ahead, MOE_NBUF)).start()

        @pl.when(g >= MOE_NBUF)
        def _():
            o_copy(g - MOE_NBUF, slot).wait()

        rows = tile_rows // MOE_SUB
        half = xbuf.shape[2]
        xs = [_unpack_pairs(xbuf[slot, r * rows:(r + 1) * rows, :]) for r in range(MOE_SUB)]
        hus = [_dot(lo, wi_b[:half, :]) + _dot(hi, wi_b[half:, :]) for lo, hi in xs]
        acts = [(_silu(hu[:, :EXPERT_FF]) * hu[:, EXPERT_FF:]).astype(BF16) for hu in hus]
        outs = [_dot(act, wo_b[...]) for act in acts]
        for r, out in enumerate(outs):
            obuf[slot, r * rows:(r + 1) * rows, :] = _pack_pairs(out)
        o_copy(g, slot).start()
        return carry

    lax.fori_loop(0, n, tile, 0)

    @pl.when(e == last)
    def _():
        for back in range(MOE_NBUF, 0, -1):
            @pl.when(n_used >= back)
            def _():
                o_copy(n_used - back, lax.rem(n_used - back, MOE_NBUF)).wait()

        obuf[0] = jnp.zeros(obuf.shape[1:], obuf.dtype)

        def clear(g, carry):
            cp = o_copy(g, 0)
            cp.start()
            cp.wait()
            return carry

        lax.fori_loop(n_used, n_tiles, clear, 0)


def moe_gemm(xs, tile_start, tile_count, exp_w_in, exp_w_out, l, tile_rows, n_tiles):
    P, half = xs.shape
    D = 2 * half
    assert P == n_tiles * tile_rows
    hbm = pl.BlockSpec(memory_space=pl.ANY)
    grid_spec = pltpu.PrefetchScalarGridSpec(
        num_scalar_prefetch=2,
        grid=(N_EXPERTS,),
        in_specs=[hbm,
                  pl.BlockSpec((None, None, D, 2 * EXPERT_FF), lambda e, ts, tn: (l, e, 0, 0)),
                  pl.BlockSpec((None, None, EXPERT_FF, D), lambda e, ts, tn: (l, e, 0, 0))],
        out_specs=hbm,
        scratch_shapes=[pltpu.VMEM((D, 2 * EXPERT_FF), BF16), pltpu.VMEM((EXPERT_FF, D), BF16),
                        pltpu.VMEM((MOE_NBUF, tile_rows, half), jnp.int32),
                        pltpu.VMEM((MOE_NBUF, tile_rows, half), jnp.int32),
                        pltpu.SemaphoreType.DMA((MOE_NBUF,)), pltpu.SemaphoreType.DMA((MOE_NBUF,))],
    )
    return pl.pallas_call(
        functools.partial(_moe_gemm_kernel, tile_rows=tile_rows, n_tiles=n_tiles),
        grid_spec=grid_spec,
        out_shape=jax.ShapeDtypeStruct((P, half), jnp.int32),
        compiler_params=_cparams(1),
        name="moe_gemm",
    )(tile_start, tile_count, xs, exp_w_in, exp_w_out)


def _moe_combine_kernel(*refs, final, norm_next):
    if final:
        y_ref, w_ref, h_ref, si_ref, so_ref, x_ref, g2_ref, fg_ref, o_ref, si_b, so_b = refs
    elif norm_next:
        (y_ref, w_ref, h_ref, si_ref, so_ref, x_ref, g2_ref, ng_ref, nsc_ref, nsh_ref,
         o_ref, hn_ref, si_b, so_b) = refs
    else:
        y_ref, w_ref, h_ref, si_ref, so_ref, x_ref, g2_ref, o_ref, si_b, so_b = refs

    @pl.when(pl.program_id(0) == 0)
    def _():
        si_b[...] = si_ref[...].astype(BF16)
        so_b[...] = so_ref[...].astype(BF16)

    bb, tt, D = x_ref.shape
    half = D // 2
    w = w_ref[...]
    acc_lo = jnp.zeros((bb * tt, half), F32)
    acc_hi = jnp.zeros((bb * tt, half), F32)
    for k in range(TOP_K):
        lo, hi = _unpack_pairs(y_ref[k], F32)
        acc_lo = acc_lo + w[:, k:k + 1] * lo
        acc_hi = acc_hi + w[:, k:k + 1] * hi
    hlo, hhi = _unpack_pairs(h_ref[...])
    hu = _dot(hlo, si_b[:half, :]) + _dot(hhi, si_b[half:, :])
    act = (_silu(hu[:, :SHARED_FF]) * hu[:, SHARED_FF:]).astype(BF16)
    y = jnp.concatenate([acc_lo, acc_hi], axis=-1) + _dot(act, so_b[...])
    x_new = x_ref[...] + g2_ref[...] * y.reshape(bb, tt, D)
    o_ref[...] = _rms(x_new, fg_ref[...]) if final else x_new
    if norm_next:
        hn = _rms(x_new, ng_ref[...]) * (1.0 + nsc_ref[...]) + nsh_ref[...]
        hn_ref[...] = hn.astype(hn_ref.dtype)


def moe_combine(y8, w_t, hp, sh_w_in, sh_w_out, x, mod, gate_idx, l, row0, final_g=None, next_norm=None,
                rows=256):
    B, T, D = x.shape
    half = D // 2
    bb, tt, nblk, ij = _row_blocks(B, T, rows)
    M = bb * tt
    assert row0 % M == 0
    off = row0 // M
    xspec = pl.BlockSpec((bb, tt, D), lambda i: ij(i) + (0,))
    in_specs = [pl.BlockSpec((TOP_K, M, half), lambda i: (0, off + i, 0)),
                pl.BlockSpec((M, TOP_K), lambda i: (off + i, 0)),
                pl.BlockSpec((M, half), lambda i: (off + i, 0)),
                pl.BlockSpec((None, D, 2 * SHARED_FF), lambda i: (l, 0, 0)),
                pl.BlockSpec((None, SHARED_FF, D), lambda i: (l, 0, 0)),
                xspec,
                pl.BlockSpec((bb, 1, D), lambda i: (ij(i)[0], 0, gate_idx))]
    args = [y8, w_t, hp, sh_w_in, sh_w_out, x, mod]
    out_specs = xspec
    out_shape = jax.ShapeDtypeStruct((B, T, D), F32)
    if final_g is not None:
        assert next_norm is None
        in_specs.append(pl.BlockSpec((1, D), lambda i: (0, 0)))
        args.append(final_g.reshape(1, D))
    if next_norm is not None:
        gain, mod_next, sc_idx, sh_idx = next_norm
        in_specs += [pl.BlockSpec((1, D), lambda i: (0, 0)),
                     pl.BlockSpec((bb, 1, D), lambda i: (ij(i)[0], 0, sc_idx)),
                     pl.BlockSpec((bb, 1, D), lambda i: (ij(i)[0], 0, sh_idx))]
        args += [gain.reshape(1, D), mod_next, mod_next]
        out_specs = [xspec, xspec]
        out_shape = [out_shape, jax.ShapeDtypeStruct((B, T, D), BF16)]
    return pl.pallas_call(
        functools.partial(_moe_combine_kernel, final=final_g is not None, norm_next=next_norm is not None),
        grid=(nblk,),
        in_specs=in_specs,
        out_specs=out_specs,
        out_shape=out_shape,
        scratch_shapes=[pltpu.VMEM((D, 2 * SHARED_FF), BF16), pltpu.VMEM((SHARED_FF, D), BF16)],
        compiler_params=_cparams(1),
        name="moe_combine",
    )(*args)


def _shared_kv_kernel(x_ref, g_ref, w_ref, lg_ref, cos_ref, sin_ref, lat_ref, kr_ref):
    bb, tt, D = x_ref.shape
    xn = _rms(x_ref[...], g_ref[...]).reshape(bb * tt, D).astype(BF16)
    z = _dot(xn, w_ref[...].astype(BF16))
    lat = _rms(z[:, :KV_LORA], lg_ref[...])
    lat_ref[...] = lat.reshape(bb, tt, KV_LORA)
    zr = z[:, KV_LORA:KV_LORA + MLA_ROPE].reshape(bb, tt, MLA_ROPE)
    zq = z[:, KV_LORA + 128:KV_LORA + 128 + MLA_ROPE].reshape(bb, tt, MLA_ROPE)
    kr_ref[...] = zr * cos_ref[...] + zq * sin_ref[...]


def shared_kv(x, kv_in_g, w_kv, kv_lat_g, cos32, sin32, rows=512):
    B, T, D = x.shape
    bb, tt, nblk, ij = _row_blocks(B, T, rows)
    tspec = pl.BlockSpec((tt, MLA_ROPE), lambda i: (ij(i)[1], 0))
    return pl.pallas_call(
        _shared_kv_kernel,
        grid=(nblk,),
        in_specs=[pl.BlockSpec((bb, tt, D), lambda i: ij(i) + (0,)),
                  pl.BlockSpec((1, D), lambda i: (0, 0)),
                  pl.BlockSpec(w_kv.shape, lambda i: (0, 0)),
                  pl.BlockSpec((1, KV_LORA), lambda i: (0, 0)),
                  tspec, tspec],
        out_specs=[pl.BlockSpec((bb, tt, KV_LORA), lambda i: ij(i) + (0,)),
                   pl.BlockSpec((bb, tt, MLA_ROPE), lambda i: ij(i) + (0,))],
        out_shape=[jax.ShapeDtypeStruct((B, T, KV_LORA), F32),
                   jax.ShapeDtypeStruct((B, T, MLA_ROPE), F32)],
        compiler_params=_cparams(1),
        name="shared_kv",
    )(x, kv_in_g.reshape(1, D), w_kv, kv_lat_g.reshape(1, KV_LORA), cos32, sin32)


def _kv_expand_kernel(lat_ref, kr_ref, wk_ref, ek_ref, wvt_ref, ones_ref, k_ref, vt_ref):
    lat = lat_ref[0].astype(BF16)
    kr = kr_ref[0].astype(BF16)
    k = _dot(lat, wk_ref[...].astype(BF16)) + _dot(kr, ek_ref[...].astype(BF16))
    k_ref[0] = k.astype(k_ref.dtype)
    vt = _dot_nt(wvt_ref[...].astype(BF16), lat) + ones_ref[...]
    vt_ref[0] = vt.astype(vt_ref.dtype)


def kv_expand(lat, kr, wk_pad, ek, wvt_ext, ones_col, rows=512):
    B, T, _ = lat.shape
    tt = rows
    NK, NVT = wk_pad.shape[1], wvt_ext.shape[0]

    def full(a):
        return pl.BlockSpec(a.shape, lambda b, t: (0, 0))

    def rowspec(n):
        return pl.BlockSpec((1, tt, n), lambda b, t: (b, t, 0))

    return pl.pallas_call(
        _kv_expand_kernel,
        grid=(B, T // tt),
        in_specs=[rowspec(KV_LORA), rowspec(MLA_ROPE), full(wk_pad), full(ek), full(wvt_ext), full(ones_col)],
        out_specs=[rowspec(NK), pl.BlockSpec((1, NVT, tt), lambda b, t: (b, 0, t))],
        out_shape=[jax.ShapeDtypeStruct((B, T, NK), BF16), jax.ShapeDtypeStruct((B, NVT, T), BF16)],
        compiler_params=_cparams(2),
        name="kv_expand",
    )(lat, kr, wk_pad, ek, wvt_ext, ones_col)


def _query_kernel(h_ref, wdq_ref, qg_ref, wq_ref, wqr_ref, c_ref, s_ref, q_ref, wdq_b, wq_b, wqr_b):
    @pl.when(pl.program_id(0) == 0)
    def _():
        wdq_b[...] = wdq_ref[...].astype(BF16)
        wq_b[...] = wq_ref[...].astype(BF16)
        wqr_b[...] = wqr_ref[...].astype(BF16)

    bb, tt, D = h_ref.shape
    h = h_ref[...].reshape(bb * tt, D)
    cq = _rms(_dot(h, wdq_b[...]), qg_ref[...]).astype(BF16)
    q1 = _dot(cq, wq_b[...]).reshape(bb, tt, -1)
    q2 = _dot(cq, wqr_b[...]).reshape(bb, tt, -1)
    c = c_ref[...]
    s = s_ref[...]
    for hd in range(MLA_HEADS):
        sl = slice(hd * HEAD_PAD, (hd + 1) * HEAD_PAD)
        q_ref[:, :, sl] = (q1[:, :, sl] * c + q2[:, :, sl] * s).astype(q_ref.dtype)


def mla_queries(h, w_dq, q_norm_g, wq_pad, wq_rot, l, c128, s128, rows=512):
    B, T, D = h.shape
    bb, tt, nblk, ij = _row_blocks(B, T, rows)
    NQ = wq_pad.shape[-1]
    tspec = pl.BlockSpec((tt, HEAD_PAD), lambda i: (ij(i)[1], 0))
    return pl.pallas_call(
        _query_kernel,
        grid=(nblk,),
        in_specs=[pl.BlockSpec((bb, tt, D), lambda i: ij(i) + (0,)),
                  pl.BlockSpec((None, D, Q_LORA), lambda i: (l, 0, 0)),
                  pl.BlockSpec((None, 1, Q_LORA), lambda i: (l, 0, 0)),
                  pl.BlockSpec((None, Q_LORA, NQ), lambda i: (l, 0, 0)),
                  pl.BlockSpec((None, Q_LORA, NQ), lambda i: (l, 0, 0)),
                  tspec, tspec],
        out_specs=pl.BlockSpec((bb, tt, NQ), lambda i: ij(i) + (0,)),
        out_shape=jax.ShapeDtypeStruct((B, T, NQ), BF16),
        scratch_shapes=[pltpu.VMEM((D, Q_LORA), BF16), pltpu.VMEM((Q_LORA, NQ), BF16),
                        pltpu.VMEM((Q_LORA, NQ), BF16)],
        compiler_params=_cparams(1),
        name="mla_queries",
    )(h, w_dq, q_norm_g.reshape(-1, 1, Q_LORA), wq_pad, wq_rot, c128, s128)


def _query_t_kernel(h_ref, wdq_ref, qg_ref, wqt_ref, wqrt_ref, cos_ref, sin_ref, qt_ref, wdq_b, wqt_b, wqrt_b):
    @pl.when((pl.program_id(0) == 0) & (pl.program_id(1) == 0))
    def _():
        wdq_b[...] = wdq_ref[...].astype(BF16)
        wqt_b[...] = wqt_ref[...].astype(BF16)
        wqrt_b[...] = wqrt_ref[...].astype(BF16)

    cq = _rms(_dot(h_ref[0], wdq_b[...]), qg_ref[...]).astype(BF16)
    q1 = _dot_nt(wqt_b[...], cq)
    q2 = _dot_nt(wqrt_b[...], cq)
    cos = cos_ref[...]
    sin = sin_ref[...]
    pad = jnp.zeros((HEAD_PAD - MLA_NOPE - MLA_ROPE, q1.shape[1]), qt_ref.dtype)
    for hd in range(MLA_HEADS):
        r0 = hd * HEAD_PAD
        rope = (q1[r0 + MLA_NOPE:r0 + MLA_NOPE + MLA_ROPE] * cos
                + q2[hd * MLA_ROPE:(hd + 1) * MLA_ROPE] * sin)
        qt_ref[0, r0:r0 + MLA_NOPE, :] = (q1[r0:r0 + MLA_NOPE] * Q_PRESCALE).astype(qt_ref.dtype)
        qt_ref[0, r0 + MLA_NOPE:r0 + MLA_NOPE + MLA_ROPE, :] = rope.astype(qt_ref.dtype)
        qt_ref[0, r0 + MLA_NOPE + MLA_ROPE:r0 + HEAD_PAD, :] = pad


def mla_queries_t(h, w_dq, q_norm_g, wq_t, wqr_t, l, cos_t, sin_t, rows=512):
    B, T, D = h.shape
    tt = rows
    NQ = wq_t.shape[1]
    NR = wqr_t.shape[1]
    tspec = pl.BlockSpec((MLA_ROPE, tt), lambda b, t: (0, t))
    return pl.pallas_call(
        _query_t_kernel,
        grid=(B, T // tt),
        in_specs=[pl.BlockSpec((1, tt, D), lambda b, t: (b, t, 0)),
                  pl.BlockSpec((None, D, Q_LORA), lambda b, t: (l, 0, 0)),
                  pl.BlockSpec((None, 1, Q_LORA), lambda b, t: (l, 0, 0)),
                  pl.BlockSpec((None, NQ, Q_LORA), lambda b, t: (l, 0, 0)),
                  pl.BlockSpec((None, NR, Q_LORA), lambda b, t: (l, 0, 0)),
                  tspec, tspec],
        out_specs=pl.BlockSpec((1, NQ, tt), lambda b, t: (b, 0, t)),
        out_shape=jax.ShapeDtypeStruct((B, NQ, T), BF16),
        scratch_shapes=[pltpu.VMEM((D, Q_LORA), BF16), pltpu.VMEM((NQ, Q_LORA), BF16),
                        pltpu.VMEM((NR, Q_LORA), BF16)],
        compiler_params=_cparams(2),
        name="mla_queries_t",
    )(h, w_dq, q_norm_g.reshape(-1, 1, Q_LORA), wq_t, wqr_t, cos_t, sin_t)


def _attn_prompt_kernel(qi_tab, ki_tab, qt_ref, k_ref, vt_ref, o_ref, *scratch, tq, tk):
    H = MLA_HEADS
    m_refs, l_refs, acc_refs = scratch[:H], scratch[H:2 * H], scratch[2 * H:]
    p_id = pl.program_id(1)
    qi = qi_tab[p_id]
    ki = ki_tab[p_id]

    @pl.when(ki == 0)
    def _():
        for hd in range(H):
            m_refs[hd][...] = jnp.full(m_refs[hd].shape, NEG_INF, F32)
            l_refs[hd][...] = jnp.zeros(l_refs[hd].shape, F32)
            acc_refs[hd][...] = jnp.zeros(acc_refs[hd].shape, F32)

    def block(masked):
        if masked:
            kchunk = (ki * tk + lax.broadcasted_iota(jnp.int32, (tk, tq), 0)) // CHUNK
            qchunk = (qi * tq + lax.broadcasted_iota(jnp.int32, (tk, tq), 1)) // CHUNK
            mask = kchunk <= qchunk
        def scores(hd):
            sl = slice(hd * HEAD_PAD, (hd + 1) * HEAD_PAD)
            return _dot(k_ref[0, :, sl], qt_ref[0, sl, :])

        pending = [scores(hd) for hd in range(ATTN_LOOKAHEAD)]
        for hd in range(H):
            if hd + ATTN_LOOKAHEAD < H:
                pending.append(scores(hd + ATTN_LOOKAHEAD))
            s = pending.pop(0)
            if masked:
                s = jnp.where(mask, s, NEG_INF)
            m_prev = m_refs[hd][...]
            m_new = jnp.maximum(m_prev, jnp.max(s, axis=0, keepdims=True))
            a = jnp.exp2(m_prev - m_new)
            p = jnp.exp2(s - m_new).astype(BF16)
            pv = _dot(vt_ref[0, hd * V_ROWS:(hd + 1) * V_ROWS, :], p)
            acc_refs[hd][...] = a * acc_refs[hd][...] + pv[:MLA_V]
            l_refs[hd][...] = a * l_refs[hd][...] + pv[MLA_V:MLA_V + 1]
            m_refs[hd][...] = m_new

    @pl.when(ki < qi)
    def _():
        block(False)

    @pl.when(ki == qi)
    def _():
        block(True)
        o_t = jnp.concatenate([acc_refs[hd][...] / l_refs[hd][...] for hd in range(H)], axis=0)
        o_ref[0] = o_t.T.astype(o_ref.dtype)


def attn_prompt(qt, k, vt, tq=256):
    B, NQ, T = qt.shape
    NVT = vt.shape[1]
    NV = MLA_HEADS * MLA_V
    tk = tq
    assert tq % CHUNK == 0
    nq = T // tq
    pairs = [(a, b) for a in range(nq) for b in range(a + 1)]
    qi_tab = jnp.asarray([a for a, _ in pairs], jnp.int32)
    ki_tab = jnp.asarray([b for _, b in pairs], jnp.int32)
    grid_spec = pltpu.PrefetchScalarGridSpec(
        num_scalar_prefetch=2,
        grid=(B, len(pairs)),
        in_specs=[pl.BlockSpec((1, NQ, tq), lambda b, p, qt, kt: (b, 0, qt[p])),
                  pl.BlockSpec((1, tk, NQ), lambda b, p, qt, kt: (b, kt[p], 0)),
                  pl.BlockSpec((1, NVT, tk), lambda b, p, qt, kt: (b, 0, kt[p]))],
        out_specs=pl.BlockSpec((1, tq, NV), lambda b, p, qt, kt: (b, qt[p], 0)),
        scratch_shapes=([pltpu.VMEM((1, tq), F32)] * (2 * MLA_HEADS)
                        + [pltpu.VMEM((MLA_V, tq), F32)] * MLA_HEADS),
    )
    return pl.pallas_call(
        functools.partial(_attn_prompt_kernel, tq=tq, tk=tk),
        grid_spec=grid_spec,
        out_shape=jax.ShapeDtypeStruct((B, T, NV), BF16),
        compiler_params=_cparams(2),
        name="attn_prompt",
    )(qi_tab, ki_tab, qt, k, vt)


def _absorb_kernel(q_ref, m_ref, o_ref):
    o_ref[...] = _dot(q_ref[...], m_ref[...].astype(BF16)).astype(o_ref.dtype)


def absorb_queries(q2d, m_abs):
    N = q2d.shape[0]
    H, _, W = m_abs.shape
    return pl.pallas_call(
        _absorb_kernel,
        grid=(H,),
        in_specs=[pl.BlockSpec((N, HEAD_PAD), lambda h: (0, h)),
                  pl.BlockSpec((None, HEAD_PAD, W), lambda h: (h, 0, 0))],
        out_specs=pl.BlockSpec((None, N, W), lambda h: (h, 0, 0)),
        out_shape=jax.ShapeDtypeStruct((H, N, W), BF16),
        compiler_params=_cparams(1),
        name="absorb_queries",
    )(q2d, m_abs)


def _attn_sample_kernel(q_ref, lat_ref, kr_ref, nlat_ref, nkr_ref, o_ref, m_ref, l_ref, acc_ref):
    kb = pl.program_id(1)
    H, Q, W = q_ref.shape
    q = q_ref[...].reshape(H * Q, W)
    q_lat = q[:, :KV_LORA]
    q_rope = q[:, KV_LORA:KV_LORA + MLA_ROPE]

    def update(lat_tile, kr_tile, n_sub, kr_transposed):
        sub = lat_tile.shape[0] // n_sub
        lats = [lat_tile[j * sub:(j + 1) * sub, :].astype(BF16) for j in range(n_sub)]
        if kr_transposed:
            krs = [kr_tile[:, j * sub:(j + 1) * sub].astype(BF16) for j in range(n_sub)]
            ss = [_dot_nt(q_lat, lat) + _dot(q_rope, kr) for lat, kr in zip(lats, krs)]
        else:
            krs = [kr_tile[j * sub:(j + 1) * sub, :].astype(BF16) for j in range(n_sub)]
            ss = [_dot_nt(q_lat, lat) + _dot_nt(q_rope, kr) for lat, kr in zip(lats, krs)]
        m_prev = m_ref[...]
        m_new = m_prev
        for s in ss:
            m_new = jnp.maximum(m_new, jnp.max(s, axis=-1, keepdims=True))
        a = jnp.exp2(m_prev - m_new)
        ps = [jnp.exp2(s - m_new[:, :1]) for s in ss]
        pv = _dot(ps[0].astype(BF16), lats[0])
        psum = jnp.sum(ps[0], axis=-1, keepdims=True)
        for p, lat in zip(ps[1:], lats[1:]):
            pv = pv + _dot(p.astype(BF16), lat)
            psum = psum + jnp.sum(p, axis=-1, keepdims=True)
        l_ref[...] = a * l_ref[...] + psum
        m_ref[...] = m_new
        acc_ref[...] = jnp.concatenate([a, a], axis=-1) * acc_ref[...] + pv

    @pl.when(kb == 0)
    def _():
        m_ref[...] = jnp.full_like(m_ref, NEG_INF)
        l_ref[...] = jnp.zeros_like(l_ref)
        acc_ref[...] = jnp.zeros_like(acc_ref)
        update(nlat_ref[0], nkr_ref[0], 1, False)

    update(lat_ref[0], kr_ref[0], SAMPLE_KEY_SUB, True)

    @pl.when(kb == pl.num_programs(1) - 1)
    def _():
        lsum = l_ref[...]
        o = acc_ref[...] / jnp.concatenate([lsum, lsum], axis=-1)
        o_ref[...] = o.reshape(H, Q, KV_LORA).astype(o_ref.dtype)


def attn_sample(q_abs, cache_lat, cache_kr_t, new_lat, new_kr, tk=2048):
    H, N, W = q_abs.shape
    B, P, _ = cache_lat.shape
    Q = new_lat.shape[1]
    qpos = P + np.arange(Q)
    kpos = np.arange(P + Q)
    assert bool(np.all((kpos // CHUNK)[None, :] <= (qpos // CHUNK)[:, None]))
    return pl.pallas_call(
        _attn_sample_kernel,
        grid=(B, P // tk),
        in_specs=[pl.BlockSpec((H, Q, W), lambda b, kb: (0, b, 0)),
                  pl.BlockSpec((1, tk, KV_LORA), lambda b, kb: (b, kb, 0)),
                  pl.BlockSpec((1, MLA_ROPE, tk), lambda b, kb: (b, 0, kb)),
                  pl.BlockSpec((1, Q, KV_LORA), lambda b, kb: (b, 0, 0)),
                  pl.BlockSpec((1, Q, MLA_ROPE), lambda b, kb: (b, 0, 0))],
        out_specs=pl.BlockSpec((H, Q, KV_LORA), lambda b, kb: (0, b, 0)),
        out_shape=jax.ShapeDtypeStruct((H, N, KV_LORA), BF16),
        scratch_shapes=[pltpu.VMEM((H * Q, 128), F32), pltpu.VMEM((H * Q, 128), F32),
                        pltpu.VMEM((H * Q, KV_LORA), F32)],
        compiler_params=_cparams(2),
        name="attn_sample",
    )(q_abs, cache_lat, cache_kr_t, new_lat, new_kr)


def _unabsorb_kernel(o_ref, w_ref, out_ref):
    out_ref[...] = (_dot(o_ref[0], w_ref[0].astype(BF16))
                    + _dot(o_ref[1], w_ref[1].astype(BF16))).astype(out_ref.dtype)


def unabsorb(o_lat, wuv_pad):
    H, N, R = o_lat.shape
    return pl.pallas_call(
        _unabsorb_kernel,
        grid=(H // 2,),
        in_specs=[pl.BlockSpec((2, N, R), lambda p: (p, 0, 0)),
                  pl.BlockSpec((2, R, 128), lambda p: (p, 0, 0))],
        out_specs=pl.BlockSpec((N, 128), lambda p: (0, p)),
        out_shape=jax.ShapeDtypeStruct((N, (H // 2) * 128), BF16),
        compiler_params=_cparams(1),
        name="unabsorb",
    )(o_lat, wuv_pad)


def _rope_tables(pos):
    half = MLA_ROPE // 2
    inv = 1.0 / (ROPE_THETA ** (np.arange(half, dtype=np.float64) * 2.0 / MLA_ROPE))
    ang = np.asarray(pos, np.float64)[:, None] * inv[None, :]
    cos = np.concatenate([np.cos(ang), np.cos(ang)], axis=-1)
    sin = np.concatenate([np.sin(ang), np.sin(ang)], axis=-1)
    T = cos.shape[0]
    c128 = np.zeros((T, HEAD_PAD)); s128 = np.zeros((T, HEAD_PAD))
    c128[:, :MLA_NOPE] = 1.0
    c128[:, MLA_NOPE:MLA_NOPE + MLA_ROPE] = cos
    s128[:, MLA_NOPE:MLA_NOPE + MLA_ROPE] = sin
    return dict(cos32=jnp.asarray(cos, F32), sin32=jnp.asarray(sin, F32),
                c128=jnp.asarray(c128 * Q_PRESCALE, F32), s128=jnp.asarray(s128 * Q_PRESCALE, F32),
                cos_t=jnp.asarray(cos.T * Q_PRESCALE, F32), sin_t=jnp.asarray(sin.T * Q_PRESCALE, F32))


def _rot_half_cols(w):
    half = w.shape[-1] // 2
    return jnp.concatenate([-w[..., half:], w[..., :half]], axis=-1)


def _prep_weights(w_dkv, w_uk, w_uv, w_uq, router_w):
    D = D_MODEL
    w_lat, w_rope = w_dkv[:, :KV_LORA], w_dkv[:, KV_LORA:]
    pad96 = jnp.zeros((D, 128 - MLA_ROPE), F32)
    w_kv = jnp.concatenate([w_lat, w_rope, pad96, _rot_half_cols(w_rope), pad96], axis=-1)

    zpad = HEAD_PAD - MLA_NOPE
    wk_pad = jnp.pad(w_uk, ((0, 0), (0, 0), (0, zpad))).reshape(KV_LORA, MLA_HEADS * HEAD_PAD)
    ek = jnp.zeros((MLA_ROPE, MLA_HEADS, HEAD_PAD), F32)
    ek = ek.at[:, :, MLA_NOPE:MLA_NOPE + MLA_ROPE].set(
        jnp.broadcast_to(jnp.eye(MLA_ROPE, dtype=F32)[:, None, :], (MLA_ROPE, MLA_HEADS, MLA_ROPE)))
    ek = ek.reshape(MLA_ROPE, MLA_HEADS * HEAD_PAD)
    wvt = jnp.transpose(w_uv, (1, 2, 0))
    wvt_ext = jnp.pad(wvt, ((0, 0), (0, V_ROWS - MLA_V), (0, 0))).reshape(MLA_HEADS * V_ROWS, KV_LORA)
    ones_col = jnp.tile((jnp.arange(V_ROWS) >= MLA_V).astype(F32), MLA_HEADS).reshape(-1, 1)

    nb = w_uq.shape[0]
    qn, qr = w_uq[..., :MLA_NOPE], w_uq[..., MLA_NOPE:]
    z32 = jnp.zeros(qr.shape[:-1] + (HEAD_PAD - MLA_NOPE - MLA_ROPE,), F32)
    wq_pad = jnp.concatenate([qn, qr, z32], axis=-1).reshape(nb, Q_LORA, MLA_HEADS * HEAD_PAD)
    wq_rot = jnp.concatenate([jnp.zeros_like(qn), _rot_half_cols(qr), z32], axis=-1)
    wq_rot = wq_rot.reshape(nb, Q_LORA, MLA_HEADS * HEAD_PAD)
    wq_t = jnp.transpose(wq_pad, (0, 2, 1))
    wqr_t = jnp.transpose(_rot_half_cols(qr).reshape(nb, Q_LORA, MLA_HEADS * MLA_ROPE), (0, 2, 1))

    m_abs = jnp.zeros((MLA_HEADS, HEAD_PAD, KV_LORA + 128), F32)
    m_abs = m_abs.at[:, :MLA_NOPE, :KV_LORA].set(jnp.transpose(w_uk, (1, 2, 0)))
    m_abs = m_abs.at[:, MLA_NOPE:MLA_NOPE + MLA_ROPE, KV_LORA:KV_LORA + MLA_ROPE].set(
        jnp.broadcast_to(jnp.eye(MLA_ROPE, dtype=F32), (MLA_HEADS, MLA_ROPE, MLA_ROPE)))

    wuv_h = jnp.transpose(w_uv, (1, 0, 2))
    even = jnp.pad(wuv_h, ((0, 0), (0, 0), (0, 64)))
    odd = jnp.pad(wuv_h, ((0, 0), (0, 0), (64, 0)))
    wuv_pad = jnp.where((jnp.arange(MLA_HEADS) % 2 == 0)[:, None, None], even, odd)

    rw_t = jnp.transpose(router_w, (0, 2, 1))
    return dict(w_kv=w_kv, wk_pad=wk_pad, ek=ek, wvt_ext=wvt_ext, ones_col=ones_col, wq_pad=wq_pad, wq_rot=wq_rot, wq_t=wq_t, wqr_t=wqr_t,
                m_abs=m_abs, wuv_pad=wuv_pad, rw_t=rw_t)


def _mixer(st, l, P, W, packed):
    rows_kw = dict(rows_total=packed["total"], row0=packed["row0"], rows_buf=packed["buf"])
    x, m = st["x"], st["mod"][l]
    B, T, _ = x.shape
    n_a = P["hg_w_in"].shape[0]
    h = st.pop("h_next", None)
    if h is None:
        h = norm_mod(x, P["norm1_g"][l], m, sc_idx=1, sh_idx=0)
    norm2 = (P["norm2_g"][l], 4, 3)
    if l < n_a:
        if st["hg_state"] is None:
            o, s_new = hgrn_fused(h, P["hg_w_in"], l, st["lbs"][l], P["hg_onorm_g"][l])
        else:
            z = linear(h, P["hg_w_in"], l, F32)
            o, s_new = gla(z, st["lbs"][l], P["hg_onorm_g"][l], st["hg_state"][l])
        st["hg_new"].append(s_new)
        st["x"], packed["buf"] = linear(o, P["hg_w_out"], l, F32, x=x, mod=m, gate_idx=2, next_norm=norm2,
                                        **rows_kw)
    else:
        bi = l - n_a
        if st["past_lat"] is None:
            qt = mla_queries_t(h, P["w_dq"], P["q_norm_g"], W["wq_t"], W["wqr_t"], bi, st["cos_t"], st["sin_t"])
            o = attn_prompt(qt, st["k_all"], st["v_all"])
        else:
            q = mla_queries(h, P["w_dq"], P["q_norm_g"], W["wq_pad"], W["wq_rot"], bi, st["c128"], st["s128"])
            q_abs = absorb_queries(q.reshape(B * T, -1), W["m_abs"])
            o_lat = attn_sample(q_abs, st["past_lat"], st["past_kr"], st["lat"], st["kr"])
            o = unabsorb(o_lat, W["wuv_pad"]).reshape(B, T, -1)
        st["x"], packed["buf"] = linear(o, P["w_o"], bi, F32, x=x, mod=m, gate_idx=2, next_norm=norm2, **rows_kw)
    packed["row0"] += B * T


def _moe(groups, hp, l, P, W):
    n_tok = hp.shape[0]
    n_tiles = (TOP_K * n_tok) // MOE_TILE + N_EXPERTS
    pos, w8, tile_start, tile_count = route(hp, W["rw_t"], P["router_bias"], l, MOE_TILE)
    pos_flat = pos.reshape(-1)
    src = sc_invert(pos_flat, n_tok, n_tiles * MOE_TILE)
    xs = sc_gather(hp, src)
    out = moe_gemm(xs, tile_start[:, 0], tile_count[:, 0], P["exp_w_in"], P["exp_w_out"], l,
                   MOE_TILE, n_tiles)
    y8 = sc_gather(out, pos_flat).reshape(TOP_K, n_tok, -1)
    w_t = w8.T
    row0 = 0
    for st in groups:
        B, T, _ = st["x"].shape
        if l == P["norm1_g"].shape[0] - 1:
            st["x"] = moe_combine(y8, w_t, hp, P["sh_w_in"], P["sh_w_out"], st["x"], st["mod"][l], 5, l, row0,
                                  final_g=P["final_g"])
        else:
            nxt = (P["norm1_g"][l + 1], st["mod"][l + 1], 1, 0)
            st["x"], st["h_next"] = moe_combine(y8, w_t, hp, P["sh_w_in"], P["sh_w_out"], st["x"],
                                                st["mod"][l], 5, l, row0, next_norm=nxt)
        row0 += B * T


def _group_state(x, mod, pos, hg_state, past_lat, past_kr, lbs):
    return dict(x=x, mod=mod, hg_state=hg_state, past_lat=past_lat, past_kr=past_kr, lbs=lbs,
                **_rope_tables(pos), hg_new=[],
                lat=None, kr=None, k_all=None, v_all=None)


def kernel(x_prompt, x_sample, state_hgrn, cache_mla_latent, cache_mla_krope, c_prompt, c_sample, ada_w, ada_b, norm1_g, norm2_g, hg_w_in, hg_lb_logits, hg_onorm_g, hg_w_out, kv_in_g, w_dkv, kv_lat_g, w_uk, w_uv, w_dq, q_norm_g, w_uq, w_o, router_w, router_bias, exp_w_in, exp_w_out, sh_w_in, sh_w_out, final_g):
    Bp, Sp, _ = x_prompt.shape
    Bs, Ss, _ = x_sample.shape
    past = cache_mla_latent.shape[1]
    P = dict(norm1_g=norm1_g, norm2_g=norm2_g, hg_w_in=hg_w_in, hg_lb_logits=hg_lb_logits,
             hg_onorm_g=hg_onorm_g, hg_w_out=hg_w_out, kv_in_g=kv_in_g, kv_lat_g=kv_lat_g,
             w_dq=w_dq, q_norm_g=q_norm_g, w_o=w_o, router_bias=router_bias,
             exp_w_in=exp_w_in, exp_w_out=exp_w_out, sh_w_in=sh_w_in, sh_w_out=sh_w_out, final_g=final_g)
    W = _prep_weights(w_dkv, w_uk, w_uv, w_uq, router_w)
    mod = ada_mod(jnp.concatenate([c_prompt, c_sample], axis=0), ada_w, ada_b)
    lbs = jnp.cumsum(jax.nn.softmax(hg_lb_logits.astype(F32), axis=0), axis=0)
    bsz = Bp // PROMPT_STREAMS
    prompts = [_group_state(x_prompt[i * bsz:(i + 1) * bsz], mod[:, i * bsz:(i + 1) * bsz, None, :],
                            np.arange(Sp), None, None, None, lbs) for i in range(PROMPT_STREAMS)]
    gs = _group_state(x_sample, mod[:, Bp:, None, :], past + np.arange(Ss), state_hgrn,
                      cache_mla_latent, jnp.transpose(cache_mla_krope, (0, 2, 1)), lbs)
    streams = [[prompts[0], gs]] + [[g] for g in prompts[1:]]
    n_a = hg_w_in.shape[0]
    for l in range(norm1_g.shape[0]):
        for groups in streams:
            packed = dict(total=sum(st["x"].shape[0] * st["x"].shape[1] for st in groups), row0=0, buf=None)
            for st in groups:
                _mixer(st, l, P, W, packed)
            _moe(groups, packed["buf"], l, P, W)
            if l == n_a - 1:
                for st in groups:
                    st["lat"], st["kr"] = shared_kv(st["x"], kv_in_g, W["w_kv"], kv_lat_g, st["cos32"], st["sin32"])
                    if st["past_lat"] is None:
                        st["k_all"], st["v_all"] = kv_expand(st["lat"], st["kr"], W["wk_pad"], W["ek"],
                                                             W["wvt_ext"], W["ones_col"])
    for st in prompts + [gs]:
        st["y"] = st["x"]
        st["hg_out"] = jnp.stack(st["hg_new"], axis=0)

    def cat(key, axis=0):
        return jnp.concatenate([g[key] for g in prompts], axis=axis)

    return (cat("y"), gs["y"], cat("hg_out", 1), gs["hg_out"], cat("lat"), cat("kr"), gs["lat"], gs["kr"])
```

```python
import dataclasses
import functools

import numpy as np
import jax
import jax.numpy as jnp
from jax import lax
from jax.experimental import pallas as pl
from jax.experimental.pallas import tpu as pltpu
from jax.experimental.pallas import tpu_sc as plsc

F32 = jnp.float32
BF16 = jnp.bfloat16

D_MODEL = 1024
CHUNK = 64
HG_HEADS = 8
HG_DK = 128
HG_DV = 128
MLA_HEADS = 16
MLA_NOPE = 64
MLA_ROPE = 32
MLA_V = 64
Q_LORA = 384
KV_LORA = 256
ROPE_THETA = 10000.0
N_EXPERTS = 64
TOP_K = 8
N_GROUPS = 8
TOPK_GROUPS = 4
EXPERT_FF = 256
SHARED_FF = 256
ROUTED_SCALE = 2.5
EPS = 1e-6

HEAD_PAD = 128
SAMPLE_KEY_SUB = 8
ATTN_LOOKAHEAD = 6
V_ROWS = MLA_V + 16
QK_SCALE = (MLA_NOPE + MLA_ROPE) ** -0.5
Q_PRESCALE = QK_SCALE * float(np.log2(np.e))
VMEM_LIMIT = 56 * 1024 * 1024
NEG_INF = float("-inf")
SC_CORES = 2
SC_SUBCORES = 16
SC_WORKERS = SC_CORES * SC_SUBCORES
SC_LANES = 16
SC_WINDOW = 64
MOE_TILE = 512
PROMPT_STREAMS = 1
MOE_NBUF = 4
MOE_SUB = 2


def _cparams(n_axes):
    return pltpu.CompilerParams(dimension_semantics=("arbitrary",) * n_axes,
                                vmem_limit_bytes=VMEM_LIMIT)


def _silu(x):
    return x * jax.nn.sigmoid(x)


def _rms(x, g):
    ms = jnp.mean(x * x, axis=-1, keepdims=True)
    return x * lax.rsqrt(ms + EPS) * g


def _dot(a, b):
    return jnp.dot(a, b, preferred_element_type=F32)


def _dot_nt(a, b):
    return lax.dot_general(a, b, (((1,), (1,)), ((), ())), preferred_element_type=F32)


def _dot_tn(a, b):
    return lax.dot_general(a, b, (((0,), (0,)), ((), ())), preferred_element_type=F32)


def _row_blocks(B, T, rows):
    if T >= rows:
        assert T % rows == 0
        bb, tt = 1, rows
    else:
        assert rows % T == 0 and B % (rows // T) == 0
        bb, tt = rows // T, T
    nt = T // tt
    return bb, tt, (B // bb) * nt, (lambda i: (i // nt, i % nt))


def _ada_kernel(c_ref, w_ref, b_ref, o_ref):
    a = _silu(c_ref[...]).astype(BF16)
    o_ref[...] = _dot(a, w_ref[...].astype(BF16)) + b_ref[...]


def ada_mod(c, ada_w, ada_b):
    R, D = c.shape
    L, _, N = ada_w.shape
    tn = 1536
    return pl.pallas_call(
        _ada_kernel,
        grid=(L, N // tn),
        in_specs=[pl.BlockSpec((R, D), lambda l, j: (0, 0)),
                  pl.BlockSpec((None, D, tn), lambda l, j: (l, 0, j)),
                  pl.BlockSpec((None, 1, tn), lambda l, j: (l, 0, j))],
        out_specs=pl.BlockSpec((None, R, tn), lambda l, j: (l, 0, j)),
        out_shape=jax.ShapeDtypeStruct((L, R, N), F32),
        compiler_params=_cparams(2),
        name="ada_mod",
    )(c, ada_w, ada_b.reshape(L, 1, N))


def _pack_pairs(y):
    half = y.shape[-1] // 2
    bits = lax.bitcast_convert_type(y.astype(BF16).astype(F32), jnp.uint32)
    word = lax.shift_right_logical(bits[:, :half], jnp.uint32(16)) | bits[:, half:]
    return lax.bitcast_convert_type(word, jnp.int32)


def _unpack_pairs(word, dtype=BF16):
    u = lax.bitcast_convert_type(word, jnp.uint32)
    lo = lax.bitcast_convert_type(lax.shift_left(u, jnp.uint32(16)), F32)
    hi = lax.bitcast_convert_type(u & jnp.uint32(0xFFFF0000), F32)
    return lo.astype(dtype), hi.astype(dtype)


def _norm_kernel(*refs, modulated, packed):
    if modulated:
        x_ref, g_ref, sc_ref, sh_ref, o_ref = refs
    else:
        x_ref, g_ref, o_ref = refs
    y = _rms(x_ref[...], g_ref[...])
    if modulated:
        y = y * (1.0 + sc_ref[...]) + sh_ref[...]
    if packed:
        bb, tt, D = y.shape
        o_ref[...] = _pack_pairs(y.reshape(bb * tt, D))
    else:
        o_ref[...] = y.astype(o_ref.dtype)


def norm_mod(x, g, mod=None, sc_idx=0, sh_idx=0, out_dtype=BF16, rows=512, packed=False):
    B, T, D = x.shape
    bb, tt, nblk, ij = _row_blocks(B, T, rows)
    xspec = pl.BlockSpec((bb, tt, D), lambda i: ij(i) + (0,))
    in_specs = [xspec, pl.BlockSpec((1, D), lambda i: (0, 0))]
    args = [x, g.reshape(1, D)]
    if mod is not None:
        in_specs += [pl.BlockSpec((bb, 1, D), lambda i: (ij(i)[0], 0, sc_idx)),
                     pl.BlockSpec((bb, 1, D), lambda i: (ij(i)[0], 0, sh_idx))]
        args += [mod, mod]
    if packed:
        out_specs = pl.BlockSpec((bb * tt, D // 2), lambda i: (i, 0))
        out_shape = jax.ShapeDtypeStruct((B * T, D // 2), jnp.int32)
    else:
        out_specs = xspec
        out_shape = jax.ShapeDtypeStruct((B, T, D), out_dtype)
    return pl.pallas_call(
        functools.partial(_norm_kernel, modulated=mod is not None, packed=packed),
        grid=(nblk,),
        in_specs=in_specs,
        out_specs=out_specs,
        out_shape=out_shape,
        compiler_params=_cparams(1),
        name="norm_mod",
    )(*args)


def _linear_kernel(*refs, residual, norm_next, shared_rows, n_main):
    if norm_next and shared_rows:
        a_ref, w_ref, x_ref, gate_ref, ng_ref, nsc_ref, nsh_ref, _, o_ref, hp_ref, wb_ref = refs
    elif norm_next:
        a_ref, w_ref, x_ref, gate_ref, ng_ref, nsc_ref, nsh_ref, o_ref, hp_ref, wb_ref = refs
    elif residual:
        a_ref, w_ref, x_ref, gate_ref, o_ref, wb_ref = refs
    else:
        a_ref, w_ref, o_ref, wb_ref = refs

    @pl.when(pl.program_id(1) == 0)
    def _():
        wb_ref[...] = w_ref[...].astype(BF16)

    def main():
        bb, tt, K = a_ref.shape
        y = _dot(a_ref[...].reshape(bb * tt, K).astype(BF16), wb_ref[...])
        y = y.reshape(bb, tt, y.shape[-1])
        if residual:
            y = x_ref[...] + gate_ref[...] * y
        o_ref[...] = y.astype(o_ref.dtype)
        if norm_next:
            h = _rms(y, ng_ref[...]) * (1.0 + nsc_ref[...]) + nsh_ref[...]
            hp_ref[...] = _pack_pairs(h.reshape(bb * tt, h.shape[-1]))

    if n_main is None:
        main()
    else:
        pl.when(pl.program_id(1) < n_main)(main)

        @pl.when(pl.program_id(1) >= n_main)
        def _():
            hp_ref[...] = jnp.zeros(hp_ref.shape, hp_ref.dtype)


def linear(a, w, l, out_dtype, x=None, mod=None, gate_idx=0, rows=512, tn=1024, next_norm=None,
           rows_total=None, row0=0, rows_buf=None, col_blocks=None):
    B, T, K = a.shape
    _, _, N = w.shape
    tn = min(tn, N)
    if col_blocks is None:
        def wcol(j):
            return j
    else:
        first, skip_from = col_blocks[0], [c - k for k, c in enumerate(col_blocks)]
        gap_at = next((k for k, d in enumerate(skip_from) if d != first), len(col_blocks))
        assert all(d == first for d in skip_from[:gap_at]) and all(d == first + 1 for d in skip_from[gap_at:])
        N = len(col_blocks) * tn

        def wcol(j):
            return j + first + (j >= gap_at)
    bb, tt, nblk, ij0 = _row_blocks(B, T, rows)
    n_extra = 0
    if next_norm is not None and rows_buf is None and rows_total is not None:
        assert row0 == 0 and (rows_total - B * T) % (bb * tt) == 0
        n_extra = (rows_total - B * T) // (bb * tt)

    def ij(i):
        return ij0(jnp.minimum(i, nblk - 1)) if n_extra else ij0(i)

    in_specs = [pl.BlockSpec((bb, tt, K), lambda j, i: ij(i) + (0,)),
                pl.BlockSpec((None, K, tn), lambda j, i: (l, 0, wcol(j)))]
    args = [a, w]
    ospec = pl.BlockSpec((bb, tt, tn), lambda j, i: ij(i) + (j,))
    out_specs = ospec
    out_shape = jax.ShapeDtypeStruct((B, T, N), out_dtype)
    aliases = {}
    if x is not None:
        gsteps = D_MODEL // tn
        in_specs += [ospec, pl.BlockSpec((bb, 1, tn), lambda j, i: (ij(i)[0], 0, gate_idx * gsteps + j))]
        args += [x, mod]
    if next_norm is not None:
        assert x is not None and tn == N
        gain, sc_idx, sh_idx = next_norm
        in_specs += [pl.BlockSpec((1, N), lambda j, i: (0, 0)),
                     pl.BlockSpec((bb, 1, N), lambda j, i: (ij(i)[0], 0, sc_idx)),
                     pl.BlockSpec((bb, 1, N), lambda j, i: (ij(i)[0], 0, sh_idx))]
        args += [gain.reshape(1, N), mod, mod]
        assert row0 % (bb * tt) == 0
        off = row0 // (bb * tt)
        out_specs = [ospec, pl.BlockSpec((bb * tt, N // 2), lambda j, i: (off + i, 0))]
        out_shape = [out_shape, jax.ShapeDtypeStruct((rows_total or B * T, N // 2), jnp.int32)]
        if rows_buf is not None:
            in_specs.append(pl.BlockSpec(memory_space=pl.ANY))
            args.append(rows_buf)
            aliases = {len(args) - 1: 1}
    return pl.pallas_call(
        functools.partial(_linear_kernel, residual=x is not None, norm_next=next_norm is not None,
                          shared_rows=rows_buf is not None, n_main=nblk if n_extra else None),
        grid=(N // tn, nblk + n_extra),
        in_specs=in_specs,
        out_specs=out_specs,
        out_shape=out_shape,
        scratch_shapes=[pltpu.VMEM((K, tn), BF16)],
        input_output_aliases=aliases,
        compiler_params=_cparams(2),
        name="linear",
    )(*args)


def _gla_kernel(*refs, L, n_chunks, has_init):
    if has_init:
        q_ref, f_ref, i_ref, g_ref, lb_ref, on_ref, s0_ref, o_ref, so_ref, st_ref = refs
    else:
        q_ref, f_ref, i_ref, g_ref, lb_ref, on_ref, o_ref, so_ref, st_ref = refs
    t = pl.program_id(1)
    H = st_ref.shape[0]

    @pl.when(t == 0)
    def _():
        for h in range(H):
            if has_init:
                st_ref[h] = s0_ref[0, h].T
            else:
                st_ref[h] = jnp.zeros(st_ref.shape[1:], F32)

    lb = lb_ref[...]
    onorm = on_ref[...]
    row = lax.broadcasted_iota(jnp.int32, (L, L), 0)
    col = lax.broadcasted_iota(jnp.int32, (L, L), 1)
    causal = col <= row
    tri = causal.astype(BF16)
    mid = L // 2 - 1

    def chunk(c, carry):
        rows = pl.ds(pl.multiple_of(c * L, L), L)

        def write_o(sl, o):
            o_ref[0, rows, sl] = o.astype(o_ref.dtype)

        _gla_chunk(q_ref[0, rows, :], f_ref[0, rows, :], i_ref[0, rows, :], g_ref[0, rows, :],
                   lb, onorm, tri, causal, st_ref, write_o)
        return carry

    lax.fori_loop(0, n_chunks, chunk, 0, unroll=4 if n_chunks % 4 == 0 else 1)

    @pl.when(t == pl.num_programs(1) - 1)
    def _():
        for h in range(H):
            so_ref[0, h] = st_ref[h].T


def _gla_chunk(q, f, v, g, lb, onorm, tri, causal, st_ref, write_o):
    L = q.shape[0]
    H = st_ref.shape[0]
    mid = L // 2 - 1
    q = _silu(q.astype(F32))
    fg = lb + (1.0 - lb) * jax.nn.sigmoid(f)
    k = 1.0 - fg
    v = v.astype(BF16)
    gate = _silu(g.astype(F32))
    logf = jnp.log(fg)
    hi = logf.astype(BF16)
    lo = (logf - hi.astype(F32)).astype(BF16)
    b = _dot(tri, hi) + _dot(tri, lo)
    b_mid = b[mid:mid + 1, :]
    b_last = b[L - 1:L, :]
    qa = q * jnp.exp(b - b_mid)
    kb = k * jnp.exp(b_mid - b)
    qe = (qa * jnp.exp(b_mid)).astype(BF16)
    kd = (kb * jnp.exp(b_last - b_mid)).astype(BF16)
    qa = qa.astype(BF16)
    kb = kb.astype(BF16)
    decay = jnp.exp(b_last)
    sls = [slice(h * HG_DK, (h + 1) * HG_DK) for h in range(H)]
    sts = [st_ref[h] for h in range(H)]
    scores = [_dot_nt(qa[:, sl], kb[:, sl]) for sl in sls]
    inter = [_dot_nt(qe[:, sl], st.astype(BF16)) for sl, st in zip(sls, sts)]
    outer = [_dot_tn(v[:, sl], kd[:, sl]) for sl in sls]
    intra = [_dot(jnp.where(causal, sc, 0.0).astype(BF16), v[:, sl]) for sc, sl in zip(scores, sls)]
    for h, sl in enumerate(sls):
        st_ref[h] = sts[h] * decay[:, sl] + outer[h]
        write_o(sl, _rms(inter[h] + intra[h], onorm[:, sl]) * gate[:, sl])


def gla(zqig, zf, lb, onorm_g, s0):
    B, T, D = zf.shape
    L = CHUNK if T % CHUNK == 0 else T
    tt = min(T, 512)
    n_chunks = tt // L
    H = HG_HEADS

    def zspec(part):
        return pl.BlockSpec((1, tt, D), lambda b, t: (b, t, part))

    hspec = pl.BlockSpec((1, D), lambda b, t: (0, 0))
    sspec = pl.BlockSpec((1, H, HG_DK, HG_DV), lambda b, t: (b, 0, 0, 0))
    in_specs = [zspec(0), zspec(0), zspec(1), zspec(2), hspec, hspec]
    args = [zqig, zf, zqig, zqig, lb.reshape(1, D), onorm_g.reshape(1, D)]
    if s0 is not None:
        in_specs.append(sspec)
        args.append(s0)
    return pl.pallas_call(
        functools.partial(_gla_kernel, L=L, n_chunks=n_chunks, has_init=s0 is not None),
        grid=(B, T // tt),
        in_specs=in_specs,
        out_specs=[pl.BlockSpec((1, tt, D), lambda b, t: (b, t, 0)), sspec],
        out_shape=[jax.ShapeDtypeStruct((B, T, D), BF16),
                   jax.ShapeDtypeStruct((B, H, HG_DK, HG_DV), F32)],
        scratch_shapes=[pltpu.VMEM((H, HG_DV, HG_DK), F32)],
        compiler_params=_cparams(2),
        name="gla",
    )(*args)


def _route_kernel(h_ref, rw_ref, bias_ref, pos_ref, w_ref, te_ref, nu_ref,
                  e_s, r_s, base_s, start_s, *, tile_rows):
    ph = pl.program_id(0)
    i = pl.program_id(1)
    M = h_ref.shape[0]
    half = h_ref.shape[1]
    G, E = N_GROUPS, N_EXPERTS // N_GROUPS
    e_flat = lax.broadcasted_iota(jnp.int32, (N_EXPERTS, M), 0)

    @pl.when(ph == 1)
    def _():
        @pl.when(i == 0)
        def _():
            cnt = base_s[...]
            padded = jnp.floor((cnt + (tile_rows - 1)) * (1.0 / tile_rows)) * tile_rows
            r = lax.broadcasted_iota(jnp.int32, (N_EXPERTS, N_EXPERTS), 0)
            c = lax.broadcasted_iota(jnp.int32, (N_EXPERTS, N_EXPERTS), 1)
            start = jnp.dot((c < r).astype(F32), padded, preferred_element_type=F32,
                            precision=lax.Precision.HIGHEST)
            start_s[...] = start
            te_ref[...] = (start * (1.0 / tile_rows)).astype(jnp.int32)
            nu_ref[...] = (padded * (1.0 / tile_rows)).astype(jnp.int32)

        start_col = start_s[:, :1]
        for k in range(TOP_K):
            hit = e_flat == e_s[i, k:k + 1, :]
            seg = jnp.sum(jnp.where(hit, start_col, 0.0), axis=0, keepdims=True)
            pos_ref[k:k + 1, :] = (seg + r_s[i, k:k + 1, :]).astype(jnp.int32)

    @pl.when(ph == 0)
    def _():
        _route_pass0(h_ref, rw_ref, bias_ref, w_ref, e_s, r_s, base_s, i, M, half, G, E)


def _route_pass0(h_ref, rw_ref, bias_ref, w_ref, e_s, r_s, base_s, i, M, half, G, E):
    @pl.when(i == 0)
    def _():
        base_s[...] = jnp.zeros_like(base_s)

    lo, hi = _unpack_pairs(h_ref[...])
    rw = rw_ref[...].astype(BF16)
    logits = _dot_nt(rw[:, :half], lo) + _dot_nt(rw[:, half:], hi)
    s = jax.nn.sigmoid(logits)
    sb = (s + bias_ref[...]).reshape(G, E, M)
    s = s.reshape(G, E, M)
    e_in = lax.broadcasted_iota(jnp.int32, (G, E, M), 1).astype(F32)
    g_id = lax.broadcasted_iota(jnp.int32, (G, 1, M), 0)
    e_id = lax.broadcasted_iota(jnp.int32, (G, E, M), 0).astype(F32) * E + e_in

    def all_max(a):
        return jnp.max(jnp.max(a, axis=0, keepdims=True), axis=1, keepdims=True)

    def all_min(a):
        return jnp.min(jnp.min(a, axis=0, keepdims=True), axis=1, keepdims=True)

    def all_sum(a):
        return jnp.sum(jnp.sum(a, axis=0, keepdims=True), axis=1, keepdims=True)

    m1 = jnp.max(sb, axis=1, keepdims=True)
    first = jnp.min(jnp.where(sb == m1, e_in, float(E)), axis=1, keepdims=True)
    m2 = jnp.max(jnp.where(e_in == first, NEG_INF, sb), axis=1, keepdims=True)
    gs = m1 + m2

    rank = jnp.zeros((G, 1, M), jnp.int32)
    for j in range(G):
        gj = gs[j:j + 1]
        beats = (gj > gs) | ((gj == gs) & (j < g_id))
        rank = rank + beats.astype(jnp.int32)
    gsel = rank < TOPK_GROUPS

    vals = jnp.where(gsel, sb, NEG_INF)
    selm = jnp.zeros((G, E, M), F32)
    chosen, score = [], []
    for _ in range(TOP_K):
        m = all_max(vals)
        first = all_min(jnp.where(vals == m, e_id, float(N_EXPERTS)))
        hit = e_id == first
        score.append(all_sum(jnp.where(hit, s, 0.0)))
        selm = jnp.where(hit, 1.0, selm)
        vals = jnp.where(hit, NEG_INF, vals)
        chosen.append(first)

    tot = score[0]
    for sc in score[1:]:
        tot = tot + sc
    norm = ROUTED_SCALE / tot

    selm = selm.reshape(N_EXPERTS, M)
    earlier = (lax.broadcasted_iota(jnp.int32, (M, M), 0)
               < lax.broadcasted_iota(jnp.int32, (M, M), 1)).astype(BF16)
    rank = (base_s[:, :1] + _dot(selm.astype(BF16), earlier)).reshape(G, E, M)
    base_s[...] = base_s[...] + jnp.sum(selm, axis=1, keepdims=True)
    for k in range(TOP_K):
        hit = e_id == chosen[k]
        e_s[i, k:k + 1, :] = chosen[k].reshape(1, M).astype(jnp.int32)
        r_s[i, k:k + 1, :] = all_sum(jnp.where(hit, rank, 0.0)).reshape(1, M)
        w_ref[k:k + 1, :] = (score[k] * norm).reshape(1, M)


def route(hp, router_w_t, router_bias, l, tile_rows, rows=512):
    N, half = hp.shape
    M = rows
    nT = N // M
    assert N % M == 0

    def p0(ph, i):
        return i * (1 - ph) + (nT - 1) * ph

    return pl.pallas_call(
        functools.partial(_route_kernel, tile_rows=tile_rows),
        grid=(2, nT),
        in_specs=[pl.BlockSpec((M, half), lambda ph, i: (p0(ph, i), 0)),
                  pl.BlockSpec((None, N_EXPERTS, 2 * half), lambda ph, i: (l, 0, 0)),
                  pl.BlockSpec((None, N_EXPERTS, 1), lambda ph, i: (l, 0, 0))],
        out_specs=[pl.BlockSpec((TOP_K, M), lambda ph, i: (0, i * ph)),
                   pl.BlockSpec((TOP_K, M), lambda ph, i: (0, p0(ph, i))),
                   pl.BlockSpec((N_EXPERTS, 128), lambda ph, i: (0, 0)),
                   pl.BlockSpec((N_EXPERTS, 128), lambda ph, i: (0, 0))],
        out_shape=[jax.ShapeDtypeStruct((TOP_K, N), jnp.int32),
                   jax.ShapeDtypeStruct((TOP_K, N), F32),
                   jax.ShapeDtypeStruct((N_EXPERTS, 128), jnp.int32),
                   jax.ShapeDtypeStruct((N_EXPERTS, 128), jnp.int32)],
        scratch_shapes=[pltpu.VMEM((nT, TOP_K, M), jnp.int32), pltpu.VMEM((nT, TOP_K, M), F32),
                        pltpu.VMEM((N_EXPERTS, 128), F32), pltpu.VMEM((N_EXPERTS, 128), F32)],
        compiler_params=_cparams(2),
        name="route",
    )(hp, router_w_t, router_bias.reshape(-1, N_EXPERTS, 1))


def _sc_mesh():
    return plsc.VectorSubcoreMesh(core_axis_name="core", subcore_axis_name="subcore")


def sc_invert(pos_flat, n_tok, n_out):
    n = pos_flat.shape[0]
    per = n_out // SC_WORKERS
    chunk = n_tok
    assert n_out % SC_WORKERS == 0 and per % SC_LANES == 0
    assert n_tok % chunk == 0 and n % chunk == 0 and chunk % SC_LANES == 0
    cp = pltpu.CompilerParams()
    if "needs_layout_passes" in pltpu.CompilerParams.__dataclass_fields__:
        cp = dataclasses.replace(cp, needs_layout_passes=False)

    @functools.partial(
        pl.kernel, out_type=jax.ShapeDtypeStruct((n_out,), jnp.int32), mesh=_sc_mesh(),
        scratch_types=[pltpu.VMEM((chunk,), jnp.int32), pltpu.VMEM((per,), jnp.int32)],
        compiler_params=cp, name="sc_invert")
    def k(pos_hbm, src_hbm, pos_v, src_v):
        wid = lax.axis_index("subcore") * SC_CORES + lax.axis_index("core")
        lo = wid * per
        lane = lax.iota(jnp.int32, SC_LANES)

        @pl.loop(0, per, step=SC_LANES)
        def _(r):
            src_v[pl.ds(r, SC_LANES)] = lax.rem(lo + r + lane, n_tok)

        @pl.loop(0, n // chunk)
        def _(c):
            base = c * chunk
            pltpu.sync_copy(pos_hbm.at[pl.ds(base, chunk)], pos_v)
            tok0 = lax.rem(base, n_tok)

            @plsc.parallel_loop(0, chunk, step=SC_LANES, unroll=8)
            def _(r):
                p = pos_v[pl.ds(r, SC_LANES)] - lo
                mine = (p >= 0) & (p < per)
                plsc.store_scatter(src_v, [jnp.where(mine, p, 0)], tok0 + r + lane, mask=mine)

        pltpu.sync_copy(src_v, src_hbm.at[pl.ds(lo, per)])

    return k(pos_flat)


def sc_gather(x, idx):
    n = idx.shape[0]
    dim = x.shape[1]
    assert n % (SC_WINDOW * SC_WORKERS) == 0

    @functools.partial(
        pl.kernel, out_type=jax.ShapeDtypeStruct((n, dim), x.dtype), mesh=_sc_mesh(),
        scratch_types=[], name="sc_gather")
    def k(x_hbm, i_hbm, o_hbm):
        def body(i_vmem, o_vmem):
            pltpu.sync_copy(x_hbm.at[i_vmem.at[0]], o_vmem)

        pltpu.emit_pipeline(
            body, grid=(n // SC_WINDOW,),
            in_specs=[pl.BlockSpec((1, SC_WINDOW), index_map=lambda i: (i, 0))],
            out_specs=[pl.BlockSpec((SC_WINDOW, dim), index_map=lambda i: (i, 0))],
            core_axis_name=("core", "subcore"),
            dimension_semantics=(pltpu.PARALLEL,),
        )(i_hbm, o_hbm)

    return k(x, idx.reshape(n // SC_WINDOW, SC_WINDOW))


def _moe_gemm_kernel(ts_ref, tn_ref, x_hbm, wi_ref, wo_ref, o_hbm, wi_b, wo_b, xbuf, obuf, in_sem, out_sem,
                     *, tile_rows, n_tiles):
    e = pl.program_id(0)
    last = pl.num_programs(0) - 1
    t0 = ts_ref[e]
    n = tn_ref[e]
    n_used = ts_ref[last] + tn_ref[last]

    def x_copy(g, slot):
        rows = pl.ds(pl.multiple_of(g * tile_rows, tile_rows), tile_rows)
        return pltpu.make_async_copy(x_hbm.at[rows], xbuf.at[slot], in_sem.at[slot])

    def o_copy(g, slot):
        rows = pl.ds(pl.multiple_of(g * tile_rows, tile_rows), tile_rows)
        return pltpu.make_async_copy(obuf.at[slot], o_hbm.at[rows], out_sem.at[slot])

    @pl.when(e == 0)
    def _():
        for g0 in range(MOE_NBUF - 1):
            @pl.when(g0 < n_used)
            def _():
                x_copy(g0, g0).start()

    @pl.when(n > 0)
    def _():
        wi_b[...] = wi_ref[...].astype(BF16)
        wo_b[...] = wo_ref[...].astype(BF16)

    def tile(i, carry):
        g = t0 + i
        slot = lax.rem(g, MOE_NBUF)
        x_copy(g, slot).wait()
        ahead = g + (MOE_NBUF - 1)

        @pl.when(ahead < n_used)
        def _():
            x_copy(ahead, lax.rem(ahead, MOE_NBUF)).start()

        @pl.when(g >= MOE_NBUF)
        def _():
            o_copy(g - MOE_NBUF, slot).wait()

        rows = tile_rows // MOE_SUB
        half = xbuf.shape[2]
        xs = [_unpack_pairs(xbuf[slot, r * rows:(r + 1) * rows, :]) for r in range(MOE_SUB)]
        hus = [_dot(lo, wi_b[:half, :]) + _dot(hi, wi_b[half:, :]) for lo, hi in xs]
        acts = [(_silu(hu[:, :EXPERT_FF]) * hu[:, EXPERT_FF:]).astype(BF16) for hu in hus]
        outs = [_dot(act, wo_b[...]) for act in acts]
        for r, out in enumerate(outs):
            obuf[slot, r * rows:(r + 1) * rows, :] = _pack_pairs(out)
        o_copy(g, slot).start()
        return carry

    lax.fori_loop(0, n, tile, 0)

    @pl.when(e == last)
    def _():
        for back in range(MOE_NBUF, 0, -1):
            @pl.when(n_used >= back)
            def _():
                o_copy(n_used - back, lax.rem(n_used - back, MOE_NBUF)).wait()

        obuf[0] = jnp.zeros(obuf.shape[1:], obuf.dtype)

        def clear(g, carry):
            cp = o_copy(g, 0)
            cp.start()
            cp.wait()
            return carry

        lax.fori_loop(n_used, n_tiles, clear, 0)


def moe_gemm(xs, tile_start, tile_count, exp_w_in, exp_w_out, l, tile_rows, n_tiles):
    P, half = xs.shape
    D = 2 * half
    assert P == n_tiles * tile_rows
    hbm = pl.BlockSpec(memory_space=pl.ANY)
    grid_spec = pltpu.PrefetchScalarGridSpec(
        num_scalar_prefetch=2,
        grid=(N_EXPERTS,),
        in_specs=[hbm,
                  pl.BlockSpec((None, None, D, 2 * EXPERT_FF), lambda e, ts, tn: (l, e, 0, 0)),
                  pl.BlockSpec((None, None, EXPERT_FF, D), lambda e, ts, tn: (l, e, 0, 0))],
        out_specs=hbm,
        scratch_shapes=[pltpu.VMEM((D, 2 * EXPERT_FF), BF16), pltpu.VMEM((EXPERT_FF, D), BF16),
                        pltpu.VMEM((MOE_NBUF, tile_rows, half), jnp.int32),
                        pltpu.VMEM((MOE_NBUF, tile_rows, half), jnp.int32),
                        pltpu.SemaphoreType.DMA((MOE_NBUF,)), pltpu.SemaphoreType.DMA((MOE_NBUF,))],
    )
    return pl.pallas_call(
        functools.partial(_moe_gemm_kernel, tile_rows=tile_rows, n_tiles=n_tiles),
        grid_spec=grid_spec,
        out_shape=jax.ShapeDtypeStruct((P, half), jnp.int32),
        compiler_params=_cparams(1),
        name="moe_gemm",
    )(tile_start, tile_count, xs, exp_w_in, exp_w_out)


def _moe_combine_kernel(*refs, final, norm_next):
    if final:
        y_ref, w_ref, h_ref, si_ref, so_ref, x_ref, g2_ref, fg_ref, o_ref, si_b, so_b = refs
    elif norm_next:
        (y_ref, w_ref, h_ref, si_ref, so_ref, x_ref, g2_ref, ng_ref, nsc_ref, nsh_ref,
         o_ref, hn_ref, si_b, so_b) = refs
    else:
        y_ref, w_ref, h_ref, si_ref, so_ref, x_ref, g2_ref, o_ref, si_b, so_b = refs

    @pl.when(pl.program_id(0) == 0)
    def _():
        si_b[...] = si_ref[...].astype(BF16)
        so_b[...] = so_ref[...].astype(BF16)

    bb, tt, D = x_ref.shape
    half = D // 2
    w = w_ref[...]
    acc_lo = jnp.zeros((bb * tt, half), F32)
    acc_hi = jnp.zeros((bb * tt, half), F32)
    for k in range(TOP_K):
        lo, hi = _unpack_pairs(y_ref[k], F32)
        acc_lo = acc_lo + w[:, k:k + 1] * lo
        acc_hi = acc_hi + w[:, k:k + 1] * hi
    hlo, hhi = _unpack_pairs(h_ref[...])
    hu = _dot(hlo, si_b[:half, :]) + _dot(hhi, si_b[half:, :])
    act = (_silu(hu[:, :SHARED_FF]) * hu[:, SHARED_FF:]).astype(BF16)
    y = jnp.concatenate([acc_lo, acc_hi], axis=-1) + _dot(act, so_b[...])
    x_new = x_ref[...] + g2_ref[...] * y.reshape(bb, tt, D)
    o_ref[...] = _rms(x_new, fg_ref[...]) if final else x_new
    if norm_next:
        hn = _rms(x_new, ng_ref[...]) * (1.0 + nsc_ref[...]) + nsh_ref[...]
        hn_ref[...] = hn.astype(hn_ref.dtype)


def moe_combine(y8, w_t, hp, sh_w_in, sh_w_out, x, mod, gate_idx, l, row0, final_g=None, next_norm=None,
                rows=256):
    B, T, D = x.shape
    half = D // 2
    bb, tt, nblk, ij = _row_blocks(B, T, rows)
    M = bb * tt
    assert row0 % M == 0
    off = row0 // M
    xspec = pl.BlockSpec((bb, tt, D), lambda i: ij(i) + (0,))
    in_specs = [pl.BlockSpec((TOP_K, M, half), lambda i: (0, off + i, 0)),
                pl.BlockSpec((M, TOP_K), lambda i: (off + i, 0)),
                pl.BlockSpec((M, half), lambda i: (off + i, 0)),
                pl.BlockSpec((None, D, 2 * SHARED_FF), lambda i: (l, 0, 0)),
                pl.BlockSpec((None, SHARED_FF, D), lambda i: (l, 0, 0)),
                xspec,
                pl.BlockSpec((bb, 1, D), lambda i: (ij(i)[0], 0, gate_idx))]
    args = [y8, w_t, hp, sh_w_in, sh_w_out, x, mod]
    out_specs = xspec
    out_shape = jax.ShapeDtypeStruct((B, T, D), F32)
    if final_g is not None:
        assert next_norm is None
        in_specs.append(pl.BlockSpec((1, D), lambda i: (0, 0)))
        args.append(final_g.reshape(1, D))
    if next_norm is not None:
        gain, mod_next, sc_idx, sh_idx = next_norm
        in_specs += [pl.BlockSpec((1, D), lambda i: (0, 0)),
                     pl.BlockSpec((bb, 1, D), lambda i: (ij(i)[0], 0, sc_idx)),
                     pl.BlockSpec((bb, 1, D), lambda i: (ij(i)[0], 0, sh_idx))]
        args += [gain.reshape(1, D), mod_next, mod_next]
        out_specs = [xspec, xspec]
        out_shape = [out_shape, jax.ShapeDtypeStruct((B, T, D), BF16)]
    return pl.pallas_call(
        functools.partial(_moe_combine_kernel, final=final_g is not None, norm_next=next_norm is not None),
        grid=(nblk,),
        in_specs=in_specs,
        out_specs=out_specs,
        out_shape=out_shape,
        scratch_shapes=[pltpu.VMEM((D, 2 * SHARED_FF), BF16), pltpu.VMEM((SHARED_FF, D), BF16)],
        compiler_params=_cparams(1),
        name="moe_combine",
    )(*args)


def _shared_kv_kernel(x_ref, g_ref, w_ref, lg_ref, cos_ref, sin_ref, lat_ref, kr_ref):
    bb, tt, D = x_ref.shape
    xn = _rms(x_ref[...], g_ref[...]).reshape(bb * tt, D).astype(BF16)
    z = _dot(xn, w_ref[...].astype(BF16))
    lat = _rms(z[:, :KV_LORA], lg_ref[...])
    lat_ref[...] = lat.reshape(bb, tt, KV_LORA)
    zr = z[:, KV_LORA:KV_LORA + MLA_ROPE].reshape(bb, tt, MLA_ROPE)
    zq = z[:, KV_LORA + 128:KV_LORA + 128 + MLA_ROPE].reshape(bb, tt, MLA_ROPE)
    kr_ref[...] = zr * cos_ref[...] + zq * sin_ref[...]


def shared_kv(x, kv_in_g, w_kv, kv_lat_g, cos32, sin32, rows=512):
    B, T, D = x.shape
    bb, tt, nblk, ij = _row_blocks(B, T, rows)
    tspec = pl.BlockSpec((tt, MLA_ROPE), lambda i: (ij(i)[1], 0))
    return pl.pallas_call(
        _shared_kv_kernel,
        grid=(nblk,),
        in_specs=[pl.BlockSpec((bb, tt, D), lambda i: ij(i) + (0,)),
                  pl.BlockSpec((1, D), lambda i: (0, 0)),
                  pl.BlockSpec(w_kv.shape, lambda i: (0, 0)),
                  pl.BlockSpec((1, KV_LORA), lambda i: (0, 0)),
                  tspec, tspec],
        out_specs=[pl.BlockSpec((bb, tt, KV_LORA), lambda i: ij(i) + (0,)),
                   pl.BlockSpec((bb, tt, MLA_ROPE), lambda i: ij(i) + (0,))],
        out_shape=[jax.ShapeDtypeStruct((B, T, KV_LORA), F32),
                   jax.ShapeDtypeStruct((B, T, MLA_ROPE), F32)],
        compiler_params=_cparams(1),
        name="shared_kv",
    )(x, kv_in_g.reshape(1, D), w_kv, kv_lat_g.reshape(1, KV_LORA), cos32, sin32)


def _kv_expand_kernel(lat_ref, kr_ref, wk_ref, ek_ref, wvt_ref, ones_ref, k_ref, vt_ref):
    lat = lat_ref[0].astype(BF16)
    kr = kr_ref[0].astype(BF16)
    k = _dot(lat, wk_ref[...].astype(BF16)) + _dot(kr, ek_ref[...].astype(BF16))
    k_ref[0] = k.astype(k_ref.dtype)
    vt = _dot_nt(wvt_ref[...].astype(BF16), lat) + ones_ref[...]
    vt_ref[0] = vt.astype(vt_ref.dtype)


def kv_expand(lat, kr, wk_pad, ek, wvt_ext, ones_col, rows=512):
    B, T, _ = lat.shape
    tt = rows
    NK, NVT = wk_pad.shape[1], wvt_ext.shape[0]

    def full(a):
        return pl.BlockSpec(a.shape, lambda b, t: (0, 0))

    def rowspec(n):
        return pl.BlockSpec((1, tt, n), lambda b, t: (b, t, 0))

    return pl.pallas_call(
        _kv_expand_kernel,
        grid=(B, T // tt),
        in_specs=[rowspec(KV_LORA), rowspec(MLA_ROPE), full(wk_pad), full(ek), full(wvt_ext), full(ones_col)],
        out_specs=[rowspec(NK), pl.BlockSpec((1, NVT, tt), lambda b, t: (b, 0, t))],
        out_shape=[jax.ShapeDtypeStruct((B, T, NK), BF16), jax.ShapeDtypeStruct((B, NVT, T), BF16)],
        compiler_params=_cparams(2),
        name="kv_expand",
    )(lat, kr, wk_pad, ek, wvt_ext, ones_col)


def _query_kernel(h_ref, wdq_ref, qg_ref, wq_ref, wqr_ref, c_ref, s_ref, q_ref, wdq_b, wq_b, wqr_b):
    @pl.when(pl.program_id(0) == 0)
    def _():
        wdq_b[...] = wdq_ref[...].astype(BF16)
        wq_b[...] = wq_ref[...].astype(BF16)
        wqr_b[...] = wqr_ref[...].astype(BF16)

    bb, tt, D = h_ref.shape
    h = h_ref[...].reshape(bb * tt, D)
    cq = _rms(_dot(h, wdq_b[...]), qg_ref[...]).astype(BF16)
    q1 = _dot(cq, wq_b[...]).reshape(bb, tt, -1)
    q2 = _dot(cq, wqr_b[...]).reshape(bb, tt, -1)
    c = c_ref[...]
    s = s_ref[...]
    for hd in range(MLA_HEADS):
        sl = slice(hd * HEAD_PAD, (hd + 1) * HEAD_PAD)
        q_ref[:, :, sl] = (q1[:, :, sl] * c + q2[:, :, sl] * s).astype(q_ref.dtype)


def mla_queries(h, w_dq, q_norm_g, wq_pad, wq_rot, l, c128, s128, rows=512):
    B, T, D = h.shape
    bb, tt, nblk, ij = _row_blocks(B, T, rows)
    NQ = wq_pad.shape[-1]
    tspec = pl.BlockSpec((tt, HEAD_PAD), lambda i: (ij(i)[1], 0))
    return pl.pallas_call(
        _query_kernel,
        grid=(nblk,),
        in_specs=[pl.BlockSpec((bb, tt, D), lambda i: ij(i) + (0,)),
                  pl.BlockSpec((None, D, Q_LORA), lambda i: (l, 0, 0)),
                  pl.BlockSpec((None, 1, Q_LORA), lambda i: (l, 0, 0)),
                  pl.BlockSpec((None, Q_LORA, NQ), lambda i: (l, 0, 0)),
                  pl.BlockSpec((None, Q_LORA, NQ), lambda i: (l, 0, 0)),
                  tspec, tspec],
        out_specs=pl.BlockSpec((bb, tt, NQ), lambda i: ij(i) + (0,)),
        out_shape=jax.ShapeDtypeStruct((B, T, NQ), BF16),
        scratch_shapes=[pltpu.VMEM((D, Q_LORA), BF16), pltpu.VMEM((Q_LORA, NQ), BF16),
                        pltpu.VMEM((Q_LORA, NQ), BF16)],
        compiler_params=_cparams(1),
        name="mla_queries",
    )(h, w_dq, q_norm_g.reshape(-1, 1, Q_LORA), wq_pad, wq_rot, c128, s128)


def _query_t_kernel(h_ref, wdq_ref, qg_ref, wqt_ref, wqrt_ref, cos_ref, sin_ref, qt_ref, wdq_b, wqt_b, wqrt_b):
    @pl.when((pl.program_id(0) == 0) & (pl.program_id(1) == 0))
    def _():
        wdq_b[...] = wdq_ref[...].astype(BF16)
        wqt_b[...] = wqt_ref[...].astype(BF16)
        wqrt_b[...] = wqrt_ref[...].astype(BF16)

    cq = _rms(_dot(h_ref[0], wdq_b[...]), qg_ref[...]).astype(BF16)
    q1 = _dot_nt(wqt_b[...], cq)
    q2 = _dot_nt(wqrt_b[...], cq)
    cos = cos_ref[...]
    sin = sin_ref[...]
    pad = jnp.zeros((HEAD_PAD - MLA_NOPE - MLA_ROPE, q1.shape[1]), qt_ref.dtype)
    for hd in range(MLA_HEADS):
        r0 = hd * HEAD_PAD
        rope = (q1[r0 + MLA_NOPE:r0 + MLA_NOPE + MLA_ROPE] * cos
                + q2[hd * MLA_ROPE:(hd + 1) * MLA_ROPE] * sin)
        qt_ref[0, r0:r0 + MLA_NOPE, :] = (q1[r0:r0 + MLA_NOPE] * Q_PRESCALE).astype(qt_ref.dtype)
        qt_ref[0, r0 + MLA_NOPE:r0 + MLA_NOPE + MLA_ROPE, :] = rope.astype(qt_ref.dtype)
        qt_ref[0, r0 + MLA_NOPE + MLA_ROPE:r0 + HEAD_PAD, :] = pad


def mla_queries_t(h, w_dq, q_norm_g, wq_t, wqr_t, l, cos_t, sin_t, rows=512):
    B, T, D = h.shape
    tt = rows
    NQ = wq_t.shape[1]
    NR = wqr_t.shape[1]
    tspec = pl.BlockSpec((MLA_ROPE, tt), lambda b, t: (0, t))
    return pl.pallas_call(
        _query_t_kernel,
        grid=(B, T // tt),
        in_specs=[pl.BlockSpec((1, tt, D), lambda b, t: (b, t, 0)),
                  pl.BlockSpec((None, D, Q_LORA), lambda b, t: (l, 0, 0)),
                  pl.BlockSpec((None, 1, Q_LORA), lambda b, t: (l, 0, 0)),
                  pl.BlockSpec((None, NQ, Q_LORA), lambda b, t: (l, 0, 0)),
                  pl.BlockSpec((None, NR, Q_LORA), lambda b, t: (l, 0, 0)),
                  tspec, tspec],
        out_specs=pl.BlockSpec((1, NQ, tt), lambda b, t: (b, 0, t)),
        out_shape=jax.ShapeDtypeStruct((B, NQ, T), BF16),
        scratch_shapes=[pltpu.VMEM((D, Q_LORA), BF16), pltpu.VMEM((NQ, Q_LORA), BF16),
                        pltpu.VMEM((NR, Q_LORA), BF16)],
        compiler_params=_cparams(2),
        name="mla_queries_t",
    )(h, w_dq, q_norm_g.reshape(-1, 1, Q_LORA), wq_t, wqr_t, cos_t, sin_t)


def _attn_prompt_kernel(qi_tab, ki_tab, qt_ref, k_ref, vt_ref, o_ref, *scratch, tq, tk):
    H = MLA_HEADS
    m_refs, l_refs, acc_refs = scratch[:H], scratch[H:2 * H], scratch[2 * H:]
    p_id = pl.program_id(1)
    qi = qi_tab[p_id]
    ki = ki_tab[p_id]

    @pl.when(ki == 0)
    def _():
        for hd in range(H):
            m_refs[hd][...] = jnp.full(m_refs[hd].shape, NEG_INF, F32)
            l_refs[hd][...] = jnp.zeros(l_refs[hd].shape, F32)
            acc_refs[hd][...] = jnp.zeros(acc_refs[hd].shape, F32)

    def block(masked):
        if masked:
            kchunk = (ki * tk + lax.broadcasted_iota(jnp.int32, (tk, tq), 0)) // CHUNK
            qchunk = (qi * tq + lax.broadcasted_iota(jnp.int32, (tk, tq), 1)) // CHUNK
            mask = kchunk <= qchunk
        def scores(hd):
            sl = slice(hd * HEAD_PAD, (hd + 1) * HEAD_PAD)
            return _dot(k_ref[0, :, sl], qt_ref[0, sl, :])

        pending = [scores(hd) for hd in range(ATTN_LOOKAHEAD)]
        for hd in range(H):
            if hd + ATTN_LOOKAHEAD < H:
                pending.append(scores(hd + ATTN_LOOKAHEAD))
            s = pending.pop(0)
            if masked:
                s = jnp.where(mask, s, NEG_INF)
            m_prev = m_refs[hd][...]
            m_new = jnp.maximum(m_prev, jnp.max(s, axis=0, keepdims=True))
            a = jnp.exp2(m_prev - m_new)
            p = jnp.exp2(s - m_new).astype(BF16)
            pv = _dot(vt_ref[0, hd * V_ROWS:(hd + 1) * V_ROWS, :], p)
            acc_refs[hd][...] = a * acc_refs[hd][...] + pv[:MLA_V]
            l_refs[hd][...] = a * l_refs[hd][...] + pv[MLA_V:MLA_V + 1]
            m_refs[hd][...] = m_new

    @pl.when(ki < qi)
    def _():
        block(False)

    @pl.when(ki == qi)
    def _():
        block(True)
        o_t = jnp.concatenate([acc_refs[hd][...] / l_refs[hd][...] for hd in range(H)], axis=0)
        o_ref[0] = o_t.T.astype(o_ref.dtype)


def attn_prompt(qt, k, vt, tq=256):
    B, NQ, T = qt.shape
    NVT = vt.shape[1]
    NV = MLA_HEADS * MLA_V
    tk = tq
    assert tq % CHUNK == 0
    nq = T // tq
    pairs = [(a, b) for a in range(nq) for b in range(a + 1)]
    qi_tab = jnp.asarray([a for a, _ in pairs], jnp.int32)
    ki_tab = jnp.asarray([b for _, b in pairs], jnp.int32)
    grid_spec = pltpu.PrefetchScalarGridSpec(
        num_scalar_prefetch=2,
        grid=(B, len(pairs)),
        in_specs=[pl.BlockSpec((1, NQ, tq), lambda b, p, qt, kt: (b, 0, qt[p])),
                  pl.BlockSpec((1, tk, NQ), lambda b, p, qt, kt: (b, kt[p], 0)),
                  pl.BlockSpec((1, NVT, tk), lambda b, p, qt, kt: (b, 0, kt[p]))],
        out_specs=pl.BlockSpec((1, tq, NV), lambda b, p, qt, kt: (b, qt[p], 0)),
        scratch_shapes=([pltpu.VMEM((1, tq), F32)] * (2 * MLA_HEADS)
                        + [pltpu.VMEM((MLA_V, tq), F32)] * MLA_HEADS),
    )
    return pl.pallas_call(
        functools.partial(_attn_prompt_kernel, tq=tq, tk=tk),
        grid_spec=grid_spec,
        out_shape=jax.ShapeDtypeStruct((B, T, NV), BF16),
        compiler_params=_cparams(2),
        name="attn_prompt",
    )(qi_tab, ki_tab, qt, k, vt)


def _absorb_kernel(q_ref, m_ref, o_ref):
    o_ref[...] = _dot(q_ref[...], m_ref[...].astype(BF16)).astype(o_ref.dtype)


def absorb_queries(q2d, m_abs):
    N = q2d.shape[0]
    H, _, W = m_abs.shape
    return pl.pallas_call(
        _absorb_kernel,
        grid=(H,),
        in_specs=[pl.BlockSpec((N, HEAD_PAD), lambda h: (0, h)),
                  pl.BlockSpec((None, HEAD_PAD, W), lambda h: (h, 0, 0))],
        out_specs=pl.BlockSpec((None, N, W), lambda h: (h, 0, 0)),
        out_shape=jax.ShapeDtypeStruct((H, N, W), BF16),
        compiler_params=_cparams(1),
        name="absorb_queries",
    )(q2d, m_abs)


def _attn_sample_kernel(q_ref, lat_ref, kr_ref, nlat_ref, nkr_ref, o_ref, m_ref, l_ref, acc_ref):
    kb = pl.program_id(1)
    H, Q, W = q_ref.shape
    q = q_ref[...].reshape(H * Q, W)
    q_lat = q[:, :KV_LORA]
    q_rope = q[:, KV_LORA:KV_LORA + MLA_ROPE]

    def update(lat_tile, kr_tile, n_sub, kr_transposed):
        sub = lat_tile.shape[0] // n_sub
        lats = [lat_tile[j * sub:(j + 1) * sub, :].astype(BF16) for j in range(n_sub)]
        if kr_transposed:
            krs = [kr_tile[:, j * sub:(j + 1) * sub].astype(BF16) for j in range(n_sub)]
            ss = [_dot_nt(q_lat, lat) + _dot(q_rope, kr) for lat, kr in zip(lats, krs)]
        else:
            krs = [kr_tile[j * sub:(j + 1) * sub, :].astype(BF16) for j in range(n_sub)]
            ss = [_dot_nt(q_lat, lat) + _dot_nt(q_rope, kr) for lat, kr in zip(lats, krs)]
        m_prev = m_ref[...]
        m_new = m_prev
        for s in ss:
            m_new = jnp.maximum(m_new, jnp.max(s, axis=-1, keepdims=True))
        a = jnp.exp2(m_prev - m_new)
        ps = [jnp.exp2(s - m_new[:, :1]) for s in ss]
        pv = _dot(ps[0].astype(BF16), lats[0])
        psum = jnp.sum(ps[0], axis=-1, keepdims=True)
        for p, lat in zip(ps[1:], lats[1:]):
            pv = pv + _dot(p.astype(BF16), lat)
            psum = psum + jnp.sum(p, axis=-1, keepdims=True)
        l_ref[...] = a * l_ref[...] + psum
        m_ref[...] = m_new
        acc_ref[...] = jnp.concatenate([a, a], axis=-1) * acc_ref[...] + pv

    @pl.when(kb == 0)
    def _():
        m_ref[...] = jnp.full_like(m_ref, NEG_INF)
        l_ref[...] = jnp.zeros_like(l_ref)
        acc_ref[...] = jnp.zeros_like(acc_ref)
        update(nlat_ref[0], nkr_ref[0], 1, False)

    update(lat_ref[0], kr_ref[0], SAMPLE_KEY_SUB, True)

    @pl.when(kb == pl.num_programs(1) - 1)
    def _():
        lsum = l_ref[...]
        o = acc_ref[...] / jnp.concatenate([lsum, lsum], axis=-1)
        o_ref[...] = o.reshape(H, Q, KV_LORA).astype(o_ref.dtype)


def attn_sample(q_abs, cache_lat, cache_kr_t, new_lat, new_kr, tk=4096):
    H, N, W = q_abs.shape
    B, P, _ = cache_lat.shape
    Q = new_lat.shape[1]
    qpos = P + np.arange(Q)
    kpos = np.arange(P + Q)
    assert bool(np.all((kpos // CHUNK)[None, :] <= (qpos // CHUNK)[:, None]))
    return pl.pallas_call(
        _attn_sample_kernel,
        grid=(B, P // tk),
        in_specs=[pl.BlockSpec((H, Q, W), lambda b, kb: (0, b, 0)),
                  pl.BlockSpec((1, tk, KV_LORA), lambda b, kb: (b, kb, 0)),
                  pl.BlockSpec((1, MLA_ROPE, tk), lambda b, kb: (b, 0, kb)),
                  pl.BlockSpec((1, Q, KV_LORA), lambda b, kb: (b, 0, 0)),
                  pl.BlockSpec((1, Q, MLA_ROPE), lambda b, kb: (b, 0, 0))],
        out_specs=pl.BlockSpec((H, Q, KV_LORA), lambda b, kb: (0, b, 0)),
        out_shape=jax.ShapeDtypeStruct((H, N, KV_LORA), BF16),
        scratch_shapes=[pltpu.VMEM((H * Q, 128), F32), pltpu.VMEM((H * Q, 128), F32),
                        pltpu.VMEM((H * Q, KV_LORA), F32)],
        compiler_params=_cparams(2),
        name="attn_sample",
    )(q_abs, cache_lat, cache_kr_t, new_lat, new_kr)


def _unabsorb_kernel(o_ref, w_ref, out_ref):
    out_ref[...] = (_dot(o_ref[0], w_ref[0].astype(BF16))
                    + _dot(o_ref[1], w_ref[1].astype(BF16))).astype(out_ref.dtype)


def unabsorb(o_lat, wuv_pad):
    H, N, R = o_lat.shape
    return pl.pallas_call(
        _unabsorb_kernel,
        grid=(H // 2,),
        in_specs=[pl.BlockSpec((2, N, R), lambda p: (p, 0, 0)),
                  pl.BlockSpec((2, R, 128), lambda p: (p, 0, 0))],
        out_specs=pl.BlockSpec((N, 128), lambda p: (0, p)),
        out_shape=jax.ShapeDtypeStruct((N, (H // 2) * 128), BF16),
        compiler_params=_cparams(1),
        name="unabsorb",
    )(o_lat, wuv_pad)


def _rope_tables(pos):
    half = MLA_ROPE // 2
    inv = 1.0 / (ROPE_THETA ** (np.arange(half, dtype=np.float64) * 2.0 / MLA_ROPE))
    ang = np.asarray(pos, np.float64)[:, None] * inv[None, :]
    cos = np.concatenate([np.cos(ang), np.cos(ang)], axis=-1)
    sin = np.concatenate([np.sin(ang), np.sin(ang)], axis=-1)
    T = cos.shape[0]
    c128 = np.zeros((T, HEAD_PAD)); s128 = np.zeros((T, HEAD_PAD))
    c128[:, :MLA_NOPE] = 1.0
    c128[:, MLA_NOPE:MLA_NOPE + MLA_ROPE] = cos
    s128[:, MLA_NOPE:MLA_NOPE + MLA_ROPE] = sin
    return dict(cos32=jnp.asarray(cos, F32), sin32=jnp.asarray(sin, F32),
                c128=jnp.asarray(c128 * Q_PRESCALE, F32), s128=jnp.asarray(s128 * Q_PRESCALE, F32),
                cos_t=jnp.asarray(cos.T * Q_PRESCALE, F32), sin_t=jnp.asarray(sin.T * Q_PRESCALE, F32))


def _rot_half_cols(w):
    half = w.shape[-1] // 2
    return jnp.concatenate([-w[..., half:], w[..., :half]], axis=-1)


def _prep_weights(w_dkv, w_uk, w_uv, w_uq, router_w):
    D = D_MODEL
    w_lat, w_rope = w_dkv[:, :KV_LORA], w_dkv[:, KV_LORA:]
    pad96 = jnp.zeros((D, 128 - MLA_ROPE), F32)
    w_kv = jnp.concatenate([w_lat, w_rope, pad96, _rot_half_cols(w_rope), pad96], axis=-1)

    zpad = HEAD_PAD - MLA_NOPE
    wk_pad = jnp.pad(w_uk, ((0, 0), (0, 0), (0, zpad))).reshape(KV_LORA, MLA_HEADS * HEAD_PAD)
    ek = jnp.zeros((MLA_ROPE, MLA_HEADS, HEAD_PAD), F32)
    ek = ek.at[:, :, MLA_NOPE:MLA_NOPE + MLA_ROPE].set(
        jnp.broadcast_to(jnp.eye(MLA_ROPE, dtype=F32)[:, None, :], (MLA_ROPE, MLA_HEADS, MLA_ROPE)))
    ek = ek.reshape(MLA_ROPE, MLA_HEADS * HEAD_PAD)
    wvt = jnp.transpose(w_uv, (1, 2, 0))
    wvt_ext = jnp.pad(wvt, ((0, 0), (0, V_ROWS - MLA_V), (0, 0))).reshape(MLA_HEADS * V_ROWS, KV_LORA)
    ones_col = jnp.tile((jnp.arange(V_ROWS) >= MLA_V).astype(F32), MLA_HEADS).reshape(-1, 1)

    nb = w_uq.shape[0]
    qn, qr = w_uq[..., :MLA_NOPE], w_uq[..., MLA_NOPE:]
    z32 = jnp.zeros(qr.shape[:-1] + (HEAD_PAD - MLA_NOPE - MLA_ROPE,), F32)
    wq_pad = jnp.concatenate([qn, qr, z32], axis=-1).reshape(nb, Q_LORA, MLA_HEADS * HEAD_PAD)
    wq_rot = jnp.concatenate([jnp.zeros_like(qn), _rot_half_cols(qr), z32], axis=-1)
    wq_rot = wq_rot.reshape(nb, Q_LORA, MLA_HEADS * HEAD_PAD)
    wq_t = jnp.transpose(wq_pad, (0, 2, 1))
    wqr_t = jnp.transpose(_rot_half_cols(qr).reshape(nb, Q_LORA, MLA_HEADS * MLA_ROPE), (0, 2, 1))

    m_abs = jnp.zeros((MLA_HEADS, HEAD_PAD, KV_LORA + 128), F32)
    m_abs = m_abs.at[:, :MLA_NOPE, :KV_LORA].set(jnp.transpose(w_uk, (1, 2, 0)))
    m_abs = m_abs.at[:, MLA_NOPE:MLA_NOPE + MLA_ROPE, KV_LORA:KV_LORA + MLA_ROPE].set(
        jnp.broadcast_to(jnp.eye(MLA_ROPE, dtype=F32), (MLA_HEADS, MLA_ROPE, MLA_ROPE)))

    wuv_h = jnp.transpose(w_uv, (1, 0, 2))
    even = jnp.pad(wuv_h, ((0, 0), (0, 0), (0, 64)))
    odd = jnp.pad(wuv_h, ((0, 0), (0, 0), (64, 0)))
    wuv_pad = jnp.where((jnp.arange(MLA_HEADS) % 2 == 0)[:, None, None], even, odd)

    rw_t = jnp.transpose(router_w, (0, 2, 1))
    return dict(w_kv=w_kv, wk_pad=wk_pad, ek=ek, wvt_ext=wvt_ext, ones_col=ones_col, wq_pad=wq_pad, wq_rot=wq_rot, wq_t=wq_t, wqr_t=wqr_t,
                m_abs=m_abs, wuv_pad=wuv_pad, rw_t=rw_t)


def _mixer(st, l, P, W, packed):
    rows_kw = dict(rows_total=packed["total"], row0=packed["row0"], rows_buf=packed["buf"])
    x, m = st["x"], st["mod"][l]
    B, T, _ = x.shape
    n_a = P["hg_w_in"].shape[0]
    h = st.pop("h_next", None)
    if h is None:
        h = norm_mod(x, P["norm1_g"][l], m, sc_idx=1, sh_idx=0)
    norm2 = (P["norm2_g"][l], 4, 3)
    if l < n_a:
        zf = linear(h, P["hg_w_in"], l, F32, col_blocks=(1,))
        zqig = linear(h, P["hg_w_in"], l, BF16, col_blocks=(0, 2, 3))
        s0 = None if st["hg_state"] is None else st["hg_state"][l]
        o, s_new = gla(zqig, zf, st["lbs"][l], P["hg_onorm_g"][l], s0)
        st["hg_new"].append(s_new)
        st["x"], packed["buf"] = linear(o, P["hg_w_out"], l, F32, x=x, mod=m, gate_idx=2, next_norm=norm2,
                                        **rows_kw)
    else:
        bi = l - n_a
        if st["past_lat"] is None:
            qt = mla_queries_t(h, P["w_dq"], P["q_norm_g"], W["wq_t"], W["wqr_t"], bi, st["cos_t"], st["sin_t"])
            o = attn_prompt(qt, st["k_all"], st["v_all"])
        else:
            q = mla_queries(h, P["w_dq"], P["q_norm_g"], W["wq_pad"], W["wq_rot"], bi, st["c128"], st["s128"])
            q_abs = absorb_queries(q.reshape(B * T, -1), W["m_abs"])
            o_lat = attn_sample(q_abs, st["past_lat"], st["past_kr"], st["lat"], st["kr"])
            o = unabsorb(o_lat, W["wuv_pad"]).reshape(B, T, -1)
        st["x"], packed["buf"] = linear(o, P["w_o"], bi, F32, x=x, mod=m, gate_idx=2, next_norm=norm2, **rows_kw)
    packed["row0"] += B * T


def _moe(groups, hp, l, P, W):
    n_tok = hp.shape[0]
    n_tiles = (TOP_K * n_tok) // MOE_TILE + N_EXPERTS
    pos, w8, tile_start, tile_count = route(hp, W["rw_t"], P["router_bias"], l, MOE_TILE)
    pos_flat = pos.reshape(-1)
    src = sc_invert(pos_flat, n_tok, n_tiles * MOE_TILE)
    xs = sc_gather(hp, src)
    out = moe_gemm(xs, tile_start[:, 0], tile_count[:, 0], P["exp_w_in"], P["exp_w_out"], l,
                   MOE_TILE, n_tiles)
    y8 = sc_gather(out, pos_flat).reshape(TOP_K, n_tok, -1)
    w_t = w8.T
    row0 = 0
    for st in groups:
        B, T, _ = st["x"].shape
        if l == P["norm1_g"].shape[0] - 1:
            st["x"] = moe_combine(y8, w_t, hp, P["sh_w_in"], P["sh_w_out"], st["x"], st["mod"][l], 5, l, row0,
                                  final_g=P["final_g"])
        else:
            nxt = (P["norm1_g"][l + 1], st["mod"][l + 1], 1, 0)
            st["x"], st["h_next"] = moe_combine(y8, w_t, hp, P["sh_w_in"], P["sh_w_out"], st["x"],
                                                st["mod"][l], 5, l, row0, next_norm=nxt)
        row0 += B * T


def _group_state(x, mod, pos, hg_state, past_lat, past_kr, lbs):
    return dict(x=x, mod=mod, hg_state=hg_state, past_lat=past_lat, past_kr=past_kr, lbs=lbs,
                **_rope_tables(pos), hg_new=[],
                lat=None, kr=None, k_all=None, v_all=None)


def kernel(x_prompt, x_sample, state_hgrn, cache_mla_latent, cache_mla_krope, c_prompt, c_sample, ada_w, ada_b, norm1_g, norm2_g, hg_w_in, hg_lb_logits, hg_onorm_g, hg_w_out, kv_in_g, w_dkv, kv_lat_g, w_uk, w_uv, w_dq, q_norm_g, w_uq, w_o, router_w, router_bias, exp_w_in, exp_w_out, sh_w_in, sh_w_out, final_g):
    Bp, Sp, _ = x_prompt.shape
    Bs, Ss, _ = x_sample.shape
    past = cache_mla_latent.shape[1]
    P = dict(norm1_g=norm1_g, norm2_g=norm2_g, hg_w_in=hg_w_in, hg_lb_logits=hg_lb_logits,
             hg_onorm_g=hg_onorm_g, hg_w_out=hg_w_out, kv_in_g=kv_in_g, kv_lat_g=kv_lat_g,
             w_dq=w_dq, q_norm_g=q_norm_g, w_o=w_o, router_bias=router_bias,
             exp_w_in=exp_w_in, exp_w_out=exp_w_out, sh_w_in=sh_w_in, sh_w_out=sh_w_out, final_g=final_g)
    W = _prep_weights(w_dkv, w_uk, w_uv, w_uq, router_w)
    mod = ada_mod(jnp.concatenate([c_prompt, c_sample], axis=0), ada_w, ada_b)
    lbs = jnp.cumsum(jax.nn.softmax(hg_lb_logits.astype(F32), axis=0), axis=0)
    bsz = Bp // PROMPT_STREAMS
    prompts = [_group_state(x_prompt[i * bsz:(i + 1) * bsz], mod[:, i * bsz:(i + 1) * bsz, None, :],
                            np.arange(Sp), None, None, None, lbs) for i in range(PROMPT_STREAMS)]
    gs = _group_state(x_sample, mod[:, Bp:, None, :], past + np.arange(Ss), state_hgrn,
                      cache_mla_latent, jnp.transpose(cache_mla_krope, (0, 2, 1)), lbs)
    streams = [[prompts[0], gs]] + [[g] for g in prompts[1:]]
    n_a = hg_w_in.shape[0]
    for l in range(norm1_g.shape[0]):
        for groups in streams:
            packed = dict(total=sum(st["x"].shape[0] * st["x"].shape[1] for st in groups), row0=0, buf=None)
            for st in groups:
                _mixer(st, l, P, W, packed)
            _moe(groups, packed["buf"], l, P, W)
            if l == n_a - 1:
                for st in groups:
                    st["lat"], st["kr"] = shared_kv(st["x"], kv_in_g, W["w_kv"], kv_lat_g, st["cos32"], st["sin32"])
                    if st["past_lat"] is None:
                        st["k_all"], st["v_all"] = kv_expand(st["lat"], st["kr"], W["wk_pad"], W["ek"],
                                                             W["wvt_ext"], W["ones_col"])
    for st in prompts + [gs]:
        st["y"] = st["x"]
        st["hg_out"] = jnp.stack(st["hg_new"], axis=0)

    def cat(key, axis=0):
        return jnp.concatenate([g[key] for g in prompts], axis=axis)

    return (cat("y"), gs["y"], cat("hg_out", 1), gs["hg_out"], cat("lat"), cat("kr"), gs["lat"], gs["kr"])
```

```python
import dataclasses
import functools

import numpy as np
import jax
import jax.numpy as jnp
from jax import lax
from jax.experimental import pallas as pl
from jax.experimental.pallas import tpu as pltpu
from jax.experimental.pallas import tpu_sc as plsc

F32 = jnp.float32
BF16 = jnp.bfloat16

D_MODEL = 1024
CHUNK = 64
HG_HEADS = 8
HG_DK = 128
HG_DV = 128
MLA_HEADS = 16
MLA_NOPE = 64
MLA_ROPE = 32
MLA_V = 64
Q_LORA = 384
KV_LORA = 256
ROPE_THETA = 10000.0
N_EXPERTS = 64
TOP_K = 8
N_GROUPS = 8
TOPK_GROUPS = 4
EXPERT_FF = 256
SHARED_FF = 256
ROUTED_SCALE = 2.5
EPS = 1e-6

HEAD_PAD = 128
SAMPLE_KEY_SUB = 8
ATTN_LOOKAHEAD = 6
V_ROWS = MLA_V + 16
QK_SCALE = (MLA_NOPE + MLA_ROPE) ** -0.5
Q_PRESCALE = QK_SCALE * float(np.log2(np.e))
VMEM_LIMIT = 56 * 1024 * 1024
NEG_INF = float("-inf")
SC_CORES = 2
SC_SUBCORES = 16
SC_WORKERS = SC_CORES * SC_SUBCORES
SC_LANES = 16
SC_WINDOW = 64
MOE_TILE = 256
PROMPT_STREAMS = 1
MOE_NBUF = 8
MOE_SUB = 1


def _cparams(n_axes):
    return pltpu.CompilerParams(dimension_semantics=("arbitrary",) * n_axes,
                                vmem_limit_bytes=VMEM_LIMIT)


def _silu(x):
    return x * jax.nn.sigmoid(x)


def _rms(x, g):
    ms = jnp.mean(x * x, axis=-1, keepdims=True)
    return x * lax.rsqrt(ms + EPS) * g


def _dot(a, b):
    return jnp.dot(a, b, preferred_element_type=F32)


def _dot_nt(a, b):
    return lax.dot_general(a, b, (((1,), (1,)), ((), ())), preferred_element_type=F32)


def _dot_tn(a, b):
    return lax.dot_general(a, b, (((0,), (0,)), ((), ())), preferred_element_type=F32)


def _row_blocks(B, T, rows):
    if T >= rows:
        assert T % rows == 0
        bb, tt = 1, rows
    else:
        assert rows % T == 0 and B % (rows // T) == 0
        bb, tt = rows // T, T
    nt = T // tt
    return bb, tt, (B // bb) * nt, (lambda i: (i // nt, i % nt))


def _ada_kernel(c_ref, w_ref, b_ref, o_ref):
    a = _silu(c_ref[...]).astype(BF16)
    o_ref[...] = _dot(a, w_ref[...].astype(BF16)) + b_ref[...]


def ada_mod(c, ada_w, ada_b):
    R, D = c.shape
    L, _, N = ada_w.shape
    tn = 1536
    return pl.pallas_call(
        _ada_kernel,
        grid=(L, N // tn),
        in_specs=[pl.BlockSpec((R, D), lambda l, j: (0, 0)),
                  pl.BlockSpec((None, D, tn), lambda l, j: (l, 0, j)),
                  pl.BlockSpec((None, 1, tn), lambda l, j: (l, 0, j))],
        out_specs=pl.BlockSpec((None, R, tn), lambda l, j: (l, 0, j)),
        out_shape=jax.ShapeDtypeStruct((L, R, N), F32),
        compiler_params=_cparams(2),
        name="ada_mod",
    )(c, ada_w, ada_b.reshape(L, 1, N))


def _pack_pairs(y):
    half = y.shape[-1] // 2
    bits = lax.bitcast_convert_type(y.astype(BF16).astype(F32), jnp.uint32)
    word = lax.shift_right_logical(bits[:, :half], jnp.uint32(16)) | bits[:, half:]
    return lax.bitcast_convert_type(word, jnp.int32)


def _unpack_pairs(word, dtype=BF16):
    u = lax.bitcast_convert_type(word, jnp.uint32)
    lo = lax.bitcast_convert_type(lax.shift_left(u, jnp.uint32(16)), F32)
    hi = lax.bitcast_convert_type(u & jnp.uint32(0xFFFF0000), F32)
    return lo.astype(dtype), hi.astype(dtype)


def _norm_kernel(*refs, modulated, packed):
    if modulated:
        x_ref, g_ref, sc_ref, sh_ref, o_ref = refs
    else:
        x_ref, g_ref, o_ref = refs
    y = _rms(x_ref[...], g_ref[...])
    if modulated:
        y = y * (1.0 + sc_ref[...]) + sh_ref[...]
    if packed:
        bb, tt, D = y.shape
        o_ref[...] = _pack_pairs(y.reshape(bb * tt, D))
    else:
        o_ref[...] = y.astype(o_ref.dtype)


def norm_mod(x, g, mod=None, sc_idx=0, sh_idx=0, out_dtype=BF16, rows=512, packed=False):
    B, T, D = x.shape
    bb, tt, nblk, ij = _row_blocks(B, T, rows)
    xspec = pl.BlockSpec((bb, tt, D), lambda i: ij(i) + (0,))
    in_specs = [xspec, pl.BlockSpec((1, D), lambda i: (0, 0))]
    args = [x, g.reshape(1, D)]
    if mod is not None:
        in_specs += [pl.BlockSpec((bb, 1, D), lambda i: (ij(i)[0], 0, sc_idx)),
                     pl.BlockSpec((bb, 1, D), lambda i: (ij(i)[0], 0, sh_idx))]
        args += [mod, mod]
    if packed:
        out_specs = pl.BlockSpec((bb * tt, D // 2), lambda i: (i, 0))
        out_shape = jax.ShapeDtypeStruct((B * T, D // 2), jnp.int32)
    else:
        out_specs = xspec
        out_shape = jax.ShapeDtypeStruct((B, T, D), out_dtype)
    return pl.pallas_call(
        functools.partial(_norm_kernel, modulated=mod is not None, packed=packed),
        grid=(nblk,),
        in_specs=in_specs,
        out_specs=out_specs,
        out_shape=out_shape,
        compiler_params=_cparams(1),
        name="norm_mod",
    )(*args)


def _linear_kernel(*refs, residual, norm_next, shared_rows, n_main):
    if norm_next and shared_rows:
        a_ref, w_ref, x_ref, gate_ref, ng_ref, nsc_ref, nsh_ref, _, o_ref, hp_ref, wb_ref = refs
    elif norm_next:
        a_ref, w_ref, x_ref, gate_ref, ng_ref, nsc_ref, nsh_ref, o_ref, hp_ref, wb_ref = refs
    elif residual:
        a_ref, w_ref, x_ref, gate_ref, o_ref, wb_ref = refs
    else:
        a_ref, w_ref, o_ref, wb_ref = refs

    @pl.when(pl.program_id(1) == 0)
    def _():
        wb_ref[...] = w_ref[...].astype(BF16)

    def main():
        bb, tt, K = a_ref.shape
        y = _dot(a_ref[...].reshape(bb * tt, K).astype(BF16), wb_ref[...])
        y = y.reshape(bb, tt, y.shape[-1])
        if residual:
            y = x_ref[...] + gate_ref[...] * y
        o_ref[...] = y.astype(o_ref.dtype)
        if norm_next:
            h = _rms(y, ng_ref[...]) * (1.0 + nsc_ref[...]) + nsh_ref[...]
            hp_ref[...] = _pack_pairs(h.reshape(bb * tt, h.shape[-1]))

    if n_main is None:
        main()
    else:
        pl.when(pl.program_id(1) < n_main)(main)

        @pl.when(pl.program_id(1) >= n_main)
        def _():
            hp_ref[...] = jnp.zeros(hp_ref.shape, hp_ref.dtype)


def linear(a, w, l, out_dtype, x=None, mod=None, gate_idx=0, rows=512, tn=1024, next_norm=None,
           rows_total=None, row0=0, rows_buf=None, col_blocks=None):
    B, T, K = a.shape
    _, _, N = w.shape
    tn = min(tn, N)
    if col_blocks is None:
        def wcol(j):
            return j
    else:
        first, skip_from = col_blocks[0], [c - k for k, c in enumerate(col_blocks)]
        gap_at = next((k for k, d in enumerate(skip_from) if d != first), len(col_blocks))
        assert all(d == first for d in skip_from[:gap_at]) and all(d == first + 1 for d in skip_from[gap_at:])
        N = len(col_blocks) * tn

        def wcol(j):
            return j + first + (j >= gap_at)
    bb, tt, nblk, ij0 = _row_blocks(B, T, rows)
    n_extra = 0
    if next_norm is not None and rows_buf is None and rows_total is not None:
        assert row0 == 0 and (rows_total - B * T) % (bb * tt) == 0
        n_extra = (rows_total - B * T) // (bb * tt)

    def ij(i):
        return ij0(jnp.minimum(i, nblk - 1)) if n_extra else ij0(i)

    in_specs = [pl.BlockSpec((bb, tt, K), lambda j, i: ij(i) + (0,)),
                pl.BlockSpec((None, K, tn), lambda j, i: (l, 0, wcol(j)))]
    args = [a, w]
    ospec = pl.BlockSpec((bb, tt, tn), lambda j, i: ij(i) + (j,))
    out_specs = ospec
    out_shape = jax.ShapeDtypeStruct((B, T, N), out_dtype)
    aliases = {}
    if x is not None:
        gsteps = D_MODEL // tn
        in_specs += [ospec, pl.BlockSpec((bb, 1, tn), lambda j, i: (ij(i)[0], 0, gate_idx * gsteps + j))]
        args += [x, mod]
    if next_norm is not None:
        assert x is not None and tn == N
        gain, sc_idx, sh_idx = next_norm
        in_specs += [pl.BlockSpec((1, N), lambda j, i: (0, 0)),
                     pl.BlockSpec((bb, 1, N), lambda j, i: (ij(i)[0], 0, sc_idx)),
                     pl.BlockSpec((bb, 1, N), lambda j, i: (ij(i)[0], 0, sh_idx))]
        args += [gain.reshape(1, N), mod, mod]
        assert row0 % (bb * tt) == 0
        off = row0 // (bb * tt)
        out_specs = [ospec, pl.BlockSpec((bb * tt, N // 2), lambda j, i: (off + i, 0))]
        out_shape = [out_shape, jax.ShapeDtypeStruct((rows_total or B * T, N // 2), jnp.int32)]
        if rows_buf is not None:
            in_specs.append(pl.BlockSpec(memory_space=pl.ANY))
            args.append(rows_buf)
            aliases = {len(args) - 1: 1}
    return pl.pallas_call(
        functools.partial(_linear_kernel, residual=x is not None, norm_next=next_norm is not None,
                          shared_rows=rows_buf is not None, n_main=nblk if n_extra else None),
        grid=(N // tn, nblk + n_extra),
        in_specs=in_specs,
        out_specs=out_specs,
        out_shape=out_shape,
        scratch_shapes=[pltpu.VMEM((K, tn), BF16)],
        input_output_aliases=aliases,
        compiler_params=_cparams(2),
        name="linear",
    )(*args)


def _gla_kernel(*refs, L, n_chunks, has_init):
    if has_init:
        q_ref, f_ref, i_ref, g_ref, lb_ref, on_ref, s0_ref, o_ref, so_ref, st_ref = refs
    else:
        q_ref, f_ref, i_ref, g_ref, lb_ref, on_ref, o_ref, so_ref, st_ref = refs
    t = pl.program_id(1)
    H = st_ref.shape[0]

    @pl.when(t == 0)
    def _():
        for h in range(H):
            if has_init:
                st_ref[h] = s0_ref[0, h].T
            else:
                st_ref[h] = jnp.zeros(st_ref.shape[1:], F32)

    lb = lb_ref[...]
    onorm = on_ref[...]
    row = lax.broadcasted_iota(jnp.int32, (L, L), 0)
    col = lax.broadcasted_iota(jnp.int32, (L, L), 1)
    causal = col <= row
    tri = causal.astype(BF16)
    mid = L // 2 - 1

    def chunk(c, carry):
        rows = pl.ds(pl.multiple_of(c * L, L), L)

        def write_o(sl, o):
            o_ref[0, rows, sl] = o.astype(o_ref.dtype)

        _gla_chunk(q_ref[0, rows, :], f_ref[0, rows, :], i_ref[0, rows, :], g_ref[0, rows, :],
                   lb, onorm, tri, causal, st_ref, write_o)
        return carry

    lax.fori_loop(0, n_chunks, chunk, 0, unroll=4 if n_chunks % 4 == 0 else 1)

    @pl.when(t == pl.num_programs(1) - 1)
    def _():
        for h in range(H):
            so_ref[0, h] = st_ref[h].T


def _gla_chunk(q, f, v, g, lb, onorm, tri, causal, st_ref, write_o):
    L = q.shape[0]
    H = st_ref.shape[0]
    mid = L // 2 - 1
    q = _silu(q.astype(F32))
    fg = lb + (1.0 - lb) * jax.nn.sigmoid(f)
    k = 1.0 - fg
    v = v.astype(BF16)
    gate = _silu(g.astype(F32))
    logf = jnp.log(fg)
    hi = logf.astype(BF16)
    lo = (logf - hi.astype(F32)).astype(BF16)
    b = _dot(tri, hi) + _dot(tri, lo)
    b_mid = b[mid:mid + 1, :]
    b_last = b[L - 1:L, :]
    qa = q * jnp.exp(b - b_mid)
    kb = k * jnp.exp(b_mid - b)
    qe = (qa * jnp.exp(b_mid)).astype(BF16)
    kd = (kb * jnp.exp(b_last - b_mid)).astype(BF16)
    qa = qa.astype(BF16)
    kb = kb.astype(BF16)
    decay = jnp.exp(b_last)
    sls = [slice(h * HG_DK, (h + 1) * HG_DK) for h in range(H)]
    sts = [st_ref[h] for h in range(H)]
    scores = [_dot_nt(qa[:, sl], kb[:, sl]) for sl in sls]
    inter = [_dot_nt(qe[:, sl], st.astype(BF16)) for sl, st in zip(sls, sts)]
    outer = [_dot_tn(v[:, sl], kd[:, sl]) for sl in sls]
    intra = [_dot(jnp.where(causal, sc, 0.0).astype(BF16), v[:, sl]) for sc, sl in zip(scores, sls)]
    for h, sl in enumerate(sls):
        st_ref[h] = sts[h] * decay[:, sl] + outer[h]
        write_o(sl, _rms(inter[h] + intra[h], onorm[:, sl]) * gate[:, sl])


def gla(zqig, zf, lb, onorm_g, s0):
    B, T, D = zf.shape
    L = CHUNK if T % CHUNK == 0 else T
    tt = min(T, 512)
    n_chunks = tt // L
    H = HG_HEADS

    def zspec(part):
        return pl.BlockSpec((1, tt, D), lambda b, t: (b, t, part))

    hspec = pl.BlockSpec((1, D), lambda b, t: (0, 0))
    sspec = pl.BlockSpec((1, H, HG_DK, HG_DV), lambda b, t: (b, 0, 0, 0))
    in_specs = [zspec(0), zspec(0), zspec(1), zspec(2), hspec, hspec]
    args = [zqig, zf, zqig, zqig, lb.reshape(1, D), onorm_g.reshape(1, D)]
    if s0 is not None:
        in_specs.append(sspec)
        args.append(s0)
    return pl.pallas_call(
        functools.partial(_gla_kernel, L=L, n_chunks=n_chunks, has_init=s0 is not None),
        grid=(B, T // tt),
        in_specs=in_specs,
        out_specs=[pl.BlockSpec((1, tt, D), lambda b, t: (b, t, 0)), sspec],
        out_shape=[jax.ShapeDtypeStruct((B, T, D), BF16),
                   jax.ShapeDtypeStruct((B, H, HG_DK, HG_DV), F32)],
        scratch_shapes=[pltpu.VMEM((H, HG_DV, HG_DK), F32)],
        compiler_params=_cparams(2),
        name="gla",
    )(*args)


def _route_kernel(h_ref, rw_ref, bias_ref, pos_ref, w_ref, te_ref, nu_ref,
                  e_s, r_s, base_s, start_s, *, tile_rows):
    ph = pl.program_id(0)
    i = pl.program_id(1)
    M = h_ref.shape[0]
    half = h_ref.shape[1]
    G, E = N_GROUPS, N_EXPERTS // N_GROUPS
    e_flat = lax.broadcasted_iota(jnp.int32, (N_EXPERTS, M), 0)

    @pl.when(ph == 1)
    def _():
        @pl.when(i == 0)
        def _():
            cnt = base_s[...]
            padded = jnp.floor((cnt + (tile_rows - 1)) * (1.0 / tile_rows)) * tile_rows
            r = lax.broadcasted_iota(jnp.int32, (N_EXPERTS, N_EXPERTS), 0)
            c = lax.broadcasted_iota(jnp.int32, (N_EXPERTS, N_EXPERTS), 1)
            start = jnp.dot((c < r).astype(F32), padded, preferred_element_type=F32,
                            precision=lax.Precision.HIGHEST)
            start_s[...] = start
            te_ref[...] = (start * (1.0 / tile_rows)).astype(jnp.int32)
            nu_ref[...] = (padded * (1.0 / tile_rows)).astype(jnp.int32)

        start_col = start_s[:, :1]
        for k in range(TOP_K):
            hit = e_flat == e_s[i, k:k + 1, :]
            seg = jnp.sum(jnp.where(hit, start_col, 0.0), axis=0, keepdims=True)
            pos_ref[k:k + 1, :] = (seg + r_s[i, k:k + 1, :]).astype(jnp.int32)

    @pl.when(ph == 0)
    def _():
        _route_pass0(h_ref, rw_ref, bias_ref, w_ref, e_s, r_s, base_s, i, M, half, G, E)


def _route_pass0(h_ref, rw_ref, bias_ref, w_ref, e_s, r_s, base_s, i, M, half, G, E):
    @pl.when(i == 0)
    def _():
        base_s[...] = jnp.zeros_like(base_s)

    lo, hi = _unpack_pairs(h_ref[...])
    rw = rw_ref[...].astype(BF16)
    logits = _dot_nt(rw[:, :half], lo) + _dot_nt(rw[:, half:], hi)
    s = jax.nn.sigmoid(logits)
    sb = (s + bias_ref[...]).reshape(G, E, M)
    s = s.reshape(G, E, M)
    e_in = lax.broadcasted_iota(jnp.int32, (G, E, M), 1).astype(F32)
    g_id = lax.broadcasted_iota(jnp.int32, (G, 1, M), 0)
    e_id = lax.broadcasted_iota(jnp.int32, (G, E, M), 0).astype(F32) * E + e_in

    def all_max(a):
        return jnp.max(jnp.max(a, axis=0, keepdims=True), axis=1, keepdims=True)

    def all_min(a):
        return jnp.min(jnp.min(a, axis=0, keepdims=True), axis=1, keepdims=True)

    def all_sum(a):
        return jnp.sum(jnp.sum(a, axis=0, keepdims=True), axis=1, keepdims=True)

    m1 = jnp.max(sb, axis=1, keepdims=True)
    first = jnp.min(jnp.where(sb == m1, e_in, float(E)), axis=1, keepdims=True)
    m2 = jnp.max(jnp.where(e_in == first, NEG_INF, sb), axis=1, keepdims=True)
    gs = m1 + m2

    rank = jnp.zeros((G, 1, M), jnp.int32)
    for j in range(G):
        gj = gs[j:j + 1]
        beats = (gj > gs) | ((gj == gs) & (j < g_id))
        rank = rank + beats.astype(jnp.int32)
    gsel = rank < TOPK_GROUPS

    vals = jnp.where(gsel, sb, NEG_INF)
    selm = jnp.zeros((G, E, M), F32)
    chosen, score = [], []
    for _ in range(TOP_K):
        m = all_max(vals)
        first = all_min(jnp.where(vals == m, e_id, float(N_EXPERTS)))
        hit = e_id == first
        score.append(all_sum(jnp.where(hit, s, 0.0)))
        selm = jnp.where(hit, 1.0, selm)
        vals = jnp.where(hit, NEG_INF, vals)
        chosen.append(first)

    tot = score[0]
    for sc in score[1:]:
        tot = tot + sc
    norm = ROUTED_SCALE / tot

    selm = selm.reshape(N_EXPERTS, M)
    earlier = (lax.broadcasted_iota(jnp.int32, (M, M), 0)
               < lax.broadcasted_iota(jnp.int32, (M, M), 1)).astype(BF16)
    rank = (base_s[:, :1] + _dot(selm.astype(BF16), earlier)).reshape(G, E, M)
    base_s[...] = base_s[...] + jnp.sum(selm, axis=1, keepdims=True)
    for k in range(TOP_K):
        hit = e_id == chosen[k]
        e_s[i, k:k + 1, :] = chosen[k].reshape(1, M).astype(jnp.int32)
        r_s[i, k:k + 1, :] = all_sum(jnp.where(hit, rank, 0.0)).reshape(1, M)
        w_ref[k:k + 1, :] = (score[k] * norm).reshape(1, M)


def route(hp, router_w_t, router_bias, l, tile_rows, rows=512):
    N, half = hp.shape
    M = rows
    nT = N // M
    assert N % M == 0

    def p0(ph, i):
        return i * (1 - ph) + (nT - 1) * ph

    return pl.pallas_call(
        functools.partial(_route_kernel, tile_rows=tile_rows),
        grid=(2, nT),
        in_specs=[pl.BlockSpec((M, half), lambda ph, i: (p0(ph, i), 0)),
                  pl.BlockSpec((None, N_EXPERTS, 2 * half), lambda ph, i: (l, 0, 0)),
                  pl.BlockSpec((None, N_EXPERTS, 1), lambda ph, i: (l, 0, 0))],
        out_specs=[pl.BlockSpec((TOP_K, M), lambda ph, i: (0, i * ph)),
                   pl.BlockSpec((TOP_K, M), lambda ph, i: (0, p0(ph, i))),
                   pl.BlockSpec((N_EXPERTS, 128), lambda ph, i: (0, 0)),
                   pl.BlockSpec((N_EXPERTS, 128), lambda ph, i: (0, 0))],
        out_shape=[jax.ShapeDtypeStruct((TOP_K, N), jnp.int32),
                   jax.ShapeDtypeStruct((TOP_K, N), F32),
                   jax.ShapeDtypeStruct((N_EXPERTS, 128), jnp.int32),
                   jax.ShapeDtypeStruct((N_EXPERTS, 128), jnp.int32)],
        scratch_shapes=[pltpu.VMEM((nT, TOP_K, M), jnp.int32), pltpu.VMEM((nT, TOP_K, M), F32),
                        pltpu.VMEM((N_EXPERTS, 128), F32), pltpu.VMEM((N_EXPERTS, 128), F32)],
        compiler_params=_cparams(2),
        name="route",
    )(hp, router_w_t, router_bias.reshape(-1, N_EXPERTS, 1))


def _sc_mesh():
    return plsc.VectorSubcoreMesh(core_axis_name="core", subcore_axis_name="subcore")


def sc_invert(pos_flat, n_tok, n_out):
    n = pos_flat.shape[0]
    per = n_out // SC_WORKERS
    chunk = n_tok
    assert n_out % SC_WORKERS == 0 and per % SC_LANES == 0
    assert n_tok % chunk == 0 and n % chunk == 0 and chunk % SC_LANES == 0
    cp = pltpu.CompilerParams()
    if "needs_layout_passes" in pltpu.CompilerParams.__dataclass_fields__:
        cp = dataclasses.replace(cp, needs_layout_passes=False)

    @functools.partial(
        pl.kernel, out_type=jax.ShapeDtypeStruct((n_out,), jnp.int32), mesh=_sc_mesh(),
        scratch_types=[pltpu.VMEM((chunk,), jnp.int32), pltpu.VMEM((per,), jnp.int32)],
        compiler_params=cp, name="sc_invert")
    def k(pos_hbm, src_hbm, pos_v, src_v):
        wid = lax.axis_index("subcore") * SC_CORES + lax.axis_index("core")
        lo = wid * per
        lane = lax.iota(jnp.int32, SC_LANES)

        @pl.loop(0, per, step=SC_LANES)
        def _(r):
            src_v[pl.ds(r, SC_LANES)] = lax.rem(lo + r + lane, n_tok)

        @pl.loop(0, n // chunk)
        def _(c):
            base = c * chunk
            pltpu.sync_copy(pos_hbm.at[pl.ds(base, chunk)], pos_v)
            tok0 = lax.rem(base, n_tok)

            @plsc.parallel_loop(0, chunk, step=SC_LANES, unroll=8)
            def _(r):
                p = pos_v[pl.ds(r, SC_LANES)] - lo
                mine = (p >= 0) & (p < per)
                plsc.store_scatter(src_v, [jnp.where(mine, p, 0)], tok0 + r + lane, mask=mine)

        pltpu.sync_copy(src_v, src_hbm.at[pl.ds(lo, per)])

    return k(pos_flat)


def sc_gather(x, idx):
    n = idx.shape[0]
    dim = x.shape[1]
    assert n % (SC_WINDOW * SC_WORKERS) == 0

    @functools.partial(
        pl.kernel, out_type=jax.ShapeDtypeStruct((n, dim), x.dtype), mesh=_sc_mesh(),
        scratch_types=[], name="sc_gather")
    def k(x_hbm, i_hbm, o_hbm):
        def body(i_vmem, o_vmem):
            pltpu.sync_copy(x_hbm.at[i_vmem.at[0]], o_vmem)

        pltpu.emit_pipeline(
            body, grid=(n // SC_WINDOW,),
            in_specs=[pl.BlockSpec((1, SC_WINDOW), index_map=lambda i: (i, 0))],
            out_specs=[pl.BlockSpec((SC_WINDOW, dim), index_map=lambda i: (i, 0))],
            core_axis_name=("core", "subcore"),
            dimension_semantics=(pltpu.PARALLEL,),
        )(i_hbm, o_hbm)

    return k(x, idx.reshape(n // SC_WINDOW, SC_WINDOW))


def _moe_gemm_kernel(ts_ref, tn_ref, x_hbm, wi_ref, wo_ref, o_hbm, wi_b, wo_b, xbuf, obuf, in_sem, out_sem,
                     *, tile_rows, n_tiles):
    e = pl.program_id(0)
    last = pl.num_programs(0) - 1
    t0 = ts_ref[e]
    n = tn_ref[e]
    n_used = ts_ref[last] + tn_ref[last]

    def x_copy(g, slot):
        rows = pl.ds(pl.multiple_of(g * tile_rows, tile_rows), tile_rows)
        return pltpu.make_async_copy(x_hbm.at[rows], xbuf.at[slot], in_sem.at[slot])

    def o_copy(g, slot):
        rows = pl.ds(pl.multiple_of(g * tile_rows, tile_rows), tile_rows)
        return pltpu.make_async_copy(obuf.at[slot], o_hbm.at[rows], out_sem.at[slot])

    @pl.when(e == 0)
    def _():
        for g0 in range(MOE_NBUF - 1):
            @pl.when(g0 < n_used)
            def _():
                x_copy(g0, g0).start()

    @pl.when(n > 0)
    def _():
        wi_b[...] = wi_ref[...].astype(BF16)
        wo_b[...] = wo_ref[...].astype(BF16)

    def tile(i, carry):
        g = t0 + i
        slot = lax.rem(g, MOE_NBUF)
        x_copy(g, slot).wait()
        ahead = g + (MOE_NBUF - 1)

        @pl.when(ahead < n_used)
        def _():
            x_copy(ahead, lax.rem(ahead, MOE_NBUF)).start()

        @pl.when(g >= MOE_NBUF)
        def _():
            o_copy(g - MOE_NBUF, slot).wait()

        rows = tile_rows // MOE_SUB
        half = xbuf.shape[2]
        xs = [_unpack_pairs(xbuf[slot, r * rows:(r + 1) * rows, :]) for r in range(MOE_SUB)]
        hus = [_dot(lo, wi_b[:half, :]) + _dot(hi, wi_b[half:, :]) for lo, hi in xs]
        acts = [(_silu(hu[:, :EXPERT_FF]) * hu[:, EXPERT_FF:]).astype(BF16) for hu in hus]
        outs = [_dot(act, wo_b[...]) for act in acts]
        for r, out in enumerate(outs):
            obuf[slot, r * rows:(r + 1) * rows, :] = _pack_pairs(out)
        o_copy(g, slot).start()
        return carry

    lax.fori_loop(0, n, tile, 0)

    @pl.when(e == last)
    def _():
        for back in range(MOE_NBUF, 0, -1):
            @pl.when(n_used >= back)
            def _():
                o_copy(n_used - back, lax.rem(n_used - back, MOE_NBUF)).wait()

        obuf[0] = jnp.zeros(obuf.shape[1:], obuf.dtype)

        def clear(g, carry):
            cp = o_copy(g, 0)
            cp.start()
            cp.wait()
            return carry

        lax.fori_loop(n_used, n_tiles, clear, 0)


def moe_gemm(xs, tile_start, tile_count, exp_w_in, exp_w_out, l, tile_rows, n_tiles):
    P, half = xs.shape
    D = 2 * half
    assert P == n_tiles * tile_rows
    hbm = pl.BlockSpec(memory_space=pl.ANY)
    grid_spec = pltpu.PrefetchScalarGridSpec(
        num_scalar_prefetch=2,
        grid=(N_EXPERTS,),
        in_specs=[hbm,
                  pl.BlockSpec((None, None, D, 2 * EXPERT_FF), lambda e, ts, tn: (l, e, 0, 0)),
                  pl.BlockSpec((None, None, EXPERT_FF, D), lambda e, ts, tn: (l, e, 0, 0))],
        out_specs=hbm,
        scratch_shapes=[pltpu.VMEM((D, 2 * EXPERT_FF), BF16), pltpu.VMEM((EXPERT_FF, D), BF16),
                        pltpu.VMEM((MOE_NBUF, tile_rows, half), jnp.int32),
                        pltpu.VMEM((MOE_NBUF, tile_rows, half), jnp.int32),
                        pltpu.SemaphoreType.DMA((MOE_NBUF,)), pltpu.SemaphoreType.DMA((MOE_NBUF,))],
    )
    return pl.pallas_call(
        functools.partial(_moe_gemm_kernel, tile_rows=tile_rows, n_tiles=n_tiles),
        grid_spec=grid_spec,
        out_shape=jax.ShapeDtypeStruct((P, half), jnp.int32),
        compiler_params=_cparams(1),
        name="moe_gemm",
    )(tile_start, tile_count, xs, exp_w_in, exp_w_out)


def _moe_combine_kernel(*refs, final, norm_next):
    if final:
        y_ref, w_ref, h_ref, si_ref, so_ref, x_ref, g2_ref, fg_ref, o_ref, si_b, so_b = refs
    elif norm_next:
        (y_ref, w_ref, h_ref, si_ref, so_ref, x_ref, g2_ref, ng_ref, nsc_ref, nsh_ref,
         o_ref, hn_ref, si_b, so_b) = refs
    else:
        y_ref, w_ref, h_ref, si_ref, so_ref, x_ref, g2_ref, o_ref, si_b, so_b = refs

    @pl.when(pl.program_id(0) == 0)
    def _():
        si_b[...] = si_ref[...].astype(BF16)
        so_b[...] = so_ref[...].astype(BF16)

    bb, tt, D = x_ref.shape
    half = D // 2
    w = w_ref[...]
    acc_lo = jnp.zeros((bb * tt, half), F32)
    acc_hi = jnp.zeros((bb * tt, half), F32)
    for k in range(TOP_K):
        lo, hi = _unpack_pairs(y_ref[k], F32)
        acc_lo = acc_lo + w[:, k:k + 1] * lo
        acc_hi = acc_hi + w[:, k:k + 1] * hi
    hlo, hhi = _unpack_pairs(h_ref[...])
    hu = _dot(hlo, si_b[:half, :]) + _dot(hhi, si_b[half:, :])
    act = (_silu(hu[:, :SHARED_FF]) * hu[:, SHARED_FF:]).astype(BF16)
    y = jnp.concatenate([acc_lo, acc_hi], axis=-1) + _dot(act, so_b[...])
    x_new = x_ref[...] + g2_ref[...] * y.reshape(bb, tt, D)
    o_ref[...] = _rms(x_new, fg_ref[...]) if final else x_new
    if norm_next:
        hn = _rms(x_new, ng_ref[...]) * (1.0 + nsc_ref[...]) + nsh_ref[...]
        hn_ref[...] = hn.astype(hn_ref.dtype)


def moe_combine(y8, w_t, hp, sh_w_in, sh_w_out, x, mod, gate_idx, l, row0, final_g=None, next_norm=None,
                rows=256):
    B, T, D = x.shape
    half = D // 2
    bb, tt, nblk, ij = _row_blocks(B, T, rows)
    M = bb * tt
    assert row0 % M == 0
    off = row0 // M
    xspec = pl.BlockSpec((bb, tt, D), lambda i: ij(i) + (0,))
    in_specs = [pl.BlockSpec((TOP_K, M, half), lambda i: (0, off + i, 0)),
                pl.BlockSpec((M, TOP_K), lambda i: (off + i, 0)),
                pl.BlockSpec((M, half), lambda i: (off + i, 0)),
                pl.BlockSpec((None, D, 2 * SHARED_FF), lambda i: (l, 0, 0)),
                pl.BlockSpec((None, SHARED_FF, D), lambda i: (l, 0, 0)),
                xspec,
                pl.BlockSpec((bb, 1, D), lambda i: (ij(i)[0], 0, gate_idx))]
    args = [y8, w_t, hp, sh_w_in, sh_w_out, x, mod]
    out_specs = xspec
    out_shape = jax.ShapeDtypeStruct((B, T, D), F32)
    if final_g is not None:
        assert next_norm is None
        in_specs.append(pl.BlockSpec((1, D), lambda i: (0, 0)))
        args.append(final_g.reshape(1, D))
    if next_norm is not None:
        gain, mod_next, sc_idx, sh_idx = next_norm
        in_specs += [pl.BlockSpec((1, D), lambda i: (0, 0)),
                     pl.BlockSpec((bb, 1, D), lambda i: (ij(i)[0], 0, sc_idx)),
                     pl.BlockSpec((bb, 1, D), lambda i: (ij(i)[0], 0, sh_idx))]
        args += [gain.reshape(1, D), mod_next, mod_next]
        out_specs = [xspec, xspec]
        out_shape = [out_shape, jax.ShapeDtypeStruct((B, T, D), BF16)]
    return pl.pallas_call(
        functools.partial(_moe_combine_kernel, final=final_g is not None, norm_next=next_norm is not None),
        grid=(nblk,),
        in_specs=in_specs,
        out_specs=out_specs,
        out_shape=out_shape,
        scratch_shapes=[pltpu.VMEM((D, 2 * SHARED_FF), BF16), pltpu.VMEM((SHARED_FF, D), BF16)],
        compiler_params=_cparams(1),
        name="moe_combine",
    )(*args)


def _shared_kv_kernel(x_ref, g_ref, w_ref, lg_ref, cos_ref, sin_ref, lat_ref, kr_ref):
    bb, tt, D = x_ref.shape
    xn = _rms(x_ref[...], g_ref[...]).reshape(bb * tt, D).astype(BF16)
    z = _dot(xn, w_ref[...].astype(BF16))
    lat = _rms(z[:, :KV_LORA], lg_ref[...])
    lat_ref[...] = lat.reshape(bb, tt, KV_LORA)
    zr = z[:, KV_LORA:KV_LORA + MLA_ROPE].reshape(bb, tt, MLA_ROPE)
    zq = z[:, KV_LORA + 128:KV_LORA + 128 + MLA_ROPE].reshape(bb, tt, MLA_ROPE)
    kr_ref[...] = zr * cos_ref[...] + zq * sin_ref[...]


def shared_kv(x, kv_in_g, w_kv, kv_lat_g, cos32, sin32, rows=512):
    B, T, D = x.shape
    bb, tt, nblk, ij = _row_blocks(B, T, rows)
    tspec = pl.BlockSpec((tt, MLA_ROPE), lambda i: (ij(i)[1], 0))
    return pl.pallas_call(
        _shared_kv_kernel,
        grid=(nblk,),
        in_specs=[pl.BlockSpec((bb, tt, D), lambda i: ij(i) + (0,)),
                  pl.BlockSpec((1, D), lambda i: (0, 0)),
                  pl.BlockSpec(w_kv.shape, lambda i: (0, 0)),
                  pl.BlockSpec((1, KV_LORA), lambda i: (0, 0)),
                  tspec, tspec],
        out_specs=[pl.BlockSpec((bb, tt, KV_LORA), lambda i: ij(i) + (0,)),
                   pl.BlockSpec((bb, tt, MLA_ROPE), lambda i: ij(i) + (0,))],
        out_shape=[jax.ShapeDtypeStruct((B, T, KV_LORA), F32),
                   jax.ShapeDtypeStruct((B, T, MLA_ROPE), F32)],
        compiler_params=_cparams(1),
        name="shared_kv",
    )(x, kv_in_g.reshape(1, D), w_kv, kv_lat_g.reshape(1, KV_LORA), cos32, sin32)


def _kv_expand_kernel(lat_ref, kr_ref, wk_ref, ek_ref, wvt_ref, ones_ref, k_ref, vt_ref):
    lat = lat_ref[0].astype(BF16)
    kr = kr_ref[0].astype(BF16)
    k = _dot(lat, wk_ref[...].astype(BF16)) + _dot(kr, ek_ref[...].astype(BF16))
    k_ref[0] = k.astype(k_ref.dtype)
    vt = _dot_nt(wvt_ref[...].astype(BF16), lat) + ones_ref[...]
    vt_ref[0] = vt.astype(vt_ref.dtype)


def kv_expand(lat, kr, wk_pad, ek, wvt_ext, ones_col, rows=512):
    B, T, _ = lat.shape
    tt = rows
    NK, NVT = wk_pad.shape[1], wvt_ext.shape[0]

    def full(a):
        return pl.BlockSpec(a.shape, lambda b, t: (0, 0))

    def rowspec(n):
        return pl.BlockSpec((1, tt, n), lambda b, t: (b, t, 0))

    return pl.pallas_call(
        _kv_expand_kernel,
        grid=(B, T // tt),
        in_specs=[rowspec(KV_LORA), rowspec(MLA_ROPE), full(wk_pad), full(ek), full(wvt_ext), full(ones_col)],
        out_specs=[rowspec(NK), pl.BlockSpec((1, NVT, tt), lambda b, t: (b, 0, t))],
        out_shape=[jax.ShapeDtypeStruct((B, T, NK), BF16), jax.ShapeDtypeStruct((B, NVT, T), BF16)],
        compiler_params=_cparams(2),
        name="kv_expand",
    )(lat, kr, wk_pad, ek, wvt_ext, ones_col)


def _query_kernel(h_ref, wdq_ref, qg_ref, wq_ref, wqr_ref, c_ref, s_ref, q_ref, wdq_b, wq_b, wqr_b):
    @pl.when(pl.program_id(0) == 0)
    def _():
        wdq_b[...] = wdq_ref[...].astype(BF16)
        wq_b[...] = wq_ref[...].astype(BF16)
        wqr_b[...] = wqr_ref[...].astype(BF16)

    bb, tt, D = h_ref.shape
    h = h_ref[...].reshape(bb * tt, D)
    cq = _rms(_dot(h, wdq_b[...]), qg_ref[...]).astype(BF16)
    q1 = _dot(cq, wq_b[...]).reshape(bb, tt, -1)
    q2 = _dot(cq, wqr_b[...]).reshape(bb, tt, -1)
    c = c_ref[...]
    s = s_ref[...]
    for hd in range(MLA_HEADS):
        sl = slice(hd * HEAD_PAD, (hd + 1) * HEAD_PAD)
        q_ref[:, :, sl] = (q1[:, :, sl] * c + q2[:, :, sl] * s).astype(q_ref.dtype)


def mla_queries(h, w_dq, q_norm_g, wq_pad, wq_rot, l, c128, s128, rows=512):
    B, T, D = h.shape
    bb, tt, nblk, ij = _row_blocks(B, T, rows)
    NQ = wq_pad.shape[-1]
    tspec = pl.BlockSpec((tt, HEAD_PAD), lambda i: (ij(i)[1], 0))
    return pl.pallas_call(
        _query_kernel,
        grid=(nblk,),
        in_specs=[pl.BlockSpec((bb, tt, D), lambda i: ij(i) + (0,)),
                  pl.BlockSpec((None, D, Q_LORA), lambda i: (l, 0, 0)),
                  pl.BlockSpec((None, 1, Q_LORA), lambda i: (l, 0, 0)),
                  pl.BlockSpec((None, Q_LORA, NQ), lambda i: (l, 0, 0)),
                  pl.BlockSpec((None, Q_LORA, NQ), lambda i: (l, 0, 0)),
                  tspec, tspec],
        out_specs=pl.BlockSpec((bb, tt, NQ), lambda i: ij(i) + (0,)),
        out_shape=jax.ShapeDtypeStruct((B, T, NQ), BF16),
        scratch_shapes=[pltpu.VMEM((D, Q_LORA), BF16), pltpu.VMEM((Q_LORA, NQ), BF16),
                        pltpu.VMEM((Q_LORA, NQ), BF16)],
        compiler_params=_cparams(1),
        name="mla_queries",
    )(h, w_dq, q_norm_g.reshape(-1, 1, Q_LORA), wq_pad, wq_rot, c128, s128)


def _query_t_kernel(h_ref, wdq_ref, qg_ref, wqt_ref, wqrt_ref, cos_ref, sin_ref, qt_ref, wdq_b, wqt_b, wqrt_b):
    @pl.when((pl.program_id(0) == 0) & (pl.program_id(1) == 0))
    def _():
        wdq_b[...] = wdq_ref[...].astype(BF16)
        wqt_b[...] = wqt_ref[...].astype(BF16)
        wqrt_b[...] = wqrt_ref[...].astype(BF16)

    cq = _rms(_dot(h_ref[0], wdq_b[...]), qg_ref[...]).astype(BF16)
    q1 = _dot_nt(wqt_b[...], cq)
    q2 = _dot_nt(wqrt_b[...], cq)
    cos = cos_ref[...]
    sin = sin_ref[...]
    pad = jnp.zeros((HEAD_PAD - MLA_NOPE - MLA_ROPE, q1.shape[1]), qt_ref.dtype)
    for hd in range(MLA_HEADS):
        r0 = hd * HEAD_PAD
        rope = (q1[r0 + MLA_NOPE:r0 + MLA_NOPE + MLA_ROPE] * cos
                + q2[hd * MLA_ROPE:(hd + 1) * MLA_ROPE] * sin)
        qt_ref[0, r0:r0 + MLA_NOPE, :] = (q1[r0:r0 + MLA_NOPE] * Q_PRESCALE).astype(qt_ref.dtype)
        qt_ref[0, r0 + MLA_NOPE:r0 + MLA_NOPE + MLA_ROPE, :] = rope.astype(qt_ref.dtype)
        qt_ref[0, r0 + MLA_NOPE + MLA_ROPE:r0 + HEAD_PAD, :] = pad


def mla_queries_t(h, w_dq, q_norm_g, wq_t, wqr_t, l, cos_t, sin_t, rows=512):
    B, T, D = h.shape
    tt = rows
    NQ = wq_t.shape[1]
    NR = wqr_t.shape[1]
    tspec = pl.BlockSpec((MLA_ROPE, tt), lambda b, t: (0, t))
    return pl.pallas_call(
        _query_t_kernel,
        grid=(B, T // tt),
        in_specs=[pl.BlockSpec((1, tt, D), lambda b, t: (b, t, 0)),
                  pl.BlockSpec((None, D, Q_LORA), lambda b, t: (l, 0, 0)),
                  pl.BlockSpec((None, 1, Q_LORA), lambda b, t: (l, 0, 0)),
                  pl.BlockSpec((None, NQ, Q_LORA), lambda b, t: (l, 0, 0)),
                  pl.BlockSpec((None, NR, Q_LORA), lambda b, t: (l, 0, 0)),
                  tspec, tspec],
        out_specs=pl.BlockSpec((1, NQ, tt), lambda b, t: (b, 0, t)),
        out_shape=jax.ShapeDtypeStruct((B, NQ, T), BF16),
        scratch_shapes=[pltpu.VMEM((D, Q_LORA), BF16), pltpu.VMEM((NQ, Q_LORA), BF16),
                        pltpu.VMEM((NR, Q_LORA), BF16)],
        compiler_params=_cparams(2),
        name="mla_queries_t",
    )(h, w_dq, q_norm_g.reshape(-1, 1, Q_LORA), wq_t, wqr_t, cos_t, sin_t)


def _attn_prompt_kernel(qi_tab, ki_tab, qt_ref, k_ref, vt_ref, o_ref, *scratch, tq, tk):
    H = MLA_HEADS
    m_refs, l_refs, acc_refs = scratch[:H], scratch[H:2 * H], scratch[2 * H:]
    p_id = pl.program_id(1)
    qi = qi_tab[p_id]
    ki = ki_tab[p_id]

    @pl.when(ki == 0)
    def _():
        for hd in range(H):
            m_refs[hd][...] = jnp.full(m_refs[hd].shape, NEG_INF, F32)
            l_refs[hd][...] = jnp.zeros(l_refs[hd].shape, F32)
            acc_refs[hd][...] = jnp.zeros(acc_refs[hd].shape, F32)

    def block(masked):
        if masked:
            kchunk = (ki * tk + lax.broadcasted_iota(jnp.int32, (tk, tq), 0)) // CHUNK
            qchunk = (qi * tq + lax.broadcasted_iota(jnp.int32, (tk, tq), 1)) // CHUNK
            mask = kchunk <= qchunk
        def scores(hd):
            sl = slice(hd * HEAD_PAD, (hd + 1) * HEAD_PAD)
            return _dot(k_ref[0, :, sl], qt_ref[0, sl, :])

        pending = [scores(hd) for hd in range(ATTN_LOOKAHEAD)]
        for hd in range(H):
            if hd + ATTN_LOOKAHEAD < H:
                pending.append(scores(hd + ATTN_LOOKAHEAD))
            s = pending.pop(0)
            if masked:
                s = jnp.where(mask, s, NEG_INF)
            m_prev = m_refs[hd][...]
            m_new = jnp.maximum(m_prev, jnp.max(s, axis=0, keepdims=True))
            a = jnp.exp2(m_prev - m_new)
            p = jnp.exp2(s - m_new).astype(BF16)
            pv = _dot(vt_ref[0, hd * V_ROWS:(hd + 1) * V_ROWS, :], p)
            acc_refs[hd][...] = a * acc_refs[hd][...] + pv[:MLA_V]
            l_refs[hd][...] = a * l_refs[hd][...] + pv[MLA_V:MLA_V + 1]
            m_refs[hd][...] = m_new

    @pl.when(ki < qi)
    def _():
        block(False)

    @pl.when(ki == qi)
    def _():
        block(True)
        o_t = jnp.concatenate([acc_refs[hd][...] / l_refs[hd][...] for hd in range(H)], axis=0)
        o_ref[0] = o_t.T.astype(o_ref.dtype)


def attn_prompt(qt, k, vt, tq=256):
    B, NQ, T = qt.shape
    NVT = vt.shape[1]
    NV = MLA_HEADS * MLA_V
    tk = tq
    assert tq % CHUNK == 0
    nq = T // tq
    pairs = [(a, b) for a in range(nq) for b in range(a + 1)]
    qi_tab = jnp.asarray([a for a, _ in pairs], jnp.int32)
    ki_tab = jnp.asarray([b for _, b in pairs], jnp.int32)
    grid_spec = pltpu.PrefetchScalarGridSpec(
        num_scalar_prefetch=2,
        grid=(B, len(pairs)),
        in_specs=[pl.BlockSpec((1, NQ, tq), lambda b, p, qt, kt: (b, 0, qt[p])),
                  pl.BlockSpec((1, tk, NQ), lambda b, p, qt, kt: (b, kt[p], 0)),
                  pl.BlockSpec((1, NVT, tk), lambda b, p, qt, kt: (b, 0, kt[p]))],
        out_specs=pl.BlockSpec((1, tq, NV), lambda b, p, qt, kt: (b, qt[p], 0)),
        scratch_shapes=([pltpu.VMEM((1, tq), F32)] * (2 * MLA_HEADS)
                        + [pltpu.VMEM((MLA_V, tq), F32)] * MLA_HEADS),
    )
    return pl.pallas_call(
        functools.partial(_attn_prompt_kernel, tq=tq, tk=tk),
        grid_spec=grid_spec,
        out_shape=jax.ShapeDtypeStruct((B, T, NV), BF16),
        compiler_params=_cparams(2),
        name="attn_prompt",
    )(qi_tab, ki_tab, qt, k, vt)


def _absorb_kernel(q_ref, m_ref, o_ref):
    o_ref[...] = _dot(q_ref[...], m_ref[...].astype(BF16)).astype(o_ref.dtype)


def absorb_queries(q2d, m_abs):
    N = q2d.shape[0]
    H, _, W = m_abs.shape
    return pl.pallas_call(
        _absorb_kernel,
        grid=(H,),
        in_specs=[pl.BlockSpec((N, HEAD_PAD), lambda h: (0, h)),
                  pl.BlockSpec((None, HEAD_PAD, W), lambda h: (h, 0, 0))],
        out_specs=pl.BlockSpec((None, N, W), lambda h: (h, 0, 0)),
        out_shape=jax.ShapeDtypeStruct((H, N, W), BF16),
        compiler_params=_cparams(1),
        name="absorb_queries",
    )(q2d, m_abs)


def _attn_sample_kernel(q_ref, lat_ref, kr_ref, nlat_ref, nkr_ref, o_ref, m_ref, l_ref, acc_ref):
    kb = pl.program_id(1)
    H, Q, W = q_ref.shape
    q = q_ref[...].reshape(H * Q, W)
    q_lat = q[:, :KV_LORA]
    q_rope = q[:, KV_LORA:KV_LORA + MLA_ROPE]

    def update(lat_tile, kr_tile, n_sub, kr_transposed):
        sub = lat_tile.shape[0] // n_sub
        lats = [lat_tile[j * sub:(j + 1) * sub, :].astype(BF16) for j in range(n_sub)]
        if kr_transposed:
            krs = [kr_tile[:, j * sub:(j + 1) * sub].astype(BF16) for j in range(n_sub)]
            ss = [_dot_nt(q_lat, lat) + _dot(q_rope, kr) for lat, kr in zip(lats, krs)]
        else:
            krs = [kr_tile[j * sub:(j + 1) * sub, :].astype(BF16) for j in range(n_sub)]
            ss = [_dot_nt(q_lat, lat) + _dot_nt(q_rope, kr) for lat, kr in zip(lats, krs)]
        m_prev = m_ref[...]
        m_new = m_prev
        for s in ss:
            m_new = jnp.maximum(m_new, jnp.max(s, axis=-1, keepdims=True))
        a = jnp.exp2(m_prev - m_new)
        ps = [jnp.exp2(s - m_new[:, :1]) for s in ss]
        pv = _dot(ps[0].astype(BF16), lats[0])
        psum = jnp.sum(ps[0], axis=-1, keepdims=True)
        for p, lat in zip(ps[1:], lats[1:]):
            pv = pv + _dot(p.astype(BF16), lat)
            psum = psum + jnp.sum(p, axis=-1, keepdims=True)
        l_ref[...] = a * l_ref[...] + psum
        m_ref[...] = m_new
        acc_ref[...] = jnp.concatenate([a, a], axis=-1) * acc_ref[...] + pv

    @pl.when(kb == 0)
    def _():
        m_ref[...] = jnp.full_like(m_ref, NEG_INF)
        l_ref[...] = jnp.zeros_like(l_ref)
        acc_ref[...] = jnp.zeros_like(acc_ref)
        update(nlat_ref[0], nkr_ref[0], 1, False)

    update(lat_ref[0], kr_ref[0], SAMPLE_KEY_SUB, True)

    @pl.when(kb == pl.num_programs(1) - 1)
    def _():
        lsum = l_ref[...]
        o = acc_ref[...] / jnp.concatenate([lsum, lsum], axis=-1)
        o_ref[...] = o.reshape(H, Q, KV_LORA).astype(o_ref.dtype)


def attn_sample(q_abs, cache_lat, cache_kr_t, new_lat, new_kr, tk=4096):
    H, N, W = q_abs.shape
    B, P, _ = cache_lat.shape
    Q = new_lat.shape[1]
    qpos = P + np.arange(Q)
    kpos = np.arange(P + Q)
    assert bool(np.all((kpos // CHUNK)[None, :] <= (qpos // CHUNK)[:, None]))
    return pl.pallas_call(
        _attn_sample_kernel,
        grid=(B, P // tk),
        in_specs=[pl.BlockSpec((H, Q, W), lambda b, kb: (0, b, 0)),
                  pl.BlockSpec((1, tk, KV_LORA), lambda b, kb: (b, kb, 0)),
                  pl.BlockSpec((1, MLA_ROPE, tk), lambda b, kb: (b, 0, kb)),
                  pl.BlockSpec((1, Q, KV_LORA), lambda b, kb: (b, 0, 0)),
                  pl.BlockSpec((1, Q, MLA_ROPE), lambda b, kb: (b, 0, 0))],
        out_specs=pl.BlockSpec((H, Q, KV_LORA), lambda b, kb: (0, b, 0)),
        out_shape=jax.ShapeDtypeStruct((H, N, KV_LORA), BF16),
        scratch_shapes=[pltpu.VMEM((H * Q, 128), F32), pltpu.VMEM((H * Q, 128), F32),
                        pltpu.VMEM((H * Q, KV_LORA), F32)],
        compiler_params=_cparams(2),
        name="attn_sample",
    )(q_abs, cache_lat, cache_kr_t, new_lat, new_kr)


def _unabsorb_kernel(o_ref, w_ref, out_ref):
    out_ref[...] = (_dot(o_ref[0], w_ref[0].astype(BF16))
                    + _dot(o_ref[1], w_ref[1].astype(BF16))).astype(out_ref.dtype)


def unabsorb(o_lat, wuv_pad):
    H, N, R = o_lat.shape
    return pl.pallas_call(
        _unabsorb_kernel,
        grid=(H // 2,),
        in_specs=[pl.BlockSpec((2, N, R), lambda p: (p, 0, 0)),
                  pl.BlockSpec((2, R, 128), lambda p: (p, 0, 0))],
        out_specs=pl.BlockSpec((N, 128), lambda p: (0, p)),
        out_shape=jax.ShapeDtypeStruct((N, (H // 2) * 128), BF16),
        compiler_params=_cparams(1),
        name="unabsorb",
    )(o_lat, wuv_pad)


def _rope_tables(pos):
    half = MLA_ROPE // 2
    inv = 1.0 / (ROPE_THETA ** (np.arange(half, dtype=np.float64) * 2.0 / MLA_ROPE))
    ang = np.asarray(pos, np.float64)[:, None] * inv[None, :]
    cos = np.concatenate([np.cos(ang), np.cos(ang)], axis=-1)
    sin = np.concatenate([np.sin(ang), np.sin(ang)], axis=-1)
    T = cos.shape[0]
    c128 = np.zeros((T, HEAD_PAD)); s128 = np.zeros((T, HEAD_PAD))
    c128[:, :MLA_NOPE] = 1.0
    c128[:, MLA_NOPE:MLA_NOPE + MLA_ROPE] = cos
    s128[:, MLA_NOPE:MLA_NOPE + MLA_ROPE] = sin
    return dict(cos32=jnp.asarray(cos, F32), sin32=jnp.asarray(sin, F32),
                c128=jnp.asarray(c128 * Q_PRESCALE, F32), s128=jnp.asarray(s128 * Q_PRESCALE, F32),
                cos_t=jnp.asarray(cos.T * Q_PRESCALE, F32), sin_t=jnp.asarray(sin.T * Q_PRESCALE, F32))


def _rot_half_cols(w):
    half = w.shape[-1] // 2
    return jnp.concatenate([-w[..., half:], w[..., :half]], axis=-1)


def _prep_weights(w_dkv, w_uk, w_uv, w_uq, router_w):
    D = D_MODEL
    w_lat, w_rope = w_dkv[:, :KV_LORA], w_dkv[:, KV_LORA:]
    pad96 = jnp.zeros((D, 128 - MLA_ROPE), F32)
    w_kv = jnp.concatenate([w_lat, w_rope, pad96, _rot_half_cols(w_rope), pad96], axis=-1)

    zpad = HEAD_PAD - MLA_NOPE
    wk_pad = jnp.pad(w_uk, ((0, 0), (0, 0), (0, zpad))).reshape(KV_LORA, MLA_HEADS * HEAD_PAD)
    ek = jnp.zeros((MLA_ROPE, MLA_HEADS, HEAD_PAD), F32)
    ek = ek.at[:, :, MLA_NOPE:MLA_NOPE + MLA_ROPE].set(
        jnp.broadcast_to(jnp.eye(MLA_ROPE, dtype=F32)[:, None, :], (MLA_ROPE, MLA_HEADS, MLA_ROPE)))
    ek = ek.reshape(MLA_ROPE, MLA_HEADS * HEAD_PAD)
    wvt = jnp.transpose(w_uv, (1, 2, 0))
    wvt_ext = jnp.pad(wvt, ((0, 0), (0, V_ROWS - MLA_V), (0, 0))).reshape(MLA_HEADS * V_ROWS, KV_LORA)
    ones_col = jnp.tile((jnp.arange(V_ROWS) >= MLA_V).astype(F32), MLA_HEADS).reshape(-1, 1)

    nb = w_uq.shape[0]
    qn, qr = w_uq[..., :MLA_NOPE], w_uq[..., MLA_NOPE:]
    z32 = jnp.zeros(qr.shape[:-1] + (HEAD_PAD - MLA_NOPE - MLA_ROPE,), F32)
    wq_pad = jnp.concatenate([qn, qr, z32], axis=-1).reshape(nb, Q_LORA, MLA_HEADS * HEAD_PAD)
    wq_rot = jnp.concatenate([jnp.zeros_like(qn), _rot_half_cols(qr), z32], axis=-1)
    wq_rot = wq_rot.reshape(nb, Q_LORA, MLA_HEADS * HEAD_PAD)
    wq_t = jnp.transpose(wq_pad, (0, 2, 1))
    wqr_t = jnp.transpose(_rot_half_cols(qr).reshape(nb, Q_LORA, MLA_HEADS * MLA_ROPE), (0, 2, 1))

    m_abs = jnp.zeros((MLA_HEADS, HEAD_PAD, KV_LORA + 128), F32)
    m_abs = m_abs.at[:, :MLA_NOPE, :KV_LORA].set(jnp.transpose(w_uk, (1, 2, 0)))
    m_abs = m_abs.at[:, MLA_NOPE:MLA_NOPE + MLA_ROPE, KV_LORA:KV_LORA + MLA_ROPE].set(
        jnp.broadcast_to(jnp.eye(MLA_ROPE, dtype=F32), (MLA_HEADS, MLA_ROPE, MLA_ROPE)))

    wuv_h = jnp.transpose(w_uv, (1, 0, 2))
    even = jnp.pad(wuv_h, ((0, 0), (0, 0), (0, 64)))
    odd = jnp.pad(wuv_h, ((0, 0), (0, 0), (64, 0)))
    wuv_pad = jnp.where((jnp.arange(MLA_HEADS) % 2 == 0)[:, None, None], even, odd)

    rw_t = jnp.transpose(router_w, (0, 2, 1))
    return dict(w_kv=w_kv, wk_pad=wk_pad, ek=ek, wvt_ext=wvt_ext, ones_col=ones_col, wq_pad=wq_pad, wq_rot=wq_rot, wq_t=wq_t, wqr_t=wqr_t,
                m_abs=m_abs, wuv_pad=wuv_pad, rw_t=rw_t)


def _mixer(st, l, P, W, packed):
    rows_kw = dict(rows_total=packed["total"], row0=packed["row0"], rows_buf=packed["buf"])
    x, m = st["x"], st["mod"][l]
    B, T, _ = x.shape
    n_a = P["hg_w_in"].shape[0]
    h = st.pop("h_next", None)
    if h is None:
        h = norm_mod(x, P["norm1_g"][l], m, sc_idx=1, sh_idx=0)
    norm2 = (P["norm2_g"][l], 4, 3)
    if l < n_a:
        rows = min(1024, B * T)
        zf = linear(h, P["hg_w_in"], l, F32, col_blocks=(1,), rows=rows)
        zqig = linear(h, P["hg_w_in"], l, BF16, col_blocks=(0, 2, 3), rows=rows)
        s0 = None if st["hg_state"] is None else st["hg_state"][l]
        o, s_new = gla(zqig, zf, st["lbs"][l], P["hg_onorm_g"][l], s0)
        st["hg_new"].append(s_new)
        st["x"], packed["buf"] = linear(o, P["hg_w_out"], l, F32, x=x, mod=m, gate_idx=2, next_norm=norm2,
                                        **rows_kw)
    else:
        bi = l - n_a
        if st["past_lat"] is None:
            qt = mla_queries_t(h, P["w_dq"], P["q_norm_g"], W["wq_t"], W["wqr_t"], bi, st["cos_t"], st["sin_t"])
            o = attn_prompt(qt, st["k_all"], st["v_all"])
        else:
            q = mla_queries(h, P["w_dq"], P["q_norm_g"], W["wq_pad"], W["wq_rot"], bi, st["c128"], st["s128"])
            q_abs = absorb_queries(q.reshape(B * T, -1), W["m_abs"])
            o_lat = attn_sample(q_abs, st["past_lat"], st["past_kr"], st["lat"], st["kr"])
            o = unabsorb(o_lat, W["wuv_pad"]).reshape(B, T, -1)
        st["x"], packed["buf"] = linear(o, P["w_o"], bi, F32, x=x, mod=m, gate_idx=2, next_norm=norm2, **rows_kw)
    packed["row0"] += B * T


def _moe(groups, hp, l, P, W):
    n_tok = hp.shape[0]
    n_tiles = (TOP_K * n_tok) // MOE_TILE + N_EXPERTS
    pos, w8, tile_start, tile_count = route(hp, W["rw_t"], P["router_bias"], l, MOE_TILE)
    pos_flat = pos.reshape(-1)
    src = sc_invert(pos_flat, n_tok, n_tiles * MOE_TILE)
    xs = sc_gather(hp, src)
    out = moe_gemm(xs, tile_start[:, 0], tile_count[:, 0], P["exp_w_in"], P["exp_w_out"], l,
                   MOE_TILE, n_tiles)
    y8 = sc_gather(out, pos_flat).reshape(TOP_K, n_tok, -1)
    w_t = w8.T
    row0 = 0
    for st in groups:
        B, T, _ = st["x"].shape
        if l == P["norm1_g"].shape[0] - 1:
            st["x"] = moe_combine(y8, w_t, hp, P["sh_w_in"], P["sh_w_out"], st["x"], st["mod"][l], 5, l, row0,
                                  final_g=P["final_g"])
        else:
            nxt = (P["norm1_g"][l + 1], st["mod"][l + 1], 1, 0)
            st["x"], st["h_next"] = moe_combine(y8, w_t, hp, P["sh_w_in"], P["sh_w_out"], st["x"],
                                                st["mod"][l], 5, l, row0, next_norm=nxt)
        row0 += B * T


def _group_state(x, mod, pos, hg_state, past_lat, past_kr, lbs):
    return dict(x=x, mod=mod, hg_state=hg_state, past_lat=past_lat, past_kr=past_kr, lbs=lbs,
                **_rope_tables(pos), hg_new=[],
                lat=None, kr=None, k_all=None, v_all=None)


def kernel(x_prompt, x_sample, state_hgrn, cache_mla_latent, cache_mla_krope, c_prompt, c_sample, ada_w, ada_b, norm1_g, norm2_g, hg_w_in, hg_lb_logits, hg_onorm_g, hg_w_out, kv_in_g, w_dkv, kv_lat_g, w_uk, w_uv, w_dq, q_norm_g, w_uq, w_o, router_w, router_bias, exp_w_in, exp_w_out, sh_w_in, sh_w_out, final_g):
    Bp, Sp, _ = x_prompt.shape
    Bs, Ss, _ = x_sample.shape
    past = cache_mla_latent.shape[1]
    P = dict(norm1_g=norm1_g, norm2_g=norm2_g, hg_w_in=hg_w_in, hg_lb_logits=hg_lb_logits,
             hg_onorm_g=hg_onorm_g, hg_w_out=hg_w_out, kv_in_g=kv_in_g, kv_lat_g=kv_lat_g,
             w_dq=w_dq, q_norm_g=q_norm_g, w_o=w_o, router_bias=router_bias,
             exp_w_in=exp_w_in, exp_w_out=exp_w_out, sh_w_in=sh_w_in, sh_w_out=sh_w_out, final_g=final_g)
    W = _prep_weights(w_dkv, w_uk, w_uv, w_uq, router_w)
    mod = ada_mod(jnp.concatenate([c_prompt, c_sample], axis=0), ada_w, ada_b)
    lbs = jnp.cumsum(jax.nn.softmax(hg_lb_logits.astype(F32), axis=0), axis=0)
    bsz = Bp // PROMPT_STREAMS
    prompts = [_group_state(x_prompt[i * bsz:(i + 1) * bsz], mod[:, i * bsz:(i + 1) * bsz, None, :],
                            np.arange(Sp), None, None, None, lbs) for i in range(PROMPT_STREAMS)]
    gs = _group_state(x_sample, mod[:, Bp:, None, :], past + np.arange(Ss), state_hgrn,
                      cache_mla_latent, jnp.transpose(cache_mla_krope, (0, 2, 1)), lbs)
    streams = [[prompts[0], gs]] + [[g] for g in prompts[1:]]
    n_a = hg_w_in.shape[0]
    for l in range(norm1_g.shape[0]):
        for groups in streams:
            packed = dict(total=sum(st["x"].shape[0] * st["x"].shape[1] for st in groups), row0=0, buf=None)
            for st in groups:
                _mixer(st, l, P, W, packed)
            _moe(groups, packed["buf"], l, P, W)
            if l == n_a - 1:
                for st in groups:
                    st["lat"], st["kr"] = shared_kv(st["x"], kv_in_g, W["w_kv"], kv_lat_g, st["cos32"], st["sin32"])
                    if st["past_lat"] is None:
                        st["k_all"], st["v_all"] = kv_expand(st["lat"], st["kr"], W["wk_pad"], W["ek"],
                                                             W["wvt_ext"], W["ones_col"])
    for st in prompts + [gs]:
        st["y"] = st["x"]
        st["hg_out"] = jnp.stack(st["hg_new"], axis=0)

    def cat(key, axis=0):
        return jnp.concatenate([g[key] for g in prompts], axis=axis)

    return (cat("y"), gs["y"], cat("hg_out", 1), gs["hg_out"], cat("lat"), cat("kr"), gs["lat"], gs["kr"])
```

```python
import dataclasses
import functools

import numpy as np
import jax
import jax.numpy as jnp
from jax import lax
from jax.experimental import pallas as pl
from jax.experimental.pallas import tpu as pltpu
from jax.experimental.pallas import tpu_sc as plsc

F32 = jnp.float32
BF16 = jnp.bfloat16

D_MODEL = 1024
CHUNK = 64
HG_HEADS = 8
HG_DK = 128
HG_DV = 128
MLA_HEADS = 16
MLA_NOPE = 64
MLA_ROPE = 32
MLA_V = 64
Q_LORA = 384
KV_LORA = 256
ROPE_THETA = 10000.0
N_EXPERTS = 64
TOP_K = 8
N_GROUPS = 8
TOPK_GROUPS = 4
EXPERT_FF = 256
SHARED_FF = 256
ROUTED_SCALE = 2.5
EPS = 1e-6

HEAD_PAD = 128
SAMPLE_KEY_SUB = 8
ATTN_LOOKAHEAD = 6
V_ROWS = MLA_V + 16
QK_SCALE = (MLA_NOPE + MLA_ROPE) ** -0.5
Q_PRESCALE = QK_SCALE * float(np.log2(np.e))
VMEM_LIMIT = 56 * 1024 * 1024
NEG_INF = float("-inf")
SC_CORES = 2
SC_SUBCORES = 16
SC_WORKERS = SC_CORES * SC_SUBCORES
SC_LANES = 16
SC_WINDOW = 64
MOE_TILE = 512
PROMPT_STREAMS = 1
MOE_NBUF = 4
MOE_SUB = 1


def _cparams(n_axes):
    return pltpu.CompilerParams(dimension_semantics=("arbitrary",) * n_axes,
                                vmem_limit_bytes=VMEM_LIMIT)


def _silu(x):
    return x * jax.nn.sigmoid(x)


def _rms(x, g):
    ms = jnp.mean(x * x, axis=-1, keepdims=True)
    return x * lax.rsqrt(ms + EPS) * g


def _dot(a, b):
    return jnp.dot(a, b, preferred_element_type=F32)


def _dot_nt(a, b):
    return lax.dot_general(a, b, (((1,), (1,)), ((), ())), preferred_element_type=F32)


def _dot_tn(a, b):
    return lax.dot_general(a, b, (((0,), (0,)), ((), ())), preferred_element_type=F32)


def _row_blocks(B, T, rows):
    if T >= rows:
        assert T % rows == 0
        bb, tt = 1, rows
    else:
        assert rows % T == 0 and B % (rows // T) == 0
        bb, tt = rows // T, T
    nt = T // tt
    return bb, tt, (B // bb) * nt, (lambda i: (i // nt, i % nt))


def _ada_kernel(c_ref, w_ref, b_ref, o_ref):
    a = _silu(c_ref[...]).astype(BF16)
    o_ref[...] = _dot(a, w_ref[...].astype(BF16)) + b_ref[...]


def ada_mod(c, ada_w, ada_b):
    R, D = c.shape
    L, _, N = ada_w.shape
    tn = 1536
    return pl.pallas_call(
        _ada_kernel,
        grid=(L, N // tn),
        in_specs=[pl.BlockSpec((R, D), lambda l, j: (0, 0)),
                  pl.BlockSpec((None, D, tn), lambda l, j: (l, 0, j)),
                  pl.BlockSpec((None, 1, tn), lambda l, j: (l, 0, j))],
        out_specs=pl.BlockSpec((None, R, tn), lambda l, j: (l, 0, j)),
        out_shape=jax.ShapeDtypeStruct((L, R, N), F32),
        compiler_params=_cparams(2),
        name="ada_mod",
    )(c, ada_w, ada_b.reshape(L, 1, N))


def _pack_pairs(y):
    half = y.shape[-1] // 2
    bits = lax.bitcast_convert_type(y.astype(BF16).astype(F32), jnp.uint32)
    word = lax.shift_right_logical(bits[:, :half], jnp.uint32(16)) | bits[:, half:]
    return lax.bitcast_convert_type(word, jnp.int32)


def _unpack_pairs(word, dtype=BF16):
    u = lax.bitcast_convert_type(word, jnp.uint32)
    lo = lax.bitcast_convert_type(lax.shift_left(u, jnp.uint32(16)), F32)
    hi = lax.bitcast_convert_type(u & jnp.uint32(0xFFFF0000), F32)
    return lo.astype(dtype), hi.astype(dtype)


def _norm_kernel(*refs, modulated, packed):
    if modulated:
        x_ref, g_ref, sc_ref, sh_ref, o_ref = refs
    else:
        x_ref, g_ref, o_ref = refs
    y = _rms(x_ref[...], g_ref[...])
    if modulated:
        y = y * (1.0 + sc_ref[...]) + sh_ref[...]
    if packed:
        bb, tt, D = y.shape
        o_ref[...] = _pack_pairs(y.reshape(bb * tt, D))
    else:
        o_ref[...] = y.astype(o_ref.dtype)


def norm_mod(x, g, mod=None, sc_idx=0, sh_idx=0, out_dtype=BF16, rows=512, packed=False):
    B, T, D = x.shape
    bb, tt, nblk, ij = _row_blocks(B, T, rows)
    xspec = pl.BlockSpec((bb, tt, D), lambda i: ij(i) + (0,))
    in_specs = [xspec, pl.BlockSpec((1, D), lambda i: (0, 0))]
    args = [x, g.reshape(1, D)]
    if mod is not None:
        in_specs += [pl.BlockSpec((bb, 1, D), lambda i: (ij(i)[0], 0, sc_idx)),
                     pl.BlockSpec((bb, 1, D), lambda i: (ij(i)[0], 0, sh_idx))]
        args += [mod, mod]
    if packed:
        out_specs = pl.BlockSpec((bb * tt, D // 2), lambda i: (i, 0))
        out_shape = jax.ShapeDtypeStruct((B * T, D // 2), jnp.int32)
    else:
        out_specs = xspec
        out_shape = jax.ShapeDtypeStruct((B, T, D), out_dtype)
    return pl.pallas_call(
        functools.partial(_norm_kernel, modulated=mod is not None, packed=packed),
        grid=(nblk,),
        in_specs=in_specs,
        out_specs=out_specs,
        out_shape=out_shape,
        compiler_params=_cparams(1),
        name="norm_mod",
    )(*args)


def _linear_kernel(*refs, residual, norm_next, shared_rows, n_main):
    if norm_next and shared_rows:
        a_ref, w_ref, x_ref, gate_ref, ng_ref, nsc_ref, nsh_ref, _, o_ref, hp_ref, wb_ref = refs
    elif norm_next:
        a_ref, w_ref, x_ref, gate_ref, ng_ref, nsc_ref, nsh_ref, o_ref, hp_ref, wb_ref = refs
    elif residual:
        a_ref, w_ref, x_ref, gate_ref, o_ref, wb_ref = refs
    else:
        a_ref, w_ref, o_ref, wb_ref = refs

    @pl.when(pl.program_id(1) == 0)
    def _():
        wb_ref[...] = w_ref[...].astype(BF16)

    def main():
        bb, tt, K = a_ref.shape
        y = _dot(a_ref[...].reshape(bb * tt, K).astype(BF16), wb_ref[...])
        y = y.reshape(bb, tt, y.shape[-1])
        if residual:
            y = x_ref[...] + gate_ref[...] * y
        o_ref[...] = y.astype(o_ref.dtype)
        if norm_next:
            h = _rms(y, ng_ref[...]) * (1.0 + nsc_ref[...]) + nsh_ref[...]
            hp_ref[...] = _pack_pairs(h.reshape(bb * tt, h.shape[-1]))

    if n_main is None:
        main()
    else:
        pl.when(pl.program_id(1) < n_main)(main)

        @pl.when(pl.program_id(1) >= n_main)
        def _():
            hp_ref[...] = jnp.zeros(hp_ref.shape, hp_ref.dtype)


def linear(a, w, l, out_dtype, x=None, mod=None, gate_idx=0, rows=512, tn=1024, next_norm=None,
           rows_total=None, row0=0, rows_buf=None, col_blocks=None):
    B, T, K = a.shape
    _, _, N = w.shape
    tn = min(tn, N)
    if col_blocks is None:
        def wcol(j):
            return j
    else:
        first, skip_from = col_blocks[0], [c - k for k, c in enumerate(col_blocks)]
        gap_at = next((k for k, d in enumerate(skip_from) if d != first), len(col_blocks))
        assert all(d == first for d in skip_from[:gap_at]) and all(d == first + 1 for d in skip_from[gap_at:])
        N = len(col_blocks) * tn

        def wcol(j):
            return j + first + (j >= gap_at)
    bb, tt, nblk, ij0 = _row_blocks(B, T, rows)
    n_extra = 0
    if next_norm is not None and rows_buf is None and rows_total is not None:
        assert row0 == 0 and (rows_total - B * T) % (bb * tt) == 0
        n_extra = (rows_total - B * T) // (bb * tt)

    def ij(i):
        return ij0(jnp.minimum(i, nblk - 1)) if n_extra else ij0(i)

    in_specs = [pl.BlockSpec((bb, tt, K), lambda j, i: ij(i) + (0,)),
                pl.BlockSpec((None, K, tn), lambda j, i: (l, 0, wcol(j)))]
    args = [a, w]
    ospec = pl.BlockSpec((bb, tt, tn), lambda j, i: ij(i) + (j,))
    out_specs = ospec
    out_shape = jax.ShapeDtypeStruct((B, T, N), out_dtype)
    aliases = {}
    if x is not None:
        gsteps = D_MODEL // tn
        in_specs += [ospec, pl.BlockSpec((bb, 1, tn), lambda j, i: (ij(i)[0], 0, gate_idx * gsteps + j))]
        args += [x, mod]
    if next_norm is not None:
        assert x is not None and tn == N
        gain, sc_idx, sh_idx = next_norm
        in_specs += [pl.BlockSpec((1, N), lambda j, i: (0, 0)),
                     pl.BlockSpec((bb, 1, N), lambda j, i: (ij(i)[0], 0, sc_idx)),
                     pl.BlockSpec((bb, 1, N), lambda j, i: (ij(i)[0], 0, sh_idx))]
        args += [gain.reshape(1, N), mod, mod]
        assert row0 % (bb * tt) == 0
        off = row0 // (bb * tt)
        out_specs = [ospec, pl.BlockSpec((bb * tt, N // 2), lambda j, i: (off + i, 0))]
        out_shape = [out_shape, jax.ShapeDtypeStruct((rows_total or B * T, N // 2), jnp.int32)]
        if rows_buf is not None:
            in_specs.append(pl.BlockSpec(memory_space=pl.ANY))
            args.append(rows_buf)
            aliases = {len(args) - 1: 1}
    return pl.pallas_call(
        functools.partial(_linear_kernel, residual=x is not None, norm_next=next_norm is not None,
                          shared_rows=rows_buf is not None, n_main=nblk if n_extra else None),
        grid=(N // tn, nblk + n_extra),
        in_specs=in_specs,
        out_specs=out_specs,
        out_shape=out_shape,
        scratch_shapes=[pltpu.VMEM((K, tn), BF16)],
        input_output_aliases=aliases,
        compiler_params=_cparams(2),
        name="linear",
    )(*args)


def _gla_kernel(*refs, L, n_chunks, has_init):
    if has_init:
        q_ref, f_ref, i_ref, g_ref, lb_ref, on_ref, s0_ref, o_ref, so_ref, st_ref = refs
    else:
        q_ref, f_ref, i_ref, g_ref, lb_ref, on_ref, o_ref, so_ref, st_ref = refs
    t = pl.program_id(1)
    H = st_ref.shape[0]

    @pl.when(t == 0)
    def _():
        for h in range(H):
            if has_init:
                st_ref[h] = s0_ref[0, h].T
            else:
                st_ref[h] = jnp.zeros(st_ref.shape[1:], F32)

    lb = lb_ref[...]
    onorm = on_ref[...]
    row = lax.broadcasted_iota(jnp.int32, (L, L), 0)
    col = lax.broadcasted_iota(jnp.int32, (L, L), 1)
    causal = col <= row
    tri = causal.astype(BF16)
    mid = L // 2 - 1

    def chunk(c, carry):
        rows = pl.ds(pl.multiple_of(c * L, L), L)

        def write_o(sl, o):
            o_ref[0, rows, sl] = o.astype(o_ref.dtype)

        _gla_chunk(q_ref[0, rows, :], f_ref[0, rows, :], i_ref[0, rows, :], g_ref[0, rows, :],
                   lb, onorm, tri, causal, st_ref, write_o)
        return carry

    lax.fori_loop(0, n_chunks, chunk, 0, unroll=4 if n_chunks % 4 == 0 else 1)

    @pl.when(t == pl.num_programs(1) - 1)
    def _():
        for h in range(H):
            so_ref[0, h] = st_ref[h].T


def _gla_chunk(q, f, v, g, lb, onorm, tri, causal, st_ref, write_o):
    L = q.shape[0]
    H = st_ref.shape[0]
    mid = L // 2 - 1
    q = _silu(q.astype(F32))
    fg = lb + (1.0 - lb) * jax.nn.sigmoid(f)
    k = 1.0 - fg
    v = v.astype(BF16)
    gate = _silu(g.astype(F32))
    logf = jnp.log(fg)
    hi = logf.astype(BF16)
    lo = (logf - hi.astype(F32)).astype(BF16)
    b = _dot(tri, hi) + _dot(tri, lo)
    b_mid = b[mid:mid + 1, :]
    b_last = b[L - 1:L, :]
    qa = q * jnp.exp(b - b_mid)
    kb = k * jnp.exp(b_mid - b)
    qe = (qa * jnp.exp(b_mid)).astype(BF16)
    kd = (kb * jnp.exp(b_last - b_mid)).astype(BF16)
    qa = qa.astype(BF16)
    kb = kb.astype(BF16)
    decay = jnp.exp(b_last)
    sls = [slice(h * HG_DK, (h + 1) * HG_DK) for h in range(H)]
    sts = [st_ref[h] for h in range(H)]
    scores = [_dot_nt(qa[:, sl], kb[:, sl]) for sl in sls]
    inter = [_dot_nt(qe[:, sl], st.astype(BF16)) for sl, st in zip(sls, sts)]
    outer = [_dot_tn(v[:, sl], kd[:, sl]) for sl in sls]
    intra = [_dot(jnp.where(causal, sc, 0.0).astype(BF16), v[:, sl]) for sc, sl in zip(scores, sls)]
    for h, sl in enumerate(sls):
        st_ref[h] = sts[h] * decay[:, sl] + outer[h]
        write_o(sl, _rms(inter[h] + intra[h], onorm[:, sl]) * gate[:, sl])


def gla(zqig, zf, lb, onorm_g, s0):
    B, T, D = zf.shape
    L = CHUNK if T % CHUNK == 0 else T
    tt = min(T, 512)
    n_chunks = tt // L
    H = HG_HEADS

    def zspec(part):
        return pl.BlockSpec((1, tt, D), lambda b, t: (b, t, part))

    hspec = pl.BlockSpec((1, D), lambda b, t: (0, 0))
    sspec = pl.BlockSpec((1, H, HG_DK, HG_DV), lambda b, t: (b, 0, 0, 0))
    in_specs = [zspec(0), zspec(0), zspec(1), zspec(2), hspec, hspec]
    args = [zqig, zf, zqig, zqig, lb.reshape(1, D), onorm_g.reshape(1, D)]
    if s0 is not None:
        in_specs.append(sspec)
        args.append(s0)
    return pl.pallas_call(
        functools.partial(_gla_kernel, L=L, n_chunks=n_chunks, has_init=s0 is not None),
        grid=(B, T // tt),
        in_specs=in_specs,
        out_specs=[pl.BlockSpec((1, tt, D), lambda b, t: (b, t, 0)), sspec],
        out_shape=[jax.ShapeDtypeStruct((B, T, D), BF16),
                   jax.ShapeDtypeStruct((B, H, HG_DK, HG_DV), F32)],
        scratch_shapes=[pltpu.VMEM((H, HG_DV, HG_DK), F32)],
        compiler_params=_cparams(2),
        name="gla",
    )(*args)


def _route_kernel(h_ref, rw_ref, bias_ref, pos_ref, w_ref, te_ref, nu_ref,
                  e_s, r_s, base_s, start_s, *, tile_rows):
    ph = pl.program_id(0)
    i = pl.program_id(1)
    M = h_ref.shape[0]
    half = h_ref.shape[1]
    G, E = N_GROUPS, N_EXPERTS // N_GROUPS
    e_flat = lax.broadcasted_iota(jnp.int32, (N_EXPERTS, M), 0)

    @pl.when(ph == 1)
    def _():
        @pl.when(i == 0)
        def _():
            cnt = base_s[...]
            padded = jnp.floor((cnt + (tile_rows - 1)) * (1.0 / tile_rows)) * tile_rows
            r = lax.broadcasted_iota(jnp.int32, (N_EXPERTS, N_EXPERTS), 0)
            c = lax.broadcasted_iota(jnp.int32, (N_EXPERTS, N_EXPERTS), 1)
            start = jnp.dot((c < r).astype(F32), padded, preferred_element_type=F32,
                            precision=lax.Precision.HIGHEST)
            start_s[...] = start
            te_ref[...] = (start * (1.0 / tile_rows)).astype(jnp.int32)
            nu_ref[...] = (padded * (1.0 / tile_rows)).astype(jnp.int32)

        start_col = start_s[:, :1]
        for k in range(TOP_K):
            hit = e_flat == e_s[i, k:k + 1, :]
            seg = jnp.sum(jnp.where(hit, start_col, 0.0), axis=0, keepdims=True)
            pos_ref[k:k + 1, :] = (seg + r_s[i, k:k + 1, :]).astype(jnp.int32)

    @pl.when(ph == 0)
    def _():
        _route_pass0(h_ref, rw_ref, bias_ref, w_ref, e_s, r_s, base_s, i, M, half, G, E)


def _route_pass0(h_ref, rw_ref, bias_ref, w_ref, e_s, r_s, base_s, i, M, half, G, E):
    @pl.when(i == 0)
    def _():
        base_s[...] = jnp.zeros_like(base_s)

    lo, hi = _unpack_pairs(h_ref[...])
    rw = rw_ref[...].astype(BF16)
    logits = _dot_nt(rw[:, :half], lo) + _dot_nt(rw[:, half:], hi)
    s = jax.nn.sigmoid(logits)
    sb = (s + bias_ref[...]).reshape(G, E, M)
    s = s.reshape(G, E, M)
    e_in = lax.broadcasted_iota(jnp.int32, (G, E, M), 1).astype(F32)
    g_id = lax.broadcasted_iota(jnp.int32, (G, 1, M), 0)
    e_id = lax.broadcasted_iota(jnp.int32, (G, E, M), 0).astype(F32) * E + e_in

    def all_max(a):
        return jnp.max(jnp.max(a, axis=0, keepdims=True), axis=1, keepdims=True)

    def all_min(a):
        return jnp.min(jnp.min(a, axis=0, keepdims=True), axis=1, keepdims=True)

    def all_sum(a):
        return jnp.sum(jnp.sum(a, axis=0, keepdims=True), axis=1, keepdims=True)

    m1 = jnp.max(sb, axis=1, keepdims=True)
    first = jnp.min(jnp.where(sb == m1, e_in, float(E)), axis=1, keepdims=True)
    m2 = jnp.max(jnp.where(e_in == first, NEG_INF, sb), axis=1, keepdims=True)
    gs = m1 + m2

    rank = jnp.zeros((G, 1, M), jnp.int32)
    for j in range(G):
        gj = gs[j:j + 1]
        beats = (gj > gs) | ((gj == gs) & (j < g_id))
        rank = rank + beats.astype(jnp.int32)
    gsel = rank < TOPK_GROUPS

    vals = jnp.where(gsel, sb, NEG_INF)
    selm = jnp.zeros((G, E, M), F32)
    chosen, score = [], []
    for _ in range(TOP_K):
        m = all_max(vals)
        first = all_min(jnp.where(vals == m, e_id, float(N_EXPERTS)))
        hit = e_id == first
        score.append(all_sum(jnp.where(hit, s, 0.0)))
        selm = jnp.where(hit, 1.0, selm)
        vals = jnp.where(hit, NEG_INF, vals)
        chosen.append(first)

    tot = score[0]
    for sc in score[1:]:
        tot = tot + sc
    norm = ROUTED_SCALE / tot

    selm = selm.reshape(N_EXPERTS, M)
    earlier = (lax.broadcasted_iota(jnp.int32, (M, M), 0)
               < lax.broadcasted_iota(jnp.int32, (M, M), 1)).astype(BF16)
    rank = (base_s[:, :1] + _dot(selm.astype(BF16), earlier)).reshape(G, E, M)
    base_s[...] = base_s[...] + jnp.sum(selm, axis=1, keepdims=True)
    for k in range(TOP_K):
        hit = e_id == chosen[k]
        e_s[i, k:k + 1, :] = chosen[k].reshape(1, M).astype(jnp.int32)
        r_s[i, k:k + 1, :] = all_sum(jnp.where(hit, rank, 0.0)).reshape(1, M)
        w_ref[k:k + 1, :] = (score[k] * norm).reshape(1, M)


def route(hp, router_w_t, router_bias, l, tile_rows, rows=512):
    N, half = hp.shape
    M = rows
    nT = N // M
    assert N % M == 0

    def p0(ph, i):
        return i * (1 - ph) + (nT - 1) * ph

    return pl.pallas_call(
        functools.partial(_route_kernel, tile_rows=tile_rows),
        grid=(2, nT),
        in_specs=[pl.BlockSpec((M, half), lambda ph, i: (p0(ph, i), 0)),
                  pl.BlockSpec((None, N_EXPERTS, 2 * half), lambda ph, i: (l, 0, 0)),
                  pl.BlockSpec((None, N_EXPERTS, 1), lambda ph, i: (l, 0, 0))],
        out_specs=[pl.BlockSpec((TOP_K, M), lambda ph, i: (0, i * ph)),
                   pl.BlockSpec((TOP_K, M), lambda ph, i: (0, p0(ph, i))),
                   pl.BlockSpec((N_EXPERTS, 128), lambda ph, i: (0, 0)),
                   pl.BlockSpec((N_EXPERTS, 128), lambda ph, i: (0, 0))],
        out_shape=[jax.ShapeDtypeStruct((TOP_K, N), jnp.int32),
                   jax.ShapeDtypeStruct((TOP_K, N), F32),
                   jax.ShapeDtypeStruct((N_EXPERTS, 128), jnp.int32),
                   jax.ShapeDtypeStruct((N_EXPERTS, 128), jnp.int32)],
        scratch_shapes=[pltpu.VMEM((nT, TOP_K, M), jnp.int32), pltpu.VMEM((nT, TOP_K, M), F32),
                        pltpu.VMEM((N_EXPERTS, 128), F32), pltpu.VMEM((N_EXPERTS, 128), F32)],
        compiler_params=_cparams(2),
        name="route",
    )(hp, router_w_t, router_bias.reshape(-1, N_EXPERTS, 1))


def _sc_mesh():
    return plsc.VectorSubcoreMesh(core_axis_name="core", subcore_axis_name="subcore")


def sc_invert(pos_flat, n_tok, n_out):
    n = pos_flat.shape[0]
    per = n_out // SC_WORKERS
    chunk = n_tok
    assert n_out % SC_WORKERS == 0 and per % SC_LANES == 0
    assert n_tok % chunk == 0 and n % chunk == 0 and chunk % SC_LANES == 0
    cp = pltpu.CompilerParams()
    if "needs_layout_passes" in pltpu.CompilerParams.__dataclass_fields__:
        cp = dataclasses.replace(cp, needs_layout_passes=False)

    @functools.partial(
        pl.kernel, out_type=jax.ShapeDtypeStruct((n_out,), jnp.int32), mesh=_sc_mesh(),
        scratch_types=[pltpu.VMEM((chunk,), jnp.int32), pltpu.VMEM((per,), jnp.int32)],
        compiler_params=cp, name="sc_invert")
    def k(pos_hbm, src_hbm, pos_v, src_v):
        wid = lax.axis_index("subcore") * SC_CORES + lax.axis_index("core")
        lo = wid * per
        lane = lax.iota(jnp.int32, SC_LANES)

        @pl.loop(0, per, step=SC_LANES)
        def _(r):
            src_v[pl.ds(r, SC_LANES)] = lax.rem(lo + r + lane, n_tok)

        @pl.loop(0, n // chunk)
        def _(c):
            base = c * chunk
            pltpu.sync_copy(pos_hbm.at[pl.ds(base, chunk)], pos_v)
            tok0 = lax.rem(base, n_tok)

            @plsc.parallel_loop(0, chunk, step=SC_LANES, unroll=8)
            def _(r):
                p = pos_v[pl.ds(r, SC_LANES)] - lo
                mine = (p >= 0) & (p < per)
                plsc.store_scatter(src_v, [jnp.where(mine, p, 0)], tok0 + r + lane, mask=mine)

        pltpu.sync_copy(src_v, src_hbm.at[pl.ds(lo, per)])

    return k(pos_flat)


def sc_gather(x, idx):
    n = idx.shape[0]
    dim = x.shape[1]
    assert n % (SC_WINDOW * SC_WORKERS) == 0

    @functools.partial(
        pl.kernel, out_type=jax.ShapeDtypeStruct((n, dim), x.dtype), mesh=_sc_mesh(),
        scratch_types=[], name="sc_gather")
    def k(x_hbm, i_hbm, o_hbm):
        def body(i_vmem, o_vmem):
            pltpu.sync_copy(x_hbm.at[i_vmem.at[0]], o_vmem)

        pltpu.emit_pipeline(
            body, grid=(n // SC_WINDOW,),
            in_specs=[pl.BlockSpec((1, SC_WINDOW), index_map=lambda i: (i, 0))],
            out_specs=[pl.BlockSpec((SC_WINDOW, dim), index_map=lambda i: (i, 0))],
            core_axis_name=("core", "subcore"),
            dimension_semantics=(pltpu.PARALLEL,),
        )(i_hbm, o_hbm)

    return k(x, idx.reshape(n // SC_WINDOW, SC_WINDOW))


def _moe_gemm_kernel(ts_ref, tn_ref, x_hbm, wi_ref, wo_ref, o_hbm, wi_b, wo_b, xbuf, obuf, in_sem, out_sem,
                     *, tile_rows, n_tiles):
    e = pl.program_id(0)
    last = pl.num_programs(0) - 1
    t0 = ts_ref[e]
    n = tn_ref[e]
    n_used = ts_ref[last] + tn_ref[last]

    def x_copy(g, slot):
        rows = pl.ds(pl.multiple_of(g * tile_rows, tile_rows), tile_rows)
        return pltpu.make_async_copy(x_hbm.at[rows], xbuf.at[slot], in_sem.at[slot])

    def o_copy(g, slot):
        rows = pl.ds(pl.multiple_of(g * tile_rows, tile_rows), tile_rows)
        return pltpu.make_async_copy(obuf.at[slot], o_hbm.at[rows], out_sem.at[slot])

    @pl.when(e == 0)
    def _():
        for g0 in range(MOE_NBUF - 1):
            @pl.when(g0 < n_used)
            def _():
                x_copy(g0, g0).start()

    @pl.when(n > 0)
    def _():
        wi_b[...] = wi_ref[...].astype(BF16)
        wo_b[...] = wo_ref[...].astype(BF16)

    def tile(i, carry):
        g = t0 + i
        slot = lax.rem(g, MOE_NBUF)
        x_copy(g, slot).wait()
        ahead = g + (MOE_NBUF - 1)

        @pl.when(ahead < n_used)
        def _():
            x_copy(ahead, lax.rem(ahead, MOE_NBUF)).start()

        @pl.when(g >= MOE_NBUF)
        def _():
            o_copy(g - MOE_NBUF, slot).wait()

        rows = tile_rows // MOE_SUB
        half = xbuf.shape[2]
        xs = [_unpack_pairs(xbuf[slot, r * rows:(r + 1) * rows, :]) for r in range(MOE_SUB)]
        hus = [_dot(lo, wi_b[:half, :]) + _dot(hi, wi_b[half:, :]) for lo, hi in xs]
        acts = [(_silu(hu[:, :EXPERT_FF]) * hu[:, EXPERT_FF:]).astype(BF16) for hu in hus]
        outs = [_dot(act, wo_b[...]) for act in acts]
        for r, out in enumerate(outs):
            obuf[slot, r * rows:(r + 1) * rows, :] = _pack_pairs(out)
        o_copy(g, slot).start()
        return carry

    lax.fori_loop(0, n, tile, 0)

    @pl.when(e == last)
    def _():
        for back in range(MOE_NBUF, 0, -1):
            @pl.when(n_used >= back)
            def _():
                o_copy(n_used - back, lax.rem(n_used - back, MOE_NBUF)).wait()

        obuf[0] = jnp.zeros(obuf.shape[1:], obuf.dtype)

        def clear(g, carry):
            cp = o_copy(g, 0)
            cp.start()
            cp.wait()
            return carry

        lax.fori_loop(n_used, n_tiles, clear, 0)


def moe_gemm(xs, tile_start, tile_count, exp_w_in, exp_w_out, l, tile_rows, n_tiles):
    P, half = xs.shape
    D = 2 * half
    assert P == n_tiles * tile_rows
    hbm = pl.BlockSpec(memory_space=pl.ANY)
    grid_spec = pltpu.PrefetchScalarGridSpec(
        num_scalar_prefetch=2,
        grid=(N_EXPERTS,),
        in_specs=[hbm,
                  pl.BlockSpec((None, None, D, 2 * EXPERT_FF), lambda e, ts, tn: (l, e, 0, 0)),
                  pl.BlockSpec((None, None, EXPERT_FF, D), lambda e, ts, tn: (l, e, 0, 0))],
        out_specs=hbm,
        scratch_shapes=[pltpu.VMEM((D, 2 * EXPERT_FF), BF16), pltpu.VMEM((EXPERT_FF, D), BF16),
                        pltpu.VMEM((MOE_NBUF, tile_rows, half), jnp.int32),
                        pltpu.VMEM((MOE_NBUF, tile_rows, half), jnp.int32),
                        pltpu.SemaphoreType.DMA((MOE_NBUF,)), pltpu.SemaphoreType.DMA((MOE_NBUF,))],
    )
    return pl.pallas_call(
        functools.partial(_moe_gemm_kernel, tile_rows=tile_rows, n_tiles=n_tiles),
        grid_spec=grid_spec,
        out_shape=jax.ShapeDtypeStruct((P, half), jnp.int32),
        compiler_params=_cparams(1),
        name="moe_gemm",
    )(tile_start, tile_count, xs, exp_w_in, exp_w_out)


def _moe_combine_kernel(*refs, final, norm_next):
    if final:
        y_ref, w_ref, h_ref, si_ref, so_ref, x_ref, g2_ref, fg_ref, o_ref, si_b, so_b = refs
    elif norm_next:
        (y_ref, w_ref, h_ref, si_ref, so_ref, x_ref, g2_ref, ng_ref, nsc_ref, nsh_ref,
         o_ref, hn_ref, si_b, so_b) = refs
    else:
        y_ref, w_ref, h_ref, si_ref, so_ref, x_ref, g2_ref, o_ref, si_b, so_b = refs

    @pl.when(pl.program_id(0) == 0)
    def _():
        si_b[...] = si_ref[...].astype(BF16)
        so_b[...] = so_ref[...].astype(BF16)

    bb, tt, D = x_ref.shape
    half = D // 2
    w = w_ref[...]
    acc_lo = jnp.zeros((bb * tt, half), F32)
    acc_hi = jnp.zeros((bb * tt, half), F32)
    for k in range(TOP_K):
        lo, hi = _unpack_pairs(y_ref[k], F32)
        acc_lo = acc_lo + w[:, k:k + 1] * lo
        acc_hi = acc_hi + w[:, k:k + 1] * hi
    hlo, hhi = _unpack_pairs(h_ref[...])
    hu = _dot(hlo, si_b[:half, :]) + _dot(hhi, si_b[half:, :])
    act = (_silu(hu[:, :SHARED_FF]) * hu[:, SHARED_FF:]).astype(BF16)
    y = jnp.concatenate([acc_lo, acc_hi], axis=-1) + _dot(act, so_b[...])
    x_new = x_ref[...] + g2_ref[...] * y.reshape(bb, tt, D)
    o_ref[...] = _rms(x_new, fg_ref[...]) if final else x_new
    if norm_next:
        hn = _rms(x_new, ng_ref[...]) * (1.0 + nsc_ref[...]) + nsh_ref[...]
        hn_ref[...] = hn.astype(hn_ref.dtype)


def moe_combine(y8, w_t, hp, sh_w_in, sh_w_out, x, mod, gate_idx, l, row0, final_g=None, next_norm=None,
                rows=256):
    B, T, D = x.shape
    half = D // 2
    bb, tt, nblk, ij = _row_blocks(B, T, rows)
    M = bb * tt
    assert row0 % M == 0
    off = row0 // M
    xspec = pl.BlockSpec((bb, tt, D), lambda i: ij(i) + (0,))
    in_specs = [pl.BlockSpec((TOP_K, M, half), lambda i: (0, off + i, 0)),
                pl.BlockSpec((M, TOP_K), lambda i: (off + i, 0)),
                pl.BlockSpec((M, half), lambda i: (off + i, 0)),
                pl.BlockSpec((None, D, 2 * SHARED_FF), lambda i: (l, 0, 0)),
                pl.BlockSpec((None, SHARED_FF, D), lambda i: (l, 0, 0)),
                xspec,
                pl.BlockSpec((bb, 1, D), lambda i: (ij(i)[0], 0, gate_idx))]
    args = [y8, w_t, hp, sh_w_in, sh_w_out, x, mod]
    out_specs = xspec
    out_shape = jax.ShapeDtypeStruct((B, T, D), F32)
    if final_g is not None:
        assert next_norm is None
        in_specs.append(pl.BlockSpec((1, D), lambda i: (0, 0)))
        args.append(final_g.reshape(1, D))
    if next_norm is not None:
        gain, mod_next, sc_idx, sh_idx = next_norm
        in_specs += [pl.BlockSpec((1, D), lambda i: (0, 0)),
                     pl.BlockSpec((bb, 1, D), lambda i: (ij(i)[0], 0, sc_idx)),
                     pl.BlockSpec((bb, 1, D), lambda i: (ij(i)[0], 0, sh_idx))]
        args += [gain.reshape(1, D), mod_next, mod_next]
        out_specs = [xspec, xspec]
        out_shape = [out_shape, jax.ShapeDtypeStruct((B, T, D), BF16)]
    return pl.pallas_call(
        functools.partial(_moe_combine_kernel, final=final_g is not None, norm_next=next_norm is not None),
        grid=(nblk,),
        in_specs=in_specs,
        out_specs=out_specs,
        out_shape=out_shape,
        scratch_shapes=[pltpu.VMEM((D, 2 * SHARED_FF), BF16), pltpu.VMEM((SHARED_FF, D), BF16)],
        compiler_params=_cparams(1),
        name="moe_combine",
    )(*args)


def _shared_kv_kernel(x_ref, g_ref, w_ref, lg_ref, cos_ref, sin_ref, lat_ref, kr_ref):
    bb, tt, D = x_ref.shape
    xn = _rms(x_ref[...], g_ref[...]).reshape(bb * tt, D).astype(BF16)
    z = _dot(xn, w_ref[...].astype(BF16))
    lat = _rms(z[:, :KV_LORA], lg_ref[...])
    lat_ref[...] = lat.reshape(bb, tt, KV_LORA)
    zr = z[:, KV_LORA:KV_LORA + MLA_ROPE].reshape(bb, tt, MLA_ROPE)
    zq = z[:, KV_LORA + 128:KV_LORA + 128 + MLA_ROPE].reshape(bb, tt, MLA_ROPE)
    kr_ref[...] = zr * cos_ref[...] + zq * sin_ref[...]


def shared_kv(x, kv_in_g, w_kv, kv_lat_g, cos32, sin32, rows=512):
    B, T, D = x.shape
    bb, tt, nblk, ij = _row_blocks(B, T, rows)
    tspec = pl.BlockSpec((tt, MLA_ROPE), lambda i: (ij(i)[1], 0))
    return pl.pallas_call(
        _shared_kv_kernel,
        grid=(nblk,),
        in_specs=[pl.BlockSpec((bb, tt, D), lambda i: ij(i) + (0,)),
                  pl.BlockSpec((1, D), lambda i: (0, 0)),
                  pl.BlockSpec(w_kv.shape, lambda i: (0, 0)),
                  pl.BlockSpec((1, KV_LORA), lambda i: (0, 0)),
                  tspec, tspec],
        out_specs=[pl.BlockSpec((bb, tt, KV_LORA), lambda i: ij(i) + (0,)),
                   pl.BlockSpec((bb, tt, MLA_ROPE), lambda i: ij(i) + (0,))],
        out_shape=[jax.ShapeDtypeStruct((B, T, KV_LORA), F32),
                   jax.ShapeDtypeStruct((B, T, MLA_ROPE), F32)],
        compiler_params=_cparams(1),
        name="shared_kv",
    )(x, kv_in_g.reshape(1, D), w_kv, kv_lat_g.reshape(1, KV_LORA), cos32, sin32)


def _kv_expand_kernel(lat_ref, kr_ref, wk_ref, ek_ref, wvt_ref, ones_ref, k_ref, vt_ref):
    lat = lat_ref[0].astype(BF16)
    kr = kr_ref[0].astype(BF16)
    k = _dot(lat, wk_ref[...].astype(BF16)) + _dot(kr, ek_ref[...].astype(BF16))
    k_ref[0] = k.astype(k_ref.dtype)
    vt = _dot_nt(wvt_ref[...].astype(BF16), lat) + ones_ref[...]
    vt_ref[0] = vt.astype(vt_ref.dtype)


def kv_expand(lat, kr, wk_pad, ek, wvt_ext, ones_col, rows=512):
    B, T, _ = lat.shape
    tt = rows
    NK, NVT = wk_pad.shape[1], wvt_ext.shape[0]

    def full(a):
        return pl.BlockSpec(a.shape, lambda b, t: (0, 0))

    def rowspec(n):
        return pl.BlockSpec((1, tt, n), lambda b, t: (b, t, 0))

    return pl.pallas_call(
        _kv_expand_kernel,
        grid=(B, T // tt),
        in_specs=[rowspec(KV_LORA), rowspec(MLA_ROPE), full(wk_pad), full(ek), full(wvt_ext), full(ones_col)],
        out_specs=[rowspec(NK), pl.BlockSpec((1, NVT, tt), lambda b, t: (b, 0, t))],
        out_shape=[jax.ShapeDtypeStruct((B, T, NK), BF16), jax.ShapeDtypeStruct((B, NVT, T), BF16)],
        compiler_params=_cparams(2),
        name="kv_expand",
    )(lat, kr, wk_pad, ek, wvt_ext, ones_col)


def _query_kernel(h_ref, wdq_ref, qg_ref, wq_ref, wqr_ref, c_ref, s_ref, q_ref, wdq_b, wq_b, wqr_b):
    @pl.when(pl.program_id(0) == 0)
    def _():
        wdq_b[...] = wdq_ref[...].astype(BF16)
        wq_b[...] = wq_ref[...].astype(BF16)
        wqr_b[...] = wqr_ref[...].astype(BF16)

    bb, tt, D = h_ref.shape
    h = h_ref[...].reshape(bb * tt, D)
    cq = _rms(_dot(h, wdq_b[...]), qg_ref[...]).astype(BF16)
    q1 = _dot(cq, wq_b[...]).reshape(bb, tt, -1)
    q2 = _dot(cq, wqr_b[...]).reshape(bb, tt, -1)
    c = c_ref[...]
    s = s_ref[...]
    for hd in range(MLA_HEADS):
        sl = slice(hd * HEAD_PAD, (hd + 1) * HEAD_PAD)
        q_ref[:, :, sl] = (q1[:, :, sl] * c + q2[:, :, sl] * s).astype(q_ref.dtype)


def mla_queries(h, w_dq, q_norm_g, wq_pad, wq_rot, l, c128, s128, rows=512):
    B, T, D = h.shape
    bb, tt, nblk, ij = _row_blocks(B, T, rows)
    NQ = wq_pad.shape[-1]
    tspec = pl.BlockSpec((tt, HEAD_PAD), lambda i: (ij(i)[1], 0))
    return pl.pallas_call(
        _query_kernel,
        grid=(nblk,),
        in_specs=[pl.BlockSpec((bb, tt, D), lambda i: ij(i) + (0,)),
                  pl.BlockSpec((None, D, Q_LORA), lambda i: (l, 0, 0)),
                  pl.BlockSpec((None, 1, Q_LORA), lambda i: (l, 0, 0)),
                  pl.BlockSpec((None, Q_LORA, NQ), lambda i: (l, 0, 0)),
                  pl.BlockSpec((None, Q_LORA, NQ), lambda i: (l, 0, 0)),
                  tspec, tspec],
        out_specs=pl.BlockSpec((bb, tt, NQ), lambda i: ij(i) + (0,)),
        out_shape=jax.ShapeDtypeStruct((B, T, NQ), BF16),
        scratch_shapes=[pltpu.VMEM((D, Q_LORA), BF16), pltpu.VMEM((Q_LORA, NQ), BF16),
                        pltpu.VMEM((Q_LORA, NQ), BF16)],
        compiler_params=_cparams(1),
        name="mla_queries",
    )(h, w_dq, q_norm_g.reshape(-1, 1, Q_LORA), wq_pad, wq_rot, c128, s128)


def _query_t_kernel(h_ref, wdq_ref, qg_ref, wqt_ref, wqrt_ref, cos_ref, sin_ref, qt_ref, wdq_b, wqt_b, wqrt_b):
    @pl.when((pl.program_id(0) == 0) & (pl.program_id(1) == 0))
    def _():
        wdq_b[...] = wdq_ref[...].astype(BF16)
        wqt_b[...] = wqt_ref[...].astype(BF16)
        wqrt_b[...] = wqrt_ref[...].astype(BF16)

    cq = _rms(_dot(h_ref[0], wdq_b[...]), qg_ref[...]).astype(BF16)
    q1 = _dot_nt(wqt_b[...], cq)
    q2 = _dot_nt(wqrt_b[...], cq)
    cos = cos_ref[...]
    sin = sin_ref[...]
    pad = jnp.zeros((HEAD_PAD - MLA_NOPE - MLA_ROPE, q1.shape[1]), qt_ref.dtype)
    for hd in range(MLA_HEADS):
        r0 = hd * HEAD_PAD
        rope = (q1[r0 + MLA_NOPE:r0 + MLA_NOPE + MLA_ROPE] * cos
                + q2[hd * MLA_ROPE:(hd + 1) * MLA_ROPE] * sin)
        qt_ref[0, r0:r0 + MLA_NOPE, :] = (q1[r0:r0 + MLA_NOPE] * Q_PRESCALE).astype(qt_ref.dtype)
        qt_ref[0, r0 + MLA_NOPE:r0 + MLA_NOPE + MLA_ROPE, :] = rope.astype(qt_ref.dtype)
        qt_ref[0, r0 + MLA_NOPE + MLA_ROPE:r0 + HEAD_PAD, :] = pad


def mla_queries_t(h, w_dq, q_norm_g, wq_t, wqr_t, l, cos_t, sin_t, rows=512):
    B, T, D = h.shape
    tt = rows
    NQ = wq_t.shape[1]
    NR = wqr_t.shape[1]
    tspec = pl.BlockSpec((MLA_ROPE, tt), lambda b, t: (0, t))
    return pl.pallas_call(
        _query_t_kernel,
        grid=(B, T // tt),
        in_specs=[pl.BlockSpec((1, tt, D), lambda b, t: (b, t, 0)),
                  pl.BlockSpec((None, D, Q_LORA), lambda b, t: (l, 0, 0)),
                  pl.BlockSpec((None, 1, Q_LORA), lambda b, t: (l, 0, 0)),
                  pl.BlockSpec((None, NQ, Q_LORA), lambda b, t: (l, 0, 0)),
                  pl.BlockSpec((None, NR, Q_LORA), lambda b, t: (l, 0, 0)),
                  tspec, tspec],
        out_specs=pl.BlockSpec((1, NQ, tt), lambda b, t: (b, 0, t)),
        out_shape=jax.ShapeDtypeStruct((B, NQ, T), BF16),
        scratch_shapes=[pltpu.VMEM((D, Q_LORA), BF16), pltpu.VMEM((NQ, Q_LORA), BF16),
                        pltpu.VMEM((NR, Q_LORA), BF16)],
        compiler_params=_cparams(2),
        name="mla_queries_t",
    )(h, w_dq, q_norm_g.reshape(-1, 1, Q_LORA), wq_t, wqr_t, cos_t, sin_t)


def _attn_prompt_kernel(qi_tab, ki_tab, qt_ref, k_ref, vt_ref, o_ref, *scratch, tq, tk):
    H = MLA_HEADS
    m_refs, l_refs, acc_refs = scratch[:H], scratch[H:2 * H], scratch[2 * H:]
    p_id = pl.program_id(1)
    qi = qi_tab[p_id]
    ki = ki_tab[p_id]

    @pl.when(ki == 0)
    def _():
        for hd in range(H):
            m_refs[hd][...] = jnp.full(m_refs[hd].shape, NEG_INF, F32)
            l_refs[hd][...] = jnp.zeros(l_refs[hd].shape, F32)
            acc_refs[hd][...] = jnp.zeros(acc_refs[hd].shape, F32)

    def block(masked):
        if masked:
            kchunk = (ki * tk + lax.broadcasted_iota(jnp.int32, (tk, tq), 0)) // CHUNK
            qchunk = (qi * tq + lax.broadcasted_iota(jnp.int32, (tk, tq), 1)) // CHUNK
            mask = kchunk <= qchunk
        def scores(hd):
            sl = slice(hd * HEAD_PAD, (hd + 1) * HEAD_PAD)
            return _dot(k_ref[0, :, sl], qt_ref[0, sl, :])

        pending = [scores(hd) for hd in range(ATTN_LOOKAHEAD)]
        for hd in range(H):
            if hd + ATTN_LOOKAHEAD < H:
                pending.append(scores(hd + ATTN_LOOKAHEAD))
            s = pending.pop(0)
            if masked:
                s = jnp.where(mask, s, NEG_INF)
            m_prev = m_refs[hd][...]
            m_new = jnp.maximum(m_prev, jnp.max(s, axis=0, keepdims=True))
            a = jnp.exp2(m_prev - m_new)
            p = jnp.exp2(s - m_new).astype(BF16)
            pv = _dot(vt_ref[0, hd * V_ROWS:(hd + 1) * V_ROWS, :], p)
            acc_refs[hd][...] = a * acc_refs[hd][...] + pv[:MLA_V]
            l_refs[hd][...] = a * l_refs[hd][...] + pv[MLA_V:MLA_V + 1]
            m_refs[hd][...] = m_new

    @pl.when(ki < qi)
    def _():
        block(False)

    @pl.when(ki == qi)
    def _():
        block(True)
        o_t = jnp.concatenate([acc_refs[hd][...] / l_refs[hd][...] for hd in range(H)], axis=0)
        o_ref[0] = o_t.T.astype(o_ref.dtype)


def attn_prompt(qt, k, vt, tq=256):
    B, NQ, T = qt.shape
    NVT = vt.shape[1]
    NV = MLA_HEADS * MLA_V
    tk = tq
    assert tq % CHUNK == 0
    nq = T // tq
    pairs = [(a, b) for a in range(nq) for b in range(a + 1)]
    qi_tab = jnp.asarray([a for a, _ in pairs], jnp.int32)
    ki_tab = jnp.asarray([b for _, b in pairs], jnp.int32)
    grid_spec = pltpu.PrefetchScalarGridSpec(
        num_scalar_prefetch=2,
        grid=(B, len(pairs)),
        in_specs=[pl.BlockSpec((1, NQ, tq), lambda b, p, qt, kt: (b, 0, qt[p])),
                  pl.BlockSpec((1, tk, NQ), lambda b, p, qt, kt: (b, kt[p], 0)),
                  pl.BlockSpec((1, NVT, tk), lambda b, p, qt, kt: (b, 0, kt[p]))],
        out_specs=pl.BlockSpec((1, tq, NV), lambda b, p, qt, kt: (b, qt[p], 0)),
        scratch_shapes=([pltpu.VMEM((1, tq), F32)] * (2 * MLA_HEADS)
                        + [pltpu.VMEM((MLA_V, tq), F32)] * MLA_HEADS),
    )
    return pl.pallas_call(
        functools.partial(_attn_prompt_kernel, tq=tq, tk=tk),
        grid_spec=grid_spec,
        out_shape=jax.ShapeDtypeStruct((B, T, NV), BF16),
        compiler_params=_cparams(2),
        name="attn_prompt",
    )(qi_tab, ki_tab, qt, k, vt)


def _absorb_kernel(q_ref, m_ref, o_ref):
    o_ref[...] = _dot(q_ref[...], m_ref[...].astype(BF16)).astype(o_ref.dtype)


def absorb_queries(q2d, m_abs):
    N = q2d.shape[0]
    H, _, W = m_abs.shape
    return pl.pallas_call(
        _absorb_kernel,
        grid=(H,),
        in_specs=[pl.BlockSpec((N, HEAD_PAD), lambda h: (0, h)),
                  pl.BlockSpec((None, HEAD_PAD, W), lambda h: (h, 0, 0))],
        out_specs=pl.BlockSpec((None, N, W), lambda h: (h, 0, 0)),
        out_shape=jax.ShapeDtypeStruct((H, N, W), BF16),
        compiler_params=_cparams(1),
        name="absorb_queries",
    )(q2d, m_abs)


def _attn_sample_kernel(q_ref, lat_ref, kr_ref, nlat_ref, nkr_ref, o_ref, m_ref, l_ref, acc_ref):
    kb = pl.program_id(1)
    H, Q, W = q_ref.shape
    q = q_ref[...].reshape(H * Q, W)
    q_lat = q[:, :KV_LORA]
    q_rope = q[:, KV_LORA:KV_LORA + MLA_ROPE]

    def update(lat_tile, kr_tile, n_sub, kr_transposed):
        sub = lat_tile.shape[0] // n_sub
        lats = [lat_tile[j * sub:(j + 1) * sub, :].astype(BF16) for j in range(n_sub)]
        if kr_transposed:
            krs = [kr_tile[:, j * sub:(j + 1) * sub].astype(BF16) for j in range(n_sub)]
            ss = [_dot_nt(q_lat, lat) + _dot(q_rope, kr) for lat, kr in zip(lats, krs)]
        else:
            krs = [kr_tile[j * sub:(j + 1) * sub, :].astype(BF16) for j in range(n_sub)]
            ss = [_dot_nt(q_lat, lat) + _dot_nt(q_rope, kr) for lat, kr in zip(lats, krs)]
        m_prev = m_ref[...]
        m_new = m_prev
        for s in ss:
            m_new = jnp.maximum(m_new, jnp.max(s, axis=-1, keepdims=True))
        a = jnp.exp2(m_prev - m_new)
        ps = [jnp.exp2(s - m_new[:, :1]) for s in ss]
        pv = _dot(ps[0].astype(BF16), lats[0])
        psum = jnp.sum(ps[0], axis=-1, keepdims=True)
        for p, lat in zip(ps[1:], lats[1:]):
            pv = pv + _dot(p.astype(BF16), lat)
            psum = psum + jnp.sum(p, axis=-1, keepdims=True)
        l_ref[...] = a * l_ref[...] + psum
        m_ref[...] = m_new
        acc_ref[...] = jnp.concatenate([a, a], axis=-1) * acc_ref[...] + pv

    @pl.when(kb == 0)
    def _():
        m_ref[...] = jnp.full_like(m_ref, NEG_INF)
        l_ref[...] = jnp.zeros_like(l_ref)
        acc_ref[...] = jnp.zeros_like(acc_ref)
        update(nlat_ref[0], nkr_ref[0], 1, False)

    update(lat_ref[0], kr_ref[0], SAMPLE_KEY_SUB, True)

    @pl.when(kb == pl.num_programs(1) - 1)
    def _():
        lsum = l_ref[...]
        o = acc_ref[...] / jnp.concatenate([lsum, lsum], axis=-1)
        o_ref[...] = o.reshape(H, Q, KV_LORA).astype(o_ref.dtype)


def attn_sample(q_abs, cache_lat, cache_kr_t, new_lat, new_kr, tk=4096):
    H, N, W = q_abs.shape
    B, P, _ = cache_lat.shape
    Q = new_lat.shape[1]
    qpos = P + np.arange(Q)
    kpos = np.arange(P + Q)
    assert bool(np.all((kpos // CHUNK)[None, :] <= (qpos // CHUNK)[:, None]))
    return pl.pallas_call(
        _attn_sample_kernel,
        grid=(B, P // tk),
        in_specs=[pl.BlockSpec((H, Q, W), lambda b, kb: (0, b, 0)),
                  pl.BlockSpec((1, tk, KV_LORA), lambda b, kb: (b, kb, 0)),
                  pl.BlockSpec((1, MLA_ROPE, tk), lambda b, kb: (b, 0, kb)),
                  pl.BlockSpec((1, Q, KV_LORA), lambda b, kb: (b, 0, 0)),
                  pl.BlockSpec((1, Q, MLA_ROPE), lambda b, kb: (b, 0, 0))],
        out_specs=pl.BlockSpec((H, Q, KV_LORA), lambda b, kb: (0, b, 0)),
        out_shape=jax.ShapeDtypeStruct((H, N, KV_LORA), BF16),
        scratch_shapes=[pltpu.VMEM((H * Q, 128), F32), pltpu.VMEM((H * Q, 128), F32),
                        pltpu.VMEM((H * Q, KV_LORA), F32)],
        compiler_params=_cparams(2),
        name="attn_sample",
    )(q_abs, cache_lat, cache_kr_t, new_lat, new_kr)


def _unabsorb_kernel(o_ref, w_ref, out_ref):
    out_ref[...] = (_dot(o_ref[0], w_ref[0].astype(BF16))
                    + _dot(o_ref[1], w_ref[1].astype(BF16))).astype(out_ref.dtype)


def unabsorb(o_lat, wuv_pad):
    H, N, R = o_lat.shape
    return pl.pallas_call(
        _unabsorb_kernel,
        grid=(H // 2,),
        in_specs=[pl.BlockSpec((2, N, R), lambda p: (p, 0, 0)),
                  pl.BlockSpec((2, R, 128), lambda p: (p, 0, 0))],
        out_specs=pl.BlockSpec((N, 128), lambda p: (0, p)),
        out_shape=jax.ShapeDtypeStruct((N, (H // 2) * 128), BF16),
        compiler_params=_cparams(1),
        name="unabsorb",
    )(o_lat, wuv_pad)


def _rope_tables(pos):
    half = MLA_ROPE // 2
    inv = 1.0 / (ROPE_THETA ** (np.arange(half, dtype=np.float64) * 2.0 / MLA_ROPE))
    ang = np.asarray(pos, np.float64)[:, None] * inv[None, :]
    cos = np.concatenate([np.cos(ang), np.cos(ang)], axis=-1)
    sin = np.concatenate([np.sin(ang), np.sin(ang)], axis=-1)
    T = cos.shape[0]
    c128 = np.zeros((T, HEAD_PAD)); s128 = np.zeros((T, HEAD_PAD))
    c128[:, :MLA_NOPE] = 1.0
    c128[:, MLA_NOPE:MLA_NOPE + MLA_ROPE] = cos
    s128[:, MLA_NOPE:MLA_NOPE + MLA_ROPE] = sin
    return dict(cos32=jnp.asarray(cos, F32), sin32=jnp.asarray(sin, F32),
                c128=jnp.asarray(c128 * Q_PRESCALE, F32), s128=jnp.asarray(s128 * Q_PRESCALE, F32),
                cos_t=jnp.asarray(cos.T * Q_PRESCALE, F32), sin_t=jnp.asarray(sin.T * Q_PRESCALE, F32))


def _rot_half_cols(w):
    half = w.shape[-1] // 2
    return jnp.concatenate([-w[..., half:], w[..., :half]], axis=-1)


def _prep_weights(w_dkv, w_uk, w_uv, w_uq, router_w):
    D = D_MODEL
    w_lat, w_rope = w_dkv[:, :KV_LORA], w_dkv[:, KV_LORA:]
    pad96 = jnp.zeros((D, 128 - MLA_ROPE), F32)
    w_kv = jnp.concatenate([w_lat, w_rope, pad96, _rot_half_cols(w_rope), pad96], axis=-1)

    zpad = HEAD_PAD - MLA_NOPE
    wk_pad = jnp.pad(w_uk, ((0, 0), (0, 0), (0, zpad))).reshape(KV_LORA, MLA_HEADS * HEAD_PAD)
    ek = jnp.zeros((MLA_ROPE, MLA_HEADS, HEAD_PAD), F32)
    ek = ek.at[:, :, MLA_NOPE:MLA_NOPE + MLA_ROPE].set(
        jnp.broadcast_to(jnp.eye(MLA_ROPE, dtype=F32)[:, None, :], (MLA_ROPE, MLA_HEADS, MLA_ROPE)))
    ek = ek.reshape(MLA_ROPE, MLA_HEADS * HEAD_PAD)
    wvt = jnp.transpose(w_uv, (1, 2, 0))
    wvt_ext = jnp.pad(wvt, ((0, 0), (0, V_ROWS - MLA_V), (0, 0))).reshape(MLA_HEADS * V_ROWS, KV_LORA)
    ones_col = jnp.tile((jnp.arange(V_ROWS) >= MLA_V).astype(F32), MLA_HEADS).reshape(-1, 1)

    nb = w_uq.shape[0]
    qn, qr = w_uq[..., :MLA_NOPE], w_uq[..., MLA_NOPE:]
    z32 = jnp.zeros(qr.shape[:-1] + (HEAD_PAD - MLA_NOPE - MLA_ROPE,), F32)
    wq_pad = jnp.concatenate([qn, qr, z32], axis=-1).reshape(nb, Q_LORA, MLA_HEADS * HEAD_PAD)
    wq_rot = jnp.concatenate([jnp.zeros_like(qn), _rot_half_cols(qr), z32], axis=-1)
    wq_rot = wq_rot.reshape(nb, Q_LORA, MLA_HEADS * HEAD_PAD)
    wq_t = jnp.transpose(wq_pad, (0, 2, 1))
    wqr_t = jnp.transpose(_rot_half_cols(qr).reshape(nb, Q_LORA, MLA_HEADS * MLA_ROPE), (0, 2, 1))

    m_abs = jnp.zeros((MLA_HEADS, HEAD_PAD, KV_LORA + 128), F32)
    m_abs = m_abs.at[:, :MLA_NOPE, :KV_LORA].set(jnp.transpose(w_uk, (1, 2, 0)))
    m_abs = m_abs.at[:, MLA_NOPE:MLA_NOPE + MLA_ROPE, KV_LORA:KV_LORA + MLA_ROPE].set(
        jnp.broadcast_to(jnp.eye(MLA_ROPE, dtype=F32), (MLA_HEADS, MLA_ROPE, MLA_ROPE)))

    wuv_h = jnp.transpose(w_uv, (1, 0, 2))
    even = jnp.pad(wuv_h, ((0, 0), (0, 0), (0, 64)))
    odd = jnp.pad(wuv_h, ((0, 0), (0, 0), (64, 0)))
    wuv_pad = jnp.where((jnp.arange(MLA_HEADS) % 2 == 0)[:, None, None], even, odd)

    rw_t = jnp.transpose(router_w, (0, 2, 1))
    return dict(w_kv=w_kv, wk_pad=wk_pad, ek=ek, wvt_ext=wvt_ext, ones_col=ones_col, wq_pad=wq_pad, wq_rot=wq_rot, wq_t=wq_t, wqr_t=wqr_t,
                m_abs=m_abs, wuv_pad=wuv_pad, rw_t=rw_t)


def _mixer(st, l, P, W, packed):
    rows_kw = dict(rows_total=packed["total"], row0=packed["row0"], rows_buf=packed["buf"])
    x, m = st["x"], st["mod"][l]
    B, T, _ = x.shape
    n_a = P["hg_w_in"].shape[0]
    h = st.pop("h_next", None)
    if h is None:
        h = norm_mod(x, P["norm1_g"][l], m, sc_idx=1, sh_idx=0)
    norm2 = (P["norm2_g"][l], 4, 3)
    if l < n_a:
        rows = min(1024, B * T)
        zf = linear(h, P["hg_w_in"], l, F32, col_blocks=(1,), rows=rows)
        zqig = linear(h, P["hg_w_in"], l, BF16, col_blocks=(0, 2, 3), rows=rows)
        s0 = None if st["hg_state"] is None else st["hg_state"][l]
        o, s_new = gla(zqig, zf, st["lbs"][l], P["hg_onorm_g"][l], s0)
        st["hg_new"].append(s_new)
        st["x"], packed["buf"] = linear(o, P["hg_w_out"], l, F32, x=x, mod=m, gate_idx=2, next_norm=norm2,
                                        **rows_kw)
    else:
        bi = l - n_a
        if st["past_lat"] is None:
            qt = mla_queries_t(h, P["w_dq"], P["q_norm_g"], W["wq_t"], W["wqr_t"], bi, st["cos_t"], st["sin_t"])
            o = attn_prompt(qt, st["k_all"], st["v_all"])
        else:
            q = mla_queries(h, P["w_dq"], P["q_norm_g"], W["wq_pad"], W["wq_rot"], bi, st["c128"], st["s128"])
            q_abs = absorb_queries(q.reshape(B * T, -1), W["m_abs"])
            o_lat = attn_sample(q_abs, st["past_lat"], st["past_kr"], st["lat"], st["kr"])
            o = unabsorb(o_lat, W["wuv_pad"]).reshape(B, T, -1)
        st["x"], packed["buf"] = linear(o, P["w_o"], bi, F32, x=x, mod=m, gate_idx=2, next_norm=norm2, **rows_kw)
    packed["row0"] += B * T


def _moe(groups, hp, l, P, W):
    n_tok = hp.shape[0]
    n_tiles = (TOP_K * n_tok) // MOE_TILE + N_EXPERTS
    pos, w8, tile_start, tile_count = route(hp, W["rw_t"], P["router_bias"], l, MOE_TILE)
    pos_flat = pos.reshape(-1)
    src = sc_invert(pos_flat, n_tok, n_tiles * MOE_TILE)
    xs = sc_gather(hp, src)
    out = moe_gemm(xs, tile_start[:, 0], tile_count[:, 0], P["exp_w_in"], P["exp_w_out"], l,
                   MOE_TILE, n_tiles)
    y8 = sc_gather(out, pos_flat).reshape(TOP_K, n_tok, -1)
    w_t = w8.T
    row0 = 0
    for st in groups:
        B, T, _ = st["x"].shape
        if l == P["norm1_g"].shape[0] - 1:
            st["x"] = moe_combine(y8, w_t, hp, P["sh_w_in"], P["sh_w_out"], st["x"], st["mod"][l], 5, l, row0,
                                  final_g=P["final_g"])
        else:
            nxt = (P["norm1_g"][l + 1], st["mod"][l + 1], 1, 0)
            st["x"], st["h_next"] = moe_combine(y8, w_t, hp, P["sh_w_in"], P["sh_w_out"], st["x"],
                                                st["mod"][l], 5, l, row0, next_norm=nxt)
        row0 += B * T


def _group_state(x, mod, pos, hg_state, past_lat, past_kr, lbs):
    return dict(x=x, mod=mod, hg_state=hg_state, past_lat=past_lat, past_kr=past_kr, lbs=lbs,
                **_rope_tables(pos), hg_new=[],
                lat=None, kr=None, k_all=None, v_all=None)


def kernel(x_prompt, x_sample, state_hgrn, cache_mla_latent, cache_mla_krope, c_prompt, c_sample, ada_w, ada_b, norm1_g, norm2_g, hg_w_in, hg_lb_logits, hg_onorm_g, hg_w_out, kv_in_g, w_dkv, kv_lat_g, w_uk, w_uv, w_dq, q_norm_g, w_uq, w_o, router_w, router_bias, exp_w_in, exp_w_out, sh_w_in, sh_w_out, final_g):
    Bp, Sp, _ = x_prompt.shape
    Bs, Ss, _ = x_sample.shape
    past = cache_mla_latent.shape[1]
    P = dict(norm1_g=norm1_g, norm2_g=norm2_g, hg_w_in=hg_w_in, hg_lb_logits=hg_lb_logits,
             hg_onorm_g=hg_onorm_g, hg_w_out=hg_w_out, kv_in_g=kv_in_g, kv_lat_g=kv_lat_g,
             w_dq=w_dq, q_norm_g=q_norm_g, w_o=w_o, router_bias=router_bias,
             exp_w_in=exp_w_in, exp_w_out=exp_w_out, sh_w_in=sh_w_in, sh_w_out=sh_w_out, final_g=final_g)
    W = _prep_weights(w_dkv, w_uk, w_uv, w_uq, router_w)
    mod = ada_mod(jnp.concatenate([c_prompt, c_sample], axis=0), ada_w, ada_b)
    lbs = jnp.cumsum(jax.nn.softmax(hg_lb_logits.astype(F32), axis=0), axis=0)
    bsz = Bp // PROMPT_STREAMS
    prompts = [_group_state(x_prompt[i * bsz:(i + 1) * bsz], mod[:, i * bsz:(i + 1) * bsz, None, :],
                            np.arange(Sp), None, None, None, lbs) for i in range(PROMPT_STREAMS)]
    gs = _group_state(x_sample, mod[:, Bp:, None, :], past + np.arange(Ss), state_hgrn,
                      cache_mla_latent, jnp.transpose(cache_mla_krope, (0, 2, 1)), lbs)
    streams = [[prompts[0], gs]] + [[g] for g in prompts[1:]]
    n_a = hg_w_in.shape[0]
    for l in range(norm1_g.shape[0]):
        for groups in streams:
            packed = dict(total=sum(st["x"].shape[0] * st["x"].shape[1] for st in groups), row0=0, buf=None)
            for st in groups:
                _mixer(st, l, P, W, packed)
            _moe(groups, packed["buf"], l, P, W)
            if l == n_a - 1:
                for st in groups:
                    st["lat"], st["kr"] = shared_kv(st["x"], kv_in_g, W["w_kv"], kv_lat_g, st["cos32"], st["sin32"])
                    if st["past_lat"] is None:
                        st["k_all"], st["v_all"] = kv_expand(st["lat"], st["kr"], W["wk_pad"], W["ek"],
                                                             W["wvt_ext"], W["ones_col"])
    for st in prompts + [gs]:
        st["y"] = st["x"]
        st["hg_out"] = jnp.stack(st["hg_new"], axis=0)

    def cat(key, axis=0):
        return jnp.concatenate([g[key] for g in prompts], axis=axis)

    return (cat("y"), gs["y"], cat("hg_out", 1), gs["hg_out"], cat("lat"), cat("kr"), gs["lat"], gs["kr"])
```

```python
import dataclasses
import functools

import numpy as np
import jax
import jax.numpy as jnp
from jax import lax
from jax.experimental import pallas as pl
from jax.experimental.pallas import tpu as pltpu
from jax.experimental.pallas import tpu_sc as plsc

F32 = jnp.float32
BF16 = jnp.bfloat16

D_MODEL = 1024
CHUNK = 64
HG_HEADS = 8
HG_DK = 128
HG_DV = 128
MLA_HEADS = 16
MLA_NOPE = 64
MLA_ROPE = 32
MLA_V = 64
Q_LORA = 384
KV_LORA = 256
ROPE_THETA = 10000.0
N_EXPERTS = 64
TOP_K = 8
N_GROUPS = 8
TOPK_GROUPS = 4
EXPERT_FF = 256
SHARED_FF = 256
ROUTED_SCALE = 2.5
EPS = 1e-6

HEAD_PAD = 128
SAMPLE_KEY_SUB = 8
ATTN_LOOKAHEAD = 6
V_ROWS = MLA_V + 16
QK_SCALE = (MLA_NOPE + MLA_ROPE) ** -0.5
Q_PRESCALE = QK_SCALE * float(np.log2(np.e))
VMEM_LIMIT = 56 * 1024 * 1024
NEG_INF = float("-inf")
SC_CORES = 2
SC_SUBCORES = 16
SC_WORKERS = SC_CORES * SC_SUBCORES
SC_LANES = 16
SC_WINDOW = 64
MOE_TILE = 512
MOE_NBUF = 4
MOE_SUB = 1


def _cparams(n_axes):
    return pltpu.CompilerParams(dimension_semantics=("arbitrary",) * n_axes,
                                vmem_limit_bytes=VMEM_LIMIT)


def _silu(x):
    return x * jax.nn.sigmoid(x)


def _rms(x, g):
    ms = jnp.mean(x * x, axis=-1, keepdims=True)
    return x * lax.rsqrt(ms + EPS) * g


def _dot(a, b):
    return jnp.dot(a, b, preferred_element_type=F32)


def _dot_nt(a, b):
    return lax.dot_general(a, b, (((1,), (1,)), ((), ())), preferred_element_type=F32)


def _dot_tn(a, b):
    return lax.dot_general(a, b, (((0,), (0,)), ((), ())), preferred_element_type=F32)


def _row_blocks(B, T, rows):
    if T >= rows:
        assert T % rows == 0
        bb, tt = 1, rows
    else:
        assert rows % T == 0 and B % (rows // T) == 0
        bb, tt = rows // T, T
    nt = T // tt
    return bb, tt, (B // bb) * nt, (lambda i: (i // nt, i % nt))


def _ada_kernel(c_ref, w_ref, b_ref, o_ref):
    a = _silu(c_ref[...]).astype(BF16)
    o_ref[...] = _dot(a, w_ref[...].astype(BF16)) + b_ref[...]


def ada_mod(c, ada_w, ada_b):
    R, D = c.shape
    L, _, N = ada_w.shape
    tn = 1536
    return pl.pallas_call(
        _ada_kernel,
        grid=(L, N // tn),
        in_specs=[pl.BlockSpec((R, D), lambda l, j: (0, 0)),
                  pl.BlockSpec((None, D, tn), lambda l, j: (l, 0, j)),
                  pl.BlockSpec((None, 1, tn), lambda l, j: (l, 0, j))],
        out_specs=pl.BlockSpec((None, R, tn), lambda l, j: (l, 0, j)),
        out_shape=jax.ShapeDtypeStruct((L, R, N), F32),
        compiler_params=_cparams(2),
        name="ada_mod",
    )(c, ada_w, ada_b.reshape(L, 1, N))


def _pack_pairs(y):
    half = y.shape[-1] // 2
    bits = lax.bitcast_convert_type(y.astype(BF16).astype(F32), jnp.uint32)
    word = lax.shift_right_logical(bits[:, :half], jnp.uint32(16)) | bits[:, half:]
    return lax.bitcast_convert_type(word, jnp.int32)


def _unpack_pairs(word, dtype=BF16):
    u = lax.bitcast_convert_type(word, jnp.uint32)
    lo = lax.bitcast_convert_type(lax.shift_left(u, jnp.uint32(16)), F32)
    hi = lax.bitcast_convert_type(u & jnp.uint32(0xFFFF0000), F32)
    return lo.astype(dtype), hi.astype(dtype)


def _norm_kernel(x_ref, g_ref, sc_ref, sh_ref, o_ref):
    y = _rms(x_ref[...], g_ref[...]) * (1.0 + sc_ref[...]) + sh_ref[...]
    o_ref[...] = y.astype(o_ref.dtype)


def norm_mod(x, g, mod, sc_idx, sh_idx, rows=512):
    B, T, D = x.shape
    bb, tt, nblk, ij = _row_blocks(B, T, rows)
    xspec = pl.BlockSpec((bb, tt, D), lambda i: ij(i) + (0,))
    return pl.pallas_call(
        _norm_kernel,
        grid=(nblk,),
        in_specs=[xspec, pl.BlockSpec((1, D), lambda i: (0, 0)),
                  pl.BlockSpec((bb, 1, D), lambda i: (ij(i)[0], 0, sc_idx)),
                  pl.BlockSpec((bb, 1, D), lambda i: (ij(i)[0], 0, sh_idx))],
        out_specs=xspec,
        out_shape=jax.ShapeDtypeStruct((B, T, D), BF16),
        compiler_params=_cparams(1),
        name="norm_mod",
    )(x, g.reshape(1, D), mod, mod)


def _linear_kernel(*refs, residual, norm_next, shared_rows, n_main):
    if norm_next and shared_rows:
        a_ref, w_ref, x_ref, gate_ref, ng_ref, nsc_ref, nsh_ref, _, o_ref, hp_ref, wb_ref = refs
    elif norm_next:
        a_ref, w_ref, x_ref, gate_ref, ng_ref, nsc_ref, nsh_ref, o_ref, hp_ref, wb_ref = refs
    elif residual:
        a_ref, w_ref, x_ref, gate_ref, o_ref, wb_ref = refs
    else:
        a_ref, w_ref, o_ref, wb_ref = refs

    @pl.when(pl.program_id(1) == 0)
    def _():
        wb_ref[...] = w_ref[...].astype(BF16)

    def main():
        bb, tt, K = a_ref.shape
        y = _dot(a_ref[...].reshape(bb * tt, K).astype(BF16), wb_ref[...])
        y = y.reshape(bb, tt, y.shape[-1])
        if residual:
            y = x_ref[...] + gate_ref[...] * y
        o_ref[...] = y.astype(o_ref.dtype)
        if norm_next:
            h = _rms(y, ng_ref[...]) * (1.0 + nsc_ref[...]) + nsh_ref[...]
            hp_ref[...] = _pack_pairs(h.reshape(bb * tt, h.shape[-1]))

    if n_main is None:
        main()
    else:
        pl.when(pl.program_id(1) < n_main)(main)

        @pl.when(pl.program_id(1) >= n_main)
        def _():
            hp_ref[...] = jnp.zeros(hp_ref.shape, hp_ref.dtype)


def linear(a, w, l, out_dtype, x=None, mod=None, gate_idx=0, rows=512, tn=1024, next_norm=None,
           rows_total=None, row0=0, rows_buf=None, col_blocks=None):
    B, T, K = a.shape
    _, _, N = w.shape
    tn = min(tn, N)
    if col_blocks is None:
        def wcol(j):
            return j
    else:
        first, skip_from = col_blocks[0], [c - k for k, c in enumerate(col_blocks)]
        gap_at = next((k for k, d in enumerate(skip_from) if d != first), len(col_blocks))
        assert all(d == first for d in skip_from[:gap_at]) and all(d == first + 1 for d in skip_from[gap_at:])
        N = len(col_blocks) * tn

        def wcol(j):
            return j + first + (j >= gap_at)
    bb, tt, nblk, ij0 = _row_blocks(B, T, rows)
    n_extra = 0
    if next_norm is not None and rows_buf is None and rows_total is not None:
        assert row0 == 0 and (rows_total - B * T) % (bb * tt) == 0
        n_extra = (rows_total - B * T) // (bb * tt)

    def ij(i):
        return ij0(jnp.minimum(i, nblk - 1)) if n_extra else ij0(i)

    in_specs = [pl.BlockSpec((bb, tt, K), lambda j, i: ij(i) + (0,)),
                pl.BlockSpec((None, K, tn), lambda j, i: (l, 0, wcol(j)))]
    args = [a, w]
    ospec = pl.BlockSpec((bb, tt, tn), lambda j, i: ij(i) + (j,))
    out_specs = ospec
    out_shape = jax.ShapeDtypeStruct((B, T, N), out_dtype)
    aliases = {}
    if x is not None:
        gsteps = D_MODEL // tn
        in_specs += [ospec, pl.BlockSpec((bb, 1, tn), lambda j, i: (ij(i)[0], 0, gate_idx * gsteps + j))]
        args += [x, mod]
    if next_norm is not None:
        assert x is not None and tn == N
        gain, sc_idx, sh_idx = next_norm
        in_specs += [pl.BlockSpec((1, N), lambda j, i: (0, 0)),
                     pl.BlockSpec((bb, 1, N), lambda j, i: (ij(i)[0], 0, sc_idx)),
                     pl.BlockSpec((bb, 1, N), lambda j, i: (ij(i)[0], 0, sh_idx))]
        args += [gain.reshape(1, N), mod, mod]
        assert row0 % (bb * tt) == 0
        off = row0 // (bb * tt)
        out_specs = [ospec, pl.BlockSpec((bb * tt, N // 2), lambda j, i: (off + i, 0))]
        out_shape = [out_shape, jax.ShapeDtypeStruct((rows_total or B * T, N // 2), jnp.int32)]
        if rows_buf is not None:
            in_specs.append(pl.BlockSpec(memory_space=pl.ANY))
            args.append(rows_buf)
            aliases = {len(args) - 1: 1}
    return pl.pallas_call(
        functools.partial(_linear_kernel, residual=x is not None, norm_next=next_norm is not None,
                          shared_rows=rows_buf is not None, n_main=nblk if n_extra else None),
        grid=(N // tn, nblk + n_extra),
        in_specs=in_specs,
        out_specs=out_specs,
        out_shape=out_shape,
        scratch_shapes=[pltpu.VMEM((K, tn), BF16)],
        input_output_aliases=aliases,
        compiler_params=_cparams(2),
        name="linear",
    )(*args)


def _gla_kernel(*refs, L, n_chunks, has_init):
    if has_init:
        q_ref, f_ref, i_ref, g_ref, lb_ref, on_ref, s0_ref, o_ref, so_ref, st_ref = refs
    else:
        q_ref, f_ref, i_ref, g_ref, lb_ref, on_ref, o_ref, so_ref, st_ref = refs
    t = pl.program_id(1)
    H = st_ref.shape[0]

    @pl.when(t == 0)
    def _():
        for h in range(H):
            if has_init:
                st_ref[h] = s0_ref[0, h].T
            else:
                st_ref[h] = jnp.zeros(st_ref.shape[1:], F32)

    lb = lb_ref[...]
    onorm = on_ref[...]
    row = lax.broadcasted_iota(jnp.int32, (L, L), 0)
    col = lax.broadcasted_iota(jnp.int32, (L, L), 1)
    causal = col <= row
    tri = causal.astype(BF16)

    def chunk(c, carry):
        rows = pl.ds(pl.multiple_of(c * L, L), L)

        def write_o(sl, o):
            o_ref[0, rows, sl] = o.astype(o_ref.dtype)

        _gla_chunk(q_ref[0, rows, :], f_ref[0, rows, :], i_ref[0, rows, :], g_ref[0, rows, :],
                   lb, onorm, tri, causal, st_ref, write_o)
        return carry

    lax.fori_loop(0, n_chunks, chunk, 0, unroll=4 if n_chunks % 4 == 0 else 1)

    @pl.when(t == pl.num_programs(1) - 1)
    def _():
        for h in range(H):
            so_ref[0, h] = st_ref[h].T


def _gla_chunk(q, f, v, g, lb, onorm, tri, causal, st_ref, write_o):
    L = q.shape[0]
    H = st_ref.shape[0]
    mid = L // 2 - 1
    q = _silu(q.astype(F32))
    fg = lb + (1.0 - lb) * jax.nn.sigmoid(f)
    k = 1.0 - fg
    v = v.astype(BF16)
    gate = _silu(g.astype(F32))
    logf = jnp.log(fg)
    hi = logf.astype(BF16)
    lo = (logf - hi.astype(F32)).astype(BF16)
    b = _dot(tri, hi) + _dot(tri, lo)
    b_mid = b[mid:mid + 1, :]
    b_last = b[L - 1:L, :]
    qa = q * jnp.exp(b - b_mid)
    kb = k * jnp.exp(b_mid - b)
    qe = (qa * jnp.exp(b_mid)).astype(BF16)
    kd = (kb * jnp.exp(b_last - b_mid)).astype(BF16)
    qa = qa.astype(BF16)
    kb = kb.astype(BF16)
    decay = jnp.exp(b_last)
    sls = [slice(h * HG_DK, (h + 1) * HG_DK) for h in range(H)]
    sts = [st_ref[h] for h in range(H)]
    scores = [_dot_nt(qa[:, sl], kb[:, sl]) for sl in sls]
    inter = [_dot_nt(qe[:, sl], st.astype(BF16)) for sl, st in zip(sls, sts)]
    outer = [_dot_tn(v[:, sl], kd[:, sl]) for sl in sls]
    intra = [_dot(jnp.where(causal, sc, 0.0).astype(BF16), v[:, sl]) for sc, sl in zip(scores, sls)]
    for h, sl in enumerate(sls):
        st_ref[h] = sts[h] * decay[:, sl] + outer[h]
        write_o(sl, _rms(inter[h] + intra[h], onorm[:, sl]) * gate[:, sl])


def gla(zqig, zf, lb, onorm_g, s0):
    B, T, D = zf.shape
    L = CHUNK if T % CHUNK == 0 else T
    tt = min(T, 512)
    n_chunks = tt // L
    H = HG_HEADS

    def zspec(part):
        return pl.BlockSpec((1, tt, D), lambda b, t: (b, t, part))

    hspec = pl.BlockSpec((1, D), lambda b, t: (0, 0))
    sspec = pl.BlockSpec((1, H, HG_DK, HG_DV), lambda b, t: (b, 0, 0, 0))
    in_specs = [zspec(0), zspec(0), zspec(1), zspec(2), hspec, hspec]
    args = [zqig, zf, zqig, zqig, lb.reshape(1, D), onorm_g.reshape(1, D)]
    if s0 is not None:
        in_specs.append(sspec)
        args.append(s0)
    return pl.pallas_call(
        functools.partial(_gla_kernel, L=L, n_chunks=n_chunks, has_init=s0 is not None),
        grid=(B, T // tt),
        in_specs=in_specs,
        out_specs=[pl.BlockSpec((1, tt, D), lambda b, t: (b, t, 0)), sspec],
        out_shape=[jax.ShapeDtypeStruct((B, T, D), BF16),
                   jax.ShapeDtypeStruct((B, H, HG_DK, HG_DV), F32)],
        scratch_shapes=[pltpu.VMEM((H, HG_DV, HG_DK), F32)],
        compiler_params=_cparams(2),
        name="gla",
    )(*args)


def _route_kernel(h_ref, rw_ref, bias_ref, pos_ref, w_ref, te_ref, nu_ref,
                  e_s, r_s, base_s, start_s, *, tile_rows):
    ph = pl.program_id(0)
    i = pl.program_id(1)
    M = h_ref.shape[0]
    half = h_ref.shape[1]
    G, E = N_GROUPS, N_EXPERTS // N_GROUPS
    e_flat = lax.broadcasted_iota(jnp.int32, (N_EXPERTS, M), 0)

    @pl.when(ph == 1)
    def _():
        @pl.when(i == 0)
        def _():
            cnt = base_s[...]
            padded = jnp.floor((cnt + (tile_rows - 1)) * (1.0 / tile_rows)) * tile_rows
            r = lax.broadcasted_iota(jnp.int32, (N_EXPERTS, N_EXPERTS), 0)
            c = lax.broadcasted_iota(jnp.int32, (N_EXPERTS, N_EXPERTS), 1)
            start = jnp.dot((c < r).astype(F32), padded, preferred_element_type=F32,
                            precision=lax.Precision.HIGHEST)
            start_s[...] = start
            te_ref[...] = (start * (1.0 / tile_rows)).astype(jnp.int32)
            nu_ref[...] = (padded * (1.0 / tile_rows)).astype(jnp.int32)

        start_col = start_s[:, :1]
        for k in range(TOP_K):
            hit = e_flat == e_s[i, k:k + 1, :]
            seg = jnp.sum(jnp.where(hit, start_col, 0.0), axis=0, keepdims=True)
            pos_ref[k:k + 1, :] = (seg + r_s[i, k:k + 1, :]).astype(jnp.int32)

    @pl.when(ph == 0)
    def _():
        _route_pass0(h_ref, rw_ref, bias_ref, w_ref, e_s, r_s, base_s, i, M, half, G, E)


def _route_pass0(h_ref, rw_ref, bias_ref, w_ref, e_s, r_s, base_s, i, M, half, G, E):
    @pl.when(i == 0)
    def _():
        base_s[...] = jnp.zeros_like(base_s)

    lo, hi = _unpack_pairs(h_ref[...])
    rw = rw_ref[...].astype(BF16)
    logits = _dot_nt(rw[:, :half], lo) + _dot_nt(rw[:, half:], hi)
    s = jax.nn.sigmoid(logits)
    sb = (s + bias_ref[...]).reshape(G, E, M)
    s = s.reshape(G, E, M)
    e_in = lax.broadcasted_iota(jnp.int32, (G, E, M), 1).astype(F32)
    g_id = lax.broadcasted_iota(jnp.int32, (G, 1, M), 0)
    e_id = lax.broadcasted_iota(jnp.int32, (G, E, M), 0).astype(F32) * E + e_in

    def all_max(a):
        return jnp.max(jnp.max(a, axis=0, keepdims=True), axis=1, keepdims=True)

    def all_min(a):
        return jnp.min(jnp.min(a, axis=0, keepdims=True), axis=1, keepdims=True)

    def all_sum(a):
        return jnp.sum(jnp.sum(a, axis=0, keepdims=True), axis=1, keepdims=True)

    m1 = jnp.max(sb, axis=1, keepdims=True)
    first = jnp.min(jnp.where(sb == m1, e_in, float(E)), axis=1, keepdims=True)
    m2 = jnp.max(jnp.where(e_in == first, NEG_INF, sb), axis=1, keepdims=True)
    gs = m1 + m2

    rank = jnp.zeros((G, 1, M), jnp.int32)
    for j in range(G):
        gj = gs[j:j + 1]
        beats = (gj > gs) | ((gj == gs) & (j < g_id))
        rank = rank + beats.astype(jnp.int32)
    gsel = rank < TOPK_GROUPS

    vals = jnp.where(gsel, sb, NEG_INF)
    selm = jnp.zeros((G, E, M), F32)
    chosen, score = [], []
    for _ in range(TOP_K):
        m = all_max(vals)
        first = all_min(jnp.where(vals == m, e_id, float(N_EXPERTS)))
        hit = e_id == first
        score.append(all_sum(jnp.where(hit, s, 0.0)))
        selm = jnp.where(hit, 1.0, selm)
        vals = jnp.where(hit, NEG_INF, vals)
        chosen.append(first)

    tot = score[0]
    for sc in score[1:]:
        tot = tot + sc
    norm = ROUTED_SCALE / tot

    selm = selm.reshape(N_EXPERTS, M)
    earlier = (lax.broadcasted_iota(jnp.int32, (M, M), 0)
               < lax.broadcasted_iota(jnp.int32, (M, M), 1)).astype(BF16)
    rank = (base_s[:, :1] + _dot(selm.astype(BF16), earlier)).reshape(G, E, M)
    base_s[...] = base_s[...] + jnp.sum(selm, axis=1, keepdims=True)
    for k in range(TOP_K):
        hit = e_id == chosen[k]
        e_s[i, k:k + 1, :] = chosen[k].reshape(1, M).astype(jnp.int32)
        r_s[i, k:k + 1, :] = all_sum(jnp.where(hit, rank, 0.0)).reshape(1, M)
        w_ref[k:k + 1, :] = (score[k] * norm).reshape(1, M)


def route(hp, router_w_t, router_bias, l, tile_rows, rows=512):
    N, half = hp.shape
    M = rows
    nT = N // M
    assert N % M == 0

    def p0(ph, i):
        return i * (1 - ph) + (nT - 1) * ph

    return pl.pallas_call(
        functools.partial(_route_kernel, tile_rows=tile_rows),
        grid=(2, nT),
        in_specs=[pl.BlockSpec((M, half), lambda ph, i: (p0(ph, i), 0)),
                  pl.BlockSpec((None, N_EXPERTS, 2 * half), lambda ph, i: (l, 0, 0)),
                  pl.BlockSpec((None, N_EXPERTS, 1), lambda ph, i: (l, 0, 0))],
        out_specs=[pl.BlockSpec((TOP_K, M), lambda ph, i: (0, i * ph)),
                   pl.BlockSpec((TOP_K, M), lambda ph, i: (0, p0(ph, i))),
                   pl.BlockSpec((N_EXPERTS, 128), lambda ph, i: (0, 0)),
                   pl.BlockSpec((N_EXPERTS, 128), lambda ph, i: (0, 0))],
        out_shape=[jax.ShapeDtypeStruct((TOP_K, N), jnp.int32),
                   jax.ShapeDtypeStruct((TOP_K, N), F32),
                   jax.ShapeDtypeStruct((N_EXPERTS, 128), jnp.int32),
                   jax.ShapeDtypeStruct((N_EXPERTS, 128), jnp.int32)],
        scratch_shapes=[pltpu.VMEM((nT, TOP_K, M), jnp.int32), pltpu.VMEM((nT, TOP_K, M), F32),
                        pltpu.VMEM((N_EXPERTS, 128), F32), pltpu.VMEM((N_EXPERTS, 128), F32)],
        compiler_params=_cparams(2),
        name="route",
    )(hp, router_w_t, router_bias.reshape(-1, N_EXPERTS, 1))


def _sc_mesh():
    return plsc.VectorSubcoreMesh(core_axis_name="core", subcore_axis_name="subcore")


def sc_invert(pos_flat, n_tok, n_out):
    n = pos_flat.shape[0]
    per = n_out // SC_WORKERS
    chunk = n_tok
    assert n_out % SC_WORKERS == 0 and per % SC_LANES == 0
    assert n_tok % chunk == 0 and n % chunk == 0 and chunk % SC_LANES == 0
    cp = pltpu.CompilerParams()
    if "needs_layout_passes" in pltpu.CompilerParams.__dataclass_fields__:
        cp = dataclasses.replace(cp, needs_layout_passes=False)

    @functools.partial(
        pl.kernel, out_type=jax.ShapeDtypeStruct((n_out,), jnp.int32), mesh=_sc_mesh(),
        scratch_types=[pltpu.VMEM((chunk,), jnp.int32), pltpu.VMEM((per,), jnp.int32)],
        compiler_params=cp, name="sc_invert")
    def k(pos_hbm, src_hbm, pos_v, src_v):
        wid = lax.axis_index("subcore") * SC_CORES + lax.axis_index("core")
        lo = wid * per
        lane = lax.iota(jnp.int32, SC_LANES)

        @pl.loop(0, per, step=SC_LANES)
        def _(r):
            src_v[pl.ds(r, SC_LANES)] = lax.rem(lo + r + lane, n_tok)

        @pl.loop(0, n // chunk)
        def _(c):
            base = c * chunk
            pltpu.sync_copy(pos_hbm.at[pl.ds(base, chunk)], pos_v)
            tok0 = lax.rem(base, n_tok)

            @plsc.parallel_loop(0, chunk, step=SC_LANES, unroll=8)
            def _(r):
                p = pos_v[pl.ds(r, SC_LANES)] - lo
                mine = (p >= 0) & (p < per)
                plsc.store_scatter(src_v, [jnp.where(mine, p, 0)], tok0 + r + lane, mask=mine)

        pltpu.sync_copy(src_v, src_hbm.at[pl.ds(lo, per)])

    return k(pos_flat)


def sc_gather(x, idx):
    n = idx.shape[0]
    dim = x.shape[1]
    assert n % (SC_WINDOW * SC_WORKERS) == 0

    @functools.partial(
        pl.kernel, out_type=jax.ShapeDtypeStruct((n, dim), x.dtype), mesh=_sc_mesh(),
        scratch_types=[], name="sc_gather")
    def k(x_hbm, i_hbm, o_hbm):
        def body(i_vmem, o_vmem):
            pltpu.sync_copy(x_hbm.at[i_vmem.at[0]], o_vmem)

        pltpu.emit_pipeline(
            body, grid=(n // SC_WINDOW,),
            in_specs=[pl.BlockSpec((1, SC_WINDOW), index_map=lambda i: (i, 0))],
            out_specs=[pl.BlockSpec((SC_WINDOW, dim), index_map=lambda i: (i, 0))],
            core_axis_name=("core", "subcore"),
            dimension_semantics=(pltpu.PARALLEL,),
        )(i_hbm, o_hbm)

    return k(x, idx.reshape(n // SC_WINDOW, SC_WINDOW))


def _moe_gemm_kernel(ts_ref, tn_ref, x_hbm, wi_ref, wo_ref, o_hbm, wi_b, wo_b, xbuf, obuf, in_sem, out_sem,
                     *, tile_rows, n_tiles):
    e = pl.program_id(0)
    last = pl.num_programs(0) - 1
    t0 = ts_ref[e]
    n = tn_ref[e]
    n_used = ts_ref[last] + tn_ref[last]

    def x_copy(g, slot):
        rows = pl.ds(pl.multiple_of(g * tile_rows, tile_rows), tile_rows)
        return pltpu.make_async_copy(x_hbm.at[rows], xbuf.at[slot], in_sem.at[slot])

    def o_copy(g, slot):
        rows = pl.ds(pl.multiple_of(g * tile_rows, tile_rows), tile_rows)
        return pltpu.make_async_copy(obuf.at[slot], o_hbm.at[rows], out_sem.at[slot])

    @pl.when(e == 0)
    def _():
        for g0 in range(MOE_NBUF - 1):
            @pl.when(g0 < n_used)
            def _():
                x_copy(g0, g0).start()

    @pl.when(n > 0)
    def _():
        wi_b[...] = wi_ref[...].astype(BF16)
        wo_b[...] = wo_ref[...].astype(BF16)

    def tile(i, carry):
        g = t0 + i
        slot = lax.rem(g, MOE_NBUF)
        x_copy(g, slot).wait()
        ahead = g + (MOE_NBUF - 1)

        @pl.when(ahead < n_used)
        def _():
            x_copy(ahead, lax.rem(ahead, MOE_NBUF)).start()

        @pl.when(g >= MOE_NBUF)
        def _():
            o_copy(g - MOE_NBUF, slot).wait()

        rows = tile_rows // MOE_SUB
        half = xbuf.shape[2]
        xs = [_unpack_pairs(xbuf[slot, r * rows:(r + 1) * rows, :]) for r in range(MOE_SUB)]
        hus = [_dot(lo, wi_b[:half, :]) + _dot(hi, wi_b[half:, :]) for lo, hi in xs]
        acts = [(_silu(hu[:, :EXPERT_FF]) * hu[:, EXPERT_FF:]).astype(BF16) for hu in hus]
        outs = [_dot(act, wo_b[...]) for act in acts]
        for r, out in enumerate(outs):
            obuf[slot, r * rows:(r + 1) * rows, :] = _pack_pairs(out)
        o_copy(g, slot).start()
        return carry

    lax.fori_loop(0, n, tile, 0)

    @pl.when(e == last)
    def _():
        for back in range(MOE_NBUF, 0, -1):
            @pl.when(n_used >= back)
            def _():
                o_copy(n_used - back, lax.rem(n_used - back, MOE_NBUF)).wait()

        obuf[0] = jnp.zeros(obuf.shape[1:], obuf.dtype)

        def clear(g, carry):
            cp = o_copy(g, 0)
            cp.start()
            cp.wait()
            return carry

        lax.fori_loop(n_used, n_tiles, clear, 0)


def moe_gemm(xs, tile_start, tile_count, exp_w_in, exp_w_out, l, tile_rows, n_tiles):
    P, half = xs.shape
    D = 2 * half
    assert P == n_tiles * tile_rows
    hbm = pl.BlockSpec(memory_space=pl.ANY)
    grid_spec = pltpu.PrefetchScalarGridSpec(
        num_scalar_prefetch=2,
        grid=(N_EXPERTS,),
        in_specs=[hbm,
                  pl.BlockSpec((None, None, D, 2 * EXPERT_FF), lambda e, ts, tn: (l, e, 0, 0)),
                  pl.BlockSpec((None, None, EXPERT_FF, D), lambda e, ts, tn: (l, e, 0, 0))],
        out_specs=hbm,
        scratch_shapes=[pltpu.VMEM((D, 2 * EXPERT_FF), BF16), pltpu.VMEM((EXPERT_FF, D), BF16),
                        pltpu.VMEM((MOE_NBUF, tile_rows, half), jnp.int32),
                        pltpu.VMEM((MOE_NBUF, tile_rows, half), jnp.int32),
                        pltpu.SemaphoreType.DMA((MOE_NBUF,)), pltpu.SemaphoreType.DMA((MOE_NBUF,))],
    )
    return pl.pallas_call(
        functools.partial(_moe_gemm_kernel, tile_rows=tile_rows, n_tiles=n_tiles),
        grid_spec=grid_spec,
        out_shape=jax.ShapeDtypeStruct((P, half), jnp.int32),
        compiler_params=_cparams(1),
        name="moe_gemm",
    )(tile_start, tile_count, xs, exp_w_in, exp_w_out)


def _moe_combine_kernel(*refs, final, norm_next):
    if final:
        y_ref, w_ref, h_ref, si_ref, so_ref, x_ref, g2_ref, fg_ref, o_ref, si_b, so_b = refs
    elif norm_next:
        (y_ref, w_ref, h_ref, si_ref, so_ref, x_ref, g2_ref, ng_ref, nsc_ref, nsh_ref,
         o_ref, hn_ref, si_b, so_b) = refs
    else:
        y_ref, w_ref, h_ref, si_ref, so_ref, x_ref, g2_ref, o_ref, si_b, so_b = refs

    @pl.when(pl.program_id(0) == 0)
    def _():
        si_b[...] = si_ref[...].astype(BF16)
        so_b[...] = so_ref[...].astype(BF16)

    bb, tt, D = x_ref.shape
    half = D // 2
    w = w_ref[...]
    acc_lo = jnp.zeros((bb * tt, half), F32)
    acc_hi = jnp.zeros((bb * tt, half), F32)
    for k in range(TOP_K):
        lo, hi = _unpack_pairs(y_ref[k], F32)
        acc_lo = acc_lo + w[:, k:k + 1] * lo
        acc_hi = acc_hi + w[:, k:k + 1] * hi
    hlo, hhi = _unpack_pairs(h_ref[...])
    hu = _dot(hlo, si_b[:half, :]) + _dot(hhi, si_b[half:, :])
    act = (_silu(hu[:, :SHARED_FF]) * hu[:, SHARED_FF:]).astype(BF16)
    y = jnp.concatenate([acc_lo, acc_hi], axis=-1) + _dot(act, so_b[...])
    x_new = x_ref[...] + g2_ref[...] * y.reshape(bb, tt, D)
    o_ref[...] = _rms(x_new, fg_ref[...]) if final else x_new
    if norm_next:
        hn = _rms(x_new, ng_ref[...]) * (1.0 + nsc_ref[...]) + nsh_ref[...]
        hn_ref[...] = hn.astype(hn_ref.dtype)


def moe_combine(y8, w_t, hp, sh_w_in, sh_w_out, x, mod, gate_idx, l, row0, final_g=None, next_norm=None,
                rows=256):
    B, T, D = x.shape
    half = D // 2
    bb, tt, nblk, ij = _row_blocks(B, T, rows)
    M = bb * tt
    assert row0 % M == 0
    off = row0 // M
    xspec = pl.BlockSpec((bb, tt, D), lambda i: ij(i) + (0,))
    in_specs = [pl.BlockSpec((TOP_K, M, half), lambda i: (0, off + i, 0)),
                pl.BlockSpec((M, TOP_K), lambda i: (off + i, 0)),
                pl.BlockSpec((M, half), lambda i: (off + i, 0)),
                pl.BlockSpec((None, D, 2 * SHARED_FF), lambda i: (l, 0, 0)),
                pl.BlockSpec((None, SHARED_FF, D), lambda i: (l, 0, 0)),
                xspec,
                pl.BlockSpec((bb, 1, D), lambda i: (ij(i)[0], 0, gate_idx))]
    args = [y8, w_t, hp, sh_w_in, sh_w_out, x, mod]
    out_specs = xspec
    out_shape = jax.ShapeDtypeStruct((B, T, D), F32)
    if final_g is not None:
        assert next_norm is None
        in_specs.append(pl.BlockSpec((1, D), lambda i: (0, 0)))
        args.append(final_g.reshape(1, D))
    if next_norm is not None:
        gain, mod_next, sc_idx, sh_idx = next_norm
        in_specs += [pl.BlockSpec((1, D), lambda i: (0, 0)),
                     pl.BlockSpec((bb, 1, D), lambda i: (ij(i)[0], 0, sc_idx)),
                     pl.BlockSpec((bb, 1, D), lambda i: (ij(i)[0], 0, sh_idx))]
        args += [gain.reshape(1, D), mod_next, mod_next]
        out_specs = [xspec, xspec]
        out_shape = [out_shape, jax.ShapeDtypeStruct((B, T, D), BF16)]
    return pl.pallas_call(
        functools.partial(_moe_combine_kernel, final=final_g is not None, norm_next=next_norm is not None),
        grid=(nblk,),
        in_specs=in_specs,
        out_specs=out_specs,
        out_shape=out_shape,
        scratch_shapes=[pltpu.VMEM((D, 2 * SHARED_FF), BF16), pltpu.VMEM((SHARED_FF, D), BF16)],
        compiler_params=_cparams(1),
        name="moe_combine",
    )(*args)


def _shared_kv_kernel(x_ref, g_ref, w_ref, lg_ref, cos_ref, sin_ref, lat_ref, kr_ref):
    bb, tt, D = x_ref.shape
    xn = _rms(x_ref[...], g_ref[...]).reshape(bb * tt, D).astype(BF16)
    z = _dot(xn, w_ref[...].astype(BF16))
    lat = _rms(z[:, :KV_LORA], lg_ref[...])
    lat_ref[...] = lat.reshape(bb, tt, KV_LORA)
    zr = z[:, KV_LORA:KV_LORA + MLA_ROPE].reshape(bb, tt, MLA_ROPE)
    zq = z[:, KV_LORA + 128:KV_LORA + 128 + MLA_ROPE].reshape(bb, tt, MLA_ROPE)
    kr_ref[...] = zr * cos_ref[...] + zq * sin_ref[...]


def shared_kv(x, kv_in_g, w_kv, kv_lat_g, cos32, sin32, rows=512):
    B, T, D = x.shape
    bb, tt, nblk, ij = _row_blocks(B, T, rows)
    tspec = pl.BlockSpec((tt, MLA_ROPE), lambda i: (ij(i)[1], 0))
    return pl.pallas_call(
        _shared_kv_kernel,
        grid=(nblk,),
        in_specs=[pl.BlockSpec((bb, tt, D), lambda i: ij(i) + (0,)),
                  pl.BlockSpec((1, D), lambda i: (0, 0)),
                  pl.BlockSpec(w_kv.shape, lambda i: (0, 0)),
                  pl.BlockSpec((1, KV_LORA), lambda i: (0, 0)),
                  tspec, tspec],
        out_specs=[pl.BlockSpec((bb, tt, KV_LORA), lambda i: ij(i) + (0,)),
                   pl.BlockSpec((bb, tt, MLA_ROPE), lambda i: ij(i) + (0,))],
        out_shape=[jax.ShapeDtypeStruct((B, T, KV_LORA), F32),
                   jax.ShapeDtypeStruct((B, T, MLA_ROPE), F32)],
        compiler_params=_cparams(1),
        name="shared_kv",
    )(x, kv_in_g.reshape(1, D), w_kv, kv_lat_g.reshape(1, KV_LORA), cos32, sin32)


def _kv_expand_kernel(lat_ref, kr_ref, wk_ref, ek_ref, wvt_ref, ones_ref, k_ref, vt_ref):
    lat = lat_ref[0].astype(BF16)
    kr = kr_ref[0].astype(BF16)
    k = _dot(lat, wk_ref[...].astype(BF16)) + _dot(kr, ek_ref[...].astype(BF16))
    k_ref[0] = k.astype(k_ref.dtype)
    vt = _dot_nt(wvt_ref[...].astype(BF16), lat) + ones_ref[...]
    vt_ref[0] = vt.astype(vt_ref.dtype)


def kv_expand(lat, kr, wk_pad, ek, wvt_ext, ones_col, rows=512):
    B, T, _ = lat.shape
    tt = rows
    NK, NVT = wk_pad.shape[1], wvt_ext.shape[0]

    def full(a):
        return pl.BlockSpec(a.shape, lambda b, t: (0, 0))

    def rowspec(n):
        return pl.BlockSpec((1, tt, n), lambda b, t: (b, t, 0))

    return pl.pallas_call(
        _kv_expand_kernel,
        grid=(B, T // tt),
        in_specs=[rowspec(KV_LORA), rowspec(MLA_ROPE), full(wk_pad), full(ek), full(wvt_ext), full(ones_col)],
        out_specs=[rowspec(NK), pl.BlockSpec((1, NVT, tt), lambda b, t: (b, 0, t))],
        out_shape=[jax.ShapeDtypeStruct((B, T, NK), BF16), jax.ShapeDtypeStruct((B, NVT, T), BF16)],
        compiler_params=_cparams(2),
        name="kv_expand",
    )(lat, kr, wk_pad, ek, wvt_ext, ones_col)


def _query_kernel(h_ref, wdq_ref, qg_ref, wq_ref, wqr_ref, c_ref, s_ref, q_ref, wdq_b, wq_b, wqr_b):
    @pl.when(pl.program_id(0) == 0)
    def _():
        wdq_b[...] = wdq_ref[...].astype(BF16)
        wq_b[...] = wq_ref[...].astype(BF16)
        wqr_b[...] = wqr_ref[...].astype(BF16)

    bb, tt, D = h_ref.shape
    h = h_ref[...].reshape(bb * tt, D)
    cq = _rms(_dot(h, wdq_b[...]), qg_ref[...]).astype(BF16)
    q1 = _dot(cq, wq_b[...]).reshape(bb, tt, -1)
    q2 = _dot(cq, wqr_b[...]).reshape(bb, tt, -1)
    c = c_ref[...]
    s = s_ref[...]
    for hd in range(MLA_HEADS):
        sl = slice(hd * HEAD_PAD, (hd + 1) * HEAD_PAD)
        q_ref[:, :, sl] = (q1[:, :, sl] * c + q2[:, :, sl] * s).astype(q_ref.dtype)


def mla_queries(h, w_dq, q_norm_g, wq_pad, wq_rot, l, c128, s128, rows=512):
    B, T, D = h.shape
    bb, tt, nblk, ij = _row_blocks(B, T, rows)
    NQ = wq_pad.shape[-1]
    tspec = pl.BlockSpec((tt, HEAD_PAD), lambda i: (ij(i)[1], 0))
    return pl.pallas_call(
        _query_kernel,
        grid=(nblk,),
        in_specs=[pl.BlockSpec((bb, tt, D), lambda i: ij(i) + (0,)),
                  pl.BlockSpec((None, D, Q_LORA), lambda i: (l, 0, 0)),
                  pl.BlockSpec((None, 1, Q_LORA), lambda i: (l, 0, 0)),
                  pl.BlockSpec((None, Q_LORA, NQ), lambda i: (l, 0, 0)),
                  pl.BlockSpec((None, Q_LORA, NQ), lambda i: (l, 0, 0)),
                  tspec, tspec],
        out_specs=pl.BlockSpec((bb, tt, NQ), lambda i: ij(i) + (0,)),
        out_shape=jax.ShapeDtypeStruct((B, T, NQ), BF16),
        scratch_shapes=[pltpu.VMEM((D, Q_LORA), BF16), pltpu.VMEM((Q_LORA, NQ), BF16),
                        pltpu.VMEM((Q_LORA, NQ), BF16)],
        compiler_params=_cparams(1),
        name="mla_queries",
    )(h, w_dq, q_norm_g.reshape(-1, 1, Q_LORA), wq_pad, wq_rot, c128, s128)


def _query_t_kernel(h_ref, wdq_ref, qg_ref, wqt_ref, wqrt_ref, cos_ref, sin_ref, qt_ref, wdq_b, wqt_b, wqrt_b):
    @pl.when((pl.program_id(0) == 0) & (pl.program_id(1) == 0))
    def _():
        wdq_b[...] = wdq_ref[...].astype(BF16)
        wqt_b[...] = wqt_ref[...].astype(BF16)
        wqrt_b[...] = wqrt_ref[...].astype(BF16)

    cq = _rms(_dot(h_ref[0], wdq_b[...]), qg_ref[...]).astype(BF16)
    q1 = _dot_nt(wqt_b[...], cq)
    q2 = _dot_nt(wqrt_b[...], cq)
    cos = cos_ref[...]
    sin = sin_ref[...]
    pad = jnp.zeros((HEAD_PAD - MLA_NOPE - MLA_ROPE, q1.shape[1]), qt_ref.dtype)
    for hd in range(MLA_HEADS):
        r0 = hd * HEAD_PAD
        rope = (q1[r0 + MLA_NOPE:r0 + MLA_NOPE + MLA_ROPE] * cos
                + q2[hd * MLA_ROPE:(hd + 1) * MLA_ROPE] * sin)
        qt_ref[0, r0:r0 + MLA_NOPE, :] = (q1[r0:r0 + MLA_NOPE] * Q_PRESCALE).astype(qt_ref.dtype)
        qt_ref[0, r0 + MLA_NOPE:r0 + MLA_NOPE + MLA_ROPE, :] = rope.astype(qt_ref.dtype)
        qt_ref[0, r0 + MLA_NOPE + MLA_ROPE:r0 + HEAD_PAD, :] = pad


def mla_queries_t(h, w_dq, q_norm_g, wq_t, wqr_t, l, cos_t, sin_t, rows=512):
    B, T, D = h.shape
    tt = rows
    NQ = wq_t.shape[1]
    NR = wqr_t.shape[1]
    tspec = pl.BlockSpec((MLA_ROPE, tt), lambda b, t: (0, t))
    return pl.pallas_call(
        _query_t_kernel,
        grid=(B, T // tt),
        in_specs=[pl.BlockSpec((1, tt, D), lambda b, t: (b, t, 0)),
                  pl.BlockSpec((None, D, Q_LORA), lambda b, t: (l, 0, 0)),
                  pl.BlockSpec((None, 1, Q_LORA), lambda b, t: (l, 0, 0)),
                  pl.BlockSpec((None, NQ, Q_LORA), lambda b, t: (l, 0, 0)),
                  pl.BlockSpec((None, NR, Q_LORA), lambda b, t: (l, 0, 0)),
                  tspec, tspec],
        out_specs=pl.BlockSpec((1, NQ, tt), lambda b, t: (b, 0, t)),
        out_shape=jax.ShapeDtypeStruct((B, NQ, T), BF16),
        scratch_shapes=[pltpu.VMEM((D, Q_LORA), BF16), pltpu.VMEM((NQ, Q_LORA), BF16),
                        pltpu.VMEM((NR, Q_LORA), BF16)],
        compiler_params=_cparams(2),
        name="mla_queries_t",
    )(h, w_dq, q_norm_g.reshape(-1, 1, Q_LORA), wq_t, wqr_t, cos_t, sin_t)


def _attn_prompt_kernel(qi_tab, ki_tab, qt_ref, k_ref, vt_ref, o_ref, *scratch, tq, tk):
    H = MLA_HEADS
    m_refs, l_refs, acc_refs = scratch[:H], scratch[H:2 * H], scratch[2 * H:]
    p_id = pl.program_id(1)
    qi = qi_tab[p_id]
    ki = ki_tab[p_id]

    @pl.when(ki == 0)
    def _():
        for hd in range(H):
            m_refs[hd][...] = jnp.full(m_refs[hd].shape, NEG_INF, F32)
            l_refs[hd][...] = jnp.zeros(l_refs[hd].shape, F32)
            acc_refs[hd][...] = jnp.zeros(acc_refs[hd].shape, F32)

    def block(masked):
        if masked:
            kchunk = (ki * tk + lax.broadcasted_iota(jnp.int32, (tk, tq), 0)) // CHUNK
            qchunk = (qi * tq + lax.broadcasted_iota(jnp.int32, (tk, tq), 1)) // CHUNK
            mask = kchunk <= qchunk
        def scores(hd):
            sl = slice(hd * HEAD_PAD, (hd + 1) * HEAD_PAD)
            return _dot(k_ref[0, :, sl], qt_ref[0, sl, :])

        pending = [scores(hd) for hd in range(ATTN_LOOKAHEAD)]
        for hd in range(H):
            if hd + ATTN_LOOKAHEAD < H:
                pending.append(scores(hd + ATTN_LOOKAHEAD))
            s = pending.pop(0)
            if masked:
                s = jnp.where(mask, s, NEG_INF)
            m_prev = m_refs[hd][...]
            m_new = jnp.maximum(m_prev, jnp.max(s, axis=0, keepdims=True))
            a = jnp.exp2(m_prev - m_new)
            p = jnp.exp2(s - m_new).astype(BF16)
            pv = _dot(vt_ref[0, hd * V_ROWS:(hd + 1) * V_ROWS, :], p)
            acc_refs[hd][...] = a * acc_refs[hd][...] + pv[:MLA_V]
            l_refs[hd][...] = a * l_refs[hd][...] + pv[MLA_V:MLA_V + 1]
            m_refs[hd][...] = m_new

    @pl.when(ki < qi)
    def _():
        block(False)

    @pl.when(ki == qi)
    def _():
        block(True)
        o_t = jnp.concatenate([acc_refs[hd][...] / l_refs[hd][...] for hd in range(H)], axis=0)
        o_ref[0] = o_t.T.astype(o_ref.dtype)


def attn_prompt(qt, k, vt, tq=512):
    B, NQ, T = qt.shape
    NVT = vt.shape[1]
    NV = MLA_HEADS * MLA_V
    tk = tq
    assert tq % CHUNK == 0
    nq = T // tq
    pairs = [(a, b) for a in range(nq) for b in range(a + 1)]
    qi_tab = jnp.asarray([a for a, _ in pairs], jnp.int32)
    ki_tab = jnp.asarray([b for _, b in pairs], jnp.int32)
    grid_spec = pltpu.PrefetchScalarGridSpec(
        num_scalar_prefetch=2,
        grid=(B, len(pairs)),
        in_specs=[pl.BlockSpec((1, NQ, tq), lambda b, p, qt, kt: (b, 0, qt[p])),
                  pl.BlockSpec((1, tk, NQ), lambda b, p, qt, kt: (b, kt[p], 0)),
                  pl.BlockSpec((1, NVT, tk), lambda b, p, qt, kt: (b, 0, kt[p]))],
        out_specs=pl.BlockSpec((1, tq, NV), lambda b, p, qt, kt: (b, qt[p], 0)),
        scratch_shapes=([pltpu.VMEM((1, tq), F32)] * (2 * MLA_HEADS)
                        + [pltpu.VMEM((MLA_V, tq), F32)] * MLA_HEADS),
    )
    return pl.pallas_call(
        functools.partial(_attn_prompt_kernel, tq=tq, tk=tk),
        grid_spec=grid_spec,
        out_shape=jax.ShapeDtypeStruct((B, T, NV), BF16),
        compiler_params=_cparams(2),
        name="attn_prompt",
    )(qi_tab, ki_tab, qt, k, vt)


def _absorb_kernel(q_ref, m_ref, o_ref):
    o_ref[...] = _dot(q_ref[...], m_ref[...].astype(BF16)).astype(o_ref.dtype)


def absorb_queries(q2d, m_abs):
    N = q2d.shape[0]
    H, _, W = m_abs.shape
    return pl.pallas_call(
        _absorb_kernel,
        grid=(H,),
        in_specs=[pl.BlockSpec((N, HEAD_PAD), lambda h: (0, h)),
                  pl.BlockSpec((None, HEAD_PAD, W), lambda h: (h, 0, 0))],
        out_specs=pl.BlockSpec((None, N, W), lambda h: (h, 0, 0)),
        out_shape=jax.ShapeDtypeStruct((H, N, W), BF16),
        compiler_params=_cparams(1),
        name="absorb_queries",
    )(q2d, m_abs)


def _attn_sample_kernel(q_ref, lat_ref, kr_ref, nlat_ref, nkr_ref, o_ref, m_ref, l_ref, acc_ref):
    kb = pl.program_id(1)
    H, Q, W = q_ref.shape
    q = q_ref[...].reshape(H * Q, W)
    q_lat = q[:, :KV_LORA]
    q_rope = q[:, KV_LORA:KV_LORA + MLA_ROPE]

    def update(lat_tile, kr_tile, n_sub, kr_transposed):
        sub = lat_tile.shape[0] // n_sub
        lats = [lat_tile[j * sub:(j + 1) * sub, :].astype(BF16) for j in range(n_sub)]
        if kr_transposed:
            krs = [kr_tile[:, j * sub:(j + 1) * sub].astype(BF16) for j in range(n_sub)]
            ss = [_dot_nt(q_lat, lat) + _dot(q_rope, kr) for lat, kr in zip(lats, krs)]
        else:
            krs = [kr_tile[j * sub:(j + 1) * sub, :].astype(BF16) for j in range(n_sub)]
            ss = [_dot_nt(q_lat, lat) + _dot_nt(q_rope, kr) for lat, kr in zip(lats, krs)]
        m_prev = m_ref[...]
        m_new = m_prev
        for s in ss:
            m_new = jnp.maximum(m_new, jnp.max(s, axis=-1, keepdims=True))
        a = jnp.exp2(m_prev - m_new)
        ps = [jnp.exp2(s - m_new[:, :1]) for s in ss]
        pv = _dot(ps[0].astype(BF16), lats[0])
        psum = jnp.sum(ps[0], axis=-1, keepdims=True)
        for p, lat in zip(ps[1:], lats[1:]):
            pv = pv + _dot(p.astype(BF16), lat)
            psum = psum + jnp.sum(p, axis=-1, keepdims=True)
        l_ref[...] = a * l_ref[...] + psum
        m_ref[...] = m_new
        acc_ref[...] = jnp.concatenate([a, a], axis=-1) * acc_ref[...] + pv

    @pl.when(kb == 0)
    def _():
        m_ref[...] = jnp.full_like(m_ref, NEG_INF)
        l_ref[...] = jnp.zeros_like(l_ref)
        acc_ref[...] = jnp.zeros_like(acc_ref)
        update(nlat_ref[0], nkr_ref[0], 1, False)

    update(lat_ref[0], kr_ref[0], SAMPLE_KEY_SUB, True)

    @pl.when(kb == pl.num_programs(1) - 1)
    def _():
        lsum = l_ref[...]
        o = acc_ref[...] / jnp.concatenate([lsum, lsum], axis=-1)
        o_ref[...] = o.reshape(H, Q, KV_LORA).astype(o_ref.dtype)


def attn_sample(q_abs, cache_lat, cache_kr_t, new_lat, new_kr, tk=4096):
    H, N, W = q_abs.shape
    B, P, _ = cache_lat.shape
    Q = new_lat.shape[1]
    qpos = P + np.arange(Q)
    kpos = np.arange(P + Q)
    assert bool(np.all((kpos // CHUNK)[None, :] <= (qpos // CHUNK)[:, None]))
    return pl.pallas_call(
        _attn_sample_kernel,
        grid=(B, P // tk),
        in_specs=[pl.BlockSpec((H, Q, W), lambda b, kb: (0, b, 0)),
                  pl.BlockSpec((1, tk, KV_LORA), lambda b, kb: (b, kb, 0)),
                  pl.BlockSpec((1, MLA_ROPE, tk), lambda b, kb: (b, 0, kb)),
                  pl.BlockSpec((1, Q, KV_LORA), lambda b, kb: (b, 0, 0)),
                  pl.BlockSpec((1, Q, MLA_ROPE), lambda b, kb: (b, 0, 0))],
        out_specs=pl.BlockSpec((H, Q, KV_LORA), lambda b, kb: (0, b, 0)),
        out_shape=jax.ShapeDtypeStruct((H, N, KV_LORA), BF16),
        scratch_shapes=[pltpu.VMEM((H * Q, 128), F32), pltpu.VMEM((H * Q, 128), F32),
                        pltpu.VMEM((H * Q, KV_LORA), F32)],
        compiler_params=_cparams(2),
        name="attn_sample",
    )(q_abs, cache_lat, cache_kr_t, new_lat, new_kr)


def _unabsorb_kernel(o_ref, w_ref, out_ref):
    out_ref[...] = (_dot(o_ref[0], w_ref[0].astype(BF16))
                    + _dot(o_ref[1], w_ref[1].astype(BF16))).astype(out_ref.dtype)


def unabsorb(o_lat, wuv_pad):
    H, N, R = o_lat.shape
    return pl.pallas_call(
        _unabsorb_kernel,
        grid=(H // 2,),
        in_specs=[pl.BlockSpec((2, N, R), lambda p: (p, 0, 0)),
                  pl.BlockSpec((2, R, 128), lambda p: (p, 0, 0))],
        out_specs=pl.BlockSpec((N, 128), lambda p: (0, p)),
        out_shape=jax.ShapeDtypeStruct((N, (H // 2) * 128), BF16),
        compiler_params=_cparams(1),
        name="unabsorb",
    )(o_lat, wuv_pad)


def _rope_tables(pos):
    half = MLA_ROPE // 2
    inv = 1.0 / (ROPE_THETA ** (np.arange(half, dtype=np.float64) * 2.0 / MLA_ROPE))
    ang = np.asarray(pos, np.float64)[:, None] * inv[None, :]
    cos = np.concatenate([np.cos(ang), np.cos(ang)], axis=-1)
    sin = np.concatenate([np.sin(ang), np.sin(ang)], axis=-1)
    T = cos.shape[0]
    c128 = np.zeros((T, HEAD_PAD)); s128 = np.zeros((T, HEAD_PAD))
    c128[:, :MLA_NOPE] = 1.0
    c128[:, MLA_NOPE:MLA_NOPE + MLA_ROPE] = cos
    s128[:, MLA_NOPE:MLA_NOPE + MLA_ROPE] = sin
    return dict(cos32=jnp.asarray(cos, F32), sin32=jnp.asarray(sin, F32),
                c128=jnp.asarray(c128 * Q_PRESCALE, F32), s128=jnp.asarray(s128 * Q_PRESCALE, F32),
                cos_t=jnp.asarray(cos.T * Q_PRESCALE, F32), sin_t=jnp.asarray(sin.T * Q_PRESCALE, F32))


def _rot_half_cols(w):
    half = w.shape[-1] // 2
    return jnp.concatenate([-w[..., half:], w[..., :half]], axis=-1)


def _prep_weights(w_dkv, w_uk, w_uv, w_uq, router_w):
    D = D_MODEL
    w_lat, w_rope = w_dkv[:, :KV_LORA], w_dkv[:, KV_LORA:]
    pad96 = jnp.zeros((D, 128 - MLA_ROPE), F32)
    w_kv = jnp.concatenate([w_lat, w_rope, pad96, _rot_half_cols(w_rope), pad96], axis=-1)

    zpad = HEAD_PAD - MLA_NOPE
    wk_pad = jnp.pad(w_uk, ((0, 0), (0, 0), (0, zpad))).reshape(KV_LORA, MLA_HEADS * HEAD_PAD)
    ek = jnp.zeros((MLA_ROPE, MLA_HEADS, HEAD_PAD), F32)
    ek = ek.at[:, :, MLA_NOPE:MLA_NOPE + MLA_ROPE].set(
        jnp.broadcast_to(jnp.eye(MLA_ROPE, dtype=F32)[:, None, :], (MLA_ROPE, MLA_HEADS, MLA_ROPE)))
    ek = ek.reshape(MLA_ROPE, MLA_HEADS * HEAD_PAD)
    wvt = jnp.transpose(w_uv, (1, 2, 0))
    wvt_ext = jnp.pad(wvt, ((0, 0), (0, V_ROWS - MLA_V), (0, 0))).reshape(MLA_HEADS * V_ROWS, KV_LORA)
    ones_col = jnp.tile((jnp.arange(V_ROWS) >= MLA_V).astype(F32), MLA_HEADS).reshape(-1, 1)

    nb = w_uq.shape[0]
    qn, qr = w_uq[..., :MLA_NOPE], w_uq[..., MLA_NOPE:]
    z32 = jnp.zeros(qr.shape[:-1] + (HEAD_PAD - MLA_NOPE - MLA_ROPE,), F32)
    wq_pad = jnp.concatenate([qn, qr, z32], axis=-1).reshape(nb, Q_LORA, MLA_HEADS * HEAD_PAD)
    wq_rot = jnp.concatenate([jnp.zeros_like(qn), _rot_half_cols(qr), z32], axis=-1)
    wq_rot = wq_rot.reshape(nb, Q_LORA, MLA_HEADS * HEAD_PAD)
    wq_t = jnp.transpose(wq_pad, (0, 2, 1))
    wqr_t = jnp.transpose(_rot_half_cols(qr).reshape(nb, Q_LORA, MLA_HEADS * MLA_ROPE), (0, 2, 1))

    m_abs = jnp.zeros((MLA_HEADS, HEAD_PAD, KV_LORA + 128), F32)
    m_abs = m_abs.at[:, :MLA_NOPE, :KV_LORA].set(jnp.transpose(w_uk, (1, 2, 0)))
    m_abs = m_abs.at[:, MLA_NOPE:MLA_NOPE + MLA_ROPE, KV_LORA:KV_LORA + MLA_ROPE].set(
        jnp.broadcast_to(jnp.eye(MLA_ROPE, dtype=F32), (MLA_HEADS, MLA_ROPE, MLA_ROPE)))

    wuv_h = jnp.transpose(w_uv, (1, 0, 2))
    even = jnp.pad(wuv_h, ((0, 0), (0, 0), (0, 64)))
    odd = jnp.pad(wuv_h, ((0, 0), (0, 0), (64, 0)))
    wuv_pad = jnp.where((jnp.arange(MLA_HEADS) % 2 == 0)[:, None, None], even, odd)

    rw_t = jnp.transpose(router_w, (0, 2, 1))
    return dict(w_kv=w_kv, wk_pad=wk_pad, ek=ek, wvt_ext=wvt_ext, ones_col=ones_col, wq_pad=wq_pad, wq_rot=wq_rot, wq_t=wq_t, wqr_t=wqr_t,
                m_abs=m_abs, wuv_pad=wuv_pad, rw_t=rw_t)


def _mixer(st, l, P, W, packed):
    rows_kw = dict(rows_total=packed["total"], row0=packed["row0"], rows_buf=packed["buf"])
    x, m = st["x"], st["mod"][l]
    B, T, _ = x.shape
    n_a = P["hg_w_in"].shape[0]
    h = st.pop("h_next", None)
    if h is None:
        h = norm_mod(x, P["norm1_g"][l], m, sc_idx=1, sh_idx=0)
    norm2 = (P["norm2_g"][l], 4, 3)
    if l < n_a:
        rows = min(1024, B * T)
        zf = linear(h, P["hg_w_in"], l, F32, col_blocks=(1,), rows=rows)
        zqig = linear(h, P["hg_w_in"], l, BF16, col_blocks=(0, 2, 3), rows=rows)
        s0 = None if st["hg_state"] is None else st["hg_state"][l]
        o, s_new = gla(zqig, zf, st["lbs"][l], P["hg_onorm_g"][l], s0)
        st["hg_new"].append(s_new)
        st["x"], packed["buf"] = linear(o, P["hg_w_out"], l, F32, x=x, mod=m, gate_idx=2, next_norm=norm2,
                                        **rows_kw)
    else:
        bi = l - n_a
        if st["past_lat"] is None:
            qt = mla_queries_t(h, P["w_dq"], P["q_norm_g"], W["wq_t"], W["wqr_t"], bi, st["cos_t"], st["sin_t"])
            o = attn_prompt(qt, st["k_all"], st["v_all"])
        else:
            q = mla_queries(h, P["w_dq"], P["q_norm_g"], W["wq_pad"], W["wq_rot"], bi, st["c128"], st["s128"])
            q_abs = absorb_queries(q.reshape(B * T, -1), W["m_abs"])
            o_lat = attn_sample(q_abs, st["past_lat"], st["past_kr"], st["lat"], st["kr"])
            o = unabsorb(o_lat, W["wuv_pad"]).reshape(B, T, -1)
        st["x"], packed["buf"] = linear(o, P["w_o"], bi, F32, x=x, mod=m, gate_idx=2, next_norm=norm2, **rows_kw)
    packed["row0"] += B * T


def _moe(groups, hp, l, P, W):
    n_tok = hp.shape[0]
    n_tiles = (TOP_K * n_tok) // MOE_TILE + N_EXPERTS
    pos, w8, tile_start, tile_count = route(hp, W["rw_t"], P["router_bias"], l, MOE_TILE)
    pos_flat = pos.reshape(-1)
    src = sc_invert(pos_flat, n_tok, n_tiles * MOE_TILE)
    xs = sc_gather(hp, src)
    out = moe_gemm(xs, tile_start[:, 0], tile_count[:, 0], P["exp_w_in"], P["exp_w_out"], l,
                   MOE_TILE, n_tiles)
    y8 = sc_gather(out, pos_flat).reshape(TOP_K, n_tok, -1)
    w_t = w8.T
    row0 = 0
    for st in groups:
        B, T, _ = st["x"].shape
        if l == P["norm1_g"].shape[0] - 1:
            st["x"] = moe_combine(y8, w_t, hp, P["sh_w_in"], P["sh_w_out"], st["x"], st["mod"][l], 5, l, row0,
                                  final_g=P["final_g"])
        else:
            nxt = (P["norm1_g"][l + 1], st["mod"][l + 1], 1, 0)
            st["x"], st["h_next"] = moe_combine(y8, w_t, hp, P["sh_w_in"], P["sh_w_out"], st["x"],
                                                st["mod"][l], 5, l, row0, next_norm=nxt)
        row0 += B * T


def _group_state(x, mod, pos, hg_state, past_lat, past_kr, lbs):
    return dict(x=x, mod=mod, hg_state=hg_state, past_lat=past_lat, past_kr=past_kr, lbs=lbs,
                **_rope_tables(pos), hg_new=[],
                lat=None, kr=None, k_all=None, v_all=None)


def kernel(x_prompt, x_sample, state_hgrn, cache_mla_latent, cache_mla_krope, c_prompt, c_sample, ada_w, ada_b, norm1_g, norm2_g, hg_w_in, hg_lb_logits, hg_onorm_g, hg_w_out, kv_in_g, w_dkv, kv_lat_g, w_uk, w_uv, w_dq, q_norm_g, w_uq, w_o, router_w, router_bias, exp_w_in, exp_w_out, sh_w_in, sh_w_out, final_g):
    Bp, Sp, _ = x_prompt.shape
    Bs, Ss, _ = x_sample.shape
    past = cache_mla_latent.shape[1]
    P = dict(norm1_g=norm1_g, norm2_g=norm2_g, hg_w_in=hg_w_in, hg_lb_logits=hg_lb_logits,
             hg_onorm_g=hg_onorm_g, hg_w_out=hg_w_out, kv_in_g=kv_in_g, kv_lat_g=kv_lat_g,
             w_dq=w_dq, q_norm_g=q_norm_g, w_o=w_o, router_bias=router_bias,
             exp_w_in=exp_w_in, exp_w_out=exp_w_out, sh_w_in=sh_w_in, sh_w_out=sh_w_out, final_g=final_g)
    W = _prep_weights(w_dkv, w_uk, w_uv, w_uq, router_w)
    mod = ada_mod(jnp.concatenate([c_prompt, c_sample], axis=0), ada_w, ada_b)
    lbs = jnp.cumsum(jax.nn.softmax(hg_lb_logits.astype(F32), axis=0), axis=0)
    gp = _group_state(x_prompt, mod[:, :Bp, None, :], np.arange(Sp), None, None, None, lbs)
    gs = _group_state(x_sample, mod[:, Bp:, None, :], past + np.arange(Ss), state_hgrn,
                      cache_mla_latent, jnp.transpose(cache_mla_krope, (0, 2, 1)), lbs)
    groups = [gp, gs]
    n_tok = sum(st["x"].shape[0] * st["x"].shape[1] for st in groups)
    n_a = hg_w_in.shape[0]
    for l in range(norm1_g.shape[0]):
        packed = dict(total=n_tok, row0=0, buf=None)
        for st in groups:
            _mixer(st, l, P, W, packed)
        _moe(groups, packed["buf"], l, P, W)
        if l == n_a - 1:
            for st in groups:
                st["lat"], st["kr"] = shared_kv(st["x"], kv_in_g, W["w_kv"], kv_lat_g, st["cos32"], st["sin32"])
            gp["k_all"], gp["v_all"] = kv_expand(gp["lat"], gp["kr"], W["wk_pad"], W["ek"], W["wvt_ext"],
                                                 W["ones_col"])
    return (gp["x"], gs["x"], jnp.stack(gp["hg_new"], axis=0), jnp.stack(gs["hg_new"], axis=0),
            gp["lat"], gp["kr"], gs["lat"], gs["kr"])
```

```python
import dataclasses
import functools

import numpy as np
import jax
import jax.numpy as jnp
from jax import lax
from jax.experimental import pallas as pl
from jax.experimental.pallas import tpu as pltpu
from jax.experimental.pallas import tpu_sc as plsc

F32 = jnp.float32
BF16 = jnp.bfloat16

D_MODEL = 1024
CHUNK = 64
HG_HEADS = 8
HG_DK = 128
HG_DV = 128
MLA_HEADS = 16
MLA_NOPE = 64
MLA_ROPE = 32
MLA_V = 64
Q_LORA = 384
KV_LORA = 256
ROPE_THETA = 10000.0
N_EXPERTS = 64
TOP_K = 8
N_GROUPS = 8
TOPK_GROUPS = 4
EXPERT_FF = 256
SHARED_FF = 256
ROUTED_SCALE = 2.5
EPS = 1e-6

HEAD_PAD = 128
SAMPLE_KEY_SUB = 8
ATTN_LOOKAHEAD = 8
V_ROWS = MLA_V + 16
QK_SCALE = (MLA_NOPE + MLA_ROPE) ** -0.5
Q_PRESCALE = QK_SCALE * float(np.log2(np.e))
VMEM_LIMIT = 56 * 1024 * 1024
NEG_INF = float("-inf")
SC_CORES = 2
SC_SUBCORES = 16
SC_WORKERS = SC_CORES * SC_SUBCORES
SC_LANES = 16
SC_WINDOW = 64
MOE_TILE = 512
MOE_NBUF = 4
MOE_SUB = 1


def _cparams(n_axes):
    return pltpu.CompilerParams(dimension_semantics=("arbitrary",) * n_axes,
                                vmem_limit_bytes=VMEM_LIMIT)


def _silu(x):
    return x * jax.nn.sigmoid(x)


def _rms(x, g):
    ms = jnp.mean(x * x, axis=-1, keepdims=True)
    return x * lax.rsqrt(ms + EPS) * g


def _dot(a, b):
    return jnp.dot(a, b, preferred_element_type=F32)


def _dot_nt(a, b):
    return lax.dot_general(a, b, (((1,), (1,)), ((), ())), preferred_element_type=F32)


def _dot_tn(a, b):
    return lax.dot_general(a, b, (((0,), (0,)), ((), ())), preferred_element_type=F32)


def _row_blocks(B, T, rows):
    if T >= rows:
        assert T % rows == 0
        bb, tt = 1, rows
    else:
        assert rows % T == 0 and B % (rows // T) == 0
        bb, tt = rows // T, T
    nt = T // tt
    return bb, tt, (B // bb) * nt, (lambda i: (i // nt, i % nt))


def _ada_kernel(c_ref, w_ref, b_ref, o_ref):
    a = _silu(c_ref[...]).astype(BF16)
    o_ref[...] = _dot(a, w_ref[...].astype(BF16)) + b_ref[...]


def ada_mod(c, ada_w, ada_b):
    R, D = c.shape
    L, _, N = ada_w.shape
    tn = 1536
    return pl.pallas_call(
        _ada_kernel,
        grid=(L, N // tn),
        in_specs=[pl.BlockSpec((R, D), lambda l, j: (0, 0)),
                  pl.BlockSpec((None, D, tn), lambda l, j: (l, 0, j)),
                  pl.BlockSpec((None, 1, tn), lambda l, j: (l, 0, j))],
        out_specs=pl.BlockSpec((None, R, tn), lambda l, j: (l, 0, j)),
        out_shape=jax.ShapeDtypeStruct((L, R, N), F32),
        compiler_params=_cparams(2),
        name="ada_mod",
    )(c, ada_w, ada_b.reshape(L, 1, N))


def _pack_pairs(y):
    half = y.shape[-1] // 2
    bits = lax.bitcast_convert_type(y.astype(BF16).astype(F32), jnp.uint32)
    word = lax.shift_right_logical(bits[:, :half], jnp.uint32(16)) | bits[:, half:]
    return lax.bitcast_convert_type(word, jnp.int32)


def _unpack_pairs(word, dtype=BF16):
    u = lax.bitcast_convert_type(word, jnp.uint32)
    lo = lax.bitcast_convert_type(lax.shift_left(u, jnp.uint32(16)), F32)
    hi = lax.bitcast_convert_type(u & jnp.uint32(0xFFFF0000), F32)
    return lo.astype(dtype), hi.astype(dtype)


def _norm_kernel(x_ref, g_ref, sc_ref, sh_ref, o_ref):
    y = _rms(x_ref[...], g_ref[...]) * (1.0 + sc_ref[...]) + sh_ref[...]
    o_ref[...] = y.astype(o_ref.dtype)


def norm_mod(x, g, mod, sc_idx, sh_idx, rows=512):
    B, T, D = x.shape
    bb, tt, nblk, ij = _row_blocks(B, T, rows)
    xspec = pl.BlockSpec((bb, tt, D), lambda i: ij(i) + (0,))
    return pl.pallas_call(
        _norm_kernel,
        grid=(nblk,),
        in_specs=[xspec, pl.BlockSpec((1, D), lambda i: (0, 0)),
                  pl.BlockSpec((bb, 1, D), lambda i: (ij(i)[0], 0, sc_idx)),
                  pl.BlockSpec((bb, 1, D), lambda i: (ij(i)[0], 0, sh_idx))],
        out_specs=xspec,
        out_shape=jax.ShapeDtypeStruct((B, T, D), BF16),
        compiler_params=_cparams(1),
        name="norm_mod",
    )(x, g.reshape(1, D), mod, mod)


def _linear_kernel(*refs, residual, norm_next, shared_rows, n_main):
    if norm_next and shared_rows:
        a_ref, w_ref, x_ref, gate_ref, ng_ref, nsc_ref, nsh_ref, _, o_ref, hp_ref, wb_ref = refs
    elif norm_next:
        a_ref, w_ref, x_ref, gate_ref, ng_ref, nsc_ref, nsh_ref, o_ref, hp_ref, wb_ref = refs
    elif residual:
        a_ref, w_ref, x_ref, gate_ref, o_ref, wb_ref = refs
    else:
        a_ref, w_ref, o_ref, wb_ref = refs

    @pl.when(pl.program_id(1) == 0)
    def _():
        wb_ref[...] = w_ref[...].astype(BF16)

    def main():
        bb, tt, K = a_ref.shape
        y = _dot(a_ref[...].reshape(bb * tt, K).astype(BF16), wb_ref[...])
        y = y.reshape(bb, tt, y.shape[-1])
        if residual:
            y = x_ref[...] + gate_ref[...] * y
        o_ref[...] = y.astype(o_ref.dtype)
        if norm_next:
            h = _rms(y, ng_ref[...]) * (1.0 + nsc_ref[...]) + nsh_ref[...]
            hp_ref[...] = _pack_pairs(h.reshape(bb * tt, h.shape[-1]))

    if n_main is None:
        main()
    else:
        pl.when(pl.program_id(1) < n_main)(main)

        @pl.when(pl.program_id(1) >= n_main)
        def _():
            hp_ref[...] = jnp.zeros(hp_ref.shape, hp_ref.dtype)


def linear(a, w, l, out_dtype, x=None, mod=None, gate_idx=0, rows=512, tn=1024, next_norm=None,
           rows_total=None, row0=0, rows_buf=None, col_blocks=None):
    B, T, K = a.shape
    _, _, N = w.shape
    tn = min(tn, N)
    if col_blocks is None:
        def wcol(j):
            return j
    else:
        first, skip_from = col_blocks[0], [c - k for k, c in enumerate(col_blocks)]
        gap_at = next((k for k, d in enumerate(skip_from) if d != first), len(col_blocks))
        assert all(d == first for d in skip_from[:gap_at]) and all(d == first + 1 for d in skip_from[gap_at:])
        N = len(col_blocks) * tn

        def wcol(j):
            return j + first + (j >= gap_at)
    bb, tt, nblk, ij0 = _row_blocks(B, T, rows)
    n_extra = 0
    if next_norm is not None and rows_buf is None and rows_total is not None:
        assert row0 == 0 and (rows_total - B * T) % (bb * tt) == 0
        n_extra = (rows_total - B * T) // (bb * tt)

    def ij(i):
        return ij0(jnp.minimum(i, nblk - 1)) if n_extra else ij0(i)

    in_specs = [pl.BlockSpec((bb, tt, K), lambda j, i: ij(i) + (0,)),
                pl.BlockSpec((None, K, tn), lambda j, i: (l, 0, wcol(j)))]
    args = [a, w]
    ospec = pl.BlockSpec((bb, tt, tn), lambda j, i: ij(i) + (j,))
    out_specs = ospec
    out_shape = jax.ShapeDtypeStruct((B, T, N), out_dtype)
    aliases = {}
    if x is not None:
        gsteps = D_MODEL // tn
        in_specs += [ospec, pl.BlockSpec((bb, 1, tn), lambda j, i: (ij(i)[0], 0, gate_idx * gsteps + j))]
        args += [x, mod]
    if next_norm is not None:
        assert x is not None and tn == N
        gain, sc_idx, sh_idx = next_norm
        in_specs += [pl.BlockSpec((1, N), lambda j, i: (0, 0)),
                     pl.BlockSpec((bb, 1, N), lambda j, i: (ij(i)[0], 0, sc_idx)),
                     pl.BlockSpec((bb, 1, N), lambda j, i: (ij(i)[0], 0, sh_idx))]
        args += [gain.reshape(1, N), mod, mod]
        assert row0 % (bb * tt) == 0
        off = row0 // (bb * tt)
        out_specs = [ospec, pl.BlockSpec((bb * tt, N // 2), lambda j, i: (off + i, 0))]
        out_shape = [out_shape, jax.ShapeDtypeStruct((rows_total or B * T, N // 2), jnp.int32)]
        if rows_buf is not None:
            in_specs.append(pl.BlockSpec(memory_space=pl.ANY))
            args.append(rows_buf)
            aliases = {len(args) - 1: 1}
    return pl.pallas_call(
        functools.partial(_linear_kernel, residual=x is not None, norm_next=next_norm is not None,
                          shared_rows=rows_buf is not None, n_main=nblk if n_extra else None),
        grid=(N // tn, nblk + n_extra),
        in_specs=in_specs,
        out_specs=out_specs,
        out_shape=out_shape,
        scratch_shapes=[pltpu.VMEM((K, tn), BF16)],
        input_output_aliases=aliases,
        compiler_params=_cparams(2),
        name="linear",
    )(*args)


def _gla_kernel(*refs, L, n_chunks, has_init):
    if has_init:
        q_ref, f_ref, i_ref, g_ref, lb_ref, on_ref, s0_ref, o_ref, so_ref, st_ref = refs
    else:
        q_ref, f_ref, i_ref, g_ref, lb_ref, on_ref, o_ref, so_ref, st_ref = refs
    t = pl.program_id(1)
    H = st_ref.shape[0]

    @pl.when(t == 0)
    def _():
        for h in range(H):
            if has_init:
                st_ref[h] = s0_ref[0, h].T
            else:
                st_ref[h] = jnp.zeros(st_ref.shape[1:], F32)

    lb = lb_ref[...]
    onorm = on_ref[...]
    row = lax.broadcasted_iota(jnp.int32, (L, L), 0)
    col = lax.broadcasted_iota(jnp.int32, (L, L), 1)
    causal = col <= row
    tri = causal.astype(BF16)

    def chunk(c, carry):
        rows = pl.ds(pl.multiple_of(c * L, L), L)

        def write_o(sl, o):
            o_ref[0, rows, sl] = o.astype(o_ref.dtype)

        _gla_chunk(q_ref[0, rows, :], f_ref[0, rows, :], i_ref[0, rows, :], g_ref[0, rows, :],
                   lb, onorm, tri, causal, st_ref, write_o)
        return carry

    lax.fori_loop(0, n_chunks, chunk, 0, unroll=4 if n_chunks % 4 == 0 else 1)

    @pl.when(t == pl.num_programs(1) - 1)
    def _():
        for h in range(H):
            so_ref[0, h] = st_ref[h].T


def _gla_chunk(q, f, v, g, lb, onorm, tri, causal, st_ref, write_o):
    L = q.shape[0]
    H = st_ref.shape[0]
    mid = L // 2 - 1
    q = _silu(q.astype(F32))
    fg = lb + (1.0 - lb) * jax.nn.sigmoid(f)
    k = 1.0 - fg
    v = v.astype(BF16)
    gate = _silu(g.astype(F32))
    logf = jnp.log(fg)
    hi = logf.astype(BF16)
    lo = (logf - hi.astype(F32)).astype(BF16)
    b = _dot(tri, hi) + _dot(tri, lo)
    b_mid = b[mid:mid + 1, :]
    b_last = b[L - 1:L, :]
    qa = q * jnp.exp(b - b_mid)
    kb = k * jnp.exp(b_mid - b)
    qe = (qa * jnp.exp(b_mid)).astype(BF16)
    kd = (kb * jnp.exp(b_last - b_mid)).astype(BF16)
    qa = qa.astype(BF16)
    kb = kb.astype(BF16)
    decay = jnp.exp(b_last)
    sls = [slice(h * HG_DK, (h + 1) * HG_DK) for h in range(H)]
    sts = [st_ref[h] for h in range(H)]
    scores = [_dot_nt(qa[:, sl], kb[:, sl]) for sl in sls]
    inter = [_dot_nt(qe[:, sl], st.astype(BF16)) for sl, st in zip(sls, sts)]
    outer = [_dot_tn(v[:, sl], kd[:, sl]) for sl in sls]
    intra = [_dot(jnp.where(causal, sc, 0.0).astype(BF16), v[:, sl]) for sc, sl in zip(scores, sls)]
    for h, sl in enumerate(sls):
        st_ref[h] = sts[h] * decay[:, sl] + outer[h]
        write_o(sl, _rms(inter[h] + intra[h], onorm[:, sl]) * gate[:, sl])


def gla(zqig, zf, lb, onorm_g, s0):
    B, T, D = zf.shape
    L = CHUNK if T % CHUNK == 0 else T
    tt = min(T, 512)
    n_chunks = tt // L
    H = HG_HEADS

    def zspec(part):
        return pl.BlockSpec((1, tt, D), lambda b, t: (b, t, part))

    hspec = pl.BlockSpec((1, D), lambda b, t: (0, 0))
    sspec = pl.BlockSpec((1, H, HG_DK, HG_DV), lambda b, t: (b, 0, 0, 0))
    in_specs = [zspec(0), zspec(0), zspec(1), zspec(2), hspec, hspec]
    args = [zqig, zf, zqig, zqig, lb.reshape(1, D), onorm_g.reshape(1, D)]
    if s0 is not None:
        in_specs.append(sspec)
        args.append(s0)
    return pl.pallas_call(
        functools.partial(_gla_kernel, L=L, n_chunks=n_chunks, has_init=s0 is not None),
        grid=(B, T // tt),
        in_specs=in_specs,
        out_specs=[pl.BlockSpec((1, tt, D), lambda b, t: (b, t, 0)), sspec],
        out_shape=[jax.ShapeDtypeStruct((B, T, D), BF16),
                   jax.ShapeDtypeStruct((B, H, HG_DK, HG_DV), F32)],
        scratch_shapes=[pltpu.VMEM((H, HG_DV, HG_DK), F32)],
        compiler_params=_cparams(2),
        name="gla",
    )(*args)


def _route_kernel(h_ref, rw_ref, bias_ref, pos_ref, w_ref, te_ref, nu_ref,
                  e_s, r_s, base_s, start_s, *, tile_rows):
    ph = pl.program_id(0)
    i = pl.program_id(1)
    M = h_ref.shape[0]
    half = h_ref.shape[1]
    G, E = N_GROUPS, N_EXPERTS // N_GROUPS
    e_flat = lax.broadcasted_iota(jnp.int32, (N_EXPERTS, M), 0)

    @pl.when(ph == 1)
    def _():
        @pl.when(i == 0)
        def _():
            cnt = base_s[...]
            padded = jnp.floor((cnt + (tile_rows - 1)) * (1.0 / tile_rows)) * tile_rows
            r = lax.broadcasted_iota(jnp.int32, (N_EXPERTS, N_EXPERTS), 0)
            c = lax.broadcasted_iota(jnp.int32, (N_EXPERTS, N_EXPERTS), 1)
            start = jnp.dot((c < r).astype(F32), padded, preferred_element_type=F32,
                            precision=lax.Precision.HIGHEST)
            start_s[...] = start
            te_ref[...] = (start * (1.0 / tile_rows)).astype(jnp.int32)
            nu_ref[...] = (padded * (1.0 / tile_rows)).astype(jnp.int32)

        start_col = start_s[:, :1]
        for k in range(TOP_K):
            hit = e_flat == e_s[i, k:k + 1, :]
            seg = jnp.sum(jnp.where(hit, start_col, 0.0), axis=0, keepdims=True)
            pos_ref[k:k + 1, :] = (seg + r_s[i, k:k + 1, :]).astype(jnp.int32)

    @pl.when(ph == 0)
    def _():
        _route_pass0(h_ref, rw_ref, bias_ref, w_ref, e_s, r_s, base_s, i, M, half, G, E)


def _route_pass0(h_ref, rw_ref, bias_ref, w_ref, e_s, r_s, base_s, i, M, half, G, E):
    @pl.when(i == 0)
    def _():
        base_s[...] = jnp.zeros_like(base_s)

    lo, hi = _unpack_pairs(h_ref[...])
    rw = rw_ref[...].astype(BF16)
    logits = _dot_nt(rw[:, :half], lo) + _dot_nt(rw[:, half:], hi)
    s = jax.nn.sigmoid(logits)
    sb = (s + bias_ref[...]).reshape(G, E, M)
    s = s.reshape(G, E, M)
    e_in = lax.broadcasted_iota(jnp.int32, (G, E, M), 1).astype(F32)
    g_id = lax.broadcasted_iota(jnp.int32, (G, 1, M), 0)
    e_id = lax.broadcasted_iota(jnp.int32, (G, E, M), 0).astype(F32) * E + e_in

    def all_max(a):
        return jnp.max(jnp.max(a, axis=0, keepdims=True), axis=1, keepdims=True)

    def all_min(a):
        return jnp.min(jnp.min(a, axis=0, keepdims=True), axis=1, keepdims=True)

    def all_sum(a):
        return jnp.sum(jnp.sum(a, axis=0, keepdims=True), axis=1, keepdims=True)

    m1 = jnp.max(sb, axis=1, keepdims=True)
    first = jnp.min(jnp.where(sb == m1, e_in, float(E)), axis=1, keepdims=True)
    m2 = jnp.max(jnp.where(e_in == first, NEG_INF, sb), axis=1, keepdims=True)
    gs = m1 + m2

    rank = jnp.zeros((G, 1, M), jnp.int32)
    for j in range(G):
        gj = gs[j:j + 1]
        beats = (gj > gs) | ((gj == gs) & (j < g_id))
        rank = rank + beats.astype(jnp.int32)
    gsel = rank < TOPK_GROUPS

    vals = jnp.where(gsel, sb, NEG_INF)
    selm = jnp.zeros((G, E, M), F32)
    chosen, score = [], []
    for _ in range(TOP_K):
        m = all_max(vals)
        first = all_min(jnp.where(vals == m, e_id, float(N_EXPERTS)))
        hit = e_id == first
        score.append(all_sum(jnp.where(hit, s, 0.0)))
        selm = jnp.where(hit, 1.0, selm)
        vals = jnp.where(hit, NEG_INF, vals)
        chosen.append(first)

    tot = score[0]
    for sc in score[1:]:
        tot = tot + sc
    norm = ROUTED_SCALE / tot

    selm = selm.reshape(N_EXPERTS, M)
    earlier = (lax.broadcasted_iota(jnp.int32, (M, M), 0)
               < lax.broadcasted_iota(jnp.int32, (M, M), 1)).astype(BF16)
    rank = (base_s[:, :1] + _dot(selm.astype(BF16), earlier)).reshape(G, E, M)
    base_s[...] = base_s[...] + jnp.sum(selm, axis=1, keepdims=True)
    for k in range(TOP_K):
        hit = e_id == chosen[k]
        e_s[i, k:k + 1, :] = chosen[k].reshape(1, M).astype(jnp.int32)
        r_s[i, k:k + 1, :] = all_sum(jnp.where(hit, rank, 0.0)).reshape(1, M)
        w_ref[k:k + 1, :] = (score[k] * norm).reshape(1, M)


def route(hp, router_w_t, router_bias, l, tile_rows, rows=512):
    N, half = hp.shape
    M = rows
    nT = N // M
    assert N % M == 0

    def p0(ph, i):
        return i * (1 - ph) + (nT - 1) * ph

    return pl.pallas_call(
        functools.partial(_route_kernel, tile_rows=tile_rows),
        grid=(2, nT),
        in_specs=[pl.BlockSpec((M, half), lambda ph, i: (p0(ph, i), 0)),
                  pl.BlockSpec((None, N_EXPERTS, 2 * half), lambda ph, i: (l, 0, 0)),
                  pl.BlockSpec((None, N_EXPERTS, 1), lambda ph, i: (l, 0, 0))],
        out_specs=[pl.BlockSpec((TOP_K, M), lambda ph, i: (0, i * ph)),
                   pl.BlockSpec((TOP_K, M), lambda ph, i: (0, p0(ph, i))),
                   pl.BlockSpec((N_EXPERTS, 128), lambda ph, i: (0, 0)),
                   pl.BlockSpec((N_EXPERTS, 128), lambda ph, i: (0, 0))],
        out_shape=[jax.ShapeDtypeStruct((TOP_K, N), jnp.int32),
                   jax.ShapeDtypeStruct((TOP_K, N), F32),
                   jax.ShapeDtypeStruct((N_EXPERTS, 128), jnp.int32),
                   jax.ShapeDtypeStruct((N_EXPERTS, 128), jnp.int32)],
        scratch_shapes=[pltpu.VMEM((nT, TOP_K, M), jnp.int32), pltpu.VMEM((nT, TOP_K, M), F32),
                        pltpu.VMEM((N_EXPERTS, 128), F32), pltpu.VMEM((N_EXPERTS, 128), F32)],
        compiler_params=_cparams(2),
        name="route",
    )(hp, router_w_t, router_bias.reshape(-1, N_EXPERTS, 1))


def _sc_mesh():
    return plsc.VectorSubcoreMesh(core_axis_name="core", subcore_axis_name="subcore")


def sc_invert(pos_flat, n_tok, n_out):
    n = pos_flat.shape[0]
    per = n_out // SC_WORKERS
    chunk = n_tok
    assert n_out % SC_WORKERS == 0 and per % SC_LANES == 0
    assert n_tok % chunk == 0 and n % chunk == 0 and chunk % SC_LANES == 0
    cp = pltpu.CompilerParams()
    if "needs_layout_passes" in pltpu.CompilerParams.__dataclass_fields__:
        cp = dataclasses.replace(cp, needs_layout_passes=False)

    @functools.partial(
        pl.kernel, out_type=jax.ShapeDtypeStruct((n_out,), jnp.int32), mesh=_sc_mesh(),
        scratch_types=[pltpu.VMEM((chunk,), jnp.int32), pltpu.VMEM((per,), jnp.int32)],
        compiler_params=cp, name="sc_invert")
    def k(pos_hbm, src_hbm, pos_v, src_v):
        wid = lax.axis_index("subcore") * SC_CORES + lax.axis_index("core")
        lo = wid * per
        lane = lax.iota(jnp.int32, SC_LANES)

        @pl.loop(0, per, step=SC_LANES)
        def _(r):
            src_v[pl.ds(r, SC_LANES)] = lax.rem(lo + r + lane, n_tok)

        @pl.loop(0, n // chunk)
        def _(c):
            base = c * chunk
            pltpu.sync_copy(pos_hbm.at[pl.ds(base, chunk)], pos_v)
            tok0 = lax.rem(base, n_tok)

            @plsc.parallel_loop(0, chunk, step=SC_LANES, unroll=8)
            def _(r):
                p = pos_v[pl.ds(r, SC_LANES)] - lo
                mine = (p >= 0) & (p < per)
                plsc.store_scatter(src_v, [jnp.where(mine, p, 0)], tok0 + r + lane, mask=mine)

        pltpu.sync_copy(src_v, src_hbm.at[pl.ds(lo, per)])

    return k(pos_flat)


def sc_gather(x, idx):
    n = idx.shape[0]
    dim = x.shape[1]
    assert n % (SC_WINDOW * SC_WORKERS) == 0

    @functools.partial(
        pl.kernel, out_type=jax.ShapeDtypeStruct((n, dim), x.dtype), mesh=_sc_mesh(),
        scratch_types=[], name="sc_gather")
    def k(x_hbm, i_hbm, o_hbm):
        def body(i_vmem, o_vmem):
            pltpu.sync_copy(x_hbm.at[i_vmem.at[0]], o_vmem)

        pltpu.emit_pipeline(
            body, grid=(n // SC_WINDOW,),
            in_specs=[pl.BlockSpec((1, SC_WINDOW), index_map=lambda i: (i, 0))],
            out_specs=[pl.BlockSpec((SC_WINDOW, dim), index_map=lambda i: (i, 0))],
            core_axis_name=("core", "subcore"),
            dimension_semantics=(pltpu.PARALLEL,),
        )(i_hbm, o_hbm)

    return k(x, idx.reshape(n // SC_WINDOW, SC_WINDOW))


def _moe_gemm_kernel(ts_ref, tn_ref, x_hbm, wi_ref, wo_ref, o_hbm, wi_b, wo_b, xbuf, obuf, in_sem, out_sem,
                     *, tile_rows, n_tiles):
    e = pl.program_id(0)
    last = pl.num_programs(0) - 1
    t0 = ts_ref[e]
    n = tn_ref[e]
    n_used = ts_ref[last] + tn_ref[last]

    def x_copy(g, slot):
        rows = pl.ds(pl.multiple_of(g * tile_rows, tile_rows), tile_rows)
        return pltpu.make_async_copy(x_hbm.at[rows], xbuf.at[slot], in_sem.at[slot])

    def o_copy(g, slot):
        rows = pl.ds(pl.multiple_of(g * tile_rows, tile_rows), tile_rows)
        return pltpu.make_async_copy(obuf.at[slot], o_hbm.at[rows], out_sem.at[slot])

    @pl.when(e == 0)
    def _():
        for g0 in range(MOE_NBUF - 1):
            @pl.when(g0 < n_used)
            def _():
                x_copy(g0, g0).start()

    @pl.when(n > 0)
    def _():
        wi_b[...] = wi_ref[...].astype(BF16)
        wo_b[...] = wo_ref[...].astype(BF16)

    def tile(i, carry):
        g = t0 + i
        slot = lax.rem(g, MOE_NBUF)
        x_copy(g, slot).wait()
        ahead = g + (MOE_NBUF - 1)

        @pl.when(ahead < n_used)
        def _():
            x_copy(ahead, lax.rem(ahead, MOE_NBUF)).start()

        @pl.when(g >= MOE_NBUF)
        def _():
            o_copy(g - MOE_NBUF, slot).wait()

        rows = tile_rows // MOE_SUB
        half = xbuf.shape[2]
        xs = [_unpack_pairs(xbuf[slot, r * rows:(r + 1) * rows, :]) for r in range(MOE_SUB)]
        hus = [_dot(lo, wi_b[:half, :]) + _dot(hi, wi_b[half:, :]) for lo, hi in xs]
        acts = [(_silu(hu[:, :EXPERT_FF]) * hu[:, EXPERT_FF:]).astype(BF16) for hu in hus]
        outs = [_dot(act, wo_b[...]) for act in acts]
        for r, out in enumerate(outs):
            obuf[slot, r * rows:(r + 1) * rows, :] = _pack_pairs(out)
        o_copy(g, slot).start()
        return carry

    lax.fori_loop(0, n, tile, 0)

    @pl.when(e == last)
    def _():
        for back in range(MOE_NBUF, 0, -1):
            @pl.when(n_used >= back)
            def _():
                o_copy(n_used - back, lax.rem(n_used - back, MOE_NBUF)).wait()

        obuf[0] = jnp.zeros(obuf.shape[1:], obuf.dtype)

        def clear(g, carry):
            cp = o_copy(g, 0)
            cp.start()
            cp.wait()
            return carry

        lax.fori_loop(n_used, n_tiles, clear, 0)


def moe_gemm(xs, tile_start, tile_count, exp_w_in, exp_w_out, l, tile_rows, n_tiles):
    P, half = xs.shape
    D = 2 * half
    assert P == n_tiles * tile_rows
    hbm = pl.BlockSpec(memory_space=pl.ANY)
    grid_spec = pltpu.PrefetchScalarGridSpec(
        num_scalar_prefetch=2,
        grid=(N_EXPERTS,),
        in_specs=[hbm,
                  pl.BlockSpec((None, None, D, 2 * EXPERT_FF), lambda e, ts, tn: (l, e, 0, 0)),
                  pl.BlockSpec((None, None, EXPERT_FF, D), lambda e, ts, tn: (l, e, 0, 0))],
        out_specs=hbm,
        scratch_shapes=[pltpu.VMEM((D, 2 * EXPERT_FF), BF16), pltpu.VMEM((EXPERT_FF, D), BF16),
                        pltpu.VMEM((MOE_NBUF, tile_rows, half), jnp.int32),
                        pltpu.VMEM((MOE_NBUF, tile_rows, half), jnp.int32),
                        pltpu.SemaphoreType.DMA((MOE_NBUF,)), pltpu.SemaphoreType.DMA((MOE_NBUF,))],
    )
    return pl.pallas_call(
        functools.partial(_moe_gemm_kernel, tile_rows=tile_rows, n_tiles=n_tiles),
        grid_spec=grid_spec,
        out_shape=jax.ShapeDtypeStruct((P, half), jnp.int32),
        compiler_params=_cparams(1),
        name="moe_gemm",
    )(tile_start, tile_count, xs, exp_w_in, exp_w_out)


def _moe_combine_kernel(*refs, final, norm_next):
    if final:
        y_ref, w_ref, h_ref, si_ref, so_ref, x_ref, g2_ref, fg_ref, o_ref, si_b, so_b = refs
    elif norm_next:
        (y_ref, w_ref, h_ref, si_ref, so_ref, x_ref, g2_ref, ng_ref, nsc_ref, nsh_ref,
         o_ref, hn_ref, si_b, so_b) = refs
    else:
        y_ref, w_ref, h_ref, si_ref, so_ref, x_ref, g2_ref, o_ref, si_b, so_b = refs

    @pl.when(pl.program_id(0) == 0)
    def _():
        si_b[...] = si_ref[...].astype(BF16)
        so_b[...] = so_ref[...].astype(BF16)

    bb, tt, D = x_ref.shape
    half = D // 2
    w = w_ref[...].T
    acc_lo = jnp.zeros((bb * tt, half), F32)
    acc_hi = jnp.zeros((bb * tt, half), F32)
    for k in range(TOP_K):
        lo, hi = _unpack_pairs(y_ref[k], F32)
        acc_lo = acc_lo + w[:, k:k + 1] * lo
        acc_hi = acc_hi + w[:, k:k + 1] * hi
    hlo, hhi = _unpack_pairs(h_ref[...])
    hu = _dot(hlo, si_b[:half, :]) + _dot(hhi, si_b[half:, :])
    act = (_silu(hu[:, :SHARED_FF]) * hu[:, SHARED_FF:]).astype(BF16)
    y = jnp.concatenate([acc_lo, acc_hi], axis=-1) + _dot(act, so_b[...])
    x_new = x_ref[...] + g2_ref[...] * y.reshape(bb, tt, D)
    o_ref[...] = _rms(x_new, fg_ref[...]) if final else x_new
    if norm_next:
        hn = _rms(x_new, ng_ref[...]) * (1.0 + nsc_ref[...]) + nsh_ref[...]
        hn_ref[...] = hn.astype(hn_ref.dtype)


def moe_combine(y8, w8, hp, sh_w_in, sh_w_out, x, mod, gate_idx, l, row0, final_g=None, next_norm=None,
                rows=512):
    B, T, D = x.shape
    half = D // 2
    bb, tt, nblk, ij = _row_blocks(B, T, rows)
    M = bb * tt
    assert row0 % M == 0
    off = row0 // M
    xspec = pl.BlockSpec((bb, tt, D), lambda i: ij(i) + (0,))
    in_specs = [pl.BlockSpec((TOP_K, M, half), lambda i: (0, off + i, 0)),
                pl.BlockSpec((TOP_K, M), lambda i: (0, off + i)),
                pl.BlockSpec((M, half), lambda i: (off + i, 0)),
                pl.BlockSpec((None, D, 2 * SHARED_FF), lambda i: (l, 0, 0)),
                pl.BlockSpec((None, SHARED_FF, D), lambda i: (l, 0, 0)),
                xspec,
                pl.BlockSpec((bb, 1, D), lambda i: (ij(i)[0], 0, gate_idx))]
    args = [y8, w8, hp, sh_w_in, sh_w_out, x, mod]
    out_specs = xspec
    out_shape = jax.ShapeDtypeStruct((B, T, D), F32)
    if final_g is not None:
        assert next_norm is None
        in_specs.append(pl.BlockSpec((1, D), lambda i: (0, 0)))
        args.append(final_g.reshape(1, D))
    if next_norm is not None:
        gain, mod_next, sc_idx, sh_idx = next_norm
        in_specs += [pl.BlockSpec((1, D), lambda i: (0, 0)),
                     pl.BlockSpec((bb, 1, D), lambda i: (ij(i)[0], 0, sc_idx)),
                     pl.BlockSpec((bb, 1, D), lambda i: (ij(i)[0], 0, sh_idx))]
        args += [gain.reshape(1, D), mod_next, mod_next]
        out_specs = [xspec, xspec]
        out_shape = [out_shape, jax.ShapeDtypeStruct((B, T, D), BF16)]
    return pl.pallas_call(
        functools.partial(_moe_combine_kernel, final=final_g is not None, norm_next=next_norm is not None),
        grid=(nblk,),
        in_specs=in_specs,
        out_specs=out_specs,
        out_shape=out_shape,
        scratch_shapes=[pltpu.VMEM((D, 2 * SHARED_FF), BF16), pltpu.VMEM((SHARED_FF, D), BF16)],
        compiler_params=_cparams(1),
        name="moe_combine",
    )(*args)


def _shared_kv_kernel(x_ref, g_ref, w_ref, lg_ref, cos_ref, sin_ref, lat_ref, kr_ref):
    bb, tt, D = x_ref.shape
    xn = _rms(x_ref[...], g_ref[...]).reshape(bb * tt, D).astype(BF16)
    z = _dot(xn, w_ref[...].astype(BF16))
    lat = _rms(z[:, :KV_LORA], lg_ref[...])
    lat_ref[...] = lat.reshape(bb, tt, KV_LORA)
    zr = z[:, KV_LORA:KV_LORA + MLA_ROPE].reshape(bb, tt, MLA_ROPE)
    zq = z[:, KV_LORA + 128:KV_LORA + 128 + MLA_ROPE].reshape(bb, tt, MLA_ROPE)
    kr_ref[...] = zr * cos_ref[...] + zq * sin_ref[...]


def shared_kv(x, kv_in_g, w_kv, kv_lat_g, cos32, sin32, rows=512):
    B, T, D = x.shape
    bb, tt, nblk, ij = _row_blocks(B, T, rows)
    tspec = pl.BlockSpec((tt, MLA_ROPE), lambda i: (ij(i)[1], 0))
    return pl.pallas_call(
        _shared_kv_kernel,
        grid=(nblk,),
        in_specs=[pl.BlockSpec((bb, tt, D), lambda i: ij(i) + (0,)),
                  pl.BlockSpec((1, D), lambda i: (0, 0)),
                  pl.BlockSpec(w_kv.shape, lambda i: (0, 0)),
                  pl.BlockSpec((1, KV_LORA), lambda i: (0, 0)),
                  tspec, tspec],
        out_specs=[pl.BlockSpec((bb, tt, KV_LORA), lambda i: ij(i) + (0,)),
                   pl.BlockSpec((bb, tt, MLA_ROPE), lambda i: ij(i) + (0,))],
        out_shape=[jax.ShapeDtypeStruct((B, T, KV_LORA), F32),
                   jax.ShapeDtypeStruct((B, T, MLA_ROPE), F32)],
        compiler_params=_cparams(1),
        name="shared_kv",
    )(x, kv_in_g.reshape(1, D), w_kv, kv_lat_g.reshape(1, KV_LORA), cos32, sin32)


def _kv_expand_kernel(lat_ref, kr_ref, wk_ref, ek_ref, wvt_ref, ones_ref, k_ref, vt_ref):
    lat = lat_ref[0].astype(BF16)
    kr = kr_ref[0].astype(BF16)
    k = _dot(lat, wk_ref[...].astype(BF16)) + _dot(kr, ek_ref[...].astype(BF16))
    k_ref[0] = k.astype(k_ref.dtype)
    vt = _dot_nt(wvt_ref[...].astype(BF16), lat) + ones_ref[...]
    vt_ref[0] = vt.astype(vt_ref.dtype)


def kv_expand(lat, kr, wk_pad, ek, wvt_ext, ones_col, rows=512):
    B, T, _ = lat.shape
    tt = rows
    NK, NVT = wk_pad.shape[1], wvt_ext.shape[0]

    def full(a):
        return pl.BlockSpec(a.shape, lambda b, t: (0, 0))

    def rowspec(n):
        return pl.BlockSpec((1, tt, n), lambda b, t: (b, t, 0))

    return pl.pallas_call(
        _kv_expand_kernel,
        grid=(B, T // tt),
        in_specs=[rowspec(KV_LORA), rowspec(MLA_ROPE), full(wk_pad), full(ek), full(wvt_ext), full(ones_col)],
        out_specs=[rowspec(NK), pl.BlockSpec((1, NVT, tt), lambda b, t: (b, 0, t))],
        out_shape=[jax.ShapeDtypeStruct((B, T, NK), BF16), jax.ShapeDtypeStruct((B, NVT, T), BF16)],
        compiler_params=_cparams(2),
        name="kv_expand",
    )(lat, kr, wk_pad, ek, wvt_ext, ones_col)


def _query_kernel(h_ref, wdq_ref, qg_ref, wq_ref, wqr_ref, c_ref, s_ref, q_ref, wdq_b, wq_b, wqr_b):
    @pl.when(pl.program_id(0) == 0)
    def _():
        wdq_b[...] = wdq_ref[...].astype(BF16)
        wq_b[...] = wq_ref[...].astype(BF16)
        wqr_b[...] = wqr_ref[...].astype(BF16)

    bb, tt, D = h_ref.shape
    h = h_ref[...].reshape(bb * tt, D)
    cq = _rms(_dot(h, wdq_b[...]), qg_ref[...]).astype(BF16)
    q1 = _dot(cq, wq_b[...]).reshape(bb, tt, -1)
    q2 = _dot(cq, wqr_b[...]).reshape(bb, tt, -1)
    c = c_ref[...]
    s = s_ref[...]
    for hd in range(MLA_HEADS):
        sl = slice(hd * HEAD_PAD, (hd + 1) * HEAD_PAD)
        q_ref[:, :, sl] = (q1[:, :, sl] * c + q2[:, :, sl] * s).astype(q_ref.dtype)


def mla_queries(h, w_dq, q_norm_g, wq_pad, wq_rot, l, c128, s128, rows=512):
    B, T, D = h.shape
    bb, tt, nblk, ij = _row_blocks(B, T, rows)
    NQ = wq_pad.shape[-1]
    tspec = pl.BlockSpec((tt, HEAD_PAD), lambda i: (ij(i)[1], 0))
    return pl.pallas_call(
        _query_kernel,
        grid=(nblk,),
        in_specs=[pl.BlockSpec((bb, tt, D), lambda i: ij(i) + (0,)),
                  pl.BlockSpec((None, D, Q_LORA), lambda i: (l, 0, 0)),
                  pl.BlockSpec((None, 1, Q_LORA), lambda i: (l, 0, 0)),
                  pl.BlockSpec((None, Q_LORA, NQ), lambda i: (l, 0, 0)),
                  pl.BlockSpec((None, Q_LORA, NQ), lambda i: (l, 0, 0)),
                  tspec, tspec],
        out_specs=pl.BlockSpec((bb, tt, NQ), lambda i: ij(i) + (0,)),
        out_shape=jax.ShapeDtypeStruct((B, T, NQ), BF16),
        scratch_shapes=[pltpu.VMEM((D, Q_LORA), BF16), pltpu.VMEM((Q_LORA, NQ), BF16),
                        pltpu.VMEM((Q_LORA, NQ), BF16)],
        compiler_params=_cparams(1),
        name="mla_queries",
    )(h, w_dq, q_norm_g.reshape(-1, 1, Q_LORA), wq_pad, wq_rot, c128, s128)


def _query_t_kernel(h_ref, wdq_ref, qg_ref, wqt_ref, wqrt_ref, cos_ref, sin_ref, qt_ref, wdq_b, wqt_b, wqrt_b):
    @pl.when((pl.program_id(0) == 0) & (pl.program_id(1) == 0))
    def _():
        wdq_b[...] = wdq_ref[...].astype(BF16)
        wqt_b[...] = wqt_ref[...].astype(BF16)
        wqrt_b[...] = wqrt_ref[...].astype(BF16)

    cq = _rms(_dot(h_ref[0], wdq_b[...]), qg_ref[...]).astype(BF16)
    q1 = _dot_nt(wqt_b[...], cq)
    q2 = _dot_nt(wqrt_b[...], cq)
    cos = cos_ref[...]
    sin = sin_ref[...]
    pad = jnp.zeros((HEAD_PAD - MLA_NOPE - MLA_ROPE, q1.shape[1]), qt_ref.dtype)
    for hd in range(MLA_HEADS):
        r0 = hd * HEAD_PAD
        rope = (q1[r0 + MLA_NOPE:r0 + MLA_NOPE + MLA_ROPE] * cos
                + q2[hd * MLA_ROPE:(hd + 1) * MLA_ROPE] * sin)
        qt_ref[0, r0:r0 + MLA_NOPE, :] = (q1[r0:r0 + MLA_NOPE] * Q_PRESCALE).astype(qt_ref.dtype)
        qt_ref[0, r0 + MLA_NOPE:r0 + MLA_NOPE + MLA_ROPE, :] = rope.astype(qt_ref.dtype)
        qt_ref[0, r0 + MLA_NOPE + MLA_ROPE:r0 + HEAD_PAD, :] = pad


def mla_queries_t(h, w_dq, q_norm_g, wq_t, wqr_t, l, cos_t, sin_t, rows=512):
    B, T, D = h.shape
    tt = rows
    NQ = wq_t.shape[1]
    NR = wqr_t.shape[1]
    tspec = pl.BlockSpec((MLA_ROPE, tt), lambda b, t: (0, t))
    return pl.pallas_call(
        _query_t_kernel,
        grid=(B, T // tt),
        in_specs=[pl.BlockSpec((1, tt, D), lambda b, t: (b, t, 0)),
                  pl.BlockSpec((None, D, Q_LORA), lambda b, t: (l, 0, 0)),
                  pl.BlockSpec((None, 1, Q_LORA), lambda b, t: (l, 0, 0)),
                  pl.BlockSpec((None, NQ, Q_LORA), lambda b, t: (l, 0, 0)),
                  pl.BlockSpec((None, NR, Q_LORA), lambda b, t: (l, 0, 0)),
                  tspec, tspec],
        out_specs=pl.BlockSpec((1, NQ, tt), lambda b, t: (b, 0, t)),
        out_shape=jax.ShapeDtypeStruct((B, NQ, T), BF16),
        scratch_shapes=[pltpu.VMEM((D, Q_LORA), BF16), pltpu.VMEM((NQ, Q_LORA), BF16),
                        pltpu.VMEM((NR, Q_LORA), BF16)],
        compiler_params=_cparams(2),
        name="mla_queries_t",
    )(h, w_dq, q_norm_g.reshape(-1, 1, Q_LORA), wq_t, wqr_t, cos_t, sin_t)


def _attn_prompt_kernel(qi_tab, ki_tab, qt_ref, k_ref, vt_ref, o_ref, *scratch, tq, tk):
    H = MLA_HEADS
    m_refs, l_refs, acc_refs = scratch[:H], scratch[H:2 * H], scratch[2 * H:]
    p_id = pl.program_id(1)
    qi = qi_tab[p_id]
    ki = ki_tab[p_id]

    @pl.when(ki == 0)
    def _():
        for hd in range(H):
            m_refs[hd][...] = jnp.full(m_refs[hd].shape, NEG_INF, F32)
            l_refs[hd][...] = jnp.zeros(l_refs[hd].shape, F32)
            acc_refs[hd][...] = jnp.zeros(acc_refs[hd].shape, F32)

    def block(masked):
        if masked:
            kchunk = (ki * tk + lax.broadcasted_iota(jnp.int32, (tk, tq), 0)) // CHUNK
            qchunk = (qi * tq + lax.broadcasted_iota(jnp.int32, (tk, tq), 1)) // CHUNK
            mask = kchunk <= qchunk
        def scores(hd):
            sl = slice(hd * HEAD_PAD, (hd + 1) * HEAD_PAD)
            return _dot(k_ref[0, :, sl], qt_ref[0, sl, :])

        pending = [scores(hd) for hd in range(ATTN_LOOKAHEAD)]
        for hd in range(H):
            if hd + ATTN_LOOKAHEAD < H:
                pending.append(scores(hd + ATTN_LOOKAHEAD))
            s = pending.pop(0)
            if masked:
                s = jnp.where(mask, s, NEG_INF)
            m_prev = m_refs[hd][...]
            m_new = jnp.maximum(m_prev, jnp.max(s, axis=0, keepdims=True))
            a = jnp.exp2(m_prev - m_new)
            p = jnp.exp2(s - m_new).astype(BF16)
            pv = _dot(vt_ref[0, hd * V_ROWS:(hd + 1) * V_ROWS, :], p)
            acc_refs[hd][...] = a * acc_refs[hd][...] + pv[:MLA_V]
            l_refs[hd][...] = a * l_refs[hd][...] + pv[MLA_V:MLA_V + 1]
            m_refs[hd][...] = m_new

    @pl.when(ki < qi)
    def _():
        block(False)

    @pl.when(ki == qi)
    def _():
        block(True)
        o_t = jnp.concatenate([acc_refs[hd][...] / l_refs[hd][...] for hd in range(H)], axis=0)
        o_ref[0] = o_t.T.astype(o_ref.dtype)


def attn_prompt(qt, k, vt, tq=256):
    B, NQ, T = qt.shape
    NVT = vt.shape[1]
    NV = MLA_HEADS * MLA_V
    tk = tq
    assert tq % CHUNK == 0
    nq = T // tq
    pairs = [(a, b) for a in range(nq) for b in range(a + 1)]
    qi_tab = jnp.asarray([a for a, _ in pairs], jnp.int32)
    ki_tab = jnp.asarray([b for _, b in pairs], jnp.int32)
    grid_spec = pltpu.PrefetchScalarGridSpec(
        num_scalar_prefetch=2,
        grid=(B, len(pairs)),
        in_specs=[pl.BlockSpec((1, NQ, tq), lambda b, p, qt, kt: (b, 0, qt[p])),
                  pl.BlockSpec((1, tk, NQ), lambda b, p, qt, kt: (b, kt[p], 0)),
                  pl.BlockSpec((1, NVT, tk), lambda b, p, qt, kt: (b, 0, kt[p]))],
        out_specs=pl.BlockSpec((1, tq, NV), lambda b, p, qt, kt: (b, qt[p], 0)),
        scratch_shapes=([pltpu.VMEM((1, tq), F32)] * (2 * MLA_HEADS)
                        + [pltpu.VMEM((MLA_V, tq), F32)] * MLA_HEADS),
    )
    return pl.pallas_call(
        functools.partial(_attn_prompt_kernel, tq=tq, tk=tk),
        grid_spec=grid_spec,
        out_shape=jax.ShapeDtypeStruct((B, T, NV), BF16),
        compiler_params=_cparams(2),
        name="attn_prompt",
    )(qi_tab, ki_tab, qt, k, vt)


def _absorb_kernel(q_ref, m_ref, o_ref):
    o_ref[...] = _dot(q_ref[...], m_ref[...].astype(BF16)).astype(o_ref.dtype)


def absorb_queries(q2d, m_abs):
    N = q2d.shape[0]
    H, _, W = m_abs.shape
    return pl.pallas_call(
        _absorb_kernel,
        grid=(H,),
        in_specs=[pl.BlockSpec((N, HEAD_PAD), lambda h: (0, h)),
                  pl.BlockSpec((None, HEAD_PAD, W), lambda h: (h, 0, 0))],
        out_specs=pl.BlockSpec((None, N, W), lambda h: (h, 0, 0)),
        out_shape=jax.ShapeDtypeStruct((H, N, W), BF16),
        compiler_params=_cparams(1),
        name="absorb_queries",
    )(q2d, m_abs)


def _attn_sample_kernel(q_ref, lat_ref, kr_ref, nlat_ref, nkr_ref, o_ref, m_ref, l_ref, acc_ref):
    kb = pl.program_id(1)
    H, Q, W = q_ref.shape
    q = q_ref[...].reshape(H * Q, W)
    q_lat = q[:, :KV_LORA]
    q_rope = q[:, KV_LORA:KV_LORA + MLA_ROPE]

    def update(lat_tile, kr_tile, n_sub, kr_transposed):
        sub = lat_tile.shape[0] // n_sub
        lats = [lat_tile[j * sub:(j + 1) * sub, :].astype(BF16) for j in range(n_sub)]
        if kr_transposed:
            krs = [kr_tile[:, j * sub:(j + 1) * sub].astype(BF16) for j in range(n_sub)]
            ss = [_dot_nt(q_lat, lat) + _dot(q_rope, kr) for lat, kr in zip(lats, krs)]
        else:
            krs = [kr_tile[j * sub:(j + 1) * sub, :].astype(BF16) for j in range(n_sub)]
            ss = [_dot_nt(q_lat, lat) + _dot_nt(q_rope, kr) for lat, kr in zip(lats, krs)]
        m_prev = m_ref[...]
        m_new = m_prev
        for s in ss:
            m_new = jnp.maximum(m_new, jnp.max(s, axis=-1, keepdims=True))
        a = jnp.exp2(m_prev - m_new)
        ps = [jnp.exp2(s - m_new[:, :1]) for s in ss]
        pv = _dot(ps[0].astype(BF16), lats[0])
        psum = jnp.sum(ps[0], axis=-1, keepdims=True)
        for p, lat in zip(ps[1:], lats[1:]):
            pv = pv + _dot(p.astype(BF16), lat)
            psum = psum + jnp.sum(p, axis=-1, keepdims=True)
        l_ref[...] = a * l_ref[...] + psum
        m_ref[...] = m_new
        acc_ref[...] = jnp.concatenate([a, a], axis=-1) * acc_ref[...] + pv

    @pl.when(kb == 0)
    def _():
        m_ref[...] = jnp.full_like(m_ref, NEG_INF)
        l_ref[...] = jnp.zeros_like(l_ref)
        acc_ref[...] = jnp.zeros_like(acc_ref)
        update(nlat_ref[0], nkr_ref[0], 1, False)

    update(lat_ref[0], kr_ref[0], SAMPLE_KEY_SUB, True)

    @pl.when(kb == pl.num_programs(1) - 1)
    def _():
        lsum = l_ref[...]
        o = acc_ref[...] / jnp.concatenate([lsum, lsum], axis=-1)
        o_ref[...] = o.reshape(H, Q, KV_LORA).astype(o_ref.dtype)


def attn_sample(q_abs, cache_lat, cache_kr_t, new_lat, new_kr, tk=4096):
    H, N, W = q_abs.shape
    B, P, _ = cache_lat.shape
    Q = new_lat.shape[1]
    qpos = P + np.arange(Q)
    kpos = np.arange(P + Q)
    assert bool(np.all((kpos // CHUNK)[None, :] <= (qpos // CHUNK)[:, None]))
    return pl.pallas_call(
        _attn_sample_kernel,
        grid=(B, P // tk),
        in_specs=[pl.BlockSpec((H, Q, W), lambda b, kb: (0, b, 0)),
                  pl.BlockSpec((1, tk, KV_LORA), lambda b, kb: (b, kb, 0)),
                  pl.BlockSpec((1, MLA_ROPE, tk), lambda b, kb: (b, 0, kb)),
                  pl.BlockSpec((1, Q, KV_LORA), lambda b, kb: (b, 0, 0)),
                  pl.BlockSpec((1, Q, MLA_ROPE), lambda b, kb: (b, 0, 0))],
        out_specs=pl.BlockSpec((H, Q, KV_LORA), lambda b, kb: (0, b, 0)),
        out_shape=jax.ShapeDtypeStruct((H, N, KV_LORA), BF16),
        scratch_shapes=[pltpu.VMEM((H * Q, 128), F32), pltpu.VMEM((H * Q, 128), F32),
                        pltpu.VMEM((H * Q, KV_LORA), F32)],
        compiler_params=_cparams(2),
        name="attn_sample",
    )(q_abs, cache_lat, cache_kr_t, new_lat, new_kr)


def _unabsorb_kernel(o_ref, w_ref, out_ref):
    out_ref[...] = (_dot(o_ref[0], w_ref[0].astype(BF16))
                    + _dot(o_ref[1], w_ref[1].astype(BF16))).astype(out_ref.dtype)


def unabsorb(o_lat, wuv_pad):
    H, N, R = o_lat.shape
    return pl.pallas_call(
        _unabsorb_kernel,
        grid=(H // 2,),
        in_specs=[pl.BlockSpec((2, N, R), lambda p: (p, 0, 0)),
                  pl.BlockSpec((2, R, 128), lambda p: (p, 0, 0))],
        out_specs=pl.BlockSpec((N, 128), lambda p: (0, p)),
        out_shape=jax.ShapeDtypeStruct((N, (H // 2) * 128), BF16),
        compiler_params=_cparams(1),
        name="unabsorb",
    )(o_lat, wuv_pad)


def _rope_tables(pos):
    half = MLA_ROPE // 2
    inv = 1.0 / (ROPE_THETA ** (np.arange(half, dtype=np.float64) * 2.0 / MLA_ROPE))
    ang = np.asarray(pos, np.float64)[:, None] * inv[None, :]
    cos = np.concatenate([np.cos(ang), np.cos(ang)], axis=-1)
    sin = np.concatenate([np.sin(ang), np.sin(ang)], axis=-1)
    T = cos.shape[0]
    c128 = np.zeros((T, HEAD_PAD)); s128 = np.zeros((T, HEAD_PAD))
    c128[:, :MLA_NOPE] = 1.0
    c128[:, MLA_NOPE:MLA_NOPE + MLA_ROPE] = cos
    s128[:, MLA_NOPE:MLA_NOPE + MLA_ROPE] = sin
    return dict(cos32=jnp.asarray(cos, F32), sin32=jnp.asarray(sin, F32),
                c128=jnp.asarray(c128 * Q_PRESCALE, F32), s128=jnp.asarray(s128 * Q_PRESCALE, F32),
                cos_t=jnp.asarray(cos.T * Q_PRESCALE, F32), sin_t=jnp.asarray(sin.T * Q_PRESCALE, F32))


def _rot_half_cols(w):
    half = w.shape[-1] // 2
    return jnp.concatenate([-w[..., half:], w[..., :half]], axis=-1)


def _prep_weights(w_dkv, w_uk, w_uv, w_uq, router_w):
    D = D_MODEL
    w_lat, w_rope = w_dkv[:, :KV_LORA], w_dkv[:, KV_LORA:]
    pad96 = jnp.zeros((D, 128 - MLA_ROPE), F32)
    w_kv = jnp.concatenate([w_lat, w_rope, pad96, _rot_half_cols(w_rope), pad96], axis=-1)

    zpad = HEAD_PAD - MLA_NOPE
    wk_pad = jnp.pad(w_uk, ((0, 0), (0, 0), (0, zpad))).reshape(KV_LORA, MLA_HEADS * HEAD_PAD)
    ek = jnp.zeros((MLA_ROPE, MLA_HEADS, HEAD_PAD), F32)
    ek = ek.at[:, :, MLA_NOPE:MLA_NOPE + MLA_ROPE].set(
        jnp.broadcast_to(jnp.eye(MLA_ROPE, dtype=F32)[:, None, :], (MLA_ROPE, MLA_HEADS, MLA_ROPE)))
    ek = ek.reshape(MLA_ROPE, MLA_HEADS * HEAD_PAD)
    wvt = jnp.transpose(w_uv, (1, 2, 0))
    wvt_ext = jnp.pad(wvt, ((0, 0), (0, V_ROWS - MLA_V), (0, 0))).reshape(MLA_HEADS * V_ROWS, KV_LORA)
    ones_col = jnp.tile((jnp.arange(V_ROWS) >= MLA_V).astype(F32), MLA_HEADS).reshape(-1, 1)

    nb = w_uq.shape[0]
    qn, qr = w_uq[..., :MLA_NOPE], w_uq[..., MLA_NOPE:]
    z32 = jnp.zeros(qr.shape[:-1] + (HEAD_PAD - MLA_NOPE - MLA_ROPE,), F32)
    wq_pad = jnp.concatenate([qn, qr, z32], axis=-1).reshape(nb, Q_LORA, MLA_HEADS * HEAD_PAD)
    wq_rot = jnp.concatenate([jnp.zeros_like(qn), _rot_half_cols(qr), z32], axis=-1)
    wq_rot = wq_rot.reshape(nb, Q_LORA, MLA_HEADS * HEAD_PAD)
    wq_t = jnp.transpose(wq_pad, (0, 2, 1))
    wqr_t = jnp.transpose(_rot_half_cols(qr).reshape(nb, Q_LORA, MLA_HEADS * MLA_ROPE), (0, 2, 1))

    m_abs = jnp.zeros((MLA_HEADS, HEAD_PAD, KV_LORA + 128), F32)
    m_abs = m_abs.at[:, :MLA_NOPE, :KV_LORA].set(jnp.transpose(w_uk, (1, 2, 0)))
    m_abs = m_abs.at[:, MLA_NOPE:MLA_NOPE + MLA_ROPE, KV_LORA:KV_LORA + MLA_ROPE].set(
        jnp.broadcast_to(jnp.eye(MLA_ROPE, dtype=F32), (MLA_HEADS, MLA_ROPE, MLA_ROPE)))

    wuv_h = jnp.transpose(w_uv, (1, 0, 2))
    even = jnp.pad(wuv_h, ((0, 0), (0, 0), (0, 64)))
    odd = jnp.pad(wuv_h, ((0, 0), (0, 0), (64, 0)))
    wuv_pad = jnp.where((jnp.arange(MLA_HEADS) % 2 == 0)[:, None, None], even, odd)

    rw_t = jnp.transpose(router_w, (0, 2, 1))
    return dict(w_kv=w_kv, wk_pad=wk_pad, ek=ek, wvt_ext=wvt_ext, ones_col=ones_col, wq_pad=wq_pad, wq_rot=wq_rot, wq_t=wq_t, wqr_t=wqr_t,
                m_abs=m_abs, wuv_pad=wuv_pad, rw_t=rw_t)


def _mixer(st, l, P, W, packed):
    rows_kw = dict(rows_total=packed["total"], row0=packed["row0"], rows_buf=packed["buf"])
    x, m = st["x"], st["mod"][l]
    B, T, _ = x.shape
    n_a = P["hg_w_in"].shape[0]
    h = st.pop("h_next", None)
    if h is None:
        h = norm_mod(x, P["norm1_g"][l], m, sc_idx=1, sh_idx=0)
    norm2 = (P["norm2_g"][l], 4, 3)
    if l < n_a:
        rows = min(1024, B * T)
        zf = linear(h, P["hg_w_in"], l, F32, col_blocks=(1,), rows=rows)
        zqig = linear(h, P["hg_w_in"], l, BF16, col_blocks=(0, 2, 3), rows=rows)
        s0 = None if st["hg_state"] is None else st["hg_state"][l]
        o, s_new = gla(zqig, zf, st["lbs"][l], P["hg_onorm_g"][l], s0)
        st["hg_new"].append(s_new)
        st["x"], packed["buf"] = linear(o, P["hg_w_out"], l, F32, x=x, mod=m, gate_idx=2, next_norm=norm2,
                                        **rows_kw)
    else:
        bi = l - n_a
        if st["past_lat"] is None:
            qt = mla_queries_t(h, P["w_dq"], P["q_norm_g"], W["wq_t"], W["wqr_t"], bi, st["cos_t"], st["sin_t"])
            o = attn_prompt(qt, st["k_all"], st["v_all"])
        else:
            q = mla_queries(h, P["w_dq"], P["q_norm_g"], W["wq_pad"], W["wq_rot"], bi, st["c128"], st["s128"])
            q_abs = absorb_queries(q.reshape(B * T, -1), W["m_abs"])
            o_lat = attn_sample(q_abs, st["past_lat"], st["past_kr"], st["lat"], st["kr"])
            o = unabsorb(o_lat, W["wuv_pad"]).reshape(B, T, -1)
        st["x"], packed["buf"] = linear(o, P["w_o"], bi, F32, x=x, mod=m, gate_idx=2, next_norm=norm2, **rows_kw)
    packed["row0"] += B * T


def _moe(groups, hp, l, P, W):
    n_tok = hp.shape[0]
    n_tiles = (TOP_K * n_tok) // MOE_TILE + N_EXPERTS
    pos, w8, tile_start, tile_count = route(hp, W["rw_t"], P["router_bias"], l, MOE_TILE)
    pos_flat = pos.reshape(-1)
    src = sc_invert(pos_flat, n_tok, n_tiles * MOE_TILE)
    xs = sc_gather(hp, src)
    out = moe_gemm(xs, tile_start[:, 0], tile_count[:, 0], P["exp_w_in"], P["exp_w_out"], l,
                   MOE_TILE, n_tiles)
    y8 = sc_gather(out, pos_flat).reshape(TOP_K, n_tok, -1)
    row0 = 0
    for st in groups:
        B, T, _ = st["x"].shape
        if l == P["norm1_g"].shape[0] - 1:
            st["x"] = moe_combine(y8, w8, hp, P["sh_w_in"], P["sh_w_out"], st["x"], st["mod"][l], 5, l, row0,
                                  final_g=P["final_g"])
        else:
            nxt = (P["norm1_g"][l + 1], st["mod"][l + 1], 1, 0)
            st["x"], st["h_next"] = moe_combine(y8, w8, hp, P["sh_w_in"], P["sh_w_out"], st["x"],
                                                st["mod"][l], 5, l, row0, next_norm=nxt)
        row0 += B * T


def _group_state(x, mod, pos, hg_state, past_lat, past_kr, lbs):
    return dict(x=x, mod=mod, hg_state=hg_state, past_lat=past_lat, past_kr=past_kr, lbs=lbs,
                **_rope_tables(pos), hg_new=[],
                lat=None, kr=None, k_all=None, v_all=None)


def kernel(x_prompt, x_sample, state_hgrn, cache_mla_latent, cache_mla_krope, c_prompt, c_sample, ada_w, ada_b, norm1_g, norm2_g, hg_w_in, hg_lb_logits, hg_onorm_g, hg_w_out, kv_in_g, w_dkv, kv_lat_g, w_uk, w_uv, w_dq, q_norm_g, w_uq, w_o, router_w, router_bias, exp_w_in, exp_w_out, sh_w_in, sh_w_out, final_g):
    Bp, Sp, _ = x_prompt.shape
    Bs, Ss, _ = x_sample.shape
    past = cache_mla_latent.shape[1]
    P = dict(norm1_g=norm1_g, norm2_g=norm2_g, hg_w_in=hg_w_in, hg_lb_logits=hg_lb_logits,
             hg_onorm_g=hg_onorm_g, hg_w_out=hg_w_out, kv_in_g=kv_in_g, kv_lat_g=kv_lat_g,
             w_dq=w_dq, q_norm_g=q_norm_g, w_o=w_o, router_bias=router_bias,
             exp_w_in=exp_w_in, exp_w_out=exp_w_out, sh_w_in=sh_w_in, sh_w_out=sh_w_out, final_g=final_g)
    W = _prep_weights(w_dkv, w_uk, w_uv, w_uq, router_w)
    mod = ada_mod(jnp.concatenate([c_prompt, c_sample], axis=0), ada_w, ada_b)
    lbs = jnp.cumsum(jax.nn.softmax(hg_lb_logits.astype(F32), axis=0), axis=0)
    gp = _group_state(x_prompt, mod[:, :Bp, None, :], np.arange(Sp), None, None, None, lbs)
    gs = _group_state(x_sample, mod[:, Bp:, None, :], past + np.arange(Ss), state_hgrn,
                      cache_mla_latent, jnp.transpose(cache_mla_krope, (0, 2, 1)), lbs)
    groups = [gp, gs]
    n_tok = sum(st["x"].shape[0] * st["x"].shape[1] for st in groups)
    n_a = hg_w_in.shape[0]
    for l in range(norm1_g.shape[0]):
        packed = dict(total=n_tok, row0=0, buf=None)
        for st in groups:
            _mixer(st, l, P, W, packed)
        _moe(groups, packed["buf"], l, P, W)
        if l == n_a - 1:
            for st in groups:
                st["lat"], st["kr"] = shared_kv(st["x"], kv_in_g, W["w_kv"], kv_lat_g, st["cos32"], st["sin32"])
            gp["k_all"], gp["v_all"] = kv_expand(gp["lat"], gp["kr"], W["wk_pad"], W["ek"], W["wvt_ext"],
                                                 W["ones_col"])
    return (gp["x"], gs["x"], jnp.stack(gp["hg_new"], axis=0), jnp.stack(gs["hg_new"], axis=0),
            gp["lat"], gp["kr"], gs["lat"], gs["kr"])
```

```python
import dataclasses
import functools

import numpy as np
import jax
import jax.numpy as jnp
from jax import lax
from jax.experimental import pallas as pl
from jax.experimental.pallas import tpu as pltpu
from jax.experimental.pallas import tpu_sc as plsc

F32 = jnp.float32
BF16 = jnp.bfloat16

D_MODEL = 1024
CHUNK = 64
HG_HEADS = 8
HG_DK = 128
HG_DV = 128
MLA_HEADS = 16
MLA_NOPE = 64
MLA_ROPE = 32
MLA_V = 64
Q_LORA = 384
KV_LORA = 256
ROPE_THETA = 10000.0
N_EXPERTS = 64
TOP_K = 8
N_GROUPS = 8
TOPK_GROUPS = 4
EXPERT_FF = 256
SHARED_FF = 256
ROUTED_SCALE = 2.5
EPS = 1e-6

HEAD_PAD = 128
SAMPLE_KEY_SUB = 8
ATTN_LOOKAHEAD = 6
V_ROWS = MLA_V + 16
QK_SCALE = (MLA_NOPE + MLA_ROPE) ** -0.5
Q_PRESCALE = QK_SCALE * float(np.log2(np.e))
VMEM_LIMIT = 56 * 1024 * 1024
NEG_INF = float("-inf")
SC_CORES = 2
SC_SUBCORES = 16
SC_WORKERS = SC_CORES * SC_SUBCORES
SC_LANES = 16
SC_WINDOW = 64
MOE_TILE = 512
MOE_NBUF = 4
MOE_SUB = 1


def _cparams(n_axes):
    return pltpu.CompilerParams(dimension_semantics=("arbitrary",) * n_axes,
                                vmem_limit_bytes=VMEM_LIMIT)


def _silu(x):
    return x * jax.nn.sigmoid(x)


def _rms(x, g):
    ms = jnp.mean(x * x, axis=-1, keepdims=True)
    return x * lax.rsqrt(ms + EPS) * g


def _dot(a, b):
    return jnp.dot(a, b, preferred_element_type=F32)


def _dot_nt(a, b):
    return lax.dot_general(a, b, (((1,), (1,)), ((), ())), preferred_element_type=F32)


def _dot_tn(a, b):
    return lax.dot_general(a, b, (((0,), (0,)), ((), ())), preferred_element_type=F32)


def _row_blocks(B, T, rows):
    if T >= rows:
        assert T % rows == 0
        bb, tt = 1, rows
    else:
        assert rows % T == 0 and B % (rows // T) == 0
        bb, tt = rows // T, T
    nt = T // tt
    return bb, tt, (B // bb) * nt, (lambda i: (i // nt, i % nt))


def _ada_kernel(c_ref, w_ref, b_ref, o_ref):
    a = _silu(c_ref[...]).astype(BF16)
    o_ref[...] = _dot(a, w_ref[...].astype(BF16)) + b_ref[...]


def ada_mod(c, ada_w, ada_b):
    R, D = c.shape
    L, _, N = ada_w.shape
    tn = 1536
    return pl.pallas_call(
        _ada_kernel,
        grid=(L, N // tn),
        in_specs=[pl.BlockSpec((R, D), lambda l, j: (0, 0)),
                  pl.BlockSpec((None, D, tn), lambda l, j: (l, 0, j)),
                  pl.BlockSpec((None, 1, tn), lambda l, j: (l, 0, j))],
        out_specs=pl.BlockSpec((None, R, tn), lambda l, j: (l, 0, j)),
        out_shape=jax.ShapeDtypeStruct((L, R, N), F32),
        compiler_params=_cparams(2),
        name="ada_mod",
    )(c, ada_w, ada_b.reshape(L, 1, N))


def _pack_pairs(y):
    half = y.shape[-1] // 2
    bits = lax.bitcast_convert_type(y.astype(BF16).astype(F32), jnp.uint32)
    word = lax.shift_right_logical(bits[:, :half], jnp.uint32(16)) | bits[:, half:]
    return lax.bitcast_convert_type(word, jnp.int32)


def _unpack_pairs(word, dtype=BF16):
    u = lax.bitcast_convert_type(word, jnp.uint32)
    lo = lax.bitcast_convert_type(lax.shift_left(u, jnp.uint32(16)), F32)
    hi = lax.bitcast_convert_type(u & jnp.uint32(0xFFFF0000), F32)
    return lo.astype(dtype), hi.astype(dtype)


def _norm_kernel(x_ref, g_ref, sc_ref, sh_ref, o_ref):
    y = _rms(x_ref[...], g_ref[...]) * (1.0 + sc_ref[...]) + sh_ref[...]
    o_ref[...] = y.astype(o_ref.dtype)


def norm_mod(x, g, mod, sc_idx, sh_idx, rows=512):
    B, T, D = x.shape
    bb, tt, nblk, ij = _row_blocks(B, T, rows)
    xspec = pl.BlockSpec((bb, tt, D), lambda i: ij(i) + (0,))
    return pl.pallas_call(
        _norm_kernel,
        grid=(nblk,),
        in_specs=[xspec, pl.BlockSpec((1, D), lambda i: (0, 0)),
                  pl.BlockSpec((bb, 1, D), lambda i: (ij(i)[0], 0, sc_idx)),
                  pl.BlockSpec((bb, 1, D), lambda i: (ij(i)[0], 0, sh_idx))],
        out_specs=xspec,
        out_shape=jax.ShapeDtypeStruct((B, T, D), BF16),
        compiler_params=_cparams(1),
        name="norm_mod",
    )(x, g.reshape(1, D), mod, mod)


def _linear_kernel(*refs, residual, norm_next, shared_rows, n_main):
    if norm_next and shared_rows:
        a_ref, w_ref, x_ref, gate_ref, ng_ref, nsc_ref, nsh_ref, _, o_ref, hp_ref, wb_ref = refs
    elif norm_next:
        a_ref, w_ref, x_ref, gate_ref, ng_ref, nsc_ref, nsh_ref, o_ref, hp_ref, wb_ref = refs
    elif residual:
        a_ref, w_ref, x_ref, gate_ref, o_ref, wb_ref = refs
    else:
        a_ref, w_ref, o_ref, wb_ref = refs

    @pl.when(pl.program_id(1) == 0)
    def _():
        wb_ref[...] = w_ref[...].astype(BF16)

    def main():
        bb, tt, K = a_ref.shape
        y = _dot(a_ref[...].reshape(bb * tt, K).astype(BF16), wb_ref[...])
        y = y.reshape(bb, tt, y.shape[-1])
        if residual:
            y = x_ref[...] + gate_ref[...] * y
        o_ref[...] = y.astype(o_ref.dtype)
        if norm_next:
            h = _rms(y, ng_ref[...]) * (1.0 + nsc_ref[...]) + nsh_ref[...]
            hp_ref[...] = _pack_pairs(h.reshape(bb * tt, h.shape[-1]))

    if n_main is None:
        main()
    else:
        pl.when(pl.program_id(1) < n_main)(main)

        @pl.when(pl.program_id(1) >= n_main)
        def _():
            hp_ref[...] = jnp.zeros(hp_ref.shape, hp_ref.dtype)


def linear(a, w, l, out_dtype, x=None, mod=None, gate_idx=0, rows=512, tn=1024, next_norm=None,
           rows_total=None, row0=0, rows_buf=None, col_blocks=None):
    B, T, K = a.shape
    _, _, N = w.shape
    tn = min(tn, N)
    if col_blocks is None:
        def wcol(j):
            return j
    else:
        first, skip_from = col_blocks[0], [c - k for k, c in enumerate(col_blocks)]
        gap_at = next((k for k, d in enumerate(skip_from) if d != first), len(col_blocks))
        assert all(d == first for d in skip_from[:gap_at]) and all(d == first + 1 for d in skip_from[gap_at:])
        N = len(col_blocks) * tn

        def wcol(j):
            return j + first + (j >= gap_at)
    bb, tt, nblk, ij0 = _row_blocks(B, T, rows)
    n_extra = 0
    if next_norm is not None and rows_buf is None and rows_total is not None:
        assert row0 == 0 and (rows_total - B * T) % (bb * tt) == 0
        n_extra = (rows_total - B * T) // (bb * tt)

    def ij(i):
        return ij0(jnp.minimum(i, nblk - 1)) if n_extra else ij0(i)

    in_specs = [pl.BlockSpec((bb, tt, K), lambda j, i: ij(i) + (0,)),
                pl.BlockSpec((None, K, tn), lambda j, i: (l, 0, wcol(j)))]
    args = [a, w]
    ospec = pl.BlockSpec((bb, tt, tn), lambda j, i: ij(i) + (j,))
    out_specs = ospec
    out_shape = jax.ShapeDtypeStruct((B, T, N), out_dtype)
    aliases = {}
    if x is not None:
        gsteps = D_MODEL // tn
        in_specs += [ospec, pl.BlockSpec((bb, 1, tn), lambda j, i: (ij(i)[0], 0, gate_idx * gsteps + j))]
        args += [x, mod]
    if next_norm is not None:
        assert x is not None and tn == N
        gain, sc_idx, sh_idx = next_norm
        in_specs += [pl.BlockSpec((1, N), lambda j, i: (0, 0)),
                     pl.BlockSpec((bb, 1, N), lambda j, i: (ij(i)[0], 0, sc_idx)),
                     pl.BlockSpec((bb, 1, N), lambda j, i: (ij(i)[0], 0, sh_idx))]
        args += [gain.reshape(1, N), mod, mod]
        assert row0 % (bb * tt) == 0
        off = row0 // (bb * tt)
        out_specs = [ospec, pl.BlockSpec((bb * tt, N // 2), lambda j, i: (off + i, 0))]
        out_shape = [out_shape, jax.ShapeDtypeStruct((rows_total or B * T, N // 2), jnp.int32)]
        if rows_buf is not None:
            in_specs.append(pl.BlockSpec(memory_space=pl.ANY))
            args.append(rows_buf)
            aliases = {len(args) - 1: 1}
    return pl.pallas_call(
        functools.partial(_linear_kernel, residual=x is not None, norm_next=next_norm is not None,
                          shared_rows=rows_buf is not None, n_main=nblk if n_extra else None),
        grid=(N // tn, nblk + n_extra),
        in_specs=in_specs,
        out_specs=out_specs,
        out_shape=out_shape,
        scratch_shapes=[pltpu.VMEM((K, tn), BF16)],
        input_output_aliases=aliases,
        compiler_params=_cparams(2),
        name="linear",
    )(*args)


def _gla_kernel(*refs, L, n_chunks, has_init):
    if has_init:
        q_ref, f_ref, i_ref, g_ref, lb_ref, on_ref, s0_ref, o_ref, so_ref, st_ref = refs
    else:
        q_ref, f_ref, i_ref, g_ref, lb_ref, on_ref, o_ref, so_ref, st_ref = refs
    t = pl.program_id(1)
    H = st_ref.shape[0]

    @pl.when(t == 0)
    def _():
        for h in range(H):
            if has_init:
                st_ref[h] = s0_ref[0, h].T
            else:
                st_ref[h] = jnp.zeros(st_ref.shape[1:], F32)

    lb = lb_ref[...]
    onorm = on_ref[...]
    row = lax.broadcasted_iota(jnp.int32, (L, L), 0)
    col = lax.broadcasted_iota(jnp.int32, (L, L), 1)
    causal = col <= row
    tri = causal.astype(BF16)

    def chunk(c, carry):
        rows = pl.ds(pl.multiple_of(c * L, L), L)

        def write_o(sl, o):
            o_ref[0, rows, sl] = o.astype(o_ref.dtype)

        _gla_chunk(q_ref[0, rows, :], f_ref[0, rows, :], i_ref[0, rows, :], g_ref[0, rows, :],
                   lb, onorm, tri, causal, st_ref, write_o)
        return carry

    lax.fori_loop(0, n_chunks, chunk, 0, unroll=4 if n_chunks % 4 == 0 else 1)

    @pl.when(t == pl.num_programs(1) - 1)
    def _():
        for h in range(H):
            so_ref[0, h] = st_ref[h].T


def _gla_chunk(q, f, v, g, lb, onorm, tri, causal, st_ref, write_o):
    L = q.shape[0]
    H = st_ref.shape[0]
    mid = L // 2 - 1
    q = _silu(q.astype(F32))
    fg = lb + (1.0 - lb) * jax.nn.sigmoid(f)
    k = 1.0 - fg
    v = v.astype(BF16)
    gate = _silu(g.astype(F32))
    logf = jnp.log(fg)
    hi = logf.astype(BF16)
    lo = (logf - hi.astype(F32)).astype(BF16)
    b = _dot(tri, hi) + _dot(tri, lo)
    b_mid = b[mid:mid + 1, :]
    b_last = b[L - 1:L, :]
    qa = q * jnp.exp(b - b_mid)
    kb = k * jnp.exp(b_mid - b)
    qe = (qa * jnp.exp(b_mid)).astype(BF16)
    kd = (kb * jnp.exp(b_last - b_mid)).astype(BF16)
    qa = qa.astype(BF16)
    kb = kb.astype(BF16)
    decay = jnp.exp(b_last)
    sls = [slice(h * HG_DK, (h + 1) * HG_DK) for h in range(H)]
    sts = [st_ref[h] for h in range(H)]
    scores = [_dot_nt(qa[:, sl], kb[:, sl]) for sl in sls]
    inter = [_dot_nt(qe[:, sl], st.astype(BF16)) for sl, st in zip(sls, sts)]
    outer = [_dot_tn(v[:, sl], kd[:, sl]) for sl in sls]
    intra = [_dot(jnp.where(causal, sc, 0.0).astype(BF16), v[:, sl]) for sc, sl in zip(scores, sls)]
    for h, sl in enumerate(sls):
        st_ref[h] = sts[h] * decay[:, sl] + outer[h]
        write_o(sl, _rms(inter[h] + intra[h], onorm[:, sl]) * gate[:, sl])


def gla(zqig, zf, lb, onorm_g, s0):
    B, T, D = zf.shape
    L = CHUNK if T % CHUNK == 0 else T
    tt = min(T, 512)
    n_chunks = tt // L
    H = HG_HEADS

    def zspec(part):
        return pl.BlockSpec((1, tt, D), lambda b, t: (b, t, part))

    hspec = pl.BlockSpec((1, D), lambda b, t: (0, 0))
    sspec = pl.BlockSpec((1, H, HG_DK, HG_DV), lambda b, t: (b, 0, 0, 0))
    in_specs = [zspec(0), zspec(0), zspec(1), zspec(2), hspec, hspec]
    args = [zqig, zf, zqig, zqig, lb.reshape(1, D), onorm_g.reshape(1, D)]
    if s0 is not None:
        in_specs.append(sspec)
        args.append(s0)
    return pl.pallas_call(
        functools.partial(_gla_kernel, L=L, n_chunks=n_chunks, has_init=s0 is not None),
        grid=(B, T // tt),
        in_specs=in_specs,
        out_specs=[pl.BlockSpec((1, tt, D), lambda b, t: (b, t, 0)), sspec],
        out_shape=[jax.ShapeDtypeStruct((B, T, D), BF16),
                   jax.ShapeDtypeStruct((B, H, HG_DK, HG_DV), F32)],
        scratch_shapes=[pltpu.VMEM((H, HG_DV, HG_DK), F32)],
        compiler_params=_cparams(2),
        name="gla",
    )(*args)


def _route_kernel(h_ref, rw_ref, bias_ref, pos_ref, w_ref, te_ref, nu_ref,
                  e_s, r_s, base_s, start_s, *, tile_rows):
    ph = pl.program_id(0)
    i = pl.program_id(1)
    M = h_ref.shape[0]
    half = h_ref.shape[1]
    G, E = N_GROUPS, N_EXPERTS // N_GROUPS
    e_flat = lax.broadcasted_iota(jnp.int32, (N_EXPERTS, M), 0)

    @pl.when(ph == 1)
    def _():
        @pl.when(i == 0)
        def _():
            cnt = base_s[...]
            padded = jnp.floor((cnt + (tile_rows - 1)) * (1.0 / tile_rows)) * tile_rows
            r = lax.broadcasted_iota(jnp.int32, (N_EXPERTS, N_EXPERTS), 0)
            c = lax.broadcasted_iota(jnp.int32, (N_EXPERTS, N_EXPERTS), 1)
            start = jnp.dot((c < r).astype(F32), padded, preferred_element_type=F32,
                            precision=lax.Precision.HIGHEST)
            start_s[...] = start
            te_ref[...] = (start * (1.0 / tile_rows)).astype(jnp.int32)
            nu_ref[...] = (padded * (1.0 / tile_rows)).astype(jnp.int32)

        start_col = start_s[:, :1]
        for k in range(TOP_K):
            hit = e_flat == e_s[i, k:k + 1, :]
            seg = jnp.sum(jnp.where(hit, start_col, 0.0), axis=0, keepdims=True)
            pos_ref[k:k + 1, :] = (seg + r_s[i, k:k + 1, :]).astype(jnp.int32)

    @pl.when(ph == 0)
    def _():
        _route_pass0(h_ref, rw_ref, bias_ref, w_ref, e_s, r_s, base_s, i, M, half, G, E)


def _route_pass0(h_ref, rw_ref, bias_ref, w_ref, e_s, r_s, base_s, i, M, half, G, E):
    @pl.when(i == 0)
    def _():
        base_s[...] = jnp.zeros_like(base_s)

    lo, hi = _unpack_pairs(h_ref[...])
    rw = rw_ref[...].astype(BF16)
    logits = _dot_nt(rw[:, :half], lo) + _dot_nt(rw[:, half:], hi)
    s = jax.nn.sigmoid(logits)
    sb = (s + bias_ref[...]).reshape(G, E, M)
    s = s.reshape(G, E, M)
    e_in = lax.broadcasted_iota(jnp.int32, (G, E, M), 1).astype(F32)
    g_id = lax.broadcasted_iota(jnp.int32, (G, 1, M), 0)
    e_id = lax.broadcasted_iota(jnp.int32, (G, E, M), 0).astype(F32) * E + e_in

    def all_max(a):
        return jnp.max(jnp.max(a, axis=0, keepdims=True), axis=1, keepdims=True)

    def all_min(a):
        return jnp.min(jnp.min(a, axis=0, keepdims=True), axis=1, keepdims=True)

    def all_sum(a):
        return jnp.sum(jnp.sum(a, axis=0, keepdims=True), axis=1, keepdims=True)

    m1 = jnp.max(sb, axis=1, keepdims=True)
    first = jnp.min(jnp.where(sb == m1, e_in, float(E)), axis=1, keepdims=True)
    m2 = jnp.max(jnp.where(e_in == first, NEG_INF, sb), axis=1, keepdims=True)
    gs = m1 + m2

    rank = jnp.zeros((G, 1, M), jnp.int32)
    for j in range(G):
        gj = gs[j:j + 1]
        beats = (gj > gs) | ((gj == gs) & (j < g_id))
        rank = rank + beats.astype(jnp.int32)
    gsel = rank < TOPK_GROUPS

    vals = jnp.where(gsel, sb, NEG_INF)
    selm = jnp.zeros((G, E, M), F32)
    chosen, score = [], []
    for _ in range(TOP_K):
        m = all_max(vals)
        first = all_min(jnp.where(vals == m, e_id, float(N_EXPERTS)))
        hit = e_id == first
        score.append(all_sum(jnp.where(hit, s, 0.0)))
        selm = jnp.where(hit, 1.0, selm)
        vals = jnp.where(hit, NEG_INF, vals)
        chosen.append(first)

    tot = score[0]
    for sc in score[1:]:
        tot = tot + sc
    norm = ROUTED_SCALE / tot

    selm = selm.reshape(N_EXPERTS, M)
    earlier = (lax.broadcasted_iota(jnp.int32, (M, M), 0)
               < lax.broadcasted_iota(jnp.int32, (M, M), 1)).astype(BF16)
    rank = (base_s[:, :1] + _dot(selm.astype(BF16), earlier)).reshape(G, E, M)
    base_s[...] = base_s[...] + jnp.sum(selm, axis=1, keepdims=True)
    for k in range(TOP_K):
        hit = e_id == chosen[k]
        e_s[i, k:k + 1, :] = chosen[k].reshape(1, M).astype(jnp.int32)
        r_s[i, k:k + 1, :] = all_sum(jnp.where(hit, rank, 0.0)).reshape(1, M)
        w_ref[k:k + 1, :] = (score[k] * norm).reshape(1, M)


def route(hp, router_w_t, router_bias, l, tile_rows, rows=512):
    N, half = hp.shape
    M = rows
    nT = N // M
    assert N % M == 0

    def p0(ph, i):
        return i * (1 - ph) + (nT - 1) * ph

    return pl.pallas_call(
        functools.partial(_route_kernel, tile_rows=tile_rows),
        grid=(2, nT),
        in_specs=[pl.BlockSpec((M, half), lambda ph, i: (p0(ph, i), 0)),
                  pl.BlockSpec((None, N_EXPERTS, 2 * half), lambda ph, i: (l, 0, 0)),
                  pl.BlockSpec((None, N_EXPERTS, 1), lambda ph, i: (l, 0, 0))],
        out_specs=[pl.BlockSpec((TOP_K, M), lambda ph, i: (0, i * ph)),
                   pl.BlockSpec((TOP_K, M), lambda ph, i: (0, p0(ph, i))),
                   pl.BlockSpec((N_EXPERTS, 128), lambda ph, i: (0, 0)),
                   pl.BlockSpec((N_EXPERTS, 128), lambda ph, i: (0, 0))],
        out_shape=[jax.ShapeDtypeStruct((TOP_K, N), jnp.int32),
                   jax.ShapeDtypeStruct((TOP_K, N), F32),
                   jax.ShapeDtypeStruct((N_EXPERTS, 128), jnp.int32),
                   jax.ShapeDtypeStruct((N_EXPERTS, 128), jnp.int32)],
        scratch_shapes=[pltpu.VMEM((nT, TOP_K, M), jnp.int32), pltpu.VMEM((nT, TOP_K, M), F32),
                        pltpu.VMEM((N_EXPERTS, 128), F32), pltpu.VMEM((N_EXPERTS, 128), F32)],
        compiler_params=_cparams(2),
        name="route",
    )(hp, router_w_t, router_bias.reshape(-1, N_EXPERTS, 1))


def _sc_mesh():
    return plsc.VectorSubcoreMesh(core_axis_name="core", subcore_axis_name="subcore")


def sc_invert(pos_flat, n_tok, n_out):
    n = pos_flat.shape[0]
    per = n_out // SC_WORKERS
    chunk = n_tok
    assert n_out % SC_WORKERS == 0 and per % SC_LANES == 0
    assert n_tok % chunk == 0 and n % chunk == 0 and chunk % SC_LANES == 0
    cp = pltpu.CompilerParams()
    if "needs_layout_passes" in pltpu.CompilerParams.__dataclass_fields__:
        cp = dataclasses.replace(cp, needs_layout_passes=False)

    @functools.partial(
        pl.kernel, out_type=jax.ShapeDtypeStruct((n_out,), jnp.int32), mesh=_sc_mesh(),
        scratch_types=[pltpu.VMEM((chunk,), jnp.int32), pltpu.VMEM((per,), jnp.int32)],
        compiler_params=cp, name="sc_invert")
    def k(pos_hbm, src_hbm, pos_v, src_v):
        wid = lax.axis_index("subcore") * SC_CORES + lax.axis_index("core")
        lo = wid * per
        lane = lax.iota(jnp.int32, SC_LANES)

        @pl.loop(0, per, step=SC_LANES)
        def _(r):
            src_v[pl.ds(r, SC_LANES)] = lax.rem(lo + r + lane, n_tok)

        @pl.loop(0, n // chunk)
        def _(c):
            base = c * chunk
            pltpu.sync_copy(pos_hbm.at[pl.ds(base, chunk)], pos_v)
            tok0 = lax.rem(base, n_tok)

            @plsc.parallel_loop(0, chunk, step=SC_LANES, unroll=8)
            def _(r):
                p = pos_v[pl.ds(r, SC_LANES)] - lo
                mine = (p >= 0) & (p < per)
                plsc.store_scatter(src_v, [jnp.where(mine, p, 0)], tok0 + r + lane, mask=mine)

        pltpu.sync_copy(src_v, src_hbm.at[pl.ds(lo, per)])

    return k(pos_flat)


def sc_gather(x, idx):
    n = idx.shape[0]
    dim = x.shape[1]
    assert n % (SC_WINDOW * SC_WORKERS) == 0

    @functools.partial(
        pl.kernel, out_type=jax.ShapeDtypeStruct((n, dim), x.dtype), mesh=_sc_mesh(),
        scratch_types=[], name="sc_gather")
    def k(x_hbm, i_hbm, o_hbm):
        def body(i_vmem, o_vmem):
            pltpu.sync_copy(x_hbm.at[i_vmem.at[0]], o_vmem)

        pltpu.emit_pipeline(
            body, grid=(n // SC_WINDOW,),
            in_specs=[pl.BlockSpec((1, SC_WINDOW), index_map=lambda i: (i, 0))],
            out_specs=[pl.BlockSpec((SC_WINDOW, dim), index_map=lambda i: (i, 0))],
            core_axis_name=("core", "subcore"),
            dimension_semantics=(pltpu.PARALLEL,),
        )(i_hbm, o_hbm)

    return k(x, idx.reshape(n // SC_WINDOW, SC_WINDOW))


def _moe_gemm_kernel(ts_ref, tn_ref, x_hbm, wi_ref, wo_ref, o_hbm, wi_b, wo_b, xbuf, obuf, in_sem, out_sem,
                     *, tile_rows, n_tiles):
    e = pl.program_id(0)
    last = pl.num_programs(0) - 1
    t0 = ts_ref[e]
    n = tn_ref[e]
    n_used = ts_ref[last] + tn_ref[last]

    def x_copy(g, slot):
        rows = pl.ds(pl.multiple_of(g * tile_rows, tile_rows), tile_rows)
        return pltpu.make_async_copy(x_hbm.at[rows], xbuf.at[slot], in_sem.at[slot])

    def o_copy(g, slot):
        rows = pl.ds(pl.multiple_of(g * tile_rows, tile_rows), tile_rows)
        return pltpu.make_async_copy(obuf.at[slot], o_hbm.at[rows], out_sem.at[slot])

    @pl.when(e == 0)
    def _():
        for g0 in range(MOE_NBUF - 1):
            @pl.when(g0 < n_used)
            def _():
                x_copy(g0, g0).start()

    @pl.when(n > 0)
    def _():
        wi_b[...] = wi_ref[...].astype(BF16)
        wo_b[...] = wo_ref[...].astype(BF16)

    def tile(i, carry):
        g = t0 + i
        slot = lax.rem(g, MOE_NBUF)
        x_copy(g, slot).wait()
        ahead = g + (MOE_NBUF - 1)

        @pl.when(ahead < n_used)
        def _():
            x_copy(ahead, lax.rem(ahead, MOE_NBUF)).start()

        @pl.when(g >= MOE_NBUF)
        def _():
            o_copy(g - MOE_NBUF, slot).wait()

        rows = tile_rows // MOE_SUB
        half = xbuf.shape[2]
        xs = [_unpack_pairs(xbuf[slot, r * rows:(r + 1) * rows, :]) for r in range(MOE_SUB)]
        hus = [_dot(lo, wi_b[:half, :]) + _dot(hi, wi_b[half:, :]) for lo, hi in xs]
        acts = [(_silu(hu[:, :EXPERT_FF]) * hu[:, EXPERT_FF:]).astype(BF16) for hu in hus]
        outs = [_dot(act, wo_b[...]) for act in acts]
        for r, out in enumerate(outs):
            obuf[slot, r * rows:(r + 1) * rows, :] = _pack_pairs(out)
        o_copy(g, slot).start()
        return carry

    lax.fori_loop(0, n, tile, 0)

    @pl.when(e == last)
    def _():
        for back in range(MOE_NBUF, 0, -1):
            @pl.when(n_used >= back)
            def _():
                o_copy(n_used - back, lax.rem(n_used - back, MOE_NBUF)).wait()

        obuf[0] = jnp.zeros(obuf.shape[1:], obuf.dtype)

        def clear(g, carry):
            cp = o_copy(g, 0)
            cp.start()
            cp.wait()
            return carry

        lax.fori_loop(n_used, n_tiles, clear, 0)


def moe_gemm(xs, tile_start, tile_count, exp_w_in, exp_w_out, l, tile_rows, n_tiles):
    P, half = xs.shape
    D = 2 * half
    assert P == n_tiles * tile_rows
    hbm = pl.BlockSpec(memory_space=pl.ANY)
    grid_spec = pltpu.PrefetchScalarGridSpec(
        num_scalar_prefetch=2,
        grid=(N_EXPERTS,),
        in_specs=[hbm,
                  pl.BlockSpec((None, None, D, 2 * EXPERT_FF), lambda e, ts, tn: (l, e, 0, 0)),
                  pl.BlockSpec((None, None, EXPERT_FF, D), lambda e, ts, tn: (l, e, 0, 0))],
        out_specs=hbm,
        scratch_shapes=[pltpu.VMEM((D, 2 * EXPERT_FF), BF16), pltpu.VMEM((EXPERT_FF, D), BF16),
                        pltpu.VMEM((MOE_NBUF, tile_rows, half), jnp.int32),
                        pltpu.VMEM((MOE_NBUF, tile_rows, half), jnp.int32),
                        pltpu.SemaphoreType.DMA((MOE_NBUF,)), pltpu.SemaphoreType.DMA((MOE_NBUF,))],
    )
    return pl.pallas_call(
        functools.partial(_moe_gemm_kernel, tile_rows=tile_rows, n_tiles=n_tiles),
        grid_spec=grid_spec,
        out_shape=jax.ShapeDtypeStruct((P, half), jnp.int32),
        compiler_params=_cparams(1),
        name="moe_gemm",
    )(tile_start, tile_count, xs, exp_w_in, exp_w_out)


def _moe_combine_kernel(*refs, final, norm_next, shared_kv):
    it = iter(refs)
    y_ref, w_ref, h_ref, si_ref, so_ref, x_ref, g2_ref = (next(it) for _ in range(7))
    fg_ref = next(it) if final else None
    ng_ref, nsc_ref, nsh_ref = (next(it) for _ in range(3)) if norm_next else (None,) * 3
    kg_ref, wkv_ref, lg_ref, cos_ref, sin_ref = (next(it) for _ in range(5)) if shared_kv else (None,) * 5
    o_ref = next(it)
    hn_ref = next(it) if norm_next else None
    lat_ref, kr_ref = (next(it), next(it)) if shared_kv else (None, None)
    si_b, so_b = next(it), next(it)
    wkv_b = next(it) if shared_kv else None

    @pl.when(pl.program_id(0) == 0)
    def _():
        si_b[...] = si_ref[...].astype(BF16)
        so_b[...] = so_ref[...].astype(BF16)
        if shared_kv:
            wkv_b[...] = wkv_ref[...].astype(BF16)

    bb, tt, D = x_ref.shape
    half = D // 2
    w = w_ref[...].T
    acc_lo = jnp.zeros((bb * tt, half), F32)
    acc_hi = jnp.zeros((bb * tt, half), F32)
    for k in range(TOP_K):
        lo, hi = _unpack_pairs(y_ref[k], F32)
        acc_lo = acc_lo + w[:, k:k + 1] * lo
        acc_hi = acc_hi + w[:, k:k + 1] * hi
    hlo, hhi = _unpack_pairs(h_ref[...])
    hu = _dot(hlo, si_b[:half, :]) + _dot(hhi, si_b[half:, :])
    act = (_silu(hu[:, :SHARED_FF]) * hu[:, SHARED_FF:]).astype(BF16)
    y = jnp.concatenate([acc_lo, acc_hi], axis=-1) + _dot(act, so_b[...])
    x_new = x_ref[...] + g2_ref[...] * y.reshape(bb, tt, D)
    o_ref[...] = _rms(x_new, fg_ref[...]) if final else x_new
    if norm_next:
        hn = _rms(x_new, ng_ref[...]) * (1.0 + nsc_ref[...]) + nsh_ref[...]
        hn_ref[...] = hn.astype(hn_ref.dtype)
    if shared_kv:
        xn = _rms(x_new, kg_ref[...]).reshape(bb * tt, D).astype(BF16)
        z = _dot(xn, wkv_b[...])
        lat_ref[...] = _rms(z[:, :KV_LORA], lg_ref[...]).reshape(bb, tt, KV_LORA)
        zr = z[:, KV_LORA:KV_LORA + MLA_ROPE].reshape(bb, tt, MLA_ROPE)
        zq = z[:, KV_LORA + 128:KV_LORA + 128 + MLA_ROPE].reshape(bb, tt, MLA_ROPE)
        kr_ref[...] = zr * cos_ref[...] + zq * sin_ref[...]


def moe_combine(y8, w8, hp, sh_w_in, sh_w_out, x, mod, gate_idx, l, row0, final_g=None, next_norm=None,
                shared_kv=None, rows=512):
    B, T, D = x.shape
    half = D // 2
    bb, tt, nblk, ij = _row_blocks(B, T, rows)
    M = bb * tt
    assert row0 % M == 0
    off = row0 // M
    xspec = pl.BlockSpec((bb, tt, D), lambda i: ij(i) + (0,))
    in_specs = [pl.BlockSpec((TOP_K, M, half), lambda i: (0, off + i, 0)),
                pl.BlockSpec((TOP_K, M), lambda i: (0, off + i)),
                pl.BlockSpec((M, half), lambda i: (off + i, 0)),
                pl.BlockSpec((None, D, 2 * SHARED_FF), lambda i: (l, 0, 0)),
                pl.BlockSpec((None, SHARED_FF, D), lambda i: (l, 0, 0)),
                xspec,
                pl.BlockSpec((bb, 1, D), lambda i: (ij(i)[0], 0, gate_idx))]
    args = [y8, w8, hp, sh_w_in, sh_w_out, x, mod]
    out_specs = xspec
    out_shape = jax.ShapeDtypeStruct((B, T, D), F32)
    if final_g is not None:
        assert next_norm is None
        in_specs.append(pl.BlockSpec((1, D), lambda i: (0, 0)))
        args.append(final_g.reshape(1, D))
    if next_norm is not None:
        gain, mod_next, sc_idx, sh_idx = next_norm
        in_specs += [pl.BlockSpec((1, D), lambda i: (0, 0)),
                     pl.BlockSpec((bb, 1, D), lambda i: (ij(i)[0], 0, sc_idx)),
                     pl.BlockSpec((bb, 1, D), lambda i: (ij(i)[0], 0, sh_idx))]
        args += [gain.reshape(1, D), mod_next, mod_next]
        out_specs = [xspec, xspec]
        out_shape = [out_shape, jax.ShapeDtypeStruct((B, T, D), BF16)]
    scratch = [pltpu.VMEM((D, 2 * SHARED_FF), BF16), pltpu.VMEM((SHARED_FF, D), BF16)]
    if shared_kv is not None:
        kv_in_g, w_kv, kv_lat_g, cos32, sin32 = shared_kv
        tspec = pl.BlockSpec((tt, MLA_ROPE), lambda i: (ij(i)[1], 0))
        in_specs += [pl.BlockSpec((1, D), lambda i: (0, 0)),
                     pl.BlockSpec(w_kv.shape, lambda i: (0, 0)),
                     pl.BlockSpec((1, KV_LORA), lambda i: (0, 0)),
                     tspec, tspec]
        args += [kv_in_g.reshape(1, D), w_kv, kv_lat_g.reshape(1, KV_LORA), cos32, sin32]
        out_specs = list(out_specs) if isinstance(out_specs, list) else [out_specs]
        out_shape = list(out_shape) if isinstance(out_shape, list) else [out_shape]
        out_specs += [pl.BlockSpec((bb, tt, KV_LORA), lambda i: ij(i) + (0,)),
                      pl.BlockSpec((bb, tt, MLA_ROPE), lambda i: ij(i) + (0,))]
        out_shape += [jax.ShapeDtypeStruct((B, T, KV_LORA), F32), jax.ShapeDtypeStruct((B, T, MLA_ROPE), F32)]
        scratch.append(pltpu.VMEM(w_kv.shape, BF16))
    return pl.pallas_call(
        functools.partial(_moe_combine_kernel, final=final_g is not None, norm_next=next_norm is not None,
                          shared_kv=shared_kv is not None),
        grid=(nblk,),
        in_specs=in_specs,
        out_specs=out_specs,
        out_shape=out_shape,
        scratch_shapes=scratch,
        compiler_params=_cparams(1),
        name="moe_combine",
    )(*args)


def _kv_expand_kernel(lat_ref, kr_ref, wk_ref, ek_ref, wvt_ref, ones_ref, k_ref, vt_ref):
    lat = lat_ref[0].astype(BF16)
    kr = kr_ref[0].astype(BF16)
    k = _dot(lat, wk_ref[...].astype(BF16)) + _dot(kr, ek_ref[...].astype(BF16))
    k_ref[0] = k.astype(k_ref.dtype)
    vt = _dot_nt(wvt_ref[...].astype(BF16), lat) + ones_ref[...]
    vt_ref[0] = vt.astype(vt_ref.dtype)


def kv_expand(lat, kr, wk_pad, ek, wvt_ext, ones_col, rows=512):
    B, T, _ = lat.shape
    tt = rows
    NK, NVT = wk_pad.shape[1], wvt_ext.shape[0]

    def full(a):
        return pl.BlockSpec(a.shape, lambda b, t: (0, 0))

    def rowspec(n):
        return pl.BlockSpec((1, tt, n), lambda b, t: (b, t, 0))

    return pl.pallas_call(
        _kv_expand_kernel,
        grid=(B, T // tt),
        in_specs=[rowspec(KV_LORA), rowspec(MLA_ROPE), full(wk_pad), full(ek), full(wvt_ext), full(ones_col)],
        out_specs=[rowspec(NK), pl.BlockSpec((1, NVT, tt), lambda b, t: (b, 0, t))],
        out_shape=[jax.ShapeDtypeStruct((B, T, NK), BF16), jax.ShapeDtypeStruct((B, NVT, T), BF16)],
        compiler_params=_cparams(2),
        name="kv_expand",
    )(lat, kr, wk_pad, ek, wvt_ext, ones_col)


def _query_kernel(h_ref, wdq_ref, qg_ref, wq_ref, wqr_ref, c_ref, s_ref, q_ref, wdq_b, wq_b, wqr_b):
    @pl.when(pl.program_id(0) == 0)
    def _():
        wdq_b[...] = wdq_ref[...].astype(BF16)
        wq_b[...] = wq_ref[...].astype(BF16)
        wqr_b[...] = wqr_ref[...].astype(BF16)

    bb, tt, D = h_ref.shape
    h = h_ref[...].reshape(bb * tt, D)
    cq = _rms(_dot(h, wdq_b[...]), qg_ref[...]).astype(BF16)
    q1 = _dot(cq, wq_b[...]).reshape(bb, tt, -1)
    q2 = _dot(cq, wqr_b[...]).reshape(bb, tt, -1)
    c = c_ref[...]
    s = s_ref[...]
    for hd in range(MLA_HEADS):
        sl = slice(hd * HEAD_PAD, (hd + 1) * HEAD_PAD)
        q_ref[:, :, sl] = (q1[:, :, sl] * c + q2[:, :, sl] * s).astype(q_ref.dtype)


def mla_queries(h, w_dq, q_norm_g, wq_pad, wq_rot, l, c128, s128, rows=512):
    B, T, D = h.shape
    bb, tt, nblk, ij = _row_blocks(B, T, rows)
    NQ = wq_pad.shape[-1]
    tspec = pl.BlockSpec((tt, HEAD_PAD), lambda i: (ij(i)[1], 0))
    return pl.pallas_call(
        _query_kernel,
        grid=(nblk,),
        in_specs=[pl.BlockSpec((bb, tt, D), lambda i: ij(i) + (0,)),
                  pl.BlockSpec((None, D, Q_LORA), lambda i: (l, 0, 0)),
                  pl.BlockSpec((None, 1, Q_LORA), lambda i: (l, 0, 0)),
                  pl.BlockSpec((None, Q_LORA, NQ), lambda i: (l, 0, 0)),
                  pl.BlockSpec((None, Q_LORA, NQ), lambda i: (l, 0, 0)),
                  tspec, tspec],
        out_specs=pl.BlockSpec((bb, tt, NQ), lambda i: ij(i) + (0,)),
        out_shape=jax.ShapeDtypeStruct((B, T, NQ), BF16),
        scratch_shapes=[pltpu.VMEM((D, Q_LORA), BF16), pltpu.VMEM((Q_LORA, NQ), BF16),
                        pltpu.VMEM((Q_LORA, NQ), BF16)],
        compiler_params=_cparams(1),
        name="mla_queries",
    )(h, w_dq, q_norm_g.reshape(-1, 1, Q_LORA), wq_pad, wq_rot, c128, s128)


def _query_t_kernel(h_ref, wdq_ref, qg_ref, wqt_ref, wqrt_ref, cos_ref, sin_ref, qt_ref, wdq_b, wqt_b, wqrt_b):
    @pl.when((pl.program_id(0) == 0) & (pl.program_id(1) == 0))
    def _():
        wdq_b[...] = wdq_ref[...].astype(BF16)
        wqt_b[...] = wqt_ref[...].astype(BF16)
        wqrt_b[...] = wqrt_ref[...].astype(BF16)

    cq = _rms(_dot(h_ref[0], wdq_b[...]), qg_ref[...]).astype(BF16)
    q1 = _dot_nt(wqt_b[...], cq)
    q2 = _dot_nt(wqrt_b[...], cq)
    cos = cos_ref[...]
    sin = sin_ref[...]
    pad = jnp.zeros((HEAD_PAD - MLA_NOPE - MLA_ROPE, q1.shape[1]), qt_ref.dtype)
    for hd in range(MLA_HEADS):
        r0 = hd * HEAD_PAD
        rope = (q1[r0 + MLA_NOPE:r0 + MLA_NOPE + MLA_ROPE] * cos
                + q2[hd * MLA_ROPE:(hd + 1) * MLA_ROPE] * sin)
        qt_ref[0, r0:r0 + MLA_NOPE, :] = (q1[r0:r0 + MLA_NOPE] * Q_PRESCALE).astype(qt_ref.dtype)
        qt_ref[0, r0 + MLA_NOPE:r0 + MLA_NOPE + MLA_ROPE, :] = rope.astype(qt_ref.dtype)
        qt_ref[0, r0 + MLA_NOPE + MLA_ROPE:r0 + HEAD_PAD, :] = pad


def mla_queries_t(h, w_dq, q_norm_g, wq_t, wqr_t, l, cos_t, sin_t, rows=512):
    B, T, D = h.shape
    tt = rows
    NQ = wq_t.shape[1]
    NR = wqr_t.shape[1]
    tspec = pl.BlockSpec((MLA_ROPE, tt), lambda b, t: (0, t))
    return pl.pallas_call(
        _query_t_kernel,
        grid=(B, T // tt),
        in_specs=[pl.BlockSpec((1, tt, D), lambda b, t: (b, t, 0)),
                  pl.BlockSpec((None, D, Q_LORA), lambda b, t: (l, 0, 0)),
                  pl.BlockSpec((None, 1, Q_LORA), lambda b, t: (l, 0, 0)),
                  pl.BlockSpec((None, NQ, Q_LORA), lambda b, t: (l, 0, 0)),
                  pl.BlockSpec((None, NR, Q_LORA), lambda b, t: (l, 0, 0)),
                  tspec, tspec],
        out_specs=pl.BlockSpec((1, NQ, tt), lambda b, t: (b, 0, t)),
        out_shape=jax.ShapeDtypeStruct((B, NQ, T), BF16),
        scratch_shapes=[pltpu.VMEM((D, Q_LORA), BF16), pltpu.VMEM((NQ, Q_LORA), BF16),
                        pltpu.VMEM((NR, Q_LORA), BF16)],
        compiler_params=_cparams(2),
        name="mla_queries_t",
    )(h, w_dq, q_norm_g.reshape(-1, 1, Q_LORA), wq_t, wqr_t, cos_t, sin_t)


def _attn_prompt_kernel(qi_tab, ki_tab, qt_ref, k_ref, vt_ref, o_ref, *scratch, tq, tk):
    H = MLA_HEADS
    m_refs, l_refs, acc_refs = scratch[:H], scratch[H:2 * H], scratch[2 * H:]
    p_id = pl.program_id(1)
    qi = qi_tab[p_id]
    ki = ki_tab[p_id]

    @pl.when(ki == 0)
    def _():
        for hd in range(H):
            m_refs[hd][...] = jnp.full(m_refs[hd].shape, NEG_INF, F32)
            l_refs[hd][...] = jnp.zeros(l_refs[hd].shape, F32)
            acc_refs[hd][...] = jnp.zeros(acc_refs[hd].shape, F32)

    def block(masked):
        if masked:
            kchunk = (ki * tk + lax.broadcasted_iota(jnp.int32, (tk, tq), 0)) // CHUNK
            qchunk = (qi * tq + lax.broadcasted_iota(jnp.int32, (tk, tq), 1)) // CHUNK
            mask = kchunk <= qchunk
        def scores(hd):
            sl = slice(hd * HEAD_PAD, (hd + 1) * HEAD_PAD)
            return _dot(k_ref[0, :, sl], qt_ref[0, sl, :])

        pending = [scores(hd) for hd in range(ATTN_LOOKAHEAD)]
        for hd in range(H):
            if hd + ATTN_LOOKAHEAD < H:
                pending.append(scores(hd + ATTN_LOOKAHEAD))
            s = pending.pop(0)
            if masked:
                s = jnp.where(mask, s, NEG_INF)
            m_prev = m_refs[hd][...]
            m_new = jnp.maximum(m_prev, jnp.max(s, axis=0, keepdims=True))
            a = jnp.exp2(m_prev - m_new)
            p = jnp.exp2(s - m_new).astype(BF16)
            pv = _dot(vt_ref[0, hd * V_ROWS:(hd + 1) * V_ROWS, :], p)
            acc_refs[hd][...] = a * acc_refs[hd][...] + pv[:MLA_V]
            l_refs[hd][...] = a * l_refs[hd][...] + pv[MLA_V:MLA_V + 1]
            m_refs[hd][...] = m_new

    @pl.when(ki < qi)
    def _():
        block(False)

    @pl.when(ki == qi)
    def _():
        block(True)
        o_t = jnp.concatenate([acc_refs[hd][...] / l_refs[hd][...] for hd in range(H)], axis=0)
        o_ref[0] = o_t.T.astype(o_ref.dtype)


def attn_prompt(qt, k, vt, tq=256):
    B, NQ, T = qt.shape
    NVT = vt.shape[1]
    NV = MLA_HEADS * MLA_V
    tk = tq
    assert tq % CHUNK == 0
    nq = T // tq
    pairs = [(a, b) for a in range(nq) for b in range(a + 1)]
    qi_tab = jnp.asarray([a for a, _ in pairs], jnp.int32)
    ki_tab = jnp.asarray([b for _, b in pairs], jnp.int32)
    grid_spec = pltpu.PrefetchScalarGridSpec(
        num_scalar_prefetch=2,
        grid=(B, len(pairs)),
        in_specs=[pl.BlockSpec((1, NQ, tq), lambda b, p, qt, kt: (b, 0, qt[p])),
                  pl.BlockSpec((1, tk, NQ), lambda b, p, qt, kt: (b, kt[p], 0)),
                  pl.BlockSpec((1, NVT, tk), lambda b, p, qt, kt: (b, 0, kt[p]))],
        out_specs=pl.BlockSpec((1, tq, NV), lambda b, p, qt, kt: (b, qt[p], 0)),
        scratch_shapes=([pltpu.VMEM((1, tq), F32)] * (2 * MLA_HEADS)
                        + [pltpu.VMEM((MLA_V, tq), F32)] * MLA_HEADS),
    )
    return pl.pallas_call(
        functools.partial(_attn_prompt_kernel, tq=tq, tk=tk),
        grid_spec=grid_spec,
        out_shape=jax.ShapeDtypeStruct((B, T, NV), BF16),
        compiler_params=_cparams(2),
        name="attn_prompt",
    )(qi_tab, ki_tab, qt, k, vt)


def _absorb_kernel(q_ref, m_ref, o_ref):
    o_ref[...] = _dot(q_ref[...], m_ref[...].astype(BF16)).astype(o_ref.dtype)


def absorb_queries(q2d, m_abs):
    N = q2d.shape[0]
    H, _, W = m_abs.shape
    return pl.pallas_call(
        _absorb_kernel,
        grid=(H,),
        in_specs=[pl.BlockSpec((N, HEAD_PAD), lambda h: (0, h)),
                  pl.BlockSpec((None, HEAD_PAD, W), lambda h: (h, 0, 0))],
        out_specs=pl.BlockSpec((None, N, W), lambda h: (h, 0, 0)),
        out_shape=jax.ShapeDtypeStruct((H, N, W), BF16),
        compiler_params=_cparams(1),
        name="absorb_queries",
    )(q2d, m_abs)


def _attn_sample_kernel(q_ref, lat_ref, kr_ref, nlat_ref, nkr_ref, o_ref, m_ref, l_ref, acc_ref):
    kb = pl.program_id(1)
    H, Q, W = q_ref.shape
    q = q_ref[...].reshape(H * Q, W)
    q_lat = q[:, :KV_LORA]
    q_rope = q[:, KV_LORA:KV_LORA + MLA_ROPE]

    def update(lat_tile, kr_tile, n_sub, kr_transposed):
        sub = lat_tile.shape[0] // n_sub
        lats = [lat_tile[j * sub:(j + 1) * sub, :].astype(BF16) for j in range(n_sub)]
        if kr_transposed:
            krs = [kr_tile[:, j * sub:(j + 1) * sub].astype(BF16) for j in range(n_sub)]
            ss = [_dot_nt(q_lat, lat) + _dot(q_rope, kr) for lat, kr in zip(lats, krs)]
        else:
            krs = [kr_tile[j * sub:(j + 1) * sub, :].astype(BF16) for j in range(n_sub)]
            ss = [_dot_nt(q_lat, lat) + _dot_nt(q_rope, kr) for lat, kr in zip(lats, krs)]
        m_prev = m_ref[...]
        m_new = m_prev
        for s in ss:
            m_new = jnp.maximum(m_new, jnp.max(s, axis=-1, keepdims=True))
        a = jnp.exp2(m_prev - m_new)
        ps = [jnp.exp2(s - m_new[:, :1]) for s in ss]
        pv = _dot(ps[0].astype(BF16), lats[0])
        psum = jnp.sum(ps[0], axis=-1, keepdims=True)
        for p, lat in zip(ps[1:], lats[1:]):
            pv = pv + _dot(p.astype(BF16), lat)
            psum = psum + jnp.sum(p, axis=-1, keepdims=True)
        l_ref[...] = a * l_ref[...] + psum
        m_ref[...] = m_new
        acc_ref[...] = jnp.concatenate([a, a], axis=-1) * acc_ref[...] + pv

    @pl.when(kb == 0)
    def _():
        m_ref[...] = jnp.full_like(m_ref, NEG_INF)
        l_ref[...] = jnp.zeros_like(l_ref)
        acc_ref[...] = jnp.zeros_like(acc_ref)
        update(nlat_ref[0], nkr_ref[0], 1, False)

    update(lat_ref[0], kr_ref[0], SAMPLE_KEY_SUB, True)

    @pl.when(kb == pl.num_programs(1) - 1)
    def _():
        lsum = l_ref[...]
        o = acc_ref[...] / jnp.concatenate([lsum, lsum], axis=-1)
        o_ref[...] = o.reshape(H, Q, KV_LORA).astype(o_ref.dtype)


def attn_sample(q_abs, cache_lat, cache_kr_t, new_lat, new_kr, tk=4096):
    H, N, W = q_abs.shape
    B, P, _ = cache_lat.shape
    Q = new_lat.shape[1]
    qpos = P + np.arange(Q)
    kpos = np.arange(P + Q)
    assert bool(np.all((kpos // CHUNK)[None, :] <= (qpos // CHUNK)[:, None]))
    return pl.pallas_call(
        _attn_sample_kernel,
        grid=(B, P // tk),
        in_specs=[pl.BlockSpec((H, Q, W), lambda b, kb: (0, b, 0)),
                  pl.BlockSpec((1, tk, KV_LORA), lambda b, kb: (b, kb, 0)),
                  pl.BlockSpec((1, MLA_ROPE, tk), lambda b, kb: (b, 0, kb)),
                  pl.BlockSpec((1, Q, KV_LORA), lambda b, kb: (b, 0, 0)),
                  pl.BlockSpec((1, Q, MLA_ROPE), lambda b, kb: (b, 0, 0))],
        out_specs=pl.BlockSpec((H, Q, KV_LORA), lambda b, kb: (0, b, 0)),
        out_shape=jax.ShapeDtypeStruct((H, N, KV_LORA), BF16),
        scratch_shapes=[pltpu.VMEM((H * Q, 128), F32), pltpu.VMEM((H * Q, 128), F32),
                        pltpu.VMEM((H * Q, KV_LORA), F32)],
        compiler_params=_cparams(2),
        name="attn_sample",
    )(q_abs, cache_lat, cache_kr_t, new_lat, new_kr)


def _unabsorb_kernel(o_ref, w_ref, out_ref):
    out_ref[...] = (_dot(o_ref[0], w_ref[0].astype(BF16))
                    + _dot(o_ref[1], w_ref[1].astype(BF16))).astype(out_ref.dtype)


def unabsorb(o_lat, wuv_pad):
    H, N, R = o_lat.shape
    return pl.pallas_call(
        _unabsorb_kernel,
        grid=(H // 2,),
        in_specs=[pl.BlockSpec((2, N, R), lambda p: (p, 0, 0)),
                  pl.BlockSpec((2, R, 128), lambda p: (p, 0, 0))],
        out_specs=pl.BlockSpec((N, 128), lambda p: (0, p)),
        out_shape=jax.ShapeDtypeStruct((N, (H // 2) * 128), BF16),
        compiler_params=_cparams(1),
        name="unabsorb",
    )(o_lat, wuv_pad)


def _rope_tables(pos):
    half = MLA_ROPE // 2
    inv = 1.0 / (ROPE_THETA ** (np.arange(half, dtype=np.float64) * 2.0 / MLA_ROPE))
    ang = np.asarray(pos, np.float64)[:, None] * inv[None, :]
    cos = np.concatenate([np.cos(ang), np.cos(ang)], axis=-1)
    sin = np.concatenate([np.sin(ang), np.sin(ang)], axis=-1)
    T = cos.shape[0]
    c128 = np.zeros((T, HEAD_PAD)); s128 = np.zeros((T, HEAD_PAD))
    c128[:, :MLA_NOPE] = 1.0
    c128[:, MLA_NOPE:MLA_NOPE + MLA_ROPE] = cos
    s128[:, MLA_NOPE:MLA_NOPE + MLA_ROPE] = sin
    return dict(cos32=jnp.asarray(cos, F32), sin32=jnp.asarray(sin, F32),
                c128=jnp.asarray(c128 * Q_PRESCALE, F32), s128=jnp.asarray(s128 * Q_PRESCALE, F32),
                cos_t=jnp.asarray(cos.T * Q_PRESCALE, F32), sin_t=jnp.asarray(sin.T * Q_PRESCALE, F32))


def _rot_half_cols(w):
    half = w.shape[-1] // 2
    return jnp.concatenate([-w[..., half:], w[..., :half]], axis=-1)


def _prep_weights(w_dkv, w_uk, w_uv, w_uq, router_w):
    D = D_MODEL
    w_lat, w_rope = w_dkv[:, :KV_LORA], w_dkv[:, KV_LORA:]
    pad96 = jnp.zeros((D, 128 - MLA_ROPE), F32)
    w_kv = jnp.concatenate([w_lat, w_rope, pad96, _rot_half_cols(w_rope), pad96], axis=-1)

    zpad = HEAD_PAD - MLA_NOPE
    wk_pad = jnp.pad(w_uk, ((0, 0), (0, 0), (0, zpad))).reshape(KV_LORA, MLA_HEADS * HEAD_PAD)
    ek = jnp.zeros((MLA_ROPE, MLA_HEADS, HEAD_PAD), F32)
    ek = ek.at[:, :, MLA_NOPE:MLA_NOPE + MLA_ROPE].set(
        jnp.broadcast_to(jnp.eye(MLA_ROPE, dtype=F32)[:, None, :], (MLA_ROPE, MLA_HEADS, MLA_ROPE)))
    ek = ek.reshape(MLA_ROPE, MLA_HEADS * HEAD_PAD)
    wvt = jnp.transpose(w_uv, (1, 2, 0))
    wvt_ext = jnp.pad(wvt, ((0, 0), (0, V_ROWS - MLA_V), (0, 0))).reshape(MLA_HEADS * V_ROWS, KV_LORA)
    ones_col = jnp.tile((jnp.arange(V_ROWS) >= MLA_V).astype(F32), MLA_HEADS).reshape(-1, 1)

    nb = w_uq.shape[0]
    qn, qr = w_uq[..., :MLA_NOPE], w_uq[..., MLA_NOPE:]
    z32 = jnp.zeros(qr.shape[:-1] + (HEAD_PAD - MLA_NOPE - MLA_ROPE,), F32)
    wq_pad = jnp.concatenate([qn, qr, z32], axis=-1).reshape(nb, Q_LORA, MLA_HEADS * HEAD_PAD)
    wq_rot = jnp.concatenate([jnp.zeros_like(qn), _rot_half_cols(qr), z32], axis=-1)
    wq_rot = wq_rot.reshape(nb, Q_LORA, MLA_HEADS * HEAD_PAD)
    wq_t = jnp.transpose(wq_pad, (0, 2, 1))
    wqr_t = jnp.transpose(_rot_half_cols(qr).reshape(nb, Q_LORA, MLA_HEADS * MLA_ROPE), (0, 2, 1))

    m_abs = jnp.zeros((MLA_HEADS, HEAD_PAD, KV_LORA + 128), F32)
    m_abs = m_abs.at[:, :MLA_NOPE, :KV_LORA].set(jnp.transpose(w_uk, (1, 2, 0)))
    m_abs = m_abs.at[:, MLA_NOPE:MLA_NOPE + MLA_ROPE, KV_LORA:KV_LORA + MLA_ROPE].set(
        jnp.broadcast_to(jnp.eye(MLA_ROPE, dtype=F32), (MLA_HEADS, MLA_ROPE, MLA_ROPE)))

    wuv_h = jnp.transpose(w_uv, (1, 0, 2))
    even = jnp.pad(wuv_h, ((0, 0), (0, 0), (0, 64)))
    odd = jnp.pad(wuv_h, ((0, 0), (0, 0), (64, 0)))
    wuv_pad = jnp.where((jnp.arange(MLA_HEADS) % 2 == 0)[:, None, None], even, odd)

    rw_t = jnp.transpose(router_w, (0, 2, 1))
    return dict(w_kv=w_kv, wk_pad=wk_pad, ek=ek, wvt_ext=wvt_ext, ones_col=ones_col, wq_pad=wq_pad, wq_rot=wq_rot, wq_t=wq_t, wqr_t=wqr_t,
                m_abs=m_abs, wuv_pad=wuv_pad, rw_t=rw_t)


def _mixer(st, l, P, W, packed):
    rows_kw = dict(rows_total=packed["total"], row0=packed["row0"], rows_buf=packed["buf"])
    x, m = st["x"], st["mod"][l]
    B, T, _ = x.shape
    n_a = P["hg_w_in"].shape[0]
    h = st.pop("h_next", None)
    if h is None:
        h = norm_mod(x, P["norm1_g"][l], m, sc_idx=1, sh_idx=0)
    norm2 = (P["norm2_g"][l], 4, 3)
    if l < n_a:
        rows = min(1024, B * T)
        zf = linear(h, P["hg_w_in"], l, F32, col_blocks=(1,), rows=rows)
        zqig = linear(h, P["hg_w_in"], l, BF16, col_blocks=(0, 2, 3), rows=rows)
        s0 = None if st["hg_state"] is None else st["hg_state"][l]
        o, s_new = gla(zqig, zf, st["lbs"][l], P["hg_onorm_g"][l], s0)
        st["hg_new"].append(s_new)
        st["x"], packed["buf"] = linear(o, P["hg_w_out"], l, F32, x=x, mod=m, gate_idx=2, next_norm=norm2,
                                        **rows_kw)
    else:
        bi = l - n_a
        if st["past_lat"] is None:
            qt = mla_queries_t(h, P["w_dq"], P["q_norm_g"], W["wq_t"], W["wqr_t"], bi, st["cos_t"], st["sin_t"])
            o = attn_prompt(qt, st["k_all"], st["v_all"])
        else:
            q = mla_queries(h, P["w_dq"], P["q_norm_g"], W["wq_pad"], W["wq_rot"], bi, st["c128"], st["s128"])
            q_abs = absorb_queries(q.reshape(B * T, -1), W["m_abs"])
            o_lat = attn_sample(q_abs, st["past_lat"], st["past_kr"], st["lat"], st["kr"])
            o = unabsorb(o_lat, W["wuv_pad"]).reshape(B, T, -1)
        st["x"], packed["buf"] = linear(o, P["w_o"], bi, F32, x=x, mod=m, gate_idx=2, next_norm=norm2, **rows_kw)
    packed["row0"] += B * T


def _moe(groups, hp, l, P, W):
    n_tok = hp.shape[0]
    n_tiles = (TOP_K * n_tok) // MOE_TILE + N_EXPERTS
    pos, w8, tile_start, tile_count = route(hp, W["rw_t"], P["router_bias"], l, MOE_TILE)
    pos_flat = pos.reshape(-1)
    src = sc_invert(pos_flat, n_tok, n_tiles * MOE_TILE)
    xs = sc_gather(hp, src)
    out = moe_gemm(xs, tile_start[:, 0], tile_count[:, 0], P["exp_w_in"], P["exp_w_out"], l,
                   MOE_TILE, n_tiles)
    y8 = sc_gather(out, pos_flat).reshape(TOP_K, n_tok, -1)
    last = l == P["norm1_g"].shape[0] - 1
    with_kv = l == P["hg_w_in"].shape[0] - 1
    row0 = 0
    for st in groups:
        B, T, _ = st["x"].shape
        outs = moe_combine(
            y8, w8, hp, P["sh_w_in"], P["sh_w_out"], st["x"], st["mod"][l], 5, l, row0,
            final_g=P["final_g"] if last else None,
            next_norm=None if last else (P["norm1_g"][l + 1], st["mod"][l + 1], 1, 0),
            shared_kv=(P["kv_in_g"], W["w_kv"], P["kv_lat_g"], st["cos32"], st["sin32"]) if with_kv else None)
        outs = list(outs) if isinstance(outs, (list, tuple)) else [outs]
        st["x"] = outs.pop(0)
        if not last:
            st["h_next"] = outs.pop(0)
        if with_kv:
            st["lat"], st["kr"] = outs
        row0 += B * T


def _group_state(x, mod, pos, hg_state, past_lat, past_kr, lbs):
    return dict(x=x, mod=mod, hg_state=hg_state, past_lat=past_lat, past_kr=past_kr, lbs=lbs,
                **_rope_tables(pos), hg_new=[],
                lat=None, kr=None, k_all=None, v_all=None)


def kernel(x_prompt, x_sample, state_hgrn, cache_mla_latent, cache_mla_krope, c_prompt, c_sample, ada_w, ada_b, norm1_g, norm2_g, hg_w_in, hg_lb_logits, hg_onorm_g, hg_w_out, kv_in_g, w_dkv, kv_lat_g, w_uk, w_uv, w_dq, q_norm_g, w_uq, w_o, router_w, router_bias, exp_w_in, exp_w_out, sh_w_in, sh_w_out, final_g):
    Bp, Sp, _ = x_prompt.shape
    Bs, Ss, _ = x_sample.shape
    past = cache_mla_latent.shape[1]
    P = dict(norm1_g=norm1_g, norm2_g=norm2_g, hg_w_in=hg_w_in, hg_lb_logits=hg_lb_logits,
             hg_onorm_g=hg_onorm_g, hg_w_out=hg_w_out, kv_in_g=kv_in_g, kv_lat_g=kv_lat_g,
             w_dq=w_dq, q_norm_g=q_norm_g, w_o=w_o, router_bias=router_bias,
             exp_w_in=exp_w_in, exp_w_out=exp_w_out, sh_w_in=sh_w_in, sh_w_out=sh_w_out, final_g=final_g)
    W = _prep_weights(w_dkv, w_uk, w_uv, w_uq, router_w)
    mod = ada_mod(jnp.concatenate([c_prompt, c_sample], axis=0), ada_w, ada_b)
    lbs = jnp.cumsum(jax.nn.softmax(hg_lb_logits.astype(F32), axis=0), axis=0)
    gp = _group_state(x_prompt, mod[:, :Bp, None, :], np.arange(Sp), None, None, None, lbs)
    gs = _group_state(x_sample, mod[:, Bp:, None, :], past + np.arange(Ss), state_hgrn,
                      cache_mla_latent, jnp.transpose(cache_mla_krope, (0, 2, 1)), lbs)
    groups = [gp, gs]
    n_tok = sum(st["x"].shape[0] * st["x"].shape[1] for st in groups)
    n_a = hg_w_in.shape[0]
    for l in range(norm1_g.shape[0]):
        packed = dict(total=n_tok, row0=0, buf=None)
        for st in groups:
            _mixer(st, l, P, W, packed)
        _moe(groups, packed["buf"], l, P, W)
        if l == n_a - 1:
            gp["k_all"], gp["v_all"] = kv_expand(gp["lat"], gp["kr"], W["wk_pad"], W["ek"], W["wvt_ext"],
                                                 W["ones_col"])
    return (gp["x"], gs["x"], jnp.stack(gp["hg_new"], axis=0), jnp.stack(gs["hg_new"], axis=0),
            gp["lat"], gp["kr"], gs["lat"], gs["kr"])
```

```python
import dataclasses
import functools

import numpy as np
import jax
import jax.numpy as jnp
from jax import lax
from jax.experimental import pallas as pl
from jax.experimental.pallas import tpu as pltpu
from jax.experimental.pallas import tpu_sc as plsc

F32 = jnp.float32
BF16 = jnp.bfloat16

D_MODEL = 1024
CHUNK = 64
HG_HEADS = 8
HG_DK = 128
HG_DV = 128
MLA_HEADS = 16
MLA_NOPE = 64
MLA_ROPE = 32
MLA_V = 64
Q_LORA = 384
KV_LORA = 256
ROPE_THETA = 10000.0
N_EXPERTS = 64
TOP_K = 8
N_GROUPS = 8
TOPK_GROUPS = 4
EXPERT_FF = 256
SHARED_FF = 256
ROUTED_SCALE = 2.5
EPS = 1e-6

HEAD_PAD = 128
SAMPLE_KEY_SUB = 8
ATTN_LOOKAHEAD = 6
V_ROWS = MLA_V + 16
QK_SCALE = (MLA_NOPE + MLA_ROPE) ** -0.5
Q_PRESCALE = QK_SCALE * float(np.log2(np.e))
VMEM_LIMIT = 56 * 1024 * 1024
NEG_INF = float("-inf")
SC_CORES = 2
SC_SUBCORES = 16
SC_WORKERS = SC_CORES * SC_SUBCORES
SC_LANES = 16
SC_WINDOW = 64
MOE_TILE = 512
MOE_NBUF = 4
MOE_SUB = 1


def _cparams(n_axes):
    return pltpu.CompilerParams(dimension_semantics=("arbitrary",) * n_axes,
                                vmem_limit_bytes=VMEM_LIMIT)


def _silu(x):
    return x * jax.nn.sigmoid(x)


def _rms(x, g):
    ms = jnp.mean(x * x, axis=-1, keepdims=True)
    return x * lax.rsqrt(ms + EPS) * g


def _dot(a, b):
    return jnp.dot(a, b, preferred_element_type=F32)


def _dot_nt(a, b):
    return lax.dot_general(a, b, (((1,), (1,)), ((), ())), preferred_element_type=F32)


def _dot_tn(a, b):
    return lax.dot_general(a, b, (((0,), (0,)), ((), ())), preferred_element_type=F32)


def _row_blocks(B, T, rows):
    if T >= rows:
        assert T % rows == 0
        bb, tt = 1, rows
    else:
        assert rows % T == 0 and B % (rows // T) == 0
        bb, tt = rows // T, T
    nt = T // tt
    return bb, tt, (B // bb) * nt, (lambda i: (i // nt, i % nt))


def _ada_kernel(c_ref, w_ref, b_ref, o_ref):
    a = _silu(c_ref[...]).astype(BF16)
    o_ref[...] = _dot(a, w_ref[...].astype(BF16)) + b_ref[...]


def ada_mod(c, ada_w, ada_b):
    R, D = c.shape
    L, _, N = ada_w.shape
    tn = 1536
    return pl.pallas_call(
        _ada_kernel,
        grid=(L, N // tn),
        in_specs=[pl.BlockSpec((R, D), lambda l, j: (0, 0)),
                  pl.BlockSpec((None, D, tn), lambda l, j: (l, 0, j)),
                  pl.BlockSpec((None, 1, tn), lambda l, j: (l, 0, j))],
        out_specs=pl.BlockSpec((None, R, tn), lambda l, j: (l, 0, j)),
        out_shape=jax.ShapeDtypeStruct((L, R, N), F32),
        compiler_params=_cparams(2),
        name="ada_mod",
    )(c, ada_w, ada_b.reshape(L, 1, N))


def _pack_pairs(y):
    half = y.shape[-1] // 2
    bits = lax.bitcast_convert_type(y.astype(BF16).astype(F32), jnp.uint32)
    word = lax.shift_right_logical(bits[:, :half], jnp.uint32(16)) | bits[:, half:]
    return lax.bitcast_convert_type(word, jnp.int32)


def _unpack_pairs(word, dtype=BF16):
    u = lax.bitcast_convert_type(word, jnp.uint32)
    lo = lax.bitcast_convert_type(lax.shift_left(u, jnp.uint32(16)), F32)
    hi = lax.bitcast_convert_type(u & jnp.uint32(0xFFFF0000), F32)
    return lo.astype(dtype), hi.astype(dtype)


def _norm_kernel(x_ref, g_ref, sc_ref, sh_ref, o_ref):
    y = _rms(x_ref[...], g_ref[...]) * (1.0 + sc_ref[...]) + sh_ref[...]
    o_ref[...] = y.astype(o_ref.dtype)


def norm_mod(x, g, mod, sc_idx, sh_idx, rows=512):
    B, T, D = x.shape
    bb, tt, nblk, ij = _row_blocks(B, T, rows)
    xspec = pl.BlockSpec((bb, tt, D), lambda i: ij(i) + (0,))
    return pl.pallas_call(
        _norm_kernel,
        grid=(nblk,),
        in_specs=[xspec, pl.BlockSpec((1, D), lambda i: (0, 0)),
                  pl.BlockSpec((bb, 1, D), lambda i: (ij(i)[0], 0, sc_idx)),
                  pl.BlockSpec((bb, 1, D), lambda i: (ij(i)[0], 0, sh_idx))],
        out_specs=xspec,
        out_shape=jax.ShapeDtypeStruct((B, T, D), BF16),
        compiler_params=_cparams(1),
        name="norm_mod",
    )(x, g.reshape(1, D), mod, mod)


def _linear_kernel(*refs, residual, norm_next, shared_rows, n_main):
    if norm_next and shared_rows:
        a_ref, w_ref, x_ref, gate_ref, ng_ref, nsc_ref, nsh_ref, _, o_ref, hp_ref, wb_ref = refs
    elif norm_next:
        a_ref, w_ref, x_ref, gate_ref, ng_ref, nsc_ref, nsh_ref, o_ref, hp_ref, wb_ref = refs
    elif residual:
        a_ref, w_ref, x_ref, gate_ref, o_ref, wb_ref = refs
    else:
        a_ref, w_ref, o_ref, wb_ref = refs

    @pl.when(pl.program_id(1) == 0)
    def _():
        wb_ref[...] = w_ref[...].astype(BF16)

    def main():
        bb, tt, K = a_ref.shape
        y = _dot(a_ref[...].reshape(bb * tt, K).astype(BF16), wb_ref[...])
        y = y.reshape(bb, tt, y.shape[-1])
        if residual:
            y = x_ref[...] + gate_ref[...] * y
        o_ref[...] = y.astype(o_ref.dtype)
        if norm_next:
            h = _rms(y, ng_ref[...]) * (1.0 + nsc_ref[...]) + nsh_ref[...]
            hp_ref[...] = _pack_pairs(h.reshape(bb * tt, h.shape[-1]))

    if n_main is None:
        main()
    else:
        pl.when(pl.program_id(1) < n_main)(main)

        @pl.when(pl.program_id(1) >= n_main)
        def _():
            hp_ref[...] = jnp.zeros(hp_ref.shape, hp_ref.dtype)


def linear(a, w, l, out_dtype, x=None, mod=None, gate_idx=0, rows=512, tn=1024, next_norm=None,
           rows_total=None, row0=0, rows_buf=None):
    B, T, K = a.shape
    _, _, N = w.shape
    tn = min(tn, N)
    bb, tt, nblk, ij0 = _row_blocks(B, T, rows)
    n_extra = 0
    if next_norm is not None and rows_buf is None and rows_total is not None:
        assert row0 == 0 and (rows_total - B * T) % (bb * tt) == 0
        n_extra = (rows_total - B * T) // (bb * tt)

    def ij(i):
        return ij0(jnp.minimum(i, nblk - 1)) if n_extra else ij0(i)

    in_specs = [pl.BlockSpec((bb, tt, K), lambda j, i: ij(i) + (0,)),
                pl.BlockSpec((None, K, tn), lambda j, i: (l, 0, j))]
    args = [a, w]
    ospec = pl.BlockSpec((bb, tt, tn), lambda j, i: ij(i) + (j,))
    out_specs = ospec
    out_shape = jax.ShapeDtypeStruct((B, T, N), out_dtype)
    aliases = {}
    if x is not None:
        gsteps = D_MODEL // tn
        in_specs += [ospec, pl.BlockSpec((bb, 1, tn), lambda j, i: (ij(i)[0], 0, gate_idx * gsteps + j))]
        args += [x, mod]
    if next_norm is not None:
        assert x is not None and tn == N
        gain, sc_idx, sh_idx = next_norm
        in_specs += [pl.BlockSpec((1, N), lambda j, i: (0, 0)),
                     pl.BlockSpec((bb, 1, N), lambda j, i: (ij(i)[0], 0, sc_idx)),
                     pl.BlockSpec((bb, 1, N), lambda j, i: (ij(i)[0], 0, sh_idx))]
        args += [gain.reshape(1, N), mod, mod]
        assert row0 % (bb * tt) == 0
        off = row0 // (bb * tt)
        out_specs = [ospec, pl.BlockSpec((bb * tt, N // 2), lambda j, i: (off + i, 0))]
        out_shape = [out_shape, jax.ShapeDtypeStruct((rows_total or B * T, N // 2), jnp.int32)]
        if rows_buf is not None:
            in_specs.append(pl.BlockSpec(memory_space=pl.ANY))
            args.append(rows_buf)
            aliases = {len(args) - 1: 1}
    return pl.pallas_call(
        functools.partial(_linear_kernel, residual=x is not None, norm_next=next_norm is not None,
                          shared_rows=rows_buf is not None, n_main=nblk if n_extra else None),
        grid=(N // tn, nblk + n_extra),
        in_specs=in_specs,
        out_specs=out_specs,
        out_shape=out_shape,
        scratch_shapes=[pltpu.VMEM((K, tn), BF16)],
        input_output_aliases=aliases,
        compiler_params=_cparams(2),
        name="linear",
    )(*args)


def _hgrn_proj_kernel(x_ref, g_ref, sc_ref, sh_ref, w_ref, zf_ref, zqig_ref, h_b, w_b):
    i = pl.program_id(0)
    j = pl.program_id(1)
    bb, tt, D = x_ref.shape

    @pl.when(i == 0)
    def _():
        w_b[j] = w_ref[...].astype(BF16)

    @pl.when(j == 0)
    def _():
        h = _rms(x_ref[...], g_ref[...]) * (1.0 + sc_ref[...]) + sh_ref[...]
        h_b[...] = h.reshape(bb * tt, D).astype(BF16)

    y = _dot(h_b[...], w_b[j]).reshape(bb, tt, -1)

    @pl.when(j == 1)
    def _():
        zf_ref[...] = y

    @pl.when(j != 1)
    def _():
        zqig_ref[...] = y.astype(zqig_ref.dtype)


def hgrn_proj(x, g, mod, sc_idx, sh_idx, w_in, l, rows=1024):
    B, T, D = x.shape
    bb, tt, nblk, ij = _row_blocks(B, T, min(rows, B * T))
    xspec = pl.BlockSpec((bb, tt, D), lambda i, j: ij(i) + (0,))
    return pl.pallas_call(
        _hgrn_proj_kernel,
        grid=(nblk, 4),
        in_specs=[xspec,
                  pl.BlockSpec((1, D), lambda i, j: (0, 0)),
                  pl.BlockSpec((bb, 1, D), lambda i, j: (ij(i)[0], 0, sc_idx)),
                  pl.BlockSpec((bb, 1, D), lambda i, j: (ij(i)[0], 0, sh_idx)),
                  pl.BlockSpec((None, D, D), lambda i, j: (l, 0, jnp.where(i == 0, j, 3)))],
        out_specs=[xspec,
                   pl.BlockSpec((bb, tt, D), lambda i, j: ij(i) + (j - (j >= 1),))],
        out_shape=[jax.ShapeDtypeStruct((B, T, D), F32), jax.ShapeDtypeStruct((B, T, 3 * D), BF16)],
        scratch_shapes=[pltpu.VMEM((bb * tt, D), BF16), pltpu.VMEM((4, D, D), BF16)],
        compiler_params=_cparams(2),
        name="hgrn_proj",
    )(x, g.reshape(1, D), mod, mod, w_in)


def _gla_kernel(*refs, L, n_chunks, has_init):
    if has_init:
        q_ref, f_ref, i_ref, g_ref, lb_ref, on_ref, s0_ref, o_ref, so_ref, st_ref = refs
    else:
        q_ref, f_ref, i_ref, g_ref, lb_ref, on_ref, o_ref, so_ref, st_ref = refs
    t = pl.program_id(1)
    H = st_ref.shape[0]

    @pl.when(t == 0)
    def _():
        for h in range(H):
            if has_init:
                st_ref[h] = s0_ref[0, h].T
            else:
                st_ref[h] = jnp.zeros(st_ref.shape[1:], F32)

    lb = lb_ref[...]
    onorm = on_ref[...]
    row = lax.broadcasted_iota(jnp.int32, (L, L), 0)
    col = lax.broadcasted_iota(jnp.int32, (L, L), 1)
    causal = col <= row
    tri = causal.astype(BF16)

    def chunk(c, carry):
        rows = pl.ds(pl.multiple_of(c * L, L), L)

        def write_o(sl, o):
            o_ref[0, rows, sl] = o.astype(o_ref.dtype)

        _gla_chunk(q_ref[0, rows, :], f_ref[0, rows, :], i_ref[0, rows, :], g_ref[0, rows, :],
                   lb, onorm, tri, causal, st_ref, write_o)
        return carry

    lax.fori_loop(0, n_chunks, chunk, 0, unroll=4 if n_chunks % 4 == 0 else 1)

    @pl.when(t == pl.num_programs(1) - 1)
    def _():
        for h in range(H):
            so_ref[0, h] = st_ref[h].T


def _gla_chunk(q, f, v, g, lb, onorm, tri, causal, st_ref, write_o):
    L = q.shape[0]
    H = st_ref.shape[0]
    mid = L // 2 - 1
    q = _silu(q.astype(F32))
    fg = lb + (1.0 - lb) * jax.nn.sigmoid(f)
    k = 1.0 - fg
    v = v.astype(BF16)
    gate = _silu(g.astype(F32))
    logf = jnp.log(fg)
    hi = logf.astype(BF16)
    lo = (logf - hi.astype(F32)).astype(BF16)
    b = _dot(tri, hi) + _dot(tri, lo)
    b_mid = b[mid:mid + 1, :]
    b_last = b[L - 1:L, :]
    qa = q * jnp.exp(b - b_mid)
    kb = k * jnp.exp(b_mid - b)
    qe = (qa * jnp.exp(b_mid)).astype(BF16)
    kd = (kb * jnp.exp(b_last - b_mid)).astype(BF16)
    qa = qa.astype(BF16)
    kb = kb.astype(BF16)
    decay = jnp.exp(b_last)
    sls = [slice(h * HG_DK, (h + 1) * HG_DK) for h in range(H)]
    sts = [st_ref[h] for h in range(H)]
    scores = [_dot_nt(qa[:, sl], kb[:, sl]) for sl in sls]
    inter = [_dot_nt(qe[:, sl], st.astype(BF16)) for sl, st in zip(sls, sts)]
    outer = [_dot_tn(v[:, sl], kd[:, sl]) for sl in sls]
    intra = [_dot(jnp.where(causal, sc, 0.0).astype(BF16), v[:, sl]) for sc, sl in zip(scores, sls)]
    for h, sl in enumerate(sls):
        st_ref[h] = sts[h] * decay[:, sl] + outer[h]
        write_o(sl, _rms(inter[h] + intra[h], onorm[:, sl]) * gate[:, sl])


def gla(zqig, zf, lb, onorm_g, s0):
    B, T, D = zf.shape
    L = CHUNK if T % CHUNK == 0 else T
    tt = min(T, 512)
    n_chunks = tt // L
    H = HG_HEADS

    def zspec(part):
        return pl.BlockSpec((1, tt, D), lambda b, t: (b, t, part))

    hspec = pl.BlockSpec((1, D), lambda b, t: (0, 0))
    sspec = pl.BlockSpec((1, H, HG_DK, HG_DV), lambda b, t: (b, 0, 0, 0))
    in_specs = [zspec(0), zspec(0), zspec(1), zspec(2), hspec, hspec]
    args = [zqig, zf, zqig, zqig, lb.reshape(1, D), onorm_g.reshape(1, D)]
    if s0 is not None:
        in_specs.append(sspec)
        args.append(s0)
    return pl.pallas_call(
        functools.partial(_gla_kernel, L=L, n_chunks=n_chunks, has_init=s0 is not None),
        grid=(B, T // tt),
        in_specs=in_specs,
        out_specs=[pl.BlockSpec((1, tt, D), lambda b, t: (b, t, 0)), sspec],
        out_shape=[jax.ShapeDtypeStruct((B, T, D), BF16),
                   jax.ShapeDtypeStruct((B, H, HG_DK, HG_DV), F32)],
        scratch_shapes=[pltpu.VMEM((H, HG_DV, HG_DK), F32)],
        compiler_params=_cparams(2),
        name="gla",
    )(*args)


def _route_kernel(h_ref, rw_ref, bias_ref, pos_ref, w_ref, te_ref, nu_ref,
                  e_s, r_s, base_s, start_s, *, tile_rows):
    ph = pl.program_id(0)
    i = pl.program_id(1)
    M = h_ref.shape[0]
    half = h_ref.shape[1]
    G, E = N_GROUPS, N_EXPERTS // N_GROUPS
    e_flat = lax.broadcasted_iota(jnp.int32, (N_EXPERTS, M), 0)

    @pl.when(ph == 1)
    def _():
        @pl.when(i == 0)
        def _():
            cnt = base_s[...]
            padded = jnp.floor((cnt + (tile_rows - 1)) * (1.0 / tile_rows)) * tile_rows
            r = lax.broadcasted_iota(jnp.int32, (N_EXPERTS, N_EXPERTS), 0)
            c = lax.broadcasted_iota(jnp.int32, (N_EXPERTS, N_EXPERTS), 1)
            start = jnp.dot((c < r).astype(F32), padded, preferred_element_type=F32,
                            precision=lax.Precision.HIGHEST)
            start_s[...] = start
            te_ref[...] = (start * (1.0 / tile_rows)).astype(jnp.int32)
            nu_ref[...] = (padded * (1.0 / tile_rows)).astype(jnp.int32)

        start_col = start_s[:, :1]
        for k in range(TOP_K):
            hit = e_flat == e_s[i, k:k + 1, :]
            seg = jnp.sum(jnp.where(hit, start_col, 0.0), axis=0, keepdims=True)
            pos_ref[k:k + 1, :] = (seg + r_s[i, k:k + 1, :]).astype(jnp.int32)

    @pl.when(ph == 0)
    def _():
        _route_pass0(h_ref, rw_ref, bias_ref, w_ref, e_s, r_s, base_s, i, M, half, G, E)


def _route_pass0(h_ref, rw_ref, bias_ref, w_ref, e_s, r_s, base_s, i, M, half, G, E):
    @pl.when(i == 0)
    def _():
        base_s[...] = jnp.zeros_like(base_s)

    lo, hi = _unpack_pairs(h_ref[...])
    rw = rw_ref[...].astype(BF16)
    logits = _dot_nt(rw[:, :half], lo) + _dot_nt(rw[:, half:], hi)
    s = jax.nn.sigmoid(logits)
    sb = (s + bias_ref[...]).reshape(G, E, M)
    s = s.reshape(G, E, M)
    e_in = lax.broadcasted_iota(jnp.int32, (G, E, M), 1).astype(F32)
    g_id = lax.broadcasted_iota(jnp.int32, (G, 1, M), 0)
    e_id = lax.broadcasted_iota(jnp.int32, (G, E, M), 0).astype(F32) * E + e_in

    def all_max(a):
        return jnp.max(jnp.max(a, axis=0, keepdims=True), axis=1, keepdims=True)

    def all_min(a):
        return jnp.min(jnp.min(a, axis=0, keepdims=True), axis=1, keepdims=True)

    def all_sum(a):
        return jnp.sum(jnp.sum(a, axis=0, keepdims=True), axis=1, keepdims=True)

    m1 = jnp.max(sb, axis=1, keepdims=True)
    first = jnp.min(jnp.where(sb == m1, e_in, float(E)), axis=1, keepdims=True)
    m2 = jnp.max(jnp.where(e_in == first, NEG_INF, sb), axis=1, keepdims=True)
    gs = m1 + m2

    rank = jnp.zeros((G, 1, M), jnp.int32)
    for j in range(G):
        gj = gs[j:j + 1]
        beats = (gj > gs) | ((gj == gs) & (j < g_id))
        rank = rank + beats.astype(jnp.int32)
    gsel = rank < TOPK_GROUPS

    vals = jnp.where(gsel, sb, NEG_INF)
    selm = jnp.zeros((G, E, M), F32)
    chosen, score = [], []
    for _ in range(TOP_K):
        m = all_max(vals)
        first = all_min(jnp.where(vals == m, e_id, float(N_EXPERTS)))
        hit = e_id == first
        score.append(all_sum(jnp.where(hit, s, 0.0)))
        selm = jnp.where(hit, 1.0, selm)
        vals = jnp.where(hit, NEG_INF, vals)
        chosen.append(first)

    tot = score[0]
    for sc in score[1:]:
        tot = tot + sc
    norm = ROUTED_SCALE / tot

    selm = selm.reshape(N_EXPERTS, M)
    earlier = (lax.broadcasted_iota(jnp.int32, (M, M), 0)
               < lax.broadcasted_iota(jnp.int32, (M, M), 1)).astype(BF16)
    rank = (base_s[:, :1] + _dot(selm.astype(BF16), earlier)).reshape(G, E, M)
    base_s[...] = base_s[...] + jnp.sum(selm, axis=1, keepdims=True)
    for k in range(TOP_K):
        hit = e_id == chosen[k]
        e_s[i, k:k + 1, :] = chosen[k].reshape(1, M).astype(jnp.int32)
        r_s[i, k:k + 1, :] = all_sum(jnp.where(hit, rank, 0.0)).reshape(1, M)
        w_ref[k:k + 1, :] = (score[k] * norm).reshape(1, M)


def route(hp, router_w_t, router_bias, l, tile_rows, rows=512):
    N, half = hp.shape
    M = rows
    nT = N // M
    assert N % M == 0

    def p0(ph, i):
        return i * (1 - ph) + (nT - 1) * ph

    return pl.pallas_call(
        functools.partial(_route_kernel, tile_rows=tile_rows),
        grid=(2, nT),
        in_specs=[pl.BlockSpec((M, half), lambda ph, i: (p0(ph, i), 0)),
                  pl.BlockSpec((None, N_EXPERTS, 2 * half), lambda ph, i: (l, 0, 0)),
                  pl.BlockSpec((None, N_EXPERTS, 1), lambda ph, i: (l, 0, 0))],
        out_specs=[pl.BlockSpec((TOP_K, M), lambda ph, i: (0, i * ph)),
                   pl.BlockSpec((TOP_K, M), lambda ph, i: (0, p0(ph, i))),
                   pl.BlockSpec((N_EXPERTS, 128), lambda ph, i: (0, 0)),
                   pl.BlockSpec((N_EXPERTS, 128), lambda ph, i: (0, 0))],
        out_shape=[jax.ShapeDtypeStruct((TOP_K, N), jnp.int32),
                   jax.ShapeDtypeStruct((TOP_K, N), F32),
                   jax.ShapeDtypeStruct((N_EXPERTS, 128), jnp.int32),
                   jax.ShapeDtypeStruct((N_EXPERTS, 128), jnp.int32)],
        scratch_shapes=[pltpu.VMEM((nT, TOP_K, M), jnp.int32), pltpu.VMEM((nT, TOP_K, M), F32),
                        pltpu.VMEM((N_EXPERTS, 128), F32), pltpu.VMEM((N_EXPERTS, 128), F32)],
        compiler_params=_cparams(2),
        name="route",
    )(hp, router_w_t, router_bias.reshape(-1, N_EXPERTS, 1))


def _sc_mesh():
    return plsc.VectorSubcoreMesh(core_axis_name="core", subcore_axis_name="subcore")


def sc_invert(pos_flat, n_tok, n_out):
    n = pos_flat.shape[0]
    per = n_out // SC_WORKERS
    chunk = n_tok
    assert n_out % SC_WORKERS == 0 and per % SC_LANES == 0
    assert n_tok % chunk == 0 and n % chunk == 0 and chunk % SC_LANES == 0
    cp = pltpu.CompilerParams()
    if "needs_layout_passes" in pltpu.CompilerParams.__dataclass_fields__:
        cp = dataclasses.replace(cp, needs_layout_passes=False)

    @functools.partial(
        pl.kernel, out_type=jax.ShapeDtypeStruct((n_out,), jnp.int32), mesh=_sc_mesh(),
        scratch_types=[pltpu.VMEM((chunk,), jnp.int32), pltpu.VMEM((per,), jnp.int32)],
        compiler_params=cp, name="sc_invert")
    def k(pos_hbm, src_hbm, pos_v, src_v):
        wid = lax.axis_index("subcore") * SC_CORES + lax.axis_index("core")
        lo = wid * per
        lane = lax.iota(jnp.int32, SC_LANES)

        @pl.loop(0, per, step=SC_LANES)
        def _(r):
            src_v[pl.ds(r, SC_LANES)] = lax.rem(lo + r + lane, n_tok)

        @pl.loop(0, n // chunk)
        def _(c):
            base = c * chunk
            pltpu.sync_copy(pos_hbm.at[pl.ds(base, chunk)], pos_v)
            tok0 = lax.rem(base, n_tok)

            @plsc.parallel_loop(0, chunk, step=SC_LANES, unroll=8)
            def _(r):
                p = pos_v[pl.ds(r, SC_LANES)] - lo
                mine = (p >= 0) & (p < per)
                plsc.store_scatter(src_v, [jnp.where(mine, p, 0)], tok0 + r + lane, mask=mine)

        pltpu.sync_copy(src_v, src_hbm.at[pl.ds(lo, per)])

    return k(pos_flat)


def sc_gather(x, idx):
    n = idx.shape[0]
    dim = x.shape[1]
    assert n % (SC_WINDOW * SC_WORKERS) == 0

    @functools.partial(
        pl.kernel, out_type=jax.ShapeDtypeStruct((n, dim), x.dtype), mesh=_sc_mesh(),
        scratch_types=[], name="sc_gather")
    def k(x_hbm, i_hbm, o_hbm):
        def body(i_vmem, o_vmem):
            pltpu.sync_copy(x_hbm.at[i_vmem.at[0]], o_vmem)

        pltpu.emit_pipeline(
            body, grid=(n // SC_WINDOW,),
            in_specs=[pl.BlockSpec((1, SC_WINDOW), index_map=lambda i: (i, 0))],
            out_specs=[pl.BlockSpec((SC_WINDOW, dim), index_map=lambda i: (i, 0))],
            core_axis_name=("core", "subcore"),
            dimension_semantics=(pltpu.PARALLEL,),
        )(i_hbm, o_hbm)

    return k(x, idx.reshape(n // SC_WINDOW, SC_WINDOW))


def _moe_gemm_kernel(ts_ref, tn_ref, x_hbm, wi_ref, wo_ref, o_hbm, wi_b, wo_b, xbuf, obuf, in_sem, out_sem,
                     *, tile_rows, n_tiles):
    e = pl.program_id(0)
    last = pl.num_programs(0) - 1
    t0 = ts_ref[e]
    n = tn_ref[e]
    n_used = ts_ref[last] + tn_ref[last]

    def x_copy(g, slot):
        rows = pl.ds(pl.multiple_of(g * tile_rows, tile_rows), tile_rows)
        return pltpu.make_async_copy(x_hbm.at[rows], xbuf.at[slot], in_sem.at[slot])

    def o_copy(g, slot):
        rows = pl.ds(pl.multiple_of(g * tile_rows, tile_rows), tile_rows)
        return pltpu.make_async_copy(obuf.at[slot], o_hbm.at[rows], out_sem.at[slot])

    @pl.when(e == 0)
    def _():
        for g0 in range(MOE_NBUF - 1):
            @pl.when(g0 < n_used)
            def _():
                x_copy(g0, g0).start()

    @pl.when(n > 0)
    def _():
        wi_b[...] = wi_ref[...].astype(BF16)
        wo_b[...] = wo_ref[...].astype(BF16)

    def tile(i, carry):
        g = t0 + i
        slot = lax.rem(g, MOE_NBUF)
        x_copy(g, slot).wait()
        ahead = g + (MOE_NBUF - 1)

        @pl.when(ahead < n_used)
        def _():
            x_copy(ahead, lax.rem(ahead, MOE_NBUF)).start()

        @pl.when(g >= MOE_NBUF)
        def _():
            o_copy(g - MOE_NBUF, slot).wait()

        rows = tile_rows // MOE_SUB
        half = xbuf.shape[2]
        xs = [_unpack_pairs(xbuf[slot, r * rows:(r + 1) * rows, :]) for r in range(MOE_SUB)]
        hus = [_dot(lo, wi_b[:half, :]) + _dot(hi, wi_b[half:, :]) for lo, hi in xs]
        acts = [(_silu(hu[:, :EXPERT_FF]) * hu[:, EXPERT_FF:]).astype(BF16) for hu in hus]
        outs = [_dot(act, wo_b[...]) for act in acts]
        for r, out in enumerate(outs):
            obuf[slot, r * rows:(r + 1) * rows, :] = _pack_pairs(out)
        o_copy(g, slot).start()
        return carry

    lax.fori_loop(0, n, tile, 0)

    @pl.when(e == last)
    def _():
        for back in range(MOE_NBUF, 0, -1):
            @pl.when(n_used >= back)
            def _():
                o_copy(n_used - back, lax.rem(n_used - back, MOE_NBUF)).wait()

        obuf[0] = jnp.zeros(obuf.shape[1:], obuf.dtype)

        def clear(g, carry):
            cp = o_copy(g, 0)
            cp.start()
            cp.wait()
            return carry

        lax.fori_loop(n_used, n_tiles, clear, 0)


def moe_gemm(xs, tile_start, tile_count, exp_w_in, exp_w_out, l, tile_rows, n_tiles):
    P, half = xs.shape
    D = 2 * half
    assert P == n_tiles * tile_rows
    hbm = pl.BlockSpec(memory_space=pl.ANY)
    grid_spec = pltpu.PrefetchScalarGridSpec(
        num_scalar_prefetch=2,
        grid=(N_EXPERTS,),
        in_specs=[hbm,
                  pl.BlockSpec((None, None, D, 2 * EXPERT_FF), lambda e, ts, tn: (l, e, 0, 0)),
                  pl.BlockSpec((None, None, EXPERT_FF, D), lambda e, ts, tn: (l, e, 0, 0))],
        out_specs=hbm,
        scratch_shapes=[pltpu.VMEM((D, 2 * EXPERT_FF), BF16), pltpu.VMEM((EXPERT_FF, D), BF16),
                        pltpu.VMEM((MOE_NBUF, tile_rows, half), jnp.int32),
                        pltpu.VMEM((MOE_NBUF, tile_rows, half), jnp.int32),
                        pltpu.SemaphoreType.DMA((MOE_NBUF,)), pltpu.SemaphoreType.DMA((MOE_NBUF,))],
    )
    return pl.pallas_call(
        functools.partial(_moe_gemm_kernel, tile_rows=tile_rows, n_tiles=n_tiles),
        grid_spec=grid_spec,
        out_shape=jax.ShapeDtypeStruct((P, half), jnp.int32),
        compiler_params=_cparams(1),
        name="moe_gemm",
    )(tile_start, tile_count, xs, exp_w_in, exp_w_out)


def _moe_combine_kernel(*refs, final, norm_next, shared_kv):
    it = iter(refs)
    y_ref, w_ref, h_ref, si_ref, so_ref, x_ref, g2_ref = (next(it) for _ in range(7))
    fg_ref = next(it) if final else None
    ng_ref, nsc_ref, nsh_ref = (next(it) for _ in range(3)) if norm_next else (None,) * 3
    kg_ref, wkv_ref, lg_ref, cos_ref, sin_ref = (next(it) for _ in range(5)) if shared_kv else (None,) * 5
    o_ref = next(it)
    hn_ref = next(it) if norm_next else None
    lat_ref, kr_ref = (next(it), next(it)) if shared_kv else (None, None)
    si_b, so_b = next(it), next(it)
    wkv_b = next(it) if shared_kv else None

    @pl.when(pl.program_id(0) == 0)
    def _():
        si_b[...] = si_ref[...].astype(BF16)
        so_b[...] = so_ref[...].astype(BF16)
        if shared_kv:
            wkv_b[...] = wkv_ref[...].astype(BF16)

    bb, tt, D = x_ref.shape
    half = D // 2
    w = w_ref[...].T
    acc_lo = jnp.zeros((bb * tt, half), F32)
    acc_hi = jnp.zeros((bb * tt, half), F32)
    for k in range(TOP_K):
        lo, hi = _unpack_pairs(y_ref[k], F32)
        acc_lo = acc_lo + w[:, k:k + 1] * lo
        acc_hi = acc_hi + w[:, k:k + 1] * hi
    hlo, hhi = _unpack_pairs(h_ref[...])
    hu = _dot(hlo, si_b[:half, :]) + _dot(hhi, si_b[half:, :])
    act = (_silu(hu[:, :SHARED_FF]) * hu[:, SHARED_FF:]).astype(BF16)
    y = jnp.concatenate([acc_lo, acc_hi], axis=-1) + _dot(act, so_b[...])
    x_new = x_ref[...] + g2_ref[...] * y.reshape(bb, tt, D)
    o_ref[...] = _rms(x_new, fg_ref[...]) if final else x_new
    if norm_next:
        hn = _rms(x_new, ng_ref[...]) * (1.0 + nsc_ref[...]) + nsh_ref[...]
        hn_ref[...] = hn.astype(hn_ref.dtype)
    if shared_kv:
        xn = _rms(x_new, kg_ref[...]).reshape(bb * tt, D).astype(BF16)
        z = _dot(xn, wkv_b[...])
        lat_ref[...] = _rms(z[:, :KV_LORA], lg_ref[...]).reshape(bb, tt, KV_LORA)
        zr = z[:, KV_LORA:KV_LORA + MLA_ROPE].reshape(bb, tt, MLA_ROPE)
        zq = z[:, KV_LORA + 128:KV_LORA + 128 + MLA_ROPE].reshape(bb, tt, MLA_ROPE)
        kr_ref[...] = zr * cos_ref[...] + zq * sin_ref[...]


def moe_combine(y8, w8, hp, sh_w_in, sh_w_out, x, mod, gate_idx, l, row0, final_g=None, next_norm=None,
                shared_kv=None, rows=512):
    B, T, D = x.shape
    half = D // 2
    bb, tt, nblk, ij = _row_blocks(B, T, rows)
    M = bb * tt
    assert row0 % M == 0
    off = row0 // M
    xspec = pl.BlockSpec((bb, tt, D), lambda i: ij(i) + (0,))
    in_specs = [pl.BlockSpec((TOP_K, M, half), lambda i: (0, off + i, 0)),
                pl.BlockSpec((TOP_K, M), lambda i: (0, off + i)),
                pl.BlockSpec((M, half), lambda i: (off + i, 0)),
                pl.BlockSpec((None, D, 2 * SHARED_FF), lambda i: (l, 0, 0)),
                pl.BlockSpec((None, SHARED_FF, D), lambda i: (l, 0, 0)),
                xspec,
                pl.BlockSpec((bb, 1, D), lambda i: (ij(i)[0], 0, gate_idx))]
    args = [y8, w8, hp, sh_w_in, sh_w_out, x, mod]
    out_specs = xspec
    out_shape = jax.ShapeDtypeStruct((B, T, D), F32)
    if final_g is not None:
        assert next_norm is None
        in_specs.append(pl.BlockSpec((1, D), lambda i: (0, 0)))
        args.append(final_g.reshape(1, D))
    if next_norm is not None:
        gain, mod_next, sc_idx, sh_idx = next_norm
        in_specs += [pl.BlockSpec((1, D), lambda i: (0, 0)),
                     pl.BlockSpec((bb, 1, D), lambda i: (ij(i)[0], 0, sc_idx)),
                     pl.BlockSpec((bb, 1, D), lambda i: (ij(i)[0], 0, sh_idx))]
        args += [gain.reshape(1, D), mod_next, mod_next]
        out_specs = [xspec, xspec]
        out_shape = [out_shape, jax.ShapeDtypeStruct((B, T, D), BF16)]
    scratch = [pltpu.VMEM((D, 2 * SHARED_FF), BF16), pltpu.VMEM((SHARED_FF, D), BF16)]
    if shared_kv is not None:
        kv_in_g, w_kv, kv_lat_g, cos32, sin32 = shared_kv
        tspec = pl.BlockSpec((tt, MLA_ROPE), lambda i: (ij(i)[1], 0))
        in_specs += [pl.BlockSpec((1, D), lambda i: (0, 0)),
                     pl.BlockSpec(w_kv.shape, lambda i: (0, 0)),
                     pl.BlockSpec((1, KV_LORA), lambda i: (0, 0)),
                     tspec, tspec]
        args += [kv_in_g.reshape(1, D), w_kv, kv_lat_g.reshape(1, KV_LORA), cos32, sin32]
        out_specs = list(out_specs) if isinstance(out_specs, list) else [out_specs]
        out_shape = list(out_shape) if isinstance(out_shape, list) else [out_shape]
        out_specs += [pl.BlockSpec((bb, tt, KV_LORA), lambda i: ij(i) + (0,)),
                      pl.BlockSpec((bb, tt, MLA_ROPE), lambda i: ij(i) + (0,))]
        out_shape += [jax.ShapeDtypeStruct((B, T, KV_LORA), F32), jax.ShapeDtypeStruct((B, T, MLA_ROPE), F32)]
        scratch.append(pltpu.VMEM(w_kv.shape, BF16))
    return pl.pallas_call(
        functools.partial(_moe_combine_kernel, final=final_g is not None, norm_next=next_norm is not None,
                          shared_kv=shared_kv is not None),
        grid=(nblk,),
        in_specs=in_specs,
        out_specs=out_specs,
        out_shape=out_shape,
        scratch_shapes=scratch,
        compiler_params=_cparams(1),
        name="moe_combine",
    )(*args)


def _kv_expand_kernel(lat_ref, kr_ref, wk_ref, ek_ref, wvt_ref, ones_ref, k_ref, vt_ref):
    lat = lat_ref[0].astype(BF16)
    kr = kr_ref[0].astype(BF16)
    k = _dot(lat, wk_ref[...].astype(BF16)) + _dot(kr, ek_ref[...].astype(BF16))
    k_ref[0] = k.astype(k_ref.dtype)
    vt = _dot_nt(wvt_ref[...].astype(BF16), lat) + ones_ref[...]
    vt_ref[0] = vt.astype(vt_ref.dtype)


def kv_expand(lat, kr, wk_pad, ek, wvt_ext, ones_col, rows=512):
    B, T, _ = lat.shape
    tt = rows
    NK, NVT = wk_pad.shape[1], wvt_ext.shape[0]

    def full(a):
        return pl.BlockSpec(a.shape, lambda b, t: (0, 0))

    def rowspec(n):
        return pl.BlockSpec((1, tt, n), lambda b, t: (b, t, 0))

    return pl.pallas_call(
        _kv_expand_kernel,
        grid=(B, T // tt),
        in_specs=[rowspec(KV_LORA), rowspec(MLA_ROPE), full(wk_pad), full(ek), full(wvt_ext), full(ones_col)],
        out_specs=[rowspec(NK), pl.BlockSpec((1, NVT, tt), lambda b, t: (b, 0, t))],
        out_shape=[jax.ShapeDtypeStruct((B, T, NK), BF16), jax.ShapeDtypeStruct((B, NVT, T), BF16)],
        compiler_params=_cparams(2),
        name="kv_expand",
    )(lat, kr, wk_pad, ek, wvt_ext, ones_col)


def _query_kernel(h_ref, wdq_ref, qg_ref, wq_ref, wqr_ref, c_ref, s_ref, q_ref, wdq_b, wq_b, wqr_b):
    @pl.when(pl.program_id(0) == 0)
    def _():
        wdq_b[...] = wdq_ref[...].astype(BF16)
        wq_b[...] = wq_ref[...].astype(BF16)
        wqr_b[...] = wqr_ref[...].astype(BF16)

    bb, tt, D = h_ref.shape
    h = h_ref[...].reshape(bb * tt, D)
    cq = _rms(_dot(h, wdq_b[...]), qg_ref[...]).astype(BF16)
    q1 = _dot(cq, wq_b[...]).reshape(bb, tt, -1)
    q2 = _dot(cq, wqr_b[...]).reshape(bb, tt, -1)
    c = c_ref[...]
    s = s_ref[...]
    for hd in range(MLA_HEADS):
        sl = slice(hd * HEAD_PAD, (hd + 1) * HEAD_PAD)
        q_ref[:, :, sl] = (q1[:, :, sl] * c + q2[:, :, sl] * s).astype(q_ref.dtype)


def mla_queries(h, w_dq, q_norm_g, wq_pad, wq_rot, l, c128, s128, rows=512):
    B, T, D = h.shape
    bb, tt, nblk, ij = _row_blocks(B, T, rows)
    NQ = wq_pad.shape[-1]
    tspec = pl.BlockSpec((tt, HEAD_PAD), lambda i: (ij(i)[1], 0))
    return pl.pallas_call(
        _query_kernel,
        grid=(nblk,),
        in_specs=[pl.BlockSpec((bb, tt, D), lambda i: ij(i) + (0,)),
                  pl.BlockSpec((None, D, Q_LORA), lambda i: (l, 0, 0)),
                  pl.BlockSpec((None, 1, Q_LORA), lambda i: (l, 0, 0)),
                  pl.BlockSpec((None, Q_LORA, NQ), lambda i: (l, 0, 0)),
                  pl.BlockSpec((None, Q_LORA, NQ), lambda i: (l, 0, 0)),
                  tspec, tspec],
        out_specs=pl.BlockSpec((bb, tt, NQ), lambda i: ij(i) + (0,)),
        out_shape=jax.ShapeDtypeStruct((B, T, NQ), BF16),
        scratch_shapes=[pltpu.VMEM((D, Q_LORA), BF16), pltpu.VMEM((Q_LORA, NQ), BF16),
                        pltpu.VMEM((Q_LORA, NQ), BF16)],
        compiler_params=_cparams(1),
        name="mla_queries",
    )(h, w_dq, q_norm_g.reshape(-1, 1, Q_LORA), wq_pad, wq_rot, c128, s128)


def _query_t_kernel(h_ref, wdq_ref, qg_ref, wqt_ref, wqrt_ref, cos_ref, sin_ref, qt_ref, wdq_b, wqt_b, wqrt_b):
    @pl.when((pl.program_id(0) == 0) & (pl.program_id(1) == 0))
    def _():
        wdq_b[...] = wdq_ref[...].astype(BF16)
        wqt_b[...] = wqt_ref[...].astype(BF16)
        wqrt_b[...] = wqrt_ref[...].astype(BF16)

    cq = _rms(_dot(h_ref[0], wdq_b[...]), qg_ref[...]).astype(BF16)
    q1 = _dot_nt(wqt_b[...], cq)
    q2 = _dot_nt(wqrt_b[...], cq)
    cos = cos_ref[...]
    sin = sin_ref[...]
    pad = jnp.zeros((HEAD_PAD - MLA_NOPE - MLA_ROPE, q1.shape[1]), qt_ref.dtype)
    for hd in range(MLA_HEADS):
        r0 = hd * HEAD_PAD
        rope = (q1[r0 + MLA_NOPE:r0 + MLA_NOPE + MLA_ROPE] * cos
                + q2[hd * MLA_ROPE:(hd + 1) * MLA_ROPE] * sin)
        qt_ref[0, r0:r0 + MLA_NOPE, :] = (q1[r0:r0 + MLA_NOPE] * Q_PRESCALE).astype(qt_ref.dtype)
        qt_ref[0, r0 + MLA_NOPE:r0 + MLA_NOPE + MLA_ROPE, :] = rope.astype(qt_ref.dtype)
        qt_ref[0, r0 + MLA_NOPE + MLA_ROPE:r0 + HEAD_PAD, :] = pad


def mla_queries_t(h, w_dq, q_norm_g, wq_t, wqr_t, l, cos_t, sin_t, rows=512):
    B, T, D = h.shape
    tt = rows
    NQ = wq_t.shape[1]
    NR = wqr_t.shape[1]
    tspec = pl.BlockSpec((MLA_ROPE, tt), lambda b, t: (0, t))
    return pl.pallas_call(
        _query_t_kernel,
        grid=(B, T // tt),
        in_specs=[pl.BlockSpec((1, tt, D), lambda b, t: (b, t, 0)),
                  pl.BlockSpec((None, D, Q_LORA), lambda b, t: (l, 0, 0)),
                  pl.BlockSpec((None, 1, Q_LORA), lambda b, t: (l, 0, 0)),
                  pl.BlockSpec((None, NQ, Q_LORA), lambda b, t: (l, 0, 0)),
                  pl.BlockSpec((None, NR, Q_LORA), lambda b, t: (l, 0, 0)),
                  tspec, tspec],
        out_specs=pl.BlockSpec((1, NQ, tt), lambda b, t: (b, 0, t)),
        out_shape=jax.ShapeDtypeStruct((B, NQ, T), BF16),
        scratch_shapes=[pltpu.VMEM((D, Q_LORA), BF16), pltpu.VMEM((NQ, Q_LORA), BF16),
                        pltpu.VMEM((NR, Q_LORA), BF16)],
        compiler_params=_cparams(2),
        name="mla_queries_t",
    )(h, w_dq, q_norm_g.reshape(-1, 1, Q_LORA), wq_t, wqr_t, cos_t, sin_t)


def _attn_prompt_kernel(qi_tab, ki_tab, qt_ref, k_ref, vt_ref, o_ref, *scratch, tq, tk):
    H = MLA_HEADS
    m_refs, l_refs, acc_refs = scratch[:H], scratch[H:2 * H], scratch[2 * H:]
    p_id = pl.program_id(1)
    qi = qi_tab[p_id]
    ki = ki_tab[p_id]

    @pl.when(ki == 0)
    def _():
        for hd in range(H):
            m_refs[hd][...] = jnp.full(m_refs[hd].shape, NEG_INF, F32)
            l_refs[hd][...] = jnp.zeros(l_refs[hd].shape, F32)
            acc_refs[hd][...] = jnp.zeros(acc_refs[hd].shape, F32)

    def block(masked):
        if masked:
            kchunk = (ki * tk + lax.broadcasted_iota(jnp.int32, (tk, tq), 0)) // CHUNK
            qchunk = (qi * tq + lax.broadcasted_iota(jnp.int32, (tk, tq), 1)) // CHUNK
            mask = kchunk <= qchunk
        def scores(hd):
            sl = slice(hd * HEAD_PAD, (hd + 1) * HEAD_PAD)
            return _dot(k_ref[0, :, sl], qt_ref[0, sl, :])

        pending = [scores(hd) for hd in range(ATTN_LOOKAHEAD)]
        for hd in range(H):
            if hd + ATTN_LOOKAHEAD < H:
                pending.append(scores(hd + ATTN_LOOKAHEAD))
            s = pending.pop(0)
            if masked:
                s = jnp.where(mask, s, NEG_INF)
            m_prev = m_refs[hd][...]
            m_new = jnp.maximum(m_prev, jnp.max(s, axis=0, keepdims=True))
            a = jnp.exp2(m_prev - m_new)
            p = jnp.exp2(s - m_new).astype(BF16)
            pv = _dot(vt_ref[0, hd * V_ROWS:(hd + 1) * V_ROWS, :], p)
            acc_refs[hd][...] = a * acc_refs[hd][...] + pv[:MLA_V]
            l_refs[hd][...] = a * l_refs[hd][...] + pv[MLA_V:MLA_V + 1]
            m_refs[hd][...] = m_new

    @pl.when(ki < qi)
    def _():
        block(False)

    @pl.when(ki == qi)
    def _():
        block(True)
        o_t = jnp.concatenate([acc_refs[hd][...] / l_refs[hd][...] for hd in range(H)], axis=0)
        o_ref[0] = o_t.T.astype(o_ref.dtype)


def attn_prompt(qt, k, vt, tq=256):
    B, NQ, T = qt.shape
    NVT = vt.shape[1]
    NV = MLA_HEADS * MLA_V
    tk = tq
    assert tq % CHUNK == 0
    nq = T // tq
    pairs = [(a, b) for a in range(nq) for b in range(a + 1)]
    qi_tab = jnp.asarray([a for a, _ in pairs], jnp.int32)
    ki_tab = jnp.asarray([b for _, b in pairs], jnp.int32)
    grid_spec = pltpu.PrefetchScalarGridSpec(
        num_scalar_prefetch=2,
        grid=(B, len(pairs)),
        in_specs=[pl.BlockSpec((1, NQ, tq), lambda b, p, qt, kt: (b, 0, qt[p])),
                  pl.BlockSpec((1, tk, NQ), lambda b, p, qt, kt: (b, kt[p], 0)),
                  pl.BlockSpec((1, NVT, tk), lambda b, p, qt, kt: (b, 0, kt[p]))],
        out_specs=pl.BlockSpec((1, tq, NV), lambda b, p, qt, kt: (b, qt[p], 0)),
        scratch_shapes=([pltpu.VMEM((1, tq), F32)] * (2 * MLA_HEADS)
                        + [pltpu.VMEM((MLA_V, tq), F32)] * MLA_HEADS),
    )
    return pl.pallas_call(
        functools.partial(_attn_prompt_kernel, tq=tq, tk=tk),
        grid_spec=grid_spec,
        out_shape=jax.ShapeDtypeStruct((B, T, NV), BF16),
        compiler_params=_cparams(2),
        name="attn_prompt",
    )(qi_tab, ki_tab, qt, k, vt)


def _absorb_kernel(q_ref, m_ref, o_ref):
    o_ref[...] = _dot(q_ref[...], m_ref[...].astype(BF16)).astype(o_ref.dtype)


def absorb_queries(q2d, m_abs):
    N = q2d.shape[0]
    H, _, W = m_abs.shape
    return pl.pallas_call(
        _absorb_kernel,
        grid=(H,),
        in_specs=[pl.BlockSpec((N, HEAD_PAD), lambda h: (0, h)),
                  pl.BlockSpec((None, HEAD_PAD, W), lambda h: (h, 0, 0))],
        out_specs=pl.BlockSpec((None, N, W), lambda h: (h, 0, 0)),
        out_shape=jax.ShapeDtypeStruct((H, N, W), BF16),
        compiler_params=_cparams(1),
        name="absorb_queries",
    )(q2d, m_abs)


def _attn_sample_kernel(q_ref, lat_ref, kr_ref, nlat_ref, nkr_ref, o_ref, m_ref, l_ref, acc_ref):
    kb = pl.program_id(1)
    H, Q, W = q_ref.shape
    q = q_ref[...].reshape(H * Q, W)
    q_lat = q[:, :KV_LORA]
    q_rope = q[:, KV_LORA:KV_LORA + MLA_ROPE]

    def update(lat_tile, kr_tile, n_sub, kr_transposed):
        sub = lat_tile.shape[0] // n_sub
        lats = [lat_tile[j * sub:(j + 1) * sub, :].astype(BF16) for j in range(n_sub)]
        if kr_transposed:
            krs = [kr_tile[:, j * sub:(j + 1) * sub].astype(BF16) for j in range(n_sub)]
            ss = [_dot_nt(q_lat, lat) + _dot(q_rope, kr) for lat, kr in zip(lats, krs)]
        else:
            krs = [kr_tile[j * sub:(j + 1) * sub, :].astype(BF16) for j in range(n_sub)]
            ss = [_dot_nt(q_lat, lat) + _dot_nt(q_rope, kr) for lat, kr in zip(lats, krs)]
        m_prev = m_ref[...]
        m_new = m_prev
        for s in ss:
            m_new = jnp.maximum(m_new, jnp.max(s, axis=-1, keepdims=True))
        a = jnp.exp2(m_prev - m_new)
        ps = [jnp.exp2(s - m_new[:, :1]) for s in ss]
        pv = _dot(ps[0].astype(BF16), lats[0])
        psum = jnp.sum(ps[0], axis=-1, keepdims=True)
        for p, lat in zip(ps[1:], lats[1:]):
            pv = pv + _dot(p.astype(BF16), lat)
            psum = psum + jnp.sum(p, axis=-1, keepdims=True)
        l_ref[...] = a * l_ref[...] + psum
        m_ref[...] = m_new
        acc_ref[...] = jnp.concatenate([a, a], axis=-1) * acc_ref[...] + pv

    @pl.when(kb == 0)
    def _():
        m_ref[...] = jnp.full_like(m_ref, NEG_INF)
        l_ref[...] = jnp.zeros_like(l_ref)
        acc_ref[...] = jnp.zeros_like(acc_ref)
        update(nlat_ref[0], nkr_ref[0], 1, False)

    update(lat_ref[0], kr_ref[0], SAMPLE_KEY_SUB, True)

    @pl.when(kb == pl.num_programs(1) - 1)
    def _():
        lsum = l_ref[...]
        o = acc_ref[...] / jnp.concatenate([lsum, lsum], axis=-1)
        o_ref[...] = o.reshape(H, Q, KV_LORA).astype(o_ref.dtype)


def attn_sample(q_abs, cache_lat, cache_kr_t, new_lat, new_kr, tk=4096):
    H, N, W = q_abs.shape
    B, P, _ = cache_lat.shape
    Q = new_lat.shape[1]
    qpos = P + np.arange(Q)
    kpos = np.arange(P + Q)
    assert bool(np.all((kpos // CHUNK)[None, :] <= (qpos // CHUNK)[:, None]))
    return pl.pallas_call(
        _attn_sample_kernel,
        grid=(B, P // tk),
        in_specs=[pl.BlockSpec((H, Q, W), lambda b, kb: (0, b, 0)),
                  pl.BlockSpec((1, tk, KV_LORA), lambda b, kb: (b, kb, 0)),
                  pl.BlockSpec((1, MLA_ROPE, tk), lambda b, kb: (b, 0, kb)),
                  pl.BlockSpec((1, Q, KV_LORA), lambda b, kb: (b, 0, 0)),
                  pl.BlockSpec((1, Q, MLA_ROPE), lambda b, kb: (b, 0, 0))],
        out_specs=pl.BlockSpec((H, Q, KV_LORA), lambda b, kb: (0, b, 0)),
        out_shape=jax.ShapeDtypeStruct((H, N, KV_LORA), BF16),
        scratch_shapes=[pltpu.VMEM((H * Q, 128), F32), pltpu.VMEM((H * Q, 128), F32),
                        pltpu.VMEM((H * Q, KV_LORA), F32)],
        compiler_params=_cparams(2),
        name="attn_sample",
    )(q_abs, cache_lat, cache_kr_t, new_lat, new_kr)


def _unabsorb_kernel(o_ref, w_ref, out_ref):
    out_ref[...] = (_dot(o_ref[0], w_ref[0].astype(BF16))
                    + _dot(o_ref[1], w_ref[1].astype(BF16))).astype(out_ref.dtype)


def unabsorb(o_lat, wuv_pad):
    H, N, R = o_lat.shape
    return pl.pallas_call(
        _unabsorb_kernel,
        grid=(H // 2,),
        in_specs=[pl.BlockSpec((2, N, R), lambda p: (p, 0, 0)),
                  pl.BlockSpec((2, R, 128), lambda p: (p, 0, 0))],
        out_specs=pl.BlockSpec((N, 128), lambda p: (0, p)),
        out_shape=jax.ShapeDtypeStruct((N, (H // 2) * 128), BF16),
        compiler_params=_cparams(1),
        name="unabsorb",
    )(o_lat, wuv_pad)


def _rope_tables(pos):
    half = MLA_ROPE // 2
    inv = 1.0 / (ROPE_THETA ** (np.arange(half, dtype=np.float64) * 2.0 / MLA_ROPE))
    ang = np.asarray(pos, np.float64)[:, None] * inv[None, :]
    cos = np.concatenate([np.cos(ang), np.cos(ang)], axis=-1)
    sin = np.concatenate([np.sin(ang), np.sin(ang)], axis=-1)
    T = cos.shape[0]
    c128 = np.zeros((T, HEAD_PAD)); s128 = np.zeros((T, HEAD_PAD))
    c128[:, :MLA_NOPE] = 1.0
    c128[:, MLA_NOPE:MLA_NOPE + MLA_ROPE] = cos
    s128[:, MLA_NOPE:MLA_NOPE + MLA_ROPE] = sin
    return dict(cos32=jnp.asarray(cos, F32), sin32=jnp.asarray(sin, F32),
                c128=jnp.asarray(c128 * Q_PRESCALE, F32), s128=jnp.asarray(s128 * Q_PRESCALE, F32),
                cos_t=jnp.asarray(cos.T * Q_PRESCALE, F32), sin_t=jnp.asarray(sin.T * Q_PRESCALE, F32))


def _rot_half_cols(w):
    half = w.shape[-1] // 2
    return jnp.concatenate([-w[..., half:], w[..., :half]], axis=-1)


def _prep_weights(w_dkv, w_uk, w_uv, w_uq, router_w):
    D = D_MODEL
    w_lat, w_rope = w_dkv[:, :KV_LORA], w_dkv[:, KV_LORA:]
    pad96 = jnp.zeros((D, 128 - MLA_ROPE), F32)
    w_kv = jnp.concatenate([w_lat, w_rope, pad96, _rot_half_cols(w_rope), pad96], axis=-1)

    zpad = HEAD_PAD - MLA_NOPE
    wk_pad = jnp.pad(w_uk, ((0, 0), (0, 0), (0, zpad))).reshape(KV_LORA, MLA_HEADS * HEAD_PAD)
    ek = jnp.zeros((MLA_ROPE, MLA_HEADS, HEAD_PAD), F32)
    ek = ek.at[:, :, MLA_NOPE:MLA_NOPE + MLA_ROPE].set(
        jnp.broadcast_to(jnp.eye(MLA_ROPE, dtype=F32)[:, None, :], (MLA_ROPE, MLA_HEADS, MLA_ROPE)))
    ek = ek.reshape(MLA_ROPE, MLA_HEADS * HEAD_PAD)
    wvt = jnp.transpose(w_uv, (1, 2, 0))
    wvt_ext = jnp.pad(wvt, ((0, 0), (0, V_ROWS - MLA_V), (0, 0))).reshape(MLA_HEADS * V_ROWS, KV_LORA)
    ones_col = jnp.tile((jnp.arange(V_ROWS) >= MLA_V).astype(F32), MLA_HEADS).reshape(-1, 1)

    nb = w_uq.shape[0]
    qn, qr = w_uq[..., :MLA_NOPE], w_uq[..., MLA_NOPE:]
    z32 = jnp.zeros(qr.shape[:-1] + (HEAD_PAD - MLA_NOPE - MLA_ROPE,), F32)
    wq_pad = jnp.concatenate([qn, qr, z32], axis=-1).reshape(nb, Q_LORA, MLA_HEADS * HEAD_PAD)
    wq_rot = jnp.concatenate([jnp.zeros_like(qn), _rot_half_cols(qr), z32], axis=-1)
    wq_rot = wq_rot.reshape(nb, Q_LORA, MLA_HEADS * HEAD_PAD)
    wq_t = jnp.transpose(wq_pad, (0, 2, 1))
    wqr_t = jnp.transpose(_rot_half_cols(qr).reshape(nb, Q_LORA, MLA_HEADS * MLA_ROPE), (0, 2, 1))

    m_abs = jnp.zeros((MLA_HEADS, HEAD_PAD, KV_LORA + 128), F32)
    m_abs = m_abs.at[:, :MLA_NOPE, :KV_LORA].set(jnp.transpose(w_uk, (1, 2, 0)))
    m_abs = m_abs.at[:, MLA_NOPE:MLA_NOPE + MLA_ROPE, KV_LORA:KV_LORA + MLA_ROPE].set(
        jnp.broadcast_to(jnp.eye(MLA_ROPE, dtype=F32), (MLA_HEADS, MLA_ROPE, MLA_ROPE)))

    wuv_h = jnp.transpose(w_uv, (1, 0, 2))
    even = jnp.pad(wuv_h, ((0, 0), (0, 0), (0, 64)))
    odd = jnp.pad(wuv_h, ((0, 0), (0, 0), (64, 0)))
    wuv_pad = jnp.where((jnp.arange(MLA_HEADS) % 2 == 0)[:, None, None], even, odd)

    rw_t = jnp.transpose(router_w, (0, 2, 1))
    return dict(w_kv=w_kv, wk_pad=wk_pad, ek=ek, wvt_ext=wvt_ext, ones_col=ones_col, wq_pad=wq_pad, wq_rot=wq_rot, wq_t=wq_t, wqr_t=wqr_t,
                m_abs=m_abs, wuv_pad=wuv_pad, rw_t=rw_t)


def _mixer(st, l, P, W, packed):
    rows_kw = dict(rows_total=packed["total"], row0=packed["row0"], rows_buf=packed["buf"])
    x, m = st["x"], st["mod"][l]
    B, T, _ = x.shape
    n_a = P["hg_w_in"].shape[0]
    norm2 = (P["norm2_g"][l], 4, 3)
    if l < n_a:
        zf, zqig = hgrn_proj(x, P["norm1_g"][l], m, 1, 0, P["hg_w_in"], l)
        s0 = None if st["hg_state"] is None else st["hg_state"][l]
        o, s_new = gla(zqig, zf, st["lbs"][l], P["hg_onorm_g"][l], s0)
        st["hg_new"].append(s_new)
        st["x"], packed["buf"] = linear(o, P["hg_w_out"], l, F32, x=x, mod=m, gate_idx=2, next_norm=norm2,
                                        **rows_kw)
    else:
        bi = l - n_a
        h = st.pop("h_next", None)
        if h is None:
            h = norm_mod(x, P["norm1_g"][l], m, sc_idx=1, sh_idx=0)
        if st["past_lat"] is None:
            qt = mla_queries_t(h, P["w_dq"], P["q_norm_g"], W["wq_t"], W["wqr_t"], bi, st["cos_t"], st["sin_t"])
            o = attn_prompt(qt, st["k_all"], st["v_all"])
        else:
            q = mla_queries(h, P["w_dq"], P["q_norm_g"], W["wq_pad"], W["wq_rot"], bi, st["c128"], st["s128"])
            q_abs = absorb_queries(q.reshape(B * T, -1), W["m_abs"])
            o_lat = attn_sample(q_abs, st["past_lat"], st["past_kr"], st["lat"], st["kr"])
            o = unabsorb(o_lat, W["wuv_pad"]).reshape(B, T, -1)
        st["x"], packed["buf"] = linear(o, P["w_o"], bi, F32, x=x, mod=m, gate_idx=2, next_norm=norm2, **rows_kw)
    packed["row0"] += B * T


def _moe(groups, hp, l, P, W):
    n_tok = hp.shape[0]
    n_tiles = (TOP_K * n_tok) // MOE_TILE + N_EXPERTS
    pos, w8, tile_start, tile_count = route(hp, W["rw_t"], P["router_bias"], l, MOE_TILE)
    pos_flat = pos.reshape(-1)
    src = sc_invert(pos_flat, n_tok, n_tiles * MOE_TILE)
    xs = sc_gather(hp, src)
    out = moe_gemm(xs, tile_start[:, 0], tile_count[:, 0], P["exp_w_in"], P["exp_w_out"], l,
                   MOE_TILE, n_tiles)
    y8 = sc_gather(out, pos_flat).reshape(TOP_K, n_tok, -1)
    last = l == P["norm1_g"].shape[0] - 1
    with_kv = l == P["hg_w_in"].shape[0] - 1
    next_is_mla = not last and l + 1 >= P["hg_w_in"].shape[0]
    row0 = 0
    for st in groups:
        B, T, _ = st["x"].shape
        outs = moe_combine(
            y8, w8, hp, P["sh_w_in"], P["sh_w_out"], st["x"], st["mod"][l], 5, l, row0,
            final_g=P["final_g"] if last else None,
            next_norm=(P["norm1_g"][l + 1], st["mod"][l + 1], 1, 0) if next_is_mla else None,
            shared_kv=(P["kv_in_g"], W["w_kv"], P["kv_lat_g"], st["cos32"], st["sin32"]) if with_kv else None)
        outs = list(outs) if isinstance(outs, (list, tuple)) else [outs]
        st["x"] = outs.pop(0)
        if next_is_mla:
            st["h_next"] = outs.pop(0)
        if with_kv:
            st["lat"], st["kr"] = outs
        row0 += B * T


def _group_state(x, mod, pos, hg_state, past_lat, past_kr, lbs):
    return dict(x=x, mod=mod, hg_state=hg_state, past_lat=past_lat, past_kr=past_kr, lbs=lbs,
                **_rope_tables(pos), hg_new=[],
                lat=None, kr=None, k_all=None, v_all=None)


def kernel(x_prompt, x_sample, state_hgrn, cache_mla_latent, cache_mla_krope, c_prompt, c_sample, ada_w, ada_b, norm1_g, norm2_g, hg_w_in, hg_lb_logits, hg_onorm_g, hg_w_out, kv_in_g, w_dkv, kv_lat_g, w_uk, w_uv, w_dq, q_norm_g, w_uq, w_o, router_w, router_bias, exp_w_in, exp_w_out, sh_w_in, sh_w_out, final_g):
    Bp, Sp, _ = x_prompt.shape
    Bs, Ss, _ = x_sample.shape
    past = cache_mla_latent.shape[1]
    P = dict(norm1_g=norm1_g, norm2_g=norm2_g, hg_w_in=hg_w_in, hg_lb_logits=hg_lb_logits,
             hg_onorm_g=hg_onorm_g, hg_w_out=hg_w_out, kv_in_g=kv_in_g, kv_lat_g=kv_lat_g,
             w_dq=w_dq, q_norm_g=q_norm_g, w_o=w_o, router_bias=router_bias,
             exp_w_in=exp_w_in, exp_w_out=exp_w_out, sh_w_in=sh_w_in, sh_w_out=sh_w_out, final_g=final_g)
    W = _prep_weights(w_dkv, w_uk, w_uv, w_uq, router_w)
    mod = ada_mod(jnp.concatenate([c_prompt, c_sample], axis=0), ada_w, ada_b)
    lbs = jnp.cumsum(jax.nn.softmax(hg_lb_logits.astype(F32), axis=0), axis=0)
    gp = _group_state(x_prompt, mod[:, :Bp, None, :], np.arange(Sp), None, None, None, lbs)
    gs = _group_state(x_sample, mod[:, Bp:, None, :], past + np.arange(Ss), state_hgrn,
                      cache_mla_latent, jnp.transpose(cache_mla_krope, (0, 2, 1)), lbs)
    groups = [gp, gs]
    n_tok = sum(st["x"].shape[0] * st["x"].shape[1] for st in groups)
    n_a = hg_w_in.shape[0]
    for l in range(norm1_g.shape[0]):
        packed = dict(total=n_tok, row0=0, buf=None)
        for st in groups:
            _mixer(st, l, P, W, packed)
        _moe(groups, packed["buf"], l, P, W)
        if l == n_a - 1:
            gp["k_all"], gp["v_all"] = kv_expand(gp["lat"], gp["kr"], W["wk_pad"], W["ek"], W["wvt_ext"],
                                                 W["ones_col"])
    return (gp["x"], gs["x"], jnp.stack(gp["hg_new"], axis=0), jnp.stack(gs["hg_new"], axis=0),
            gp["lat"], gp["kr"], gs["lat"], gs["kr"])
```

```python
import dataclasses
import functools

import numpy as np
import jax
import jax.numpy as jnp
from jax import lax
from jax.experimental import pallas as pl
from jax.experimental.pallas import tpu as pltpu
from jax.experimental.pallas import tpu_sc as plsc

F32 = jnp.float32
BF16 = jnp.bfloat16

D_MODEL = 1024
CHUNK = 64
HG_HEADS = 8
HG_DK = 128
HG_DV = 128
MLA_HEADS = 16
MLA_NOPE = 64
MLA_ROPE = 32
MLA_V = 64
Q_LORA = 384
KV_LORA = 256
ROPE_THETA = 10000.0
N_EXPERTS = 64
TOP_K = 8
N_GROUPS = 8
TOPK_GROUPS = 4
EXPERT_FF = 256
SHARED_FF = 256
ROUTED_SCALE = 2.5
EPS = 1e-6

HEAD_PAD = 128
SAMPLE_KEY_SUB = 8
ATTN_LOOKAHEAD = 6
V_ROWS = MLA_V + 16
QK_SCALE = (MLA_NOPE + MLA_ROPE) ** -0.5
Q_PRESCALE = QK_SCALE * float(np.log2(np.e))
VMEM_LIMIT = 56 * 1024 * 1024
NEG_INF = float("-inf")
SC_CORES = 2
SC_SUBCORES = 16
SC_WORKERS = SC_CORES * SC_SUBCORES
SC_LANES = 16
SC_WINDOW = 64
MOE_TILE = 512
MOE_NBUF = 4
MOE_SUB = 1


def _cparams(n_axes):
    return pltpu.CompilerParams(dimension_semantics=("arbitrary",) * n_axes,
                                vmem_limit_bytes=VMEM_LIMIT)


def _silu(x):
    return x * jax.nn.sigmoid(x)


def _rms(x, g):
    ms = jnp.mean(x * x, axis=-1, keepdims=True)
    return x * lax.rsqrt(ms + EPS) * g


def _dot(a, b):
    return jnp.dot(a, b, preferred_element_type=F32)


def _dot_nt(a, b):
    return lax.dot_general(a, b, (((1,), (1,)), ((), ())), preferred_element_type=F32)


def _dot_tn(a, b):
    return lax.dot_general(a, b, (((0,), (0,)), ((), ())), preferred_element_type=F32)


def _row_blocks(B, T, rows):
    if T >= rows:
        assert T % rows == 0
        bb, tt = 1, rows
    else:
        assert rows % T == 0 and B % (rows // T) == 0
        bb, tt = rows // T, T
    nt = T // tt
    return bb, tt, (B // bb) * nt, (lambda i: (i // nt, i % nt))


def _ada_kernel(c_ref, w_ref, b_ref, o_ref):
    a = _silu(c_ref[...]).astype(BF16)
    o_ref[...] = _dot(a, w_ref[...].astype(BF16)) + b_ref[...]


def ada_mod(c, ada_w, ada_b):
    R, D = c.shape
    L, _, N = ada_w.shape
    tn = 1536
    return pl.pallas_call(
        _ada_kernel,
        grid=(L, N // tn),
        in_specs=[pl.BlockSpec((R, D), lambda l, j: (0, 0)),
                  pl.BlockSpec((None, D, tn), lambda l, j: (l, 0, j)),
                  pl.BlockSpec((None, 1, tn), lambda l, j: (l, 0, j))],
        out_specs=pl.BlockSpec((None, R, tn), lambda l, j: (l, 0, j)),
        out_shape=jax.ShapeDtypeStruct((L, R, N), F32),
        compiler_params=_cparams(2),
        name="ada_mod",
    )(c, ada_w, ada_b.reshape(L, 1, N))


def _pack_pairs(y):
    half = y.shape[-1] // 2
    bits = lax.bitcast_convert_type(y.astype(BF16).astype(F32), jnp.uint32)
    word = lax.shift_right_logical(bits[:, :half], jnp.uint32(16)) | bits[:, half:]
    return lax.bitcast_convert_type(word, jnp.int32)


def _unpack_pairs(word, dtype=BF16):
    u = lax.bitcast_convert_type(word, jnp.uint32)
    lo = lax.bitcast_convert_type(lax.shift_left(u, jnp.uint32(16)), F32)
    hi = lax.bitcast_convert_type(u & jnp.uint32(0xFFFF0000), F32)
    return lo.astype(dtype), hi.astype(dtype)


def _norm_kernel(x_ref, g_ref, sc_ref, sh_ref, o_ref):
    y = _rms(x_ref[...], g_ref[...]) * (1.0 + sc_ref[...]) + sh_ref[...]
    o_ref[...] = y.astype(o_ref.dtype)


def norm_mod(x, g, mod, sc_idx, sh_idx, rows=512):
    B, T, D = x.shape
    bb, tt, nblk, ij = _row_blocks(B, T, rows)
    xspec = pl.BlockSpec((bb, tt, D), lambda i: ij(i) + (0,))
    return pl.pallas_call(
        _norm_kernel,
        grid=(nblk,),
        in_specs=[xspec, pl.BlockSpec((1, D), lambda i: (0, 0)),
                  pl.BlockSpec((bb, 1, D), lambda i: (ij(i)[0], 0, sc_idx)),
                  pl.BlockSpec((bb, 1, D), lambda i: (ij(i)[0], 0, sh_idx))],
        out_specs=xspec,
        out_shape=jax.ShapeDtypeStruct((B, T, D), BF16),
        compiler_params=_cparams(1),
        name="norm_mod",
    )(x, g.reshape(1, D), mod, mod)


def _linear_kernel(*refs, residual, norm_next, shared_rows, n_main):
    if norm_next and shared_rows:
        a_ref, w_ref, x_ref, gate_ref, ng_ref, nsc_ref, nsh_ref, _, o_ref, hp_ref, wb_ref = refs
    elif norm_next:
        a_ref, w_ref, x_ref, gate_ref, ng_ref, nsc_ref, nsh_ref, o_ref, hp_ref, wb_ref = refs
    elif residual:
        a_ref, w_ref, x_ref, gate_ref, o_ref, wb_ref = refs
    else:
        a_ref, w_ref, o_ref, wb_ref = refs

    @pl.when(pl.program_id(1) == 0)
    def _():
        wb_ref[...] = w_ref[...].astype(BF16)

    def main():
        bb, tt, K = a_ref.shape
        y = _dot(a_ref[...].reshape(bb * tt, K).astype(BF16), wb_ref[...])
        y = y.reshape(bb, tt, y.shape[-1])
        if residual:
            y = x_ref[...] + gate_ref[...] * y
        o_ref[...] = y.astype(o_ref.dtype)
        if norm_next:
            h = _rms(y, ng_ref[...]) * (1.0 + nsc_ref[...]) + nsh_ref[...]
            hp_ref[...] = _pack_pairs(h.reshape(bb * tt, h.shape[-1]))

    if n_main is None:
        main()
    else:
        pl.when(pl.program_id(1) < n_main)(main)

        @pl.when(pl.program_id(1) >= n_main)
        def _():
            hp_ref[...] = jnp.zeros(hp_ref.shape, hp_ref.dtype)


def linear(a, w, l, out_dtype, x=None, mod=None, gate_idx=0, rows=512, tn=1024, next_norm=None,
           rows_total=None, row0=0, rows_buf=None):
    B, T, K = a.shape
    _, _, N = w.shape
    tn = min(tn, N)
    bb, tt, nblk, ij0 = _row_blocks(B, T, rows)
    n_extra = 0
    if next_norm is not None and rows_buf is None and rows_total is not None:
        assert row0 == 0 and (rows_total - B * T) % (bb * tt) == 0
        n_extra = (rows_total - B * T) // (bb * tt)

    def ij(i):
        return ij0(jnp.minimum(i, nblk - 1)) if n_extra else ij0(i)

    in_specs = [pl.BlockSpec((bb, tt, K), lambda j, i: ij(i) + (0,)),
                pl.BlockSpec((None, K, tn), lambda j, i: (l, 0, j))]
    args = [a, w]
    ospec = pl.BlockSpec((bb, tt, tn), lambda j, i: ij(i) + (j,))
    out_specs = ospec
    out_shape = jax.ShapeDtypeStruct((B, T, N), out_dtype)
    aliases = {}
    if x is not None:
        gsteps = D_MODEL // tn
        in_specs += [ospec, pl.BlockSpec((bb, 1, tn), lambda j, i: (ij(i)[0], 0, gate_idx * gsteps + j))]
        args += [x, mod]
    if next_norm is not None:
        assert x is not None and tn == N
        gain, sc_idx, sh_idx = next_norm
        in_specs += [pl.BlockSpec((1, N), lambda j, i: (0, 0)),
                     pl.BlockSpec((bb, 1, N), lambda j, i: (ij(i)[0], 0, sc_idx)),
                     pl.BlockSpec((bb, 1, N), lambda j, i: (ij(i)[0], 0, sh_idx))]
        args += [gain.reshape(1, N), mod, mod]
        assert row0 % (bb * tt) == 0
        off = row0 // (bb * tt)
        out_specs = [ospec, pl.BlockSpec((bb * tt, N // 2), lambda j, i: (off + i, 0))]
        out_shape = [out_shape, jax.ShapeDtypeStruct((rows_total or B * T, N // 2), jnp.int32)]
        if rows_buf is not None:
            in_specs.append(pl.BlockSpec(memory_space=pl.ANY))
            args.append(rows_buf)
            aliases = {len(args) - 1: 1}
    return pl.pallas_call(
        functools.partial(_linear_kernel, residual=x is not None, norm_next=next_norm is not None,
                          shared_rows=rows_buf is not None, n_main=nblk if n_extra else None),
        grid=(N // tn, nblk + n_extra),
        in_specs=in_specs,
        out_specs=out_specs,
        out_shape=out_shape,
        scratch_shapes=[pltpu.VMEM((K, tn), BF16)],
        input_output_aliases=aliases,
        compiler_params=_cparams(2),
        name="linear",
    )(*args)


def _hgrn_proj_kernel(x_ref, g_ref, sc_ref, sh_ref, w_ref, zf_ref, zqig_ref, h_b, w_b):
    i = pl.program_id(0)
    j = pl.program_id(1)
    bb, tt, D = x_ref.shape

    @pl.when(i == 0)
    def _():
        w_b[j] = w_ref[...].astype(BF16)

    @pl.when(j == 0)
    def _():
        h = _rms(x_ref[...], g_ref[...]) * (1.0 + sc_ref[...]) + sh_ref[...]
        h_b[...] = h.reshape(bb * tt, D).astype(BF16)

    y = _dot(h_b[...], w_b[j]).reshape(bb, tt, -1)

    @pl.when(j == 1)
    def _():
        zf_ref[...] = y

    @pl.when(j != 1)
    def _():
        zqig_ref[...] = y.astype(zqig_ref.dtype)


def hgrn_proj(x, g, mod, sc_idx, sh_idx, w_in, l, rows=1024):
    B, T, D = x.shape
    bb, tt, nblk, ij = _row_blocks(B, T, min(rows, B * T))
    xspec = pl.BlockSpec((bb, tt, D), lambda i, j: ij(i) + (0,))
    return pl.pallas_call(
        _hgrn_proj_kernel,
        grid=(nblk, 4),
        in_specs=[xspec,
                  pl.BlockSpec((1, D), lambda i, j: (0, 0)),
                  pl.BlockSpec((bb, 1, D), lambda i, j: (ij(i)[0], 0, sc_idx)),
                  pl.BlockSpec((bb, 1, D), lambda i, j: (ij(i)[0], 0, sh_idx)),
                  pl.BlockSpec((None, D, D), lambda i, j: (l, 0, jnp.where(i == 0, j, 3)))],
        out_specs=[xspec,
                   pl.BlockSpec((bb, tt, D), lambda i, j: ij(i) + (j - (j >= 1),))],
        out_shape=[jax.ShapeDtypeStruct((B, T, D), F32), jax.ShapeDtypeStruct((B, T, 3 * D), BF16)],
        scratch_shapes=[pltpu.VMEM((bb * tt, D), BF16), pltpu.VMEM((4, D, D), BF16)],
        compiler_params=_cparams(2),
        name="hgrn_proj",
    )(x, g.reshape(1, D), mod, mod, w_in)


def _gla_kernel(*refs, L, n_chunks, has_init):
    if has_init:
        q_ref, f_ref, i_ref, g_ref, lb_ref, on_ref, s0_ref, o_ref, so_ref, st_ref = refs
    else:
        q_ref, f_ref, i_ref, g_ref, lb_ref, on_ref, o_ref, so_ref, st_ref = refs
    t = pl.program_id(1)
    H = st_ref.shape[0]

    @pl.when(t == 0)
    def _():
        for h in range(H):
            if has_init:
                st_ref[h] = s0_ref[0, h].T
            else:
                st_ref[h] = jnp.zeros(st_ref.shape[1:], F32)

    lb = lb_ref[...]
    onorm = on_ref[...]
    row = lax.broadcasted_iota(jnp.int32, (L, L), 0)
    col = lax.broadcasted_iota(jnp.int32, (L, L), 1)
    causal = col <= row
    tri = causal.astype(BF16)

    def chunk(c, carry):
        rows = pl.ds(pl.multiple_of(c * L, L), L)

        def write_o(sl, o):
            o_ref[0, rows, sl] = o.astype(o_ref.dtype)

        _gla_chunk(q_ref[0, rows, :], f_ref[0, rows, :], i_ref[0, rows, :], g_ref[0, rows, :],
                   lb, onorm, tri, causal, st_ref, write_o)
        return carry

    lax.fori_loop(0, n_chunks, chunk, 0, unroll=4 if n_chunks % 4 == 0 else 1)

    @pl.when(t == pl.num_programs(1) - 1)
    def _():
        for h in range(H):
            so_ref[0, h] = st_ref[h].T


def _gla_chunk(q, f, v, g, lb, onorm, tri, causal, st_ref, write_o):
    L = q.shape[0]
    H = st_ref.shape[0]
    mid = L // 2 - 1
    q = _silu(q.astype(F32))
    fg = lb + (1.0 - lb) * jax.nn.sigmoid(f)
    k = 1.0 - fg
    v = v.astype(BF16)
    gate = _silu(g.astype(F32))
    logf = jnp.log(fg)
    hi = logf.astype(BF16)
    lo = (logf - hi.astype(F32)).astype(BF16)
    b = _dot(tri, hi) + _dot(tri, lo)
    b_mid = b[mid:mid + 1, :]
    b_last = b[L - 1:L, :]
    qa = q * jnp.exp(b - b_mid)
    kb = k * jnp.exp(b_mid - b)
    qe = (qa * jnp.exp(b_mid)).astype(BF16)
    kd = (kb * jnp.exp(b_last - b_mid)).astype(BF16)
    qa = qa.astype(BF16)
    kb = kb.astype(BF16)
    decay = jnp.exp(b_last)
    sls = [slice(h * HG_DK, (h + 1) * HG_DK) for h in range(H)]
    sts = [st_ref[h] for h in range(H)]
    scores = [_dot_nt(qa[:, sl], kb[:, sl]) for sl in sls]
    inter = [_dot_nt(qe[:, sl], st.astype(BF16)) for sl, st in zip(sls, sts)]
    outer = [_dot_tn(v[:, sl], kd[:, sl]) for sl in sls]
    intra = [_dot(jnp.where(causal, sc, 0.0).astype(BF16), v[:, sl]) for sc, sl in zip(scores, sls)]
    for h, sl in enumerate(sls):
        st_ref[h] = sts[h] * decay[:, sl] + outer[h]
        write_o(sl, _rms(inter[h] + intra[h], onorm[:, sl]) * gate[:, sl])


def gla(zqig, zf, lb, onorm_g, s0):
    B, T, D = zf.shape
    L = CHUNK if T % CHUNK == 0 else T
    tt = min(T, 512)
    n_chunks = tt // L
    H = HG_HEADS

    def zspec(part):
        return pl.BlockSpec((1, tt, D), lambda b, t: (b, t, part))

    hspec = pl.BlockSpec((1, D), lambda b, t: (0, 0))
    sspec = pl.BlockSpec((1, H, HG_DK, HG_DV), lambda b, t: (b, 0, 0, 0))
    in_specs = [zspec(0), zspec(0), zspec(1), zspec(2), hspec, hspec]
    args = [zqig, zf, zqig, zqig, lb.reshape(1, D), onorm_g.reshape(1, D)]
    if s0 is not None:
        in_specs.append(sspec)
        args.append(s0)
    return pl.pallas_call(
        functools.partial(_gla_kernel, L=L, n_chunks=n_chunks, has_init=s0 is not None),
        grid=(B, T // tt),
        in_specs=in_specs,
        out_specs=[pl.BlockSpec((1, tt, D), lambda b, t: (b, t, 0)), sspec],
        out_shape=[jax.ShapeDtypeStruct((B, T, D), BF16),
                   jax.ShapeDtypeStruct((B, H, HG_DK, HG_DV), F32)],
        scratch_shapes=[pltpu.VMEM((H, HG_DV, HG_DK), F32)],
        compiler_params=_cparams(2),
        name="gla",
    )(*args)


def _route_kernel(h_ref, rw_ref, bias_ref, pos_ref, w_ref, te_ref, nu_ref,
                  e_s, r_s, base_s, start_s, *, tile_rows):
    ph = pl.program_id(0)
    i = pl.program_id(1)
    M = h_ref.shape[0]
    half = h_ref.shape[1]
    G, E = N_GROUPS, N_EXPERTS // N_GROUPS
    e_flat = lax.broadcasted_iota(jnp.int32, (N_EXPERTS, M), 0)

    @pl.when(ph == 1)
    def _():
        @pl.when(i == 0)
        def _():
            cnt = base_s[...]
            padded = jnp.floor((cnt + (tile_rows - 1)) * (1.0 / tile_rows)) * tile_rows
            r = lax.broadcasted_iota(jnp.int32, (N_EXPERTS, N_EXPERTS), 0)
            c = lax.broadcasted_iota(jnp.int32, (N_EXPERTS, N_EXPERTS), 1)
            start = jnp.dot((c < r).astype(F32), padded, preferred_element_type=F32,
                            precision=lax.Precision.HIGHEST)
            start_s[...] = start
            te_ref[...] = (start * (1.0 / tile_rows)).astype(jnp.int32)
            nu_ref[...] = (padded * (1.0 / tile_rows)).astype(jnp.int32)

        start_col = start_s[:, :1]
        for k in range(TOP_K):
            hit = e_flat == e_s[i, k:k + 1, :]
            seg = jnp.sum(jnp.where(hit, start_col, 0.0), axis=0, keepdims=True)
            pos_ref[k:k + 1, :] = (seg + r_s[i, k:k + 1, :]).astype(jnp.int32)

    @pl.when(ph == 0)
    def _():
        _route_pass0(h_ref, rw_ref, bias_ref, w_ref, e_s, r_s, base_s, i, M, half, G, E)


def _route_pass0(h_ref, rw_ref, bias_ref, w_ref, e_s, r_s, base_s, i, M, half, G, E):
    @pl.when(i == 0)
    def _():
        base_s[...] = jnp.zeros_like(base_s)

    lo, hi = _unpack_pairs(h_ref[...])
    rw = rw_ref[...].astype(BF16)
    logits = _dot_nt(rw[:, :half], lo) + _dot_nt(rw[:, half:], hi)
    s = jax.nn.sigmoid(logits)
    sb = (s + bias_ref[...]).reshape(G, E, M)
    s = s.reshape(G, E, M)
    e_in = lax.broadcasted_iota(jnp.int32, (G, E, M), 1).astype(F32)
    g_id = lax.broadcasted_iota(jnp.int32, (G, 1, M), 0)
    e_id = lax.broadcasted_iota(jnp.int32, (G, E, M), 0).astype(F32) * E + e_in

    def all_max(a):
        return jnp.max(jnp.max(a, axis=0, keepdims=True), axis=1, keepdims=True)

    def all_min(a):
        return jnp.min(jnp.min(a, axis=0, keepdims=True), axis=1, keepdims=True)

    def all_sum(a):
        return jnp.sum(jnp.sum(a, axis=0, keepdims=True), axis=1, keepdims=True)

    m1 = jnp.max(sb, axis=1, keepdims=True)
    first = jnp.min(jnp.where(sb == m1, e_in, float(E)), axis=1, keepdims=True)
    m2 = jnp.max(jnp.where(e_in == first, NEG_INF, sb), axis=1, keepdims=True)
    gs = m1 + m2

    rank = jnp.zeros((G, 1, M), jnp.int32)
    for j in range(G):
        gj = gs[j:j + 1]
        beats = (gj > gs) | ((gj == gs) & (j < g_id))
        rank = rank + beats.astype(jnp.int32)
    gsel = rank < TOPK_GROUPS

    vals = jnp.where(gsel, sb, NEG_INF)
    selm = jnp.zeros((G, E, M), F32)
    chosen, score = [], []
    for _ in range(TOP_K):
        m = all_max(vals)
        first = all_min(jnp.where(vals == m, e_id, float(N_EXPERTS)))
        hit = e_id == first
        score.append(all_sum(jnp.where(hit, s, 0.0)))
        selm = jnp.where(hit, 1.0, selm)
        vals = jnp.where(hit, NEG_INF, vals)
        chosen.append(first)

    tot = score[0]
    for sc in score[1:]:
        tot = tot + sc
    norm = ROUTED_SCALE / tot

    selm = selm.reshape(N_EXPERTS, M)
    earlier = (lax.broadcasted_iota(jnp.int32, (M, M), 0)
               < lax.broadcasted_iota(jnp.int32, (M, M), 1)).astype(BF16)
    rank = (base_s[:, :1] + _dot(selm.astype(BF16), earlier)).reshape(G, E, M)
    base_s[...] = base_s[...] + jnp.sum(selm, axis=1, keepdims=True)
    for k in range(TOP_K):
        hit = e_id == chosen[k]
        e_s[i, k:k + 1, :] = chosen[k].reshape(1, M).astype(jnp.int32)
        r_s[i, k:k + 1, :] = all_sum(jnp.where(hit, rank, 0.0)).reshape(1, M)
        w_ref[k:k + 1, :] = (score[k] * norm).reshape(1, M)


def route(hp, router_w_t, router_bias, l, tile_rows, rows=512):
    N, half = hp.shape
    M = rows
    nT = N // M
    assert N % M == 0

    def p0(ph, i):
        return i * (1 - ph) + (nT - 1) * ph

    return pl.pallas_call(
        functools.partial(_route_kernel, tile_rows=tile_rows),
        grid=(2, nT),
        in_specs=[pl.BlockSpec((M, half), lambda ph, i: (p0(ph, i), 0)),
                  pl.BlockSpec((None, N_EXPERTS, 2 * half), lambda ph, i: (l, 0, 0)),
                  pl.BlockSpec((None, N_EXPERTS, 1), lambda ph, i: (l, 0, 0))],
        out_specs=[pl.BlockSpec((TOP_K, M), lambda ph, i: (0, i * ph)),
                   pl.BlockSpec((TOP_K, M), lambda ph, i: (0, p0(ph, i))),
                   pl.BlockSpec((N_EXPERTS, 128), lambda ph, i: (0, 0)),
                   pl.BlockSpec((N_EXPERTS, 128), lambda ph, i: (0, 0))],
        out_shape=[jax.ShapeDtypeStruct((TOP_K, N), jnp.int32),
                   jax.ShapeDtypeStruct((TOP_K, N), F32),
                   jax.ShapeDtypeStruct((N_EXPERTS, 128), jnp.int32),
                   jax.ShapeDtypeStruct((N_EXPERTS, 128), jnp.int32)],
        scratch_shapes=[pltpu.VMEM((nT, TOP_K, M), jnp.int32), pltpu.VMEM((nT, TOP_K, M), F32),
                        pltpu.VMEM((N_EXPERTS, 128), F32), pltpu.VMEM((N_EXPERTS, 128), F32)],
        compiler_params=_cparams(2),
        name="route",
    )(hp, router_w_t, router_bias.reshape(-1, N_EXPERTS, 1))


def _sc_mesh():
    return plsc.VectorSubcoreMesh(core_axis_name="core", subcore_axis_name="subcore")


def sc_invert(pos_flat, n_tok, n_out):
    n = pos_flat.shape[0]
    per = n_out // SC_WORKERS
    chunk = n_tok
    assert n_out % SC_WORKERS == 0 and per % SC_LANES == 0
    assert n_tok % chunk == 0 and n % chunk == 0 and chunk % SC_LANES == 0
    cp = pltpu.CompilerParams()
    if "needs_layout_passes" in pltpu.CompilerParams.__dataclass_fields__:
        cp = dataclasses.replace(cp, needs_layout_passes=False)

    @functools.partial(
        pl.kernel, out_type=jax.ShapeDtypeStruct((n_out,), jnp.int32), mesh=_sc_mesh(),
        scratch_types=[pltpu.VMEM((chunk,), jnp.int32), pltpu.VMEM((per,), jnp.int32)],
        compiler_params=cp, name="sc_invert")
    def k(pos_hbm, src_hbm, pos_v, src_v):
        wid = lax.axis_index("subcore") * SC_CORES + lax.axis_index("core")
        lo = wid * per
        lane = lax.iota(jnp.int32, SC_LANES)

        @pl.loop(0, per, step=SC_LANES)
        def _(r):
            src_v[pl.ds(r, SC_LANES)] = lax.rem(lo + r + lane, n_tok)

        @pl.loop(0, n // chunk)
        def _(c):
            base = c * chunk
            pltpu.sync_copy(pos_hbm.at[pl.ds(base, chunk)], pos_v)
            tok0 = lax.rem(base, n_tok)

            @plsc.parallel_loop(0, chunk, step=SC_LANES, unroll=8)
            def _(r):
                p = pos_v[pl.ds(r, SC_LANES)] - lo
                mine = (p >= 0) & (p < per)
                plsc.store_scatter(src_v, [jnp.where(mine, p, 0)], tok0 + r + lane, mask=mine)

        pltpu.sync_copy(src_v, src_hbm.at[pl.ds(lo, per)])

    return k(pos_flat)


def sc_gather(x, idx):
    n = idx.shape[0]
    dim = x.shape[1]
    assert n % (SC_WINDOW * SC_WORKERS) == 0

    @functools.partial(
        pl.kernel, out_type=jax.ShapeDtypeStruct((n, dim), x.dtype), mesh=_sc_mesh(),
        scratch_types=[], name="sc_gather")
    def k(x_hbm, i_hbm, o_hbm):
        def body(i_vmem, o_vmem):
            pltpu.sync_copy(x_hbm.at[i_vmem.at[0]], o_vmem)

        pltpu.emit_pipeline(
            body, grid=(n // SC_WINDOW,),
            in_specs=[pl.BlockSpec((1, SC_WINDOW), index_map=lambda i: (i, 0))],
            out_specs=[pl.BlockSpec((SC_WINDOW, dim), index_map=lambda i: (i, 0))],
            core_axis_name=("core", "subcore"),
            dimension_semantics=(pltpu.PARALLEL,),
        )(i_hbm, o_hbm)

    return k(x, idx.reshape(n // SC_WINDOW, SC_WINDOW))


def _moe_gemm_kernel(ts_ref, tn_ref, x_hbm, wi_ref, wo_ref, o_hbm, wi_b, wo_b, xbuf, obuf, in_sem, out_sem,
                     *, tile_rows, n_tiles):
    e = pl.program_id(0)
    last = pl.num_programs(0) - 1
    t0 = ts_ref[e]
    n = tn_ref[e]
    n_used = ts_ref[last] + tn_ref[last]

    def x_copy(g, slot):
        rows = pl.ds(pl.multiple_of(g * tile_rows, tile_rows), tile_rows)
        return pltpu.make_async_copy(x_hbm.at[rows], xbuf.at[slot], in_sem.at[slot])

    def o_copy(g, slot):
        rows = pl.ds(pl.multiple_of(g * tile_rows, tile_rows), tile_rows)
        return pltpu.make_async_copy(obuf.at[slot], o_hbm.at[rows], out_sem.at[slot])

    @pl.when(e == 0)
    def _():
        for g0 in range(MOE_NBUF - 1):
            @pl.when(g0 < n_used)
            def _():
                x_copy(g0, g0).start()

    @pl.when(n > 0)
    def _():
        wi_b[...] = wi_ref[...].astype(BF16)
        wo_b[...] = wo_ref[...].astype(BF16)

    def tile(i, carry):
        g = t0 + i
        slot = lax.rem(g, MOE_NBUF)
        x_copy(g, slot).wait()
        ahead = g + (MOE_NBUF - 1)

        @pl.when(ahead < n_used)
        def _():
            x_copy(ahead, lax.rem(ahead, MOE_NBUF)).start()

        @pl.when(g >= MOE_NBUF)
        def _():
            o_copy(g - MOE_NBUF, slot).wait()

        rows = tile_rows // MOE_SUB
        half = xbuf.shape[2]
        xs = [_unpack_pairs(xbuf[slot, r * rows:(r + 1) * rows, :]) for r in range(MOE_SUB)]
        hus = [_dot(lo, wi_b[:half, :]) + _dot(hi, wi_b[half:, :]) for lo, hi in xs]
        acts = [(_silu(hu[:, :EXPERT_FF]) * hu[:, EXPERT_FF:]).astype(BF16) for hu in hus]
        outs = [_dot(act, wo_b[...]) for act in acts]
        for r, out in enumerate(outs):
            obuf[slot, r * rows:(r + 1) * rows, :] = _pack_pairs(out)
        o_copy(g, slot).start()
        return carry

    lax.fori_loop(0, n, tile, 0)

    @pl.when(e == last)
    def _():
        for back in range(MOE_NBUF, 0, -1):
            @pl.when(n_used >= back)
            def _():
                o_copy(n_used - back, lax.rem(n_used - back, MOE_NBUF)).wait()

        obuf[0] = jnp.zeros(obuf.shape[1:], obuf.dtype)

        def clear(g, carry):
            cp = o_copy(g, 0)
            cp.start()
            cp.wait()
            return carry

        lax.fori_loop(n_used, n_tiles, clear, 0)


def moe_gemm(xs, tile_start, tile_count, exp_w_in, exp_w_out, l, tile_rows, n_tiles):
    P, half = xs.shape
    D = 2 * half
    assert P == n_tiles * tile_rows
    hbm = pl.BlockSpec(memory_space=pl.ANY)
    grid_spec = pltpu.PrefetchScalarGridSpec(
        num_scalar_prefetch=2,
        grid=(N_EXPERTS,),
        in_specs=[hbm,
                  pl.BlockSpec((None, None, D, 2 * EXPERT_FF), lambda e, ts, tn: (l, e, 0, 0)),
                  pl.BlockSpec((None, None, EXPERT_FF, D), lambda e, ts, tn: (l, e, 0, 0))],
        out_specs=hbm,
        scratch_shapes=[pltpu.VMEM((D, 2 * EXPERT_FF), BF16), pltpu.VMEM((EXPERT_FF, D), BF16),
                        pltpu.VMEM((MOE_NBUF, tile_rows, half), jnp.int32),
                        pltpu.VMEM((MOE_NBUF, tile_rows, half), jnp.int32),
                        pltpu.SemaphoreType.DMA((MOE_NBUF,)), pltpu.SemaphoreType.DMA((MOE_NBUF,))],
    )
    return pl.pallas_call(
        functools.partial(_moe_gemm_kernel, tile_rows=tile_rows, n_tiles=n_tiles),
        grid_spec=grid_spec,
        out_shape=jax.ShapeDtypeStruct((P, half), jnp.int32),
        compiler_params=_cparams(1),
        name="moe_gemm",
    )(tile_start, tile_count, xs, exp_w_in, exp_w_out)


def _moe_combine_kernel(*refs, final, norm_next, shared_kv, queries):
    it = iter(refs)
    y_ref, w_ref, h_ref, si_ref, so_ref, x_ref, g2_ref = (next(it) for _ in range(7))
    fg_ref = next(it) if final else None
    ng_ref, nsc_ref, nsh_ref = (next(it) for _ in range(3)) if norm_next else (None,) * 3
    kg_ref, wkv_ref, lg_ref, cos_ref, sin_ref = (next(it) for _ in range(5)) if shared_kv else (None,) * 5
    wdq_ref, qg_ref, wqt_ref, wqrt_ref, cost_ref, sint_ref = (next(it) for _ in range(6)) if queries else (None,) * 6
    o_ref = next(it)
    hn_ref = next(it) if norm_next and not queries else None
    lat_ref, kr_ref = (next(it), next(it)) if shared_kv else (None, None)
    qt_ref = next(it) if queries else None
    si_b, so_b = next(it), next(it)
    wkv_b = next(it) if shared_kv else None
    wdq_b, wqt_b, wqrt_b = (next(it), next(it), next(it)) if queries else (None,) * 3

    @pl.when(pl.program_id(0) == 0)
    def _():
        si_b[...] = si_ref[...].astype(BF16)
        so_b[...] = so_ref[...].astype(BF16)
        if shared_kv:
            wkv_b[...] = wkv_ref[...].astype(BF16)
        if queries:
            wdq_b[...] = wdq_ref[...].astype(BF16)
            wqt_b[...] = wqt_ref[...].astype(BF16)
            wqrt_b[...] = wqrt_ref[...].astype(BF16)

    bb, tt, D = x_ref.shape
    half = D // 2
    w = w_ref[...].T
    acc_lo = jnp.zeros((bb * tt, half), F32)
    acc_hi = jnp.zeros((bb * tt, half), F32)
    for k in range(TOP_K):
        lo, hi = _unpack_pairs(y_ref[k], F32)
        acc_lo = acc_lo + w[:, k:k + 1] * lo
        acc_hi = acc_hi + w[:, k:k + 1] * hi
    hlo, hhi = _unpack_pairs(h_ref[...])
    hu = _dot(hlo, si_b[:half, :]) + _dot(hhi, si_b[half:, :])
    act = (_silu(hu[:, :SHARED_FF]) * hu[:, SHARED_FF:]).astype(BF16)
    y = jnp.concatenate([acc_lo, acc_hi], axis=-1) + _dot(act, so_b[...])
    x_new = x_ref[...] + g2_ref[...] * y.reshape(bb, tt, D)
    o_ref[...] = _rms(x_new, fg_ref[...]) if final else x_new
    if norm_next:
        hn = (_rms(x_new, ng_ref[...]) * (1.0 + nsc_ref[...]) + nsh_ref[...]).astype(BF16)
        if queries:
            _queries_t(hn.reshape(bb * tt, D), wdq_b, qg_ref[...], wqt_b, wqrt_b, cost_ref[...], sint_ref[...],
                       qt_ref)
        else:
            hn_ref[...] = hn
    if shared_kv:
        xn = _rms(x_new, kg_ref[...]).reshape(bb * tt, D).astype(BF16)
        z = _dot(xn, wkv_b[...])
        lat_ref[...] = _rms(z[:, :KV_LORA], lg_ref[...]).reshape(bb, tt, KV_LORA)
        zr = z[:, KV_LORA:KV_LORA + MLA_ROPE].reshape(bb, tt, MLA_ROPE)
        zq = z[:, KV_LORA + 128:KV_LORA + 128 + MLA_ROPE].reshape(bb, tt, MLA_ROPE)
        kr_ref[...] = zr * cos_ref[...] + zq * sin_ref[...]


def moe_combine(y8, w8, hp, sh_w_in, sh_w_out, x, mod, gate_idx, l, row0, final_g=None, next_norm=None,
                shared_kv=None, queries=None, rows=512):
    B, T, D = x.shape
    half = D // 2
    bb, tt, nblk, ij = _row_blocks(B, T, rows)
    M = bb * tt
    assert row0 % M == 0
    off = row0 // M
    xspec = pl.BlockSpec((bb, tt, D), lambda i: ij(i) + (0,))
    in_specs = [pl.BlockSpec((TOP_K, M, half), lambda i: (0, off + i, 0)),
                pl.BlockSpec((TOP_K, M), lambda i: (0, off + i)),
                pl.BlockSpec((M, half), lambda i: (off + i, 0)),
                pl.BlockSpec((None, D, 2 * SHARED_FF), lambda i: (l, 0, 0)),
                pl.BlockSpec((None, SHARED_FF, D), lambda i: (l, 0, 0)),
                xspec,
                pl.BlockSpec((bb, 1, D), lambda i: (ij(i)[0], 0, gate_idx))]
    args = [y8, w8, hp, sh_w_in, sh_w_out, x, mod]
    out_specs = xspec
    out_shape = jax.ShapeDtypeStruct((B, T, D), F32)
    if final_g is not None:
        assert next_norm is None
        in_specs.append(pl.BlockSpec((1, D), lambda i: (0, 0)))
        args.append(final_g.reshape(1, D))
    if next_norm is not None:
        gain, mod_next, sc_idx, sh_idx = next_norm
        in_specs += [pl.BlockSpec((1, D), lambda i: (0, 0)),
                     pl.BlockSpec((bb, 1, D), lambda i: (ij(i)[0], 0, sc_idx)),
                     pl.BlockSpec((bb, 1, D), lambda i: (ij(i)[0], 0, sh_idx))]
        args += [gain.reshape(1, D), mod_next, mod_next]
        if queries is None:
            out_specs = [xspec, xspec]
            out_shape = [out_shape, jax.ShapeDtypeStruct((B, T, D), BF16)]
    scratch = [pltpu.VMEM((D, 2 * SHARED_FF), BF16), pltpu.VMEM((SHARED_FF, D), BF16)]
    if shared_kv is not None:
        kv_in_g, w_kv, kv_lat_g, cos32, sin32 = shared_kv
        tspec = pl.BlockSpec((tt, MLA_ROPE), lambda i: (ij(i)[1], 0))
        in_specs += [pl.BlockSpec((1, D), lambda i: (0, 0)),
                     pl.BlockSpec(w_kv.shape, lambda i: (0, 0)),
                     pl.BlockSpec((1, KV_LORA), lambda i: (0, 0)),
                     tspec, tspec]
        args += [kv_in_g.reshape(1, D), w_kv, kv_lat_g.reshape(1, KV_LORA), cos32, sin32]
        out_specs = list(out_specs) if isinstance(out_specs, list) else [out_specs]
        out_shape = list(out_shape) if isinstance(out_shape, list) else [out_shape]
        out_specs += [pl.BlockSpec((bb, tt, KV_LORA), lambda i: ij(i) + (0,)),
                      pl.BlockSpec((bb, tt, MLA_ROPE), lambda i: ij(i) + (0,))]
        out_shape += [jax.ShapeDtypeStruct((B, T, KV_LORA), F32), jax.ShapeDtypeStruct((B, T, MLA_ROPE), F32)]
        scratch.append(pltpu.VMEM(w_kv.shape, BF16))
    if queries is not None:
        assert next_norm is not None and bb == 1
        w_dq, q_norm_g, wq_t, wqr_t, bi, cos_t, sin_t = queries
        NQ, NR = wq_t.shape[1], wqr_t.shape[1]
        tspec_t = pl.BlockSpec((MLA_ROPE, tt), lambda i: (0, ij(i)[1]))
        in_specs += [pl.BlockSpec((None, D, Q_LORA), lambda i: (bi, 0, 0)),
                     pl.BlockSpec((None, 1, Q_LORA), lambda i: (bi, 0, 0)),
                     pl.BlockSpec((None, NQ, Q_LORA), lambda i: (bi, 0, 0)),
                     pl.BlockSpec((None, NR, Q_LORA), lambda i: (bi, 0, 0)),
                     tspec_t, tspec_t]
        args += [w_dq, q_norm_g.reshape(-1, 1, Q_LORA), wq_t, wqr_t, cos_t, sin_t]
        out_specs = list(out_specs) if isinstance(out_specs, list) else [out_specs]
        out_shape = list(out_shape) if isinstance(out_shape, list) else [out_shape]
        out_specs.append(pl.BlockSpec((1, NQ, tt), lambda i: (ij(i)[0], 0, ij(i)[1])))
        out_shape.append(jax.ShapeDtypeStruct((B, NQ, T), BF16))
        scratch += [pltpu.VMEM((D, Q_LORA), BF16), pltpu.VMEM((NQ, Q_LORA), BF16), pltpu.VMEM((NR, Q_LORA), BF16)]
    return pl.pallas_call(
        functools.partial(_moe_combine_kernel, final=final_g is not None, norm_next=next_norm is not None,
                          shared_kv=shared_kv is not None, queries=queries is not None),
        grid=(nblk,),
        in_specs=in_specs,
        out_specs=out_specs,
        out_shape=out_shape,
        scratch_shapes=scratch,
        compiler_params=_cparams(1),
        name="moe_combine",
    )(*args)


def _kv_expand_kernel(lat_ref, kr_ref, wk_ref, ek_ref, wvt_ref, ones_ref, k_ref, vt_ref):
    lat = lat_ref[0].astype(BF16)
    kr = kr_ref[0].astype(BF16)
    k = _dot(lat, wk_ref[...].astype(BF16)) + _dot(kr, ek_ref[...].astype(BF16))
    k_ref[0] = k.astype(k_ref.dtype)
    vt = _dot_nt(wvt_ref[...].astype(BF16), lat) + ones_ref[...]
    vt_ref[0] = vt.astype(vt_ref.dtype)


def kv_expand(lat, kr, wk_pad, ek, wvt_ext, ones_col, rows=512):
    B, T, _ = lat.shape
    tt = rows
    NK, NVT = wk_pad.shape[1], wvt_ext.shape[0]

    def full(a):
        return pl.BlockSpec(a.shape, lambda b, t: (0, 0))

    def rowspec(n):
        return pl.BlockSpec((1, tt, n), lambda b, t: (b, t, 0))

    return pl.pallas_call(
        _kv_expand_kernel,
        grid=(B, T // tt),
        in_specs=[rowspec(KV_LORA), rowspec(MLA_ROPE), full(wk_pad), full(ek), full(wvt_ext), full(ones_col)],
        out_specs=[rowspec(NK), pl.BlockSpec((1, NVT, tt), lambda b, t: (b, 0, t))],
        out_shape=[jax.ShapeDtypeStruct((B, T, NK), BF16), jax.ShapeDtypeStruct((B, NVT, T), BF16)],
        compiler_params=_cparams(2),
        name="kv_expand",
    )(lat, kr, wk_pad, ek, wvt_ext, ones_col)


def _query_kernel(h_ref, wdq_ref, qg_ref, wq_ref, wqr_ref, c_ref, s_ref, q_ref, wdq_b, wq_b, wqr_b):
    @pl.when(pl.program_id(0) == 0)
    def _():
        wdq_b[...] = wdq_ref[...].astype(BF16)
        wq_b[...] = wq_ref[...].astype(BF16)
        wqr_b[...] = wqr_ref[...].astype(BF16)

    bb, tt, D = h_ref.shape
    h = h_ref[...].reshape(bb * tt, D)
    cq = _rms(_dot(h, wdq_b[...]), qg_ref[...]).astype(BF16)
    q1 = _dot(cq, wq_b[...]).reshape(bb, tt, -1)
    q2 = _dot(cq, wqr_b[...]).reshape(bb, tt, -1)
    c = c_ref[...]
    s = s_ref[...]
    for hd in range(MLA_HEADS):
        sl = slice(hd * HEAD_PAD, (hd + 1) * HEAD_PAD)
        q_ref[:, :, sl] = (q1[:, :, sl] * c + q2[:, :, sl] * s).astype(q_ref.dtype)


def mla_queries(h, w_dq, q_norm_g, wq_pad, wq_rot, l, c128, s128, rows=512):
    B, T, D = h.shape
    bb, tt, nblk, ij = _row_blocks(B, T, rows)
    NQ = wq_pad.shape[-1]
    tspec = pl.BlockSpec((tt, HEAD_PAD), lambda i: (ij(i)[1], 0))
    return pl.pallas_call(
        _query_kernel,
        grid=(nblk,),
        in_specs=[pl.BlockSpec((bb, tt, D), lambda i: ij(i) + (0,)),
                  pl.BlockSpec((None, D, Q_LORA), lambda i: (l, 0, 0)),
                  pl.BlockSpec((None, 1, Q_LORA), lambda i: (l, 0, 0)),
                  pl.BlockSpec((None, Q_LORA, NQ), lambda i: (l, 0, 0)),
                  pl.BlockSpec((None, Q_LORA, NQ), lambda i: (l, 0, 0)),
                  tspec, tspec],
        out_specs=pl.BlockSpec((bb, tt, NQ), lambda i: ij(i) + (0,)),
        out_shape=jax.ShapeDtypeStruct((B, T, NQ), BF16),
        scratch_shapes=[pltpu.VMEM((D, Q_LORA), BF16), pltpu.VMEM((Q_LORA, NQ), BF16),
                        pltpu.VMEM((Q_LORA, NQ), BF16)],
        compiler_params=_cparams(1),
        name="mla_queries",
    )(h, w_dq, q_norm_g.reshape(-1, 1, Q_LORA), wq_pad, wq_rot, c128, s128)


def _queries_t(h, wdq_b, qg, wqt_b, wqrt_b, cos, sin, qt_ref):
    cq = _rms(_dot(h, wdq_b[...]), qg).astype(BF16)
    q1 = _dot_nt(wqt_b[...], cq)
    q2 = _dot_nt(wqrt_b[...], cq)
    pad = jnp.zeros((HEAD_PAD - MLA_NOPE - MLA_ROPE, q1.shape[1]), qt_ref.dtype)
    for hd in range(MLA_HEADS):
        r0 = hd * HEAD_PAD
        rope = (q1[r0 + MLA_NOPE:r0 + MLA_NOPE + MLA_ROPE] * cos
                + q2[hd * MLA_ROPE:(hd + 1) * MLA_ROPE] * sin)
        qt_ref[0, r0:r0 + MLA_NOPE, :] = (q1[r0:r0 + MLA_NOPE] * Q_PRESCALE).astype(qt_ref.dtype)
        qt_ref[0, r0 + MLA_NOPE:r0 + MLA_NOPE + MLA_ROPE, :] = rope.astype(qt_ref.dtype)
        qt_ref[0, r0 + MLA_NOPE + MLA_ROPE:r0 + HEAD_PAD, :] = pad


def _query_t_kernel(h_ref, wdq_ref, qg_ref, wqt_ref, wqrt_ref, cos_ref, sin_ref, qt_ref, wdq_b, wqt_b, wqrt_b):
    @pl.when((pl.program_id(0) == 0) & (pl.program_id(1) == 0))
    def _():
        wdq_b[...] = wdq_ref[...].astype(BF16)
        wqt_b[...] = wqt_ref[...].astype(BF16)
        wqrt_b[...] = wqrt_ref[...].astype(BF16)

    _queries_t(h_ref[0], wdq_b, qg_ref[...], wqt_b, wqrt_b, cos_ref[...], sin_ref[...], qt_ref)


def mla_queries_t(h, w_dq, q_norm_g, wq_t, wqr_t, l, cos_t, sin_t, rows=512):
    B, T, D = h.shape
    tt = rows
    NQ = wq_t.shape[1]
    NR = wqr_t.shape[1]
    tspec = pl.BlockSpec((MLA_ROPE, tt), lambda b, t: (0, t))
    return pl.pallas_call(
        _query_t_kernel,
        grid=(B, T // tt),
        in_specs=[pl.BlockSpec((1, tt, D), lambda b, t: (b, t, 0)),
                  pl.BlockSpec((None, D, Q_LORA), lambda b, t: (l, 0, 0)),
                  pl.BlockSpec((None, 1, Q_LORA), lambda b, t: (l, 0, 0)),
                  pl.BlockSpec((None, NQ, Q_LORA), lambda b, t: (l, 0, 0)),
                  pl.BlockSpec((None, NR, Q_LORA), lambda b, t: (l, 0, 0)),
                  tspec, tspec],
        out_specs=pl.BlockSpec((1, NQ, tt), lambda b, t: (b, 0, t)),
        out_shape=jax.ShapeDtypeStruct((B, NQ, T), BF16),
        scratch_shapes=[pltpu.VMEM((D, Q_LORA), BF16), pltpu.VMEM((NQ, Q_LORA), BF16),
                        pltpu.VMEM((NR, Q_LORA), BF16)],
        compiler_params=_cparams(2),
        name="mla_queries_t",
    )(h, w_dq, q_norm_g.reshape(-1, 1, Q_LORA), wq_t, wqr_t, cos_t, sin_t)


def _attn_prompt_kernel(qi_tab, ki_tab, qt_ref, k_ref, vt_ref, o_ref, *scratch, tq, tk):
    H = MLA_HEADS
    m_refs, l_refs, acc_refs = scratch[:H], scratch[H:2 * H], scratch[2 * H:]
    p_id = pl.program_id(1)
    qi = qi_tab[p_id]
    ki = ki_tab[p_id]

    @pl.when(ki == 0)
    def _():
        for hd in range(H):
            m_refs[hd][...] = jnp.full(m_refs[hd].shape, NEG_INF, F32)
            l_refs[hd][...] = jnp.zeros(l_refs[hd].shape, F32)
            acc_refs[hd][...] = jnp.zeros(acc_refs[hd].shape, F32)

    def block(masked):
        if masked:
            kchunk = (ki * tk + lax.broadcasted_iota(jnp.int32, (tk, tq), 0)) // CHUNK
            qchunk = (qi * tq + lax.broadcasted_iota(jnp.int32, (tk, tq), 1)) // CHUNK
            mask = kchunk <= qchunk
        def scores(hd):
            sl = slice(hd * HEAD_PAD, (hd + 1) * HEAD_PAD)
            return _dot(k_ref[0, :, sl], qt_ref[0, sl, :])

        pending = [scores(hd) for hd in range(ATTN_LOOKAHEAD)]
        for hd in range(H):
            if hd + ATTN_LOOKAHEAD < H:
                pending.append(scores(hd + ATTN_LOOKAHEAD))
            s = pending.pop(0)
            if masked:
                s = jnp.where(mask, s, NEG_INF)
            m_prev = m_refs[hd][...]
            m_new = jnp.maximum(m_prev, jnp.max(s, axis=0, keepdims=True))
            a = jnp.exp2(m_prev - m_new)
            p = jnp.exp2(s - m_new).astype(BF16)
            pv = _dot(vt_ref[0, hd * V_ROWS:(hd + 1) * V_ROWS, :], p)
            acc_refs[hd][...] = a * acc_refs[hd][...] + pv[:MLA_V]
            l_refs[hd][...] = a * l_refs[hd][...] + pv[MLA_V:MLA_V + 1]
            m_refs[hd][...] = m_new

    @pl.when(ki < qi)
    def _():
        block(False)

    @pl.when(ki == qi)
    def _():
        block(True)
        o_t = jnp.concatenate([acc_refs[hd][...] / l_refs[hd][...] for hd in range(H)], axis=0)
        o_ref[0] = o_t.T.astype(o_ref.dtype)


def attn_prompt(qt, k, vt, tq=256):
    B, NQ, T = qt.shape
    NVT = vt.shape[1]
    NV = MLA_HEADS * MLA_V
    tk = tq
    assert tq % CHUNK == 0
    nq = T // tq
    pairs = [(a, b) for a in range(nq) for b in range(a + 1)]
    qi_tab = jnp.asarray([a for a, _ in pairs], jnp.int32)
    ki_tab = jnp.asarray([b for _, b in pairs], jnp.int32)
    grid_spec = pltpu.PrefetchScalarGridSpec(
        num_scalar_prefetch=2,
        grid=(B, len(pairs)),
        in_specs=[pl.BlockSpec((1, NQ, tq), lambda b, p, qt, kt: (b, 0, qt[p])),
                  pl.BlockSpec((1, tk, NQ), lambda b, p, qt, kt: (b, kt[p], 0)),
                  pl.BlockSpec((1, NVT, tk), lambda b, p, qt, kt: (b, 0, kt[p]))],
        out_specs=pl.BlockSpec((1, tq, NV), lambda b, p, qt, kt: (b, qt[p], 0)),
        scratch_shapes=([pltpu.VMEM((1, tq), F32)] * (2 * MLA_HEADS)
                        + [pltpu.VMEM((MLA_V, tq), F32)] * MLA_HEADS),
    )
    return pl.pallas_call(
        functools.partial(_attn_prompt_kernel, tq=tq, tk=tk),
        grid_spec=grid_spec,
        out_shape=jax.ShapeDtypeStruct((B, T, NV), BF16),
        compiler_params=_cparams(2),
        name="attn_prompt",
    )(qi_tab, ki_tab, qt, k, vt)


def _absorb_kernel(q_ref, m_ref, o_ref):
    o_ref[...] = _dot(q_ref[...], m_ref[...].astype(BF16)).astype(o_ref.dtype)


def absorb_queries(q2d, m_abs):
    N = q2d.shape[0]
    H, _, W = m_abs.shape
    return pl.pallas_call(
        _absorb_kernel,
        grid=(H,),
        in_specs=[pl.BlockSpec((N, HEAD_PAD), lambda h: (0, h)),
                  pl.BlockSpec((None, HEAD_PAD, W), lambda h: (h, 0, 0))],
        out_specs=pl.BlockSpec((None, N, W), lambda h: (h, 0, 0)),
        out_shape=jax.ShapeDtypeStruct((H, N, W), BF16),
        compiler_params=_cparams(1),
        name="absorb_queries",
    )(q2d, m_abs)


def _attn_sample_kernel(q_ref, lat_ref, kr_ref, nlat_ref, nkr_ref, o_ref, m_ref, l_ref, acc_ref):
    kb = pl.program_id(1)
    H, Q, W = q_ref.shape
    q = q_ref[...].reshape(H * Q, W)
    q_lat = q[:, :KV_LORA]
    q_rope = q[:, KV_LORA:KV_LORA + MLA_ROPE]

    def update(lat_tile, kr_tile, n_sub, kr_transposed):
        sub = lat_tile.shape[0] // n_sub
        lats = [lat_tile[j * sub:(j + 1) * sub, :].astype(BF16) for j in range(n_sub)]
        if kr_transposed:
            krs = [kr_tile[:, j * sub:(j + 1) * sub].astype(BF16) for j in range(n_sub)]
            ss = [_dot_nt(q_lat, lat) + _dot(q_rope, kr) for lat, kr in zip(lats, krs)]
        else:
            krs = [kr_tile[j * sub:(j + 1) * sub, :].astype(BF16) for j in range(n_sub)]
            ss = [_dot_nt(q_lat, lat) + _dot_nt(q_rope, kr) for lat, kr in zip(lats, krs)]
        m_prev = m_ref[...]
        m_new = m_prev
        for s in ss:
            m_new = jnp.maximum(m_new, jnp.max(s, axis=-1, keepdims=True))
        a = jnp.exp2(m_prev - m_new)
        ps = [jnp.exp2(s - m_new[:, :1]) for s in ss]
        pv = _dot(ps[0].astype(BF16), lats[0])
        psum = jnp.sum(ps[0], axis=-1, keepdims=True)
        for p, lat in zip(ps[1:], lats[1:]):
            pv = pv + _dot(p.astype(BF16), lat)
            psum = psum + jnp.sum(p, axis=-1, keepdims=True)
        l_ref[...] = a * l_ref[...] + psum
        m_ref[...] = m_new
        acc_ref[...] = jnp.concatenate([a, a], axis=-1) * acc_ref[...] + pv

    @pl.when(kb == 0)
    def _():
        m_ref[...] = jnp.full_like(m_ref, NEG_INF)
        l_ref[...] = jnp.zeros_like(l_ref)
        acc_ref[...] = jnp.zeros_like(acc_ref)
        update(nlat_ref[0], nkr_ref[0], 1, False)

    update(lat_ref[0], kr_ref[0], SAMPLE_KEY_SUB, True)

    @pl.when(kb == pl.num_programs(1) - 1)
    def _():
        lsum = l_ref[...]
        o = acc_ref[...] / jnp.concatenate([lsum, lsum], axis=-1)
        o_ref[...] = o.reshape(H, Q, KV_LORA).astype(o_ref.dtype)


def attn_sample(q_abs, cache_lat, cache_kr_t, new_lat, new_kr, tk=4096):
    H, N, W = q_abs.shape
    B, P, _ = cache_lat.shape
    Q = new_lat.shape[1]
    qpos = P + np.arange(Q)
    kpos = np.arange(P + Q)
    assert bool(np.all((kpos // CHUNK)[None, :] <= (qpos // CHUNK)[:, None]))
    return pl.pallas_call(
        _attn_sample_kernel,
        grid=(B, P // tk),
        in_specs=[pl.BlockSpec((H, Q, W), lambda b, kb: (0, b, 0)),
                  pl.BlockSpec((1, tk, KV_LORA), lambda b, kb: (b, kb, 0)),
                  pl.BlockSpec((1, MLA_ROPE, tk), lambda b, kb: (b, 0, kb)),
                  pl.BlockSpec((1, Q, KV_LORA), lambda b, kb: (b, 0, 0)),
                  pl.BlockSpec((1, Q, MLA_ROPE), lambda b, kb: (b, 0, 0))],
        out_specs=pl.BlockSpec((H, Q, KV_LORA), lambda b, kb: (0, b, 0)),
        out_shape=jax.ShapeDtypeStruct((H, N, KV_LORA), BF16),
        scratch_shapes=[pltpu.VMEM((H * Q, 128), F32), pltpu.VMEM((H * Q, 128), F32),
                        pltpu.VMEM((H * Q, KV_LORA), F32)],
        compiler_params=_cparams(2),
        name="attn_sample",
    )(q_abs, cache_lat, cache_kr_t, new_lat, new_kr)


def _unabsorb_kernel(o_ref, w_ref, out_ref):
    out_ref[...] = (_dot(o_ref[0], w_ref[0].astype(BF16))
                    + _dot(o_ref[1], w_ref[1].astype(BF16))).astype(out_ref.dtype)


def unabsorb(o_lat, wuv_pad):
    H, N, R = o_lat.shape
    return pl.pallas_call(
        _unabsorb_kernel,
        grid=(H // 2,),
        in_specs=[pl.BlockSpec((2, N, R), lambda p: (p, 0, 0)),
                  pl.BlockSpec((2, R, 128), lambda p: (p, 0, 0))],
        out_specs=pl.BlockSpec((N, 128), lambda p: (0, p)),
        out_shape=jax.ShapeDtypeStruct((N, (H // 2) * 128), BF16),
        compiler_params=_cparams(1),
        name="unabsorb",
    )(o_lat, wuv_pad)


def _rope_tables(pos):
    half = MLA_ROPE // 2
    inv = 1.0 / (ROPE_THETA ** (np.arange(half, dtype=np.float64) * 2.0 / MLA_ROPE))
    ang = np.asarray(pos, np.float64)[:, None] * inv[None, :]
    cos = np.concatenate([np.cos(ang), np.cos(ang)], axis=-1)
    sin = np.concatenate([np.sin(ang), np.sin(ang)], axis=-1)
    T = cos.shape[0]
    c128 = np.zeros((T, HEAD_PAD)); s128 = np.zeros((T, HEAD_PAD))
    c128[:, :MLA_NOPE] = 1.0
    c128[:, MLA_NOPE:MLA_NOPE + MLA_ROPE] = cos
    s128[:, MLA_NOPE:MLA_NOPE + MLA_ROPE] = sin
    return dict(cos32=jnp.asarray(cos, F32), sin32=jnp.asarray(sin, F32),
                c128=jnp.asarray(c128 * Q_PRESCALE, F32), s128=jnp.asarray(s128 * Q_PRESCALE, F32),
                cos_t=jnp.asarray(cos.T * Q_PRESCALE, F32), sin_t=jnp.asarray(sin.T * Q_PRESCALE, F32))


def _rot_half_cols(w):
    half = w.shape[-1] // 2
    return jnp.concatenate([-w[..., half:], w[..., :half]], axis=-1)


def _prep_weights(w_dkv, w_uk, w_uv, w_uq, router_w):
    D = D_MODEL
    w_lat, w_rope = w_dkv[:, :KV_LORA], w_dkv[:, KV_LORA:]
    pad96 = jnp.zeros((D, 128 - MLA_ROPE), F32)
    w_kv = jnp.concatenate([w_lat, w_rope, pad96, _rot_half_cols(w_rope), pad96], axis=-1)

    zpad = HEAD_PAD - MLA_NOPE
    wk_pad = jnp.pad(w_uk, ((0, 0), (0, 0), (0, zpad))).reshape(KV_LORA, MLA_HEADS * HEAD_PAD)
    ek = jnp.zeros((MLA_ROPE, MLA_HEADS, HEAD_PAD), F32)
    ek = ek.at[:, :, MLA_NOPE:MLA_NOPE + MLA_ROPE].set(
        jnp.broadcast_to(jnp.eye(MLA_ROPE, dtype=F32)[:, None, :], (MLA_ROPE, MLA_HEADS, MLA_ROPE)))
    ek = ek.reshape(MLA_ROPE, MLA_HEADS * HEAD_PAD)
    wvt = jnp.transpose(w_uv, (1, 2, 0))
    wvt_ext = jnp.pad(wvt, ((0, 0), (0, V_ROWS - MLA_V), (0, 0))).reshape(MLA_HEADS * V_ROWS, KV_LORA)
    ones_col = jnp.tile((jnp.arange(V_ROWS) >= MLA_V).astype(F32), MLA_HEADS).reshape(-1, 1)

    nb = w_uq.shape[0]
    qn, qr = w_uq[..., :MLA_NOPE], w_uq[..., MLA_NOPE:]
    z32 = jnp.zeros(qr.shape[:-1] + (HEAD_PAD - MLA_NOPE - MLA_ROPE,), F32)
    wq_pad = jnp.concatenate([qn, qr, z32], axis=-1).reshape(nb, Q_LORA, MLA_HEADS * HEAD_PAD)
    wq_rot = jnp.concatenate([jnp.zeros_like(qn), _rot_half_cols(qr), z32], axis=-1)
    wq_rot = wq_rot.reshape(nb, Q_LORA, MLA_HEADS * HEAD_PAD)
    wq_t = jnp.transpose(wq_pad, (0, 2, 1))
    wqr_t = jnp.transpose(_rot_half_cols(qr).reshape(nb, Q_LORA, MLA_HEADS * MLA_ROPE), (0, 2, 1))

    m_abs = jnp.zeros((MLA_HEADS, HEAD_PAD, KV_LORA + 128), F32)
    m_abs = m_abs.at[:, :MLA_NOPE, :KV_LORA].set(jnp.transpose(w_uk, (1, 2, 0)))
    m_abs = m_abs.at[:, MLA_NOPE:MLA_NOPE + MLA_ROPE, KV_LORA:KV_LORA + MLA_ROPE].set(
        jnp.broadcast_to(jnp.eye(MLA_ROPE, dtype=F32), (MLA_HEADS, MLA_ROPE, MLA_ROPE)))

    wuv_h = jnp.transpose(w_uv, (1, 0, 2))
    even = jnp.pad(wuv_h, ((0, 0), (0, 0), (0, 64)))
    odd = jnp.pad(wuv_h, ((0, 0), (0, 0), (64, 0)))
    wuv_pad = jnp.where((jnp.arange(MLA_HEADS) % 2 == 0)[:, None, None], even, odd)

    rw_t = jnp.transpose(router_w, (0, 2, 1))
    return dict(w_kv=w_kv, wk_pad=wk_pad, ek=ek, wvt_ext=wvt_ext, ones_col=ones_col, wq_pad=wq_pad, wq_rot=wq_rot, wq_t=wq_t, wqr_t=wqr_t,
                m_abs=m_abs, wuv_pad=wuv_pad, rw_t=rw_t)


def _mixer(st, l, P, W, packed):
    rows_kw = dict(rows_total=packed["total"], row0=packed["row0"], rows_buf=packed["buf"])
    x, m = st["x"], st["mod"][l]
    B, T, _ = x.shape
    n_a = P["hg_w_in"].shape[0]
    norm2 = (P["norm2_g"][l], 4, 3)
    if l < n_a:
        zf, zqig = hgrn_proj(x, P["norm1_g"][l], m, 1, 0, P["hg_w_in"], l)
        s0 = None if st["hg_state"] is None else st["hg_state"][l]
        o, s_new = gla(zqig, zf, st["lbs"][l], P["hg_onorm_g"][l], s0)
        st["hg_new"].append(s_new)
        st["x"], packed["buf"] = linear(o, P["hg_w_out"], l, F32, x=x, mod=m, gate_idx=2, next_norm=norm2,
                                        **rows_kw)
    else:
        bi = l - n_a
        h = st.pop("h_next", None)
        qt = st.pop("qt_next", None)
        if h is None and qt is None:
            h = norm_mod(x, P["norm1_g"][l], m, sc_idx=1, sh_idx=0)
        if st["past_lat"] is None:
            if qt is None:
                qt = mla_queries_t(h, P["w_dq"], P["q_norm_g"], W["wq_t"], W["wqr_t"], bi, st["cos_t"],
                                   st["sin_t"])
            o = attn_prompt(qt, st["k_all"], st["v_all"])
        else:
            q = mla_queries(h, P["w_dq"], P["q_norm_g"], W["wq_pad"], W["wq_rot"], bi, st["c128"], st["s128"])
            q_abs = absorb_queries(q.reshape(B * T, -1), W["m_abs"])
            o_lat = attn_sample(q_abs, st["past_lat"], st["past_kr"], st["lat"], st["kr"])
            o = unabsorb(o_lat, W["wuv_pad"]).reshape(B, T, -1)
        st["x"], packed["buf"] = linear(o, P["w_o"], bi, F32, x=x, mod=m, gate_idx=2, next_norm=norm2, **rows_kw)
    packed["row0"] += B * T


def _moe(groups, hp, l, P, W):
    n_tok = hp.shape[0]
    n_tiles = (TOP_K * n_tok) // MOE_TILE + N_EXPERTS
    pos, w8, tile_start, tile_count = route(hp, W["rw_t"], P["router_bias"], l, MOE_TILE)
    pos_flat = pos.reshape(-1)
    src = sc_invert(pos_flat, n_tok, n_tiles * MOE_TILE)
    xs = sc_gather(hp, src)
    out = moe_gemm(xs, tile_start[:, 0], tile_count[:, 0], P["exp_w_in"], P["exp_w_out"], l,
                   MOE_TILE, n_tiles)
    y8 = sc_gather(out, pos_flat).reshape(TOP_K, n_tok, -1)
    last = l == P["norm1_g"].shape[0] - 1
    with_kv = l == P["hg_w_in"].shape[0] - 1
    next_is_mla = not last and l + 1 >= P["hg_w_in"].shape[0]
    row0 = 0
    for st in groups:
        B, T, _ = st["x"].shape
        with_q = next_is_mla and st["past_lat"] is None
        outs = moe_combine(
            y8, w8, hp, P["sh_w_in"], P["sh_w_out"], st["x"], st["mod"][l], 5, l, row0,
            final_g=P["final_g"] if last else None,
            next_norm=(P["norm1_g"][l + 1], st["mod"][l + 1], 1, 0) if next_is_mla else None,
            shared_kv=(P["kv_in_g"], W["w_kv"], P["kv_lat_g"], st["cos32"], st["sin32"]) if with_kv else None,
            queries=(P["w_dq"], P["q_norm_g"], W["wq_t"], W["wqr_t"], l + 1 - P["hg_w_in"].shape[0],
                     st["cos_t"], st["sin_t"]) if with_q else None,
            rows=256 if with_q else 512)
        outs = list(outs) if isinstance(outs, (list, tuple)) else [outs]
        st["x"] = outs.pop(0)
        if next_is_mla and not with_q:
            st["h_next"] = outs.pop(0)
        if with_kv:
            st["lat"], st["kr"] = outs.pop(0), outs.pop(0)
        if with_q:
            st["qt_next"] = outs.pop(0)
        row0 += B * T


def _group_state(x, mod, pos, hg_state, past_lat, past_kr, lbs):
    return dict(x=x, mod=mod, hg_state=hg_state, past_lat=past_lat, past_kr=past_kr, lbs=lbs,
                **_rope_tables(pos), hg_new=[],
                lat=None, kr=None, k_all=None, v_all=None)


def kernel(x_prompt, x_sample, state_hgrn, cache_mla_latent, cache_mla_krope, c_prompt, c_sample, ada_w, ada_b, norm1_g, norm2_g, hg_w_in, hg_lb_logits, hg_onorm_g, hg_w_out, kv_in_g, w_dkv, kv_lat_g, w_uk, w_uv, w_dq, q_norm_g, w_uq, w_o, router_w, router_bias, exp_w_in, exp_w_out, sh_w_in, sh_w_out, final_g):
    Bp, Sp, _ = x_prompt.shape
    Bs, Ss, _ = x_sample.shape
    past = cache_mla_latent.shape[1]
    P = dict(norm1_g=norm1_g, norm2_g=norm2_g, hg_w_in=hg_w_in, hg_lb_logits=hg_lb_logits,
             hg_onorm_g=hg_onorm_g, hg_w_out=hg_w_out, kv_in_g=kv_in_g, kv_lat_g=kv_lat_g,
             w_dq=w_dq, q_norm_g=q_norm_g, w_o=w_o, router_bias=router_bias,
             exp_w_in=exp_w_in, exp_w_out=exp_w_out, sh_w_in=sh_w_in, sh_w_out=sh_w_out, final_g=final_g)
    W = _prep_weights(w_dkv, w_uk, w_uv, w_uq, router_w)
    mod = ada_mod(jnp.concatenate([c_prompt, c_sample], axis=0), ada_w, ada_b)
    lbs = jnp.cumsum(jax.nn.softmax(hg_lb_logits.astype(F32), axis=0), axis=0)
    gp = _group_state(x_prompt, mod[:, :Bp, None, :], np.arange(Sp), None, None, None, lbs)
    gs = _group_state(x_sample, mod[:, Bp:, None, :], past + np.arange(Ss), state_hgrn,
                      cache_mla_latent, jnp.transpose(cache_mla_krope, (0, 2, 1)), lbs)
    groups = [gp, gs]
    n_tok = sum(st["x"].shape[0] * st["x"].shape[1] for st in groups)
    n_a = hg_w_in.shape[0]
    for l in range(norm1_g.shape[0]):
        packed = dict(total=n_tok, row0=0, buf=None)
        for st in groups:
            _mixer(st, l, P, W, packed)
        _moe(groups, packed["buf"], l, P, W)
        if l == n_a - 1:
            gp["k_all"], gp["v_all"] = kv_expand(gp["lat"], gp["kr"], W["wk_pad"], W["ek"], W["wvt_ext"],
                                                 W["ones_col"])
    return (gp["x"], gs["x"], jnp.stack(gp["hg_new"], axis=0), jnp.stack(gs["hg_new"], axis=0),
            gp["lat"], gp["kr"], gs["lat"], gs["kr"])
```

```python
import dataclasses
import functools

import numpy as np
import jax
import jax.numpy as jnp
from jax import lax
from jax.experimental import pallas as pl
from jax.experimental.pallas import tpu as pltpu
from jax.experimental.pallas import tpu_sc as plsc

F32 = jnp.float32
BF16 = jnp.bfloat16

D_MODEL = 1024
CHUNK = 64
HG_HEADS = 8
HG_DK = 128
HG_DV = 128
MLA_HEADS = 16
MLA_NOPE = 64
MLA_ROPE = 32
MLA_V = 64
Q_LORA = 384
KV_LORA = 256
ROPE_THETA = 10000.0
N_EXPERTS = 64
TOP_K = 8
N_GROUPS = 8
TOPK_GROUPS = 4
EXPERT_FF = 256
SHARED_FF = 256
ROUTED_SCALE = 2.5
EPS = 1e-6

HEAD_PAD = 128
SAMPLE_KEY_SUB = 8
ATTN_LOOKAHEAD = 6
V_ROWS = MLA_V + 16
QK_SCALE = (MLA_NOPE + MLA_ROPE) ** -0.5
Q_PRESCALE = QK_SCALE * float(np.log2(np.e))
VMEM_LIMIT = 56 * 1024 * 1024
NEG_INF = float("-inf")
SC_CORES = 2
SC_SUBCORES = 16
SC_WORKERS = SC_CORES * SC_SUBCORES
SC_LANES = 16
SC_WINDOW = 64
MOE_TILE = 512
MOE_NBUF = 4
MOE_SUB = 1


def _cparams(n_axes):
    return pltpu.CompilerParams(dimension_semantics=("arbitrary",) * n_axes,
                                vmem_limit_bytes=VMEM_LIMIT)


def _silu(x):
    return x * jax.nn.sigmoid(x)


def _rms(x, g):
    ms = jnp.mean(x * x, axis=-1, keepdims=True)
    return x * lax.rsqrt(ms + EPS) * g


def _dot(a, b):
    return jnp.dot(a, b, preferred_element_type=F32)


def _dot_nt(a, b):
    return lax.dot_general(a, b, (((1,), (1,)), ((), ())), preferred_element_type=F32)


def _dot_tn(a, b):
    return lax.dot_general(a, b, (((0,), (0,)), ((), ())), preferred_element_type=F32)


def _row_blocks(B, T, rows):
    if T >= rows:
        assert T % rows == 0
        bb, tt = 1, rows
    else:
        assert rows % T == 0 and B % (rows // T) == 0
        bb, tt = rows // T, T
    nt = T // tt
    return bb, tt, (B // bb) * nt, (lambda i: (i // nt, i % nt))


def _ada_kernel(c_ref, w_ref, b_ref, o_ref):
    a = _silu(c_ref[...]).astype(BF16)
    o_ref[...] = _dot(a, w_ref[...].astype(BF16)) + b_ref[...]


def ada_mod(c, ada_w, ada_b):
    R, D = c.shape
    L, _, N = ada_w.shape
    tn = 1536
    return pl.pallas_call(
        _ada_kernel,
        grid=(L, N // tn),
        in_specs=[pl.BlockSpec((R, D), lambda l, j: (0, 0)),
                  pl.BlockSpec((None, D, tn), lambda l, j: (l, 0, j)),
                  pl.BlockSpec((None, 1, tn), lambda l, j: (l, 0, j))],
        out_specs=pl.BlockSpec((None, R, tn), lambda l, j: (l, 0, j)),
        out_shape=jax.ShapeDtypeStruct((L, R, N), F32),
        compiler_params=_cparams(2),
        name="ada_mod",
    )(c, ada_w, ada_b.reshape(L, 1, N))


def _pack_pairs(y):
    half = y.shape[-1] // 2
    bits = lax.bitcast_convert_type(y.astype(BF16).astype(F32), jnp.uint32)
    word = lax.shift_right_logical(bits[:, :half], jnp.uint32(16)) | bits[:, half:]
    return lax.bitcast_convert_type(word, jnp.int32)


def _unpack_pairs(word, dtype=BF16):
    u = lax.bitcast_convert_type(word, jnp.uint32)
    lo = lax.bitcast_convert_type(lax.shift_left(u, jnp.uint32(16)), F32)
    hi = lax.bitcast_convert_type(u & jnp.uint32(0xFFFF0000), F32)
    return lo.astype(dtype), hi.astype(dtype)


def _norm_kernel(x_ref, g_ref, sc_ref, sh_ref, o_ref):
    y = _rms(x_ref[...], g_ref[...]) * (1.0 + sc_ref[...]) + sh_ref[...]
    o_ref[...] = y.astype(o_ref.dtype)


def norm_mod(x, g, mod, sc_idx, sh_idx, rows=512):
    B, T, D = x.shape
    bb, tt, nblk, ij = _row_blocks(B, T, rows)
    xspec = pl.BlockSpec((bb, tt, D), lambda i: ij(i) + (0,))
    return pl.pallas_call(
        _norm_kernel,
        grid=(nblk,),
        in_specs=[xspec, pl.BlockSpec((1, D), lambda i: (0, 0)),
                  pl.BlockSpec((bb, 1, D), lambda i: (ij(i)[0], 0, sc_idx)),
                  pl.BlockSpec((bb, 1, D), lambda i: (ij(i)[0], 0, sh_idx))],
        out_specs=xspec,
        out_shape=jax.ShapeDtypeStruct((B, T, D), BF16),
        compiler_params=_cparams(1),
        name="norm_mod",
    )(x, g.reshape(1, D), mod, mod)


def _linear_kernel(*refs, residual, norm_next, shared_rows, n_main):
    if norm_next and shared_rows:
        a_ref, w_ref, x_ref, gate_ref, ng_ref, nsc_ref, nsh_ref, _, o_ref, hp_ref, wb_ref = refs
    elif norm_next:
        a_ref, w_ref, x_ref, gate_ref, ng_ref, nsc_ref, nsh_ref, o_ref, hp_ref, wb_ref = refs
    elif residual:
        a_ref, w_ref, x_ref, gate_ref, o_ref, wb_ref = refs
    else:
        a_ref, w_ref, o_ref, wb_ref = refs

    @pl.when(pl.program_id(1) == 0)
    def _():
        wb_ref[...] = w_ref[...].astype(BF16)

    def main():
        bb, tt, K = a_ref.shape
        y = _dot(a_ref[...].reshape(bb * tt, K).astype(BF16), wb_ref[...])
        y = y.reshape(bb, tt, y.shape[-1])
        if residual:
            y = x_ref[...] + gate_ref[...] * y
        o_ref[...] = y.astype(o_ref.dtype)
        if norm_next:
            h = _rms(y, ng_ref[...]) * (1.0 + nsc_ref[...]) + nsh_ref[...]
            hp_ref[...] = _pack_pairs(h.reshape(bb * tt, h.shape[-1]))

    if n_main is None:
        main()
    else:
        pl.when(pl.program_id(1) < n_main)(main)

        @pl.when(pl.program_id(1) >= n_main)
        def _():
            hp_ref[...] = jnp.zeros(hp_ref.shape, hp_ref.dtype)


def linear(a, w, l, out_dtype, x=None, mod=None, gate_idx=0, rows=512, tn=1024, next_norm=None,
           rows_total=None, row0=0, rows_buf=None):
    B, T, K = a.shape
    _, _, N = w.shape
    tn = min(tn, N)
    bb, tt, nblk, ij0 = _row_blocks(B, T, rows)
    n_extra = 0
    if next_norm is not None and rows_buf is None and rows_total is not None:
        assert row0 == 0 and (rows_total - B * T) % (bb * tt) == 0
        n_extra = (rows_total - B * T) // (bb * tt)

    def ij(i):
        return ij0(jnp.minimum(i, nblk - 1)) if n_extra else ij0(i)

    in_specs = [pl.BlockSpec((bb, tt, K), lambda j, i: ij(i) + (0,)),
                pl.BlockSpec((None, K, tn), lambda j, i: (l, 0, j))]
    args = [a, w]
    ospec = pl.BlockSpec((bb, tt, tn), lambda j, i: ij(i) + (j,))
    out_specs = ospec
    out_shape = jax.ShapeDtypeStruct((B, T, N), out_dtype)
    aliases = {}
    if x is not None:
        gsteps = D_MODEL // tn
        in_specs += [ospec, pl.BlockSpec((bb, 1, tn), lambda j, i: (ij(i)[0], 0, gate_idx * gsteps + j))]
        args += [x, mod]
    if next_norm is not None:
        assert x is not None and tn == N
        gain, sc_idx, sh_idx = next_norm
        in_specs += [pl.BlockSpec((1, N), lambda j, i: (0, 0)),
                     pl.BlockSpec((bb, 1, N), lambda j, i: (ij(i)[0], 0, sc_idx)),
                     pl.BlockSpec((bb, 1, N), lambda j, i: (ij(i)[0], 0, sh_idx))]
        args += [gain.reshape(1, N), mod, mod]
        assert row0 % (bb * tt) == 0
        off = row0 // (bb * tt)
        out_specs = [ospec, pl.BlockSpec((bb * tt, N // 2), lambda j, i: (off + i, 0))]
        out_shape = [out_shape, jax.ShapeDtypeStruct((rows_total or B * T, N // 2), jnp.int32)]
        if rows_buf is not None:
            in_specs.append(pl.BlockSpec(memory_space=pl.ANY))
            args.append(rows_buf)
            aliases = {len(args) - 1: 1}
    return pl.pallas_call(
        functools.partial(_linear_kernel, residual=x is not None, norm_next=next_norm is not None,
                          shared_rows=rows_buf is not None, n_main=nblk if n_extra else None),
        grid=(N // tn, nblk + n_extra),
        in_specs=in_specs,
        out_specs=out_specs,
        out_shape=out_shape,
        scratch_shapes=[pltpu.VMEM((K, tn), BF16)],
        input_output_aliases=aliases,
        compiler_params=_cparams(2),
        name="linear",
    )(*args)


def _hgrn_proj_kernel(x_ref, g_ref, sc_ref, sh_ref, w_ref, zf_ref, zqig_ref, h_b, w_b):
    i = pl.program_id(0)
    j = pl.program_id(1)
    bb, tt, D = x_ref.shape

    @pl.when(i == 0)
    def _():
        w_b[j] = w_ref[...].astype(BF16)

    @pl.when(j == 0)
    def _():
        h = _rms(x_ref[...], g_ref[...]) * (1.0 + sc_ref[...]) + sh_ref[...]
        h_b[...] = h.reshape(bb * tt, D).astype(BF16)

    y = _dot(h_b[...], w_b[j]).reshape(bb, tt, -1)

    @pl.when(j == 1)
    def _():
        zf_ref[...] = y

    @pl.when(j != 1)
    def _():
        zqig_ref[...] = y.astype(zqig_ref.dtype)


def hgrn_proj(x, g, mod, sc_idx, sh_idx, w_in, l, rows=1024):
    B, T, D = x.shape
    bb, tt, nblk, ij = _row_blocks(B, T, min(rows, B * T))
    xspec = pl.BlockSpec((bb, tt, D), lambda i, j: ij(i) + (0,))
    return pl.pallas_call(
        _hgrn_proj_kernel,
        grid=(nblk, 4),
        in_specs=[xspec,
                  pl.BlockSpec((1, D), lambda i, j: (0, 0)),
                  pl.BlockSpec((bb, 1, D), lambda i, j: (ij(i)[0], 0, sc_idx)),
                  pl.BlockSpec((bb, 1, D), lambda i, j: (ij(i)[0], 0, sh_idx)),
                  pl.BlockSpec((None, D, D), lambda i, j: (l, 0, jnp.where(i == 0, j, 3)))],
        out_specs=[xspec,
                   pl.BlockSpec((bb, tt, D), lambda i, j: ij(i) + (j - (j >= 1),))],
        out_shape=[jax.ShapeDtypeStruct((B, T, D), F32), jax.ShapeDtypeStruct((B, T, 3 * D), BF16)],
        scratch_shapes=[pltpu.VMEM((bb * tt, D), BF16), pltpu.VMEM((4, D, D), BF16)],
        compiler_params=_cparams(2),
        name="hgrn_proj",
    )(x, g.reshape(1, D), mod, mod, w_in)


def _gla_kernel(*refs, L, n_chunks, has_init):
    if has_init:
        q_ref, f_ref, i_ref, g_ref, lb_ref, on_ref, s0_ref, o_ref, so_ref, st_ref = refs
    else:
        q_ref, f_ref, i_ref, g_ref, lb_ref, on_ref, o_ref, so_ref, st_ref = refs
    t = pl.program_id(1)
    H = st_ref.shape[0]

    @pl.when(t == 0)
    def _():
        for h in range(H):
            if has_init:
                st_ref[h] = s0_ref[0, h].T
            else:
                st_ref[h] = jnp.zeros(st_ref.shape[1:], F32)

    lb = lb_ref[...]
    onorm = on_ref[...]
    row = lax.broadcasted_iota(jnp.int32, (L, L), 0)
    col = lax.broadcasted_iota(jnp.int32, (L, L), 1)
    causal = col <= row
    tri = causal.astype(BF16)

    def chunk(c, carry):
        rows = pl.ds(pl.multiple_of(c * L, L), L)

        def write_o(sl, o):
            o_ref[0, rows, sl] = o.astype(o_ref.dtype)

        _gla_chunk(q_ref[0, rows, :], f_ref[0, rows, :], i_ref[0, rows, :], g_ref[0, rows, :],
                   lb, onorm, tri, causal, st_ref, write_o)
        return carry

    lax.fori_loop(0, n_chunks, chunk, 0, unroll=4 if n_chunks % 4 == 0 else 1)

    @pl.when(t == pl.num_programs(1) - 1)
    def _():
        for h in range(H):
            so_ref[0, h] = st_ref[h].T


def _gla_chunk(q, f, v, g, lb, onorm, tri, causal, st_ref, write_o):
    L = q.shape[0]
    H = st_ref.shape[0]
    mid = L // 2 - 1
    q = _silu(q.astype(F32))
    fg = lb + (1.0 - lb) * jax.nn.sigmoid(f)
    k = 1.0 - fg
    v = v.astype(BF16)
    gate = _silu(g.astype(F32))
    logf = jnp.log(fg)
    hi = logf.astype(BF16)
    lo = (logf - hi.astype(F32)).astype(BF16)
    b = _dot(tri, hi) + _dot(tri, lo)
    b_mid = b[mid:mid + 1, :]
    b_last = b[L - 1:L, :]
    qa = q * jnp.exp(b - b_mid)
    kb = k * jnp.exp(b_mid - b)
    qe = (qa * jnp.exp(b_mid)).astype(BF16)
    kd = (kb * jnp.exp(b_last - b_mid)).astype(BF16)
    qa = qa.astype(BF16)
    kb = kb.astype(BF16)
    decay = jnp.exp(b_last)
    sls = [slice(h * HG_DK, (h + 1) * HG_DK) for h in range(H)]
    sts = [st_ref[h] for h in range(H)]
    scores = [_dot_nt(qa[:, sl], kb[:, sl]) for sl in sls]
    inter = [_dot_nt(qe[:, sl], st.astype(BF16)) for sl, st in zip(sls, sts)]
    outer = [_dot_tn(v[:, sl], kd[:, sl]) for sl in sls]
    intra = [_dot(jnp.where(causal, sc, 0.0).astype(BF16), v[:, sl]) for sc, sl in zip(scores, sls)]
    for h, sl in enumerate(sls):
        st_ref[h] = sts[h] * decay[:, sl] + outer[h]
        write_o(sl, _rms(inter[h] + intra[h], onorm[:, sl]) * gate[:, sl])


def gla(zqig, zf, lb, onorm_g, s0):
    B, T, D = zf.shape
    L = CHUNK if T % CHUNK == 0 else T
    tt = min(T, 512)
    n_chunks = tt // L
    H = HG_HEADS

    def zspec(part):
        return pl.BlockSpec((1, tt, D), lambda b, t: (b, t, part))

    hspec = pl.BlockSpec((1, D), lambda b, t: (0, 0))
    sspec = pl.BlockSpec((1, H, HG_DK, HG_DV), lambda b, t: (b, 0, 0, 0))
    in_specs = [zspec(0), zspec(0), zspec(1), zspec(2), hspec, hspec]
    args = [zqig, zf, zqig, zqig, lb.reshape(1, D), onorm_g.reshape(1, D)]
    if s0 is not None:
        in_specs.append(sspec)
        args.append(s0)
    return pl.pallas_call(
        functools.partial(_gla_kernel, L=L, n_chunks=n_chunks, has_init=s0 is not None),
        grid=(B, T // tt),
        in_specs=in_specs,
        out_specs=[pl.BlockSpec((1, tt, D), lambda b, t: (b, t, 0)), sspec],
        out_shape=[jax.ShapeDtypeStruct((B, T, D), BF16),
                   jax.ShapeDtypeStruct((B, H, HG_DK, HG_DV), F32)],
        scratch_shapes=[pltpu.VMEM((H, HG_DV, HG_DK), F32)],
        compiler_params=_cparams(2),
        name="gla",
    )(*args)


def _route_kernel(h_ref, rw_ref, bias_ref, pos_ref, w_ref, te_ref, nu_ref,
                  e_s, r_s, base_s, start_s, *, tile_rows):
    ph = pl.program_id(0)
    i = pl.program_id(1)
    M = h_ref.shape[0]
    half = h_ref.shape[1]
    G, E = N_GROUPS, N_EXPERTS // N_GROUPS
    e_flat = lax.broadcasted_iota(jnp.int32, (N_EXPERTS, M), 0)

    @pl.when(ph == 1)
    def _():
        @pl.when(i == 0)
        def _():
            cnt = base_s[...]
            padded = jnp.floor((cnt + (tile_rows - 1)) * (1.0 / tile_rows)) * tile_rows
            r = lax.broadcasted_iota(jnp.int32, (N_EXPERTS, N_EXPERTS), 0)
            c = lax.broadcasted_iota(jnp.int32, (N_EXPERTS, N_EXPERTS), 1)
            start = jnp.dot((c < r).astype(F32), padded, preferred_element_type=F32,
                            precision=lax.Precision.HIGHEST)
            start_s[...] = start
            te_ref[...] = (start * (1.0 / tile_rows)).astype(jnp.int32)
            nu_ref[...] = (padded * (1.0 / tile_rows)).astype(jnp.int32)

        start_col = start_s[:, :1]
        for k in range(TOP_K):
            hit = e_flat == e_s[i, k:k + 1, :]
            seg = jnp.sum(jnp.where(hit, start_col, 0.0), axis=0, keepdims=True)
            pos_ref[k:k + 1, :] = (seg + r_s[i, k:k + 1, :]).astype(jnp.int32)

    @pl.when(ph == 0)
    def _():
        _route_pass0(h_ref, rw_ref, bias_ref, w_ref, e_s, r_s, base_s, i, M, half, G, E)


def _route_pass0(h_ref, rw_ref, bias_ref, w_ref, e_s, r_s, base_s, i, M, half, G, E):
    @pl.when(i == 0)
    def _():
        base_s[...] = jnp.zeros_like(base_s)

    lo, hi = _unpack_pairs(h_ref[...])
    rw = rw_ref[...].astype(BF16)
    logits = _dot_nt(rw[:, :half], lo) + _dot_nt(rw[:, half:], hi)
    s = jax.nn.sigmoid(logits)
    sb = (s + bias_ref[...]).reshape(G, E, M)
    s = s.reshape(G, E, M)
    e_in = lax.broadcasted_iota(jnp.int32, (G, E, M), 1).astype(F32)
    g_id = lax.broadcasted_iota(jnp.int32, (G, 1, M), 0)
    e_id = lax.broadcasted_iota(jnp.int32, (G, E, M), 0).astype(F32) * E + e_in

    def all_max(a):
        return jnp.max(jnp.max(a, axis=0, keepdims=True), axis=1, keepdims=True)

    def all_min(a):
        return jnp.min(jnp.min(a, axis=0, keepdims=True), axis=1, keepdims=True)

    def all_sum(a):
        return jnp.sum(jnp.sum(a, axis=0, keepdims=True), axis=1, keepdims=True)

    m1 = jnp.max(sb, axis=1, keepdims=True)
    first = jnp.min(jnp.where(sb == m1, e_in, float(E)), axis=1, keepdims=True)
    m2 = jnp.max(jnp.where(e_in == first, NEG_INF, sb), axis=1, keepdims=True)
    gs = m1 + m2

    rank = jnp.zeros((G, 1, M), jnp.int32)
    for j in range(G):
        gj = gs[j:j + 1]
        beats = (gj > gs) | ((gj == gs) & (j < g_id))
        rank = rank + beats.astype(jnp.int32)
    gsel = rank < TOPK_GROUPS

    vals = jnp.where(gsel, sb, NEG_INF)
    selm = jnp.zeros((G, E, M), F32)
    chosen, score = [], []
    for _ in range(TOP_K):
        m = all_max(vals)
        first = all_min(jnp.where(vals == m, e_id, float(N_EXPERTS)))
        hit = e_id == first
        score.append(all_sum(jnp.where(hit, s, 0.0)))
        selm = jnp.where(hit, 1.0, selm)
        vals = jnp.where(hit, NEG_INF, vals)
        chosen.append(first)

    tot = score[0]
    for sc in score[1:]:
        tot = tot + sc
    norm = ROUTED_SCALE / tot

    selm = selm.reshape(N_EXPERTS, M)
    earlier = (lax.broadcasted_iota(jnp.int32, (M, M), 0)
               < lax.broadcasted_iota(jnp.int32, (M, M), 1)).astype(BF16)
    rank = (base_s[:, :1] + _dot(selm.astype(BF16), earlier)).reshape(G, E, M)
    base_s[...] = base_s[...] + jnp.sum(selm, axis=1, keepdims=True)
    for k in range(TOP_K):
        hit = e_id == chosen[k]
        e_s[i, k:k + 1, :] = chosen[k].reshape(1, M).astype(jnp.int32)
        r_s[i, k:k + 1, :] = all_sum(jnp.where(hit, rank, 0.0)).reshape(1, M)
        w_ref[k:k + 1, :] = (score[k] * norm).reshape(1, M)


def route(hp, router_w_t, router_bias, l, tile_rows, rows=512):
    N, half = hp.shape
    M = rows
    nT = N // M
    assert N % M == 0

    def p0(ph, i):
        return i * (1 - ph) + (nT - 1) * ph

    return pl.pallas_call(
        functools.partial(_route_kernel, tile_rows=tile_rows),
        grid=(2, nT),
        in_specs=[pl.BlockSpec((M, half), lambda ph, i: (p0(ph, i), 0)),
                  pl.BlockSpec((None, N_EXPERTS, 2 * half), lambda ph, i: (l, 0, 0)),
                  pl.BlockSpec((None, N_EXPERTS, 1), lambda ph, i: (l, 0, 0))],
        out_specs=[pl.BlockSpec((TOP_K, M), lambda ph, i: (0, i * ph)),
                   pl.BlockSpec((TOP_K, M), lambda ph, i: (0, p0(ph, i))),
                   pl.BlockSpec((N_EXPERTS, 128), lambda ph, i: (0, 0)),
                   pl.BlockSpec((N_EXPERTS, 128), lambda ph, i: (0, 0))],
        out_shape=[jax.ShapeDtypeStruct((TOP_K, N), jnp.int32),
                   jax.ShapeDtypeStruct((TOP_K, N), F32),
                   jax.ShapeDtypeStruct((N_EXPERTS, 128), jnp.int32),
                   jax.ShapeDtypeStruct((N_EXPERTS, 128), jnp.int32)],
        scratch_shapes=[pltpu.VMEM((nT, TOP_K, M), jnp.int32), pltpu.VMEM((nT, TOP_K, M), F32),
                        pltpu.VMEM((N_EXPERTS, 128), F32), pltpu.VMEM((N_EXPERTS, 128), F32)],
        compiler_params=_cparams(2),
        name="route",
    )(hp, router_w_t, router_bias.reshape(-1, N_EXPERTS, 1))


def _sc_mesh():
    return plsc.VectorSubcoreMesh(core_axis_name="core", subcore_axis_name="subcore")


def sc_invert(pos_flat, n_tok, n_out):
    n = pos_flat.shape[0]
    per = n_out // SC_WORKERS
    chunk = n_tok
    assert n_out % SC_WORKERS == 0 and per % SC_LANES == 0
    assert n_tok % chunk == 0 and n % chunk == 0 and chunk % SC_LANES == 0
    cp = pltpu.CompilerParams()
    if "needs_layout_passes" in pltpu.CompilerParams.__dataclass_fields__:
        cp = dataclasses.replace(cp, needs_layout_passes=False)

    @functools.partial(
        pl.kernel, out_type=jax.ShapeDtypeStruct((n_out,), jnp.int32), mesh=_sc_mesh(),
        scratch_types=[pltpu.VMEM((chunk,), jnp.int32), pltpu.VMEM((per,), jnp.int32)],
        compiler_params=cp, name="sc_invert")
    def k(pos_hbm, src_hbm, pos_v, src_v):
        wid = lax.axis_index("subcore") * SC_CORES + lax.axis_index("core")
        lo = wid * per
        lane = lax.iota(jnp.int32, SC_LANES)

        @pl.loop(0, per, step=SC_LANES)
        def _(r):
            src_v[pl.ds(r, SC_LANES)] = lax.rem(lo + r + lane, n_tok)

        @pl.loop(0, n // chunk)
        def _(c):
            base = c * chunk
            pltpu.sync_copy(pos_hbm.at[pl.ds(base, chunk)], pos_v)
            tok0 = lax.rem(base, n_tok)

            @plsc.parallel_loop(0, chunk, step=SC_LANES, unroll=8)
            def _(r):
                p = pos_v[pl.ds(r, SC_LANES)] - lo
                mine = (p >= 0) & (p < per)
                plsc.store_scatter(src_v, [jnp.where(mine, p, 0)], tok0 + r + lane, mask=mine)

        pltpu.sync_copy(src_v, src_hbm.at[pl.ds(lo, per)])

    return k(pos_flat)


def sc_gather(x, idx):
    n = idx.shape[0]
    dim = x.shape[1]
    assert n % (SC_WINDOW * SC_WORKERS) == 0

    @functools.partial(
        pl.kernel, out_type=jax.ShapeDtypeStruct((n, dim), x.dtype), mesh=_sc_mesh(),
        scratch_types=[], name="sc_gather")
    def k(x_hbm, i_hbm, o_hbm):
        def body(i_vmem, o_vmem):
            pltpu.sync_copy(x_hbm.at[i_vmem.at[0]], o_vmem)

        pltpu.emit_pipeline(
            body, grid=(n // SC_WINDOW,),
            in_specs=[pl.BlockSpec((1, SC_WINDOW), index_map=lambda i: (i, 0))],
            out_specs=[pl.BlockSpec((SC_WINDOW, dim), index_map=lambda i: (i, 0))],
            core_axis_name=("core", "subcore"),
            dimension_semantics=(pltpu.PARALLEL,),
        )(i_hbm, o_hbm)

    return k(x, idx.reshape(n // SC_WINDOW, SC_WINDOW))


def _moe_gemm_kernel(ts_ref, tn_ref, x_hbm, wi_ref, wo_ref, o_hbm, wi_b, wo_b, xbuf, obuf, in_sem, out_sem,
                     *, tile_rows, n_tiles):
    e = pl.program_id(0)
    last = pl.num_programs(0) - 1
    t0 = ts_ref[e]
    n = tn_ref[e]
    n_used = ts_ref[last] + tn_ref[last]

    def x_copy(g, slot):
        rows = pl.ds(pl.multiple_of(g * tile_rows, tile_rows), tile_rows)
        return pltpu.make_async_copy(x_hbm.at[rows], xbuf.at[slot], in_sem.at[slot])

    def o_copy(g, slot):
        rows = pl.ds(pl.multiple_of(g * tile_rows, tile_rows), tile_rows)
        return pltpu.make_async_copy(obuf.at[slot], o_hbm.at[rows], out_sem.at[slot])

    @pl.when(e == 0)
    def _():
        for g0 in range(MOE_NBUF - 1):
            @pl.when(g0 < n_used)
            def _():
                x_copy(g0, g0).start()

    @pl.when(n > 0)
    def _():
        wi_b[...] = wi_ref[...].astype(BF16)
        wo_b[...] = wo_ref[...].astype(BF16)

    def tile(i, carry):
        g = t0 + i
        slot = lax.rem(g, MOE_NBUF)
        x_copy(g, slot).wait()
        ahead = g + (MOE_NBUF - 1)

        @pl.when(ahead < n_used)
        def _():
            x_copy(ahead, lax.rem(ahead, MOE_NBUF)).start()

        @pl.when(g >= MOE_NBUF)
        def _():
            o_copy(g - MOE_NBUF, slot).wait()

        rows = tile_rows // MOE_SUB
        half = xbuf.shape[2]
        xs = [_unpack_pairs(xbuf[slot, r * rows:(r + 1) * rows, :]) for r in range(MOE_SUB)]
        hus = [_dot(lo, wi_b[:half, :]) + _dot(hi, wi_b[half:, :]) for lo, hi in xs]
        acts = [(_silu(hu[:, :EXPERT_FF]) * hu[:, EXPERT_FF:]).astype(BF16) for hu in hus]
        outs = [_dot(act, wo_b[...]) for act in acts]
        for r, out in enumerate(outs):
            obuf[slot, r * rows:(r + 1) * rows, :] = _pack_pairs(out)
        o_copy(g, slot).start()
        return carry

    lax.fori_loop(0, n, tile, 0)

    @pl.when(e == last)
    def _():
        for back in range(MOE_NBUF, 0, -1):
            @pl.when(n_used >= back)
            def _():
                o_copy(n_used - back, lax.rem(n_used - back, MOE_NBUF)).wait()

        obuf[...] = jnp.zeros(obuf.shape, obuf.dtype)
        n_clear = n_tiles - n_used

        def clear(i, carry):
            slot = lax.rem(i, MOE_NBUF)

            @pl.when(i >= MOE_NBUF)
            def _():
                o_copy(n_used + i - MOE_NBUF, slot).wait()

            o_copy(n_used + i, slot).start()
            return carry

        lax.fori_loop(0, n_clear, clear, 0)
        for back in range(MOE_NBUF, 0, -1):
            @pl.when(n_clear >= back)
            def _():
                o_copy(n_tiles - back, lax.rem(n_clear - back, MOE_NBUF)).wait()


def moe_gemm(xs, tile_start, tile_count, exp_w_in, exp_w_out, l, tile_rows, n_tiles):
    P, half = xs.shape
    D = 2 * half
    assert P == n_tiles * tile_rows
    hbm = pl.BlockSpec(memory_space=pl.ANY)
    grid_spec = pltpu.PrefetchScalarGridSpec(
        num_scalar_prefetch=2,
        grid=(N_EXPERTS,),
        in_specs=[hbm,
                  pl.BlockSpec((None, None, D, 2 * EXPERT_FF), lambda e, ts, tn: (l, e, 0, 0)),
                  pl.BlockSpec((None, None, EXPERT_FF, D), lambda e, ts, tn: (l, e, 0, 0))],
        out_specs=hbm,
        scratch_shapes=[pltpu.VMEM((D, 2 * EXPERT_FF), BF16), pltpu.VMEM((EXPERT_FF, D), BF16),
                        pltpu.VMEM((MOE_NBUF, tile_rows, half), jnp.int32),
                        pltpu.VMEM((MOE_NBUF, tile_rows, half), jnp.int32),
                        pltpu.SemaphoreType.DMA((MOE_NBUF,)), pltpu.SemaphoreType.DMA((MOE_NBUF,))],
    )
    return pl.pallas_call(
        functools.partial(_moe_gemm_kernel, tile_rows=tile_rows, n_tiles=n_tiles),
        grid_spec=grid_spec,
        out_shape=jax.ShapeDtypeStruct((P, half), jnp.int32),
        compiler_params=_cparams(1),
        name="moe_gemm",
    )(tile_start, tile_count, xs, exp_w_in, exp_w_out)


def _moe_combine_kernel(*refs, final, norm_next, shared_kv, queries):
    it = iter(refs)
    y_ref, w_ref, h_ref, si_ref, so_ref, x_ref, g2_ref = (next(it) for _ in range(7))
    fg_ref = next(it) if final else None
    ng_ref, nsc_ref, nsh_ref = (next(it) for _ in range(3)) if norm_next else (None,) * 3
    kg_ref, wkv_ref, lg_ref, cos_ref, sin_ref = (next(it) for _ in range(5)) if shared_kv else (None,) * 5
    wdq_ref, qg_ref, wqt_ref, wqrt_ref, cost_ref, sint_ref = (next(it) for _ in range(6)) if queries else (None,) * 6
    o_ref = next(it)
    hn_ref = next(it) if norm_next and not queries else None
    lat_ref, kr_ref = (next(it), next(it)) if shared_kv else (None, None)
    qt_ref = next(it) if queries else None
    si_b, so_b = next(it), next(it)
    wkv_b = next(it) if shared_kv else None
    wdq_b, wqt_b, wqrt_b = (next(it), next(it), next(it)) if queries else (None,) * 3

    @pl.when(pl.program_id(0) == 0)
    def _():
        si_b[...] = si_ref[...].astype(BF16)
        so_b[...] = so_ref[...].astype(BF16)
        if shared_kv:
            wkv_b[...] = wkv_ref[...].astype(BF16)
        if queries:
            wdq_b[...] = wdq_ref[...].astype(BF16)
            wqt_b[...] = wqt_ref[...].astype(BF16)
            wqrt_b[...] = wqrt_ref[...].astype(BF16)

    bb, tt, D = x_ref.shape
    half = D // 2
    w = w_ref[...].T
    acc_lo = jnp.zeros((bb * tt, half), F32)
    acc_hi = jnp.zeros((bb * tt, half), F32)
    for k in range(TOP_K):
        lo, hi = _unpack_pairs(y_ref[k], F32)
        acc_lo = acc_lo + w[:, k:k + 1] * lo
        acc_hi = acc_hi + w[:, k:k + 1] * hi
    hlo, hhi = _unpack_pairs(h_ref[...])
    hu = _dot(hlo, si_b[:half, :]) + _dot(hhi, si_b[half:, :])
    act = (_silu(hu[:, :SHARED_FF]) * hu[:, SHARED_FF:]).astype(BF16)
    y = jnp.concatenate([acc_lo, acc_hi], axis=-1) + _dot(act, so_b[...])
    x_new = x_ref[...] + g2_ref[...] * y.reshape(bb, tt, D)
    o_ref[...] = _rms(x_new, fg_ref[...]) if final else x_new
    if norm_next:
        hn = (_rms(x_new, ng_ref[...]) * (1.0 + nsc_ref[...]) + nsh_ref[...]).astype(BF16)
        if queries:
            _queries_t(hn.reshape(bb * tt, D), wdq_b, qg_ref[...], wqt_b, wqrt_b, cost_ref[...], sint_ref[...],
                       qt_ref)
        else:
            hn_ref[...] = hn
    if shared_kv:
        xn = _rms(x_new, kg_ref[...]).reshape(bb * tt, D).astype(BF16)
        z = _dot(xn, wkv_b[...])
        lat_ref[...] = _rms(z[:, :KV_LORA], lg_ref[...]).reshape(bb, tt, KV_LORA)
        zr = z[:, KV_LORA:KV_LORA + MLA_ROPE].reshape(bb, tt, MLA_ROPE)
        zq = z[:, KV_LORA + 128:KV_LORA + 128 + MLA_ROPE].reshape(bb, tt, MLA_ROPE)
        kr_ref[...] = zr * cos_ref[...] + zq * sin_ref[...]


def moe_combine(y8, w8, hp, sh_w_in, sh_w_out, x, mod, gate_idx, l, row0, final_g=None, next_norm=None,
                shared_kv=None, queries=None, rows=512):
    B, T, D = x.shape
    half = D // 2
    bb, tt, nblk, ij = _row_blocks(B, T, rows)
    M = bb * tt
    assert row0 % M == 0
    off = row0 // M
    xspec = pl.BlockSpec((bb, tt, D), lambda i: ij(i) + (0,))
    in_specs = [pl.BlockSpec((TOP_K, M, half), lambda i: (0, off + i, 0)),
                pl.BlockSpec((TOP_K, M), lambda i: (0, off + i)),
                pl.BlockSpec((M, half), lambda i: (off + i, 0)),
                pl.BlockSpec((None, D, 2 * SHARED_FF), lambda i: (l, 0, 0)),
                pl.BlockSpec((None, SHARED_FF, D), lambda i: (l, 0, 0)),
                xspec,
                pl.BlockSpec((bb, 1, D), lambda i: (ij(i)[0], 0, gate_idx))]
    args = [y8, w8, hp, sh_w_in, sh_w_out, x, mod]
    out_specs = xspec
    out_shape = jax.ShapeDtypeStruct((B, T, D), F32)
    if final_g is not None:
        assert next_norm is None
        in_specs.append(pl.BlockSpec((1, D), lambda i: (0, 0)))
        args.append(final_g.reshape(1, D))
    if next_norm is not None:
        gain, mod_next, sc_idx, sh_idx = next_norm
        in_specs += [pl.BlockSpec((1, D), lambda i: (0, 0)),
                     pl.BlockSpec((bb, 1, D), lambda i: (ij(i)[0], 0, sc_idx)),
                     pl.BlockSpec((bb, 1, D), lambda i: (ij(i)[0], 0, sh_idx))]
        args += [gain.reshape(1, D), mod_next, mod_next]
        if queries is None:
            out_specs = [xspec, xspec]
            out_shape = [out_shape, jax.ShapeDtypeStruct((B, T, D), BF16)]
    scratch = [pltpu.VMEM((D, 2 * SHARED_FF), BF16), pltpu.VMEM((SHARED_FF, D), BF16)]
    if shared_kv is not None:
        kv_in_g, w_kv, kv_lat_g, cos32, sin32 = shared_kv
        tspec = pl.BlockSpec((tt, MLA_ROPE), lambda i: (ij(i)[1], 0))
        in_specs += [pl.BlockSpec((1, D), lambda i: (0, 0)),
                     pl.BlockSpec(w_kv.shape, lambda i: (0, 0)),
                     pl.BlockSpec((1, KV_LORA), lambda i: (0, 0)),
                     tspec, tspec]
        args += [kv_in_g.reshape(1, D), w_kv, kv_lat_g.reshape(1, KV_LORA), cos32, sin32]
        out_specs = list(out_specs) if isinstance(out_specs, list) else [out_specs]
        out_shape = list(out_shape) if isinstance(out_shape, list) else [out_shape]
        out_specs += [pl.BlockSpec((bb, tt, KV_LORA), lambda i: ij(i) + (0,)),
                      pl.BlockSpec((bb, tt, MLA_ROPE), lambda i: ij(i) + (0,))]
        out_shape += [jax.ShapeDtypeStruct((B, T, KV_LORA), F32), jax.ShapeDtypeStruct((B, T, MLA_ROPE), F32)]
        scratch.append(pltpu.VMEM(w_kv.shape, BF16))
    if queries is not None:
        assert next_norm is not None and bb == 1
        w_dq, q_norm_g, wq_t, wqr_t, bi, cos_t, sin_t = queries
        NQ, NR = wq_t.shape[1], wqr_t.shape[1]
        tspec_t = pl.BlockSpec((MLA_ROPE, tt), lambda i: (0, ij(i)[1]))
        in_specs += [pl.BlockSpec((None, D, Q_LORA), lambda i: (bi, 0, 0)),
                     pl.BlockSpec((None, 1, Q_LORA), lambda i: (bi, 0, 0)),
                     pl.BlockSpec((None, NQ, Q_LORA), lambda i: (bi, 0, 0)),
                     pl.BlockSpec((None, NR, Q_LORA), lambda i: (bi, 0, 0)),
                     tspec_t, tspec_t]
        args += [w_dq, q_norm_g.reshape(-1, 1, Q_LORA), wq_t, wqr_t, cos_t, sin_t]
        out_specs = list(out_specs) if isinstance(out_specs, list) else [out_specs]
        out_shape = list(out_shape) if isinstance(out_shape, list) else [out_shape]
        out_specs.append(pl.BlockSpec((1, NQ, tt), lambda i: (ij(i)[0], 0, ij(i)[1])))
        out_shape.append(jax.ShapeDtypeStruct((B, NQ, T), BF16))
        scratch += [pltpu.VMEM((D, Q_LORA), BF16), pltpu.VMEM((NQ, Q_LORA), BF16), pltpu.VMEM((NR, Q_LORA), BF16)]
    return pl.pallas_call(
        functools.partial(_moe_combine_kernel, final=final_g is not None, norm_next=next_norm is not None,
                          shared_kv=shared_kv is not None, queries=queries is not None),
        grid=(nblk,),
        in_specs=in_specs,
        out_specs=out_specs,
        out_shape=out_shape,
        scratch_shapes=scratch,
        compiler_params=_cparams(1),
        name="moe_combine",
    )(*args)


def _kv_expand_kernel(lat_ref, kr_ref, wk_ref, ek_ref, wvt_ref, ones_ref, k_ref, vt_ref):
    lat = lat_ref[0].astype(BF16)
    kr = kr_ref[0].astype(BF16)
    k = _dot(lat, wk_ref[...].astype(BF16)) + _dot(kr, ek_ref[...].astype(BF16))
    k_ref[0] = k.astype(k_ref.dtype)
    vt = _dot_nt(wvt_ref[...].astype(BF16), lat) + ones_ref[...]
    vt_ref[0] = vt.astype(vt_ref.dtype)


def kv_expand(lat, kr, wk_pad, ek, wvt_ext, ones_col, rows=512):
    B, T, _ = lat.shape
    tt = rows
    NK, NVT = wk_pad.shape[1], wvt_ext.shape[0]

    def full(a):
        return pl.BlockSpec(a.shape, lambda b, t: (0, 0))

    def rowspec(n):
        return pl.BlockSpec((1, tt, n), lambda b, t: (b, t, 0))

    return pl.pallas_call(
        _kv_expand_kernel,
        grid=(B, T // tt),
        in_specs=[rowspec(KV_LORA), rowspec(MLA_ROPE), full(wk_pad), full(ek), full(wvt_ext), full(ones_col)],
        out_specs=[rowspec(NK), pl.BlockSpec((1, NVT, tt), lambda b, t: (b, 0, t))],
        out_shape=[jax.ShapeDtypeStruct((B, T, NK), BF16), jax.ShapeDtypeStruct((B, NVT, T), BF16)],
        compiler_params=_cparams(2),
        name="kv_expand",
    )(lat, kr, wk_pad, ek, wvt_ext, ones_col)


def _query_kernel(h_ref, wdq_ref, qg_ref, wq_ref, wqr_ref, c_ref, s_ref, q_ref, wdq_b, wq_b, wqr_b):
    @pl.when(pl.program_id(0) == 0)
    def _():
        wdq_b[...] = wdq_ref[...].astype(BF16)
        wq_b[...] = wq_ref[...].astype(BF16)
        wqr_b[...] = wqr_ref[...].astype(BF16)

    bb, tt, D = h_ref.shape
    h = h_ref[...].reshape(bb * tt, D)
    cq = _rms(_dot(h, wdq_b[...]), qg_ref[...]).astype(BF16)
    q1 = _dot(cq, wq_b[...]).reshape(bb, tt, -1)
    q2 = _dot(cq, wqr_b[...]).reshape(bb, tt, -1)
    c = c_ref[...]
    s = s_ref[...]
    for hd in range(MLA_HEADS):
        sl = slice(hd * HEAD_PAD, (hd + 1) * HEAD_PAD)
        q_ref[:, :, sl] = (q1[:, :, sl] * c + q2[:, :, sl] * s).astype(q_ref.dtype)


def mla_queries(h, w_dq, q_norm_g, wq_pad, wq_rot, l, c128, s128, rows=512):
    B, T, D = h.shape
    bb, tt, nblk, ij = _row_blocks(B, T, rows)
    NQ = wq_pad.shape[-1]
    tspec = pl.BlockSpec((tt, HEAD_PAD), lambda i: (ij(i)[1], 0))
    return pl.pallas_call(
        _query_kernel,
        grid=(nblk,),
        in_specs=[pl.BlockSpec((bb, tt, D), lambda i: ij(i) + (0,)),
                  pl.BlockSpec((None, D, Q_LORA), lambda i: (l, 0, 0)),
                  pl.BlockSpec((None, 1, Q_LORA), lambda i: (l, 0, 0)),
                  pl.BlockSpec((None, Q_LORA, NQ), lambda i: (l, 0, 0)),
                  pl.BlockSpec((None, Q_LORA, NQ), lambda i: (l, 0, 0)),
                  tspec, tspec],
        out_specs=pl.BlockSpec((bb, tt, NQ), lambda i: ij(i) + (0,)),
        out_shape=jax.ShapeDtypeStruct((B, T, NQ), BF16),
        scratch_shapes=[pltpu.VMEM((D, Q_LORA), BF16), pltpu.VMEM((Q_LORA, NQ), BF16),
                        pltpu.VMEM((Q_LORA, NQ), BF16)],
        compiler_params=_cparams(1),
        name="mla_queries",
    )(h, w_dq, q_norm_g.reshape(-1, 1, Q_LORA), wq_pad, wq_rot, c128, s128)


def _queries_t(h, wdq_b, qg, wqt_b, wqrt_b, cos, sin, qt_ref):
    cq = _rms(_dot(h, wdq_b[...]), qg).astype(BF16)
    q1 = _dot_nt(wqt_b[...], cq)
    q2 = _dot_nt(wqrt_b[...], cq)
    pad = jnp.zeros((HEAD_PAD - MLA_NOPE - MLA_ROPE, q1.shape[1]), qt_ref.dtype)
    for hd in range(MLA_HEADS):
        r0 = hd * HEAD_PAD
        rope = (q1[r0 + MLA_NOPE:r0 + MLA_NOPE + MLA_ROPE] * cos
                + q2[hd * MLA_ROPE:(hd + 1) * MLA_ROPE] * sin)
        qt_ref[0, r0:r0 + MLA_NOPE, :] = (q1[r0:r0 + MLA_NOPE] * Q_PRESCALE).astype(qt_ref.dtype)
        qt_ref[0, r0 + MLA_NOPE:r0 + MLA_NOPE + MLA_ROPE, :] = rope.astype(qt_ref.dtype)
        qt_ref[0, r0 + MLA_NOPE + MLA_ROPE:r0 + HEAD_PAD, :] = pad


def _query_t_kernel(h_ref, wdq_ref, qg_ref, wqt_ref, wqrt_ref, cos_ref, sin_ref, qt_ref, wdq_b, wqt_b, wqrt_b):
    @pl.when((pl.program_id(0) == 0) & (pl.program_id(1) == 0))
    def _():
        wdq_b[...] = wdq_ref[...].astype(BF16)
        wqt_b[...] = wqt_ref[...].astype(BF16)
        wqrt_b[...] = wqrt_ref[...].astype(BF16)

    _queries_t(h_ref[0], wdq_b, qg_ref[...], wqt_b, wqrt_b, cos_ref[...], sin_ref[...], qt_ref)


def mla_queries_t(h, w_dq, q_norm_g, wq_t, wqr_t, l, cos_t, sin_t, rows=512):
    B, T, D = h.shape
    tt = rows
    NQ = wq_t.shape[1]
    NR = wqr_t.shape[1]
    tspec = pl.BlockSpec((MLA_ROPE, tt), lambda b, t: (0, t))
    return pl.pallas_call(
        _query_t_kernel,
        grid=(B, T // tt),
        in_specs=[pl.BlockSpec((1, tt, D), lambda b, t: (b, t, 0)),
                  pl.BlockSpec((None, D, Q_LORA), lambda b, t: (l, 0, 0)),
                  pl.BlockSpec((None, 1, Q_LORA), lambda b, t: (l, 0, 0)),
                  pl.BlockSpec((None, NQ, Q_LORA), lambda b, t: (l, 0, 0)),
                  pl.BlockSpec((None, NR, Q_LORA), lambda b, t: (l, 0, 0)),
                  tspec, tspec],
        out_specs=pl.BlockSpec((1, NQ, tt), lambda b, t: (b, 0, t)),
        out_shape=jax.ShapeDtypeStruct((B, NQ, T), BF16),
        scratch_shapes=[pltpu.VMEM((D, Q_LORA), BF16), pltpu.VMEM((NQ, Q_LORA), BF16),
                        pltpu.VMEM((NR, Q_LORA), BF16)],
        compiler_params=_cparams(2),
        name="mla_queries_t",
    )(h, w_dq, q_norm_g.reshape(-1, 1, Q_LORA), wq_t, wqr_t, cos_t, sin_t)


def _attn_prompt_kernel(qi_tab, ki_tab, qt_ref, k_ref, vt_ref, o_ref, *scratch, tq, tk):
    H = MLA_HEADS
    m_refs, l_refs, acc_refs = scratch[:H], scratch[H:2 * H], scratch[2 * H:]
    p_id = pl.program_id(1)
    qi = qi_tab[p_id]
    ki = ki_tab[p_id]

    @pl.when(ki == 0)
    def _():
        for hd in range(H):
            m_refs[hd][...] = jnp.full(m_refs[hd].shape, NEG_INF, F32)
            l_refs[hd][...] = jnp.zeros(l_refs[hd].shape, F32)
            acc_refs[hd][...] = jnp.zeros(acc_refs[hd].shape, F32)

    def block(masked):
        if masked:
            kchunk = (ki * tk + lax.broadcasted_iota(jnp.int32, (tk, tq), 0)) // CHUNK
            qchunk = (qi * tq + lax.broadcasted_iota(jnp.int32, (tk, tq), 1)) // CHUNK
            mask = kchunk <= qchunk
        def scores(hd):
            sl = slice(hd * HEAD_PAD, (hd + 1) * HEAD_PAD)
            return _dot(k_ref[0, :, sl], qt_ref[0, sl, :])

        pending = [scores(hd) for hd in range(ATTN_LOOKAHEAD)]
        for hd in range(H):
            if hd + ATTN_LOOKAHEAD < H:
                pending.append(scores(hd + ATTN_LOOKAHEAD))
            s = pending.pop(0)
            if masked:
                s = jnp.where(mask, s, NEG_INF)
            m_prev = m_refs[hd][...]
            m_new = jnp.maximum(m_prev, jnp.max(s, axis=0, keepdims=True))
            a = jnp.exp2(m_prev - m_new)
            p = jnp.exp2(s - m_new).astype(BF16)
            pv = _dot(vt_ref[0, hd * V_ROWS:(hd + 1) * V_ROWS, :], p)
            acc_refs[hd][...] = a * acc_refs[hd][...] + pv[:MLA_V]
            l_refs[hd][...] = a * l_refs[hd][...] + pv[MLA_V:MLA_V + 1]
            m_refs[hd][...] = m_new

    @pl.when(ki < qi)
    def _():
        block(False)

    @pl.when(ki == qi)
    def _():
        block(True)
        o_t = jnp.concatenate([acc_refs[hd][...] / l_refs[hd][...] for hd in range(H)], axis=0)
        o_ref[0] = o_t.T.astype(o_ref.dtype)


def attn_prompt(qt, k, vt, tq=256):
    B, NQ, T = qt.shape
    NVT = vt.shape[1]
    NV = MLA_HEADS * MLA_V
    tk = tq
    assert tq % CHUNK == 0
    nq = T // tq
    pairs = [(a, b) for a in range(nq) for b in range(a + 1)]
    qi_tab = jnp.asarray([a for a, _ in pairs], jnp.int32)
    ki_tab = jnp.asarray([b for _, b in pairs], jnp.int32)
    grid_spec = pltpu.PrefetchScalarGridSpec(
        num_scalar_prefetch=2,
        grid=(B, len(pairs)),
        in_specs=[pl.BlockSpec((1, NQ, tq), lambda b, p, qt, kt: (b, 0, qt[p])),
                  pl.BlockSpec((1, tk, NQ), lambda b, p, qt, kt: (b, kt[p], 0)),
                  pl.BlockSpec((1, NVT, tk), lambda b, p, qt, kt: (b, 0, kt[p]))],
        out_specs=pl.BlockSpec((1, tq, NV), lambda b, p, qt, kt: (b, qt[p], 0)),
        scratch_shapes=([pltpu.VMEM((1, tq), F32)] * (2 * MLA_HEADS)
                        + [pltpu.VMEM((MLA_V, tq), F32)] * MLA_HEADS),
    )
    return pl.pallas_call(
        functools.partial(_attn_prompt_kernel, tq=tq, tk=tk),
        grid_spec=grid_spec,
        out_shape=jax.ShapeDtypeStruct((B, T, NV), BF16),
        compiler_params=_cparams(2),
        name="attn_prompt",
    )(qi_tab, ki_tab, qt, k, vt)


def _absorb_kernel(q_ref, m_ref, o_ref):
    o_ref[...] = _dot(q_ref[...], m_ref[...].astype(BF16)).astype(o_ref.dtype)


def absorb_queries(q2d, m_abs):
    N = q2d.shape[0]
    H, _, W = m_abs.shape
    return pl.pallas_call(
        _absorb_kernel,
        grid=(H,),
        in_specs=[pl.BlockSpec((N, HEAD_PAD), lambda h: (0, h)),
                  pl.BlockSpec((None, HEAD_PAD, W), lambda h: (h, 0, 0))],
        out_specs=pl.BlockSpec((None, N, W), lambda h: (h, 0, 0)),
        out_shape=jax.ShapeDtypeStruct((H, N, W), BF16),
        compiler_params=_cparams(1),
        name="absorb_queries",
    )(q2d, m_abs)


def _attn_sample_kernel(q_ref, lat_ref, kr_ref, nlat_ref, nkr_ref, o_ref, m_ref, l_ref, acc_ref):
    kb = pl.program_id(1)
    H, Q, W = q_ref.shape
    q = q_ref[...].reshape(H * Q, W)
    q_lat = q[:, :KV_LORA]
    q_rope = q[:, KV_LORA:KV_LORA + MLA_ROPE]

    def update(lat_tile, kr_tile, n_sub, kr_transposed):
        sub = lat_tile.shape[0] // n_sub
        lats = [lat_tile[j * sub:(j + 1) * sub, :].astype(BF16) for j in range(n_sub)]
        if kr_transposed:
            krs = [kr_tile[:, j * sub:(j + 1) * sub].astype(BF16) for j in range(n_sub)]
            ss = [_dot_nt(q_lat, lat) + _dot(q_rope, kr) for lat, kr in zip(lats, krs)]
        else:
            krs = [kr_tile[j * sub:(j + 1) * sub, :].astype(BF16) for j in range(n_sub)]
            ss = [_dot_nt(q_lat, lat) + _dot_nt(q_rope, kr) for lat, kr in zip(lats, krs)]
        m_prev = m_ref[...]
        m_new = m_prev
        for s in ss:
            m_new = jnp.maximum(m_new, jnp.max(s, axis=-1, keepdims=True))
        a = jnp.exp2(m_prev - m_new)
        ps = [jnp.exp2(s - m_new[:, :1]) for s in ss]
        pv = _dot(ps[0].astype(BF16), lats[0])
        psum = jnp.sum(ps[0], axis=-1, keepdims=True)
        for p, lat in zip(ps[1:], lats[1:]):
            pv = pv + _dot(p.astype(BF16), lat)
            psum = psum + jnp.sum(p, axis=-1, keepdims=True)
        l_ref[...] = a * l_ref[...] + psum
        m_ref[...] = m_new
        acc_ref[...] = jnp.concatenate([a, a], axis=-1) * acc_ref[...] + pv

    @pl.when(kb == 0)
    def _():
        m_ref[...] = jnp.full_like(m_ref, NEG_INF)
        l_ref[...] = jnp.zeros_like(l_ref)
        acc_ref[...] = jnp.zeros_like(acc_ref)
        update(nlat_ref[0], nkr_ref[0], 1, False)

    update(lat_ref[0], kr_ref[0], SAMPLE_KEY_SUB, True)

    @pl.when(kb == pl.num_programs(1) - 1)
    def _():
        lsum = l_ref[...]
        o = acc_ref[...] / jnp.concatenate([lsum, lsum], axis=-1)
        o_ref[...] = o.reshape(H, Q, KV_LORA).astype(o_ref.dtype)


def attn_sample(q_abs, cache_lat, cache_kr_t, new_lat, new_kr, tk=4096):
    H, N, W = q_abs.shape
    B, P, _ = cache_lat.shape
    Q = new_lat.shape[1]
    qpos = P + np.arange(Q)
    kpos = np.arange(P + Q)
    assert bool(np.all((kpos // CHUNK)[None, :] <= (qpos // CHUNK)[:, None]))
    return pl.pallas_call(
        _attn_sample_kernel,
        grid=(B, P // tk),
        in_specs=[pl.BlockSpec((H, Q, W), lambda b, kb: (0, b, 0)),
                  pl.BlockSpec((1, tk, KV_LORA), lambda b, kb: (b, kb, 0)),
                  pl.BlockSpec((1, MLA_ROPE, tk), lambda b, kb: (b, 0, kb)),
                  pl.BlockSpec((1, Q, KV_LORA), lambda b, kb: (b, 0, 0)),
                  pl.BlockSpec((1, Q, MLA_ROPE), lambda b, kb: (b, 0, 0))],
        out_specs=pl.BlockSpec((H, Q, KV_LORA), lambda b, kb: (0, b, 0)),
        out_shape=jax.ShapeDtypeStruct((H, N, KV_LORA), BF16),
        scratch_shapes=[pltpu.VMEM((H * Q, 128), F32), pltpu.VMEM((H * Q, 128), F32),
                        pltpu.VMEM((H * Q, KV_LORA), F32)],
        compiler_params=_cparams(2),
        name="attn_sample",
    )(q_abs, cache_lat, cache_kr_t, new_lat, new_kr)


def _unabsorb_kernel(o_ref, w_ref, out_ref):
    out_ref[...] = (_dot(o_ref[0], w_ref[0].astype(BF16))
                    + _dot(o_ref[1], w_ref[1].astype(BF16))).astype(out_ref.dtype)


def unabsorb(o_lat, wuv_pad):
    H, N, R = o_lat.shape
    return pl.pallas_call(
        _unabsorb_kernel,
        grid=(H // 2,),
        in_specs=[pl.BlockSpec((2, N, R), lambda p: (p, 0, 0)),
                  pl.BlockSpec((2, R, 128), lambda p: (p, 0, 0))],
        out_specs=pl.BlockSpec((N, 128), lambda p: (0, p)),
        out_shape=jax.ShapeDtypeStruct((N, (H // 2) * 128), BF16),
        compiler_params=_cparams(1),
        name="unabsorb",
    )(o_lat, wuv_pad)


def _rope_tables(pos):
    half = MLA_ROPE // 2
    inv = 1.0 / (ROPE_THETA ** (np.arange(half, dtype=np.float64) * 2.0 / MLA_ROPE))
    ang = np.asarray(pos, np.float64)[:, None] * inv[None, :]
    cos = np.concatenate([np.cos(ang), np.cos(ang)], axis=-1)
    sin = np.concatenate([np.sin(ang), np.sin(ang)], axis=-1)
    T = cos.shape[0]
    c128 = np.zeros((T, HEAD_PAD)); s128 = np.zeros((T, HEAD_PAD))
    c128[:, :MLA_NOPE] = 1.0
    c128[:, MLA_NOPE:MLA_NOPE + MLA_ROPE] = cos
    s128[:, MLA_NOPE:MLA_NOPE + MLA_ROPE] = sin
    return dict(cos32=jnp.asarray(cos, F32), sin32=jnp.asarray(sin, F32),
                c128=jnp.asarray(c128 * Q_PRESCALE, F32), s128=jnp.asarray(s128 * Q_PRESCALE, F32),
                cos_t=jnp.asarray(cos.T * Q_PRESCALE, F32), sin_t=jnp.asarray(sin.T * Q_PRESCALE, F32))


def _rot_half_cols(w):
    half = w.shape[-1] // 2
    return jnp.concatenate([-w[..., half:], w[..., :half]], axis=-1)


def _prep_weights(w_dkv, w_uk, w_uv, w_uq, router_w):
    D = D_MODEL
    w_lat, w_rope = w_dkv[:, :KV_LORA], w_dkv[:, KV_LORA:]
    pad96 = jnp.zeros((D, 128 - MLA_ROPE), F32)
    w_kv = jnp.concatenate([w_lat, w_rope, pad96, _rot_half_cols(w_rope), pad96], axis=-1)

    zpad = HEAD_PAD - MLA_NOPE
    wk_pad = jnp.pad(w_uk, ((0, 0), (0, 0), (0, zpad))).reshape(KV_LORA, MLA_HEADS * HEAD_PAD)
    ek = jnp.zeros((MLA_ROPE, MLA_HEADS, HEAD_PAD), F32)
    ek = ek.at[:, :, MLA_NOPE:MLA_NOPE + MLA_ROPE].set(
        jnp.broadcast_to(jnp.eye(MLA_ROPE, dtype=F32)[:, None, :], (MLA_ROPE, MLA_HEADS, MLA_ROPE)))
    ek = ek.reshape(MLA_ROPE, MLA_HEADS * HEAD_PAD)
    wvt = jnp.transpose(w_uv, (1, 2, 0))
    wvt_ext = jnp.pad(wvt, ((0, 0), (0, V_ROWS - MLA_V), (0, 0))).reshape(MLA_HEADS * V_ROWS, KV_LORA)
    ones_col = jnp.tile((jnp.arange(V_ROWS) >= MLA_V).astype(F32), MLA_HEADS).reshape(-1, 1)

    nb = w_uq.shape[0]
    qn, qr = w_uq[..., :MLA_NOPE], w_uq[..., MLA_NOPE:]
    z32 = jnp.zeros(qr.shape[:-1] + (HEAD_PAD - MLA_NOPE - MLA_ROPE,), F32)
    wq_pad = jnp.concatenate([qn, qr, z32], axis=-1).reshape(nb, Q_LORA, MLA_HEADS * HEAD_PAD)
    wq_rot = jnp.concatenate([jnp.zeros_like(qn), _rot_half_cols(qr), z32], axis=-1)
    wq_rot = wq_rot.reshape(nb, Q_LORA, MLA_HEADS * HEAD_PAD)
    wq_t = jnp.transpose(wq_pad, (0, 2, 1))
    wqr_t = jnp.transpose(_rot_half_cols(qr).reshape(nb, Q_LORA, MLA_HEADS * MLA_ROPE), (0, 2, 1))

    m_abs = jnp.zeros((MLA_HEADS, HEAD_PAD, KV_LORA + 128), F32)
    m_abs = m_abs.at[:, :MLA_NOPE, :KV_LORA].set(jnp.transpose(w_uk, (1, 2, 0)))
    m_abs = m_abs.at[:, MLA_NOPE:MLA_NOPE + MLA_ROPE, KV_LORA:KV_LORA + MLA_ROPE].set(
        jnp.broadcast_to(jnp.eye(MLA_ROPE, dtype=F32), (MLA_HEADS, MLA_ROPE, MLA_ROPE)))

    wuv_h = jnp.transpose(w_uv, (1, 0, 2))
    even = jnp.pad(wuv_h, ((0, 0), (0, 0), (0, 64)))
    odd = jnp.pad(wuv_h, ((0, 0), (0, 0), (64, 0)))
    wuv_pad = jnp.where((jnp.arange(MLA_HEADS) % 2 == 0)[:, None, None], even, odd)

    rw_t = jnp.transpose(router_w, (0, 2, 1))
    return dict(w_kv=w_kv, wk_pad=wk_pad, ek=ek, wvt_ext=wvt_ext, ones_col=ones_col, wq_pad=wq_pad, wq_rot=wq_rot, wq_t=wq_t, wqr_t=wqr_t,
                m_abs=m_abs, wuv_pad=wuv_pad, rw_t=rw_t)


def _mixer(st, l, P, W, packed):
    rows_kw = dict(rows_total=packed["total"], row0=packed["row0"], rows_buf=packed["buf"])
    x, m = st["x"], st["mod"][l]
    B, T, _ = x.shape
    n_a = P["hg_w_in"].shape[0]
    norm2 = (P["norm2_g"][l], 4, 3)
    if l < n_a:
        zf, zqig = hgrn_proj(x, P["norm1_g"][l], m, 1, 0, P["hg_w_in"], l)
        s0 = None if st["hg_state"] is None else st["hg_state"][l]
        o, s_new = gla(zqig, zf, st["lbs"][l], P["hg_onorm_g"][l], s0)
        st["hg_new"].append(s_new)
        st["x"], packed["buf"] = linear(o, P["hg_w_out"], l, F32, x=x, mod=m, gate_idx=2, next_norm=norm2,
                                        **rows_kw)
    else:
        bi = l - n_a
        h = st.pop("h_next", None)
        qt = st.pop("qt_next", None)
        if h is None and qt is None:
            h = norm_mod(x, P["norm1_g"][l], m, sc_idx=1, sh_idx=0)
        if st["past_lat"] is None:
            if qt is None:
                qt = mla_queries_t(h, P["w_dq"], P["q_norm_g"], W["wq_t"], W["wqr_t"], bi, st["cos_t"],
                                   st["sin_t"])
            o = attn_prompt(qt, st["k_all"], st["v_all"])
        else:
            q = mla_queries(h, P["w_dq"], P["q_norm_g"], W["wq_pad"], W["wq_rot"], bi, st["c128"], st["s128"])
            q_abs = absorb_queries(q.reshape(B * T, -1), W["m_abs"])
            o_lat = attn_sample(q_abs, st["past_lat"], st["past_kr"], st["lat"], st["kr"])
            o = unabsorb(o_lat, W["wuv_pad"]).reshape(B, T, -1)
        st["x"], packed["buf"] = linear(o, P["w_o"], bi, F32, x=x, mod=m, gate_idx=2, next_norm=norm2, **rows_kw)
    packed["row0"] += B * T


def _moe(groups, hp, l, P, W):
    n_tok = hp.shape[0]
    n_tiles = (TOP_K * n_tok) // MOE_TILE + N_EXPERTS
    pos, w8, tile_start, tile_count = route(hp, W["rw_t"], P["router_bias"], l, MOE_TILE)
    pos_flat = pos.reshape(-1)
    src = sc_invert(pos_flat, n_tok, n_tiles * MOE_TILE)
    xs = sc_gather(hp, src)
    out = moe_gemm(xs, tile_start[:, 0], tile_count[:, 0], P["exp_w_in"], P["exp_w_out"], l,
                   MOE_TILE, n_tiles)
    y8 = sc_gather(out, pos_flat).reshape(TOP_K, n_tok, -1)
    last = l == P["norm1_g"].shape[0] - 1
    with_kv = l == P["hg_w_in"].shape[0] - 1
    next_is_mla = not last and l + 1 >= P["hg_w_in"].shape[0]
    row0 = 0
    for st in groups:
        B, T, _ = st["x"].shape
        with_q = next_is_mla and st["past_lat"] is None
        outs = moe_combine(
            y8, w8, hp, P["sh_w_in"], P["sh_w_out"], st["x"], st["mod"][l], 5, l, row0,
            final_g=P["final_g"] if last else None,
            next_norm=(P["norm1_g"][l + 1], st["mod"][l + 1], 1, 0) if next_is_mla else None,
            shared_kv=(P["kv_in_g"], W["w_kv"], P["kv_lat_g"], st["cos32"], st["sin32"]) if with_kv else None,
            queries=(P["w_dq"], P["q_norm_g"], W["wq_t"], W["wqr_t"], l + 1 - P["hg_w_in"].shape[0],
                     st["cos_t"], st["sin_t"]) if with_q else None,
            rows=256 if with_q else 512)
        outs = list(outs) if isinstance(outs, (list, tuple)) else [outs]
        st["x"] = outs.pop(0)
        if next_is_mla and not with_q:
            st["h_next"] = outs.pop(0)
        if with_kv:
            st["lat"], st["kr"] = outs.pop(0), outs.pop(0)
        if with_q:
            st["qt_next"] = outs.pop(0)
        row0 += B * T


def _group_state(x, mod, pos, hg_state, past_lat, past_kr, lbs):
    return dict(x=x, mod=mod, hg_state=hg_state, past_lat=past_lat, past_kr=past_kr, lbs=lbs,
                **_rope_tables(pos), hg_new=[],
                lat=None, kr=None, k_all=None, v_all=None)


def kernel(x_prompt, x_sample, state_hgrn, cache_mla_latent, cache_mla_krope, c_prompt, c_sample, ada_w, ada_b, norm1_g, norm2_g, hg_w_in, hg_lb_logits, hg_onorm_g, hg_w_out, kv_in_g, w_dkv, kv_lat_g, w_uk, w_uv, w_dq, q_norm_g, w_uq, w_o, router_w, router_bias, exp_w_in, exp_w_out, sh_w_in, sh_w_out, final_g):
    Bp, Sp, _ = x_prompt.shape
    Bs, Ss, _ = x_sample.shape
    past = cache_mla_latent.shape[1]
    P = dict(norm1_g=norm1_g, norm2_g=norm2_g, hg_w_in=hg_w_in, hg_lb_logits=hg_lb_logits,
             hg_onorm_g=hg_onorm_g, hg_w_out=hg_w_out, kv_in_g=kv_in_g, kv_lat_g=kv_lat_g,
             w_dq=w_dq, q_norm_g=q_norm_g, w_o=w_o, router_bias=router_bias,
             exp_w_in=exp_w_in, exp_w_out=exp_w_out, sh_w_in=sh_w_in, sh_w_out=sh_w_out, final_g=final_g)
    W = _prep_weights(w_dkv, w_uk, w_uv, w_uq, router_w)
    mod = ada_mod(jnp.concatenate([c_prompt, c_sample], axis=0), ada_w, ada_b)
    lbs = jnp.cumsum(jax.nn.softmax(hg_lb_logits.astype(F32), axis=0), axis=0)
    gp = _group_state(x_prompt, mod[:, :Bp, None, :], np.arange(Sp), None, None, None, lbs)
    gs = _group_state(x_sample, mod[:, Bp:, None, :], past + np.arange(Ss), state_hgrn,
                      cache_mla_latent, jnp.transpose(cache_mla_krope, (0, 2, 1)), lbs)
    groups = [gp, gs]
    n_tok = sum(st["x"].shape[0] * st["x"].shape[1] for st in groups)
    n_a = hg_w_in.shape[0]
    for l in range(norm1_g.shape[0]):
        packed = dict(total=n_tok, row0=0, buf=None)
        for st in groups:
            _mixer(st, l, P, W, packed)
        _moe(groups, packed["buf"], l, P, W)
        if l == n_a - 1:
            gp["k_all"], gp["v_all"] = kv_expand(gp["lat"], gp["kr"], W["wk_pad"], W["ek"], W["wvt_ext"],
                                                 W["ones_col"])
    return (gp["x"], gs["x"], jnp.stack(gp["hg_new"], axis=0), jnp.stack(gs["hg_new"], axis=0),
            gp["lat"], gp["kr"], gs["lat"], gs["kr"])
```

```python
import dataclasses
import functools

import numpy as np
import jax
import jax.numpy as jnp
from jax import lax
from jax.experimental import pallas as pl
from jax.experimental.pallas import tpu as pltpu
from jax.experimental.pallas import tpu_sc as plsc

F32 = jnp.float32
BF16 = jnp.bfloat16

D_MODEL = 1024
CHUNK = 64
HG_HEADS = 8
HG_DK = 128
HG_DV = 128
MLA_HEADS = 16
MLA_NOPE = 64
MLA_ROPE = 32
MLA_V = 64
Q_LORA = 384
KV_LORA = 256
ROPE_THETA = 10000.0
N_EXPERTS = 64
TOP_K = 8
N_GROUPS = 8
TOPK_GROUPS = 4
EXPERT_FF = 256
SHARED_FF = 256
ROUTED_SCALE = 2.5
EPS = 1e-6

HEAD_PAD = 128
SAMPLE_KEY_SUB = 8
ATTN_LOOKAHEAD = 6
V_ROWS = MLA_V + 16
QK_SCALE = (MLA_NOPE + MLA_ROPE) ** -0.5
Q_PRESCALE = QK_SCALE * float(np.log2(np.e))
VMEM_LIMIT = 56 * 1024 * 1024
NEG_INF = float("-inf")
SC_CORES = 2
SC_SUBCORES = 16
SC_WORKERS = SC_CORES * SC_SUBCORES
SC_LANES = 16
SC_WINDOW = 64
MOE_TILE = 512
MOE_NBUF = 6
MOE_SUB = 1


def _cparams(n_axes):
    return pltpu.CompilerParams(dimension_semantics=("arbitrary",) * n_axes,
                                vmem_limit_bytes=VMEM_LIMIT)


def _silu(x):
    return x * jax.nn.sigmoid(x)


def _rms(x, g):
    ms = jnp.mean(x * x, axis=-1, keepdims=True)
    return x * lax.rsqrt(ms + EPS) * g


def _dot(a, b):
    return jnp.dot(a, b, preferred_element_type=F32)


def _dot_nt(a, b):
    return lax.dot_general(a, b, (((1,), (1,)), ((), ())), preferred_element_type=F32)


def _dot_tn(a, b):
    return lax.dot_general(a, b, (((0,), (0,)), ((), ())), preferred_element_type=F32)


def _row_blocks(B, T, rows):
    if T >= rows:
        assert T % rows == 0
        bb, tt = 1, rows
    else:
        assert rows % T == 0 and B % (rows // T) == 0
        bb, tt = rows // T, T
    nt = T // tt
    return bb, tt, (B // bb) * nt, (lambda i: (i // nt, i % nt))


def _ada_kernel(c_ref, w_ref, b_ref, o_ref):
    a = _silu(c_ref[...]).astype(BF16)
    o_ref[...] = _dot(a, w_ref[...].astype(BF16)) + b_ref[...]


def ada_mod(c, ada_w, ada_b):
    R, D = c.shape
    L, _, N = ada_w.shape
    tn = 1536
    return pl.pallas_call(
        _ada_kernel,
        grid=(L, N // tn),
        in_specs=[pl.BlockSpec((R, D), lambda l, j: (0, 0)),
                  pl.BlockSpec((None, D, tn), lambda l, j: (l, 0, j)),
                  pl.BlockSpec((None, 1, tn), lambda l, j: (l, 0, j))],
        out_specs=pl.BlockSpec((None, R, tn), lambda l, j: (l, 0, j)),
        out_shape=jax.ShapeDtypeStruct((L, R, N), F32),
        compiler_params=_cparams(2),
        name="ada_mod",
    )(c, ada_w, ada_b.reshape(L, 1, N))


def _pack_pairs(y):
    half = y.shape[-1] // 2
    bits = lax.bitcast_convert_type(y.astype(BF16).astype(F32), jnp.uint32)
    word = lax.shift_right_logical(bits[:, :half], jnp.uint32(16)) | bits[:, half:]
    return lax.bitcast_convert_type(word, jnp.int32)


def _unpack_pairs(word, dtype=BF16):
    u = lax.bitcast_convert_type(word, jnp.uint32)
    lo = lax.bitcast_convert_type(lax.shift_left(u, jnp.uint32(16)), F32)
    hi = lax.bitcast_convert_type(u & jnp.uint32(0xFFFF0000), F32)
    return lo.astype(dtype), hi.astype(dtype)


def _norm_kernel(x_ref, g_ref, sc_ref, sh_ref, o_ref):
    y = _rms(x_ref[...], g_ref[...]) * (1.0 + sc_ref[...]) + sh_ref[...]
    o_ref[...] = y.astype(o_ref.dtype)


def norm_mod(x, g, mod, sc_idx, sh_idx, rows=512):
    B, T, D = x.shape
    bb, tt, nblk, ij = _row_blocks(B, T, rows)
    xspec = pl.BlockSpec((bb, tt, D), lambda i: ij(i) + (0,))
    return pl.pallas_call(
        _norm_kernel,
        grid=(nblk,),
        in_specs=[xspec, pl.BlockSpec((1, D), lambda i: (0, 0)),
                  pl.BlockSpec((bb, 1, D), lambda i: (ij(i)[0], 0, sc_idx)),
                  pl.BlockSpec((bb, 1, D), lambda i: (ij(i)[0], 0, sh_idx))],
        out_specs=xspec,
        out_shape=jax.ShapeDtypeStruct((B, T, D), BF16),
        compiler_params=_cparams(1),
        name="norm_mod",
    )(x, g.reshape(1, D), mod, mod)


def _linear_kernel(*refs, residual, norm_next, shared_rows, n_main):
    if norm_next and shared_rows:
        a_ref, w_ref, x_ref, gate_ref, ng_ref, nsc_ref, nsh_ref, _, o_ref, hp_ref, wb_ref = refs
    elif norm_next:
        a_ref, w_ref, x_ref, gate_ref, ng_ref, nsc_ref, nsh_ref, o_ref, hp_ref, wb_ref = refs
    elif residual:
        a_ref, w_ref, x_ref, gate_ref, o_ref, wb_ref = refs
    else:
        a_ref, w_ref, o_ref, wb_ref = refs

    @pl.when(pl.program_id(1) == 0)
    def _():
        wb_ref[...] = w_ref[...].astype(BF16)

    def main():
        bb, tt, K = a_ref.shape
        y = _dot(a_ref[...].reshape(bb * tt, K).astype(BF16), wb_ref[...])
        y = y.reshape(bb, tt, y.shape[-1])
        if residual:
            y = x_ref[...] + gate_ref[...] * y
        o_ref[...] = y.astype(o_ref.dtype)
        if norm_next:
            h = _rms(y, ng_ref[...]) * (1.0 + nsc_ref[...]) + nsh_ref[...]
            hp_ref[...] = _pack_pairs(h.reshape(bb * tt, h.shape[-1]))

    if n_main is None:
        main()
    else:
        pl.when(pl.program_id(1) < n_main)(main)

        @pl.when(pl.program_id(1) >= n_main)
        def _():
            hp_ref[...] = jnp.zeros(hp_ref.shape, hp_ref.dtype)


def linear(a, w, l, out_dtype, x=None, mod=None, gate_idx=0, rows=512, tn=1024, next_norm=None,
           rows_total=None, row0=0, rows_buf=None):
    B, T, K = a.shape
    _, _, N = w.shape
    tn = min(tn, N)
    bb, tt, nblk, ij0 = _row_blocks(B, T, rows)
    n_extra = 0
    if next_norm is not None and rows_buf is None and rows_total is not None:
        assert row0 == 0 and (rows_total - B * T) % (bb * tt) == 0
        n_extra = (rows_total - B * T) // (bb * tt)

    def ij(i):
        return ij0(jnp.minimum(i, nblk - 1)) if n_extra else ij0(i)

    in_specs = [pl.BlockSpec((bb, tt, K), lambda j, i: ij(i) + (0,)),
                pl.BlockSpec((None, K, tn), lambda j, i: (l, 0, j))]
    args = [a, w]
    ospec = pl.BlockSpec((bb, tt, tn), lambda j, i: ij(i) + (j,))
    out_specs = ospec
    out_shape = jax.ShapeDtypeStruct((B, T, N), out_dtype)
    aliases = {}
    if x is not None:
        gsteps = D_MODEL // tn
        in_specs += [ospec, pl.BlockSpec((bb, 1, tn), lambda j, i: (ij(i)[0], 0, gate_idx * gsteps + j))]
        args += [x, mod]
    if next_norm is not None:
        assert x is not None and tn == N
        gain, sc_idx, sh_idx = next_norm
        in_specs += [pl.BlockSpec((1, N), lambda j, i: (0, 0)),
                     pl.BlockSpec((bb, 1, N), lambda j, i: (ij(i)[0], 0, sc_idx)),
                     pl.BlockSpec((bb, 1, N), lambda j, i: (ij(i)[0], 0, sh_idx))]
        args += [gain.reshape(1, N), mod, mod]
        assert row0 % (bb * tt) == 0
        off = row0 // (bb * tt)
        out_specs = [ospec, pl.BlockSpec((bb * tt, N // 2), lambda j, i: (off + i, 0))]
        out_shape = [out_shape, jax.ShapeDtypeStruct((rows_total or B * T, N // 2), jnp.int32)]
        if rows_buf is not None:
            in_specs.append(pl.BlockSpec(memory_space=pl.ANY))
            args.append(rows_buf)
            aliases = {len(args) - 1: 1}
    return pl.pallas_call(
        functools.partial(_linear_kernel, residual=x is not None, norm_next=next_norm is not None,
                          shared_rows=rows_buf is not None, n_main=nblk if n_extra else None),
        grid=(N // tn, nblk + n_extra),
        in_specs=in_specs,
        out_specs=out_specs,
        out_shape=out_shape,
        scratch_shapes=[pltpu.VMEM((K, tn), BF16)],
        input_output_aliases=aliases,
        compiler_params=_cparams(2),
        name="linear",
    )(*args)


def _hgrn_proj_kernel(x_ref, g_ref, sc_ref, sh_ref, w_ref, zf_ref, zqig_ref, h_b, w_b):
    i = pl.program_id(0)
    j = pl.program_id(1)
    bb, tt, D = x_ref.shape

    @pl.when(i == 0)
    def _():
        w_b[j] = w_ref[...].astype(BF16)

    @pl.when(j == 0)
    def _():
        h = _rms(x_ref[...], g_ref[...]) * (1.0 + sc_ref[...]) + sh_ref[...]
        h_b[...] = h.reshape(bb * tt, D).astype(BF16)

    y = _dot(h_b[...], w_b[j]).reshape(bb, tt, -1)

    @pl.when(j == 1)
    def _():
        zf_ref[...] = y

    @pl.when(j != 1)
    def _():
        zqig_ref[...] = y.astype(zqig_ref.dtype)


def hgrn_proj(x, g, mod, sc_idx, sh_idx, w_in, l, rows=1024):
    B, T, D = x.shape
    bb, tt, nblk, ij = _row_blocks(B, T, min(rows, B * T))
    xspec = pl.BlockSpec((bb, tt, D), lambda i, j: ij(i) + (0,))
    return pl.pallas_call(
        _hgrn_proj_kernel,
        grid=(nblk, 4),
        in_specs=[xspec,
                  pl.BlockSpec((1, D), lambda i, j: (0, 0)),
                  pl.BlockSpec((bb, 1, D), lambda i, j: (ij(i)[0], 0, sc_idx)),
                  pl.BlockSpec((bb, 1, D), lambda i, j: (ij(i)[0], 0, sh_idx)),
                  pl.BlockSpec((None, D, D), lambda i, j: (l, 0, jnp.where(i == 0, j, 3)))],
        out_specs=[xspec,
                   pl.BlockSpec((bb, tt, D), lambda i, j: ij(i) + (j - (j >= 1),))],
        out_shape=[jax.ShapeDtypeStruct((B, T, D), F32), jax.ShapeDtypeStruct((B, T, 3 * D), BF16)],
        scratch_shapes=[pltpu.VMEM((bb * tt, D), BF16), pltpu.VMEM((4, D, D), BF16)],
        compiler_params=_cparams(2),
        name="hgrn_proj",
    )(x, g.reshape(1, D), mod, mod, w_in)


def _gla_kernel(*refs, L, n_chunks, has_init):
    if has_init:
        q_ref, f_ref, i_ref, g_ref, lb_ref, on_ref, s0_ref, o_ref, so_ref, st_ref = refs
    else:
        q_ref, f_ref, i_ref, g_ref, lb_ref, on_ref, o_ref, so_ref, st_ref = refs
    t = pl.program_id(1)
    H = st_ref.shape[0]

    @pl.when(t == 0)
    def _():
        for h in range(H):
            if has_init:
                st_ref[h] = s0_ref[0, h].T
            else:
                st_ref[h] = jnp.zeros(st_ref.shape[1:], F32)

    lb = lb_ref[...]
    onorm = on_ref[...]
    row = lax.broadcasted_iota(jnp.int32, (L, L), 0)
    col = lax.broadcasted_iota(jnp.int32, (L, L), 1)
    causal = col <= row
    tri = causal.astype(BF16)

    def chunk(c, carry):
        rows = pl.ds(pl.multiple_of(c * L, L), L)

        def write_o(sl, o):
            o_ref[0, rows, sl] = o.astype(o_ref.dtype)

        _gla_chunk(q_ref[0, rows, :], f_ref[0, rows, :], i_ref[0, rows, :], g_ref[0, rows, :],
                   lb, onorm, tri, causal, st_ref, write_o)
        return carry

    lax.fori_loop(0, n_chunks, chunk, 0, unroll=4 if n_chunks % 4 == 0 else 1)

    @pl.when(t == pl.num_programs(1) - 1)
    def _():
        for h in range(H):
            so_ref[0, h] = st_ref[h].T


def _gla_chunk(q, f, v, g, lb, onorm, tri, causal, st_ref, write_o):
    L = q.shape[0]
    H = st_ref.shape[0]
    mid = L // 2 - 1
    q = _silu(q.astype(F32))
    fg = lb + (1.0 - lb) * jax.nn.sigmoid(f)
    k = 1.0 - fg
    v = v.astype(BF16)
    gate = _silu(g.astype(F32))
    logf = jnp.log(fg)
    hi = logf.astype(BF16)
    lo = (logf - hi.astype(F32)).astype(BF16)
    b = _dot(tri, hi) + _dot(tri, lo)
    b_mid = b[mid:mid + 1, :]
    b_last = b[L - 1:L, :]
    qa = q * jnp.exp(b - b_mid)
    kb = k * jnp.exp(b_mid - b)
    qe = (qa * jnp.exp(b_mid)).astype(BF16)
    kd = (kb * jnp.exp(b_last - b_mid)).astype(BF16)
    qa = qa.astype(BF16)
    kb = kb.astype(BF16)
    decay = jnp.exp(b_last)
    sls = [slice(h * HG_DK, (h + 1) * HG_DK) for h in range(H)]
    sts = [st_ref[h] for h in range(H)]
    scores = [_dot_nt(qa[:, sl], kb[:, sl]) for sl in sls]
    inter = [_dot_nt(qe[:, sl], st.astype(BF16)) for sl, st in zip(sls, sts)]
    outer = [_dot_tn(v[:, sl], kd[:, sl]) for sl in sls]
    intra = [_dot(jnp.where(causal, sc, 0.0).astype(BF16), v[:, sl]) for sc, sl in zip(scores, sls)]
    for h, sl in enumerate(sls):
        st_ref[h] = sts[h] * decay[:, sl] + outer[h]
        write_o(sl, _rms(inter[h] + intra[h], onorm[:, sl]) * gate[:, sl])


def gla(zqig, zf, lb, onorm_g, s0):
    B, T, D = zf.shape
    L = CHUNK if T % CHUNK == 0 else T
    tt = min(T, 512)
    n_chunks = tt // L
    H = HG_HEADS

    def zspec(part):
        return pl.BlockSpec((1, tt, D), lambda b, t: (b, t, part))

    hspec = pl.BlockSpec((1, D), lambda b, t: (0, 0))
    sspec = pl.BlockSpec((1, H, HG_DK, HG_DV), lambda b, t: (b, 0, 0, 0))
    in_specs = [zspec(0), zspec(0), zspec(1), zspec(2), hspec, hspec]
    args = [zqig, zf, zqig, zqig, lb.reshape(1, D), onorm_g.reshape(1, D)]
    if s0 is not None:
        in_specs.append(sspec)
        args.append(s0)
    return pl.pallas_call(
        functools.partial(_gla_kernel, L=L, n_chunks=n_chunks, has_init=s0 is not None),
        grid=(B, T // tt),
        in_specs=in_specs,
        out_specs=[pl.BlockSpec((1, tt, D), lambda b, t: (b, t, 0)), sspec],
        out_shape=[jax.ShapeDtypeStruct((B, T, D), BF16),
                   jax.ShapeDtypeStruct((B, H, HG_DK, HG_DV), F32)],
        scratch_shapes=[pltpu.VMEM((H, HG_DV, HG_DK), F32)],
        compiler_params=_cparams(2),
        name="gla",
    )(*args)


def _route_kernel(h_ref, rw_ref, bias_ref, pos_ref, w_ref, te_ref, nu_ref,
                  e_s, r_s, base_s, start_s, *, tile_rows):
    ph = pl.program_id(0)
    i = pl.program_id(1)
    M = h_ref.shape[0]
    half = h_ref.shape[1]
    G, E = N_GROUPS, N_EXPERTS // N_GROUPS
    e_flat = lax.broadcasted_iota(jnp.int32, (N_EXPERTS, M), 0)

    @pl.when(ph == 1)
    def _():
        @pl.when(i == 0)
        def _():
            cnt = base_s[...]
            padded = jnp.floor((cnt + (tile_rows - 1)) * (1.0 / tile_rows)) * tile_rows
            r = lax.broadcasted_iota(jnp.int32, (N_EXPERTS, N_EXPERTS), 0)
            c = lax.broadcasted_iota(jnp.int32, (N_EXPERTS, N_EXPERTS), 1)
            start = jnp.dot((c < r).astype(F32), padded, preferred_element_type=F32,
                            precision=lax.Precision.HIGHEST)
            start_s[...] = start
            te_ref[...] = (start * (1.0 / tile_rows)).astype(jnp.int32)
            nu_ref[...] = (padded * (1.0 / tile_rows)).astype(jnp.int32)

        start_col = start_s[:, :1]
        for k in range(TOP_K):
            hit = e_flat == e_s[i, k:k + 1, :]
            seg = jnp.sum(jnp.where(hit, start_col, 0.0), axis=0, keepdims=True)
            pos_ref[k:k + 1, :] = (seg + r_s[i, k:k + 1, :]).astype(jnp.int32)

    @pl.when(ph == 0)
    def _():
        _route_pass0(h_ref, rw_ref, bias_ref, w_ref, e_s, r_s, base_s, i, M, half, G, E)


def _route_pass0(h_ref, rw_ref, bias_ref, w_ref, e_s, r_s, base_s, i, M, half, G, E):
    @pl.when(i == 0)
    def _():
        base_s[...] = jnp.zeros_like(base_s)

    lo, hi = _unpack_pairs(h_ref[...])
    rw = rw_ref[...].astype(BF16)
    logits = _dot_nt(rw[:, :half], lo) + _dot_nt(rw[:, half:], hi)
    s = jax.nn.sigmoid(logits)
    sb = (s + bias_ref[...]).reshape(G, E, M)
    s = s.reshape(G, E, M)
    e_in = lax.broadcasted_iota(jnp.int32, (G, E, M), 1).astype(F32)
    g_id = lax.broadcasted_iota(jnp.int32, (G, 1, M), 0)
    e_id = lax.broadcasted_iota(jnp.int32, (G, E, M), 0).astype(F32) * E + e_in

    def all_max(a):
        return jnp.max(jnp.max(a, axis=0, keepdims=True), axis=1, keepdims=True)

    def all_min(a):
        return jnp.min(jnp.min(a, axis=0, keepdims=True), axis=1, keepdims=True)

    def all_sum(a):
        return jnp.sum(jnp.sum(a, axis=0, keepdims=True), axis=1, keepdims=True)

    m1 = jnp.max(sb, axis=1, keepdims=True)
    first = jnp.min(jnp.where(sb == m1, e_in, float(E)), axis=1, keepdims=True)
    m2 = jnp.max(jnp.where(e_in == first, NEG_INF, sb), axis=1, keepdims=True)
    gs = m1 + m2

    rank = jnp.zeros((G, 1, M), jnp.int32)
    for j in range(G):
        gj = gs[j:j + 1]
        beats = (gj > gs) | ((gj == gs) & (j < g_id))
        rank = rank + beats.astype(jnp.int32)
    gsel = rank < TOPK_GROUPS

    vals = jnp.where(gsel, sb, NEG_INF)
    selm = jnp.zeros((G, E, M), F32)
    chosen, score = [], []
    for _ in range(TOP_K):
        m = all_max(vals)
        first = all_min(jnp.where(vals == m, e_id, float(N_EXPERTS)))
        hit = e_id == first
        score.append(all_sum(jnp.where(hit, s, 0.0)))
        selm = jnp.where(hit, 1.0, selm)
        vals = jnp.where(hit, NEG_INF, vals)
        chosen.append(first)

    tot = score[0]
    for sc in score[1:]:
        tot = tot + sc
    norm = ROUTED_SCALE / tot

    selm = selm.reshape(N_EXPERTS, M)
    earlier = (lax.broadcasted_iota(jnp.int32, (M, M), 0)
               < lax.broadcasted_iota(jnp.int32, (M, M), 1)).astype(BF16)
    rank = (base_s[:, :1] + _dot(selm.astype(BF16), earlier)).reshape(G, E, M)
    base_s[...] = base_s[...] + jnp.sum(selm, axis=1, keepdims=True)
    for k in range(TOP_K):
        hit = e_id == chosen[k]
        e_s[i, k:k + 1, :] = chosen[k].reshape(1, M).astype(jnp.int32)
        r_s[i, k:k + 1, :] = all_sum(jnp.where(hit, rank, 0.0)).reshape(1, M)
        w_ref[k:k + 1, :] = (score[k] * norm).reshape(1, M)


def route(hp, router_w_t, router_bias, l, tile_rows, rows=512):
    N, half = hp.shape
    M = rows
    nT = N // M
    assert N % M == 0

    def p0(ph, i):
        return i * (1 - ph) + (nT - 1) * ph

    return pl.pallas_call(
        functools.partial(_route_kernel, tile_rows=tile_rows),
        grid=(2, nT),
        in_specs=[pl.BlockSpec((M, half), lambda ph, i: (p0(ph, i), 0)),
                  pl.BlockSpec((None, N_EXPERTS, 2 * half), lambda ph, i: (l, 0, 0)),
                  pl.BlockSpec((None, N_EXPERTS, 1), lambda ph, i: (l, 0, 0))],
        out_specs=[pl.BlockSpec((TOP_K, M), lambda ph, i: (0, i * ph)),
                   pl.BlockSpec((TOP_K, M), lambda ph, i: (0, p0(ph, i))),
                   pl.BlockSpec((N_EXPERTS, 128), lambda ph, i: (0, 0)),
                   pl.BlockSpec((N_EXPERTS, 128), lambda ph, i: (0, 0))],
        out_shape=[jax.ShapeDtypeStruct((TOP_K, N), jnp.int32),
                   jax.ShapeDtypeStruct((TOP_K, N), F32),
                   jax.ShapeDtypeStruct((N_EXPERTS, 128), jnp.int32),
                   jax.ShapeDtypeStruct((N_EXPERTS, 128), jnp.int32)],
        scratch_shapes=[pltpu.VMEM((nT, TOP_K, M), jnp.int32), pltpu.VMEM((nT, TOP_K, M), F32),
                        pltpu.VMEM((N_EXPERTS, 128), F32), pltpu.VMEM((N_EXPERTS, 128), F32)],
        compiler_params=_cparams(2),
        name="route",
    )(hp, router_w_t, router_bias.reshape(-1, N_EXPERTS, 1))


def _sc_mesh():
    return plsc.VectorSubcoreMesh(core_axis_name="core", subcore_axis_name="subcore")


def sc_invert(pos_flat, n_tok, n_out):
    n = pos_flat.shape[0]
    per = n_out // SC_WORKERS
    chunk = n_tok
    assert n_out % SC_WORKERS == 0 and per % SC_LANES == 0
    assert n_tok % chunk == 0 and n % chunk == 0 and chunk % SC_LANES == 0
    cp = pltpu.CompilerParams()
    if "needs_layout_passes" in pltpu.CompilerParams.__dataclass_fields__:
        cp = dataclasses.replace(cp, needs_layout_passes=False)

    @functools.partial(
        pl.kernel, out_type=jax.ShapeDtypeStruct((n_out,), jnp.int32), mesh=_sc_mesh(),
        scratch_types=[pltpu.VMEM((chunk,), jnp.int32), pltpu.VMEM((per,), jnp.int32)],
        compiler_params=cp, name="sc_invert")
    def k(pos_hbm, src_hbm, pos_v, src_v):
        wid = lax.axis_index("subcore") * SC_CORES + lax.axis_index("core")
        lo = wid * per
        lane = lax.iota(jnp.int32, SC_LANES)

        @pl.loop(0, per, step=SC_LANES)
        def _(r):
            src_v[pl.ds(r, SC_LANES)] = lax.rem(lo + r + lane, n_tok)

        @pl.loop(0, n // chunk)
        def _(c):
            base = c * chunk
            pltpu.sync_copy(pos_hbm.at[pl.ds(base, chunk)], pos_v)
            tok0 = lax.rem(base, n_tok)

            @plsc.parallel_loop(0, chunk, step=SC_LANES, unroll=8)
            def _(r):
                p = pos_v[pl.ds(r, SC_LANES)] - lo
                mine = (p >= 0) & (p < per)
                plsc.store_scatter(src_v, [jnp.where(mine, p, 0)], tok0 + r + lane, mask=mine)

        pltpu.sync_copy(src_v, src_hbm.at[pl.ds(lo, per)])

    return k(pos_flat)


def sc_gather(x, idx):
    n = idx.shape[0]
    dim = x.shape[1]
    assert n % (SC_WINDOW * SC_WORKERS) == 0

    @functools.partial(
        pl.kernel, out_type=jax.ShapeDtypeStruct((n, dim), x.dtype), mesh=_sc_mesh(),
        scratch_types=[], name="sc_gather")
    def k(x_hbm, i_hbm, o_hbm):
        def body(i_vmem, o_vmem):
            pltpu.sync_copy(x_hbm.at[i_vmem.at[0]], o_vmem)

        pltpu.emit_pipeline(
            body, grid=(n // SC_WINDOW,),
            in_specs=[pl.BlockSpec((1, SC_WINDOW), index_map=lambda i: (i, 0))],
            out_specs=[pl.BlockSpec((SC_WINDOW, dim), index_map=lambda i: (i, 0))],
            core_axis_name=("core", "subcore"),
            dimension_semantics=(pltpu.PARALLEL,),
        )(i_hbm, o_hbm)

    return k(x, idx.reshape(n // SC_WINDOW, SC_WINDOW))


def _moe_gemm_kernel(ts_ref, tn_ref, x_hbm, wi_ref, wo_ref, o_hbm, wi_b, wo_b, xbuf, obuf, in_sem, out_sem,
                     *, tile_rows, n_tiles):
    e = pl.program_id(0)
    last = pl.num_programs(0) - 1
    t0 = ts_ref[e]
    n = tn_ref[e]
    n_used = ts_ref[last] + tn_ref[last]

    def x_copy(g, slot):
        rows = pl.ds(pl.multiple_of(g * tile_rows, tile_rows), tile_rows)
        return pltpu.make_async_copy(x_hbm.at[rows], xbuf.at[slot], in_sem.at[slot])

    def o_copy(g, slot):
        rows = pl.ds(pl.multiple_of(g * tile_rows, tile_rows), tile_rows)
        return pltpu.make_async_copy(obuf.at[slot], o_hbm.at[rows], out_sem.at[slot])

    @pl.when(e == 0)
    def _():
        for g0 in range(MOE_NBUF - 1):
            @pl.when(g0 < n_used)
            def _():
                x_copy(g0, g0).start()

    @pl.when(n > 0)
    def _():
        wi_b[...] = wi_ref[...].astype(BF16)
        wo_b[...] = wo_ref[...].astype(BF16)

    def tile(i, carry):
        g = t0 + i
        slot = lax.rem(g, MOE_NBUF)
        x_copy(g, slot).wait()
        ahead = g + (MOE_NBUF - 1)

        @pl.when(ahead < n_used)
        def _():
            x_copy(ahead, lax.rem(ahead, MOE_NBUF)).start()

        @pl.when(g >= MOE_NBUF)
        def _():
            o_copy(g - MOE_NBUF, slot).wait()

        rows = tile_rows // MOE_SUB
        half = xbuf.shape[2]
        xs = [_unpack_pairs(xbuf[slot, r * rows:(r + 1) * rows, :]) for r in range(MOE_SUB)]
        hus = [_dot(lo, wi_b[:half, :]) + _dot(hi, wi_b[half:, :]) for lo, hi in xs]
        acts = [(_silu(hu[:, :EXPERT_FF]) * hu[:, EXPERT_FF:]).astype(BF16) for hu in hus]
        outs = [_dot(act, wo_b[...]) for act in acts]
        for r, out in enumerate(outs):
            obuf[slot, r * rows:(r + 1) * rows, :] = _pack_pairs(out)
        o_copy(g, slot).start()
        return carry

    lax.fori_loop(0, n, tile, 0)

    @pl.when(e == last)
    def _():
        for back in range(MOE_NBUF, 0, -1):
            @pl.when(n_used >= back)
            def _():
                o_copy(n_used - back, lax.rem(n_used - back, MOE_NBUF)).wait()

        obuf[...] = jnp.zeros(obuf.shape, obuf.dtype)
        n_clear = n_tiles - n_used

        def clear(i, carry):
            slot = lax.rem(i, MOE_NBUF)

            @pl.when(i >= MOE_NBUF)
            def _():
                o_copy(n_used + i - MOE_NBUF, slot).wait()

            o_copy(n_used + i, slot).start()
            return carry

        lax.fori_loop(0, n_clear, clear, 0)
        for back in range(MOE_NBUF, 0, -1):
            @pl.when(n_clear >= back)
            def _():
                o_copy(n_tiles - back, lax.rem(n_clear - back, MOE_NBUF)).wait()


def moe_gemm(xs, tile_start, tile_count, exp_w_in, exp_w_out, l, tile_rows, n_tiles):
    P, half = xs.shape
    D = 2 * half
    assert P == n_tiles * tile_rows
    hbm = pl.BlockSpec(memory_space=pl.ANY)
    grid_spec = pltpu.PrefetchScalarGridSpec(
        num_scalar_prefetch=2,
        grid=(N_EXPERTS,),
        in_specs=[hbm,
                  pl.BlockSpec((None, None, D, 2 * EXPERT_FF), lambda e, ts, tn: (l, e, 0, 0)),
                  pl.BlockSpec((None, None, EXPERT_FF, D), lambda e, ts, tn: (l, e, 0, 0))],
        out_specs=hbm,
        scratch_shapes=[pltpu.VMEM((D, 2 * EXPERT_FF), BF16), pltpu.VMEM((EXPERT_FF, D), BF16),
                        pltpu.VMEM((MOE_NBUF, tile_rows, half), jnp.int32),
                        pltpu.VMEM((MOE_NBUF, tile_rows, half), jnp.int32),
                        pltpu.SemaphoreType.DMA((MOE_NBUF,)), pltpu.SemaphoreType.DMA((MOE_NBUF,))],
    )
    return pl.pallas_call(
        functools.partial(_moe_gemm_kernel, tile_rows=tile_rows, n_tiles=n_tiles),
        grid_spec=grid_spec,
        out_shape=jax.ShapeDtypeStruct((P, half), jnp.int32),
        compiler_params=_cparams(1),
        name="moe_gemm",
    )(tile_start, tile_count, xs, exp_w_in, exp_w_out)


def _moe_combine_kernel(*refs, final, norm_next, shared_kv, queries):
    it = iter(refs)
    y_ref, w_ref, h_ref, si_ref, so_ref, x_ref, g2_ref = (next(it) for _ in range(7))
    fg_ref = next(it) if final else None
    ng_ref, nsc_ref, nsh_ref = (next(it) for _ in range(3)) if norm_next else (None,) * 3
    kg_ref, wkv_ref, lg_ref, cos_ref, sin_ref = (next(it) for _ in range(5)) if shared_kv else (None,) * 5
    wdq_ref, qg_ref, wqt_ref, wqrt_ref, cost_ref, sint_ref = (next(it) for _ in range(6)) if queries else (None,) * 6
    o_ref = next(it)
    hn_ref = next(it) if norm_next and not queries else None
    lat_ref, kr_ref = (next(it), next(it)) if shared_kv else (None, None)
    qt_ref = next(it) if queries else None
    si_b, so_b = next(it), next(it)
    wkv_b = next(it) if shared_kv else None
    wdq_b, wqt_b, wqrt_b = (next(it), next(it), next(it)) if queries else (None,) * 3

    @pl.when(pl.program_id(0) == 0)
    def _():
        si_b[...] = si_ref[...].astype(BF16)
        so_b[...] = so_ref[...].astype(BF16)
        if shared_kv:
            wkv_b[...] = wkv_ref[...].astype(BF16)
        if queries:
            wdq_b[...] = wdq_ref[...].astype(BF16)
            wqt_b[...] = wqt_ref[...].astype(BF16)
            wqrt_b[...] = wqrt_ref[...].astype(BF16)

    bb, tt, D = x_ref.shape
    half = D // 2
    w = w_ref[...].T
    acc_lo = jnp.zeros((bb * tt, half), F32)
    acc_hi = jnp.zeros((bb * tt, half), F32)
    for k in range(TOP_K):
        lo, hi = _unpack_pairs(y_ref[k], F32)
        acc_lo = acc_lo + w[:, k:k + 1] * lo
        acc_hi = acc_hi + w[:, k:k + 1] * hi
    hlo, hhi = _unpack_pairs(h_ref[...])
    hu = _dot(hlo, si_b[:half, :]) + _dot(hhi, si_b[half:, :])
    act = (_silu(hu[:, :SHARED_FF]) * hu[:, SHARED_FF:]).astype(BF16)
    y = jnp.concatenate([acc_lo, acc_hi], axis=-1) + _dot(act, so_b[...])
    x_new = x_ref[...] + g2_ref[...] * y.reshape(bb, tt, D)
    o_ref[...] = _rms(x_new, fg_ref[...]) if final else x_new
    if norm_next:
        hn = (_rms(x_new, ng_ref[...]) * (1.0 + nsc_ref[...]) + nsh_ref[...]).astype(BF16)
        if queries:
            _queries_t(hn.reshape(bb * tt, D), wdq_b, qg_ref[...], wqt_b, wqrt_b, cost_ref[...], sint_ref[...],
                       qt_ref)
        else:
            hn_ref[...] = hn
    if shared_kv:
        xn = _rms(x_new, kg_ref[...]).reshape(bb * tt, D).astype(BF16)
        z = _dot(xn, wkv_b[...])
        lat_ref[...] = _rms(z[:, :KV_LORA], lg_ref[...]).reshape(bb, tt, KV_LORA)
        zr = z[:, KV_LORA:KV_LORA + MLA_ROPE].reshape(bb, tt, MLA_ROPE)
        zq = z[:, KV_LORA + 128:KV_LORA + 128 + MLA_ROPE].reshape(bb, tt, MLA_ROPE)
        kr_ref[...] = zr * cos_ref[...] + zq * sin_ref[...]


def moe_combine(y8, w8, hp, sh_w_in, sh_w_out, x, mod, gate_idx, l, row0, final_g=None, next_norm=None,
                shared_kv=None, queries=None, rows=512):
    B, T, D = x.shape
    half = D // 2
    bb, tt, nblk, ij = _row_blocks(B, T, rows)
    M = bb * tt
    assert row0 % M == 0
    off = row0 // M
    xspec = pl.BlockSpec((bb, tt, D), lambda i: ij(i) + (0,))
    in_specs = [pl.BlockSpec((TOP_K, M, half), lambda i: (0, off + i, 0)),
                pl.BlockSpec((TOP_K, M), lambda i: (0, off + i)),
                pl.BlockSpec((M, half), lambda i: (off + i, 0)),
                pl.BlockSpec((None, D, 2 * SHARED_FF), lambda i: (l, 0, 0)),
                pl.BlockSpec((None, SHARED_FF, D), lambda i: (l, 0, 0)),
                xspec,
                pl.BlockSpec((bb, 1, D), lambda i: (ij(i)[0], 0, gate_idx))]
    args = [y8, w8, hp, sh_w_in, sh_w_out, x, mod]
    out_specs = xspec
    out_shape = jax.ShapeDtypeStruct((B, T, D), F32)
    if final_g is not None:
        assert next_norm is None
        in_specs.append(pl.BlockSpec((1, D), lambda i: (0, 0)))
        args.append(final_g.reshape(1, D))
    if next_norm is not None:
        gain, mod_next, sc_idx, sh_idx = next_norm
        in_specs += [pl.BlockSpec((1, D), lambda i: (0, 0)),
                     pl.BlockSpec((bb, 1, D), lambda i: (ij(i)[0], 0, sc_idx)),
                     pl.BlockSpec((bb, 1, D), lambda i: (ij(i)[0], 0, sh_idx))]
        args += [gain.reshape(1, D), mod_next, mod_next]
        if queries is None:
            out_specs = [xspec, xspec]
            out_shape = [out_shape, jax.ShapeDtypeStruct((B, T, D), BF16)]
    scratch = [pltpu.VMEM((D, 2 * SHARED_FF), BF16), pltpu.VMEM((SHARED_FF, D), BF16)]
    if shared_kv is not None:
        kv_in_g, w_kv, kv_lat_g, cos32, sin32 = shared_kv
        tspec = pl.BlockSpec((tt, MLA_ROPE), lambda i: (ij(i)[1], 0))
        in_specs += [pl.BlockSpec((1, D), lambda i: (0, 0)),
                     pl.BlockSpec(w_kv.shape, lambda i: (0, 0)),
                     pl.BlockSpec((1, KV_LORA), lambda i: (0, 0)),
                     tspec, tspec]
        args += [kv_in_g.reshape(1, D), w_kv, kv_lat_g.reshape(1, KV_LORA), cos32, sin32]
        out_specs = list(out_specs) if isinstance(out_specs, list) else [out_specs]
        out_shape = list(out_shape) if isinstance(out_shape, list) else [out_shape]
        out_specs += [pl.BlockSpec((bb, tt, KV_LORA), lambda i: ij(i) + (0,)),
                      pl.BlockSpec((bb, tt, MLA_ROPE), lambda i: ij(i) + (0,))]
        out_shape += [jax.ShapeDtypeStruct((B, T, KV_LORA), F32), jax.ShapeDtypeStruct((B, T, MLA_ROPE), F32)]
        scratch.append(pltpu.VMEM(w_kv.shape, BF16))
    if queries is not None:
        assert next_norm is not None and bb == 1
        w_dq, q_norm_g, wq_t, wqr_t, bi, cos_t, sin_t = queries
        NQ, NR = wq_t.shape[1], wqr_t.shape[1]
        tspec_t = pl.BlockSpec((MLA_ROPE, tt), lambda i: (0, ij(i)[1]))
        in_specs += [pl.BlockSpec((None, D, Q_LORA), lambda i: (bi, 0, 0)),
                     pl.BlockSpec((None, 1, Q_LORA), lambda i: (bi, 0, 0)),
                     pl.BlockSpec((None, NQ, Q_LORA), lambda i: (bi, 0, 0)),
                     pl.BlockSpec((None, NR, Q_LORA), lambda i: (bi, 0, 0)),
                     tspec_t, tspec_t]
        args += [w_dq, q_norm_g.reshape(-1, 1, Q_LORA), wq_t, wqr_t, cos_t, sin_t]
        out_specs = list(out_specs) if isinstance(out_specs, list) else [out_specs]
        out_shape = list(out_shape) if isinstance(out_shape, list) else [out_shape]
        out_specs.append(pl.BlockSpec((1, NQ, tt), lambda i: (ij(i)[0], 0, ij(i)[1])))
        out_shape.append(jax.ShapeDtypeStruct((B, NQ, T), BF16))
        scratch += [pltpu.VMEM((D, Q_LORA), BF16), pltpu.VMEM((NQ, Q_LORA), BF16), pltpu.VMEM((NR, Q_LORA), BF16)]
    return pl.pallas_call(
        functools.partial(_moe_combine_kernel, final=final_g is not None, norm_next=next_norm is not None,
                          shared_kv=shared_kv is not None, queries=queries is not None),
        grid=(nblk,),
        in_specs=in_specs,
        out_specs=out_specs,
        out_shape=out_shape,
        scratch_shapes=scratch,
        compiler_params=_cparams(1),
        name="moe_combine",
    )(*args)


def _kv_expand_kernel(lat_ref, kr_ref, wk_ref, ek_ref, wvt_ref, ones_ref, k_ref, vt_ref):
    lat = lat_ref[0].astype(BF16)
    kr = kr_ref[0].astype(BF16)
    k = _dot(lat, wk_ref[...].astype(BF16)) + _dot(kr, ek_ref[...].astype(BF16))
    k_ref[0] = k.astype(k_ref.dtype)
    vt = _dot_nt(wvt_ref[...].astype(BF16), lat) + ones_ref[...]
    vt_ref[0] = vt.astype(vt_ref.dtype)


def kv_expand(lat, kr, wk_pad, ek, wvt_ext, ones_col, rows=512):
    B, T, _ = lat.shape
    tt = rows
    NK, NVT = wk_pad.shape[1], wvt_ext.shape[0]

    def full(a):
        return pl.BlockSpec(a.shape, lambda b, t: (0, 0))

    def rowspec(n):
        return pl.BlockSpec((1, tt, n), lambda b, t: (b, t, 0))

    return pl.pallas_call(
        _kv_expand_kernel,
        grid=(B, T // tt),
        in_specs=[rowspec(KV_LORA), rowspec(MLA_ROPE), full(wk_pad), full(ek), full(wvt_ext), full(ones_col)],
        out_specs=[rowspec(NK), pl.BlockSpec((1, NVT, tt), lambda b, t: (b, 0, t))],
        out_shape=[jax.ShapeDtypeStruct((B, T, NK), BF16), jax.ShapeDtypeStruct((B, NVT, T), BF16)],
        compiler_params=_cparams(2),
        name="kv_expand",
    )(lat, kr, wk_pad, ek, wvt_ext, ones_col)


def _query_kernel(h_ref, wdq_ref, qg_ref, wq_ref, wqr_ref, c_ref, s_ref, q_ref, wdq_b, wq_b, wqr_b):
    @pl.when(pl.program_id(0) == 0)
    def _():
        wdq_b[...] = wdq_ref[...].astype(BF16)
        wq_b[...] = wq_ref[...].astype(BF16)
        wqr_b[...] = wqr_ref[...].astype(BF16)

    bb, tt, D = h_ref.shape
    h = h_ref[...].reshape(bb * tt, D)
    cq = _rms(_dot(h, wdq_b[...]), qg_ref[...]).astype(BF16)
    q1 = _dot(cq, wq_b[...]).reshape(bb, tt, -1)
    q2 = _dot(cq, wqr_b[...]).reshape(bb, tt, -1)
    c = c_ref[...]
    s = s_ref[...]
    for hd in range(MLA_HEADS):
        sl = slice(hd * HEAD_PAD, (hd + 1) * HEAD_PAD)
        q_ref[:, :, sl] = (q1[:, :, sl] * c + q2[:, :, sl] * s).astype(q_ref.dtype)


def mla_queries(h, w_dq, q_norm_g, wq_pad, wq_rot, l, c128, s128, rows=512):
    B, T, D = h.shape
    bb, tt, nblk, ij = _row_blocks(B, T, rows)
    NQ = wq_pad.shape[-1]
    tspec = pl.BlockSpec((tt, HEAD_PAD), lambda i: (ij(i)[1], 0))
    return pl.pallas_call(
        _query_kernel,
        grid=(nblk,),
        in_specs=[pl.BlockSpec((bb, tt, D), lambda i: ij(i) + (0,)),
                  pl.BlockSpec((None, D, Q_LORA), lambda i: (l, 0, 0)),
                  pl.BlockSpec((None, 1, Q_LORA), lambda i: (l, 0, 0)),
                  pl.BlockSpec((None, Q_LORA, NQ), lambda i: (l, 0, 0)),
                  pl.BlockSpec((None, Q_LORA, NQ), lambda i: (l, 0, 0)),
                  tspec, tspec],
        out_specs=pl.BlockSpec((bb, tt, NQ), lambda i: ij(i) + (0,)),
        out_shape=jax.ShapeDtypeStruct((B, T, NQ), BF16),
        scratch_shapes=[pltpu.VMEM((D, Q_LORA), BF16), pltpu.VMEM((Q_LORA, NQ), BF16),
                        pltpu.VMEM((Q_LORA, NQ), BF16)],
        compiler_params=_cparams(1),
        name="mla_queries",
    )(h, w_dq, q_norm_g.reshape(-1, 1, Q_LORA), wq_pad, wq_rot, c128, s128)


def _queries_t(h, wdq_b, qg, wqt_b, wqrt_b, cos, sin, qt_ref):
    cq = _rms(_dot(h, wdq_b[...]), qg).astype(BF16)
    q1 = _dot_nt(wqt_b[...], cq)
    q2 = _dot_nt(wqrt_b[...], cq)
    pad = jnp.zeros((HEAD_PAD - MLA_NOPE - MLA_ROPE, q1.shape[1]), qt_ref.dtype)
    for hd in range(MLA_HEADS):
        r0 = hd * HEAD_PAD
        rope = (q1[r0 + MLA_NOPE:r0 + MLA_NOPE + MLA_ROPE] * cos
                + q2[hd * MLA_ROPE:(hd + 1) * MLA_ROPE] * sin)
        qt_ref[0, r0:r0 + MLA_NOPE, :] = (q1[r0:r0 + MLA_NOPE] * Q_PRESCALE).astype(qt_ref.dtype)
        qt_ref[0, r0 + MLA_NOPE:r0 + MLA_NOPE + MLA_ROPE, :] = rope.astype(qt_ref.dtype)
        qt_ref[0, r0 + MLA_NOPE + MLA_ROPE:r0 + HEAD_PAD, :] = pad


def _query_t_kernel(h_ref, wdq_ref, qg_ref, wqt_ref, wqrt_ref, cos_ref, sin_ref, qt_ref, wdq_b, wqt_b, wqrt_b):
    @pl.when((pl.program_id(0) == 0) & (pl.program_id(1) == 0))
    def _():
        wdq_b[...] = wdq_ref[...].astype(BF16)
        wqt_b[...] = wqt_ref[...].astype(BF16)
        wqrt_b[...] = wqrt_ref[...].astype(BF16)

    _queries_t(h_ref[0], wdq_b, qg_ref[...], wqt_b, wqrt_b, cos_ref[...], sin_ref[...], qt_ref)


def mla_queries_t(h, w_dq, q_norm_g, wq_t, wqr_t, l, cos_t, sin_t, rows=512):
    B, T, D = h.shape
    tt = rows
    NQ = wq_t.shape[1]
    NR = wqr_t.shape[1]
    tspec = pl.BlockSpec((MLA_ROPE, tt), lambda b, t: (0, t))
    return pl.pallas_call(
        _query_t_kernel,
        grid=(B, T // tt),
        in_specs=[pl.BlockSpec((1, tt, D), lambda b, t: (b, t, 0)),
                  pl.BlockSpec((None, D, Q_LORA), lambda b, t: (l, 0, 0)),
                  pl.BlockSpec((None, 1, Q_LORA), lambda b, t: (l, 0, 0)),
                  pl.BlockSpec((None, NQ, Q_LORA), lambda b, t: (l, 0, 0)),
                  pl.BlockSpec((None, NR, Q_LORA), lambda b, t: (l, 0, 0)),
                  tspec, tspec],
        out_specs=pl.BlockSpec((1, NQ, tt), lambda b, t: (b, 0, t)),
        out_shape=jax.ShapeDtypeStruct((B, NQ, T), BF16),
        scratch_shapes=[pltpu.VMEM((D, Q_LORA), BF16), pltpu.VMEM((NQ, Q_LORA), BF16),
                        pltpu.VMEM((NR, Q_LORA), BF16)],
        compiler_params=_cparams(2),
        name="mla_queries_t",
    )(h, w_dq, q_norm_g.reshape(-1, 1, Q_LORA), wq_t, wqr_t, cos_t, sin_t)


def _attn_prompt_kernel(qi_tab, ki_tab, qt_ref, k_ref, vt_ref, o_ref, *scratch, tq, tk):
    H = MLA_HEADS
    m_refs, l_refs, acc_refs = scratch[:H], scratch[H:2 * H], scratch[2 * H:]
    p_id = pl.program_id(1)
    qi = qi_tab[p_id]
    ki = ki_tab[p_id]

    @pl.when(ki == 0)
    def _():
        for hd in range(H):
            m_refs[hd][...] = jnp.full(m_refs[hd].shape, NEG_INF, F32)
            l_refs[hd][...] = jnp.zeros(l_refs[hd].shape, F32)
            acc_refs[hd][...] = jnp.zeros(acc_refs[hd].shape, F32)

    def block(masked):
        if masked:
            kchunk = (ki * tk + lax.broadcasted_iota(jnp.int32, (tk, tq), 0)) // CHUNK
            qchunk = (qi * tq + lax.broadcasted_iota(jnp.int32, (tk, tq), 1)) // CHUNK
            mask = kchunk <= qchunk
        def scores(hd):
            sl = slice(hd * HEAD_PAD, (hd + 1) * HEAD_PAD)
            return _dot(k_ref[0, :, sl], qt_ref[0, sl, :])

        pending = [scores(hd) for hd in range(ATTN_LOOKAHEAD)]
        for hd in range(H):
            if hd + ATTN_LOOKAHEAD < H:
                pending.append(scores(hd + ATTN_LOOKAHEAD))
            s = pending.pop(0)
            if masked:
                s = jnp.where(mask, s, NEG_INF)
            m_prev = m_refs[hd][...]
            m_new = jnp.maximum(m_prev, jnp.max(s, axis=0, keepdims=True))
            a = jnp.exp2(m_prev - m_new)
            p = jnp.exp2(s - m_new).astype(BF16)
            pv = _dot(vt_ref[0, hd * V_ROWS:(hd + 1) * V_ROWS, :], p)
            acc_refs[hd][...] = a * acc_refs[hd][...] + pv[:MLA_V]
            l_refs[hd][...] = a * l_refs[hd][...] + pv[MLA_V:MLA_V + 1]
            m_refs[hd][...] = m_new

    @pl.when(ki < qi)
    def _():
        block(False)

    @pl.when(ki == qi)
    def _():
        block(True)
        o_t = jnp.concatenate([acc_refs[hd][...] / l_refs[hd][...] for hd in range(H)], axis=0)
        o_ref[0] = o_t.T.astype(o_ref.dtype)


def attn_prompt(qt, k, vt, tq=256):
    B, NQ, T = qt.shape
    NVT = vt.shape[1]
    NV = MLA_HEADS * MLA_V
    tk = tq
    assert tq % CHUNK == 0
    nq = T // tq
    pairs = [(a, b) for a in range(nq) for b in range(a + 1)]
    qi_tab = jnp.asarray([a for a, _ in pairs], jnp.int32)
    ki_tab = jnp.asarray([b for _, b in pairs], jnp.int32)
    grid_spec = pltpu.PrefetchScalarGridSpec(
        num_scalar_prefetch=2,
        grid=(B, len(pairs)),
        in_specs=[pl.BlockSpec((1, NQ, tq), lambda b, p, qt, kt: (b, 0, qt[p])),
                  pl.BlockSpec((1, tk, NQ), lambda b, p, qt, kt: (b, kt[p], 0)),
                  pl.BlockSpec((1, NVT, tk), lambda b, p, qt, kt: (b, 0, kt[p]))],
        out_specs=pl.BlockSpec((1, tq, NV), lambda b, p, qt, kt: (b, qt[p], 0)),
        scratch_shapes=([pltpu.VMEM((1, tq), F32)] * (2 * MLA_HEADS)
                        + [pltpu.VMEM((MLA_V, tq), F32)] * MLA_HEADS),
    )
    return pl.pallas_call(
        functools.partial(_attn_prompt_kernel, tq=tq, tk=tk),
        grid_spec=grid_spec,
        out_shape=jax.ShapeDtypeStruct((B, T, NV), BF16),
        compiler_params=_cparams(2),
        name="attn_prompt",
    )(qi_tab, ki_tab, qt, k, vt)


def _absorb_kernel(q_ref, m_ref, o_ref):
    o_ref[...] = _dot(q_ref[...], m_ref[...].astype(BF16)).astype(o_ref.dtype)


def absorb_queries(q2d, m_abs):
    N = q2d.shape[0]
    H, _, W = m_abs.shape
    return pl.pallas_call(
        _absorb_kernel,
        grid=(H,),
        in_specs=[pl.BlockSpec((N, HEAD_PAD), lambda h: (0, h)),
                  pl.BlockSpec((None, HEAD_PAD, W), lambda h: (h, 0, 0))],
        out_specs=pl.BlockSpec((None, N, W), lambda h: (h, 0, 0)),
        out_shape=jax.ShapeDtypeStruct((H, N, W), BF16),
        compiler_params=_cparams(1),
        name="absorb_queries",
    )(q2d, m_abs)


def _attn_sample_kernel(q_ref, lat_ref, kr_ref, nlat_ref, nkr_ref, o_ref, m_ref, l_ref, acc_ref):
    kb = pl.program_id(1)
    H, Q, W = q_ref.shape
    q = q_ref[...].reshape(H * Q, W)
    q_lat = q[:, :KV_LORA]
    q_rope = q[:, KV_LORA:KV_LORA + MLA_ROPE]

    def update(lat_tile, kr_tile, n_sub, kr_transposed):
        sub = lat_tile.shape[0] // n_sub
        m_prev = m_ref[...]
        m_new = m_prev
        lats, ss = [], []
        for j in range(n_sub):
            lat = lat_tile[j * sub:(j + 1) * sub, :].astype(BF16)
            if kr_transposed:
                s = _dot_nt(q_lat, lat) + _dot(q_rope, kr_tile[:, j * sub:(j + 1) * sub].astype(BF16))
            else:
                s = _dot_nt(q_lat, lat) + _dot_nt(q_rope, kr_tile[j * sub:(j + 1) * sub, :].astype(BF16))
            m_new = jnp.maximum(m_new, jnp.max(s, axis=-1, keepdims=True))
            lats.append(lat)
            ss.append(s)
        a = jnp.exp2(m_prev - m_new)
        pv = psum = None
        for s, lat in zip(ss, lats):
            p = jnp.exp2(s - m_new[:, :1])
            pv_j = _dot(p.astype(BF16), lat)
            ps_j = jnp.sum(p, axis=-1, keepdims=True)
            pv = pv_j if pv is None else pv + pv_j
            psum = ps_j if psum is None else psum + ps_j
        l_ref[...] = a * l_ref[...] + psum
        m_ref[...] = m_new
        acc_ref[...] = jnp.concatenate([a, a], axis=-1) * acc_ref[...] + pv

    @pl.when(kb == 0)
    def _():
        m_ref[...] = jnp.full_like(m_ref, NEG_INF)
        l_ref[...] = jnp.zeros_like(l_ref)
        acc_ref[...] = jnp.zeros_like(acc_ref)
        update(nlat_ref[0], nkr_ref[0], 1, False)

    update(lat_ref[0], kr_ref[0], SAMPLE_KEY_SUB, True)

    @pl.when(kb == pl.num_programs(1) - 1)
    def _():
        lsum = l_ref[...]
        o = acc_ref[...] / jnp.concatenate([lsum, lsum], axis=-1)
        o_ref[...] = o.reshape(H, Q, KV_LORA).astype(o_ref.dtype)


def attn_sample(q_abs, cache_lat, cache_kr_t, new_lat, new_kr, tk=4096):
    H, N, W = q_abs.shape
    B, P, _ = cache_lat.shape
    Q = new_lat.shape[1]
    qpos = P + np.arange(Q)
    kpos = np.arange(P + Q)
    assert bool(np.all((kpos // CHUNK)[None, :] <= (qpos // CHUNK)[:, None]))
    return pl.pallas_call(
        _attn_sample_kernel,
        grid=(B, P // tk),
        in_specs=[pl.BlockSpec((H, Q, W), lambda b, kb: (0, b, 0)),
                  pl.BlockSpec((1, tk, KV_LORA), lambda b, kb: (b, kb, 0)),
                  pl.BlockSpec((1, MLA_ROPE, tk), lambda b, kb: (b, 0, kb)),
                  pl.BlockSpec((1, Q, KV_LORA), lambda b, kb: (b, 0, 0)),
                  pl.BlockSpec((1, Q, MLA_ROPE), lambda b, kb: (b, 0, 0))],
        out_specs=pl.BlockSpec((H, Q, KV_LORA), lambda b, kb: (0, b, 0)),
        out_shape=jax.ShapeDtypeStruct((H, N, KV_LORA), BF16),
        scratch_shapes=[pltpu.VMEM((H * Q, 128), F32), pltpu.VMEM((H * Q, 128), F32),
                        pltpu.VMEM((H * Q, KV_LORA), F32)],
        compiler_params=_cparams(2),
        name="attn_sample",
    )(q_abs, cache_lat, cache_kr_t, new_lat, new_kr)


def _unabsorb_kernel(o_ref, w_ref, out_ref):
    out_ref[...] = (_dot(o_ref[0], w_ref[0].astype(BF16))
                    + _dot(o_ref[1], w_ref[1].astype(BF16))).astype(out_ref.dtype)


def unabsorb(o_lat, wuv_pad):
    H, N, R = o_lat.shape
    return pl.pallas_call(
        _unabsorb_kernel,
        grid=(H // 2,),
        in_specs=[pl.BlockSpec((2, N, R), lambda p: (p, 0, 0)),
                  pl.BlockSpec((2, R, 128), lambda p: (p, 0, 0))],
        out_specs=pl.BlockSpec((N, 128), lambda p: (0, p)),
        out_shape=jax.ShapeDtypeStruct((N, (H // 2) * 128), BF16),
        compiler_params=_cparams(1),
        name="unabsorb",
    )(o_lat, wuv_pad)


def _rope_tables(pos):
    half = MLA_ROPE // 2
    inv = 1.0 / (ROPE_THETA ** (np.arange(half, dtype=np.float64) * 2.0 / MLA_ROPE))
    ang = np.asarray(pos, np.float64)[:, None] * inv[None, :]
    cos = np.concatenate([np.cos(ang), np.cos(ang)], axis=-1)
    sin = np.concatenate([np.sin(ang), np.sin(ang)], axis=-1)
    T = cos.shape[0]
    c128 = np.zeros((T, HEAD_PAD)); s128 = np.zeros((T, HEAD_PAD))
    c128[:, :MLA_NOPE] = 1.0
    c128[:, MLA_NOPE:MLA_NOPE + MLA_ROPE] = cos
    s128[:, MLA_NOPE:MLA_NOPE + MLA_ROPE] = sin
    return dict(cos32=jnp.asarray(cos, F32), sin32=jnp.asarray(sin, F32),
                c128=jnp.asarray(c128 * Q_PRESCALE, F32), s128=jnp.asarray(s128 * Q_PRESCALE, F32),
                cos_t=jnp.asarray(cos.T * Q_PRESCALE, F32), sin_t=jnp.asarray(sin.T * Q_PRESCALE, F32))


def _rot_half_cols(w):
    half = w.shape[-1] // 2
    return jnp.concatenate([-w[..., half:], w[..., :half]], axis=-1)


def _prep_weights(w_dkv, w_uk, w_uv, w_uq, router_w):
    D = D_MODEL
    w_lat, w_rope = w_dkv[:, :KV_LORA], w_dkv[:, KV_LORA:]
    pad96 = jnp.zeros((D, 128 - MLA_ROPE), F32)
    w_kv = jnp.concatenate([w_lat, w_rope, pad96, _rot_half_cols(w_rope), pad96], axis=-1)

    zpad = HEAD_PAD - MLA_NOPE
    wk_pad = jnp.pad(w_uk, ((0, 0), (0, 0), (0, zpad))).reshape(KV_LORA, MLA_HEADS * HEAD_PAD)
    ek = jnp.zeros((MLA_ROPE, MLA_HEADS, HEAD_PAD), F32)
    ek = ek.at[:, :, MLA_NOPE:MLA_NOPE + MLA_ROPE].set(
        jnp.broadcast_to(jnp.eye(MLA_ROPE, dtype=F32)[:, None, :], (MLA_ROPE, MLA_HEADS, MLA_ROPE)))
    ek = ek.reshape(MLA_ROPE, MLA_HEADS * HEAD_PAD)
    wvt = jnp.transpose(w_uv, (1, 2, 0))
    wvt_ext = jnp.pad(wvt, ((0, 0), (0, V_ROWS - MLA_V), (0, 0))).reshape(MLA_HEADS * V_ROWS, KV_LORA)
    ones_col = jnp.tile((jnp.arange(V_ROWS) >= MLA_V).astype(F32), MLA_HEADS).reshape(-1, 1)

    nb = w_uq.shape[0]
    qn, qr = w_uq[..., :MLA_NOPE], w_uq[..., MLA_NOPE:]
    z32 = jnp.zeros(qr.shape[:-1] + (HEAD_PAD - MLA_NOPE - MLA_ROPE,), F32)
    wq_pad = jnp.concatenate([qn, qr, z32], axis=-1).reshape(nb, Q_LORA, MLA_HEADS * HEAD_PAD)
    wq_rot = jnp.concatenate([jnp.zeros_like(qn), _rot_half_cols(qr), z32], axis=-1)
    wq_rot = wq_rot.reshape(nb, Q_LORA, MLA_HEADS * HEAD_PAD)
    wq_t = jnp.transpose(wq_pad, (0, 2, 1))
    wqr_t = jnp.transpose(_rot_half_cols(qr).reshape(nb, Q_LORA, MLA_HEADS * MLA_ROPE), (0, 2, 1))

    m_abs = jnp.zeros((MLA_HEADS, HEAD_PAD, KV_LORA + 128), F32)
    m_abs = m_abs.at[:, :MLA_NOPE, :KV_LORA].set(jnp.transpose(w_uk, (1, 2, 0)))
    m_abs = m_abs.at[:, MLA_NOPE:MLA_NOPE + MLA_ROPE, KV_LORA:KV_LORA + MLA_ROPE].set(
        jnp.broadcast_to(jnp.eye(MLA_ROPE, dtype=F32), (MLA_HEADS, MLA_ROPE, MLA_ROPE)))

    wuv_h = jnp.transpose(w_uv, (1, 0, 2))
    even = jnp.pad(wuv_h, ((0, 0), (0, 0), (0, 64)))
    odd = jnp.pad(wuv_h, ((0, 0), (0, 0), (64, 0)))
    wuv_pad = jnp.where((jnp.arange(MLA_HEADS) % 2 == 0)[:, None, None], even, odd)

    rw_t = jnp.transpose(router_w, (0, 2, 1))
    return dict(w_kv=w_kv, wk_pad=wk_pad, ek=ek, wvt_ext=wvt_ext, ones_col=ones_col, wq_pad=wq_pad, wq_rot=wq_rot, wq_t=wq_t, wqr_t=wqr_t,
                m_abs=m_abs, wuv_pad=wuv_pad, rw_t=rw_t)


def _mixer(st, l, P, W, packed):
    rows_kw = dict(rows_total=packed["total"], row0=packed["row0"], rows_buf=packed["buf"])
    x, m = st["x"], st["mod"][l]
    B, T, _ = x.shape
    n_a = P["hg_w_in"].shape[0]
    norm2 = (P["norm2_g"][l], 4, 3)
    if l < n_a:
        zf, zqig = hgrn_proj(x, P["norm1_g"][l], m, 1, 0, P["hg_w_in"], l)
        s0 = None if st["hg_state"] is None else st["hg_state"][l]
        o, s_new = gla(zqig, zf, st["lbs"][l], P["hg_onorm_g"][l], s0)
        st["hg_new"].append(s_new)
        st["x"], packed["buf"] = linear(o, P["hg_w_out"], l, F32, x=x, mod=m, gate_idx=2, next_norm=norm2,
                                        **rows_kw)
    else:
        bi = l - n_a
        h = st.pop("h_next", None)
        qt = st.pop("qt_next", None)
        if h is None and qt is None:
            h = norm_mod(x, P["norm1_g"][l], m, sc_idx=1, sh_idx=0)
        if st["past_lat"] is None:
            if qt is None:
                qt = mla_queries_t(h, P["w_dq"], P["q_norm_g"], W["wq_t"], W["wqr_t"], bi, st["cos_t"],
                                   st["sin_t"])
            o = attn_prompt(qt, st["k_all"], st["v_all"])
        else:
            q = mla_queries(h, P["w_dq"], P["q_norm_g"], W["wq_pad"], W["wq_rot"], bi, st["c128"], st["s128"])
            q_abs = absorb_queries(q.reshape(B * T, -1), W["m_abs"])
            o_lat = attn_sample(q_abs, st["past_lat"], st["past_kr"], st["lat"], st["kr"])
            o = unabsorb(o_lat, W["wuv_pad"]).reshape(B, T, -1)
        st["x"], packed["buf"] = linear(o, P["w_o"], bi, F32, x=x, mod=m, gate_idx=2, next_norm=norm2, **rows_kw)
    packed["row0"] += B * T


def _moe(groups, hp, l, P, W):
    n_tok = hp.shape[0]
    n_tiles = (TOP_K * n_tok) // MOE_TILE + N_EXPERTS
    pos, w8, tile_start, tile_count = route(hp, W["rw_t"], P["router_bias"], l, MOE_TILE)
    pos_flat = pos.reshape(-1)
    src = sc_invert(pos_flat, n_tok, n_tiles * MOE_TILE)
    xs = sc_gather(hp, src)
    out = moe_gemm(xs, tile_start[:, 0], tile_count[:, 0], P["exp_w_in"], P["exp_w_out"], l,
                   MOE_TILE, n_tiles)
    y8 = sc_gather(out, pos_flat).reshape(TOP_K, n_tok, -1)
    last = l == P["norm1_g"].shape[0] - 1
    with_kv = l == P["hg_w_in"].shape[0] - 1
    next_is_mla = not last and l + 1 >= P["hg_w_in"].shape[0]
    row0 = 0
    for st in groups:
        B, T, _ = st["x"].shape
        with_q = next_is_mla and st["past_lat"] is None
        outs = moe_combine(
            y8, w8, hp, P["sh_w_in"], P["sh_w_out"], st["x"], st["mod"][l], 5, l, row0,
            final_g=P["final_g"] if last else None,
            next_norm=(P["norm1_g"][l + 1], st["mod"][l + 1], 1, 0) if next_is_mla else None,
            shared_kv=(P["kv_in_g"], W["w_kv"], P["kv_lat_g"], st["cos32"], st["sin32"]) if with_kv else None,
            queries=(P["w_dq"], P["q_norm_g"], W["wq_t"], W["wqr_t"], l + 1 - P["hg_w_in"].shape[0],
                     st["cos_t"], st["sin_t"]) if with_q else None,
            rows=256 if with_q else 512)
        outs = list(outs) if isinstance(outs, (list, tuple)) else [outs]
        st["x"] = outs.pop(0)
        if next_is_mla and not with_q:
            st["h_next"] = outs.pop(0)
        if with_kv:
            st["lat"], st["kr"] = outs.pop(0), outs.pop(0)
        if with_q:
            st["qt_next"] = outs.pop(0)
        row0 += B * T


def _group_state(x, mod, pos, hg_state, past_lat, past_kr, lbs):
    return dict(x=x, mod=mod, hg_state=hg_state, past_lat=past_lat, past_kr=past_kr, lbs=lbs,
                **_rope_tables(pos), hg_new=[],
                lat=None, kr=None, k_all=None, v_all=None)


def kernel(x_prompt, x_sample, state_hgrn, cache_mla_latent, cache_mla_krope, c_prompt, c_sample, ada_w, ada_b, norm1_g, norm2_g, hg_w_in, hg_lb_logits, hg_onorm_g, hg_w_out, kv_in_g, w_dkv, kv_lat_g, w_uk, w_uv, w_dq, q_norm_g, w_uq, w_o, router_w, router_bias, exp_w_in, exp_w_out, sh_w_in, sh_w_out, final_g):
    Bp, Sp, _ = x_prompt.shape
    Bs, Ss, _ = x_sample.shape
    past = cache_mla_latent.shape[1]
    P = dict(norm1_g=norm1_g, norm2_g=norm2_g, hg_w_in=hg_w_in, hg_lb_logits=hg_lb_logits,
             hg_onorm_g=hg_onorm_g, hg_w_out=hg_w_out, kv_in_g=kv_in_g, kv_lat_g=kv_lat_g,
             w_dq=w_dq, q_norm_g=q_norm_g, w_o=w_o, router_bias=router_bias,
             exp_w_in=exp_w_in, exp_w_out=exp_w_out, sh_w_in=sh_w_in, sh_w_out=sh_w_out, final_g=final_g)
    W = _prep_weights(w_dkv, w_uk, w_uv, w_uq, router_w)
    mod = ada_mod(jnp.concatenate([c_prompt, c_sample], axis=0), ada_w, ada_b)
    lbs = jnp.cumsum(jax.nn.softmax(hg_lb_logits.astype(F32), axis=0), axis=0)
    gp = _group_state(x_prompt, mod[:, :Bp, None, :], np.arange(Sp), None, None, None, lbs)
    gs = _group_state(x_sample, mod[:, Bp:, None, :], past + np.arange(Ss), state_hgrn,
                      cache_mla_latent, jnp.transpose(cache_mla_krope, (0, 2, 1)), lbs)
    groups = [gp, gs]
    n_tok = sum(st["x"].shape[0] * st["x"].shape[1] for st in groups)
    n_a = hg_w_in.shape[0]
    for l in range(norm1_g.shape[0]):
        packed = dict(total=n_tok, row0=0, buf=None)
        for st in groups:
            _mixer(st, l, P, W, packed)
        _moe(groups, packed["buf"], l, P, W)
        if l == n_a - 1:
            gp["k_all"], gp["v_all"] = kv_expand(gp["lat"], gp["kr"], W["wk_pad"], W["ek"], W["wvt_ext"],
                                                 W["ones_col"])
    return (gp["x"], gs["x"], jnp.stack(gp["hg_new"], axis=0), jnp.stack(gs["hg_new"], axis=0),
            gp["lat"], gp["kr"], gs["lat"], gs["kr"])
```

```python
import dataclasses
import functools

import numpy as np
import jax
import jax.numpy as jnp
from jax import lax
from jax.experimental import pallas as pl
from jax.experimental.pallas import tpu as pltpu
from jax.experimental.pallas import tpu_sc as plsc

F32 = jnp.float32
BF16 = jnp.bfloat16

D_MODEL = 1024
CHUNK = 64
HG_HEADS = 8
HG_DK = 128
HG_DV = 128
MLA_HEADS = 16
MLA_NOPE = 64
MLA_ROPE = 32
MLA_V = 64
Q_LORA = 384
KV_LORA = 256
ROPE_THETA = 10000.0
N_EXPERTS = 64
TOP_K = 8
N_GROUPS = 8
TOPK_GROUPS = 4
EXPERT_FF = 256
SHARED_FF = 256
ROUTED_SCALE = 2.5
EPS = 1e-6

LANES = 128
HEAD_PAD = LANES
SAMPLE_KEY_SUB = 8
ATTN_LOOKAHEAD = 6
V_ROWS = MLA_V + 16
QK_SCALE = (MLA_NOPE + MLA_ROPE) ** -0.5
Q_PRESCALE = QK_SCALE * float(np.log2(np.e))
VMEM_LIMIT = 56 * 1024 * 1024
NEG_INF = float("-inf")
SC_CORES = 2
SC_SUBCORES = 16
SC_WORKERS = SC_CORES * SC_SUBCORES
SC_LANES = 16
SC_WINDOW = 64
MOE_TILE = 512
MOE_NBUF = 4
MOE_SUB = 1


def _cparams(n_axes):
    return pltpu.CompilerParams(dimension_semantics=("arbitrary",) * n_axes,
                                vmem_limit_bytes=VMEM_LIMIT)


def _silu(x):
    return x * jax.nn.sigmoid(x)


def _rms(x, g):
    ms = jnp.mean(x * x, axis=-1, keepdims=True)
    return x * lax.rsqrt(ms + EPS) * g


def _dot(a, b):
    return jnp.dot(a, b, preferred_element_type=F32)


def _dot_nt(a, b):
    return lax.dot_general(a, b, (((1,), (1,)), ((), ())), preferred_element_type=F32)


def _dot_tn(a, b):
    return lax.dot_general(a, b, (((0,), (0,)), ((), ())), preferred_element_type=F32)


def _row_blocks(B, T, rows):
    if T >= rows:
        assert T % rows == 0
        bb, tt = 1, rows
    else:
        assert rows % T == 0 and B % (rows // T) == 0
        bb, tt = rows // T, T
    nt = T // tt
    return bb, tt, (B // bb) * nt, (lambda i: (i // nt, i % nt))


def _ada_kernel(c_ref, w_ref, b_ref, o_ref):
    a = _silu(c_ref[...]).astype(BF16)
    o_ref[...] = _dot(a, w_ref[...].astype(BF16)) + b_ref[...]


def ada_mod(c, ada_w, ada_b):
    R, D = c.shape
    L, _, N = ada_w.shape
    tn = 1536
    return pl.pallas_call(
        _ada_kernel,
        grid=(L, N // tn),
        in_specs=[pl.BlockSpec((R, D), lambda l, j: (0, 0)),
                  pl.BlockSpec((None, D, tn), lambda l, j: (l, 0, j)),
                  pl.BlockSpec((None, 1, tn), lambda l, j: (l, 0, j))],
        out_specs=pl.BlockSpec((None, R, tn), lambda l, j: (l, 0, j)),
        out_shape=jax.ShapeDtypeStruct((L, R, N), F32),
        compiler_params=_cparams(2),
        name="ada_mod",
    )(c, ada_w, ada_b.reshape(L, 1, N))


def _pack_pairs(y):
    half = y.shape[-1] // 2
    bits = lax.bitcast_convert_type(y.astype(BF16).astype(F32), jnp.uint32)
    word = lax.shift_right_logical(bits[:, :half], jnp.uint32(16)) | bits[:, half:]
    return lax.bitcast_convert_type(word, jnp.int32)


def _unpack_pairs(word, dtype=BF16):
    u = lax.bitcast_convert_type(word, jnp.uint32)
    lo = lax.bitcast_convert_type(lax.shift_left(u, jnp.uint32(16)), F32)
    hi = lax.bitcast_convert_type(u & jnp.uint32(0xFFFF0000), F32)
    return lo.astype(dtype), hi.astype(dtype)


def _norm_kernel(x_ref, g_ref, sc_ref, sh_ref, o_ref):
    y = _rms(x_ref[...], g_ref[...]) * (1.0 + sc_ref[...]) + sh_ref[...]
    o_ref[...] = y.astype(o_ref.dtype)


def norm_mod(x, g, mod, sc_idx, sh_idx, rows=512):
    B, T, D = x.shape
    bb, tt, nblk, ij = _row_blocks(B, T, rows)
    xspec = pl.BlockSpec((bb, tt, D), lambda i: ij(i) + (0,))
    return pl.pallas_call(
        _norm_kernel,
        grid=(nblk,),
        in_specs=[xspec, pl.BlockSpec((1, D), lambda i: (0, 0)),
                  pl.BlockSpec((bb, 1, D), lambda i: (ij(i)[0], 0, sc_idx)),
                  pl.BlockSpec((bb, 1, D), lambda i: (ij(i)[0], 0, sh_idx))],
        out_specs=xspec,
        out_shape=jax.ShapeDtypeStruct((B, T, D), BF16),
        compiler_params=_cparams(1),
        name="norm_mod",
    )(x, g.reshape(1, D), mod, mod)


def _linear_kernel(*refs, residual, norm_next, shared_rows, n_main):
    if norm_next and shared_rows:
        a_ref, w_ref, x_ref, gate_ref, ng_ref, nsc_ref, nsh_ref, _, o_ref, hp_ref, wb_ref = refs
    elif norm_next:
        a_ref, w_ref, x_ref, gate_ref, ng_ref, nsc_ref, nsh_ref, o_ref, hp_ref, wb_ref = refs
    elif residual:
        a_ref, w_ref, x_ref, gate_ref, o_ref, wb_ref = refs
    else:
        a_ref, w_ref, o_ref, wb_ref = refs

    @pl.when(pl.program_id(1) == 0)
    def _():
        wb_ref[...] = w_ref[...].astype(BF16)

    def main():
        bb, tt, K = a_ref.shape
        y = _dot(a_ref[...].reshape(bb * tt, K).astype(BF16), wb_ref[...])
        y = y.reshape(bb, tt, y.shape[-1])
        if residual:
            y = x_ref[...] + gate_ref[...] * y
        o_ref[...] = y.astype(o_ref.dtype)
        if norm_next:
            h = _rms(y, ng_ref[...]) * (1.0 + nsc_ref[...]) + nsh_ref[...]
            hp_ref[...] = _pack_pairs(h.reshape(bb * tt, h.shape[-1]))

    if n_main is None:
        main()
    else:
        pl.when(pl.program_id(1) < n_main)(main)

        @pl.when(pl.program_id(1) >= n_main)
        def _():
            hp_ref[...] = jnp.zeros(hp_ref.shape, hp_ref.dtype)


def linear(a, w, l, out_dtype, x=None, mod=None, gate_idx=0, rows=512, tn=1024, next_norm=None,
           rows_total=None, row0=0, rows_buf=None):
    B, T, K = a.shape
    _, _, N = w.shape
    tn = min(tn, N)
    bb, tt, nblk, ij0 = _row_blocks(B, T, rows)
    n_extra = 0
    if next_norm is not None and rows_buf is None and rows_total is not None:
        assert row0 == 0 and (rows_total - B * T) % (bb * tt) == 0
        n_extra = (rows_total - B * T) // (bb * tt)

    def ij(i):
        return ij0(jnp.minimum(i, nblk - 1)) if n_extra else ij0(i)

    in_specs = [pl.BlockSpec((bb, tt, K), lambda j, i: ij(i) + (0,)),
                pl.BlockSpec((None, K, tn), lambda j, i: (l, 0, j))]
    args = [a, w]
    ospec = pl.BlockSpec((bb, tt, tn), lambda j, i: ij(i) + (j,))
    out_specs = ospec
    out_shape = jax.ShapeDtypeStruct((B, T, N), out_dtype)
    aliases = {}
    if x is not None:
        gsteps = D_MODEL // tn
        in_specs += [ospec, pl.BlockSpec((bb, 1, tn), lambda j, i: (ij(i)[0], 0, gate_idx * gsteps + j))]
        args += [x, mod]
    if next_norm is not None:
        assert x is not None and tn == N
        gain, sc_idx, sh_idx = next_norm
        in_specs += [pl.BlockSpec((1, N), lambda j, i: (0, 0)),
                     pl.BlockSpec((bb, 1, N), lambda j, i: (ij(i)[0], 0, sc_idx)),
                     pl.BlockSpec((bb, 1, N), lambda j, i: (ij(i)[0], 0, sh_idx))]
        args += [gain.reshape(1, N), mod, mod]
        assert row0 % (bb * tt) == 0
        off = row0 // (bb * tt)
        out_specs = [ospec, pl.BlockSpec((bb * tt, N // 2), lambda j, i: (off + i, 0))]
        out_shape = [out_shape, jax.ShapeDtypeStruct((rows_total or B * T, N // 2), jnp.int32)]
        if rows_buf is not None:
            in_specs.append(pl.BlockSpec(memory_space=pl.ANY))
            args.append(rows_buf)
            aliases = {len(args) - 1: 1}
    return pl.pallas_call(
        functools.partial(_linear_kernel, residual=x is not None, norm_next=next_norm is not None,
                          shared_rows=rows_buf is not None, n_main=nblk if n_extra else None),
        grid=(N // tn, nblk + n_extra),
        in_specs=in_specs,
        out_specs=out_specs,
        out_shape=out_shape,
        scratch_shapes=[pltpu.VMEM((K, tn), BF16)],
        input_output_aliases=aliases,
        compiler_params=_cparams(2),
        name="linear",
    )(*args)


def _hgrn_proj_kernel(x_ref, g_ref, sc_ref, sh_ref, w_ref, zf_ref, zqig_ref, h_b, w_b):
    i = pl.program_id(0)
    j = pl.program_id(1)
    bb, tt, D = x_ref.shape

    @pl.when(i == 0)
    def _():
        w_b[j] = w_ref[...].astype(BF16)

    @pl.when(j == 0)
    def _():
        h = _rms(x_ref[...], g_ref[...]) * (1.0 + sc_ref[...]) + sh_ref[...]
        h_b[...] = h.reshape(bb * tt, D).astype(BF16)

    y = _dot(h_b[...], w_b[j]).reshape(bb, tt, -1)

    @pl.when(j == 1)
    def _():
        zf_ref[...] = y

    @pl.when(j != 1)
    def _():
        zqig_ref[...] = y.astype(zqig_ref.dtype)


def hgrn_proj(x, g, mod, sc_idx, sh_idx, w_in, l, rows=1024):
    B, T, D = x.shape
    bb, tt, nblk, ij = _row_blocks(B, T, min(rows, B * T))
    xspec = pl.BlockSpec((bb, tt, D), lambda i, j: ij(i) + (0,))
    return pl.pallas_call(
        _hgrn_proj_kernel,
        grid=(nblk, 4),
        in_specs=[xspec,
                  pl.BlockSpec((1, D), lambda i, j: (0, 0)),
                  pl.BlockSpec((bb, 1, D), lambda i, j: (ij(i)[0], 0, sc_idx)),
                  pl.BlockSpec((bb, 1, D), lambda i, j: (ij(i)[0], 0, sh_idx)),
                  pl.BlockSpec((None, D, D), lambda i, j: (l, 0, jnp.where(i == 0, j, 3)))],
        out_specs=[xspec,
                   pl.BlockSpec((bb, tt, D), lambda i, j: ij(i) + (j - (j >= 1),))],
        out_shape=[jax.ShapeDtypeStruct((B, T, D), F32), jax.ShapeDtypeStruct((B, T, 3 * D), BF16)],
        scratch_shapes=[pltpu.VMEM((bb * tt, D), BF16), pltpu.VMEM((4, D, D), BF16)],
        compiler_params=_cparams(2),
        name="hgrn_proj",
    )(x, g.reshape(1, D), mod, mod, w_in)


def _gla_kernel(*refs, L, n_chunks, has_init):
    if has_init:
        q_ref, f_ref, i_ref, g_ref, lb_ref, on_ref, s0_ref, o_ref, so_ref, st_ref = refs
    else:
        q_ref, f_ref, i_ref, g_ref, lb_ref, on_ref, o_ref, so_ref, st_ref = refs
    t = pl.program_id(1)
    H = st_ref.shape[0]

    @pl.when(t == 0)
    def _():
        for h in range(H):
            if has_init:
                st_ref[h] = s0_ref[0, h].T
            else:
                st_ref[h] = jnp.zeros(st_ref.shape[1:], F32)

    lb = lb_ref[...]
    onorm = on_ref[...]
    row = lax.broadcasted_iota(jnp.int32, (L, L), 0)
    col = lax.broadcasted_iota(jnp.int32, (L, L), 1)
    causal = col <= row
    tri = causal.astype(BF16)

    def chunk(c, carry):
        rows = pl.ds(pl.multiple_of(c * L, L), L)

        def write_o(sl, o):
            o_ref[0, rows, sl] = o.astype(o_ref.dtype)

        _gla_chunk(q_ref[0, rows, :], f_ref[0, rows, :], i_ref[0, rows, :], g_ref[0, rows, :],
                   lb, onorm, tri, causal, st_ref, write_o)
        return carry

    lax.fori_loop(0, n_chunks, chunk, 0, unroll=4 if n_chunks % 4 == 0 else 1)

    @pl.when(t == pl.num_programs(1) - 1)
    def _():
        for h in range(H):
            so_ref[0, h] = st_ref[h].T


def _gla_chunk(q, f, v, g, lb, onorm, tri, causal, st_ref, write_o):
    L = q.shape[0]
    H = st_ref.shape[0]
    mid = L // 2 - 1
    q = _silu(q.astype(F32))
    fg = lb + (1.0 - lb) * jax.nn.sigmoid(f)
    k = 1.0 - fg
    v = v.astype(BF16)
    gate = _silu(g.astype(F32))
    logf = jnp.log(fg)
    hi = logf.astype(BF16)
    lo = (logf - hi.astype(F32)).astype(BF16)
    b = _dot(tri, hi) + _dot(tri, lo)
    b_mid = b[mid:mid + 1, :]
    b_last = b[L - 1:L, :]
    qa = q * jnp.exp(b - b_mid)
    kb = k * jnp.exp(b_mid - b)
    qe = (qa * jnp.exp(b_mid)).astype(BF16)
    kd = (kb * jnp.exp(b_last - b_mid)).astype(BF16)
    qa = qa.astype(BF16)
    kb = kb.astype(BF16)
    decay = jnp.exp(b_last)
    sls = [slice(h * HG_DK, (h + 1) * HG_DK) for h in range(H)]
    sts = [st_ref[h] for h in range(H)]
    scores = [_dot_nt(qa[:, sl], kb[:, sl]) for sl in sls]
    inter = [_dot_nt(qe[:, sl], st.astype(BF16)) for sl, st in zip(sls, sts)]
    outer = [_dot_tn(v[:, sl], kd[:, sl]) for sl in sls]
    intra = [_dot(jnp.where(causal, sc, 0.0).astype(BF16), v[:, sl]) for sc, sl in zip(scores, sls)]
    for h, sl in enumerate(sls):
        st_ref[h] = sts[h] * decay[:, sl] + outer[h]
        write_o(sl, _rms(inter[h] + intra[h], onorm[:, sl]) * gate[:, sl])


def gla(zqig, zf, lb, onorm_g, s0):
    B, T, D = zf.shape
    L = CHUNK if T % CHUNK == 0 else T
    tt = min(T, 512)
    n_chunks = tt // L
    H = HG_HEADS

    def zspec(part):
        return pl.BlockSpec((1, tt, D), lambda b, t: (b, t, part))

    hspec = pl.BlockSpec((1, D), lambda b, t: (0, 0))
    sspec = pl.BlockSpec((1, H, HG_DK, HG_DV), lambda b, t: (b, 0, 0, 0))
    in_specs = [zspec(0), zspec(0), zspec(1), zspec(2), hspec, hspec]
    args = [zqig, zf, zqig, zqig, lb.reshape(1, D), onorm_g.reshape(1, D)]
    if s0 is not None:
        in_specs.append(sspec)
        args.append(s0)
    return pl.pallas_call(
        functools.partial(_gla_kernel, L=L, n_chunks=n_chunks, has_init=s0 is not None),
        grid=(B, T // tt),
        in_specs=in_specs,
        out_specs=[pl.BlockSpec((1, tt, D), lambda b, t: (b, t, 0)), sspec],
        out_shape=[jax.ShapeDtypeStruct((B, T, D), BF16),
                   jax.ShapeDtypeStruct((B, H, HG_DK, HG_DV), F32)],
        scratch_shapes=[pltpu.VMEM((H, HG_DV, HG_DK), F32)],
        compiler_params=_cparams(2),
        name="gla",
    )(*args)


def _route_kernel(h_ref, rw_ref, bias_ref, pos_ref, w_ref, te_ref, nu_ref,
                  e_s, r_s, base_s, start_s, *, tile_rows):
    ph = pl.program_id(0)
    i = pl.program_id(1)
    M = h_ref.shape[0]
    half = h_ref.shape[1]
    G, E = N_GROUPS, N_EXPERTS // N_GROUPS
    e_flat = lax.broadcasted_iota(jnp.int32, (N_EXPERTS, M), 0)

    @pl.when(ph == 1)
    def _():
        @pl.when(i == 0)
        def _():
            cnt = base_s[...]
            padded = jnp.floor((cnt + (tile_rows - 1)) * (1.0 / tile_rows)) * tile_rows
            r = lax.broadcasted_iota(jnp.int32, (N_EXPERTS, N_EXPERTS), 0)
            c = lax.broadcasted_iota(jnp.int32, (N_EXPERTS, N_EXPERTS), 1)
            start = jnp.dot((c < r).astype(F32), padded, preferred_element_type=F32,
                            precision=lax.Precision.HIGHEST)
            start_s[...] = start
            te_ref[...] = (start * (1.0 / tile_rows)).astype(jnp.int32)
            nu_ref[...] = (padded * (1.0 / tile_rows)).astype(jnp.int32)

        start_col = start_s[:, :1]
        for k in range(TOP_K):
            hit = e_flat == e_s[i, k:k + 1, :]
            seg = jnp.sum(jnp.where(hit, start_col, 0.0), axis=0, keepdims=True)
            pos_ref[k:k + 1, :] = (seg + r_s[i, k:k + 1, :]).astype(jnp.int32)

    @pl.when(ph == 0)
    def _():
        _route_pass0(h_ref, rw_ref, bias_ref, w_ref, e_s, r_s, base_s, i, M, half, G, E)


def _route_pass0(h_ref, rw_ref, bias_ref, w_ref, e_s, r_s, base_s, i, M, half, G, E):
    @pl.when(i == 0)
    def _():
        base_s[...] = jnp.zeros_like(base_s)

    lo, hi = _unpack_pairs(h_ref[...])
    rw = rw_ref[...].astype(BF16)
    logits = _dot_nt(rw[:, :half], lo) + _dot_nt(rw[:, half:], hi)
    s = jax.nn.sigmoid(logits)
    sb = (s + bias_ref[...]).reshape(G, E, M)
    s = s.reshape(G, E, M)
    e_in = lax.broadcasted_iota(jnp.int32, (G, E, M), 1).astype(F32)
    g_id = lax.broadcasted_iota(jnp.int32, (G, 1, M), 0)
    e_id = lax.broadcasted_iota(jnp.int32, (G, E, M), 0).astype(F32) * E + e_in

    def all_max(a):
        return jnp.max(jnp.max(a, axis=0, keepdims=True), axis=1, keepdims=True)

    def all_min(a):
        return jnp.min(jnp.min(a, axis=0, keepdims=True), axis=1, keepdims=True)

    def all_sum(a):
        return jnp.sum(jnp.sum(a, axis=0, keepdims=True), axis=1, keepdims=True)

    m1 = jnp.max(sb, axis=1, keepdims=True)
    first = jnp.min(jnp.where(sb == m1, e_in, float(E)), axis=1, keepdims=True)
    m2 = jnp.max(jnp.where(e_in == first, NEG_INF, sb), axis=1, keepdims=True)
    gs = m1 + m2

    rank = jnp.zeros((G, 1, M), jnp.int32)
    for j in range(G):
        gj = gs[j:j + 1]
        beats = (gj > gs) | ((gj == gs) & (j < g_id))
        rank = rank + beats.astype(jnp.int32)
    gsel = rank < TOPK_GROUPS

    vals = jnp.where(gsel, sb, NEG_INF)
    selm = jnp.zeros((G, E, M), F32)
    chosen, score = [], []
    for _ in range(TOP_K):
        m = all_max(vals)
        first = all_min(jnp.where(vals == m, e_id, float(N_EXPERTS)))
        hit = e_id == first
        score.append(all_sum(jnp.where(hit, s, 0.0)))
        selm = jnp.where(hit, 1.0, selm)
        vals = jnp.where(hit, NEG_INF, vals)
        chosen.append(first)

    tot = score[0]
    for sc in score[1:]:
        tot = tot + sc
    norm = ROUTED_SCALE / tot

    selm = selm.reshape(N_EXPERTS, M)
    earlier = (lax.broadcasted_iota(jnp.int32, (M, M), 0)
               < lax.broadcasted_iota(jnp.int32, (M, M), 1)).astype(BF16)
    rank = (base_s[:, :1] + _dot(selm.astype(BF16), earlier)).reshape(G, E, M)
    base_s[...] = base_s[...] + jnp.sum(selm, axis=1, keepdims=True)
    for k in range(TOP_K):
        hit = e_id == chosen[k]
        e_s[i, k:k + 1, :] = chosen[k].reshape(1, M).astype(jnp.int32)
        r_s[i, k:k + 1, :] = all_sum(jnp.where(hit, rank, 0.0)).reshape(1, M)
        w_ref[k:k + 1, :] = (score[k] * norm).reshape(1, M)


def route(hp, router_w_t, router_bias, l, tile_rows, rows=512):
    N, half = hp.shape
    M = rows
    nT = N // M
    assert N % M == 0

    def p0(ph, i):
        return i * (1 - ph) + (nT - 1) * ph

    return pl.pallas_call(
        functools.partial(_route_kernel, tile_rows=tile_rows),
        grid=(2, nT),
        in_specs=[pl.BlockSpec((M, half), lambda ph, i: (p0(ph, i), 0)),
                  pl.BlockSpec((None, N_EXPERTS, 2 * half), lambda ph, i: (l, 0, 0)),
                  pl.BlockSpec((None, N_EXPERTS, 1), lambda ph, i: (l, 0, 0))],
        out_specs=[pl.BlockSpec((TOP_K, M), lambda ph, i: (0, i * ph)),
                   pl.BlockSpec((TOP_K, M), lambda ph, i: (0, p0(ph, i))),
                   pl.BlockSpec((N_EXPERTS, LANES), lambda ph, i: (0, 0)),
                   pl.BlockSpec((N_EXPERTS, LANES), lambda ph, i: (0, 0))],
        out_shape=[jax.ShapeDtypeStruct((TOP_K, N), jnp.int32),
                   jax.ShapeDtypeStruct((TOP_K, N), F32),
                   jax.ShapeDtypeStruct((N_EXPERTS, LANES), jnp.int32),
                   jax.ShapeDtypeStruct((N_EXPERTS, LANES), jnp.int32)],
        scratch_shapes=[pltpu.VMEM((nT, TOP_K, M), jnp.int32), pltpu.VMEM((nT, TOP_K, M), F32),
                        pltpu.VMEM((N_EXPERTS, LANES), F32), pltpu.VMEM((N_EXPERTS, LANES), F32)],
        compiler_params=_cparams(2),
        name="route",
    )(hp, router_w_t, router_bias.reshape(-1, N_EXPERTS, 1))


def _sc_mesh():
    return plsc.VectorSubcoreMesh(core_axis_name="core", subcore_axis_name="subcore")


def sc_invert(pos_flat, n_tok, n_out):
    n = pos_flat.shape[0]
    per = n_out // SC_WORKERS
    chunk = n_tok
    assert n_out % SC_WORKERS == 0 and per % SC_LANES == 0
    assert n_tok % chunk == 0 and n % chunk == 0 and chunk % SC_LANES == 0
    cp = pltpu.CompilerParams()
    if "needs_layout_passes" in pltpu.CompilerParams.__dataclass_fields__:
        cp = dataclasses.replace(cp, needs_layout_passes=False)

    @functools.partial(
        pl.kernel, out_type=jax.ShapeDtypeStruct((n_out,), jnp.int32), mesh=_sc_mesh(),
        scratch_types=[pltpu.VMEM((chunk,), jnp.int32), pltpu.VMEM((per,), jnp.int32)],
        compiler_params=cp, name="sc_invert")
    def k(pos_hbm, src_hbm, pos_v, src_v):
        wid = lax.axis_index("subcore") * SC_CORES + lax.axis_index("core")
        lo = wid * per
        lane = lax.iota(jnp.int32, SC_LANES)

        @pl.loop(0, per, step=SC_LANES)
        def _(r):
            src_v[pl.ds(r, SC_LANES)] = lax.rem(lo + r + lane, n_tok)

        @pl.loop(0, n // chunk)
        def _(c):
            base = c * chunk
            pltpu.sync_copy(pos_hbm.at[pl.ds(base, chunk)], pos_v)
            tok0 = lax.rem(base, n_tok)

            @plsc.parallel_loop(0, chunk, step=SC_LANES, unroll=8)
            def _(r):
                p = pos_v[pl.ds(r, SC_LANES)] - lo
                mine = (p >= 0) & (p < per)
                plsc.store_scatter(src_v, [jnp.where(mine, p, 0)], tok0 + r + lane, mask=mine)

        pltpu.sync_copy(src_v, src_hbm.at[pl.ds(lo, per)])

    return k(pos_flat)


def sc_gather(x, idx):
    n = idx.shape[0]
    dim = x.shape[1]
    assert n % (SC_WINDOW * SC_WORKERS) == 0

    @functools.partial(
        pl.kernel, out_type=jax.ShapeDtypeStruct((n, dim), x.dtype), mesh=_sc_mesh(),
        scratch_types=[], name="sc_gather")
    def k(x_hbm, i_hbm, o_hbm):
        def body(i_vmem, o_vmem):
            pltpu.sync_copy(x_hbm.at[i_vmem.at[0]], o_vmem)

        pltpu.emit_pipeline(
            body, grid=(n // SC_WINDOW,),
            in_specs=[pl.BlockSpec((1, SC_WINDOW), index_map=lambda i: (i, 0))],
            out_specs=[pl.BlockSpec((SC_WINDOW, dim), index_map=lambda i: (i, 0))],
            core_axis_name=("core", "subcore"),
            dimension_semantics=(pltpu.PARALLEL,),
        )(i_hbm, o_hbm)

    return k(x, idx.reshape(n // SC_WINDOW, SC_WINDOW))


def _moe_gemm_kernel(ts_ref, tn_ref, x_hbm, wi_ref, wo_ref, o_hbm, wi_b, wo_b, xbuf, obuf, in_sem, out_sem,
                     *, tile_rows, n_tiles):
    e = pl.program_id(0)
    last = pl.num_programs(0) - 1
    t0 = ts_ref[e]
    n = tn_ref[e]
    n_used = ts_ref[last] + tn_ref[last]

    def x_copy(g, slot):
        rows = pl.ds(pl.multiple_of(g * tile_rows, tile_rows), tile_rows)
        return pltpu.make_async_copy(x_hbm.at[rows], xbuf.at[slot], in_sem.at[slot])

    def o_copy(g, slot):
        rows = pl.ds(pl.multiple_of(g * tile_rows, tile_rows), tile_rows)
        return pltpu.make_async_copy(obuf.at[slot], o_hbm.at[rows], out_sem.at[slot])

    @pl.when(e == 0)
    def _():
        for g0 in range(MOE_NBUF - 1):
            @pl.when(g0 < n_used)
            def _():
                x_copy(g0, g0).start()

    @pl.when(n > 0)
    def _():
        wi_b[...] = wi_ref[...].astype(BF16)
        wo_b[...] = wo_ref[...].astype(BF16)

    def tile(i, carry):
        g = t0 + i
        slot = lax.rem(g, MOE_NBUF)
        x_copy(g, slot).wait()
        ahead = g + (MOE_NBUF - 1)

        @pl.when(ahead < n_used)
        def _():
            x_copy(ahead, lax.rem(ahead, MOE_NBUF)).start()

        @pl.when(g >= MOE_NBUF)
        def _():
            o_copy(g - MOE_NBUF, slot).wait()

        rows = tile_rows // MOE_SUB
        half = xbuf.shape[2]
        xs = [_unpack_pairs(xbuf[slot, r * rows:(r + 1) * rows, :]) for r in range(MOE_SUB)]
        hus = [_dot(lo, wi_b[:half, :]) + _dot(hi, wi_b[half:, :]) for lo, hi in xs]
        acts = [(_silu(hu[:, :EXPERT_FF]) * hu[:, EXPERT_FF:]).astype(BF16) for hu in hus]
        outs = [_dot(act, wo_b[...]) for act in acts]
        for r, out in enumerate(outs):
            obuf[slot, r * rows:(r + 1) * rows, :] = _pack_pairs(out)
        o_copy(g, slot).start()
        return carry

    lax.fori_loop(0, n, tile, 0)

    @pl.when(e == last)
    def _():
        for back in range(MOE_NBUF, 0, -1):
            @pl.when(n_used >= back)
            def _():
                o_copy(n_used - back, lax.rem(n_used - back, MOE_NBUF)).wait()

        obuf[...] = jnp.zeros(obuf.shape, obuf.dtype)
        n_clear = n_tiles - n_used

        def clear(i, carry):
            slot = lax.rem(i, MOE_NBUF)

            @pl.when(i >= MOE_NBUF)
            def _():
                o_copy(n_used + i - MOE_NBUF, slot).wait()

            o_copy(n_used + i, slot).start()
            return carry

        lax.fori_loop(0, n_clear, clear, 0)
        for back in range(MOE_NBUF, 0, -1):
            @pl.when(n_clear >= back)
            def _():
                o_copy(n_tiles - back, lax.rem(n_clear - back, MOE_NBUF)).wait()


def moe_gemm(xs, tile_start, tile_count, exp_w_in, exp_w_out, l, tile_rows, n_tiles):
    P, half = xs.shape
    D = 2 * half
    assert P == n_tiles * tile_rows
    hbm = pl.BlockSpec(memory_space=pl.ANY)
    grid_spec = pltpu.PrefetchScalarGridSpec(
        num_scalar_prefetch=2,
        grid=(N_EXPERTS,),
        in_specs=[hbm,
                  pl.BlockSpec((None, None, D, 2 * EXPERT_FF), lambda e, ts, tn: (l, e, 0, 0)),
                  pl.BlockSpec((None, None, EXPERT_FF, D), lambda e, ts, tn: (l, e, 0, 0))],
        out_specs=hbm,
        scratch_shapes=[pltpu.VMEM((D, 2 * EXPERT_FF), BF16), pltpu.VMEM((EXPERT_FF, D), BF16),
                        pltpu.VMEM((MOE_NBUF, tile_rows, half), jnp.int32),
                        pltpu.VMEM((MOE_NBUF, tile_rows, half), jnp.int32),
                        pltpu.SemaphoreType.DMA((MOE_NBUF,)), pltpu.SemaphoreType.DMA((MOE_NBUF,))],
    )
    return pl.pallas_call(
        functools.partial(_moe_gemm_kernel, tile_rows=tile_rows, n_tiles=n_tiles),
        grid_spec=grid_spec,
        out_shape=jax.ShapeDtypeStruct((P, half), jnp.int32),
        compiler_params=_cparams(1),
        name="moe_gemm",
    )(tile_start, tile_count, xs, exp_w_in, exp_w_out)


def _moe_combine_kernel(*refs, final, norm_next, shared_kv, queries):
    it = iter(refs)
    y_ref, w_ref, h_ref, si_ref, so_ref, x_ref, g2_ref = (next(it) for _ in range(7))
    fg_ref = next(it) if final else None
    ng_ref, nsc_ref, nsh_ref = (next(it) for _ in range(3)) if norm_next else (None,) * 3
    kg_ref, wkv_ref, lg_ref, cos_ref, sin_ref = (next(it) for _ in range(5)) if shared_kv else (None,) * 5
    wdq_ref, qg_ref, wqt_ref, wqrt_ref, cost_ref, sint_ref = (next(it) for _ in range(6)) if queries else (None,) * 6
    o_ref = next(it)
    hn_ref = next(it) if norm_next and not queries else None
    lat_ref, kr_ref = (next(it), next(it)) if shared_kv else (None, None)
    qt_ref = next(it) if queries else None
    si_b, so_b = next(it), next(it)
    wkv_b = next(it) if shared_kv else None
    wdq_b, wqt_b, wqrt_b = (next(it), next(it), next(it)) if queries else (None,) * 3

    @pl.when(pl.program_id(0) == 0)
    def _():
        si_b[...] = si_ref[...].astype(BF16)
        so_b[...] = so_ref[...].astype(BF16)
        if shared_kv:
            wkv_b[...] = wkv_ref[...].astype(BF16)
        if queries:
            wdq_b[...] = wdq_ref[...].astype(BF16)
            wqt_b[...] = wqt_ref[...].astype(BF16)
            wqrt_b[...] = wqrt_ref[...].astype(BF16)

    bb, tt, D = x_ref.shape
    half = D // 2
    w = w_ref[...].T
    acc_lo = jnp.zeros((bb * tt, half), F32)
    acc_hi = jnp.zeros((bb * tt, half), F32)
    for k in range(TOP_K):
        lo, hi = _unpack_pairs(y_ref[k], F32)
        acc_lo = acc_lo + w[:, k:k + 1] * lo
        acc_hi = acc_hi + w[:, k:k + 1] * hi
    hlo, hhi = _unpack_pairs(h_ref[...])
    hu = _dot(hlo, si_b[:half, :]) + _dot(hhi, si_b[half:, :])
    act = (_silu(hu[:, :SHARED_FF]) * hu[:, SHARED_FF:]).astype(BF16)
    y = jnp.concatenate([acc_lo, acc_hi], axis=-1) + _dot(act, so_b[...])
    x_new = x_ref[...] + g2_ref[...] * y.reshape(bb, tt, D)
    o_ref[...] = _rms(x_new, fg_ref[...]) if final else x_new
    if norm_next:
        hn = (_rms(x_new, ng_ref[...]) * (1.0 + nsc_ref[...]) + nsh_ref[...]).astype(BF16)
        if queries:
            _queries_t(hn.reshape(bb * tt, D), wdq_b, qg_ref[...], wqt_b, wqrt_b, cost_ref[...], sint_ref[...],
                       qt_ref)
        else:
            hn_ref[...] = hn
    if shared_kv:
        xn = _rms(x_new, kg_ref[...]).reshape(bb * tt, D).astype(BF16)
        z = _dot(xn, wkv_b[...])
        lat_ref[...] = _rms(z[:, :KV_LORA], lg_ref[...]).reshape(bb, tt, KV_LORA)
        zr = z[:, KV_LORA:KV_LORA + MLA_ROPE].reshape(bb, tt, MLA_ROPE)
        zq = z[:, KV_LORA + LANES:KV_LORA + LANES + MLA_ROPE].reshape(bb, tt, MLA_ROPE)
        kr_ref[...] = zr * cos_ref[...] + zq * sin_ref[...]


def moe_combine(y8, w8, hp, sh_w_in, sh_w_out, x, mod, gate_idx, l, row0, final_g=None, next_norm=None,
                shared_kv=None, queries=None, rows=512):
    B, T, D = x.shape
    half = D // 2
    bb, tt, nblk, ij = _row_blocks(B, T, rows)
    M = bb * tt
    assert row0 % M == 0
    off = row0 // M
    xspec = pl.BlockSpec((bb, tt, D), lambda i: ij(i) + (0,))
    in_specs = [pl.BlockSpec((TOP_K, M, half), lambda i: (0, off + i, 0)),
                pl.BlockSpec((TOP_K, M), lambda i: (0, off + i)),
                pl.BlockSpec((M, half), lambda i: (off + i, 0)),
                pl.BlockSpec((None, D, 2 * SHARED_FF), lambda i: (l, 0, 0)),
                pl.BlockSpec((None, SHARED_FF, D), lambda i: (l, 0, 0)),
                xspec,
                pl.BlockSpec((bb, 1, D), lambda i: (ij(i)[0], 0, gate_idx))]
    args = [y8, w8, hp, sh_w_in, sh_w_out, x, mod]
    out_specs = xspec
    out_shape = jax.ShapeDtypeStruct((B, T, D), F32)
    if final_g is not None:
        assert next_norm is None
        in_specs.append(pl.BlockSpec((1, D), lambda i: (0, 0)))
        args.append(final_g.reshape(1, D))
    if next_norm is not None:
        gain, mod_next, sc_idx, sh_idx = next_norm
        in_specs += [pl.BlockSpec((1, D), lambda i: (0, 0)),
                     pl.BlockSpec((bb, 1, D), lambda i: (ij(i)[0], 0, sc_idx)),
                     pl.BlockSpec((bb, 1, D), lambda i: (ij(i)[0], 0, sh_idx))]
        args += [gain.reshape(1, D), mod_next, mod_next]
        if queries is None:
            out_specs = [xspec, xspec]
            out_shape = [out_shape, jax.ShapeDtypeStruct((B, T, D), BF16)]
    scratch = [pltpu.VMEM((D, 2 * SHARED_FF), BF16), pltpu.VMEM((SHARED_FF, D), BF16)]
    if shared_kv is not None:
        kv_in_g, w_kv, kv_lat_g, cos32, sin32 = shared_kv
        tspec = pl.BlockSpec((tt, MLA_ROPE), lambda i: (ij(i)[1], 0))
        in_specs += [pl.BlockSpec((1, D), lambda i: (0, 0)),
                     pl.BlockSpec(w_kv.shape, lambda i: (0, 0)),
                     pl.BlockSpec((1, KV_LORA), lambda i: (0, 0)),
                     tspec, tspec]
        args += [kv_in_g.reshape(1, D), w_kv, kv_lat_g.reshape(1, KV_LORA), cos32, sin32]
        out_specs = list(out_specs) if isinstance(out_specs, list) else [out_specs]
        out_shape = list(out_shape) if isinstance(out_shape, list) else [out_shape]
        out_specs += [pl.BlockSpec((bb, tt, KV_LORA), lambda i: ij(i) + (0,)),
                      pl.BlockSpec((bb, tt, MLA_ROPE), lambda i: ij(i) + (0,))]
        out_shape += [jax.ShapeDtypeStruct((B, T, KV_LORA), F32), jax.ShapeDtypeStruct((B, T, MLA_ROPE), F32)]
        scratch.append(pltpu.VMEM(w_kv.shape, BF16))
    if queries is not None:
        assert next_norm is not None and bb == 1
        w_dq, q_norm_g, wq_t, wqr_t, bi, cos_t, sin_t = queries
        NQ, NR = wq_t.shape[1], wqr_t.shape[1]
        tspec_t = pl.BlockSpec((MLA_ROPE, tt), lambda i: (0, ij(i)[1]))
        in_specs += [pl.BlockSpec((None, D, Q_LORA), lambda i: (bi, 0, 0)),
                     pl.BlockSpec((None, 1, Q_LORA), lambda i: (bi, 0, 0)),
                     pl.BlockSpec((None, NQ, Q_LORA), lambda i: (bi, 0, 0)),
                     pl.BlockSpec((None, NR, Q_LORA), lambda i: (bi, 0, 0)),
                     tspec_t, tspec_t]
        args += [w_dq, q_norm_g.reshape(-1, 1, Q_LORA), wq_t, wqr_t, cos_t, sin_t]
        out_specs = list(out_specs) if isinstance(out_specs, list) else [out_specs]
        out_shape = list(out_shape) if isinstance(out_shape, list) else [out_shape]
        out_specs.append(pl.BlockSpec((1, NQ, tt), lambda i: (ij(i)[0], 0, ij(i)[1])))
        out_shape.append(jax.ShapeDtypeStruct((B, NQ, T), BF16))
        scratch += [pltpu.VMEM((D, Q_LORA), BF16), pltpu.VMEM((NQ, Q_LORA), BF16), pltpu.VMEM((NR, Q_LORA), BF16)]
    return pl.pallas_call(
        functools.partial(_moe_combine_kernel, final=final_g is not None, norm_next=next_norm is not None,
                          shared_kv=shared_kv is not None, queries=queries is not None),
        grid=(nblk,),
        in_specs=in_specs,
        out_specs=out_specs,
        out_shape=out_shape,
        scratch_shapes=scratch,
        compiler_params=_cparams(1),
        name="moe_combine",
    )(*args)


def _kv_expand_kernel(lat_ref, kr_ref, wk_ref, ek_ref, wvt_ref, ones_ref, k_ref, vt_ref):
    lat = lat_ref[0].astype(BF16)
    kr = kr_ref[0].astype(BF16)
    k = _dot(lat, wk_ref[...].astype(BF16)) + _dot(kr, ek_ref[...].astype(BF16))
    k_ref[0] = k.astype(k_ref.dtype)
    vt = _dot_nt(wvt_ref[...].astype(BF16), lat) + ones_ref[...]
    vt_ref[0] = vt.astype(vt_ref.dtype)


def kv_expand(lat, kr, wk_pad, ek, wvt_ext, ones_col, rows=512):
    B, T, _ = lat.shape
    tt = rows
    NK, NVT = wk_pad.shape[1], wvt_ext.shape[0]

    def full(a):
        return pl.BlockSpec(a.shape, lambda b, t: (0, 0))

    def rowspec(n):
        return pl.BlockSpec((1, tt, n), lambda b, t: (b, t, 0))

    return pl.pallas_call(
        _kv_expand_kernel,
        grid=(B, T // tt),
        in_specs=[rowspec(KV_LORA), rowspec(MLA_ROPE), full(wk_pad), full(ek), full(wvt_ext), full(ones_col)],
        out_specs=[rowspec(NK), pl.BlockSpec((1, NVT, tt), lambda b, t: (b, 0, t))],
        out_shape=[jax.ShapeDtypeStruct((B, T, NK), BF16), jax.ShapeDtypeStruct((B, NVT, T), BF16)],
        compiler_params=_cparams(2),
        name="kv_expand",
    )(lat, kr, wk_pad, ek, wvt_ext, ones_col)


def _query_kernel(h_ref, wdq_ref, qg_ref, wq_ref, wqr_ref, c_ref, s_ref, q_ref, wdq_b, wq_b, wqr_b):
    @pl.when(pl.program_id(0) == 0)
    def _():
        wdq_b[...] = wdq_ref[...].astype(BF16)
        wq_b[...] = wq_ref[...].astype(BF16)
        wqr_b[...] = wqr_ref[...].astype(BF16)

    bb, tt, D = h_ref.shape
    h = h_ref[...].reshape(bb * tt, D)
    cq = _rms(_dot(h, wdq_b[...]), qg_ref[...]).astype(BF16)
    q1 = _dot(cq, wq_b[...]).reshape(bb, tt, -1)
    q2 = _dot(cq, wqr_b[...]).reshape(bb, tt, -1)
    c = c_ref[...]
    s = s_ref[...]
    for hd in range(MLA_HEADS):
        sl = slice(hd * HEAD_PAD, (hd + 1) * HEAD_PAD)
        q_ref[:, :, sl] = (q1[:, :, sl] * c + q2[:, :, sl] * s).astype(q_ref.dtype)


def mla_queries(h, w_dq, q_norm_g, wq_pad, wq_rot, l, c128, s128, rows=512):
    B, T, D = h.shape
    bb, tt, nblk, ij = _row_blocks(B, T, rows)
    NQ = wq_pad.shape[-1]
    tspec = pl.BlockSpec((tt, HEAD_PAD), lambda i: (ij(i)[1], 0))
    return pl.pallas_call(
        _query_kernel,
        grid=(nblk,),
        in_specs=[pl.BlockSpec((bb, tt, D), lambda i: ij(i) + (0,)),
                  pl.BlockSpec((None, D, Q_LORA), lambda i: (l, 0, 0)),
                  pl.BlockSpec((None, 1, Q_LORA), lambda i: (l, 0, 0)),
                  pl.BlockSpec((None, Q_LORA, NQ), lambda i: (l, 0, 0)),
                  pl.BlockSpec((None, Q_LORA, NQ), lambda i: (l, 0, 0)),
                  tspec, tspec],
        out_specs=pl.BlockSpec((bb, tt, NQ), lambda i: ij(i) + (0,)),
        out_shape=jax.ShapeDtypeStruct((B, T, NQ), BF16),
        scratch_shapes=[pltpu.VMEM((D, Q_LORA), BF16), pltpu.VMEM((Q_LORA, NQ), BF16),
                        pltpu.VMEM((Q_LORA, NQ), BF16)],
        compiler_params=_cparams(1),
        name="mla_queries",
    )(h, w_dq, q_norm_g.reshape(-1, 1, Q_LORA), wq_pad, wq_rot, c128, s128)


def _queries_t(h, wdq_b, qg, wqt_b, wqrt_b, cos, sin, qt_ref):
    cq = _rms(_dot(h, wdq_b[...]), qg).astype(BF16)
    q1 = _dot_nt(wqt_b[...], cq)
    q2 = _dot_nt(wqrt_b[...], cq)
    pad = jnp.zeros((HEAD_PAD - MLA_NOPE - MLA_ROPE, q1.shape[1]), qt_ref.dtype)
    for hd in range(MLA_HEADS):
        r0 = hd * HEAD_PAD
        rope = (q1[r0 + MLA_NOPE:r0 + MLA_NOPE + MLA_ROPE] * cos
                + q2[hd * MLA_ROPE:(hd + 1) * MLA_ROPE] * sin)
        qt_ref[0, r0:r0 + MLA_NOPE, :] = (q1[r0:r0 + MLA_NOPE] * Q_PRESCALE).astype(qt_ref.dtype)
        qt_ref[0, r0 + MLA_NOPE:r0 + MLA_NOPE + MLA_ROPE, :] = rope.astype(qt_ref.dtype)
        qt_ref[0, r0 + MLA_NOPE + MLA_ROPE:r0 + HEAD_PAD, :] = pad


def _query_t_kernel(h_ref, wdq_ref, qg_ref, wqt_ref, wqrt_ref, cos_ref, sin_ref, qt_ref, wdq_b, wqt_b, wqrt_b):
    @pl.when((pl.program_id(0) == 0) & (pl.program_id(1) == 0))
    def _():
        wdq_b[...] = wdq_ref[...].astype(BF16)
        wqt_b[...] = wqt_ref[...].astype(BF16)
        wqrt_b[...] = wqrt_ref[...].astype(BF16)

    _queries_t(h_ref[0], wdq_b, qg_ref[...], wqt_b, wqrt_b, cos_ref[...], sin_ref[...], qt_ref)


def mla_queries_t(h, w_dq, q_norm_g, wq_t, wqr_t, l, cos_t, sin_t, rows=512):
    B, T, D = h.shape
    tt = rows
    NQ = wq_t.shape[1]
    NR = wqr_t.shape[1]
    tspec = pl.BlockSpec((MLA_ROPE, tt), lambda b, t: (0, t))
    return pl.pallas_call(
        _query_t_kernel,
        grid=(B, T // tt),
        in_specs=[pl.BlockSpec((1, tt, D), lambda b, t: (b, t, 0)),
                  pl.BlockSpec((None, D, Q_LORA), lambda b, t: (l, 0, 0)),
                  pl.BlockSpec((None, 1, Q_LORA), lambda b, t: (l, 0, 0)),
                  pl.BlockSpec((None, NQ, Q_LORA), lambda b, t: (l, 0, 0)),
                  pl.BlockSpec((None, NR, Q_LORA), lambda b, t: (l, 0, 0)),
                  tspec, tspec],
        out_specs=pl.BlockSpec((1, NQ, tt), lambda b, t: (b, 0, t)),
        out_shape=jax.ShapeDtypeStruct((B, NQ, T), BF16),
        scratch_shapes=[pltpu.VMEM((D, Q_LORA), BF16), pltpu.VMEM((NQ, Q_LORA), BF16),
                        pltpu.VMEM((NR, Q_LORA), BF16)],
        compiler_params=_cparams(2),
        name="mla_queries_t",
    )(h, w_dq, q_norm_g.reshape(-1, 1, Q_LORA), wq_t, wqr_t, cos_t, sin_t)


def _attn_prompt_kernel(qi_tab, ki_tab, qt_ref, k_ref, vt_ref, o_ref, *scratch, tq, tk):
    H = MLA_HEADS
    m_refs, l_refs, acc_refs = scratch[:H], scratch[H:2 * H], scratch[2 * H:]
    p_id = pl.program_id(1)
    qi = qi_tab[p_id]
    ki = ki_tab[p_id]

    @pl.when(ki == 0)
    def _():
        for hd in range(H):
            m_refs[hd][...] = jnp.full(m_refs[hd].shape, NEG_INF, F32)
            l_refs[hd][...] = jnp.zeros(l_refs[hd].shape, F32)
            acc_refs[hd][...] = jnp.zeros(acc_refs[hd].shape, F32)

    def block(masked):
        if masked:
            kchunk = (ki * tk + lax.broadcasted_iota(jnp.int32, (tk, tq), 0)) // CHUNK
            qchunk = (qi * tq + lax.broadcasted_iota(jnp.int32, (tk, tq), 1)) // CHUNK
            mask = kchunk <= qchunk
        def scores(hd):
            sl = slice(hd * HEAD_PAD, (hd + 1) * HEAD_PAD)
            return _dot(k_ref[0, :, sl], qt_ref[0, sl, :])

        pending = [scores(hd) for hd in range(ATTN_LOOKAHEAD)]
        for hd in range(H):
            if hd + ATTN_LOOKAHEAD < H:
                pending.append(scores(hd + ATTN_LOOKAHEAD))
            s = pending.pop(0)
            if masked:
                s = jnp.where(mask, s, NEG_INF)
            m_prev = m_refs[hd][...]
            m_new = jnp.maximum(m_prev, jnp.max(s, axis=0, keepdims=True))
            a = jnp.exp2(m_prev - m_new)
            p = jnp.exp2(s - m_new).astype(BF16)
            pv = _dot(vt_ref[0, hd * V_ROWS:(hd + 1) * V_ROWS, :], p)
            acc_refs[hd][...] = a * acc_refs[hd][...] + pv[:MLA_V]
            l_refs[hd][...] = a * l_refs[hd][...] + pv[MLA_V:MLA_V + 1]
            m_refs[hd][...] = m_new

    @pl.when(ki < qi)
    def _():
        block(False)

    @pl.when(ki == qi)
    def _():
        block(True)
        o_t = jnp.concatenate([acc_refs[hd][...] / l_refs[hd][...] for hd in range(H)], axis=0)
        o_ref[0] = o_t.T.astype(o_ref.dtype)


def attn_prompt(qt, k, vt, tq=256):
    B, NQ, T = qt.shape
    NVT = vt.shape[1]
    NV = MLA_HEADS * MLA_V
    tk = tq
    assert tq % CHUNK == 0
    nq = T // tq
    pairs = [(a, b) for a in range(nq) for b in range(a + 1)]
    qi_tab = jnp.asarray([a for a, _ in pairs], jnp.int32)
    ki_tab = jnp.asarray([b for _, b in pairs], jnp.int32)
    grid_spec = pltpu.PrefetchScalarGridSpec(
        num_scalar_prefetch=2,
        grid=(B, len(pairs)),
        in_specs=[pl.BlockSpec((1, NQ, tq), lambda b, p, qt, kt: (b, 0, qt[p])),
                  pl.BlockSpec((1, tk, NQ), lambda b, p, qt, kt: (b, kt[p], 0)),
                  pl.BlockSpec((1, NVT, tk), lambda b, p, qt, kt: (b, 0, kt[p]))],
        out_specs=pl.BlockSpec((1, tq, NV), lambda b, p, qt, kt: (b, qt[p], 0)),
        scratch_shapes=([pltpu.VMEM((1, tq), F32)] * (2 * MLA_HEADS)
                        + [pltpu.VMEM((MLA_V, tq), F32)] * MLA_HEADS),
    )
    return pl.pallas_call(
        functools.partial(_attn_prompt_kernel, tq=tq, tk=tk),
        grid_spec=grid_spec,
        out_shape=jax.ShapeDtypeStruct((B, T, NV), BF16),
        compiler_params=_cparams(2),
        name="attn_prompt",
    )(qi_tab, ki_tab, qt, k, vt)


def _absorb_kernel(q_ref, m_ref, o_ref):
    o_ref[...] = _dot(q_ref[...], m_ref[...].astype(BF16)).astype(o_ref.dtype)


def absorb_queries(q2d, m_abs):
    N = q2d.shape[0]
    H, _, W = m_abs.shape
    return pl.pallas_call(
        _absorb_kernel,
        grid=(H,),
        in_specs=[pl.BlockSpec((N, HEAD_PAD), lambda h: (0, h)),
                  pl.BlockSpec((None, HEAD_PAD, W), lambda h: (h, 0, 0))],
        out_specs=pl.BlockSpec((None, N, W), lambda h: (h, 0, 0)),
        out_shape=jax.ShapeDtypeStruct((H, N, W), BF16),
        compiler_params=_cparams(1),
        name="absorb_queries",
    )(q2d, m_abs)


def _attn_sample_kernel(q_ref, lat_ref, kr_ref, nlat_ref, nkr_ref, o_ref, m_ref, l_ref, acc_ref):
    kb = pl.program_id(1)
    H, Q, W = q_ref.shape
    q = q_ref[...].reshape(H * Q, W)
    q_lat = q[:, :KV_LORA]
    q_rope = q[:, KV_LORA:KV_LORA + MLA_ROPE]

    def update(lat_tile, kr_tile, n_sub, kr_transposed):
        sub = lat_tile.shape[0] // n_sub
        lats = [lat_tile[j * sub:(j + 1) * sub, :].astype(BF16) for j in range(n_sub)]
        if kr_transposed:
            krs = [kr_tile[:, j * sub:(j + 1) * sub].astype(BF16) for j in range(n_sub)]
            ss = [_dot_nt(q_lat, lat) + _dot(q_rope, kr) for lat, kr in zip(lats, krs)]
        else:
            krs = [kr_tile[j * sub:(j + 1) * sub, :].astype(BF16) for j in range(n_sub)]
            ss = [_dot_nt(q_lat, lat) + _dot_nt(q_rope, kr) for lat, kr in zip(lats, krs)]
        m_prev = m_ref[...]
        m_new = m_prev
        for s in ss:
            m_new = jnp.maximum(m_new, jnp.max(s, axis=-1, keepdims=True))
        a = jnp.exp2(m_prev - m_new)
        ps = [jnp.exp2(s - m_new[:, :1]) for s in ss]
        pv = _dot(ps[0].astype(BF16), lats[0])
        psum = jnp.sum(ps[0], axis=-1, keepdims=True)
        for p, lat in zip(ps[1:], lats[1:]):
            pv = pv + _dot(p.astype(BF16), lat)
            psum = psum + jnp.sum(p, axis=-1, keepdims=True)
        l_ref[...] = a * l_ref[...] + psum
        m_ref[...] = m_new
        acc_ref[...] = jnp.concatenate([a, a], axis=-1) * acc_ref[...] + pv

    @pl.when(kb == 0)
    def _():
        m_ref[...] = jnp.full_like(m_ref, NEG_INF)
        l_ref[...] = jnp.zeros_like(l_ref)
        acc_ref[...] = jnp.zeros_like(acc_ref)
        update(nlat_ref[0], nkr_ref[0], 1, False)

    update(lat_ref[0], kr_ref[0], SAMPLE_KEY_SUB, True)

    @pl.when(kb == pl.num_programs(1) - 1)
    def _():
        lsum = l_ref[...]
        o = acc_ref[...] / jnp.concatenate([lsum, lsum], axis=-1)
        o_ref[...] = o.reshape(H, Q, KV_LORA).astype(o_ref.dtype)


def attn_sample(q_abs, cache_lat, cache_kr_t, new_lat, new_kr, tk=4096):
    H, N, W = q_abs.shape
    B, P, _ = cache_lat.shape
    Q = new_lat.shape[1]
    qpos = P + np.arange(Q)
    kpos = np.arange(P + Q)
    assert bool(np.all((kpos // CHUNK)[None, :] <= (qpos // CHUNK)[:, None]))
    return pl.pallas_call(
        _attn_sample_kernel,
        grid=(B, P // tk),
        in_specs=[pl.BlockSpec((H, Q, W), lambda b, kb: (0, b, 0)),
                  pl.BlockSpec((1, tk, KV_LORA), lambda b, kb: (b, kb, 0)),
                  pl.BlockSpec((1, MLA_ROPE, tk), lambda b, kb: (b, 0, kb)),
                  pl.BlockSpec((1, Q, KV_LORA), lambda b, kb: (b, 0, 0)),
                  pl.BlockSpec((1, Q, MLA_ROPE), lambda b, kb: (b, 0, 0))],
        out_specs=pl.BlockSpec((H, Q, KV_LORA), lambda b, kb: (0, b, 0)),
        out_shape=jax.ShapeDtypeStruct((H, N, KV_LORA), BF16),
        scratch_shapes=[pltpu.VMEM((H * Q, LANES), F32), pltpu.VMEM((H * Q, LANES), F32),
                        pltpu.VMEM((H * Q, KV_LORA), F32)],
        compiler_params=_cparams(2),
        name="attn_sample",
    )(q_abs, cache_lat, cache_kr_t, new_lat, new_kr)


def _unabsorb_kernel(o_ref, w_ref, out_ref):
    out_ref[...] = (_dot(o_ref[0], w_ref[0].astype(BF16))
                    + _dot(o_ref[1], w_ref[1].astype(BF16))).astype(out_ref.dtype)


def unabsorb(o_lat, wuv_pad):
    H, N, R = o_lat.shape
    return pl.pallas_call(
        _unabsorb_kernel,
        grid=(H // 2,),
        in_specs=[pl.BlockSpec((2, N, R), lambda p: (p, 0, 0)),
                  pl.BlockSpec((2, R, 2 * MLA_V), lambda p: (p, 0, 0))],
        out_specs=pl.BlockSpec((N, 2 * MLA_V), lambda p: (0, p)),
        out_shape=jax.ShapeDtypeStruct((N, H * MLA_V), BF16),
        compiler_params=_cparams(1),
        name="unabsorb",
    )(o_lat, wuv_pad)


def _rope_tables(pos):
    half = MLA_ROPE // 2
    inv = 1.0 / (ROPE_THETA ** (np.arange(half, dtype=np.float64) * 2.0 / MLA_ROPE))
    ang = np.asarray(pos, np.float64)[:, None] * inv[None, :]
    cos = np.concatenate([np.cos(ang), np.cos(ang)], axis=-1)
    sin = np.concatenate([np.sin(ang), np.sin(ang)], axis=-1)
    T = cos.shape[0]
    c128 = np.zeros((T, HEAD_PAD)); s128 = np.zeros((T, HEAD_PAD))
    c128[:, :MLA_NOPE] = 1.0
    c128[:, MLA_NOPE:MLA_NOPE + MLA_ROPE] = cos
    s128[:, MLA_NOPE:MLA_NOPE + MLA_ROPE] = sin
    return dict(cos32=jnp.asarray(cos, F32), sin32=jnp.asarray(sin, F32),
                c128=jnp.asarray(c128 * Q_PRESCALE, F32), s128=jnp.asarray(s128 * Q_PRESCALE, F32),
                cos_t=jnp.asarray(cos.T * Q_PRESCALE, F32), sin_t=jnp.asarray(sin.T * Q_PRESCALE, F32))


def _rot_half_cols(w):
    half = w.shape[-1] // 2
    return jnp.concatenate([-w[..., half:], w[..., :half]], axis=-1)


def _prep_weights(w_dkv, w_uk, w_uv, w_uq, router_w):
    D = D_MODEL
    w_lat, w_rope = w_dkv[:, :KV_LORA], w_dkv[:, KV_LORA:]
    pad96 = jnp.zeros((D, LANES - MLA_ROPE), F32)
    w_kv = jnp.concatenate([w_lat, w_rope, pad96, _rot_half_cols(w_rope), pad96], axis=-1)

    zpad = HEAD_PAD - MLA_NOPE
    wk_pad = jnp.pad(w_uk, ((0, 0), (0, 0), (0, zpad))).reshape(KV_LORA, MLA_HEADS * HEAD_PAD)
    ek = jnp.zeros((MLA_ROPE, MLA_HEADS, HEAD_PAD), F32)
    ek = ek.at[:, :, MLA_NOPE:MLA_NOPE + MLA_ROPE].set(
        jnp.broadcast_to(jnp.eye(MLA_ROPE, dtype=F32)[:, None, :], (MLA_ROPE, MLA_HEADS, MLA_ROPE)))
    ek = ek.reshape(MLA_ROPE, MLA_HEADS * HEAD_PAD)
    wvt = jnp.transpose(w_uv, (1, 2, 0))
    wvt_ext = jnp.pad(wvt, ((0, 0), (0, V_ROWS - MLA_V), (0, 0))).reshape(MLA_HEADS * V_ROWS, KV_LORA)
    ones_col = jnp.tile((jnp.arange(V_ROWS) >= MLA_V).astype(F32), MLA_HEADS).reshape(-1, 1)

    nb = w_uq.shape[0]
    qn, qr = w_uq[..., :MLA_NOPE], w_uq[..., MLA_NOPE:]
    z32 = jnp.zeros(qr.shape[:-1] + (HEAD_PAD - MLA_NOPE - MLA_ROPE,), F32)
    wq_pad = jnp.concatenate([qn, qr, z32], axis=-1).reshape(nb, Q_LORA, MLA_HEADS * HEAD_PAD)
    wq_rot = jnp.concatenate([jnp.zeros_like(qn), _rot_half_cols(qr), z32], axis=-1)
    wq_rot = wq_rot.reshape(nb, Q_LORA, MLA_HEADS * HEAD_PAD)
    wq_t = jnp.transpose(wq_pad, (0, 2, 1))
    wqr_t = jnp.transpose(_rot_half_cols(qr).reshape(nb, Q_LORA, MLA_HEADS * MLA_ROPE), (0, 2, 1))

    m_abs = jnp.zeros((MLA_HEADS, HEAD_PAD, KV_LORA + LANES), F32)
    m_abs = m_abs.at[:, :MLA_NOPE, :KV_LORA].set(jnp.transpose(w_uk, (1, 2, 0)))
    m_abs = m_abs.at[:, MLA_NOPE:MLA_NOPE + MLA_ROPE, KV_LORA:KV_LORA + MLA_ROPE].set(
        jnp.broadcast_to(jnp.eye(MLA_ROPE, dtype=F32), (MLA_HEADS, MLA_ROPE, MLA_ROPE)))

    wuv_h = jnp.transpose(w_uv, (1, 0, 2))
    even = jnp.pad(wuv_h, ((0, 0), (0, 0), (0, MLA_V)))
    odd = jnp.pad(wuv_h, ((0, 0), (0, 0), (MLA_V, 0)))
    wuv_pad = jnp.where((jnp.arange(MLA_HEADS) % 2 == 0)[:, None, None], even, odd)

    rw_t = jnp.transpose(router_w, (0, 2, 1))
    return dict(w_kv=w_kv, wk_pad=wk_pad, ek=ek, wvt_ext=wvt_ext, ones_col=ones_col, wq_pad=wq_pad, wq_rot=wq_rot, wq_t=wq_t, wqr_t=wqr_t,
                m_abs=m_abs, wuv_pad=wuv_pad, rw_t=rw_t)


def _mixer(st, l, P, W, packed):
    rows_kw = dict(rows_total=packed["total"], row0=packed["row0"], rows_buf=packed["buf"])
    x, m = st["x"], st["mod"][l]
    B, T, _ = x.shape
    n_a = P["hg_w_in"].shape[0]
    norm2 = (P["norm2_g"][l], 4, 3)
    if l < n_a:
        zf, zqig = hgrn_proj(x, P["norm1_g"][l], m, 1, 0, P["hg_w_in"], l)
        s0 = None if st["hg_state"] is None else st["hg_state"][l]
        o, s_new = gla(zqig, zf, st["lbs"][l], P["hg_onorm_g"][l], s0)
        st["hg_new"].append(s_new)
        st["x"], packed["buf"] = linear(o, P["hg_w_out"], l, F32, x=x, mod=m, gate_idx=2, next_norm=norm2,
                                        **rows_kw)
    else:
        bi = l - n_a
        h = st.pop("h_next", None)
        qt = st.pop("qt_next", None)
        if h is None and qt is None:
            h = norm_mod(x, P["norm1_g"][l], m, sc_idx=1, sh_idx=0)
        if st["past_lat"] is None:
            if qt is None:
                qt = mla_queries_t(h, P["w_dq"], P["q_norm_g"], W["wq_t"], W["wqr_t"], bi, st["cos_t"],
                                   st["sin_t"])
            o = attn_prompt(qt, st["k_all"], st["v_all"])
        else:
            q = mla_queries(h, P["w_dq"], P["q_norm_g"], W["wq_pad"], W["wq_rot"], bi, st["c128"], st["s128"])
            q_abs = absorb_queries(q.reshape(B * T, -1), W["m_abs"])
            o_lat = attn_sample(q_abs, st["past_lat"], st["past_kr"], st["lat"], st["kr"])
            o = unabsorb(o_lat, W["wuv_pad"]).reshape(B, T, -1)
        st["x"], packed["buf"] = linear(o, P["w_o"], bi, F32, x=x, mod=m, gate_idx=2, next_norm=norm2, **rows_kw)
    packed["row0"] += B * T


def _moe(groups, hp, l, P, W):
    n_tok = hp.shape[0]
    n_tiles = (TOP_K * n_tok) // MOE_TILE + N_EXPERTS
    pos, w8, tile_start, tile_count = route(hp, W["rw_t"], P["router_bias"], l, MOE_TILE)
    pos_flat = pos.reshape(-1)
    src = sc_invert(pos_flat, n_tok, n_tiles * MOE_TILE)
    xs = sc_gather(hp, src)
    out = moe_gemm(xs, tile_start[:, 0], tile_count[:, 0], P["exp_w_in"], P["exp_w_out"], l,
                   MOE_TILE, n_tiles)
    y8 = sc_gather(out, pos_flat).reshape(TOP_K, n_tok, -1)
    last = l == P["norm1_g"].shape[0] - 1
    with_kv = l == P["hg_w_in"].shape[0] - 1
    next_is_mla = not last and l + 1 >= P["hg_w_in"].shape[0]
    row0 = 0
    for st in groups:
        B, T, _ = st["x"].shape
        with_q = next_is_mla and st["past_lat"] is None
        outs = moe_combine(
            y8, w8, hp, P["sh_w_in"], P["sh_w_out"], st["x"], st["mod"][l], 5, l, row0,
            final_g=P["final_g"] if last else None,
            next_norm=(P["norm1_g"][l + 1], st["mod"][l + 1], 1, 0) if next_is_mla else None,
            shared_kv=(P["kv_in_g"], W["w_kv"], P["kv_lat_g"], st["cos32"], st["sin32"]) if with_kv else None,
            queries=(P["w_dq"], P["q_norm_g"], W["wq_t"], W["wqr_t"], l + 1 - P["hg_w_in"].shape[0],
                     st["cos_t"], st["sin_t"]) if with_q else None,
            rows=256 if with_q else 512)
        outs = list(outs) if isinstance(outs, (list, tuple)) else [outs]
        st["x"] = outs.pop(0)
        if next_is_mla and not with_q:
            st["h_next"] = outs.pop(0)
        if with_kv:
            st["lat"], st["kr"] = outs.pop(0), outs.pop(0)
        if with_q:
            st["qt_next"] = outs.pop(0)
        row0 += B * T


def _group_state(x, mod, pos, hg_state, past_lat, past_kr, lbs):
    return dict(x=x, mod=mod, hg_state=hg_state, past_lat=past_lat, past_kr=past_kr, lbs=lbs,
                **_rope_tables(pos), hg_new=[],
                lat=None, kr=None, k_all=None, v_all=None)


def kernel(x_prompt, x_sample, state_hgrn, cache_mla_latent, cache_mla_krope, c_prompt, c_sample, ada_w, ada_b, norm1_g, norm2_g, hg_w_in, hg_lb_logits, hg_onorm_g, hg_w_out, kv_in_g, w_dkv, kv_lat_g, w_uk, w_uv, w_dq, q_norm_g, w_uq, w_o, router_w, router_bias, exp_w_in, exp_w_out, sh_w_in, sh_w_out, final_g):
    Bp, Sp, _ = x_prompt.shape
    Bs, Ss, _ = x_sample.shape
    past = cache_mla_latent.shape[1]
    P = dict(norm1_g=norm1_g, norm2_g=norm2_g, hg_w_in=hg_w_in, hg_lb_logits=hg_lb_logits,
             hg_onorm_g=hg_onorm_g, hg_w_out=hg_w_out, kv_in_g=kv_in_g, kv_lat_g=kv_lat_g,
             w_dq=w_dq, q_norm_g=q_norm_g, w_o=w_o, router_bias=router_bias,
             exp_w_in=exp_w_in, exp_w_out=exp_w_out, sh_w_in=sh_w_in, sh_w_out=sh_w_out, final_g=final_g)
    W = _prep_weights(w_dkv, w_uk, w_uv, w_uq, router_w)
    mod = ada_mod(jnp.concatenate([c_prompt, c_sample], axis=0), ada_w, ada_b)
    lbs = jnp.cumsum(jax.nn.softmax(hg_lb_logits.astype(F32), axis=0), axis=0)
    gp = _group_state(x_prompt, mod[:, :Bp, None, :], np.arange(Sp), None, None, None, lbs)
    gs = _group_state(x_sample, mod[:, Bp:, None, :], past + np.arange(Ss), state_hgrn,
                      cache_mla_latent, jnp.transpose(cache_mla_krope, (0, 2, 1)), lbs)
    groups = [gp, gs]
    n_tok = sum(st["x"].shape[0] * st["x"].shape[1] for st in groups)
    n_a = hg_w_in.shape[0]
    for l in range(norm1_g.shape[0]):
        packed = dict(total=n_tok, row0=0, buf=None)
        for st in groups:
            _mixer(st, l, P, W, packed)
        _moe(groups, packed["buf"], l, P, W)
        if l == n_a - 1:
            gp["k_all"], gp["v_all"] = kv_expand(gp["lat"], gp["kr"], W["wk_pad"], W["ek"], W["wvt_ext"],
                                                 W["ones_col"])
    return (gp["x"], gs["x"], jnp.stack(gp["hg_new"], axis=0), jnp.stack(gs["hg_new"], axis=0),
            gp["lat"], gp["kr"], gs["lat"], gs["kr"])
```

```python
import dataclasses
import functools

import numpy as np
import jax
import jax.numpy as jnp
from jax import lax
from jax.experimental import pallas as pl
from jax.experimental.pallas import tpu as pltpu
from jax.experimental.pallas import tpu_sc as plsc

F32 = jnp.float32
BF16 = jnp.bfloat16

D_MODEL = 1024
CHUNK = 64
HG_HEADS = 8
HG_DK = 128
HG_DV = 128
MLA_HEADS = 16
MLA_NOPE = 64
MLA_ROPE = 32
MLA_V = 64
Q_LORA = 384
KV_LORA = 256
ROPE_THETA = 10000.0
N_EXPERTS = 64
TOP_K = 8
N_GROUPS = 8
TOPK_GROUPS = 4
EXPERT_FF = 256
SHARED_FF = 256
ROUTED_SCALE = 2.5
EPS = 1e-6

LANES = 128
HEAD_PAD = LANES
SAMPLE_KEY_SUB = 8
ATTN_LOOKAHEAD = 6
V_ROWS = MLA_V + 16
QK_SCALE = (MLA_NOPE + MLA_ROPE) ** -0.5
Q_PRESCALE = QK_SCALE * float(np.log2(np.e))
VMEM_LIMIT = 56 * 1024 * 1024
NEG_INF = float("-inf")
SC_CORES = 2
SC_SUBCORES = 16
SC_WORKERS = SC_CORES * SC_SUBCORES
SC_LANES = 16
SC_WINDOW = 64
MOE_TILE = 512
MOE_NBUF = 4
MOE_SUB = 1


def _cparams(n_axes):
    return pltpu.CompilerParams(dimension_semantics=("arbitrary",) * n_axes,
                                vmem_limit_bytes=VMEM_LIMIT)


def _silu(x):
    return x * jax.nn.sigmoid(x)


def _rms(x, g):
    ms = jnp.mean(x * x, axis=-1, keepdims=True)
    return x * lax.rsqrt(ms + EPS) * g


def _dot(a, b):
    return jnp.dot(a, b, preferred_element_type=F32)


def _dot_nt(a, b):
    return lax.dot_general(a, b, (((1,), (1,)), ((), ())), preferred_element_type=F32)


def _dot_tn(a, b):
    return lax.dot_general(a, b, (((0,), (0,)), ((), ())), preferred_element_type=F32)


def _row_blocks(B, T, rows):
    if T >= rows:
        assert T % rows == 0
        bb, tt = 1, rows
    else:
        assert rows % T == 0 and B % (rows // T) == 0
        bb, tt = rows // T, T
    nt = T // tt
    return bb, tt, (B // bb) * nt, (lambda i: (i // nt, i % nt))


def _ada_kernel(c_ref, w_ref, b_ref, o_ref):
    a = _silu(c_ref[...]).astype(BF16)
    o_ref[...] = _dot(a, w_ref[...].astype(BF16)) + b_ref[...]


def ada_mod(c, ada_w, ada_b):
    R, D = c.shape
    L, _, N = ada_w.shape
    tn = 1536
    return pl.pallas_call(
        _ada_kernel,
        grid=(L, N // tn),
        in_specs=[pl.BlockSpec((R, D), lambda l, j: (0, 0)),
                  pl.BlockSpec((None, D, tn), lambda l, j: (l, 0, j)),
                  pl.BlockSpec((None, 1, tn), lambda l, j: (l, 0, j))],
        out_specs=pl.BlockSpec((None, R, tn), lambda l, j: (l, 0, j)),
        out_shape=jax.ShapeDtypeStruct((L, R, N), F32),
        compiler_params=_cparams(2),
        name="ada_mod",
    )(c, ada_w, ada_b.reshape(L, 1, N))


def _pack_pairs(y):
    half = y.shape[-1] // 2
    bits = lax.bitcast_convert_type(y.astype(BF16).astype(F32), jnp.uint32)
    word = lax.shift_right_logical(bits[:, :half], jnp.uint32(16)) | bits[:, half:]
    return lax.bitcast_convert_type(word, jnp.int32)


def _unpack_pairs(word, dtype=BF16):
    u = lax.bitcast_convert_type(word, jnp.uint32)
    lo = lax.bitcast_convert_type(lax.shift_left(u, jnp.uint32(16)), F32)
    hi = lax.bitcast_convert_type(u & jnp.uint32(0xFFFF0000), F32)
    return lo.astype(dtype), hi.astype(dtype)


def _norm_kernel(x_ref, g_ref, sc_ref, sh_ref, o_ref):
    y = _rms(x_ref[...], g_ref[...]) * (1.0 + sc_ref[...]) + sh_ref[...]
    o_ref[...] = y.astype(o_ref.dtype)


def norm_mod(x, g, mod, sc_idx, sh_idx, rows=512):
    B, T, D = x.shape
    bb, tt, nblk, ij = _row_blocks(B, T, rows)
    xspec = pl.BlockSpec((bb, tt, D), lambda i: ij(i) + (0,))
    return pl.pallas_call(
        _norm_kernel,
        grid=(nblk,),
        in_specs=[xspec, pl.BlockSpec((1, D), lambda i: (0, 0)),
                  pl.BlockSpec((bb, 1, D), lambda i: (ij(i)[0], 0, sc_idx)),
                  pl.BlockSpec((bb, 1, D), lambda i: (ij(i)[0], 0, sh_idx))],
        out_specs=xspec,
        out_shape=jax.ShapeDtypeStruct((B, T, D), BF16),
        compiler_params=_cparams(1),
        name="norm_mod",
    )(x, g.reshape(1, D), mod, mod)


def _linear_kernel(*refs, residual, norm_next, shared_rows, n_main):
    if norm_next and shared_rows:
        a_ref, w_ref, x_ref, gate_ref, ng_ref, nsc_ref, nsh_ref, _, o_ref, hp_ref, wb_ref = refs
    elif norm_next:
        a_ref, w_ref, x_ref, gate_ref, ng_ref, nsc_ref, nsh_ref, o_ref, hp_ref, wb_ref = refs
    elif residual:
        a_ref, w_ref, x_ref, gate_ref, o_ref, wb_ref = refs
    else:
        a_ref, w_ref, o_ref, wb_ref = refs

    @pl.when(pl.program_id(1) == 0)
    def _():
        wb_ref[...] = w_ref[...].astype(BF16)

    def main():
        bb, tt, K = a_ref.shape
        y = _dot(a_ref[...].reshape(bb * tt, K).astype(BF16), wb_ref[...])
        y = y.reshape(bb, tt, y.shape[-1])
        if residual:
            y = x_ref[...] + gate_ref[...] * y
        o_ref[...] = y.astype(o_ref.dtype)
        if norm_next:
            h = _rms(y, ng_ref[...]) * (1.0 + nsc_ref[...]) + nsh_ref[...]
            hp_ref[...] = _pack_pairs(h.reshape(bb * tt, h.shape[-1]))

    if n_main is None:
        main()
    else:
        pl.when(pl.program_id(1) < n_main)(main)

        @pl.when(pl.program_id(1) >= n_main)
        def _():
            hp_ref[...] = jnp.zeros(hp_ref.shape, hp_ref.dtype)


def linear(a, w, l, out_dtype, x=None, mod=None, gate_idx=0, rows=512, tn=1024, next_norm=None,
           rows_total=None, row0=0, rows_buf=None):
    B, T, K = a.shape
    _, _, N = w.shape
    tn = min(tn, N)
    bb, tt, nblk, ij0 = _row_blocks(B, T, rows)
    n_extra = 0
    if next_norm is not None and rows_buf is None and rows_total is not None:
        assert row0 == 0 and (rows_total - B * T) % (bb * tt) == 0
        n_extra = (rows_total - B * T) // (bb * tt)

    def ij(i):
        return ij0(jnp.minimum(i, nblk - 1)) if n_extra else ij0(i)

    in_specs = [pl.BlockSpec((bb, tt, K), lambda j, i: ij(i) + (0,)),
                pl.BlockSpec((None, K, tn), lambda j, i: (l, 0, j))]
    args = [a, w]
    ospec = pl.BlockSpec((bb, tt, tn), lambda j, i: ij(i) + (j,))
    out_specs = ospec
    out_shape = jax.ShapeDtypeStruct((B, T, N), out_dtype)
    aliases = {}
    if x is not None:
        gsteps = D_MODEL // tn
        in_specs += [ospec, pl.BlockSpec((bb, 1, tn), lambda j, i: (ij(i)[0], 0, gate_idx * gsteps + j))]
        args += [x, mod]
    if next_norm is not None:
        assert x is not None and tn == N
        gain, sc_idx, sh_idx = next_norm
        in_specs += [pl.BlockSpec((1, N), lambda j, i: (0, 0)),
                     pl.BlockSpec((bb, 1, N), lambda j, i: (ij(i)[0], 0, sc_idx)),
                     pl.BlockSpec((bb, 1, N), lambda j, i: (ij(i)[0], 0, sh_idx))]
        args += [gain.reshape(1, N), mod, mod]
        assert row0 % (bb * tt) == 0
        off = row0 // (bb * tt)
        out_specs = [ospec, pl.BlockSpec((bb * tt, N // 2), lambda j, i: (off + i, 0))]
        out_shape = [out_shape, jax.ShapeDtypeStruct((rows_total or B * T, N // 2), jnp.int32)]
        if rows_buf is not None:
            in_specs.append(pl.BlockSpec(memory_space=pl.ANY))
            args.append(rows_buf)
            aliases = {len(args) - 1: 1}
    return pl.pallas_call(
        functools.partial(_linear_kernel, residual=x is not None, norm_next=next_norm is not None,
                          shared_rows=rows_buf is not None, n_main=nblk if n_extra else None),
        grid=(N // tn, nblk + n_extra),
        in_specs=in_specs,
        out_specs=out_specs,
        out_shape=out_shape,
        scratch_shapes=[pltpu.VMEM((K, tn), BF16)],
        input_output_aliases=aliases,
        compiler_params=_cparams(2),
        name="linear",
    )(*args)


def _hgrn_proj_kernel(x_ref, g_ref, sc_ref, sh_ref, w_ref, zf_ref, zqig_ref, h_b, w_b):
    i = pl.program_id(0)
    j = pl.program_id(1)
    bb, tt, D = x_ref.shape

    @pl.when(i == 0)
    def _():
        w_b[j] = w_ref[...].astype(BF16)

    @pl.when(j == 0)
    def _():
        h = _rms(x_ref[...], g_ref[...]) * (1.0 + sc_ref[...]) + sh_ref[...]
        h_b[...] = h.reshape(bb * tt, D).astype(BF16)

    y = _dot(h_b[...], w_b[j]).reshape(bb, tt, -1)

    @pl.when(j == 1)
    def _():
        zf_ref[...] = y

    @pl.when(j != 1)
    def _():
        zqig_ref[...] = y.astype(zqig_ref.dtype)


def hgrn_proj(x, g, mod, sc_idx, sh_idx, w_in, l, rows=1024):
    B, T, D = x.shape
    bb, tt, nblk, ij = _row_blocks(B, T, min(rows, B * T))
    xspec = pl.BlockSpec((bb, tt, D), lambda i, j: ij(i) + (0,))
    return pl.pallas_call(
        _hgrn_proj_kernel,
        grid=(nblk, 4),
        in_specs=[xspec,
                  pl.BlockSpec((1, D), lambda i, j: (0, 0)),
                  pl.BlockSpec((bb, 1, D), lambda i, j: (ij(i)[0], 0, sc_idx)),
                  pl.BlockSpec((bb, 1, D), lambda i, j: (ij(i)[0], 0, sh_idx)),
                  pl.BlockSpec((None, D, D), lambda i, j: (l, 0, jnp.where(i == 0, j, 3)))],
        out_specs=[xspec,
                   pl.BlockSpec((bb, tt, D), lambda i, j: ij(i) + (j - (j >= 1),))],
        out_shape=[jax.ShapeDtypeStruct((B, T, D), F32), jax.ShapeDtypeStruct((B, T, 3 * D), BF16)],
        scratch_shapes=[pltpu.VMEM((bb * tt, D), BF16), pltpu.VMEM((4, D, D), BF16)],
        compiler_params=_cparams(2),
        name="hgrn_proj",
    )(x, g.reshape(1, D), mod, mod, w_in)


def _gla_kernel(*refs, L, n_chunks, has_init):
    if has_init:
        q_ref, f_ref, i_ref, g_ref, lb_ref, on_ref, s0_ref, o_ref, so_ref, st_ref = refs
    else:
        q_ref, f_ref, i_ref, g_ref, lb_ref, on_ref, o_ref, so_ref, st_ref = refs
    t = pl.program_id(1)
    H = st_ref.shape[0]

    @pl.when(t == 0)
    def _():
        for h in range(H):
            if has_init:
                st_ref[h] = s0_ref[0, h].T
            else:
                st_ref[h] = jnp.zeros(st_ref.shape[1:], F32)

    lb = lb_ref[...]
    onorm = on_ref[...]
    row = lax.broadcasted_iota(jnp.int32, (L, L), 0)
    col = lax.broadcasted_iota(jnp.int32, (L, L), 1)
    causal = col <= row
    tri = causal.astype(BF16)

    def chunk(c, carry):
        rows = pl.ds(pl.multiple_of(c * L, L), L)

        def write_o(sl, o):
            o_ref[0, rows, sl] = o.astype(o_ref.dtype)

        _gla_chunk(q_ref[0, rows, :], f_ref[0, rows, :], i_ref[0, rows, :], g_ref[0, rows, :],
                   lb, onorm, tri, causal, st_ref, write_o)
        return carry

    lax.fori_loop(0, n_chunks, chunk, 0, unroll=4 if n_chunks % 4 == 0 else 1)

    @pl.when(t == pl.num_programs(1) - 1)
    def _():
        for h in range(H):
            so_ref[0, h] = st_ref[h].T


def _gla_chunk(q, f, v, g, lb, onorm, tri, causal, st_ref, write_o):
    L = q.shape[0]
    H = st_ref.shape[0]
    mid = L // 2 - 1
    q = _silu(q.astype(F32))
    fg = lb + (1.0 - lb) * jax.nn.sigmoid(f)
    k = 1.0 - fg
    v = v.astype(BF16)
    gate = _silu(g.astype(F32))
    logf = jnp.log(fg)
    hi = logf.astype(BF16)
    lo = (logf - hi.astype(F32)).astype(BF16)
    b = _dot(tri, hi) + _dot(tri, lo)
    b_mid = b[mid:mid + 1, :]
    b_last = b[L - 1:L, :]
    qa = q * jnp.exp(b - b_mid)
    kb = k * jnp.exp(b_mid - b)
    qe = (qa * jnp.exp(b_mid)).astype(BF16)
    kd = (kb * jnp.exp(b_last - b_mid)).astype(BF16)
    qa = qa.astype(BF16)
    kb = kb.astype(BF16)
    decay = jnp.exp(b_last)
    sls = [slice(h * HG_DK, (h + 1) * HG_DK) for h in range(H)]
    sts = [st_ref[h] for h in range(H)]
    scores = [_dot_nt(qa[:, sl], kb[:, sl]) for sl in sls]
    inter = [_dot_nt(qe[:, sl], st.astype(BF16)) for sl, st in zip(sls, sts)]
    outer = [_dot_tn(v[:, sl], kd[:, sl]) for sl in sls]
    intra = [_dot(jnp.where(causal, sc, 0.0).astype(BF16), v[:, sl]) for sc, sl in zip(scores, sls)]
    for h, sl in enumerate(sls):
        st_ref[h] = sts[h] * decay[:, sl] + outer[h]
        write_o(sl, _rms(inter[h] + intra[h], onorm[:, sl]) * gate[:, sl])


def gla(zqig, zf, lb, onorm_g, s0):
    B, T, D = zf.shape
    L = CHUNK if T % CHUNK == 0 else T
    tt = min(T, 512)
    n_chunks = tt // L
    H = HG_HEADS

    def zspec(part):
        return pl.BlockSpec((1, tt, D), lambda b, t: (b, t, part))

    hspec = pl.BlockSpec((1, D), lambda b, t: (0, 0))
    sspec = pl.BlockSpec((1, H, HG_DK, HG_DV), lambda b, t: (b, 0, 0, 0))
    in_specs = [zspec(0), zspec(0), zspec(1), zspec(2), hspec, hspec]
    args = [zqig, zf, zqig, zqig, lb.reshape(1, D), onorm_g.reshape(1, D)]
    if s0 is not None:
        in_specs.append(sspec)
        args.append(s0)
    return pl.pallas_call(
        functools.partial(_gla_kernel, L=L, n_chunks=n_chunks, has_init=s0 is not None),
        grid=(B, T // tt),
        in_specs=in_specs,
        out_specs=[pl.BlockSpec((1, tt, D), lambda b, t: (b, t, 0)), sspec],
        out_shape=[jax.ShapeDtypeStruct((B, T, D), BF16),
                   jax.ShapeDtypeStruct((B, H, HG_DK, HG_DV), F32)],
        scratch_shapes=[pltpu.VMEM((H, HG_DV, HG_DK), F32)],
        compiler_params=_cparams(2),
        name="gla",
    )(*args)


def _route_kernel(h_ref, rw_ref, bias_ref, pos_ref, w_ref, te_ref, nu_ref,
                  e_s, r_s, base_s, start_s, *, tile_rows):
    ph = pl.program_id(0)
    i = pl.program_id(1)
    M = h_ref.shape[0]
    half = h_ref.shape[1]
    G, E = N_GROUPS, N_EXPERTS // N_GROUPS
    e_flat = lax.broadcasted_iota(jnp.int32, (N_EXPERTS, M), 0)

    @pl.when(ph == 1)
    def _():
        @pl.when(i == 0)
        def _():
            cnt = base_s[...]
            padded = jnp.floor((cnt + (tile_rows - 1)) * (1.0 / tile_rows)) * tile_rows
            r = lax.broadcasted_iota(jnp.int32, (N_EXPERTS, N_EXPERTS), 0)
            c = lax.broadcasted_iota(jnp.int32, (N_EXPERTS, N_EXPERTS), 1)
            start = jnp.dot((c < r).astype(F32), padded, preferred_element_type=F32,
                            precision=lax.Precision.HIGHEST)
            start_s[...] = start
            te_ref[...] = (start * (1.0 / tile_rows)).astype(jnp.int32)
            nu_ref[...] = (padded * (1.0 / tile_rows)).astype(jnp.int32)

        start_col = start_s[:, :1]
        for k in range(TOP_K):
            hit = e_flat == e_s[i, k:k + 1, :]
            seg = jnp.sum(jnp.where(hit, start_col, 0.0), axis=0, keepdims=True)
            pos_ref[k:k + 1, :] = (seg + r_s[i, k:k + 1, :]).astype(jnp.int32)

    @pl.when(ph == 0)
    def _():
        _route_pass0(h_ref, rw_ref, bias_ref, w_ref, e_s, r_s, base_s, i, M, half, G, E)


def _route_pass0(h_ref, rw_ref, bias_ref, w_ref, e_s, r_s, base_s, i, M, half, G, E):
    @pl.when(i == 0)
    def _():
        base_s[...] = jnp.zeros_like(base_s)

    lo, hi = _unpack_pairs(h_ref[...])
    rw = rw_ref[...].astype(BF16)
    logits = _dot_nt(rw[:, :half], lo) + _dot_nt(rw[:, half:], hi)
    s = jax.nn.sigmoid(logits)
    sb = (s + bias_ref[...]).reshape(G, E, M)
    s = s.reshape(G, E, M)
    e_in = lax.broadcasted_iota(jnp.int32, (G, E, M), 1).astype(F32)
    g_id = lax.broadcasted_iota(jnp.int32, (G, 1, M), 0)
    e_id = lax.broadcasted_iota(jnp.int32, (G, E, M), 0).astype(F32) * E + e_in

    def all_max(a):
        return jnp.max(jnp.max(a, axis=0, keepdims=True), axis=1, keepdims=True)

    def all_min(a):
        return jnp.min(jnp.min(a, axis=0, keepdims=True), axis=1, keepdims=True)

    def all_sum(a):
        return jnp.sum(jnp.sum(a, axis=0, keepdims=True), axis=1, keepdims=True)

    m1 = jnp.max(sb, axis=1, keepdims=True)
    first = jnp.min(jnp.where(sb == m1, e_in, float(E)), axis=1, keepdims=True)
    m2 = jnp.max(jnp.where(e_in == first, NEG_INF, sb), axis=1, keepdims=True)
    gs = m1 + m2

    rank = jnp.zeros((G, 1, M), jnp.int32)
    for j in range(G):
        gj = gs[j:j + 1]
        beats = (gj > gs) | ((gj == gs) & (j < g_id))
        rank = rank + beats.astype(jnp.int32)
    gsel = rank < TOPK_GROUPS

    vals = jnp.where(gsel, sb, NEG_INF)
    selm = jnp.zeros((G, E, M), F32)
    chosen, score = [], []
    for _ in range(TOP_K):
        m = all_max(vals)
        first = all_min(jnp.where(vals == m, e_id, float(N_EXPERTS)))
        hit = e_id == first
        score.append(all_sum(jnp.where(hit, s, 0.0)))
        selm = jnp.where(hit, 1.0, selm)
        vals = jnp.where(hit, NEG_INF, vals)
        chosen.append(first)

    tot = score[0]
    for sc in score[1:]:
        tot = tot + sc
    norm = ROUTED_SCALE / tot

    selm = selm.reshape(N_EXPERTS, M)
    earlier = (lax.broadcasted_iota(jnp.int32, (M, M), 0)
               < lax.broadcasted_iota(jnp.int32, (M, M), 1)).astype(BF16)
    rank = (base_s[:, :1] + _dot(selm.astype(BF16), earlier)).reshape(G, E, M)
    base_s[...] = base_s[...] + jnp.sum(selm, axis=1, keepdims=True)
    for k in range(TOP_K):
        hit = e_id == chosen[k]
        e_s[i, k:k + 1, :] = chosen[k].reshape(1, M).astype(jnp.int32)
        r_s[i, k:k + 1, :] = all_sum(jnp.where(hit, rank, 0.0)).reshape(1, M)
        w_ref[k:k + 1, :] = (score[k] * norm).reshape(1, M)


def route(hp, router_w_t, router_bias, l, tile_rows, rows=768):
    N, half = hp.shape
    M = rows
    nT = N // M
    assert N % M == 0

    def p0(ph, i):
        return i * (1 - ph) + (nT - 1) * ph

    return pl.pallas_call(
        functools.partial(_route_kernel, tile_rows=tile_rows),
        grid=(2, nT),
        in_specs=[pl.BlockSpec((M, half), lambda ph, i: (p0(ph, i), 0)),
                  pl.BlockSpec((None, N_EXPERTS, 2 * half), lambda ph, i: (l, 0, 0)),
                  pl.BlockSpec((None, N_EXPERTS, 1), lambda ph, i: (l, 0, 0))],
        out_specs=[pl.BlockSpec((TOP_K, M), lambda ph, i: (0, i * ph)),
                   pl.BlockSpec((TOP_K, M), lambda ph, i: (0, p0(ph, i))),
                   pl.BlockSpec((N_EXPERTS, LANES), lambda ph, i: (0, 0)),
                   pl.BlockSpec((N_EXPERTS, LANES), lambda ph, i: (0, 0))],
        out_shape=[jax.ShapeDtypeStruct((TOP_K, N), jnp.int32),
                   jax.ShapeDtypeStruct((TOP_K, N), F32),
                   jax.ShapeDtypeStruct((N_EXPERTS, LANES), jnp.int32),
                   jax.ShapeDtypeStruct((N_EXPERTS, LANES), jnp.int32)],
        scratch_shapes=[pltpu.VMEM((nT, TOP_K, M), jnp.int32), pltpu.VMEM((nT, TOP_K, M), F32),
                        pltpu.VMEM((N_EXPERTS, LANES), F32), pltpu.VMEM((N_EXPERTS, LANES), F32)],
        compiler_params=_cparams(2),
        name="route",
    )(hp, router_w_t, router_bias.reshape(-1, N_EXPERTS, 1))


def _sc_mesh():
    return plsc.VectorSubcoreMesh(core_axis_name="core", subcore_axis_name="subcore")


def sc_invert(pos_flat, n_tok, n_out):
    n = pos_flat.shape[0]
    per = n_out // SC_WORKERS
    chunk = n_tok
    assert n_out % SC_WORKERS == 0 and per % SC_LANES == 0
    assert n_tok % chunk == 0 and n % chunk == 0 and chunk % SC_LANES == 0
    cp = pltpu.CompilerParams()
    if "needs_layout_passes" in pltpu.CompilerParams.__dataclass_fields__:
        cp = dataclasses.replace(cp, needs_layout_passes=False)

    @functools.partial(
        pl.kernel, out_type=jax.ShapeDtypeStruct((n_out,), jnp.int32), mesh=_sc_mesh(),
        scratch_types=[pltpu.VMEM((chunk,), jnp.int32), pltpu.VMEM((per,), jnp.int32)],
        compiler_params=cp, name="sc_invert")
    def k(pos_hbm, src_hbm, pos_v, src_v):
        wid = lax.axis_index("subcore") * SC_CORES + lax.axis_index("core")
        lo = wid * per
        lane = lax.iota(jnp.int32, SC_LANES)

        @pl.loop(0, per, step=SC_LANES)
        def _(r):
            src_v[pl.ds(r, SC_LANES)] = lax.rem(lo + r + lane, n_tok)

        @pl.loop(0, n // chunk)
        def _(c):
            base = c * chunk
            pltpu.sync_copy(pos_hbm.at[pl.ds(base, chunk)], pos_v)
            tok0 = lax.rem(base, n_tok)

            @plsc.parallel_loop(0, chunk, step=SC_LANES, unroll=8)
            def _(r):
                p = pos_v[pl.ds(r, SC_LANES)] - lo
                mine = (p >= 0) & (p < per)
                plsc.store_scatter(src_v, [jnp.where(mine, p, 0)], tok0 + r + lane, mask=mine)

        pltpu.sync_copy(src_v, src_hbm.at[pl.ds(lo, per)])

    return k(pos_flat)


def sc_gather(x, idx):
    n = idx.shape[0]
    dim = x.shape[1]
    assert n % (SC_WINDOW * SC_WORKERS) == 0

    @functools.partial(
        pl.kernel, out_type=jax.ShapeDtypeStruct((n, dim), x.dtype), mesh=_sc_mesh(),
        scratch_types=[], name="sc_gather")
    def k(x_hbm, i_hbm, o_hbm):
        def body(i_vmem, o_vmem):
            pltpu.sync_copy(x_hbm.at[i_vmem.at[0]], o_vmem)

        pltpu.emit_pipeline(
            body, grid=(n // SC_WINDOW,),
            in_specs=[pl.BlockSpec((1, SC_WINDOW), index_map=lambda i: (i, 0))],
            out_specs=[pl.BlockSpec((SC_WINDOW, dim), index_map=lambda i: (i, 0))],
            core_axis_name=("core", "subcore"),
            dimension_semantics=(pltpu.PARALLEL,),
        )(i_hbm, o_hbm)

    return k(x, idx.reshape(n // SC_WINDOW, SC_WINDOW))


def _moe_gemm_kernel(ts_ref, tn_ref, x_hbm, wi_ref, wo_ref, o_hbm, wi_b, wo_b, xbuf, obuf, in_sem, out_sem,
                     *, tile_rows, n_tiles):
    e = pl.program_id(0)
    last = pl.num_programs(0) - 1
    t0 = ts_ref[e]
    n = tn_ref[e]
    n_used = ts_ref[last] + tn_ref[last]

    def x_copy(g, slot):
        rows = pl.ds(pl.multiple_of(g * tile_rows, tile_rows), tile_rows)
        return pltpu.make_async_copy(x_hbm.at[rows], xbuf.at[slot], in_sem.at[slot])

    def o_copy(g, slot):
        rows = pl.ds(pl.multiple_of(g * tile_rows, tile_rows), tile_rows)
        return pltpu.make_async_copy(obuf.at[slot], o_hbm.at[rows], out_sem.at[slot])

    @pl.when(e == 0)
    def _():
        for g0 in range(MOE_NBUF - 1):
            @pl.when(g0 < n_used)
            def _():
                x_copy(g0, g0).start()

    @pl.when(n > 0)
    def _():
        wi_b[...] = wi_ref[...].astype(BF16)
        wo_b[...] = wo_ref[...].astype(BF16)

    def tile(i, carry):
        g = t0 + i
        slot = lax.rem(g, MOE_NBUF)
        x_copy(g, slot).wait()
        ahead = g + (MOE_NBUF - 1)

        @pl.when(ahead < n_used)
        def _():
            x_copy(ahead, lax.rem(ahead, MOE_NBUF)).start()

        @pl.when(g >= MOE_NBUF)
        def _():
            o_copy(g - MOE_NBUF, slot).wait()

        rows = tile_rows // MOE_SUB
        half = xbuf.shape[2]
        xs = [_unpack_pairs(xbuf[slot, r * rows:(r + 1) * rows, :]) for r in range(MOE_SUB)]
        hus = [_dot(lo, wi_b[:half, :]) + _dot(hi, wi_b[half:, :]) for lo, hi in xs]
        acts = [(_silu(hu[:, :EXPERT_FF]) * hu[:, EXPERT_FF:]).astype(BF16) for hu in hus]
        outs = [_dot(act, wo_b[...]) for act in acts]
        for r, out in enumerate(outs):
            obuf[slot, r * rows:(r + 1) * rows, :] = _pack_pairs(out)
        o_copy(g, slot).start()
        return carry

    lax.fori_loop(0, n, tile, 0)

    @pl.when(e == last)
    def _():
        for back in range(MOE_NBUF, 0, -1):
            @pl.when(n_used >= back)
            def _():
                o_copy(n_used - back, lax.rem(n_used - back, MOE_NBUF)).wait()

        obuf[...] = jnp.zeros(obuf.shape, obuf.dtype)
        n_clear = n_tiles - n_used

        def clear(i, carry):
            slot = lax.rem(i, MOE_NBUF)

            @pl.when(i >= MOE_NBUF)
            def _():
                o_copy(n_used + i - MOE_NBUF, slot).wait()

            o_copy(n_used + i, slot).start()
            return carry

        lax.fori_loop(0, n_clear, clear, 0)
        for back in range(MOE_NBUF, 0, -1):
            @pl.when(n_clear >= back)
            def _():
                o_copy(n_tiles - back, lax.rem(n_clear - back, MOE_NBUF)).wait()


def moe_gemm(xs, tile_start, tile_count, exp_w_in, exp_w_out, l, tile_rows, n_tiles):
    P, half = xs.shape
    D = 2 * half
    assert P == n_tiles * tile_rows
    hbm = pl.BlockSpec(memory_space=pl.ANY)
    grid_spec = pltpu.PrefetchScalarGridSpec(
        num_scalar_prefetch=2,
        grid=(N_EXPERTS,),
        in_specs=[hbm,
                  pl.BlockSpec((None, None, D, 2 * EXPERT_FF), lambda e, ts, tn: (l, e, 0, 0)),
                  pl.BlockSpec((None, None, EXPERT_FF, D), lambda e, ts, tn: (l, e, 0, 0))],
        out_specs=hbm,
        scratch_shapes=[pltpu.VMEM((D, 2 * EXPERT_FF), BF16), pltpu.VMEM((EXPERT_FF, D), BF16),
                        pltpu.VMEM((MOE_NBUF, tile_rows, half), jnp.int32),
                        pltpu.VMEM((MOE_NBUF, tile_rows, half), jnp.int32),
                        pltpu.SemaphoreType.DMA((MOE_NBUF,)), pltpu.SemaphoreType.DMA((MOE_NBUF,))],
    )
    return pl.pallas_call(
        functools.partial(_moe_gemm_kernel, tile_rows=tile_rows, n_tiles=n_tiles),
        grid_spec=grid_spec,
        out_shape=jax.ShapeDtypeStruct((P, half), jnp.int32),
        compiler_params=_cparams(1),
        name="moe_gemm",
    )(tile_start, tile_count, xs, exp_w_in, exp_w_out)


def _moe_combine_kernel(*refs, final, norm_next, shared_kv, queries):
    it = iter(refs)
    y_ref, w_ref, h_ref, si_ref, so_ref, x_ref, g2_ref = (next(it) for _ in range(7))
    fg_ref = next(it) if final else None
    ng_ref, nsc_ref, nsh_ref = (next(it) for _ in range(3)) if norm_next else (None,) * 3
    kg_ref, wkv_ref, lg_ref, cos_ref, sin_ref = (next(it) for _ in range(5)) if shared_kv else (None,) * 5
    wdq_ref, qg_ref, wqt_ref, wqrt_ref, cost_ref, sint_ref = (next(it) for _ in range(6)) if queries else (None,) * 6
    o_ref = next(it)
    hn_ref = next(it) if norm_next and not queries else None
    lat_ref, kr_ref = (next(it), next(it)) if shared_kv else (None, None)
    qt_ref = next(it) if queries else None
    si_b, so_b = next(it), next(it)
    wkv_b = next(it) if shared_kv else None
    wdq_b, wqt_b, wqrt_b = (next(it), next(it), next(it)) if queries else (None,) * 3

    @pl.when(pl.program_id(0) == 0)
    def _():
        si_b[...] = si_ref[...].astype(BF16)
        so_b[...] = so_ref[...].astype(BF16)
        if shared_kv:
            wkv_b[...] = wkv_ref[...].astype(BF16)
        if queries:
            wdq_b[...] = wdq_ref[...].astype(BF16)
            wqt_b[...] = wqt_ref[...].astype(BF16)
            wqrt_b[...] = wqrt_ref[...].astype(BF16)

    bb, tt, D = x_ref.shape
    half = D // 2
    w = w_ref[...].T
    acc_lo = jnp.zeros((bb * tt, half), F32)
    acc_hi = jnp.zeros((bb * tt, half), F32)
    for k in range(TOP_K):
        lo, hi = _unpack_pairs(y_ref[k], F32)
        acc_lo = acc_lo + w[:, k:k + 1] * lo
        acc_hi = acc_hi + w[:, k:k + 1] * hi
    hlo, hhi = _unpack_pairs(h_ref[...])
    hu = _dot(hlo, si_b[:half, :]) + _dot(hhi, si_b[half:, :])
    act = (_silu(hu[:, :SHARED_FF]) * hu[:, SHARED_FF:]).astype(BF16)
    y = jnp.concatenate([acc_lo, acc_hi], axis=-1) + _dot(act, so_b[...])
    x_new = x_ref[...] + g2_ref[...] * y.reshape(bb, tt, D)
    o_ref[...] = _rms(x_new, fg_ref[...]) if final else x_new
    if norm_next:
        hn = (_rms(x_new, ng_ref[...]) * (1.0 + nsc_ref[...]) + nsh_ref[...]).astype(BF16)
        if queries:
            _queries_t(hn.reshape(bb * tt, D), wdq_b, qg_ref[...], wqt_b, wqrt_b, cost_ref[...], sint_ref[...],
                       qt_ref)
        else:
            hn_ref[...] = hn
    if shared_kv:
        xn = _rms(x_new, kg_ref[...]).reshape(bb * tt, D).astype(BF16)
        z = _dot(xn, wkv_b[...])
        lat_ref[...] = _rms(z[:, :KV_LORA], lg_ref[...]).reshape(bb, tt, KV_LORA)
        zr = z[:, KV_LORA:KV_LORA + MLA_ROPE].reshape(bb, tt, MLA_ROPE)
        zq = z[:, KV_LORA + LANES:KV_LORA + LANES + MLA_ROPE].reshape(bb, tt, MLA_ROPE)
        kr_ref[...] = zr * cos_ref[...] + zq * sin_ref[...]


def moe_combine(y8, w8, hp, sh_w_in, sh_w_out, x, mod, gate_idx, l, row0, final_g=None, next_norm=None,
                shared_kv=None, queries=None, rows=512):
    B, T, D = x.shape
    half = D // 2
    bb, tt, nblk, ij = _row_blocks(B, T, rows)
    M = bb * tt
    assert row0 % M == 0
    off = row0 // M
    xspec = pl.BlockSpec((bb, tt, D), lambda i: ij(i) + (0,))
    in_specs = [pl.BlockSpec((TOP_K, M, half), lambda i: (0, off + i, 0)),
                pl.BlockSpec((TOP_K, M), lambda i: (0, off + i)),
                pl.BlockSpec((M, half), lambda i: (off + i, 0)),
                pl.BlockSpec((None, D, 2 * SHARED_FF), lambda i: (l, 0, 0)),
                pl.BlockSpec((None, SHARED_FF, D), lambda i: (l, 0, 0)),
                xspec,
                pl.BlockSpec((bb, 1, D), lambda i: (ij(i)[0], 0, gate_idx))]
    args = [y8, w8, hp, sh_w_in, sh_w_out, x, mod]
    out_specs = xspec
    out_shape = jax.ShapeDtypeStruct((B, T, D), F32)
    if final_g is not None:
        assert next_norm is None
        in_specs.append(pl.BlockSpec((1, D), lambda i: (0, 0)))
        args.append(final_g.reshape(1, D))
    if next_norm is not None:
        gain, mod_next, sc_idx, sh_idx = next_norm
        in_specs += [pl.BlockSpec((1, D), lambda i: (0, 0)),
                     pl.BlockSpec((bb, 1, D), lambda i: (ij(i)[0], 0, sc_idx)),
                     pl.BlockSpec((bb, 1, D), lambda i: (ij(i)[0], 0, sh_idx))]
        args += [gain.reshape(1, D), mod_next, mod_next]
        if queries is None:
            out_specs = [xspec, xspec]
            out_shape = [out_shape, jax.ShapeDtypeStruct((B, T, D), BF16)]
    scratch = [pltpu.VMEM((D, 2 * SHARED_FF), BF16), pltpu.VMEM((SHARED_FF, D), BF16)]
    if shared_kv is not None:
        kv_in_g, w_kv, kv_lat_g, cos32, sin32 = shared_kv
        tspec = pl.BlockSpec((tt, MLA_ROPE), lambda i: (ij(i)[1], 0))
        in_specs += [pl.BlockSpec((1, D), lambda i: (0, 0)),
                     pl.BlockSpec(w_kv.shape, lambda i: (0, 0)),
                     pl.BlockSpec((1, KV_LORA), lambda i: (0, 0)),
                     tspec, tspec]
        args += [kv_in_g.reshape(1, D), w_kv, kv_lat_g.reshape(1, KV_LORA), cos32, sin32]
        out_specs = list(out_specs) if isinstance(out_specs, list) else [out_specs]
        out_shape = list(out_shape) if isinstance(out_shape, list) else [out_shape]
        out_specs += [pl.BlockSpec((bb, tt, KV_LORA), lambda i: ij(i) + (0,)),
                      pl.BlockSpec((bb, tt, MLA_ROPE), lambda i: ij(i) + (0,))]
        out_shape += [jax.ShapeDtypeStruct((B, T, KV_LORA), F32), jax.ShapeDtypeStruct((B, T, MLA_ROPE), F32)]
        scratch.append(pltpu.VMEM(w_kv.shape, BF16))
    if queries is not None:
        assert next_norm is not None and bb == 1
        w_dq, q_norm_g, wq_t, wqr_t, bi, cos_t, sin_t = queries
        NQ, NR = wq_t.shape[1], wqr_t.shape[1]
        tspec_t = pl.BlockSpec((MLA_ROPE, tt), lambda i: (0, ij(i)[1]))
        in_specs += [pl.BlockSpec((None, D, Q_LORA), lambda i: (bi, 0, 0)),
                     pl.BlockSpec((None, 1, Q_LORA), lambda i: (bi, 0, 0)),
                     pl.BlockSpec((None, NQ, Q_LORA), lambda i: (bi, 0, 0)),
                     pl.BlockSpec((None, NR, Q_LORA), lambda i: (bi, 0, 0)),
                     tspec_t, tspec_t]
        args += [w_dq, q_norm_g.reshape(-1, 1, Q_LORA), wq_t, wqr_t, cos_t, sin_t]
        out_specs = list(out_specs) if isinstance(out_specs, list) else [out_specs]
        out_shape = list(out_shape) if isinstance(out_shape, list) else [out_shape]
        out_specs.append(pl.BlockSpec((1, NQ, tt), lambda i: (ij(i)[0], 0, ij(i)[1])))
        out_shape.append(jax.ShapeDtypeStruct((B, NQ, T), BF16))
        scratch += [pltpu.VMEM((D, Q_LORA), BF16), pltpu.VMEM((NQ, Q_LORA), BF16), pltpu.VMEM((NR, Q_LORA), BF16)]
    return pl.pallas_call(
        functools.partial(_moe_combine_kernel, final=final_g is not None, norm_next=next_norm is not None,
                          shared_kv=shared_kv is not None, queries=queries is not None),
        grid=(nblk,),
        in_specs=in_specs,
        out_specs=out_specs,
        out_shape=out_shape,
        scratch_shapes=scratch,
        compiler_params=_cparams(1),
        name="moe_combine",
    )(*args)


def _kv_expand_kernel(lat_ref, kr_ref, wk_ref, ek_ref, wvt_ref, ones_ref, k_ref, vt_ref):
    lat = lat_ref[0].astype(BF16)
    kr = kr_ref[0].astype(BF16)
    k = _dot(lat, wk_ref[...].astype(BF16)) + _dot(kr, ek_ref[...].astype(BF16))
    k_ref[0] = k.astype(k_ref.dtype)
    vt = _dot_nt(wvt_ref[...].astype(BF16), lat) + ones_ref[...]
    vt_ref[0] = vt.astype(vt_ref.dtype)


def kv_expand(lat, kr, wk_pad, ek, wvt_ext, ones_col, rows=512):
    B, T, _ = lat.shape
    tt = rows
    NK, NVT = wk_pad.shape[1], wvt_ext.shape[0]

    def full(a):
        return pl.BlockSpec(a.shape, lambda b, t: (0, 0))

    def rowspec(n):
        return pl.BlockSpec((1, tt, n), lambda b, t: (b, t, 0))

    return pl.pallas_call(
        _kv_expand_kernel,
        grid=(B, T // tt),
        in_specs=[rowspec(KV_LORA), rowspec(MLA_ROPE), full(wk_pad), full(ek), full(wvt_ext), full(ones_col)],
        out_specs=[rowspec(NK), pl.BlockSpec((1, NVT, tt), lambda b, t: (b, 0, t))],
        out_shape=[jax.ShapeDtypeStruct((B, T, NK), BF16), jax.ShapeDtypeStruct((B, NVT, T), BF16)],
        compiler_params=_cparams(2),
        name="kv_expand",
    )(lat, kr, wk_pad, ek, wvt_ext, ones_col)


def _query_kernel(h_ref, wdq_ref, qg_ref, wq_ref, wqr_ref, c_ref, s_ref, q_ref, wdq_b, wq_b, wqr_b):
    @pl.when(pl.program_id(0) == 0)
    def _():
        wdq_b[...] = wdq_ref[...].astype(BF16)
        wq_b[...] = wq_ref[...].astype(BF16)
        wqr_b[...] = wqr_ref[...].astype(BF16)

    bb, tt, D = h_ref.shape
    h = h_ref[...].reshape(bb * tt, D)
    cq = _rms(_dot(h, wdq_b[...]), qg_ref[...]).astype(BF16)
    q1 = _dot(cq, wq_b[...]).reshape(bb, tt, -1)
    q2 = _dot(cq, wqr_b[...]).reshape(bb, tt, -1)
    c = c_ref[...]
    s = s_ref[...]
    for hd in range(MLA_HEADS):
        sl = slice(hd * HEAD_PAD, (hd + 1) * HEAD_PAD)
        q_ref[:, :, sl] = (q1[:, :, sl] * c + q2[:, :, sl] * s).astype(q_ref.dtype)


def mla_queries(h, w_dq, q_norm_g, wq_pad, wq_rot, l, c128, s128, rows=512):
    B, T, D = h.shape
    bb, tt, nblk, ij = _row_blocks(B, T, rows)
    NQ = wq_pad.shape[-1]
    tspec = pl.BlockSpec((tt, HEAD_PAD), lambda i: (ij(i)[1], 0))
    return pl.pallas_call(
        _query_kernel,
        grid=(nblk,),
        in_specs=[pl.BlockSpec((bb, tt, D), lambda i: ij(i) + (0,)),
                  pl.BlockSpec((None, D, Q_LORA), lambda i: (l, 0, 0)),
                  pl.BlockSpec((None, 1, Q_LORA), lambda i: (l, 0, 0)),
                  pl.BlockSpec((None, Q_LORA, NQ), lambda i: (l, 0, 0)),
                  pl.BlockSpec((None, Q_LORA, NQ), lambda i: (l, 0, 0)),
                  tspec, tspec],
        out_specs=pl.BlockSpec((bb, tt, NQ), lambda i: ij(i) + (0,)),
        out_shape=jax.ShapeDtypeStruct((B, T, NQ), BF16),
        scratch_shapes=[pltpu.VMEM((D, Q_LORA), BF16), pltpu.VMEM((Q_LORA, NQ), BF16),
                        pltpu.VMEM((Q_LORA, NQ), BF16)],
        compiler_params=_cparams(1),
        name="mla_queries",
    )(h, w_dq, q_norm_g.reshape(-1, 1, Q_LORA), wq_pad, wq_rot, c128, s128)


def _queries_t(h, wdq_b, qg, wqt_b, wqrt_b, cos, sin, qt_ref):
    cq = _rms(_dot(h, wdq_b[...]), qg).astype(BF16)
    q1 = _dot_nt(wqt_b[...], cq)
    q2 = _dot_nt(wqrt_b[...], cq)
    pad = jnp.zeros((HEAD_PAD - MLA_NOPE - MLA_ROPE, q1.shape[1]), qt_ref.dtype)
    for hd in range(MLA_HEADS):
        r0 = hd * HEAD_PAD
        rope = (q1[r0 + MLA_NOPE:r0 + MLA_NOPE + MLA_ROPE] * cos
                + q2[hd * MLA_ROPE:(hd + 1) * MLA_ROPE] * sin)
        qt_ref[0, r0:r0 + MLA_NOPE, :] = (q1[r0:r0 + MLA_NOPE] * Q_PRESCALE).astype(qt_ref.dtype)
        qt_ref[0, r0 + MLA_NOPE:r0 + MLA_NOPE + MLA_ROPE, :] = rope.astype(qt_ref.dtype)
        qt_ref[0, r0 + MLA_NOPE + MLA_ROPE:r0 + HEAD_PAD, :] = pad


def _query_t_kernel(h_ref, wdq_ref, qg_ref, wqt_ref, wqrt_ref, cos_ref, sin_ref, qt_ref, wdq_b, wqt_b, wqrt_b):
    @pl.when((pl.program_id(0) == 0) & (pl.program_id(1) == 0))
    def _():
        wdq_b[...] = wdq_ref[...].astype(BF16)
        wqt_b[...] = wqt_ref[...].astype(BF16)
        wqrt_b[...] = wqrt_ref[...].astype(BF16)

    _queries_t(h_ref[0], wdq_b, qg_ref[...], wqt_b, wqrt_b, cos_ref[...], sin_ref[...], qt_ref)


def mla_queries_t(h, w_dq, q_norm_g, wq_t, wqr_t, l, cos_t, sin_t, rows=512):
    B, T, D = h.shape
    tt = rows
    NQ = wq_t.shape[1]
    NR = wqr_t.shape[1]
    tspec = pl.BlockSpec((MLA_ROPE, tt), lambda b, t: (0, t))
    return pl.pallas_call(
        _query_t_kernel,
        grid=(B, T // tt),
        in_specs=[pl.BlockSpec((1, tt, D), lambda b, t: (b, t, 0)),
                  pl.BlockSpec((None, D, Q_LORA), lambda b, t: (l, 0, 0)),
                  pl.BlockSpec((None, 1, Q_LORA), lambda b, t: (l, 0, 0)),
                  pl.BlockSpec((None, NQ, Q_LORA), lambda b, t: (l, 0, 0)),
                  pl.BlockSpec((None, NR, Q_LORA), lambda b, t: (l, 0, 0)),
                  tspec, tspec],
        out_specs=pl.BlockSpec((1, NQ, tt), lambda b, t: (b, 0, t)),
        out_shape=jax.ShapeDtypeStruct((B, NQ, T), BF16),
        scratch_shapes=[pltpu.VMEM((D, Q_LORA), BF16), pltpu.VMEM((NQ, Q_LORA), BF16),
                        pltpu.VMEM((NR, Q_LORA), BF16)],
        compiler_params=_cparams(2),
        name="mla_queries_t",
    )(h, w_dq, q_norm_g.reshape(-1, 1, Q_LORA), wq_t, wqr_t, cos_t, sin_t)


def _attn_prompt_kernel(qi_tab, ki_tab, qt_ref, k_ref, vt_ref, o_ref, *scratch, tq, tk):
    H = MLA_HEADS
    m_refs, l_refs, acc_refs = scratch[:H], scratch[H:2 * H], scratch[2 * H:]
    p_id = pl.program_id(1)
    qi = qi_tab[p_id]
    ki = ki_tab[p_id]

    @pl.when(ki == 0)
    def _():
        for hd in range(H):
            m_refs[hd][...] = jnp.full(m_refs[hd].shape, NEG_INF, F32)
            l_refs[hd][...] = jnp.zeros(l_refs[hd].shape, F32)
            acc_refs[hd][...] = jnp.zeros(acc_refs[hd].shape, F32)

    def block(masked):
        if masked:
            kchunk = (ki * tk + lax.broadcasted_iota(jnp.int32, (tk, tq), 0)) // CHUNK
            qchunk = (qi * tq + lax.broadcasted_iota(jnp.int32, (tk, tq), 1)) // CHUNK
            mask = kchunk <= qchunk
        def scores(hd):
            sl = slice(hd * HEAD_PAD, (hd + 1) * HEAD_PAD)
            return _dot(k_ref[0, :, sl], qt_ref[0, sl, :])

        pending = [scores(hd) for hd in range(ATTN_LOOKAHEAD)]
        for hd in range(H):
            if hd + ATTN_LOOKAHEAD < H:
                pending.append(scores(hd + ATTN_LOOKAHEAD))
            s = pending.pop(0)
            if masked:
                s = jnp.where(mask, s, NEG_INF)
            m_prev = m_refs[hd][...]
            m_new = jnp.maximum(m_prev, jnp.max(s, axis=0, keepdims=True))
            a = jnp.exp2(m_prev - m_new)
            p = jnp.exp2(s - m_new).astype(BF16)
            pv = _dot(vt_ref[0, hd * V_ROWS:(hd + 1) * V_ROWS, :], p)
            acc_refs[hd][...] = a * acc_refs[hd][...] + pv[:MLA_V]
            l_refs[hd][...] = a * l_refs[hd][...] + pv[MLA_V:MLA_V + 1]
            m_refs[hd][...] = m_new

    @pl.when(ki < qi)
    def _():
        block(False)

    @pl.when(ki == qi)
    def _():
        block(True)
        o_t = jnp.concatenate([acc_refs[hd][...] / l_refs[hd][...] for hd in range(H)], axis=0)
        o_ref[0] = o_t.T.astype(o_ref.dtype)


def attn_prompt(qt, k, vt, tq=256):
    B, NQ, T = qt.shape
    NVT = vt.shape[1]
    NV = MLA_HEADS * MLA_V
    tk = tq
    assert tq % CHUNK == 0
    nq = T // tq
    pairs = [(a, b) for a in range(nq) for b in range(a + 1)]
    qi_tab = jnp.asarray([a for a, _ in pairs], jnp.int32)
    ki_tab = jnp.asarray([b for _, b in pairs], jnp.int32)
    grid_spec = pltpu.PrefetchScalarGridSpec(
        num_scalar_prefetch=2,
        grid=(B, len(pairs)),
        in_specs=[pl.BlockSpec((1, NQ, tq), lambda b, p, qt, kt: (b, 0, qt[p])),
                  pl.BlockSpec((1, tk, NQ), lambda b, p, qt, kt: (b, kt[p], 0)),
                  pl.BlockSpec((1, NVT, tk), lambda b, p, qt, kt: (b, 0, kt[p]))],
        out_specs=pl.BlockSpec((1, tq, NV), lambda b, p, qt, kt: (b, qt[p], 0)),
        scratch_shapes=([pltpu.VMEM((1, tq), F32)] * (2 * MLA_HEADS)
                        + [pltpu.VMEM((MLA_V, tq), F32)] * MLA_HEADS),
    )
    return pl.pallas_call(
        functools.partial(_attn_prompt_kernel, tq=tq, tk=tk),
        grid_spec=grid_spec,
        out_shape=jax.ShapeDtypeStruct((B, T, NV), BF16),
        compiler_params=_cparams(2),
        name="attn_prompt",
    )(qi_tab, ki_tab, qt, k, vt)


def _absorb_kernel(q_ref, m_ref, o_ref):
    o_ref[...] = _dot(q_ref[...], m_ref[...].astype(BF16)).astype(o_ref.dtype)


def absorb_queries(q2d, m_abs):
    N = q2d.shape[0]
    H, _, W = m_abs.shape
    return pl.pallas_call(
        _absorb_kernel,
        grid=(H,),
        in_specs=[pl.BlockSpec((N, HEAD_PAD), lambda h: (0, h)),
                  pl.BlockSpec((None, HEAD_PAD, W), lambda h: (h, 0, 0))],
        out_specs=pl.BlockSpec((None, N, W), lambda h: (h, 0, 0)),
        out_shape=jax.ShapeDtypeStruct((H, N, W), BF16),
        compiler_params=_cparams(1),
        name="absorb_queries",
    )(q2d, m_abs)


def _attn_sample_kernel(q_ref, lat_ref, kr_ref, nlat_ref, nkr_ref, o_ref, m_ref, l_ref, acc_ref):
    kb = pl.program_id(1)
    H, Q, W = q_ref.shape
    q = q_ref[...].reshape(H * Q, W)
    q_lat = q[:, :KV_LORA]
    q_rope = q[:, KV_LORA:KV_LORA + MLA_ROPE]

    def update(lat_tile, kr_tile, n_sub, kr_transposed):
        sub = lat_tile.shape[0] // n_sub
        lats = [lat_tile[j * sub:(j + 1) * sub, :].astype(BF16) for j in range(n_sub)]
        if kr_transposed:
            krs = [kr_tile[:, j * sub:(j + 1) * sub].astype(BF16) for j in range(n_sub)]
            ss = [_dot_nt(q_lat, lat) + _dot(q_rope, kr) for lat, kr in zip(lats, krs)]
        else:
            krs = [kr_tile[j * sub:(j + 1) * sub, :].astype(BF16) for j in range(n_sub)]
            ss = [_dot_nt(q_lat, lat) + _dot_nt(q_rope, kr) for lat, kr in zip(lats, krs)]
        m_prev = m_ref[...]
        m_new = m_prev
        for s in ss:
            m_new = jnp.maximum(m_new, jnp.max(s, axis=-1, keepdims=True))
        a = jnp.exp2(m_prev - m_new)
        ps = [jnp.exp2(s - m_new[:, :1]) for s in ss]
        pv = _dot(ps[0].astype(BF16), lats[0])
        psum = jnp.sum(ps[0], axis=-1, keepdims=True)
        for p, lat in zip(ps[1:], lats[1:]):
            pv = pv + _dot(p.astype(BF16), lat)
            psum = psum + jnp.sum(p, axis=-1, keepdims=True)
        l_ref[...] = a * l_ref[...] + psum
        m_ref[...] = m_new
        acc_ref[...] = jnp.concatenate([a, a], axis=-1) * acc_ref[...] + pv

    @pl.when(kb == 0)
    def _():
        m_ref[...] = jnp.full_like(m_ref, NEG_INF)
        l_ref[...] = jnp.zeros_like(l_ref)
        acc_ref[...] = jnp.zeros_like(acc_ref)
        update(nlat_ref[0], nkr_ref[0], 1, False)

    update(lat_ref[0], kr_ref[0], SAMPLE_KEY_SUB, True)

    @pl.when(kb == pl.num_programs(1) - 1)
    def _():
        lsum = l_ref[...]
        o = acc_ref[...] / jnp.concatenate([lsum, lsum], axis=-1)
        o_ref[...] = o.reshape(H, Q, KV_LORA).astype(o_ref.dtype)


def attn_sample(q_abs, cache_lat, cache_kr_t, new_lat, new_kr, tk=4096):
    H, N, W = q_abs.shape
    B, P, _ = cache_lat.shape
    Q = new_lat.shape[1]
    qpos = P + np.arange(Q)
    kpos = np.arange(P + Q)
    assert bool(np.all((kpos // CHUNK)[None, :] <= (qpos // CHUNK)[:, None]))
    return pl.pallas_call(
        _attn_sample_kernel,
        grid=(B, P // tk),
        in_specs=[pl.BlockSpec((H, Q, W), lambda b, kb: (0, b, 0)),
                  pl.BlockSpec((1, tk, KV_LORA), lambda b, kb: (b, kb, 0)),
                  pl.BlockSpec((1, MLA_ROPE, tk), lambda b, kb: (b, 0, kb)),
                  pl.BlockSpec((1, Q, KV_LORA), lambda b, kb: (b, 0, 0)),
                  pl.BlockSpec((1, Q, MLA_ROPE), lambda b, kb: (b, 0, 0))],
        out_specs=pl.BlockSpec((H, Q, KV_LORA), lambda b, kb: (0, b, 0)),
        out_shape=jax.ShapeDtypeStruct((H, N, KV_LORA), BF16),
        scratch_shapes=[pltpu.VMEM((H * Q, LANES), F32), pltpu.VMEM((H * Q, LANES), F32),
                        pltpu.VMEM((H * Q, KV_LORA), F32)],
        compiler_params=_cparams(2),
        name="attn_sample",
    )(q_abs, cache_lat, cache_kr_t, new_lat, new_kr)


def _unabsorb_kernel(o_ref, w_ref, out_ref):
    out_ref[...] = (_dot(o_ref[0], w_ref[0].astype(BF16))
                    + _dot(o_ref[1], w_ref[1].astype(BF16))).astype(out_ref.dtype)


def unabsorb(o_lat, wuv_pad):
    H, N, R = o_lat.shape
    return pl.pallas_call(
        _unabsorb_kernel,
        grid=(H // 2,),
        in_specs=[pl.BlockSpec((2, N, R), lambda p: (p, 0, 0)),
                  pl.BlockSpec((2, R, 2 * MLA_V), lambda p: (p, 0, 0))],
        out_specs=pl.BlockSpec((N, 2 * MLA_V), lambda p: (0, p)),
        out_shape=jax.ShapeDtypeStruct((N, H * MLA_V), BF16),
        compiler_params=_cparams(1),
        name="unabsorb",
    )(o_lat, wuv_pad)


def _rope_tables(pos):
    half = MLA_ROPE // 2
    inv = 1.0 / (ROPE_THETA ** (np.arange(half, dtype=np.float64) * 2.0 / MLA_ROPE))
    ang = np.asarray(pos, np.float64)[:, None] * inv[None, :]
    cos = np.concatenate([np.cos(ang), np.cos(ang)], axis=-1)
    sin = np.concatenate([np.sin(ang), np.sin(ang)], axis=-1)
    T = cos.shape[0]
    c128 = np.zeros((T, HEAD_PAD)); s128 = np.zeros((T, HEAD_PAD))
    c128[:, :MLA_NOPE] = 1.0
    c128[:, MLA_NOPE:MLA_NOPE + MLA_ROPE] = cos
    s128[:, MLA_NOPE:MLA_NOPE + MLA_ROPE] = sin
    return dict(cos32=jnp.asarray(cos, F32), sin32=jnp.asarray(sin, F32),
                c128=jnp.asarray(c128 * Q_PRESCALE, F32), s128=jnp.asarray(s128 * Q_PRESCALE, F32),
                cos_t=jnp.asarray(cos.T * Q_PRESCALE, F32), sin_t=jnp.asarray(sin.T * Q_PRESCALE, F32))


def _rot_half_cols(w):
    half = w.shape[-1] // 2
    return jnp.concatenate([-w[..., half:], w[..., :half]], axis=-1)


def _prep_weights(w_dkv, w_uk, w_uv, w_uq, router_w):
    D = D_MODEL
    w_lat, w_rope = w_dkv[:, :KV_LORA], w_dkv[:, KV_LORA:]
    pad96 = jnp.zeros((D, LANES - MLA_ROPE), F32)
    w_kv = jnp.concatenate([w_lat, w_rope, pad96, _rot_half_cols(w_rope), pad96], axis=-1)

    zpad = HEAD_PAD - MLA_NOPE
    wk_pad = jnp.pad(w_uk, ((0, 0), (0, 0), (0, zpad))).reshape(KV_LORA, MLA_HEADS * HEAD_PAD)
    ek = jnp.zeros((MLA_ROPE, MLA_HEADS, HEAD_PAD), F32)
    ek = ek.at[:, :, MLA_NOPE:MLA_NOPE + MLA_ROPE].set(
        jnp.broadcast_to(jnp.eye(MLA_ROPE, dtype=F32)[:, None, :], (MLA_ROPE, MLA_HEADS, MLA_ROPE)))
    ek = ek.reshape(MLA_ROPE, MLA_HEADS * HEAD_PAD)
    wvt = jnp.transpose(w_uv, (1, 2, 0))
    wvt_ext = jnp.pad(wvt, ((0, 0), (0, V_ROWS - MLA_V), (0, 0))).reshape(MLA_HEADS * V_ROWS, KV_LORA)
    ones_col = jnp.tile((jnp.arange(V_ROWS) >= MLA_V).astype(F32), MLA_HEADS).reshape(-1, 1)

    nb = w_uq.shape[0]
    qn, qr = w_uq[..., :MLA_NOPE], w_uq[..., MLA_NOPE:]
    z32 = jnp.zeros(qr.shape[:-1] + (HEAD_PAD - MLA_NOPE - MLA_ROPE,), F32)
    wq_pad = jnp.concatenate([qn, qr, z32], axis=-1).reshape(nb, Q_LORA, MLA_HEADS * HEAD_PAD)
    wq_rot = jnp.concatenate([jnp.zeros_like(qn), _rot_half_cols(qr), z32], axis=-1)
    wq_rot = wq_rot.reshape(nb, Q_LORA, MLA_HEADS * HEAD_PAD)
    wq_t = jnp.transpose(wq_pad, (0, 2, 1))
    wqr_t = jnp.transpose(_rot_half_cols(qr).reshape(nb, Q_LORA, MLA_HEADS * MLA_ROPE), (0, 2, 1))

    m_abs = jnp.zeros((MLA_HEADS, HEAD_PAD, KV_LORA + LANES), F32)
    m_abs = m_abs.at[:, :MLA_NOPE, :KV_LORA].set(jnp.transpose(w_uk, (1, 2, 0)))
    m_abs = m_abs.at[:, MLA_NOPE:MLA_NOPE + MLA_ROPE, KV_LORA:KV_LORA + MLA_ROPE].set(
        jnp.broadcast_to(jnp.eye(MLA_ROPE, dtype=F32), (MLA_HEADS, MLA_ROPE, MLA_ROPE)))

    wuv_h = jnp.transpose(w_uv, (1, 0, 2))
    even = jnp.pad(wuv_h, ((0, 0), (0, 0), (0, MLA_V)))
    odd = jnp.pad(wuv_h, ((0, 0), (0, 0), (MLA_V, 0)))
    wuv_pad = jnp.where((jnp.arange(MLA_HEADS) % 2 == 0)[:, None, None], even, odd)

    rw_t = jnp.transpose(router_w, (0, 2, 1))
    return dict(w_kv=w_kv, wk_pad=wk_pad, ek=ek, wvt_ext=wvt_ext, ones_col=ones_col, wq_pad=wq_pad, wq_rot=wq_rot, wq_t=wq_t, wqr_t=wqr_t,
                m_abs=m_abs, wuv_pad=wuv_pad, rw_t=rw_t)


def _mixer(st, l, P, W, packed):
    rows_kw = dict(rows_total=packed["total"], row0=packed["row0"], rows_buf=packed["buf"])
    x, m = st["x"], st["mod"][l]
    B, T, _ = x.shape
    n_a = P["hg_w_in"].shape[0]
    norm2 = (P["norm2_g"][l], 4, 3)
    if l < n_a:
        zf, zqig = hgrn_proj(x, P["norm1_g"][l], m, 1, 0, P["hg_w_in"], l)
        s0 = None if st["hg_state"] is None else st["hg_state"][l]
        o, s_new = gla(zqig, zf, st["lbs"][l], P["hg_onorm_g"][l], s0)
        st["hg_new"].append(s_new)
        st["x"], packed["buf"] = linear(o, P["hg_w_out"], l, F32, x=x, mod=m, gate_idx=2, next_norm=norm2,
                                        **rows_kw)
    else:
        bi = l - n_a
        h = st.pop("h_next", None)
        qt = st.pop("qt_next", None)
        if h is None and qt is None:
            h = norm_mod(x, P["norm1_g"][l], m, sc_idx=1, sh_idx=0)
        if st["past_lat"] is None:
            if qt is None:
                qt = mla_queries_t(h, P["w_dq"], P["q_norm_g"], W["wq_t"], W["wqr_t"], bi, st["cos_t"],
                                   st["sin_t"])
            o = attn_prompt(qt, st["k_all"], st["v_all"])
        else:
            q = mla_queries(h, P["w_dq"], P["q_norm_g"], W["wq_pad"], W["wq_rot"], bi, st["c128"], st["s128"])
            q_abs = absorb_queries(q.reshape(B * T, -1), W["m_abs"])
            o_lat = attn_sample(q_abs, st["past_lat"], st["past_kr"], st["lat"], st["kr"])
            o = unabsorb(o_lat, W["wuv_pad"]).reshape(B, T, -1)
        st["x"], packed["buf"] = linear(o, P["w_o"], bi, F32, x=x, mod=m, gate_idx=2, next_norm=norm2, **rows_kw)
    packed["row0"] += B * T


def _moe(groups, hp, l, P, W):
    n_tok = hp.shape[0]
    n_tiles = (TOP_K * n_tok) // MOE_TILE + N_EXPERTS
    pos, w8, tile_start, tile_count = route(hp, W["rw_t"], P["router_bias"], l, MOE_TILE)
    pos_flat = pos.reshape(-1)
    src = sc_invert(pos_flat, n_tok, n_tiles * MOE_TILE)
    xs = sc_gather(hp, src)
    out = moe_gemm(xs, tile_start[:, 0], tile_count[:, 0], P["exp_w_in"], P["exp_w_out"], l,
                   MOE_TILE, n_tiles)
    y8 = sc_gather(out, pos_flat).reshape(TOP_K, n_tok, -1)
    last = l == P["norm1_g"].shape[0] - 1
    with_kv = l == P["hg_w_in"].shape[0] - 1
    next_is_mla = not last and l + 1 >= P["hg_w_in"].shape[0]
    row0 = 0
    for st in groups:
        B, T, _ = st["x"].shape
        with_q = next_is_mla and st["past_lat"] is None
        outs = moe_combine(
            y8, w8, hp, P["sh_w_in"], P["sh_w_out"], st["x"], st["mod"][l], 5, l, row0,
            final_g=P["final_g"] if last else None,
            next_norm=(P["norm1_g"][l + 1], st["mod"][l + 1], 1, 0) if next_is_mla else None,
            shared_kv=(P["kv_in_g"], W["w_kv"], P["kv_lat_g"], st["cos32"], st["sin32"]) if with_kv else None,
            queries=(P["w_dq"], P["q_norm_g"], W["wq_t"], W["wqr_t"], l + 1 - P["hg_w_in"].shape[0],
                     st["cos_t"], st["sin_t"]) if with_q else None,
            rows=256 if with_q else 512)
        outs = list(outs) if isinstance(outs, (list, tuple)) else [outs]
        st["x"] = outs.pop(0)
        if next_is_mla and not with_q:
            st["h_next"] = outs.pop(0)
        if with_kv:
            st["lat"], st["kr"] = outs.pop(0), outs.pop(0)
        if with_q:
            st["qt_next"] = outs.pop(0)
        row0 += B * T


def _group_state(x, mod, pos, hg_state, past_lat, past_kr, lbs):
    return dict(x=x, mod=mod, hg_state=hg_state, past_lat=past_lat, past_kr=past_kr, lbs=lbs,
                **_rope_tables(pos), hg_new=[],
                lat=None, kr=None, k_all=None, v_all=None)


def kernel(x_prompt, x_sample, state_hgrn, cache_mla_latent, cache_mla_krope, c_prompt, c_sample, ada_w, ada_b, norm1_g, norm2_g, hg_w_in, hg_lb_logits, hg_onorm_g, hg_w_out, kv_in_g, w_dkv, kv_lat_g, w_uk, w_uv, w_dq, q_norm_g, w_uq, w_o, router_w, router_bias, exp_w_in, exp_w_out, sh_w_in, sh_w_out, final_g):
    Bp, Sp, _ = x_prompt.shape
    Bs, Ss, _ = x_sample.shape
    past = cache_mla_latent.shape[1]
    P = dict(norm1_g=norm1_g, norm2_g=norm2_g, hg_w_in=hg_w_in, hg_lb_logits=hg_lb_logits,
             hg_onorm_g=hg_onorm_g, hg_w_out=hg_w_out, kv_in_g=kv_in_g, kv_lat_g=kv_lat_g,
             w_dq=w_dq, q_norm_g=q_norm_g, w_o=w_o, router_bias=router_bias,
             exp_w_in=exp_w_in, exp_w_out=exp_w_out, sh_w_in=sh_w_in, sh_w_out=sh_w_out, final_g=final_g)
    W = _prep_weights(w_dkv, w_uk, w_uv, w_uq, router_w)
    mod = ada_mod(jnp.concatenate([c_prompt, c_sample], axis=0), ada_w, ada_b)
    lbs = jnp.cumsum(jax.nn.softmax(hg_lb_logits.astype(F32), axis=0), axis=0)
    gp = _group_state(x_prompt, mod[:, :Bp, None, :], np.arange(Sp), None, None, None, lbs)
    gs = _group_state(x_sample, mod[:, Bp:, None, :], past + np.arange(Ss), state_hgrn,
                      cache_mla_latent, jnp.transpose(cache_mla_krope, (0, 2, 1)), lbs)
    groups = [gp, gs]
    n_tok = sum(st["x"].shape[0] * st["x"].shape[1] for st in groups)
    n_a = hg_w_in.shape[0]
    for l in range(norm1_g.shape[0]):
        packed = dict(total=n_tok, row0=0, buf=None)
        for st in groups:
            _mixer(st, l, P, W, packed)
        _moe(groups, packed["buf"], l, P, W)
        if l == n_a - 1:
            gp["k_all"], gp["v_all"] = kv_expand(gp["lat"], gp["kr"], W["wk_pad"], W["ek"], W["wvt_ext"],
                                                 W["ones_col"])
    return (gp["x"], gs["x"], jnp.stack(gp["hg_new"], axis=0), jnp.stack(gs["hg_new"], axis=0),
            gp["lat"], gp["kr"], gs["lat"], gs["kr"])
```

```python
import dataclasses
import functools

import numpy as np
import jax
import jax.numpy as jnp
from jax import lax
from jax.experimental import pallas as pl
from jax.experimental.pallas import tpu as pltpu
from jax.experimental.pallas import tpu_sc as plsc

F32 = jnp.float32
BF16 = jnp.bfloat16

D_MODEL = 1024
CHUNK = 64
HG_HEADS = 8
HG_DK = 128
HG_DV = 128
MLA_HEADS = 16
MLA_NOPE = 64
MLA_ROPE = 32
MLA_V = 64
Q_LORA = 384
KV_LORA = 256
ROPE_THETA = 10000.0
N_EXPERTS = 64
TOP_K = 8
N_GROUPS = 8
TOPK_GROUPS = 4
EXPERT_FF = 256
SHARED_FF = 256
ROUTED_SCALE = 2.5
EPS = 1e-6

LANES = 128
HEAD_PAD = LANES
SAMPLE_KEY_SUB = 8
ATTN_LOOKAHEAD = 6
V_ROWS = MLA_V + 16
QK_SCALE = (MLA_NOPE + MLA_ROPE) ** -0.5
Q_PRESCALE = QK_SCALE * float(np.log2(np.e))
VMEM_LIMIT = 56 * 1024 * 1024
NEG_INF = float("-inf")
SC_CORES = 2
SC_SUBCORES = 16
SC_WORKERS = SC_CORES * SC_SUBCORES
SC_LANES = 16
SC_WINDOW = 64
MOE_TILE = 512
MOE_NBUF = 4
MOE_SUB = 1


def _cparams(n_axes):
    return pltpu.CompilerParams(dimension_semantics=("arbitrary",) * n_axes,
                                vmem_limit_bytes=VMEM_LIMIT)


def _silu(x):
    return x * jax.nn.sigmoid(x)


def _rms(x, g):
    ms = jnp.mean(x * x, axis=-1, keepdims=True)
    return x * lax.rsqrt(ms + EPS) * g


def _dot(a, b):
    return jnp.dot(a, b, preferred_element_type=F32)


def _dot_nt(a, b):
    return lax.dot_general(a, b, (((1,), (1,)), ((), ())), preferred_element_type=F32)


def _dot_tn(a, b):
    return lax.dot_general(a, b, (((0,), (0,)), ((), ())), preferred_element_type=F32)


def _row_blocks(B, T, rows):
    if T >= rows:
        assert T % rows == 0
        bb, tt = 1, rows
    else:
        assert rows % T == 0 and B % (rows // T) == 0
        bb, tt = rows // T, T
    nt = T // tt
    return bb, tt, (B // bb) * nt, (lambda i: (i // nt, i % nt))


def _ada_kernel(c_ref, w_ref, b_ref, o_ref):
    a = _silu(c_ref[...]).astype(BF16)
    o_ref[...] = _dot(a, w_ref[...].astype(BF16)) + b_ref[...]


def ada_mod(c, ada_w, ada_b):
    R, D = c.shape
    L, _, N = ada_w.shape
    tn = 1536
    return pl.pallas_call(
        _ada_kernel,
        grid=(L, N // tn),
        in_specs=[pl.BlockSpec((R, D), lambda l, j: (0, 0)),
                  pl.BlockSpec((None, D, tn), lambda l, j: (l, 0, j)),
                  pl.BlockSpec((None, 1, tn), lambda l, j: (l, 0, j))],
        out_specs=pl.BlockSpec((None, R, tn), lambda l, j: (l, 0, j)),
        out_shape=jax.ShapeDtypeStruct((L, R, N), F32),
        compiler_params=_cparams(2),
        name="ada_mod",
    )(c, ada_w, ada_b.reshape(L, 1, N))


def _pack_pairs(y):
    half = y.shape[-1] // 2
    bits = lax.bitcast_convert_type(y.astype(BF16).astype(F32), jnp.uint32)
    word = lax.shift_right_logical(bits[:, :half], jnp.uint32(16)) | bits[:, half:]
    return lax.bitcast_convert_type(word, jnp.int32)


def _unpack_pairs(word, dtype=BF16):
    u = lax.bitcast_convert_type(word, jnp.uint32)
    lo = lax.bitcast_convert_type(lax.shift_left(u, jnp.uint32(16)), F32)
    hi = lax.bitcast_convert_type(u & jnp.uint32(0xFFFF0000), F32)
    return lo.astype(dtype), hi.astype(dtype)


def _norm_kernel(x_ref, g_ref, sc_ref, sh_ref, o_ref):
    y = _rms(x_ref[...], g_ref[...]) * (1.0 + sc_ref[...]) + sh_ref[...]
    o_ref[...] = y.astype(o_ref.dtype)


def norm_mod(x, g, mod, sc_idx, sh_idx, rows=512):
    B, T, D = x.shape
    bb, tt, nblk, ij = _row_blocks(B, T, rows)
    xspec = pl.BlockSpec((bb, tt, D), lambda i: ij(i) + (0,))
    return pl.pallas_call(
        _norm_kernel,
        grid=(nblk,),
        in_specs=[xspec, pl.BlockSpec((1, D), lambda i: (0, 0)),
                  pl.BlockSpec((bb, 1, D), lambda i: (ij(i)[0], 0, sc_idx)),
                  pl.BlockSpec((bb, 1, D), lambda i: (ij(i)[0], 0, sh_idx))],
        out_specs=xspec,
        out_shape=jax.ShapeDtypeStruct((B, T, D), BF16),
        compiler_params=_cparams(1),
        name="norm_mod",
    )(x, g.reshape(1, D), mod, mod)


def _linear_kernel(*refs, residual, norm_next, shared_rows, n_main):
    if norm_next and shared_rows:
        a_ref, w_ref, x_ref, gate_ref, ng_ref, nsc_ref, nsh_ref, _, o_ref, hp_ref, wb_ref = refs
    elif norm_next:
        a_ref, w_ref, x_ref, gate_ref, ng_ref, nsc_ref, nsh_ref, o_ref, hp_ref, wb_ref = refs
    elif residual:
        a_ref, w_ref, x_ref, gate_ref, o_ref, wb_ref = refs
    else:
        a_ref, w_ref, o_ref, wb_ref = refs

    @pl.when(pl.program_id(1) == 0)
    def _():
        wb_ref[...] = w_ref[...].astype(BF16)

    def main():
        bb, tt, K = a_ref.shape
        y = _dot(a_ref[...].reshape(bb * tt, K).astype(BF16), wb_ref[...])
        y = y.reshape(bb, tt, y.shape[-1])
        if residual:
            y = x_ref[...] + gate_ref[...] * y
        o_ref[...] = y.astype(o_ref.dtype)
        if norm_next:
            h = _rms(y, ng_ref[...]) * (1.0 + nsc_ref[...]) + nsh_ref[...]
            hp_ref[...] = _pack_pairs(h.reshape(bb * tt, h.shape[-1]))

    if n_main is None:
        main()
    else:
        pl.when(pl.program_id(1) < n_main)(main)

        @pl.when(pl.program_id(1) >= n_main)
        def _():
            hp_ref[...] = jnp.zeros(hp_ref.shape, hp_ref.dtype)


def linear(a, w, l, out_dtype, x=None, mod=None, gate_idx=0, rows=512, tn=1024, next_norm=None,
           rows_total=None, row0=0, rows_buf=None):
    B, T, K = a.shape
    _, _, N = w.shape
    tn = min(tn, N)
    bb, tt, nblk, ij0 = _row_blocks(B, T, rows)
    n_extra = 0
    if next_norm is not None and rows_buf is None and rows_total is not None:
        assert row0 == 0 and (rows_total - B * T) % (bb * tt) == 0
        n_extra = (rows_total - B * T) // (bb * tt)

    def ij(i):
        return ij0(jnp.minimum(i, nblk - 1)) if n_extra else ij0(i)

    in_specs = [pl.BlockSpec((bb, tt, K), lambda j, i: ij(i) + (0,)),
                pl.BlockSpec((None, K, tn), lambda j, i: (l, 0, j))]
    args = [a, w]
    ospec = pl.BlockSpec((bb, tt, tn), lambda j, i: ij(i) + (j,))
    out_specs = ospec
    out_shape = jax.ShapeDtypeStruct((B, T, N), out_dtype)
    aliases = {}
    if x is not None:
        gsteps = D_MODEL // tn
        in_specs += [ospec, pl.BlockSpec((bb, 1, tn), lambda j, i: (ij(i)[0], 0, gate_idx * gsteps + j))]
        args += [x, mod]
    if next_norm is not None:
        assert x is not None and tn == N
        gain, sc_idx, sh_idx = next_norm
        in_specs += [pl.BlockSpec((1, N), lambda j, i: (0, 0)),
                     pl.BlockSpec((bb, 1, N), lambda j, i: (ij(i)[0], 0, sc_idx)),
                     pl.BlockSpec((bb, 1, N), lambda j, i: (ij(i)[0], 0, sh_idx))]
        args += [gain.reshape(1, N), mod, mod]
        assert row0 % (bb * tt) == 0
        off = row0 // (bb * tt)
        out_specs = [ospec, pl.BlockSpec((bb * tt, N // 2), lambda j, i: (off + i, 0))]
        out_shape = [out_shape, jax.ShapeDtypeStruct((rows_total or B * T, N // 2), jnp.int32)]
        if rows_buf is not None:
            in_specs.append(pl.BlockSpec(memory_space=pl.ANY))
            args.append(rows_buf)
            aliases = {len(args) - 1: 1}
    return pl.pallas_call(
        functools.partial(_linear_kernel, residual=x is not None, norm_next=next_norm is not None,
                          shared_rows=rows_buf is not None, n_main=nblk if n_extra else None),
        grid=(N // tn, nblk + n_extra),
        in_specs=in_specs,
        out_specs=out_specs,
        out_shape=out_shape,
        scratch_shapes=[pltpu.VMEM((K, tn), BF16)],
        input_output_aliases=aliases,
        compiler_params=_cparams(2),
        name="linear",
    )(*args)


def _hgrn_proj_kernel(x_ref, g_ref, sc_ref, sh_ref, w_ref, zf_ref, zqig_ref, h_b, w_b):
    i = pl.program_id(0)
    j = pl.program_id(1)
    bb, tt, D = x_ref.shape

    @pl.when(i == 0)
    def _():
        w_b[j] = w_ref[...].astype(BF16)

    @pl.when(j == 0)
    def _():
        h = _rms(x_ref[...], g_ref[...]) * (1.0 + sc_ref[...]) + sh_ref[...]
        h_b[...] = h.reshape(bb * tt, D).astype(BF16)

    y = _dot(h_b[...], w_b[j]).reshape(bb, tt, -1)

    @pl.when(j == 1)
    def _():
        zf_ref[...] = y

    @pl.when(j != 1)
    def _():
        zqig_ref[...] = y.astype(zqig_ref.dtype)


def hgrn_proj(x, g, mod, sc_idx, sh_idx, w_in, l, rows=1024):
    B, T, D = x.shape
    bb, tt, nblk, ij = _row_blocks(B, T, min(rows, B * T))
    xspec = pl.BlockSpec((bb, tt, D), lambda i, j: ij(i) + (0,))
    return pl.pallas_call(
        _hgrn_proj_kernel,
        grid=(nblk, 4),
        in_specs=[xspec,
                  pl.BlockSpec((1, D), lambda i, j: (0, 0)),
                  pl.BlockSpec((bb, 1, D), lambda i, j: (ij(i)[0], 0, sc_idx)),
                  pl.BlockSpec((bb, 1, D), lambda i, j: (ij(i)[0], 0, sh_idx)),
                  pl.BlockSpec((None, D, D), lambda i, j: (l, 0, jnp.where(i == 0, j, 3)))],
        out_specs=[xspec,
                   pl.BlockSpec((bb, tt, D), lambda i, j: ij(i) + (j - (j >= 1),))],
        out_shape=[jax.ShapeDtypeStruct((B, T, D), F32), jax.ShapeDtypeStruct((B, T, 3 * D), BF16)],
        scratch_shapes=[pltpu.VMEM((bb * tt, D), BF16), pltpu.VMEM((4, D, D), BF16)],
        compiler_params=_cparams(2),
        name="hgrn_proj",
    )(x, g.reshape(1, D), mod, mod, w_in)


def _gla_kernel(*refs, L, n_chunks, has_init):
    if has_init:
        q_ref, f_ref, i_ref, g_ref, lb_ref, on_ref, s0_ref, o_ref, so_ref, st_ref = refs
    else:
        q_ref, f_ref, i_ref, g_ref, lb_ref, on_ref, o_ref, so_ref, st_ref = refs
    t = pl.program_id(1)
    H = st_ref.shape[0]

    @pl.when(t == 0)
    def _():
        for h in range(H):
            if has_init:
                st_ref[h] = s0_ref[0, h].T
            else:
                st_ref[h] = jnp.zeros(st_ref.shape[1:], F32)

    lb = lb_ref[...]
    onorm = on_ref[...]
    row = lax.broadcasted_iota(jnp.int32, (L, L), 0)
    col = lax.broadcasted_iota(jnp.int32, (L, L), 1)
    causal = col <= row
    tri = causal.astype(BF16)

    def chunk(c, carry):
        rows = pl.ds(pl.multiple_of(c * L, L), L)

        def write_o(sl, o):
            o_ref[0, rows, sl] = o.astype(o_ref.dtype)

        _gla_chunk(q_ref[0, rows, :], f_ref[0, rows, :], i_ref[0, rows, :], g_ref[0, rows, :],
                   lb, onorm, tri, causal, st_ref, write_o)
        return carry

    lax.fori_loop(0, n_chunks, chunk, 0, unroll=4 if n_chunks % 4 == 0 else 1)

    @pl.when(t == pl.num_programs(1) - 1)
    def _():
        for h in range(H):
            so_ref[0, h] = st_ref[h].T


def _gla_chunk(q, f, v, g, lb, onorm, tri, causal, st_ref, write_o):
    L = q.shape[0]
    H = st_ref.shape[0]
    mid = L // 2 - 1
    q = _silu(q.astype(F32))
    fg = lb + (1.0 - lb) * jax.nn.sigmoid(f)
    k = 1.0 - fg
    v = v.astype(BF16)
    gate = _silu(g.astype(F32))
    logf = jnp.log(fg)
    hi = logf.astype(BF16)
    lo = (logf - hi.astype(F32)).astype(BF16)
    b = _dot(tri, hi) + _dot(tri, lo)
    b_mid = b[mid:mid + 1, :]
    b_last = b[L - 1:L, :]
    qa = q * jnp.exp(b - b_mid)
    kb = k * jnp.exp(b_mid - b)
    qe = (qa * jnp.exp(b_mid)).astype(BF16)
    kd = (kb * jnp.exp(b_last - b_mid)).astype(BF16)
    qa = qa.astype(BF16)
    kb = kb.astype(BF16)
    decay = jnp.exp(b_last)
    sls = [slice(h * HG_DK, (h + 1) * HG_DK) for h in range(H)]
    sts = [st_ref[h] for h in range(H)]
    scores = [_dot_nt(qa[:, sl], kb[:, sl]) for sl in sls]
    inter = [_dot_nt(qe[:, sl], st.astype(BF16)) for sl, st in zip(sls, sts)]
    outer = [_dot_tn(v[:, sl], kd[:, sl]) for sl in sls]
    intra = [_dot(jnp.where(causal, sc, 0.0).astype(BF16), v[:, sl]) for sc, sl in zip(scores, sls)]
    for h, sl in enumerate(sls):
        st_ref[h] = sts[h] * decay[:, sl] + outer[h]
        write_o(sl, _rms(inter[h] + intra[h], onorm[:, sl]) * gate[:, sl])


def gla(zqig, zf, lb, onorm_g, s0):
    B, T, D = zf.shape
    L = CHUNK if T % CHUNK == 0 else T
    tt = min(T, 1024)
    n_chunks = tt // L
    H = HG_HEADS

    def zspec(part):
        return pl.BlockSpec((1, tt, D), lambda b, t: (b, t, part))

    hspec = pl.BlockSpec((1, D), lambda b, t: (0, 0))
    sspec = pl.BlockSpec((1, H, HG_DK, HG_DV), lambda b, t: (b, 0, 0, 0))
    in_specs = [zspec(0), zspec(0), zspec(1), zspec(2), hspec, hspec]
    args = [zqig, zf, zqig, zqig, lb.reshape(1, D), onorm_g.reshape(1, D)]
    if s0 is not None:
        in_specs.append(sspec)
        args.append(s0)
    return pl.pallas_call(
        functools.partial(_gla_kernel, L=L, n_chunks=n_chunks, has_init=s0 is not None),
        grid=(B, T // tt),
        in_specs=in_specs,
        out_specs=[pl.BlockSpec((1, tt, D), lambda b, t: (b, t, 0)), sspec],
        out_shape=[jax.ShapeDtypeStruct((B, T, D), BF16),
                   jax.ShapeDtypeStruct((B, H, HG_DK, HG_DV), F32)],
        scratch_shapes=[pltpu.VMEM((H, HG_DV, HG_DK), F32)],
        compiler_params=_cparams(2),
        name="gla",
    )(*args)


def _route_kernel(h_ref, rw_ref, bias_ref, pos_ref, w_ref, te_ref, nu_ref,
                  e_s, r_s, base_s, start_s, *, tile_rows):
    ph = pl.program_id(0)
    i = pl.program_id(1)
    M = h_ref.shape[0]
    half = h_ref.shape[1]
    G, E = N_GROUPS, N_EXPERTS // N_GROUPS
    e_flat = lax.broadcasted_iota(jnp.int32, (N_EXPERTS, M), 0)

    @pl.when(ph == 1)
    def _():
        @pl.when(i == 0)
        def _():
            cnt = base_s[...]
            padded = jnp.floor((cnt + (tile_rows - 1)) * (1.0 / tile_rows)) * tile_rows
            r = lax.broadcasted_iota(jnp.int32, (N_EXPERTS, N_EXPERTS), 0)
            c = lax.broadcasted_iota(jnp.int32, (N_EXPERTS, N_EXPERTS), 1)
            start = jnp.dot((c < r).astype(F32), padded, preferred_element_type=F32,
                            precision=lax.Precision.HIGHEST)
            start_s[...] = start
            te_ref[...] = (start * (1.0 / tile_rows)).astype(jnp.int32)
            nu_ref[...] = (padded * (1.0 / tile_rows)).astype(jnp.int32)

        start_col = start_s[:, :1]
        for k in range(TOP_K):
            hit = e_flat == e_s[i, k:k + 1, :]
            seg = jnp.sum(jnp.where(hit, start_col, 0.0), axis=0, keepdims=True)
            pos_ref[k:k + 1, :] = (seg + r_s[i, k:k + 1, :]).astype(jnp.int32)

    @pl.when(ph == 0)
    def _():
        _route_pass0(h_ref, rw_ref, bias_ref, w_ref, e_s, r_s, base_s, i, M, half, G, E)


def _route_pass0(h_ref, rw_ref, bias_ref, w_ref, e_s, r_s, base_s, i, M, half, G, E):
    @pl.when(i == 0)
    def _():
        base_s[...] = jnp.zeros_like(base_s)

    lo, hi = _unpack_pairs(h_ref[...])
    rw = rw_ref[...].astype(BF16)
    logits = _dot_nt(rw[:, :half], lo) + _dot_nt(rw[:, half:], hi)
    s = jax.nn.sigmoid(logits)
    sb = (s + bias_ref[...]).reshape(G, E, M)
    s = s.reshape(G, E, M)
    e_in = lax.broadcasted_iota(jnp.int32, (G, E, M), 1).astype(F32)
    g_id = lax.broadcasted_iota(jnp.int32, (G, 1, M), 0)
    e_id = lax.broadcasted_iota(jnp.int32, (G, E, M), 0).astype(F32) * E + e_in

    def all_max(a):
        return jnp.max(jnp.max(a, axis=0, keepdims=True), axis=1, keepdims=True)

    def all_min(a):
        return jnp.min(jnp.min(a, axis=0, keepdims=True), axis=1, keepdims=True)

    def all_sum(a):
        return jnp.sum(jnp.sum(a, axis=0, keepdims=True), axis=1, keepdims=True)

    m1 = jnp.max(sb, axis=1, keepdims=True)
    first = jnp.min(jnp.where(sb == m1, e_in, float(E)), axis=1, keepdims=True)
    m2 = jnp.max(jnp.where(e_in == first, NEG_INF, sb), axis=1, keepdims=True)
    gs = m1 + m2

    rank = jnp.zeros((G, 1, M), jnp.int32)
    for j in range(G):
        gj = gs[j:j + 1]
        beats = (gj > gs) | ((gj == gs) & (j < g_id))
        rank = rank + beats.astype(jnp.int32)
    gsel = rank < TOPK_GROUPS

    vals = jnp.where(gsel, sb, NEG_INF)
    selm = jnp.zeros((G, E, M), F32)
    chosen, score = [], []
    for _ in range(TOP_K):
        m = all_max(vals)
        first = all_min(jnp.where(vals == m, e_id, float(N_EXPERTS)))
        hit = e_id == first
        score.append(all_sum(jnp.where(hit, s, 0.0)))
        selm = jnp.where(hit, 1.0, selm)
        vals = jnp.where(hit, NEG_INF, vals)
        chosen.append(first)

    tot = score[0]
    for sc in score[1:]:
        tot = tot + sc
    norm = ROUTED_SCALE / tot

    selm = selm.reshape(N_EXPERTS, M)
    earlier = (lax.broadcasted_iota(jnp.int32, (M, M), 0)
               < lax.broadcasted_iota(jnp.int32, (M, M), 1)).astype(BF16)
    rank = (base_s[:, :1] + _dot(selm.astype(BF16), earlier)).reshape(G, E, M)
    base_s[...] = base_s[...] + jnp.sum(selm, axis=1, keepdims=True)
    for k in range(TOP_K):
        hit = e_id == chosen[k]
        e_s[i, k:k + 1, :] = chosen[k].reshape(1, M).astype(jnp.int32)
        r_s[i, k:k + 1, :] = all_sum(jnp.where(hit, rank, 0.0)).reshape(1, M)
        w_ref[k:k + 1, :] = (score[k] * norm).reshape(1, M)


def route(hp, router_w_t, router_bias, l, tile_rows, rows=768):
    N, half = hp.shape
    M = rows
    nT = N // M
    assert N % M == 0

    def p0(ph, i):
        return i * (1 - ph) + (nT - 1) * ph

    return pl.pallas_call(
        functools.partial(_route_kernel, tile_rows=tile_rows),
        grid=(2, nT),
        in_specs=[pl.BlockSpec((M, half), lambda ph, i: (p0(ph, i), 0)),
                  pl.BlockSpec((None, N_EXPERTS, 2 * half), lambda ph, i: (l, 0, 0)),
                  pl.BlockSpec((None, N_EXPERTS, 1), lambda ph, i: (l, 0, 0))],
        out_specs=[pl.BlockSpec((TOP_K, M), lambda ph, i: (0, i * ph)),
                   pl.BlockSpec((TOP_K, M), lambda ph, i: (0, p0(ph, i))),
                   pl.BlockSpec((N_EXPERTS, LANES), lambda ph, i: (0, 0)),
                   pl.BlockSpec((N_EXPERTS, LANES), lambda ph, i: (0, 0))],
        out_shape=[jax.ShapeDtypeStruct((TOP_K, N), jnp.int32),
                   jax.ShapeDtypeStruct((TOP_K, N), F32),
                   jax.ShapeDtypeStruct((N_EXPERTS, LANES), jnp.int32),
                   jax.ShapeDtypeStruct((N_EXPERTS, LANES), jnp.int32)],
        scratch_shapes=[pltpu.VMEM((nT, TOP_K, M), jnp.int32), pltpu.VMEM((nT, TOP_K, M), F32),
                        pltpu.VMEM((N_EXPERTS, LANES), F32), pltpu.VMEM((N_EXPERTS, LANES), F32)],
        compiler_params=_cparams(2),
        name="route",
    )(hp, router_w_t, router_bias.reshape(-1, N_EXPERTS, 1))


def _sc_mesh():
    return plsc.VectorSubcoreMesh(core_axis_name="core", subcore_axis_name="subcore")


def sc_invert(pos_flat, n_tok, n_out):
    n = pos_flat.shape[0]
    per = n_out // SC_WORKERS
    chunk = n_tok
    assert n_out % SC_WORKERS == 0 and per % SC_LANES == 0
    assert n_tok % chunk == 0 and n % chunk == 0 and chunk % SC_LANES == 0
    cp = pltpu.CompilerParams()
    if "needs_layout_passes" in pltpu.CompilerParams.__dataclass_fields__:
        cp = dataclasses.replace(cp, needs_layout_passes=False)

    @functools.partial(
        pl.kernel, out_type=jax.ShapeDtypeStruct((n_out,), jnp.int32), mesh=_sc_mesh(),
        scratch_types=[pltpu.VMEM((chunk,), jnp.int32), pltpu.VMEM((per,), jnp.int32)],
        compiler_params=cp, name="sc_invert")
    def k(pos_hbm, src_hbm, pos_v, src_v):
        wid = lax.axis_index("subcore") * SC_CORES + lax.axis_index("core")
        lo = wid * per
        lane = lax.iota(jnp.int32, SC_LANES)

        @pl.loop(0, per, step=SC_LANES)
        def _(r):
            src_v[pl.ds(r, SC_LANES)] = lax.rem(lo + r + lane, n_tok)

        @pl.loop(0, n // chunk)
        def _(c):
            base = c * chunk
            pltpu.sync_copy(pos_hbm.at[pl.ds(base, chunk)], pos_v)
            tok0 = lax.rem(base, n_tok)

            @plsc.parallel_loop(0, chunk, step=SC_LANES, unroll=8)
            def _(r):
                p = pos_v[pl.ds(r, SC_LANES)] - lo
                mine = (p >= 0) & (p < per)
                plsc.store_scatter(src_v, [jnp.where(mine, p, 0)], tok0 + r + lane, mask=mine)

        pltpu.sync_copy(src_v, src_hbm.at[pl.ds(lo, per)])

    return k(pos_flat)


def sc_gather(x, idx):
    n = idx.shape[0]
    dim = x.shape[1]
    assert n % (SC_WINDOW * SC_WORKERS) == 0

    @functools.partial(
        pl.kernel, out_type=jax.ShapeDtypeStruct((n, dim), x.dtype), mesh=_sc_mesh(),
        scratch_types=[], name="sc_gather")
    def k(x_hbm, i_hbm, o_hbm):
        def body(i_vmem, o_vmem):
            pltpu.sync_copy(x_hbm.at[i_vmem.at[0]], o_vmem)

        pltpu.emit_pipeline(
            body, grid=(n // SC_WINDOW,),
            in_specs=[pl.BlockSpec((1, SC_WINDOW), index_map=lambda i: (i, 0))],
            out_specs=[pl.BlockSpec((SC_WINDOW, dim), index_map=lambda i: (i, 0))],
            core_axis_name=("core", "subcore"),
            dimension_semantics=(pltpu.PARALLEL,),
        )(i_hbm, o_hbm)

    return k(x, idx.reshape(n // SC_WINDOW, SC_WINDOW))


def _moe_gemm_kernel(ts_ref, tn_ref, x_hbm, wi_ref, wo_ref, o_hbm, wi_b, wo_b, xbuf, obuf, in_sem, out_sem,
                     *, tile_rows, n_tiles):
    e = pl.program_id(0)
    last = pl.num_programs(0) - 1
    t0 = ts_ref[e]
    n = tn_ref[e]
    n_used = ts_ref[last] + tn_ref[last]

    def x_copy(g, slot):
        rows = pl.ds(pl.multiple_of(g * tile_rows, tile_rows), tile_rows)
        return pltpu.make_async_copy(x_hbm.at[rows], xbuf.at[slot], in_sem.at[slot])

    def o_copy(g, slot):
        rows = pl.ds(pl.multiple_of(g * tile_rows, tile_rows), tile_rows)
        return pltpu.make_async_copy(obuf.at[slot], o_hbm.at[rows], out_sem.at[slot])

    @pl.when(e == 0)
    def _():
        for g0 in range(MOE_NBUF - 1):
            @pl.when(g0 < n_used)
            def _():
                x_copy(g0, g0).start()

    @pl.when(n > 0)
    def _():
        wi_b[...] = wi_ref[...].astype(BF16)
        wo_b[...] = wo_ref[...].astype(BF16)

    def tile(i, carry):
        g = t0 + i
        slot = lax.rem(g, MOE_NBUF)
        x_copy(g, slot).wait()
        ahead = g + (MOE_NBUF - 1)

        @pl.when(ahead < n_used)
        def _():
            x_copy(ahead, lax.rem(ahead, MOE_NBUF)).start()

        @pl.when(g >= MOE_NBUF)
        def _():
            o_copy(g - MOE_NBUF, slot).wait()

        rows = tile_rows // MOE_SUB
        half = xbuf.shape[2]
        xs = [_unpack_pairs(xbuf[slot, r * rows:(r + 1) * rows, :]) for r in range(MOE_SUB)]
        hus = [_dot(lo, wi_b[:half, :]) + _dot(hi, wi_b[half:, :]) for lo, hi in xs]
        acts = [(_silu(hu[:, :EXPERT_FF]) * hu[:, EXPERT_FF:]).astype(BF16) for hu in hus]
        outs = [_dot(act, wo_b[...]) for act in acts]
        for r, out in enumerate(outs):
            obuf[slot, r * rows:(r + 1) * rows, :] = _pack_pairs(out)
        o_copy(g, slot).start()
        return carry

    lax.fori_loop(0, n, tile, 0)

    @pl.when(e == last)
    def _():
        for back in range(MOE_NBUF, 0, -1):
            @pl.when(n_used >= back)
            def _():
                o_copy(n_used - back, lax.rem(n_used - back, MOE_NBUF)).wait()

        obuf[...] = jnp.zeros(obuf.shape, obuf.dtype)
        n_clear = n_tiles - n_used

        def clear(i, carry):
            slot = lax.rem(i, MOE_NBUF)

            @pl.when(i >= MOE_NBUF)
            def _():
                o_copy(n_used + i - MOE_NBUF, slot).wait()

            o_copy(n_used + i, slot).start()
            return carry

        lax.fori_loop(0, n_clear, clear, 0)
        for back in range(MOE_NBUF, 0, -1):
            @pl.when(n_clear >= back)
            def _():
                o_copy(n_tiles - back, lax.rem(n_clear - back, MOE_NBUF)).wait()


def moe_gemm(xs, tile_start, tile_count, exp_w_in, exp_w_out, l, tile_rows, n_tiles):
    P, half = xs.shape
    D = 2 * half
    assert P == n_tiles * tile_rows
    hbm = pl.BlockSpec(memory_space=pl.ANY)
    grid_spec = pltpu.PrefetchScalarGridSpec(
        num_scalar_prefetch=2,
        grid=(N_EXPERTS,),
        in_specs=[hbm,
                  pl.BlockSpec((None, None, D, 2 * EXPERT_FF), lambda e, ts, tn: (l, e, 0, 0)),
                  pl.BlockSpec((None, None, EXPERT_FF, D), lambda e, ts, tn: (l, e, 0, 0))],
        out_specs=hbm,
        scratch_shapes=[pltpu.VMEM((D, 2 * EXPERT_FF), BF16), pltpu.VMEM((EXPERT_FF, D), BF16),
                        pltpu.VMEM((MOE_NBUF, tile_rows, half), jnp.int32),
                        pltpu.VMEM((MOE_NBUF, tile_rows, half), jnp.int32),
                        pltpu.SemaphoreType.DMA((MOE_NBUF,)), pltpu.SemaphoreType.DMA((MOE_NBUF,))],
    )
    return pl.pallas_call(
        functools.partial(_moe_gemm_kernel, tile_rows=tile_rows, n_tiles=n_tiles),
        grid_spec=grid_spec,
        out_shape=jax.ShapeDtypeStruct((P, half), jnp.int32),
        compiler_params=_cparams(1),
        name="moe_gemm",
    )(tile_start, tile_count, xs, exp_w_in, exp_w_out)


def _moe_combine_kernel(*refs, final, norm_next, shared_kv, queries):
    it = iter(refs)
    y_ref, w_ref, h_ref, si_ref, so_ref, x_ref, g2_ref = (next(it) for _ in range(7))
    fg_ref = next(it) if final else None
    ng_ref, nsc_ref, nsh_ref = (next(it) for _ in range(3)) if norm_next else (None,) * 3
    kg_ref, wkv_ref, lg_ref, cos_ref, sin_ref = (next(it) for _ in range(5)) if shared_kv else (None,) * 5
    wdq_ref, qg_ref, wqt_ref, wqrt_ref, cost_ref, sint_ref = (next(it) for _ in range(6)) if queries else (None,) * 6
    o_ref = next(it)
    hn_ref = next(it) if norm_next and not queries else None
    lat_ref, kr_ref = (next(it), next(it)) if shared_kv else (None, None)
    qt_ref = next(it) if queries else None
    si_b, so_b = next(it), next(it)
    wkv_b = next(it) if shared_kv else None
    wdq_b, wqt_b, wqrt_b = (next(it), next(it), next(it)) if queries else (None,) * 3

    @pl.when(pl.program_id(0) == 0)
    def _():
        si_b[...] = si_ref[...].astype(BF16)
        so_b[...] = so_ref[...].astype(BF16)
        if shared_kv:
            wkv_b[...] = wkv_ref[...].astype(BF16)
        if queries:
            wdq_b[...] = wdq_ref[...].astype(BF16)
            wqt_b[...] = wqt_ref[...].astype(BF16)
            wqrt_b[...] = wqrt_ref[...].astype(BF16)

    bb, tt, D = x_ref.shape
    half = D // 2
    w = w_ref[...].T
    acc_lo = jnp.zeros((bb * tt, half), F32)
    acc_hi = jnp.zeros((bb * tt, half), F32)
    for k in range(TOP_K):
        lo, hi = _unpack_pairs(y_ref[k], F32)
        acc_lo = acc_lo + w[:, k:k + 1] * lo
        acc_hi = acc_hi + w[:, k:k + 1] * hi
    hlo, hhi = _unpack_pairs(h_ref[...])
    hu = _dot(hlo, si_b[:half, :]) + _dot(hhi, si_b[half:, :])
    act = (_silu(hu[:, :SHARED_FF]) * hu[:, SHARED_FF:]).astype(BF16)
    y = jnp.concatenate([acc_lo, acc_hi], axis=-1) + _dot(act, so_b[...])
    x_new = x_ref[...] + g2_ref[...] * y.reshape(bb, tt, D)
    o_ref[...] = _rms(x_new, fg_ref[...]) if final else x_new
    if norm_next:
        hn = (_rms(x_new, ng_ref[...]) * (1.0 + nsc_ref[...]) + nsh_ref[...]).astype(BF16)
        if queries:
            _queries_t(hn.reshape(bb * tt, D), wdq_b, qg_ref[...], wqt_b, wqrt_b, cost_ref[...], sint_ref[...],
                       qt_ref)
        else:
            hn_ref[...] = hn
    if shared_kv:
        xn = _rms(x_new, kg_ref[...]).reshape(bb * tt, D).astype(BF16)
        z = _dot(xn, wkv_b[...])
        lat_ref[...] = _rms(z[:, :KV_LORA], lg_ref[...]).reshape(bb, tt, KV_LORA)
        zr = z[:, KV_LORA:KV_LORA + MLA_ROPE].reshape(bb, tt, MLA_ROPE)
        zq = z[:, KV_LORA + LANES:KV_LORA + LANES + MLA_ROPE].reshape(bb, tt, MLA_ROPE)
        kr_ref[...] = zr * cos_ref[...] + zq * sin_ref[...]


def moe_combine(y8, w8, hp, sh_w_in, sh_w_out, x, mod, gate_idx, l, row0, final_g=None, next_norm=None,
                shared_kv=None, queries=None, rows=512):
    B, T, D = x.shape
    half = D // 2
    bb, tt, nblk, ij = _row_blocks(B, T, rows)
    M = bb * tt
    assert row0 % M == 0
    off = row0 // M
    xspec = pl.BlockSpec((bb, tt, D), lambda i: ij(i) + (0,))
    in_specs = [pl.BlockSpec((TOP_K, M, half), lambda i: (0, off + i, 0)),
                pl.BlockSpec((TOP_K, M), lambda i: (0, off + i)),
                pl.BlockSpec((M, half), lambda i: (off + i, 0)),
                pl.BlockSpec((None, D, 2 * SHARED_FF), lambda i: (l, 0, 0)),
                pl.BlockSpec((None, SHARED_FF, D), lambda i: (l, 0, 0)),
                xspec,
                pl.BlockSpec((bb, 1, D), lambda i: (ij(i)[0], 0, gate_idx))]
    args = [y8, w8, hp, sh_w_in, sh_w_out, x, mod]
    out_specs = xspec
    out_shape = jax.ShapeDtypeStruct((B, T, D), F32)
    if final_g is not None:
        assert next_norm is None
        in_specs.append(pl.BlockSpec((1, D), lambda i: (0, 0)))
        args.append(final_g.reshape(1, D))
    if next_norm is not None:
        gain, mod_next, sc_idx, sh_idx = next_norm
        in_specs += [pl.BlockSpec((1, D), lambda i: (0, 0)),
                     pl.BlockSpec((bb, 1, D), lambda i: (ij(i)[0], 0, sc_idx)),
                     pl.BlockSpec((bb, 1, D), lambda i: (ij(i)[0], 0, sh_idx))]
        args += [gain.reshape(1, D), mod_next, mod_next]
        if queries is None:
            out_specs = [xspec, xspec]
            out_shape = [out_shape, jax.ShapeDtypeStruct((B, T, D), BF16)]
    scratch = [pltpu.VMEM((D, 2 * SHARED_FF), BF16), pltpu.VMEM((SHARED_FF, D), BF16)]
    if shared_kv is not None:
        kv_in_g, w_kv, kv_lat_g, cos32, sin32 = shared_kv
        tspec = pl.BlockSpec((tt, MLA_ROPE), lambda i: (ij(i)[1], 0))
        in_specs += [pl.BlockSpec((1, D), lambda i: (0, 0)),
                     pl.BlockSpec(w_kv.shape, lambda i: (0, 0)),
                     pl.BlockSpec((1, KV_LORA), lambda i: (0, 0)),
                     tspec, tspec]
        args += [kv_in_g.reshape(1, D), w_kv, kv_lat_g.reshape(1, KV_LORA), cos32, sin32]
        out_specs = list(out_specs) if isinstance(out_specs, list) else [out_specs]
        out_shape = list(out_shape) if isinstance(out_shape, list) else [out_shape]
        out_specs += [pl.BlockSpec((bb, tt, KV_LORA), lambda i: ij(i) + (0,)),
                      pl.BlockSpec((bb, tt, MLA_ROPE), lambda i: ij(i) + (0,))]
        out_shape += [jax.ShapeDtypeStruct((B, T, KV_LORA), F32), jax.ShapeDtypeStruct((B, T, MLA_ROPE), F32)]
        scratch.append(pltpu.VMEM(w_kv.shape, BF16))
    if queries is not None:
        assert next_norm is not None and bb == 1
        w_dq, q_norm_g, wq_t, wqr_t, bi, cos_t, sin_t = queries
        NQ, NR = wq_t.shape[1], wqr_t.shape[1]
        tspec_t = pl.BlockSpec((MLA_ROPE, tt), lambda i: (0, ij(i)[1]))
        in_specs += [pl.BlockSpec((None, D, Q_LORA), lambda i: (bi, 0, 0)),
                     pl.BlockSpec((None, 1, Q_LORA), lambda i: (bi, 0, 0)),
                     pl.BlockSpec((None, NQ, Q_LORA), lambda i: (bi, 0, 0)),
                     pl.BlockSpec((None, NR, Q_LORA), lambda i: (bi, 0, 0)),
                     tspec_t, tspec_t]
        args += [w_dq, q_norm_g.reshape(-1, 1, Q_LORA), wq_t, wqr_t, cos_t, sin_t]
        out_specs = list(out_specs) if isinstance(out_specs, list) else [out_specs]
        out_shape = list(out_shape) if isinstance(out_shape, list) else [out_shape]
        out_specs.append(pl.BlockSpec((1, NQ, tt), lambda i: (ij(i)[0], 0, ij(i)[1])))
        out_shape.append(jax.ShapeDtypeStruct((B, NQ, T), BF16))
        scratch += [pltpu.VMEM((D, Q_LORA), BF16), pltpu.VMEM((NQ, Q_LORA), BF16), pltpu.VMEM((NR, Q_LORA), BF16)]
    return pl.pallas_call(
        functools.partial(_moe_combine_kernel, final=final_g is not None, norm_next=next_norm is not None,
                          shared_kv=shared_kv is not None, queries=queries is not None),
        grid=(nblk,),
        in_specs=in_specs,
        out_specs=out_specs,
        out_shape=out_shape,
        scratch_shapes=scratch,
        compiler_params=_cparams(1),
        name="moe_combine",
    )(*args)


def _kv_expand_kernel(lat_ref, kr_ref, wk_ref, ek_ref, wvt_ref, ones_ref, k_ref, vt_ref):
    lat = lat_ref[0].astype(BF16)
    kr = kr_ref[0].astype(BF16)
    k = _dot(lat, wk_ref[...].astype(BF16)) + _dot(kr, ek_ref[...].astype(BF16))
    k_ref[0] = k.astype(k_ref.dtype)
    vt = _dot_nt(wvt_ref[...].astype(BF16), lat) + ones_ref[...]
    vt_ref[0] = vt.astype(vt_ref.dtype)


def kv_expand(lat, kr, wk_pad, ek, wvt_ext, ones_col, rows=1024):
    B, T, _ = lat.shape
    tt = rows
    NK, NVT = wk_pad.shape[1], wvt_ext.shape[0]

    def full(a):
        return pl.BlockSpec(a.shape, lambda b, t: (0, 0))

    def rowspec(n):
        return pl.BlockSpec((1, tt, n), lambda b, t: (b, t, 0))

    return pl.pallas_call(
        _kv_expand_kernel,
        grid=(B, T // tt),
        in_specs=[rowspec(KV_LORA), rowspec(MLA_ROPE), full(wk_pad), full(ek), full(wvt_ext), full(ones_col)],
        out_specs=[rowspec(NK), pl.BlockSpec((1, NVT, tt), lambda b, t: (b, 0, t))],
        out_shape=[jax.ShapeDtypeStruct((B, T, NK), BF16), jax.ShapeDtypeStruct((B, NVT, T), BF16)],
        compiler_params=_cparams(2),
        name="kv_expand",
    )(lat, kr, wk_pad, ek, wvt_ext, ones_col)


def _query_kernel(h_ref, wdq_ref, qg_ref, wq_ref, wqr_ref, c_ref, s_ref, q_ref, wdq_b, wq_b, wqr_b):
    @pl.when(pl.program_id(0) == 0)
    def _():
        wdq_b[...] = wdq_ref[...].astype(BF16)
        wq_b[...] = wq_ref[...].astype(BF16)
        wqr_b[...] = wqr_ref[...].astype(BF16)

    bb, tt, D = h_ref.shape
    h = h_ref[...].reshape(bb * tt, D)
    cq = _rms(_dot(h, wdq_b[...]), qg_ref[...]).astype(BF16)
    q1 = _dot(cq, wq_b[...]).reshape(bb, tt, -1)
    q2 = _dot(cq, wqr_b[...]).reshape(bb, tt, -1)
    c = c_ref[...]
    s = s_ref[...]
    for hd in range(MLA_HEADS):
        sl = slice(hd * HEAD_PAD, (hd + 1) * HEAD_PAD)
        q_ref[:, :, sl] = (q1[:, :, sl] * c + q2[:, :, sl] * s).astype(q_ref.dtype)


def mla_queries(h, w_dq, q_norm_g, wq_pad, wq_rot, l, c128, s128, rows=512):
    B, T, D = h.shape
    bb, tt, nblk, ij = _row_blocks(B, T, rows)
    NQ = wq_pad.shape[-1]
    tspec = pl.BlockSpec((tt, HEAD_PAD), lambda i: (ij(i)[1], 0))
    return pl.pallas_call(
        _query_kernel,
        grid=(nblk,),
        in_specs=[pl.BlockSpec((bb, tt, D), lambda i: ij(i) + (0,)),
                  pl.BlockSpec((None, D, Q_LORA), lambda i: (l, 0, 0)),
                  pl.BlockSpec((None, 1, Q_LORA), lambda i: (l, 0, 0)),
                  pl.BlockSpec((None, Q_LORA, NQ), lambda i: (l, 0, 0)),
                  pl.BlockSpec((None, Q_LORA, NQ), lambda i: (l, 0, 0)),
                  tspec, tspec],
        out_specs=pl.BlockSpec((bb, tt, NQ), lambda i: ij(i) + (0,)),
        out_shape=jax.ShapeDtypeStruct((B, T, NQ), BF16),
        scratch_shapes=[pltpu.VMEM((D, Q_LORA), BF16), pltpu.VMEM((Q_LORA, NQ), BF16),
                        pltpu.VMEM((Q_LORA, NQ), BF16)],
        compiler_params=_cparams(1),
        name="mla_queries",
    )(h, w_dq, q_norm_g.reshape(-1, 1, Q_LORA), wq_pad, wq_rot, c128, s128)


def _queries_t(h, wdq_b, qg, wqt_b, wqrt_b, cos, sin, qt_ref):
    cq = _rms(_dot(h, wdq_b[...]), qg).astype(BF16)
    q1 = _dot_nt(wqt_b[...], cq)
    q2 = _dot_nt(wqrt_b[...], cq)
    pad = jnp.zeros((HEAD_PAD - MLA_NOPE - MLA_ROPE, q1.shape[1]), qt_ref.dtype)
    for hd in range(MLA_HEADS):
        r0 = hd * HEAD_PAD
        rope = (q1[r0 + MLA_NOPE:r0 + MLA_NOPE + MLA_ROPE] * cos
                + q2[hd * MLA_ROPE:(hd + 1) * MLA_ROPE] * sin)
        qt_ref[0, r0:r0 + MLA_NOPE, :] = (q1[r0:r0 + MLA_NOPE] * Q_PRESCALE).astype(qt_ref.dtype)
        qt_ref[0, r0 + MLA_NOPE:r0 + MLA_NOPE + MLA_ROPE, :] = rope.astype(qt_ref.dtype)
        qt_ref[0, r0 + MLA_NOPE + MLA_ROPE:r0 + HEAD_PAD, :] = pad


def _query_t_kernel(h_ref, wdq_ref, qg_ref, wqt_ref, wqrt_ref, cos_ref, sin_ref, qt_ref, wdq_b, wqt_b, wqrt_b):
    @pl.when((pl.program_id(0) == 0) & (pl.program_id(1) == 0))
    def _():
        wdq_b[...] = wdq_ref[...].astype(BF16)
        wqt_b[...] = wqt_ref[...].astype(BF16)
        wqrt_b[...] = wqrt_ref[...].astype(BF16)

    _queries_t(h_ref[0], wdq_b, qg_ref[...], wqt_b, wqrt_b, cos_ref[...], sin_ref[...], qt_ref)


def mla_queries_t(h, w_dq, q_norm_g, wq_t, wqr_t, l, cos_t, sin_t, rows=512):
    B, T, D = h.shape
    tt = rows
    NQ = wq_t.shape[1]
    NR = wqr_t.shape[1]
    tspec = pl.BlockSpec((MLA_ROPE, tt), lambda b, t: (0, t))
    return pl.pallas_call(
        _query_t_kernel,
        grid=(B, T // tt),
        in_specs=[pl.BlockSpec((1, tt, D), lambda b, t: (b, t, 0)),
                  pl.BlockSpec((None, D, Q_LORA), lambda b, t: (l, 0, 0)),
                  pl.BlockSpec((None, 1, Q_LORA), lambda b, t: (l, 0, 0)),
                  pl.BlockSpec((None, NQ, Q_LORA), lambda b, t: (l, 0, 0)),
                  pl.BlockSpec((None, NR, Q_LORA), lambda b, t: (l, 0, 0)),
                  tspec, tspec],
        out_specs=pl.BlockSpec((1, NQ, tt), lambda b, t: (b, 0, t)),
        out_shape=jax.ShapeDtypeStruct((B, NQ, T), BF16),
        scratch_shapes=[pltpu.VMEM((D, Q_LORA), BF16), pltpu.VMEM((NQ, Q_LORA), BF16),
                        pltpu.VMEM((NR, Q_LORA), BF16)],
        compiler_params=_cparams(2),
        name="mla_queries_t",
    )(h, w_dq, q_norm_g.reshape(-1, 1, Q_LORA), wq_t, wqr_t, cos_t, sin_t)


def _attn_prompt_kernel(qi_tab, ki_tab, qt_ref, k_ref, vt_ref, o_ref, *scratch, tq, tk):
    H = MLA_HEADS
    m_refs, l_refs, acc_refs = scratch[:H], scratch[H:2 * H], scratch[2 * H:]
    p_id = pl.program_id(1)
    qi = qi_tab[p_id]
    ki = ki_tab[p_id]

    @pl.when(ki == 0)
    def _():
        for hd in range(H):
            m_refs[hd][...] = jnp.full(m_refs[hd].shape, NEG_INF, F32)
            l_refs[hd][...] = jnp.zeros(l_refs[hd].shape, F32)
            acc_refs[hd][...] = jnp.zeros(acc_refs[hd].shape, F32)

    def block(masked):
        if masked:
            kchunk = (ki * tk + lax.broadcasted_iota(jnp.int32, (tk, tq), 0)) // CHUNK
            qchunk = (qi * tq + lax.broadcasted_iota(jnp.int32, (tk, tq), 1)) // CHUNK
            mask = kchunk <= qchunk
        def scores(hd):
            sl = slice(hd * HEAD_PAD, (hd + 1) * HEAD_PAD)
            return _dot(k_ref[0, :, sl], qt_ref[0, sl, :])

        pending = [scores(hd) for hd in range(ATTN_LOOKAHEAD)]
        for hd in range(H):
            if hd + ATTN_LOOKAHEAD < H:
                pending.append(scores(hd + ATTN_LOOKAHEAD))
            s = pending.pop(0)
            if masked:
                s = jnp.where(mask, s, NEG_INF)
            m_prev = m_refs[hd][...]
            m_new = jnp.maximum(m_prev, jnp.max(s, axis=0, keepdims=True))
            a = jnp.exp2(m_prev - m_new)
            p = jnp.exp2(s - m_new).astype(BF16)
            pv = _dot(vt_ref[0, hd * V_ROWS:(hd + 1) * V_ROWS, :], p)
            acc_refs[hd][...] = a * acc_refs[hd][...] + pv[:MLA_V]
            l_refs[hd][...] = a * l_refs[hd][...] + pv[MLA_V:MLA_V + 1]
            m_refs[hd][...] = m_new

    @pl.when(ki < qi)
    def _():
        block(False)

    @pl.when(ki == qi)
    def _():
        block(True)
        o_t = jnp.concatenate([acc_refs[hd][...] / l_refs[hd][...] for hd in range(H)], axis=0)
        o_ref[0] = o_t.T.astype(o_ref.dtype)


def attn_prompt(qt, k, vt, tq=256):
    B, NQ, T = qt.shape
    NVT = vt.shape[1]
    NV = MLA_HEADS * MLA_V
    tk = tq
    assert tq % CHUNK == 0
    nq = T // tq
    pairs = [(a, b) for a in range(nq) for b in range(a + 1)]
    qi_tab = jnp.asarray([a for a, _ in pairs], jnp.int32)
    ki_tab = jnp.asarray([b for _, b in pairs], jnp.int32)
    grid_spec = pltpu.PrefetchScalarGridSpec(
        num_scalar_prefetch=2,
        grid=(B, len(pairs)),
        in_specs=[pl.BlockSpec((1, NQ, tq), lambda b, p, qt, kt: (b, 0, qt[p])),
                  pl.BlockSpec((1, tk, NQ), lambda b, p, qt, kt: (b, kt[p], 0)),
                  pl.BlockSpec((1, NVT, tk), lambda b, p, qt, kt: (b, 0, kt[p]))],
        out_specs=pl.BlockSpec((1, tq, NV), lambda b, p, qt, kt: (b, qt[p], 0)),
        scratch_shapes=([pltpu.VMEM((1, tq), F32)] * (2 * MLA_HEADS)
                        + [pltpu.VMEM((MLA_V, tq), F32)] * MLA_HEADS),
    )
    return pl.pallas_call(
        functools.partial(_attn_prompt_kernel, tq=tq, tk=tk),
        grid_spec=grid_spec,
        out_shape=jax.ShapeDtypeStruct((B, T, NV), BF16),
        compiler_params=_cparams(2),
        name="attn_prompt",
    )(qi_tab, ki_tab, qt, k, vt)


def _absorb_kernel(q_ref, m_ref, o_ref):
    o_ref[...] = _dot(q_ref[...], m_ref[...].astype(BF16)).astype(o_ref.dtype)


def absorb_queries(q2d, m_abs):
    N = q2d.shape[0]
    H, _, W = m_abs.shape
    return pl.pallas_call(
        _absorb_kernel,
        grid=(H,),
        in_specs=[pl.BlockSpec((N, HEAD_PAD), lambda h: (0, h)),
                  pl.BlockSpec((None, HEAD_PAD, W), lambda h: (h, 0, 0))],
        out_specs=pl.BlockSpec((None, N, W), lambda h: (h, 0, 0)),
        out_shape=jax.ShapeDtypeStruct((H, N, W), BF16),
        compiler_params=_cparams(1),
        name="absorb_queries",
    )(q2d, m_abs)


def _attn_sample_kernel(q_ref, lat_ref, kr_ref, nlat_ref, nkr_ref, o_ref, m_ref, l_ref, acc_ref):
    kb = pl.program_id(1)
    H, Q, W = q_ref.shape
    q = q_ref[...].reshape(H * Q, W)
    q_lat = q[:, :KV_LORA]
    q_rope = q[:, KV_LORA:KV_LORA + MLA_ROPE]

    def update(lat_tile, kr_tile, n_sub, kr_transposed):
        sub = lat_tile.shape[0] // n_sub
        lats = [lat_tile[j * sub:(j + 1) * sub, :].astype(BF16) for j in range(n_sub)]
        if kr_transposed:
            krs = [kr_tile[:, j * sub:(j + 1) * sub].astype(BF16) for j in range(n_sub)]
            ss = [_dot_nt(q_lat, lat) + _dot(q_rope, kr) for lat, kr in zip(lats, krs)]
        else:
            krs = [kr_tile[j * sub:(j + 1) * sub, :].astype(BF16) for j in range(n_sub)]
            ss = [_dot_nt(q_lat, lat) + _dot_nt(q_rope, kr) for lat, kr in zip(lats, krs)]
        m_prev = m_ref[...]
        m_new = m_prev
        for s in ss:
            m_new = jnp.maximum(m_new, jnp.max(s, axis=-1, keepdims=True))
        a = jnp.exp2(m_prev - m_new)
        ps = [jnp.exp2(s - m_new[:, :1]) for s in ss]
        pv = _dot(ps[0].astype(BF16), lats[0])
        psum = jnp.sum(ps[0], axis=-1, keepdims=True)
        for p, lat in zip(ps[1:], lats[1:]):
            pv = pv + _dot(p.astype(BF16), lat)
            psum = psum + jnp.sum(p, axis=-1, keepdims=True)
        l_ref[...] = a * l_ref[...] + psum
        m_ref[...] = m_new
        acc_ref[...] = jnp.concatenate([a, a], axis=-1) * acc_ref[...] + pv

    @pl.when(kb == 0)
    def _():
        m_ref[...] = jnp.full_like(m_ref, NEG_INF)
        l_ref[...] = jnp.zeros_like(l_ref)
        acc_ref[...] = jnp.zeros_like(acc_ref)
        update(nlat_ref[0], nkr_ref[0], 1, False)

    update(lat_ref[0], kr_ref[0], SAMPLE_KEY_SUB, True)

    @pl.when(kb == pl.num_programs(1) - 1)
    def _():
        lsum = l_ref[...]
        o = acc_ref[...] / jnp.concatenate([lsum, lsum], axis=-1)
        o_ref[...] = o.reshape(H, Q, KV_LORA).astype(o_ref.dtype)


def attn_sample(q_abs, cache_lat, cache_kr_t, new_lat, new_kr, tk=4096):
    H, N, W = q_abs.shape
    B, P, _ = cache_lat.shape
    Q = new_lat.shape[1]
    qpos = P + np.arange(Q)
    kpos = np.arange(P + Q)
    assert bool(np.all((kpos // CHUNK)[None, :] <= (qpos // CHUNK)[:, None]))
    return pl.pallas_call(
        _attn_sample_kernel,
        grid=(B, P // tk),
        in_specs=[pl.BlockSpec((H, Q, W), lambda b, kb: (0, b, 0)),
                  pl.BlockSpec((1, tk, KV_LORA), lambda b, kb: (b, kb, 0)),
                  pl.BlockSpec((1, MLA_ROPE, tk), lambda b, kb: (b, 0, kb)),
                  pl.BlockSpec((1, Q, KV_LORA), lambda b, kb: (b, 0, 0)),
                  pl.BlockSpec((1, Q, MLA_ROPE), lambda b, kb: (b, 0, 0))],
        out_specs=pl.BlockSpec((H, Q, KV_LORA), lambda b, kb: (0, b, 0)),
        out_shape=jax.ShapeDtypeStruct((H, N, KV_LORA), BF16),
        scratch_shapes=[pltpu.VMEM((H * Q, LANES), F32), pltpu.VMEM((H * Q, LANES), F32),
                        pltpu.VMEM((H * Q, KV_LORA), F32)],
        compiler_params=_cparams(2),
        name="attn_sample",
    )(q_abs, cache_lat, cache_kr_t, new_lat, new_kr)


def _unabsorb_kernel(o_ref, w_ref, out_ref):
    out_ref[...] = (_dot(o_ref[0], w_ref[0].astype(BF16))
                    + _dot(o_ref[1], w_ref[1].astype(BF16))).astype(out_ref.dtype)


def unabsorb(o_lat, wuv_pad):
    H, N, R = o_lat.shape
    return pl.pallas_call(
        _unabsorb_kernel,
        grid=(H // 2,),
        in_specs=[pl.BlockSpec((2, N, R), lambda p: (p, 0, 0)),
                  pl.BlockSpec((2, R, 2 * MLA_V), lambda p: (p, 0, 0))],
        out_specs=pl.BlockSpec((N, 2 * MLA_V), lambda p: (0, p)),
        out_shape=jax.ShapeDtypeStruct((N, H * MLA_V), BF16),
        compiler_params=_cparams(1),
        name="unabsorb",
    )(o_lat, wuv_pad)


def _rope_tables(pos):
    half = MLA_ROPE // 2
    inv = 1.0 / (ROPE_THETA ** (np.arange(half, dtype=np.float64) * 2.0 / MLA_ROPE))
    ang = np.asarray(pos, np.float64)[:, None] * inv[None, :]
    cos = np.concatenate([np.cos(ang), np.cos(ang)], axis=-1)
    sin = np.concatenate([np.sin(ang), np.sin(ang)], axis=-1)
    T = cos.shape[0]
    c128 = np.zeros((T, HEAD_PAD)); s128 = np.zeros((T, HEAD_PAD))
    c128[:, :MLA_NOPE] = 1.0
    c128[:, MLA_NOPE:MLA_NOPE + MLA_ROPE] = cos
    s128[:, MLA_NOPE:MLA_NOPE + MLA_ROPE] = sin
    return dict(cos32=jnp.asarray(cos, F32), sin32=jnp.asarray(sin, F32),
                c128=jnp.asarray(c128 * Q_PRESCALE, F32), s128=jnp.asarray(s128 * Q_PRESCALE, F32),
                cos_t=jnp.asarray(cos.T * Q_PRESCALE, F32), sin_t=jnp.asarray(sin.T * Q_PRESCALE, F32))


def _rot_half_cols(w):
    half = w.shape[-1] // 2
    return jnp.concatenate([-w[..., half:], w[..., :half]], axis=-1)


def _prep_weights(w_dkv, w_uk, w_uv, w_uq, router_w):
    D = D_MODEL
    w_lat, w_rope = w_dkv[:, :KV_LORA], w_dkv[:, KV_LORA:]
    pad96 = jnp.zeros((D, LANES - MLA_ROPE), F32)
    w_kv = jnp.concatenate([w_lat, w_rope, pad96, _rot_half_cols(w_rope), pad96], axis=-1)

    zpad = HEAD_PAD - MLA_NOPE
    wk_pad = jnp.pad(w_uk, ((0, 0), (0, 0), (0, zpad))).reshape(KV_LORA, MLA_HEADS * HEAD_PAD)
    ek = jnp.zeros((MLA_ROPE, MLA_HEADS, HEAD_PAD), F32)
    ek = ek.at[:, :, MLA_NOPE:MLA_NOPE + MLA_ROPE].set(
        jnp.broadcast_to(jnp.eye(MLA_ROPE, dtype=F32)[:, None, :], (MLA_ROPE, MLA_HEADS, MLA_ROPE)))
    ek = ek.reshape(MLA_ROPE, MLA_HEADS * HEAD_PAD)
    wvt = jnp.transpose(w_uv, (1, 2, 0))
    wvt_ext = jnp.pad(wvt, ((0, 0), (0, V_ROWS - MLA_V), (0, 0))).reshape(MLA_HEADS * V_ROWS, KV_LORA)
    ones_col = jnp.tile((jnp.arange(V_ROWS) >= MLA_V).astype(F32), MLA_HEADS).reshape(-1, 1)

    nb = w_uq.shape[0]
    qn, qr = w_uq[..., :MLA_NOPE], w_uq[..., MLA_NOPE:]
    z32 = jnp.zeros(qr.shape[:-1] + (HEAD_PAD - MLA_NOPE - MLA_ROPE,), F32)
    wq_pad = jnp.concatenate([qn, qr, z32], axis=-1).reshape(nb, Q_LORA, MLA_HEADS * HEAD_PAD)
    wq_rot = jnp.concatenate([jnp.zeros_like(qn), _rot_half_cols(qr), z32], axis=-1)
    wq_rot = wq_rot.reshape(nb, Q_LORA, MLA_HEADS * HEAD_PAD)
    wq_t = jnp.transpose(wq_pad, (0, 2, 1))
    wqr_t = jnp.transpose(_rot_half_cols(qr).reshape(nb, Q_LORA, MLA_HEADS * MLA_ROPE), (0, 2, 1))

    m_abs = jnp.zeros((MLA_HEADS, HEAD_PAD, KV_LORA + LANES), F32)
    m_abs = m_abs.at[:, :MLA_NOPE, :KV_LORA].set(jnp.transpose(w_uk, (1, 2, 0)))
    m_abs = m_abs.at[:, MLA_NOPE:MLA_NOPE + MLA_ROPE, KV_LORA:KV_LORA + MLA_ROPE].set(
        jnp.broadcast_to(jnp.eye(MLA_ROPE, dtype=F32), (MLA_HEADS, MLA_ROPE, MLA_ROPE)))

    wuv_h = jnp.transpose(w_uv, (1, 0, 2))
    even = jnp.pad(wuv_h, ((0, 0), (0, 0), (0, MLA_V)))
    odd = jnp.pad(wuv_h, ((0, 0), (0, 0), (MLA_V, 0)))
    wuv_pad = jnp.where((jnp.arange(MLA_HEADS) % 2 == 0)[:, None, None], even, odd)

    rw_t = jnp.transpose(router_w, (0, 2, 1))
    return dict(w_kv=w_kv, wk_pad=wk_pad, ek=ek, wvt_ext=wvt_ext, ones_col=ones_col, wq_pad=wq_pad, wq_rot=wq_rot, wq_t=wq_t, wqr_t=wqr_t,
                m_abs=m_abs, wuv_pad=wuv_pad, rw_t=rw_t)


def _mixer(st, l, P, W, packed):
    rows_kw = dict(rows_total=packed["total"], row0=packed["row0"], rows_buf=packed["buf"])
    x, m = st["x"], st["mod"][l]
    B, T, _ = x.shape
    n_a = P["hg_w_in"].shape[0]
    norm2 = (P["norm2_g"][l], 4, 3)
    if l < n_a:
        zf, zqig = hgrn_proj(x, P["norm1_g"][l], m, 1, 0, P["hg_w_in"], l)
        s0 = None if st["hg_state"] is None else st["hg_state"][l]
        o, s_new = gla(zqig, zf, st["lbs"][l], P["hg_onorm_g"][l], s0)
        st["hg_new"].append(s_new)
        st["x"], packed["buf"] = linear(o, P["hg_w_out"], l, F32, x=x, mod=m, gate_idx=2, next_norm=norm2,
                                        **rows_kw)
    else:
        bi = l - n_a
        h = st.pop("h_next", None)
        qt = st.pop("qt_next", None)
        if h is None and qt is None:
            h = norm_mod(x, P["norm1_g"][l], m, sc_idx=1, sh_idx=0)
        if st["past_lat"] is None:
            if qt is None:
                qt = mla_queries_t(h, P["w_dq"], P["q_norm_g"], W["wq_t"], W["wqr_t"], bi, st["cos_t"],
                                   st["sin_t"])
            o = attn_prompt(qt, st["k_all"], st["v_all"])
        else:
            q = mla_queries(h, P["w_dq"], P["q_norm_g"], W["wq_pad"], W["wq_rot"], bi, st["c128"], st["s128"])
            q_abs = absorb_queries(q.reshape(B * T, -1), W["m_abs"])
            o_lat = attn_sample(q_abs, st["past_lat"], st["past_kr"], st["lat"], st["kr"])
            o = unabsorb(o_lat, W["wuv_pad"]).reshape(B, T, -1)
        st["x"], packed["buf"] = linear(o, P["w_o"], bi, F32, x=x, mod=m, gate_idx=2, next_norm=norm2, **rows_kw)
    packed["row0"] += B * T


def _moe(groups, hp, l, P, W):
    n_tok = hp.shape[0]
    n_tiles = (TOP_K * n_tok) // MOE_TILE + N_EXPERTS
    pos, w8, tile_start, tile_count = route(hp, W["rw_t"], P["router_bias"], l, MOE_TILE)
    pos_flat = pos.reshape(-1)
    src = sc_invert(pos_flat, n_tok, n_tiles * MOE_TILE)
    xs = sc_gather(hp, src)
    out = moe_gemm(xs, tile_start[:, 0], tile_count[:, 0], P["exp_w_in"], P["exp_w_out"], l,
                   MOE_TILE, n_tiles)
    y8 = sc_gather(out, pos_flat).reshape(TOP_K, n_tok, -1)
    last = l == P["norm1_g"].shape[0] - 1
    with_kv = l == P["hg_w_in"].shape[0] - 1
    next_is_mla = not last and l + 1 >= P["hg_w_in"].shape[0]
    row0 = 0
    for st in groups:
        B, T, _ = st["x"].shape
        with_q = next_is_mla and st["past_lat"] is None
        outs = moe_combine(
            y8, w8, hp, P["sh_w_in"], P["sh_w_out"], st["x"], st["mod"][l], 5, l, row0,
            final_g=P["final_g"] if last else None,
            next_norm=(P["norm1_g"][l + 1], st["mod"][l + 1], 1, 0) if next_is_mla else None,
            shared_kv=(P["kv_in_g"], W["w_kv"], P["kv_lat_g"], st["cos32"], st["sin32"]) if with_kv else None,
            queries=(P["w_dq"], P["q_norm_g"], W["wq_t"], W["wqr_t"], l + 1 - P["hg_w_in"].shape[0],
                     st["cos_t"], st["sin_t"]) if with_q else None,
            rows=256 if with_q else 512)
        outs = list(outs) if isinstance(outs, (list, tuple)) else [outs]
        st["x"] = outs.pop(0)
        if next_is_mla and not with_q:
            st["h_next"] = outs.pop(0)
        if with_kv:
            st["lat"], st["kr"] = outs.pop(0), outs.pop(0)
        if with_q:
            st["qt_next"] = outs.pop(0)
        row0 += B * T


def _group_state(x, mod, pos, hg_state, past_lat, past_kr, lbs):
    return dict(x=x, mod=mod, hg_state=hg_state, past_lat=past_lat, past_kr=past_kr, lbs=lbs,
                **_rope_tables(pos), hg_new=[],
                lat=None, kr=None, k_all=None, v_all=None)


def kernel(x_prompt, x_sample, state_hgrn, cache_mla_latent, cache_mla_krope, c_prompt, c_sample, ada_w, ada_b, norm1_g, norm2_g, hg_w_in, hg_lb_logits, hg_onorm_g, hg_w_out, kv_in_g, w_dkv, kv_lat_g, w_uk, w_uv, w_dq, q_norm_g, w_uq, w_o, router_w, router_bias, exp_w_in, exp_w_out, sh_w_in, sh_w_out, final_g):
    Bp, Sp, _ = x_prompt.shape
    Bs, Ss, _ = x_sample.shape
    past = cache_mla_latent.shape[1]
    P = dict(norm1_g=norm1_g, norm2_g=norm2_g, hg_w_in=hg_w_in, hg_lb_logits=hg_lb_logits,
             hg_onorm_g=hg_onorm_g, hg_w_out=hg_w_out, kv_in_g=kv_in_g, kv_lat_g=kv_lat_g,
             w_dq=w_dq, q_norm_g=q_norm_g, w_o=w_o, router_bias=router_bias,
             exp_w_in=exp_w_in, exp_w_out=exp_w_out, sh_w_in=sh_w_in, sh_w_out=sh_w_out, final_g=final_g)
    W = _prep_weights(w_dkv, w_uk, w_uv, w_uq, router_w)
    mod = ada_mod(jnp.concatenate([c_prompt, c_sample], axis=0), ada_w, ada_b)
    lbs = jnp.cumsum(jax.nn.softmax(hg_lb_logits.astype(F32), axis=0), axis=0)
    gp = _group_state(x_prompt, mod[:, :Bp, None, :], np.arange(Sp), None, None, None, lbs)
    gs = _group_state(x_sample, mod[:, Bp:, None, :], past + np.arange(Ss), state_hgrn,
                      cache_mla_latent, jnp.transpose(cache_mla_krope, (0, 2, 1)), lbs)
    groups = [gp, gs]
    n_tok = sum(st["x"].shape[0] * st["x"].shape[1] for st in groups)
    n_a = hg_w_in.shape[0]
    for l in range(norm1_g.shape[0]):
        packed = dict(total=n_tok, row0=0, buf=None)
        for st in groups:
            _mixer(st, l, P, W, packed)
        _moe(groups, packed["buf"], l, P, W)
        if l == n_a - 1:
            gp["k_all"], gp["v_all"] = kv_expand(gp["lat"], gp["kr"], W["wk_pad"], W["ek"], W["wvt_ext"],
                                                 W["ones_col"])
    return (gp["x"], gs["x"], jnp.stack(gp["hg_new"], axis=0), jnp.stack(gs["hg_new"], axis=0),
            gp["lat"], gp["kr"], gs["lat"], gs["kr"])
```

```python
import dataclasses
import functools

import numpy as np
import jax
import jax.numpy as jnp
from jax import lax
from jax.experimental import pallas as pl
from jax.experimental.pallas import tpu as pltpu
from jax.experimental.pallas import tpu_sc as plsc

F32 = jnp.float32
BF16 = jnp.bfloat16

D_MODEL = 1024
CHUNK = 64
HG_HEADS = 8
HG_DK = 128
HG_DV = 128
MLA_HEADS = 16
MLA_NOPE = 64
MLA_ROPE = 32
MLA_V = 64
Q_LORA = 384
KV_LORA = 256
ROPE_THETA = 10000.0
N_EXPERTS = 64
TOP_K = 8
N_GROUPS = 8
TOPK_GROUPS = 4
EXPERT_FF = 256
SHARED_FF = 256
ROUTED_SCALE = 2.5
EPS = 1e-6

LANES = 128
HEAD_PAD = LANES
SAMPLE_KEY_SUB = 8
ATTN_LOOKAHEAD = 6
V_ROWS = MLA_V + 16
QK_SCALE = (MLA_NOPE + MLA_ROPE) ** -0.5
Q_PRESCALE = QK_SCALE * float(np.log2(np.e))
VMEM_LIMIT = 56 * 1024 * 1024
NEG_INF = float("-inf")
SC_CORES = 2
SC_SUBCORES = 16
SC_WORKERS = SC_CORES * SC_SUBCORES
SC_LANES = 16
SC_WINDOW = 64
MOE_TILE = 512
MOE_NBUF = 4
MOE_SUB = 1


def _cparams(n_axes):
    return pltpu.CompilerParams(dimension_semantics=("arbitrary",) * n_axes,
                                vmem_limit_bytes=VMEM_LIMIT)


def _silu(x):
    return x * jax.nn.sigmoid(x)


def _rms(x, g):
    ms = jnp.mean(x * x, axis=-1, keepdims=True)
    return x * lax.rsqrt(ms + EPS) * g


def _dot(a, b):
    return jnp.dot(a, b, preferred_element_type=F32)


def _dot_nt(a, b):
    return lax.dot_general(a, b, (((1,), (1,)), ((), ())), preferred_element_type=F32)


def _dot_tn(a, b):
    return lax.dot_general(a, b, (((0,), (0,)), ((), ())), preferred_element_type=F32)


def _row_blocks(B, T, rows):
    if T >= rows:
        assert T % rows == 0
        bb, tt = 1, rows
    else:
        assert rows % T == 0 and B % (rows // T) == 0
        bb, tt = rows // T, T
    nt = T // tt
    return bb, tt, (B // bb) * nt, (lambda i: (i // nt, i % nt))


def _ada_kernel(c_ref, w_ref, b_ref, o_ref):
    a = _silu(c_ref[...]).astype(BF16)
    o_ref[...] = _dot(a, w_ref[...].astype(BF16)) + b_ref[...]


def ada_mod(c, ada_w, ada_b):
    R, D = c.shape
    L, _, N = ada_w.shape
    tn = 1536
    return pl.pallas_call(
        _ada_kernel,
        grid=(L, N // tn),
        in_specs=[pl.BlockSpec((R, D), lambda l, j: (0, 0)),
                  pl.BlockSpec((None, D, tn), lambda l, j: (l, 0, j)),
                  pl.BlockSpec((None, 1, tn), lambda l, j: (l, 0, j))],
        out_specs=pl.BlockSpec((None, R, tn), lambda l, j: (l, 0, j)),
        out_shape=jax.ShapeDtypeStruct((L, R, N), F32),
        compiler_params=_cparams(2),
        name="ada_mod",
    )(c, ada_w, ada_b.reshape(L, 1, N))


def _pack_pairs(y):
    half = y.shape[-1] // 2
    bits = lax.bitcast_convert_type(y.astype(BF16).astype(F32), jnp.uint32)
    word = lax.shift_right_logical(bits[:, :half], jnp.uint32(16)) | bits[:, half:]
    return lax.bitcast_convert_type(word, jnp.int32)


def _unpack_pairs(word, dtype=BF16):
    u = lax.bitcast_convert_type(word, jnp.uint32)
    lo = lax.bitcast_convert_type(lax.shift_left(u, jnp.uint32(16)), F32)
    hi = lax.bitcast_convert_type(u & jnp.uint32(0xFFFF0000), F32)
    return lo.astype(dtype), hi.astype(dtype)


def _norm_kernel(x_ref, g_ref, sc_ref, sh_ref, o_ref):
    y = _rms(x_ref[...], g_ref[...]) * (1.0 + sc_ref[...]) + sh_ref[...]
    o_ref[...] = y.astype(o_ref.dtype)


def norm_mod(x, g, mod, sc_idx, sh_idx, rows=512):
    B, T, D = x.shape
    bb, tt, nblk, ij = _row_blocks(B, T, rows)
    xspec = pl.BlockSpec((bb, tt, D), lambda i: ij(i) + (0,))
    return pl.pallas_call(
        _norm_kernel,
        grid=(nblk,),
        in_specs=[xspec, pl.BlockSpec((1, D), lambda i: (0, 0)),
                  pl.BlockSpec((bb, 1, D), lambda i: (ij(i)[0], 0, sc_idx)),
                  pl.BlockSpec((bb, 1, D), lambda i: (ij(i)[0], 0, sh_idx))],
        out_specs=xspec,
        out_shape=jax.ShapeDtypeStruct((B, T, D), BF16),
        compiler_params=_cparams(1),
        name="norm_mod",
    )(x, g.reshape(1, D), mod, mod)


def _linear_kernel(*refs, residual, norm_next, shared_rows, n_main):
    if norm_next and shared_rows:
        a_ref, w_ref, x_ref, gate_ref, ng_ref, nsc_ref, nsh_ref, _, o_ref, hp_ref, wb_ref = refs
    elif norm_next:
        a_ref, w_ref, x_ref, gate_ref, ng_ref, nsc_ref, nsh_ref, o_ref, hp_ref, wb_ref = refs
    elif residual:
        a_ref, w_ref, x_ref, gate_ref, o_ref, wb_ref = refs
    else:
        a_ref, w_ref, o_ref, wb_ref = refs

    @pl.when(pl.program_id(1) == 0)
    def _():
        wb_ref[...] = w_ref[...].astype(BF16)

    def main():
        bb, tt, K = a_ref.shape
        y = _dot(a_ref[...].reshape(bb * tt, K).astype(BF16), wb_ref[...])
        y = y.reshape(bb, tt, y.shape[-1])
        if residual:
            y = x_ref[...] + gate_ref[...] * y
        o_ref[...] = y.astype(o_ref.dtype)
        if norm_next:
            h = _rms(y, ng_ref[...]) * (1.0 + nsc_ref[...]) + nsh_ref[...]
            hp_ref[...] = _pack_pairs(h.reshape(bb * tt, h.shape[-1]))

    if n_main is None:
        main()
    else:
        pl.when(pl.program_id(1) < n_main)(main)

        @pl.when(pl.program_id(1) >= n_main)
        def _():
            hp_ref[...] = jnp.zeros(hp_ref.shape, hp_ref.dtype)


def linear(a, w, l, out_dtype, x=None, mod=None, gate_idx=0, rows=512, tn=1024, next_norm=None,
           rows_total=None, row0=0, rows_buf=None):
    B, T, K = a.shape
    _, _, N = w.shape
    tn = min(tn, N)
    bb, tt, nblk, ij0 = _row_blocks(B, T, rows)
    n_extra = 0
    if next_norm is not None and rows_buf is None and rows_total is not None:
        assert row0 == 0 and (rows_total - B * T) % (bb * tt) == 0
        n_extra = (rows_total - B * T) // (bb * tt)

    def ij(i):
        return ij0(jnp.minimum(i, nblk - 1)) if n_extra else ij0(i)

    in_specs = [pl.BlockSpec((bb, tt, K), lambda j, i: ij(i) + (0,)),
                pl.BlockSpec((None, K, tn), lambda j, i: (l, 0, j))]
    args = [a, w]
    ospec = pl.BlockSpec((bb, tt, tn), lambda j, i: ij(i) + (j,))
    out_specs = ospec
    out_shape = jax.ShapeDtypeStruct((B, T, N), out_dtype)
    aliases = {}
    if x is not None:
        gsteps = D_MODEL // tn
        in_specs += [ospec, pl.BlockSpec((bb, 1, tn), lambda j, i: (ij(i)[0], 0, gate_idx * gsteps + j))]
        args += [x, mod]
    if next_norm is not None:
        assert x is not None and tn == N
        gain, sc_idx, sh_idx = next_norm
        in_specs += [pl.BlockSpec((1, N), lambda j, i: (0, 0)),
                     pl.BlockSpec((bb, 1, N), lambda j, i: (ij(i)[0], 0, sc_idx)),
                     pl.BlockSpec((bb, 1, N), lambda j, i: (ij(i)[0], 0, sh_idx))]
        args += [gain.reshape(1, N), mod, mod]
        assert row0 % (bb * tt) == 0
        off = row0 // (bb * tt)
        out_specs = [ospec, pl.BlockSpec((bb * tt, N // 2), lambda j, i: (off + i, 0))]
        out_shape = [out_shape, jax.ShapeDtypeStruct((rows_total or B * T, N // 2), jnp.int32)]
        if rows_buf is not None:
            in_specs.append(pl.BlockSpec(memory_space=pl.ANY))
            args.append(rows_buf)
            aliases = {len(args) - 1: 1}
    return pl.pallas_call(
        functools.partial(_linear_kernel, residual=x is not None, norm_next=next_norm is not None,
                          shared_rows=rows_buf is not None, n_main=nblk if n_extra else None),
        grid=(N // tn, nblk + n_extra),
        in_specs=in_specs,
        out_specs=out_specs,
        out_shape=out_shape,
        scratch_shapes=[pltpu.VMEM((K, tn), BF16)],
        input_output_aliases=aliases,
        compiler_params=_cparams(2),
        name="linear",
    )(*args)


def _hgrn_proj_kernel(x_ref, g_ref, sc_ref, sh_ref, w_ref, zf_ref, zqig_ref, h_b, w_b):
    i = pl.program_id(0)
    j = pl.program_id(1)
    bb, tt, D = x_ref.shape

    @pl.when(i == 0)
    def _():
        w_b[j] = w_ref[...].astype(BF16)

    @pl.when(j == 0)
    def _():
        h = _rms(x_ref[...], g_ref[...]) * (1.0 + sc_ref[...]) + sh_ref[...]
        h_b[...] = h.reshape(bb * tt, D).astype(BF16)

    y = _dot(h_b[...], w_b[j]).reshape(bb, tt, -1)

    @pl.when(j == 1)
    def _():
        zf_ref[...] = y

    @pl.when(j != 1)
    def _():
        zqig_ref[...] = y.astype(zqig_ref.dtype)


def hgrn_proj(x, g, mod, sc_idx, sh_idx, w_in, l, rows=1024):
    B, T, D = x.shape
    bb, tt, nblk, ij = _row_blocks(B, T, min(rows, B * T))
    xspec = pl.BlockSpec((bb, tt, D), lambda i, j: ij(i) + (0,))
    return pl.pallas_call(
        _hgrn_proj_kernel,
        grid=(nblk, 4),
        in_specs=[xspec,
                  pl.BlockSpec((1, D), lambda i, j: (0, 0)),
                  pl.BlockSpec((bb, 1, D), lambda i, j: (ij(i)[0], 0, sc_idx)),
                  pl.BlockSpec((bb, 1, D), lambda i, j: (ij(i)[0], 0, sh_idx)),
                  pl.BlockSpec((None, D, D), lambda i, j: (l, 0, jnp.where(i == 0, j, 3)))],
        out_specs=[xspec,
                   pl.BlockSpec((bb, tt, D), lambda i, j: ij(i) + (j - (j >= 1),))],
        out_shape=[jax.ShapeDtypeStruct((B, T, D), F32), jax.ShapeDtypeStruct((B, T, 3 * D), BF16)],
        scratch_shapes=[pltpu.VMEM((bb * tt, D), BF16), pltpu.VMEM((4, D, D), BF16)],
        compiler_params=_cparams(2),
        name="hgrn_proj",
    )(x, g.reshape(1, D), mod, mod, w_in)


def _gla_kernel(*refs, L, n_chunks, has_init):
    if has_init:
        q_ref, f_ref, i_ref, g_ref, lb_ref, on_ref, s0_ref, o_ref, so_ref, st_ref = refs
    else:
        q_ref, f_ref, i_ref, g_ref, lb_ref, on_ref, o_ref, so_ref, st_ref = refs
    t = pl.program_id(1)
    H = st_ref.shape[0]

    @pl.when(t == 0)
    def _():
        for h in range(H):
            if has_init:
                st_ref[h] = s0_ref[0, h].T
            else:
                st_ref[h] = jnp.zeros(st_ref.shape[1:], F32)

    lb = lb_ref[...]
    onorm = on_ref[...]
    row = lax.broadcasted_iota(jnp.int32, (L, L), 0)
    col = lax.broadcasted_iota(jnp.int32, (L, L), 1)
    causal = col <= row
    tri = causal.astype(BF16)

    def chunk(c, carry):
        rows = pl.ds(pl.multiple_of(c * L, L), L)

        def write_o(sl, o):
            o_ref[0, rows, sl] = o.astype(o_ref.dtype)

        _gla_chunk(q_ref[0, rows, :], f_ref[0, rows, :], i_ref[0, rows, :], g_ref[0, rows, :],
                   lb, onorm, tri, causal, st_ref, write_o)
        return carry

    lax.fori_loop(0, n_chunks, chunk, 0, unroll=4 if n_chunks % 4 == 0 else 1)

    @pl.when(t == pl.num_programs(1) - 1)
    def _():
        for h in range(H):
            so_ref[0, h] = st_ref[h].T


def _gla_chunk(q, f, v, g, lb, onorm, tri, causal, st_ref, write_o):
    L = q.shape[0]
    H = st_ref.shape[0]
    mid = L // 2 - 1
    q = _silu(q.astype(F32))
    fg = lb + (1.0 - lb) * jax.nn.sigmoid(f)
    k = 1.0 - fg
    v = v.astype(BF16)
    gate = _silu(g.astype(F32))
    logf = jnp.log(fg)
    hi = logf.astype(BF16)
    lo = (logf - hi.astype(F32)).astype(BF16)
    b = _dot(tri, hi) + _dot(tri, lo)
    b_mid = b[mid:mid + 1, :]
    b_last = b[L - 1:L, :]
    qa = q * jnp.exp(b - b_mid)
    kb = k * jnp.exp(b_mid - b)
    qe = (qa * jnp.exp(b_mid)).astype(BF16)
    kd = (kb * jnp.exp(b_last - b_mid)).astype(BF16)
    qa = qa.astype(BF16)
    kb = kb.astype(BF16)
    decay = jnp.exp(b_last)
    sls = [slice(h * HG_DK, (h + 1) * HG_DK) for h in range(H)]
    sts = [st_ref[h] for h in range(H)]
    scores = [_dot_nt(qa[:, sl], kb[:, sl]) for sl in sls]
    inter = [_dot_nt(qe[:, sl], st.astype(BF16)) for sl, st in zip(sls, sts)]
    outer = [_dot_tn(v[:, sl], kd[:, sl]) for sl in sls]
    intra = [_dot(jnp.where(causal, sc, 0.0).astype(BF16), v[:, sl]) for sc, sl in zip(scores, sls)]
    for h, sl in enumerate(sls):
        st_ref[h] = sts[h] * decay[:, sl] + outer[h]
        write_o(sl, _rms(inter[h] + intra[h], onorm[:, sl]) * gate[:, sl])


def gla(zqig, zf, lb, onorm_g, s0):
    B, T, D = zf.shape
    L = CHUNK if T % CHUNK == 0 else T
    tt = min(T, 1024)
    n_chunks = tt // L
    H = HG_HEADS

    def zspec(part):
        return pl.BlockSpec((1, tt, D), lambda b, t: (b, t, part))

    hspec = pl.BlockSpec((1, D), lambda b, t: (0, 0))
    sspec = pl.BlockSpec((1, H, HG_DK, HG_DV), lambda b, t: (b, 0, 0, 0))
    in_specs = [zspec(0), zspec(0), zspec(1), zspec(2), hspec, hspec]
    args = [zqig, zf, zqig, zqig, lb.reshape(1, D), onorm_g.reshape(1, D)]
    if s0 is not None:
        in_specs.append(sspec)
        args.append(s0)
    return pl.pallas_call(
        functools.partial(_gla_kernel, L=L, n_chunks=n_chunks, has_init=s0 is not None),
        grid=(B, T // tt),
        in_specs=in_specs,
        out_specs=[pl.BlockSpec((1, tt, D), lambda b, t: (b, t, 0)), sspec],
        out_shape=[jax.ShapeDtypeStruct((B, T, D), BF16),
                   jax.ShapeDtypeStruct((B, H, HG_DK, HG_DV), F32)],
        scratch_shapes=[pltpu.VMEM((H, HG_DV, HG_DK), F32)],
        compiler_params=_cparams(2),
        name="gla",
    )(*args)


def _route_kernel(h_ref, rw_ref, bias_ref, pos_ref, w_ref, te_ref, nu_ref,
                  e_s, r_s, base_s, start_s, *, tile_rows):
    ph = pl.program_id(0)
    i = pl.program_id(1)
    M = h_ref.shape[0]
    half = h_ref.shape[1]
    G, E = N_GROUPS, N_EXPERTS // N_GROUPS
    e_flat = lax.broadcasted_iota(jnp.int32, (N_EXPERTS, M), 0)

    @pl.when(ph == 1)
    def _():
        @pl.when(i == 0)
        def _():
            cnt = base_s[...]
            padded = jnp.floor((cnt + (tile_rows - 1)) * (1.0 / tile_rows)) * tile_rows
            r = lax.broadcasted_iota(jnp.int32, (N_EXPERTS, N_EXPERTS), 0)
            c = lax.broadcasted_iota(jnp.int32, (N_EXPERTS, N_EXPERTS), 1)
            start = jnp.dot((c < r).astype(F32), padded, preferred_element_type=F32,
                            precision=lax.Precision.HIGHEST)
            start_s[...] = start
            te_ref[...] = (start * (1.0 / tile_rows)).astype(jnp.int32)
            nu_ref[...] = (padded * (1.0 / tile_rows)).astype(jnp.int32)

        start_col = start_s[:, :1]
        for k in range(TOP_K):
            hit = e_flat == e_s[i, k:k + 1, :]
            seg = jnp.sum(jnp.where(hit, start_col, 0.0), axis=0, keepdims=True)
            pos_ref[k:k + 1, :] = (seg + r_s[i, k:k + 1, :]).astype(jnp.int32)

    @pl.when(ph == 0)
    def _():
        _route_pass0(h_ref, rw_ref, bias_ref, w_ref, e_s, r_s, base_s, i, M, half, G, E)


def _route_pass0(h_ref, rw_ref, bias_ref, w_ref, e_s, r_s, base_s, i, M, half, G, E):
    @pl.when(i == 0)
    def _():
        base_s[...] = jnp.zeros_like(base_s)

    lo, hi = _unpack_pairs(h_ref[...])
    rw = rw_ref[...].astype(BF16)
    logits = _dot_nt(rw[:, :half], lo) + _dot_nt(rw[:, half:], hi)
    s = jax.nn.sigmoid(logits)
    sb = (s + bias_ref[...]).reshape(G, E, M)
    s = s.reshape(G, E, M)
    e_in = lax.broadcasted_iota(jnp.int32, (G, E, M), 1).astype(F32)
    g_id = lax.broadcasted_iota(jnp.int32, (G, 1, M), 0)
    e_id = lax.broadcasted_iota(jnp.int32, (G, E, M), 0).astype(F32) * E + e_in

    def all_max(a):
        return jnp.max(jnp.max(a, axis=0, keepdims=True), axis=1, keepdims=True)

    def all_min(a):
        return jnp.min(jnp.min(a, axis=0, keepdims=True), axis=1, keepdims=True)

    def all_sum(a):
        return jnp.sum(jnp.sum(a, axis=0, keepdims=True), axis=1, keepdims=True)

    m1 = jnp.max(sb, axis=1, keepdims=True)
    first = jnp.min(jnp.where(sb == m1, e_in, float(E)), axis=1, keepdims=True)
    m2 = jnp.max(jnp.where(e_in == first, NEG_INF, sb), axis=1, keepdims=True)
    gs = m1 + m2

    rank = jnp.zeros((G, 1, M), jnp.int32)
    for j in range(G):
        gj = gs[j:j + 1]
        beats = (gj > gs) | ((gj == gs) & (j < g_id))
        rank = rank + beats.astype(jnp.int32)
    gsel = rank < TOPK_GROUPS

    vals = jnp.where(gsel, sb, NEG_INF)
    selm = jnp.zeros((G, E, M), F32)
    chosen, score = [], []
    for _ in range(TOP_K):
        m = all_max(vals)
        first = all_min(jnp.where(vals == m, e_id, float(N_EXPERTS)))
        hit = e_id == first
        score.append(all_sum(jnp.where(hit, s, 0.0)))
        selm = jnp.where(hit, 1.0, selm)
        vals = jnp.where(hit, NEG_INF, vals)
        chosen.append(first)

    tot = score[0]
    for sc in score[1:]:
        tot = tot + sc
    norm = ROUTED_SCALE / tot

    selm = selm.reshape(N_EXPERTS, M)
    earlier = (lax.broadcasted_iota(jnp.int32, (M, M), 0)
               < lax.broadcasted_iota(jnp.int32, (M, M), 1)).astype(BF16)
    rank = (base_s[:, :1] + _dot(selm.astype(BF16), earlier)).reshape(G, E, M)
    base_s[...] = base_s[...] + jnp.sum(selm, axis=1, keepdims=True)
    for k in range(TOP_K):
        hit = e_id == chosen[k]
        e_s[i, k:k + 1, :] = chosen[k].reshape(1, M).astype(jnp.int32)
        r_s[i, k:k + 1, :] = all_sum(jnp.where(hit, rank, 0.0)).reshape(1, M)
        w_ref[k:k + 1, :] = (score[k] * norm).reshape(1, M)


def route(hp, router_w_t, router_bias, l, tile_rows, rows=1408):
    N, half = hp.shape
    M = rows
    nT = N // M
    assert N % M == 0

    def p0(ph, i):
        return i * (1 - ph) + (nT - 1) * ph

    return pl.pallas_call(
        functools.partial(_route_kernel, tile_rows=tile_rows),
        grid=(2, nT),
        in_specs=[pl.BlockSpec((M, half), lambda ph, i: (p0(ph, i), 0)),
                  pl.BlockSpec((None, N_EXPERTS, 2 * half), lambda ph, i: (l, 0, 0)),
                  pl.BlockSpec((None, N_EXPERTS, 1), lambda ph, i: (l, 0, 0))],
        out_specs=[pl.BlockSpec((TOP_K, M), lambda ph, i: (0, i * ph)),
                   pl.BlockSpec((TOP_K, M), lambda ph, i: (0, p0(ph, i))),
                   pl.BlockSpec((N_EXPERTS, LANES), lambda ph, i: (0, 0)),
                   pl.BlockSpec((N_EXPERTS, LANES), lambda ph, i: (0, 0))],
        out_shape=[jax.ShapeDtypeStruct((TOP_K, N), jnp.int32),
                   jax.ShapeDtypeStruct((TOP_K, N), F32),
                   jax.ShapeDtypeStruct((N_EXPERTS, LANES), jnp.int32),
                   jax.ShapeDtypeStruct((N_EXPERTS, LANES), jnp.int32)],
        scratch_shapes=[pltpu.VMEM((nT, TOP_K, M), jnp.int32), pltpu.VMEM((nT, TOP_K, M), F32),
                        pltpu.VMEM((N_EXPERTS, LANES), F32), pltpu.VMEM((N_EXPERTS, LANES), F32)],
        compiler_params=_cparams(2),
        name="route",
    )(hp, router_w_t, router_bias.reshape(-1, N_EXPERTS, 1))


def _sc_mesh():
    return plsc.VectorSubcoreMesh(core_axis_name="core", subcore_axis_name="subcore")


def sc_invert(pos_flat, n_tok, n_out):
    n = pos_flat.shape[0]
    per = n_out // SC_WORKERS
    chunk = n_tok
    assert n_out % SC_WORKERS == 0 and per % SC_LANES == 0
    assert n_tok % chunk == 0 and n % chunk == 0 and chunk % SC_LANES == 0
    cp = pltpu.CompilerParams()
    if "needs_layout_passes" in pltpu.CompilerParams.__dataclass_fields__:
        cp = dataclasses.replace(cp, needs_layout_passes=False)

    @functools.partial(
        pl.kernel, out_type=jax.ShapeDtypeStruct((n_out,), jnp.int32), mesh=_sc_mesh(),
        scratch_types=[pltpu.VMEM((chunk,), jnp.int32), pltpu.VMEM((per,), jnp.int32)],
        compiler_params=cp, name="sc_invert")
    def k(pos_hbm, src_hbm, pos_v, src_v):
        wid = lax.axis_index("subcore") * SC_CORES + lax.axis_index("core")
        lo = wid * per
        lane = lax.iota(jnp.int32, SC_LANES)

        @pl.loop(0, per, step=SC_LANES)
        def _(r):
            src_v[pl.ds(r, SC_LANES)] = lax.rem(lo + r + lane, n_tok)

        @pl.loop(0, n // chunk)
        def _(c):
            base = c * chunk
            pltpu.sync_copy(pos_hbm.at[pl.ds(base, chunk)], pos_v)
            tok0 = lax.rem(base, n_tok)

            @plsc.parallel_loop(0, chunk, step=SC_LANES, unroll=8)
            def _(r):
                p = pos_v[pl.ds(r, SC_LANES)] - lo
                mine = (p >= 0) & (p < per)
                plsc.store_scatter(src_v, [jnp.where(mine, p, 0)], tok0 + r + lane, mask=mine)

        pltpu.sync_copy(src_v, src_hbm.at[pl.ds(lo, per)])

    return k(pos_flat)


def sc_gather(x, idx):
    n = idx.shape[0]
    dim = x.shape[1]
    assert n % (SC_WINDOW * SC_WORKERS) == 0

    @functools.partial(
        pl.kernel, out_type=jax.ShapeDtypeStruct((n, dim), x.dtype), mesh=_sc_mesh(),
        scratch_types=[], name="sc_gather")
    def k(x_hbm, i_hbm, o_hbm):
        def body(i_vmem, o_vmem):
            pltpu.sync_copy(x_hbm.at[i_vmem.at[0]], o_vmem)

        pltpu.emit_pipeline(
            body, grid=(n // SC_WINDOW,),
            in_specs=[pl.BlockSpec((1, SC_WINDOW), index_map=lambda i: (i, 0))],
            out_specs=[pl.BlockSpec((SC_WINDOW, dim), index_map=lambda i: (i, 0))],
            core_axis_name=("core", "subcore"),
            dimension_semantics=(pltpu.PARALLEL,),
        )(i_hbm, o_hbm)

    return k(x, idx.reshape(n // SC_WINDOW, SC_WINDOW))


def _moe_gemm_kernel(ts_ref, tn_ref, x_hbm, wi_ref, wo_ref, o_hbm, wi_b, wo_b, xbuf, obuf, in_sem, out_sem,
                     *, tile_rows, n_tiles):
    e = pl.program_id(0)
    last = pl.num_programs(0) - 1
    t0 = ts_ref[e]
    n = tn_ref[e]
    n_used = ts_ref[last] + tn_ref[last]

    def x_copy(g, slot):
        rows = pl.ds(pl.multiple_of(g * tile_rows, tile_rows), tile_rows)
        return pltpu.make_async_copy(x_hbm.at[rows], xbuf.at[slot], in_sem.at[slot])

    def o_copy(g, slot):
        rows = pl.ds(pl.multiple_of(g * tile_rows, tile_rows), tile_rows)
        return pltpu.make_async_copy(obuf.at[slot], o_hbm.at[rows], out_sem.at[slot])

    @pl.when(e == 0)
    def _():
        for g0 in range(MOE_NBUF - 1):
            @pl.when(g0 < n_used)
            def _():
                x_copy(g0, g0).start()

    @pl.when(n > 0)
    def _():
        wi_b[...] = wi_ref[...].astype(BF16)
        wo_b[...] = wo_ref[...].astype(BF16)

    def tile(i, carry):
        g = t0 + i
        slot = lax.rem(g, MOE_NBUF)
        x_copy(g, slot).wait()
        ahead = g + (MOE_NBUF - 1)

        @pl.when(ahead < n_used)
        def _():
            x_copy(ahead, lax.rem(ahead, MOE_NBUF)).start()

        @pl.when(g >= MOE_NBUF)
        def _():
            o_copy(g - MOE_NBUF, slot).wait()

        rows = tile_rows // MOE_SUB
        half = xbuf.shape[2]
        xs = [_unpack_pairs(xbuf[slot, r * rows:(r + 1) * rows, :]) for r in range(MOE_SUB)]
        hus = [_dot(lo, wi_b[:half, :]) + _dot(hi, wi_b[half:, :]) for lo, hi in xs]
        acts = [(_silu(hu[:, :EXPERT_FF]) * hu[:, EXPERT_FF:]).astype(BF16) for hu in hus]
        outs = [_dot(act, wo_b[...]) for act in acts]
        for r, out in enumerate(outs):
            obuf[slot, r * rows:(r + 1) * rows, :] = _pack_pairs(out)
        o_copy(g, slot).start()
        return carry

    lax.fori_loop(0, n, tile, 0)

    @pl.when(e == last)
    def _():
        for back in range(MOE_NBUF, 0, -1):
            @pl.when(n_used >= back)
            def _():
                o_copy(n_used - back, lax.rem(n_used - back, MOE_NBUF)).wait()

        obuf[...] = jnp.zeros(obuf.shape, obuf.dtype)
        n_clear = n_tiles - n_used

        def clear(i, carry):
            slot = lax.rem(i, MOE_NBUF)

            @pl.when(i >= MOE_NBUF)
            def _():
                o_copy(n_used + i - MOE_NBUF, slot).wait()

            o_copy(n_used + i, slot).start()
            return carry

        lax.fori_loop(0, n_clear, clear, 0)
        for back in range(MOE_NBUF, 0, -1):
            @pl.when(n_clear >= back)
            def _():
                o_copy(n_tiles - back, lax.rem(n_clear - back, MOE_NBUF)).wait()


def moe_gemm(xs, tile_start, tile_count, exp_w_in, exp_w_out, l, tile_rows, n_tiles):
    P, half = xs.shape
    D = 2 * half
    assert P == n_tiles * tile_rows
    hbm = pl.BlockSpec(memory_space=pl.ANY)
    grid_spec = pltpu.PrefetchScalarGridSpec(
        num_scalar_prefetch=2,
        grid=(N_EXPERTS,),
        in_specs=[hbm,
                  pl.BlockSpec((None, None, D, 2 * EXPERT_FF), lambda e, ts, tn: (l, e, 0, 0)),
                  pl.BlockSpec((None, None, EXPERT_FF, D), lambda e, ts, tn: (l, e, 0, 0))],
        out_specs=hbm,
        scratch_shapes=[pltpu.VMEM((D, 2 * EXPERT_FF), BF16), pltpu.VMEM((EXPERT_FF, D), BF16),
                        pltpu.VMEM((MOE_NBUF, tile_rows, half), jnp.int32),
                        pltpu.VMEM((MOE_NBUF, tile_rows, half), jnp.int32),
                        pltpu.SemaphoreType.DMA((MOE_NBUF,)), pltpu.SemaphoreType.DMA((MOE_NBUF,))],
    )
    return pl.pallas_call(
        functools.partial(_moe_gemm_kernel, tile_rows=tile_rows, n_tiles=n_tiles),
        grid_spec=grid_spec,
        out_shape=jax.ShapeDtypeStruct((P, half), jnp.int32),
        compiler_params=_cparams(1),
        name="moe_gemm",
    )(tile_start, tile_count, xs, exp_w_in, exp_w_out)


def _moe_combine_kernel(*refs, final, norm_next, shared_kv, queries):
    it = iter(refs)
    y_ref, w_ref, h_ref, si_ref, so_ref, x_ref, g2_ref = (next(it) for _ in range(7))
    fg_ref = next(it) if final else None
    ng_ref, nsc_ref, nsh_ref = (next(it) for _ in range(3)) if norm_next else (None,) * 3
    kg_ref, wkv_ref, lg_ref, cos_ref, sin_ref = (next(it) for _ in range(5)) if shared_kv else (None,) * 5
    wdq_ref, qg_ref, wqt_ref, wqrt_ref, cost_ref, sint_ref = (next(it) for _ in range(6)) if queries else (None,) * 6
    o_ref = next(it)
    hn_ref = next(it) if norm_next and not queries else None
    lat_ref, kr_ref = (next(it), next(it)) if shared_kv else (None, None)
    qt_ref = next(it) if queries else None
    si_b, so_b = next(it), next(it)
    wkv_b = next(it) if shared_kv else None
    wdq_b, wqt_b, wqrt_b = (next(it), next(it), next(it)) if queries else (None,) * 3

    @pl.when(pl.program_id(0) == 0)
    def _():
        si_b[...] = si_ref[...].astype(BF16)
        so_b[...] = so_ref[...].astype(BF16)
        if shared_kv:
            wkv_b[...] = wkv_ref[...].astype(BF16)
        if queries:
            wdq_b[...] = wdq_ref[...].astype(BF16)
            wqt_b[...] = wqt_ref[...].astype(BF16)
            wqrt_b[...] = wqrt_ref[...].astype(BF16)

    bb, tt, D = x_ref.shape
    half = D // 2
    w = w_ref[...].T
    acc_lo = jnp.zeros((bb * tt, half), F32)
    acc_hi = jnp.zeros((bb * tt, half), F32)
    for k in range(TOP_K):
        lo, hi = _unpack_pairs(y_ref[k], F32)
        acc_lo = acc_lo + w[:, k:k + 1] * lo
        acc_hi = acc_hi + w[:, k:k + 1] * hi
    hlo, hhi = _unpack_pairs(h_ref[...])
    hu = _dot(hlo, si_b[:half, :]) + _dot(hhi, si_b[half:, :])
    act = (_silu(hu[:, :SHARED_FF]) * hu[:, SHARED_FF:]).astype(BF16)
    y = jnp.concatenate([acc_lo, acc_hi], axis=-1) + _dot(act, so_b[...])
    x_new = x_ref[...] + g2_ref[...] * y.reshape(bb, tt, D)
    o_ref[...] = _rms(x_new, fg_ref[...]) if final else x_new
    if norm_next:
        hn = (_rms(x_new, ng_ref[...]) * (1.0 + nsc_ref[...]) + nsh_ref[...]).astype(BF16)
        if queries:
            _queries_t(hn.reshape(bb * tt, D), wdq_b, qg_ref[...], wqt_b, wqrt_b, cost_ref[...], sint_ref[...],
                       qt_ref)
        else:
            hn_ref[...] = hn
    if shared_kv:
        xn = _rms(x_new, kg_ref[...]).reshape(bb * tt, D).astype(BF16)
        z = _dot(xn, wkv_b[...])
        lat_ref[...] = _rms(z[:, :KV_LORA], lg_ref[...]).reshape(bb, tt, KV_LORA)
        zr = z[:, KV_LORA:KV_LORA + MLA_ROPE].reshape(bb, tt, MLA_ROPE)
        zq = z[:, KV_LORA + LANES:KV_LORA + LANES + MLA_ROPE].reshape(bb, tt, MLA_ROPE)
        kr_ref[...] = zr * cos_ref[...] + zq * sin_ref[...]


def moe_combine(y8, w8, hp, sh_w_in, sh_w_out, x, mod, gate_idx, l, row0, final_g=None, next_norm=None,
                shared_kv=None, queries=None, rows=512):
    B, T, D = x.shape
    half = D // 2
    bb, tt, nblk, ij = _row_blocks(B, T, rows)
    M = bb * tt
    assert row0 % M == 0
    off = row0 // M
    xspec = pl.BlockSpec((bb, tt, D), lambda i: ij(i) + (0,))
    in_specs = [pl.BlockSpec((TOP_K, M, half), lambda i: (0, off + i, 0)),
                pl.BlockSpec((TOP_K, M), lambda i: (0, off + i)),
                pl.BlockSpec((M, half), lambda i: (off + i, 0)),
                pl.BlockSpec((None, D, 2 * SHARED_FF), lambda i: (l, 0, 0)),
                pl.BlockSpec((None, SHARED_FF, D), lambda i: (l, 0, 0)),
                xspec,
                pl.BlockSpec((bb, 1, D), lambda i: (ij(i)[0], 0, gate_idx))]
    args = [y8, w8, hp, sh_w_in, sh_w_out, x, mod]
    out_specs = xspec
    out_shape = jax.ShapeDtypeStruct((B, T, D), F32)
    if final_g is not None:
        assert next_norm is None
        in_specs.append(pl.BlockSpec((1, D), lambda i: (0, 0)))
        args.append(final_g.reshape(1, D))
    if next_norm is not None:
        gain, mod_next, sc_idx, sh_idx = next_norm
        in_specs += [pl.BlockSpec((1, D), lambda i: (0, 0)),
                     pl.BlockSpec((bb, 1, D), lambda i: (ij(i)[0], 0, sc_idx)),
                     pl.BlockSpec((bb, 1, D), lambda i: (ij(i)[0], 0, sh_idx))]
        args += [gain.reshape(1, D), mod_next, mod_next]
        if queries is None:
            out_specs = [xspec, xspec]
            out_shape = [out_shape, jax.ShapeDtypeStruct((B, T, D), BF16)]
    scratch = [pltpu.VMEM((D, 2 * SHARED_FF), BF16), pltpu.VMEM((SHARED_FF, D), BF16)]
    if shared_kv is not None:
        kv_in_g, w_kv, kv_lat_g, cos32, sin32 = shared_kv
        tspec = pl.BlockSpec((tt, MLA_ROPE), lambda i: (ij(i)[1], 0))
        in_specs += [pl.BlockSpec((1, D), lambda i: (0, 0)),
                     pl.BlockSpec(w_kv.shape, lambda i: (0, 0)),
                     pl.BlockSpec((1, KV_LORA), lambda i: (0, 0)),
                     tspec, tspec]
        args += [kv_in_g.reshape(1, D), w_kv, kv_lat_g.reshape(1, KV_LORA), cos32, sin32]
        out_specs = list(out_specs) if isinstance(out_specs, list) else [out_specs]
        out_shape = list(out_shape) if isinstance(out_shape, list) else [out_shape]
        out_specs += [pl.BlockSpec((bb, tt, KV_LORA), lambda i: ij(i) + (0,)),
                      pl.BlockSpec((bb, tt, MLA_ROPE), lambda i: ij(i) + (0,))]
        out_shape += [jax.ShapeDtypeStruct((B, T, KV_LORA), F32), jax.ShapeDtypeStruct((B, T, MLA_ROPE), F32)]
        scratch.append(pltpu.VMEM(w_kv.shape, BF16))
    if queries is not None:
        assert next_norm is not None and bb == 1
        w_dq, q_norm_g, wq_t, wqr_t, bi, cos_t, sin_t = queries
        NQ, NR = wq_t.shape[1], wqr_t.shape[1]
        tspec_t = pl.BlockSpec((MLA_ROPE, tt), lambda i: (0, ij(i)[1]))
        in_specs += [pl.BlockSpec((None, D, Q_LORA), lambda i: (bi, 0, 0)),
                     pl.BlockSpec((None, 1, Q_LORA), lambda i: (bi, 0, 0)),
                     pl.BlockSpec((None, NQ, Q_LORA), lambda i: (bi, 0, 0)),
                     pl.BlockSpec((None, NR, Q_LORA), lambda i: (bi, 0, 0)),
                     tspec_t, tspec_t]
        args += [w_dq, q_norm_g.reshape(-1, 1, Q_LORA), wq_t, wqr_t, cos_t, sin_t]
        out_specs = list(out_specs) if isinstance(out_specs, list) else [out_specs]
        out_shape = list(out_shape) if isinstance(out_shape, list) else [out_shape]
        out_specs.append(pl.BlockSpec((1, NQ, tt), lambda i: (ij(i)[0], 0, ij(i)[1])))
        out_shape.append(jax.ShapeDtypeStruct((B, NQ, T), BF16))
        scratch += [pltpu.VMEM((D, Q_LORA), BF16), pltpu.VMEM((NQ, Q_LORA), BF16), pltpu.VMEM((NR, Q_LORA), BF16)]
    return pl.pallas_call(
        functools.partial(_moe_combine_kernel, final=final_g is not None, norm_next=next_norm is not None,
                          shared_kv=shared_kv is not None, queries=queries is not None),
        grid=(nblk,),
        in_specs=in_specs,
        out_specs=out_specs,
        out_shape=out_shape,
        scratch_shapes=scratch,
        compiler_params=_cparams(1),
        name="moe_combine",
    )(*args)


def _kv_expand_kernel(lat_ref, kr_ref, wk_ref, ek_ref, wvt_ref, ones_ref, k_ref, vt_ref):
    lat = lat_ref[0].astype(BF16)
    kr = kr_ref[0].astype(BF16)
    k = _dot(lat, wk_ref[...].astype(BF16)) + _dot(kr, ek_ref[...].astype(BF16))
    k_ref[0] = k.astype(k_ref.dtype)
    vt = _dot_nt(wvt_ref[...].astype(BF16), lat) + ones_ref[...]
    vt_ref[0] = vt.astype(vt_ref.dtype)


def kv_expand(lat, kr, wk_pad, ek, wvt_ext, ones_col, rows=1024):
    B, T, _ = lat.shape
    tt = rows
    NK, NVT = wk_pad.shape[1], wvt_ext.shape[0]

    def full(a):
        return pl.BlockSpec(a.shape, lambda b, t: (0, 0))

    def rowspec(n):
        return pl.BlockSpec((1, tt, n), lambda b, t: (b, t, 0))

    return pl.pallas_call(
        _kv_expand_kernel,
        grid=(B, T // tt),
        in_specs=[rowspec(KV_LORA), rowspec(MLA_ROPE), full(wk_pad), full(ek), full(wvt_ext), full(ones_col)],
        out_specs=[rowspec(NK), pl.BlockSpec((1, NVT, tt), lambda b, t: (b, 0, t))],
        out_shape=[jax.ShapeDtypeStruct((B, T, NK), BF16), jax.ShapeDtypeStruct((B, NVT, T), BF16)],
        compiler_params=_cparams(2),
        name="kv_expand",
    )(lat, kr, wk_pad, ek, wvt_ext, ones_col)


def _query_kernel(h_ref, wdq_ref, qg_ref, wq_ref, wqr_ref, c_ref, s_ref, q_ref, wdq_b, wq_b, wqr_b):
    @pl.when(pl.program_id(0) == 0)
    def _():
        wdq_b[...] = wdq_ref[...].astype(BF16)
        wq_b[...] = wq_ref[...].astype(BF16)
        wqr_b[...] = wqr_ref[...].astype(BF16)

    bb, tt, D = h_ref.shape
    h = h_ref[...].reshape(bb * tt, D)
    cq = _rms(_dot(h, wdq_b[...]), qg_ref[...]).astype(BF16)
    q1 = _dot(cq, wq_b[...]).reshape(bb, tt, -1)
    q2 = _dot(cq, wqr_b[...]).reshape(bb, tt, -1)
    c = c_ref[...]
    s = s_ref[...]
    for hd in range(MLA_HEADS):
        sl = slice(hd * HEAD_PAD, (hd + 1) * HEAD_PAD)
        q_ref[:, :, sl] = (q1[:, :, sl] * c + q2[:, :, sl] * s).astype(q_ref.dtype)


def mla_queries(h, w_dq, q_norm_g, wq_pad, wq_rot, l, c128, s128, rows=512):
    B, T, D = h.shape
    bb, tt, nblk, ij = _row_blocks(B, T, rows)
    NQ = wq_pad.shape[-1]
    tspec = pl.BlockSpec((tt, HEAD_PAD), lambda i: (ij(i)[1], 0))
    return pl.pallas_call(
        _query_kernel,
        grid=(nblk,),
        in_specs=[pl.BlockSpec((bb, tt, D), lambda i: ij(i) + (0,)),
                  pl.BlockSpec((None, D, Q_LORA), lambda i: (l, 0, 0)),
                  pl.BlockSpec((None, 1, Q_LORA), lambda i: (l, 0, 0)),
                  pl.BlockSpec((None, Q_LORA, NQ), lambda i: (l, 0, 0)),
                  pl.BlockSpec((None, Q_LORA, NQ), lambda i: (l, 0, 0)),
                  tspec, tspec],
        out_specs=pl.BlockSpec((bb, tt, NQ), lambda i: ij(i) + (0,)),
        out_shape=jax.ShapeDtypeStruct((B, T, NQ), BF16),
        scratch_shapes=[pltpu.VMEM((D, Q_LORA), BF16), pltpu.VMEM((Q_LORA, NQ), BF16),
                        pltpu.VMEM((Q_LORA, NQ), BF16)],
        compiler_params=_cparams(1),
        name="mla_queries",
    )(h, w_dq, q_norm_g.reshape(-1, 1, Q_LORA), wq_pad, wq_rot, c128, s128)


def _queries_t(h, wdq_b, qg, wqt_b, wqrt_b, cos, sin, qt_ref):
    cq = _rms(_dot(h, wdq_b[...]), qg).astype(BF16)
    q1 = _dot_nt(wqt_b[...], cq)
    q2 = _dot_nt(wqrt_b[...], cq)
    pad = jnp.zeros((HEAD_PAD - MLA_NOPE - MLA_ROPE, q1.shape[1]), qt_ref.dtype)
    for hd in range(MLA_HEADS):
        r0 = hd * HEAD_PAD
        rope = (q1[r0 + MLA_NOPE:r0 + MLA_NOPE + MLA_ROPE] * cos
                + q2[hd * MLA_ROPE:(hd + 1) * MLA_ROPE] * sin)
        qt_ref[0, r0:r0 + MLA_NOPE, :] = (q1[r0:r0 + MLA_NOPE] * Q_PRESCALE).astype(qt_ref.dtype)
        qt_ref[0, r0 + MLA_NOPE:r0 + MLA_NOPE + MLA_ROPE, :] = rope.astype(qt_ref.dtype)
        qt_ref[0, r0 + MLA_NOPE + MLA_ROPE:r0 + HEAD_PAD, :] = pad


def _query_t_kernel(h_ref, wdq_ref, qg_ref, wqt_ref, wqrt_ref, cos_ref, sin_ref, qt_ref, wdq_b, wqt_b, wqrt_b):
    @pl.when((pl.program_id(0) == 0) & (pl.program_id(1) == 0))
    def _():
        wdq_b[...] = wdq_ref[...].astype(BF16)
        wqt_b[...] = wqt_ref[...].astype(BF16)
        wqrt_b[...] = wqrt_ref[...].astype(BF16)

    _queries_t(h_ref[0], wdq_b, qg_ref[...], wqt_b, wqrt_b, cos_ref[...], sin_ref[...], qt_ref)


def mla_queries_t(h, w_dq, q_norm_g, wq_t, wqr_t, l, cos_t, sin_t, rows=512):
    B, T, D = h.shape
    tt = rows
    NQ = wq_t.shape[1]
    NR = wqr_t.shape[1]
    tspec = pl.BlockSpec((MLA_ROPE, tt), lambda b, t: (0, t))
    return pl.pallas_call(
        _query_t_kernel,
        grid=(B, T // tt),
        in_specs=[pl.BlockSpec((1, tt, D), lambda b, t: (b, t, 0)),
                  pl.BlockSpec((None, D, Q_LORA), lambda b, t: (l, 0, 0)),
                  pl.BlockSpec((None, 1, Q_LORA), lambda b, t: (l, 0, 0)),
                  pl.BlockSpec((None, NQ, Q_LORA), lambda b, t: (l, 0, 0)),
                  pl.BlockSpec((None, NR, Q_LORA), lambda b, t: (l, 0, 0)),
                  tspec, tspec],
        out_specs=pl.BlockSpec((1, NQ, tt), lambda b, t: (b, 0, t)),
        out_shape=jax.ShapeDtypeStruct((B, NQ, T), BF16),
        scratch_shapes=[pltpu.VMEM((D, Q_LORA), BF16), pltpu.VMEM((NQ, Q_LORA), BF16),
                        pltpu.VMEM((NR, Q_LORA), BF16)],
        compiler_params=_cparams(2),
        name="mla_queries_t",
    )(h, w_dq, q_norm_g.reshape(-1, 1, Q_LORA), wq_t, wqr_t, cos_t, sin_t)


def _attn_prompt_kernel(qi_tab, ki_tab, qt_ref, k_ref, vt_ref, o_ref, *scratch, tq, tk):
    H = MLA_HEADS
    m_refs, l_refs, acc_refs = scratch[:H], scratch[H:2 * H], scratch[2 * H:]
    p_id = pl.program_id(1)
    qi = qi_tab[p_id]
    ki = ki_tab[p_id]

    @pl.when(ki == 0)
    def _():
        for hd in range(H):
            m_refs[hd][...] = jnp.full(m_refs[hd].shape, NEG_INF, F32)
            l_refs[hd][...] = jnp.zeros(l_refs[hd].shape, F32)
            acc_refs[hd][...] = jnp.zeros(acc_refs[hd].shape, F32)

    def block(masked):
        if masked:
            kchunk = (ki * tk + lax.broadcasted_iota(jnp.int32, (tk, tq), 0)) // CHUNK
            qchunk = (qi * tq + lax.broadcasted_iota(jnp.int32, (tk, tq), 1)) // CHUNK
            mask = kchunk <= qchunk
        def scores(hd):
            sl = slice(hd * HEAD_PAD, (hd + 1) * HEAD_PAD)
            return _dot(k_ref[0, :, sl], qt_ref[0, sl, :])

        pending = [scores(hd) for hd in range(ATTN_LOOKAHEAD)]
        for hd in range(H):
            if hd + ATTN_LOOKAHEAD < H:
                pending.append(scores(hd + ATTN_LOOKAHEAD))
            s = pending.pop(0)
            if masked:
                s = jnp.where(mask, s, NEG_INF)
            m_prev = m_refs[hd][...]
            m_new = jnp.maximum(m_prev, jnp.max(s, axis=0, keepdims=True))
            a = jnp.exp2(m_prev - m_new)
            p = jnp.exp2(s - m_new).astype(BF16)
            pv = _dot(vt_ref[0, hd * V_ROWS:(hd + 1) * V_ROWS, :], p)
            acc_refs[hd][...] = a * acc_refs[hd][...] + pv[:MLA_V]
            l_refs[hd][...] = a * l_refs[hd][...] + pv[MLA_V:MLA_V + 1]
            m_refs[hd][...] = m_new

    @pl.when(ki < qi)
    def _():
        block(False)

    @pl.when(ki == qi)
    def _():
        block(True)
        o_t = jnp.concatenate([acc_refs[hd][...] / l_refs[hd][...] for hd in range(H)], axis=0)
        o_ref[0] = o_t.T.astype(o_ref.dtype)


def attn_prompt(qt, k, vt, tq=256):
    B, NQ, T = qt.shape
    NVT = vt.shape[1]
    NV = MLA_HEADS * MLA_V
    tk = tq
    assert tq % CHUNK == 0
    nq = T // tq
    pairs = [(a, b) for a in range(nq) for b in range(a + 1)]
    qi_tab = jnp.asarray([a for a, _ in pairs], jnp.int32)
    ki_tab = jnp.asarray([b for _, b in pairs], jnp.int32)
    grid_spec = pltpu.PrefetchScalarGridSpec(
        num_scalar_prefetch=2,
        grid=(B, len(pairs)),
        in_specs=[pl.BlockSpec((1, NQ, tq), lambda b, p, qt, kt: (b, 0, qt[p])),
                  pl.BlockSpec((1, tk, NQ), lambda b, p, qt, kt: (b, kt[p], 0)),
                  pl.BlockSpec((1, NVT, tk), lambda b, p, qt, kt: (b, 0, kt[p]))],
        out_specs=pl.BlockSpec((1, tq, NV), lambda b, p, qt, kt: (b, qt[p], 0)),
        scratch_shapes=([pltpu.VMEM((1, tq), F32)] * (2 * MLA_HEADS)
                        + [pltpu.VMEM((MLA_V, tq), F32)] * MLA_HEADS),
    )
    return pl.pallas_call(
        functools.partial(_attn_prompt_kernel, tq=tq, tk=tk),
        grid_spec=grid_spec,
        out_shape=jax.ShapeDtypeStruct((B, T, NV), BF16),
        compiler_params=_cparams(2),
        name="attn_prompt",
    )(qi_tab, ki_tab, qt, k, vt)


def _absorb_kernel(q_ref, m_ref, o_ref):
    o_ref[...] = _dot(q_ref[...], m_ref[...].astype(BF16)).astype(o_ref.dtype)


def absorb_queries(q2d, m_abs):
    N = q2d.shape[0]
    H, _, W = m_abs.shape
    return pl.pallas_call(
        _absorb_kernel,
        grid=(H,),
        in_specs=[pl.BlockSpec((N, HEAD_PAD), lambda h: (0, h)),
                  pl.BlockSpec((None, HEAD_PAD, W), lambda h: (h, 0, 0))],
        out_specs=pl.BlockSpec((None, N, W), lambda h: (h, 0, 0)),
        out_shape=jax.ShapeDtypeStruct((H, N, W), BF16),
        compiler_params=_cparams(1),
        name="absorb_queries",
    )(q2d, m_abs)


def _attn_sample_kernel(q_ref, lat_ref, kr_ref, nlat_ref, nkr_ref, o_ref, m_ref, l_ref, acc_ref):
    kb = pl.program_id(1)
    H, Q, W = q_ref.shape
    q = q_ref[...].reshape(H * Q, W)
    q_lat = q[:, :KV_LORA]
    q_rope = q[:, KV_LORA:KV_LORA + MLA_ROPE]

    def update(lat_tile, kr_tile, n_sub, kr_transposed):
        sub = lat_tile.shape[0] // n_sub
        lats = [lat_tile[j * sub:(j + 1) * sub, :].astype(BF16) for j in range(n_sub)]
        if kr_transposed:
            krs = [kr_tile[:, j * sub:(j + 1) * sub].astype(BF16) for j in range(n_sub)]
            ss = [_dot_nt(q_lat, lat) + _dot(q_rope, kr) for lat, kr in zip(lats, krs)]
        else:
            krs = [kr_tile[j * sub:(j + 1) * sub, :].astype(BF16) for j in range(n_sub)]
            ss = [_dot_nt(q_lat, lat) + _dot_nt(q_rope, kr) for lat, kr in zip(lats, krs)]
        m_prev = m_ref[...]
        m_new = m_prev
        for s in ss:
            m_new = jnp.maximum(m_new, jnp.max(s, axis=-1, keepdims=True))
        a = jnp.exp2(m_prev - m_new)
        ps = [jnp.exp2(s - m_new[:, :1]) for s in ss]
        pv = _dot(ps[0].astype(BF16), lats[0])
        psum = jnp.sum(ps[0], axis=-1, keepdims=True)
        for p, lat in zip(ps[1:], lats[1:]):
            pv = pv + _dot(p.astype(BF16), lat)
            psum = psum + jnp.sum(p, axis=-1, keepdims=True)
        l_ref[...] = a * l_ref[...] + psum
        m_ref[...] = m_new
        acc_ref[...] = jnp.concatenate([a, a], axis=-1) * acc_ref[...] + pv

    @pl.when(kb == 0)
    def _():
        m_ref[...] = jnp.full_like(m_ref, NEG_INF)
        l_ref[...] = jnp.zeros_like(l_ref)
        acc_ref[...] = jnp.zeros_like(acc_ref)
        update(nlat_ref[0], nkr_ref[0], 1, False)

    update(lat_ref[0], kr_ref[0], SAMPLE_KEY_SUB, True)

    @pl.when(kb == pl.num_programs(1) - 1)
    def _():
        lsum = l_ref[...]
        o = acc_ref[...] / jnp.concatenate([lsum, lsum], axis=-1)
        o_ref[...] = o.reshape(H, Q, KV_LORA).astype(o_ref.dtype)


def attn_sample(q_abs, cache_lat, cache_kr_t, new_lat, new_kr, tk=4096):
    H, N, W = q_abs.shape
    B, P, _ = cache_lat.shape
    Q = new_lat.shape[1]
    qpos = P + np.arange(Q)
    kpos = np.arange(P + Q)
    assert bool(np.all((kpos // CHUNK)[None, :] <= (qpos // CHUNK)[:, None]))
    return pl.pallas_call(
        _attn_sample_kernel,
        grid=(B, P // tk),
        in_specs=[pl.BlockSpec((H, Q, W), lambda b, kb: (0, b, 0)),
                  pl.BlockSpec((1, tk, KV_LORA), lambda b, kb: (b, kb, 0)),
                  pl.BlockSpec((1, MLA_ROPE, tk), lambda b, kb: (b, 0, kb)),
                  pl.BlockSpec((1, Q, KV_LORA), lambda b, kb: (b, 0, 0)),
                  pl.BlockSpec((1, Q, MLA_ROPE), lambda b, kb: (b, 0, 0))],
        out_specs=pl.BlockSpec((H, Q, KV_LORA), lambda b, kb: (0, b, 0)),
        out_shape=jax.ShapeDtypeStruct((H, N, KV_LORA), BF16),
        scratch_shapes=[pltpu.VMEM((H * Q, LANES), F32), pltpu.VMEM((H * Q, LANES), F32),
                        pltpu.VMEM((H * Q, KV_LORA), F32)],
        compiler_params=_cparams(2),
        name="attn_sample",
    )(q_abs, cache_lat, cache_kr_t, new_lat, new_kr)


def _unabsorb_kernel(o_ref, w_ref, out_ref):
    out_ref[...] = (_dot(o_ref[0], w_ref[0].astype(BF16))
                    + _dot(o_ref[1], w_ref[1].astype(BF16))).astype(out_ref.dtype)


def unabsorb(o_lat, wuv_pad):
    H, N, R = o_lat.shape
    return pl.pallas_call(
        _unabsorb_kernel,
        grid=(H // 2,),
        in_specs=[pl.BlockSpec((2, N, R), lambda p: (p, 0, 0)),
                  pl.BlockSpec((2, R, 2 * MLA_V), lambda p: (p, 0, 0))],
        out_specs=pl.BlockSpec((N, 2 * MLA_V), lambda p: (0, p)),
        out_shape=jax.ShapeDtypeStruct((N, H * MLA_V), BF16),
        compiler_params=_cparams(1),
        name="unabsorb",
    )(o_lat, wuv_pad)


def _rope_tables(pos):
    half = MLA_ROPE // 2
    inv = 1.0 / (ROPE_THETA ** (np.arange(half, dtype=np.float64) * 2.0 / MLA_ROPE))
    ang = np.asarray(pos, np.float64)[:, None] * inv[None, :]
    cos = np.concatenate([np.cos(ang), np.cos(ang)], axis=-1)
    sin = np.concatenate([np.sin(ang), np.sin(ang)], axis=-1)
    T = cos.shape[0]
    c128 = np.zeros((T, HEAD_PAD)); s128 = np.zeros((T, HEAD_PAD))
    c128[:, :MLA_NOPE] = 1.0
    c128[:, MLA_NOPE:MLA_NOPE + MLA_ROPE] = cos
    s128[:, MLA_NOPE:MLA_NOPE + MLA_ROPE] = sin
    return dict(cos32=jnp.asarray(cos, F32), sin32=jnp.asarray(sin, F32),
                c128=jnp.asarray(c128 * Q_PRESCALE, F32), s128=jnp.asarray(s128 * Q_PRESCALE, F32),
                cos_t=jnp.asarray(cos.T * Q_PRESCALE, F32), sin_t=jnp.asarray(sin.T * Q_PRESCALE, F32))


def _rot_half_cols(w):
    half = w.shape[-1] // 2
    return jnp.concatenate([-w[..., half:], w[..., :half]], axis=-1)


def _prep_weights(w_dkv, w_uk, w_uv, w_uq, router_w):
    D = D_MODEL
    w_lat, w_rope = w_dkv[:, :KV_LORA], w_dkv[:, KV_LORA:]
    pad96 = jnp.zeros((D, LANES - MLA_ROPE), F32)
    w_kv = jnp.concatenate([w_lat, w_rope, pad96, _rot_half_cols(w_rope), pad96], axis=-1)

    zpad = HEAD_PAD - MLA_NOPE
    wk_pad = jnp.pad(w_uk, ((0, 0), (0, 0), (0, zpad))).reshape(KV_LORA, MLA_HEADS * HEAD_PAD)
    ek = jnp.zeros((MLA_ROPE, MLA_HEADS, HEAD_PAD), F32)
    ek = ek.at[:, :, MLA_NOPE:MLA_NOPE + MLA_ROPE].set(
        jnp.broadcast_to(jnp.eye(MLA_ROPE, dtype=F32)[:, None, :], (MLA_ROPE, MLA_HEADS, MLA_ROPE)))
    ek = ek.reshape(MLA_ROPE, MLA_HEADS * HEAD_PAD)
    wvt = jnp.transpose(w_uv, (1, 2, 0))
    wvt_ext = jnp.pad(wvt, ((0, 0), (0, V_ROWS - MLA_V), (0, 0))).reshape(MLA_HEADS * V_ROWS, KV_LORA)
    ones_col = jnp.tile((jnp.arange(V_ROWS) >= MLA_V).astype(F32), MLA_HEADS).reshape(-1, 1)

    nb = w_uq.shape[0]
    qn, qr = w_uq[..., :MLA_NOPE], w_uq[..., MLA_NOPE:]
    z32 = jnp.zeros(qr.shape[:-1] + (HEAD_PAD - MLA_NOPE - MLA_ROPE,), F32)
    wq_pad = jnp.concatenate([qn, qr, z32], axis=-1).reshape(nb, Q_LORA, MLA_HEADS * HEAD_PAD)
    wq_rot = jnp.concatenate([jnp.zeros_like(qn), _rot_half_cols(qr), z32], axis=-1)
    wq_rot = wq_rot.reshape(nb, Q_LORA, MLA_HEADS * HEAD_PAD)
    wq_t = jnp.transpose(wq_pad, (0, 2, 1))
    wqr_t = jnp.transpose(_rot_half_cols(qr).reshape(nb, Q_LORA, MLA_HEADS * MLA_ROPE), (0, 2, 1))

    m_abs = jnp.zeros((MLA_HEADS, HEAD_PAD, KV_LORA + LANES), F32)
    m_abs = m_abs.at[:, :MLA_NOPE, :KV_LORA].set(jnp.transpose(w_uk, (1, 2, 0)))
    m_abs = m_abs.at[:, MLA_NOPE:MLA_NOPE + MLA_ROPE, KV_LORA:KV_LORA + MLA_ROPE].set(
        jnp.broadcast_to(jnp.eye(MLA_ROPE, dtype=F32), (MLA_HEADS, MLA_ROPE, MLA_ROPE)))

    wuv_h = jnp.transpose(w_uv, (1, 0, 2))
    even = jnp.pad(wuv_h, ((0, 0), (0, 0), (0, MLA_V)))
    odd = jnp.pad(wuv_h, ((0, 0), (0, 0), (MLA_V, 0)))
    wuv_pad = jnp.where((jnp.arange(MLA_HEADS) % 2 == 0)[:, None, None], even, odd)

    rw_t = jnp.transpose(router_w, (0, 2, 1))
    return dict(w_kv=w_kv, wk_pad=wk_pad, ek=ek, wvt_ext=wvt_ext, ones_col=ones_col, wq_pad=wq_pad, wq_rot=wq_rot, wq_t=wq_t, wqr_t=wqr_t,
                m_abs=m_abs, wuv_pad=wuv_pad, rw_t=rw_t)


def _mixer(st, l, P, W, packed):
    rows_kw = dict(rows_total=packed["total"], row0=packed["row0"], rows_buf=packed["buf"])
    x, m = st["x"], st["mod"][l]
    B, T, _ = x.shape
    n_a = P["hg_w_in"].shape[0]
    norm2 = (P["norm2_g"][l], 4, 3)
    if l < n_a:
        zf, zqig = hgrn_proj(x, P["norm1_g"][l], m, 1, 0, P["hg_w_in"], l)
        s0 = None if st["hg_state"] is None else st["hg_state"][l]
        o, s_new = gla(zqig, zf, st["lbs"][l], P["hg_onorm_g"][l], s0)
        st["hg_new"].append(s_new)
        st["x"], packed["buf"] = linear(o, P["hg_w_out"], l, F32, x=x, mod=m, gate_idx=2, next_norm=norm2,
                                        **rows_kw)
    else:
        bi = l - n_a
        h = st.pop("h_next", None)
        qt = st.pop("qt_next", None)
        if h is None and qt is None:
            h = norm_mod(x, P["norm1_g"][l], m, sc_idx=1, sh_idx=0)
        if st["past_lat"] is None:
            if qt is None:
                qt = mla_queries_t(h, P["w_dq"], P["q_norm_g"], W["wq_t"], W["wqr_t"], bi, st["cos_t"],
                                   st["sin_t"])
            o = attn_prompt(qt, st["k_all"], st["v_all"])
        else:
            q = mla_queries(h, P["w_dq"], P["q_norm_g"], W["wq_pad"], W["wq_rot"], bi, st["c128"], st["s128"])
            q_abs = absorb_queries(q.reshape(B * T, -1), W["m_abs"])
            o_lat = attn_sample(q_abs, st["past_lat"], st["past_kr"], st["lat"], st["kr"])
            o = unabsorb(o_lat, W["wuv_pad"]).reshape(B, T, -1)
        st["x"], packed["buf"] = linear(o, P["w_o"], bi, F32, x=x, mod=m, gate_idx=2, next_norm=norm2, **rows_kw)
    packed["row0"] += B * T


def _moe(groups, hp, l, P, W):
    n_tok = hp.shape[0]
    n_tiles = (TOP_K * n_tok) // MOE_TILE + N_EXPERTS
    pos, w8, tile_start, tile_count = route(hp, W["rw_t"], P["router_bias"], l, MOE_TILE)
    pos_flat = pos.reshape(-1)
    src = sc_invert(pos_flat, n_tok, n_tiles * MOE_TILE)
    xs = sc_gather(hp, src)
    out = moe_gemm(xs, tile_start[:, 0], tile_count[:, 0], P["exp_w_in"], P["exp_w_out"], l,
                   MOE_TILE, n_tiles)
    y8 = sc_gather(out, pos_flat).reshape(TOP_K, n_tok, -1)
    last = l == P["norm1_g"].shape[0] - 1
    with_kv = l == P["hg_w_in"].shape[0] - 1
    next_is_mla = not last and l + 1 >= P["hg_w_in"].shape[0]
    row0 = 0
    for st in groups:
        B, T, _ = st["x"].shape
        with_q = next_is_mla and st["past_lat"] is None
        outs = moe_combine(
            y8, w8, hp, P["sh_w_in"], P["sh_w_out"], st["x"], st["mod"][l], 5, l, row0,
            final_g=P["final_g"] if last else None,
            next_norm=(P["norm1_g"][l + 1], st["mod"][l + 1], 1, 0) if next_is_mla else None,
            shared_kv=(P["kv_in_g"], W["w_kv"], P["kv_lat_g"], st["cos32"], st["sin32"]) if with_kv else None,
            queries=(P["w_dq"], P["q_norm_g"], W["wq_t"], W["wqr_t"], l + 1 - P["hg_w_in"].shape[0],
                     st["cos_t"], st["sin_t"]) if with_q else None,
            rows=256 if with_q else 512)
        outs = list(outs) if isinstance(outs, (list, tuple)) else [outs]
        st["x"] = outs.pop(0)
        if next_is_mla and not with_q:
            st["h_next"] = outs.pop(0)
        if with_kv:
            st["lat"], st["kr"] = outs.pop(0), outs.pop(0)
        if with_q:
            st["qt_next"] = outs.pop(0)
        row0 += B * T


def _group_state(x, mod, pos, hg_state, past_lat, past_kr, lbs):
    return dict(x=x, mod=mod, hg_state=hg_state, past_lat=past_lat, past_kr=past_kr, lbs=lbs,
                **_rope_tables(pos), hg_new=[],
                lat=None, kr=None, k_all=None, v_all=None)


def kernel(x_prompt, x_sample, state_hgrn, cache_mla_latent, cache_mla_krope, c_prompt, c_sample, ada_w, ada_b, norm1_g, norm2_g, hg_w_in, hg_lb_logits, hg_onorm_g, hg_w_out, kv_in_g, w_dkv, kv_lat_g, w_uk, w_uv, w_dq, q_norm_g, w_uq, w_o, router_w, router_bias, exp_w_in, exp_w_out, sh_w_in, sh_w_out, final_g):
    Bp, Sp, _ = x_prompt.shape
    Bs, Ss, _ = x_sample.shape
    past = cache_mla_latent.shape[1]
    P = dict(norm1_g=norm1_g, norm2_g=norm2_g, hg_w_in=hg_w_in, hg_lb_logits=hg_lb_logits,
             hg_onorm_g=hg_onorm_g, hg_w_out=hg_w_out, kv_in_g=kv_in_g, kv_lat_g=kv_lat_g,
             w_dq=w_dq, q_norm_g=q_norm_g, w_o=w_o, router_bias=router_bias,
             exp_w_in=exp_w_in, exp_w_out=exp_w_out, sh_w_in=sh_w_in, sh_w_out=sh_w_out, final_g=final_g)
    W = _prep_weights(w_dkv, w_uk, w_uv, w_uq, router_w)
    mod = ada_mod(jnp.concatenate([c_prompt, c_sample], axis=0), ada_w, ada_b)
    lbs = jnp.cumsum(jax.nn.softmax(hg_lb_logits.astype(F32), axis=0), axis=0)
    gp = _group_state(x_prompt, mod[:, :Bp, None, :], np.arange(Sp), None, None, None, lbs)
    gs = _group_state(x_sample, mod[:, Bp:, None, :], past + np.arange(Ss), state_hgrn,
                      cache_mla_latent, jnp.transpose(cache_mla_krope, (0, 2, 1)), lbs)
    groups = [gp, gs]
    n_tok = sum(st["x"].shape[0] * st["x"].shape[1] for st in groups)
    n_a = hg_w_in.shape[0]
    for l in range(norm1_g.shape[0]):
        packed = dict(total=n_tok, row0=0, buf=None)
        for st in groups:
            _mixer(st, l, P, W, packed)
        _moe(groups, packed["buf"], l, P, W)
        if l == n_a - 1:
            gp["k_all"], gp["v_all"] = kv_expand(gp["lat"], gp["kr"], W["wk_pad"], W["ek"], W["wvt_ext"],
                                                 W["ones_col"])
    return (gp["x"], gs["x"], jnp.stack(gp["hg_new"], axis=0), jnp.stack(gs["hg_new"], axis=0),
            gp["lat"], gp["kr"], gs["lat"], gs["kr"])
```

```python
import dataclasses
import functools

import numpy as np
import jax
import jax.numpy as jnp
from jax import lax
from jax.experimental import pallas as pl
from jax.experimental.pallas import tpu as pltpu
from jax.experimental.pallas import tpu_sc as plsc

F32 = jnp.float32
BF16 = jnp.bfloat16

D_MODEL = 1024
CHUNK = 64
HG_HEADS = 8
HG_DK = 128
HG_DV = 128
MLA_HEADS = 16
MLA_NOPE = 64
MLA_ROPE = 32
MLA_V = 64
Q_LORA = 384
KV_LORA = 256
ROPE_THETA = 10000.0
N_EXPERTS = 64
TOP_K = 8
N_GROUPS = 8
TOPK_GROUPS = 4
EXPERT_FF = 256
SHARED_FF = 256
ROUTED_SCALE = 2.5
EPS = 1e-6

LANES = 128
HEAD_PAD = LANES
SAMPLE_KEY_SUB = 8
ATTN_LOOKAHEAD = 6
V_ROWS = MLA_V + 16
QK_SCALE = (MLA_NOPE + MLA_ROPE) ** -0.5
Q_PRESCALE = QK_SCALE * float(np.log2(np.e))
VMEM_LIMIT = 56 * 1024 * 1024
NEG_INF = float("-inf")
SC_CORES = 2
SC_SUBCORES = 16
SC_WORKERS = SC_CORES * SC_SUBCORES
SC_LANES = 16
SC_WINDOW = 64
MOE_TILE = 512
MOE_NBUF = 4
MOE_SUB = 1


def _cparams(n_axes):
    return pltpu.CompilerParams(dimension_semantics=("arbitrary",) * n_axes,
                                vmem_limit_bytes=VMEM_LIMIT)


def _silu(x):
    return x * jax.nn.sigmoid(x)


def _rms(x, g):
    ms = jnp.mean(x * x, axis=-1, keepdims=True)
    return x * lax.rsqrt(ms + EPS) * g


def _dot(a, b):
    return jnp.dot(a, b, preferred_element_type=F32)


def _dot_nt(a, b):
    return lax.dot_general(a, b, (((1,), (1,)), ((), ())), preferred_element_type=F32)


def _dot_tn(a, b):
    return lax.dot_general(a, b, (((0,), (0,)), ((), ())), preferred_element_type=F32)


def _row_blocks(B, T, rows):
    if T >= rows:
        assert T % rows == 0
        bb, tt = 1, rows
    else:
        assert rows % T == 0 and B % (rows // T) == 0
        bb, tt = rows // T, T
    nt = T // tt
    return bb, tt, (B // bb) * nt, (lambda i: (i // nt, i % nt))


def _ada_kernel(c_ref, w_ref, b_ref, o_ref):
    a = _silu(c_ref[...]).astype(BF16)
    o_ref[...] = _dot(a, w_ref[...].astype(BF16)) + b_ref[...]


def ada_mod(c, ada_w, ada_b):
    R, D = c.shape
    L, _, N = ada_w.shape
    tn = 1536
    return pl.pallas_call(
        _ada_kernel,
        grid=(L, N // tn),
        in_specs=[pl.BlockSpec((R, D), lambda l, j: (0, 0)),
                  pl.BlockSpec((None, D, tn), lambda l, j: (l, 0, j)),
                  pl.BlockSpec((None, 1, tn), lambda l, j: (l, 0, j))],
        out_specs=pl.BlockSpec((None, R, tn), lambda l, j: (l, 0, j)),
        out_shape=jax.ShapeDtypeStruct((L, R, N), F32),
        compiler_params=_cparams(2),
        name="ada_mod",
    )(c, ada_w, ada_b.reshape(L, 1, N))


def _pack_pairs(y):
    half = y.shape[-1] // 2
    bits = lax.bitcast_convert_type(y.astype(BF16).astype(F32), jnp.uint32)
    word = lax.shift_right_logical(bits[:, :half], jnp.uint32(16)) | bits[:, half:]
    return lax.bitcast_convert_type(word, jnp.int32)


def _unpack_pairs(word, dtype=BF16):
    u = lax.bitcast_convert_type(word, jnp.uint32)
    lo = lax.bitcast_convert_type(lax.shift_left(u, jnp.uint32(16)), F32)
    hi = lax.bitcast_convert_type(u & jnp.uint32(0xFFFF0000), F32)
    return lo.astype(dtype), hi.astype(dtype)


def _norm_kernel(x_ref, g_ref, sc_ref, sh_ref, o_ref):
    y = _rms(x_ref[...], g_ref[...]) * (1.0 + sc_ref[...]) + sh_ref[...]
    o_ref[...] = y.astype(o_ref.dtype)


def norm_mod(x, g, mod, sc_idx, sh_idx, rows=512):
    B, T, D = x.shape
    bb, tt, nblk, ij = _row_blocks(B, T, rows)
    xspec = pl.BlockSpec((bb, tt, D), lambda i: ij(i) + (0,))
    return pl.pallas_call(
        _norm_kernel,
        grid=(nblk,),
        in_specs=[xspec, pl.BlockSpec((1, D), lambda i: (0, 0)),
                  pl.BlockSpec((bb, 1, D), lambda i: (ij(i)[0], 0, sc_idx)),
                  pl.BlockSpec((bb, 1, D), lambda i: (ij(i)[0], 0, sh_idx))],
        out_specs=xspec,
        out_shape=jax.ShapeDtypeStruct((B, T, D), BF16),
        compiler_params=_cparams(1),
        name="norm_mod",
    )(x, g.reshape(1, D), mod, mod)


def _linear_kernel(*refs, residual, norm_next, shared_rows, n_main):
    if norm_next and shared_rows:
        a_ref, w_ref, x_ref, gate_ref, ng_ref, nsc_ref, nsh_ref, _, o_ref, hp_ref, wb_ref = refs
    elif norm_next:
        a_ref, w_ref, x_ref, gate_ref, ng_ref, nsc_ref, nsh_ref, o_ref, hp_ref, wb_ref = refs
    elif residual:
        a_ref, w_ref, x_ref, gate_ref, o_ref, wb_ref = refs
    else:
        a_ref, w_ref, o_ref, wb_ref = refs

    @pl.when(pl.program_id(1) == 0)
    def _():
        wb_ref[...] = w_ref[...].astype(BF16)

    def main():
        bb, tt, K = a_ref.shape
        y = _dot(a_ref[...].reshape(bb * tt, K).astype(BF16), wb_ref[...])
        y = y.reshape(bb, tt, y.shape[-1])
        if residual:
            y = x_ref[...] + gate_ref[...] * y
        o_ref[...] = y.astype(o_ref.dtype)
        if norm_next:
            h = _rms(y, ng_ref[...]) * (1.0 + nsc_ref[...]) + nsh_ref[...]
            hp_ref[...] = _pack_pairs(h.reshape(bb * tt, h.shape[-1]))

    if n_main is None:
        main()
    else:
        pl.when(pl.program_id(1) < n_main)(main)

        @pl.when(pl.program_id(1) >= n_main)
        def _():
            hp_ref[...] = jnp.zeros(hp_ref.shape, hp_ref.dtype)


def linear(a, w, l, out_dtype, x=None, mod=None, gate_idx=0, rows=512, tn=1024, next_norm=None,
           rows_total=None, row0=0, rows_buf=None):
    B, T, K = a.shape
    _, _, N = w.shape
    tn = min(tn, N)
    bb, tt, nblk, ij0 = _row_blocks(B, T, rows)
    n_extra = 0
    if next_norm is not None and rows_buf is None and rows_total is not None:
        assert row0 == 0 and (rows_total - B * T) % (bb * tt) == 0
        n_extra = (rows_total - B * T) // (bb * tt)

    def ij(i):
        return ij0(jnp.minimum(i, nblk - 1)) if n_extra else ij0(i)

    in_specs = [pl.BlockSpec((bb, tt, K), lambda j, i: ij(i) + (0,)),
                pl.BlockSpec((None, K, tn), lambda j, i: (l, 0, j))]
    args = [a, w]
    ospec = pl.BlockSpec((bb, tt, tn), lambda j, i: ij(i) + (j,))
    out_specs = ospec
    out_shape = jax.ShapeDtypeStruct((B, T, N), out_dtype)
    aliases = {}
    if x is not None:
        gsteps = D_MODEL // tn
        in_specs += [ospec, pl.BlockSpec((bb, 1, tn), lambda j, i: (ij(i)[0], 0, gate_idx * gsteps + j))]
        args += [x, mod]
    if next_norm is not None:
        assert x is not None and tn == N
        gain, sc_idx, sh_idx = next_norm
        in_specs += [pl.BlockSpec((1, N), lambda j, i: (0, 0)),
                     pl.BlockSpec((bb, 1, N), lambda j, i: (ij(i)[0], 0, sc_idx)),
                     pl.BlockSpec((bb, 1, N), lambda j, i: (ij(i)[0], 0, sh_idx))]
        args += [gain.reshape(1, N), mod, mod]
        assert row0 % (bb * tt) == 0
        off = row0 // (bb * tt)
        out_specs = [ospec, pl.BlockSpec((bb * tt, N // 2), lambda j, i: (off + i, 0))]
        out_shape = [out_shape, jax.ShapeDtypeStruct((rows_total or B * T, N // 2), jnp.int32)]
        if rows_buf is not None:
            in_specs.append(pl.BlockSpec(memory_space=pl.ANY))
            args.append(rows_buf)
            aliases = {len(args) - 1: 1}
    return pl.pallas_call(
        functools.partial(_linear_kernel, residual=x is not None, norm_next=next_norm is not None,
                          shared_rows=rows_buf is not None, n_main=nblk if n_extra else None),
        grid=(N // tn, nblk + n_extra),
        in_specs=in_specs,
        out_specs=out_specs,
        out_shape=out_shape,
        scratch_shapes=[pltpu.VMEM((K, tn), BF16)],
        input_output_aliases=aliases,
        compiler_params=_cparams(2),
        name="linear",
    )(*args)


def _hgrn_proj_kernel(x_ref, g_ref, sc_ref, sh_ref, w_ref, zf_ref, zqig_ref, h_b, w_b):
    i = pl.program_id(0)
    j = pl.program_id(1)
    bb, tt, D = x_ref.shape

    @pl.when(i == 0)
    def _():
        w_b[j] = w_ref[...].astype(BF16)

    @pl.when(j == 0)
    def _():
        h = _rms(x_ref[...], g_ref[...]) * (1.0 + sc_ref[...]) + sh_ref[...]
        h_b[...] = h.reshape(bb * tt, D).astype(BF16)

    y = _dot(h_b[...], w_b[j]).reshape(bb, tt, -1)

    @pl.when(j == 1)
    def _():
        zf_ref[...] = y

    @pl.when(j != 1)
    def _():
        zqig_ref[...] = y.astype(zqig_ref.dtype)


def hgrn_proj(x, g, mod, sc_idx, sh_idx, w_in, l, rows=1024):
    B, T, D = x.shape
    bb, tt, nblk, ij = _row_blocks(B, T, min(rows, B * T))
    xspec = pl.BlockSpec((bb, tt, D), lambda i, j: ij(i) + (0,))
    return pl.pallas_call(
        _hgrn_proj_kernel,
        grid=(nblk, 4),
        in_specs=[xspec,
                  pl.BlockSpec((1, D), lambda i, j: (0, 0)),
                  pl.BlockSpec((bb, 1, D), lambda i, j: (ij(i)[0], 0, sc_idx)),
                  pl.BlockSpec((bb, 1, D), lambda i, j: (ij(i)[0], 0, sh_idx)),
                  pl.BlockSpec((None, D, D), lambda i, j: (l, 0, jnp.where(i == 0, j, 3)))],
        out_specs=[xspec,
                   pl.BlockSpec((bb, tt, D), lambda i, j: ij(i) + (j - (j >= 1),))],
        out_shape=[jax.ShapeDtypeStruct((B, T, D), F32), jax.ShapeDtypeStruct((B, T, 3 * D), BF16)],
        scratch_shapes=[pltpu.VMEM((bb * tt, D), BF16), pltpu.VMEM((4, D, D), BF16)],
        compiler_params=_cparams(2),
        name="hgrn_proj",
    )(x, g.reshape(1, D), mod, mod, w_in)


def _gla_kernel(*refs, L, n_chunks, has_init):
    if has_init:
        q_ref, f_ref, i_ref, g_ref, lb_ref, on_ref, s0_ref, o_ref, so_ref, st_ref = refs
    else:
        q_ref, f_ref, i_ref, g_ref, lb_ref, on_ref, o_ref, so_ref, st_ref = refs
    t = pl.program_id(1)
    H = st_ref.shape[0]

    @pl.when(t == 0)
    def _():
        for h in range(H):
            if has_init:
                st_ref[h] = s0_ref[0, h].T
            else:
                st_ref[h] = jnp.zeros(st_ref.shape[1:], F32)

    lb = lb_ref[...]
    onorm = on_ref[...]
    row = lax.broadcasted_iota(jnp.int32, (L, L), 0)
    col = lax.broadcasted_iota(jnp.int32, (L, L), 1)
    causal = col <= row
    tri = causal.astype(BF16)

    def chunk(c, carry):
        rows = pl.ds(pl.multiple_of(c * L, L), L)

        def write_o(sl, o):
            o_ref[0, rows, sl] = o.astype(o_ref.dtype)

        _gla_chunk(q_ref[0, rows, :], f_ref[0, rows, :], i_ref[0, rows, :], g_ref[0, rows, :],
                   lb, onorm, tri, causal, st_ref, write_o)
        return carry

    lax.fori_loop(0, n_chunks, chunk, 0, unroll=4 if n_chunks % 4 == 0 else 1)

    @pl.when(t == pl.num_programs(1) - 1)
    def _():
        for h in range(H):
            so_ref[0, h] = st_ref[h].T


def _gla_chunk(q, f, v, g, lb, onorm, tri, causal, st_ref, write_o):
    L = q.shape[0]
    H = st_ref.shape[0]
    mid = L // 2 - 1
    q = _silu(q.astype(F32))
    fg = lb + (1.0 - lb) * jax.nn.sigmoid(f)
    k = 1.0 - fg
    v = v.astype(BF16)
    gate = _silu(g.astype(F32))
    logf = jnp.log(fg)
    hi = logf.astype(BF16)
    lo = (logf - hi.astype(F32)).astype(BF16)
    b = _dot(tri, hi) + _dot(tri, lo)
    b_mid = b[mid:mid + 1, :]
    b_last = b[L - 1:L, :]
    qa = q * jnp.exp(b - b_mid)
    kb = k * jnp.exp(b_mid - b)
    qe = (qa * jnp.exp(b_mid)).astype(BF16)
    kd = (kb * jnp.exp(b_last - b_mid)).astype(BF16)
    qa = qa.astype(BF16)
    kb = kb.astype(BF16)
    decay = jnp.exp(b_last)
    sls = [slice(h * HG_DK, (h + 1) * HG_DK) for h in range(H)]
    sts = [st_ref[h] for h in range(H)]
    scores = [_dot_nt(qa[:, sl], kb[:, sl]) for sl in sls]
    inter = [_dot_nt(qe[:, sl], st.astype(BF16)) for sl, st in zip(sls, sts)]
    outer = [_dot_tn(v[:, sl], kd[:, sl]) for sl in sls]
    intra = [_dot(jnp.where(causal, sc, 0.0).astype(BF16), v[:, sl]) for sc, sl in zip(scores, sls)]
    for h, sl in enumerate(sls):
        st_ref[h] = sts[h] * decay[:, sl] + outer[h]
        write_o(sl, _rms(inter[h] + intra[h], onorm[:, sl]) * gate[:, sl])


def gla(zqig, zf, lb, onorm_g, s0):
    B, T, D = zf.shape
    L = CHUNK if T % CHUNK == 0 else T
    tt = min(T, 1024)
    n_chunks = tt // L
    H = HG_HEADS

    def zspec(part):
        return pl.BlockSpec((1, tt, D), lambda b, t: (b, t, part))

    hspec = pl.BlockSpec((1, D), lambda b, t: (0, 0))
    sspec = pl.BlockSpec((1, H, HG_DK, HG_DV), lambda b, t: (b, 0, 0, 0))
    in_specs = [zspec(0), zspec(0), zspec(1), zspec(2), hspec, hspec]
    args = [zqig, zf, zqig, zqig, lb.reshape(1, D), onorm_g.reshape(1, D)]
    if s0 is not None:
        in_specs.append(sspec)
        args.append(s0)
    return pl.pallas_call(
        functools.partial(_gla_kernel, L=L, n_chunks=n_chunks, has_init=s0 is not None),
        grid=(B, T // tt),
        in_specs=in_specs,
        out_specs=[pl.BlockSpec((1, tt, D), lambda b, t: (b, t, 0)), sspec],
        out_shape=[jax.ShapeDtypeStruct((B, T, D), BF16),
                   jax.ShapeDtypeStruct((B, H, HG_DK, HG_DV), F32)],
        scratch_shapes=[pltpu.VMEM((H, HG_DV, HG_DK), F32)],
        compiler_params=_cparams(2),
        name="gla",
    )(*args)


def _route_kernel(h_ref, rw_ref, bias_ref, pos_ref, w_ref, te_ref, nu_ref,
                  e_s, r_s, base_s, start_s, *, tile_rows):
    ph = pl.program_id(0)
    i = pl.program_id(1)
    M = h_ref.shape[0]
    half = h_ref.shape[1]
    G, E = N_GROUPS, N_EXPERTS // N_GROUPS
    e_flat = lax.broadcasted_iota(jnp.int32, (N_EXPERTS, M), 0)

    @pl.when(ph == 1)
    def _():
        @pl.when(i == 0)
        def _():
            cnt = base_s[...]
            padded = jnp.floor((cnt + (tile_rows - 1)) * (1.0 / tile_rows)) * tile_rows
            r = lax.broadcasted_iota(jnp.int32, (N_EXPERTS, N_EXPERTS), 0)
            c = lax.broadcasted_iota(jnp.int32, (N_EXPERTS, N_EXPERTS), 1)
            start = jnp.dot((c < r).astype(F32), padded, preferred_element_type=F32,
                            precision=lax.Precision.HIGHEST)
            start_s[...] = start
            te_ref[...] = (start * (1.0 / tile_rows)).astype(jnp.int32)
            nu_ref[...] = (padded * (1.0 / tile_rows)).astype(jnp.int32)

        start_col = start_s[:, :1]
        for k in range(TOP_K):
            hit = e_flat == e_s[i, k:k + 1, :]
            seg = jnp.sum(jnp.where(hit, start_col, 0.0), axis=0, keepdims=True)
            pos_ref[k:k + 1, :] = (seg + r_s[i, k:k + 1, :]).astype(jnp.int32)

    @pl.when(ph == 0)
    def _():
        _route_pass0(h_ref, rw_ref, bias_ref, w_ref, e_s, r_s, base_s, i, M, half, G, E)


def _route_pass0(h_ref, rw_ref, bias_ref, w_ref, e_s, r_s, base_s, i, M, half, G, E):
    @pl.when(i == 0)
    def _():
        base_s[...] = jnp.zeros_like(base_s)

    lo, hi = _unpack_pairs(h_ref[...])
    rw = rw_ref[...].astype(BF16)
    logits = _dot_nt(rw[:, :half], lo) + _dot_nt(rw[:, half:], hi)
    s = jax.nn.sigmoid(logits)
    sb = (s + bias_ref[...]).reshape(G, E, M)
    s = s.reshape(G, E, M)
    e_in = lax.broadcasted_iota(jnp.int32, (G, E, M), 1).astype(F32)
    g_id = lax.broadcasted_iota(jnp.int32, (G, 1, M), 0)
    e_id = lax.broadcasted_iota(jnp.int32, (G, E, M), 0).astype(F32) * E + e_in

    def all_max(a):
        return jnp.max(jnp.max(a, axis=0, keepdims=True), axis=1, keepdims=True)

    def all_min(a):
        return jnp.min(jnp.min(a, axis=0, keepdims=True), axis=1, keepdims=True)

    def all_sum(a):
        return jnp.sum(jnp.sum(a, axis=0, keepdims=True), axis=1, keepdims=True)

    m1 = jnp.max(sb, axis=1, keepdims=True)
    first = jnp.min(jnp.where(sb == m1, e_in, float(E)), axis=1, keepdims=True)
    m2 = jnp.max(jnp.where(e_in == first, NEG_INF, sb), axis=1, keepdims=True)
    gs = m1 + m2

    rank = jnp.zeros((G, 1, M), jnp.int32)
    for j in range(G):
        gj = gs[j:j + 1]
        beats = (gj > gs) | ((gj == gs) & (j < g_id))
        rank = rank + beats.astype(jnp.int32)
    gsel = rank < TOPK_GROUPS

    vals = jnp.where(gsel, sb, NEG_INF)
    selm = jnp.zeros((G, E, M), F32)
    chosen, score = [], []
    for _ in range(TOP_K):
        m = all_max(vals)
        first = all_min(jnp.where(vals == m, e_id, float(N_EXPERTS)))
        hit = e_id == first
        score.append(all_sum(jnp.where(hit, s, 0.0)))
        selm = jnp.where(hit, 1.0, selm)
        vals = jnp.where(hit, NEG_INF, vals)
        chosen.append(first)

    tot = score[0]
    for sc in score[1:]:
        tot = tot + sc
    norm = ROUTED_SCALE / tot

    selm = selm.reshape(N_EXPERTS, M)
    earlier = (lax.broadcasted_iota(jnp.int32, (M, M), 0)
               < lax.broadcasted_iota(jnp.int32, (M, M), 1)).astype(BF16)
    rank = (base_s[:, :1] + _dot(selm.astype(BF16), earlier)).reshape(G, E, M)
    base_s[...] = base_s[...] + jnp.sum(selm, axis=1, keepdims=True)
    for k in range(TOP_K):
        hit = e_id == chosen[k]
        e_s[i, k:k + 1, :] = chosen[k].reshape(1, M).astype(jnp.int32)
        r_s[i, k:k + 1, :] = all_sum(jnp.where(hit, rank, 0.0)).reshape(1, M)
        w_ref[k:k + 1, :] = (score[k] * norm).reshape(1, M)


def route(hp, router_w_t, router_bias, l, tile_rows, rows=1408):
    N, half = hp.shape
    M = rows
    nT = N // M
    assert N % M == 0

    def p0(ph, i):
        return i * (1 - ph) + (nT - 1) * ph

    return pl.pallas_call(
        functools.partial(_route_kernel, tile_rows=tile_rows),
        grid=(2, nT),
        in_specs=[pl.BlockSpec((M, half), lambda ph, i: (p0(ph, i), 0)),
                  pl.BlockSpec((None, N_EXPERTS, 2 * half), lambda ph, i: (l, 0, 0)),
                  pl.BlockSpec((None, N_EXPERTS, 1), lambda ph, i: (l, 0, 0))],
        out_specs=[pl.BlockSpec((TOP_K, M), lambda ph, i: (0, i * ph)),
                   pl.BlockSpec((TOP_K, M), lambda ph, i: (0, p0(ph, i))),
                   pl.BlockSpec((N_EXPERTS, LANES), lambda ph, i: (0, 0)),
                   pl.BlockSpec((N_EXPERTS, LANES), lambda ph, i: (0, 0))],
        out_shape=[jax.ShapeDtypeStruct((TOP_K, N), jnp.int32),
                   jax.ShapeDtypeStruct((TOP_K, N), F32),
                   jax.ShapeDtypeStruct((N_EXPERTS, LANES), jnp.int32),
                   jax.ShapeDtypeStruct((N_EXPERTS, LANES), jnp.int32)],
        scratch_shapes=[pltpu.VMEM((nT, TOP_K, M), jnp.int32), pltpu.VMEM((nT, TOP_K, M), F32),
                        pltpu.VMEM((N_EXPERTS, LANES), F32), pltpu.VMEM((N_EXPERTS, LANES), F32)],
        compiler_params=_cparams(2),
        name="route",
    )(hp, router_w_t, router_bias.reshape(-1, N_EXPERTS, 1))


def _sc_mesh():
    return plsc.VectorSubcoreMesh(core_axis_name="core", subcore_axis_name="subcore")


def sc_invert(pos_flat, n_tok, n_out):
    n = pos_flat.shape[0]
    per = n_out // SC_WORKERS
    chunk = n_tok
    assert n_out % SC_WORKERS == 0 and per % SC_LANES == 0
    assert n_tok % chunk == 0 and n % chunk == 0 and chunk % SC_LANES == 0
    cp = pltpu.CompilerParams()
    if "needs_layout_passes" in pltpu.CompilerParams.__dataclass_fields__:
        cp = dataclasses.replace(cp, needs_layout_passes=False)

    @functools.partial(
        pl.kernel, out_type=jax.ShapeDtypeStruct((n_out,), jnp.int32), mesh=_sc_mesh(),
        scratch_types=[pltpu.VMEM((chunk,), jnp.int32), pltpu.VMEM((per,), jnp.int32)],
        compiler_params=cp, name="sc_invert")
    def k(pos_hbm, src_hbm, pos_v, src_v):
        wid = lax.axis_index("subcore") * SC_CORES + lax.axis_index("core")
        lo = wid * per
        lane = lax.iota(jnp.int32, SC_LANES)

        @pl.loop(0, per, step=SC_LANES)
        def _(r):
            src_v[pl.ds(r, SC_LANES)] = lax.rem(lo + r + lane, n_tok)

        @pl.loop(0, n // chunk)
        def _(c):
            base = c * chunk
            pltpu.sync_copy(pos_hbm.at[pl.ds(base, chunk)], pos_v)
            tok0 = lax.rem(base, n_tok)

            @plsc.parallel_loop(0, chunk, step=SC_LANES, unroll=8)
            def _(r):
                p = pos_v[pl.ds(r, SC_LANES)] - lo
                mine = (p >= 0) & (p < per)
                plsc.store_scatter(src_v, [jnp.where(mine, p, 0)], tok0 + r + lane, mask=mine)

        pltpu.sync_copy(src_v, src_hbm.at[pl.ds(lo, per)])

    return k(pos_flat)


def sc_gather(x, idx):
    n = idx.shape[0]
    dim = x.shape[1]
    assert n % (SC_WINDOW * SC_WORKERS) == 0

    @functools.partial(
        pl.kernel, out_type=jax.ShapeDtypeStruct((n, dim), x.dtype), mesh=_sc_mesh(),
        scratch_types=[], name="sc_gather")
    def k(x_hbm, i_hbm, o_hbm):
        def body(i_vmem, o_vmem):
            pltpu.sync_copy(x_hbm.at[i_vmem.at[0]], o_vmem)

        pltpu.emit_pipeline(
            body, grid=(n // SC_WINDOW,),
            in_specs=[pl.BlockSpec((1, SC_WINDOW), index_map=lambda i: (i, 0))],
            out_specs=[pl.BlockSpec((SC_WINDOW, dim), index_map=lambda i: (i, 0))],
            core_axis_name=("core", "subcore"),
            dimension_semantics=(pltpu.PARALLEL,),
        )(i_hbm, o_hbm)

    return k(x, idx.reshape(n // SC_WINDOW, SC_WINDOW))


def _moe_gemm_kernel(ts_ref, tn_ref, x_hbm, wi_ref, wo_ref, o_hbm, wi_b, wo_b, xbuf, obuf, in_sem, out_sem,
                     *, tile_rows, n_tiles):
    e = pl.program_id(0)
    last = pl.num_programs(0) - 1
    t0 = ts_ref[e]
    n = tn_ref[e]
    n_used = ts_ref[last] + tn_ref[last]

    def x_copy(g, slot):
        rows = pl.ds(pl.multiple_of(g * tile_rows, tile_rows), tile_rows)
        return pltpu.make_async_copy(x_hbm.at[rows], xbuf.at[slot], in_sem.at[slot])

    def o_copy(g, slot):
        rows = pl.ds(pl.multiple_of(g * tile_rows, tile_rows), tile_rows)
        return pltpu.make_async_copy(obuf.at[slot], o_hbm.at[rows], out_sem.at[slot])

    @pl.when(e == 0)
    def _():
        for g0 in range(MOE_NBUF - 1):
            @pl.when(g0 < n_used)
            def _():
                x_copy(g0, g0).start()

    @pl.when(n > 0)
    def _():
        wi_b[...] = wi_ref[...].astype(BF16)
        wo_b[...] = wo_ref[...].astype(BF16)

    def tile(i, carry):
        g = t0 + i
        slot = lax.rem(g, MOE_NBUF)
        x_copy(g, slot).wait()
        ahead = g + (MOE_NBUF - 1)

        @pl.when(ahead < n_used)
        def _():
            x_copy(ahead, lax.rem(ahead, MOE_NBUF)).start()

        @pl.when(g >= MOE_NBUF)
        def _():
            o_copy(g - MOE_NBUF, slot).wait()

        rows = tile_rows // MOE_SUB
        half = xbuf.shape[2]
        xs = [_unpack_pairs(xbuf[slot, r * rows:(r + 1) * rows, :]) for r in range(MOE_SUB)]
        hus = [_dot(lo, wi_b[:half, :]) + _dot(hi, wi_b[half:, :]) for lo, hi in xs]
        acts = [(_silu(hu[:, :EXPERT_FF]) * hu[:, EXPERT_FF:]).astype(BF16) for hu in hus]
        outs = [_dot(act, wo_b[...]) for act in acts]
        for r, out in enumerate(outs):
            obuf[slot, r * rows:(r + 1) * rows, :] = _pack_pairs(out)
        o_copy(g, slot).start()
        return carry

    lax.fori_loop(0, n, tile, 0)

    @pl.when(e == last)
    def _():
        for back in range(MOE_NBUF, 0, -1):
            @pl.when(n_used >= back)
            def _():
                o_copy(n_used - back, lax.rem(n_used - back, MOE_NBUF)).wait()

        obuf[...] = jnp.zeros(obuf.shape, obuf.dtype)
        n_clear = n_tiles - n_used

        def clear(i, carry):
            slot = lax.rem(i, MOE_NBUF)

            @pl.when(i >= MOE_NBUF)
            def _():
                o_copy(n_used + i - MOE_NBUF, slot).wait()

            o_copy(n_used + i, slot).start()
            return carry

        lax.fori_loop(0, n_clear, clear, 0)
        for back in range(MOE_NBUF, 0, -1):
            @pl.when(n_clear >= back)
            def _():
                o_copy(n_tiles - back, lax.rem(n_clear - back, MOE_NBUF)).wait()


def moe_gemm(xs, tile_start, tile_count, exp_w_in, exp_w_out, l, tile_rows, n_tiles):
    P, half = xs.shape
    D = 2 * half
    assert P == n_tiles * tile_rows
    hbm = pl.BlockSpec(memory_space=pl.ANY)
    grid_spec = pltpu.PrefetchScalarGridSpec(
        num_scalar_prefetch=2,
        grid=(N_EXPERTS,),
        in_specs=[hbm,
                  pl.BlockSpec((None, None, D, 2 * EXPERT_FF), lambda e, ts, tn: (l, e, 0, 0)),
                  pl.BlockSpec((None, None, EXPERT_FF, D), lambda e, ts, tn: (l, e, 0, 0))],
        out_specs=hbm,
        scratch_shapes=[pltpu.VMEM((D, 2 * EXPERT_FF), BF16), pltpu.VMEM((EXPERT_FF, D), BF16),
                        pltpu.VMEM((MOE_NBUF, tile_rows, half), jnp.int32),
                        pltpu.VMEM((MOE_NBUF, tile_rows, half), jnp.int32),
                        pltpu.SemaphoreType.DMA((MOE_NBUF,)), pltpu.SemaphoreType.DMA((MOE_NBUF,))],
    )
    return pl.pallas_call(
        functools.partial(_moe_gemm_kernel, tile_rows=tile_rows, n_tiles=n_tiles),
        grid_spec=grid_spec,
        out_shape=jax.ShapeDtypeStruct((P, half), jnp.int32),
        compiler_params=_cparams(1),
        name="moe_gemm",
    )(tile_start, tile_count, xs, exp_w_in, exp_w_out)


def _shared_expert_kernel(h_ref, si_ref, so_ref, o_ref, si_b, so_b):
    @pl.when(pl.program_id(0) == 0)
    def _():
        si_b[...] = si_ref[...].astype(BF16)
        so_b[...] = so_ref[...].astype(BF16)

    half = h_ref.shape[1]
    hlo, hhi = _unpack_pairs(h_ref[...])
    hu = _dot(hlo, si_b[:half, :]) + _dot(hhi, si_b[half:, :])
    act = (_silu(hu[:, :SHARED_FF]) * hu[:, SHARED_FF:]).astype(BF16)
    o_ref[...] = _pack_pairs(_dot(act, so_b[...]))


def shared_expert(hp, sh_w_in, sh_w_out, l, rows=1408):
    N, half = hp.shape
    D = 2 * half
    assert N % rows == 0
    rspec = pl.BlockSpec((rows, half), lambda i: (i, 0))
    return pl.pallas_call(
        _shared_expert_kernel,
        grid=(N // rows,),
        in_specs=[rspec,
                  pl.BlockSpec((None, D, 2 * SHARED_FF), lambda i: (l, 0, 0)),
                  pl.BlockSpec((None, SHARED_FF, D), lambda i: (l, 0, 0))],
        out_specs=rspec,
        out_shape=jax.ShapeDtypeStruct((N, half), jnp.int32),
        scratch_shapes=[pltpu.VMEM((D, 2 * SHARED_FF), BF16), pltpu.VMEM((SHARED_FF, D), BF16)],
        compiler_params=_cparams(1),
        name="shared_expert",
    )(hp, sh_w_in, sh_w_out)


def _moe_combine_kernel(*refs, final, norm_next, shared_kv, queries):
    it = iter(refs)
    y_ref, w_ref, h_ref, si_ref, so_ref, x_ref, g2_ref = (next(it) for _ in range(7))
    fg_ref = next(it) if final else None
    ng_ref, nsc_ref, nsh_ref = (next(it) for _ in range(3)) if norm_next else (None,) * 3
    kg_ref, wkv_ref, lg_ref, cos_ref, sin_ref = (next(it) for _ in range(5)) if shared_kv else (None,) * 5
    wdq_ref, qg_ref, wqt_ref, wqrt_ref, cost_ref, sint_ref = (next(it) for _ in range(6)) if queries else (None,) * 6
    o_ref = next(it)
    hn_ref = next(it) if norm_next and not queries else None
    lat_ref, kr_ref = (next(it), next(it)) if shared_kv else (None, None)
    qt_ref = next(it) if queries else None
    si_b, so_b = next(it), next(it)
    wkv_b = next(it) if shared_kv else None
    wdq_b, wqt_b, wqrt_b = (next(it), next(it), next(it)) if queries else (None,) * 3

    @pl.when(pl.program_id(0) == 0)
    def _():
        if shared_kv:
            wkv_b[...] = wkv_ref[...].astype(BF16)
        if queries:
            wdq_b[...] = wdq_ref[...].astype(BF16)
            wqt_b[...] = wqt_ref[...].astype(BF16)
            wqrt_b[...] = wqrt_ref[...].astype(BF16)

    bb, tt, D = x_ref.shape
    half = D // 2
    w = w_ref[...].T
    acc_lo = jnp.zeros((bb * tt, half), F32)
    acc_hi = jnp.zeros((bb * tt, half), F32)
    for k in range(TOP_K):
        lo, hi = _unpack_pairs(y_ref[k], F32)
        acc_lo = acc_lo + w[:, k:k + 1] * lo
        acc_hi = acc_hi + w[:, k:k + 1] * hi
    slo, shi = _unpack_pairs(h_ref[...], F32)
    y = jnp.concatenate([acc_lo + slo, acc_hi + shi], axis=-1)
    x_new = x_ref[...] + g2_ref[...] * y.reshape(bb, tt, D)
    o_ref[...] = _rms(x_new, fg_ref[...]) if final else x_new
    if norm_next:
        hn = (_rms(x_new, ng_ref[...]) * (1.0 + nsc_ref[...]) + nsh_ref[...]).astype(BF16)
        if queries:
            _queries_t(hn.reshape(bb * tt, D), wdq_b, qg_ref[...], wqt_b, wqrt_b, cost_ref[...], sint_ref[...],
                       qt_ref)
        else:
            hn_ref[...] = hn
    if shared_kv:
        xn = _rms(x_new, kg_ref[...]).reshape(bb * tt, D).astype(BF16)
        z = _dot(xn, wkv_b[...])
        lat_ref[...] = _rms(z[:, :KV_LORA], lg_ref[...]).reshape(bb, tt, KV_LORA)
        zr = z[:, KV_LORA:KV_LORA + MLA_ROPE].reshape(bb, tt, MLA_ROPE)
        zq = z[:, KV_LORA + LANES:KV_LORA + LANES + MLA_ROPE].reshape(bb, tt, MLA_ROPE)
        kr_ref[...] = zr * cos_ref[...] + zq * sin_ref[...]


def moe_combine(y8, w8, hp, sh_w_in, sh_w_out, x, mod, gate_idx, l, row0, final_g=None, next_norm=None,
                shared_kv=None, queries=None, rows=512):
    B, T, D = x.shape
    half = D // 2
    bb, tt, nblk, ij = _row_blocks(B, T, rows)
    M = bb * tt
    assert row0 % M == 0
    off = row0 // M
    xspec = pl.BlockSpec((bb, tt, D), lambda i: ij(i) + (0,))
    in_specs = [pl.BlockSpec((TOP_K, M, half), lambda i: (0, off + i, 0)),
                pl.BlockSpec((TOP_K, M), lambda i: (0, off + i)),
                pl.BlockSpec((M, half), lambda i: (off + i, 0)),
                pl.BlockSpec((None, D, 2 * SHARED_FF), lambda i: (l, 0, 0)),
                pl.BlockSpec((None, SHARED_FF, D), lambda i: (l, 0, 0)),
                xspec,
                pl.BlockSpec((bb, 1, D), lambda i: (ij(i)[0], 0, gate_idx))]
    args = [y8, w8, hp, sh_w_in, sh_w_out, x, mod]
    out_specs = xspec
    out_shape = jax.ShapeDtypeStruct((B, T, D), F32)
    if final_g is not None:
        assert next_norm is None
        in_specs.append(pl.BlockSpec((1, D), lambda i: (0, 0)))
        args.append(final_g.reshape(1, D))
    if next_norm is not None:
        gain, mod_next, sc_idx, sh_idx = next_norm
        in_specs += [pl.BlockSpec((1, D), lambda i: (0, 0)),
                     pl.BlockSpec((bb, 1, D), lambda i: (ij(i)[0], 0, sc_idx)),
                     pl.BlockSpec((bb, 1, D), lambda i: (ij(i)[0], 0, sh_idx))]
        args += [gain.reshape(1, D), mod_next, mod_next]
        if queries is None:
            out_specs = [xspec, xspec]
            out_shape = [out_shape, jax.ShapeDtypeStruct((B, T, D), BF16)]
    scratch = [pltpu.VMEM((D, 2 * SHARED_FF), BF16), pltpu.VMEM((SHARED_FF, D), BF16)]
    if shared_kv is not None:
        kv_in_g, w_kv, kv_lat_g, cos32, sin32 = shared_kv
        tspec = pl.BlockSpec((tt, MLA_ROPE), lambda i: (ij(i)[1], 0))
        in_specs += [pl.BlockSpec((1, D), lambda i: (0, 0)),
                     pl.BlockSpec(w_kv.shape, lambda i: (0, 0)),
                     pl.BlockSpec((1, KV_LORA), lambda i: (0, 0)),
                     tspec, tspec]
        args += [kv_in_g.reshape(1, D), w_kv, kv_lat_g.reshape(1, KV_LORA), cos32, sin32]
        out_specs = list(out_specs) if isinstance(out_specs, list) else [out_specs]
        out_shape = list(out_shape) if isinstance(out_shape, list) else [out_shape]
        out_specs += [pl.BlockSpec((bb, tt, KV_LORA), lambda i: ij(i) + (0,)),
                      pl.BlockSpec((bb, tt, MLA_ROPE), lambda i: ij(i) + (0,))]
        out_shape += [jax.ShapeDtypeStruct((B, T, KV_LORA), F32), jax.ShapeDtypeStruct((B, T, MLA_ROPE), F32)]
        scratch.append(pltpu.VMEM(w_kv.shape, BF16))
    if queries is not None:
        assert next_norm is not None and bb == 1
        w_dq, q_norm_g, wq_t, wqr_t, bi, cos_t, sin_t = queries
        NQ, NR = wq_t.shape[1], wqr_t.shape[1]
        tspec_t = pl.BlockSpec((MLA_ROPE, tt), lambda i: (0, ij(i)[1]))
        in_specs += [pl.BlockSpec((None, D, Q_LORA), lambda i: (bi, 0, 0)),
                     pl.BlockSpec((None, 1, Q_LORA), lambda i: (bi, 0, 0)),
                     pl.BlockSpec((None, NQ, Q_LORA), lambda i: (bi, 0, 0)),
                     pl.BlockSpec((None, NR, Q_LORA), lambda i: (bi, 0, 0)),
                     tspec_t, tspec_t]
        args += [w_dq, q_norm_g.reshape(-1, 1, Q_LORA), wq_t, wqr_t, cos_t, sin_t]
        out_specs = list(out_specs) if isinstance(out_specs, list) else [out_specs]
        out_shape = list(out_shape) if isinstance(out_shape, list) else [out_shape]
        out_specs.append(pl.BlockSpec((1, NQ, tt), lambda i: (ij(i)[0], 0, ij(i)[1])))
        out_shape.append(jax.ShapeDtypeStruct((B, NQ, T), BF16))
        scratch += [pltpu.VMEM((D, Q_LORA), BF16), pltpu.VMEM((NQ, Q_LORA), BF16), pltpu.VMEM((NR, Q_LORA), BF16)]
    return pl.pallas_call(
        functools.partial(_moe_combine_kernel, final=final_g is not None, norm_next=next_norm is not None,
                          shared_kv=shared_kv is not None, queries=queries is not None),
        grid=(nblk,),
        in_specs=in_specs,
        out_specs=out_specs,
        out_shape=out_shape,
        scratch_shapes=scratch,
        compiler_params=_cparams(1),
        name="moe_combine",
    )(*args)


def _kv_expand_kernel(lat_ref, kr_ref, wk_ref, ek_ref, wvt_ref, ones_ref, k_ref, vt_ref):
    lat = lat_ref[0].astype(BF16)
    kr = kr_ref[0].astype(BF16)
    k = _dot(lat, wk_ref[...].astype(BF16)) + _dot(kr, ek_ref[...].astype(BF16))
    k_ref[0] = k.astype(k_ref.dtype)
    vt = _dot_nt(wvt_ref[...].astype(BF16), lat) + ones_ref[...]
    vt_ref[0] = vt.astype(vt_ref.dtype)


def kv_expand(lat, kr, wk_pad, ek, wvt_ext, ones_col, rows=1024):
    B, T, _ = lat.shape
    tt = rows
    NK, NVT = wk_pad.shape[1], wvt_ext.shape[0]

    def full(a):
        return pl.BlockSpec(a.shape, lambda b, t: (0, 0))

    def rowspec(n):
        return pl.BlockSpec((1, tt, n), lambda b, t: (b, t, 0))

    return pl.pallas_call(
        _kv_expand_kernel,
        grid=(B, T // tt),
        in_specs=[rowspec(KV_LORA), rowspec(MLA_ROPE), full(wk_pad), full(ek), full(wvt_ext), full(ones_col)],
        out_specs=[rowspec(NK), pl.BlockSpec((1, NVT, tt), lambda b, t: (b, 0, t))],
        out_shape=[jax.ShapeDtypeStruct((B, T, NK), BF16), jax.ShapeDtypeStruct((B, NVT, T), BF16)],
        compiler_params=_cparams(2),
        name="kv_expand",
    )(lat, kr, wk_pad, ek, wvt_ext, ones_col)


def _query_kernel(h_ref, wdq_ref, qg_ref, wq_ref, wqr_ref, c_ref, s_ref, q_ref, wdq_b, wq_b, wqr_b):
    @pl.when(pl.program_id(0) == 0)
    def _():
        wdq_b[...] = wdq_ref[...].astype(BF16)
        wq_b[...] = wq_ref[...].astype(BF16)
        wqr_b[...] = wqr_ref[...].astype(BF16)

    bb, tt, D = h_ref.shape
    h = h_ref[...].reshape(bb * tt, D)
    cq = _rms(_dot(h, wdq_b[...]), qg_ref[...]).astype(BF16)
    q1 = _dot(cq, wq_b[...]).reshape(bb, tt, -1)
    q2 = _dot(cq, wqr_b[...]).reshape(bb, tt, -1)
    c = c_ref[...]
    s = s_ref[...]
    for hd in range(MLA_HEADS):
        sl = slice(hd * HEAD_PAD, (hd + 1) * HEAD_PAD)
        q_ref[:, :, sl] = (q1[:, :, sl] * c + q2[:, :, sl] * s).astype(q_ref.dtype)


def mla_queries(h, w_dq, q_norm_g, wq_pad, wq_rot, l, c128, s128, rows=512):
    B, T, D = h.shape
    bb, tt, nblk, ij = _row_blocks(B, T, rows)
    NQ = wq_pad.shape[-1]
    tspec = pl.BlockSpec((tt, HEAD_PAD), lambda i: (ij(i)[1], 0))
    return pl.pallas_call(
        _query_kernel,
        grid=(nblk,),
        in_specs=[pl.BlockSpec((bb, tt, D), lambda i: ij(i) + (0,)),
                  pl.BlockSpec((None, D, Q_LORA), lambda i: (l, 0, 0)),
                  pl.BlockSpec((None, 1, Q_LORA), lambda i: (l, 0, 0)),
                  pl.BlockSpec((None, Q_LORA, NQ), lambda i: (l, 0, 0)),
                  pl.BlockSpec((None, Q_LORA, NQ), lambda i: (l, 0, 0)),
                  tspec, tspec],
        out_specs=pl.BlockSpec((bb, tt, NQ), lambda i: ij(i) + (0,)),
        out_shape=jax.ShapeDtypeStruct((B, T, NQ), BF16),
        scratch_shapes=[pltpu.VMEM((D, Q_LORA), BF16), pltpu.VMEM((Q_LORA, NQ), BF16),
                        pltpu.VMEM((Q_LORA, NQ), BF16)],
        compiler_params=_cparams(1),
        name="mla_queries",
    )(h, w_dq, q_norm_g.reshape(-1, 1, Q_LORA), wq_pad, wq_rot, c128, s128)


def _queries_t(h, wdq_b, qg, wqt_b, wqrt_b, cos, sin, qt_ref):
    cq = _rms(_dot(h, wdq_b[...]), qg).astype(BF16)
    q1 = _dot_nt(wqt_b[...], cq)
    q2 = _dot_nt(wqrt_b[...], cq)
    pad = jnp.zeros((HEAD_PAD - MLA_NOPE - MLA_ROPE, q1.shape[1]), qt_ref.dtype)
    for hd in range(MLA_HEADS):
        r0 = hd * HEAD_PAD
        rope = (q1[r0 + MLA_NOPE:r0 + MLA_NOPE + MLA_ROPE] * cos
                + q2[hd * MLA_ROPE:(hd + 1) * MLA_ROPE] * sin)
        qt_ref[0, r0:r0 + MLA_NOPE, :] = (q1[r0:r0 + MLA_NOPE] * Q_PRESCALE).astype(qt_ref.dtype)
        qt_ref[0, r0 + MLA_NOPE:r0 + MLA_NOPE + MLA_ROPE, :] = rope.astype(qt_ref.dtype)
        qt_ref[0, r0 + MLA_NOPE + MLA_ROPE:r0 + HEAD_PAD, :] = pad


def _query_t_kernel(h_ref, wdq_ref, qg_ref, wqt_ref, wqrt_ref, cos_ref, sin_ref, qt_ref, wdq_b, wqt_b, wqrt_b):
    @pl.when((pl.program_id(0) == 0) & (pl.program_id(1) == 0))
    def _():
        wdq_b[...] = wdq_ref[...].astype(BF16)
        wqt_b[...] = wqt_ref[...].astype(BF16)
        wqrt_b[...] = wqrt_ref[...].astype(BF16)

    _queries_t(h_ref[0], wdq_b, qg_ref[...], wqt_b, wqrt_b, cos_ref[...], sin_ref[...], qt_ref)


def mla_queries_t(h, w_dq, q_norm_g, wq_t, wqr_t, l, cos_t, sin_t, rows=512):
    B, T, D = h.shape
    tt = rows
    NQ = wq_t.shape[1]
    NR = wqr_t.shape[1]
    tspec = pl.BlockSpec((MLA_ROPE, tt), lambda b, t: (0, t))
    return pl.pallas_call(
        _query_t_kernel,
        grid=(B, T // tt),
        in_specs=[pl.BlockSpec((1, tt, D), lambda b, t: (b, t, 0)),
                  pl.BlockSpec((None, D, Q_LORA), lambda b, t: (l, 0, 0)),
                  pl.BlockSpec((None, 1, Q_LORA), lambda b, t: (l, 0, 0)),
                  pl.BlockSpec((None, NQ, Q_LORA), lambda b, t: (l, 0, 0)),
                  pl.BlockSpec((None, NR, Q_LORA), lambda b, t: (l, 0, 0)),
                  tspec, tspec],
        out_specs=pl.BlockSpec((1, NQ, tt), lambda b, t: (b, 0, t)),
        out_shape=jax.ShapeDtypeStruct((B, NQ, T), BF16),
        scratch_shapes=[pltpu.VMEM((D, Q_LORA), BF16), pltpu.VMEM((NQ, Q_LORA), BF16),
                        pltpu.VMEM((NR, Q_LORA), BF16)],
        compiler_params=_cparams(2),
        name="mla_queries_t",
    )(h, w_dq, q_norm_g.reshape(-1, 1, Q_LORA), wq_t, wqr_t, cos_t, sin_t)


def _attn_prompt_kernel(qi_tab, ki_tab, qt_ref, k_ref, vt_ref, o_ref, *scratch, tq, tk):
    H = MLA_HEADS
    m_refs, l_refs, acc_refs = scratch[:H], scratch[H:2 * H], scratch[2 * H:]
    p_id = pl.program_id(1)
    qi = qi_tab[p_id]
    ki = ki_tab[p_id]

    @pl.when(ki == 0)
    def _():
        for hd in range(H):
            m_refs[hd][...] = jnp.full(m_refs[hd].shape, NEG_INF, F32)
            l_refs[hd][...] = jnp.zeros(l_refs[hd].shape, F32)
            acc_refs[hd][...] = jnp.zeros(acc_refs[hd].shape, F32)

    def block(masked):
        if masked:
            kchunk = (ki * tk + lax.broadcasted_iota(jnp.int32, (tk, tq), 0)) // CHUNK
            qchunk = (qi * tq + lax.broadcasted_iota(jnp.int32, (tk, tq), 1)) // CHUNK
            mask = kchunk <= qchunk
        def scores(hd):
            sl = slice(hd * HEAD_PAD, (hd + 1) * HEAD_PAD)
            return _dot(k_ref[0, :, sl], qt_ref[0, sl, :])

        pending = [scores(hd) for hd in range(ATTN_LOOKAHEAD)]
        for hd in range(H):
            if hd + ATTN_LOOKAHEAD < H:
                pending.append(scores(hd + ATTN_LOOKAHEAD))
            s = pending.pop(0)
            if masked:
                s = jnp.where(mask, s, NEG_INF)
            m_prev = m_refs[hd][...]
            m_new = jnp.maximum(m_prev, jnp.max(s, axis=0, keepdims=True))
            a = jnp.exp2(m_prev - m_new)
            p = jnp.exp2(s - m_new).astype(BF16)
            pv = _dot(vt_ref[0, hd * V_ROWS:(hd + 1) * V_ROWS, :], p)
            acc_refs[hd][...] = a * acc_refs[hd][...] + pv[:MLA_V]
            l_refs[hd][...] = a * l_refs[hd][...] + pv[MLA_V:MLA_V + 1]
            m_refs[hd][...] = m_new

    @pl.when(ki < qi)
    def _():
        block(False)

    @pl.when(ki == qi)
    def _():
        block(True)
        o_t = jnp.concatenate([acc_refs[hd][...] / l_refs[hd][...] for hd in range(H)], axis=0)
        o_ref[0] = o_t.T.astype(o_ref.dtype)


def attn_prompt(qt, k, vt, tq=256):
    B, NQ, T = qt.shape
    NVT = vt.shape[1]
    NV = MLA_HEADS * MLA_V
    tk = tq
    assert tq % CHUNK == 0
    nq = T // tq
    pairs = [(a, b) for a in range(nq) for b in range(a + 1)]
    qi_tab = jnp.asarray([a for a, _ in pairs], jnp.int32)
    ki_tab = jnp.asarray([b for _, b in pairs], jnp.int32)
    grid_spec = pltpu.PrefetchScalarGridSpec(
        num_scalar_prefetch=2,
        grid=(B, len(pairs)),
        in_specs=[pl.BlockSpec((1, NQ, tq), lambda b, p, qt, kt: (b, 0, qt[p])),
                  pl.BlockSpec((1, tk, NQ), lambda b, p, qt, kt: (b, kt[p], 0)),
                  pl.BlockSpec((1, NVT, tk), lambda b, p, qt, kt: (b, 0, kt[p]))],
        out_specs=pl.BlockSpec((1, tq, NV), lambda b, p, qt, kt: (b, qt[p], 0)),
        scratch_shapes=([pltpu.VMEM((1, tq), F32)] * (2 * MLA_HEADS)
                        + [pltpu.VMEM((MLA_V, tq), F32)] * MLA_HEADS),
    )
    return pl.pallas_call(
        functools.partial(_attn_prompt_kernel, tq=tq, tk=tk),
        grid_spec=grid_spec,
        out_shape=jax.ShapeDtypeStruct((B, T, NV), BF16),
        compiler_params=_cparams(2),
        name="attn_prompt",
    )(qi_tab, ki_tab, qt, k, vt)


def _absorb_kernel(q_ref, m_ref, o_ref):
    o_ref[...] = _dot(q_ref[...], m_ref[...].astype(BF16)).astype(o_ref.dtype)


def absorb_queries(q2d, m_abs):
    N = q2d.shape[0]
    H, _, W = m_abs.shape
    return pl.pallas_call(
        _absorb_kernel,
        grid=(H,),
        in_specs=[pl.BlockSpec((N, HEAD_PAD), lambda h: (0, h)),
                  pl.BlockSpec((None, HEAD_PAD, W), lambda h: (h, 0, 0))],
        out_specs=pl.BlockSpec((None, N, W), lambda h: (h, 0, 0)),
        out_shape=jax.ShapeDtypeStruct((H, N, W), BF16),
        compiler_params=_cparams(1),
        name="absorb_queries",
    )(q2d, m_abs)


def _attn_sample_kernel(q_ref, lat_ref, kr_ref, nlat_ref, nkr_ref, o_ref, m_ref, l_ref, acc_ref):
    kb = pl.program_id(1)
    H, Q, W = q_ref.shape
    q = q_ref[...].reshape(H * Q, W)
    q_lat = q[:, :KV_LORA]
    q_rope = q[:, KV_LORA:KV_LORA + MLA_ROPE]

    def update(lat_tile, kr_tile, n_sub, kr_transposed):
        sub = lat_tile.shape[0] // n_sub
        lats = [lat_tile[j * sub:(j + 1) * sub, :].astype(BF16) for j in range(n_sub)]
        if kr_transposed:
            krs = [kr_tile[:, j * sub:(j + 1) * sub].astype(BF16) for j in range(n_sub)]
            ss = [_dot_nt(q_lat, lat) + _dot(q_rope, kr) for lat, kr in zip(lats, krs)]
        else:
            krs = [kr_tile[j * sub:(j + 1) * sub, :].astype(BF16) for j in range(n_sub)]
            ss = [_dot_nt(q_lat, lat) + _dot_nt(q_rope, kr) for lat, kr in zip(lats, krs)]
        m_prev = m_ref[...]
        m_new = m_prev
        for s in ss:
            m_new = jnp.maximum(m_new, jnp.max(s, axis=-1, keepdims=True))
        a = jnp.exp2(m_prev - m_new)
        ps = [jnp.exp2(s - m_new[:, :1]) for s in ss]
        pv = _dot(ps[0].astype(BF16), lats[0])
        psum = jnp.sum(ps[0], axis=-1, keepdims=True)
        for p, lat in zip(ps[1:], lats[1:]):
            pv = pv + _dot(p.astype(BF16), lat)
            psum = psum + jnp.sum(p, axis=-1, keepdims=True)
        l_ref[...] = a * l_ref[...] + psum
        m_ref[...] = m_new
        acc_ref[...] = jnp.concatenate([a, a], axis=-1) * acc_ref[...] + pv

    @pl.when(kb == 0)
    def _():
        m_ref[...] = jnp.full_like(m_ref, NEG_INF)
        l_ref[...] = jnp.zeros_like(l_ref)
        acc_ref[...] = jnp.zeros_like(acc_ref)
        update(nlat_ref[0], nkr_ref[0], 1, False)

    update(lat_ref[0], kr_ref[0], SAMPLE_KEY_SUB, True)

    @pl.when(kb == pl.num_programs(1) - 1)
    def _():
        lsum = l_ref[...]
        o = acc_ref[...] / jnp.concatenate([lsum, lsum], axis=-1)
        o_ref[...] = o.reshape(H, Q, KV_LORA).astype(o_ref.dtype)


def attn_sample(q_abs, cache_lat, cache_kr_t, new_lat, new_kr, tk=4096):
    H, N, W = q_abs.shape
    B, P, _ = cache_lat.shape
    Q = new_lat.shape[1]
    qpos = P + np.arange(Q)
    kpos = np.arange(P + Q)
    assert bool(np.all((kpos // CHUNK)[None, :] <= (qpos // CHUNK)[:, None]))
    return pl.pallas_call(
        _attn_sample_kernel,
        grid=(B, P // tk),
        in_specs=[pl.BlockSpec((H, Q, W), lambda b, kb: (0, b, 0)),
                  pl.BlockSpec((1, tk, KV_LORA), lambda b, kb: (b, kb, 0)),
                  pl.BlockSpec((1, MLA_ROPE, tk), lambda b, kb: (b, 0, kb)),
                  pl.BlockSpec((1, Q, KV_LORA), lambda b, kb: (b, 0, 0)),
                  pl.BlockSpec((1, Q, MLA_ROPE), lambda b, kb: (b, 0, 0))],
        out_specs=pl.BlockSpec((H, Q, KV_LORA), lambda b, kb: (0, b, 0)),
        out_shape=jax.ShapeDtypeStruct((H, N, KV_LORA), BF16),
        scratch_shapes=[pltpu.VMEM((H * Q, LANES), F32), pltpu.VMEM((H * Q, LANES), F32),
                        pltpu.VMEM((H * Q, KV_LORA), F32)],
        compiler_params=_cparams(2),
        name="attn_sample",
    )(q_abs, cache_lat, cache_kr_t, new_lat, new_kr)


def _unabsorb_kernel(o_ref, w_ref, out_ref):
    out_ref[...] = (_dot(o_ref[0], w_ref[0].astype(BF16))
                    + _dot(o_ref[1], w_ref[1].astype(BF16))).astype(out_ref.dtype)


def unabsorb(o_lat, wuv_pad):
    H, N, R = o_lat.shape
    return pl.pallas_call(
        _unabsorb_kernel,
        grid=(H // 2,),
        in_specs=[pl.BlockSpec((2, N, R), lambda p: (p, 0, 0)),
                  pl.BlockSpec((2, R, 2 * MLA_V), lambda p: (p, 0, 0))],
        out_specs=pl.BlockSpec((N, 2 * MLA_V), lambda p: (0, p)),
        out_shape=jax.ShapeDtypeStruct((N, H * MLA_V), BF16),
        compiler_params=_cparams(1),
        name="unabsorb",
    )(o_lat, wuv_pad)


def _rope_tables(pos):
    half = MLA_ROPE // 2
    inv = 1.0 / (ROPE_THETA ** (np.arange(half, dtype=np.float64) * 2.0 / MLA_ROPE))
    ang = np.asarray(pos, np.float64)[:, None] * inv[None, :]
    cos = np.concatenate([np.cos(ang), np.cos(ang)], axis=-1)
    sin = np.concatenate([np.sin(ang), np.sin(ang)], axis=-1)
    T = cos.shape[0]
    c128 = np.zeros((T, HEAD_PAD)); s128 = np.zeros((T, HEAD_PAD))
    c128[:, :MLA_NOPE] = 1.0
    c128[:, MLA_NOPE:MLA_NOPE + MLA_ROPE] = cos
    s128[:, MLA_NOPE:MLA_NOPE + MLA_ROPE] = sin
    return dict(cos32=jnp.asarray(cos, F32), sin32=jnp.asarray(sin, F32),
                c128=jnp.asarray(c128 * Q_PRESCALE, F32), s128=jnp.asarray(s128 * Q_PRESCALE, F32),
                cos_t=jnp.asarray(cos.T * Q_PRESCALE, F32), sin_t=jnp.asarray(sin.T * Q_PRESCALE, F32))


def _rot_half_cols(w):
    half = w.shape[-1] // 2
    return jnp.concatenate([-w[..., half:], w[..., :half]], axis=-1)


def _prep_weights(w_dkv, w_uk, w_uv, w_uq, router_w):
    D = D_MODEL
    w_lat, w_rope = w_dkv[:, :KV_LORA], w_dkv[:, KV_LORA:]
    pad96 = jnp.zeros((D, LANES - MLA_ROPE), F32)
    w_kv = jnp.concatenate([w_lat, w_rope, pad96, _rot_half_cols(w_rope), pad96], axis=-1)

    zpad = HEAD_PAD - MLA_NOPE
    wk_pad = jnp.pad(w_uk, ((0, 0), (0, 0), (0, zpad))).reshape(KV_LORA, MLA_HEADS * HEAD_PAD)
    ek = jnp.zeros((MLA_ROPE, MLA_HEADS, HEAD_PAD), F32)
    ek = ek.at[:, :, MLA_NOPE:MLA_NOPE + MLA_ROPE].set(
        jnp.broadcast_to(jnp.eye(MLA_ROPE, dtype=F32)[:, None, :], (MLA_ROPE, MLA_HEADS, MLA_ROPE)))
    ek = ek.reshape(MLA_ROPE, MLA_HEADS * HEAD_PAD)
    wvt = jnp.transpose(w_uv, (1, 2, 0))
    wvt_ext = jnp.pad(wvt, ((0, 0), (0, V_ROWS - MLA_V), (0, 0))).reshape(MLA_HEADS * V_ROWS, KV_LORA)
    ones_col = jnp.tile((jnp.arange(V_ROWS) >= MLA_V).astype(F32), MLA_HEADS).reshape(-1, 1)

    nb = w_uq.shape[0]
    qn, qr = w_uq[..., :MLA_NOPE], w_uq[..., MLA_NOPE:]
    z32 = jnp.zeros(qr.shape[:-1] + (HEAD_PAD - MLA_NOPE - MLA_ROPE,), F32)
    wq_pad = jnp.concatenate([qn, qr, z32], axis=-1).reshape(nb, Q_LORA, MLA_HEADS * HEAD_PAD)
    wq_rot = jnp.concatenate([jnp.zeros_like(qn), _rot_half_cols(qr), z32], axis=-1)
    wq_rot = wq_rot.reshape(nb, Q_LORA, MLA_HEADS * HEAD_PAD)
    wq_t = jnp.transpose(wq_pad, (0, 2, 1))
    wqr_t = jnp.transpose(_rot_half_cols(qr).reshape(nb, Q_LORA, MLA_HEADS * MLA_ROPE), (0, 2, 1))

    m_abs = jnp.zeros((MLA_HEADS, HEAD_PAD, KV_LORA + LANES), F32)
    m_abs = m_abs.at[:, :MLA_NOPE, :KV_LORA].set(jnp.transpose(w_uk, (1, 2, 0)))
    m_abs = m_abs.at[:, MLA_NOPE:MLA_NOPE + MLA_ROPE, KV_LORA:KV_LORA + MLA_ROPE].set(
        jnp.broadcast_to(jnp.eye(MLA_ROPE, dtype=F32), (MLA_HEADS, MLA_ROPE, MLA_ROPE)))

    wuv_h = jnp.transpose(w_uv, (1, 0, 2))
    even = jnp.pad(wuv_h, ((0, 0), (0, 0), (0, MLA_V)))
    odd = jnp.pad(wuv_h, ((0, 0), (0, 0), (MLA_V, 0)))
    wuv_pad = jnp.where((jnp.arange(MLA_HEADS) % 2 == 0)[:, None, None], even, odd)

    rw_t = jnp.transpose(router_w, (0, 2, 1))
    return dict(w_kv=w_kv, wk_pad=wk_pad, ek=ek, wvt_ext=wvt_ext, ones_col=ones_col, wq_pad=wq_pad, wq_rot=wq_rot, wq_t=wq_t, wqr_t=wqr_t,
                m_abs=m_abs, wuv_pad=wuv_pad, rw_t=rw_t)


def _mixer(st, l, P, W, packed):
    rows_kw = dict(rows_total=packed["total"], row0=packed["row0"], rows_buf=packed["buf"])
    x, m = st["x"], st["mod"][l]
    B, T, _ = x.shape
    n_a = P["hg_w_in"].shape[0]
    norm2 = (P["norm2_g"][l], 4, 3)
    if l < n_a:
        zf, zqig = hgrn_proj(x, P["norm1_g"][l], m, 1, 0, P["hg_w_in"], l)
        s0 = None if st["hg_state"] is None else st["hg_state"][l]
        o, s_new = gla(zqig, zf, st["lbs"][l], P["hg_onorm_g"][l], s0)
        st["hg_new"].append(s_new)
        st["x"], packed["buf"] = linear(o, P["hg_w_out"], l, F32, x=x, mod=m, gate_idx=2, next_norm=norm2,
                                        **rows_kw)
    else:
        bi = l - n_a
        h = st.pop("h_next", None)
        qt = st.pop("qt_next", None)
        if h is None and qt is None:
            h = norm_mod(x, P["norm1_g"][l], m, sc_idx=1, sh_idx=0)
        if st["past_lat"] is None:
            if qt is None:
                qt = mla_queries_t(h, P["w_dq"], P["q_norm_g"], W["wq_t"], W["wqr_t"], bi, st["cos_t"],
                                   st["sin_t"])
            o = attn_prompt(qt, st["k_all"], st["v_all"])
        else:
            q = mla_queries(h, P["w_dq"], P["q_norm_g"], W["wq_pad"], W["wq_rot"], bi, st["c128"], st["s128"])
            q_abs = absorb_queries(q.reshape(B * T, -1), W["m_abs"])
            o_lat = attn_sample(q_abs, st["past_lat"], st["past_kr"], st["lat"], st["kr"])
            o = unabsorb(o_lat, W["wuv_pad"]).reshape(B, T, -1)
        st["x"], packed["buf"] = linear(o, P["w_o"], bi, F32, x=x, mod=m, gate_idx=2, next_norm=norm2, **rows_kw)
    packed["row0"] += B * T


def _moe(groups, hp, l, P, W):
    n_tok = hp.shape[0]
    n_tiles = (TOP_K * n_tok) // MOE_TILE + N_EXPERTS
    pos, w8, tile_start, tile_count = route(hp, W["rw_t"], P["router_bias"], l, MOE_TILE)
    pos_flat = pos.reshape(-1)
    src = sc_invert(pos_flat, n_tok, n_tiles * MOE_TILE)
    xs = sc_gather(hp, src)
    hp = shared_expert(hp, P["sh_w_in"], P["sh_w_out"], l)
    out = moe_gemm(xs, tile_start[:, 0], tile_count[:, 0], P["exp_w_in"], P["exp_w_out"], l,
                   MOE_TILE, n_tiles)
    y8 = sc_gather(out, pos_flat).reshape(TOP_K, n_tok, -1)
    last = l == P["norm1_g"].shape[0] - 1
    with_kv = l == P["hg_w_in"].shape[0] - 1
    next_is_mla = not last and l + 1 >= P["hg_w_in"].shape[0]
    row0 = 0
    for st in groups:
        B, T, _ = st["x"].shape
        with_q = next_is_mla and st["past_lat"] is None
        outs = moe_combine(
            y8, w8, hp, P["sh_w_in"], P["sh_w_out"], st["x"], st["mod"][l], 5, l, row0,
            final_g=P["final_g"] if last else None,
            next_norm=(P["norm1_g"][l + 1], st["mod"][l + 1], 1, 0) if next_is_mla else None,
            shared_kv=(P["kv_in_g"], W["w_kv"], P["kv_lat_g"], st["cos32"], st["sin32"]) if with_kv else None,
            queries=(P["w_dq"], P["q_norm_g"], W["wq_t"], W["wqr_t"], l + 1 - P["hg_w_in"].shape[0],
                     st["cos_t"], st["sin_t"]) if with_q else None,
            rows=256 if with_q else 512)
        outs = list(outs) if isinstance(outs, (list, tuple)) else [outs]
        st["x"] = outs.pop(0)
        if next_is_mla and not with_q:
            st["h_next"] = outs.pop(0)
        if with_kv:
            st["lat"], st["kr"] = outs.pop(0), outs.pop(0)
        if with_q:
            st["qt_next"] = outs.pop(0)
        row0 += B * T


def _group_state(x, mod, pos, hg_state, past_lat, past_kr, lbs):
    return dict(x=x, mod=mod, hg_state=hg_state, past_lat=past_lat, past_kr=past_kr, lbs=lbs,
                **_rope_tables(pos), hg_new=[],
                lat=None, kr=None, k_all=None, v_all=None)


def kernel(x_prompt, x_sample, state_hgrn, cache_mla_latent, cache_mla_krope, c_prompt, c_sample, ada_w, ada_b, norm1_g, norm2_g, hg_w_in, hg_lb_logits, hg_onorm_g, hg_w_out, kv_in_g, w_dkv, kv_lat_g, w_uk, w_uv, w_dq, q_norm_g, w_uq, w_o, router_w, router_bias, exp_w_in, exp_w_out, sh_w_in, sh_w_out, final_g):
    Bp, Sp, _ = x_prompt.shape
    Bs, Ss, _ = x_sample.shape
    past = cache_mla_latent.shape[1]
    P = dict(norm1_g=norm1_g, norm2_g=norm2_g, hg_w_in=hg_w_in, hg_lb_logits=hg_lb_logits,
             hg_onorm_g=hg_onorm_g, hg_w_out=hg_w_out, kv_in_g=kv_in_g, kv_lat_g=kv_lat_g,
             w_dq=w_dq, q_norm_g=q_norm_g, w_o=w_o, router_bias=router_bias,
             exp_w_in=exp_w_in, exp_w_out=exp_w_out, sh_w_in=sh_w_in, sh_w_out=sh_w_out, final_g=final_g)
    W = _prep_weights(w_dkv, w_uk, w_uv, w_uq, router_w)
    mod = ada_mod(jnp.concatenate([c_prompt, c_sample], axis=0), ada_w, ada_b)
    lbs = jnp.cumsum(jax.nn.softmax(hg_lb_logits.astype(F32), axis=0), axis=0)
    gp = _group_state(x_prompt, mod[:, :Bp, None, :], np.arange(Sp), None, None, None, lbs)
    gs = _group_state(x_sample, mod[:, Bp:, None, :], past + np.arange(Ss), state_hgrn,
                      cache_mla_latent, jnp.transpose(cache_mla_krope, (0, 2, 1)), lbs)
    groups = [gp, gs]
    n_tok = sum(st["x"].shape[0] * st["x"].shape[1] for st in groups)
    n_a = hg_w_in.shape[0]
    for l in range(norm1_g.shape[0]):
        packed = dict(total=n_tok, row0=0, buf=None)
        for st in groups:
            _mixer(st, l, P, W, packed)
        _moe(groups, packed["buf"], l, P, W)
        if l == n_a - 1:
            gp["k_all"], gp["v_all"] = kv_expand(gp["lat"], gp["kr"], W["wk_pad"], W["ek"], W["wvt_ext"],
                                                 W["ones_col"])
    return (gp["x"], gs["x"], jnp.stack(gp["hg_new"], axis=0), jnp.stack(gs["hg_new"], axis=0),
            gp["lat"], gp["kr"], gs["lat"], gs["kr"])
```
